```python
import jax, jax.numpy as jnp
from jax import lax
import numpy as np

D_MODEL = 1024
BATCH = 16
SEQ = 2048
DEPTH = 1

D_MIX = 2 * D_MODEL
A_HEADS = 16
A_HEAD_DIM = 64
A_WIDTH = A_HEADS * A_HEAD_DIM
A_ROT_DIM = A_HEAD_DIM // 4
DILATED_PATTERNS = ((128, 1), (512, 4), (2048, 16))
MLA_HEADS = 8
MLA_Q_RANK = 256
MLA_KV_RANK = 128
MLA_NOPE_DIM = 64
MLA_ROPE_DIM = 32
MLA_V_DIM = 64
MLA_WIDTH = MLA_HEADS * MLA_V_DIM
N_MEM = 256
MEM_HEADS = 4
MEM_HEAD_DIM = 128
MEM_WIDTH = MEM_HEADS * MEM_HEAD_DIM

ROPE_THETA = 500000.0
Q_BLOCK = 128
NORM_EPS = 1e-5
NEG_INF = -1e30
DEEPNORM_ALPHA = (2 * DEPTH) ** 0.25
DEEPNORM_BETA = (8 * DEPTH) ** -0.25

IN_SPLITS = (A_WIDTH, A_WIDTH, A_WIDTH, A_WIDTH,
             MLA_Q_RANK, MLA_KV_RANK, MLA_ROPE_DIM, MLA_WIDTH,
             MEM_WIDTH, MEM_WIDTH)
D_IN = sum(IN_SPLITS)

kernel_name = "hymba_dilated_mla_memory_deepnorm"


def _layer_norm(x, g, b):
    xf = x.astype(jnp.float32)
    mu = jnp.mean(xf, axis=-1, keepdims=True)
    var = jnp.mean(jnp.square(xf - mu), axis=-1, keepdims=True)
    return ((xf - mu) * lax.rsqrt(var + NORM_EPS) * g.astype(jnp.float32) + b.astype(jnp.float32)).astype(x.dtype)


def _rms_norm(x, g, out_dtype):
    xf = x.astype(jnp.float32)
    ms = jnp.mean(jnp.square(xf), axis=-1, keepdims=True)
    return (xf * lax.rsqrt(ms + NORM_EPS) * g.astype(jnp.float32)).astype(out_dtype)


def _rope(x, pos):
    r = x.shape[-1]
    inv_freq = ROPE_THETA ** (-(jnp.arange(0, r, 2, dtype=jnp.float32) / r))
    ang = pos.astype(jnp.float32)[..., None] * inv_freq
    cos, sin = jnp.cos(ang)[:, :, None, :], jnp.sin(ang)[:, :, None, :]
    xf = x.astype(jnp.float32)
    x1, x2 = xf[..., : r // 2], xf[..., r // 2:]
    return jnp.concatenate([x1 * cos - x2 * sin, x2 * cos + x1 * sin], axis=-1).astype(x.dtype)


def _partial_rope(x, pos):
    return jnp.concatenate([_rope(x[..., :A_ROT_DIM], pos), x[..., A_ROT_DIM:]], axis=-1)


def _window_attn(q, k, v, n_side):
    n, length, h, e = q.shape
    blk = n_side
    nb = -(-length // blk)
    pad = nb * blk - length
    qb = jnp.pad(q, ((0, 0), (0, pad), (0, 0), (0, 0))).reshape(n, nb, blk, h, e).astype(jnp.float32)

    def bands(t):
        tb = jnp.pad(t, ((0, 0), (blk, pad + blk), (0, 0), (0, 0))).reshape(n, nb + 2, blk, h, t.shape[-1])
        return jnp.concatenate([tb[:, :-2], tb[:, 1:-1], tb[:, 2:]], axis=2).astype(jnp.float32)

    kb, vb = bands(k), bands(v)
    qpos = jnp.arange(nb)[:, None] * blk + jnp.arange(blk)[None, :]
    kpos = (jnp.arange(nb)[:, None] - 1) * blk + jnp.arange(3 * blk)[None, :]
    off = kpos[:, None, :] - qpos[:, :, None]
    valid = (jnp.abs(off) <= n_side) & (kpos[:, None, :] >= 0) & (kpos[:, None, :] < length)
    s = jnp.einsum('nbqhe,nbkhe->nbhqk', qb, kb) * (e ** -0.5)
    s = jnp.where(valid[None, :, None], s, NEG_INF)
    m = jnp.max(s, axis=-1, keepdims=True)
    p = jnp.exp(s - m)
    den = jnp.sum(p, axis=-1, keepdims=True)
    o = jnp.einsum('nbhqk,nbkhe->nbqhe', p / den, vb).reshape(n, nb * blk, h, vb.shape[-1])[:, :length]
    lse = (m + jnp.log(den))[..., 0]
    lse = lse.transpose(0, 1, 3, 2).reshape(n, nb * blk, h)[:, :length]
    return o, lse


def _dilated_attention(q, k, v):
    b, s, h, e = q.shape
    outs, lses = [], []
    for window, dil in DILATED_PATTERNS:
        n_side = window // (2 * dil)
        length = s // dil

        def to_sub(t):
            return t.reshape(b, length, dil, h, t.shape[-1]).transpose(0, 2, 1, 3, 4).reshape(b * dil, length, h, t.shape[-1])

        o, lse = _window_attn(to_sub(q), to_sub(k), to_sub(v), n_side)
        outs.append(o.reshape(b, dil, length, h, e).transpose(0, 2, 1, 3, 4).reshape(b, s, h, e))
        lses.append(lse.reshape(b, dil, length, h).transpose(0, 2, 1, 3).reshape(b, s, h))
    w = jax.nn.softmax(jnp.stack(lses, axis=0), axis=0)
    return jnp.einsum('gbsh,gbshe->bshe', w, jnp.stack(outs, axis=0))


def _mla_attention(q_nope, q_rope, k_nope, k_rope, v):
    b, s, h, _ = q_nope.shape
    scale = (MLA_NOPE_DIM + MLA_ROPE_DIM) ** -0.5
    nq = s // Q_BLOCK
    kn, kr, vf = k_nope.astype(jnp.float32), k_rope.astype(jnp.float32), v.astype(jnp.float32)

    def blocks(t):
        return t.reshape((b, nq, Q_BLOCK) + t.shape[2:]).swapaxes(0, 1)

    def one_block(args):
        qn, qr = args
        sc = (jnp.einsum('bqhe,bkhe->bhqk', qn.astype(jnp.float32), kn)
              + jnp.einsum('bqhr,bkr->bhqk', qr.astype(jnp.float32), kr)) * scale
        p = jax.nn.softmax(sc, axis=-1)
        return jnp.einsum('bhqk,bkhe->bqhe', p, vf)

    o = lax.map(one_block, (blocks(q_nope), blocks(q_rope)))
    return o.swapaxes(0, 1).reshape(b, s, h, v.shape[-1])


def _memory_attention(q, k, v):
    sc = jnp.einsum('bshe,bmhe->bhsm', q.astype(jnp.float32), k.astype(jnp.float32)) * (q.shape[-1] ** -0.5)
    p = jax.nn.softmax(sc, axis=-1)
    return jnp.einsum('bhsm,bmhe->bshe', p, v.astype(jnp.float32))


def _hybrid_layer(h, pos, mem, w_in, g_cq, g_ckv, w_uq, w_ukv, w_mem_kv,
                  g_out_a, g_out_b, g_out_m, w_out, g_post, b_post):
    b, s, _ = h.shape
    dt = h.dtype
    idx = [int(i) for i in np.cumsum(IN_SPLITS)[:-1]]
    proj = h @ w_in
    a_q, a_k, a_v, a_g, c_q, c_kv, b_kr, b_g, m_q, m_g = jnp.split(proj, idx, axis=-1)

    hd = (b, s, A_HEADS, A_HEAD_DIM)
    y_a = _dilated_attention(_partial_rope(a_q.reshape(hd), pos),
                             _partial_rope(a_k.reshape(hd), pos),
                             a_v.reshape(hd)).reshape(b, s, A_WIDTH)

    q = (_rms_norm(c_q, g_cq, dt) @ w_uq).reshape(b, s, MLA_HEADS, MLA_NOPE_DIM + MLA_ROPE_DIM)
    q_nope, q_rope = q[..., :MLA_NOPE_DIM], _rope(q[..., MLA_NOPE_DIM:], pos)
    kv = (_rms_norm(c_kv, g_ckv, dt) @ w_ukv).reshape(b, s, MLA_HEADS, MLA_NOPE_DIM + MLA_V_DIM)
    k_nope, v = kv[..., :MLA_NOPE_DIM], kv[..., MLA_NOPE_DIM:]
    k_rope = _rope(b_kr[:, :, None, :], pos)[:, :, 0]
    y_b = _mla_attention(q_nope, q_rope, k_nope, k_rope, v).reshape(b, s, MLA_WIDTH)

    mkv = mem @ w_mem_kv
    mk = mkv[..., :MEM_WIDTH].reshape(b, -1, MEM_HEADS, MEM_HEAD_DIM)
    mv = mkv[..., MEM_WIDTH:].reshape(b, -1, MEM_HEADS, MEM_HEAD_DIM)
    y_m = _memory_attention(m_q.reshape(b, s, MEM_HEADS, MEM_HEAD_DIM), mk, mv).reshape(b, s, MEM_WIDTH)

    y = jnp.concatenate([_rms_norm(y_a, g_out_a, dt) * jax.nn.silu(a_g),
                         _rms_norm(y_b, g_out_b, dt) * jax.nn.silu(b_g),
                         _rms_norm(y_m, g_out_m, dt) * jax.nn.silu(m_g)], axis=-1)
    sub = y @ w_out
    return _layer_norm(DEEPNORM_ALPHA * h + sub, g_post, b_post)


def _fwd_setup_inputs(seed: int = 0) -> dict:
    key = jax.random.key(seed)
    ks = jax.random.split(key, 20)
    f32 = jnp.float32

    def nrm(k, shape, fan_in, scale=1.0):
        return jax.random.normal(k, shape, f32) * (fan_in ** -0.5) * scale

    def gain(k, shape):
        return 1.0 + 0.02 * jax.random.normal(k, shape, f32)

    x = jax.random.normal(ks[0], (BATCH, SEQ, D_MODEL), f32)
    mem = jax.random.normal(ks[1], (BATCH, N_MEM, D_MODEL), f32)
    offsets = jax.random.randint(ks[2], (BATCH, 1), 0, 4096, dtype=jnp.int32)
    positions = offsets + jnp.arange(SEQ, dtype=jnp.int32)[None, :]
    return {
        "x": x,
        "mem": mem,
        "positions": positions,
        "g_emb": gain(ks[3], (D_MODEL,)),
        "b_emb": 0.02 * jax.random.normal(ks[4], (D_MODEL,), f32),
        "w_in": nrm(ks[5], (DEPTH, D_MODEL, D_IN), D_MODEL),
        "g_cq": gain(ks[6], (DEPTH, MLA_Q_RANK)),
        "g_ckv": gain(ks[7], (DEPTH, MLA_KV_RANK)),
        "w_uq": nrm(ks[8], (DEPTH, MLA_Q_RANK, MLA_HEADS * (MLA_NOPE_DIM + MLA_ROPE_DIM)), MLA_Q_RANK),
        "w_ukv": nrm(ks[9], (DEPTH, MLA_KV_RANK, MLA_HEADS * (MLA_NOPE_DIM + MLA_V_DIM)), MLA_KV_RANK),
        "w_mem_kv": nrm(ks[10], (DEPTH, D_MODEL, 2 * MEM_WIDTH), D_MODEL),
        "g_out_a": gain(ks[11], (DEPTH, A_WIDTH)),
        "g_out_b": gain(ks[12], (DEPTH, MLA_WIDTH)),
        "g_out_m": gain(ks[13], (DEPTH, MEM_WIDTH)),
        "w_out": nrm(ks[14], (DEPTH, D_MIX, D_MODEL), D_MIX, DEEPNORM_BETA),
        "g_post": gain(ks[15], (DEPTH, D_MODEL)),
        "b_post": 0.02 * jax.random.normal(ks[16], (DEPTH, D_MODEL), f32),
    }


def _fwd_reference(x, mem, positions, g_emb, b_emb, w_in, g_cq, g_ckv, w_uq, w_ukv, w_mem_kv,
              g_out_a, g_out_b, g_out_m, w_out, g_post, b_post):
    h = _layer_norm(x, g_emb, b_emb)
    for l in range(DEPTH):
        h = _hybrid_layer(h, positions, mem, w_in[l], g_cq[l], g_ckv[l], w_uq[l], w_ukv[l], w_mem_kv[l],
                          g_out_a[l], g_out_b[l], g_out_m[l], w_out[l], g_post[l], b_post[l])
    return h


import jax as _jax
import jax.numpy as _jnp

TWIN_FORMAT = 'train_step'
FWD_PARAMS = ['x', 'mem', 'positions', 'g_emb', 'b_emb', 'w_in', 'g_cq', 'g_ckv', 'w_uq', 'w_ukv', 'w_mem_kv', 'g_out_a', 'g_out_b', 'g_out_m', 'w_out', 'g_post', 'b_post']
TWIN_WEIGHTS = ['g_emb', 'b_emb', 'w_in', 'g_cq', 'g_ckv', 'w_uq', 'w_ukv', 'w_mem_kv', 'g_out_a', 'g_out_b', 'g_out_m', 'w_out', 'g_post', 'b_post']
TWIN_DIFF_INPUT = 'x'
TWIN_INPUTS = ['x', 'mem', 'positions', 'g_emb', 'b_emb', 'w_in', 'g_cq', 'g_ckv', 'w_uq', 'w_ukv', 'w_mem_kv', 'g_out_a', 'g_out_b', 'g_out_m', 'w_out', 'g_post', 'b_post', 'loss_target', 'm_g_emb', 'm_b_emb', 'm_w_in', 'm_g_cq', 'm_g_ckv', 'm_w_uq', 'm_w_ukv', 'm_w_mem_kv', 'm_g_out_a', 'm_g_out_b', 'm_g_out_m', 'm_w_out', 'm_g_post', 'm_b_post', 'v_g_emb', 'v_b_emb', 'v_w_in', 'v_g_cq', 'v_g_ckv', 'v_w_uq', 'v_w_ukv', 'v_w_mem_kv', 'v_g_out_a', 'v_g_out_b', 'v_g_out_m', 'v_w_out', 'v_g_post', 'v_b_post']
TWIN_OUTPUTS = ['loss', 'grad_x', 'grad_g_emb', 'grad_b_emb', 'grad_w_in', 'grad_g_cq', 'grad_g_ckv', 'grad_w_uq', 'grad_w_ukv', 'grad_w_mem_kv', 'grad_g_out_a', 'grad_g_out_b', 'grad_g_out_m', 'grad_w_out', 'grad_g_post', 'grad_b_post', 'delta_g_emb', 'delta_b_emb', 'delta_w_in', 'delta_g_cq', 'delta_g_ckv', 'delta_w_uq', 'delta_w_ukv', 'delta_w_mem_kv', 'delta_g_out_a', 'delta_g_out_b', 'delta_g_out_m', 'delta_w_out', 'delta_g_post', 'delta_b_post', 'new_m_g_emb', 'new_m_b_emb', 'new_m_w_in', 'new_m_g_cq', 'new_m_g_ckv', 'new_m_w_uq', 'new_m_w_ukv', 'new_m_w_mem_kv', 'new_m_g_out_a', 'new_m_g_out_b', 'new_m_g_out_m', 'new_m_w_out', 'new_m_g_post', 'new_m_b_post', 'new_v_g_emb', 'new_v_b_emb', 'new_v_w_in', 'new_v_g_cq', 'new_v_g_ckv', 'new_v_w_uq', 'new_v_w_ukv', 'new_v_w_mem_kv', 'new_v_g_out_a', 'new_v_g_out_b', 'new_v_g_out_m', 'new_v_w_out', 'new_v_g_post', 'new_v_b_post']
TWIN_LEAF_KINDS = {'loss': 'loss', 'grad_x': 'grad_x', 'grad_g_emb': 'grad_w', 'grad_b_emb': 'grad_w', 'grad_w_in': 'grad_w', 'grad_g_cq': 'grad_w', 'grad_g_ckv': 'grad_w', 'grad_w_uq': 'grad_w', 'grad_w_ukv': 'grad_w', 'grad_w_mem_kv': 'grad_w', 'grad_g_out_a': 'grad_w', 'grad_g_out_b': 'grad_w', 'grad_g_out_m': 'grad_w', 'grad_w_out': 'grad_w', 'grad_g_post': 'grad_w', 'grad_b_post': 'grad_w', 'delta_g_emb': 'delta_w', 'delta_b_emb': 'delta_w', 'delta_w_in': 'delta_w', 'delta_g_cq': 'delta_w', 'delta_g_ckv': 'delta_w', 'delta_w_uq': 'delta_w', 'delta_w_ukv': 'delta_w', 'delta_w_mem_kv': 'delta_w', 'delta_g_out_a': 'delta_w', 'delta_g_out_b': 'delta_w', 'delta_g_out_m': 'delta_w', 'delta_w_out': 'delta_w', 'delta_g_post': 'delta_w', 'delta_b_post': 'delta_w', 'new_m_g_emb': 'new_m', 'new_m_b_emb': 'new_m', 'new_m_w_in': 'new_m', 'new_m_g_cq': 'new_m', 'new_m_g_ckv': 'new_m', 'new_m_w_uq': 'new_m', 'new_m_w_ukv': 'new_m', 'new_m_w_mem_kv': 'new_m', 'new_m_g_out_a': 'new_m', 'new_m_g_out_b': 'new_m', 'new_m_g_out_m': 'new_m', 'new_m_w_out': 'new_m', 'new_m_g_post': 'new_m', 'new_m_b_post': 'new_m', 'new_v_g_emb': 'new_v', 'new_v_b_emb': 'new_v', 'new_v_w_in': 'new_v', 'new_v_g_cq': 'new_v', 'new_v_g_ckv': 'new_v', 'new_v_w_uq': 'new_v', 'new_v_w_ukv': 'new_v', 'new_v_w_mem_kv': 'new_v', 'new_v_g_out_a': 'new_v', 'new_v_g_out_b': 'new_v', 'new_v_g_out_m': 'new_v', 'new_v_w_out': 'new_v', 'new_v_g_post': 'new_v', 'new_v_b_post': 'new_v'}


def _forward(args):
    return _fwd_reference(*[args[k] for k in FWD_PARAMS])


def _output_shape():
    out = _jax.eval_shape(lambda: _forward(_fwd_setup_inputs(0)))
    return out.shape, out.dtype

N_MICROBATCH = 1
ADAM_LR = 0.001
ADAM_B1 = 0.9
ADAM_B2 = 0.999
ADAM_EPS = 1e-08
ADAM_WD = 0.01
ADAM_STEP = 10
PER_EXAMPLE_BATCH_AXIS = {'x': 0, 'mem': 0, 'positions': 0, 'loss_target': 0}
SHARED_INPUTS = []
_WEIGHT_DTYPES = {'g_emb': _jnp.float32, 'b_emb': _jnp.float32, 'w_in': _jnp.float32, 'g_cq': _jnp.float32, 'g_ckv': _jnp.float32, 'w_uq': _jnp.float32, 'w_ukv': _jnp.float32, 'w_mem_kv': _jnp.float32, 'g_out_a': _jnp.float32, 'g_out_b': _jnp.float32, 'g_out_m': _jnp.float32, 'w_out': _jnp.float32, 'g_post': _jnp.float32, 'b_post': _jnp.float32}
MOMENT_SCALE = {'g_emb': 9.528774e-01, 'b_emb': 8.660039e-01, 'w_in': 3.772511e-02, 'g_cq': 4.796140e-02, 'g_ckv': 1.485723e-01, 'w_uq': 2.851589e-02, 'w_ukv': 3.545336e-02, 'w_mem_kv': 3.602894e-02, 'g_out_a': 3.683415e-02, 'g_out_b': 4.022919e-02, 'g_out_m': 3.601730e-02, 'w_out': 8.622117e-02, 'g_post': 3.200162e+01, 'b_post': 8.726190e-01}


def _to_microbatches(a, axis):
    t = _jnp.moveaxis(a, axis, 0)
    t = t.reshape((N_MICROBATCH, t.shape[0] // N_MICROBATCH) + t.shape[1:])
    return _jnp.moveaxis(t, 1, axis + 1)


def setup_inputs(seed: int = 0) -> dict:
    inp = _fwd_setup_inputs(seed)
    key = _jax.random.fold_in(_jax.random.key(seed), 7919)
    shape, _ = _output_shape()
    out = dict(inp)
    out["loss_target"] = _jax.random.normal(_jax.random.fold_in(key, 0), shape, _jnp.float32)
    for i, name in enumerate(TWIN_WEIGHTS):
        w = inp[name].astype(_jnp.float32)
        if MOMENT_SCALE is None:
            s = _jnp.sqrt(_jnp.mean(_jnp.square(w)) + 1e-30)
        else:
            s = MOMENT_SCALE[name]
        km, kv = _jax.random.split(_jax.random.fold_in(key, i + 1))
        out[name] = w
        out["m_" + name] = s * _jax.random.normal(km, w.shape, _jnp.float32)
        out["v_" + name] = (s * s) * _jax.random.uniform(kv, w.shape, _jnp.float32, 0.5, 1.5)
    if N_MICROBATCH > 1:
        for name, axis in PER_EXAMPLE_BATCH_AXIS.items():
            out[name] = _to_microbatches(out[name], axis)
    return {'x': out['x'], 'mem': out['mem'], 'positions': out['positions'], 'g_emb': out['g_emb'], 'b_emb': out['b_emb'], 'w_in': out['w_in'], 'g_cq': out['g_cq'], 'g_ckv': out['g_ckv'], 'w_uq': out['w_uq'], 'w_ukv': out['w_ukv'], 'w_mem_kv': out['w_mem_kv'], 'g_out_a': out['g_out_a'], 'g_out_b': out['g_out_b'], 'g_out_m': out['g_out_m'], 'w_out': out['w_out'], 'g_post': out['g_post'], 'b_post': out['b_post'], 'loss_target': out['loss_target'], 'm_g_emb': out['m_g_emb'], 'm_b_emb': out['m_b_emb'], 'm_w_in': out['m_w_in'], 'm_g_cq': out['m_g_cq'], 'm_g_ckv': out['m_g_ckv'], 'm_w_uq': out['m_w_uq'], 'm_w_ukv': out['m_w_ukv'], 'm_w_mem_kv': out['m_w_mem_kv'], 'm_g_out_a': out['m_g_out_a'], 'm_g_out_b': out['m_g_out_b'], 'm_g_out_m': out['m_g_out_m'], 'm_w_out': out['m_w_out'], 'm_g_post': out['m_g_post'], 'm_b_post': out['m_b_post'], 'v_g_emb': out['v_g_emb'], 'v_b_emb': out['v_b_emb'], 'v_w_in': out['v_w_in'], 'v_g_cq': out['v_g_cq'], 'v_g_ckv': out['v_g_ckv'], 'v_w_uq': out['v_w_uq'], 'v_w_ukv': out['v_w_ukv'], 'v_w_mem_kv': out['v_w_mem_kv'], 'v_g_out_a': out['v_g_out_a'], 'v_g_out_b': out['v_g_out_b'], 'v_g_out_m': out['v_g_out_m'], 'v_w_out': out['v_w_out'], 'v_g_post': out['v_g_post'], 'v_b_post': out['v_b_post']}


def _loss(weights, diff, rest, loss_target):
    with _jax.named_scope("forward"):
        args = {**rest, TWIN_DIFF_INPUT: diff, **{k: w.astype(_WEIGHT_DTYPES[k]) for k, w in weights.items()}}
        y = _forward(args)
    with _jax.named_scope("loss_head"):
        err = _jnp.square(y.astype(_jnp.float32) - loss_target)
        return 0.5 * _jnp.sum(_jnp.mean(err, axis=-1)) if err.ndim else 0.5 * err


def _adamw(w, g, m, v):
    m = ADAM_B1 * m + (1.0 - ADAM_B1) * g
    v = ADAM_B2 * v + (1.0 - ADAM_B2) * _jnp.square(g)
    m_hat = m / (1.0 - ADAM_B1 ** ADAM_STEP)
    v_hat = v / (1.0 - ADAM_B2 ** ADAM_STEP)
    delta = -ADAM_LR * (m_hat / (_jnp.sqrt(v_hat) + ADAM_EPS) + ADAM_WD * w)
    return delta, m, v


def reference(x, mem, positions, g_emb, b_emb, w_in, g_cq, g_ckv, w_uq, w_ukv, w_mem_kv, g_out_a, g_out_b, g_out_m, w_out, g_post, b_post, loss_target, m_g_emb, m_b_emb, m_w_in, m_g_cq, m_g_ckv, m_w_uq, m_w_ukv, m_w_mem_kv, m_g_out_a, m_g_out_b, m_g_out_m, m_w_out, m_g_post, m_b_post, v_g_emb, v_b_emb, v_w_in, v_g_cq, v_g_ckv, v_w_uq, v_w_ukv, v_w_mem_kv, v_g_out_a, v_g_out_b, v_g_out_m, v_w_out, v_g_post, v_b_post):
    given = dict(x=x, mem=mem, positions=positions, g_emb=g_emb, b_emb=b_emb, w_in=w_in, g_cq=g_cq, g_ckv=g_ckv, w_uq=w_uq, w_ukv=w_ukv, w_mem_kv=w_mem_kv, g_out_a=g_out_a, g_out_b=g_out_b, g_out_m=g_out_m, w_out=w_out, g_post=g_post, b_post=b_post, loss_target=loss_target, m_g_emb=m_g_emb, m_b_emb=m_b_emb, m_w_in=m_w_in, m_g_cq=m_g_cq, m_g_ckv=m_g_ckv, m_w_uq=m_w_uq, m_w_ukv=m_w_ukv, m_w_mem_kv=m_w_mem_kv, m_g_out_a=m_g_out_a, m_g_out_b=m_g_out_b, m_g_out_m=m_g_out_m, m_w_out=m_w_out, m_g_post=m_g_post, m_b_post=m_b_post, v_g_emb=v_g_emb, v_b_emb=v_b_emb, v_w_in=v_w_in, v_g_cq=v_g_cq, v_g_ckv=v_g_ckv, v_w_uq=v_w_uq, v_w_ukv=v_w_ukv, v_w_mem_kv=v_w_mem_kv, v_g_out_a=v_g_out_a, v_g_out_b=v_g_out_b, v_g_out_m=v_g_out_m, v_w_out=v_w_out, v_g_post=v_g_post, v_b_post=v_b_post)
    weights = {n: given[n] for n in TWIN_WEIGHTS}
    shared = {n: given[n] for n in SHARED_INPUTS}
    per_example = {n: given[n] for n in ['x', 'mem', 'positions']}
    grad_fn = _jax.value_and_grad(_loss, argnums=(0, 1))

    def one_microbatch(ex, loss_target):
        ex = dict(ex)
        diff = ex.pop(TWIN_DIFF_INPUT)
        return grad_fn(weights, diff, {**shared, **ex}, loss_target)

    if N_MICROBATCH == 1:
        loss, (grad_w, grad_x) = one_microbatch(per_example, given["loss_target"])
    else:
        def body(carry, xs):
            loss_sum, grad_sum = carry
            l_k, (gw_k, gx_k) = one_microbatch(xs[0], xs[1])
            with _jax.named_scope("update"):
                return (loss_sum + l_k, _jax.tree.map(_jnp.add, grad_sum, gw_k)), gx_k

        init = (_jnp.zeros((), _jnp.float32), _jax.tree.map(_jnp.zeros_like, weights))
        (loss, grad_w), grad_x = _jax.lax.scan(body, init, (per_example, given["loss_target"]))
    with _jax.named_scope("update"):
        delta_w, new_m, new_v = {}, {}, {}
        for n in TWIN_WEIGHTS:
            delta_w[n], new_m[n], new_v[n] = _adamw(weights[n], grad_w[n], given["m_" + n], given["v_" + n])
    return (loss, grad_x, *[grad_w[n] for n in TWIN_WEIGHTS], *[delta_w[n] for n in TWIN_WEIGHTS],
            *[new_m[n] for n in TWIN_WEIGHTS], *[new_v[n] for n in TWIN_WEIGHTS])
```

```python
import functools

import jax
import jax.numpy as jnp
from jax import lax
from jax.experimental import pallas as pl
from jax.experimental.pallas import tpu as pltpu

F32 = jnp.float32
BF16 = jnp.bfloat16
SDS = jax.ShapeDtypeStruct
MESH = pl.DeviceIdType.MESH

D_MODEL = 1024
SEQ = 2048
A_HEADS, A_HEAD_DIM, A_ROT = 16, 64, 16
A_WIDTH = 1024
DILATIONS = (1, 4, 16)
N_SIDE = 64
MLA_HEADS, MLA_Q_RANK, MLA_KV_RANK = 8, 256, 128
MLA_NOPE, MLA_ROPE, MLA_V = 64, 32, 64
MLA_WIDTH = 512
N_MEM, MEM_HEADS, MEM_HEAD_DIM, MEM_WIDTH = 256, 4, 128, 512
ROPE_THETA = 500000.0
NORM_EPS = 1e-5
NEG_INF = -1e30
ALPHA = 2.0 ** 0.25
D_IN = 6048
N_DEV = 8

ADAM_LR, ADAM_B1, ADAM_B2, ADAM_EPS, ADAM_WD, ADAM_STEP = 0.001, 0.9, 0.999, 1e-08, 0.01, 10

D_INW = 6144
PIECE_WIDTHS = (1024, 1024, 1024, 1024, 512, 512, 512, 512)
PIECE_OFFS = (0, 1024, 2048, 3072, 4096, 4608, 5120, 5632)
LANES = 128
VMEM_LIMIT = 56 * 1024 * 1024


def _params(*sem):
    kw = dict(vmem_limit_bytes=VMEM_LIMIT)
    if sem:
        kw["dimension_semantics"] = sem
    return pltpu.CompilerParams(**kw)


def _dot(a, b):
    return jnp.dot(a, b, preferred_element_type=F32)


def _dot_nt(a, b):
    return lax.dot_general(a, b, (((1,), (1,)), ((), ())), preferred_element_type=F32)


def _dot_tn(a, b):
    return lax.dot_general(a, b, (((0,), (0,)), ((), ())), preferred_element_type=F32)


def _sigmoid(x):
    return 1.0 / (1.0 + jnp.exp(-x))


def _rope_fwd(x, c, sa, sb, half):
    n = x.shape[-1]
    return x * c + pltpu.roll(x, n - half, 1) * sa + pltpu.roll(x, half, 1) * sb


def _rope_bwd(dy, c, sa, sb, half):
    n = dy.shape[-1]
    return dy * c + pltpu.roll(dy * sa, half, 1) + pltpu.roll(dy * sb, n - half, 1)


def mm_nn(a, b, out_dtype, tm, tn, name):
    m, k = a.shape
    n = b.shape[1]

    def body(a_ref, b_ref, o_ref):
        o_ref[...] = _dot(a_ref[...].astype(BF16), b_ref[...].astype(BF16)).astype(o_ref.dtype)

    return pl.pallas_call(
        body, grid=(n // tn, m // tm),
        in_specs=[pl.BlockSpec((tm, k), lambda j, i: (i, 0)), pl.BlockSpec((k, tn), lambda j, i: (0, j))],
        out_specs=pl.BlockSpec((tm, tn), lambda j, i: (i, j)),
        out_shape=SDS((m, n), out_dtype), name=name,
        compiler_params=_params("parallel", "parallel"))(a, b)


def mm_tn(a, b, tt, name):
    t, m = a.shape
    n = b.shape[1]

    def body(a_ref, b_ref, o_ref):
        @pl.when(pl.program_id(0) == 0)
        def _():
            o_ref[...] = jnp.zeros_like(o_ref)

        o_ref[...] += _dot_tn(a_ref[...].astype(BF16), b_ref[...].astype(BF16))

    return pl.pallas_call(
        body, grid=(t // tt,),
        in_specs=[pl.BlockSpec((tt, m), lambda i: (i, 0)), pl.BlockSpec((tt, n), lambda i: (i, 0))],
        out_specs=pl.BlockSpec((m, n), lambda i: (0, 0)),
        out_shape=SDS((m, n), F32), name=name,
        compiler_params=_params("arbitrary"))(a, b)


def ln_emb_fwd(x2, g, b):
    t, d = x2.shape
    tm = 512

    def body(x_ref, g_ref, b_ref, h32_ref, h16_ref):
        x = x_ref[...]
        mu = jnp.mean(x, axis=-1, keepdims=True)
        xc = x - mu
        var = jnp.mean(xc * xc, axis=-1, keepdims=True)
        h = xc * lax.rsqrt(var + NORM_EPS) * g_ref[...] + b_ref[...]
        h32_ref[...] = h
        h16_ref[...] = h.astype(BF16)

    row = pl.BlockSpec((tm, d), lambda i: (i, 0))
    vec = pl.BlockSpec((1, d), lambda i: (0, 0))
    return pl.pallas_call(
        body, grid=(t // tm,), in_specs=[row, vec, vec], out_specs=[row, row],
        out_shape=[SDS((t, d), F32), SDS((t, d), BF16)], name="ln_emb_fwd",
        compiler_params=_params("parallel"))(x2, g, b)


Q_BLK = 128


def _pattern_geometry(d):
    length = SEQ // d
    nblk = length // Q_BLK
    kwin = min(2 * Q_BLK, length)
    return length, nblk, kwin


def _block_coords(idx, d):
    length, nblk, kwin = _pattern_geometry(d)
    r = lax.shift_right_logical(idx, nblk.bit_length() - 1)
    i = idx & (nblk - 1)
    q0 = pl.multiple_of(r * length + i * Q_BLK, Q_BLK)
    ks = jnp.clip(i * Q_BLK - N_SIDE, 0, length - kwin)
    k0 = pl.multiple_of(r * length + ks, N_SIDE)
    qpos = i * Q_BLK + lax.broadcasted_iota(jnp.int32, (Q_BLK, kwin), 0)
    kpos = ks + lax.broadcasted_iota(jnp.int32, (Q_BLK, kwin), 1)
    valid = jnp.abs(kpos - qpos) <= N_SIDE
    return q0, k0, kwin, valid


def _deinterleave(src_ref, dst_ref, d, dtype):
    length = SEQ // d
    if d == 1:
        dst_ref[...] = src_ref[...].astype(dtype)
        return
    for r in range(d):
        dst_ref[r * length:(r + 1) * length, :] = src_ref[pl.ds(r, length, stride=d), :].astype(dtype)


def a_attn_fwd(proj, ca, sa, sb, nb):
    t = proj.shape[0]
    n_pairs = A_WIDTH // LANES

    def body(q_ref, k_ref, v_ref, c_ref, sa_ref, sb_ref, y_ref, lse_ref,
             qr_s, kr_s, qd_s, kd_s, vd_s, oc_s, lc_s, o1_s, l1_s, o2_s, l2_s, o3_s, l3_s):
        c, s_a, s_b = c_ref[...], sa_ref[...], sb_ref[...]
        qr_s[...] = _rope_fwd(q_ref[...], c, s_a, s_b, A_ROT // 2) * (A_HEAD_DIM ** -0.5)
        kr_s[...] = _rope_fwd(k_ref[...], c, s_a, s_b, A_ROT // 2)
        head0 = lax.broadcasted_iota(jnp.int32, (Q_BLK, LANES), 1) < A_HEAD_DIM
        nat = ((o1_s, l1_s), (o2_s, l2_s), (o3_s, l3_s))

        for g, d in enumerate(DILATIONS):
            length, nblk, _ = _pattern_geometry(d)
            _deinterleave(qr_s, qd_s, d, BF16)
            _deinterleave(kr_s, kd_s, d, BF16)
            _deinterleave(v_ref, vd_s, d, BF16)
            o_dst, l_dst = (nat[g] if d == 1 else (oc_s, lc_s))

            def block(idx, carry, d=d, o_dst=o_dst, l_dst=l_dst):
                q0, k0, kwin, valid = _block_coords(idx, d)
                qb = qd_s[pl.ds(q0, Q_BLK), :]
                kb = kd_s[pl.ds(k0, kwin), :]
                vb = vd_s[pl.ds(k0, kwin), :]
                outs, lses = [], []
                for h in range(2):
                    qh = jnp.where(head0 if h == 0 else ~head0, qb, jnp.zeros_like(qb))
                    s = jnp.where(valid, _dot_nt(qh, kb), NEG_INF)
                    m = jnp.max(s, axis=-1, keepdims=True)
                    p = jnp.exp(s - m)
                    l = jnp.sum(p, axis=-1, keepdims=True)
                    outs.append(_dot(p.astype(BF16), vb) / l)
                    lses.append(m + jnp.log(l))
                o_dst[pl.ds(q0, Q_BLK), :] = jnp.where(head0, outs[0], outs[1])
                l_dst[pl.ds(q0, Q_BLK), :] = jnp.where(head0, lses[0], lses[1])
                return carry

            lax.fori_loop(0, SEQ // Q_BLK, block, 0)
            if d > 1:
                for r in range(d):
                    nat[g][0][pl.ds(r, length, stride=d), :] = oc_s[r * length:(r + 1) * length, :]
                    nat[g][1][pl.ds(r, length, stride=d), :] = lc_s[r * length:(r + 1) * length, :]

        def merge(ci, carry):
            rows = pl.ds(pl.multiple_of(ci * 256, 256), 256)
            l1, l2, l3 = l1_s[rows, :], l2_s[rows, :], l3_s[rows, :]
            m = jnp.maximum(jnp.maximum(l1, l2), l3)
            w1, w2, w3 = jnp.exp(l1 - m), jnp.exp(l2 - m), jnp.exp(l3 - m)
            w = w1 + w2 + w3
            y_ref[rows, :] = (w1 * o1_s[rows, :] + w2 * o2_s[rows, :] + w3 * o3_s[rows, :]) / w
            lse_ref[rows, :] = m + jnp.log(w)
            return carry

        lax.fori_loop(0, SEQ // 256, merge, 0)

    def col(off):
        return pl.BlockSpec((SEQ, LANES), lambda b, hp: (b, off + hp))

    tab = pl.BlockSpec((SEQ, LANES), lambda b, hp: (b, 0))
    out = pl.BlockSpec((SEQ, LANES), lambda b, hp: (b, hp))
    f32s = pltpu.VMEM((SEQ, LANES), F32)
    b16s = pltpu.VMEM((SEQ, LANES), BF16)
    return pl.pallas_call(
        body, grid=(nb, n_pairs),
        in_specs=[col(0), col(n_pairs), col(2 * n_pairs), tab, tab, tab],
        out_specs=[out, out],
        out_shape=[SDS((t, A_WIDTH), F32), SDS((t, A_WIDTH), F32)],
        scratch_shapes=[f32s, f32s, b16s, b16s, b16s] + [f32s] * 8,
        name="a_attn_fwd", compiler_params=_params("parallel", "parallel"))(proj, proj, proj, ca, sa, sb)


def a_attn_bwd(proj, ca, sa, sb, dy, y, lse, nb):
    t = proj.shape[0]
    n_pairs = A_WIDTH // LANES

    def body(q_ref, k_ref, v_ref, c_ref, sa_ref, sb_ref, do_ref, y_ref, lse_ref, dq_ref, dk_ref, dv_ref,
             qr_s, kr_s, dd_s, qd_s, kd_s, vd_s, dod_s, lsed_s, ddd_s, dqc_s, dkc_s, dvc_s, dqn_s, dkn_s, dvn_s):
        c, s_a, s_b = c_ref[...], sa_ref[...], sb_ref[...]
        qr_s[...] = _rope_fwd(q_ref[...], c, s_a, s_b, A_ROT // 2) * (A_HEAD_DIM ** -0.5)
        kr_s[...] = _rope_fwd(k_ref[...], c, s_a, s_b, A_ROT // 2)
        head0 = lax.broadcasted_iota(jnp.int32, (Q_BLK, LANES), 1) < A_HEAD_DIM

        def rowdot(ci, carry):
            rows = pl.ds(pl.multiple_of(ci * 256, 256), 256)
            h0 = lax.broadcasted_iota(jnp.int32, (256, LANES), 1) < A_HEAD_DIM
            tt = do_ref[rows, :] * y_ref[rows, :]
            d0 = jnp.sum(jnp.where(h0, tt, 0.0), axis=-1, keepdims=True)
            d1 = jnp.sum(jnp.where(h0, 0.0, tt), axis=-1, keepdims=True)
            dd_s[rows, :] = jnp.where(h0, d0, d1)
            return carry

        lax.fori_loop(0, SEQ // 256, rowdot, 0)
        dqn_s[...] = jnp.zeros_like(dqn_s)
        dkn_s[...] = jnp.zeros_like(dkn_s)
        dvn_s[...] = jnp.zeros_like(dvn_s)

        for d in DILATIONS:
            length, nblk, _ = _pattern_geometry(d)
            _deinterleave(qr_s, qd_s, d, BF16)
            _deinterleave(kr_s, kd_s, d, BF16)
            _deinterleave(v_ref, vd_s, d, BF16)
            _deinterleave(do_ref, dod_s, d, BF16)
            _deinterleave(lse_ref, lsed_s, d, F32)
            _deinterleave(dd_s, ddd_s, d, F32)
            dkc_s[...] = jnp.zeros_like(dkc_s)
            dvc_s[...] = jnp.zeros_like(dvc_s)

            def block(idx, carry, d=d):
                q0, k0, kwin, valid = _block_coords(idx, d)
                qb = qd_s[pl.ds(q0, Q_BLK), :]
                dob = dod_s[pl.ds(q0, Q_BLK), :]
                kb = kd_s[pl.ds(k0, kwin), :]
                vb = vd_s[pl.ds(k0, kwin), :]
                lseb = lsed_s[pl.ds(q0, Q_BLK), :]
                ddb = ddd_s[pl.ds(q0, Q_BLK), :]
                dqs = []
                dk = jnp.zeros((kwin, LANES), F32)
                dv = jnp.zeros((kwin, LANES), F32)
                for h in range(2):
                    sel = head0 if h == 0 else ~head0
                    lo = h * A_HEAD_DIM
                    qh = jnp.where(sel, qb, jnp.zeros_like(qb))
                    doh = jnp.where(sel, dob, jnp.zeros_like(dob))
                    s = jnp.where(valid, _dot_nt(qh, kb), NEG_INF)
                    p = jnp.exp(s - lseb[:, lo:lo + 1])
                    dp = _dot_nt(doh, vb)
                    ds = (p * (dp - ddb[:, lo:lo + 1])).astype(BF16)
                    dqs.append(_dot(ds, kb))
                    dk = dk + _dot_tn(ds, qh)
                    dv = dv + _dot_tn(p.astype(BF16), doh)
                dqc_s[pl.ds(q0, Q_BLK), :] = jnp.where(head0, dqs[0], dqs[1])
                dkc_s[pl.ds(k0, kwin), :] += dk
                dvc_s[pl.ds(k0, kwin), :] += dv
                return carry

            lax.fori_loop(0, SEQ // Q_BLK, block, 0)
            for r in range(d):
                cls = slice(r * length, (r + 1) * length)
                rows = pl.ds(r, length, stride=d) if d > 1 else slice(None)
                dqn_s[rows, :] = dqn_s[rows, :] + dqc_s[cls, :]
                dkn_s[rows, :] = dkn_s[rows, :] + dkc_s[cls, :]
                dvn_s[rows, :] = dvn_s[rows, :] + dvc_s[cls, :]

        dq_ref[...] = _rope_bwd(dqn_s[...] * (A_HEAD_DIM ** -0.5), c, s_a, s_b, A_ROT // 2).astype(BF16)
        dk_ref[...] = _rope_bwd(dkn_s[...], c, s_a, s_b, A_ROT // 2).astype(BF16)
        dv_ref[...] = dvn_s[...].astype(BF16)

    def col(off):
        return pl.BlockSpec((SEQ, LANES), lambda b, hp: (b, off + hp))

    tab = pl.BlockSpec((SEQ, LANES), lambda b, hp: (b, 0))
    blk = pl.BlockSpec((SEQ, LANES), lambda b, hp: (b, hp))
    f32s = pltpu.VMEM((SEQ, LANES), F32)
    b16s = pltpu.VMEM((SEQ, LANES), BF16)
    return pl.pallas_call(
        body, grid=(nb, n_pairs),
        in_specs=[col(0), col(n_pairs), col(2 * n_pairs), tab, tab, tab, blk, blk, blk],
        out_specs=[blk, blk, blk],
        out_shape=[SDS((t, A_WIDTH), BF16)] * 3,
        scratch_shapes=[f32s, f32s, f32s, b16s, b16s, b16s, b16s] + [f32s] * 8,
        name="a_attn_bwd", compiler_params=_params("parallel", "parallel"))(proj, proj, proj, ca, sa, sb, dy, y, lse)


MLA_SCALE = (MLA_NOPE + MLA_ROPE) ** -0.5
MLA_QW = MLA_HEADS * LANES
MLA_KVW = MLA_QW + MLA_WIDTH


def _rms(x, g):
    r = lax.rsqrt(jnp.mean(x * x, axis=-1, keepdims=True) + NORM_EPS)
    return x * r * g, r


def _rms_bwd(dn, x, r, g):
    tg = dn * g
    dx = r * tg - x * (r * r * r) * jnp.mean(tg * x, axis=-1, keepdims=True)
    return dx, jnp.sum(dn * x * r, axis=0, keepdims=True)


def mla_prep_fwd(proj, cm, sma, smb, g_cq, g_ckv, wuq, wkv):
    t = proj.shape[0]
    tm = 512

    def body(cq_ref, ckv_ref, kr_ref, c_ref, sa_ref, sb_ref, gq_ref, gkv_ref, wuq_ref, wkv_ref, q_ref, k_ref, v_ref):
        c, s_a, s_b = c_ref[...], sa_ref[...], sb_ref[...]
        cqn, _ = _rms(cq_ref[...], gq_ref[...])
        qf = _dot(cqn.astype(BF16), wuq_ref[...])
        ckvn, _ = _rms(ckv_ref[...], gkv_ref[...])
        kvf = _dot(ckvn.astype(BF16), wkv_ref[...])
        krope = _rope_fwd(kr_ref[...], c, s_a, s_b, MLA_ROPE // 2)
        for h in range(MLA_HEADS):
            cols = slice(h * LANES, (h + 1) * LANES)
            q_ref[:, cols] = (_rope_fwd(qf[:, cols], c, s_a, s_b, MLA_ROPE // 2) * MLA_SCALE).astype(BF16)
            k_ref[:, cols] = (kvf[:, cols] + krope).astype(BF16)
        v_ref[...] = kvf[:, MLA_QW:].astype(BF16)

    def row(w, j):
        return pl.BlockSpec((tm, w), lambda i: (i, j))

    def full(a):
        return pl.BlockSpec(a.shape, lambda i: (0, 0))

    return pl.pallas_call(
        body, grid=(t // tm,),
        in_specs=[row(256, 4096 // 256), row(128, 4352 // 128), row(128, 4480 // 128), row(128, 0), row(128, 0), row(128, 0),
                  full(g_cq), full(g_ckv), full(wuq), full(wkv)],
        out_specs=[row(MLA_QW, 0), row(MLA_QW, 0), row(MLA_WIDTH, 0)],
        out_shape=[SDS((t, MLA_QW), BF16), SDS((t, MLA_QW), BF16), SDS((t, MLA_WIDTH), BF16)],
        name="mla_prep_fwd", compiler_params=_params("parallel"))(proj, proj, proj, cm, sma, smb, g_cq, g_ckv, wuq, wkv)


def mla_prep_bwd(proj, cm, sma, smb, g_cq, g_ckv, wuq, wkv, dq, dk, dv):
    t = proj.shape[0]
    tm = 512

    def body(cq_ref, ckv_ref, c_ref, sa_ref, sb_ref, gq_ref, gkv_ref, wuq_ref, wkv_ref, dq_ref, dk_ref, dv_ref,
             dcc_ref, dqf_ref, cqn_ref, dkvf_ref, ckvn_ref, dgq_ref, dgkv_ref):
        @pl.when(pl.program_id(0) == 0)
        def _():
            dgq_ref[...] = jnp.zeros_like(dgq_ref)
            dgkv_ref[...] = jnp.zeros_like(dgkv_ref)

        c, s_a, s_b = c_ref[...], sa_ref[...], sb_ref[...]
        cq, ckv = cq_ref[...], ckv_ref[...]
        cqn, rq = _rms(cq, gq_ref[...])
        ckvn, rkv = _rms(ckv, gkv_ref[...])
        cqn_ref[...] = cqn.astype(BF16)
        ckvn_ref[...] = ckvn.astype(BF16)
        lane = lax.broadcasted_iota(jnp.int32, (tm, LANES), 1)
        rope_lanes = (lane >= MLA_NOPE) & (lane < MLA_NOPE + MLA_ROPE)
        dkrope = jnp.zeros((tm, LANES), F32)
        for h in range(MLA_HEADS):
            cols = slice(h * LANES, (h + 1) * LANES)
            dqf_ref[:, cols] = _rope_bwd(dq_ref[:, cols] * MLA_SCALE, c, s_a, s_b, MLA_ROPE // 2).astype(BF16)
            dkh = dk_ref[:, cols]
            dkvf_ref[:, cols] = dkh.astype(BF16)
            dkrope = dkrope + dkh
        dkvf_ref[:, MLA_QW:] = dv_ref[...].astype(BF16)
        dkr = _rope_bwd(jnp.where(rope_lanes, dkrope, 0.0), c, s_a, s_b, MLA_ROPE // 2)
        dcqn = _dot_nt(dqf_ref[...], wuq_ref[...])
        dckvn = _dot_nt(dkvf_ref[...], wkv_ref[...])
        dcq, dgq = _rms_bwd(dcqn, cq, rq, gq_ref[...])
        dckv, dgkv = _rms_bwd(dckvn, ckv, rkv, gkv_ref[...])
        dgq_ref[...] += dgq
        dgkv_ref[...] += dgkv
        dcc_ref[:, 0:256] = dcq.astype(BF16)
        dcc_ref[:, 256:384] = dckv.astype(BF16)
        dcc_ref[:, 384:512] = dkr.astype(BF16)

    def row(w, j):
        return pl.BlockSpec((tm, w), lambda i: (i, j))

    def full(a):
        return pl.BlockSpec(a.shape, lambda i: (0, 0))

    return pl.pallas_call(
        body, grid=(t // tm,),
        in_specs=[row(256, 4096 // 256), row(128, 4352 // 128), row(128, 0), row(128, 0), row(128, 0),
                  full(g_cq), full(g_ckv), full(wuq), full(wkv), row(MLA_QW, 0), row(MLA_QW, 0), row(MLA_WIDTH, 0)],
        out_specs=[row(512, 0), row(MLA_QW, 0), row(256, 0), row(MLA_KVW, 0), row(128, 0), full(g_cq), full(g_ckv)],
        out_shape=[SDS((t, 512), BF16), SDS((t, MLA_QW), BF16), SDS((t, 256), BF16), SDS((t, MLA_KVW), BF16),
                   SDS((t, 128), BF16), SDS(g_cq.shape, F32), SDS(g_ckv.shape, F32)],
        name="mla_prep_bwd", compiler_params=_params("arbitrary"))(proj, proj, cm, sma, smb, g_cq, g_ckv, wuq, wkv, dq, dk, dv)


MLA_TQ = 256


def mla_attn_fwd(qb, kb, vb, nb):
    t = qb.shape[0]
    nq = SEQ // MLA_TQ
    n_pairs = MLA_HEADS // 2

    def body(q_ref, k_ref, v_ref, y_ref, lse_ref):
        head0 = lax.broadcasted_iota(jnp.int32, (MLA_TQ, LANES), 1) < MLA_V
        v = v_ref[...]
        outs, lses = [], []
        for h in range(2):
            cols = slice(h * LANES, (h + 1) * LANES)
            s = _dot_nt(q_ref[:, cols], k_ref[:, cols])
            m = jnp.max(s, axis=-1, keepdims=True)
            p = jnp.exp(s - m)
            l = jnp.sum(p, axis=-1, keepdims=True)
            outs.append(_dot(p.astype(BF16), v) / l)
            lses.append(m + jnp.log(l))
        y_ref[...] = jnp.where(head0, outs[0], outs[1])
        lse_ref[...] = jnp.where(head0, lses[0], lses[1])

    return pl.pallas_call(
        body, grid=(nb, n_pairs, nq),
        in_specs=[pl.BlockSpec((MLA_TQ, 2 * LANES), lambda b, hp, i: (b * nq + i, hp)),
                  pl.BlockSpec((SEQ, 2 * LANES), lambda b, hp, i: (b, hp)),
                  pl.BlockSpec((SEQ, LANES), lambda b, hp, i: (b, hp))],
        out_specs=[pl.BlockSpec((MLA_TQ, LANES), lambda b, hp, i: (b * nq + i, hp))] * 2,
        out_shape=[SDS((t, MLA_WIDTH), F32)] * 2,
        name="mla_attn_fwd", compiler_params=_params("parallel", "parallel", "parallel"))(qb, kb, vb)


def mla_attn_bwd(qb, kb, vb, dy, y, lse, nb):
    t = qb.shape[0]
    nq = SEQ // MLA_TQ
    n_pairs = MLA_HEADS // 2

    def body(q_ref, k_ref, v_ref, do_ref, y_ref, lse_ref, dq_ref, dk_ref, dv_ref):
        @pl.when(pl.program_id(2) == 0)
        def _():
            dk_ref[...] = jnp.zeros_like(dk_ref)
            dv_ref[...] = jnp.zeros_like(dv_ref)

        head0 = lax.broadcasted_iota(jnp.int32, (MLA_TQ, LANES), 1) < MLA_V
        v = v_ref[...]
        do = do_ref[...]
        lse = lse_ref[...]
        tt = do * y_ref[...]
        dv = jnp.zeros((SEQ, LANES), F32)
        for h in range(2):
            sel = head0 if h == 0 else ~head0
            lo = h * MLA_V
            cols = slice(h * LANES, (h + 1) * LANES)
            q = q_ref[:, cols]
            k = k_ref[:, cols]
            dd = jnp.sum(jnp.where(sel, tt, 0.0), axis=-1, keepdims=True)
            doh = jnp.where(sel, do, 0.0).astype(BF16)
            p = jnp.exp(_dot_nt(q, k) - lse[:, lo:lo + 1])
            dp = _dot_nt(doh, v)
            ds = (p * (dp - dd)).astype(BF16)
            dq_ref[:, cols] = _dot(ds, k)
            dk_ref[:, cols] += _dot_tn(ds, q)
            dv = dv + _dot_tn(p.astype(BF16), doh)
        dv_ref[...] += dv

    qspec = pl.BlockSpec((MLA_TQ, 2 * LANES), lambda b, hp, i: (b * nq + i, hp))
    kspec = pl.BlockSpec((SEQ, 2 * LANES), lambda b, hp, i: (b, hp))
    vspec = pl.BlockSpec((SEQ, LANES), lambda b, hp, i: (b, hp))
    ospec = pl.BlockSpec((MLA_TQ, LANES), lambda b, hp, i: (b * nq + i, hp))
    return pl.pallas_call(
        body, grid=(nb, n_pairs, nq),
        in_specs=[qspec, kspec, vspec, ospec, ospec, ospec],
        out_specs=[qspec, kspec, vspec],
        out_shape=[SDS((t, MLA_QW), F32), SDS((t, MLA_QW), F32), SDS((t, MLA_WIDTH), F32)],
        name="mla_attn_bwd", compiler_params=_params("parallel", "parallel", "arbitrary"))(qb, kb, vb, dy, y, lse)


MEM_TQ = 512
MEM_SCALE = MEM_HEAD_DIM ** -0.5
MQ_BLK = 5120 // LANES


def mem_attn_fwd(proj, mkv, nb):
    t = proj.shape[0]
    nq = SEQ // MEM_TQ

    def body(q_ref, mk_ref, mv_ref, y_ref):
        s = _dot_nt(q_ref[...].astype(BF16), mk_ref[...]) * MEM_SCALE
        m = jnp.max(s, axis=-1, keepdims=True)
        p = jnp.exp(s - m)
        l = jnp.sum(p, axis=-1, keepdims=True)
        y_ref[...] = _dot(p.astype(BF16), mv_ref[...]) / l

    return pl.pallas_call(
        body, grid=(nb, MEM_HEADS, nq),
        in_specs=[pl.BlockSpec((MEM_TQ, LANES), lambda b, h, i: (b * nq + i, MQ_BLK + h)),
                  pl.BlockSpec((N_MEM, LANES), lambda b, h, i: (b, h)),
                  pl.BlockSpec((N_MEM, LANES), lambda b, h, i: (b, MEM_HEADS + h))],
        out_specs=pl.BlockSpec((MEM_TQ, LANES), lambda b, h, i: (b * nq + i, h)),
        out_shape=SDS((t, MEM_WIDTH), F32),
        name="mem_attn_fwd", compiler_params=_params("parallel", "parallel", "parallel"))(proj, mkv, mkv)


def mem_attn_bwd(proj, mkv, dy, nb):
    t = proj.shape[0]
    nq = SEQ // MEM_TQ

    def body(q_ref, mk_ref, mv_ref, do_ref, dq_ref, dmk_ref, dmv_ref):
        @pl.when(pl.program_id(2) == 0)
        def _():
            dmk_ref[...] = jnp.zeros_like(dmk_ref)
            dmv_ref[...] = jnp.zeros_like(dmv_ref)

        q = q_ref[...].astype(BF16)
        mk, mv = mk_ref[...], mv_ref[...]
        do = do_ref[...].astype(BF16)
        s = _dot_nt(q, mk) * MEM_SCALE
        e = jnp.exp(s - jnp.max(s, axis=-1, keepdims=True))
        p = e / jnp.sum(e, axis=-1, keepdims=True)
        dp = _dot_nt(do, mv)
        ds = (p * (dp - jnp.sum(p * dp, axis=-1, keepdims=True)) * MEM_SCALE).astype(BF16)
        dq_ref[...] = _dot(ds, mk).astype(BF16)
        dmk_ref[...] += _dot_tn(ds, q)
        dmv_ref[...] += _dot_tn(p.astype(BF16), do)

    ospec = pl.BlockSpec((MEM_TQ, LANES), lambda b, h, i: (b * nq + i, h))
    kspec = pl.BlockSpec((N_MEM, LANES), lambda b, h, i: (b, h))
    return pl.pallas_call(
        body, grid=(nb, MEM_HEADS, nq),
        in_specs=[pl.BlockSpec((MEM_TQ, LANES), lambda b, h, i: (b * nq + i, MQ_BLK + h)),
                  kspec, pl.BlockSpec((N_MEM, LANES), lambda b, h, i: (b, MEM_HEADS + h)), ospec],
        out_specs=[ospec, kspec, kspec],
        out_shape=[SDS((t, MEM_WIDTH), BF16), SDS((nb * N_MEM, MEM_WIDTH), F32), SDS((nb * N_MEM, MEM_WIDTH), F32)],
        name="mem_attn_bwd", compiler_params=_params("parallel", "parallel", "arbitrary"))(proj, mkv, mkv, dy)


ROW_TM = 256
AG_BLK = 3072 // 1024
BG_BLK = 4608 // 512
MG_BLK = 5632 // 512
GROUPS = ((0, A_WIDTH), (A_WIDTH, MLA_WIDTH), (A_WIDTH + MLA_WIDTH, MEM_WIDTH))
D_MIX = 2048


def _gate_specs():
    def row(w, j):
        return pl.BlockSpec((ROW_TM, w), lambda i: (i, j))

    def vec(w):
        return pl.BlockSpec((1, w), lambda i: (0, 0))

    ys = [row(A_WIDTH, 0), row(MLA_WIDTH, 0), row(MEM_WIDTH, 0)]
    gates = [row(A_WIDTH, AG_BLK), row(MLA_WIDTH, BG_BLK), row(MEM_WIDTH, MG_BLK)]
    gains = [vec(A_WIDTH), vec(MLA_WIDTH), vec(MEM_WIDTH)]
    return row, vec, ys, gates, gains


def gate_fwd(ya, yb, ym, proj, goa, gob, gom):
    t = ya.shape[0]
    row, vec, ys, gates, gains = _gate_specs()

    def body(ya_ref, yb_ref, ym_ref, ga_ref, gb_ref, gm_ref, goa_ref, gob_ref, gom_ref, z_ref):
        for (off, w), y_ref, g_ref, go_ref in zip(GROUPS, (ya_ref, yb_ref, ym_ref), (ga_ref, gb_ref, gm_ref),
                                                  (goa_ref, gob_ref, gom_ref)):
            n, _ = _rms(y_ref[...], go_ref[...])
            gt = g_ref[...]
            z_ref[:, off:off + w] = (n * (gt * _sigmoid(gt))).astype(BF16)

    return pl.pallas_call(
        body, grid=(t // ROW_TM,), in_specs=ys + gates + gains, out_specs=row(D_MIX, 0),
        out_shape=SDS((t, D_MIX), BF16), name="gate_fwd",
        compiler_params=_params("parallel"))(ya, yb, ym, proj, proj, proj, goa, gob, gom)


def out_ln_loss(z, wout, h32, target, gp, bp):
    t, d = h32.shape

    def body(z_ref, w_ref, h_ref, t_ref, gp_ref, bp_ref, du32_ref, du16_ref, loss_ref, dgp_ref, dbp_ref):
        @pl.when(pl.program_id(0) == 0)
        def _():
            loss_ref[...] = jnp.zeros_like(loss_ref)
            dgp_ref[...] = jnp.zeros_like(dgp_ref)
            dbp_ref[...] = jnp.zeros_like(dbp_ref)

        g = gp_ref[...]
        u = ALPHA * h_ref[...] + _dot(z_ref[...], w_ref[...])
        mu = jnp.mean(u, axis=-1, keepdims=True)
        uc = u - mu
        rstd = lax.rsqrt(jnp.mean(uc * uc, axis=-1, keepdims=True) + NORM_EPS)
        xhat = uc * rstd
        err = xhat * g + bp_ref[...] - t_ref[...]
        tok = jnp.sum(err * err, axis=-1, keepdims=True) * (1.0 / d)
        loss_ref[...] += 0.5 * jnp.sum(tok, axis=0, keepdims=True)
        dout = err * (1.0 / d)
        dxhat = dout * g
        du = rstd * (dxhat - jnp.mean(dxhat, axis=-1, keepdims=True)
                     - xhat * jnp.mean(dxhat * xhat, axis=-1, keepdims=True))
        du32_ref[...] = du
        du16_ref[...] = du.astype(BF16)
        dgp_ref[...] += jnp.sum(dout * xhat, axis=0, keepdims=True)
        dbp_ref[...] += jnp.sum(dout, axis=0, keepdims=True)

    row = pl.BlockSpec((ROW_TM, d), lambda i: (i, 0))
    vec = pl.BlockSpec((1, d), lambda i: (0, 0))
    return pl.pallas_call(
        body, grid=(t // ROW_TM,),
        in_specs=[pl.BlockSpec((ROW_TM, D_MIX), lambda i: (i, 0)), pl.BlockSpec((D_MIX, d), lambda i: (0, 0)), row, row, vec, vec],
        out_specs=[row, row, pl.BlockSpec((1, LANES), lambda i: (0, 0)), vec, vec],
        out_shape=[SDS((t, d), F32), SDS((t, d), BF16), SDS((1, LANES), F32), SDS((1, d), F32), SDS((1, d), F32)],
        name="out_ln_loss", compiler_params=_params("arbitrary"))(z, wout, h32, target, gp, bp)


def gate_bwd(du16, wout, ya, yb, ym, proj, goa, gob, gom):
    t = ya.shape[0]
    row, vec, ys, gates, gains = _gate_specs()

    def body(du_ref, w_ref, ya_ref, yb_ref, ym_ref, ga_ref, gb_ref, gm_ref, goa_ref, gob_ref, gom_ref,
             dya_ref, dyb_ref, dym_ref, dga_ref, dgb_ref, dgm_ref, dgoa_ref, dgob_ref, dgom_ref):
        @pl.when(pl.program_id(0) == 0)
        def _():
            dgoa_ref[...] = jnp.zeros_like(dgoa_ref)
            dgob_ref[...] = jnp.zeros_like(dgob_ref)
            dgom_ref[...] = jnp.zeros_like(dgom_ref)

        dz = _dot_nt(du_ref[...], w_ref[...])
        for (off, w), y_ref, g_ref, go_ref, dy_ref, dg_ref, dgo_ref in zip(
                GROUPS, (ya_ref, yb_ref, ym_ref), (ga_ref, gb_ref, gm_ref), (goa_ref, gob_ref, gom_ref),
                (dya_ref, dyb_ref, dym_ref), (dga_ref, dgb_ref, dgm_ref), (dgoa_ref, dgob_ref, dgom_ref)):
            dzg = dz[:, off:off + w]
            y, gt, go = y_ref[...], g_ref[...], go_ref[...]
            n, r = _rms(y, go)
            sg = _sigmoid(gt)
            dg_ref[...] = (dzg * n * (sg * (1.0 + gt * (1.0 - sg)))).astype(BF16)
            dy, dgo = _rms_bwd(dzg * (gt * sg), y, r, go)
            dy_ref[...] = dy
            dgo_ref[...] += dgo

    widths = (A_WIDTH, MLA_WIDTH, MEM_WIDTH)
    return pl.pallas_call(
        body, grid=(t // ROW_TM,),
        in_specs=[row(D_MODEL, 0), pl.BlockSpec((D_MIX, D_MODEL), lambda i: (0, 0))] + ys + gates + gains,
        out_specs=[row(w, 0) for w in widths] * 2 + [vec(w) for w in widths],
        out_shape=[SDS((t, w), F32) for w in widths] + [SDS((t, w), BF16) for w in widths] + [SDS((1, w), F32) for w in widths],
        name="gate_bwd", compiler_params=_params("arbitrary"))(du16, wout, ya, yb, ym, proj, proj, proj, goa, gob, gom)


def dh_ln_bwd(pieces, win_w, du32, x2, g_emb):
    t, d = x2.shape

    def body(*refs):
        p_refs = refs[:len(pieces)]
        w_ref, du_ref, x_ref, g_ref, dx_ref, dg_ref, db_ref = refs[len(pieces):]

        @pl.when(pl.program_id(0) == 0)
        def _():
            dg_ref[...] = jnp.zeros_like(dg_ref)
            db_ref[...] = jnp.zeros_like(db_ref)

        dh = ALPHA * du_ref[...]
        for p_ref, off, w in zip(p_refs, PIECE_OFFS, PIECE_WIDTHS):
            dh = dh + _dot_nt(p_ref[...], w_ref[:, off:off + w])
        x = x_ref[...]
        xc = x - jnp.mean(x, axis=-1, keepdims=True)
        rstd = lax.rsqrt(jnp.mean(xc * xc, axis=-1, keepdims=True) + NORM_EPS)
        xhat = xc * rstd
        dg_ref[...] += jnp.sum(dh * xhat, axis=0, keepdims=True)
        db_ref[...] += jnp.sum(dh, axis=0, keepdims=True)
        tg = dh * g_ref[...]
        dx_ref[...] = rstd * (tg - jnp.mean(tg, axis=-1, keepdims=True)
                              - xhat * jnp.mean(tg * xhat, axis=-1, keepdims=True))

    row = pl.BlockSpec((ROW_TM, d), lambda i: (i, 0))
    vec = pl.BlockSpec((1, d), lambda i: (0, 0))
    return pl.pallas_call(
        body, grid=(t // ROW_TM,),
        in_specs=[pl.BlockSpec((ROW_TM, w), lambda i: (i, 0)) for w in PIECE_WIDTHS]
        + [pl.BlockSpec(win_w.shape, lambda i: (0, 0)), row, row, vec],
        out_specs=[row, vec, vec],
        out_shape=[SDS((t, d), F32), SDS((1, d), F32), SDS((1, d), F32)],
        name="dh_ln_bwd", compiler_params=_params("arbitrary"))(*pieces, win_w, du32, x2, g_emb)


def _adamw(w, g, m, v):
    m2 = ADAM_B1 * m + (1.0 - ADAM_B1) * g
    v2 = ADAM_B2 * v + (1.0 - ADAM_B2) * (g * g)
    m_hat = m2 / (1.0 - ADAM_B1 ** ADAM_STEP)
    v_hat = v2 / (1.0 - ADAM_B2 ** ADAM_STEP)
    return -ADAM_LR * (m_hat / (jnp.sqrt(v_hat) + ADAM_EPS) + ADAM_WD * w), m2, v2


def adamw_shard(w, parts, m, v, name):
    r, c = w.shape
    tr = min(r, 256)

    def body(w_ref, p_ref, m_ref, v_ref, g_ref, d_ref, nm_ref, nv_ref):
        g = p_ref[0]
        for k in range(1, N_DEV):
            g = g + p_ref[k]
        g_ref[...] = g
        d_ref[...], nm_ref[...], nv_ref[...] = _adamw(w_ref[...], g, m_ref[...], v_ref[...])

    blk = pl.BlockSpec((tr, c), lambda i: (i, 0))
    return pl.pallas_call(
        body, grid=(r // tr,),
        in_specs=[blk, pl.BlockSpec((N_DEV, tr, c), lambda i: (0, i, 0)), blk, blk],
        out_specs=[blk] * 4, out_shape=[SDS((r, c), F32)] * 4, name=name,
        compiler_params=_params("parallel"))(w, parts, m, v)


def _place():
    return lax.axis_index("x"), lax.axis_index("y"), lax.axis_index("c")


def _flat(px, py, pc):
    return 4 * px + 2 * py + pc


def _peer(x, y, c, k):
    return (1 - x if k & 4 else x, 1 - y if k & 2 else y, 1 - c if k & 1 else c)


def cast_shards(shards):
    def body(*refs):
        n = len(refs) // 2
        for i_ref, o_ref in zip(refs[:n], refs[n:]):
            o_ref[...] = i_ref[...].astype(BF16)

    return pl.pallas_call(body, out_shape=[SDS(s.shape, BF16) for s in shards], name="cast_shards",
                          compiler_params=_params())(*shards)


def allgather_weights(shards):
    n = len(shards)

    def body(*refs):
        ins, outs = refs[:n], refs[n:2 * n]
        send_sems, recv_sems, local_sems = refs[2 * n:]
        x, y, c = _place()
        me, sib = (x, y, c), (x, y, 1 - c)
        chips = [(1 - x, y), (x, 1 - y), (1 - x, 1 - y)]

        def copy(a, k, block, to, src=None):
            dst = outs[a].at[_flat(*block)]
            return pltpu.make_async_remote_copy(
                src_ref=dst if src is None else src, dst_ref=dst,
                send_sem=send_sems.at[a * 7 + k], recv_sem=recv_sems.at[a * 7 + k],
                device_id=to, device_id_type=MESH)

        mine = [pltpu.make_async_copy(ins[a], outs[a].at[_flat(*me)], local_sems.at[a]) for a in range(n)]
        for cp in mine:
            cp.start()
        first = []
        for a in range(n):
            first.append(copy(a, 0, me, sib, src=ins[a]))
            first += [copy(a, 1 + j, me, (*chip, c), src=ins[a]) for j, chip in enumerate(chips)]
        for cp in first:
            cp.start()
        passed = []
        for j, chip in enumerate(chips):
            for a in range(n):
                copy(a, 1 + j, (*chip, c), me).wait_recv()
                fwd = copy(a, 4 + j, (*chip, c), sib)
                fwd.start()
                passed.append(fwd)
        for a in range(n):
            copy(a, 0, sib, me).wait_recv()
            for j, chip in enumerate(chips):
                copy(a, 4 + j, (*chip, 1 - c), me).wait_recv()
        for cp in first + passed:
            cp.wait_send()
        for cp in mine:
            cp.wait()

    hbm = pl.BlockSpec(memory_space=pl.ANY)
    return pl.pallas_call(
        body, out_shape=[SDS((N_DEV,) + s.shape, s.dtype) for s in shards],
        in_specs=[hbm] * n, out_specs=[hbm] * n,
        scratch_shapes=[pltpu.SemaphoreType.DMA((7 * n,)), pltpu.SemaphoreType.DMA((7 * n,)), pltpu.SemaphoreType.DMA((n,))],
        name="allgather_weights", compiler_params=_params())(*shards)


def exchange_partials(parts):
    n = len(parts)

    def body(*refs):
        ins, outs = refs[:n], refs[n:2 * n]
        send_sems, recv_sems, local_sems = refs[2 * n:]
        x, y, c = _place()
        me = _flat(x, y, c)
        mine = [pltpu.make_async_copy(ins[a].at[me], outs[a].at[me], local_sems.at[a]) for a in range(n)]
        for cp in mine:
            cp.start()
        copies = []
        for k in range(1, N_DEV):
            peer = _peer(x, y, c, k)
            for a in range(n):
                copies.append(pltpu.make_async_remote_copy(
                    src_ref=ins[a].at[_flat(*peer)], dst_ref=outs[a].at[me],
                    send_sem=send_sems.at[a * 7 + k - 1], recv_sem=recv_sems.at[a * 7 + k - 1],
                    device_id=peer, device_id_type=MESH))
        for cp in copies:
            cp.start()
        for cp in copies:
            cp.wait_recv()
        for cp in copies:
            cp.wait_send()
        for cp in mine:
            cp.wait()

    hbm = pl.BlockSpec(memory_space=pl.ANY)
    return pl.pallas_call(
        body, out_shape=[SDS(p.shape, p.dtype) for p in parts],
        in_specs=[hbm] * n, out_specs=[hbm] * n,
        scratch_shapes=[pltpu.SemaphoreType.DMA((7 * n,)), pltpu.SemaphoreType.DMA((7 * n,)), pltpu.SemaphoreType.DMA((n,))],
        name="exchange_partials", compiler_params=_params())(*parts)


def small_allreduce_adamw(vec, wv, mv, vv):
    shape = vec.shape

    def body(vec_ref, w_ref, m_ref, v_ref, g_ref, d_ref, nm_ref, nv_ref, gath, send_sems, recv_sems):
        x, y, c = _place()
        me = _flat(x, y, c)
        gath[me] = vec_ref[...]
        copies = []
        for k in range(1, N_DEV):
            peer = _peer(x, y, c, k)
            copies.append(pltpu.make_async_remote_copy(
                src_ref=vec_ref, dst_ref=gath.at[me], send_sem=send_sems.at[k - 1], recv_sem=recv_sems.at[k - 1],
                device_id=peer, device_id_type=MESH))
        for cp in copies:
            cp.start()
        for cp in copies:
            cp.wait_recv()
        for cp in copies:
            cp.wait_send()
        g = gath[0]
        for j in range(1, N_DEV):
            g = g + gath[j]
        g_ref[...] = g
        d_ref[...], nm_ref[...], nv_ref[...] = _adamw(w_ref[...], g, m_ref[...], v_ref[...])

    return pl.pallas_call(
        body, out_shape=[SDS(shape, F32)] * 4,
        scratch_shapes=[pltpu.VMEM((N_DEV,) + shape, F32), pltpu.SemaphoreType.DMA((7,)), pltpu.SemaphoreType.DMA((7,))],
        name="small_allreduce_adamw", compiler_params=_params())(vec, wv, mv, vv)


def _rope_tables(positions):
    pos = positions.astype(F32).reshape(-1, 1)

    def cs(r):
        inv_freq = ROPE_THETA ** (-(jnp.arange(0, r, 2, dtype=F32) / r))
        ang = pos * inv_freq
        return jnp.cos(ang), jnp.sin(ang)

    n = pos.shape[0]
    one = lambda w: jnp.ones((n, w), F32)
    zero = lambda w: jnp.zeros((n, w), F32)
    ca, sa = cs(A_ROT)
    rest = A_HEAD_DIM - A_ROT
    a_c = jnp.tile(jnp.concatenate([ca, ca, one(rest)], 1), (1, 2))
    a_sa = jnp.tile(jnp.concatenate([-sa, zero(A_ROT // 2 + rest)], 1), (1, 2))
    a_sb = jnp.tile(jnp.concatenate([zero(A_ROT // 2), sa, zero(rest)], 1), (1, 2))
    cm, sm = cs(MLA_ROPE)
    tail = LANES - MLA_NOPE - MLA_ROPE
    m_c = jnp.concatenate([one(MLA_NOPE), cm, cm, one(tail)], 1)
    m_sa = jnp.concatenate([zero(MLA_NOPE), -sm, zero(MLA_ROPE // 2 + tail)], 1)
    m_sb = jnp.concatenate([zero(MLA_NOPE + MLA_ROPE // 2), sm, zero(tail)], 1)
    return (a_c, a_sa, a_sb), (m_c, m_sa, m_sb)


KR_LO, KR_HI = 4480, 4512


def _w_in_working(g):
    w = g.transpose(1, 0, 2).reshape(D_MODEL, D_IN)
    z = lambda n: jnp.zeros((D_MODEL, n), w.dtype)
    return jnp.concatenate([w[:, :KR_LO], z(MLA_NOPE), w[:, KR_LO:KR_HI], z(LANES - MLA_NOPE - MLA_ROPE), w[:, KR_HI:]], 1)


def _w_in_parts(dw):
    lo = KR_LO + MLA_NOPE
    d = jnp.concatenate([dw[:, :KR_LO], dw[:, lo:lo + MLA_ROPE], dw[:, KR_LO + LANES:]], 1)
    return d.reshape(D_MODEL, N_DEV, D_IN // N_DEV).transpose(1, 0, 2)


def _w_uq_working(g):
    w = jnp.pad(g.transpose(1, 0, 2), ((0, 0), (0, 0), (0, LANES - MLA_NOPE - MLA_ROPE)))
    return w.reshape(MLA_Q_RANK, MLA_QW)


def _w_uq_parts(dw):
    return dw.reshape(MLA_Q_RANK, MLA_HEADS, LANES)[:, :, :MLA_NOPE + MLA_ROPE].transpose(1, 0, 2)


def _w_ukv_working(g):
    wk = jnp.pad(g[:, :, :MLA_NOPE].transpose(1, 0, 2), ((0, 0), (0, 0), (0, LANES - MLA_NOPE)))
    wv = g[:, :, MLA_NOPE:].transpose(1, 0, 2)
    return jnp.concatenate([wk.reshape(MLA_KV_RANK, MLA_QW), wv.reshape(MLA_KV_RANK, MLA_WIDTH)], 1)


def _w_ukv_parts(dw):
    dk = dw[:, :MLA_QW].reshape(MLA_KV_RANK, MLA_HEADS, LANES)[:, :, :MLA_NOPE]
    dv = dw[:, MLA_QW:].reshape(MLA_KV_RANK, MLA_HEADS, MLA_V)
    return jnp.concatenate([dk, dv], -1).transpose(1, 0, 2)


SMALL = (("g_emb", 1024), ("b_emb", 1024), ("g_cq", 256), ("g_ckv", 128), ("g_out_a", 1024), ("g_out_b", 512),
         ("g_out_m", 512), ("g_post", 1024), ("b_post", 1024))
SMALL_ROWS = 56


def _pack_small(first_row, vals):
    rows = [first_row] + [v.reshape(-1, LANES) for v in vals]
    used = sum(r.shape[0] for r in rows)
    return jnp.concatenate(rows + [jnp.zeros((SMALL_ROWS - used, LANES), F32)], 0)


def _unpack_small(packed, shapes):
    out, r = [], 1
    for (_, n), shp in zip(SMALL, shapes):
        out.append(packed[r:r + n // LANES].reshape(shp))
        r += n // LANES
    return out


def kernel(x, mem, positions, g_emb, b_emb, w_in, g_cq, g_ckv, w_uq, w_ukv, w_mem_kv, g_out_a, g_out_b, g_out_m, w_out, g_post, b_post, loss_target, m_g_emb, m_b_emb, m_w_in, m_g_cq, m_g_ckv, m_w_uq, m_w_ukv, m_w_mem_kv, m_g_out_a, m_g_out_b, m_g_out_m, m_w_out, m_g_post, m_b_post, v_g_emb, v_b_emb, v_w_in, v_g_cq, v_g_ckv, v_w_uq, v_w_ukv, v_w_mem_kv, v_g_out_a, v_g_out_b, v_g_out_m, v_w_out, v_g_post, v_b_post):
    nb = x.shape[0]
    t = nb * SEQ
    x2 = x.reshape(t, D_MODEL)
    tgt2 = loss_target.reshape(t, D_MODEL)
    mem2 = mem.reshape(nb * N_MEM, D_MODEL)
    g_emb2, b_emb2 = g_emb.reshape(1, -1), b_emb.reshape(1, -1)
    (a_c, a_sa, a_sb), (m_c, m_sa, m_sb) = _rope_tables(positions)

    big_w = (w_in[0], w_uq[0], w_ukv[0], w_mem_kv[0], w_out[0])
    g_in, g_uq, g_ukv, g_mem, g_out = allgather_weights(cast_shards(big_w))
    win_w = _w_in_working(g_in)
    wuq_w = _w_uq_working(g_uq)
    wkv_w = _w_ukv_working(g_ukv)
    wmem = g_mem.reshape(D_MODEL, 2 * MEM_WIDTH)
    wout = g_out.reshape(D_MIX, D_MODEL)

    h32, h16 = ln_emb_fwd(x2, g_emb2, b_emb2)
    proj = mm_nn(h16, win_w, F32, 512, 1536, "proj")
    ya, lse_a = a_attn_fwd(proj, a_c, a_sa, a_sb, nb)
    qb, kb, vb = mla_prep_fwd(proj, m_c, m_sa, m_sb, g_cq, g_ckv, wuq_w, wkv_w)
    yb, lse_b = mla_attn_fwd(qb, kb, vb, nb)
    mkv = mm_nn(mem2, wmem, BF16, nb * N_MEM, 512, "mem_kv")
    ym = mem_attn_fwd(proj, mkv, nb)
    z = gate_fwd(ya, yb, ym, proj, g_out_a, g_out_b, g_out_m)
    du32, du16, loss_sum, dg_post, db_post = out_ln_loss(z, wout, h32, tgt2, g_post, b_post)

    dya, dyb, dym, dag, dbg, dmg, dg_out_a, dg_out_b, dg_out_m = gate_bwd(
        du16, wout, ya, yb, ym, proj, g_out_a, g_out_b, g_out_m)
    dw_out = mm_tn(z, du16, 512, "dw_out")
    dmq, dmk, dmv = mem_attn_bwd(proj, mkv, dym, nb)
    dw_mem = mm_tn(mem2, jnp.concatenate([dmk, dmv], 1), nb * N_MEM, "dw_mem")
    dqb, dkb, dvb = mla_attn_bwd(qb, kb, vb, dyb, yb, lse_b, nb)
    dcc, dqf, cqn, dkvf, ckvn, dg_cq, dg_ckv = mla_prep_bwd(proj, m_c, m_sa, m_sb, g_cq, g_ckv, wuq_w, wkv_w, dqb, dkb, dvb)
    dw_uq = mm_tn(cqn, dqf, 512, "dw_uq")
    dw_ukv = mm_tn(ckvn, dkvf, 512, "dw_ukv")
    daq, dak, dav = a_attn_bwd(proj, a_c, a_sa, a_sb, dya, ya, lse_a, nb)
    pieces = (daq, dak, dav, dag, dcc, dbg, dmq, dmg)
    grad_x, dg_emb, db_emb = dh_ln_bwd(pieces, win_w, du32, x2, g_emb2)
    dw_in = jnp.concatenate([mm_tn(h16, p, 512, "dw_in_%d" % i) for i, p in enumerate(pieces)], 1)

    parts = exchange_partials((_w_in_parts(dw_in), _w_uq_parts(dw_uq), _w_ukv_parts(dw_ukv),
                               dw_mem.reshape(N_DEV, D_MODEL // N_DEV, 2 * MEM_WIDTH),
                               dw_out.reshape(N_DEV, D_MIX // N_DEV, D_MODEL)))
    big_m = (m_w_in[0], m_w_uq[0], m_w_ukv[0], m_w_mem_kv[0], m_w_out[0])
    big_v = (v_w_in[0], v_w_uq[0], v_w_ukv[0], v_w_mem_kv[0], v_w_out[0])
    big = {}
    for name, w, p, m, v in zip(("w_in", "w_uq", "w_ukv", "w_mem_kv", "w_out"), big_w, parts, big_m, big_v):
        big[name] = [o[None] for o in adamw_shard(w, p, m, v, "adamw_" + name)]

    small_w = (g_emb, b_emb, g_cq, g_ckv, g_out_a, g_out_b, g_out_m, g_post, b_post)
    small_m = (m_g_emb, m_b_emb, m_g_cq, m_g_ckv, m_g_out_a, m_g_out_b, m_g_out_m, m_g_post, m_b_post)
    small_v = (v_g_emb, v_b_emb, v_g_cq, v_g_ckv, v_g_out_a, v_g_out_b, v_g_out_m, v_g_post, v_b_post)
    small_g = (dg_emb, db_emb, dg_cq, dg_ckv, dg_out_a, dg_out_b, dg_out_m, dg_post, db_post)
    zrow = jnp.zeros((1, LANES), F32)
    ones_row = jnp.ones((1, LANES), F32)
    packed = small_allreduce_adamw(_pack_small(loss_sum, small_g), _pack_small(zrow, small_w),
                                   _pack_small(zrow, small_m), _pack_small(ones_row, small_v))
    shapes = [w.shape for w in small_w]
    sg, sd, sm, sv = [_unpack_small(p, shapes) for p in packed]
    loss = packed[0][0, 0]

    order = ("g_emb", "b_emb", "w_in", "g_cq", "g_ckv", "w_uq", "w_ukv", "w_mem_kv", "g_out_a", "g_out_b", "g_out_m",
             "w_out", "g_post", "b_post")
    small_idx = {n: i for i, (n, _) in enumerate(SMALL)}
    outs = [loss, grad_x.reshape(x.shape)]
    for kind in range(4):
        for name in order:
            outs.append(big[name][kind] if name in big else (sg, sd, sm, sv)[kind][small_idx[name]])
    return tuple(outs)
```

```python
import functools

import jax
import jax.numpy as jnp
from jax import lax
from jax.experimental import pallas as pl
from jax.experimental.pallas import tpu as pltpu

F32 = jnp.float32
BF16 = jnp.bfloat16
SDS = jax.ShapeDtypeStruct
MESH = pl.DeviceIdType.MESH

D_MODEL = 1024
SEQ = 2048
A_HEADS, A_HEAD_DIM, A_ROT = 16, 64, 16
A_WIDTH = 1024
DILATIONS = (1, 4, 16)
N_SIDE = 64
MLA_HEADS, MLA_Q_RANK, MLA_KV_RANK = 8, 256, 128
MLA_NOPE, MLA_ROPE, MLA_V = 64, 32, 64
MLA_WIDTH = 512
N_MEM, MEM_HEADS, MEM_HEAD_DIM, MEM_WIDTH = 256, 4, 128, 512
ROPE_THETA = 500000.0
NORM_EPS = 1e-5
NEG_INF = -1e30
ALPHA = 2.0 ** 0.25
D_IN = 6048
N_DEV = 8

ADAM_LR, ADAM_B1, ADAM_B2, ADAM_EPS, ADAM_WD, ADAM_STEP = 0.001, 0.9, 0.999, 1e-08, 0.01, 10

D_INW = 6144
PIECE_WIDTHS = (1024, 1024, 1024, 1024, 512, 512, 512, 512)
PIECE_OFFS = (0, 1024, 2048, 3072, 4096, 4608, 5120, 5632)
LANES = 128
VMEM_LIMIT = 56 * 1024 * 1024


def _params(*sem):
    kw = dict(vmem_limit_bytes=VMEM_LIMIT)
    if sem:
        kw["dimension_semantics"] = sem
    return pltpu.CompilerParams(**kw)


def _dot(a, b):
    return jnp.dot(a, b, preferred_element_type=F32)


def _dot_nt(a, b):
    return lax.dot_general(a, b, (((1,), (1,)), ((), ())), preferred_element_type=F32)


def _dot_tn(a, b):
    return lax.dot_general(a, b, (((0,), (0,)), ((), ())), preferred_element_type=F32)


def _sigmoid(x):
    return 1.0 / (1.0 + jnp.exp(-x))


def _rope_fwd(x, c, sa, sb, half):
    n = x.shape[-1]
    return x * c + pltpu.roll(x, n - half, 1) * sa + pltpu.roll(x, half, 1) * sb


def _rope_bwd(dy, c, sa, sb, half):
    n = dy.shape[-1]
    return dy * c + pltpu.roll(dy * sa, half, 1) + pltpu.roll(dy * sb, n - half, 1)


def mm_nn(a, b, out_dtype, tm, tn, name):
    m, k = a.shape
    n = b.shape[1]

    def body(a_ref, b_ref, o_ref):
        o_ref[...] = _dot(a_ref[...].astype(BF16), b_ref[...].astype(BF16)).astype(o_ref.dtype)

    return pl.pallas_call(
        body, grid=(n // tn, m // tm),
        in_specs=[pl.BlockSpec((tm, k), lambda j, i: (i, 0)), pl.BlockSpec((k, tn), lambda j, i: (0, j))],
        out_specs=pl.BlockSpec((tm, tn), lambda j, i: (i, j)),
        out_shape=SDS((m, n), out_dtype), name=name,
        compiler_params=_params("parallel", "parallel"))(a, b)


def mm_tn(a, b, tt, name):
    t, m = a.shape
    n = b.shape[1]

    def body(a_ref, b_ref, o_ref):
        @pl.when(pl.program_id(0) == 0)
        def _():
            o_ref[...] = jnp.zeros_like(o_ref)

        o_ref[...] += _dot_tn(a_ref[...].astype(BF16), b_ref[...].astype(BF16))

    return pl.pallas_call(
        body, grid=(t // tt,),
        in_specs=[pl.BlockSpec((tt, m), lambda i: (i, 0)), pl.BlockSpec((tt, n), lambda i: (i, 0))],
        out_specs=pl.BlockSpec((m, n), lambda i: (0, 0)),
        out_shape=SDS((m, n), F32), name=name,
        compiler_params=_params("arbitrary"))(a, b)


def ln_emb_fwd(x2, g, b):
    t, d = x2.shape
    tm = 512

    def body(x_ref, g_ref, b_ref, h32_ref, h16_ref):
        x = x_ref[...]
        mu = jnp.mean(x, axis=-1, keepdims=True)
        xc = x - mu
        var = jnp.mean(xc * xc, axis=-1, keepdims=True)
        h = xc * lax.rsqrt(var + NORM_EPS) * g_ref[...] + b_ref[...]
        h32_ref[...] = h
        h16_ref[...] = h.astype(BF16)

    row = pl.BlockSpec((tm, d), lambda i: (i, 0))
    vec = pl.BlockSpec((1, d), lambda i: (0, 0))
    return pl.pallas_call(
        body, grid=(t // tm,), in_specs=[row, vec, vec], out_specs=[row, row],
        out_shape=[SDS((t, d), F32), SDS((t, d), BF16)], name="ln_emb_fwd",
        compiler_params=_params("parallel"))(x2, g, b)


Q_BLK = 128
UNROLL_FWD = 8
UNROLL_BWD = 4


def _pattern_geometry(d):
    length = SEQ // d
    nblk = length // Q_BLK
    kwin = min(2 * Q_BLK, length)
    return length, nblk, kwin


def _block_coords(idx, d):
    length, nblk, kwin = _pattern_geometry(d)
    r = lax.shift_right_logical(idx, nblk.bit_length() - 1)
    i = idx & (nblk - 1)
    q0 = pl.multiple_of(r * length + i * Q_BLK, Q_BLK)
    ks = jnp.clip(i * Q_BLK - N_SIDE, 0, length - kwin)
    k0 = pl.multiple_of(r * length + ks, N_SIDE)
    qpos = i * Q_BLK + lax.broadcasted_iota(jnp.int32, (Q_BLK, kwin), 0)
    kpos = ks + lax.broadcasted_iota(jnp.int32, (Q_BLK, kwin), 1)
    valid = jnp.abs(kpos - qpos) <= N_SIDE
    return q0, k0, kwin, valid


def _deinterleave(src_ref, dst_ref, d, dtype, tmp_ref):
    if d == 1:
        dst_ref[...] = src_ref[...].astype(dtype)
        return
    q = SEQ // 4
    if d == 4:
        for r in range(4):
            dst_ref[r * q:(r + 1) * q, :] = src_ref[pl.ds(r, q, stride=4), :].astype(dtype)
        return
    assert d == 16
    n = SEQ // 16
    for r in range(4):
        tmp_ref[r * q:(r + 1) * q, :] = src_ref[pl.ds(r, q, stride=4), :]
    for r in range(4):
        for j in range(4):
            dst_ref[(r + 4 * j) * n:(r + 4 * j + 1) * n, :] = tmp_ref[pl.ds(r * q + j, n, stride=4), :].astype(dtype)


def _interleave(src_ref, dst_ref, d, tmp_ref, accumulate):
    q = SEQ // 4
    if d == 16:
        n = SEQ // 16
        for r in range(4):
            for j in range(4):
                tmp_ref[pl.ds(r * q + j, n, stride=4), :] = src_ref[(r + 4 * j) * n:(r + 4 * j + 1) * n, :]
        src_ref = tmp_ref
    else:
        assert d == 4
    for r in range(4):
        rows = pl.ds(r, q, stride=4)
        val = src_ref[r * q:(r + 1) * q, :]
        dst_ref[rows, :] = dst_ref[rows, :] + val if accumulate else val


def a_attn_fwd(proj, ca, sa, sb, nb):
    t = proj.shape[0]
    n_pairs = A_WIDTH // LANES

    def body(q_ref, k_ref, v_ref, c_ref, sa_ref, sb_ref, y_ref, lse_ref,
             qr_s, kr_s, qd_s, kd_s, vd_s, oc_s, lc_s, o1_s, l1_s, o2_s, l2_s, o3_s, l3_s, tmp_s):
        c, s_a, s_b = c_ref[...], sa_ref[...], sb_ref[...]
        qr_s[...] = _rope_fwd(q_ref[...], c, s_a, s_b, A_ROT // 2) * (A_HEAD_DIM ** -0.5)
        kr_s[...] = _rope_fwd(k_ref[...], c, s_a, s_b, A_ROT // 2)
        head0 = lax.broadcasted_iota(jnp.int32, (Q_BLK, LANES), 1) < A_HEAD_DIM
        nat = ((o1_s, l1_s), (o2_s, l2_s), (o3_s, l3_s))

        for g, d in enumerate(DILATIONS):
            _deinterleave(qr_s, qd_s, d, BF16, tmp_s)
            _deinterleave(kr_s, kd_s, d, BF16, tmp_s)
            _deinterleave(v_ref, vd_s, d, BF16, tmp_s)
            o_dst, l_dst = (nat[g] if d == 1 else (oc_s, lc_s))

            def block(idx, carry, d=d, o_dst=o_dst, l_dst=l_dst):
                q0, k0, kwin, valid = _block_coords(idx, d)
                qb = qd_s[pl.ds(q0, Q_BLK), :]
                kb = kd_s[pl.ds(k0, kwin), :]
                vb = vd_s[pl.ds(k0, kwin), :]
                zero = jnp.zeros_like(qb)
                q2 = jnp.concatenate([jnp.where(head0, qb, zero), jnp.where(head0, zero, qb)], 0)
                s = jnp.where(jnp.concatenate([valid, valid], 0), _dot_nt(q2, kb), NEG_INF)
                m = jnp.max(s, axis=-1, keepdims=True)
                p = jnp.exp(s - m)
                l = jnp.sum(p, axis=-1, keepdims=True)
                o2 = _dot(p.astype(BF16), vb) / l
                l2 = m + jnp.log(l)
                o_dst[pl.ds(q0, Q_BLK), :] = jnp.where(head0, o2[:Q_BLK], o2[Q_BLK:])
                l_dst[pl.ds(q0, Q_BLK), :] = jnp.where(head0, l2[:Q_BLK], l2[Q_BLK:])
                return carry

            lax.fori_loop(0, SEQ // Q_BLK, block, 0, unroll=UNROLL_FWD)
            if d > 1:
                _interleave(oc_s, nat[g][0], d, tmp_s, False)
                _interleave(lc_s, nat[g][1], d, tmp_s, False)

        def merge(ci, carry):
            rows = pl.ds(pl.multiple_of(ci * 256, 256), 256)
            l1, l2, l3 = l1_s[rows, :], l2_s[rows, :], l3_s[rows, :]
            m = jnp.maximum(jnp.maximum(l1, l2), l3)
            w1, w2, w3 = jnp.exp(l1 - m), jnp.exp(l2 - m), jnp.exp(l3 - m)
            w = w1 + w2 + w3
            y_ref[rows, :] = (w1 * o1_s[rows, :] + w2 * o2_s[rows, :] + w3 * o3_s[rows, :]) / w
            lse_ref[rows, :] = m + jnp.log(w)
            return carry

        lax.fori_loop(0, SEQ // 256, merge, 0)

    def col(off):
        return pl.BlockSpec((SEQ, LANES), lambda b, hp: (b, off + hp))

    tab = pl.BlockSpec((SEQ, LANES), lambda b, hp: (b, 0))
    out = pl.BlockSpec((SEQ, LANES), lambda b, hp: (b, hp))
    f32s = pltpu.VMEM((SEQ, LANES), F32)
    b16s = pltpu.VMEM((SEQ, LANES), BF16)
    return pl.pallas_call(
        body, grid=(nb, n_pairs),
        in_specs=[col(0), col(n_pairs), col(2 * n_pairs), tab, tab, tab],
        out_specs=[out, out],
        out_shape=[SDS((t, A_WIDTH), F32), SDS((t, A_WIDTH), F32)],
        scratch_shapes=[f32s, f32s, b16s, b16s, b16s] + [f32s] * 9,
        name="a_attn_fwd", compiler_params=_params("parallel", "parallel"))(proj, proj, proj, ca, sa, sb)


def a_attn_bwd(proj, ca, sa, sb, dy, y, lse, nb):
    t = proj.shape[0]
    n_pairs = A_WIDTH // LANES

    def body(q_ref, k_ref, v_ref, c_ref, sa_ref, sb_ref, do_ref, y_ref, lse_ref, dq_ref, dk_ref, dv_ref,
             qr_s, kr_s, l0n_s, l1n_s, d0n_s, d1n_s, qd_s, kd_s, vd_s, dod_s, l0d_s, l1d_s, d0d_s, d1d_s,
             dqc_s, dkc_s, dvc_s, dqn_s, dkn_s, dvn_s, tmp_s):
        c, s_a, s_b = c_ref[...], sa_ref[...], sb_ref[...]
        qr_s[...] = _rope_fwd(q_ref[...], c, s_a, s_b, A_ROT // 2) * (A_HEAD_DIM ** -0.5)
        kr_s[...] = _rope_fwd(k_ref[...], c, s_a, s_b, A_ROT // 2)
        head0 = lax.broadcasted_iota(jnp.int32, (Q_BLK, LANES), 1) < A_HEAD_DIM

        def per_head_rows(ci, carry):
            rows = pl.ds(pl.multiple_of(ci * 256, 256), 256)
            h0 = lax.broadcasted_iota(jnp.int32, (256, LANES), 1) < A_HEAD_DIM
            tt = do_ref[rows, :] * y_ref[rows, :]
            d0n_s[rows, :] = jnp.broadcast_to(jnp.sum(jnp.where(h0, tt, 0.0), axis=-1, keepdims=True), (256, LANES))
            d1n_s[rows, :] = jnp.broadcast_to(jnp.sum(jnp.where(h0, 0.0, tt), axis=-1, keepdims=True), (256, LANES))
            l = lse_ref[rows, :]
            lr = pltpu.roll(l, A_HEAD_DIM, 1)
            l0n_s[rows, :] = jnp.where(h0, l, lr)
            l1n_s[rows, :] = jnp.where(h0, lr, l)
            return carry

        lax.fori_loop(0, SEQ // 256, per_head_rows, 0)
        assert DILATIONS[0] == 1

        for d in DILATIONS:
            _deinterleave(qr_s, qd_s, d, BF16, tmp_s)
            _deinterleave(kr_s, kd_s, d, BF16, tmp_s)
            _deinterleave(v_ref, vd_s, d, BF16, tmp_s)
            _deinterleave(do_ref, dod_s, d, BF16, tmp_s)
            if d > 1:
                for src, dst in ((l0n_s, l0d_s), (l1n_s, l1d_s), (d0n_s, d0d_s), (d1n_s, d1d_s)):
                    _deinterleave(src, dst, d, F32, tmp_s)
            l0, l1, d0, d1 = (l0n_s, l1n_s, d0n_s, d1n_s) if d == 1 else (l0d_s, l1d_s, d0d_s, d1d_s)
            dq_dst, dk_dst, dv_dst = (dqn_s, dkn_s, dvn_s) if d == 1 else (dqc_s, dkc_s, dvc_s)
            dk_dst[...] = jnp.zeros_like(dk_dst)
            dv_dst[...] = jnp.zeros_like(dv_dst)

            def block(idx, carry, d=d, l0=l0, l1=l1, d0=d0, d1=d1, dq_dst=dq_dst, dk_dst=dk_dst, dv_dst=dv_dst):
                q0, k0, kwin, valid = _block_coords(idx, d)
                qrows = pl.ds(q0, Q_BLK)
                krows = pl.ds(k0, kwin)
                qb, dob = qd_s[qrows, :], dod_s[qrows, :]
                kb, vb = kd_s[krows, :], vd_s[krows, :]
                zero = jnp.zeros_like(qb)
                q2 = jnp.concatenate([jnp.where(head0, qb, zero), jnp.where(head0, zero, qb)], 0)
                do2 = jnp.concatenate([jnp.where(head0, dob, zero), jnp.where(head0, zero, dob)], 0)
                wide = lambda x: jnp.concatenate([x] * (kwin // LANES), 1)
                lse2 = wide(jnp.concatenate([l0[qrows, :], l1[qrows, :]], 0))
                dd2 = wide(jnp.concatenate([d0[qrows, :], d1[qrows, :]], 0))
                s = jnp.where(jnp.concatenate([valid, valid], 0), _dot_nt(q2, kb), NEG_INF)
                p = jnp.exp(s - lse2)
                ds = (p * (_dot_nt(do2, vb) - dd2)).astype(BF16)
                dq2 = _dot(ds, kb)
                dq_dst[qrows, :] = jnp.where(head0, dq2[:Q_BLK], dq2[Q_BLK:])
                dk_dst[krows, :] += _dot_tn(ds, q2)
                dv_dst[krows, :] += _dot_tn(p.astype(BF16), do2)
                return carry

            lax.fori_loop(0, SEQ // Q_BLK, block, 0, unroll=UNROLL_BWD)
            if d > 1:
                _interleave(dqc_s, dqn_s, d, tmp_s, True)
                _interleave(dkc_s, dkn_s, d, tmp_s, True)
                _interleave(dvc_s, dvn_s, d, tmp_s, True)

        dq_ref[...] = _rope_bwd(dqn_s[...] * (A_HEAD_DIM ** -0.5), c, s_a, s_b, A_ROT // 2).astype(BF16)
        dk_ref[...] = _rope_bwd(dkn_s[...], c, s_a, s_b, A_ROT // 2).astype(BF16)
        dv_ref[...] = dvn_s[...].astype(BF16)

    def col(off):
        return pl.BlockSpec((SEQ, LANES), lambda b, hp: (b, off + hp))

    tab = pl.BlockSpec((SEQ, LANES), lambda b, hp: (b, 0))
    blk = pl.BlockSpec((SEQ, LANES), lambda b, hp: (b, hp))
    f32s = pltpu.VMEM((SEQ, LANES), F32)
    b16s = pltpu.VMEM((SEQ, LANES), BF16)
    return pl.pallas_call(
        body, grid=(nb, n_pairs),
        in_specs=[col(0), col(n_pairs), col(2 * n_pairs), tab, tab, tab, blk, blk, blk],
        out_specs=[blk, blk, blk],
        out_shape=[SDS((t, A_WIDTH), BF16)] * 3,
        scratch_shapes=[f32s] * 6 + [b16s] * 4 + [f32s] * 11,
        name="a_attn_bwd", compiler_params=_params("parallel", "parallel"))(proj, proj, proj, ca, sa, sb, dy, y, lse)


MLA_SCALE = (MLA_NOPE + MLA_ROPE) ** -0.5
MLA_QW = MLA_HEADS * LANES
MLA_KVW = MLA_QW + MLA_WIDTH


def _rms(x, g):
    r = lax.rsqrt(jnp.mean(x * x, axis=-1, keepdims=True) + NORM_EPS)
    return x * r * g, r


def _rms_bwd(dn, x, r, g):
    tg = dn * g
    dx = r * tg - x * (r * r * r) * jnp.mean(tg * x, axis=-1, keepdims=True)
    return dx, jnp.sum(dn * x * r, axis=0, keepdims=True)


def mla_prep_fwd(proj, cm, sma, smb, g_cq, g_ckv, wuq, wkv):
    t = proj.shape[0]
    tm = 512

    def body(cq_ref, ckv_ref, kr_ref, c_ref, sa_ref, sb_ref, gq_ref, gkv_ref, wuq_ref, wkv_ref, q_ref, k_ref, v_ref):
        c, s_a, s_b = c_ref[...], sa_ref[...], sb_ref[...]
        cqn, _ = _rms(cq_ref[...], gq_ref[...])
        qf = _dot(cqn.astype(BF16), wuq_ref[...])
        ckvn, _ = _rms(ckv_ref[...], gkv_ref[...])
        kvf = _dot(ckvn.astype(BF16), wkv_ref[...])
        krope = _rope_fwd(kr_ref[...], c, s_a, s_b, MLA_ROPE // 2)
        for h in range(MLA_HEADS):
            cols = slice(h * LANES, (h + 1) * LANES)
            q_ref[:, cols] = (_rope_fwd(qf[:, cols], c, s_a, s_b, MLA_ROPE // 2) * MLA_SCALE).astype(BF16)
            k_ref[:, cols] = (kvf[:, cols] + krope).astype(BF16)
        v_ref[...] = kvf[:, MLA_QW:].astype(BF16)

    def row(w, j):
        return pl.BlockSpec((tm, w), lambda i: (i, j))

    def full(a):
        return pl.BlockSpec(a.shape, lambda i: (0, 0))

    return pl.pallas_call(
        body, grid=(t // tm,),
        in_specs=[row(256, 4096 // 256), row(128, 4352 // 128), row(128, 4480 // 128), row(128, 0), row(128, 0), row(128, 0),
                  full(g_cq), full(g_ckv), full(wuq), full(wkv)],
        out_specs=[row(MLA_QW, 0), row(MLA_QW, 0), row(MLA_WIDTH, 0)],
        out_shape=[SDS((t, MLA_QW), BF16), SDS((t, MLA_QW), BF16), SDS((t, MLA_WIDTH), BF16)],
        name="mla_prep_fwd", compiler_params=_params("parallel"))(proj, proj, proj, cm, sma, smb, g_cq, g_ckv, wuq, wkv)


def mla_prep_bwd(proj, cm, sma, smb, g_cq, g_ckv, wuq, wkv, dq, dk, dv):
    t = proj.shape[0]
    tm = 512

    def body(cq_ref, ckv_ref, c_ref, sa_ref, sb_ref, gq_ref, gkv_ref, wuq_ref, wkv_ref, dq_ref, dk_ref, dv_ref,
             dcc_ref, dqf_ref, cqn_ref, dkvf_ref, ckvn_ref, dgq_ref, dgkv_ref):
        @pl.when(pl.program_id(0) == 0)
        def _():
            dgq_ref[...] = jnp.zeros_like(dgq_ref)
            dgkv_ref[...] = jnp.zeros_like(dgkv_ref)

        c, s_a, s_b = c_ref[...], sa_ref[...], sb_ref[...]
        cq, ckv = cq_ref[...], ckv_ref[...]
        cqn, rq = _rms(cq, gq_ref[...])
        ckvn, rkv = _rms(ckv, gkv_ref[...])
        cqn_ref[...] = cqn.astype(BF16)
        ckvn_ref[...] = ckvn.astype(BF16)
        lane = lax.broadcasted_iota(jnp.int32, (tm, LANES), 1)
        rope_lanes = (lane >= MLA_NOPE) & (lane < MLA_NOPE + MLA_ROPE)
        dkrope = jnp.zeros((tm, LANES), F32)
        for h in range(MLA_HEADS):
            cols = slice(h * LANES, (h + 1) * LANES)
            dqf_ref[:, cols] = _rope_bwd(dq_ref[:, cols] * MLA_SCALE, c, s_a, s_b, MLA_ROPE // 2).astype(BF16)
            dkh = dk_ref[:, cols]
            dkvf_ref[:, cols] = dkh.astype(BF16)
            dkrope = dkrope + dkh
        dkvf_ref[:, MLA_QW:] = dv_ref[...].astype(BF16)
        dkr = _rope_bwd(jnp.where(rope_lanes, dkrope, 0.0), c, s_a, s_b, MLA_ROPE // 2)
        dcqn = _dot_nt(dqf_ref[...], wuq_ref[...])
        dckvn = _dot_nt(dkvf_ref[...], wkv_ref[...])
        dcq, dgq = _rms_bwd(dcqn, cq, rq, gq_ref[...])
        dckv, dgkv = _rms_bwd(dckvn, ckv, rkv, gkv_ref[...])
        dgq_ref[...] += dgq
        dgkv_ref[...] += dgkv
        dcc_ref[:, 0:256] = dcq.astype(BF16)
        dcc_ref[:, 256:384] = dckv.astype(BF16)
        dcc_ref[:, 384:512] = dkr.astype(BF16)

    def row(w, j):
        return pl.BlockSpec((tm, w), lambda i: (i, j))

    def full(a):
        return pl.BlockSpec(a.shape, lambda i: (0, 0))

    return pl.pallas_call(
        body, grid=(t // tm,),
        in_specs=[row(256, 4096 // 256), row(128, 4352 // 128), row(128, 0), row(128, 0), row(128, 0),
                  full(g_cq), full(g_ckv), full(wuq), full(wkv), row(MLA_QW, 0), row(MLA_QW, 0), row(MLA_WIDTH, 0)],
        out_specs=[row(512, 0), row(MLA_QW, 0), row(256, 0), row(MLA_KVW, 0), row(128, 0), full(g_cq), full(g_ckv)],
        out_shape=[SDS((t, 512), BF16), SDS((t, MLA_QW), BF16), SDS((t, 256), BF16), SDS((t, MLA_KVW), BF16),
                   SDS((t, 128), BF16), SDS(g_cq.shape, F32), SDS(g_ckv.shape, F32)],
        name="mla_prep_bwd", compiler_params=_params("arbitrary"))(proj, proj, cm, sma, smb, g_cq, g_ckv, wuq, wkv, dq, dk, dv)


MLA_TQ = 256


def mla_attn_fwd(qb, kb, vb, nb):
    t = qb.shape[0]
    nq = SEQ // MLA_TQ
    n_pairs = MLA_HEADS // 2

    def body(q_ref, k_ref, v_ref, y_ref, lse_ref):
        head0 = lax.broadcasted_iota(jnp.int32, (MLA_TQ, LANES), 1) < MLA_V
        v = v_ref[...]
        outs, lses = [], []
        for h in range(2):
            cols = slice(h * LANES, (h + 1) * LANES)
            s = _dot_nt(q_ref[:, cols], k_ref[:, cols])
            m = jnp.max(s, axis=-1, keepdims=True)
            p = jnp.exp(s - m)
            l = jnp.sum(p, axis=-1, keepdims=True)
            outs.append(_dot(p.astype(BF16), v) / l)
            lses.append(m + jnp.log(l))
        y_ref[...] = jnp.where(head0, outs[0], outs[1])
        lse_ref[...] = jnp.where(head0, lses[0], lses[1])

    return pl.pallas_call(
        body, grid=(nb, n_pairs, nq),
        in_specs=[pl.BlockSpec((MLA_TQ, 2 * LANES), lambda b, hp, i: (b * nq + i, hp)),
                  pl.BlockSpec((SEQ, 2 * LANES), lambda b, hp, i: (b, hp)),
                  pl.BlockSpec((SEQ, LANES), lambda b, hp, i: (b, hp))],
        out_specs=[pl.BlockSpec((MLA_TQ, LANES), lambda b, hp, i: (b * nq + i, hp))] * 2,
        out_shape=[SDS((t, MLA_WIDTH), F32)] * 2,
        name="mla_attn_fwd", compiler_params=_params("parallel", "parallel", "parallel"))(qb, kb, vb)


def mla_attn_bwd(qb, kb, vb, dy, y, lse, nb):
    t = qb.shape[0]
    nq = SEQ // MLA_TQ
    n_pairs = MLA_HEADS // 2

    def body(q_ref, k_ref, v_ref, do_ref, y_ref, lse_ref, dq_ref, dk_ref, dv_ref):
        @pl.when(pl.program_id(2) == 0)
        def _():
            dk_ref[...] = jnp.zeros_like(dk_ref)
            dv_ref[...] = jnp.zeros_like(dv_ref)

        head0 = lax.broadcasted_iota(jnp.int32, (MLA_TQ, LANES), 1) < MLA_V
        v = v_ref[...]
        do = do_ref[...]
        lse = lse_ref[...]
        tt = do * y_ref[...]
        dv = jnp.zeros((SEQ, LANES), F32)
        for h in range(2):
            sel = head0 if h == 0 else ~head0
            lo = h * MLA_V
            cols = slice(h * LANES, (h + 1) * LANES)
            q = q_ref[:, cols]
            k = k_ref[:, cols]
            dd = jnp.sum(jnp.where(sel, tt, 0.0), axis=-1, keepdims=True)
            doh = jnp.where(sel, do, 0.0).astype(BF16)
            p = jnp.exp(_dot_nt(q, k) - lse[:, lo:lo + 1])
            dp = _dot_nt(doh, v)
            ds = (p * (dp - dd)).astype(BF16)
            dq_ref[:, cols] = _dot(ds, k)
            dk_ref[:, cols] += _dot_tn(ds, q)
            dv = dv + _dot_tn(p.astype(BF16), doh)
        dv_ref[...] += dv

    qspec = pl.BlockSpec((MLA_TQ, 2 * LANES), lambda b, hp, i: (b * nq + i, hp))
    kspec = pl.BlockSpec((SEQ, 2 * LANES), lambda b, hp, i: (b, hp))
    vspec = pl.BlockSpec((SEQ, LANES), lambda b, hp, i: (b, hp))
    ospec = pl.BlockSpec((MLA_TQ, LANES), lambda b, hp, i: (b * nq + i, hp))
    return pl.pallas_call(
        body, grid=(nb, n_pairs, nq),
        in_specs=[qspec, kspec, vspec, ospec, ospec, ospec],
        out_specs=[qspec, kspec, vspec],
        out_shape=[SDS((t, MLA_QW), F32), SDS((t, MLA_QW), F32), SDS((t, MLA_WIDTH), F32)],
        name="mla_attn_bwd", compiler_params=_params("parallel", "parallel", "arbitrary"))(qb, kb, vb, dy, y, lse)


MEM_TQ = 512
MEM_SCALE = MEM_HEAD_DIM ** -0.5
MQ_BLK = 5120 // LANES


def mem_attn_fwd(proj, mkv, nb):
    t = proj.shape[0]
    nq = SEQ // MEM_TQ

    def body(q_ref, mk_ref, mv_ref, y_ref):
        s = _dot_nt(q_ref[...].astype(BF16), mk_ref[...]) * MEM_SCALE
        m = jnp.max(s, axis=-1, keepdims=True)
        p = jnp.exp(s - m)
        l = jnp.sum(p, axis=-1, keepdims=True)
        y_ref[...] = _dot(p.astype(BF16), mv_ref[...]) / l

    return pl.pallas_call(
        body, grid=(nb, MEM_HEADS, nq),
        in_specs=[pl.BlockSpec((MEM_TQ, LANES), lambda b, h, i: (b * nq + i, MQ_BLK + h)),
                  pl.BlockSpec((N_MEM, LANES), lambda b, h, i: (b, h)),
                  pl.BlockSpec((N_MEM, LANES), lambda b, h, i: (b, MEM_HEADS + h))],
        out_specs=pl.BlockSpec((MEM_TQ, LANES), lambda b, h, i: (b * nq + i, h)),
        out_shape=SDS((t, MEM_WIDTH), F32),
        name="mem_attn_fwd", compiler_params=_params("parallel", "parallel", "parallel"))(proj, mkv, mkv)


def mem_attn_bwd(proj, mkv, dy, nb):
    t = proj.shape[0]
    nq = SEQ // MEM_TQ

    def body(q_ref, mk_ref, mv_ref, do_ref, dq_ref, dmk_ref, dmv_ref):
        @pl.when(pl.program_id(2) == 0)
        def _():
            dmk_ref[...] = jnp.zeros_like(dmk_ref)
            dmv_ref[...] = jnp.zeros_like(dmv_ref)

        q = q_ref[...].astype(BF16)
        mk, mv = mk_ref[...], mv_ref[...]
        do = do_ref[...].astype(BF16)
        s = _dot_nt(q, mk) * MEM_SCALE
        e = jnp.exp(s - jnp.max(s, axis=-1, keepdims=True))
        p = e / jnp.sum(e, axis=-1, keepdims=True)
        dp = _dot_nt(do, mv)
        ds = (p * (dp - jnp.sum(p * dp, axis=-1, keepdims=True)) * MEM_SCALE).astype(BF16)
        dq_ref[...] = _dot(ds, mk).astype(BF16)
        dmk_ref[...] += _dot_tn(ds, q)
        dmv_ref[...] += _dot_tn(p.astype(BF16), do)

    ospec = pl.BlockSpec((MEM_TQ, LANES), lambda b, h, i: (b * nq + i, h))
    kspec = pl.BlockSpec((N_MEM, LANES), lambda b, h, i: (b, h))
    return pl.pallas_call(
        body, grid=(nb, MEM_HEADS, nq),
        in_specs=[pl.BlockSpec((MEM_TQ, LANES), lambda b, h, i: (b * nq + i, MQ_BLK + h)),
                  kspec, pl.BlockSpec((N_MEM, LANES), lambda b, h, i: (b, MEM_HEADS + h)), ospec],
        out_specs=[ospec, kspec, kspec],
        out_shape=[SDS((t, MEM_WIDTH), BF16), SDS((nb * N_MEM, MEM_WIDTH), F32), SDS((nb * N_MEM, MEM_WIDTH), F32)],
        name="mem_attn_bwd", compiler_params=_params("parallel", "parallel", "arbitrary"))(proj, mkv, mkv, dy)


ROW_TM = 256
AG_BLK = 3072 // 1024
BG_BLK = 4608 // 512
MG_BLK = 5632 // 512
GROUPS = ((0, A_WIDTH), (A_WIDTH, MLA_WIDTH), (A_WIDTH + MLA_WIDTH, MEM_WIDTH))
D_MIX = 2048


def _gate_specs():
    def row(w, j):
        return pl.BlockSpec((ROW_TM, w), lambda i: (i, j))

    def vec(w):
        return pl.BlockSpec((1, w), lambda i: (0, 0))

    ys = [row(A_WIDTH, 0), row(MLA_WIDTH, 0), row(MEM_WIDTH, 0)]
    gates = [row(A_WIDTH, AG_BLK), row(MLA_WIDTH, BG_BLK), row(MEM_WIDTH, MG_BLK)]
    gains = [vec(A_WIDTH), vec(MLA_WIDTH), vec(MEM_WIDTH)]
    return row, vec, ys, gates, gains


def gate_fwd(ya, yb, ym, proj, goa, gob, gom):
    t = ya.shape[0]
    row, vec, ys, gates, gains = _gate_specs()

    def body(ya_ref, yb_ref, ym_ref, ga_ref, gb_ref, gm_ref, goa_ref, gob_ref, gom_ref, z_ref):
        for (off, w), y_ref, g_ref, go_ref in zip(GROUPS, (ya_ref, yb_ref, ym_ref), (ga_ref, gb_ref, gm_ref),
                                                  (goa_ref, gob_ref, gom_ref)):
            n, _ = _rms(y_ref[...], go_ref[...])
            gt = g_ref[...]
            z_ref[:, off:off + w] = (n * (gt * _sigmoid(gt))).astype(BF16)

    return pl.pallas_call(
        body, grid=(t // ROW_TM,), in_specs=ys + gates + gains, out_specs=row(D_MIX, 0),
        out_shape=SDS((t, D_MIX), BF16), name="gate_fwd",
        compiler_params=_params("parallel"))(ya, yb, ym, proj, proj, proj, goa, gob, gom)


def out_ln_loss(z, wout, h32, target, gp, bp):
    t, d = h32.shape

    def body(z_ref, w_ref, h_ref, t_ref, gp_ref, bp_ref, du32_ref, du16_ref, loss_ref, dgp_ref, dbp_ref):
        @pl.when(pl.program_id(0) == 0)
        def _():
            loss_ref[...] = jnp.zeros_like(loss_ref)
            dgp_ref[...] = jnp.zeros_like(dgp_ref)
            dbp_ref[...] = jnp.zeros_like(dbp_ref)

        g = gp_ref[...]
        u = ALPHA * h_ref[...] + _dot(z_ref[...], w_ref[...])
        mu = jnp.mean(u, axis=-1, keepdims=True)
        uc = u - mu
        rstd = lax.rsqrt(jnp.mean(uc * uc, axis=-1, keepdims=True) + NORM_EPS)
        xhat = uc * rstd
        err = xhat * g + bp_ref[...] - t_ref[...]
        tok = jnp.sum(err * err, axis=-1, keepdims=True) * (1.0 / d)
        loss_ref[...] += 0.5 * jnp.sum(tok, axis=0, keepdims=True)
        dout = err * (1.0 / d)
        dxhat = dout * g
        du = rstd * (dxhat - jnp.mean(dxhat, axis=-1, keepdims=True)
                     - xhat * jnp.mean(dxhat * xhat, axis=-1, keepdims=True))
        du32_ref[...] = du
        du16_ref[...] = du.astype(BF16)
        dgp_ref[...] += jnp.sum(dout * xhat, axis=0, keepdims=True)
        dbp_ref[...] += jnp.sum(dout, axis=0, keepdims=True)

    row = pl.BlockSpec((ROW_TM, d), lambda i: (i, 0))
    vec = pl.BlockSpec((1, d), lambda i: (0, 0))
    return pl.pallas_call(
        body, grid=(t // ROW_TM,),
        in_specs=[pl.BlockSpec((ROW_TM, D_MIX), lambda i: (i, 0)), pl.BlockSpec((D_MIX, d), lambda i: (0, 0)), row, row, vec, vec],
        out_specs=[row, row, pl.BlockSpec((1, LANES), lambda i: (0, 0)), vec, vec],
        out_shape=[SDS((t, d), F32), SDS((t, d), BF16), SDS((1, LANES), F32), SDS((1, d), F32), SDS((1, d), F32)],
        name="out_ln_loss", compiler_params=_params("arbitrary"))(z, wout, h32, target, gp, bp)


def gate_bwd(du16, wout, ya, yb, ym, proj, goa, gob, gom):
    t = ya.shape[0]
    row, vec, ys, gates, gains = _gate_specs()

    def body(du_ref, w_ref, ya_ref, yb_ref, ym_ref, ga_ref, gb_ref, gm_ref, goa_ref, gob_ref, gom_ref,
             dya_ref, dyb_ref, dym_ref, dga_ref, dgb_ref, dgm_ref, dgoa_ref, dgob_ref, dgom_ref):
        @pl.when(pl.program_id(0) == 0)
        def _():
            dgoa_ref[...] = jnp.zeros_like(dgoa_ref)
            dgob_ref[...] = jnp.zeros_like(dgob_ref)
            dgom_ref[...] = jnp.zeros_like(dgom_ref)

        dz = _dot_nt(du_ref[...], w_ref[...])
        for (off, w), y_ref, g_ref, go_ref, dy_ref, dg_ref, dgo_ref in zip(
                GROUPS, (ya_ref, yb_ref, ym_ref), (ga_ref, gb_ref, gm_ref), (goa_ref, gob_ref, gom_ref),
                (dya_ref, dyb_ref, dym_ref), (dga_ref, dgb_ref, dgm_ref), (dgoa_ref, dgob_ref, dgom_ref)):
            dzg = dz[:, off:off + w]
            y, gt, go = y_ref[...], g_ref[...], go_ref[...]
            n, r = _rms(y, go)
            sg = _sigmoid(gt)
            dg_ref[...] = (dzg * n * (sg * (1.0 + gt * (1.0 - sg)))).astype(BF16)
            dy, dgo = _rms_bwd(dzg * (gt * sg), y, r, go)
            dy_ref[...] = dy
            dgo_ref[...] += dgo

    widths = (A_WIDTH, MLA_WIDTH, MEM_WIDTH)
    return pl.pallas_call(
        body, grid=(t // ROW_TM,),
        in_specs=[row(D_MODEL, 0), pl.BlockSpec((D_MIX, D_MODEL), lambda i: (0, 0))] + ys + gates + gains,
        out_specs=[row(w, 0) for w in widths] * 2 + [vec(w) for w in widths],
        out_shape=[SDS((t, w), F32) for w in widths] + [SDS((t, w), BF16) for w in widths] + [SDS((1, w), F32) for w in widths],
        name="gate_bwd", compiler_params=_params("arbitrary"))(du16, wout, ya, yb, ym, proj, proj, proj, goa, gob, gom)


def dh_ln_bwd(pieces, win_w, du32, x2, g_emb):
    t, d = x2.shape

    def body(*refs):
        p_refs = refs[:len(pieces)]
        w_ref, du_ref, x_ref, g_ref, dx_ref, dg_ref, db_ref = refs[len(pieces):]

        @pl.when(pl.program_id(0) == 0)
        def _():
            dg_ref[...] = jnp.zeros_like(dg_ref)
            db_ref[...] = jnp.zeros_like(db_ref)

        dh = ALPHA * du_ref[...]
        for p_ref, off, w in zip(p_refs, PIECE_OFFS, PIECE_WIDTHS):
            dh = dh + _dot_nt(p_ref[...], w_ref[:, off:off + w])
        x = x_ref[...]
        xc = x - jnp.mean(x, axis=-1, keepdims=True)
        rstd = lax.rsqrt(jnp.mean(xc * xc, axis=-1, keepdims=True) + NORM_EPS)
        xhat = xc * rstd
        dg_ref[...] += jnp.sum(dh * xhat, axis=0, keepdims=True)
        db_ref[...] += jnp.sum(dh, axis=0, keepdims=True)
        tg = dh * g_ref[...]
        dx_ref[...] = rstd * (tg - jnp.mean(tg, axis=-1, keepdims=True)
                              - xhat * jnp.mean(tg * xhat, axis=-1, keepdims=True))

    row = pl.BlockSpec((ROW_TM, d), lambda i: (i, 0))
    vec = pl.BlockSpec((1, d), lambda i: (0, 0))
    return pl.pallas_call(
        body, grid=(t // ROW_TM,),
        in_specs=[pl.BlockSpec((ROW_TM, w), lambda i: (i, 0)) for w in PIECE_WIDTHS]
        + [pl.BlockSpec(win_w.shape, lambda i: (0, 0)), row, row, vec],
        out_specs=[row, vec, vec],
        out_shape=[SDS((t, d), F32), SDS((1, d), F32), SDS((1, d), F32)],
        name="dh_ln_bwd", compiler_params=_params("arbitrary"))(*pieces, win_w, du32, x2, g_emb)


def _adamw(w, g, m, v):
    m2 = ADAM_B1 * m + (1.0 - ADAM_B1) * g
    v2 = ADAM_B2 * v + (1.0 - ADAM_B2) * (g * g)
    m_hat = m2 / (1.0 - ADAM_B1 ** ADAM_STEP)
    v_hat = v2 / (1.0 - ADAM_B2 ** ADAM_STEP)
    return -ADAM_LR * (m_hat / (jnp.sqrt(v_hat) + ADAM_EPS) + ADAM_WD * w), m2, v2


def adamw_shard(w, parts, m, v, name):
    r, c = w.shape
    tr = min(r, 256)

    def body(w_ref, p_ref, m_ref, v_ref, g_ref, d_ref, nm_ref, nv_ref):
        g = p_ref[0]
        for k in range(1, N_DEV):
            g = g + p_ref[k]
        g_ref[...] = g
        d_ref[...], nm_ref[...], nv_ref[...] = _adamw(w_ref[...], g, m_ref[...], v_ref[...])

    blk = pl.BlockSpec((tr, c), lambda i: (i, 0))
    return pl.pallas_call(
        body, grid=(r // tr,),
        in_specs=[blk, pl.BlockSpec((N_DEV, tr, c), lambda i: (0, i, 0)), blk, blk],
        out_specs=[blk] * 4, out_shape=[SDS((r, c), F32)] * 4, name=name,
        compiler_params=_params("parallel"))(w, parts, m, v)


def _place():
    return lax.axis_index("x"), lax.axis_index("y"), lax.axis_index("c")


def _flat(px, py, pc):
    return 4 * px + 2 * py + pc


def _peer(x, y, c, k):
    return (1 - x if k & 4 else x, 1 - y if k & 2 else y, 1 - c if k & 1 else c)


def cast_shards(shards):
    def body(*refs):
        n = len(refs) // 2
        for i_ref, o_ref in zip(refs[:n], refs[n:]):
            o_ref[...] = i_ref[...].astype(BF16)

    return pl.pallas_call(body, out_shape=[SDS(s.shape, BF16) for s in shards], name="cast_shards",
                          compiler_params=_params())(*shards)


def allgather_weights(shards):
    n = len(shards)

    def body(*refs):
        ins, outs = refs[:n], refs[n:2 * n]
        send_sems, recv_sems, local_sems = refs[2 * n:]
        x, y, c = _place()
        me, sib = (x, y, c), (x, y, 1 - c)
        chips = [(1 - x, y), (x, 1 - y), (1 - x, 1 - y)]

        def copy(a, k, block, to, src=None):
            dst = outs[a].at[_flat(*block)]
            return pltpu.make_async_remote_copy(
                src_ref=dst if src is None else src, dst_ref=dst,
                send_sem=send_sems.at[a * 7 + k], recv_sem=recv_sems.at[a * 7 + k],
                device_id=to, device_id_type=MESH)

        mine = [pltpu.make_async_copy(ins[a], outs[a].at[_flat(*me)], local_sems.at[a]) for a in range(n)]
        for cp in mine:
            cp.start()
        first = []
        for a in range(n):
            first.append(copy(a, 0, me, sib, src=ins[a]))
            first += [copy(a, 1 + j, me, (*chip, c), src=ins[a]) for j, chip in enumerate(chips)]
        for cp in first:
            cp.start()
        passed = []
        for j, chip in enumerate(chips):
            for a in range(n):
                copy(a, 1 + j, (*chip, c), me).wait_recv()
                fwd = copy(a, 4 + j, (*chip, c), sib)
                fwd.start()
                passed.append(fwd)
        for a in range(n):
            copy(a, 0, sib, me).wait_recv()
            for j, chip in enumerate(chips):
                copy(a, 4 + j, (*chip, 1 - c), me).wait_recv()
        for cp in first + passed:
            cp.wait_send()
        for cp in mine:
            cp.wait()

    hbm = pl.BlockSpec(memory_space=pl.ANY)
    return pl.pallas_call(
        body, out_shape=[SDS((N_DEV,) + s.shape, s.dtype) for s in shards],
        in_specs=[hbm] * n, out_specs=[hbm] * n,
        scratch_shapes=[pltpu.SemaphoreType.DMA((7 * n,)), pltpu.SemaphoreType.DMA((7 * n,)), pltpu.SemaphoreType.DMA((n,))],
        name="allgather_weights", compiler_params=_params())(*shards)


def exchange_partials(parts):
    n = len(parts)

    def body(*refs):
        ins, outs = refs[:n], refs[n:2 * n]
        send_sems, recv_sems, local_sems = refs[2 * n:]
        x, y, c = _place()
        me = _flat(x, y, c)
        mine = [pltpu.make_async_copy(ins[a].at[me], outs[a].at[me], local_sems.at[a]) for a in range(n)]
        for cp in mine:
            cp.start()
        copies = []
        for k in range(1, N_DEV):
            peer = _peer(x, y, c, k)
            for a in range(n):
                copies.append(pltpu.make_async_remote_copy(
                    src_ref=ins[a].at[_flat(*peer)], dst_ref=outs[a].at[me],
                    send_sem=send_sems.at[a * 7 + k - 1], recv_sem=recv_sems.at[a * 7 + k - 1],
                    device_id=peer, device_id_type=MESH))
        for cp in copies:
            cp.start()
        for cp in copies:
            cp.wait_recv()
        for cp in copies:
            cp.wait_send()
        for cp in mine:
            cp.wait()

    hbm = pl.BlockSpec(memory_space=pl.ANY)
    return pl.pallas_call(
        body, out_shape=[SDS(p.shape, p.dtype) for p in parts],
        in_specs=[hbm] * n, out_specs=[hbm] * n,
        scratch_shapes=[pltpu.SemaphoreType.DMA((7 * n,)), pltpu.SemaphoreType.DMA((7 * n,)), pltpu.SemaphoreType.DMA((n,))],
        name="exchange_partials", compiler_params=_params())(*parts)


SLOT_ROWS = 8


def small_allreduce_adamw(loss_sum, grads, ws, ms, vs):
    n = len(grads)
    rows = [g.shape[0] for g in grads]
    total = SLOT_ROWS * (n + 1)

    def body(*refs):
        loss_ref, g_refs, w_refs = refs[0], refs[1:1 + n], refs[1 + n:1 + 2 * n]
        m_refs, v_refs = refs[1 + 2 * n:1 + 3 * n], refs[1 + 3 * n:1 + 4 * n]
        outs = refs[1 + 4 * n:2 + 8 * n]
        vec, gath, tot, send_sems, recv_sems = refs[2 + 8 * n:]
        x, y, c = _place()
        me = _flat(x, y, c)
        vec[...] = jnp.zeros_like(vec)
        vec[0:1, :] = loss_ref[...]
        for i in range(n):
            vec[SLOT_ROWS * (i + 1):SLOT_ROWS * (i + 1) + rows[i], :] = g_refs[i][...]
        gath[me] = vec[...]
        copies = []
        for k in range(1, N_DEV):
            peer = _peer(x, y, c, k)
            copies.append(pltpu.make_async_remote_copy(
                src_ref=vec, dst_ref=gath.at[me], send_sem=send_sems.at[k - 1], recv_sem=recv_sems.at[k - 1],
                device_id=peer, device_id_type=MESH))
        for cp in copies:
            cp.start()
        for cp in copies:
            cp.wait_recv()
        for cp in copies:
            cp.wait_send()
        g = gath[0]
        for j in range(1, N_DEV):
            g = g + gath[j]
        tot[...] = g
        outs[0][...] = tot[0:1, :]
        for i in range(n):
            gi = tot[SLOT_ROWS * (i + 1):SLOT_ROWS * (i + 1) + rows[i], :]
            outs[1 + i][...] = gi
            outs[1 + n + i][...], outs[1 + 2 * n + i][...], outs[1 + 3 * n + i][...] = _adamw(
                w_refs[i][...], gi, m_refs[i][...], v_refs[i][...])

    shapes = [SDS(g.shape, F32) for g in grads]
    return pl.pallas_call(
        body, out_shape=[SDS((1, LANES), F32)] + shapes * 4,
        scratch_shapes=[pltpu.VMEM((total, LANES), F32), pltpu.VMEM((N_DEV, total, LANES), F32), pltpu.VMEM((total, LANES), F32),
                        pltpu.SemaphoreType.DMA((7,)), pltpu.SemaphoreType.DMA((7,))],
        name="small_allreduce_adamw", compiler_params=_params())(loss_sum, *grads, *ws, *ms, *vs)


def _rope_tables(positions):
    pos = positions.astype(F32).reshape(-1, 1)

    def cs(r):
        inv_freq = ROPE_THETA ** (-(jnp.arange(0, r, 2, dtype=F32) / r))
        ang = pos * inv_freq
        return jnp.cos(ang), jnp.sin(ang)

    n = pos.shape[0]
    one = lambda w: jnp.ones((n, w), F32)
    zero = lambda w: jnp.zeros((n, w), F32)
    ca, sa = cs(A_ROT)
    rest = A_HEAD_DIM - A_ROT
    a_c = jnp.tile(jnp.concatenate([ca, ca, one(rest)], 1), (1, 2))
    a_sa = jnp.tile(jnp.concatenate([-sa, zero(A_ROT // 2 + rest)], 1), (1, 2))
    a_sb = jnp.tile(jnp.concatenate([zero(A_ROT // 2), sa, zero(rest)], 1), (1, 2))
    cm, sm = cs(MLA_ROPE)
    tail = LANES - MLA_NOPE - MLA_ROPE
    m_c = jnp.concatenate([one(MLA_NOPE), cm, cm, one(tail)], 1)
    m_sa = jnp.concatenate([zero(MLA_NOPE), -sm, zero(MLA_ROPE // 2 + tail)], 1)
    m_sb = jnp.concatenate([zero(MLA_NOPE + MLA_ROPE // 2), sm, zero(tail)], 1)
    return (a_c, a_sa, a_sb), (m_c, m_sa, m_sb)


KR_LO, KR_HI = 4480, 4512


def _w_in_working(g):
    w = g.transpose(1, 0, 2).reshape(D_MODEL, D_IN)
    z = lambda n: jnp.zeros((D_MODEL, n), w.dtype)
    return jnp.concatenate([w[:, :KR_LO], z(MLA_NOPE), w[:, KR_LO:KR_HI], z(LANES - MLA_NOPE - MLA_ROPE), w[:, KR_HI:]], 1)


def _w_in_parts(dw):
    lo = KR_LO + MLA_NOPE
    d = jnp.concatenate([dw[:, :KR_LO], dw[:, lo:lo + MLA_ROPE], dw[:, KR_LO + LANES:]], 1)
    return d.reshape(D_MODEL, N_DEV, D_IN // N_DEV).transpose(1, 0, 2)


def _w_uq_working(g):
    w = jnp.pad(g.transpose(1, 0, 2), ((0, 0), (0, 0), (0, LANES - MLA_NOPE - MLA_ROPE)))
    return w.reshape(MLA_Q_RANK, MLA_QW)


def _w_uq_parts(dw):
    return dw.reshape(MLA_Q_RANK, MLA_HEADS, LANES)[:, :, :MLA_NOPE + MLA_ROPE].transpose(1, 0, 2)


def _w_ukv_working(g):
    wk = jnp.pad(g[:, :, :MLA_NOPE].transpose(1, 0, 2), ((0, 0), (0, 0), (0, LANES - MLA_NOPE)))
    wv = g[:, :, MLA_NOPE:].transpose(1, 0, 2)
    return jnp.concatenate([wk.reshape(MLA_KV_RANK, MLA_QW), wv.reshape(MLA_KV_RANK, MLA_WIDTH)], 1)


def _w_ukv_parts(dw):
    dk = dw[:, :MLA_QW].reshape(MLA_KV_RANK, MLA_HEADS, LANES)[:, :, :MLA_NOPE]
    dv = dw[:, MLA_QW:].reshape(MLA_KV_RANK, MLA_HEADS, MLA_V)
    return jnp.concatenate([dk, dv], -1).transpose(1, 0, 2)


SMALL_NAMES = ("g_emb", "b_emb", "g_cq", "g_ckv", "g_out_a", "g_out_b", "g_out_m", "g_post", "b_post")


def kernel(x, mem, positions, g_emb, b_emb, w_in, g_cq, g_ckv, w_uq, w_ukv, w_mem_kv, g_out_a, g_out_b, g_out_m, w_out, g_post, b_post, loss_target, m_g_emb, m_b_emb, m_w_in, m_g_cq, m_g_ckv, m_w_uq, m_w_ukv, m_w_mem_kv, m_g_out_a, m_g_out_b, m_g_out_m, m_w_out, m_g_post, m_b_post, v_g_emb, v_b_emb, v_w_in, v_g_cq, v_g_ckv, v_w_uq, v_w_ukv, v_w_mem_kv, v_g_out_a, v_g_out_b, v_g_out_m, v_w_out, v_g_post, v_b_post):
    nb = x.shape[0]
    t = nb * SEQ
    x2 = x.reshape(t, D_MODEL)
    tgt2 = loss_target.reshape(t, D_MODEL)
    mem2 = mem.reshape(nb * N_MEM, D_MODEL)
    g_emb2, b_emb2 = g_emb.reshape(1, -1), b_emb.reshape(1, -1)
    (a_c, a_sa, a_sb), (m_c, m_sa, m_sb) = _rope_tables(positions)

    big_w = (w_in[0], w_uq[0], w_ukv[0], w_mem_kv[0], w_out[0])
    g_in, g_uq, g_ukv, g_mem, g_out = allgather_weights(cast_shards(big_w))
    win_w = _w_in_working(g_in)
    wuq_w = _w_uq_working(g_uq)
    wkv_w = _w_ukv_working(g_ukv)
    wmem = g_mem.reshape(D_MODEL, 2 * MEM_WIDTH)
    wout = g_out.reshape(D_MIX, D_MODEL)

    h32, h16 = ln_emb_fwd(x2, g_emb2, b_emb2)
    proj = mm_nn(h16, win_w, F32, 512, 1536, "proj")
    ya, lse_a = a_attn_fwd(proj, a_c, a_sa, a_sb, nb)
    qb, kb, vb = mla_prep_fwd(proj, m_c, m_sa, m_sb, g_cq, g_ckv, wuq_w, wkv_w)
    yb, lse_b = mla_attn_fwd(qb, kb, vb, nb)
    mkv = mm_nn(mem2, wmem, BF16, nb * N_MEM, 512, "mem_kv")
    ym = mem_attn_fwd(proj, mkv, nb)
    z = gate_fwd(ya, yb, ym, proj, g_out_a, g_out_b, g_out_m)
    du32, du16, loss_sum, dg_post, db_post = out_ln_loss(z, wout, h32, tgt2, g_post, b_post)

    dya, dyb, dym, dag, dbg, dmg, dg_out_a, dg_out_b, dg_out_m = gate_bwd(
        du16, wout, ya, yb, ym, proj, g_out_a, g_out_b, g_out_m)
    dw_out = mm_tn(z, du16, 512, "dw_out")
    dmq, dmk, dmv = mem_attn_bwd(proj, mkv, dym, nb)
    dw_mem = mm_tn(mem2, jnp.concatenate([dmk, dmv], 1), nb * N_MEM, "dw_mem")
    dqb, dkb, dvb = mla_attn_bwd(qb, kb, vb, dyb, yb, lse_b, nb)
    dcc, dqf, cqn, dkvf, ckvn, dg_cq, dg_ckv = mla_prep_bwd(proj, m_c, m_sa, m_sb, g_cq, g_ckv, wuq_w, wkv_w, dqb, dkb, dvb)
    dw_uq = mm_tn(cqn, dqf, 512, "dw_uq")
    dw_ukv = mm_tn(ckvn, dkvf, 512, "dw_ukv")
    daq, dak, dav = a_attn_bwd(proj, a_c, a_sa, a_sb, dya, ya, lse_a, nb)
    pieces = (daq, dak, dav, dag, dcc, dbg, dmq, dmg)
    grad_x, dg_emb, db_emb = dh_ln_bwd(pieces, win_w, du32, x2, g_emb2)
    dw_in = jnp.concatenate([mm_tn(h16, p, 512, "dw_in_%d" % i) for i, p in enumerate(pieces)], 1)

    parts = exchange_partials((_w_in_parts(dw_in), _w_uq_parts(dw_uq), _w_ukv_parts(dw_ukv),
                               dw_mem.reshape(N_DEV, D_MODEL // N_DEV, 2 * MEM_WIDTH),
                               dw_out.reshape(N_DEV, D_MIX // N_DEV, D_MODEL)))
    big_m = (m_w_in[0], m_w_uq[0], m_w_ukv[0], m_w_mem_kv[0], m_w_out[0])
    big_v = (v_w_in[0], v_w_uq[0], v_w_ukv[0], v_w_mem_kv[0], v_w_out[0])
    big = {}
    for name, w, p, m, v in zip(("w_in", "w_uq", "w_ukv", "w_mem_kv", "w_out"), big_w, parts, big_m, big_v):
        big[name] = [o[None] for o in adamw_shard(w, p, m, v, "adamw_" + name)]

    small_w = (g_emb, b_emb, g_cq, g_ckv, g_out_a, g_out_b, g_out_m, g_post, b_post)
    small_m = (m_g_emb, m_b_emb, m_g_cq, m_g_ckv, m_g_out_a, m_g_out_b, m_g_out_m, m_g_post, m_b_post)
    small_v = (v_g_emb, v_b_emb, v_g_cq, v_g_ckv, v_g_out_a, v_g_out_b, v_g_out_m, v_g_post, v_b_post)
    small_g = (dg_emb, db_emb, dg_cq, dg_ckv, dg_out_a, dg_out_b, dg_out_m, dg_post, db_post)
    rows128 = lambda vals: [v.reshape(-1, LANES) for v in vals]
    res = small_allreduce_adamw(loss_sum, rows128(small_g), rows128(small_w), rows128(small_m), rows128(small_v))
    loss = res[0][0, 0]
    n_small = len(small_w)
    sg, sd, sm, sv = [[r.reshape(w.shape) for r, w in zip(res[1 + k * n_small:1 + (k + 1) * n_small], small_w)]
                      for k in range(4)]

    order = ("g_emb", "b_emb", "w_in", "g_cq", "g_ckv", "w_uq", "w_ukv", "w_mem_kv", "g_out_a", "g_out_b", "g_out_m",
             "w_out", "g_post", "b_post")
    small_idx = {n: i for i, n in enumerate(SMALL_NAMES)}
    outs = [loss, grad_x.reshape(x.shape)]
    for kind in range(4):
        for name in order:
            outs.append(big[name][kind] if name in big else (sg, sd, sm, sv)[kind][small_idx[name]])
    return tuple(outs)
```

```python
import functools

import jax
import jax.numpy as jnp
from jax import lax
from jax.experimental import pallas as pl
from jax.experimental.pallas import tpu as pltpu

F32 = jnp.float32
BF16 = jnp.bfloat16
SDS = jax.ShapeDtypeStruct
MESH = pl.DeviceIdType.MESH

D_MODEL = 1024
SEQ = 2048
A_HEADS, A_HEAD_DIM, A_ROT = 16, 64, 16
A_WIDTH = 1024
DILATIONS = (1, 4, 16)
N_SIDE = 64
MLA_HEADS, MLA_Q_RANK, MLA_KV_RANK = 8, 256, 128
MLA_NOPE, MLA_ROPE, MLA_V = 64, 32, 64
MLA_WIDTH = 512
N_MEM, MEM_HEADS, MEM_HEAD_DIM, MEM_WIDTH = 256, 4, 128, 512
ROPE_THETA = 500000.0
NORM_EPS = 1e-5
NEG_INF = -1e30
ALPHA = 2.0 ** 0.25
D_IN = 6048
N_DEV = 8

ADAM_LR, ADAM_B1, ADAM_B2, ADAM_EPS, ADAM_WD, ADAM_STEP = 0.001, 0.9, 0.999, 1e-08, 0.01, 10

D_INW = 6144
PIECE_WIDTHS = (1024, 1024, 1024, 1024, 512, 512, 512, 512)
PIECE_OFFS = (0, 1024, 2048, 3072, 4096, 4608, 5120, 5632)
LANES = 128
VMEM_LIMIT = 56 * 1024 * 1024


def _params(*sem):
    kw = dict(vmem_limit_bytes=VMEM_LIMIT)
    if sem:
        kw["dimension_semantics"] = sem
    return pltpu.CompilerParams(**kw)


def _dot(a, b):
    return jnp.dot(a, b, preferred_element_type=F32)


def _dot_nt(a, b):
    return lax.dot_general(a, b, (((1,), (1,)), ((), ())), preferred_element_type=F32)


def _dot_tn(a, b):
    return lax.dot_general(a, b, (((0,), (0,)), ((), ())), preferred_element_type=F32)


def _sigmoid(x):
    return 1.0 / (1.0 + jnp.exp(-x))


def _rope_fwd(x, c, sa, sb, half):
    n = x.shape[-1]
    return x * c + pltpu.roll(x, n - half, 1) * sa + pltpu.roll(x, half, 1) * sb


def _rope_bwd(dy, c, sa, sb, half):
    n = dy.shape[-1]
    return dy * c + pltpu.roll(dy * sa, half, 1) + pltpu.roll(dy * sb, n - half, 1)


def mm_nn(a, b, out_dtype, tm, tn, name):
    m, k = a.shape
    n = b.shape[1]

    def body(a_ref, b_ref, o_ref):
        o_ref[...] = _dot(a_ref[...].astype(BF16), b_ref[...].astype(BF16)).astype(o_ref.dtype)

    return pl.pallas_call(
        body, grid=(n // tn, m // tm),
        in_specs=[pl.BlockSpec((tm, k), lambda j, i: (i, 0)), pl.BlockSpec((k, tn), lambda j, i: (0, j))],
        out_specs=pl.BlockSpec((tm, tn), lambda j, i: (i, j)),
        out_shape=SDS((m, n), out_dtype), name=name,
        compiler_params=_params("parallel", "parallel"))(a, b)


def mm_tn(a, b, tt, name):
    t, m = a.shape
    n = b.shape[1]

    def body(a_ref, b_ref, o_ref):
        @pl.when(pl.program_id(0) == 0)
        def _():
            o_ref[...] = jnp.zeros_like(o_ref)

        o_ref[...] += _dot_tn(a_ref[...].astype(BF16), b_ref[...].astype(BF16))

    return pl.pallas_call(
        body, grid=(t // tt,),
        in_specs=[pl.BlockSpec((tt, m), lambda i: (i, 0)), pl.BlockSpec((tt, n), lambda i: (i, 0))],
        out_specs=pl.BlockSpec((m, n), lambda i: (0, 0)),
        out_shape=SDS((m, n), F32), name=name,
        compiler_params=_params("arbitrary"))(a, b)


def ln_emb_fwd(x2, g, b):
    t, d = x2.shape
    tm = 512

    def body(x_ref, g_ref, b_ref, h32_ref, h16_ref):
        x = x_ref[...]
        mu = jnp.mean(x, axis=-1, keepdims=True)
        xc = x - mu
        var = jnp.mean(xc * xc, axis=-1, keepdims=True)
        h = xc * lax.rsqrt(var + NORM_EPS) * g_ref[...] + b_ref[...]
        h32_ref[...] = h
        h16_ref[...] = h.astype(BF16)

    row = pl.BlockSpec((tm, d), lambda i: (i, 0))
    vec = pl.BlockSpec((1, d), lambda i: (0, 0))
    return pl.pallas_call(
        body, grid=(t // tm,), in_specs=[row, vec, vec], out_specs=[row, row],
        out_shape=[SDS((t, d), F32), SDS((t, d), BF16)], name="ln_emb_fwd",
        compiler_params=_params("parallel"))(x2, g, b)


Q_BLK = 128
UNROLL_FWD = 8
UNROLL_BWD = 4


def _pattern_geometry(d):
    length = SEQ // d
    nblk = length // Q_BLK
    kwin = min(2 * Q_BLK, length)
    return length, nblk, kwin


def _block_coords(idx, d):
    length, nblk, kwin = _pattern_geometry(d)
    r = lax.shift_right_logical(idx, nblk.bit_length() - 1)
    i = idx & (nblk - 1)
    q0 = pl.multiple_of(r * length + i * Q_BLK, Q_BLK)
    ks = jnp.clip(i * Q_BLK - N_SIDE, 0, length - kwin)
    k0 = pl.multiple_of(r * length + ks, N_SIDE)
    qpos = i * Q_BLK + lax.broadcasted_iota(jnp.int32, (Q_BLK, kwin), 0)
    kpos = ks + lax.broadcasted_iota(jnp.int32, (Q_BLK, kwin), 1)
    valid = jnp.abs(kpos - qpos) <= N_SIDE
    return q0, k0, kwin, valid


def _deinterleave(src_ref, dst_ref, d, dtype, tmp_ref):
    if d == 1:
        dst_ref[...] = src_ref[...].astype(dtype)
        return
    q = SEQ // 4
    if d == 4:
        for r in range(4):
            dst_ref[r * q:(r + 1) * q, :] = src_ref[pl.ds(r, q, stride=4), :].astype(dtype)
        return
    assert d == 16
    n = SEQ // 16
    for r in range(4):
        tmp_ref[r * q:(r + 1) * q, :] = src_ref[pl.ds(r, q, stride=4), :]
    for r in range(4):
        for j in range(4):
            dst_ref[(r + 4 * j) * n:(r + 4 * j + 1) * n, :] = tmp_ref[pl.ds(r * q + j, n, stride=4), :].astype(dtype)


def _interleave(src_ref, dst_ref, d, tmp_ref, accumulate):
    q = SEQ // 4
    if d == 16:
        n = SEQ // 16
        for r in range(4):
            for j in range(4):
                tmp_ref[pl.ds(r * q + j, n, stride=4), :] = src_ref[(r + 4 * j) * n:(r + 4 * j + 1) * n, :]
        src_ref = tmp_ref
    else:
        assert d == 4
    for r in range(4):
        rows = pl.ds(r, q, stride=4)
        val = src_ref[r * q:(r + 1) * q, :]
        dst_ref[rows, :] = dst_ref[rows, :] + val if accumulate else val


def a_attn_fwd(proj, ca, sa, sb, nb):
    t = proj.shape[0]
    n_pairs = A_WIDTH // LANES

    def body(q_ref, k_ref, v_ref, c_ref, sa_ref, sb_ref, y_ref, lse_ref,
             qr_s, kr_s, qd_s, kd_s, vd_s, oc_s, lc_s, o1_s, l1_s, o2_s, l2_s, o3_s, l3_s, tmp_s):
        c, s_a, s_b = c_ref[...], sa_ref[...], sb_ref[...]
        qr_s[...] = _rope_fwd(q_ref[...], c, s_a, s_b, A_ROT // 2) * (A_HEAD_DIM ** -0.5)
        kr_s[...] = _rope_fwd(k_ref[...], c, s_a, s_b, A_ROT // 2)
        head0 = lax.broadcasted_iota(jnp.int32, (Q_BLK, LANES), 1) < A_HEAD_DIM
        nat = ((o1_s, l1_s), (o2_s, l2_s), (o3_s, l3_s))

        for g, d in enumerate(DILATIONS):
            _deinterleave(qr_s, qd_s, d, BF16, tmp_s)
            _deinterleave(kr_s, kd_s, d, BF16, tmp_s)
            _deinterleave(v_ref, vd_s, d, BF16, tmp_s)
            o_dst, l_dst = (nat[g] if d == 1 else (oc_s, lc_s))

            def block(idx, carry, d=d, o_dst=o_dst, l_dst=l_dst):
                q0, k0, kwin, valid = _block_coords(idx, d)
                qb = qd_s[pl.ds(q0, Q_BLK), :]
                kb = kd_s[pl.ds(k0, kwin), :]
                vb = vd_s[pl.ds(k0, kwin), :]
                zero = jnp.zeros_like(qb)
                q2 = jnp.concatenate([jnp.where(head0, qb, zero), jnp.where(head0, zero, qb)], 0)
                s = jnp.where(jnp.concatenate([valid, valid], 0), _dot_nt(q2, kb), NEG_INF)
                m = jnp.max(s, axis=-1, keepdims=True)
                p = jnp.exp(s - m)
                l = jnp.sum(p, axis=-1, keepdims=True)
                o2 = _dot(p.astype(BF16), vb) / l
                l2 = m + jnp.log(l)
                o_dst[pl.ds(q0, Q_BLK), :] = jnp.where(head0, o2[:Q_BLK], o2[Q_BLK:])
                l_dst[pl.ds(q0, Q_BLK), :] = jnp.where(head0, l2[:Q_BLK], l2[Q_BLK:])
                return carry

            lax.fori_loop(0, SEQ // Q_BLK, block, 0, unroll=UNROLL_FWD)
            if d > 1:
                _interleave(oc_s, nat[g][0], d, tmp_s, False)
                _interleave(lc_s, nat[g][1], d, tmp_s, False)

        def merge(ci, carry):
            rows = pl.ds(pl.multiple_of(ci * 256, 256), 256)
            l1, l2, l3 = l1_s[rows, :], l2_s[rows, :], l3_s[rows, :]
            m = jnp.maximum(jnp.maximum(l1, l2), l3)
            w1, w2, w3 = jnp.exp(l1 - m), jnp.exp(l2 - m), jnp.exp(l3 - m)
            w = w1 + w2 + w3
            y_ref[rows, :] = (w1 * o1_s[rows, :] + w2 * o2_s[rows, :] + w3 * o3_s[rows, :]) / w
            lse_ref[rows, :] = m + jnp.log(w)
            return carry

        lax.fori_loop(0, SEQ // 256, merge, 0)

    def col(off):
        return pl.BlockSpec((SEQ, LANES), lambda b, hp: (b, off + hp))

    tab = pl.BlockSpec((SEQ, LANES), lambda b, hp: (b, 0))
    out = pl.BlockSpec((SEQ, LANES), lambda b, hp: (b, hp))
    f32s = pltpu.VMEM((SEQ, LANES), F32)
    b16s = pltpu.VMEM((SEQ, LANES), BF16)
    return pl.pallas_call(
        body, grid=(nb, n_pairs),
        in_specs=[col(0), col(n_pairs), col(2 * n_pairs), tab, tab, tab],
        out_specs=[out, out],
        out_shape=[SDS((t, A_WIDTH), F32), SDS((t, A_WIDTH), F32)],
        scratch_shapes=[f32s, f32s, b16s, b16s, b16s] + [f32s] * 9,
        name="a_attn_fwd", compiler_params=_params("parallel", "parallel"))(proj, proj, proj, ca, sa, sb)


def a_attn_bwd(proj, ca, sa, sb, dy, y, lse, nb, xch):
    t = proj.shape[0]
    n_pairs = A_WIDTH // LANES

    def body(q_ref, k_ref, v_ref, c_ref, sa_ref, sb_ref, do_ref, y_ref, lse_ref, dq_ref, dk_ref, dv_ref,
             qr_s, kr_s, l0n_s, l1n_s, d0n_s, d1n_s, qd_s, kd_s, vd_s, dod_s, l0d_s, l1d_s, d0d_s, d1d_s,
             dqc_s, dkc_s, dvc_s, dqn_s, dkn_s, dvn_s, tmp_s):
        c, s_a, s_b = c_ref[...], sa_ref[...], sb_ref[...]
        qr_s[...] = _rope_fwd(q_ref[...], c, s_a, s_b, A_ROT // 2) * (A_HEAD_DIM ** -0.5)
        kr_s[...] = _rope_fwd(k_ref[...], c, s_a, s_b, A_ROT // 2)
        head0 = lax.broadcasted_iota(jnp.int32, (Q_BLK, LANES), 1) < A_HEAD_DIM

        def per_head_rows(ci, carry):
            rows = pl.ds(pl.multiple_of(ci * 256, 256), 256)
            h0 = lax.broadcasted_iota(jnp.int32, (256, LANES), 1) < A_HEAD_DIM
            tt = do_ref[rows, :] * y_ref[rows, :]
            d0n_s[rows, :] = jnp.broadcast_to(jnp.sum(jnp.where(h0, tt, 0.0), axis=-1, keepdims=True), (256, LANES))
            d1n_s[rows, :] = jnp.broadcast_to(jnp.sum(jnp.where(h0, 0.0, tt), axis=-1, keepdims=True), (256, LANES))
            l = lse_ref[rows, :]
            lr = pltpu.roll(l, A_HEAD_DIM, 1)
            l0n_s[rows, :] = jnp.where(h0, l, lr)
            l1n_s[rows, :] = jnp.where(h0, lr, l)
            return carry

        lax.fori_loop(0, SEQ // 256, per_head_rows, 0)
        assert DILATIONS[0] == 1

        for d in DILATIONS:
            _deinterleave(qr_s, qd_s, d, BF16, tmp_s)
            _deinterleave(kr_s, kd_s, d, BF16, tmp_s)
            _deinterleave(v_ref, vd_s, d, BF16, tmp_s)
            _deinterleave(do_ref, dod_s, d, BF16, tmp_s)
            if d > 1:
                for src, dst in ((l0n_s, l0d_s), (l1n_s, l1d_s), (d0n_s, d0d_s), (d1n_s, d1d_s)):
                    _deinterleave(src, dst, d, F32, tmp_s)
            l0, l1, d0, d1 = (l0n_s, l1n_s, d0n_s, d1n_s) if d == 1 else (l0d_s, l1d_s, d0d_s, d1d_s)
            dq_dst, dk_dst, dv_dst = (dqn_s, dkn_s, dvn_s) if d == 1 else (dqc_s, dkc_s, dvc_s)
            dk_dst[...] = jnp.zeros_like(dk_dst)
            dv_dst[...] = jnp.zeros_like(dv_dst)

            def block(idx, carry, d=d, l0=l0, l1=l1, d0=d0, d1=d1, dq_dst=dq_dst, dk_dst=dk_dst, dv_dst=dv_dst):
                q0, k0, kwin, valid = _block_coords(idx, d)
                qrows = pl.ds(q0, Q_BLK)
                krows = pl.ds(k0, kwin)
                qb, dob = qd_s[qrows, :], dod_s[qrows, :]
                kb, vb = kd_s[krows, :], vd_s[krows, :]
                zero = jnp.zeros_like(qb)
                q2 = jnp.concatenate([jnp.where(head0, qb, zero), jnp.where(head0, zero, qb)], 0)
                do2 = jnp.concatenate([jnp.where(head0, dob, zero), jnp.where(head0, zero, dob)], 0)
                wide = lambda x: jnp.concatenate([x] * (kwin // LANES), 1)
                lse2 = wide(jnp.concatenate([l0[qrows, :], l1[qrows, :]], 0))
                dd2 = wide(jnp.concatenate([d0[qrows, :], d1[qrows, :]], 0))
                s = jnp.where(jnp.concatenate([valid, valid], 0), _dot_nt(q2, kb), NEG_INF)
                p = jnp.exp(s - lse2)
                ds = (p * (_dot_nt(do2, vb) - dd2)).astype(BF16)
                dq2 = _dot(ds, kb)
                dq_dst[qrows, :] = jnp.where(head0, dq2[:Q_BLK], dq2[Q_BLK:])
                dk_dst[krows, :] += _dot_tn(ds, q2)
                dv_dst[krows, :] += _dot_tn(p.astype(BF16), do2)
                return carry

            lax.fori_loop(0, SEQ // Q_BLK, block, 0, unroll=UNROLL_BWD)
            if d > 1:
                _interleave(dqc_s, dqn_s, d, tmp_s, True)
                _interleave(dkc_s, dkn_s, d, tmp_s, True)
                _interleave(dvc_s, dvn_s, d, tmp_s, True)

        dq_ref[...] = _rope_bwd(dqn_s[...] * (A_HEAD_DIM ** -0.5), c, s_a, s_b, A_ROT // 2).astype(BF16)
        dk_ref[...] = _rope_bwd(dkn_s[...], c, s_a, s_b, A_ROT // 2).astype(BF16)
        dv_ref[...] = dvn_s[...].astype(BF16)

    def col(off):
        return pl.BlockSpec((SEQ, LANES), lambda b, hp: (b, off + hp))

    tab = pl.BlockSpec((SEQ, LANES), lambda b, hp: (b, 0))
    blk = pl.BlockSpec((SEQ, LANES), lambda b, hp: (b, hp))
    f32s = pltpu.VMEM((SEQ, LANES), F32)
    b16s = pltpu.VMEM((SEQ, LANES), BF16)
    return call_hosting_exchange(
        body, xch, grid=(nb, n_pairs),
        in_specs=[col(0), col(n_pairs), col(2 * n_pairs), tab, tab, tab, blk, blk, blk],
        out_specs=[blk, blk, blk],
        out_shape=[SDS((t, A_WIDTH), BF16)] * 3,
        scratch_shapes=[f32s] * 6 + [b16s] * 4 + [f32s] * 11,
        name="a_attn_bwd", operands=(proj, proj, proj, ca, sa, sb, dy, y, lse))


MLA_SCALE = (MLA_NOPE + MLA_ROPE) ** -0.5
MLA_QW = MLA_HEADS * LANES
MLA_KVW = MLA_QW + MLA_WIDTH


def _rms(x, g):
    r = lax.rsqrt(jnp.mean(x * x, axis=-1, keepdims=True) + NORM_EPS)
    return x * r * g, r


def _rms_bwd(dn, x, r, g):
    tg = dn * g
    dx = r * tg - x * (r * r * r) * jnp.mean(tg * x, axis=-1, keepdims=True)
    return dx, jnp.sum(dn * x * r, axis=0, keepdims=True)


def mla_prep_fwd(proj, cm, sma, smb, g_cq, g_ckv, wuq, wkv):
    t = proj.shape[0]
    tm = 512

    def body(cq_ref, ckv_ref, kr_ref, c_ref, sa_ref, sb_ref, gq_ref, gkv_ref, wuq_ref, wkv_ref, q_ref, k_ref, v_ref):
        c, s_a, s_b = c_ref[...], sa_ref[...], sb_ref[...]
        cqn, _ = _rms(cq_ref[...], gq_ref[...])
        qf = _dot(cqn.astype(BF16), wuq_ref[...])
        ckvn, _ = _rms(ckv_ref[...], gkv_ref[...])
        kvf = _dot(ckvn.astype(BF16), wkv_ref[...])
        krope = _rope_fwd(kr_ref[...], c, s_a, s_b, MLA_ROPE // 2)
        for h in range(MLA_HEADS):
            cols = slice(h * LANES, (h + 1) * LANES)
            q_ref[:, cols] = (_rope_fwd(qf[:, cols], c, s_a, s_b, MLA_ROPE // 2) * MLA_SCALE).astype(BF16)
            k_ref[:, cols] = (kvf[:, cols] + krope).astype(BF16)
        v_ref[...] = kvf[:, MLA_QW:].astype(BF16)

    def row(w, j):
        return pl.BlockSpec((tm, w), lambda i: (i, j))

    def full(a):
        return pl.BlockSpec(a.shape, lambda i: (0, 0))

    return pl.pallas_call(
        body, grid=(t // tm,),
        in_specs=[row(256, 4096 // 256), row(128, 4352 // 128), row(128, 4480 // 128), row(128, 0), row(128, 0), row(128, 0),
                  full(g_cq), full(g_ckv), full(wuq), full(wkv)],
        out_specs=[row(MLA_QW, 0), row(MLA_QW, 0), row(MLA_WIDTH, 0)],
        out_shape=[SDS((t, MLA_QW), BF16), SDS((t, MLA_QW), BF16), SDS((t, MLA_WIDTH), BF16)],
        name="mla_prep_fwd", compiler_params=_params("parallel"))(proj, proj, proj, cm, sma, smb, g_cq, g_ckv, wuq, wkv)


def mla_prep_bwd(proj, cm, sma, smb, g_cq, g_ckv, wuq, wkv, dq, dk, dv):
    t = proj.shape[0]
    tm = 512

    def body(cq_ref, ckv_ref, c_ref, sa_ref, sb_ref, gq_ref, gkv_ref, wuq_ref, wkv_ref, dq_ref, dk_ref, dv_ref,
             dcc_ref, dqf_ref, cqn_ref, dkvf_ref, ckvn_ref, dgq_ref, dgkv_ref):
        @pl.when(pl.program_id(0) == 0)
        def _():
            dgq_ref[...] = jnp.zeros_like(dgq_ref)
            dgkv_ref[...] = jnp.zeros_like(dgkv_ref)

        c, s_a, s_b = c_ref[...], sa_ref[...], sb_ref[...]
        cq, ckv = cq_ref[...], ckv_ref[...]
        cqn, rq = _rms(cq, gq_ref[...])
        ckvn, rkv = _rms(ckv, gkv_ref[...])
        cqn_ref[...] = cqn.astype(BF16)
        ckvn_ref[...] = ckvn.astype(BF16)
        lane = lax.broadcasted_iota(jnp.int32, (tm, LANES), 1)
        rope_lanes = (lane >= MLA_NOPE) & (lane < MLA_NOPE + MLA_ROPE)
        dkrope = jnp.zeros((tm, LANES), F32)
        for h in range(MLA_HEADS):
            cols = slice(h * LANES, (h + 1) * LANES)
            dqf_ref[:, cols] = _rope_bwd(dq_ref[:, cols] * MLA_SCALE, c, s_a, s_b, MLA_ROPE // 2).astype(BF16)
            dkh = dk_ref[:, cols]
            dkvf_ref[:, cols] = dkh.astype(BF16)
            dkrope = dkrope + dkh
        dkvf_ref[:, MLA_QW:] = dv_ref[...].astype(BF16)
        dkr = _rope_bwd(jnp.where(rope_lanes, dkrope, 0.0), c, s_a, s_b, MLA_ROPE // 2)
        dcqn = _dot_nt(dqf_ref[...], wuq_ref[...])
        dckvn = _dot_nt(dkvf_ref[...], wkv_ref[...])
        dcq, dgq = _rms_bwd(dcqn, cq, rq, gq_ref[...])
        dckv, dgkv = _rms_bwd(dckvn, ckv, rkv, gkv_ref[...])
        dgq_ref[...] += dgq
        dgkv_ref[...] += dgkv
        dcc_ref[:, 0:256] = dcq.astype(BF16)
        dcc_ref[:, 256:384] = dckv.astype(BF16)
        dcc_ref[:, 384:512] = dkr.astype(BF16)

    def row(w, j):
        return pl.BlockSpec((tm, w), lambda i: (i, j))

    def full(a):
        return pl.BlockSpec(a.shape, lambda i: (0, 0))

    return pl.pallas_call(
        body, grid=(t // tm,),
        in_specs=[row(256, 4096 // 256), row(128, 4352 // 128), row(128, 0), row(128, 0), row(128, 0),
                  full(g_cq), full(g_ckv), full(wuq), full(wkv), row(MLA_QW, 0), row(MLA_QW, 0), row(MLA_WIDTH, 0)],
        out_specs=[row(512, 0), row(MLA_QW, 0), row(256, 0), row(MLA_KVW, 0), row(128, 0), full(g_cq), full(g_ckv)],
        out_shape=[SDS((t, 512), BF16), SDS((t, MLA_QW), BF16), SDS((t, 256), BF16), SDS((t, MLA_KVW), BF16),
                   SDS((t, 128), BF16), SDS(g_cq.shape, F32), SDS(g_ckv.shape, F32)],
        name="mla_prep_bwd", compiler_params=_params("arbitrary"))(proj, proj, cm, sma, smb, g_cq, g_ckv, wuq, wkv, dq, dk, dv)


MLA_TQ = 256


def mla_attn_fwd(qb, kb, vb, nb):
    t = qb.shape[0]
    nq = SEQ // MLA_TQ
    n_pairs = MLA_HEADS // 2

    def body(q_ref, k_ref, v_ref, y_ref, lse_ref):
        head0 = lax.broadcasted_iota(jnp.int32, (MLA_TQ, LANES), 1) < MLA_V
        v = v_ref[...]
        outs, lses = [], []
        for h in range(2):
            cols = slice(h * LANES, (h + 1) * LANES)
            s = _dot_nt(q_ref[:, cols], k_ref[:, cols])
            m = jnp.max(s, axis=-1, keepdims=True)
            p = jnp.exp(s - m)
            l = jnp.sum(p, axis=-1, keepdims=True)
            outs.append(_dot(p.astype(BF16), v) / l)
            lses.append(m + jnp.log(l))
        y_ref[...] = jnp.where(head0, outs[0], outs[1])
        lse_ref[...] = jnp.where(head0, lses[0], lses[1])

    return pl.pallas_call(
        body, grid=(nb, n_pairs, nq),
        in_specs=[pl.BlockSpec((MLA_TQ, 2 * LANES), lambda b, hp, i: (b * nq + i, hp)),
                  pl.BlockSpec((SEQ, 2 * LANES), lambda b, hp, i: (b, hp)),
                  pl.BlockSpec((SEQ, LANES), lambda b, hp, i: (b, hp))],
        out_specs=[pl.BlockSpec((MLA_TQ, LANES), lambda b, hp, i: (b * nq + i, hp))] * 2,
        out_shape=[SDS((t, MLA_WIDTH), F32)] * 2,
        name="mla_attn_fwd", compiler_params=_params("parallel", "parallel", "parallel"))(qb, kb, vb)


def mla_attn_bwd(qb, kb, vb, dy, y, lse, nb, xch):
    t = qb.shape[0]
    nq = SEQ // MLA_TQ
    n_pairs = MLA_HEADS // 2

    def body(q_ref, k_ref, v_ref, do_ref, y_ref, lse_ref, dq_ref, dk_ref, dv_ref):
        @pl.when(pl.program_id(2) == 0)
        def _():
            dk_ref[...] = jnp.zeros_like(dk_ref)
            dv_ref[...] = jnp.zeros_like(dv_ref)

        head0 = lax.broadcasted_iota(jnp.int32, (MLA_TQ, LANES), 1) < MLA_V
        v = v_ref[...]
        do = do_ref[...]
        lse = lse_ref[...]
        tt = do * y_ref[...]
        dv = jnp.zeros((SEQ, LANES), F32)
        for h in range(2):
            sel = head0 if h == 0 else ~head0
            lo = h * MLA_V
            cols = slice(h * LANES, (h + 1) * LANES)
            q = q_ref[:, cols]
            k = k_ref[:, cols]
            dd = jnp.sum(jnp.where(sel, tt, 0.0), axis=-1, keepdims=True)
            doh = jnp.where(sel, do, 0.0).astype(BF16)
            p = jnp.exp(_dot_nt(q, k) - lse[:, lo:lo + 1])
            dp = _dot_nt(doh, v)
            ds = (p * (dp - dd)).astype(BF16)
            dq_ref[:, cols] = _dot(ds, k)
            dk_ref[:, cols] += _dot_tn(ds, q)
            dv = dv + _dot_tn(p.astype(BF16), doh)
        dv_ref[...] += dv

    qspec = pl.BlockSpec((MLA_TQ, 2 * LANES), lambda b, hp, i: (b * nq + i, hp))
    kspec = pl.BlockSpec((SEQ, 2 * LANES), lambda b, hp, i: (b, hp))
    vspec = pl.BlockSpec((SEQ, LANES), lambda b, hp, i: (b, hp))
    ospec = pl.BlockSpec((MLA_TQ, LANES), lambda b, hp, i: (b * nq + i, hp))
    return call_hosting_exchange(
        body, xch, grid=(nb, n_pairs, nq),
        in_specs=[qspec, kspec, vspec, ospec, ospec, ospec],
        out_specs=[qspec, kspec, vspec],
        out_shape=[SDS((t, MLA_QW), F32), SDS((t, MLA_QW), F32), SDS((t, MLA_WIDTH), F32)],
        scratch_shapes=[], name="mla_attn_bwd", operands=(qb, kb, vb, dy, y, lse))


MEM_TQ = 512
MEM_SCALE = MEM_HEAD_DIM ** -0.5
MQ_BLK = 5120 // LANES


def mem_attn_fwd(proj, mkv, nb):
    t = proj.shape[0]
    nq = SEQ // MEM_TQ

    def body(q_ref, mk_ref, mv_ref, y_ref):
        s = _dot_nt(q_ref[...].astype(BF16), mk_ref[...]) * MEM_SCALE
        m = jnp.max(s, axis=-1, keepdims=True)
        p = jnp.exp(s - m)
        l = jnp.sum(p, axis=-1, keepdims=True)
        y_ref[...] = _dot(p.astype(BF16), mv_ref[...]) / l

    return pl.pallas_call(
        body, grid=(nb, MEM_HEADS, nq),
        in_specs=[pl.BlockSpec((MEM_TQ, LANES), lambda b, h, i: (b * nq + i, MQ_BLK + h)),
                  pl.BlockSpec((N_MEM, LANES), lambda b, h, i: (b, h)),
                  pl.BlockSpec((N_MEM, LANES), lambda b, h, i: (b, MEM_HEADS + h))],
        out_specs=pl.BlockSpec((MEM_TQ, LANES), lambda b, h, i: (b * nq + i, h)),
        out_shape=SDS((t, MEM_WIDTH), F32),
        name="mem_attn_fwd", compiler_params=_params("parallel", "parallel", "parallel"))(proj, mkv, mkv)


def mem_attn_bwd(proj, mkv, dy, nb):
    t = proj.shape[0]
    nq = SEQ // MEM_TQ

    def body(q_ref, mk_ref, mv_ref, do_ref, dq_ref, dmk_ref, dmv_ref):
        @pl.when(pl.program_id(2) == 0)
        def _():
            dmk_ref[...] = jnp.zeros_like(dmk_ref)
            dmv_ref[...] = jnp.zeros_like(dmv_ref)

        q = q_ref[...].astype(BF16)
        mk, mv = mk_ref[...], mv_ref[...]
        do = do_ref[...].astype(BF16)
        s = _dot_nt(q, mk) * MEM_SCALE
        e = jnp.exp(s - jnp.max(s, axis=-1, keepdims=True))
        p = e / jnp.sum(e, axis=-1, keepdims=True)
        dp = _dot_nt(do, mv)
        ds = (p * (dp - jnp.sum(p * dp, axis=-1, keepdims=True)) * MEM_SCALE).astype(BF16)
        dq_ref[...] = _dot(ds, mk).astype(BF16)
        dmk_ref[...] += _dot_tn(ds, q)
        dmv_ref[...] += _dot_tn(p.astype(BF16), do)

    ospec = pl.BlockSpec((MEM_TQ, LANES), lambda b, h, i: (b * nq + i, h))
    kspec = pl.BlockSpec((N_MEM, LANES), lambda b, h, i: (b, h))
    return pl.pallas_call(
        body, grid=(nb, MEM_HEADS, nq),
        in_specs=[pl.BlockSpec((MEM_TQ, LANES), lambda b, h, i: (b * nq + i, MQ_BLK + h)),
                  kspec, pl.BlockSpec((N_MEM, LANES), lambda b, h, i: (b, MEM_HEADS + h)), ospec],
        out_specs=[ospec, kspec, kspec],
        out_shape=[SDS((t, MEM_WIDTH), BF16), SDS((nb * N_MEM, MEM_WIDTH), F32), SDS((nb * N_MEM, MEM_WIDTH), F32)],
        name="mem_attn_bwd", compiler_params=_params("parallel", "parallel", "arbitrary"))(proj, mkv, mkv, dy)


ROW_TM = 256
AG_BLK = 3072 // 1024
BG_BLK = 4608 // 512
MG_BLK = 5632 // 512
GROUPS = ((0, A_WIDTH), (A_WIDTH, MLA_WIDTH), (A_WIDTH + MLA_WIDTH, MEM_WIDTH))
D_MIX = 2048


def _gate_specs():
    def row(w, j):
        return pl.BlockSpec((ROW_TM, w), lambda i: (i, j))

    def vec(w):
        return pl.BlockSpec((1, w), lambda i: (0, 0))

    ys = [row(A_WIDTH, 0), row(MLA_WIDTH, 0), row(MEM_WIDTH, 0)]
    gates = [row(A_WIDTH, AG_BLK), row(MLA_WIDTH, BG_BLK), row(MEM_WIDTH, MG_BLK)]
    gains = [vec(A_WIDTH), vec(MLA_WIDTH), vec(MEM_WIDTH)]
    return row, vec, ys, gates, gains


def gate_fwd(ya, yb, ym, proj, goa, gob, gom):
    t = ya.shape[0]
    row, vec, ys, gates, gains = _gate_specs()

    def body(ya_ref, yb_ref, ym_ref, ga_ref, gb_ref, gm_ref, goa_ref, gob_ref, gom_ref, z_ref):
        for (off, w), y_ref, g_ref, go_ref in zip(GROUPS, (ya_ref, yb_ref, ym_ref), (ga_ref, gb_ref, gm_ref),
                                                  (goa_ref, gob_ref, gom_ref)):
            n, _ = _rms(y_ref[...], go_ref[...])
            gt = g_ref[...]
            z_ref[:, off:off + w] = (n * (gt * _sigmoid(gt))).astype(BF16)

    return pl.pallas_call(
        body, grid=(t // ROW_TM,), in_specs=ys + gates + gains, out_specs=row(D_MIX, 0),
        out_shape=SDS((t, D_MIX), BF16), name="gate_fwd",
        compiler_params=_params("parallel"))(ya, yb, ym, proj, proj, proj, goa, gob, gom)


def out_ln_loss(z, wout, h32, target, gp, bp):
    t, d = h32.shape

    def body(z_ref, w_ref, h_ref, t_ref, gp_ref, bp_ref, du32_ref, du16_ref, loss_ref, dgp_ref, dbp_ref):
        @pl.when(pl.program_id(0) == 0)
        def _():
            loss_ref[...] = jnp.zeros_like(loss_ref)
            dgp_ref[...] = jnp.zeros_like(dgp_ref)
            dbp_ref[...] = jnp.zeros_like(dbp_ref)

        g = gp_ref[...]
        u = ALPHA * h_ref[...] + _dot(z_ref[...], w_ref[...])
        mu = jnp.mean(u, axis=-1, keepdims=True)
        uc = u - mu
        rstd = lax.rsqrt(jnp.mean(uc * uc, axis=-1, keepdims=True) + NORM_EPS)
        xhat = uc * rstd
        err = xhat * g + bp_ref[...] - t_ref[...]
        tok = jnp.sum(err * err, axis=-1, keepdims=True) * (1.0 / d)
        loss_ref[...] += 0.5 * jnp.sum(tok, axis=0, keepdims=True)
        dout = err * (1.0 / d)
        dxhat = dout * g
        du = rstd * (dxhat - jnp.mean(dxhat, axis=-1, keepdims=True)
                     - xhat * jnp.mean(dxhat * xhat, axis=-1, keepdims=True))
        du32_ref[...] = du
        du16_ref[...] = du.astype(BF16)
        dgp_ref[...] += jnp.sum(dout * xhat, axis=0, keepdims=True)
        dbp_ref[...] += jnp.sum(dout, axis=0, keepdims=True)

    row = pl.BlockSpec((ROW_TM, d), lambda i: (i, 0))
    vec = pl.BlockSpec((1, d), lambda i: (0, 0))
    return pl.pallas_call(
        body, grid=(t // ROW_TM,),
        in_specs=[pl.BlockSpec((ROW_TM, D_MIX), lambda i: (i, 0)), pl.BlockSpec((D_MIX, d), lambda i: (0, 0)), row, row, vec, vec],
        out_specs=[row, row, pl.BlockSpec((1, LANES), lambda i: (0, 0)), vec, vec],
        out_shape=[SDS((t, d), F32), SDS((t, d), BF16), SDS((1, LANES), F32), SDS((1, d), F32), SDS((1, d), F32)],
        name="out_ln_loss", compiler_params=_params("arbitrary"))(z, wout, h32, target, gp, bp)


def gate_bwd(du16, wout, ya, yb, ym, proj, goa, gob, gom):
    t = ya.shape[0]
    row, vec, ys, gates, gains = _gate_specs()

    def body(du_ref, w_ref, ya_ref, yb_ref, ym_ref, ga_ref, gb_ref, gm_ref, goa_ref, gob_ref, gom_ref,
             dya_ref, dyb_ref, dym_ref, dga_ref, dgb_ref, dgm_ref, dgoa_ref, dgob_ref, dgom_ref):
        @pl.when(pl.program_id(0) == 0)
        def _():
            dgoa_ref[...] = jnp.zeros_like(dgoa_ref)
            dgob_ref[...] = jnp.zeros_like(dgob_ref)
            dgom_ref[...] = jnp.zeros_like(dgom_ref)

        dz = _dot_nt(du_ref[...], w_ref[...])
        for (off, w), y_ref, g_ref, go_ref, dy_ref, dg_ref, dgo_ref in zip(
                GROUPS, (ya_ref, yb_ref, ym_ref), (ga_ref, gb_ref, gm_ref), (goa_ref, gob_ref, gom_ref),
                (dya_ref, dyb_ref, dym_ref), (dga_ref, dgb_ref, dgm_ref), (dgoa_ref, dgob_ref, dgom_ref)):
            dzg = dz[:, off:off + w]
            y, gt, go = y_ref[...], g_ref[...], go_ref[...]
            n, r = _rms(y, go)
            sg = _sigmoid(gt)
            dg_ref[...] = (dzg * n * (sg * (1.0 + gt * (1.0 - sg)))).astype(BF16)
            dy, dgo = _rms_bwd(dzg * (gt * sg), y, r, go)
            dy_ref[...] = dy
            dgo_ref[...] += dgo

    widths = (A_WIDTH, MLA_WIDTH, MEM_WIDTH)
    return pl.pallas_call(
        body, grid=(t // ROW_TM,),
        in_specs=[row(D_MODEL, 0), pl.BlockSpec((D_MIX, D_MODEL), lambda i: (0, 0))] + ys + gates + gains,
        out_specs=[row(w, 0) for w in widths] * 2 + [vec(w) for w in widths],
        out_shape=[SDS((t, w), F32) for w in widths] + [SDS((t, w), BF16) for w in widths] + [SDS((1, w), F32) for w in widths],
        name="gate_bwd", compiler_params=_params("arbitrary"))(du16, wout, ya, yb, ym, proj, proj, proj, goa, gob, gom)


def dh_ln_bwd(pieces, win_w, du32, x2, g_emb, xch):
    t, d = x2.shape

    def body(*refs):
        p_refs = refs[:len(pieces)]
        w_ref, du_ref, x_ref, g_ref, dx_ref, dg_ref, db_ref = refs[len(pieces):]

        @pl.when(pl.program_id(0) == 0)
        def _():
            dg_ref[...] = jnp.zeros_like(dg_ref)
            db_ref[...] = jnp.zeros_like(db_ref)

        dh = ALPHA * du_ref[...]
        for p_ref, off, w in zip(p_refs, PIECE_OFFS, PIECE_WIDTHS):
            dh = dh + _dot_nt(p_ref[...], w_ref[:, off:off + w])
        x = x_ref[...]
        xc = x - jnp.mean(x, axis=-1, keepdims=True)
        rstd = lax.rsqrt(jnp.mean(xc * xc, axis=-1, keepdims=True) + NORM_EPS)
        xhat = xc * rstd
        dg_ref[...] += jnp.sum(dh * xhat, axis=0, keepdims=True)
        db_ref[...] += jnp.sum(dh, axis=0, keepdims=True)
        tg = dh * g_ref[...]
        dx_ref[...] = rstd * (tg - jnp.mean(tg, axis=-1, keepdims=True)
                              - xhat * jnp.mean(tg * xhat, axis=-1, keepdims=True))

    row = pl.BlockSpec((ROW_TM, d), lambda i: (i, 0))
    vec = pl.BlockSpec((1, d), lambda i: (0, 0))
    return call_hosting_exchange(
        body, xch, grid=(t // ROW_TM,),
        in_specs=[pl.BlockSpec((ROW_TM, w), lambda i: (i, 0)) for w in PIECE_WIDTHS]
        + [pl.BlockSpec(win_w.shape, lambda i: (0, 0)), row, row, vec],
        out_specs=[row, vec, vec],
        out_shape=[SDS((t, d), F32), SDS((1, d), F32), SDS((1, d), F32)],
        scratch_shapes=[], name="dh_ln_bwd", operands=(*pieces, win_w, du32, x2, g_emb))


def _adamw(w, g, m, v):
    m2 = ADAM_B1 * m + (1.0 - ADAM_B1) * g
    v2 = ADAM_B2 * v + (1.0 - ADAM_B2) * (g * g)
    m_hat = m2 / (1.0 - ADAM_B1 ** ADAM_STEP)
    v_hat = v2 / (1.0 - ADAM_B2 ** ADAM_STEP)
    return -ADAM_LR * (m_hat / (jnp.sqrt(v_hat) + ADAM_EPS) + ADAM_WD * w), m2, v2


def adamw_shard(w, parts, m, v, name):
    r, c = w.shape
    tr = min(r, 256)

    def body(w_ref, p_ref, m_ref, v_ref, g_ref, d_ref, nm_ref, nv_ref):
        g = p_ref[0]
        for k in range(1, N_DEV):
            g = g + p_ref[k]
        g_ref[...] = g
        d_ref[...], nm_ref[...], nv_ref[...] = _adamw(w_ref[...], g, m_ref[...], v_ref[...])

    blk = pl.BlockSpec((tr, c), lambda i: (i, 0))
    return pl.pallas_call(
        body, grid=(r // tr,),
        in_specs=[blk, pl.BlockSpec((N_DEV, tr, c), lambda i: (0, i, 0)), blk, blk],
        out_specs=[blk] * 4, out_shape=[SDS((r, c), F32)] * 4, name=name,
        compiler_params=_params("parallel"))(w, parts, m, v)


def _place():
    return lax.axis_index("x"), lax.axis_index("y"), lax.axis_index("c")


def _flat(px, py, pc):
    return 4 * px + 2 * py + pc


def _peer(x, y, c, k):
    return (1 - x if k & 4 else x, 1 - y if k & 2 else y, 1 - c if k & 1 else c)


def cast_shards(shards):
    def body(*refs):
        n = len(refs) // 2
        for i_ref, o_ref in zip(refs[:n], refs[n:]):
            o_ref[...] = i_ref[...].astype(BF16)

    return pl.pallas_call(body, out_shape=[SDS(s.shape, BF16) for s in shards], name="cast_shards",
                          compiler_params=_params())(*shards)


def allgather_weights(shards):
    n = len(shards)

    def body(*refs):
        ins, outs = refs[:n], refs[n:2 * n]
        send_sems, recv_sems, local_sems = refs[2 * n:]
        x, y, c = _place()
        me, sib = (x, y, c), (x, y, 1 - c)
        chips = [(1 - x, y), (x, 1 - y), (1 - x, 1 - y)]

        def copy(a, k, block, to, src=None):
            dst = outs[a].at[_flat(*block)]
            return pltpu.make_async_remote_copy(
                src_ref=dst if src is None else src, dst_ref=dst,
                send_sem=send_sems.at[a * 7 + k], recv_sem=recv_sems.at[a * 7 + k],
                device_id=to, device_id_type=MESH)

        mine = [pltpu.make_async_copy(ins[a], outs[a].at[_flat(*me)], local_sems.at[a]) for a in range(n)]
        for cp in mine:
            cp.start()
        first = []
        for a in range(n):
            first.append(copy(a, 0, me, sib, src=ins[a]))
            first += [copy(a, 1 + j, me, (*chip, c), src=ins[a]) for j, chip in enumerate(chips)]
        for cp in first:
            cp.start()
        passed = []
        for j, chip in enumerate(chips):
            for a in range(n):
                copy(a, 1 + j, (*chip, c), me).wait_recv()
                fwd = copy(a, 4 + j, (*chip, c), sib)
                fwd.start()
                passed.append(fwd)
        for a in range(n):
            copy(a, 0, sib, me).wait_recv()
            for j, chip in enumerate(chips):
                copy(a, 4 + j, (*chip, 1 - c), me).wait_recv()
        for cp in first + passed:
            cp.wait_send()
        for cp in mine:
            cp.wait()

    hbm = pl.BlockSpec(memory_space=pl.ANY)
    return pl.pallas_call(
        body, out_shape=[SDS((N_DEV,) + s.shape, s.dtype) for s in shards],
        in_specs=[hbm] * n, out_specs=[hbm] * n,
        scratch_shapes=[pltpu.SemaphoreType.DMA((7 * n,)), pltpu.SemaphoreType.DMA((7 * n,)), pltpu.SemaphoreType.DMA((n,))],
        name="allgather_weights", compiler_params=_params())(*shards)


ALL_DEVICES = tuple(range(N_DEV))


def _exchange_plan(src_refs, land_refs, dests, send_sems, recv_sems, local_sems):
    x, y, c = _place()
    me = _flat(x, y, c)
    plan = []
    for a, (src, land, dl) in enumerate(zip(src_refs, land_refs, dests)):
        for li, j in enumerate(dl):
            to = ((j >> 2) & 1, (j >> 1) & 1, j & 1)

            def push(slot, a=a, src=src, land=land, li=li, j=j, to=to):
                return pltpu.make_async_remote_copy(
                    src_ref=src.at[li], dst_ref=land.at[slot], send_sem=send_sems.at[a * N_DEV + j],
                    recv_sem=recv_sems.at[a * N_DEV + slot], device_id=to, device_id_type=MESH)

            own = pltpu.make_async_copy(src.at[li], land.at[j], local_sems.at[a])
            plan.append((j, push(me), own, [push(s) for s in range(N_DEV) if s != j]))
    return me, plan


def _exchange_start(me, plan):
    for j, send, own, _ in plan:
        @pl.when(me != j)
        def _(send=send):
            send.start()

        @pl.when(me == j)
        def _(own=own):
            own.start()


def _exchange_wait(me, plan):
    for j, send, own, arrivals in plan:
        @pl.when(me != j)
        def _(send=send):
            send.wait_send()

        @pl.when(me == j)
        def _(own=own, arrivals=arrivals):
            own.wait()
            for arrival in arrivals:
                arrival.wait_recv()


def call_hosting_exchange(core, xch, *, grid, in_specs, out_specs, out_shape, scratch_shapes, name, operands):
    srcs, dests, landing = xch
    n, n_in, n_out, n_scr = len(srcs), len(in_specs), len(out_specs), len(scratch_shapes)

    def body(*refs):
        ins, src_refs = refs[:n_in], refs[n_in:n_in + n]
        outs = refs[n_in + 2 * n:n_in + 2 * n + n_out]
        land_refs = refs[n_in + 2 * n + n_out:n_in + 3 * n + n_out]
        scratch = refs[n_in + 3 * n + n_out:n_in + 3 * n + n_out + n_scr]
        sems = refs[n_in + 3 * n + n_out + n_scr:]
        first = functools.reduce(jnp.logical_and, [pl.program_id(i) == 0 for i in range(len(grid))])
        last = functools.reduce(jnp.logical_and, [pl.program_id(i) == grid[i] - 1 for i in range(len(grid))])
        me, plan = _exchange_plan(src_refs, land_refs, dests, *sems)

        @pl.when(first)
        def _():
            _exchange_start(me, plan)

        core(*ins, *outs, *scratch)

        @pl.when(last)
        def _():
            _exchange_wait(me, plan)

    hbm = pl.BlockSpec(memory_space=pl.ANY)
    res = pl.pallas_call(
        body, grid=grid,
        in_specs=list(in_specs) + [hbm] * (2 * n), out_specs=list(out_specs) + [hbm] * n,
        out_shape=list(out_shape) + [SDS(l.shape, l.dtype) for l in landing],
        scratch_shapes=list(scratch_shapes) + [pltpu.SemaphoreType.DMA((N_DEV * n,)), pltpu.SemaphoreType.DMA((N_DEV * n,)),
                                               pltpu.SemaphoreType.DMA((n,))],
        input_output_aliases={n_in + n + k: n_out + k for k in range(n)},
        name=name, compiler_params=_params(*(("arbitrary",) * len(grid))))(*operands, *srcs, *landing)
    return res[:n_out], res[n_out:]


SLOT_ROWS = 8


def small_allreduce_adamw(loss_sum, grads, ws, ms, vs):
    n = len(grads)
    rows = [g.shape[0] for g in grads]
    total = SLOT_ROWS * (n + 1)

    def body(*refs):
        loss_ref, g_refs, w_refs = refs[0], refs[1:1 + n], refs[1 + n:1 + 2 * n]
        m_refs, v_refs = refs[1 + 2 * n:1 + 3 * n], refs[1 + 3 * n:1 + 4 * n]
        outs = refs[1 + 4 * n:2 + 8 * n]
        vec, gath, tot, send_sems, recv_sems = refs[2 + 8 * n:]
        x, y, c = _place()
        me = _flat(x, y, c)
        vec[...] = jnp.zeros_like(vec)
        vec[0:1, :] = loss_ref[...]
        for i in range(n):
            vec[SLOT_ROWS * (i + 1):SLOT_ROWS * (i + 1) + rows[i], :] = g_refs[i][...]
        gath[me] = vec[...]
        copies = []
        for k in range(1, N_DEV):
            peer = _peer(x, y, c, k)
            copies.append(pltpu.make_async_remote_copy(
                src_ref=vec, dst_ref=gath.at[me], send_sem=send_sems.at[k - 1], recv_sem=recv_sems.at[k - 1],
                device_id=peer, device_id_type=MESH))
        for cp in copies:
            cp.start()
        for cp in copies:
            cp.wait_recv()
        for cp in copies:
            cp.wait_send()
        g = gath[0]
        for j in range(1, N_DEV):
            g = g + gath[j]
        tot[...] = g
        outs[0][...] = tot[0:1, :]
        for i in range(n):
            gi = tot[SLOT_ROWS * (i + 1):SLOT_ROWS * (i + 1) + rows[i], :]
            outs[1 + i][...] = gi
            outs[1 + n + i][...], outs[1 + 2 * n + i][...], outs[1 + 3 * n + i][...] = _adamw(
                w_refs[i][...], gi, m_refs[i][...], v_refs[i][...])

    shapes = [SDS(g.shape, F32) for g in grads]
    return pl.pallas_call(
        body, out_shape=[SDS((1, LANES), F32)] + shapes * 4,
        scratch_shapes=[pltpu.VMEM((total, LANES), F32), pltpu.VMEM((N_DEV, total, LANES), F32), pltpu.VMEM((total, LANES), F32),
                        pltpu.SemaphoreType.DMA((7,)), pltpu.SemaphoreType.DMA((7,))],
        name="small_allreduce_adamw", compiler_params=_params())(loss_sum, *grads, *ws, *ms, *vs)


def _rope_tables(positions):
    pos = positions.astype(F32).reshape(-1, 1)

    def cs(r):
        inv_freq = ROPE_THETA ** (-(jnp.arange(0, r, 2, dtype=F32) / r))
        ang = pos * inv_freq
        return jnp.cos(ang), jnp.sin(ang)

    n = pos.shape[0]
    one = lambda w: jnp.ones((n, w), F32)
    zero = lambda w: jnp.zeros((n, w), F32)
    ca, sa = cs(A_ROT)
    rest = A_HEAD_DIM - A_ROT
    a_c = jnp.tile(jnp.concatenate([ca, ca, one(rest)], 1), (1, 2))
    a_sa = jnp.tile(jnp.concatenate([-sa, zero(A_ROT // 2 + rest)], 1), (1, 2))
    a_sb = jnp.tile(jnp.concatenate([zero(A_ROT // 2), sa, zero(rest)], 1), (1, 2))
    cm, sm = cs(MLA_ROPE)
    tail = LANES - MLA_NOPE - MLA_ROPE
    m_c = jnp.concatenate([one(MLA_NOPE), cm, cm, one(tail)], 1)
    m_sa = jnp.concatenate([zero(MLA_NOPE), -sm, zero(MLA_ROPE // 2 + tail)], 1)
    m_sb = jnp.concatenate([zero(MLA_NOPE + MLA_ROPE // 2), sm, zero(tail)], 1)
    return (a_c, a_sa, a_sb), (m_c, m_sa, m_sb)


KR_LO, KR_HI = 4480, 4512


def _w_in_working(g):
    w = g.transpose(1, 0, 2).reshape(D_MODEL, D_IN)
    z = lambda n: jnp.zeros((D_MODEL, n), w.dtype)
    return jnp.concatenate([w[:, :KR_LO], z(MLA_NOPE), w[:, KR_LO:KR_HI], z(LANES - MLA_NOPE - MLA_ROPE), w[:, KR_HI:]], 1)


W_IN_SHARD = D_IN // N_DEV
AG_SPLIT = 5 * W_IN_SHARD - 3 * A_WIDTH
BG_SPLIT = 6 * W_IN_SHARD - KR_HI


def _shards(cols):
    return cols.reshape(D_MODEL, -1, W_IN_SHARD).transpose(1, 0, 2)


def _w_in_shards_0_4(d_aq, d_ak, d_av, d_ag):
    return _shards(jnp.concatenate([d_aq, d_ak, d_av, d_ag[:, :AG_SPLIT]], 1))


def _w_in_shard_5(d_ag, d_cc, d_bg):
    kr = MLA_Q_RANK + MLA_KV_RANK + MLA_NOPE
    return _shards(jnp.concatenate([d_ag[:, AG_SPLIT:], d_cc[:, :MLA_Q_RANK + MLA_KV_RANK], d_cc[:, kr:kr + MLA_ROPE],
                                    d_bg[:, :BG_SPLIT]], 1))


def _w_in_shards_6_7(d_bg, d_mq, d_mg):
    return _shards(jnp.concatenate([d_bg[:, BG_SPLIT:], d_mq, d_mg], 1))


def _w_uq_working(g):
    w = jnp.pad(g.transpose(1, 0, 2), ((0, 0), (0, 0), (0, LANES - MLA_NOPE - MLA_ROPE)))
    return w.reshape(MLA_Q_RANK, MLA_QW)


def _w_uq_parts(dw):
    return dw.reshape(MLA_Q_RANK, MLA_HEADS, LANES)[:, :, :MLA_NOPE + MLA_ROPE].transpose(1, 0, 2)


def _w_ukv_working(g):
    wk = jnp.pad(g[:, :, :MLA_NOPE].transpose(1, 0, 2), ((0, 0), (0, 0), (0, LANES - MLA_NOPE)))
    wv = g[:, :, MLA_NOPE:].transpose(1, 0, 2)
    return jnp.concatenate([wk.reshape(MLA_KV_RANK, MLA_QW), wv.reshape(MLA_KV_RANK, MLA_WIDTH)], 1)


def _w_ukv_parts(dw):
    dk = dw[:, :MLA_QW].reshape(MLA_KV_RANK, MLA_HEADS, LANES)[:, :, :MLA_NOPE]
    dv = dw[:, MLA_QW:].reshape(MLA_KV_RANK, MLA_HEADS, MLA_V)
    return jnp.concatenate([dk, dv], -1).transpose(1, 0, 2)


SMALL_NAMES = ("g_emb", "b_emb", "g_cq", "g_ckv", "g_out_a", "g_out_b", "g_out_m", "g_post", "b_post")


def kernel(x, mem, positions, g_emb, b_emb, w_in, g_cq, g_ckv, w_uq, w_ukv, w_mem_kv, g_out_a, g_out_b, g_out_m, w_out, g_post, b_post, loss_target, m_g_emb, m_b_emb, m_w_in, m_g_cq, m_g_ckv, m_w_uq, m_w_ukv, m_w_mem_kv, m_g_out_a, m_g_out_b, m_g_out_m, m_w_out, m_g_post, m_b_post, v_g_emb, v_b_emb, v_w_in, v_g_cq, v_g_ckv, v_w_uq, v_w_ukv, v_w_mem_kv, v_g_out_a, v_g_out_b, v_g_out_m, v_w_out, v_g_post, v_b_post):
    nb = x.shape[0]
    t = nb * SEQ
    x2 = x.reshape(t, D_MODEL)
    tgt2 = loss_target.reshape(t, D_MODEL)
    mem2 = mem.reshape(nb * N_MEM, D_MODEL)
    g_emb2, b_emb2 = g_emb.reshape(1, -1), b_emb.reshape(1, -1)
    (a_c, a_sa, a_sb), (m_c, m_sa, m_sb) = _rope_tables(positions)

    g_in, g_uq, g_ukv, g_mem, g_out = allgather_weights(cast_shards((w_in[0], w_uq[0], w_ukv[0], w_mem_kv[0], w_out[0])))
    win_w = _w_in_working(g_in)
    wuq_w = _w_uq_working(g_uq)
    wkv_w = _w_ukv_working(g_ukv)
    wmem = g_mem.reshape(D_MODEL, 2 * MEM_WIDTH)
    wout = g_out.reshape(D_MIX, D_MODEL)

    h32, h16 = ln_emb_fwd(x2, g_emb2, b_emb2)
    proj = mm_nn(h16, win_w, F32, 512, 1536, "proj")
    ya, lse_a = a_attn_fwd(proj, a_c, a_sa, a_sb, nb)
    qb, kb, vb = mla_prep_fwd(proj, m_c, m_sa, m_sb, g_cq, g_ckv, wuq_w, wkv_w)
    yb, lse_b = mla_attn_fwd(qb, kb, vb, nb)
    mkv = mm_nn(mem2, wmem, BF16, nb * N_MEM, 512, "mem_kv")
    ym = mem_attn_fwd(proj, mkv, nb)
    z = gate_fwd(ya, yb, ym, proj, g_out_a, g_out_b, g_out_m)
    du32, du16, loss_sum, dg_post, db_post = out_ln_loss(z, wout, h32, tgt2, g_post, b_post)

    dya, dyb, dym, dag, dbg, dmg, dg_out_a, dg_out_b, dg_out_m = gate_bwd(
        du16, wout, ya, yb, ym, proj, g_out_a, g_out_b, g_out_m)
    dw_out = mm_tn(z, du16, 512, "dw_out")
    dmq, dmk, dmv = mem_attn_bwd(proj, mkv, dym, nb)
    dw_mem = mm_tn(mem2, jnp.concatenate([dmk, dmv], 1), nb * N_MEM, "dw_mem")
    d_ag, d_bg, d_mq, d_mg = [mm_tn(h16, p, 512, "dw_in_" + n) for n, p in (("ag", dag), ("bg", dbg), ("mq", dmq), ("mg", dmg))]
    landing = lambda w: lax.empty((N_DEV,) + w.shape, F32)
    big_w = (w_in[0], w_uq[0], w_ukv[0], w_mem_kv[0], w_out[0])
    (daq, dak, dav), (p_out, p_mem, p_in) = a_attn_bwd(
        proj, a_c, a_sa, a_sb, dya, ya, lse_a, nb,
        ((dw_out.reshape(N_DEV, D_MIX // N_DEV, D_MODEL), dw_mem.reshape(N_DEV, D_MODEL // N_DEV, 2 * MEM_WIDTH),
          _w_in_shards_6_7(d_bg, d_mq, d_mg)),
         (ALL_DEVICES, ALL_DEVICES, (6, 7)),
         (landing(w_out[0]), landing(w_mem_kv[0]), landing(w_in[0]))))
    d_aq, d_ak, d_av = [mm_tn(h16, p, 512, "dw_in_" + n) for n, p in (("aq", daq), ("ak", dak), ("av", dav))]
    (dqb, dkb, dvb), (p_in,) = mla_attn_bwd(
        qb, kb, vb, dyb, yb, lse_b, nb, ((_w_in_shards_0_4(d_aq, d_ak, d_av, d_ag),), ((0, 1, 2, 3, 4),), (p_in,)))
    dcc, dqf, cqn, dkvf, ckvn, dg_cq, dg_ckv = mla_prep_bwd(proj, m_c, m_sa, m_sb, g_cq, g_ckv, wuq_w, wkv_w, dqb, dkb, dvb)
    dw_uq = mm_tn(cqn, dqf, 512, "dw_uq")
    dw_ukv = mm_tn(ckvn, dkvf, 512, "dw_ukv")
    d_cc = mm_tn(h16, dcc, 512, "dw_in_cc")
    pieces = (daq, dak, dav, dag, dcc, dbg, dmq, dmg)
    (grad_x, dg_emb, db_emb), (p_in, p_uq, p_ukv) = dh_ln_bwd(
        pieces, win_w, du32, x2, g_emb2,
        ((_w_in_shard_5(d_ag, d_cc, d_bg), _w_uq_parts(dw_uq), _w_ukv_parts(dw_ukv)),
         ((5,), ALL_DEVICES, ALL_DEVICES),
         (p_in, landing(w_uq[0]), landing(w_ukv[0]))))

    parts = (p_in, p_uq, p_ukv, p_mem, p_out)
    big_m = (m_w_in[0], m_w_uq[0], m_w_ukv[0], m_w_mem_kv[0], m_w_out[0])
    big_v = (v_w_in[0], v_w_uq[0], v_w_ukv[0], v_w_mem_kv[0], v_w_out[0])
    big = {}
    for name, w, p, m, v in zip(("w_in", "w_uq", "w_ukv", "w_mem_kv", "w_out"), big_w, parts, big_m, big_v):
        big[name] = [o[None] for o in adamw_shard(w, p, m, v, "adamw_" + name)]

    small_w = (g_emb, b_emb, g_cq, g_ckv, g_out_a, g_out_b, g_out_m, g_post, b_post)
    small_m = (m_g_emb, m_b_emb, m_g_cq, m_g_ckv, m_g_out_a, m_g_out_b, m_g_out_m, m_g_post, m_b_post)
    small_v = (v_g_emb, v_b_emb, v_g_cq, v_g_ckv, v_g_out_a, v_g_out_b, v_g_out_m, v_g_post, v_b_post)
    small_g = (dg_emb, db_emb, dg_cq, dg_ckv, dg_out_a, dg_out_b, dg_out_m, dg_post, db_post)
    rows128 = lambda vals: [v.reshape(-1, LANES) for v in vals]
    res = small_allreduce_adamw(loss_sum, rows128(small_g), rows128(small_w), rows128(small_m), rows128(small_v))
    loss = res[0][0, 0]
    n_small = len(small_w)
    sg, sd, sm, sv = [[r.reshape(w.shape) for r, w in zip(res[1 + k * n_small:1 + (k + 1) * n_small], small_w)]
                      for k in range(4)]

    order = ("g_emb", "b_emb", "w_in", "g_cq", "g_ckv", "w_uq", "w_ukv", "w_mem_kv", "g_out_a", "g_out_b", "g_out_m",
             "w_out", "g_post", "b_post")
    small_idx = {n: i for i, n in enumerate(SMALL_NAMES)}
    outs = [loss, grad_x.reshape(x.shape)]
    for kind in range(4):
        for name in order:
            outs.append(big[name][kind] if name in big else (sg, sd, sm, sv)[kind][small_idx[name]])
    return tuple(outs)
```

```python
import functools

import jax
import jax.numpy as jnp
from jax import lax
from jax.experimental import pallas as pl
from jax.experimental.pallas import tpu as pltpu

F32 = jnp.float32
BF16 = jnp.bfloat16
SDS = jax.ShapeDtypeStruct
MESH = pl.DeviceIdType.MESH

D_MODEL = 1024
SEQ = 2048
A_HEADS, A_HEAD_DIM, A_ROT = 16, 64, 16
A_WIDTH = 1024
DILATIONS = (1, 4, 16)
N_SIDE = 64
MLA_HEADS, MLA_Q_RANK, MLA_KV_RANK = 8, 256, 128
MLA_NOPE, MLA_ROPE, MLA_V = 64, 32, 64
MLA_WIDTH = 512
N_MEM, MEM_HEADS, MEM_HEAD_DIM, MEM_WIDTH = 256, 4, 128, 512
ROPE_THETA = 500000.0
NORM_EPS = 1e-5
NEG_INF = -1e30
ALPHA = 2.0 ** 0.25
D_IN = 6048
N_DEV = 8

ADAM_LR, ADAM_B1, ADAM_B2, ADAM_EPS, ADAM_WD, ADAM_STEP = 0.001, 0.9, 0.999, 1e-08, 0.01, 10

D_INW = 6144
PIECE_WIDTHS = (1024, 1024, 1024, 1024, 512, 512, 512, 512)
PIECE_OFFS = (0, 1024, 2048, 3072, 4096, 4608, 5120, 5632)
LANES = 128
VMEM_LIMIT = 56 * 1024 * 1024


def _params(*sem):
    kw = dict(vmem_limit_bytes=VMEM_LIMIT)
    if sem:
        kw["dimension_semantics"] = sem
    return pltpu.CompilerParams(**kw)


def _dot(a, b):
    return jnp.dot(a, b, preferred_element_type=F32)


def _dot_nt(a, b):
    return lax.dot_general(a, b, (((1,), (1,)), ((), ())), preferred_element_type=F32)


def _dot_tn(a, b):
    return lax.dot_general(a, b, (((0,), (0,)), ((), ())), preferred_element_type=F32)


def _sigmoid(x):
    return 1.0 / (1.0 + jnp.exp(-x))


def _rope_fwd(x, c, sa, sb, half):
    n = x.shape[-1]
    return x * c + pltpu.roll(x, n - half, 1) * sa + pltpu.roll(x, half, 1) * sb


def _rope_bwd(dy, c, sa, sb, half):
    n = dy.shape[-1]
    return dy * c + pltpu.roll(dy * sa, half, 1) + pltpu.roll(dy * sb, n - half, 1)


def mm_nn(a, b, out_dtype, tm, tn, name):
    m, k = a.shape
    n = b.shape[1]

    def body(a_ref, b_ref, o_ref):
        o_ref[...] = _dot(a_ref[...].astype(BF16), b_ref[...].astype(BF16)).astype(o_ref.dtype)

    return pl.pallas_call(
        body, grid=(n // tn, m // tm),
        in_specs=[pl.BlockSpec((tm, k), lambda j, i: (i, 0)), pl.BlockSpec((k, tn), lambda j, i: (0, j))],
        out_specs=pl.BlockSpec((tm, tn), lambda j, i: (i, j)),
        out_shape=SDS((m, n), out_dtype), name=name,
        compiler_params=_params("parallel", "parallel"))(a, b)


def mm_tn(a, b, tt, name):
    t, m = a.shape
    n = b.shape[1]

    def body(a_ref, b_ref, o_ref):
        @pl.when(pl.program_id(0) == 0)
        def _():
            o_ref[...] = jnp.zeros_like(o_ref)

        o_ref[...] += _dot_tn(a_ref[...].astype(BF16), b_ref[...].astype(BF16))

    return pl.pallas_call(
        body, grid=(t // tt,),
        in_specs=[pl.BlockSpec((tt, m), lambda i: (i, 0)), pl.BlockSpec((tt, n), lambda i: (i, 0))],
        out_specs=pl.BlockSpec((m, n), lambda i: (0, 0)),
        out_shape=SDS((m, n), F32), name=name,
        compiler_params=_params("arbitrary"))(a, b)


def ln_emb_fwd(x2, g, b):
    t, d = x2.shape
    tm = 512

    def body(x_ref, g_ref, b_ref, h32_ref, h16_ref):
        x = x_ref[...]
        mu = jnp.mean(x, axis=-1, keepdims=True)
        xc = x - mu
        var = jnp.mean(xc * xc, axis=-1, keepdims=True)
        h = xc * lax.rsqrt(var + NORM_EPS) * g_ref[...] + b_ref[...]
        h32_ref[...] = h
        h16_ref[...] = h.astype(BF16)

    row = pl.BlockSpec((tm, d), lambda i: (i, 0))
    vec = pl.BlockSpec((1, d), lambda i: (0, 0))
    return pl.pallas_call(
        body, grid=(t // tm,), in_specs=[row, vec, vec], out_specs=[row, row],
        out_shape=[SDS((t, d), F32), SDS((t, d), BF16)], name="ln_emb_fwd",
        compiler_params=_params("parallel"))(x2, g, b)


Q_BLK = 128
UNROLL_FWD = 8
UNROLL_BWD = 4


def _pattern_geometry(d):
    length = SEQ // d
    nblk = length // Q_BLK
    kwin = min(2 * Q_BLK, length)
    return length, nblk, kwin


def _block_coords(idx, d):
    length, nblk, kwin = _pattern_geometry(d)
    r = lax.shift_right_logical(idx, nblk.bit_length() - 1)
    i = idx & (nblk - 1)
    q0 = pl.multiple_of(r * length + i * Q_BLK, Q_BLK)
    ks = jnp.clip(i * Q_BLK - N_SIDE, 0, length - kwin)
    k0 = pl.multiple_of(r * length + ks, N_SIDE)
    qpos = i * Q_BLK + lax.broadcasted_iota(jnp.int32, (Q_BLK, kwin), 0)
    kpos = ks + lax.broadcasted_iota(jnp.int32, (Q_BLK, kwin), 1)
    valid = jnp.abs(kpos - qpos) <= N_SIDE
    return q0, k0, kwin, valid


def _deinterleave(src_ref, dst_ref, d, dtype, tmp_ref):
    if d == 1:
        dst_ref[...] = src_ref[...].astype(dtype)
        return
    q = SEQ // 4
    if d == 4:
        for r in range(4):
            dst_ref[r * q:(r + 1) * q, :] = src_ref[pl.ds(r, q, stride=4), :].astype(dtype)
        return
    assert d == 16
    n = SEQ // 16
    for r in range(4):
        tmp_ref[r * q:(r + 1) * q, :] = src_ref[pl.ds(r, q, stride=4), :]
    for r in range(4):
        for j in range(4):
            dst_ref[(r + 4 * j) * n:(r + 4 * j + 1) * n, :] = tmp_ref[pl.ds(r * q + j, n, stride=4), :].astype(dtype)


def _interleave(src_ref, dst_ref, d, tmp_ref, accumulate):
    q = SEQ // 4
    if d == 16:
        n = SEQ // 16
        for r in range(4):
            for j in range(4):
                tmp_ref[pl.ds(r * q + j, n, stride=4), :] = src_ref[(r + 4 * j) * n:(r + 4 * j + 1) * n, :]
        src_ref = tmp_ref
    else:
        assert d == 4
    for r in range(4):
        rows = pl.ds(r, q, stride=4)
        val = src_ref[r * q:(r + 1) * q, :]
        dst_ref[rows, :] = dst_ref[rows, :] + val if accumulate else val


def a_attn_fwd(proj, ca, sa, sb, nb, xch):
    t = proj.shape[0]
    n_pairs = A_WIDTH // LANES

    def body(q_ref, k_ref, v_ref, c_ref, sa_ref, sb_ref, y_ref, lse_ref,
             qr_s, kr_s, qd_s, kd_s, vd_s, oc_s, lc_s, o1_s, l1_s, o2_s, l2_s, o3_s, l3_s, tmp_s):
        c, s_a, s_b = c_ref[...], sa_ref[...], sb_ref[...]
        qr_s[...] = _rope_fwd(q_ref[...], c, s_a, s_b, A_ROT // 2) * (A_HEAD_DIM ** -0.5)
        kr_s[...] = _rope_fwd(k_ref[...], c, s_a, s_b, A_ROT // 2)
        head0 = lax.broadcasted_iota(jnp.int32, (Q_BLK, LANES), 1) < A_HEAD_DIM
        nat = ((o1_s, l1_s), (o2_s, l2_s), (o3_s, l3_s))

        for g, d in enumerate(DILATIONS):
            _deinterleave(qr_s, qd_s, d, BF16, tmp_s)
            _deinterleave(kr_s, kd_s, d, BF16, tmp_s)
            _deinterleave(v_ref, vd_s, d, BF16, tmp_s)
            o_dst, l_dst = (nat[g] if d == 1 else (oc_s, lc_s))

            def block(idx, carry, d=d, o_dst=o_dst, l_dst=l_dst):
                q0, k0, kwin, valid = _block_coords(idx, d)
                qb = qd_s[pl.ds(q0, Q_BLK), :]
                kb = kd_s[pl.ds(k0, kwin), :]
                vb = vd_s[pl.ds(k0, kwin), :]
                zero = jnp.zeros_like(qb)
                q2 = jnp.concatenate([jnp.where(head0, qb, zero), jnp.where(head0, zero, qb)], 0)
                s = jnp.where(jnp.concatenate([valid, valid], 0), _dot_nt(q2, kb), NEG_INF)
                m = jnp.max(s, axis=-1, keepdims=True)
                p = jnp.exp(s - m)
                l = jnp.sum(p, axis=-1, keepdims=True)
                o2 = _dot(p.astype(BF16), vb) / l
                l2 = m + jnp.log(l)
                o_dst[pl.ds(q0, Q_BLK), :] = jnp.where(head0, o2[:Q_BLK], o2[Q_BLK:])
                l_dst[pl.ds(q0, Q_BLK), :] = jnp.where(head0, l2[:Q_BLK], l2[Q_BLK:])
                return carry

            lax.fori_loop(0, SEQ // Q_BLK, block, 0, unroll=UNROLL_FWD)
            if d > 1:
                _interleave(oc_s, nat[g][0], d, tmp_s, False)
                _interleave(lc_s, nat[g][1], d, tmp_s, False)

        def merge(ci, carry):
            rows = pl.ds(pl.multiple_of(ci * 256, 256), 256)
            l1, l2, l3 = l1_s[rows, :], l2_s[rows, :], l3_s[rows, :]
            m = jnp.maximum(jnp.maximum(l1, l2), l3)
            w1, w2, w3 = jnp.exp(l1 - m), jnp.exp(l2 - m), jnp.exp(l3 - m)
            w = w1 + w2 + w3
            y_ref[rows, :] = (w1 * o1_s[rows, :] + w2 * o2_s[rows, :] + w3 * o3_s[rows, :]) / w
            lse_ref[rows, :] = m + jnp.log(w)
            return carry

        lax.fori_loop(0, SEQ // 256, merge, 0)

    def col(off):
        return pl.BlockSpec((SEQ, LANES), lambda b, hp: (b, off + hp))

    tab = pl.BlockSpec((SEQ, LANES), lambda b, hp: (b, 0))
    out = pl.BlockSpec((SEQ, LANES), lambda b, hp: (b, hp))
    f32s = pltpu.VMEM((SEQ, LANES), F32)
    b16s = pltpu.VMEM((SEQ, LANES), BF16)
    return call_hosting_exchange(
        body, xch, grid=(nb, n_pairs),
        in_specs=[col(0), col(n_pairs), col(2 * n_pairs), tab, tab, tab],
        out_specs=[out, out],
        out_shape=[SDS((t, A_WIDTH), F32), SDS((t, A_WIDTH), F32)],
        scratch_shapes=[f32s, f32s, b16s, b16s, b16s] + [f32s] * 9,
        name="a_attn_fwd", operands=(proj, proj, proj, ca, sa, sb))


def a_attn_bwd(proj, ca, sa, sb, dy, y, lse, nb, xch):
    t = proj.shape[0]
    n_pairs = A_WIDTH // LANES

    def body(q_ref, k_ref, v_ref, c_ref, sa_ref, sb_ref, do_ref, y_ref, lse_ref, dq_ref, dk_ref, dv_ref,
             qr_s, kr_s, l0n_s, l1n_s, d0n_s, d1n_s, qd_s, kd_s, vd_s, dod_s, l0d_s, l1d_s, d0d_s, d1d_s,
             dqc_s, dkc_s, dvc_s, dqn_s, dkn_s, dvn_s, tmp_s):
        c, s_a, s_b = c_ref[...], sa_ref[...], sb_ref[...]
        qr_s[...] = _rope_fwd(q_ref[...], c, s_a, s_b, A_ROT // 2) * (A_HEAD_DIM ** -0.5)
        kr_s[...] = _rope_fwd(k_ref[...], c, s_a, s_b, A_ROT // 2)
        head0 = lax.broadcasted_iota(jnp.int32, (Q_BLK, LANES), 1) < A_HEAD_DIM

        def per_head_rows(ci, carry):
            rows = pl.ds(pl.multiple_of(ci * 256, 256), 256)
            h0 = lax.broadcasted_iota(jnp.int32, (256, LANES), 1) < A_HEAD_DIM
            tt = do_ref[rows, :] * y_ref[rows, :]
            d0n_s[rows, :] = jnp.broadcast_to(jnp.sum(jnp.where(h0, tt, 0.0), axis=-1, keepdims=True), (256, LANES))
            d1n_s[rows, :] = jnp.broadcast_to(jnp.sum(jnp.where(h0, 0.0, tt), axis=-1, keepdims=True), (256, LANES))
            l = lse_ref[rows, :]
            lr = pltpu.roll(l, A_HEAD_DIM, 1)
            l0n_s[rows, :] = jnp.where(h0, l, lr)
            l1n_s[rows, :] = jnp.where(h0, lr, l)
            return carry

        lax.fori_loop(0, SEQ // 256, per_head_rows, 0)
        assert DILATIONS[0] == 1

        for d in DILATIONS:
            _deinterleave(qr_s, qd_s, d, BF16, tmp_s)
            _deinterleave(kr_s, kd_s, d, BF16, tmp_s)
            _deinterleave(v_ref, vd_s, d, BF16, tmp_s)
            _deinterleave(do_ref, dod_s, d, BF16, tmp_s)
            if d > 1:
                for src, dst in ((l0n_s, l0d_s), (l1n_s, l1d_s), (d0n_s, d0d_s), (d1n_s, d1d_s)):
                    _deinterleave(src, dst, d, F32, tmp_s)
            l0, l1, d0, d1 = (l0n_s, l1n_s, d0n_s, d1n_s) if d == 1 else (l0d_s, l1d_s, d0d_s, d1d_s)
            dq_dst, dk_dst, dv_dst = (dqn_s, dkn_s, dvn_s) if d == 1 else (dqc_s, dkc_s, dvc_s)
            dk_dst[...] = jnp.zeros_like(dk_dst)
            dv_dst[...] = jnp.zeros_like(dv_dst)

            def block(idx, carry, d=d, l0=l0, l1=l1, d0=d0, d1=d1, dq_dst=dq_dst, dk_dst=dk_dst, dv_dst=dv_dst):
                q0, k0, kwin, valid = _block_coords(idx, d)
                qrows = pl.ds(q0, Q_BLK)
                krows = pl.ds(k0, kwin)
                qb, dob = qd_s[qrows, :], dod_s[qrows, :]
                kb, vb = kd_s[krows, :], vd_s[krows, :]
                zero = jnp.zeros_like(qb)
                q2 = jnp.concatenate([jnp.where(head0, qb, zero), jnp.where(head0, zero, qb)], 0)
                do2 = jnp.concatenate([jnp.where(head0, dob, zero), jnp.where(head0, zero, dob)], 0)
                wide = lambda x: jnp.concatenate([x] * (kwin // LANES), 1)
                lse2 = wide(jnp.concatenate([l0[qrows, :], l1[qrows, :]], 0))
                dd2 = wide(jnp.concatenate([d0[qrows, :], d1[qrows, :]], 0))
                s = jnp.where(jnp.concatenate([valid, valid], 0), _dot_nt(q2, kb), NEG_INF)
                p = jnp.exp(s - lse2)
                ds = (p * (_dot_nt(do2, vb) - dd2)).astype(BF16)
                dq2 = _dot(ds, kb)
                dq_dst[qrows, :] = jnp.where(head0, dq2[:Q_BLK], dq2[Q_BLK:])
                dk_dst[krows, :] += _dot_tn(ds, q2)
                dv_dst[krows, :] += _dot_tn(p.astype(BF16), do2)
                return carry

            lax.fori_loop(0, SEQ // Q_BLK, block, 0, unroll=UNROLL_BWD)
            if d > 1:
                _interleave(dqc_s, dqn_s, d, tmp_s, True)
                _interleave(dkc_s, dkn_s, d, tmp_s, True)
                _interleave(dvc_s, dvn_s, d, tmp_s, True)

        dq_ref[...] = _rope_bwd(dqn_s[...] * (A_HEAD_DIM ** -0.5), c, s_a, s_b, A_ROT // 2).astype(BF16)
        dk_ref[...] = _rope_bwd(dkn_s[...], c, s_a, s_b, A_ROT // 2).astype(BF16)
        dv_ref[...] = dvn_s[...].astype(BF16)

    def col(off):
        return pl.BlockSpec((SEQ, LANES), lambda b, hp: (b, off + hp))

    tab = pl.BlockSpec((SEQ, LANES), lambda b, hp: (b, 0))
    blk = pl.BlockSpec((SEQ, LANES), lambda b, hp: (b, hp))
    f32s = pltpu.VMEM((SEQ, LANES), F32)
    b16s = pltpu.VMEM((SEQ, LANES), BF16)
    return call_hosting_exchange(
        body, xch, grid=(nb, n_pairs),
        in_specs=[col(0), col(n_pairs), col(2 * n_pairs), tab, tab, tab, blk, blk, blk],
        out_specs=[blk, blk, blk],
        out_shape=[SDS((t, A_WIDTH), BF16)] * 3,
        scratch_shapes=[f32s] * 6 + [b16s] * 4 + [f32s] * 11,
        name="a_attn_bwd", operands=(proj, proj, proj, ca, sa, sb, dy, y, lse))


MLA_SCALE = (MLA_NOPE + MLA_ROPE) ** -0.5
MLA_QW = MLA_HEADS * LANES
MLA_KVW = MLA_QW + MLA_WIDTH


def _rms(x, g):
    r = lax.rsqrt(jnp.mean(x * x, axis=-1, keepdims=True) + NORM_EPS)
    return x * r * g, r


def _rms_bwd(dn, x, r, g):
    tg = dn * g
    dx = r * tg - x * (r * r * r) * jnp.mean(tg * x, axis=-1, keepdims=True)
    return dx, jnp.sum(dn * x * r, axis=0, keepdims=True)


def mla_prep_fwd(proj, cm, sma, smb, g_cq, g_ckv, wuq, wkv):
    t = proj.shape[0]
    tm = 512

    def body(cq_ref, ckv_ref, kr_ref, c_ref, sa_ref, sb_ref, gq_ref, gkv_ref, wuq_ref, wkv_ref, q_ref, k_ref, v_ref):
        c, s_a, s_b = c_ref[...], sa_ref[...], sb_ref[...]
        cqn, _ = _rms(cq_ref[...], gq_ref[...])
        qf = _dot(cqn.astype(BF16), wuq_ref[...])
        ckvn, _ = _rms(ckv_ref[...], gkv_ref[...])
        kvf = _dot(ckvn.astype(BF16), wkv_ref[...])
        krope = _rope_fwd(kr_ref[...], c, s_a, s_b, MLA_ROPE // 2)
        for h in range(MLA_HEADS):
            cols = slice(h * LANES, (h + 1) * LANES)
            q_ref[:, cols] = (_rope_fwd(qf[:, cols], c, s_a, s_b, MLA_ROPE // 2) * MLA_SCALE).astype(BF16)
            k_ref[:, cols] = (kvf[:, cols] + krope).astype(BF16)
        v_ref[...] = kvf[:, MLA_QW:].astype(BF16)

    def row(w, j):
        return pl.BlockSpec((tm, w), lambda i: (i, j))

    def full(a):
        return pl.BlockSpec(a.shape, lambda i: (0, 0))

    return pl.pallas_call(
        body, grid=(t // tm,),
        in_specs=[row(256, 4096 // 256), row(128, 4352 // 128), row(128, 4480 // 128), row(128, 0), row(128, 0), row(128, 0),
                  full(g_cq), full(g_ckv), full(wuq), full(wkv)],
        out_specs=[row(MLA_QW, 0), row(MLA_QW, 0), row(MLA_WIDTH, 0)],
        out_shape=[SDS((t, MLA_QW), BF16), SDS((t, MLA_QW), BF16), SDS((t, MLA_WIDTH), BF16)],
        name="mla_prep_fwd", compiler_params=_params("parallel"))(proj, proj, proj, cm, sma, smb, g_cq, g_ckv, wuq, wkv)


def mla_prep_bwd(proj, cm, sma, smb, g_cq, g_ckv, wuq, wkv, dq, dk, dv):
    t = proj.shape[0]
    tm = 512

    def body(cq_ref, ckv_ref, c_ref, sa_ref, sb_ref, gq_ref, gkv_ref, wuq_ref, wkv_ref, dq_ref, dk_ref, dv_ref,
             dcc_ref, dqf_ref, cqn_ref, dkvf_ref, ckvn_ref, dgq_ref, dgkv_ref):
        @pl.when(pl.program_id(0) == 0)
        def _():
            dgq_ref[...] = jnp.zeros_like(dgq_ref)
            dgkv_ref[...] = jnp.zeros_like(dgkv_ref)

        c, s_a, s_b = c_ref[...], sa_ref[...], sb_ref[...]
        cq, ckv = cq_ref[...], ckv_ref[...]
        cqn, rq = _rms(cq, gq_ref[...])
        ckvn, rkv = _rms(ckv, gkv_ref[...])
        cqn_ref[...] = cqn.astype(BF16)
        ckvn_ref[...] = ckvn.astype(BF16)
        lane = lax.broadcasted_iota(jnp.int32, (tm, LANES), 1)
        rope_lanes = (lane >= MLA_NOPE) & (lane < MLA_NOPE + MLA_ROPE)
        dkrope = jnp.zeros((tm, LANES), F32)
        for h in range(MLA_HEADS):
            cols = slice(h * LANES, (h + 1) * LANES)
            dqf_ref[:, cols] = _rope_bwd(dq_ref[:, cols] * MLA_SCALE, c, s_a, s_b, MLA_ROPE // 2).astype(BF16)
            dkh = dk_ref[:, cols]
            dkvf_ref[:, cols] = dkh.astype(BF16)
            dkrope = dkrope + dkh
        dkvf_ref[:, MLA_QW:] = dv_ref[...].astype(BF16)
        dkr = _rope_bwd(jnp.where(rope_lanes, dkrope, 0.0), c, s_a, s_b, MLA_ROPE // 2)
        dcqn = _dot_nt(dqf_ref[...], wuq_ref[...])
        dckvn = _dot_nt(dkvf_ref[...], wkv_ref[...])
        dcq, dgq = _rms_bwd(dcqn, cq, rq, gq_ref[...])
        dckv, dgkv = _rms_bwd(dckvn, ckv, rkv, gkv_ref[...])
        dgq_ref[...] += dgq
        dgkv_ref[...] += dgkv
        dcc_ref[:, 0:256] = dcq.astype(BF16)
        dcc_ref[:, 256:384] = dckv.astype(BF16)
        dcc_ref[:, 384:512] = dkr.astype(BF16)

    def row(w, j):
        return pl.BlockSpec((tm, w), lambda i: (i, j))

    def full(a):
        return pl.BlockSpec(a.shape, lambda i: (0, 0))

    return pl.pallas_call(
        body, grid=(t // tm,),
        in_specs=[row(256, 4096 // 256), row(128, 4352 // 128), row(128, 0), row(128, 0), row(128, 0),
                  full(g_cq), full(g_ckv), full(wuq), full(wkv), row(MLA_QW, 0), row(MLA_QW, 0), row(MLA_WIDTH, 0)],
        out_specs=[row(512, 0), row(MLA_QW, 0), row(256, 0), row(MLA_KVW, 0), row(128, 0), full(g_cq), full(g_ckv)],
        out_shape=[SDS((t, 512), BF16), SDS((t, MLA_QW), BF16), SDS((t, 256), BF16), SDS((t, MLA_KVW), BF16),
                   SDS((t, 128), BF16), SDS(g_cq.shape, F32), SDS(g_ckv.shape, F32)],
        name="mla_prep_bwd", compiler_params=_params("arbitrary"))(proj, proj, cm, sma, smb, g_cq, g_ckv, wuq, wkv, dq, dk, dv)


MLA_TQ = 256


def mla_attn_fwd(qb, kb, vb, nb):
    t = qb.shape[0]
    nq = SEQ // MLA_TQ
    n_pairs = MLA_HEADS // 2

    def body(q_ref, k_ref, v_ref, y_ref, lse_ref):
        head0 = lax.broadcasted_iota(jnp.int32, (MLA_TQ, LANES), 1) < MLA_V
        v = v_ref[...]
        outs, lses = [], []
        for h in range(2):
            cols = slice(h * LANES, (h + 1) * LANES)
            s = _dot_nt(q_ref[:, cols], k_ref[:, cols])
            m = jnp.max(s, axis=-1, keepdims=True)
            p = jnp.exp(s - m)
            l = jnp.sum(p, axis=-1, keepdims=True)
            outs.append(_dot(p.astype(BF16), v) / l)
            lses.append(m + jnp.log(l))
        y_ref[...] = jnp.where(head0, outs[0], outs[1])
        lse_ref[...] = jnp.where(head0, lses[0], lses[1])

    return pl.pallas_call(
        body, grid=(nb, n_pairs, nq),
        in_specs=[pl.BlockSpec((MLA_TQ, 2 * LANES), lambda b, hp, i: (b * nq + i, hp)),
                  pl.BlockSpec((SEQ, 2 * LANES), lambda b, hp, i: (b, hp)),
                  pl.BlockSpec((SEQ, LANES), lambda b, hp, i: (b, hp))],
        out_specs=[pl.BlockSpec((MLA_TQ, LANES), lambda b, hp, i: (b * nq + i, hp))] * 2,
        out_shape=[SDS((t, MLA_WIDTH), F32)] * 2,
        name="mla_attn_fwd", compiler_params=_params("parallel", "parallel", "parallel"))(qb, kb, vb)


def mla_attn_bwd(qb, kb, vb, dy, y, lse, nb, xch):
    t = qb.shape[0]
    nq = SEQ // MLA_TQ
    n_pairs = MLA_HEADS // 2

    def body(q_ref, k_ref, v_ref, do_ref, y_ref, lse_ref, dq_ref, dk_ref, dv_ref):
        @pl.when(pl.program_id(2) == 0)
        def _():
            dk_ref[...] = jnp.zeros_like(dk_ref)
            dv_ref[...] = jnp.zeros_like(dv_ref)

        head0 = lax.broadcasted_iota(jnp.int32, (MLA_TQ, LANES), 1) < MLA_V
        v = v_ref[...]
        do = do_ref[...]
        lse = lse_ref[...]
        tt = do * y_ref[...]
        dv = jnp.zeros((SEQ, LANES), F32)
        for h in range(2):
            sel = head0 if h == 0 else ~head0
            lo = h * MLA_V
            cols = slice(h * LANES, (h + 1) * LANES)
            q = q_ref[:, cols]
            k = k_ref[:, cols]
            dd = jnp.sum(jnp.where(sel, tt, 0.0), axis=-1, keepdims=True)
            doh = jnp.where(sel, do, 0.0).astype(BF16)
            p = jnp.exp(_dot_nt(q, k) - lse[:, lo:lo + 1])
            dp = _dot_nt(doh, v)
            ds = (p * (dp - dd)).astype(BF16)
            dq_ref[:, cols] = _dot(ds, k)
            dk_ref[:, cols] += _dot_tn(ds, q)
            dv = dv + _dot_tn(p.astype(BF16), doh)
        dv_ref[...] += dv

    qspec = pl.BlockSpec((MLA_TQ, 2 * LANES), lambda b, hp, i: (b * nq + i, hp))
    kspec = pl.BlockSpec((SEQ, 2 * LANES), lambda b, hp, i: (b, hp))
    vspec = pl.BlockSpec((SEQ, LANES), lambda b, hp, i: (b, hp))
    ospec = pl.BlockSpec((MLA_TQ, LANES), lambda b, hp, i: (b * nq + i, hp))
    return call_hosting_exchange(
        body, xch, grid=(nb, n_pairs, nq),
        in_specs=[qspec, kspec, vspec, ospec, ospec, ospec],
        out_specs=[qspec, kspec, vspec],
        out_shape=[SDS((t, MLA_QW), F32), SDS((t, MLA_QW), F32), SDS((t, MLA_WIDTH), F32)],
        scratch_shapes=[], name="mla_attn_bwd", operands=(qb, kb, vb, dy, y, lse))


MEM_TQ = 512
MEM_SCALE = MEM_HEAD_DIM ** -0.5
MQ_BLK = 5120 // LANES


def mem_attn_fwd(proj, mkv, nb):
    t = proj.shape[0]
    nq = SEQ // MEM_TQ

    def body(q_ref, mk_ref, mv_ref, y_ref):
        s = _dot_nt(q_ref[...].astype(BF16), mk_ref[...]) * MEM_SCALE
        m = jnp.max(s, axis=-1, keepdims=True)
        p = jnp.exp(s - m)
        l = jnp.sum(p, axis=-1, keepdims=True)
        y_ref[...] = _dot(p.astype(BF16), mv_ref[...]) / l

    return pl.pallas_call(
        body, grid=(nb, MEM_HEADS, nq),
        in_specs=[pl.BlockSpec((MEM_TQ, LANES), lambda b, h, i: (b * nq + i, MQ_BLK + h)),
                  pl.BlockSpec((N_MEM, LANES), lambda b, h, i: (b, h)),
                  pl.BlockSpec((N_MEM, LANES), lambda b, h, i: (b, MEM_HEADS + h))],
        out_specs=pl.BlockSpec((MEM_TQ, LANES), lambda b, h, i: (b * nq + i, h)),
        out_shape=SDS((t, MEM_WIDTH), F32),
        name="mem_attn_fwd", compiler_params=_params("parallel", "parallel", "parallel"))(proj, mkv, mkv)


def mem_attn_bwd(proj, mkv, dy, nb):
    t = proj.shape[0]
    nq = SEQ // MEM_TQ

    def body(q_ref, mk_ref, mv_ref, do_ref, dq_ref, dmk_ref, dmv_ref):
        @pl.when(pl.program_id(2) == 0)
        def _():
            dmk_ref[...] = jnp.zeros_like(dmk_ref)
            dmv_ref[...] = jnp.zeros_like(dmv_ref)

        q = q_ref[...].astype(BF16)
        mk, mv = mk_ref[...], mv_ref[...]
        do = do_ref[...].astype(BF16)
        s = _dot_nt(q, mk) * MEM_SCALE
        e = jnp.exp(s - jnp.max(s, axis=-1, keepdims=True))
        p = e / jnp.sum(e, axis=-1, keepdims=True)
        dp = _dot_nt(do, mv)
        ds = (p * (dp - jnp.sum(p * dp, axis=-1, keepdims=True)) * MEM_SCALE).astype(BF16)
        dq_ref[...] = _dot(ds, mk).astype(BF16)
        dmk_ref[...] += _dot_tn(ds, q)
        dmv_ref[...] += _dot_tn(p.astype(BF16), do)

    ospec = pl.BlockSpec((MEM_TQ, LANES), lambda b, h, i: (b * nq + i, h))
    kspec = pl.BlockSpec((N_MEM, LANES), lambda b, h, i: (b, h))
    return pl.pallas_call(
        body, grid=(nb, MEM_HEADS, nq),
        in_specs=[pl.BlockSpec((MEM_TQ, LANES), lambda b, h, i: (b * nq + i, MQ_BLK + h)),
                  kspec, pl.BlockSpec((N_MEM, LANES), lambda b, h, i: (b, MEM_HEADS + h)), ospec],
        out_specs=[ospec, kspec, kspec],
        out_shape=[SDS((t, MEM_WIDTH), BF16), SDS((nb * N_MEM, MEM_WIDTH), F32), SDS((nb * N_MEM, MEM_WIDTH), F32)],
        name="mem_attn_bwd", compiler_params=_params("parallel", "parallel", "arbitrary"))(proj, mkv, mkv, dy)


ROW_TM = 256
AG_BLK = 3072 // 1024
BG_BLK = 4608 // 512
MG_BLK = 5632 // 512
GROUPS = ((0, A_WIDTH), (A_WIDTH, MLA_WIDTH), (A_WIDTH + MLA_WIDTH, MEM_WIDTH))
D_MIX = 2048


def _gate_specs():
    def row(w, j):
        return pl.BlockSpec((ROW_TM, w), lambda i: (i, j))

    def vec(w):
        return pl.BlockSpec((1, w), lambda i: (0, 0))

    ys = [row(A_WIDTH, 0), row(MLA_WIDTH, 0), row(MEM_WIDTH, 0)]
    gates = [row(A_WIDTH, AG_BLK), row(MLA_WIDTH, BG_BLK), row(MEM_WIDTH, MG_BLK)]
    gains = [vec(A_WIDTH), vec(MLA_WIDTH), vec(MEM_WIDTH)]
    return row, vec, ys, gates, gains


def gate_fwd(ya, yb, ym, proj, goa, gob, gom):
    t = ya.shape[0]
    row, vec, ys, gates, gains = _gate_specs()

    def body(ya_ref, yb_ref, ym_ref, ga_ref, gb_ref, gm_ref, goa_ref, gob_ref, gom_ref, z_ref):
        for (off, w), y_ref, g_ref, go_ref in zip(GROUPS, (ya_ref, yb_ref, ym_ref), (ga_ref, gb_ref, gm_ref),
                                                  (goa_ref, gob_ref, gom_ref)):
            n, _ = _rms(y_ref[...], go_ref[...])
            gt = g_ref[...]
            z_ref[:, off:off + w] = (n * (gt * _sigmoid(gt))).astype(BF16)

    return pl.pallas_call(
        body, grid=(t // ROW_TM,), in_specs=ys + gates + gains, out_specs=row(D_MIX, 0),
        out_shape=SDS((t, D_MIX), BF16), name="gate_fwd",
        compiler_params=_params("parallel"))(ya, yb, ym, proj, proj, proj, goa, gob, gom)


def out_ln_loss(z, wout, h32, target, gp, bp):
    t, d = h32.shape

    def body(z_ref, w_ref, h_ref, t_ref, gp_ref, bp_ref, du32_ref, du16_ref, loss_ref, dgp_ref, dbp_ref):
        @pl.when(pl.program_id(0) == 0)
        def _():
            loss_ref[...] = jnp.zeros_like(loss_ref)
            dgp_ref[...] = jnp.zeros_like(dgp_ref)
            dbp_ref[...] = jnp.zeros_like(dbp_ref)

        g = gp_ref[...]
        u = ALPHA * h_ref[...] + _dot(z_ref[...], w_ref[...])
        mu = jnp.mean(u, axis=-1, keepdims=True)
        uc = u - mu
        rstd = lax.rsqrt(jnp.mean(uc * uc, axis=-1, keepdims=True) + NORM_EPS)
        xhat = uc * rstd
        err = xhat * g + bp_ref[...] - t_ref[...]
        tok = jnp.sum(err * err, axis=-1, keepdims=True) * (1.0 / d)
        loss_ref[...] += 0.5 * jnp.sum(tok, axis=0, keepdims=True)
        dout = err * (1.0 / d)
        dxhat = dout * g
        du = rstd * (dxhat - jnp.mean(dxhat, axis=-1, keepdims=True)
                     - xhat * jnp.mean(dxhat * xhat, axis=-1, keepdims=True))
        du32_ref[...] = du
        du16_ref[...] = du.astype(BF16)
        dgp_ref[...] += jnp.sum(dout * xhat, axis=0, keepdims=True)
        dbp_ref[...] += jnp.sum(dout, axis=0, keepdims=True)

    row = pl.BlockSpec((ROW_TM, d), lambda i: (i, 0))
    vec = pl.BlockSpec((1, d), lambda i: (0, 0))
    return pl.pallas_call(
        body, grid=(t // ROW_TM,),
        in_specs=[pl.BlockSpec((ROW_TM, D_MIX), lambda i: (i, 0)), pl.BlockSpec((D_MIX, d), lambda i: (0, 0)), row, row, vec, vec],
        out_specs=[row, row, pl.BlockSpec((1, LANES), lambda i: (0, 0)), vec, vec],
        out_shape=[SDS((t, d), F32), SDS((t, d), BF16), SDS((1, LANES), F32), SDS((1, d), F32), SDS((1, d), F32)],
        name="out_ln_loss", compiler_params=_params("arbitrary"))(z, wout, h32, target, gp, bp)


def gate_bwd(du16, wout, ya, yb, ym, proj, goa, gob, gom):
    t = ya.shape[0]
    row, vec, ys, gates, gains = _gate_specs()

    def body(du_ref, w_ref, ya_ref, yb_ref, ym_ref, ga_ref, gb_ref, gm_ref, goa_ref, gob_ref, gom_ref,
             dya_ref, dyb_ref, dym_ref, dga_ref, dgb_ref, dgm_ref, dgoa_ref, dgob_ref, dgom_ref):
        @pl.when(pl.program_id(0) == 0)
        def _():
            dgoa_ref[...] = jnp.zeros_like(dgoa_ref)
            dgob_ref[...] = jnp.zeros_like(dgob_ref)
            dgom_ref[...] = jnp.zeros_like(dgom_ref)

        dz = _dot_nt(du_ref[...], w_ref[...])
        for (off, w), y_ref, g_ref, go_ref, dy_ref, dg_ref, dgo_ref in zip(
                GROUPS, (ya_ref, yb_ref, ym_ref), (ga_ref, gb_ref, gm_ref), (goa_ref, gob_ref, gom_ref),
                (dya_ref, dyb_ref, dym_ref), (dga_ref, dgb_ref, dgm_ref), (dgoa_ref, dgob_ref, dgom_ref)):
            dzg = dz[:, off:off + w]
            y, gt, go = y_ref[...], g_ref[...], go_ref[...]
            n, r = _rms(y, go)
            sg = _sigmoid(gt)
            dg_ref[...] = (dzg * n * (sg * (1.0 + gt * (1.0 - sg)))).astype(BF16)
            dy, dgo = _rms_bwd(dzg * (gt * sg), y, r, go)
            dy_ref[...] = dy
            dgo_ref[...] += dgo

    widths = (A_WIDTH, MLA_WIDTH, MEM_WIDTH)
    return pl.pallas_call(
        body, grid=(t // ROW_TM,),
        in_specs=[row(D_MODEL, 0), pl.BlockSpec((D_MIX, D_MODEL), lambda i: (0, 0))] + ys + gates + gains,
        out_specs=[row(w, 0) for w in widths] * 2 + [vec(w) for w in widths],
        out_shape=[SDS((t, w), F32) for w in widths] + [SDS((t, w), BF16) for w in widths] + [SDS((1, w), F32) for w in widths],
        name="gate_bwd", compiler_params=_params("arbitrary"))(du16, wout, ya, yb, ym, proj, proj, proj, goa, gob, gom)


def dh_ln_bwd(pieces, win_w, du32, x2, g_emb, xch):
    t, d = x2.shape

    def body(*refs):
        p_refs = refs[:len(pieces)]
        w_ref, du_ref, x_ref, g_ref, dx_ref, dg_ref, db_ref = refs[len(pieces):]

        @pl.when(pl.program_id(0) == 0)
        def _():
            dg_ref[...] = jnp.zeros_like(dg_ref)
            db_ref[...] = jnp.zeros_like(db_ref)

        dh = ALPHA * du_ref[...]
        for p_ref, off, w in zip(p_refs, PIECE_OFFS, PIECE_WIDTHS):
            dh = dh + _dot_nt(p_ref[...], w_ref[:, off:off + w])
        x = x_ref[...]
        xc = x - jnp.mean(x, axis=-1, keepdims=True)
        rstd = lax.rsqrt(jnp.mean(xc * xc, axis=-1, keepdims=True) + NORM_EPS)
        xhat = xc * rstd
        dg_ref[...] += jnp.sum(dh * xhat, axis=0, keepdims=True)
        db_ref[...] += jnp.sum(dh, axis=0, keepdims=True)
        tg = dh * g_ref[...]
        dx_ref[...] = rstd * (tg - jnp.mean(tg, axis=-1, keepdims=True)
                              - xhat * jnp.mean(tg * xhat, axis=-1, keepdims=True))

    row = pl.BlockSpec((ROW_TM, d), lambda i: (i, 0))
    vec = pl.BlockSpec((1, d), lambda i: (0, 0))
    return call_hosting_exchange(
        body, xch, grid=(t // ROW_TM,),
        in_specs=[pl.BlockSpec((ROW_TM, w), lambda i: (i, 0)) for w in PIECE_WIDTHS]
        + [pl.BlockSpec(win_w.shape, lambda i: (0, 0)), row, row, vec],
        out_specs=[row, vec, vec],
        out_shape=[SDS((t, d), F32), SDS((1, d), F32), SDS((1, d), F32)],
        scratch_shapes=[], name="dh_ln_bwd", operands=(*pieces, win_w, du32, x2, g_emb))


def _adamw(w, g, m, v):
    m2 = ADAM_B1 * m + (1.0 - ADAM_B1) * g
    v2 = ADAM_B2 * v + (1.0 - ADAM_B2) * (g * g)
    m_hat = m2 / (1.0 - ADAM_B1 ** ADAM_STEP)
    v_hat = v2 / (1.0 - ADAM_B2 ** ADAM_STEP)
    return -ADAM_LR * (m_hat / (jnp.sqrt(v_hat) + ADAM_EPS) + ADAM_WD * w), m2, v2


def adamw_shard(w, parts, m, v, name):
    r, c = w.shape
    tr = min(r, 256)

    def body(w_ref, p_ref, m_ref, v_ref, g_ref, d_ref, nm_ref, nv_ref):
        g = p_ref[0].astype(F32)
        for k in range(1, N_DEV):
            g = g + p_ref[k].astype(F32)
        g_ref[...] = g
        d_ref[...], nm_ref[...], nv_ref[...] = _adamw(w_ref[...], g, m_ref[...], v_ref[...])

    blk = pl.BlockSpec((tr, c), lambda i: (i, 0))
    return pl.pallas_call(
        body, grid=(r // tr,),
        in_specs=[blk, pl.BlockSpec((N_DEV, tr, c), lambda i: (0, i, 0)), blk, blk],
        out_specs=[blk] * 4, out_shape=[SDS((r, c), F32)] * 4, name=name,
        compiler_params=_params("parallel"))(w, parts, m, v)


def _place():
    return lax.axis_index("x"), lax.axis_index("y"), lax.axis_index("c")


def _flat(px, py, pc):
    return 4 * px + 2 * py + pc


def _peer(x, y, c, k):
    return (1 - x if k & 4 else x, 1 - y if k & 2 else y, 1 - c if k & 1 else c)


def cast_shards(shards):
    def body(*refs):
        n = len(refs) // 2
        for i_ref, o_ref in zip(refs[:n], refs[n:]):
            o_ref[...] = i_ref[...].astype(BF16)

    return pl.pallas_call(body, out_shape=[SDS(s.shape, BF16) for s in shards], name="cast_shards",
                          compiler_params=_params())(*shards)


def allgather_weights(shards):
    n = len(shards)

    def body(*refs):
        ins, outs = refs[:n], refs[n:2 * n]
        send_sems, recv_sems, local_sems = refs[2 * n:]
        x, y, c = _place()
        me, sib = (x, y, c), (x, y, 1 - c)
        chips = [(1 - x, y), (x, 1 - y), (1 - x, 1 - y)]

        def copy(a, k, block, to, src=None):
            dst = outs[a].at[_flat(*block)]
            return pltpu.make_async_remote_copy(
                src_ref=dst if src is None else src, dst_ref=dst,
                send_sem=send_sems.at[a * 7 + k], recv_sem=recv_sems.at[a * 7 + k],
                device_id=to, device_id_type=MESH)

        mine = [pltpu.make_async_copy(ins[a], outs[a].at[_flat(*me)], local_sems.at[a]) for a in range(n)]
        for cp in mine:
            cp.start()
        first = []
        for a in range(n):
            first.append(copy(a, 0, me, sib, src=ins[a]))
            first += [copy(a, 1 + j, me, (*chip, c), src=ins[a]) for j, chip in enumerate(chips)]
        for cp in first:
            cp.start()
        passed = []
        for j, chip in enumerate(chips):
            for a in range(n):
                copy(a, 1 + j, (*chip, c), me).wait_recv()
                fwd = copy(a, 4 + j, (*chip, c), sib)
                fwd.start()
                passed.append(fwd)
        for a in range(n):
            copy(a, 0, sib, me).wait_recv()
            for j, chip in enumerate(chips):
                copy(a, 4 + j, (*chip, 1 - c), me).wait_recv()
        for cp in first + passed:
            cp.wait_send()
        for cp in mine:
            cp.wait()

    hbm = pl.BlockSpec(memory_space=pl.ANY)
    return pl.pallas_call(
        body, out_shape=[SDS((N_DEV,) + s.shape, s.dtype) for s in shards],
        in_specs=[hbm] * n, out_specs=[hbm] * n,
        scratch_shapes=[pltpu.SemaphoreType.DMA((7 * n,)), pltpu.SemaphoreType.DMA((7 * n,)), pltpu.SemaphoreType.DMA((n,))],
        name="allgather_weights", compiler_params=_params())(*shards)


ALL_DEVICES = tuple(range(N_DEV))


def _exchange_plan(src_refs, land_refs, dests, send_sems, recv_sems, local_sems):
    x, y, c = _place()
    me = _flat(x, y, c)
    plan = []
    for a, (src, land, dl) in enumerate(zip(src_refs, land_refs, dests)):
        for li, j in enumerate(dl):
            to = ((j >> 2) & 1, (j >> 1) & 1, j & 1)
            block = src.at[li] if len(src.shape) == len(land.shape) else src

            def push(slot, a=a, block=block, land=land, j=j, to=to):
                return pltpu.make_async_remote_copy(
                    src_ref=block, dst_ref=land.at[slot], send_sem=send_sems.at[a * N_DEV + j],
                    recv_sem=recv_sems.at[a * N_DEV + slot], device_id=to, device_id_type=MESH)

            own = pltpu.make_async_copy(block, land.at[j], local_sems.at[a])
            plan.append((j, push(me), own, [push(s) for s in range(N_DEV) if s != j]))
    return me, plan


def _exchange_start(me, plan):
    for j, send, own, _ in plan:
        @pl.when(me != j)
        def _(send=send):
            send.start()

        @pl.when(me == j)
        def _(own=own):
            own.start()


def _exchange_wait(me, plan):
    for j, send, own, arrivals in plan:
        @pl.when(me != j)
        def _(send=send):
            send.wait_send()

        @pl.when(me == j)
        def _(own=own, arrivals=arrivals):
            own.wait()
            for arrival in arrivals:
                arrival.wait_recv()


def call_hosting_exchange(core, xch, *, grid, in_specs, out_specs, out_shape, scratch_shapes, name, operands):
    srcs, dests, landing = xch
    n, n_in, n_out, n_scr = len(srcs), len(in_specs), len(out_specs), len(scratch_shapes)

    def body(*refs):
        ins, src_refs = refs[:n_in], refs[n_in:n_in + n]
        outs = refs[n_in + 2 * n:n_in + 2 * n + n_out]
        land_refs = refs[n_in + 2 * n + n_out:n_in + 3 * n + n_out]
        scratch = refs[n_in + 3 * n + n_out:n_in + 3 * n + n_out + n_scr]
        sems = refs[n_in + 3 * n + n_out + n_scr:]
        first = functools.reduce(jnp.logical_and, [pl.program_id(i) == 0 for i in range(len(grid))])
        last = functools.reduce(jnp.logical_and, [pl.program_id(i) == grid[i] - 1 for i in range(len(grid))])
        me, plan = _exchange_plan(src_refs, land_refs, dests, *sems)

        @pl.when(first)
        def _():
            _exchange_start(me, plan)

        core(*ins, *outs, *scratch)

        @pl.when(last)
        def _():
            _exchange_wait(me, plan)

    hbm = pl.BlockSpec(memory_space=pl.ANY)
    res = pl.pallas_call(
        body, grid=grid,
        in_specs=list(in_specs) + [hbm] * (2 * n), out_specs=list(out_specs) + [hbm] * n,
        out_shape=list(out_shape) + [SDS(l.shape, l.dtype) for l in landing],
        scratch_shapes=list(scratch_shapes) + [pltpu.SemaphoreType.DMA((N_DEV * n,)), pltpu.SemaphoreType.DMA((N_DEV * n,)),
                                               pltpu.SemaphoreType.DMA((n,))],
        input_output_aliases={n_in + n + k: n_out + k for k in range(n)},
        name=name, compiler_params=_params(*(("arbitrary",) * len(grid))))(*operands, *srcs, *landing)
    return res[:n_out], res[n_out:]


SLOT_ROWS = 8


def small_allreduce_adamw(loss_sum, grads, ws, ms, vs):
    n = len(grads)
    rows = [g.shape[0] for g in grads]
    total = SLOT_ROWS * (n + 1)

    def body(*refs):
        loss_ref, g_refs, w_refs = refs[0], refs[1:1 + n], refs[1 + n:1 + 2 * n]
        m_refs, v_refs = refs[1 + 2 * n:1 + 3 * n], refs[1 + 3 * n:1 + 4 * n]
        outs = refs[1 + 4 * n:2 + 8 * n]
        vec, gath, tot, send_sems, recv_sems = refs[2 + 8 * n:]
        x, y, c = _place()
        me = _flat(x, y, c)
        vec[...] = jnp.zeros_like(vec)
        vec[0:1, :] = loss_ref[...]
        for i in range(n):
            vec[SLOT_ROWS * (i + 1):SLOT_ROWS * (i + 1) + rows[i], :] = g_refs[i][...]
        gath[me] = vec[...]
        copies = []
        for k in range(1, N_DEV):
            peer = _peer(x, y, c, k)
            copies.append(pltpu.make_async_remote_copy(
                src_ref=vec, dst_ref=gath.at[me], send_sem=send_sems.at[k - 1], recv_sem=recv_sems.at[k - 1],
                device_id=peer, device_id_type=MESH))
        for cp in copies:
            cp.start()
        for cp in copies:
            cp.wait_recv()
        for cp in copies:
            cp.wait_send()
        g = gath[0]
        for j in range(1, N_DEV):
            g = g + gath[j]
        tot[...] = g
        outs[0][...] = tot[0:1, :]
        for i in range(n):
            gi = tot[SLOT_ROWS * (i + 1):SLOT_ROWS * (i + 1) + rows[i], :]
            outs[1 + i][...] = gi
            outs[1 + n + i][...], outs[1 + 2 * n + i][...], outs[1 + 3 * n + i][...] = _adamw(
                w_refs[i][...], gi, m_refs[i][...], v_refs[i][...])

    shapes = [SDS(g.shape, F32) for g in grads]
    return pl.pallas_call(
        body, out_shape=[SDS((1, LANES), F32)] + shapes * 4,
        scratch_shapes=[pltpu.VMEM((total, LANES), F32), pltpu.VMEM((N_DEV, total, LANES), F32), pltpu.VMEM((total, LANES), F32),
                        pltpu.SemaphoreType.DMA((7,)), pltpu.SemaphoreType.DMA((7,))],
        name="small_allreduce_adamw", compiler_params=_params())(loss_sum, *grads, *ws, *ms, *vs)


def _rope_tables(positions):
    pos = positions.astype(F32).reshape(-1, 1)

    def cs(r):
        inv_freq = ROPE_THETA ** (-(jnp.arange(0, r, 2, dtype=F32) / r))
        ang = pos * inv_freq
        return jnp.cos(ang), jnp.sin(ang)

    n = pos.shape[0]
    one = lambda w: jnp.ones((n, w), F32)
    zero = lambda w: jnp.zeros((n, w), F32)
    ca, sa = cs(A_ROT)
    rest = A_HEAD_DIM - A_ROT
    a_c = jnp.tile(jnp.concatenate([ca, ca, one(rest)], 1), (1, 2))
    a_sa = jnp.tile(jnp.concatenate([-sa, zero(A_ROT // 2 + rest)], 1), (1, 2))
    a_sb = jnp.tile(jnp.concatenate([zero(A_ROT // 2), sa, zero(rest)], 1), (1, 2))
    cm, sm = cs(MLA_ROPE)
    tail = LANES - MLA_NOPE - MLA_ROPE
    m_c = jnp.concatenate([one(MLA_NOPE), cm, cm, one(tail)], 1)
    m_sa = jnp.concatenate([zero(MLA_NOPE), -sm, zero(MLA_ROPE // 2 + tail)], 1)
    m_sb = jnp.concatenate([zero(MLA_NOPE + MLA_ROPE // 2), sm, zero(tail)], 1)
    return (a_c, a_sa, a_sb), (m_c, m_sa, m_sb)


KR_LO, KR_HI = 4480, 4512


def _w_in_working(g):
    w = g.transpose(1, 0, 2).reshape(D_MODEL, D_IN)
    z = lambda n: jnp.zeros((D_MODEL, n), w.dtype)
    return jnp.concatenate([w[:, :KR_LO], z(MLA_NOPE), w[:, KR_LO:KR_HI], z(LANES - MLA_NOPE - MLA_ROPE), w[:, KR_HI:]], 1)


W_IN_SHARD = D_IN // N_DEV
AG_SPLIT = 5 * W_IN_SHARD - 3 * A_WIDTH
BG_SPLIT = 6 * W_IN_SHARD - KR_HI


def _shards(cols):
    return cols.reshape(D_MODEL, -1, W_IN_SHARD).transpose(1, 0, 2).astype(BF16)


def _w_in_shards_0_4(d_aq, d_ak, d_av, d_ag):
    return _shards(jnp.concatenate([d_aq, d_ak, d_av, d_ag[:, :AG_SPLIT]], 1))


def _w_in_shard_5(d_ag, d_cc, d_bg):
    kr = MLA_Q_RANK + MLA_KV_RANK + MLA_NOPE
    return _shards(jnp.concatenate([d_ag[:, AG_SPLIT:], d_cc[:, :MLA_Q_RANK + MLA_KV_RANK], d_cc[:, kr:kr + MLA_ROPE],
                                    d_bg[:, :BG_SPLIT]], 1))


def _w_in_shards_6_7(d_bg, d_mq, d_mg):
    return _shards(jnp.concatenate([d_bg[:, BG_SPLIT:], d_mq, d_mg], 1))


def _w_uq_working(g):
    w = jnp.pad(g.transpose(1, 0, 2), ((0, 0), (0, 0), (0, LANES - MLA_NOPE - MLA_ROPE)))
    return w.reshape(MLA_Q_RANK, MLA_QW)


def _w_uq_parts(dw):
    return dw.reshape(MLA_Q_RANK, MLA_HEADS, LANES)[:, :, :MLA_NOPE + MLA_ROPE].transpose(1, 0, 2)


def _w_ukv_working(g):
    wk = jnp.pad(g[:, :, :MLA_NOPE].transpose(1, 0, 2), ((0, 0), (0, 0), (0, LANES - MLA_NOPE)))
    wv = g[:, :, MLA_NOPE:].transpose(1, 0, 2)
    return jnp.concatenate([wk.reshape(MLA_KV_RANK, MLA_QW), wv.reshape(MLA_KV_RANK, MLA_WIDTH)], 1)


def _w_ukv_parts(dw):
    dk = dw[:, :MLA_QW].reshape(MLA_KV_RANK, MLA_HEADS, LANES)[:, :, :MLA_NOPE]
    dv = dw[:, MLA_QW:].reshape(MLA_KV_RANK, MLA_HEADS, MLA_V)
    return jnp.concatenate([dk, dv], -1).transpose(1, 0, 2)


SMALL_NAMES = ("g_emb", "b_emb", "g_cq", "g_ckv", "g_out_a", "g_out_b", "g_out_m", "g_post", "b_post")


def kernel(x, mem, positions, g_emb, b_emb, w_in, g_cq, g_ckv, w_uq, w_ukv, w_mem_kv, g_out_a, g_out_b, g_out_m, w_out, g_post, b_post, loss_target, m_g_emb, m_b_emb, m_w_in, m_g_cq, m_g_ckv, m_w_uq, m_w_ukv, m_w_mem_kv, m_g_out_a, m_g_out_b, m_g_out_m, m_w_out, m_g_post, m_b_post, v_g_emb, v_b_emb, v_w_in, v_g_cq, v_g_ckv, v_w_uq, v_w_ukv, v_w_mem_kv, v_g_out_a, v_g_out_b, v_g_out_m, v_w_out, v_g_post, v_b_post):
    nb = x.shape[0]
    t = nb * SEQ
    x2 = x.reshape(t, D_MODEL)
    tgt2 = loss_target.reshape(t, D_MODEL)
    mem2 = mem.reshape(nb * N_MEM, D_MODEL)
    g_emb2, b_emb2 = g_emb.reshape(1, -1), b_emb.reshape(1, -1)
    (a_c, a_sa, a_sb), (m_c, m_sa, m_sb) = _rope_tables(positions)

    s_in, s_uq, s_ukv, s_mem, s_out = cast_shards((w_in[0], w_uq[0], w_ukv[0], w_mem_kv[0], w_out[0]))
    (g_in,) = allgather_weights((s_in,))
    win_w = _w_in_working(g_in)

    h32, h16 = ln_emb_fwd(x2, g_emb2, b_emb2)
    proj = mm_nn(h16, win_w, F32, 512, 1536, "proj")
    later = (s_uq, s_ukv, s_mem, s_out)
    (ya, lse_a), (g_uq, g_ukv, g_mem, g_out) = a_attn_fwd(
        proj, a_c, a_sa, a_sb, nb,
        (later, (ALL_DEVICES,) * len(later), tuple(lax.empty((N_DEV,) + w.shape, BF16) for w in later)))
    wuq_w = _w_uq_working(g_uq)
    wkv_w = _w_ukv_working(g_ukv)
    wmem = g_mem.reshape(D_MODEL, 2 * MEM_WIDTH)
    wout = g_out.reshape(D_MIX, D_MODEL)
    qb, kb, vb = mla_prep_fwd(proj, m_c, m_sa, m_sb, g_cq, g_ckv, wuq_w, wkv_w)
    yb, lse_b = mla_attn_fwd(qb, kb, vb, nb)
    mkv = mm_nn(mem2, wmem, BF16, nb * N_MEM, 512, "mem_kv")
    ym = mem_attn_fwd(proj, mkv, nb)
    z = gate_fwd(ya, yb, ym, proj, g_out_a, g_out_b, g_out_m)
    du32, du16, loss_sum, dg_post, db_post = out_ln_loss(z, wout, h32, tgt2, g_post, b_post)

    dya, dyb, dym, dag, dbg, dmg, dg_out_a, dg_out_b, dg_out_m = gate_bwd(
        du16, wout, ya, yb, ym, proj, g_out_a, g_out_b, g_out_m)
    dw_out = mm_tn(z, du16, 512, "dw_out")
    dmq, dmk, dmv = mem_attn_bwd(proj, mkv, dym, nb)
    dw_mem = mm_tn(mem2, jnp.concatenate([dmk, dmv], 1), nb * N_MEM, "dw_mem")
    d_ag, d_bg, d_mq, d_mg = [mm_tn(h16, p, 512, "dw_in_" + n) for n, p in (("ag", dag), ("bg", dbg), ("mq", dmq), ("mg", dmg))]
    landing = lambda w, dtype=F32: lax.empty((N_DEV,) + w.shape, dtype)
    big_w = (w_in[0], w_uq[0], w_ukv[0], w_mem_kv[0], w_out[0])
    (daq, dak, dav), (p_out, p_mem, p_in) = a_attn_bwd(
        proj, a_c, a_sa, a_sb, dya, ya, lse_a, nb,
        ((dw_out.reshape(N_DEV, D_MIX // N_DEV, D_MODEL), dw_mem.reshape(N_DEV, D_MODEL // N_DEV, 2 * MEM_WIDTH),
          _w_in_shards_6_7(d_bg, d_mq, d_mg)),
         (ALL_DEVICES, ALL_DEVICES, (6, 7)),
         (landing(w_out[0]), landing(w_mem_kv[0]), landing(w_in[0], BF16))))
    d_aq, d_ak, d_av = [mm_tn(h16, p, 512, "dw_in_" + n) for n, p in (("aq", daq), ("ak", dak), ("av", dav))]
    (dqb, dkb, dvb), (p_in,) = mla_attn_bwd(
        qb, kb, vb, dyb, yb, lse_b, nb, ((_w_in_shards_0_4(d_aq, d_ak, d_av, d_ag),), ((0, 1, 2, 3, 4),), (p_in,)))
    dcc, dqf, cqn, dkvf, ckvn, dg_cq, dg_ckv = mla_prep_bwd(proj, m_c, m_sa, m_sb, g_cq, g_ckv, wuq_w, wkv_w, dqb, dkb, dvb)
    dw_uq = mm_tn(cqn, dqf, 512, "dw_uq")
    dw_ukv = mm_tn(ckvn, dkvf, 512, "dw_ukv")
    d_cc = mm_tn(h16, dcc, 512, "dw_in_cc")
    pieces = (daq, dak, dav, dag, dcc, dbg, dmq, dmg)
    (grad_x, dg_emb, db_emb), (p_in, p_uq, p_ukv) = dh_ln_bwd(
        pieces, win_w, du32, x2, g_emb2,
        ((_w_in_shard_5(d_ag, d_cc, d_bg), _w_uq_parts(dw_uq), _w_ukv_parts(dw_ukv)),
         ((5,), ALL_DEVICES, ALL_DEVICES),
         (p_in, landing(w_uq[0]), landing(w_ukv[0]))))

    parts = (p_in, p_uq, p_ukv, p_mem, p_out)
    big_m = (m_w_in[0], m_w_uq[0], m_w_ukv[0], m_w_mem_kv[0], m_w_out[0])
    big_v = (v_w_in[0], v_w_uq[0], v_w_ukv[0], v_w_mem_kv[0], v_w_out[0])
    big = {}
    for name, w, p, m, v in zip(("w_in", "w_uq", "w_ukv", "w_mem_kv", "w_out"), big_w, parts, big_m, big_v):
        big[name] = [o[None] for o in adamw_shard(w, p, m, v, "adamw_" + name)]

    small_w = (g_emb, b_emb, g_cq, g_ckv, g_out_a, g_out_b, g_out_m, g_post, b_post)
    small_m = (m_g_emb, m_b_emb, m_g_cq, m_g_ckv, m_g_out_a, m_g_out_b, m_g_out_m, m_g_post, m_b_post)
    small_v = (v_g_emb, v_b_emb, v_g_cq, v_g_ckv, v_g_out_a, v_g_out_b, v_g_out_m, v_g_post, v_b_post)
    small_g = (dg_emb, db_emb, dg_cq, dg_ckv, dg_out_a, dg_out_b, dg_out_m, dg_post, db_post)
    rows128 = lambda vals: [v.reshape(-1, LANES) for v in vals]
    res = small_allreduce_adamw(loss_sum, rows128(small_g), rows128(small_w), rows128(small_m), rows128(small_v))
    loss = res[0][0, 0]
    n_small = len(small_w)
    sg, sd, sm, sv = [[r.reshape(w.shape) for r, w in zip(res[1 + k * n_small:1 + (k + 1) * n_small], small_w)]
                      for k in range(4)]

    order = ("g_emb", "b_emb", "w_in", "g_cq", "g_ckv", "w_uq", "w_ukv", "w_mem_kv", "g_out_a", "g_out_b", "g_out_m",
             "w_out", "g_post", "b_post")
    small_idx = {n: i for i, n in enumerate(SMALL_NAMES)}
    outs = [loss, grad_x.reshape(x.shape)]
    for kind in range(4):
        for name in order:
            outs.append(big[name][kind] if name in big else (sg, sd, sm, sv)[kind][small_idx[name]])
    return tuple(outs)
```

```python
import functools

import jax
import jax.numpy as jnp
from jax import lax
from jax.experimental import pallas as pl
from jax.experimental.pallas import tpu as pltpu

F32 = jnp.float32
BF16 = jnp.bfloat16
SDS = jax.ShapeDtypeStruct
MESH = pl.DeviceIdType.MESH

D_MODEL = 1024
SEQ = 2048
A_HEADS, A_HEAD_DIM, A_ROT = 16, 64, 16
A_WIDTH = 1024
DILATIONS = (1, 4, 16)
N_SIDE = 64
MLA_HEADS, MLA_Q_RANK, MLA_KV_RANK = 8, 256, 128
MLA_NOPE, MLA_ROPE, MLA_V = 64, 32, 64
MLA_WIDTH = 512
N_MEM, MEM_HEADS, MEM_HEAD_DIM, MEM_WIDTH = 256, 4, 128, 512
ROPE_THETA = 500000.0
NORM_EPS = 1e-5
NEG_INF = -1e30
ALPHA = 2.0 ** 0.25
D_IN = 6048
N_DEV = 8

ADAM_LR, ADAM_B1, ADAM_B2, ADAM_EPS, ADAM_WD, ADAM_STEP = 0.001, 0.9, 0.999, 1e-08, 0.01, 10

D_INW = 6144
PIECE_WIDTHS = (1024, 1024, 1024, 1024, 512, 512, 512, 512)
PIECE_OFFS = (0, 1024, 2048, 3072, 4096, 4608, 5120, 5632)
LANES = 128
VMEM_LIMIT = 56 * 1024 * 1024


def _params(*sem):
    kw = dict(vmem_limit_bytes=VMEM_LIMIT)
    if sem:
        kw["dimension_semantics"] = sem
    return pltpu.CompilerParams(**kw)


def _dot(a, b):
    return jnp.dot(a, b, preferred_element_type=F32)


def _dot_nt(a, b):
    return lax.dot_general(a, b, (((1,), (1,)), ((), ())), preferred_element_type=F32)


def _dot_tn(a, b):
    return lax.dot_general(a, b, (((0,), (0,)), ((), ())), preferred_element_type=F32)


def _sigmoid(x):
    return 1.0 / (1.0 + jnp.exp(-x))


def _rope_fwd(x, c, sa, sb, half):
    n = x.shape[-1]
    return x * c + pltpu.roll(x, n - half, 1) * sa + pltpu.roll(x, half, 1) * sb


def _rope_bwd(dy, c, sa, sb, half):
    n = dy.shape[-1]
    return dy * c + pltpu.roll(dy * sa, half, 1) + pltpu.roll(dy * sb, n - half, 1)


def mm_nn(a, b, out_dtype, tm, tn, name, rhs_transposed=False):
    m, k = a.shape
    n = b.shape[0] if rhs_transposed else b.shape[1]
    dot = _dot_nt if rhs_transposed else _dot

    def body(a_ref, b_ref, o_ref):
        o_ref[...] = dot(a_ref[...].astype(BF16), b_ref[...].astype(BF16)).astype(o_ref.dtype)

    b_spec = pl.BlockSpec((tn, k), lambda j, i: (j, 0)) if rhs_transposed else pl.BlockSpec((k, tn), lambda j, i: (0, j))
    return pl.pallas_call(
        body, grid=(n // tn, m // tm),
        in_specs=[pl.BlockSpec((tm, k), lambda j, i: (i, 0)), b_spec],
        out_specs=pl.BlockSpec((tm, tn), lambda j, i: (i, j)),
        out_shape=SDS((m, n), out_dtype), name=name,
        compiler_params=_params("parallel", "parallel"))(a, b)


def mm_tn(a, b, tt, name):
    t, m = a.shape
    n = b.shape[1]

    def body(a_ref, b_ref, o_ref):
        @pl.when(pl.program_id(0) == 0)
        def _():
            o_ref[...] = jnp.zeros_like(o_ref)

        o_ref[...] += _dot_tn(a_ref[...].astype(BF16), b_ref[...].astype(BF16))

    return pl.pallas_call(
        body, grid=(t // tt,),
        in_specs=[pl.BlockSpec((tt, m), lambda i: (i, 0)), pl.BlockSpec((tt, n), lambda i: (i, 0))],
        out_specs=pl.BlockSpec((m, n), lambda i: (0, 0)),
        out_shape=SDS((m, n), F32), name=name,
        compiler_params=_params("arbitrary"))(a, b)


def ln_emb_fwd(x2, g, b):
    t, d = x2.shape
    tm = 512

    def body(x_ref, g_ref, b_ref, h32_ref, h16_ref):
        x = x_ref[...]
        mu = jnp.mean(x, axis=-1, keepdims=True)
        xc = x - mu
        var = jnp.mean(xc * xc, axis=-1, keepdims=True)
        h = xc * lax.rsqrt(var + NORM_EPS) * g_ref[...] + b_ref[...]
        h32_ref[...] = h
        h16_ref[...] = h.astype(BF16)

    row = pl.BlockSpec((tm, d), lambda i: (i, 0))
    vec = pl.BlockSpec((1, d), lambda i: (0, 0))
    return pl.pallas_call(
        body, grid=(t // tm,), in_specs=[row, vec, vec], out_specs=[row, row],
        out_shape=[SDS((t, d), F32), SDS((t, d), BF16)], name="ln_emb_fwd",
        compiler_params=_params("parallel"))(x2, g, b)


Q_BLK = 128
UNROLL_FWD = 8
UNROLL_BWD = 4


def _pattern_geometry(d):
    length = SEQ // d
    nblk = length // Q_BLK
    kwin = min(2 * Q_BLK, length)
    return length, nblk, kwin


def _block_coords(idx, d):
    length, nblk, kwin = _pattern_geometry(d)
    r = lax.shift_right_logical(idx, nblk.bit_length() - 1)
    i = idx & (nblk - 1)
    q0 = pl.multiple_of(r * length + i * Q_BLK, Q_BLK)
    ks = jnp.clip(i * Q_BLK - N_SIDE, 0, length - kwin)
    k0 = pl.multiple_of(r * length + ks, N_SIDE)
    qpos = i * Q_BLK + lax.broadcasted_iota(jnp.int32, (Q_BLK, kwin), 0)
    kpos = ks + lax.broadcasted_iota(jnp.int32, (Q_BLK, kwin), 1)
    valid = jnp.abs(kpos - qpos) <= N_SIDE
    return q0, k0, kwin, valid


def _deinterleave(src_ref, dst_ref, d, dtype, tmp_ref):
    if d == 1:
        dst_ref[...] = src_ref[...].astype(dtype)
        return
    q = SEQ // 4
    if d == 4:
        for r in range(4):
            dst_ref[r * q:(r + 1) * q, :] = src_ref[pl.ds(r, q, stride=4), :].astype(dtype)
        return
    assert d == 16
    n = SEQ // 16
    for r in range(4):
        tmp_ref[r * q:(r + 1) * q, :] = src_ref[pl.ds(r, q, stride=4), :]
    for r in range(4):
        for j in range(4):
            dst_ref[(r + 4 * j) * n:(r + 4 * j + 1) * n, :] = tmp_ref[pl.ds(r * q + j, n, stride=4), :].astype(dtype)


def _interleave(src_ref, dst_ref, d, tmp_ref, accumulate):
    q = SEQ // 4
    if d == 16:
        n = SEQ // 16
        for r in range(4):
            for j in range(4):
                tmp_ref[pl.ds(r * q + j, n, stride=4), :] = src_ref[(r + 4 * j) * n:(r + 4 * j + 1) * n, :]
        src_ref = tmp_ref
    else:
        assert d == 4
    for r in range(4):
        rows = pl.ds(r, q, stride=4)
        val = src_ref[r * q:(r + 1) * q, :]
        dst_ref[rows, :] = dst_ref[rows, :] + val if accumulate else val


def a_attn_fwd(proj, ca, sa, sb, nb, xch):
    t = proj.shape[0]
    n_pairs = A_WIDTH // LANES

    def body(q_ref, k_ref, v_ref, c_ref, sa_ref, sb_ref, y_ref, lse_ref,
             qr_s, kr_s, qd_s, kd_s, vd_s, oc_s, lc_s, o1_s, l1_s, o2_s, l2_s, o3_s, l3_s, tmp_s):
        c, s_a, s_b = c_ref[...], sa_ref[...], sb_ref[...]
        qr_s[...] = _rope_fwd(q_ref[...], c, s_a, s_b, A_ROT // 2) * (A_HEAD_DIM ** -0.5)
        kr_s[...] = _rope_fwd(k_ref[...], c, s_a, s_b, A_ROT // 2)
        head0 = lax.broadcasted_iota(jnp.int32, (Q_BLK, LANES), 1) < A_HEAD_DIM
        nat = ((o1_s, l1_s), (o2_s, l2_s), (o3_s, l3_s))

        for g, d in enumerate(DILATIONS):
            _deinterleave(qr_s, qd_s, d, BF16, tmp_s)
            _deinterleave(kr_s, kd_s, d, BF16, tmp_s)
            _deinterleave(v_ref, vd_s, d, BF16, tmp_s)
            o_dst, l_dst = (nat[g] if d == 1 else (oc_s, lc_s))

            def block(idx, carry, d=d, o_dst=o_dst, l_dst=l_dst):
                q0, k0, kwin, valid = _block_coords(idx, d)
                qb = qd_s[pl.ds(q0, Q_BLK), :]
                kb = kd_s[pl.ds(k0, kwin), :]
                vb = vd_s[pl.ds(k0, kwin), :]
                zero = jnp.zeros_like(qb)
                q2 = jnp.concatenate([jnp.where(head0, qb, zero), jnp.where(head0, zero, qb)], 0)
                s = jnp.where(jnp.concatenate([valid, valid], 0), _dot_nt(q2, kb), NEG_INF)
                m = jnp.max(s, axis=-1, keepdims=True)
                p = jnp.exp(s - m)
                l = jnp.sum(p, axis=-1, keepdims=True)
                o2 = _dot(p.astype(BF16), vb) / l
                l2 = m + jnp.log(l)
                o_dst[pl.ds(q0, Q_BLK), :] = jnp.where(head0, o2[:Q_BLK], o2[Q_BLK:])
                l_dst[pl.ds(q0, Q_BLK), :] = jnp.where(head0, l2[:Q_BLK], l2[Q_BLK:])
                return carry

            lax.fori_loop(0, SEQ // Q_BLK, block, 0, unroll=UNROLL_FWD)
            if d > 1:
                _interleave(oc_s, nat[g][0], d, tmp_s, False)
                _interleave(lc_s, nat[g][1], d, tmp_s, False)

        def merge(ci, carry):
            rows = pl.ds(pl.multiple_of(ci * 256, 256), 256)
            l1, l2, l3 = l1_s[rows, :], l2_s[rows, :], l3_s[rows, :]
            m = jnp.maximum(jnp.maximum(l1, l2), l3)
            w1, w2, w3 = jnp.exp(l1 - m), jnp.exp(l2 - m), jnp.exp(l3 - m)
            w = w1 + w2 + w3
            y_ref[rows, :] = (w1 * o1_s[rows, :] + w2 * o2_s[rows, :] + w3 * o3_s[rows, :]) / w
            lse_ref[rows, :] = m + jnp.log(w)
            return carry

        lax.fori_loop(0, SEQ // 256, merge, 0)

    def col(off):
        return pl.BlockSpec((SEQ, LANES), lambda b, hp: (b, off + hp))

    tab = pl.BlockSpec((SEQ, LANES), lambda b, hp: (b, 0))
    out = pl.BlockSpec((SEQ, LANES), lambda b, hp: (b, hp))
    f32s = pltpu.VMEM((SEQ, LANES), F32)
    b16s = pltpu.VMEM((SEQ, LANES), BF16)
    return call_hosting_exchange(
        body, xch, grid=(nb, n_pairs),
        in_specs=[col(0), col(n_pairs), col(2 * n_pairs), tab, tab, tab],
        out_specs=[out, out],
        out_shape=[SDS((t, A_WIDTH), F32), SDS((t, A_WIDTH), F32)],
        scratch_shapes=[f32s, f32s, b16s, b16s, b16s] + [f32s] * 9,
        name="a_attn_fwd", operands=(proj, proj, proj, ca, sa, sb))


def a_attn_bwd(proj, ca, sa, sb, dy, y, lse, nb, xch):
    t = proj.shape[0]
    n_pairs = A_WIDTH // LANES

    def body(q_ref, k_ref, v_ref, c_ref, sa_ref, sb_ref, do_ref, y_ref, lse_ref, dq_ref, dk_ref, dv_ref,
             qr_s, kr_s, l0n_s, l1n_s, d0n_s, d1n_s, qd_s, kd_s, vd_s, dod_s, l0d_s, l1d_s, d0d_s, d1d_s,
             dqc_s, dkc_s, dvc_s, dqn_s, dkn_s, dvn_s, tmp_s):
        c, s_a, s_b = c_ref[...], sa_ref[...], sb_ref[...]
        qr_s[...] = _rope_fwd(q_ref[...], c, s_a, s_b, A_ROT // 2) * (A_HEAD_DIM ** -0.5)
        kr_s[...] = _rope_fwd(k_ref[...], c, s_a, s_b, A_ROT // 2)
        head0 = lax.broadcasted_iota(jnp.int32, (Q_BLK, LANES), 1) < A_HEAD_DIM

        def per_head_rows(ci, carry):
            rows = pl.ds(pl.multiple_of(ci * 256, 256), 256)
            h0 = lax.broadcasted_iota(jnp.int32, (256, LANES), 1) < A_HEAD_DIM
            tt = do_ref[rows, :] * y_ref[rows, :]
            d0n_s[rows, :] = jnp.broadcast_to(jnp.sum(jnp.where(h0, tt, 0.0), axis=-1, keepdims=True), (256, LANES))
            d1n_s[rows, :] = jnp.broadcast_to(jnp.sum(jnp.where(h0, 0.0, tt), axis=-1, keepdims=True), (256, LANES))
            l = lse_ref[rows, :]
            lr = pltpu.roll(l, A_HEAD_DIM, 1)
            l0n_s[rows, :] = jnp.where(h0, l, lr)
            l1n_s[rows, :] = jnp.where(h0, lr, l)
            return carry

        lax.fori_loop(0, SEQ // 256, per_head_rows, 0)
        assert DILATIONS[0] == 1

        for d in DILATIONS:
            _deinterleave(qr_s, qd_s, d, BF16, tmp_s)
            _deinterleave(kr_s, kd_s, d, BF16, tmp_s)
            _deinterleave(v_ref, vd_s, d, BF16, tmp_s)
            _deinterleave(do_ref, dod_s, d, BF16, tmp_s)
            if d > 1:
                for src, dst in ((l0n_s, l0d_s), (l1n_s, l1d_s), (d0n_s, d0d_s), (d1n_s, d1d_s)):
                    _deinterleave(src, dst, d, F32, tmp_s)
            l0, l1, d0, d1 = (l0n_s, l1n_s, d0n_s, d1n_s) if d == 1 else (l0d_s, l1d_s, d0d_s, d1d_s)
            dq_dst, dk_dst, dv_dst = (dqn_s, dkn_s, dvn_s) if d == 1 else (dqc_s, dkc_s, dvc_s)
            dk_dst[...] = jnp.zeros_like(dk_dst)
            dv_dst[...] = jnp.zeros_like(dv_dst)

            def block(idx, carry, d=d, l0=l0, l1=l1, d0=d0, d1=d1, dq_dst=dq_dst, dk_dst=dk_dst, dv_dst=dv_dst):
                q0, k0, kwin, valid = _block_coords(idx, d)
                qrows = pl.ds(q0, Q_BLK)
                krows = pl.ds(k0, kwin)
                qb, dob = qd_s[qrows, :], dod_s[qrows, :]
                kb, vb = kd_s[krows, :], vd_s[krows, :]
                zero = jnp.zeros_like(qb)
                q2 = jnp.concatenate([jnp.where(head0, qb, zero), jnp.where(head0, zero, qb)], 0)
                do2 = jnp.concatenate([jnp.where(head0, dob, zero), jnp.where(head0, zero, dob)], 0)
                wide = lambda x: jnp.concatenate([x] * (kwin // LANES), 1)
                lse2 = wide(jnp.concatenate([l0[qrows, :], l1[qrows, :]], 0))
                dd2 = wide(jnp.concatenate([d0[qrows, :], d1[qrows, :]], 0))
                s = jnp.where(jnp.concatenate([valid, valid], 0), _dot_nt(q2, kb), NEG_INF)
                p = jnp.exp(s - lse2)
                ds = (p * (_dot_nt(do2, vb) - dd2)).astype(BF16)
                dq2 = _dot(ds, kb)
                dq_dst[qrows, :] = jnp.where(head0, dq2[:Q_BLK], dq2[Q_BLK:])
                dk_dst[krows, :] += _dot_tn(ds, q2)
                dv_dst[krows, :] += _dot_tn(p.astype(BF16), do2)
                return carry

            lax.fori_loop(0, SEQ // Q_BLK, block, 0, unroll=UNROLL_BWD)
            if d > 1:
                _interleave(dqc_s, dqn_s, d, tmp_s, True)
                _interleave(dkc_s, dkn_s, d, tmp_s, True)
                _interleave(dvc_s, dvn_s, d, tmp_s, True)

        dq_ref[...] = _rope_bwd(dqn_s[...] * (A_HEAD_DIM ** -0.5), c, s_a, s_b, A_ROT // 2).astype(BF16)
        dk_ref[...] = _rope_bwd(dkn_s[...], c, s_a, s_b, A_ROT // 2).astype(BF16)
        dv_ref[...] = dvn_s[...].astype(BF16)

    def col(off):
        return pl.BlockSpec((SEQ, LANES), lambda b, hp: (b, off + hp))

    tab = pl.BlockSpec((SEQ, LANES), lambda b, hp: (b, 0))
    blk = pl.BlockSpec((SEQ, LANES), lambda b, hp: (b, hp))
    f32s = pltpu.VMEM((SEQ, LANES), F32)
    b16s = pltpu.VMEM((SEQ, LANES), BF16)
    return call_hosting_exchange(
        body, xch, grid=(nb, n_pairs),
        in_specs=[col(0), col(n_pairs), col(2 * n_pairs), tab, tab, tab, blk, blk, blk],
        out_specs=[blk, blk, blk],
        out_shape=[SDS((t, A_WIDTH), BF16)] * 3,
        scratch_shapes=[f32s] * 6 + [b16s] * 4 + [f32s] * 11,
        name="a_attn_bwd", operands=(proj, proj, proj, ca, sa, sb, dy, y, lse))


MLA_SCALE = (MLA_NOPE + MLA_ROPE) ** -0.5
MLA_QW = MLA_HEADS * LANES
MLA_KVW = MLA_QW + MLA_WIDTH


def _rms(x, g):
    r = lax.rsqrt(jnp.mean(x * x, axis=-1, keepdims=True) + NORM_EPS)
    return x * r * g, r


def _rms_bwd(dn, x, r, g):
    tg = dn * g
    dx = r * tg - x * (r * r * r) * jnp.mean(tg * x, axis=-1, keepdims=True)
    return dx, jnp.sum(dn * x * r, axis=0, keepdims=True)


def mla_prep_fwd(proj, cm, sma, smb, g_cq, g_ckv, wuq, wkv):
    t = proj.shape[0]
    tm = 512

    def body(cq_ref, ckv_ref, kr_ref, c_ref, sa_ref, sb_ref, gq_ref, gkv_ref, wuq_ref, wkv_ref, q_ref, k_ref, v_ref):
        c, s_a, s_b = c_ref[...], sa_ref[...], sb_ref[...]
        cqn, _ = _rms(cq_ref[...], gq_ref[...])
        qf = _dot(cqn.astype(BF16), wuq_ref[...])
        ckvn, _ = _rms(ckv_ref[...], gkv_ref[...])
        kvf = _dot(ckvn.astype(BF16), wkv_ref[...])
        krope = _rope_fwd(kr_ref[...], c, s_a, s_b, MLA_ROPE // 2)
        for h in range(MLA_HEADS):
            cols = slice(h * LANES, (h + 1) * LANES)
            q_ref[:, cols] = (_rope_fwd(qf[:, cols], c, s_a, s_b, MLA_ROPE // 2) * MLA_SCALE).astype(BF16)
            k_ref[:, cols] = (kvf[:, cols] + krope).astype(BF16)
        v_ref[...] = kvf[:, MLA_QW:].astype(BF16)

    def row(w, j):
        return pl.BlockSpec((tm, w), lambda i: (i, j))

    def full(a):
        return pl.BlockSpec(a.shape, lambda i: (0, 0))

    return pl.pallas_call(
        body, grid=(t // tm,),
        in_specs=[row(256, 4096 // 256), row(128, 4352 // 128), row(128, 4480 // 128), row(128, 0), row(128, 0), row(128, 0),
                  full(g_cq), full(g_ckv), full(wuq), full(wkv)],
        out_specs=[row(MLA_QW, 0), row(MLA_QW, 0), row(MLA_WIDTH, 0)],
        out_shape=[SDS((t, MLA_QW), BF16), SDS((t, MLA_QW), BF16), SDS((t, MLA_WIDTH), BF16)],
        name="mla_prep_fwd", compiler_params=_params("parallel"))(proj, proj, proj, cm, sma, smb, g_cq, g_ckv, wuq, wkv)


def mla_prep_bwd(proj, cm, sma, smb, g_cq, g_ckv, wuq, wkv, dq, dk, dv):
    t = proj.shape[0]
    tm = 512

    def body(cq_ref, ckv_ref, c_ref, sa_ref, sb_ref, gq_ref, gkv_ref, wuq_ref, wkv_ref, dq_ref, dk_ref, dv_ref,
             dcc_ref, dqf_ref, cqn_ref, dkvf_ref, ckvn_ref, dgq_ref, dgkv_ref):
        @pl.when(pl.program_id(0) == 0)
        def _():
            dgq_ref[...] = jnp.zeros_like(dgq_ref)
            dgkv_ref[...] = jnp.zeros_like(dgkv_ref)

        c, s_a, s_b = c_ref[...], sa_ref[...], sb_ref[...]
        cq, ckv = cq_ref[...], ckv_ref[...]
        cqn, rq = _rms(cq, gq_ref[...])
        ckvn, rkv = _rms(ckv, gkv_ref[...])
        cqn_ref[...] = cqn.astype(BF16)
        ckvn_ref[...] = ckvn.astype(BF16)
        lane = lax.broadcasted_iota(jnp.int32, (tm, LANES), 1)
        rope_lanes = (lane >= MLA_NOPE) & (lane < MLA_NOPE + MLA_ROPE)
        dkrope = jnp.zeros((tm, LANES), F32)
        for h in range(MLA_HEADS):
            cols = slice(h * LANES, (h + 1) * LANES)
            dqf_ref[:, cols] = _rope_bwd(dq_ref[:, cols] * MLA_SCALE, c, s_a, s_b, MLA_ROPE // 2).astype(BF16)
            dkh = dk_ref[:, cols]
            dkvf_ref[:, cols] = dkh.astype(BF16)
            dkrope = dkrope + dkh
        dkvf_ref[:, MLA_QW:] = dv_ref[...].astype(BF16)
        dkr = _rope_bwd(jnp.where(rope_lanes, dkrope, 0.0), c, s_a, s_b, MLA_ROPE // 2)
        dcqn = _dot_nt(dqf_ref[...], wuq_ref[...])
        dckvn = _dot_nt(dkvf_ref[...], wkv_ref[...])
        dcq, dgq = _rms_bwd(dcqn, cq, rq, gq_ref[...])
        dckv, dgkv = _rms_bwd(dckvn, ckv, rkv, gkv_ref[...])
        dgq_ref[...] += dgq
        dgkv_ref[...] += dgkv
        dcc_ref[:, 0:256] = dcq.astype(BF16)
        dcc_ref[:, 256:384] = dckv.astype(BF16)
        dcc_ref[:, 384:512] = dkr.astype(BF16)

    def row(w, j):
        return pl.BlockSpec((tm, w), lambda i: (i, j))

    def full(a):
        return pl.BlockSpec(a.shape, lambda i: (0, 0))

    return pl.pallas_call(
        body, grid=(t // tm,),
        in_specs=[row(256, 4096 // 256), row(128, 4352 // 128), row(128, 0), row(128, 0), row(128, 0),
                  full(g_cq), full(g_ckv), full(wuq), full(wkv), row(MLA_QW, 0), row(MLA_QW, 0), row(MLA_WIDTH, 0)],
        out_specs=[row(512, 0), row(MLA_QW, 0), row(256, 0), row(MLA_KVW, 0), row(128, 0), full(g_cq), full(g_ckv)],
        out_shape=[SDS((t, 512), BF16), SDS((t, MLA_QW), BF16), SDS((t, 256), BF16), SDS((t, MLA_KVW), BF16),
                   SDS((t, 128), BF16), SDS(g_cq.shape, F32), SDS(g_ckv.shape, F32)],
        name="mla_prep_bwd", compiler_params=_params("arbitrary"))(proj, proj, cm, sma, smb, g_cq, g_ckv, wuq, wkv, dq, dk, dv)


MLA_TQ = 256


def mla_attn_fwd(qb, kb, vb, nb):
    t = qb.shape[0]
    nq = SEQ // MLA_TQ
    n_pairs = MLA_HEADS // 2

    def body(q_ref, k_ref, v_ref, y_ref, lse_ref):
        head0 = lax.broadcasted_iota(jnp.int32, (MLA_TQ, LANES), 1) < MLA_V
        v = v_ref[...]
        outs, lses = [], []
        for h in range(2):
            cols = slice(h * LANES, (h + 1) * LANES)
            s = _dot_nt(q_ref[:, cols], k_ref[:, cols])
            m = jnp.max(s, axis=-1, keepdims=True)
            p = jnp.exp(s - m)
            l = jnp.sum(p, axis=-1, keepdims=True)
            outs.append(_dot(p.astype(BF16), v) / l)
            lses.append(m + jnp.log(l))
        y_ref[...] = jnp.where(head0, outs[0], outs[1])
        lse_ref[...] = jnp.where(head0, lses[0], lses[1])

    return pl.pallas_call(
        body, grid=(nb, n_pairs, nq),
        in_specs=[pl.BlockSpec((MLA_TQ, 2 * LANES), lambda b, hp, i: (b * nq + i, hp)),
                  pl.BlockSpec((SEQ, 2 * LANES), lambda b, hp, i: (b, hp)),
                  pl.BlockSpec((SEQ, LANES), lambda b, hp, i: (b, hp))],
        out_specs=[pl.BlockSpec((MLA_TQ, LANES), lambda b, hp, i: (b * nq + i, hp))] * 2,
        out_shape=[SDS((t, MLA_WIDTH), F32)] * 2,
        name="mla_attn_fwd", compiler_params=_params("parallel", "parallel", "parallel"))(qb, kb, vb)


def mla_attn_bwd(qb, kb, vb, dy, y, lse, nb, xch):
    t = qb.shape[0]
    nq = SEQ // MLA_TQ
    n_pairs = MLA_HEADS // 2

    def body(q_ref, k_ref, v_ref, do_ref, y_ref, lse_ref, dq_ref, dk_ref, dv_ref):
        @pl.when(pl.program_id(2) == 0)
        def _():
            dk_ref[...] = jnp.zeros_like(dk_ref)
            dv_ref[...] = jnp.zeros_like(dv_ref)

        head0 = lax.broadcasted_iota(jnp.int32, (MLA_TQ, LANES), 1) < MLA_V
        v = v_ref[...]
        do = do_ref[...]
        lse = lse_ref[...]
        tt = do * y_ref[...]
        dv = jnp.zeros((SEQ, LANES), F32)
        for h in range(2):
            sel = head0 if h == 0 else ~head0
            lo = h * MLA_V
            cols = slice(h * LANES, (h + 1) * LANES)
            q = q_ref[:, cols]
            k = k_ref[:, cols]
            dd = jnp.sum(jnp.where(sel, tt, 0.0), axis=-1, keepdims=True)
            doh = jnp.where(sel, do, 0.0).astype(BF16)
            p = jnp.exp(_dot_nt(q, k) - lse[:, lo:lo + 1])
            dp = _dot_nt(doh, v)
            ds = (p * (dp - dd)).astype(BF16)
            dq_ref[:, cols] = _dot(ds, k)
            dk_ref[:, cols] += _dot_tn(ds, q)
            dv = dv + _dot_tn(p.astype(BF16), doh)
        dv_ref[...] += dv

    qspec = pl.BlockSpec((MLA_TQ, 2 * LANES), lambda b, hp, i: (b * nq + i, hp))
    kspec = pl.BlockSpec((SEQ, 2 * LANES), lambda b, hp, i: (b, hp))
    vspec = pl.BlockSpec((SEQ, LANES), lambda b, hp, i: (b, hp))
    ospec = pl.BlockSpec((MLA_TQ, LANES), lambda b, hp, i: (b * nq + i, hp))
    return call_hosting_exchange(
        body, xch, grid=(nb, n_pairs, nq),
        in_specs=[qspec, kspec, vspec, ospec, ospec, ospec],
        out_specs=[qspec, kspec, vspec],
        out_shape=[SDS((t, MLA_QW), F32), SDS((t, MLA_QW), F32), SDS((t, MLA_WIDTH), F32)],
        scratch_shapes=[], name="mla_attn_bwd", operands=(qb, kb, vb, dy, y, lse))


MEM_TQ = 512
MEM_SCALE = MEM_HEAD_DIM ** -0.5
MQ_BLK = 5120 // LANES


def mem_attn_fwd(proj, mkv, nb):
    t = proj.shape[0]
    nq = SEQ // MEM_TQ

    def body(q_ref, mk_ref, mv_ref, y_ref):
        s = _dot_nt(q_ref[...].astype(BF16), mk_ref[...]) * MEM_SCALE
        m = jnp.max(s, axis=-1, keepdims=True)
        p = jnp.exp(s - m)
        l = jnp.sum(p, axis=-1, keepdims=True)
        y_ref[...] = _dot(p.astype(BF16), mv_ref[...]) / l

    return pl.pallas_call(
        body, grid=(nb, MEM_HEADS, nq),
        in_specs=[pl.BlockSpec((MEM_TQ, LANES), lambda b, h, i: (b * nq + i, MQ_BLK + h)),
                  pl.BlockSpec((N_MEM, LANES), lambda b, h, i: (b, h)),
                  pl.BlockSpec((N_MEM, LANES), lambda b, h, i: (b, MEM_HEADS + h))],
        out_specs=pl.BlockSpec((MEM_TQ, LANES), lambda b, h, i: (b * nq + i, h)),
        out_shape=SDS((t, MEM_WIDTH), F32),
        name="mem_attn_fwd", compiler_params=_params("parallel", "parallel", "parallel"))(proj, mkv, mkv)


def mem_attn_bwd(proj, mkv, dy, nb):
    t = proj.shape[0]
    nq = SEQ // MEM_TQ

    def body(q_ref, mk_ref, mv_ref, do_ref, dq_ref, dmk_ref, dmv_ref):
        @pl.when(pl.program_id(2) == 0)
        def _():
            dmk_ref[...] = jnp.zeros_like(dmk_ref)
            dmv_ref[...] = jnp.zeros_like(dmv_ref)

        q = q_ref[...].astype(BF16)
        mk, mv = mk_ref[...], mv_ref[...]
        do = do_ref[...].astype(BF16)
        s = _dot_nt(q, mk) * MEM_SCALE
        e = jnp.exp(s - jnp.max(s, axis=-1, keepdims=True))
        p = e / jnp.sum(e, axis=-1, keepdims=True)
        dp = _dot_nt(do, mv)
        ds = (p * (dp - jnp.sum(p * dp, axis=-1, keepdims=True)) * MEM_SCALE).astype(BF16)
        dq_ref[...] = _dot(ds, mk).astype(BF16)
        dmk_ref[...] += _dot_tn(ds, q)
        dmv_ref[...] += _dot_tn(p.astype(BF16), do)

    ospec = pl.BlockSpec((MEM_TQ, LANES), lambda b, h, i: (b * nq + i, h))
    kspec = pl.BlockSpec((N_MEM, LANES), lambda b, h, i: (b, h))
    return pl.pallas_call(
        body, grid=(nb, MEM_HEADS, nq),
        in_specs=[pl.BlockSpec((MEM_TQ, LANES), lambda b, h, i: (b * nq + i, MQ_BLK + h)),
                  kspec, pl.BlockSpec((N_MEM, LANES), lambda b, h, i: (b, MEM_HEADS + h)), ospec],
        out_specs=[ospec, kspec, kspec],
        out_shape=[SDS((t, MEM_WIDTH), BF16), SDS((nb * N_MEM, MEM_WIDTH), F32), SDS((nb * N_MEM, MEM_WIDTH), F32)],
        name="mem_attn_bwd", compiler_params=_params("parallel", "parallel", "arbitrary"))(proj, mkv, mkv, dy)


ROW_TM = 256
AG_BLK = 3072 // 1024
BG_BLK = 4608 // 512
MG_BLK = 5632 // 512
GROUPS = ((0, A_WIDTH), (A_WIDTH, MLA_WIDTH), (A_WIDTH + MLA_WIDTH, MEM_WIDTH))
D_MIX = 2048


def _gate_specs():
    def row(w, j):
        return pl.BlockSpec((ROW_TM, w), lambda i: (i, j))

    def vec(w):
        return pl.BlockSpec((1, w), lambda i: (0, 0))

    ys = [row(A_WIDTH, 0), row(MLA_WIDTH, 0), row(MEM_WIDTH, 0)]
    gates = [row(A_WIDTH, AG_BLK), row(MLA_WIDTH, BG_BLK), row(MEM_WIDTH, MG_BLK)]
    gains = [vec(A_WIDTH), vec(MLA_WIDTH), vec(MEM_WIDTH)]
    return row, vec, ys, gates, gains


def gate_fwd(ya, yb, ym, proj, goa, gob, gom):
    t = ya.shape[0]
    row, vec, ys, gates, gains = _gate_specs()

    def body(ya_ref, yb_ref, ym_ref, ga_ref, gb_ref, gm_ref, goa_ref, gob_ref, gom_ref, z_ref):
        for (off, w), y_ref, g_ref, go_ref in zip(GROUPS, (ya_ref, yb_ref, ym_ref), (ga_ref, gb_ref, gm_ref),
                                                  (goa_ref, gob_ref, gom_ref)):
            n, _ = _rms(y_ref[...], go_ref[...])
            gt = g_ref[...]
            z_ref[:, off:off + w] = (n * (gt * _sigmoid(gt))).astype(BF16)

    return pl.pallas_call(
        body, grid=(t // ROW_TM,), in_specs=ys + gates + gains, out_specs=row(D_MIX, 0),
        out_shape=SDS((t, D_MIX), BF16), name="gate_fwd",
        compiler_params=_params("parallel"))(ya, yb, ym, proj, proj, proj, goa, gob, gom)


def out_ln_loss(z, wout, h32, target, gp, bp):
    t, d = h32.shape

    def body(z_ref, w_ref, h_ref, t_ref, gp_ref, bp_ref, du32_ref, du16_ref, loss_ref, dgp_ref, dbp_ref):
        @pl.when(pl.program_id(0) == 0)
        def _():
            loss_ref[...] = jnp.zeros_like(loss_ref)
            dgp_ref[...] = jnp.zeros_like(dgp_ref)
            dbp_ref[...] = jnp.zeros_like(dbp_ref)

        g = gp_ref[...]
        u = ALPHA * h_ref[...] + _dot(z_ref[...], w_ref[...])
        mu = jnp.mean(u, axis=-1, keepdims=True)
        uc = u - mu
        rstd = lax.rsqrt(jnp.mean(uc * uc, axis=-1, keepdims=True) + NORM_EPS)
        xhat = uc * rstd
        err = xhat * g + bp_ref[...] - t_ref[...]
        tok = jnp.sum(err * err, axis=-1, keepdims=True) * (1.0 / d)
        loss_ref[...] += 0.5 * jnp.sum(tok, axis=0, keepdims=True)
        dout = err * (1.0 / d)
        dxhat = dout * g
        du = rstd * (dxhat - jnp.mean(dxhat, axis=-1, keepdims=True)
                     - xhat * jnp.mean(dxhat * xhat, axis=-1, keepdims=True))
        du32_ref[...] = du
        du16_ref[...] = du.astype(BF16)
        dgp_ref[...] += jnp.sum(dout * xhat, axis=0, keepdims=True)
        dbp_ref[...] += jnp.sum(dout, axis=0, keepdims=True)

    row = pl.BlockSpec((ROW_TM, d), lambda i: (i, 0))
    vec = pl.BlockSpec((1, d), lambda i: (0, 0))
    return pl.pallas_call(
        body, grid=(t // ROW_TM,),
        in_specs=[pl.BlockSpec((ROW_TM, D_MIX), lambda i: (i, 0)), pl.BlockSpec((D_MIX, d), lambda i: (0, 0)), row, row, vec, vec],
        out_specs=[row, row, pl.BlockSpec((1, LANES), lambda i: (0, 0)), vec, vec],
        out_shape=[SDS((t, d), F32), SDS((t, d), BF16), SDS((1, LANES), F32), SDS((1, d), F32), SDS((1, d), F32)],
        name="out_ln_loss", compiler_params=_params("arbitrary"))(z, wout, h32, target, gp, bp)


def gate_bwd(du16, wout, ya, yb, ym, proj, goa, gob, gom):
    t = ya.shape[0]
    row, vec, ys, gates, gains = _gate_specs()

    def body(du_ref, w_ref, ya_ref, yb_ref, ym_ref, ga_ref, gb_ref, gm_ref, goa_ref, gob_ref, gom_ref,
             dya_ref, dyb_ref, dym_ref, dga_ref, dgb_ref, dgm_ref, dgoa_ref, dgob_ref, dgom_ref):
        @pl.when(pl.program_id(0) == 0)
        def _():
            dgoa_ref[...] = jnp.zeros_like(dgoa_ref)
            dgob_ref[...] = jnp.zeros_like(dgob_ref)
            dgom_ref[...] = jnp.zeros_like(dgom_ref)

        dz = _dot_nt(du_ref[...], w_ref[...])
        for (off, w), y_ref, g_ref, go_ref, dy_ref, dg_ref, dgo_ref in zip(
                GROUPS, (ya_ref, yb_ref, ym_ref), (ga_ref, gb_ref, gm_ref), (goa_ref, gob_ref, gom_ref),
                (dya_ref, dyb_ref, dym_ref), (dga_ref, dgb_ref, dgm_ref), (dgoa_ref, dgob_ref, dgom_ref)):
            dzg = dz[:, off:off + w]
            y, gt, go = y_ref[...], g_ref[...], go_ref[...]
            n, r = _rms(y, go)
            sg = _sigmoid(gt)
            dg_ref[...] = (dzg * n * (sg * (1.0 + gt * (1.0 - sg)))).astype(BF16)
            dy, dgo = _rms_bwd(dzg * (gt * sg), y, r, go)
            dy_ref[...] = dy
            dgo_ref[...] += dgo

    widths = (A_WIDTH, MLA_WIDTH, MEM_WIDTH)
    return pl.pallas_call(
        body, grid=(t // ROW_TM,),
        in_specs=[row(D_MODEL, 0), pl.BlockSpec((D_MIX, D_MODEL), lambda i: (0, 0))] + ys + gates + gains,
        out_specs=[row(w, 0) for w in widths] * 2 + [vec(w) for w in widths],
        out_shape=[SDS((t, w), F32) for w in widths] + [SDS((t, w), BF16) for w in widths] + [SDS((1, w), F32) for w in widths],
        name="gate_bwd", compiler_params=_params("arbitrary"))(du16, wout, ya, yb, ym, proj, proj, proj, goa, gob, gom)


def dh_ln_bwd(pieces, win_t, du32, x2, g_emb, xch):
    t, d = x2.shape

    def body(*refs):
        p_refs = refs[:len(pieces)]
        w_ref, du_ref, x_ref, g_ref, dx_ref, dg_ref, db_ref = refs[len(pieces):]

        @pl.when(pl.program_id(0) == 0)
        def _():
            dg_ref[...] = jnp.zeros_like(dg_ref)
            db_ref[...] = jnp.zeros_like(db_ref)

        dh = ALPHA * du_ref[...]
        for p_ref, off, w in zip(p_refs, PIECE_OFFS, PIECE_WIDTHS):
            dh = dh + _dot(p_ref[...], w_ref[off:off + w, :])
        x = x_ref[...]
        xc = x - jnp.mean(x, axis=-1, keepdims=True)
        rstd = lax.rsqrt(jnp.mean(xc * xc, axis=-1, keepdims=True) + NORM_EPS)
        xhat = xc * rstd
        dg_ref[...] += jnp.sum(dh * xhat, axis=0, keepdims=True)
        db_ref[...] += jnp.sum(dh, axis=0, keepdims=True)
        tg = dh * g_ref[...]
        dx_ref[...] = rstd * (tg - jnp.mean(tg, axis=-1, keepdims=True)
                              - xhat * jnp.mean(tg * xhat, axis=-1, keepdims=True))

    row = pl.BlockSpec((ROW_TM, d), lambda i: (i, 0))
    vec = pl.BlockSpec((1, d), lambda i: (0, 0))
    return call_hosting_exchange(
        body, xch, grid=(t // ROW_TM,),
        in_specs=[pl.BlockSpec((ROW_TM, w), lambda i: (i, 0)) for w in PIECE_WIDTHS]
        + [pl.BlockSpec(win_t.shape, lambda i: (0, 0)), row, row, vec],
        out_specs=[row, vec, vec],
        out_shape=[SDS((t, d), F32), SDS((1, d), F32), SDS((1, d), F32)],
        scratch_shapes=[], name="dh_ln_bwd", operands=(*pieces, win_t, du32, x2, g_emb))


def _adamw(w, g, m, v):
    m2 = ADAM_B1 * m + (1.0 - ADAM_B1) * g
    v2 = ADAM_B2 * v + (1.0 - ADAM_B2) * (g * g)
    m_hat = m2 / (1.0 - ADAM_B1 ** ADAM_STEP)
    v_hat = v2 / (1.0 - ADAM_B2 ** ADAM_STEP)
    return -ADAM_LR * (m_hat / (jnp.sqrt(v_hat) + ADAM_EPS) + ADAM_WD * w), m2, v2


def adamw_shard(w, parts, m, v, name):
    r, c = w.shape
    if r % 256 == 0 or r * c <= 256 * 1024:
        tr, tc = min(r, 256), c
    else:
        tr, tc = r, 256

    def body(w_ref, p_ref, m_ref, v_ref, g_ref, d_ref, nm_ref, nv_ref):
        g = p_ref[0].astype(F32)
        for k in range(1, N_DEV):
            g = g + p_ref[k].astype(F32)
        g_ref[...] = g
        d_ref[...], nm_ref[...], nv_ref[...] = _adamw(w_ref[...], g, m_ref[...], v_ref[...])

    blk = pl.BlockSpec((tr, tc), lambda i, j: (i, j))
    return pl.pallas_call(
        body, grid=(r // tr, c // tc),
        in_specs=[blk, pl.BlockSpec((N_DEV, tr, tc), lambda i, j: (0, i, j)), blk, blk],
        out_specs=[blk] * 4, out_shape=[SDS((r, c), F32)] * 4, name=name,
        compiler_params=_params("parallel", "parallel"))(w, parts, m, v)


def _place():
    return lax.axis_index("x"), lax.axis_index("y"), lax.axis_index("c")


def _flat(px, py, pc):
    return 4 * px + 2 * py + pc


def _peer(x, y, c, k):
    return (1 - x if k & 4 else x, 1 - y if k & 2 else y, 1 - c if k & 1 else c)


def cast_shards(shards):
    def body(*refs):
        n = len(refs) // 2
        for i_ref, o_ref in zip(refs[:n], refs[n:]):
            o_ref[...] = i_ref[...].astype(BF16)

    return pl.pallas_call(body, out_shape=[SDS(s.shape, BF16) for s in shards], name="cast_shards",
                          compiler_params=_params())(*shards)


def allgather_weights(shards):
    n = len(shards)

    def body(*refs):
        ins, outs = refs[:n], refs[n:2 * n]
        send_sems, recv_sems, local_sems = refs[2 * n:]
        x, y, c = _place()
        me, sib = (x, y, c), (x, y, 1 - c)
        chips = [(1 - x, y), (x, 1 - y), (1 - x, 1 - y)]

        def copy(a, k, block, to, src=None):
            dst = outs[a].at[_flat(*block)]
            return pltpu.make_async_remote_copy(
                src_ref=dst if src is None else src, dst_ref=dst,
                send_sem=send_sems.at[a * 7 + k], recv_sem=recv_sems.at[a * 7 + k],
                device_id=to, device_id_type=MESH)

        mine = [pltpu.make_async_copy(ins[a], outs[a].at[_flat(*me)], local_sems.at[a]) for a in range(n)]
        for cp in mine:
            cp.start()
        first = []
        for a in range(n):
            first.append(copy(a, 0, me, sib, src=ins[a]))
            first += [copy(a, 1 + j, me, (*chip, c), src=ins[a]) for j, chip in enumerate(chips)]
        for cp in first:
            cp.start()
        passed = []
        for j, chip in enumerate(chips):
            for a in range(n):
                copy(a, 1 + j, (*chip, c), me).wait_recv()
                fwd = copy(a, 4 + j, (*chip, c), sib)
                fwd.start()
                passed.append(fwd)
        for a in range(n):
            copy(a, 0, sib, me).wait_recv()
            for j, chip in enumerate(chips):
                copy(a, 4 + j, (*chip, 1 - c), me).wait_recv()
        for cp in first + passed:
            cp.wait_send()
        for cp in mine:
            cp.wait()

    hbm = pl.BlockSpec(memory_space=pl.ANY)
    return pl.pallas_call(
        body, out_shape=[SDS((N_DEV,) + s.shape, s.dtype) for s in shards],
        in_specs=[hbm] * n, out_specs=[hbm] * n,
        scratch_shapes=[pltpu.SemaphoreType.DMA((7 * n,)), pltpu.SemaphoreType.DMA((7 * n,)), pltpu.SemaphoreType.DMA((n,))],
        name="allgather_weights", compiler_params=_params())(*shards)


ALL_DEVICES = tuple(range(N_DEV))


def _exchange_plan(src_refs, land_refs, dests, send_sems, recv_sems, local_sems):
    x, y, c = _place()
    me = _flat(x, y, c)
    plan = []
    for a, (src, land, dl) in enumerate(zip(src_refs, land_refs, dests)):
        for li, j in enumerate(dl):
            to = ((j >> 2) & 1, (j >> 1) & 1, j & 1)
            block = src.at[li] if len(src.shape) == len(land.shape) else src

            def push(slot, a=a, block=block, land=land, j=j, to=to):
                return pltpu.make_async_remote_copy(
                    src_ref=block, dst_ref=land.at[slot], send_sem=send_sems.at[a * N_DEV + j],
                    recv_sem=recv_sems.at[a * N_DEV + slot], device_id=to, device_id_type=MESH)

            own = pltpu.make_async_copy(block, land.at[j], local_sems.at[a])
            plan.append((j, push(me), own, [push(s) for s in range(N_DEV) if s != j]))
    return me, plan


def _exchange_start(me, plan):
    for j, send, own, _ in plan:
        @pl.when(me != j)
        def _(send=send):
            send.start()

        @pl.when(me == j)
        def _(own=own):
            own.start()


def _exchange_wait(me, plan):
    for j, send, own, arrivals in plan:
        @pl.when(me != j)
        def _(send=send):
            send.wait_send()

        @pl.when(me == j)
        def _(own=own, arrivals=arrivals):
            own.wait()
            for arrival in arrivals:
                arrival.wait_recv()


def call_hosting_exchange(core, xch, *, grid, in_specs, out_specs, out_shape, scratch_shapes, name, operands):
    srcs, dests, landing = xch
    n, n_in, n_out, n_scr = len(srcs), len(in_specs), len(out_specs), len(scratch_shapes)

    def body(*refs):
        ins, src_refs = refs[:n_in], refs[n_in:n_in + n]
        outs = refs[n_in + 2 * n:n_in + 2 * n + n_out]
        land_refs = refs[n_in + 2 * n + n_out:n_in + 3 * n + n_out]
        scratch = refs[n_in + 3 * n + n_out:n_in + 3 * n + n_out + n_scr]
        sems = refs[n_in + 3 * n + n_out + n_scr:]
        first = functools.reduce(jnp.logical_and, [pl.program_id(i) == 0 for i in range(len(grid))])
        last = functools.reduce(jnp.logical_and, [pl.program_id(i) == grid[i] - 1 for i in range(len(grid))])
        me, plan = _exchange_plan(src_refs, land_refs, dests, *sems)

        @pl.when(first)
        def _():
            _exchange_start(me, plan)

        core(*ins, *outs, *scratch)

        @pl.when(last)
        def _():
            _exchange_wait(me, plan)

    hbm = pl.BlockSpec(memory_space=pl.ANY)
    res = pl.pallas_call(
        body, grid=grid,
        in_specs=list(in_specs) + [hbm] * (2 * n), out_specs=list(out_specs) + [hbm] * n,
        out_shape=list(out_shape) + [SDS(l.shape, l.dtype) for l in landing],
        scratch_shapes=list(scratch_shapes) + [pltpu.SemaphoreType.DMA((N_DEV * n,)), pltpu.SemaphoreType.DMA((N_DEV * n,)),
                                               pltpu.SemaphoreType.DMA((n,))],
        input_output_aliases={n_in + n + k: n_out + k for k in range(n)},
        name=name, compiler_params=_params(*(("arbitrary",) * len(grid))))(*operands, *srcs, *landing)
    return res[:n_out], res[n_out:]


SLOT_ROWS = 8


def small_allreduce_adamw(loss_sum, grads, ws, ms, vs):
    n = len(grads)
    rows = [g.shape[0] for g in grads]
    total = SLOT_ROWS * (n + 1)

    def body(*refs):
        loss_ref, g_refs, w_refs = refs[0], refs[1:1 + n], refs[1 + n:1 + 2 * n]
        m_refs, v_refs = refs[1 + 2 * n:1 + 3 * n], refs[1 + 3 * n:1 + 4 * n]
        outs = refs[1 + 4 * n:2 + 8 * n]
        vec, gath, tot, send_sems, recv_sems = refs[2 + 8 * n:]
        x, y, c = _place()
        me = _flat(x, y, c)
        vec[...] = jnp.zeros_like(vec)
        vec[0:1, :] = loss_ref[...]
        for i in range(n):
            vec[SLOT_ROWS * (i + 1):SLOT_ROWS * (i + 1) + rows[i], :] = g_refs[i][...]
        gath[me] = vec[...]
        copies = []
        for k in range(1, N_DEV):
            peer = _peer(x, y, c, k)
            copies.append(pltpu.make_async_remote_copy(
                src_ref=vec, dst_ref=gath.at[me], send_sem=send_sems.at[k - 1], recv_sem=recv_sems.at[k - 1],
                device_id=peer, device_id_type=MESH))
        for cp in copies:
            cp.start()
        for cp in copies:
            cp.wait_recv()
        for cp in copies:
            cp.wait_send()
        g = gath[0]
        for j in range(1, N_DEV):
            g = g + gath[j]
        tot[...] = g
        outs[0][...] = tot[0:1, :]
        for i in range(n):
            gi = tot[SLOT_ROWS * (i + 1):SLOT_ROWS * (i + 1) + rows[i], :]
            outs[1 + i][...] = gi
            outs[1 + n + i][...], outs[1 + 2 * n + i][...], outs[1 + 3 * n + i][...] = _adamw(
                w_refs[i][...], gi, m_refs[i][...], v_refs[i][...])

    shapes = [SDS(g.shape, F32) for g in grads]
    return pl.pallas_call(
        body, out_shape=[SDS((1, LANES), F32)] + shapes * 4,
        scratch_shapes=[pltpu.VMEM((total, LANES), F32), pltpu.VMEM((N_DEV, total, LANES), F32), pltpu.VMEM((total, LANES), F32),
                        pltpu.SemaphoreType.DMA((7,)), pltpu.SemaphoreType.DMA((7,))],
        name="small_allreduce_adamw", compiler_params=_params())(loss_sum, *grads, *ws, *ms, *vs)


def _rope_tables(positions):
    pos = positions.astype(F32).reshape(-1, 1)

    def cs(r):
        inv_freq = ROPE_THETA ** (-(jnp.arange(0, r, 2, dtype=F32) / r))
        ang = pos * inv_freq
        return jnp.cos(ang), jnp.sin(ang)

    n = pos.shape[0]
    one = lambda w: jnp.ones((n, w), F32)
    zero = lambda w: jnp.zeros((n, w), F32)
    ca, sa = cs(A_ROT)
    rest = A_HEAD_DIM - A_ROT
    a_c = jnp.tile(jnp.concatenate([ca, ca, one(rest)], 1), (1, 2))
    a_sa = jnp.tile(jnp.concatenate([-sa, zero(A_ROT // 2 + rest)], 1), (1, 2))
    a_sb = jnp.tile(jnp.concatenate([zero(A_ROT // 2), sa, zero(rest)], 1), (1, 2))
    cm, sm = cs(MLA_ROPE)
    tail = LANES - MLA_NOPE - MLA_ROPE
    m_c = jnp.concatenate([one(MLA_NOPE), cm, cm, one(tail)], 1)
    m_sa = jnp.concatenate([zero(MLA_NOPE), -sm, zero(MLA_ROPE // 2 + tail)], 1)
    m_sb = jnp.concatenate([zero(MLA_NOPE + MLA_ROPE // 2), sm, zero(tail)], 1)
    return (a_c, a_sa, a_sb), (m_c, m_sa, m_sb)


KR_LO, KR_HI = 4480, 4512
W_IN_SHARD = D_IN // N_DEV
AG_SPLIT = 5 * W_IN_SHARD - 3 * A_WIDTH
BG_SPLIT = 6 * W_IN_SHARD - KR_HI


def _w_in_working_t(g):
    w = g.reshape(D_IN, D_MODEL)
    z = lambda n: jnp.zeros((n, D_MODEL), w.dtype)
    return jnp.concatenate([w[:KR_LO], z(MLA_NOPE), w[KR_LO:KR_HI], z(LANES - MLA_NOPE - MLA_ROPE), w[KR_HI:]], 0)


def _shards(rows):
    return rows.reshape(-1, W_IN_SHARD, D_MODEL).astype(BF16)


def _w_in_shards_0_4(d_aq, d_ak, d_av, d_ag):
    return _shards(jnp.concatenate([d_aq, d_ak, d_av, d_ag[:AG_SPLIT]], 0))


def _w_in_shard_5(d_ag, d_cc, d_bg):
    kr = MLA_Q_RANK + MLA_KV_RANK + MLA_NOPE
    return _shards(jnp.concatenate([d_ag[AG_SPLIT:], d_cc[:MLA_Q_RANK + MLA_KV_RANK], d_cc[kr:kr + MLA_ROPE], d_bg[:BG_SPLIT]], 0))


def _w_in_shards_6_7(d_bg, d_mq, d_mg):
    return _shards(jnp.concatenate([d_bg[BG_SPLIT:], d_mq, d_mg], 0))


def _w_uq_working(g):
    w = jnp.pad(g.transpose(1, 0, 2), ((0, 0), (0, 0), (0, LANES - MLA_NOPE - MLA_ROPE)))
    return w.reshape(MLA_Q_RANK, MLA_QW)


def _w_uq_parts(dw):
    return dw.reshape(MLA_Q_RANK, MLA_HEADS, LANES)[:, :, :MLA_NOPE + MLA_ROPE].transpose(1, 0, 2)


def _w_ukv_working(g):
    wk = jnp.pad(g[:, :, :MLA_NOPE].transpose(1, 0, 2), ((0, 0), (0, 0), (0, LANES - MLA_NOPE)))
    wv = g[:, :, MLA_NOPE:].transpose(1, 0, 2)
    return jnp.concatenate([wk.reshape(MLA_KV_RANK, MLA_QW), wv.reshape(MLA_KV_RANK, MLA_WIDTH)], 1)


def _w_ukv_parts(dw):
    dk = dw[:, :MLA_QW].reshape(MLA_KV_RANK, MLA_HEADS, LANES)[:, :, :MLA_NOPE]
    dv = dw[:, MLA_QW:].reshape(MLA_KV_RANK, MLA_HEADS, MLA_V)
    return jnp.concatenate([dk, dv], -1).transpose(1, 0, 2)


SMALL_NAMES = ("g_emb", "b_emb", "g_cq", "g_ckv", "g_out_a", "g_out_b", "g_out_m", "g_post", "b_post")


def kernel(x, mem, positions, g_emb, b_emb, w_in, g_cq, g_ckv, w_uq, w_ukv, w_mem_kv, g_out_a, g_out_b, g_out_m, w_out, g_post, b_post, loss_target, m_g_emb, m_b_emb, m_w_in, m_g_cq, m_g_ckv, m_w_uq, m_w_ukv, m_w_mem_kv, m_g_out_a, m_g_out_b, m_g_out_m, m_w_out, m_g_post, m_b_post, v_g_emb, v_b_emb, v_w_in, v_g_cq, v_g_ckv, v_w_uq, v_w_ukv, v_w_mem_kv, v_g_out_a, v_g_out_b, v_g_out_m, v_w_out, v_g_post, v_b_post):
    nb = x.shape[0]
    t = nb * SEQ
    x2 = x.reshape(t, D_MODEL)
    tgt2 = loss_target.reshape(t, D_MODEL)
    mem2 = mem.reshape(nb * N_MEM, D_MODEL)
    g_emb2, b_emb2 = g_emb.reshape(1, -1), b_emb.reshape(1, -1)
    (a_c, a_sa, a_sb), (m_c, m_sa, m_sb) = _rope_tables(positions)

    w_in_t, m_w_in_t, v_w_in_t = w_in[0].T, m_w_in[0].T, v_w_in[0].T
    s_in, s_uq, s_ukv, s_mem, s_out = cast_shards((w_in_t, w_uq[0], w_ukv[0], w_mem_kv[0], w_out[0]))
    (g_in,) = allgather_weights((s_in,))
    win_t = _w_in_working_t(g_in)

    h32, h16 = ln_emb_fwd(x2, g_emb2, b_emb2)
    proj = mm_nn(h16, win_t, F32, 512, 1536, "proj", rhs_transposed=True)
    later = (s_uq, s_ukv, s_mem, s_out)
    (ya, lse_a), (g_uq, g_ukv, g_mem, g_out) = a_attn_fwd(
        proj, a_c, a_sa, a_sb, nb,
        (later, (ALL_DEVICES,) * len(later), tuple(lax.empty((N_DEV,) + w.shape, BF16) for w in later)))
    wuq_w = _w_uq_working(g_uq)
    wkv_w = _w_ukv_working(g_ukv)
    wmem = g_mem.reshape(D_MODEL, 2 * MEM_WIDTH)
    wout = g_out.reshape(D_MIX, D_MODEL)
    qb, kb, vb = mla_prep_fwd(proj, m_c, m_sa, m_sb, g_cq, g_ckv, wuq_w, wkv_w)
    yb, lse_b = mla_attn_fwd(qb, kb, vb, nb)
    mkv = mm_nn(mem2, wmem, BF16, nb * N_MEM, 512, "mem_kv")
    ym = mem_attn_fwd(proj, mkv, nb)
    z = gate_fwd(ya, yb, ym, proj, g_out_a, g_out_b, g_out_m)
    du32, du16, loss_sum, dg_post, db_post = out_ln_loss(z, wout, h32, tgt2, g_post, b_post)

    dya, dyb, dym, dag, dbg, dmg, dg_out_a, dg_out_b, dg_out_m = gate_bwd(
        du16, wout, ya, yb, ym, proj, g_out_a, g_out_b, g_out_m)
    dw_out = mm_tn(z, du16, 512, "dw_out")
    dmq, dmk, dmv = mem_attn_bwd(proj, mkv, dym, nb)
    dw_mem = mm_tn(mem2, jnp.concatenate([dmk, dmv], 1), nb * N_MEM, "dw_mem")
    d_ag, d_bg, d_mq, d_mg = [mm_tn(p, h16, 512, "dw_in_" + n) for n, p in (("ag", dag), ("bg", dbg), ("mq", dmq), ("mg", dmg))]
    landing = lambda w, dtype=F32: lax.empty((N_DEV,) + w.shape, dtype)
    big_w = (w_in_t, w_uq[0], w_ukv[0], w_mem_kv[0], w_out[0])
    (daq, dak, dav), (p_out, p_mem, p_in) = a_attn_bwd(
        proj, a_c, a_sa, a_sb, dya, ya, lse_a, nb,
        ((dw_out.reshape(N_DEV, D_MIX // N_DEV, D_MODEL), dw_mem.reshape(N_DEV, D_MODEL // N_DEV, 2 * MEM_WIDTH),
          _w_in_shards_6_7(d_bg, d_mq, d_mg)),
         (ALL_DEVICES, ALL_DEVICES, (6, 7)),
         (landing(w_out[0]), landing(w_mem_kv[0]), landing(w_in_t, BF16))))
    d_aq, d_ak, d_av = [mm_tn(p, h16, 512, "dw_in_" + n) for n, p in (("aq", daq), ("ak", dak), ("av", dav))]
    (dqb, dkb, dvb), (p_in,) = mla_attn_bwd(
        qb, kb, vb, dyb, yb, lse_b, nb, ((_w_in_shards_0_4(d_aq, d_ak, d_av, d_ag),), ((0, 1, 2, 3, 4),), (p_in,)))
    dcc, dqf, cqn, dkvf, ckvn, dg_cq, dg_ckv = mla_prep_bwd(proj, m_c, m_sa, m_sb, g_cq, g_ckv, wuq_w, wkv_w, dqb, dkb, dvb)
    dw_uq = mm_tn(cqn, dqf, 512, "dw_uq")
    dw_ukv = mm_tn(ckvn, dkvf, 512, "dw_ukv")
    d_cc = mm_tn(dcc, h16, 512, "dw_in_cc")
    pieces = (daq, dak, dav, dag, dcc, dbg, dmq, dmg)
    (grad_x, dg_emb, db_emb), (p_in, p_uq, p_ukv) = dh_ln_bwd(
        pieces, win_t, du32, x2, g_emb2,
        ((_w_in_shard_5(d_ag, d_cc, d_bg), _w_uq_parts(dw_uq), _w_ukv_parts(dw_ukv)),
         ((5,), ALL_DEVICES, ALL_DEVICES),
         (p_in, landing(w_uq[0]), landing(w_ukv[0]))))

    parts = (p_in, p_uq, p_ukv, p_mem, p_out)
    big_m = (m_w_in_t, m_w_uq[0], m_w_ukv[0], m_w_mem_kv[0], m_w_out[0])
    big_v = (v_w_in_t, v_w_uq[0], v_w_ukv[0], v_w_mem_kv[0], v_w_out[0])
    big = {}
    for name, w, p, m, v in zip(("w_in", "w_uq", "w_ukv", "w_mem_kv", "w_out"), big_w, parts, big_m, big_v):
        res = adamw_shard(w, p, m, v, "adamw_" + name)
        big[name] = [(o.T if name == "w_in" else o)[None] for o in res]

    small_w = (g_emb, b_emb, g_cq, g_ckv, g_out_a, g_out_b, g_out_m, g_post, b_post)
    small_m = (m_g_emb, m_b_emb, m_g_cq, m_g_ckv, m_g_out_a, m_g_out_b, m_g_out_m, m_g_post, m_b_post)
    small_v = (v_g_emb, v_b_emb, v_g_cq, v_g_ckv, v_g_out_a, v_g_out_b, v_g_out_m, v_g_post, v_b_post)
    small_g = (dg_emb, db_emb, dg_cq, dg_ckv, dg_out_a, dg_out_b, dg_out_m, dg_post, db_post)
    rows128 = lambda vals: [v.reshape(-1, LANES) for v in vals]
    res = small_allreduce_adamw(loss_sum, rows128(small_g), rows128(small_w), rows128(small_m), rows128(small_v))
    loss = res[0][0, 0]
    n_small = len(small_w)
    sg, sd, sm, sv = [[r.reshape(w.shape) for r, w in zip(res[1 + k * n_small:1 + (k + 1) * n_small], small_w)]
                      for k in range(4)]

    order = ("g_emb", "b_emb", "w_in", "g_cq", "g_ckv", "w_uq", "w_ukv", "w_mem_kv", "g_out_a", "g_out_b", "g_out_m",
             "w_out", "g_post", "b_post")
    small_idx = {n: i for i, n in enumerate(SMALL_NAMES)}
    outs = [loss, grad_x.reshape(x.shape)]
    for kind in range(4):
        for name in order:
            outs.append(big[name][kind] if name in big else (sg, sd, sm, sv)[kind][small_idx[name]])
    return tuple(outs)
```

```python
import functools

import jax
import jax.numpy as jnp
from jax import lax
from jax.experimental import pallas as pl
from jax.experimental.pallas import tpu as pltpu

F32 = jnp.float32
BF16 = jnp.bfloat16
SDS = jax.ShapeDtypeStruct
MESH = pl.DeviceIdType.MESH

D_MODEL = 1024
SEQ = 2048
A_HEADS, A_HEAD_DIM, A_ROT = 16, 64, 16
A_WIDTH = 1024
DILATIONS = (1, 4, 16)
N_SIDE = 64
MLA_HEADS, MLA_Q_RANK, MLA_KV_RANK = 8, 256, 128
MLA_NOPE, MLA_ROPE, MLA_V = 64, 32, 64
MLA_WIDTH = 512
N_MEM, MEM_HEADS, MEM_HEAD_DIM, MEM_WIDTH = 256, 4, 128, 512
ROPE_THETA = 500000.0
NORM_EPS = 1e-5
NEG_INF = -1e30
ALPHA = 2.0 ** 0.25
D_IN = 6048
N_DEV = 8

ADAM_LR, ADAM_B1, ADAM_B2, ADAM_EPS, ADAM_WD, ADAM_STEP = 0.001, 0.9, 0.999, 1e-08, 0.01, 10

D_INW = 6144
PIECE_WIDTHS = (1024, 1024, 1024, 1024, 512, 512, 512, 512)
PIECE_OFFS = (0, 1024, 2048, 3072, 4096, 4608, 5120, 5632)
LANES = 128
VMEM_LIMIT = 56 * 1024 * 1024


def _params(*sem):
    kw = dict(vmem_limit_bytes=VMEM_LIMIT)
    if sem:
        kw["dimension_semantics"] = sem
    return pltpu.CompilerParams(**kw)


def _dot(a, b):
    return jnp.dot(a, b, preferred_element_type=F32)


def _dot_nt(a, b):
    return lax.dot_general(a, b, (((1,), (1,)), ((), ())), preferred_element_type=F32)


def _dot_tn(a, b):
    return lax.dot_general(a, b, (((0,), (0,)), ((), ())), preferred_element_type=F32)


def _sigmoid(x):
    return 1.0 / (1.0 + jnp.exp(-x))


def _rope_fwd(x, c, sa, sb, half):
    n = x.shape[-1]
    return x * c + pltpu.roll(x, n - half, 1) * sa + pltpu.roll(x, half, 1) * sb


def _rope_bwd(dy, c, sa, sb, half):
    n = dy.shape[-1]
    return dy * c + pltpu.roll(dy * sa, half, 1) + pltpu.roll(dy * sb, n - half, 1)


def mm_nn(a, b, out_dtype, tm, tn, name, rhs_transposed=False):
    m, k = a.shape
    n = b.shape[0] if rhs_transposed else b.shape[1]
    dot = _dot_nt if rhs_transposed else _dot

    def body(a_ref, b_ref, o_ref):
        o_ref[...] = dot(a_ref[...].astype(BF16), b_ref[...].astype(BF16)).astype(o_ref.dtype)

    b_spec = pl.BlockSpec((tn, k), lambda j, i: (j, 0)) if rhs_transposed else pl.BlockSpec((k, tn), lambda j, i: (0, j))
    return pl.pallas_call(
        body, grid=(n // tn, m // tm),
        in_specs=[pl.BlockSpec((tm, k), lambda j, i: (i, 0)), b_spec],
        out_specs=pl.BlockSpec((tm, tn), lambda j, i: (i, j)),
        out_shape=SDS((m, n), out_dtype), name=name,
        compiler_params=_params("parallel", "parallel"))(a, b)


def mm_tn(a, b, tt, name):
    t, m = a.shape
    n = b.shape[1]

    def body(a_ref, b_ref, o_ref):
        @pl.when(pl.program_id(0) == 0)
        def _():
            o_ref[...] = jnp.zeros_like(o_ref)

        o_ref[...] += _dot_tn(a_ref[...].astype(BF16), b_ref[...].astype(BF16))

    return pl.pallas_call(
        body, grid=(t // tt,),
        in_specs=[pl.BlockSpec((tt, m), lambda i: (i, 0)), pl.BlockSpec((tt, n), lambda i: (i, 0))],
        out_specs=pl.BlockSpec((m, n), lambda i: (0, 0)),
        out_shape=SDS((m, n), F32), name=name,
        compiler_params=_params("arbitrary"))(a, b)


def ln_emb_fwd(x2, g, b):
    t, d = x2.shape
    tm = 512

    def body(x_ref, g_ref, b_ref, h32_ref, h16_ref):
        x = x_ref[...]
        mu = jnp.mean(x, axis=-1, keepdims=True)
        xc = x - mu
        var = jnp.mean(xc * xc, axis=-1, keepdims=True)
        h = xc * lax.rsqrt(var + NORM_EPS) * g_ref[...] + b_ref[...]
        h32_ref[...] = h
        h16_ref[...] = h.astype(BF16)

    row = pl.BlockSpec((tm, d), lambda i: (i, 0))
    vec = pl.BlockSpec((1, d), lambda i: (0, 0))
    return pl.pallas_call(
        body, grid=(t // tm,), in_specs=[row, vec, vec], out_specs=[row, row],
        out_shape=[SDS((t, d), F32), SDS((t, d), BF16)], name="ln_emb_fwd",
        compiler_params=_params("parallel"))(x2, g, b)


Q_BLK = 128
UNROLL_FWD = 8
UNROLL_BWD = 8


def _pattern_geometry(d):
    length = SEQ // d
    nblk = length // Q_BLK
    kwin = min(2 * Q_BLK, length)
    return length, nblk, kwin


def _block_coords(idx, d):
    length, nblk, kwin = _pattern_geometry(d)
    r = lax.shift_right_logical(idx, nblk.bit_length() - 1)
    i = idx & (nblk - 1)
    q0 = pl.multiple_of(r * length + i * Q_BLK, Q_BLK)
    ks = jnp.clip(i * Q_BLK - N_SIDE, 0, length - kwin)
    k0 = pl.multiple_of(r * length + ks, N_SIDE)
    qpos = i * Q_BLK + lax.broadcasted_iota(jnp.int32, (Q_BLK, kwin), 0)
    kpos = ks + lax.broadcasted_iota(jnp.int32, (Q_BLK, kwin), 1)
    valid = jnp.abs(kpos - qpos) <= N_SIDE
    return q0, k0, kwin, valid


def _deinterleave(src_ref, dst_ref, d, dtype, tmp_ref):
    if d == 1:
        dst_ref[...] = src_ref[...].astype(dtype)
        return
    q = SEQ // 4
    if d == 4:
        for r in range(4):
            dst_ref[r * q:(r + 1) * q, :] = src_ref[pl.ds(r, q, stride=4), :].astype(dtype)
        return
    assert d == 16
    n = SEQ // 16
    for r in range(4):
        tmp_ref[r * q:(r + 1) * q, :] = src_ref[pl.ds(r, q, stride=4), :]
    for r in range(4):
        for j in range(4):
            dst_ref[(r + 4 * j) * n:(r + 4 * j + 1) * n, :] = tmp_ref[pl.ds(r * q + j, n, stride=4), :].astype(dtype)


def _interleave(src_ref, dst_ref, d, tmp_ref, accumulate):
    q = SEQ // 4
    if d == 16:
        n = SEQ // 16
        for r in range(4):
            for j in range(4):
                tmp_ref[pl.ds(r * q + j, n, stride=4), :] = src_ref[(r + 4 * j) * n:(r + 4 * j + 1) * n, :]
        src_ref = tmp_ref
    else:
        assert d == 4
    for r in range(4):
        rows = pl.ds(r, q, stride=4)
        val = src_ref[r * q:(r + 1) * q, :]
        dst_ref[rows, :] = dst_ref[rows, :] + val if accumulate else val


def a_attn_fwd(proj, ca, sa, sb, nb, xch):
    t = proj.shape[0]
    n_pairs = A_WIDTH // LANES

    def body(q_ref, k_ref, v_ref, c_ref, sa_ref, sb_ref, y_ref, lse_ref, *rest):
        qkv_d, (qr_s, kr_s, oc_s, lc_s, o1_s, l1_s, o2_s, l2_s, o3_s, l3_s, tmp_s) = rest[:9], rest[9:]
        c, s_a, s_b = c_ref[...], sa_ref[...], sb_ref[...]
        qr_s[...] = _rope_fwd(q_ref[...], c, s_a, s_b, A_ROT // 2) * (A_HEAD_DIM ** -0.5)
        kr_s[...] = _rope_fwd(k_ref[...], c, s_a, s_b, A_ROT // 2)
        head0 = lax.broadcasted_iota(jnp.int32, (Q_BLK, LANES), 1) < A_HEAD_DIM
        nat = ((o1_s, l1_s), (o2_s, l2_s), (o3_s, l3_s))

        for g, d in enumerate(DILATIONS):
            qd_s, kd_s, vd_s = qkv_d[3 * g:3 * g + 3]
            _deinterleave(qr_s, qd_s, d, BF16, tmp_s)
            _deinterleave(kr_s, kd_s, d, BF16, tmp_s)
            _deinterleave(v_ref, vd_s, d, BF16, tmp_s)
            o_dst, l_dst = (nat[g] if d == 1 else (oc_s, lc_s))

            def block(idx, carry, d=d, o_dst=o_dst, l_dst=l_dst, qd_s=qd_s, kd_s=kd_s, vd_s=vd_s):
                q0, k0, kwin, valid = _block_coords(idx, d)
                qb = qd_s[pl.ds(q0, Q_BLK), :]
                kb = kd_s[pl.ds(k0, kwin), :]
                vb = vd_s[pl.ds(k0, kwin), :]
                zero = jnp.zeros_like(qb)
                q2 = jnp.concatenate([jnp.where(head0, qb, zero), jnp.where(head0, zero, qb)], 0)
                s = jnp.where(jnp.concatenate([valid, valid], 0), _dot_nt(q2, kb), NEG_INF)
                m = jnp.max(s, axis=-1, keepdims=True)
                p = jnp.exp(s - m)
                l = jnp.sum(p, axis=-1, keepdims=True)
                o2 = _dot(p.astype(BF16), vb) / l
                l2 = m + jnp.log(l)
                o_dst[pl.ds(q0, Q_BLK), :] = jnp.where(head0, o2[:Q_BLK], o2[Q_BLK:])
                l_dst[pl.ds(q0, Q_BLK), :] = jnp.where(head0, l2[:Q_BLK], l2[Q_BLK:])
                return carry

            lax.fori_loop(0, SEQ // Q_BLK, block, 0, unroll=UNROLL_FWD)
            if d > 1:
                _interleave(oc_s, nat[g][0], d, tmp_s, False)
                _interleave(lc_s, nat[g][1], d, tmp_s, False)

        def merge(ci, carry):
            rows = pl.ds(pl.multiple_of(ci * 256, 256), 256)
            l1, l2, l3 = l1_s[rows, :], l2_s[rows, :], l3_s[rows, :]
            m = jnp.maximum(jnp.maximum(l1, l2), l3)
            w1, w2, w3 = jnp.exp(l1 - m), jnp.exp(l2 - m), jnp.exp(l3 - m)
            w = w1 + w2 + w3
            y_ref[rows, :] = (w1 * o1_s[rows, :] + w2 * o2_s[rows, :] + w3 * o3_s[rows, :]) / w
            lse_ref[rows, :] = m + jnp.log(w)
            return carry

        lax.fori_loop(0, SEQ // 256, merge, 0)

    def col(off):
        return pl.BlockSpec((SEQ, LANES), lambda b, hp: (b, off + hp))

    tab = pl.BlockSpec((SEQ, LANES), lambda b, hp: (b, 0))
    out = pl.BlockSpec((SEQ, LANES), lambda b, hp: (b, hp))
    f32s = pltpu.VMEM((SEQ, LANES), F32)
    res, landed = call_hosting_exchange(
        body, xch, grid=(nb, n_pairs),
        in_specs=[col(0), col(n_pairs), col(2 * n_pairs), tab, tab, tab],
        out_specs=[out] * 11,
        out_shape=[SDS((t, A_WIDTH), F32)] * 2 + [SDS((t, A_WIDTH), BF16)] * 9,
        scratch_shapes=[f32s] * 11,
        name="a_attn_fwd", operands=(proj, proj, proj, ca, sa, sb))
    return res[:2], res[2:], landed


def a_attn_bwd(qkv_d, ca, sa, sb, dy, y, lse, nb, xch):
    t = dy.shape[0]
    n_pairs = A_WIDTH // LANES

    def body(*refs):
        qkv_refs = refs[:9]
        (c_ref, sa_ref, sb_ref, do_ref, y_ref, lse_ref, dq_ref, dk_ref, dv_ref,
         l0n_s, l1n_s, d0n_s, d1n_s, dod_s, l0d_s, l1d_s, d0d_s, d1d_s,
         dqc_s, dkc_s, dvc_s, dqn_s, dkn_s, dvn_s, tmp_s) = refs[9:]
        c, s_a, s_b = c_ref[...], sa_ref[...], sb_ref[...]
        head0 = lax.broadcasted_iota(jnp.int32, (Q_BLK, LANES), 1) < A_HEAD_DIM

        def per_head_rows(ci, carry):
            rows = pl.ds(pl.multiple_of(ci * 256, 256), 256)
            h0 = lax.broadcasted_iota(jnp.int32, (256, LANES), 1) < A_HEAD_DIM
            tt = do_ref[rows, :] * y_ref[rows, :]
            d0n_s[rows, :] = jnp.broadcast_to(jnp.sum(jnp.where(h0, tt, 0.0), axis=-1, keepdims=True), (256, LANES))
            d1n_s[rows, :] = jnp.broadcast_to(jnp.sum(jnp.where(h0, 0.0, tt), axis=-1, keepdims=True), (256, LANES))
            l = lse_ref[rows, :]
            lr = pltpu.roll(l, A_HEAD_DIM, 1)
            l0n_s[rows, :] = jnp.where(h0, l, lr)
            l1n_s[rows, :] = jnp.where(h0, lr, l)
            return carry

        lax.fori_loop(0, SEQ // 256, per_head_rows, 0)
        assert DILATIONS[0] == 1

        for g, d in enumerate(DILATIONS):
            qd_s, kd_s, vd_s = qkv_refs[3 * g:3 * g + 3]
            _deinterleave(do_ref, dod_s, d, BF16, tmp_s)
            if d > 1:
                for src, dst in ((l0n_s, l0d_s), (l1n_s, l1d_s), (d0n_s, d0d_s), (d1n_s, d1d_s)):
                    _deinterleave(src, dst, d, F32, tmp_s)
            l0, l1, d0, d1 = (l0n_s, l1n_s, d0n_s, d1n_s) if d == 1 else (l0d_s, l1d_s, d0d_s, d1d_s)
            dq_dst, dk_dst, dv_dst = (dqn_s, dkn_s, dvn_s) if d == 1 else (dqc_s, dkc_s, dvc_s)
            dk_dst[...] = jnp.zeros_like(dk_dst)
            dv_dst[...] = jnp.zeros_like(dv_dst)

            def block(idx, carry, d=d, l0=l0, l1=l1, d0=d0, d1=d1, dq_dst=dq_dst, dk_dst=dk_dst, dv_dst=dv_dst,
                      qd_s=qd_s, kd_s=kd_s, vd_s=vd_s):
                q0, k0, kwin, valid = _block_coords(idx, d)
                qrows = pl.ds(q0, Q_BLK)
                krows = pl.ds(k0, kwin)
                qb, dob = qd_s[qrows, :], dod_s[qrows, :]
                kb, vb = kd_s[krows, :], vd_s[krows, :]
                zero = jnp.zeros_like(qb)
                q2 = jnp.concatenate([jnp.where(head0, qb, zero), jnp.where(head0, zero, qb)], 0)
                do2 = jnp.concatenate([jnp.where(head0, dob, zero), jnp.where(head0, zero, dob)], 0)
                wide = lambda x: jnp.concatenate([x] * (kwin // LANES), 1)
                lse2 = wide(jnp.concatenate([l0[qrows, :], l1[qrows, :]], 0))
                dd2 = wide(jnp.concatenate([d0[qrows, :], d1[qrows, :]], 0))
                s = jnp.where(jnp.concatenate([valid, valid], 0), _dot_nt(q2, kb), NEG_INF)
                p = jnp.exp(s - lse2)
                ds = (p * (_dot_nt(do2, vb) - dd2)).astype(BF16)
                dq2 = _dot(ds, kb)
                dq_dst[qrows, :] = jnp.where(head0, dq2[:Q_BLK], dq2[Q_BLK:])
                dk_dst[krows, :] += _dot_tn(ds, q2)
                dv_dst[krows, :] += _dot_tn(p.astype(BF16), do2)
                return carry

            lax.fori_loop(0, SEQ // Q_BLK, block, 0, unroll=UNROLL_BWD)
            if d > 1:
                _interleave(dqc_s, dqn_s, d, tmp_s, True)
                _interleave(dkc_s, dkn_s, d, tmp_s, True)
                _interleave(dvc_s, dvn_s, d, tmp_s, True)

        dq_ref[...] = _rope_bwd(dqn_s[...] * (A_HEAD_DIM ** -0.5), c, s_a, s_b, A_ROT // 2).astype(BF16)
        dk_ref[...] = _rope_bwd(dkn_s[...], c, s_a, s_b, A_ROT // 2).astype(BF16)
        dv_ref[...] = dvn_s[...].astype(BF16)

    tab = pl.BlockSpec((SEQ, LANES), lambda b, hp: (b, 0))
    blk = pl.BlockSpec((SEQ, LANES), lambda b, hp: (b, hp))
    f32s = pltpu.VMEM((SEQ, LANES), F32)
    b16s = pltpu.VMEM((SEQ, LANES), BF16)
    return call_hosting_exchange(
        body, xch, grid=(nb, n_pairs),
        in_specs=[blk] * 9 + [tab, tab, tab, blk, blk, blk],
        out_specs=[blk, blk, blk],
        out_shape=[SDS((t, A_WIDTH), BF16)] * 3,
        scratch_shapes=[f32s] * 4 + [b16s] + [f32s] * 11,
        name="a_attn_bwd", operands=(*qkv_d, ca, sa, sb, dy, y, lse))


MLA_SCALE = (MLA_NOPE + MLA_ROPE) ** -0.5
MLA_QW = MLA_HEADS * LANES
MLA_KVW = MLA_QW + MLA_WIDTH


def _rms(x, g):
    r = lax.rsqrt(jnp.mean(x * x, axis=-1, keepdims=True) + NORM_EPS)
    return x * r * g, r


def _rms_bwd(dn, x, r, g):
    tg = dn * g
    dx = r * tg - x * (r * r * r) * jnp.mean(tg * x, axis=-1, keepdims=True)
    return dx, jnp.sum(dn * x * r, axis=0, keepdims=True)


def mla_prep_fwd(proj, cm, sma, smb, g_cq, g_ckv, wuq, wkv):
    t = proj.shape[0]
    tm = 512

    def body(cq_ref, ckv_ref, kr_ref, c_ref, sa_ref, sb_ref, gq_ref, gkv_ref, wuq_ref, wkv_ref, q_ref, k_ref, v_ref):
        c, s_a, s_b = c_ref[...], sa_ref[...], sb_ref[...]
        cqn, _ = _rms(cq_ref[...], gq_ref[...])
        qf = _dot(cqn.astype(BF16), wuq_ref[...])
        ckvn, _ = _rms(ckv_ref[...], gkv_ref[...])
        kvf = _dot(ckvn.astype(BF16), wkv_ref[...])
        krope = _rope_fwd(kr_ref[...], c, s_a, s_b, MLA_ROPE // 2)
        for h in range(MLA_HEADS):
            cols = slice(h * LANES, (h + 1) * LANES)
            q_ref[:, cols] = (_rope_fwd(qf[:, cols], c, s_a, s_b, MLA_ROPE // 2) * MLA_SCALE).astype(BF16)
            k_ref[:, cols] = (kvf[:, cols] + krope).astype(BF16)
        v_ref[...] = kvf[:, MLA_QW:].astype(BF16)

    def row(w, j):
        return pl.BlockSpec((tm, w), lambda i: (i, j))

    def full(a):
        return pl.BlockSpec(a.shape, lambda i: (0, 0))

    return pl.pallas_call(
        body, grid=(t // tm,),
        in_specs=[row(256, 4096 // 256), row(128, 4352 // 128), row(128, 4480 // 128), row(128, 0), row(128, 0), row(128, 0),
                  full(g_cq), full(g_ckv), full(wuq), full(wkv)],
        out_specs=[row(MLA_QW, 0), row(MLA_QW, 0), row(MLA_WIDTH, 0)],
        out_shape=[SDS((t, MLA_QW), BF16), SDS((t, MLA_QW), BF16), SDS((t, MLA_WIDTH), BF16)],
        name="mla_prep_fwd", compiler_params=_params("parallel"))(proj, proj, proj, cm, sma, smb, g_cq, g_ckv, wuq, wkv)


def mla_prep_bwd(proj, cm, sma, smb, g_cq, g_ckv, wuq, wkv, dq, dk, dv):
    t = proj.shape[0]
    tm = 512

    def body(cq_ref, ckv_ref, c_ref, sa_ref, sb_ref, gq_ref, gkv_ref, wuq_ref, wkv_ref, dq_ref, dk_ref, dv_ref,
             dcc_ref, dqf_ref, cqn_ref, dkvf_ref, ckvn_ref, dgq_ref, dgkv_ref):
        @pl.when(pl.program_id(0) == 0)
        def _():
            dgq_ref[...] = jnp.zeros_like(dgq_ref)
            dgkv_ref[...] = jnp.zeros_like(dgkv_ref)

        c, s_a, s_b = c_ref[...], sa_ref[...], sb_ref[...]
        cq, ckv = cq_ref[...], ckv_ref[...]
        cqn, rq = _rms(cq, gq_ref[...])
        ckvn, rkv = _rms(ckv, gkv_ref[...])
        cqn_ref[...] = cqn.astype(BF16)
        ckvn_ref[...] = ckvn.astype(BF16)
        lane = lax.broadcasted_iota(jnp.int32, (tm, LANES), 1)
        rope_lanes = (lane >= MLA_NOPE) & (lane < MLA_NOPE + MLA_ROPE)
        dkrope = jnp.zeros((tm, LANES), F32)
        for h in range(MLA_HEADS):
            cols = slice(h * LANES, (h + 1) * LANES)
            dqf_ref[:, cols] = _rope_bwd(dq_ref[:, cols] * MLA_SCALE, c, s_a, s_b, MLA_ROPE // 2).astype(BF16)
            dkh = dk_ref[:, cols]
            dkvf_ref[:, cols] = dkh.astype(BF16)
            dkrope = dkrope + dkh
        dkvf_ref[:, MLA_QW:] = dv_ref[...].astype(BF16)
        dkr = _rope_bwd(jnp.where(rope_lanes, dkrope, 0.0), c, s_a, s_b, MLA_ROPE // 2)
        dcqn = _dot_nt(dqf_ref[...], wuq_ref[...])
        dckvn = _dot_nt(dkvf_ref[...], wkv_ref[...])
        dcq, dgq = _rms_bwd(dcqn, cq, rq, gq_ref[...])
        dckv, dgkv = _rms_bwd(dckvn, ckv, rkv, gkv_ref[...])
        dgq_ref[...] += dgq
        dgkv_ref[...] += dgkv
        dcc_ref[:, 0:256] = dcq.astype(BF16)
        dcc_ref[:, 256:384] = dckv.astype(BF16)
        dcc_ref[:, 384:512] = dkr.astype(BF16)

    def row(w, j):
        return pl.BlockSpec((tm, w), lambda i: (i, j))

    def full(a):
        return pl.BlockSpec(a.shape, lambda i: (0, 0))

    return pl.pallas_call(
        body, grid=(t // tm,),
        in_specs=[row(256, 4096 // 256), row(128, 4352 // 128), row(128, 0), row(128, 0), row(128, 0),
                  full(g_cq), full(g_ckv), full(wuq), full(wkv), row(MLA_QW, 0), row(MLA_QW, 0), row(MLA_WIDTH, 0)],
        out_specs=[row(512, 0), row(MLA_QW, 0), row(256, 0), row(MLA_KVW, 0), row(128, 0), full(g_cq), full(g_ckv)],
        out_shape=[SDS((t, 512), BF16), SDS((t, MLA_QW), BF16), SDS((t, 256), BF16), SDS((t, MLA_KVW), BF16),
                   SDS((t, 128), BF16), SDS(g_cq.shape, F32), SDS(g_ckv.shape, F32)],
        name="mla_prep_bwd", compiler_params=_params("arbitrary"))(proj, proj, cm, sma, smb, g_cq, g_ckv, wuq, wkv, dq, dk, dv)


MLA_TQ = 256


def mla_attn_fwd(qb, kb, vb, nb):
    t = qb.shape[0]
    nq = SEQ // MLA_TQ
    n_pairs = MLA_HEADS // 2

    def body(q_ref, k_ref, v_ref, y_ref, lse_ref):
        head0 = lax.broadcasted_iota(jnp.int32, (MLA_TQ, LANES), 1) < MLA_V
        v = v_ref[...]
        outs, lses = [], []
        for h in range(2):
            cols = slice(h * LANES, (h + 1) * LANES)
            s = _dot_nt(q_ref[:, cols], k_ref[:, cols])
            m = jnp.max(s, axis=-1, keepdims=True)
            p = jnp.exp(s - m)
            l = jnp.sum(p, axis=-1, keepdims=True)
            outs.append(_dot(p.astype(BF16), v) / l)
            lses.append(m + jnp.log(l))
        y_ref[...] = jnp.where(head0, outs[0], outs[1])
        lse_ref[...] = jnp.where(head0, lses[0], lses[1])

    return pl.pallas_call(
        body, grid=(nb, n_pairs, nq),
        in_specs=[pl.BlockSpec((MLA_TQ, 2 * LANES), lambda b, hp, i: (b * nq + i, hp)),
                  pl.BlockSpec((SEQ, 2 * LANES), lambda b, hp, i: (b, hp)),
                  pl.BlockSpec((SEQ, LANES), lambda b, hp, i: (b, hp))],
        out_specs=[pl.BlockSpec((MLA_TQ, LANES), lambda b, hp, i: (b * nq + i, hp))] * 2,
        out_shape=[SDS((t, MLA_WIDTH), F32)] * 2,
        name="mla_attn_fwd", compiler_params=_params("parallel", "parallel", "parallel"))(qb, kb, vb)


def mla_attn_bwd(qb, kb, vb, dy, y, lse, nb, xch):
    t = qb.shape[0]
    nq = SEQ // MLA_TQ
    n_pairs = MLA_HEADS // 2

    def body(q_ref, k_ref, v_ref, do_ref, y_ref, lse_ref, dq_ref, dk_ref, dv_ref):
        @pl.when(pl.program_id(2) == 0)
        def _():
            dk_ref[...] = jnp.zeros_like(dk_ref)
            dv_ref[...] = jnp.zeros_like(dv_ref)

        head0 = lax.broadcasted_iota(jnp.int32, (MLA_TQ, LANES), 1) < MLA_V
        v = v_ref[...]
        do = do_ref[...]
        lse = lse_ref[...]
        tt = do * y_ref[...]
        dv = jnp.zeros((SEQ, LANES), F32)
        for h in range(2):
            sel = head0 if h == 0 else ~head0
            lo = h * MLA_V
            cols = slice(h * LANES, (h + 1) * LANES)
            q = q_ref[:, cols]
            k = k_ref[:, cols]
            dd = jnp.sum(jnp.where(sel, tt, 0.0), axis=-1, keepdims=True)
            doh = jnp.where(sel, do, 0.0).astype(BF16)
            p = jnp.exp(_dot_nt(q, k) - lse[:, lo:lo + 1])
            dp = _dot_nt(doh, v)
            ds = (p * (dp - dd)).astype(BF16)
            dq_ref[:, cols] = _dot(ds, k)
            dk_ref[:, cols] += _dot_tn(ds, q)
            dv = dv + _dot_tn(p.astype(BF16), doh)
        dv_ref[...] += dv

    qspec = pl.BlockSpec((MLA_TQ, 2 * LANES), lambda b, hp, i: (b * nq + i, hp))
    kspec = pl.BlockSpec((SEQ, 2 * LANES), lambda b, hp, i: (b, hp))
    vspec = pl.BlockSpec((SEQ, LANES), lambda b, hp, i: (b, hp))
    ospec = pl.BlockSpec((MLA_TQ, LANES), lambda b, hp, i: (b * nq + i, hp))
    return call_hosting_exchange(
        body, xch, grid=(nb, n_pairs, nq),
        in_specs=[qspec, kspec, vspec, ospec, ospec, ospec],
        out_specs=[qspec, kspec, vspec],
        out_shape=[SDS((t, MLA_QW), F32), SDS((t, MLA_QW), F32), SDS((t, MLA_WIDTH), F32)],
        scratch_shapes=[], name="mla_attn_bwd", operands=(qb, kb, vb, dy, y, lse))


MEM_TQ = 512
MEM_SCALE = MEM_HEAD_DIM ** -0.5
MQ_BLK = 5120 // LANES


def mem_attn_fwd(proj, mkv, nb):
    t = proj.shape[0]
    nq = SEQ // MEM_TQ

    def body(q_ref, mk_ref, mv_ref, y_ref):
        s = _dot_nt(q_ref[...].astype(BF16), mk_ref[...]) * MEM_SCALE
        m = jnp.max(s, axis=-1, keepdims=True)
        p = jnp.exp(s - m)
        l = jnp.sum(p, axis=-1, keepdims=True)
        y_ref[...] = _dot(p.astype(BF16), mv_ref[...]) / l

    return pl.pallas_call(
        body, grid=(nb, MEM_HEADS, nq),
        in_specs=[pl.BlockSpec((MEM_TQ, LANES), lambda b, h, i: (b * nq + i, MQ_BLK + h)),
                  pl.BlockSpec((N_MEM, LANES), lambda b, h, i: (b, h)),
                  pl.BlockSpec((N_MEM, LANES), lambda b, h, i: (b, MEM_HEADS + h))],
        out_specs=pl.BlockSpec((MEM_TQ, LANES), lambda b, h, i: (b * nq + i, h)),
        out_shape=SDS((t, MEM_WIDTH), F32),
        name="mem_attn_fwd", compiler_params=_params("parallel", "parallel", "parallel"))(proj, mkv, mkv)


def mem_attn_bwd(proj, mkv, dy, nb):
    t = proj.shape[0]
    nq = SEQ // MEM_TQ

    def body(q_ref, mk_ref, mv_ref, do_ref, dq_ref, dmk_ref, dmv_ref):
        @pl.when(pl.program_id(2) == 0)
        def _():
            dmk_ref[...] = jnp.zeros_like(dmk_ref)
            dmv_ref[...] = jnp.zeros_like(dmv_ref)

        q = q_ref[...].astype(BF16)
        mk, mv = mk_ref[...], mv_ref[...]
        do = do_ref[...].astype(BF16)
        s = _dot_nt(q, mk) * MEM_SCALE
        e = jnp.exp(s - jnp.max(s, axis=-1, keepdims=True))
        p = e / jnp.sum(e, axis=-1, keepdims=True)
        dp = _dot_nt(do, mv)
        ds = (p * (dp - jnp.sum(p * dp, axis=-1, keepdims=True)) * MEM_SCALE).astype(BF16)
        dq_ref[...] = _dot(ds, mk).astype(BF16)
        dmk_ref[...] += _dot_tn(ds, q)
        dmv_ref[...] += _dot_tn(p.astype(BF16), do)

    ospec = pl.BlockSpec((MEM_TQ, LANES), lambda b, h, i: (b * nq + i, h))
    kspec = pl.BlockSpec((N_MEM, LANES), lambda b, h, i: (b, h))
    return pl.pallas_call(
        body, grid=(nb, MEM_HEADS, nq),
        in_specs=[pl.BlockSpec((MEM_TQ, LANES), lambda b, h, i: (b * nq + i, MQ_BLK + h)),
                  kspec, pl.BlockSpec((N_MEM, LANES), lambda b, h, i: (b, MEM_HEADS + h)), ospec],
        out_specs=[ospec, kspec, kspec],
        out_shape=[SDS((t, MEM_WIDTH), BF16), SDS((nb * N_MEM, MEM_WIDTH), F32), SDS((nb * N_MEM, MEM_WIDTH), F32)],
        name="mem_attn_bwd", compiler_params=_params("parallel", "parallel", "arbitrary"))(proj, mkv, mkv, dy)


ROW_TM = 256
AG_BLK = 3072 // 1024
BG_BLK = 4608 // 512
MG_BLK = 5632 // 512
GROUPS = ((0, A_WIDTH), (A_WIDTH, MLA_WIDTH), (A_WIDTH + MLA_WIDTH, MEM_WIDTH))
D_MIX = 2048


def _gate_specs():
    def row(w, j):
        return pl.BlockSpec((ROW_TM, w), lambda i: (i, j))

    def vec(w):
        return pl.BlockSpec((1, w), lambda i: (0, 0))

    ys = [row(A_WIDTH, 0), row(MLA_WIDTH, 0), row(MEM_WIDTH, 0)]
    gates = [row(A_WIDTH, AG_BLK), row(MLA_WIDTH, BG_BLK), row(MEM_WIDTH, MG_BLK)]
    gains = [vec(A_WIDTH), vec(MLA_WIDTH), vec(MEM_WIDTH)]
    return row, vec, ys, gates, gains


def gate_out_ln_loss(ya, yb, ym, proj, goa, gob, gom, wout, h32, target, gp, bp):
    t, d = h32.shape
    _, _, ys, gates, gains = _gate_specs()

    def body(ya_ref, yb_ref, ym_ref, ga_ref, gb_ref, gm_ref, goa_ref, gob_ref, gom_ref, w_ref, h_ref, t_ref, gp_ref, bp_ref,
             z_ref, du32_ref, du16_ref, loss_ref, dgp_ref, dbp_ref):
        @pl.when(pl.program_id(0) == 0)
        def _():
            loss_ref[...] = jnp.zeros_like(loss_ref)
            dgp_ref[...] = jnp.zeros_like(dgp_ref)
            dbp_ref[...] = jnp.zeros_like(dbp_ref)

        for (off, w), y_ref, g_ref, go_ref in zip(GROUPS, (ya_ref, yb_ref, ym_ref), (ga_ref, gb_ref, gm_ref),
                                                  (goa_ref, gob_ref, gom_ref)):
            n, _ = _rms(y_ref[...], go_ref[...])
            gt = g_ref[...]
            z_ref[:, off:off + w] = (n * (gt * _sigmoid(gt))).astype(BF16)
        g = gp_ref[...]
        u = ALPHA * h_ref[...] + _dot(z_ref[...], w_ref[...])
        mu = jnp.mean(u, axis=-1, keepdims=True)
        uc = u - mu
        rstd = lax.rsqrt(jnp.mean(uc * uc, axis=-1, keepdims=True) + NORM_EPS)
        xhat = uc * rstd
        err = xhat * g + bp_ref[...] - t_ref[...]
        tok = jnp.sum(err * err, axis=-1, keepdims=True) * (1.0 / d)
        loss_ref[...] += 0.5 * jnp.sum(tok, axis=0, keepdims=True)
        dout = err * (1.0 / d)
        dxhat = dout * g
        du = rstd * (dxhat - jnp.mean(dxhat, axis=-1, keepdims=True)
                     - xhat * jnp.mean(dxhat * xhat, axis=-1, keepdims=True))
        du32_ref[...] = du
        du16_ref[...] = du.astype(BF16)
        dgp_ref[...] += jnp.sum(dout * xhat, axis=0, keepdims=True)
        dbp_ref[...] += jnp.sum(dout, axis=0, keepdims=True)

    row = pl.BlockSpec((ROW_TM, d), lambda i: (i, 0))
    vec = pl.BlockSpec((1, d), lambda i: (0, 0))
    zrow = pl.BlockSpec((ROW_TM, D_MIX), lambda i: (i, 0))
    return pl.pallas_call(
        body, grid=(t // ROW_TM,),
        in_specs=ys + gates + gains + [pl.BlockSpec((D_MIX, d), lambda i: (0, 0)), row, row, vec, vec],
        out_specs=[zrow, row, row, pl.BlockSpec((1, LANES), lambda i: (0, 0)), vec, vec],
        out_shape=[SDS((t, D_MIX), BF16), SDS((t, d), F32), SDS((t, d), BF16), SDS((1, LANES), F32), SDS((1, d), F32),
                   SDS((1, d), F32)],
        name="gate_out_ln_loss", compiler_params=_params("arbitrary"))(
            ya, yb, ym, proj, proj, proj, goa, gob, gom, wout, h32, target, gp, bp)


def gate_bwd(du16, wout, ya, yb, ym, proj, goa, gob, gom):
    t = ya.shape[0]
    row, vec, ys, gates, gains = _gate_specs()

    def body(du_ref, w_ref, ya_ref, yb_ref, ym_ref, ga_ref, gb_ref, gm_ref, goa_ref, gob_ref, gom_ref,
             dya_ref, dyb_ref, dym_ref, dga_ref, dgb_ref, dgm_ref, dgoa_ref, dgob_ref, dgom_ref):
        @pl.when(pl.program_id(0) == 0)
        def _():
            dgoa_ref[...] = jnp.zeros_like(dgoa_ref)
            dgob_ref[...] = jnp.zeros_like(dgob_ref)
            dgom_ref[...] = jnp.zeros_like(dgom_ref)

        dz = _dot_nt(du_ref[...], w_ref[...])
        for (off, w), y_ref, g_ref, go_ref, dy_ref, dg_ref, dgo_ref in zip(
                GROUPS, (ya_ref, yb_ref, ym_ref), (ga_ref, gb_ref, gm_ref), (goa_ref, gob_ref, gom_ref),
                (dya_ref, dyb_ref, dym_ref), (dga_ref, dgb_ref, dgm_ref), (dgoa_ref, dgob_ref, dgom_ref)):
            dzg = dz[:, off:off + w]
            y, gt, go = y_ref[...], g_ref[...], go_ref[...]
            n, r = _rms(y, go)
            sg = _sigmoid(gt)
            dg_ref[...] = (dzg * n * (sg * (1.0 + gt * (1.0 - sg)))).astype(BF16)
            dy, dgo = _rms_bwd(dzg * (gt * sg), y, r, go)
            dy_ref[...] = dy
            dgo_ref[...] += dgo

    widths = (A_WIDTH, MLA_WIDTH, MEM_WIDTH)
    return pl.pallas_call(
        body, grid=(t // ROW_TM,),
        in_specs=[row(D_MODEL, 0), pl.BlockSpec((D_MIX, D_MODEL), lambda i: (0, 0))] + ys + gates + gains,
        out_specs=[row(w, 0) for w in widths] * 2 + [vec(w) for w in widths],
        out_shape=[SDS((t, w), F32) for w in widths] + [SDS((t, w), BF16) for w in widths] + [SDS((1, w), F32) for w in widths],
        name="gate_bwd", compiler_params=_params("arbitrary"))(du16, wout, ya, yb, ym, proj, proj, proj, goa, gob, gom)


def dh_ln_bwd(pieces, win_t, du32, x2, g_emb, xch):
    t, d = x2.shape

    def body(*refs):
        p_refs = refs[:len(pieces)]
        w_ref, du_ref, x_ref, g_ref, dx_ref, dg_ref, db_ref = refs[len(pieces):]

        @pl.when(pl.program_id(0) == 0)
        def _():
            dg_ref[...] = jnp.zeros_like(dg_ref)
            db_ref[...] = jnp.zeros_like(db_ref)

        dh = ALPHA * du_ref[...]
        for p_ref, off, w in zip(p_refs, PIECE_OFFS, PIECE_WIDTHS):
            dh = dh + _dot(p_ref[...], w_ref[off:off + w, :])
        x = x_ref[...]
        xc = x - jnp.mean(x, axis=-1, keepdims=True)
        rstd = lax.rsqrt(jnp.mean(xc * xc, axis=-1, keepdims=True) + NORM_EPS)
        xhat = xc * rstd
        dg_ref[...] += jnp.sum(dh * xhat, axis=0, keepdims=True)
        db_ref[...] += jnp.sum(dh, axis=0, keepdims=True)
        tg = dh * g_ref[...]
        dx_ref[...] = rstd * (tg - jnp.mean(tg, axis=-1, keepdims=True)
                              - xhat * jnp.mean(tg * xhat, axis=-1, keepdims=True))

    row = pl.BlockSpec((ROW_TM, d), lambda i: (i, 0))
    vec = pl.BlockSpec((1, d), lambda i: (0, 0))
    return call_hosting_exchange(
        body, xch, grid=(t // ROW_TM,),
        in_specs=[pl.BlockSpec((ROW_TM, w), lambda i: (i, 0)) for w in PIECE_WIDTHS]
        + [pl.BlockSpec(win_t.shape, lambda i: (0, 0)), row, row, vec],
        out_specs=[row, vec, vec],
        out_shape=[SDS((t, d), F32), SDS((1, d), F32), SDS((1, d), F32)],
        scratch_shapes=[], name="dh_ln_bwd", operands=(*pieces, win_t, du32, x2, g_emb))


def _adamw(w, g, m, v):
    m2 = ADAM_B1 * m + (1.0 - ADAM_B1) * g
    v2 = ADAM_B2 * v + (1.0 - ADAM_B2) * (g * g)
    m_hat = m2 / (1.0 - ADAM_B1 ** ADAM_STEP)
    v_hat = v2 / (1.0 - ADAM_B2 ** ADAM_STEP)
    return -ADAM_LR * (m_hat / (jnp.sqrt(v_hat) + ADAM_EPS) + ADAM_WD * w), m2, v2


def adamw_shard(w, parts, m, v, name):
    r, c = w.shape
    if r % 256 == 0 or r * c <= 256 * 1024:
        tr, tc = min(r, 256), c
    else:
        tr, tc = r, 256

    def body(w_ref, p_ref, m_ref, v_ref, g_ref, d_ref, nm_ref, nv_ref):
        g = p_ref[0].astype(F32)
        for k in range(1, N_DEV):
            g = g + p_ref[k].astype(F32)
        g_ref[...] = g
        d_ref[...], nm_ref[...], nv_ref[...] = _adamw(w_ref[...], g, m_ref[...], v_ref[...])

    blk = pl.BlockSpec((tr, tc), lambda i, j: (i, j))
    return pl.pallas_call(
        body, grid=(r // tr, c // tc),
        in_specs=[blk, pl.BlockSpec((N_DEV, tr, tc), lambda i, j: (0, i, j)), blk, blk],
        out_specs=[blk] * 4, out_shape=[SDS((r, c), F32)] * 4, name=name,
        compiler_params=_params("parallel", "parallel"))(w, parts, m, v)


def _place():
    return lax.axis_index("x"), lax.axis_index("y"), lax.axis_index("c")


def _flat(px, py, pc):
    return 4 * px + 2 * py + pc


def _peer(x, y, c, k):
    return (1 - x if k & 4 else x, 1 - y if k & 2 else y, 1 - c if k & 1 else c)


def cast_shards(shards):
    def body(*refs):
        n = len(refs) // 2
        for i_ref, o_ref in zip(refs[:n], refs[n:]):
            o_ref[...] = i_ref[...].astype(BF16)

    return pl.pallas_call(body, out_shape=[SDS(s.shape, BF16) for s in shards], name="cast_shards",
                          compiler_params=_params())(*shards)


def allgather_weights(shards):
    n = len(shards)

    def body(*refs):
        ins, outs = refs[:n], refs[n:2 * n]
        send_sems, recv_sems, local_sems = refs[2 * n:]
        x, y, c = _place()
        me, sib = (x, y, c), (x, y, 1 - c)
        chips = [(1 - x, y), (x, 1 - y), (1 - x, 1 - y)]

        def copy(a, k, block, to, src=None):
            dst = outs[a].at[_flat(*block)]
            return pltpu.make_async_remote_copy(
                src_ref=dst if src is None else src, dst_ref=dst,
                send_sem=send_sems.at[a * 7 + k], recv_sem=recv_sems.at[a * 7 + k],
                device_id=to, device_id_type=MESH)

        mine = [pltpu.make_async_copy(ins[a], outs[a].at[_flat(*me)], local_sems.at[a]) for a in range(n)]
        for cp in mine:
            cp.start()
        first = []
        for a in range(n):
            first.append(copy(a, 0, me, sib, src=ins[a]))
            first += [copy(a, 1 + j, me, (*chip, c), src=ins[a]) for j, chip in enumerate(chips)]
        for cp in first:
            cp.start()
        passed = []
        for j, chip in enumerate(chips):
            for a in range(n):
                copy(a, 1 + j, (*chip, c), me).wait_recv()
                fwd = copy(a, 4 + j, (*chip, c), sib)
                fwd.start()
                passed.append(fwd)
        for a in range(n):
            copy(a, 0, sib, me).wait_recv()
            for j, chip in enumerate(chips):
                copy(a, 4 + j, (*chip, 1 - c), me).wait_recv()
        for cp in first + passed:
            cp.wait_send()
        for cp in mine:
            cp.wait()

    hbm = pl.BlockSpec(memory_space=pl.ANY)
    return pl.pallas_call(
        body, out_shape=[SDS((N_DEV,) + s.shape, s.dtype) for s in shards],
        in_specs=[hbm] * n, out_specs=[hbm] * n,
        scratch_shapes=[pltpu.SemaphoreType.DMA((7 * n,)), pltpu.SemaphoreType.DMA((7 * n,)), pltpu.SemaphoreType.DMA((n,))],
        name="allgather_weights", compiler_params=_params())(*shards)


ALL_DEVICES = tuple(range(N_DEV))


def _exchange_plan(src_refs, land_refs, dests, send_sems, recv_sems, local_sems):
    x, y, c = _place()
    me = _flat(x, y, c)
    plan = []
    for a, (src, land, dl) in enumerate(zip(src_refs, land_refs, dests)):
        for li, j in enumerate(dl):
            to = ((j >> 2) & 1, (j >> 1) & 1, j & 1)
            block = src.at[li] if len(src.shape) == len(land.shape) else src

            def push(slot, a=a, block=block, land=land, j=j, to=to):
                return pltpu.make_async_remote_copy(
                    src_ref=block, dst_ref=land.at[slot], send_sem=send_sems.at[a * N_DEV + j],
                    recv_sem=recv_sems.at[a * N_DEV + slot], device_id=to, device_id_type=MESH)

            own = pltpu.make_async_copy(block, land.at[j], local_sems.at[a])
            plan.append((j, push(me), own, [push(s) for s in range(N_DEV) if s != j]))
    return me, plan


def _exchange_start(me, plan):
    for j, send, own, _ in plan:
        @pl.when(me != j)
        def _(send=send):
            send.start()

        @pl.when(me == j)
        def _(own=own):
            own.start()


def _exchange_wait(me, plan):
    for j, send, own, arrivals in plan:
        @pl.when(me != j)
        def _(send=send):
            send.wait_send()

        @pl.when(me == j)
        def _(own=own, arrivals=arrivals):
            own.wait()
            for arrival in arrivals:
                arrival.wait_recv()


def call_hosting_exchange(core, xch, *, grid, in_specs, out_specs, out_shape, scratch_shapes, name, operands):
    srcs, dests, landing = xch
    n, n_in, n_out, n_scr = len(srcs), len(in_specs), len(out_specs), len(scratch_shapes)

    def body(*refs):
        ins, src_refs = refs[:n_in], refs[n_in:n_in + n]
        outs = refs[n_in + 2 * n:n_in + 2 * n + n_out]
        land_refs = refs[n_in + 2 * n + n_out:n_in + 3 * n + n_out]
        scratch = refs[n_in + 3 * n + n_out:n_in + 3 * n + n_out + n_scr]
        sems = refs[n_in + 3 * n + n_out + n_scr:]
        first = functools.reduce(jnp.logical_and, [pl.program_id(i) == 0 for i in range(len(grid))])
        last = functools.reduce(jnp.logical_and, [pl.program_id(i) == grid[i] - 1 for i in range(len(grid))])
        me, plan = _exchange_plan(src_refs, land_refs, dests, *sems)

        @pl.when(first)
        def _():
            _exchange_start(me, plan)

        core(*ins, *outs, *scratch)

        @pl.when(last)
        def _():
            _exchange_wait(me, plan)

    hbm = pl.BlockSpec(memory_space=pl.ANY)
    res = pl.pallas_call(
        body, grid=grid,
        in_specs=list(in_specs) + [hbm] * (2 * n), out_specs=list(out_specs) + [hbm] * n,
        out_shape=list(out_shape) + [SDS(l.shape, l.dtype) for l in landing],
        scratch_shapes=list(scratch_shapes) + [pltpu.SemaphoreType.DMA((N_DEV * n,)), pltpu.SemaphoreType.DMA((N_DEV * n,)),
                                               pltpu.SemaphoreType.DMA((n,))],
        input_output_aliases={n_in + n + k: n_out + k for k in range(n)},
        name=name, compiler_params=_params(*(("arbitrary",) * len(grid))))(*operands, *srcs, *landing)
    return res[:n_out], res[n_out:]


SLOT_ROWS = 8


def small_allreduce_adamw(loss_sum, grads, ws, ms, vs):
    n = len(grads)
    rows = [g.shape[0] for g in grads]
    total = SLOT_ROWS * (n + 1)

    def body(*refs):
        loss_ref, g_refs, w_refs = refs[0], refs[1:1 + n], refs[1 + n:1 + 2 * n]
        m_refs, v_refs = refs[1 + 2 * n:1 + 3 * n], refs[1 + 3 * n:1 + 4 * n]
        outs = refs[1 + 4 * n:2 + 8 * n]
        vec, gath, tot, send_sems, recv_sems = refs[2 + 8 * n:]
        x, y, c = _place()
        me = _flat(x, y, c)
        vec[...] = jnp.zeros_like(vec)
        vec[0:1, :] = loss_ref[...]
        for i in range(n):
            vec[SLOT_ROWS * (i + 1):SLOT_ROWS * (i + 1) + rows[i], :] = g_refs[i][...]
        gath[me] = vec[...]
        copies = []
        for k in range(1, N_DEV):
            peer = _peer(x, y, c, k)
            copies.append(pltpu.make_async_remote_copy(
                src_ref=vec, dst_ref=gath.at[me], send_sem=send_sems.at[k - 1], recv_sem=recv_sems.at[k - 1],
                device_id=peer, device_id_type=MESH))
        for cp in copies:
            cp.start()
        for cp in copies:
            cp.wait_recv()
        for cp in copies:
            cp.wait_send()
        g = gath[0]
        for j in range(1, N_DEV):
            g = g + gath[j]
        tot[...] = g
        outs[0][...] = tot[0:1, :]
        for i in range(n):
            gi = tot[SLOT_ROWS * (i + 1):SLOT_ROWS * (i + 1) + rows[i], :]
            outs[1 + i][...] = gi
            outs[1 + n + i][...], outs[1 + 2 * n + i][...], outs[1 + 3 * n + i][...] = _adamw(
                w_refs[i][...], gi, m_refs[i][...], v_refs[i][...])

    shapes = [SDS(g.shape, F32) for g in grads]
    return pl.pallas_call(
        body, out_shape=[SDS((1, LANES), F32)] + shapes * 4,
        scratch_shapes=[pltpu.VMEM((total, LANES), F32), pltpu.VMEM((N_DEV, total, LANES), F32), pltpu.VMEM((total, LANES), F32),
                        pltpu.SemaphoreType.DMA((7,)), pltpu.SemaphoreType.DMA((7,))],
        name="small_allreduce_adamw", compiler_params=_params())(loss_sum, *grads, *ws, *ms, *vs)


def _rope_tables(positions):
    pos = positions.astype(F32).reshape(-1, 1)

    def cs(r):
        inv_freq = ROPE_THETA ** (-(jnp.arange(0, r, 2, dtype=F32) / r))
        ang = pos * inv_freq
        return jnp.cos(ang), jnp.sin(ang)

    n = pos.shape[0]
    one = lambda w: jnp.ones((n, w), F32)
    zero = lambda w: jnp.zeros((n, w), F32)
    ca, sa = cs(A_ROT)
    rest = A_HEAD_DIM - A_ROT
    a_c = jnp.tile(jnp.concatenate([ca, ca, one(rest)], 1), (1, 2))
    a_sa = jnp.tile(jnp.concatenate([-sa, zero(A_ROT // 2 + rest)], 1), (1, 2))
    a_sb = jnp.tile(jnp.concatenate([zero(A_ROT // 2), sa, zero(rest)], 1), (1, 2))
    cm, sm = cs(MLA_ROPE)
    tail = LANES - MLA_NOPE - MLA_ROPE
    m_c = jnp.concatenate([one(MLA_NOPE), cm, cm, one(tail)], 1)
    m_sa = jnp.concatenate([zero(MLA_NOPE), -sm, zero(MLA_ROPE // 2 + tail)], 1)
    m_sb = jnp.concatenate([zero(MLA_NOPE + MLA_ROPE // 2), sm, zero(tail)], 1)
    return (a_c, a_sa, a_sb), (m_c, m_sa, m_sb)


KR_LO, KR_HI = 4480, 4512
W_IN_SHARD = D_IN // N_DEV
AG_SPLIT = 5 * W_IN_SHARD - 3 * A_WIDTH
BG_SPLIT = 6 * W_IN_SHARD - KR_HI


def _w_in_working_t(g):
    w = g.reshape(D_IN, D_MODEL)
    z = lambda n: jnp.zeros((n, D_MODEL), w.dtype)
    return jnp.concatenate([w[:KR_LO], z(MLA_NOPE), w[KR_LO:KR_HI], z(LANES - MLA_NOPE - MLA_ROPE), w[KR_HI:]], 0)


def _shards(rows):
    return rows.reshape(-1, W_IN_SHARD, D_MODEL).astype(BF16)


def _w_in_shards_0_4(d_aq, d_ak, d_av, d_ag):
    return _shards(jnp.concatenate([d_aq, d_ak, d_av, d_ag[:AG_SPLIT]], 0))


def _w_in_shard_5(d_ag, d_cc, d_bg):
    kr = MLA_Q_RANK + MLA_KV_RANK + MLA_NOPE
    return _shards(jnp.concatenate([d_ag[AG_SPLIT:], d_cc[:MLA_Q_RANK + MLA_KV_RANK], d_cc[kr:kr + MLA_ROPE], d_bg[:BG_SPLIT]], 0))


def _w_in_shards_6_7(d_bg, d_mq, d_mg):
    return _shards(jnp.concatenate([d_bg[BG_SPLIT:], d_mq, d_mg], 0))


def _w_uq_working(g):
    w = jnp.pad(g.transpose(1, 0, 2), ((0, 0), (0, 0), (0, LANES - MLA_NOPE - MLA_ROPE)))
    return w.reshape(MLA_Q_RANK, MLA_QW)


def _w_uq_parts(dw):
    return dw.reshape(MLA_Q_RANK, MLA_HEADS, LANES)[:, :, :MLA_NOPE + MLA_ROPE].transpose(1, 0, 2)


def _w_ukv_working(g):
    wk = jnp.pad(g[:, :, :MLA_NOPE].transpose(1, 0, 2), ((0, 0), (0, 0), (0, LANES - MLA_NOPE)))
    wv = g[:, :, MLA_NOPE:].transpose(1, 0, 2)
    return jnp.concatenate([wk.reshape(MLA_KV_RANK, MLA_QW), wv.reshape(MLA_KV_RANK, MLA_WIDTH)], 1)


def _w_ukv_parts(dw):
    dk = dw[:, :MLA_QW].reshape(MLA_KV_RANK, MLA_HEADS, LANES)[:, :, :MLA_NOPE]
    dv = dw[:, MLA_QW:].reshape(MLA_KV_RANK, MLA_HEADS, MLA_V)
    return jnp.concatenate([dk, dv], -1).transpose(1, 0, 2)


SMALL_NAMES = ("g_emb", "b_emb", "g_cq", "g_ckv", "g_out_a", "g_out_b", "g_out_m", "g_post", "b_post")


def kernel(x, mem, positions, g_emb, b_emb, w_in, g_cq, g_ckv, w_uq, w_ukv, w_mem_kv, g_out_a, g_out_b, g_out_m, w_out, g_post, b_post, loss_target, m_g_emb, m_b_emb, m_w_in, m_g_cq, m_g_ckv, m_w_uq, m_w_ukv, m_w_mem_kv, m_g_out_a, m_g_out_b, m_g_out_m, m_w_out, m_g_post, m_b_post, v_g_emb, v_b_emb, v_w_in, v_g_cq, v_g_ckv, v_w_uq, v_w_ukv, v_w_mem_kv, v_g_out_a, v_g_out_b, v_g_out_m, v_w_out, v_g_post, v_b_post):
    nb = x.shape[0]
    t = nb * SEQ
    x2 = x.reshape(t, D_MODEL)
    tgt2 = loss_target.reshape(t, D_MODEL)
    mem2 = mem.reshape(nb * N_MEM, D_MODEL)
    g_emb2, b_emb2 = g_emb.reshape(1, -1), b_emb.reshape(1, -1)
    (a_c, a_sa, a_sb), (m_c, m_sa, m_sb) = _rope_tables(positions)

    w_in_t, m_w_in_t, v_w_in_t = w_in[0].T, m_w_in[0].T, v_w_in[0].T
    s_in, s_uq, s_ukv, s_mem, s_out = cast_shards((w_in_t, w_uq[0], w_ukv[0], w_mem_kv[0], w_out[0]))
    (g_in,) = allgather_weights((s_in,))
    win_t = _w_in_working_t(g_in)

    h32, h16 = ln_emb_fwd(x2, g_emb2, b_emb2)
    proj = mm_nn(h16, win_t, F32, 512, 1536, "proj", rhs_transposed=True)
    later = (s_uq, s_ukv, s_mem, s_out)
    (ya, lse_a), qkv_d, (g_uq, g_ukv, g_mem, g_out) = a_attn_fwd(
        proj, a_c, a_sa, a_sb, nb,
        (later, (ALL_DEVICES,) * len(later), tuple(lax.empty((N_DEV,) + w.shape, BF16) for w in later)))
    wuq_w = _w_uq_working(g_uq)
    wkv_w = _w_ukv_working(g_ukv)
    wmem = g_mem.reshape(D_MODEL, 2 * MEM_WIDTH)
    wout = g_out.reshape(D_MIX, D_MODEL)
    qb, kb, vb = mla_prep_fwd(proj, m_c, m_sa, m_sb, g_cq, g_ckv, wuq_w, wkv_w)
    yb, lse_b = mla_attn_fwd(qb, kb, vb, nb)
    mkv = mm_nn(mem2, wmem, BF16, nb * N_MEM, 512, "mem_kv")
    ym = mem_attn_fwd(proj, mkv, nb)
    z, du32, du16, loss_sum, dg_post, db_post = gate_out_ln_loss(
        ya, yb, ym, proj, g_out_a, g_out_b, g_out_m, wout, h32, tgt2, g_post, b_post)

    dya, dyb, dym, dag, dbg, dmg, dg_out_a, dg_out_b, dg_out_m = gate_bwd(
        du16, wout, ya, yb, ym, proj, g_out_a, g_out_b, g_out_m)
    dw_out = mm_tn(z, du16, 512, "dw_out")
    dmq, dmk, dmv = mem_attn_bwd(proj, mkv, dym, nb)
    dw_mem = mm_tn(mem2, jnp.concatenate([dmk, dmv], 1), nb * N_MEM, "dw_mem")
    d_ag, d_bg, d_mq, d_mg = [mm_tn(p, h16, 512, "dw_in_" + n) for n, p in (("ag", dag), ("bg", dbg), ("mq", dmq), ("mg", dmg))]
    landing = lambda w, dtype=F32: lax.empty((N_DEV,) + w.shape, dtype)
    big_w = (w_in_t, w_uq[0], w_ukv[0], w_mem_kv[0], w_out[0])
    (daq, dak, dav), (p_out, p_mem, p_in) = a_attn_bwd(
        qkv_d, a_c, a_sa, a_sb, dya, ya, lse_a, nb,
        ((dw_out.reshape(N_DEV, D_MIX // N_DEV, D_MODEL), dw_mem.reshape(N_DEV, D_MODEL // N_DEV, 2 * MEM_WIDTH),
          _w_in_shards_6_7(d_bg, d_mq, d_mg)),
         (ALL_DEVICES, ALL_DEVICES, (6, 7)),
         (landing(w_out[0]), landing(w_mem_kv[0]), landing(w_in_t, BF16))))
    d_aq, d_ak, d_av = [mm_tn(p, h16, 512, "dw_in_" + n) for n, p in (("aq", daq), ("ak", dak), ("av", dav))]
    (dqb, dkb, dvb), (p_in,) = mla_attn_bwd(
        qb, kb, vb, dyb, yb, lse_b, nb, ((_w_in_shards_0_4(d_aq, d_ak, d_av, d_ag),), ((0, 1, 2, 3, 4),), (p_in,)))
    dcc, dqf, cqn, dkvf, ckvn, dg_cq, dg_ckv = mla_prep_bwd(proj, m_c, m_sa, m_sb, g_cq, g_ckv, wuq_w, wkv_w, dqb, dkb, dvb)
    dw_uq = mm_tn(cqn, dqf, 512, "dw_uq")
    dw_ukv = mm_tn(ckvn, dkvf, 512, "dw_ukv")
    d_cc = mm_tn(dcc, h16, 512, "dw_in_cc")
    pieces = (daq, dak, dav, dag, dcc, dbg, dmq, dmg)
    (grad_x, dg_emb, db_emb), (p_in, p_uq, p_ukv) = dh_ln_bwd(
        pieces, win_t, du32, x2, g_emb2,
        ((_w_in_shard_5(d_ag, d_cc, d_bg), _w_uq_parts(dw_uq), _w_ukv_parts(dw_ukv)),
         ((5,), ALL_DEVICES, ALL_DEVICES),
         (p_in, landing(w_uq[0]), landing(w_ukv[0]))))

    parts = (p_in, p_uq, p_ukv, p_mem, p_out)
    big_m = (m_w_in_t, m_w_uq[0], m_w_ukv[0], m_w_mem_kv[0], m_w_out[0])
    big_v = (v_w_in_t, v_w_uq[0], v_w_ukv[0], v_w_mem_kv[0], v_w_out[0])
    big = {}
    for name, w, p, m, v in zip(("w_in", "w_uq", "w_ukv", "w_mem_kv", "w_out"), big_w, parts, big_m, big_v):
        res = adamw_shard(w, p, m, v, "adamw_" + name)
        big[name] = [(o.T if name == "w_in" else o)[None] for o in res]

    small_w = (g_emb, b_emb, g_cq, g_ckv, g_out_a, g_out_b, g_out_m, g_post, b_post)
    small_m = (m_g_emb, m_b_emb, m_g_cq, m_g_ckv, m_g_out_a, m_g_out_b, m_g_out_m, m_g_post, m_b_post)
    small_v = (v_g_emb, v_b_emb, v_g_cq, v_g_ckv, v_g_out_a, v_g_out_b, v_g_out_m, v_g_post, v_b_post)
    small_g = (dg_emb, db_emb, dg_cq, dg_ckv, dg_out_a, dg_out_b, dg_out_m, dg_post, db_post)
    rows128 = lambda vals: [v.reshape(-1, LANES) for v in vals]
    res = small_allreduce_adamw(loss_sum, rows128(small_g), rows128(small_w), rows128(small_m), rows128(small_v))
    loss = res[0][0, 0]
    n_small = len(small_w)
    sg, sd, sm, sv = [[r.reshape(w.shape) for r, w in zip(res[1 + k * n_small:1 + (k + 1) * n_small], small_w)]
                      for k in range(4)]

    order = ("g_emb", "b_emb", "w_in", "g_cq", "g_ckv", "w_uq", "w_ukv", "w_mem_kv", "g_out_a", "g_out_b", "g_out_m",
             "w_out", "g_post", "b_post")
    small_idx = {n: i for i, n in enumerate(SMALL_NAMES)}
    outs = [loss, grad_x.reshape(x.shape)]
    for kind in range(4):
        for name in order:
            outs.append(big[name][kind] if name in big else (sg, sd, sm, sv)[kind][small_idx[name]])
    return tuple(outs)
```

```python
import functools

import jax
import jax.numpy as jnp
from jax import lax
from jax.experimental import pallas as pl
from jax.experimental.pallas import tpu as pltpu

F32 = jnp.float32
BF16 = jnp.bfloat16
SDS = jax.ShapeDtypeStruct
MESH = pl.DeviceIdType.MESH

D_MODEL = 1024
SEQ = 2048
A_HEADS, A_HEAD_DIM, A_ROT = 16, 64, 16
A_WIDTH = 1024
DILATIONS = (1, 4, 16)
N_SIDE = 64
MLA_HEADS, MLA_Q_RANK, MLA_KV_RANK = 8, 256, 128
MLA_NOPE, MLA_ROPE, MLA_V = 64, 32, 64
MLA_WIDTH = 512
N_MEM, MEM_HEADS, MEM_HEAD_DIM, MEM_WIDTH = 256, 4, 128, 512
ROPE_THETA = 500000.0
NORM_EPS = 1e-5
NEG_INF = -1e30
ALPHA = 2.0 ** 0.25
D_IN = 6048
N_DEV = 8

ADAM_LR, ADAM_B1, ADAM_B2, ADAM_EPS, ADAM_WD, ADAM_STEP = 0.001, 0.9, 0.999, 1e-08, 0.01, 10

D_INW = 6144
PIECE_WIDTHS = (1024, 1024, 1024, 1024, 512, 512, 512, 512)
PIECE_OFFS = (0, 1024, 2048, 3072, 4096, 4608, 5120, 5632)
LANES = 128
VMEM_LIMIT = 56 * 1024 * 1024


def _params(*sem):
    kw = dict(vmem_limit_bytes=VMEM_LIMIT)
    if sem:
        kw["dimension_semantics"] = sem
    return pltpu.CompilerParams(**kw)


def _dot(a, b):
    return jnp.dot(a, b, preferred_element_type=F32)


def _dot_nt(a, b):
    return lax.dot_general(a, b, (((1,), (1,)), ((), ())), preferred_element_type=F32)


def _dot_tn(a, b):
    return lax.dot_general(a, b, (((0,), (0,)), ((), ())), preferred_element_type=F32)


def _sigmoid(x):
    return 1.0 / (1.0 + jnp.exp(-x))


def _rope_fwd(x, c, sa, sb, half):
    n = x.shape[-1]
    return x * c + pltpu.roll(x, n - half, 1) * sa + pltpu.roll(x, half, 1) * sb


def _rope_bwd(dy, c, sa, sb, half):
    n = dy.shape[-1]
    return dy * c + pltpu.roll(dy * sa, half, 1) + pltpu.roll(dy * sb, n - half, 1)


def mm_nn(a, b, out_dtype, tm, tn, name, rhs_transposed=False):
    m, k = a.shape
    n = b.shape[0] if rhs_transposed else b.shape[1]
    dot = _dot_nt if rhs_transposed else _dot

    def body(a_ref, b_ref, o_ref):
        o_ref[...] = dot(a_ref[...].astype(BF16), b_ref[...].astype(BF16)).astype(o_ref.dtype)

    b_spec = pl.BlockSpec((tn, k), lambda j, i: (j, 0)) if rhs_transposed else pl.BlockSpec((k, tn), lambda j, i: (0, j))
    return pl.pallas_call(
        body, grid=(n // tn, m // tm),
        in_specs=[pl.BlockSpec((tm, k), lambda j, i: (i, 0)), b_spec],
        out_specs=pl.BlockSpec((tm, tn), lambda j, i: (i, j)),
        out_shape=SDS((m, n), out_dtype), name=name,
        compiler_params=_params("parallel", "parallel"))(a, b)


def mm_tn(a, b, tt, name):
    t, m = a.shape
    n = b.shape[1]

    def body(a_ref, b_ref, o_ref):
        @pl.when(pl.program_id(0) == 0)
        def _():
            o_ref[...] = jnp.zeros_like(o_ref)

        o_ref[...] += _dot_tn(a_ref[...].astype(BF16), b_ref[...].astype(BF16))

    return pl.pallas_call(
        body, grid=(t // tt,),
        in_specs=[pl.BlockSpec((tt, m), lambda i: (i, 0)), pl.BlockSpec((tt, n), lambda i: (i, 0))],
        out_specs=pl.BlockSpec((m, n), lambda i: (0, 0)),
        out_shape=SDS((m, n), F32), name=name,
        compiler_params=_params("arbitrary"))(a, b)


def ln_emb_fwd(x2, g, b):
    t, d = x2.shape
    tm = 512

    def body(x_ref, g_ref, b_ref, h32_ref, h16_ref):
        x = x_ref[...]
        mu = jnp.mean(x, axis=-1, keepdims=True)
        xc = x - mu
        var = jnp.mean(xc * xc, axis=-1, keepdims=True)
        h = xc * lax.rsqrt(var + NORM_EPS) * g_ref[...] + b_ref[...]
        h32_ref[...] = h
        h16_ref[...] = h.astype(BF16)

    row = pl.BlockSpec((tm, d), lambda i: (i, 0))
    vec = pl.BlockSpec((1, d), lambda i: (0, 0))
    return pl.pallas_call(
        body, grid=(t // tm,), in_specs=[row, vec, vec], out_specs=[row, row],
        out_shape=[SDS((t, d), F32), SDS((t, d), BF16)], name="ln_emb_fwd",
        compiler_params=_params("parallel"))(x2, g, b)


Q_BLK = 128
UNROLL_FWD = 8
UNROLL_BWD = 8


def _pattern_geometry(d):
    length = SEQ // d
    nblk = length // Q_BLK
    kwin = min(2 * Q_BLK, length)
    return length, nblk, kwin


def _block_coords(idx, d):
    length, nblk, kwin = _pattern_geometry(d)
    r = lax.shift_right_logical(idx, nblk.bit_length() - 1)
    i = idx & (nblk - 1)
    q0 = pl.multiple_of(r * length + i * Q_BLK, Q_BLK)
    ks = jnp.clip(i * Q_BLK - N_SIDE, 0, length - kwin)
    k0 = pl.multiple_of(r * length + ks, N_SIDE)
    qpos = i * Q_BLK + lax.broadcasted_iota(jnp.int32, (Q_BLK, kwin), 0)
    kpos = ks + lax.broadcasted_iota(jnp.int32, (Q_BLK, kwin), 1)
    valid = jnp.abs(kpos - qpos) <= N_SIDE
    return q0, k0, kwin, valid


def _deinterleave(src_ref, dst_ref, d, dtype, tmp_ref):
    if d == 1:
        dst_ref[...] = src_ref[...].astype(dtype)
        return
    q = SEQ // 4
    if d == 4:
        for r in range(4):
            dst_ref[r * q:(r + 1) * q, :] = src_ref[pl.ds(r, q, stride=4), :].astype(dtype)
        return
    assert d == 16
    n = SEQ // 16
    for r in range(4):
        tmp_ref[r * q:(r + 1) * q, :] = src_ref[pl.ds(r, q, stride=4), :]
    for r in range(4):
        for j in range(4):
            dst_ref[(r + 4 * j) * n:(r + 4 * j + 1) * n, :] = tmp_ref[pl.ds(r * q + j, n, stride=4), :].astype(dtype)


def _interleave(src_ref, dst_ref, d, tmp_ref, accumulate):
    q = SEQ // 4
    if d == 16:
        n = SEQ // 16
        for r in range(4):
            for j in range(4):
                tmp_ref[pl.ds(r * q + j, n, stride=4), :] = src_ref[(r + 4 * j) * n:(r + 4 * j + 1) * n, :]
        src_ref = tmp_ref
    else:
        assert d == 4
    for r in range(4):
        rows = pl.ds(r, q, stride=4)
        val = src_ref[r * q:(r + 1) * q, :]
        dst_ref[rows, :] = dst_ref[rows, :] + val if accumulate else val


def a_attn_fwd(proj, ca, sa, sb, nb, xch):
    t = proj.shape[0]
    n_pairs = A_WIDTH // LANES

    def body(q_ref, k_ref, v_ref, c_ref, sa_ref, sb_ref, y_ref, lse_ref, *rest):
        qkv_d, (qr_s, kr_s, oc_s, lc_s, o1_s, l1_s, o2_s, l2_s, o3_s, l3_s, tmp_s) = rest[:9], rest[9:]
        c, s_a, s_b = c_ref[...], sa_ref[...], sb_ref[...]
        qr_s[...] = _rope_fwd(q_ref[...], c, s_a, s_b, A_ROT // 2) * (A_HEAD_DIM ** -0.5)
        kr_s[...] = _rope_fwd(k_ref[...], c, s_a, s_b, A_ROT // 2)
        head0 = lax.broadcasted_iota(jnp.int32, (Q_BLK, LANES), 1) < A_HEAD_DIM
        nat = ((o1_s, l1_s), (o2_s, l2_s), (o3_s, l3_s))

        for g, d in enumerate(DILATIONS):
            qd_s, kd_s, vd_s = qkv_d[3 * g:3 * g + 3]
            _deinterleave(qr_s, qd_s, d, BF16, tmp_s)
            _deinterleave(kr_s, kd_s, d, BF16, tmp_s)
            _deinterleave(v_ref, vd_s, d, BF16, tmp_s)
            o_dst, l_dst = (nat[g] if d == 1 else (oc_s, lc_s))

            def block(idx, carry, d=d, o_dst=o_dst, l_dst=l_dst, qd_s=qd_s, kd_s=kd_s, vd_s=vd_s):
                q0, k0, kwin, valid = _block_coords(idx, d)
                qb = qd_s[pl.ds(q0, Q_BLK), :]
                kb = kd_s[pl.ds(k0, kwin), :]
                vb = vd_s[pl.ds(k0, kwin), :]
                zero = jnp.zeros_like(qb)
                q2 = jnp.concatenate([jnp.where(head0, qb, zero), jnp.where(head0, zero, qb)], 0)
                s = jnp.where(jnp.concatenate([valid, valid], 0), _dot_nt(q2, kb), NEG_INF)
                m = jnp.max(s, axis=-1, keepdims=True)
                p = jnp.exp(s - m)
                l = jnp.sum(p, axis=-1, keepdims=True)
                o2 = _dot(p.astype(BF16), vb) / l
                l2 = m + jnp.log(l)
                o_dst[pl.ds(q0, Q_BLK), :] = jnp.where(head0, o2[:Q_BLK], o2[Q_BLK:])
                l_dst[pl.ds(q0, Q_BLK), :] = jnp.where(head0, l2[:Q_BLK], l2[Q_BLK:])
                return carry

            lax.fori_loop(0, SEQ // Q_BLK, block, 0, unroll=UNROLL_FWD)
            if d > 1:
                _interleave(oc_s, nat[g][0], d, tmp_s, False)
                _interleave(lc_s, nat[g][1], d, tmp_s, False)

        def merge(ci, carry):
            rows = pl.ds(pl.multiple_of(ci * 256, 256), 256)
            l1, l2, l3 = l1_s[rows, :], l2_s[rows, :], l3_s[rows, :]
            m = jnp.maximum(jnp.maximum(l1, l2), l3)
            w1, w2, w3 = jnp.exp(l1 - m), jnp.exp(l2 - m), jnp.exp(l3 - m)
            w = w1 + w2 + w3
            y_ref[rows, :] = (w1 * o1_s[rows, :] + w2 * o2_s[rows, :] + w3 * o3_s[rows, :]) / w
            lse_ref[rows, :] = m + jnp.log(w)
            return carry

        lax.fori_loop(0, SEQ // 256, merge, 0)

    def col(off):
        return pl.BlockSpec((SEQ, LANES), lambda b, hp: (b, off + hp))

    tab = pl.BlockSpec((SEQ, LANES), lambda b, hp: (b, 0))
    out = pl.BlockSpec((SEQ, LANES), lambda b, hp: (b, hp))
    f32s = pltpu.VMEM((SEQ, LANES), F32)
    res, landed = call_hosting_exchange(
        body, xch, grid=(nb, n_pairs),
        in_specs=[col(0), col(n_pairs), col(2 * n_pairs), tab, tab, tab],
        out_specs=[out] * 11,
        out_shape=[SDS((t, A_WIDTH), F32)] * 2 + [SDS((t, A_WIDTH), BF16)] * 9,
        scratch_shapes=[f32s] * 11,
        name="a_attn_fwd", operands=(proj, proj, proj, ca, sa, sb))
    return res[:2], res[2:], landed


def a_attn_bwd(qkv_d, ca, sa, sb, dy, y, lse, nb, xch):
    t = dy.shape[0]
    n_pairs = A_WIDTH // LANES

    def body(*refs):
        qkv_refs = refs[:9]
        (c_ref, sa_ref, sb_ref, do_ref, y_ref, lse_ref, dq_ref, dk_ref, dv_ref,
         l0n_s, l1n_s, d0n_s, d1n_s, dod_s, l0d_s, l1d_s, d0d_s, d1d_s,
         dqc_s, dkc_s, dvc_s, dqn_s, dkn_s, dvn_s, tmp_s) = refs[9:]
        c, s_a, s_b = c_ref[...], sa_ref[...], sb_ref[...]
        head0 = lax.broadcasted_iota(jnp.int32, (Q_BLK, LANES), 1) < A_HEAD_DIM

        def per_head_rows(ci, carry):
            rows = pl.ds(pl.multiple_of(ci * 256, 256), 256)
            h0 = lax.broadcasted_iota(jnp.int32, (256, LANES), 1) < A_HEAD_DIM
            tt = do_ref[rows, :] * y_ref[rows, :]
            d0n_s[rows, :] = jnp.broadcast_to(jnp.sum(jnp.where(h0, tt, 0.0), axis=-1, keepdims=True), (256, LANES))
            d1n_s[rows, :] = jnp.broadcast_to(jnp.sum(jnp.where(h0, 0.0, tt), axis=-1, keepdims=True), (256, LANES))
            l = lse_ref[rows, :]
            lr = pltpu.roll(l, A_HEAD_DIM, 1)
            l0n_s[rows, :] = jnp.where(h0, l, lr)
            l1n_s[rows, :] = jnp.where(h0, lr, l)
            return carry

        lax.fori_loop(0, SEQ // 256, per_head_rows, 0)
        assert DILATIONS[0] == 1

        for g, d in enumerate(DILATIONS):
            qd_s, kd_s, vd_s = qkv_refs[3 * g:3 * g + 3]
            _deinterleave(do_ref, dod_s, d, BF16, tmp_s)
            if d > 1:
                for src, dst in ((l0n_s, l0d_s), (l1n_s, l1d_s), (d0n_s, d0d_s), (d1n_s, d1d_s)):
                    _deinterleave(src, dst, d, F32, tmp_s)
            l0, l1, d0, d1 = (l0n_s, l1n_s, d0n_s, d1n_s) if d == 1 else (l0d_s, l1d_s, d0d_s, d1d_s)
            dq_dst, dk_dst, dv_dst = (dqn_s, dkn_s, dvn_s) if d == 1 else (dqc_s, dkc_s, dvc_s)
            dk_dst[...] = jnp.zeros_like(dk_dst)
            dv_dst[...] = jnp.zeros_like(dv_dst)

            def block(idx, carry, d=d, l0=l0, l1=l1, d0=d0, d1=d1, dq_dst=dq_dst, dk_dst=dk_dst, dv_dst=dv_dst,
                      qd_s=qd_s, kd_s=kd_s, vd_s=vd_s):
                q0, k0, kwin, valid = _block_coords(idx, d)
                qrows = pl.ds(q0, Q_BLK)
                krows = pl.ds(k0, kwin)
                qb, dob = qd_s[qrows, :], dod_s[qrows, :]
                kb, vb = kd_s[krows, :], vd_s[krows, :]
                zero = jnp.zeros_like(qb)
                q2 = jnp.concatenate([jnp.where(head0, qb, zero), jnp.where(head0, zero, qb)], 0)
                do2 = jnp.concatenate([jnp.where(head0, dob, zero), jnp.where(head0, zero, dob)], 0)
                wide = lambda x: jnp.concatenate([x] * (kwin // LANES), 1)
                lse2 = wide(jnp.concatenate([l0[qrows, :], l1[qrows, :]], 0))
                dd2 = wide(jnp.concatenate([d0[qrows, :], d1[qrows, :]], 0))
                s = jnp.where(jnp.concatenate([valid, valid], 0), _dot_nt(q2, kb), NEG_INF)
                p = jnp.exp(s - lse2)
                ds = (p * (_dot_nt(do2, vb) - dd2)).astype(BF16)
                dq2 = _dot(ds, kb)
                dq_dst[qrows, :] = jnp.where(head0, dq2[:Q_BLK], dq2[Q_BLK:])
                dk_dst[krows, :] += _dot_tn(ds, q2)
                dv_dst[krows, :] += _dot_tn(p.astype(BF16), do2)
                return carry

            lax.fori_loop(0, SEQ // Q_BLK, block, 0, unroll=UNROLL_BWD)
            if d > 1:
                _interleave(dqc_s, dqn_s, d, tmp_s, True)
                _interleave(dkc_s, dkn_s, d, tmp_s, True)
                _interleave(dvc_s, dvn_s, d, tmp_s, True)

        dq_ref[...] = _rope_bwd(dqn_s[...] * (A_HEAD_DIM ** -0.5), c, s_a, s_b, A_ROT // 2).astype(BF16)
        dk_ref[...] = _rope_bwd(dkn_s[...], c, s_a, s_b, A_ROT // 2).astype(BF16)
        dv_ref[...] = dvn_s[...].astype(BF16)

    tab = pl.BlockSpec((SEQ, LANES), lambda b, hp: (b, 0))
    blk = pl.BlockSpec((SEQ, LANES), lambda b, hp: (b, hp))
    f32s = pltpu.VMEM((SEQ, LANES), F32)
    b16s = pltpu.VMEM((SEQ, LANES), BF16)
    return call_hosting_exchange(
        body, xch, grid=(nb, n_pairs),
        in_specs=[blk] * 9 + [tab, tab, tab, blk, blk, blk],
        out_specs=[blk, blk, blk],
        out_shape=[SDS((t, A_WIDTH), BF16)] * 3,
        scratch_shapes=[f32s] * 4 + [b16s] + [f32s] * 11,
        name="a_attn_bwd", operands=(*qkv_d, ca, sa, sb, dy, y, lse))


MLA_SCALE = (MLA_NOPE + MLA_ROPE) ** -0.5
MLA_QW = MLA_HEADS * LANES
MLA_KVW = MLA_QW + MLA_WIDTH


def _rms(x, g):
    r = lax.rsqrt(jnp.mean(x * x, axis=-1, keepdims=True) + NORM_EPS)
    return x * r * g, r


def _rms_bwd(dn, x, r, g):
    tg = dn * g
    dx = r * tg - x * (r * r * r) * jnp.mean(tg * x, axis=-1, keepdims=True)
    return dx, jnp.sum(dn * x * r, axis=0, keepdims=True)


def mla_prep_fwd(proj, cm, sma, smb, g_cq, g_ckv, wuq, wkv):
    t = proj.shape[0]
    tm = 512

    def body(cq_ref, ckv_ref, kr_ref, c_ref, sa_ref, sb_ref, gq_ref, gkv_ref, wuq_ref, wkv_ref, q_ref, k_ref, v_ref):
        c, s_a, s_b = c_ref[...], sa_ref[...], sb_ref[...]
        cqn, _ = _rms(cq_ref[...], gq_ref[...])
        qf = _dot(cqn.astype(BF16), wuq_ref[...])
        ckvn, _ = _rms(ckv_ref[...], gkv_ref[...])
        kvf = _dot(ckvn.astype(BF16), wkv_ref[...])
        krope = _rope_fwd(kr_ref[...], c, s_a, s_b, MLA_ROPE // 2)
        for h in range(MLA_HEADS):
            cols = slice(h * LANES, (h + 1) * LANES)
            q_ref[:, cols] = (_rope_fwd(qf[:, cols], c, s_a, s_b, MLA_ROPE // 2) * MLA_SCALE).astype(BF16)
            k_ref[:, cols] = (kvf[:, cols] + krope).astype(BF16)
        v_ref[...] = kvf[:, MLA_QW:].astype(BF16)

    def row(w, j):
        return pl.BlockSpec((tm, w), lambda i: (i, j))

    def full(a):
        return pl.BlockSpec(a.shape, lambda i: (0, 0))

    return pl.pallas_call(
        body, grid=(t // tm,),
        in_specs=[row(256, 4096 // 256), row(128, 4352 // 128), row(128, 4480 // 128), row(128, 0), row(128, 0), row(128, 0),
                  full(g_cq), full(g_ckv), full(wuq), full(wkv)],
        out_specs=[row(MLA_QW, 0), row(MLA_QW, 0), row(MLA_WIDTH, 0)],
        out_shape=[SDS((t, MLA_QW), BF16), SDS((t, MLA_QW), BF16), SDS((t, MLA_WIDTH), BF16)],
        name="mla_prep_fwd", compiler_params=_params("parallel"))(proj, proj, proj, cm, sma, smb, g_cq, g_ckv, wuq, wkv)


def mla_prep_bwd(proj, cm, sma, smb, g_cq, g_ckv, wuq, wkv, dq, dk, dv):
    t = proj.shape[0]
    tm = 512

    def body(cq_ref, ckv_ref, c_ref, sa_ref, sb_ref, gq_ref, gkv_ref, wuq_ref, wkv_ref, dq_ref, dk_ref, dv_ref,
             dcc_ref, dqf_ref, cqn_ref, dkvf_ref, ckvn_ref, dgq_ref, dgkv_ref):
        @pl.when(pl.program_id(0) == 0)
        def _():
            dgq_ref[...] = jnp.zeros_like(dgq_ref)
            dgkv_ref[...] = jnp.zeros_like(dgkv_ref)

        c, s_a, s_b = c_ref[...], sa_ref[...], sb_ref[...]
        cq, ckv = cq_ref[...], ckv_ref[...]
        cqn, rq = _rms(cq, gq_ref[...])
        ckvn, rkv = _rms(ckv, gkv_ref[...])
        cqn_ref[...] = cqn.astype(BF16)
        ckvn_ref[...] = ckvn.astype(BF16)
        lane = lax.broadcasted_iota(jnp.int32, (tm, LANES), 1)
        rope_lanes = (lane >= MLA_NOPE) & (lane < MLA_NOPE + MLA_ROPE)
        dkrope = jnp.zeros((tm, LANES), F32)
        for h in range(MLA_HEADS):
            cols = slice(h * LANES, (h + 1) * LANES)
            dqf_ref[:, cols] = _rope_bwd(dq_ref[:, cols] * MLA_SCALE, c, s_a, s_b, MLA_ROPE // 2).astype(BF16)
            dkh = dk_ref[:, cols]
            dkvf_ref[:, cols] = dkh.astype(BF16)
            dkrope = dkrope + dkh
        dkvf_ref[:, MLA_QW:] = dv_ref[...].astype(BF16)
        dkr = _rope_bwd(jnp.where(rope_lanes, dkrope, 0.0), c, s_a, s_b, MLA_ROPE // 2)
        dcqn = _dot_nt(dqf_ref[...], wuq_ref[...])
        dckvn = _dot_nt(dkvf_ref[...], wkv_ref[...])
        dcq, dgq = _rms_bwd(dcqn, cq, rq, gq_ref[...])
        dckv, dgkv = _rms_bwd(dckvn, ckv, rkv, gkv_ref[...])
        dgq_ref[...] += dgq
        dgkv_ref[...] += dgkv
        dcc_ref[:, 0:256] = dcq.astype(BF16)
        dcc_ref[:, 256:384] = dckv.astype(BF16)
        dcc_ref[:, 384:512] = dkr.astype(BF16)

    def row(w, j):
        return pl.BlockSpec((tm, w), lambda i: (i, j))

    def full(a):
        return pl.BlockSpec(a.shape, lambda i: (0, 0))

    return pl.pallas_call(
        body, grid=(t // tm,),
        in_specs=[row(256, 4096 // 256), row(128, 4352 // 128), row(128, 0), row(128, 0), row(128, 0),
                  full(g_cq), full(g_ckv), full(wuq), full(wkv), row(MLA_QW, 0), row(MLA_QW, 0), row(MLA_WIDTH, 0)],
        out_specs=[row(512, 0), row(MLA_QW, 0), row(256, 0), row(MLA_KVW, 0), row(128, 0), full(g_cq), full(g_ckv)],
        out_shape=[SDS((t, 512), BF16), SDS((t, MLA_QW), BF16), SDS((t, 256), BF16), SDS((t, MLA_KVW), BF16),
                   SDS((t, 128), BF16), SDS(g_cq.shape, F32), SDS(g_ckv.shape, F32)],
        name="mla_prep_bwd", compiler_params=_params("arbitrary"))(proj, proj, cm, sma, smb, g_cq, g_ckv, wuq, wkv, dq, dk, dv)


MLA_TQ = 256


def mla_attn_fwd(qb, kb, vb, nb):
    t = qb.shape[0]
    nq = SEQ // MLA_TQ
    n_pairs = MLA_HEADS // 2

    def body(q_ref, k_ref, v_ref, y_ref, lse_ref):
        head0 = lax.broadcasted_iota(jnp.int32, (MLA_TQ, LANES), 1) < MLA_V
        v = v_ref[...]
        outs, lses = [], []
        for h in range(2):
            cols = slice(h * LANES, (h + 1) * LANES)
            s = _dot_nt(q_ref[:, cols], k_ref[:, cols])
            m = jnp.max(s, axis=-1, keepdims=True)
            p = jnp.exp(s - m)
            l = jnp.sum(p, axis=-1, keepdims=True)
            outs.append(_dot(p.astype(BF16), v) / l)
            lses.append(m + jnp.log(l))
        y_ref[...] = jnp.where(head0, outs[0], outs[1])
        lse_ref[...] = jnp.where(head0, lses[0], lses[1])

    return pl.pallas_call(
        body, grid=(nb, n_pairs, nq),
        in_specs=[pl.BlockSpec((MLA_TQ, 2 * LANES), lambda b, hp, i: (b * nq + i, hp)),
                  pl.BlockSpec((SEQ, 2 * LANES), lambda b, hp, i: (b, hp)),
                  pl.BlockSpec((SEQ, LANES), lambda b, hp, i: (b, hp))],
        out_specs=[pl.BlockSpec((MLA_TQ, LANES), lambda b, hp, i: (b * nq + i, hp))] * 2,
        out_shape=[SDS((t, MLA_WIDTH), F32)] * 2,
        name="mla_attn_fwd", compiler_params=_params("parallel", "parallel", "parallel"))(qb, kb, vb)


def mla_attn_bwd(qb, kb, vb, dy, y, lse, nb, xch):
    t = qb.shape[0]
    nq = SEQ // MLA_TQ
    n_pairs = MLA_HEADS // 2

    def body(q_ref, k_ref, v_ref, do_ref, y_ref, lse_ref, dq_ref, dk_ref, dv_ref):
        @pl.when(pl.program_id(2) == 0)
        def _():
            dk_ref[...] = jnp.zeros_like(dk_ref)
            dv_ref[...] = jnp.zeros_like(dv_ref)

        head0 = lax.broadcasted_iota(jnp.int32, (MLA_TQ, LANES), 1) < MLA_V
        v = v_ref[...]
        do = do_ref[...]
        lse = lse_ref[...]
        tt = do * y_ref[...]
        dv = jnp.zeros((SEQ, LANES), F32)
        for h in range(2):
            sel = head0 if h == 0 else ~head0
            lo = h * MLA_V
            cols = slice(h * LANES, (h + 1) * LANES)
            q = q_ref[:, cols]
            k = k_ref[:, cols]
            dd = jnp.sum(jnp.where(sel, tt, 0.0), axis=-1, keepdims=True)
            doh = jnp.where(sel, do, 0.0).astype(BF16)
            p = jnp.exp(_dot_nt(q, k) - lse[:, lo:lo + 1])
            dp = _dot_nt(doh, v)
            ds = (p * (dp - dd)).astype(BF16)
            dq_ref[:, cols] = _dot(ds, k)
            dk_ref[:, cols] += _dot_tn(ds, q)
            dv = dv + _dot_tn(p.astype(BF16), doh)
        dv_ref[...] += dv

    qspec = pl.BlockSpec((MLA_TQ, 2 * LANES), lambda b, hp, i: (b * nq + i, hp))
    kspec = pl.BlockSpec((SEQ, 2 * LANES), lambda b, hp, i: (b, hp))
    vspec = pl.BlockSpec((SEQ, LANES), lambda b, hp, i: (b, hp))
    ospec = pl.BlockSpec((MLA_TQ, LANES), lambda b, hp, i: (b * nq + i, hp))
    return call_hosting_exchange(
        body, xch, grid=(nb, n_pairs, nq),
        in_specs=[qspec, kspec, vspec, ospec, ospec, ospec],
        out_specs=[qspec, kspec, vspec],
        out_shape=[SDS((t, MLA_QW), F32), SDS((t, MLA_QW), F32), SDS((t, MLA_WIDTH), F32)],
        scratch_shapes=[], name="mla_attn_bwd", operands=(qb, kb, vb, dy, y, lse))


MEM_TQ = 512
MEM_SCALE = MEM_HEAD_DIM ** -0.5
MQ_BLK4 = 5120 // MEM_WIDTH


def mem_attn_fwd(proj, mkv, nb):
    t = proj.shape[0]
    nq = SEQ // MEM_TQ

    def body(q_ref, mk_ref, mv_ref, y_ref):
        for h in range(MEM_HEADS):
            cols = slice(h * LANES, (h + 1) * LANES)
            s = _dot_nt(q_ref[:, cols].astype(BF16), mk_ref[:, cols]) * MEM_SCALE
            m = jnp.max(s, axis=-1, keepdims=True)
            p = jnp.exp(s - m)
            l = jnp.sum(p, axis=-1, keepdims=True)
            y_ref[:, cols] = _dot(p.astype(BF16), mv_ref[:, cols]) / l

    return pl.pallas_call(
        body, grid=(nb, nq),
        in_specs=[pl.BlockSpec((MEM_TQ, MEM_WIDTH), lambda b, i: (b * nq + i, MQ_BLK4)),
                  pl.BlockSpec((N_MEM, MEM_WIDTH), lambda b, i: (b, 0)),
                  pl.BlockSpec((N_MEM, MEM_WIDTH), lambda b, i: (b, 1))],
        out_specs=pl.BlockSpec((MEM_TQ, MEM_WIDTH), lambda b, i: (b * nq + i, 0)),
        out_shape=SDS((t, MEM_WIDTH), F32),
        name="mem_attn_fwd", compiler_params=_params("parallel", "parallel"))(proj, mkv, mkv)


def mem_attn_bwd(proj, mkv, dy, nb):
    t = proj.shape[0]
    nq = SEQ // MEM_TQ

    def body(q_ref, mk_ref, mv_ref, do_ref, dq_ref, dmk_ref, dmv_ref):
        @pl.when(pl.program_id(1) == 0)
        def _():
            dmk_ref[...] = jnp.zeros_like(dmk_ref)
            dmv_ref[...] = jnp.zeros_like(dmv_ref)

        for h in range(MEM_HEADS):
            cols = slice(h * LANES, (h + 1) * LANES)
            q = q_ref[:, cols].astype(BF16)
            mk, mv = mk_ref[:, cols], mv_ref[:, cols]
            do = do_ref[:, cols].astype(BF16)
            s = _dot_nt(q, mk) * MEM_SCALE
            e = jnp.exp(s - jnp.max(s, axis=-1, keepdims=True))
            p = e / jnp.sum(e, axis=-1, keepdims=True)
            dp = _dot_nt(do, mv)
            ds = (p * (dp - jnp.sum(p * dp, axis=-1, keepdims=True)) * MEM_SCALE).astype(BF16)
            dq_ref[:, cols] = _dot(ds, mk).astype(BF16)
            dmk_ref[:, cols] += _dot_tn(ds, q)
            dmv_ref[:, cols] += _dot_tn(p.astype(BF16), do)

    ospec = pl.BlockSpec((MEM_TQ, MEM_WIDTH), lambda b, i: (b * nq + i, 0))
    kspec = pl.BlockSpec((N_MEM, MEM_WIDTH), lambda b, i: (b, 0))
    return pl.pallas_call(
        body, grid=(nb, nq),
        in_specs=[pl.BlockSpec((MEM_TQ, MEM_WIDTH), lambda b, i: (b * nq + i, MQ_BLK4)),
                  kspec, pl.BlockSpec((N_MEM, MEM_WIDTH), lambda b, i: (b, 1)), ospec],
        out_specs=[ospec, kspec, kspec],
        out_shape=[SDS((t, MEM_WIDTH), BF16), SDS((nb * N_MEM, MEM_WIDTH), F32), SDS((nb * N_MEM, MEM_WIDTH), F32)],
        name="mem_attn_bwd", compiler_params=_params("parallel", "arbitrary"))(proj, mkv, mkv, dy)


ROW_TM = 512
AG_BLK = 3072 // 1024
BG_BLK = 4608 // 512
MG_BLK = 5632 // 512
GROUPS = ((0, A_WIDTH), (A_WIDTH, MLA_WIDTH), (A_WIDTH + MLA_WIDTH, MEM_WIDTH))
D_MIX = 2048


def _gate_specs():
    def row(w, j):
        return pl.BlockSpec((ROW_TM, w), lambda i: (i, j))

    def vec(w):
        return pl.BlockSpec((1, w), lambda i: (0, 0))

    ys = [row(A_WIDTH, 0), row(MLA_WIDTH, 0), row(MEM_WIDTH, 0)]
    gates = [row(A_WIDTH, AG_BLK), row(MLA_WIDTH, BG_BLK), row(MEM_WIDTH, MG_BLK)]
    gains = [vec(A_WIDTH), vec(MLA_WIDTH), vec(MEM_WIDTH)]
    return row, vec, ys, gates, gains


def gate_out_ln_loss(ya, yb, ym, proj, goa, gob, gom, wout, h32, target, gp, bp):
    t, d = h32.shape
    _, _, ys, gates, gains = _gate_specs()

    def body(ya_ref, yb_ref, ym_ref, ga_ref, gb_ref, gm_ref, goa_ref, gob_ref, gom_ref, w_ref, h_ref, t_ref, gp_ref, bp_ref,
             z_ref, du32_ref, du16_ref, loss_ref, dgp_ref, dbp_ref):
        @pl.when(pl.program_id(0) == 0)
        def _():
            loss_ref[...] = jnp.zeros_like(loss_ref)
            dgp_ref[...] = jnp.zeros_like(dgp_ref)
            dbp_ref[...] = jnp.zeros_like(dbp_ref)

        for (off, w), y_ref, g_ref, go_ref in zip(GROUPS, (ya_ref, yb_ref, ym_ref), (ga_ref, gb_ref, gm_ref),
                                                  (goa_ref, gob_ref, gom_ref)):
            n, _ = _rms(y_ref[...], go_ref[...])
            gt = g_ref[...]
            z_ref[:, off:off + w] = (n * (gt * _sigmoid(gt))).astype(BF16)
        g = gp_ref[...]
        u = ALPHA * h_ref[...] + _dot(z_ref[...], w_ref[...])
        mu = jnp.mean(u, axis=-1, keepdims=True)
        uc = u - mu
        rstd = lax.rsqrt(jnp.mean(uc * uc, axis=-1, keepdims=True) + NORM_EPS)
        xhat = uc * rstd
        err = xhat * g + bp_ref[...] - t_ref[...]
        tok = jnp.sum(err * err, axis=-1, keepdims=True) * (1.0 / d)
        loss_ref[...] += 0.5 * jnp.sum(tok, axis=0, keepdims=True)
        dout = err * (1.0 / d)
        dxhat = dout * g
        du = rstd * (dxhat - jnp.mean(dxhat, axis=-1, keepdims=True)
                     - xhat * jnp.mean(dxhat * xhat, axis=-1, keepdims=True))
        du32_ref[...] = du
        du16_ref[...] = du.astype(BF16)
        dgp_ref[...] += jnp.sum(dout * xhat, axis=0, keepdims=True)
        dbp_ref[...] += jnp.sum(dout, axis=0, keepdims=True)

    row = pl.BlockSpec((ROW_TM, d), lambda i: (i, 0))
    vec = pl.BlockSpec((1, d), lambda i: (0, 0))
    zrow = pl.BlockSpec((ROW_TM, D_MIX), lambda i: (i, 0))
    return pl.pallas_call(
        body, grid=(t // ROW_TM,),
        in_specs=ys + gates + gains + [pl.BlockSpec((D_MIX, d), lambda i: (0, 0)), row, row, vec, vec],
        out_specs=[zrow, row, row, pl.BlockSpec((1, LANES), lambda i: (0, 0)), vec, vec],
        out_shape=[SDS((t, D_MIX), BF16), SDS((t, d), F32), SDS((t, d), BF16), SDS((1, LANES), F32), SDS((1, d), F32),
                   SDS((1, d), F32)],
        name="gate_out_ln_loss", compiler_params=_params("arbitrary"))(
            ya, yb, ym, proj, proj, proj, goa, gob, gom, wout, h32, target, gp, bp)


def gate_bwd(du16, wout, ya, yb, ym, proj, goa, gob, gom):
    t = ya.shape[0]
    row, vec, ys, gates, gains = _gate_specs()

    def body(du_ref, w_ref, ya_ref, yb_ref, ym_ref, ga_ref, gb_ref, gm_ref, goa_ref, gob_ref, gom_ref,
             dya_ref, dyb_ref, dym_ref, dga_ref, dgb_ref, dgm_ref, dgoa_ref, dgob_ref, dgom_ref):
        @pl.when(pl.program_id(0) == 0)
        def _():
            dgoa_ref[...] = jnp.zeros_like(dgoa_ref)
            dgob_ref[...] = jnp.zeros_like(dgob_ref)
            dgom_ref[...] = jnp.zeros_like(dgom_ref)

        dz = _dot_nt(du_ref[...], w_ref[...])
        for (off, w), y_ref, g_ref, go_ref, dy_ref, dg_ref, dgo_ref in zip(
                GROUPS, (ya_ref, yb_ref, ym_ref), (ga_ref, gb_ref, gm_ref), (goa_ref, gob_ref, gom_ref),
                (dya_ref, dyb_ref, dym_ref), (dga_ref, dgb_ref, dgm_ref), (dgoa_ref, dgob_ref, dgom_ref)):
            dzg = dz[:, off:off + w]
            y, gt, go = y_ref[...], g_ref[...], go_ref[...]
            n, r = _rms(y, go)
            sg = _sigmoid(gt)
            dg_ref[...] = (dzg * n * (sg * (1.0 + gt * (1.0 - sg)))).astype(BF16)
            dy, dgo = _rms_bwd(dzg * (gt * sg), y, r, go)
            dy_ref[...] = dy
            dgo_ref[...] += dgo

    widths = (A_WIDTH, MLA_WIDTH, MEM_WIDTH)
    return pl.pallas_call(
        body, grid=(t // ROW_TM,),
        in_specs=[row(D_MODEL, 0), pl.BlockSpec((D_MIX, D_MODEL), lambda i: (0, 0))] + ys + gates + gains,
        out_specs=[row(w, 0) for w in widths] * 2 + [vec(w) for w in widths],
        out_shape=[SDS((t, w), F32) for w in widths] + [SDS((t, w), BF16) for w in widths] + [SDS((1, w), F32) for w in widths],
        name="gate_bwd", compiler_params=_params("arbitrary"))(du16, wout, ya, yb, ym, proj, proj, proj, goa, gob, gom)


def dh_ln_bwd(pieces, win_t, du32, x2, g_emb, xch):
    t, d = x2.shape

    def body(*refs):
        p_refs = refs[:len(pieces)]
        w_ref, du_ref, x_ref, g_ref, dx_ref, dg_ref, db_ref = refs[len(pieces):]

        @pl.when(pl.program_id(0) == 0)
        def _():
            dg_ref[...] = jnp.zeros_like(dg_ref)
            db_ref[...] = jnp.zeros_like(db_ref)

        dh = ALPHA * du_ref[...]
        for p_ref, off, w in zip(p_refs, PIECE_OFFS, PIECE_WIDTHS):
            dh = dh + _dot(p_ref[...], w_ref[off:off + w, :])
        x = x_ref[...]
        xc = x - jnp.mean(x, axis=-1, keepdims=True)
        rstd = lax.rsqrt(jnp.mean(xc * xc, axis=-1, keepdims=True) + NORM_EPS)
        xhat = xc * rstd
        dg_ref[...] += jnp.sum(dh * xhat, axis=0, keepdims=True)
        db_ref[...] += jnp.sum(dh, axis=0, keepdims=True)
        tg = dh * g_ref[...]
        dx_ref[...] = rstd * (tg - jnp.mean(tg, axis=-1, keepdims=True)
                              - xhat * jnp.mean(tg * xhat, axis=-1, keepdims=True))

    row = pl.BlockSpec((ROW_TM, d), lambda i: (i, 0))
    vec = pl.BlockSpec((1, d), lambda i: (0, 0))
    return call_hosting_exchange(
        body, xch, grid=(t // ROW_TM,),
        in_specs=[pl.BlockSpec((ROW_TM, w), lambda i: (i, 0)) for w in PIECE_WIDTHS]
        + [pl.BlockSpec(win_t.shape, lambda i: (0, 0)), row, row, vec],
        out_specs=[row, vec, vec],
        out_shape=[SDS((t, d), F32), SDS((1, d), F32), SDS((1, d), F32)],
        scratch_shapes=[], name="dh_ln_bwd", operands=(*pieces, win_t, du32, x2, g_emb))


def _adamw(w, g, m, v):
    m2 = ADAM_B1 * m + (1.0 - ADAM_B1) * g
    v2 = ADAM_B2 * v + (1.0 - ADAM_B2) * (g * g)
    m_hat = m2 / (1.0 - ADAM_B1 ** ADAM_STEP)
    v_hat = v2 / (1.0 - ADAM_B2 ** ADAM_STEP)
    return -ADAM_LR * (m_hat / (jnp.sqrt(v_hat) + ADAM_EPS) + ADAM_WD * w), m2, v2


def adamw_shard(w, parts, m, v, name):
    r, c = w.shape
    if r % 256 == 0 or r * c <= 256 * 1024:
        tr, tc = min(r, 256), c
    else:
        tr, tc = r, 256

    def body(w_ref, p_ref, m_ref, v_ref, g_ref, d_ref, nm_ref, nv_ref):
        g = p_ref[0].astype(F32)
        for k in range(1, N_DEV):
            g = g + p_ref[k].astype(F32)
        g_ref[...] = g
        d_ref[...], nm_ref[...], nv_ref[...] = _adamw(w_ref[...], g, m_ref[...], v_ref[...])

    blk = pl.BlockSpec((tr, tc), lambda i, j: (i, j))
    return pl.pallas_call(
        body, grid=(r // tr, c // tc),
        in_specs=[blk, pl.BlockSpec((N_DEV, tr, tc), lambda i, j: (0, i, j)), blk, blk],
        out_specs=[blk] * 4, out_shape=[SDS((r, c), F32)] * 4, name=name,
        compiler_params=_params("parallel", "parallel"))(w, parts, m, v)


def _place():
    return lax.axis_index("x"), lax.axis_index("y"), lax.axis_index("c")


def _flat(px, py, pc):
    return 4 * px + 2 * py + pc


def _peer(x, y, c, k):
    return (1 - x if k & 4 else x, 1 - y if k & 2 else y, 1 - c if k & 1 else c)


def cast_shards(shards):
    def body(*refs):
        n = len(refs) // 2
        for i_ref, o_ref in zip(refs[:n], refs[n:]):
            o_ref[...] = i_ref[...].astype(BF16)

    return pl.pallas_call(body, out_shape=[SDS(s.shape, BF16) for s in shards], name="cast_shards",
                          compiler_params=_params())(*shards)


def allgather_weights(shards):
    n = len(shards)

    def body(*refs):
        ins, outs = refs[:n], refs[n:2 * n]
        send_sems, recv_sems, local_sems = refs[2 * n:]
        x, y, c = _place()
        me, sib = (x, y, c), (x, y, 1 - c)
        chips = [(1 - x, y), (x, 1 - y), (1 - x, 1 - y)]

        def copy(a, k, block, to, src=None):
            dst = outs[a].at[_flat(*block)]
            return pltpu.make_async_remote_copy(
                src_ref=dst if src is None else src, dst_ref=dst,
                send_sem=send_sems.at[a * 7 + k], recv_sem=recv_sems.at[a * 7 + k],
                device_id=to, device_id_type=MESH)

        mine = [pltpu.make_async_copy(ins[a], outs[a].at[_flat(*me)], local_sems.at[a]) for a in range(n)]
        for cp in mine:
            cp.start()
        first = []
        for a in range(n):
            first.append(copy(a, 0, me, sib, src=ins[a]))
            first += [copy(a, 1 + j, me, (*chip, c), src=ins[a]) for j, chip in enumerate(chips)]
        for cp in first:
            cp.start()
        passed = []
        for j, chip in enumerate(chips):
            for a in range(n):
                copy(a, 1 + j, (*chip, c), me).wait_recv()
                fwd = copy(a, 4 + j, (*chip, c), sib)
                fwd.start()
                passed.append(fwd)
        for a in range(n):
            copy(a, 0, sib, me).wait_recv()
            for j, chip in enumerate(chips):
                copy(a, 4 + j, (*chip, 1 - c), me).wait_recv()
        for cp in first + passed:
            cp.wait_send()
        for cp in mine:
            cp.wait()

    hbm = pl.BlockSpec(memory_space=pl.ANY)
    return pl.pallas_call(
        body, out_shape=[SDS((N_DEV,) + s.shape, s.dtype) for s in shards],
        in_specs=[hbm] * n, out_specs=[hbm] * n,
        scratch_shapes=[pltpu.SemaphoreType.DMA((7 * n,)), pltpu.SemaphoreType.DMA((7 * n,)), pltpu.SemaphoreType.DMA((n,))],
        name="allgather_weights", compiler_params=_params())(*shards)


ALL_DEVICES = tuple(range(N_DEV))


def _exchange_plan(src_refs, land_refs, dests, send_sems, recv_sems, local_sems):
    x, y, c = _place()
    me = _flat(x, y, c)
    plan = []
    for a, (src, land, dl) in enumerate(zip(src_refs, land_refs, dests)):
        for li, j in enumerate(dl):
            to = ((j >> 2) & 1, (j >> 1) & 1, j & 1)
            block = src.at[li] if len(src.shape) == len(land.shape) else src

            def push(slot, a=a, block=block, land=land, j=j, to=to):
                return pltpu.make_async_remote_copy(
                    src_ref=block, dst_ref=land.at[slot], send_sem=send_sems.at[a * N_DEV + j],
                    recv_sem=recv_sems.at[a * N_DEV + slot], device_id=to, device_id_type=MESH)

            own = pltpu.make_async_copy(block, land.at[j], local_sems.at[a])
            plan.append((j, push(me), own, [push(s) for s in range(N_DEV) if s != j]))
    return me, plan


def _exchange_start(me, plan):
    for j, send, own, _ in plan:
        @pl.when(me != j)
        def _(send=send):
            send.start()

        @pl.when(me == j)
        def _(own=own):
            own.start()


def _exchange_wait(me, plan):
    for j, send, own, arrivals in plan:
        @pl.when(me != j)
        def _(send=send):
            send.wait_send()

        @pl.when(me == j)
        def _(own=own, arrivals=arrivals):
            own.wait()
            for arrival in arrivals:
                arrival.wait_recv()


def call_hosting_exchange(core, xch, *, grid, in_specs, out_specs, out_shape, scratch_shapes, name, operands):
    srcs, dests, landing = xch
    n, n_in, n_out, n_scr = len(srcs), len(in_specs), len(out_specs), len(scratch_shapes)

    def body(*refs):
        ins, src_refs = refs[:n_in], refs[n_in:n_in + n]
        outs = refs[n_in + 2 * n:n_in + 2 * n + n_out]
        land_refs = refs[n_in + 2 * n + n_out:n_in + 3 * n + n_out]
        scratch = refs[n_in + 3 * n + n_out:n_in + 3 * n + n_out + n_scr]
        sems = refs[n_in + 3 * n + n_out + n_scr:]
        first = functools.reduce(jnp.logical_and, [pl.program_id(i) == 0 for i in range(len(grid))])
        last = functools.reduce(jnp.logical_and, [pl.program_id(i) == grid[i] - 1 for i in range(len(grid))])
        me, plan = _exchange_plan(src_refs, land_refs, dests, *sems)

        @pl.when(first)
        def _():
            _exchange_start(me, plan)

        core(*ins, *outs, *scratch)

        @pl.when(last)
        def _():
            _exchange_wait(me, plan)

    hbm = pl.BlockSpec(memory_space=pl.ANY)
    res = pl.pallas_call(
        body, grid=grid,
        in_specs=list(in_specs) + [hbm] * (2 * n), out_specs=list(out_specs) + [hbm] * n,
        out_shape=list(out_shape) + [SDS(l.shape, l.dtype) for l in landing],
        scratch_shapes=list(scratch_shapes) + [pltpu.SemaphoreType.DMA((N_DEV * n,)), pltpu.SemaphoreType.DMA((N_DEV * n,)),
                                               pltpu.SemaphoreType.DMA((n,))],
        input_output_aliases={n_in + n + k: n_out + k for k in range(n)},
        name=name, compiler_params=_params(*(("arbitrary",) * len(grid))))(*operands, *srcs, *landing)
    return res[:n_out], res[n_out:]


SLOT_ROWS = 8


def small_allreduce_adamw(loss_sum, grads, ws, ms, vs):
    n = len(grads)
    rows = [g.shape[0] for g in grads]
    total = SLOT_ROWS * (n + 1)

    def body(*refs):
        loss_ref, g_refs, w_refs = refs[0], refs[1:1 + n], refs[1 + n:1 + 2 * n]
        m_refs, v_refs = refs[1 + 2 * n:1 + 3 * n], refs[1 + 3 * n:1 + 4 * n]
        outs = refs[1 + 4 * n:2 + 8 * n]
        vec, gath, tot, send_sems, recv_sems = refs[2 + 8 * n:]
        x, y, c = _place()
        me = _flat(x, y, c)
        vec[...] = jnp.zeros_like(vec)
        vec[0:1, :] = loss_ref[...]
        for i in range(n):
            vec[SLOT_ROWS * (i + 1):SLOT_ROWS * (i + 1) + rows[i], :] = g_refs[i][...]
        gath[me] = vec[...]
        copies = []
        for k in range(1, N_DEV):
            peer = _peer(x, y, c, k)
            copies.append(pltpu.make_async_remote_copy(
                src_ref=vec, dst_ref=gath.at[me], send_sem=send_sems.at[k - 1], recv_sem=recv_sems.at[k - 1],
                device_id=peer, device_id_type=MESH))
        for cp in copies:
            cp.start()
        for cp in copies:
            cp.wait_recv()
        for cp in copies:
            cp.wait_send()
        g = gath[0]
        for j in range(1, N_DEV):
            g = g + gath[j]
        tot[...] = g
        outs[0][...] = tot[0:1, :]
        for i in range(n):
            gi = tot[SLOT_ROWS * (i + 1):SLOT_ROWS * (i + 1) + rows[i], :]
            outs[1 + i][...] = gi
            outs[1 + n + i][...], outs[1 + 2 * n + i][...], outs[1 + 3 * n + i][...] = _adamw(
                w_refs[i][...], gi, m_refs[i][...], v_refs[i][...])

    shapes = [SDS(g.shape, F32) for g in grads]
    return pl.pallas_call(
        body, out_shape=[SDS((1, LANES), F32)] + shapes * 4,
        scratch_shapes=[pltpu.VMEM((total, LANES), F32), pltpu.VMEM((N_DEV, total, LANES), F32), pltpu.VMEM((total, LANES), F32),
                        pltpu.SemaphoreType.DMA((7,)), pltpu.SemaphoreType.DMA((7,))],
        name="small_allreduce_adamw", compiler_params=_params())(loss_sum, *grads, *ws, *ms, *vs)


def _rope_tables(positions):
    pos = positions.astype(F32).reshape(-1, 1)

    def cs(r):
        inv_freq = ROPE_THETA ** (-(jnp.arange(0, r, 2, dtype=F32) / r))
        ang = pos * inv_freq
        return jnp.cos(ang), jnp.sin(ang)

    n = pos.shape[0]
    one = lambda w: jnp.ones((n, w), F32)
    zero = lambda w: jnp.zeros((n, w), F32)
    ca, sa = cs(A_ROT)
    rest = A_HEAD_DIM - A_ROT
    a_c = jnp.tile(jnp.concatenate([ca, ca, one(rest)], 1), (1, 2))
    a_sa = jnp.tile(jnp.concatenate([-sa, zero(A_ROT // 2 + rest)], 1), (1, 2))
    a_sb = jnp.tile(jnp.concatenate([zero(A_ROT // 2), sa, zero(rest)], 1), (1, 2))
    cm, sm = cs(MLA_ROPE)
    tail = LANES - MLA_NOPE - MLA_ROPE
    m_c = jnp.concatenate([one(MLA_NOPE), cm, cm, one(tail)], 1)
    m_sa = jnp.concatenate([zero(MLA_NOPE), -sm, zero(MLA_ROPE // 2 + tail)], 1)
    m_sb = jnp.concatenate([zero(MLA_NOPE + MLA_ROPE // 2), sm, zero(tail)], 1)
    return (a_c, a_sa, a_sb), (m_c, m_sa, m_sb)


KR_LO, KR_HI = 4480, 4512
W_IN_SHARD = D_IN // N_DEV
AG_SPLIT = 5 * W_IN_SHARD - 3 * A_WIDTH
BG_SPLIT = 6 * W_IN_SHARD - KR_HI


def _w_in_working_t(g):
    w = g.reshape(D_IN, D_MODEL)
    z = lambda n: jnp.zeros((n, D_MODEL), w.dtype)
    return jnp.concatenate([w[:KR_LO], z(MLA_NOPE), w[KR_LO:KR_HI], z(LANES - MLA_NOPE - MLA_ROPE), w[KR_HI:]], 0)


def _shards(rows):
    return rows.reshape(-1, W_IN_SHARD, D_MODEL).astype(BF16)


def _w_in_shards_0_4(d_aq, d_ak, d_av, d_ag):
    return _shards(jnp.concatenate([d_aq, d_ak, d_av, d_ag[:AG_SPLIT]], 0))


def _w_in_shard_5(d_ag, d_cc, d_bg):
    kr = MLA_Q_RANK + MLA_KV_RANK + MLA_NOPE
    return _shards(jnp.concatenate([d_ag[AG_SPLIT:], d_cc[:MLA_Q_RANK + MLA_KV_RANK], d_cc[kr:kr + MLA_ROPE], d_bg[:BG_SPLIT]], 0))


def _w_in_shards_6_7(d_bg, d_mq, d_mg):
    return _shards(jnp.concatenate([d_bg[BG_SPLIT:], d_mq, d_mg], 0))


def _w_uq_working(g):
    w = jnp.pad(g.transpose(1, 0, 2), ((0, 0), (0, 0), (0, LANES - MLA_NOPE - MLA_ROPE)))
    return w.reshape(MLA_Q_RANK, MLA_QW)


def _w_uq_parts(dw):
    return dw.reshape(MLA_Q_RANK, MLA_HEADS, LANES)[:, :, :MLA_NOPE + MLA_ROPE].transpose(1, 0, 2)


def _w_ukv_working(g):
    wk = jnp.pad(g[:, :, :MLA_NOPE].transpose(1, 0, 2), ((0, 0), (0, 0), (0, LANES - MLA_NOPE)))
    wv = g[:, :, MLA_NOPE:].transpose(1, 0, 2)
    return jnp.concatenate([wk.reshape(MLA_KV_RANK, MLA_QW), wv.reshape(MLA_KV_RANK, MLA_WIDTH)], 1)


def _w_ukv_parts(dw):
    dk = dw[:, :MLA_QW].reshape(MLA_KV_RANK, MLA_HEADS, LANES)[:, :, :MLA_NOPE]
    dv = dw[:, MLA_QW:].reshape(MLA_KV_RANK, MLA_HEADS, MLA_V)
    return jnp.concatenate([dk, dv], -1).transpose(1, 0, 2)


SMALL_NAMES = ("g_emb", "b_emb", "g_cq", "g_ckv", "g_out_a", "g_out_b", "g_out_m", "g_post", "b_post")


def kernel(x, mem, positions, g_emb, b_emb, w_in, g_cq, g_ckv, w_uq, w_ukv, w_mem_kv, g_out_a, g_out_b, g_out_m, w_out, g_post, b_post, loss_target, m_g_emb, m_b_emb, m_w_in, m_g_cq, m_g_ckv, m_w_uq, m_w_ukv, m_w_mem_kv, m_g_out_a, m_g_out_b, m_g_out_m, m_w_out, m_g_post, m_b_post, v_g_emb, v_b_emb, v_w_in, v_g_cq, v_g_ckv, v_w_uq, v_w_ukv, v_w_mem_kv, v_g_out_a, v_g_out_b, v_g_out_m, v_w_out, v_g_post, v_b_post):
    nb = x.shape[0]
    t = nb * SEQ
    x2 = x.reshape(t, D_MODEL)
    tgt2 = loss_target.reshape(t, D_MODEL)
    mem2 = mem.reshape(nb * N_MEM, D_MODEL)
    g_emb2, b_emb2 = g_emb.reshape(1, -1), b_emb.reshape(1, -1)
    (a_c, a_sa, a_sb), (m_c, m_sa, m_sb) = _rope_tables(positions)

    w_in_t, m_w_in_t, v_w_in_t = w_in[0].T, m_w_in[0].T, v_w_in[0].T
    s_in, s_uq, s_ukv, s_mem, s_out = cast_shards((w_in_t, w_uq[0], w_ukv[0], w_mem_kv[0], w_out[0]))
    (g_in,) = allgather_weights((s_in,))
    win_t = _w_in_working_t(g_in)

    h32, h16 = ln_emb_fwd(x2, g_emb2, b_emb2)
    proj = mm_nn(h16, win_t, F32, 512, 1536, "proj", rhs_transposed=True)
    later = (s_uq, s_ukv, s_mem, s_out)
    (ya, lse_a), qkv_d, (g_uq, g_ukv, g_mem, g_out) = a_attn_fwd(
        proj, a_c, a_sa, a_sb, nb,
        (later, (ALL_DEVICES,) * len(later), tuple(lax.empty((N_DEV,) + w.shape, BF16) for w in later)))
    wuq_w = _w_uq_working(g_uq)
    wkv_w = _w_ukv_working(g_ukv)
    wmem = g_mem.reshape(D_MODEL, 2 * MEM_WIDTH)
    wout = g_out.reshape(D_MIX, D_MODEL)
    qb, kb, vb = mla_prep_fwd(proj, m_c, m_sa, m_sb, g_cq, g_ckv, wuq_w, wkv_w)
    yb, lse_b = mla_attn_fwd(qb, kb, vb, nb)
    mkv = mm_nn(mem2, wmem, BF16, nb * N_MEM, 512, "mem_kv")
    ym = mem_attn_fwd(proj, mkv, nb)
    z, du32, du16, loss_sum, dg_post, db_post = gate_out_ln_loss(
        ya, yb, ym, proj, g_out_a, g_out_b, g_out_m, wout, h32, tgt2, g_post, b_post)

    dya, dyb, dym, dag, dbg, dmg, dg_out_a, dg_out_b, dg_out_m = gate_bwd(
        du16, wout, ya, yb, ym, proj, g_out_a, g_out_b, g_out_m)
    dw_out = mm_tn(z, du16, 1024, "dw_out")
    dmq, dmk, dmv = mem_attn_bwd(proj, mkv, dym, nb)
    dw_mem = mm_tn(mem2, jnp.concatenate([dmk, dmv], 1), nb * N_MEM, "dw_mem")
    d_ag, d_bg, d_mq, d_mg = [mm_tn(p, h16, 2048, "dw_in_" + n) for n, p in (("ag", dag), ("bg", dbg), ("mq", dmq), ("mg", dmg))]
    landing = lambda w, dtype=F32: lax.empty((N_DEV,) + w.shape, dtype)
    big_w = (w_in_t, w_uq[0], w_ukv[0], w_mem_kv[0], w_out[0])
    (daq, dak, dav), (p_out, p_mem, p_in) = a_attn_bwd(
        qkv_d, a_c, a_sa, a_sb, dya, ya, lse_a, nb,
        ((dw_out.reshape(N_DEV, D_MIX // N_DEV, D_MODEL), dw_mem.reshape(N_DEV, D_MODEL // N_DEV, 2 * MEM_WIDTH),
          _w_in_shards_6_7(d_bg, d_mq, d_mg)),
         (ALL_DEVICES, ALL_DEVICES, (6, 7)),
         (landing(w_out[0]), landing(w_mem_kv[0]), landing(w_in_t, BF16))))
    d_aq, d_ak, d_av = [mm_tn(p, h16, 2048, "dw_in_" + n) for n, p in (("aq", daq), ("ak", dak), ("av", dav))]
    (dqb, dkb, dvb), (p_in,) = mla_attn_bwd(
        qb, kb, vb, dyb, yb, lse_b, nb, ((_w_in_shards_0_4(d_aq, d_ak, d_av, d_ag),), ((0, 1, 2, 3, 4),), (p_in,)))
    dcc, dqf, cqn, dkvf, ckvn, dg_cq, dg_ckv = mla_prep_bwd(proj, m_c, m_sa, m_sb, g_cq, g_ckv, wuq_w, wkv_w, dqb, dkb, dvb)
    dw_uq = mm_tn(cqn, dqf, 2048, "dw_uq")
    dw_ukv = mm_tn(ckvn, dkvf, 2048, "dw_ukv")
    d_cc = mm_tn(dcc, h16, 2048, "dw_in_cc")
    pieces = (daq, dak, dav, dag, dcc, dbg, dmq, dmg)
    (grad_x, dg_emb, db_emb), (p_in, p_uq, p_ukv) = dh_ln_bwd(
        pieces, win_t, du32, x2, g_emb2,
        ((_w_in_shard_5(d_ag, d_cc, d_bg), _w_uq_parts(dw_uq), _w_ukv_parts(dw_ukv)),
         ((5,), ALL_DEVICES, ALL_DEVICES),
         (p_in, landing(w_uq[0]), landing(w_ukv[0]))))

    parts = (p_in, p_uq, p_ukv, p_mem, p_out)
    big_m = (m_w_in_t, m_w_uq[0], m_w_ukv[0], m_w_mem_kv[0], m_w_out[0])
    big_v = (v_w_in_t, v_w_uq[0], v_w_ukv[0], v_w_mem_kv[0], v_w_out[0])
    big = {}
    for name, w, p, m, v in zip(("w_in", "w_uq", "w_ukv", "w_mem_kv", "w_out"), big_w, parts, big_m, big_v):
        res = adamw_shard(w, p, m, v, "adamw_" + name)
        big[name] = [(o.T if name == "w_in" else o)[None] for o in res]

    small_w = (g_emb, b_emb, g_cq, g_ckv, g_out_a, g_out_b, g_out_m, g_post, b_post)
    small_m = (m_g_emb, m_b_emb, m_g_cq, m_g_ckv, m_g_out_a, m_g_out_b, m_g_out_m, m_g_post, m_b_post)
    small_v = (v_g_emb, v_b_emb, v_g_cq, v_g_ckv, v_g_out_a, v_g_out_b, v_g_out_m, v_g_post, v_b_post)
    small_g = (dg_emb, db_emb, dg_cq, dg_ckv, dg_out_a, dg_out_b, dg_out_m, dg_post, db_post)
    rows128 = lambda vals: [v.reshape(-1, LANES) for v in vals]
    res = small_allreduce_adamw(loss_sum, rows128(small_g), rows128(small_w), rows128(small_m), rows128(small_v))
    loss = res[0][0, 0]
    n_small = len(small_w)
    sg, sd, sm, sv = [[r.reshape(w.shape) for r, w in zip(res[1 + k * n_small:1 + (k + 1) * n_small], small_w)]
                      for k in range(4)]

    order = ("g_emb", "b_emb", "w_in", "g_cq", "g_ckv", "w_uq", "w_ukv", "w_mem_kv", "g_out_a", "g_out_b", "g_out_m",
             "w_out", "g_post", "b_post")
    small_idx = {n: i for i, n in enumerate(SMALL_NAMES)}
    outs = [loss, grad_x.reshape(x.shape)]
    for kind in range(4):
        for name in order:
            outs.append(big[name][kind] if name in big else (sg, sd, sm, sv)[kind][small_idx[name]])
    return tuple(outs)
```

```python
import functools

import jax
import jax.numpy as jnp
from jax import lax
from jax.experimental import pallas as pl
from jax.experimental.pallas import tpu as pltpu

F32 = jnp.float32
BF16 = jnp.bfloat16
SDS = jax.ShapeDtypeStruct
MESH = pl.DeviceIdType.MESH

D_MODEL = 1024
SEQ = 2048
A_HEADS, A_HEAD_DIM, A_ROT = 16, 64, 16
A_WIDTH = 1024
DILATIONS = (1, 4, 16)
N_SIDE = 64
MLA_HEADS, MLA_Q_RANK, MLA_KV_RANK = 8, 256, 128
MLA_NOPE, MLA_ROPE, MLA_V = 64, 32, 64
MLA_WIDTH = 512
N_MEM, MEM_HEADS, MEM_HEAD_DIM, MEM_WIDTH = 256, 4, 128, 512
ROPE_THETA = 500000.0
NORM_EPS = 1e-5
NEG_INF = -1e30
ALPHA = 2.0 ** 0.25
D_IN = 6048
N_DEV = 8

ADAM_LR, ADAM_B1, ADAM_B2, ADAM_EPS, ADAM_WD, ADAM_STEP = 0.001, 0.9, 0.999, 1e-08, 0.01, 10

D_INW = 6144
PIECE_WIDTHS = (1024, 1024, 1024, 1024, 512, 512, 512, 512)
PIECE_OFFS = (0, 1024, 2048, 3072, 4096, 4608, 5120, 5632)
LANES = 128
VMEM_LIMIT = 56 * 1024 * 1024


def _params(*sem):
    kw = dict(vmem_limit_bytes=VMEM_LIMIT)
    if sem:
        kw["dimension_semantics"] = sem
    return pltpu.CompilerParams(**kw)


def _dot(a, b):
    return jnp.dot(a, b, preferred_element_type=F32)


def _dot_nt(a, b):
    return lax.dot_general(a, b, (((1,), (1,)), ((), ())), preferred_element_type=F32)


def _dot_tn(a, b):
    return lax.dot_general(a, b, (((0,), (0,)), ((), ())), preferred_element_type=F32)


def _sigmoid(x):
    return 1.0 / (1.0 + jnp.exp(-x))


def _rope_fwd(x, c, sa, sb, half):
    n = x.shape[-1]
    return x * c + pltpu.roll(x, n - half, 1) * sa + pltpu.roll(x, half, 1) * sb


def _rope_bwd(dy, c, sa, sb, half):
    n = dy.shape[-1]
    return dy * c + pltpu.roll(dy * sa, half, 1) + pltpu.roll(dy * sb, n - half, 1)


def mm_nn(a, b, out_dtype, tm, tn, name, rhs_transposed=False):
    m, k = a.shape
    n = b.shape[0] if rhs_transposed else b.shape[1]
    dot = _dot_nt if rhs_transposed else _dot

    def body(a_ref, b_ref, o_ref):
        o_ref[...] = dot(a_ref[...].astype(BF16), b_ref[...].astype(BF16)).astype(o_ref.dtype)

    b_spec = pl.BlockSpec((tn, k), lambda j, i: (j, 0)) if rhs_transposed else pl.BlockSpec((k, tn), lambda j, i: (0, j))
    return pl.pallas_call(
        body, grid=(n // tn, m // tm),
        in_specs=[pl.BlockSpec((tm, k), lambda j, i: (i, 0)), b_spec],
        out_specs=pl.BlockSpec((tm, tn), lambda j, i: (i, j)),
        out_shape=SDS((m, n), out_dtype), name=name,
        compiler_params=_params("parallel", "parallel"))(a, b)


def mm_tn(a, b, tt, name):
    t, m = a.shape
    n = b.shape[1]

    def body(a_ref, b_ref, o_ref):
        @pl.when(pl.program_id(0) == 0)
        def _():
            o_ref[...] = jnp.zeros_like(o_ref)

        o_ref[...] += _dot_tn(a_ref[...].astype(BF16), b_ref[...].astype(BF16))

    return pl.pallas_call(
        body, grid=(t // tt,),
        in_specs=[pl.BlockSpec((tt, m), lambda i: (i, 0)), pl.BlockSpec((tt, n), lambda i: (i, 0))],
        out_specs=pl.BlockSpec((m, n), lambda i: (0, 0)),
        out_shape=SDS((m, n), F32), name=name,
        compiler_params=_params("arbitrary"))(a, b)


def ln_emb_fwd(x2, g, b):
    t, d = x2.shape
    tm = 512

    def body(x_ref, g_ref, b_ref, h32_ref, h16_ref):
        x = x_ref[...]
        mu = jnp.mean(x, axis=-1, keepdims=True)
        xc = x - mu
        var = jnp.mean(xc * xc, axis=-1, keepdims=True)
        h = xc * lax.rsqrt(var + NORM_EPS) * g_ref[...] + b_ref[...]
        h32_ref[...] = h
        h16_ref[...] = h.astype(BF16)

    row = pl.BlockSpec((tm, d), lambda i: (i, 0))
    vec = pl.BlockSpec((1, d), lambda i: (0, 0))
    return pl.pallas_call(
        body, grid=(t // tm,), in_specs=[row, vec, vec], out_specs=[row, row],
        out_shape=[SDS((t, d), F32), SDS((t, d), BF16)], name="ln_emb_fwd",
        compiler_params=_params("parallel"))(x2, g, b)


Q_BLK = 128
UNROLL_FWD = 8
UNROLL_BWD = 8


def _pattern_geometry(d):
    length = SEQ // d
    nblk = length // Q_BLK
    kwin = min(2 * Q_BLK, length)
    return length, nblk, kwin


def _block_coords(idx, d):
    length, nblk, kwin = _pattern_geometry(d)
    r = lax.shift_right_logical(idx, nblk.bit_length() - 1)
    i = idx & (nblk - 1)
    q0 = pl.multiple_of(r * length + i * Q_BLK, Q_BLK)
    ks = jnp.clip(i * Q_BLK - N_SIDE, 0, length - kwin)
    k0 = pl.multiple_of(r * length + ks, N_SIDE)
    qpos = i * Q_BLK + lax.broadcasted_iota(jnp.int32, (Q_BLK, kwin), 0)
    kpos = ks + lax.broadcasted_iota(jnp.int32, (Q_BLK, kwin), 1)
    valid = jnp.abs(kpos - qpos) <= N_SIDE
    return q0, k0, kwin, valid


def _deinterleave(src_ref, dst_ref, d, dtype, tmp_ref):
    if d == 1:
        dst_ref[...] = src_ref[...].astype(dtype)
        return
    q = SEQ // 4
    if d == 4:
        for r in range(4):
            dst_ref[r * q:(r + 1) * q, :] = src_ref[pl.ds(r, q, stride=4), :].astype(dtype)
        return
    assert d == 16
    n = SEQ // 16
    for r in range(4):
        tmp_ref[r * q:(r + 1) * q, :] = src_ref[pl.ds(r, q, stride=4), :]
    for r in range(4):
        for j in range(4):
            dst_ref[(r + 4 * j) * n:(r + 4 * j + 1) * n, :] = tmp_ref[pl.ds(r * q + j, n, stride=4), :].astype(dtype)


def _interleave(src_ref, dst_ref, d, tmp_ref, accumulate):
    q = SEQ // 4
    if d == 16:
        n = SEQ // 16
        for r in range(4):
            for j in range(4):
                tmp_ref[pl.ds(r * q + j, n, stride=4), :] = src_ref[(r + 4 * j) * n:(r + 4 * j + 1) * n, :]
        src_ref = tmp_ref
    else:
        assert d == 4
    for r in range(4):
        rows = pl.ds(r, q, stride=4)
        val = src_ref[r * q:(r + 1) * q, :]
        dst_ref[rows, :] = dst_ref[rows, :] + val if accumulate else val


def a_attn_fwd(proj, ca, sa, sb, nb, xch):
    t = proj.shape[0]
    n_pairs = A_WIDTH // LANES

    def body(q_ref, k_ref, v_ref, c_ref, sa_ref, sb_ref, y_ref, lse_ref, *rest):
        qkv_d, (qr_s, kr_s, oc_s, lc_s, o1_s, l1_s, o2_s, l2_s, o3_s, l3_s, tmp_s) = rest[:9], rest[9:]
        c, s_a, s_b = c_ref[...], sa_ref[...], sb_ref[...]
        qr_s[...] = _rope_fwd(q_ref[...], c, s_a, s_b, A_ROT // 2) * (A_HEAD_DIM ** -0.5)
        kr_s[...] = _rope_fwd(k_ref[...], c, s_a, s_b, A_ROT // 2)
        head0 = lax.broadcasted_iota(jnp.int32, (Q_BLK, LANES), 1) < A_HEAD_DIM
        nat = ((o1_s, l1_s), (o2_s, l2_s), (o3_s, l3_s))

        for g, d in enumerate(DILATIONS):
            qd_s, kd_s, vd_s = qkv_d[3 * g:3 * g + 3]
            _deinterleave(qr_s, qd_s, d, BF16, tmp_s)
            _deinterleave(kr_s, kd_s, d, BF16, tmp_s)
            _deinterleave(v_ref, vd_s, d, BF16, tmp_s)
            o_dst, l_dst = (nat[g] if d == 1 else (oc_s, lc_s))

            def block(idx, carry, d=d, o_dst=o_dst, l_dst=l_dst, qd_s=qd_s, kd_s=kd_s, vd_s=vd_s):
                q0, k0, kwin, valid = _block_coords(idx, d)
                qb = qd_s[pl.ds(q0, Q_BLK), :]
                kb = kd_s[pl.ds(k0, kwin), :]
                vb = vd_s[pl.ds(k0, kwin), :]
                zero = jnp.zeros_like(qb)
                q2 = jnp.concatenate([jnp.where(head0, qb, zero), jnp.where(head0, zero, qb)], 0)
                s = jnp.where(jnp.concatenate([valid, valid], 0), _dot_nt(q2, kb), NEG_INF)
                m = jnp.max(s, axis=-1, keepdims=True)
                p = jnp.exp(s - m)
                l = jnp.sum(p, axis=-1, keepdims=True)
                o2 = _dot(p.astype(BF16), vb) / l
                l2 = m + jnp.log(l)
                o_dst[pl.ds(q0, Q_BLK), :] = jnp.where(head0, o2[:Q_BLK], o2[Q_BLK:])
                l_dst[pl.ds(q0, Q_BLK), :] = jnp.where(head0, l2[:Q_BLK], l2[Q_BLK:])
                return carry

            lax.fori_loop(0, SEQ // Q_BLK, block, 0, unroll=UNROLL_FWD)
            if d > 1:
                _interleave(oc_s, nat[g][0], d, tmp_s, False)
                _interleave(lc_s, nat[g][1], d, tmp_s, False)

        def merge(ci, carry):
            rows = pl.ds(pl.multiple_of(ci * 256, 256), 256)
            l1, l2, l3 = l1_s[rows, :], l2_s[rows, :], l3_s[rows, :]
            m = jnp.maximum(jnp.maximum(l1, l2), l3)
            w1, w2, w3 = jnp.exp(l1 - m), jnp.exp(l2 - m), jnp.exp(l3 - m)
            w = w1 + w2 + w3
            y_ref[rows, :] = (w1 * o1_s[rows, :] + w2 * o2_s[rows, :] + w3 * o3_s[rows, :]) / w
            lse_ref[rows, :] = m + jnp.log(w)
            return carry

        lax.fori_loop(0, SEQ // 256, merge, 0)

    def col(off):
        return pl.BlockSpec((SEQ, LANES), lambda b, hp: (b, off + hp))

    tab = pl.BlockSpec((SEQ, LANES), lambda b, hp: (b, 0))
    out = pl.BlockSpec((SEQ, LANES), lambda b, hp: (b, hp))
    f32s = pltpu.VMEM((SEQ, LANES), F32)
    res, landed = call_hosting_exchange(
        body, xch, grid=(nb, n_pairs),
        in_specs=[col(0), col(n_pairs), col(2 * n_pairs), tab, tab, tab],
        out_specs=[out] * 11,
        out_shape=[SDS((t, A_WIDTH), F32)] * 2 + [SDS((t, A_WIDTH), BF16)] * 9,
        scratch_shapes=[f32s] * 11,
        name="a_attn_fwd", operands=(proj, proj, proj, ca, sa, sb))
    return res[:2], res[2:], landed


def a_attn_bwd(qkv_d, ca, sa, sb, dy, y, lse, nb, xch):
    t = dy.shape[0]
    n_pairs = A_WIDTH // LANES

    def body(*refs):
        qkv_refs = refs[:9]
        (c_ref, sa_ref, sb_ref, do_ref, y_ref, lse_ref, dq_ref, dk_ref, dv_ref,
         l0n_s, l1n_s, d0n_s, d1n_s, dod_s, l0d_s, l1d_s, d0d_s, d1d_s,
         dqc_s, dkc_s, dvc_s, dqn_s, dkn_s, dvn_s, tmp_s) = refs[9:]
        c, s_a, s_b = c_ref[...], sa_ref[...], sb_ref[...]
        head0 = lax.broadcasted_iota(jnp.int32, (Q_BLK, LANES), 1) < A_HEAD_DIM

        def per_head_rows(ci, carry):
            rows = pl.ds(pl.multiple_of(ci * 256, 256), 256)
            h0 = lax.broadcasted_iota(jnp.int32, (256, LANES), 1) < A_HEAD_DIM
            tt = do_ref[rows, :] * y_ref[rows, :]
            d0n_s[rows, :] = jnp.broadcast_to(jnp.sum(jnp.where(h0, tt, 0.0), axis=-1, keepdims=True), (256, LANES))
            d1n_s[rows, :] = jnp.broadcast_to(jnp.sum(jnp.where(h0, 0.0, tt), axis=-1, keepdims=True), (256, LANES))
            l = lse_ref[rows, :]
            lr = pltpu.roll(l, A_HEAD_DIM, 1)
            l0n_s[rows, :] = jnp.where(h0, l, lr)
            l1n_s[rows, :] = jnp.where(h0, lr, l)
            return carry

        lax.fori_loop(0, SEQ // 256, per_head_rows, 0)
        assert DILATIONS[0] == 1

        for g, d in enumerate(DILATIONS):
            qd_s, kd_s, vd_s = qkv_refs[3 * g:3 * g + 3]
            _deinterleave(do_ref, dod_s, d, BF16, tmp_s)
            if d > 1:
                for src, dst in ((l0n_s, l0d_s), (l1n_s, l1d_s), (d0n_s, d0d_s), (d1n_s, d1d_s)):
                    _deinterleave(src, dst, d, F32, tmp_s)
            l0, l1, d0, d1 = (l0n_s, l1n_s, d0n_s, d1n_s) if d == 1 else (l0d_s, l1d_s, d0d_s, d1d_s)
            dq_dst, dk_dst, dv_dst = (dqn_s, dkn_s, dvn_s) if d == 1 else (dqc_s, dkc_s, dvc_s)
            dk_dst[...] = jnp.zeros_like(dk_dst)
            dv_dst[...] = jnp.zeros_like(dv_dst)

            def block(idx, carry, d=d, l0=l0, l1=l1, d0=d0, d1=d1, dq_dst=dq_dst, dk_dst=dk_dst, dv_dst=dv_dst,
                      qd_s=qd_s, kd_s=kd_s, vd_s=vd_s):
                q0, k0, kwin, valid = _block_coords(idx, d)
                qrows = pl.ds(q0, Q_BLK)
                krows = pl.ds(k0, kwin)
                qb, dob = qd_s[qrows, :], dod_s[qrows, :]
                kb, vb = kd_s[krows, :], vd_s[krows, :]
                zero = jnp.zeros_like(qb)
                q2 = jnp.concatenate([jnp.where(head0, qb, zero), jnp.where(head0, zero, qb)], 0)
                do2 = jnp.concatenate([jnp.where(head0, dob, zero), jnp.where(head0, zero, dob)], 0)
                wide = lambda x: jnp.concatenate([x] * (kwin // LANES), 1)
                lse2 = wide(jnp.concatenate([l0[qrows, :], l1[qrows, :]], 0))
                dd2 = wide(jnp.concatenate([d0[qrows, :], d1[qrows, :]], 0))
                s = jnp.where(jnp.concatenate([valid, valid], 0), _dot_nt(q2, kb), NEG_INF)
                p = jnp.exp(s - lse2)
                ds = (p * (_dot_nt(do2, vb) - dd2)).astype(BF16)
                dq2 = _dot(ds, kb)
                dq_dst[qrows, :] = jnp.where(head0, dq2[:Q_BLK], dq2[Q_BLK:])
                dk_dst[krows, :] += _dot_tn(ds, q2)
                dv_dst[krows, :] += _dot_tn(p.astype(BF16), do2)
                return carry

            lax.fori_loop(0, SEQ // Q_BLK, block, 0, unroll=UNROLL_BWD)
            if d > 1:
                _interleave(dqc_s, dqn_s, d, tmp_s, True)
                _interleave(dkc_s, dkn_s, d, tmp_s, True)
                _interleave(dvc_s, dvn_s, d, tmp_s, True)

        dq_ref[...] = _rope_bwd(dqn_s[...] * (A_HEAD_DIM ** -0.5), c, s_a, s_b, A_ROT // 2).astype(BF16)
        dk_ref[...] = _rope_bwd(dkn_s[...], c, s_a, s_b, A_ROT // 2).astype(BF16)
        dv_ref[...] = dvn_s[...].astype(BF16)

    tab = pl.BlockSpec((SEQ, LANES), lambda b, hp: (b, 0))
    blk = pl.BlockSpec((SEQ, LANES), lambda b, hp: (b, hp))
    f32s = pltpu.VMEM((SEQ, LANES), F32)
    b16s = pltpu.VMEM((SEQ, LANES), BF16)
    return call_hosting_exchange(
        body, xch, grid=(nb, n_pairs),
        in_specs=[blk] * 9 + [tab, tab, tab, blk, blk, blk],
        out_specs=[blk, blk, blk],
        out_shape=[SDS((t, A_WIDTH), BF16)] * 3,
        scratch_shapes=[f32s] * 4 + [b16s] + [f32s] * 11,
        name="a_attn_bwd", operands=(*qkv_d, ca, sa, sb, dy, y, lse))


MLA_SCALE = (MLA_NOPE + MLA_ROPE) ** -0.5
MLA_QW = MLA_HEADS * LANES
MLA_KVW = MLA_QW + MLA_WIDTH


def _rms(x, g):
    r = lax.rsqrt(jnp.mean(x * x, axis=-1, keepdims=True) + NORM_EPS)
    return x * r * g, r


def _rms_bwd(dn, x, r, g):
    tg = dn * g
    dx = r * tg - x * (r * r * r) * jnp.mean(tg * x, axis=-1, keepdims=True)
    return dx, jnp.sum(dn * x * r, axis=0, keepdims=True)


def mla_prep_fwd(proj, cm, sma, smb, g_cq, g_ckv, wuq, wkv):
    t = proj.shape[0]
    tm = 512

    def body(cq_ref, ckv_ref, kr_ref, c_ref, sa_ref, sb_ref, gq_ref, gkv_ref, wuq_ref, wkv_ref, q_ref, k_ref, v_ref):
        c, s_a, s_b = c_ref[...], sa_ref[...], sb_ref[...]
        cqn, _ = _rms(cq_ref[...], gq_ref[...])
        qf = _dot(cqn.astype(BF16), wuq_ref[...])
        ckvn, _ = _rms(ckv_ref[...], gkv_ref[...])
        kvf = _dot(ckvn.astype(BF16), wkv_ref[...])
        krope = _rope_fwd(kr_ref[...], c, s_a, s_b, MLA_ROPE // 2)
        for h in range(MLA_HEADS):
            cols = slice(h * LANES, (h + 1) * LANES)
            q_ref[:, cols] = (_rope_fwd(qf[:, cols], c, s_a, s_b, MLA_ROPE // 2) * MLA_SCALE).astype(BF16)
            k_ref[:, cols] = (kvf[:, cols] + krope).astype(BF16)
        v_ref[...] = kvf[:, MLA_QW:].astype(BF16)

    def row(w, j):
        return pl.BlockSpec((tm, w), lambda i: (i, j))

    def full(a):
        return pl.BlockSpec(a.shape, lambda i: (0, 0))

    return pl.pallas_call(
        body, grid=(t // tm,),
        in_specs=[row(256, 4096 // 256), row(128, 4352 // 128), row(128, 4480 // 128), row(128, 0), row(128, 0), row(128, 0),
                  full(g_cq), full(g_ckv), full(wuq), full(wkv)],
        out_specs=[row(MLA_QW, 0), row(MLA_QW, 0), row(MLA_WIDTH, 0)],
        out_shape=[SDS((t, MLA_QW), BF16), SDS((t, MLA_QW), BF16), SDS((t, MLA_WIDTH), BF16)],
        name="mla_prep_fwd", compiler_params=_params("parallel"))(proj, proj, proj, cm, sma, smb, g_cq, g_ckv, wuq, wkv)


def mla_prep_bwd(proj, cm, sma, smb, g_cq, g_ckv, wuq, wkv, dq, dk, dv):
    t = proj.shape[0]
    tm = 512

    def body(cq_ref, ckv_ref, c_ref, sa_ref, sb_ref, gq_ref, gkv_ref, wuq_ref, wkv_ref, dq_ref, dk_ref, dv_ref,
             dcc_ref, dqf_ref, cqn_ref, dkvf_ref, ckvn_ref, dgq_ref, dgkv_ref):
        @pl.when(pl.program_id(0) == 0)
        def _():
            dgq_ref[...] = jnp.zeros_like(dgq_ref)
            dgkv_ref[...] = jnp.zeros_like(dgkv_ref)

        c, s_a, s_b = c_ref[...], sa_ref[...], sb_ref[...]
        cq, ckv = cq_ref[...], ckv_ref[...]
        cqn, rq = _rms(cq, gq_ref[...])
        ckvn, rkv = _rms(ckv, gkv_ref[...])
        cqn_ref[...] = cqn.astype(BF16)
        ckvn_ref[...] = ckvn.astype(BF16)
        lane = lax.broadcasted_iota(jnp.int32, (tm, LANES), 1)
        rope_lanes = (lane >= MLA_NOPE) & (lane < MLA_NOPE + MLA_ROPE)
        dkrope = jnp.zeros((tm, LANES), F32)
        for h in range(MLA_HEADS):
            cols = slice(h * LANES, (h + 1) * LANES)
            dqf_ref[:, cols] = _rope_bwd(dq_ref[:, cols] * MLA_SCALE, c, s_a, s_b, MLA_ROPE // 2).astype(BF16)
            dkh = dk_ref[:, cols]
            dkvf_ref[:, cols] = dkh.astype(BF16)
            dkrope = dkrope + dkh
        dkvf_ref[:, MLA_QW:] = dv_ref[...].astype(BF16)
        dkr = _rope_bwd(jnp.where(rope_lanes, dkrope, 0.0), c, s_a, s_b, MLA_ROPE // 2)
        dcqn = _dot_nt(dqf_ref[...], wuq_ref[...])
        dckvn = _dot_nt(dkvf_ref[...], wkv_ref[...])
        dcq, dgq = _rms_bwd(dcqn, cq, rq, gq_ref[...])
        dckv, dgkv = _rms_bwd(dckvn, ckv, rkv, gkv_ref[...])
        dgq_ref[...] += dgq
        dgkv_ref[...] += dgkv
        dcc_ref[:, 0:256] = dcq.astype(BF16)
        dcc_ref[:, 256:384] = dckv.astype(BF16)
        dcc_ref[:, 384:512] = dkr.astype(BF16)

    def row(w, j):
        return pl.BlockSpec((tm, w), lambda i: (i, j))

    def full(a):
        return pl.BlockSpec(a.shape, lambda i: (0, 0))

    return pl.pallas_call(
        body, grid=(t // tm,),
        in_specs=[row(256, 4096 // 256), row(128, 4352 // 128), row(128, 0), row(128, 0), row(128, 0),
                  full(g_cq), full(g_ckv), full(wuq), full(wkv), row(MLA_QW, 0), row(MLA_QW, 0), row(MLA_WIDTH, 0)],
        out_specs=[row(512, 0), row(MLA_QW, 0), row(256, 0), row(MLA_KVW, 0), row(128, 0), full(g_cq), full(g_ckv)],
        out_shape=[SDS((t, 512), BF16), SDS((t, MLA_QW), BF16), SDS((t, 256), BF16), SDS((t, MLA_KVW), BF16),
                   SDS((t, 128), BF16), SDS(g_cq.shape, F32), SDS(g_ckv.shape, F32)],
        name="mla_prep_bwd", compiler_params=_params("arbitrary"))(proj, proj, cm, sma, smb, g_cq, g_ckv, wuq, wkv, dq, dk, dv)


MLA_TQ = 256


def mla_attn_fwd(qb, kb, vb, nb):
    t = qb.shape[0]
    nq = SEQ // MLA_TQ
    n_pairs = MLA_HEADS // 2

    def body(q_ref, k_ref, v_ref, y_ref, lse_ref):
        head0 = lax.broadcasted_iota(jnp.int32, (MLA_TQ, LANES), 1) < MLA_V
        v = v_ref[...]
        outs, lses = [], []
        for h in range(2):
            cols = slice(h * LANES, (h + 1) * LANES)
            s = _dot_nt(q_ref[:, cols], k_ref[:, cols])
            m = jnp.max(s, axis=-1, keepdims=True)
            p = jnp.exp(s - m)
            l = jnp.sum(p, axis=-1, keepdims=True)
            outs.append(_dot(p.astype(BF16), v) / l)
            lses.append(m + jnp.log(l))
        y_ref[...] = jnp.where(head0, outs[0], outs[1])
        lse_ref[...] = jnp.where(head0, lses[0], lses[1])

    return pl.pallas_call(
        body, grid=(nb, n_pairs, nq),
        in_specs=[pl.BlockSpec((MLA_TQ, 2 * LANES), lambda b, hp, i: (b * nq + i, hp)),
                  pl.BlockSpec((SEQ, 2 * LANES), lambda b, hp, i: (b, hp)),
                  pl.BlockSpec((SEQ, LANES), lambda b, hp, i: (b, hp))],
        out_specs=[pl.BlockSpec((MLA_TQ, LANES), lambda b, hp, i: (b * nq + i, hp))] * 2,
        out_shape=[SDS((t, MLA_WIDTH), F32)] * 2,
        name="mla_attn_fwd", compiler_params=_params("parallel", "parallel", "parallel"))(qb, kb, vb)


def mla_attn_bwd(qb, kb, vb, dy, y, lse, nb, xch):
    t = qb.shape[0]
    nq = SEQ // MLA_TQ
    n_pairs = MLA_HEADS // 2

    def body(q_ref, k_ref, v_ref, do_ref, y_ref, lse_ref, dq_ref, dk_ref, dv_ref):
        @pl.when(pl.program_id(2) == 0)
        def _():
            dk_ref[...] = jnp.zeros_like(dk_ref)
            dv_ref[...] = jnp.zeros_like(dv_ref)

        head0 = lax.broadcasted_iota(jnp.int32, (MLA_TQ, LANES), 1) < MLA_V
        v = v_ref[...]
        do = do_ref[...]
        lse = lse_ref[...]
        tt = do * y_ref[...]
        dv = jnp.zeros((SEQ, LANES), F32)
        for h in range(2):
            sel = head0 if h == 0 else ~head0
            lo = h * MLA_V
            cols = slice(h * LANES, (h + 1) * LANES)
            q = q_ref[:, cols]
            k = k_ref[:, cols]
            dd = jnp.sum(jnp.where(sel, tt, 0.0), axis=-1, keepdims=True)
            doh = jnp.where(sel, do, 0.0).astype(BF16)
            p = jnp.exp(_dot_nt(q, k) - lse[:, lo:lo + 1])
            dp = _dot_nt(doh, v)
            ds = (p * (dp - dd)).astype(BF16)
            dq_ref[:, cols] = _dot(ds, k)
            dk_ref[:, cols] += _dot_tn(ds, q)
            dv = dv + _dot_tn(p.astype(BF16), doh)
        dv_ref[...] += dv

    qspec = pl.BlockSpec((MLA_TQ, 2 * LANES), lambda b, hp, i: (b * nq + i, hp))
    kspec = pl.BlockSpec((SEQ, 2 * LANES), lambda b, hp, i: (b, hp))
    vspec = pl.BlockSpec((SEQ, LANES), lambda b, hp, i: (b, hp))
    ospec = pl.BlockSpec((MLA_TQ, LANES), lambda b, hp, i: (b * nq + i, hp))
    return call_hosting_exchange(
        body, xch, grid=(nb, n_pairs, nq),
        in_specs=[qspec, kspec, vspec, ospec, ospec, ospec],
        out_specs=[qspec, kspec, vspec],
        out_shape=[SDS((t, MLA_QW), F32), SDS((t, MLA_QW), F32), SDS((t, MLA_WIDTH), F32)],
        scratch_shapes=[], name="mla_attn_bwd", operands=(qb, kb, vb, dy, y, lse))


MEM_TQ = 512
MEM_SCALE = MEM_HEAD_DIM ** -0.5
MQ_BLK4 = 5120 // MEM_WIDTH


def mem_attn_fwd(proj, mkv, nb):
    t = proj.shape[0]
    nq = SEQ // MEM_TQ

    def body(q_ref, mk_ref, mv_ref, y_ref):
        for h in range(MEM_HEADS):
            cols = slice(h * LANES, (h + 1) * LANES)
            s = _dot_nt(q_ref[:, cols].astype(BF16), mk_ref[:, cols]) * MEM_SCALE
            m = jnp.max(s, axis=-1, keepdims=True)
            p = jnp.exp(s - m)
            l = jnp.sum(p, axis=-1, keepdims=True)
            y_ref[:, cols] = _dot(p.astype(BF16), mv_ref[:, cols]) / l

    return pl.pallas_call(
        body, grid=(nb, nq),
        in_specs=[pl.BlockSpec((MEM_TQ, MEM_WIDTH), lambda b, i: (b * nq + i, MQ_BLK4)),
                  pl.BlockSpec((N_MEM, MEM_WIDTH), lambda b, i: (b, 0)),
                  pl.BlockSpec((N_MEM, MEM_WIDTH), lambda b, i: (b, 1))],
        out_specs=pl.BlockSpec((MEM_TQ, MEM_WIDTH), lambda b, i: (b * nq + i, 0)),
        out_shape=SDS((t, MEM_WIDTH), F32),
        name="mem_attn_fwd", compiler_params=_params("parallel", "parallel"))(proj, mkv, mkv)


def mem_attn_bwd(proj, mkv, dy, nb):
    t = proj.shape[0]
    nq = SEQ // MEM_TQ

    def body(q_ref, mk_ref, mv_ref, do_ref, dq_ref, dmk_ref, dmv_ref):
        @pl.when(pl.program_id(1) == 0)
        def _():
            dmk_ref[...] = jnp.zeros_like(dmk_ref)
            dmv_ref[...] = jnp.zeros_like(dmv_ref)

        for h in range(MEM_HEADS):
            cols = slice(h * LANES, (h + 1) * LANES)
            q = q_ref[:, cols].astype(BF16)
            mk, mv = mk_ref[:, cols], mv_ref[:, cols]
            do = do_ref[:, cols].astype(BF16)
            s = _dot_nt(q, mk) * MEM_SCALE
            e = jnp.exp(s - jnp.max(s, axis=-1, keepdims=True))
            p = e / jnp.sum(e, axis=-1, keepdims=True)
            dp = _dot_nt(do, mv)
            ds = (p * (dp - jnp.sum(p * dp, axis=-1, keepdims=True)) * MEM_SCALE).astype(BF16)
            dq_ref[:, cols] = _dot(ds, mk).astype(BF16)
            dmk_ref[:, cols] += _dot_tn(ds, q)
            dmv_ref[:, cols] += _dot_tn(p.astype(BF16), do)

    ospec = pl.BlockSpec((MEM_TQ, MEM_WIDTH), lambda b, i: (b * nq + i, 0))
    kspec = pl.BlockSpec((N_MEM, MEM_WIDTH), lambda b, i: (b, 0))
    return pl.pallas_call(
        body, grid=(nb, nq),
        in_specs=[pl.BlockSpec((MEM_TQ, MEM_WIDTH), lambda b, i: (b * nq + i, MQ_BLK4)),
                  kspec, pl.BlockSpec((N_MEM, MEM_WIDTH), lambda b, i: (b, 1)), ospec],
        out_specs=[ospec, kspec, kspec],
        out_shape=[SDS((t, MEM_WIDTH), BF16), SDS((nb * N_MEM, MEM_WIDTH), F32), SDS((nb * N_MEM, MEM_WIDTH), F32)],
        name="mem_attn_bwd", compiler_params=_params("parallel", "arbitrary"))(proj, mkv, mkv, dy)


ROW_TM = 512
AG_BLK = 3072 // 1024
BG_BLK = 4608 // 512
MG_BLK = 5632 // 512
GROUPS = ((0, A_WIDTH), (A_WIDTH, MLA_WIDTH), (A_WIDTH + MLA_WIDTH, MEM_WIDTH))
D_MIX = 2048


def _gate_specs():
    def row(w, j):
        return pl.BlockSpec((ROW_TM, w), lambda i: (i, j))

    def vec(w):
        return pl.BlockSpec((1, w), lambda i: (0, 0))

    ys = [row(A_WIDTH, 0), row(MLA_WIDTH, 0), row(MEM_WIDTH, 0)]
    gates = [row(A_WIDTH, AG_BLK), row(MLA_WIDTH, BG_BLK), row(MEM_WIDTH, MG_BLK)]
    gains = [vec(A_WIDTH), vec(MLA_WIDTH), vec(MEM_WIDTH)]
    return row, vec, ys, gates, gains


def gate_out_ln_loss(ya, yb, ym, proj, goa, gob, gom, wout, h32, target, gp, bp):
    t, d = h32.shape
    _, _, ys, gates, gains = _gate_specs()

    def body(ya_ref, yb_ref, ym_ref, ga_ref, gb_ref, gm_ref, goa_ref, gob_ref, gom_ref, w_ref, h_ref, t_ref, gp_ref, bp_ref,
             z_ref, du32_ref, du16_ref, loss_ref, dgp_ref, dbp_ref):
        @pl.when(pl.program_id(0) == 0)
        def _():
            loss_ref[...] = jnp.zeros_like(loss_ref)
            dgp_ref[...] = jnp.zeros_like(dgp_ref)
            dbp_ref[...] = jnp.zeros_like(dbp_ref)

        for (off, w), y_ref, g_ref, go_ref in zip(GROUPS, (ya_ref, yb_ref, ym_ref), (ga_ref, gb_ref, gm_ref),
                                                  (goa_ref, gob_ref, gom_ref)):
            n, _ = _rms(y_ref[...], go_ref[...])
            gt = g_ref[...]
            z_ref[:, off:off + w] = (n * (gt * _sigmoid(gt))).astype(BF16)
        g = gp_ref[...]
        u = ALPHA * h_ref[...] + _dot(z_ref[...], w_ref[...])
        mu = jnp.mean(u, axis=-1, keepdims=True)
        uc = u - mu
        rstd = lax.rsqrt(jnp.mean(uc * uc, axis=-1, keepdims=True) + NORM_EPS)
        xhat = uc * rstd
        err = xhat * g + bp_ref[...] - t_ref[...]
        tok = jnp.sum(err * err, axis=-1, keepdims=True) * (1.0 / d)
        loss_ref[...] += 0.5 * jnp.sum(tok, axis=0, keepdims=True)
        dout = err * (1.0 / d)
        dxhat = dout * g
        du = rstd * (dxhat - jnp.mean(dxhat, axis=-1, keepdims=True)
                     - xhat * jnp.mean(dxhat * xhat, axis=-1, keepdims=True))
        du32_ref[...] = du
        du16_ref[...] = du.astype(BF16)
        dgp_ref[...] += jnp.sum(dout * xhat, axis=0, keepdims=True)
        dbp_ref[...] += jnp.sum(dout, axis=0, keepdims=True)

    row = pl.BlockSpec((ROW_TM, d), lambda i: (i, 0))
    vec = pl.BlockSpec((1, d), lambda i: (0, 0))
    zrow = pl.BlockSpec((ROW_TM, D_MIX), lambda i: (i, 0))
    return pl.pallas_call(
        body, grid=(t // ROW_TM,),
        in_specs=ys + gates + gains + [pl.BlockSpec((D_MIX, d), lambda i: (0, 0)), row, row, vec, vec],
        out_specs=[zrow, row, row, pl.BlockSpec((1, LANES), lambda i: (0, 0)), vec, vec],
        out_shape=[SDS((t, D_MIX), BF16), SDS((t, d), F32), SDS((t, d), BF16), SDS((1, LANES), F32), SDS((1, d), F32),
                   SDS((1, d), F32)],
        name="gate_out_ln_loss", compiler_params=_params("arbitrary"))(
            ya, yb, ym, proj, proj, proj, goa, gob, gom, wout, h32, target, gp, bp)


def gate_bwd(du16, wout, ya, yb, ym, proj, goa, gob, gom):
    t = ya.shape[0]
    row, vec, ys, gates, gains = _gate_specs()

    def body(du_ref, w_ref, ya_ref, yb_ref, ym_ref, ga_ref, gb_ref, gm_ref, goa_ref, gob_ref, gom_ref,
             dya_ref, dyb_ref, dym_ref, dga_ref, dgb_ref, dgm_ref, dgoa_ref, dgob_ref, dgom_ref):
        @pl.when(pl.program_id(0) == 0)
        def _():
            dgoa_ref[...] = jnp.zeros_like(dgoa_ref)
            dgob_ref[...] = jnp.zeros_like(dgob_ref)
            dgom_ref[...] = jnp.zeros_like(dgom_ref)

        dz = _dot_nt(du_ref[...], w_ref[...])
        for (off, w), y_ref, g_ref, go_ref, dy_ref, dg_ref, dgo_ref in zip(
                GROUPS, (ya_ref, yb_ref, ym_ref), (ga_ref, gb_ref, gm_ref), (goa_ref, gob_ref, gom_ref),
                (dya_ref, dyb_ref, dym_ref), (dga_ref, dgb_ref, dgm_ref), (dgoa_ref, dgob_ref, dgom_ref)):
            dzg = dz[:, off:off + w]
            y, gt, go = y_ref[...], g_ref[...], go_ref[...]
            n, r = _rms(y, go)
            sg = _sigmoid(gt)
            dg_ref[...] = (dzg * n * (sg * (1.0 + gt * (1.0 - sg)))).astype(BF16)
            dy, dgo = _rms_bwd(dzg * (gt * sg), y, r, go)
            dy_ref[...] = dy
            dgo_ref[...] += dgo

    widths = (A_WIDTH, MLA_WIDTH, MEM_WIDTH)
    return pl.pallas_call(
        body, grid=(t // ROW_TM,),
        in_specs=[row(D_MODEL, 0), pl.BlockSpec((D_MIX, D_MODEL), lambda i: (0, 0))] + ys + gates + gains,
        out_specs=[row(w, 0) for w in widths] * 2 + [vec(w) for w in widths],
        out_shape=[SDS((t, w), F32) for w in widths] + [SDS((t, w), BF16) for w in widths] + [SDS((1, w), F32) for w in widths],
        name="gate_bwd", compiler_params=_params("arbitrary"))(du16, wout, ya, yb, ym, proj, proj, proj, goa, gob, gom)


def dh_ln_bwd(pieces, win_t, du32, x2, g_emb, xch):
    t, d = x2.shape

    def body(*refs):
        p_refs = refs[:len(pieces)]
        w_ref, du_ref, x_ref, g_ref, dx_ref, dg_ref, db_ref = refs[len(pieces):]

        @pl.when(pl.program_id(0) == 0)
        def _():
            dg_ref[...] = jnp.zeros_like(dg_ref)
            db_ref[...] = jnp.zeros_like(db_ref)

        dh = ALPHA * du_ref[...]
        for p_ref, off, w in zip(p_refs, PIECE_OFFS, PIECE_WIDTHS):
            dh = dh + _dot(p_ref[...], w_ref[off:off + w, :])
        x = x_ref[...]
        xc = x - jnp.mean(x, axis=-1, keepdims=True)
        rstd = lax.rsqrt(jnp.mean(xc * xc, axis=-1, keepdims=True) + NORM_EPS)
        xhat = xc * rstd
        dg_ref[...] += jnp.sum(dh * xhat, axis=0, keepdims=True)
        db_ref[...] += jnp.sum(dh, axis=0, keepdims=True)
        tg = dh * g_ref[...]
        dx_ref[...] = rstd * (tg - jnp.mean(tg, axis=-1, keepdims=True)
                              - xhat * jnp.mean(tg * xhat, axis=-1, keepdims=True))

    row = pl.BlockSpec((ROW_TM, d), lambda i: (i, 0))
    vec = pl.BlockSpec((1, d), lambda i: (0, 0))
    return call_hosting_exchange(
        body, xch, grid=(t // ROW_TM,),
        in_specs=[pl.BlockSpec((ROW_TM, w), lambda i: (i, 0)) for w in PIECE_WIDTHS]
        + [pl.BlockSpec(win_t.shape, lambda i: (0, 0)), row, row, vec],
        out_specs=[row, vec, vec],
        out_shape=[SDS((t, d), F32), SDS((1, d), F32), SDS((1, d), F32)],
        scratch_shapes=[], name="dh_ln_bwd", operands=(*pieces, win_t, du32, x2, g_emb))


def _adamw(w, g, m, v):
    m2 = ADAM_B1 * m + (1.0 - ADAM_B1) * g
    v2 = ADAM_B2 * v + (1.0 - ADAM_B2) * (g * g)
    m_hat = m2 / (1.0 - ADAM_B1 ** ADAM_STEP)
    v_hat = v2 / (1.0 - ADAM_B2 ** ADAM_STEP)
    return -ADAM_LR * (m_hat / (jnp.sqrt(v_hat) + ADAM_EPS) + ADAM_WD * w), m2, v2


def adamw_shard(w, parts, m, v, name):
    r, c = w.shape
    if r % 256 == 0 or r * c <= 256 * 1024:
        tr, tc = min(r, 256), c
    else:
        tr, tc = r, 256

    def body(w_ref, p_ref, m_ref, v_ref, g_ref, d_ref, nm_ref, nv_ref):
        g = p_ref[0].astype(F32)
        for k in range(1, N_DEV):
            g = g + p_ref[k].astype(F32)
        g_ref[...] = g
        d_ref[...], nm_ref[...], nv_ref[...] = _adamw(w_ref[...], g, m_ref[...], v_ref[...])

    blk = pl.BlockSpec((tr, tc), lambda i, j: (i, j))
    return pl.pallas_call(
        body, grid=(r // tr, c // tc),
        in_specs=[blk, pl.BlockSpec((N_DEV, tr, tc), lambda i, j: (0, i, j)), blk, blk],
        out_specs=[blk] * 4, out_shape=[SDS((r, c), F32)] * 4, name=name,
        compiler_params=_params("parallel", "parallel"))(w, parts, m, v)


def _place():
    return lax.axis_index("x"), lax.axis_index("y"), lax.axis_index("c")


def _flat(px, py, pc):
    return 4 * px + 2 * py + pc


def _peer(x, y, c, k):
    return (1 - x if k & 4 else x, 1 - y if k & 2 else y, 1 - c if k & 1 else c)


def cast_shards(shards):
    def body(*refs):
        n = len(refs) // 2
        for i_ref, o_ref in zip(refs[:n], refs[n:]):
            o_ref[...] = i_ref[...].astype(BF16)

    return pl.pallas_call(body, out_shape=[SDS(s.shape, BF16) for s in shards], name="cast_shards",
                          compiler_params=_params())(*shards)


def allgather_weights(shards):
    n = len(shards)

    def body(*refs):
        ins, outs = refs[:n], refs[n:2 * n]
        send_sems, recv_sems, local_sems = refs[2 * n:]
        x, y, c = _place()
        me, sib = (x, y, c), (x, y, 1 - c)
        chips = [(1 - x, y), (x, 1 - y), (1 - x, 1 - y)]

        def copy(a, k, block, to, src=None):
            dst = outs[a].at[_flat(*block)]
            return pltpu.make_async_remote_copy(
                src_ref=dst if src is None else src, dst_ref=dst,
                send_sem=send_sems.at[a * 7 + k], recv_sem=recv_sems.at[a * 7 + k],
                device_id=to, device_id_type=MESH)

        mine = [pltpu.make_async_copy(ins[a], outs[a].at[_flat(*me)], local_sems.at[a]) for a in range(n)]
        for cp in mine:
            cp.start()
        first = []
        for a in range(n):
            first.append(copy(a, 0, me, sib, src=ins[a]))
            first += [copy(a, 1 + j, me, (*chip, c), src=ins[a]) for j, chip in enumerate(chips)]
        for cp in first:
            cp.start()
        passed = []
        for j, chip in enumerate(chips):
            for a in range(n):
                copy(a, 1 + j, (*chip, c), me).wait_recv()
                fwd = copy(a, 4 + j, (*chip, c), sib)
                fwd.start()
                passed.append(fwd)
        for a in range(n):
            copy(a, 0, sib, me).wait_recv()
            for j, chip in enumerate(chips):
                copy(a, 4 + j, (*chip, 1 - c), me).wait_recv()
        for cp in first + passed:
            cp.wait_send()
        for cp in mine:
            cp.wait()

    hbm = pl.BlockSpec(memory_space=pl.ANY)
    return pl.pallas_call(
        body, out_shape=[SDS((N_DEV,) + s.shape, s.dtype) for s in shards],
        in_specs=[hbm] * n, out_specs=[hbm] * n,
        scratch_shapes=[pltpu.SemaphoreType.DMA((7 * n,)), pltpu.SemaphoreType.DMA((7 * n,)), pltpu.SemaphoreType.DMA((n,))],
        name="allgather_weights", compiler_params=_params())(*shards)


ALL_DEVICES = tuple(range(N_DEV))


def _exchange_plan(src_refs, land_refs, dests, send_sems, recv_sems, local_sems):
    x, y, c = _place()
    me = _flat(x, y, c)
    plan = []
    for a, (src, land, dl) in enumerate(zip(src_refs, land_refs, dests)):
        for li, j in enumerate(dl):
            to = ((j >> 2) & 1, (j >> 1) & 1, j & 1)
            block = src.at[li] if len(src.shape) == len(land.shape) else src

            def push(slot, a=a, block=block, land=land, j=j, to=to):
                return pltpu.make_async_remote_copy(
                    src_ref=block, dst_ref=land.at[slot], send_sem=send_sems.at[a * N_DEV + j],
                    recv_sem=recv_sems.at[a * N_DEV + slot], device_id=to, device_id_type=MESH)

            own = pltpu.make_async_copy(block, land.at[j], local_sems.at[a])
            plan.append((j, push(me), own, [push(s) for s in range(N_DEV) if s != j]))
    return me, plan


def _exchange_start(me, plan):
    for j, send, own, _ in plan:
        @pl.when(me != j)
        def _(send=send):
            send.start()

        @pl.when(me == j)
        def _(own=own):
            own.start()


def _exchange_wait(me, plan):
    for j, send, own, arrivals in plan:
        @pl.when(me != j)
        def _(send=send):
            send.wait_send()

        @pl.when(me == j)
        def _(own=own, arrivals=arrivals):
            own.wait()
            for arrival in arrivals:
                arrival.wait_recv()


def call_hosting_exchange(core, xch, *, grid, in_specs, out_specs, out_shape, scratch_shapes, name, operands):
    srcs, dests, landing = xch
    n, n_in, n_out, n_scr = len(srcs), len(in_specs), len(out_specs), len(scratch_shapes)

    def body(*refs):
        ins, src_refs = refs[:n_in], refs[n_in:n_in + n]
        outs = refs[n_in + 2 * n:n_in + 2 * n + n_out]
        land_refs = refs[n_in + 2 * n + n_out:n_in + 3 * n + n_out]
        scratch = refs[n_in + 3 * n + n_out:n_in + 3 * n + n_out + n_scr]
        sems = refs[n_in + 3 * n + n_out + n_scr:]
        first = functools.reduce(jnp.logical_and, [pl.program_id(i) == 0 for i in range(len(grid))])
        last = functools.reduce(jnp.logical_and, [pl.program_id(i) == grid[i] - 1 for i in range(len(grid))])
        me, plan = _exchange_plan(src_refs, land_refs, dests, *sems)

        @pl.when(first)
        def _():
            _exchange_start(me, plan)

        core(*ins, *outs, *scratch)

        @pl.when(last)
        def _():
            _exchange_wait(me, plan)

    hbm = pl.BlockSpec(memory_space=pl.ANY)
    res = pl.pallas_call(
        body, grid=grid,
        in_specs=list(in_specs) + [hbm] * (2 * n), out_specs=list(out_specs) + [hbm] * n,
        out_shape=list(out_shape) + [SDS(l.shape, l.dtype) for l in landing],
        scratch_shapes=list(scratch_shapes) + [pltpu.SemaphoreType.DMA((N_DEV * n,)), pltpu.SemaphoreType.DMA((N_DEV * n,)),
                                               pltpu.SemaphoreType.DMA((n,))],
        input_output_aliases={n_in + n + k: n_out + k for k in range(n)},
        name=name, compiler_params=_params(*(("arbitrary",) * len(grid))))(*operands, *srcs, *landing)
    return res[:n_out], res[n_out:]


SLOT_ROWS = 8


def small_allreduce_adamw(loss_sum, grads, ws, ms, vs):
    n = len(grads)
    rows = [g.shape[0] for g in grads]
    total = SLOT_ROWS * (n + 1)

    def body(*refs):
        loss_ref, g_refs, w_refs = refs[0], refs[1:1 + n], refs[1 + n:1 + 2 * n]
        m_refs, v_refs = refs[1 + 2 * n:1 + 3 * n], refs[1 + 3 * n:1 + 4 * n]
        outs = refs[1 + 4 * n:2 + 8 * n]
        vec, gath, tot, send_sems, recv_sems = refs[2 + 8 * n:]
        x, y, c = _place()
        me = _flat(x, y, c)
        vec[...] = jnp.zeros_like(vec)
        vec[0:1, :] = loss_ref[...]
        for i in range(n):
            vec[SLOT_ROWS * (i + 1):SLOT_ROWS * (i + 1) + rows[i], :] = g_refs[i][...]
        gath[me] = vec[...]
        copies = []
        for k in range(1, N_DEV):
            peer = _peer(x, y, c, k)
            copies.append(pltpu.make_async_remote_copy(
                src_ref=vec, dst_ref=gath.at[me], send_sem=send_sems.at[k - 1], recv_sem=recv_sems.at[k - 1],
                device_id=peer, device_id_type=MESH))
        for cp in copies:
            cp.start()
        for cp in copies:
            cp.wait_recv()
        for cp in copies:
            cp.wait_send()
        g = gath[0]
        for j in range(1, N_DEV):
            g = g + gath[j]
        tot[...] = g
        outs[0][...] = tot[0:1, :]
        for i in range(n):
            gi = tot[SLOT_ROWS * (i + 1):SLOT_ROWS * (i + 1) + rows[i], :]
            outs[1 + i][...] = gi
            outs[1 + n + i][...], outs[1 + 2 * n + i][...], outs[1 + 3 * n + i][...] = _adamw(
                w_refs[i][...], gi, m_refs[i][...], v_refs[i][...])

    shapes = [SDS(g.shape, F32) for g in grads]
    return pl.pallas_call(
        body, out_shape=[SDS((1, LANES), F32)] + shapes * 4,
        scratch_shapes=[pltpu.VMEM((total, LANES), F32), pltpu.VMEM((N_DEV, total, LANES), F32), pltpu.VMEM((total, LANES), F32),
                        pltpu.SemaphoreType.DMA((7,)), pltpu.SemaphoreType.DMA((7,))],
        name="small_allreduce_adamw", compiler_params=_params())(loss_sum, *grads, *ws, *ms, *vs)


def _rope_lane_patterns():
    inv = lambda r: ROPE_THETA ** (-(jnp.arange(0, r, 2, dtype=F32) / r))
    z = lambda n: jnp.zeros((n,), F32)
    o = lambda n: jnp.ones((n,), F32)
    half, rest = A_ROT // 2, A_HEAD_DIM - A_ROT
    ia, im = inv(A_ROT), inv(MLA_ROPE)
    mh, tail = MLA_ROPE // 2, LANES - MLA_NOPE - MLA_ROPE
    rows = [jnp.tile(jnp.concatenate([ia, ia, z(rest)]), 2),
            jnp.tile(jnp.concatenate([o(half), z(half + rest)]), 2),
            jnp.tile(jnp.concatenate([z(half), o(half), z(rest)]), 2),
            jnp.concatenate([z(MLA_NOPE), im, im, z(tail)]),
            jnp.concatenate([z(MLA_NOPE), o(mh), z(mh + tail)]),
            jnp.concatenate([z(MLA_NOPE + mh), o(mh), z(tail)]),
            z(LANES), z(LANES)]
    return jnp.stack(rows)


def rope_tables(positions):
    pos = positions.astype(F32).reshape(-1, 1)
    t = pos.shape[0]
    tm = 512

    def body(pos_ref, pat_ref, *outs):
        p = pos_ref[...]
        for k in range(2):
            inv, first, second = pat_ref[3 * k:3 * k + 1, :], pat_ref[3 * k + 1:3 * k + 2, :], pat_ref[3 * k + 2:3 * k + 3, :]
            ang = p * inv
            sn = jnp.sin(ang)
            outs[3 * k][...] = jnp.where(first + second > 0.0, jnp.cos(ang), 1.0)
            outs[3 * k + 1][...] = -first * sn
            outs[3 * k + 2][...] = second * sn

    row = pl.BlockSpec((tm, LANES), lambda i: (i, 0))
    res = pl.pallas_call(
        body, grid=(t // tm,),
        in_specs=[pl.BlockSpec((tm, 1), lambda i: (i, 0)), pl.BlockSpec((8, LANES), lambda i: (0, 0))],
        out_specs=[row] * 6, out_shape=[SDS((t, LANES), F32)] * 6, name="rope_tables",
        compiler_params=_params("parallel"))(pos, _rope_lane_patterns())
    return tuple(res[:3]), tuple(res[3:])


KR_LO, KR_HI = 4480, 4512
W_IN_SHARD = D_IN // N_DEV
AG_SPLIT = 5 * W_IN_SHARD - 3 * A_WIDTH
BG_SPLIT = 6 * W_IN_SHARD - KR_HI


def _w_in_working_t(g):
    w = g.reshape(D_IN, D_MODEL)
    z = lambda n: jnp.zeros((n, D_MODEL), w.dtype)
    return jnp.concatenate([w[:KR_LO], z(MLA_NOPE), w[KR_LO:KR_HI], z(LANES - MLA_NOPE - MLA_ROPE), w[KR_HI:]], 0)


def _shards(rows):
    return rows.reshape(-1, W_IN_SHARD, D_MODEL).astype(BF16)


def _w_in_shards_0_4(d_aq, d_ak, d_av, d_ag):
    return _shards(jnp.concatenate([d_aq, d_ak, d_av, d_ag[:AG_SPLIT]], 0))


def _w_in_shard_5(d_ag, d_cc, d_bg):
    kr = MLA_Q_RANK + MLA_KV_RANK + MLA_NOPE
    return _shards(jnp.concatenate([d_ag[AG_SPLIT:], d_cc[:MLA_Q_RANK + MLA_KV_RANK], d_cc[kr:kr + MLA_ROPE], d_bg[:BG_SPLIT]], 0))


def _w_in_shards_6_7(d_bg, d_mq, d_mg):
    return _shards(jnp.concatenate([d_bg[BG_SPLIT:], d_mq, d_mg], 0))


def _w_uq_working(g):
    w = jnp.pad(g.transpose(1, 0, 2), ((0, 0), (0, 0), (0, LANES - MLA_NOPE - MLA_ROPE)))
    return w.reshape(MLA_Q_RANK, MLA_QW)


def _w_uq_parts(dw):
    return dw.reshape(MLA_Q_RANK, MLA_HEADS, LANES)[:, :, :MLA_NOPE + MLA_ROPE].transpose(1, 0, 2)


def _w_ukv_working(g):
    wk = jnp.pad(g[:, :, :MLA_NOPE].transpose(1, 0, 2), ((0, 0), (0, 0), (0, LANES - MLA_NOPE)))
    wv = g[:, :, MLA_NOPE:].transpose(1, 0, 2)
    return jnp.concatenate([wk.reshape(MLA_KV_RANK, MLA_QW), wv.reshape(MLA_KV_RANK, MLA_WIDTH)], 1)


def _w_ukv_parts(dw):
    dk = dw[:, :MLA_QW].reshape(MLA_KV_RANK, MLA_HEADS, LANES)[:, :, :MLA_NOPE]
    dv = dw[:, MLA_QW:].reshape(MLA_KV_RANK, MLA_HEADS, MLA_V)
    return jnp.concatenate([dk, dv], -1).transpose(1, 0, 2)


SMALL_NAMES = ("g_emb", "b_emb", "g_cq", "g_ckv", "g_out_a", "g_out_b", "g_out_m", "g_post", "b_post")


def kernel(x, mem, positions, g_emb, b_emb, w_in, g_cq, g_ckv, w_uq, w_ukv, w_mem_kv, g_out_a, g_out_b, g_out_m, w_out, g_post, b_post, loss_target, m_g_emb, m_b_emb, m_w_in, m_g_cq, m_g_ckv, m_w_uq, m_w_ukv, m_w_mem_kv, m_g_out_a, m_g_out_b, m_g_out_m, m_w_out, m_g_post, m_b_post, v_g_emb, v_b_emb, v_w_in, v_g_cq, v_g_ckv, v_w_uq, v_w_ukv, v_w_mem_kv, v_g_out_a, v_g_out_b, v_g_out_m, v_w_out, v_g_post, v_b_post):
    nb = x.shape[0]
    t = nb * SEQ
    x2 = x.reshape(t, D_MODEL)
    tgt2 = loss_target.reshape(t, D_MODEL)
    mem2 = mem.reshape(nb * N_MEM, D_MODEL)
    g_emb2, b_emb2 = g_emb.reshape(1, -1), b_emb.reshape(1, -1)
    (a_c, a_sa, a_sb), (m_c, m_sa, m_sb) = rope_tables(positions)

    w_in_t, m_w_in_t, v_w_in_t = w_in[0].T, m_w_in[0].T, v_w_in[0].T
    s_in, s_uq, s_ukv, s_mem, s_out = cast_shards((w_in_t, w_uq[0], w_ukv[0], w_mem_kv[0], w_out[0]))
    (g_in,) = allgather_weights((s_in,))
    win_t = _w_in_working_t(g_in)

    h32, h16 = ln_emb_fwd(x2, g_emb2, b_emb2)
    proj = mm_nn(h16, win_t, F32, 512, 1536, "proj", rhs_transposed=True)
    later = (s_uq, s_ukv, s_mem, s_out)
    (ya, lse_a), qkv_d, (g_uq, g_ukv, g_mem, g_out) = a_attn_fwd(
        proj, a_c, a_sa, a_sb, nb,
        (later, (ALL_DEVICES,) * len(later), tuple(lax.empty((N_DEV,) + w.shape, BF16) for w in later)))
    wuq_w = _w_uq_working(g_uq)
    wkv_w = _w_ukv_working(g_ukv)
    wmem = g_mem.reshape(D_MODEL, 2 * MEM_WIDTH)
    wout = g_out.reshape(D_MIX, D_MODEL)
    qb, kb, vb = mla_prep_fwd(proj, m_c, m_sa, m_sb, g_cq, g_ckv, wuq_w, wkv_w)
    yb, lse_b = mla_attn_fwd(qb, kb, vb, nb)
    mkv = mm_nn(mem2, wmem, BF16, nb * N_MEM, 512, "mem_kv")
    ym = mem_attn_fwd(proj, mkv, nb)
    z, du32, du16, loss_sum, dg_post, db_post = gate_out_ln_loss(
        ya, yb, ym, proj, g_out_a, g_out_b, g_out_m, wout, h32, tgt2, g_post, b_post)

    dya, dyb, dym, dag, dbg, dmg, dg_out_a, dg_out_b, dg_out_m = gate_bwd(
        du16, wout, ya, yb, ym, proj, g_out_a, g_out_b, g_out_m)
    dw_out = mm_tn(z, du16, 1024, "dw_out")
    dmq, dmk, dmv = mem_attn_bwd(proj, mkv, dym, nb)
    dw_mem = mm_tn(mem2, jnp.concatenate([dmk, dmv], 1), nb * N_MEM, "dw_mem")
    d_ag, d_bg, d_mq, d_mg = [mm_tn(p, h16, 2048, "dw_in_" + n) for n, p in (("ag", dag), ("bg", dbg), ("mq", dmq), ("mg", dmg))]
    landing = lambda w, dtype=F32: lax.empty((N_DEV,) + w.shape, dtype)
    big_w = (w_in_t, w_uq[0], w_ukv[0], w_mem_kv[0], w_out[0])
    (daq, dak, dav), (p_out, p_mem, p_in) = a_attn_bwd(
        qkv_d, a_c, a_sa, a_sb, dya, ya, lse_a, nb,
        ((dw_out.reshape(N_DEV, D_MIX // N_DEV, D_MODEL), dw_mem.reshape(N_DEV, D_MODEL // N_DEV, 2 * MEM_WIDTH),
          _w_in_shards_6_7(d_bg, d_mq, d_mg)),
         (ALL_DEVICES, ALL_DEVICES, (6, 7)),
         (landing(w_out[0]), landing(w_mem_kv[0]), landing(w_in_t, BF16))))
    d_aq, d_ak, d_av = [mm_tn(p, h16, 2048, "dw_in_" + n) for n, p in (("aq", daq), ("ak", dak), ("av", dav))]
    (dqb, dkb, dvb), (p_in,) = mla_attn_bwd(
        qb, kb, vb, dyb, yb, lse_b, nb, ((_w_in_shards_0_4(d_aq, d_ak, d_av, d_ag),), ((0, 1, 2, 3, 4),), (p_in,)))
    dcc, dqf, cqn, dkvf, ckvn, dg_cq, dg_ckv = mla_prep_bwd(proj, m_c, m_sa, m_sb, g_cq, g_ckv, wuq_w, wkv_w, dqb, dkb, dvb)
    dw_uq = mm_tn(cqn, dqf, 2048, "dw_uq")
    dw_ukv = mm_tn(ckvn, dkvf, 2048, "dw_ukv")
    d_cc = mm_tn(dcc, h16, 2048, "dw_in_cc")
    pieces = (daq, dak, dav, dag, dcc, dbg, dmq, dmg)
    (grad_x, dg_emb, db_emb), (p_in, p_uq, p_ukv) = dh_ln_bwd(
        pieces, win_t, du32, x2, g_emb2,
        ((_w_in_shard_5(d_ag, d_cc, d_bg), _w_uq_parts(dw_uq), _w_ukv_parts(dw_ukv)),
         ((5,), ALL_DEVICES, ALL_DEVICES),
         (p_in, landing(w_uq[0]), landing(w_ukv[0]))))

    parts = (p_in, p_uq, p_ukv, p_mem, p_out)
    big_m = (m_w_in_t, m_w_uq[0], m_w_ukv[0], m_w_mem_kv[0], m_w_out[0])
    big_v = (v_w_in_t, v_w_uq[0], v_w_ukv[0], v_w_mem_kv[0], v_w_out[0])
    big = {}
    for name, w, p, m, v in zip(("w_in", "w_uq", "w_ukv", "w_mem_kv", "w_out"), big_w, parts, big_m, big_v):
        res = adamw_shard(w, p, m, v, "adamw_" + name)
        big[name] = [(o.T if name == "w_in" else o)[None] for o in res]

    small_w = (g_emb, b_emb, g_cq, g_ckv, g_out_a, g_out_b, g_out_m, g_post, b_post)
    small_m = (m_g_emb, m_b_emb, m_g_cq, m_g_ckv, m_g_out_a, m_g_out_b, m_g_out_m, m_g_post, m_b_post)
    small_v = (v_g_emb, v_b_emb, v_g_cq, v_g_ckv, v_g_out_a, v_g_out_b, v_g_out_m, v_g_post, v_b_post)
    small_g = (dg_emb, db_emb, dg_cq, dg_ckv, dg_out_a, dg_out_b, dg_out_m, dg_post, db_post)
    rows128 = lambda vals: [v.reshape(-1, LANES) for v in vals]
    res = small_allreduce_adamw(loss_sum, rows128(small_g), rows128(small_w), rows128(small_m), rows128(small_v))
    loss = res[0][0, 0]
    n_small = len(small_w)
    sg, sd, sm, sv = [[r.reshape(w.shape) for r, w in zip(res[1 + k * n_small:1 + (k + 1) * n_small], small_w)]
                      for k in range(4)]

    order = ("g_emb", "b_emb", "w_in", "g_cq", "g_ckv", "w_uq", "w_ukv", "w_mem_kv", "g_out_a", "g_out_b", "g_out_m",
             "w_out", "g_post", "b_post")
    small_idx = {n: i for i, n in enumerate(SMALL_NAMES)}
    outs = [loss, grad_x.reshape(x.shape)]
    for kind in range(4):
        for name in order:
            outs.append(big[name][kind] if name in big else (sg, sd, sm, sv)[kind][small_idx[name]])
    return tuple(outs)
```

```python
import functools

import jax
import jax.numpy as jnp
from jax import lax
from jax.experimental import pallas as pl
from jax.experimental.pallas import tpu as pltpu

F32 = jnp.float32
BF16 = jnp.bfloat16
SDS = jax.ShapeDtypeStruct
MESH = pl.DeviceIdType.MESH

D_MODEL = 1024
SEQ = 2048
A_HEADS, A_HEAD_DIM, A_ROT = 16, 64, 16
A_WIDTH = 1024
DILATIONS = (1, 4, 16)
N_SIDE = 64
MLA_HEADS, MLA_Q_RANK, MLA_KV_RANK = 8, 256, 128
MLA_NOPE, MLA_ROPE, MLA_V = 64, 32, 64
MLA_WIDTH = 512
N_MEM, MEM_HEADS, MEM_HEAD_DIM, MEM_WIDTH = 256, 4, 128, 512
ROPE_THETA = 500000.0
NORM_EPS = 1e-5
NEG_INF = -1e30
ALPHA = 2.0 ** 0.25
D_IN = 6048
N_DEV = 8

ADAM_LR, ADAM_B1, ADAM_B2, ADAM_EPS, ADAM_WD, ADAM_STEP = 0.001, 0.9, 0.999, 1e-08, 0.01, 10

D_INW = 6144
PIECE_WIDTHS = (1024, 1024, 1024, 1024, 512, 512, 512, 512)
PIECE_OFFS = (0, 1024, 2048, 3072, 4096, 4608, 5120, 5632)
LANES = 128
VMEM_LIMIT = 56 * 1024 * 1024


def _params(*sem):
    kw = dict(vmem_limit_bytes=VMEM_LIMIT)
    if sem:
        kw["dimension_semantics"] = sem
    return pltpu.CompilerParams(**kw)


def _dot(a, b):
    return jnp.dot(a, b, preferred_element_type=F32)


def _dot_nt(a, b):
    return lax.dot_general(a, b, (((1,), (1,)), ((), ())), preferred_element_type=F32)


def _dot_tn(a, b):
    return lax.dot_general(a, b, (((0,), (0,)), ((), ())), preferred_element_type=F32)


def _sigmoid(x):
    return 1.0 / (1.0 + jnp.exp(-x))


def _rope_fwd(x, c, sa, sb, half):
    n = x.shape[-1]
    return x * c + pltpu.roll(x, n - half, 1) * sa + pltpu.roll(x, half, 1) * sb


def _rope_bwd(dy, c, sa, sb, half):
    n = dy.shape[-1]
    return dy * c + pltpu.roll(dy * sa, half, 1) + pltpu.roll(dy * sb, n - half, 1)


def mm_nn(a, b, out_dtype, tm, tn, name, rhs_transposed=False):
    m, k = a.shape
    n = b.shape[0] if rhs_transposed else b.shape[1]
    dot = _dot_nt if rhs_transposed else _dot

    def body(a_ref, b_ref, o_ref):
        o_ref[...] = dot(a_ref[...].astype(BF16), b_ref[...].astype(BF16)).astype(o_ref.dtype)

    b_spec = pl.BlockSpec((tn, k), lambda j, i: (j, 0)) if rhs_transposed else pl.BlockSpec((k, tn), lambda j, i: (0, j))
    return pl.pallas_call(
        body, grid=(n // tn, m // tm),
        in_specs=[pl.BlockSpec((tm, k), lambda j, i: (i, 0)), b_spec],
        out_specs=pl.BlockSpec((tm, tn), lambda j, i: (i, j)),
        out_shape=SDS((m, n), out_dtype), name=name,
        compiler_params=_params("parallel", "parallel"))(a, b)


def mm_tn(a, b, tt, name):
    t, m = a.shape
    n = b.shape[1]

    def body(a_ref, b_ref, o_ref):
        @pl.when(pl.program_id(0) == 0)
        def _():
            o_ref[...] = jnp.zeros_like(o_ref)

        o_ref[...] += _dot_tn(a_ref[...].astype(BF16), b_ref[...].astype(BF16))

    return pl.pallas_call(
        body, grid=(t // tt,),
        in_specs=[pl.BlockSpec((tt, m), lambda i: (i, 0)), pl.BlockSpec((tt, n), lambda i: (i, 0))],
        out_specs=pl.BlockSpec((m, n), lambda i: (0, 0)),
        out_shape=SDS((m, n), F32), name=name,
        compiler_params=_params("arbitrary"))(a, b)


def ln_emb_fwd(x2, g, b):
    t, d = x2.shape
    tm = 512

    def body(x_ref, g_ref, b_ref, h32_ref, h16_ref):
        x = x_ref[...]
        mu = jnp.mean(x, axis=-1, keepdims=True)
        xc = x - mu
        var = jnp.mean(xc * xc, axis=-1, keepdims=True)
        h = xc * lax.rsqrt(var + NORM_EPS) * g_ref[...] + b_ref[...]
        h32_ref[...] = h
        h16_ref[...] = h.astype(BF16)

    row = pl.BlockSpec((tm, d), lambda i: (i, 0))
    vec = pl.BlockSpec((1, d), lambda i: (0, 0))
    return pl.pallas_call(
        body, grid=(t // tm,), in_specs=[row, vec, vec], out_specs=[row, row],
        out_shape=[SDS((t, d), F32), SDS((t, d), BF16)], name="ln_emb_fwd",
        compiler_params=_params("parallel"))(x2, g, b)


Q_BLK = 128
UNROLL_FWD = 8
UNROLL_BWD = 8


def _pattern_geometry(d):
    length = SEQ // d
    nblk = length // Q_BLK
    kwin = min(2 * Q_BLK, length)
    return length, nblk, kwin


def _block_coords(idx, d):
    length, nblk, kwin = _pattern_geometry(d)
    r = lax.shift_right_logical(idx, nblk.bit_length() - 1)
    i = idx & (nblk - 1)
    q0 = pl.multiple_of(r * length + i * Q_BLK, Q_BLK)
    ks = jnp.clip(i * Q_BLK - N_SIDE, 0, length - kwin)
    k0 = pl.multiple_of(r * length + ks, N_SIDE)
    qpos = i * Q_BLK + lax.broadcasted_iota(jnp.int32, (Q_BLK, kwin), 0)
    kpos = ks + lax.broadcasted_iota(jnp.int32, (Q_BLK, kwin), 1)
    valid = jnp.abs(kpos - qpos) <= N_SIDE
    return q0, k0, kwin, valid


def _deinterleave(src_ref, dst_ref, d, dtype, tmp_ref):
    if d == 1:
        dst_ref[...] = src_ref[...].astype(dtype)
        return
    q = SEQ // 4
    if d == 4:
        for r in range(4):
            dst_ref[r * q:(r + 1) * q, :] = src_ref[pl.ds(r, q, stride=4), :].astype(dtype)
        return
    assert d == 16
    n = SEQ // 16
    for r in range(4):
        tmp_ref[r * q:(r + 1) * q, :] = src_ref[pl.ds(r, q, stride=4), :]
    for r in range(4):
        for j in range(4):
            dst_ref[(r + 4 * j) * n:(r + 4 * j + 1) * n, :] = tmp_ref[pl.ds(r * q + j, n, stride=4), :].astype(dtype)


def _class16_to_class4(src_ref, dst_ref):
    q, n = SEQ // 4, SEQ // 16
    for r in range(4):
        for j in range(4):
            dst_ref[pl.ds(r * q + j, n, stride=4), :] = src_ref[(r + 4 * j) * n:(r + 4 * j + 1) * n, :]


def _interleave(src_ref, dst_ref, d, tmp_ref, accumulate):
    q = SEQ // 4
    if d == 16:
        _class16_to_class4(src_ref, tmp_ref)
        src_ref = tmp_ref
    else:
        assert d == 4
    for r in range(4):
        rows = pl.ds(r, q, stride=4)
        val = src_ref[r * q:(r + 1) * q, :]
        dst_ref[rows, :] = dst_ref[rows, :] + val if accumulate else val


def a_attn_fwd(proj, ca, sa, sb, nb, xch):
    t = proj.shape[0]
    n_pairs = A_WIDTH // LANES

    def body(q_ref, k_ref, v_ref, c_ref, sa_ref, sb_ref, y_ref, lse_ref, *rest):
        qkv_d, (qr_s, kr_s, oc_s, lc_s, o1_s, l1_s, o2_s, l2_s, o3_s, l3_s, tmp_s) = rest[:9], rest[9:]
        c, s_a, s_b = c_ref[...], sa_ref[...], sb_ref[...]
        qr_s[...] = _rope_fwd(q_ref[...], c, s_a, s_b, A_ROT // 2) * (A_HEAD_DIM ** -0.5)
        kr_s[...] = _rope_fwd(k_ref[...], c, s_a, s_b, A_ROT // 2)
        head0 = lax.broadcasted_iota(jnp.int32, (Q_BLK, LANES), 1) < A_HEAD_DIM
        nat = ((o1_s, l1_s), (o2_s, l2_s), (o3_s, l3_s))

        for g, d in enumerate(DILATIONS):
            qd_s, kd_s, vd_s = qkv_d[3 * g:3 * g + 3]
            _deinterleave(qr_s, qd_s, d, BF16, tmp_s)
            _deinterleave(kr_s, kd_s, d, BF16, tmp_s)
            _deinterleave(v_ref, vd_s, d, BF16, tmp_s)
            o_dst, l_dst = (nat[g] if d == 1 else (oc_s, lc_s))

            def block(idx, carry, d=d, o_dst=o_dst, l_dst=l_dst, qd_s=qd_s, kd_s=kd_s, vd_s=vd_s):
                q0, k0, kwin, valid = _block_coords(idx, d)
                qb = qd_s[pl.ds(q0, Q_BLK), :]
                kb = kd_s[pl.ds(k0, kwin), :]
                vb = vd_s[pl.ds(k0, kwin), :]
                zero = jnp.zeros_like(qb)
                q2 = jnp.concatenate([jnp.where(head0, qb, zero), jnp.where(head0, zero, qb)], 0)
                s = jnp.where(jnp.concatenate([valid, valid], 0), _dot_nt(q2, kb), NEG_INF)
                m = jnp.max(s, axis=-1, keepdims=True)
                p = jnp.exp(s - m)
                l = jnp.sum(p, axis=-1, keepdims=True)
                o2 = _dot(p.astype(BF16), vb) / l
                l2 = m + jnp.log(l)
                o_dst[pl.ds(q0, Q_BLK), :] = jnp.where(head0, o2[:Q_BLK], o2[Q_BLK:])
                l_dst[pl.ds(q0, Q_BLK), :] = jnp.where(head0, l2[:Q_BLK], l2[Q_BLK:])
                return carry

            lax.fori_loop(0, SEQ // Q_BLK, block, 0, unroll=UNROLL_FWD)
            if d > 1:
                _interleave(oc_s, nat[g][0], d, tmp_s, False)
                _interleave(lc_s, nat[g][1], d, tmp_s, False)

        def merge(ci, carry):
            rows = pl.ds(pl.multiple_of(ci * 256, 256), 256)
            l1, l2, l3 = l1_s[rows, :], l2_s[rows, :], l3_s[rows, :]
            m = jnp.maximum(jnp.maximum(l1, l2), l3)
            w1, w2, w3 = jnp.exp(l1 - m), jnp.exp(l2 - m), jnp.exp(l3 - m)
            w = w1 + w2 + w3
            y_ref[rows, :] = (w1 * o1_s[rows, :] + w2 * o2_s[rows, :] + w3 * o3_s[rows, :]) / w
            lse_ref[rows, :] = m + jnp.log(w)
            return carry

        lax.fori_loop(0, SEQ // 256, merge, 0)

    def col(off):
        return pl.BlockSpec((SEQ, LANES), lambda b, hp: (b, off + hp))

    tab = pl.BlockSpec((SEQ, LANES), lambda b, hp: (b, 0))
    out = pl.BlockSpec((SEQ, LANES), lambda b, hp: (b, hp))
    f32s = pltpu.VMEM((SEQ, LANES), F32)
    res, landed = call_hosting_exchange(
        body, xch, grid=(nb, n_pairs),
        in_specs=[col(0), col(n_pairs), col(2 * n_pairs), tab, tab, tab],
        out_specs=[out] * 11,
        out_shape=[SDS((t, A_WIDTH), F32)] * 2 + [SDS((t, A_WIDTH), BF16)] * 9,
        scratch_shapes=[f32s] * 11,
        name="a_attn_fwd", operands=(proj, proj, proj, ca, sa, sb))
    return res[:2], res[2:], landed


def a_attn_bwd(qkv_d, ca, sa, sb, dy, y, lse, nb, xch):
    t = dy.shape[0]
    n_pairs = A_WIDTH // LANES

    def body(*refs):
        qkv_refs = refs[:9]
        (c_ref, sa_ref, sb_ref, do_ref, y_ref, lse_ref, dq_ref, dk_ref, dv_ref,
         l0n_s, l1n_s, d0n_s, d1n_s, dod_s, l0d_s, l1d_s, d0d_s, d1d_s,
         dqc_s, dkc_s, dvc_s, dq4_s, dk4_s, dv4_s, dqn_s, dkn_s, dvn_s, tmp_s) = refs[9:]
        c, s_a, s_b = c_ref[...], sa_ref[...], sb_ref[...]
        head0 = lax.broadcasted_iota(jnp.int32, (Q_BLK, LANES), 1) < A_HEAD_DIM

        def per_head_rows(ci, carry):
            rows = pl.ds(pl.multiple_of(ci * 256, 256), 256)
            h0 = lax.broadcasted_iota(jnp.int32, (256, LANES), 1) < A_HEAD_DIM
            tt = do_ref[rows, :] * y_ref[rows, :]
            d0n_s[rows, :] = jnp.broadcast_to(jnp.sum(jnp.where(h0, tt, 0.0), axis=-1, keepdims=True), (256, LANES))
            d1n_s[rows, :] = jnp.broadcast_to(jnp.sum(jnp.where(h0, 0.0, tt), axis=-1, keepdims=True), (256, LANES))
            l = lse_ref[rows, :]
            lr = pltpu.roll(l, A_HEAD_DIM, 1)
            l0n_s[rows, :] = jnp.where(h0, l, lr)
            l1n_s[rows, :] = jnp.where(h0, lr, l)
            return carry

        lax.fori_loop(0, SEQ // 256, per_head_rows, 0)
        assert DILATIONS == (1, 4, 16)

        for g, d in enumerate(DILATIONS):
            qd_s, kd_s, vd_s = qkv_refs[3 * g:3 * g + 3]
            _deinterleave(do_ref, dod_s, d, BF16, tmp_s)
            if d > 1:
                for src, dst in ((l0n_s, l0d_s), (l1n_s, l1d_s), (d0n_s, d0d_s), (d1n_s, d1d_s)):
                    _deinterleave(src, dst, d, F32, tmp_s)
            l0, l1, d0, d1 = (l0n_s, l1n_s, d0n_s, d1n_s) if d == 1 else (l0d_s, l1d_s, d0d_s, d1d_s)
            dq_dst, dk_dst, dv_dst = {1: (dqn_s, dkn_s, dvn_s), 4: (dq4_s, dk4_s, dv4_s), 16: (dqc_s, dkc_s, dvc_s)}[d]
            dk_dst[...] = jnp.zeros_like(dk_dst)
            dv_dst[...] = jnp.zeros_like(dv_dst)

            def block(idx, carry, d=d, l0=l0, l1=l1, d0=d0, d1=d1, dq_dst=dq_dst, dk_dst=dk_dst, dv_dst=dv_dst,
                      qd_s=qd_s, kd_s=kd_s, vd_s=vd_s):
                q0, k0, kwin, valid = _block_coords(idx, d)
                qrows = pl.ds(q0, Q_BLK)
                krows = pl.ds(k0, kwin)
                qb, dob = qd_s[qrows, :], dod_s[qrows, :]
                kb, vb = kd_s[krows, :], vd_s[krows, :]
                zero = jnp.zeros_like(qb)
                q2 = jnp.concatenate([jnp.where(head0, qb, zero), jnp.where(head0, zero, qb)], 0)
                do2 = jnp.concatenate([jnp.where(head0, dob, zero), jnp.where(head0, zero, dob)], 0)
                wide = lambda x: jnp.concatenate([x] * (kwin // LANES), 1)
                lse2 = wide(jnp.concatenate([l0[qrows, :], l1[qrows, :]], 0))
                dd2 = wide(jnp.concatenate([d0[qrows, :], d1[qrows, :]], 0))
                s = jnp.where(jnp.concatenate([valid, valid], 0), _dot_nt(q2, kb), NEG_INF)
                p = jnp.exp(s - lse2)
                ds = (p * (_dot_nt(do2, vb) - dd2)).astype(BF16)
                dq2 = _dot(ds, kb)
                dq_dst[qrows, :] = jnp.where(head0, dq2[:Q_BLK], dq2[Q_BLK:])
                dk_dst[krows, :] += _dot_tn(ds, q2)
                dv_dst[krows, :] += _dot_tn(p.astype(BF16), do2)
                return carry

            lax.fori_loop(0, SEQ // Q_BLK, block, 0, unroll=UNROLL_BWD)

        for c16, c4, nat in ((dqc_s, dq4_s, dqn_s), (dkc_s, dk4_s, dkn_s), (dvc_s, dv4_s, dvn_s)):
            _class16_to_class4(c16, tmp_s)
            c4[...] = c4[...] + tmp_s[...]
            _interleave(c4, nat, 4, tmp_s, True)

        dq_ref[...] = _rope_bwd(dqn_s[...] * (A_HEAD_DIM ** -0.5), c, s_a, s_b, A_ROT // 2).astype(BF16)
        dk_ref[...] = _rope_bwd(dkn_s[...], c, s_a, s_b, A_ROT // 2).astype(BF16)
        dv_ref[...] = dvn_s[...].astype(BF16)

    tab = pl.BlockSpec((SEQ, LANES), lambda b, hp: (b, 0))
    blk = pl.BlockSpec((SEQ, LANES), lambda b, hp: (b, hp))
    f32s = pltpu.VMEM((SEQ, LANES), F32)
    b16s = pltpu.VMEM((SEQ, LANES), BF16)
    return call_hosting_exchange(
        body, xch, grid=(nb, n_pairs),
        in_specs=[blk] * 9 + [tab, tab, tab, blk, blk, blk],
        out_specs=[blk, blk, blk],
        out_shape=[SDS((t, A_WIDTH), BF16)] * 3,
        scratch_shapes=[f32s] * 4 + [b16s] + [f32s] * 14,
        name="a_attn_bwd", operands=(*qkv_d, ca, sa, sb, dy, y, lse))


MLA_SCALE = (MLA_NOPE + MLA_ROPE) ** -0.5
LOG2E = 1.4426950408889634
MLA_QW = MLA_HEADS * LANES
MLA_KVW = MLA_QW + MLA_WIDTH


def _rms(x, g):
    r = lax.rsqrt(jnp.mean(x * x, axis=-1, keepdims=True) + NORM_EPS)
    return x * r * g, r


def _rms_bwd(dn, x, r, g):
    tg = dn * g
    dx = r * tg - x * (r * r * r) * jnp.mean(tg * x, axis=-1, keepdims=True)
    return dx, jnp.sum(dn * x * r, axis=0, keepdims=True)


def mla_prep_fwd(proj, cm, sma, smb, g_cq, g_ckv, wuq, wkv):
    t = proj.shape[0]
    tm = 512

    def body(cq_ref, ckv_ref, kr_ref, c_ref, sa_ref, sb_ref, gq_ref, gkv_ref, wuq_ref, wkv_ref, q_ref, k_ref, v_ref):
        c, s_a, s_b = c_ref[...], sa_ref[...], sb_ref[...]
        cqn, _ = _rms(cq_ref[...], gq_ref[...])
        qf = _dot(cqn.astype(BF16), wuq_ref[...])
        ckvn, _ = _rms(ckv_ref[...], gkv_ref[...])
        kvf = _dot(ckvn.astype(BF16), wkv_ref[...])
        krope = _rope_fwd(kr_ref[...], c, s_a, s_b, MLA_ROPE // 2)
        for h in range(MLA_HEADS):
            cols = slice(h * LANES, (h + 1) * LANES)
            q_ref[:, cols] = (_rope_fwd(qf[:, cols], c, s_a, s_b, MLA_ROPE // 2) * (MLA_SCALE * LOG2E)).astype(BF16)
            k_ref[:, cols] = (kvf[:, cols] + krope).astype(BF16)
        v_ref[...] = kvf[:, MLA_QW:].astype(BF16)

    def row(w, j):
        return pl.BlockSpec((tm, w), lambda i: (i, j))

    def full(a):
        return pl.BlockSpec(a.shape, lambda i: (0, 0))

    return pl.pallas_call(
        body, grid=(t // tm,),
        in_specs=[row(256, 4096 // 256), row(128, 4352 // 128), row(128, 4480 // 128), row(128, 0), row(128, 0), row(128, 0),
                  full(g_cq), full(g_ckv), full(wuq), full(wkv)],
        out_specs=[row(MLA_QW, 0), row(MLA_QW, 0), row(MLA_WIDTH, 0)],
        out_shape=[SDS((t, MLA_QW), BF16), SDS((t, MLA_QW), BF16), SDS((t, MLA_WIDTH), BF16)],
        name="mla_prep_fwd", compiler_params=_params("parallel"))(proj, proj, proj, cm, sma, smb, g_cq, g_ckv, wuq, wkv)


def mla_prep_bwd(proj, cm, sma, smb, g_cq, g_ckv, wuq, wkv, dq, dk, dv):
    t = proj.shape[0]
    tm = 512

    def body(cq_ref, ckv_ref, c_ref, sa_ref, sb_ref, gq_ref, gkv_ref, wuq_ref, wkv_ref, dq_ref, dk_ref, dv_ref,
             dcc_ref, dqf_ref, cqn_ref, dkvf_ref, ckvn_ref, dgq_ref, dgkv_ref):
        @pl.when(pl.program_id(0) == 0)
        def _():
            dgq_ref[...] = jnp.zeros_like(dgq_ref)
            dgkv_ref[...] = jnp.zeros_like(dgkv_ref)

        c, s_a, s_b = c_ref[...], sa_ref[...], sb_ref[...]
        cq, ckv = cq_ref[...], ckv_ref[...]
        cqn, rq = _rms(cq, gq_ref[...])
        ckvn, rkv = _rms(ckv, gkv_ref[...])
        cqn_ref[...] = cqn.astype(BF16)
        ckvn_ref[...] = ckvn.astype(BF16)
        lane = lax.broadcasted_iota(jnp.int32, (tm, LANES), 1)
        rope_lanes = (lane >= MLA_NOPE) & (lane < MLA_NOPE + MLA_ROPE)
        dkrope = jnp.zeros((tm, LANES), F32)
        for h in range(MLA_HEADS):
            cols = slice(h * LANES, (h + 1) * LANES)
            dqf_ref[:, cols] = _rope_bwd(dq_ref[:, cols] * MLA_SCALE, c, s_a, s_b, MLA_ROPE // 2).astype(BF16)
            dkh = dk_ref[:, cols] * (1.0 / LOG2E)
            dkvf_ref[:, cols] = dkh.astype(BF16)
            dkrope = dkrope + dkh
        dkvf_ref[:, MLA_QW:] = dv_ref[...].astype(BF16)
        dkr = _rope_bwd(jnp.where(rope_lanes, dkrope, 0.0), c, s_a, s_b, MLA_ROPE // 2)
        dcqn = _dot_nt(dqf_ref[...], wuq_ref[...])
        dckvn = _dot_nt(dkvf_ref[...], wkv_ref[...])
        dcq, dgq = _rms_bwd(dcqn, cq, rq, gq_ref[...])
        dckv, dgkv = _rms_bwd(dckvn, ckv, rkv, gkv_ref[...])
        dgq_ref[...] += dgq
        dgkv_ref[...] += dgkv
        dcc_ref[:, 0:256] = dcq.astype(BF16)
        dcc_ref[:, 256:384] = dckv.astype(BF16)
        dcc_ref[:, 384:512] = dkr.astype(BF16)

    def row(w, j):
        return pl.BlockSpec((tm, w), lambda i: (i, j))

    def full(a):
        return pl.BlockSpec(a.shape, lambda i: (0, 0))

    return pl.pallas_call(
        body, grid=(t // tm,),
        in_specs=[row(256, 4096 // 256), row(128, 4352 // 128), row(128, 0), row(128, 0), row(128, 0),
                  full(g_cq), full(g_ckv), full(wuq), full(wkv), row(MLA_QW, 0), row(MLA_QW, 0), row(MLA_WIDTH, 0)],
        out_specs=[row(512, 0), row(MLA_QW, 0), row(256, 0), row(MLA_KVW, 0), row(128, 0), full(g_cq), full(g_ckv)],
        out_shape=[SDS((t, 512), BF16), SDS((t, MLA_QW), BF16), SDS((t, 256), BF16), SDS((t, MLA_KVW), BF16),
                   SDS((t, 128), BF16), SDS(g_cq.shape, F32), SDS(g_ckv.shape, F32)],
        name="mla_prep_bwd", compiler_params=_params("arbitrary"))(proj, proj, cm, sma, smb, g_cq, g_ckv, wuq, wkv, dq, dk, dv)


MLA_TQ = 256


def mla_attn_fwd(qb, kb, vb, nb):
    t = qb.shape[0]
    nq = SEQ // MLA_TQ
    n_pairs = MLA_HEADS // 2

    def body(q_ref, k_ref, v_ref, y_ref, lse_ref):
        head0 = lax.broadcasted_iota(jnp.int32, (MLA_TQ, LANES), 1) < MLA_V
        v = v_ref[...]
        vhead0 = lax.broadcasted_iota(jnp.int32, v.shape, 1) < MLA_V
        one = jnp.ones_like(v)
        outs, lses = [], []
        for h in range(2):
            cols = slice(h * LANES, (h + 1) * LANES)
            s = _dot_nt(q_ref[:, cols], k_ref[:, cols])
            m = jnp.max(s, axis=-1, keepdims=True)
            p = jnp.exp2(s - m).astype(BF16)
            ol = _dot(p, jnp.where(vhead0 == (h == 0), v, one))
            l = pltpu.roll(ol, MLA_V, 1)
            outs.append(ol / l)
            lses.append(m + jnp.log2(l))
        y_ref[...] = jnp.where(head0, outs[0], outs[1])
        lse_ref[...] = jnp.where(head0, lses[0], lses[1])

    return pl.pallas_call(
        body, grid=(nb, n_pairs, nq),
        in_specs=[pl.BlockSpec((MLA_TQ, 2 * LANES), lambda b, hp, i: (b * nq + i, hp)),
                  pl.BlockSpec((SEQ, 2 * LANES), lambda b, hp, i: (b, hp)),
                  pl.BlockSpec((SEQ, LANES), lambda b, hp, i: (b, hp))],
        out_specs=[pl.BlockSpec((MLA_TQ, LANES), lambda b, hp, i: (b * nq + i, hp))] * 2,
        out_shape=[SDS((t, MLA_WIDTH), F32)] * 2,
        name="mla_attn_fwd", compiler_params=_params("parallel", "parallel", "parallel"))(qb, kb, vb)


def mla_attn_bwd(qb, kb, vb, dy, y, lse, nb, xch):
    t = qb.shape[0]
    nq = SEQ // MLA_TQ
    n_pairs = MLA_HEADS // 2

    def body(q_ref, k_ref, v_ref, do_ref, y_ref, lse_ref, dq_ref, dk_ref, dv_ref):
        @pl.when(pl.program_id(2) == 0)
        def _():
            dk_ref[...] = jnp.zeros_like(dk_ref)
            dv_ref[...] = jnp.zeros_like(dv_ref)

        head0 = lax.broadcasted_iota(jnp.int32, (MLA_TQ, LANES), 1) < MLA_V
        v = v_ref[...]
        do = do_ref[...]
        lse = lse_ref[...]
        tt = do * y_ref[...]
        dv = jnp.zeros((SEQ, LANES), F32)
        for h in range(2):
            sel = head0 if h == 0 else ~head0
            lo = h * MLA_V
            cols = slice(h * LANES, (h + 1) * LANES)
            q = q_ref[:, cols]
            k = k_ref[:, cols]
            dd = jnp.sum(jnp.where(sel, tt, 0.0), axis=-1, keepdims=True)
            doh = jnp.where(sel, do, 0.0).astype(BF16)
            p = jnp.exp2(_dot_nt(q, k) - lse[:, lo:lo + 1])
            dp = _dot_nt(doh, v)
            ds = (p * (dp - dd)).astype(BF16)
            dq_ref[:, cols] = _dot(ds, k)
            dk_ref[:, cols] += _dot_tn(ds, q)
            dv = dv + _dot_tn(p.astype(BF16), doh)
        dv_ref[...] += dv

    qspec = pl.BlockSpec((MLA_TQ, 2 * LANES), lambda b, hp, i: (b * nq + i, hp))
    kspec = pl.BlockSpec((SEQ, 2 * LANES), lambda b, hp, i: (b, hp))
    vspec = pl.BlockSpec((SEQ, LANES), lambda b, hp, i: (b, hp))
    ospec = pl.BlockSpec((MLA_TQ, LANES), lambda b, hp, i: (b * nq + i, hp))
    return call_hosting_exchange(
        body, xch, grid=(nb, n_pairs, nq),
        in_specs=[qspec, kspec, vspec, ospec, ospec, ospec],
        out_specs=[qspec, kspec, vspec],
        out_shape=[SDS((t, MLA_QW), F32), SDS((t, MLA_QW), F32), SDS((t, MLA_WIDTH), F32)],
        scratch_shapes=[], name="mla_attn_bwd", operands=(qb, kb, vb, dy, y, lse))


MEM_TQ = 512
MEM_SCALE = MEM_HEAD_DIM ** -0.5
MQ_BLK4 = 5120 // MEM_WIDTH


def mem_attn_fwd(proj, mkv, nb):
    t = proj.shape[0]
    nq = SEQ // MEM_TQ

    def body(q_ref, mk_ref, mv_ref, y_ref):
        for h in range(MEM_HEADS):
            cols = slice(h * LANES, (h + 1) * LANES)
            s = _dot_nt(q_ref[:, cols].astype(BF16), mk_ref[:, cols]) * MEM_SCALE
            m = jnp.max(s, axis=-1, keepdims=True)
            p = jnp.exp(s - m)
            l = jnp.sum(p, axis=-1, keepdims=True)
            y_ref[:, cols] = _dot(p.astype(BF16), mv_ref[:, cols]) / l

    return pl.pallas_call(
        body, grid=(nb, nq),
        in_specs=[pl.BlockSpec((MEM_TQ, MEM_WIDTH), lambda b, i: (b * nq + i, MQ_BLK4)),
                  pl.BlockSpec((N_MEM, MEM_WIDTH), lambda b, i: (b, 0)),
                  pl.BlockSpec((N_MEM, MEM_WIDTH), lambda b, i: (b, 1))],
        out_specs=pl.BlockSpec((MEM_TQ, MEM_WIDTH), lambda b, i: (b * nq + i, 0)),
        out_shape=SDS((t, MEM_WIDTH), F32),
        name="mem_attn_fwd", compiler_params=_params("parallel", "parallel"))(proj, mkv, mkv)


def mem_attn_bwd(proj, mkv, dy, nb):
    t = proj.shape[0]
    nq = SEQ // MEM_TQ

    def body(q_ref, mk_ref, mv_ref, do_ref, dq_ref, dmk_ref, dmv_ref):
        @pl.when(pl.program_id(1) == 0)
        def _():
            dmk_ref[...] = jnp.zeros_like(dmk_ref)
            dmv_ref[...] = jnp.zeros_like(dmv_ref)

        for h in range(MEM_HEADS):
            cols = slice(h * LANES, (h + 1) * LANES)
            q = q_ref[:, cols].astype(BF16)
            mk, mv = mk_ref[:, cols], mv_ref[:, cols]
            do = do_ref[:, cols].astype(BF16)
            s = _dot_nt(q, mk) * MEM_SCALE
            e = jnp.exp(s - jnp.max(s, axis=-1, keepdims=True))
            p = e / jnp.sum(e, axis=-1, keepdims=True)
            dp = _dot_nt(do, mv)
            ds = (p * (dp - jnp.sum(p * dp, axis=-1, keepdims=True)) * MEM_SCALE).astype(BF16)
            dq_ref[:, cols] = _dot(ds, mk).astype(BF16)
            dmk_ref[:, cols] += _dot_tn(ds, q)
            dmv_ref[:, cols] += _dot_tn(p.astype(BF16), do)

    ospec = pl.BlockSpec((MEM_TQ, MEM_WIDTH), lambda b, i: (b * nq + i, 0))
    kspec = pl.BlockSpec((N_MEM, MEM_WIDTH), lambda b, i: (b, 0))
    return pl.pallas_call(
        body, grid=(nb, nq),
        in_specs=[pl.BlockSpec((MEM_TQ, MEM_WIDTH), lambda b, i: (b * nq + i, MQ_BLK4)),
                  kspec, pl.BlockSpec((N_MEM, MEM_WIDTH), lambda b, i: (b, 1)), ospec],
        out_specs=[ospec, kspec, kspec],
        out_shape=[SDS((t, MEM_WIDTH), BF16), SDS((nb * N_MEM, MEM_WIDTH), F32), SDS((nb * N_MEM, MEM_WIDTH), F32)],
        name="mem_attn_bwd", compiler_params=_params("parallel", "arbitrary"))(proj, mkv, mkv, dy)


ROW_TM = 512
AG_BLK = 3072 // 1024
BG_BLK = 4608 // 512
MG_BLK = 5632 // 512
GROUPS = ((0, A_WIDTH), (A_WIDTH, MLA_WIDTH), (A_WIDTH + MLA_WIDTH, MEM_WIDTH))
D_MIX = 2048


def _gate_specs():
    def row(w, j):
        return pl.BlockSpec((ROW_TM, w), lambda i: (i, j))

    def vec(w):
        return pl.BlockSpec((1, w), lambda i: (0, 0))

    ys = [row(A_WIDTH, 0), row(MLA_WIDTH, 0), row(MEM_WIDTH, 0)]
    gates = [row(A_WIDTH, AG_BLK), row(MLA_WIDTH, BG_BLK), row(MEM_WIDTH, MG_BLK)]
    gains = [vec(A_WIDTH), vec(MLA_WIDTH), vec(MEM_WIDTH)]
    return row, vec, ys, gates, gains


def gate_out_ln_loss(ya, yb, ym, proj, goa, gob, gom, wout, h32, target, gp, bp):
    t, d = h32.shape
    _, _, ys, gates, gains = _gate_specs()

    def body(ya_ref, yb_ref, ym_ref, ga_ref, gb_ref, gm_ref, goa_ref, gob_ref, gom_ref, w_ref, h_ref, t_ref, gp_ref, bp_ref,
             z_ref, du32_ref, du16_ref, loss_ref, dgp_ref, dbp_ref):
        @pl.when(pl.program_id(0) == 0)
        def _():
            loss_ref[...] = jnp.zeros_like(loss_ref)
            dgp_ref[...] = jnp.zeros_like(dgp_ref)
            dbp_ref[...] = jnp.zeros_like(dbp_ref)

        for (off, w), y_ref, g_ref, go_ref in zip(GROUPS, (ya_ref, yb_ref, ym_ref), (ga_ref, gb_ref, gm_ref),
                                                  (goa_ref, gob_ref, gom_ref)):
            n, _ = _rms(y_ref[...], go_ref[...])
            gt = g_ref[...]
            z_ref[:, off:off + w] = (n * (gt * _sigmoid(gt))).astype(BF16)
        g = gp_ref[...]
        u = ALPHA * h_ref[...] + _dot(z_ref[...], w_ref[...])
        mu = jnp.mean(u, axis=-1, keepdims=True)
        uc = u - mu
        rstd = lax.rsqrt(jnp.mean(uc * uc, axis=-1, keepdims=True) + NORM_EPS)
        xhat = uc * rstd
        err = xhat * g + bp_ref[...] - t_ref[...]
        tok = jnp.sum(err * err, axis=-1, keepdims=True) * (1.0 / d)
        loss_ref[...] += 0.5 * jnp.sum(tok, axis=0, keepdims=True)
        dout = err * (1.0 / d)
        dxhat = dout * g
        du = rstd * (dxhat - jnp.mean(dxhat, axis=-1, keepdims=True)
                     - xhat * jnp.mean(dxhat * xhat, axis=-1, keepdims=True))
        du32_ref[...] = du
        du16_ref[...] = du.astype(BF16)
        dgp_ref[...] += jnp.sum(dout * xhat, axis=0, keepdims=True)
        dbp_ref[...] += jnp.sum(dout, axis=0, keepdims=True)

    row = pl.BlockSpec((ROW_TM, d), lambda i: (i, 0))
    vec = pl.BlockSpec((1, d), lambda i: (0, 0))
    zrow = pl.BlockSpec((ROW_TM, D_MIX), lambda i: (i, 0))
    return pl.pallas_call(
        body, grid=(t // ROW_TM,),
        in_specs=ys + gates + gains + [pl.BlockSpec((D_MIX, d), lambda i: (0, 0)), row, row, vec, vec],
        out_specs=[zrow, row, row, pl.BlockSpec((1, LANES), lambda i: (0, 0)), vec, vec],
        out_shape=[SDS((t, D_MIX), BF16), SDS((t, d), F32), SDS((t, d), BF16), SDS((1, LANES), F32), SDS((1, d), F32),
                   SDS((1, d), F32)],
        name="gate_out_ln_loss", compiler_params=_params("arbitrary"))(
            ya, yb, ym, proj, proj, proj, goa, gob, gom, wout, h32, target, gp, bp)


def gate_bwd(du16, wout, ya, yb, ym, proj, goa, gob, gom):
    t = ya.shape[0]
    row, vec, ys, gates, gains = _gate_specs()

    def body(du_ref, w_ref, ya_ref, yb_ref, ym_ref, ga_ref, gb_ref, gm_ref, goa_ref, gob_ref, gom_ref,
             dya_ref, dyb_ref, dym_ref, dga_ref, dgb_ref, dgm_ref, dgoa_ref, dgob_ref, dgom_ref):
        @pl.when(pl.program_id(0) == 0)
        def _():
            dgoa_ref[...] = jnp.zeros_like(dgoa_ref)
            dgob_ref[...] = jnp.zeros_like(dgob_ref)
            dgom_ref[...] = jnp.zeros_like(dgom_ref)

        dz = _dot_nt(du_ref[...], w_ref[...])
        for (off, w), y_ref, g_ref, go_ref, dy_ref, dg_ref, dgo_ref in zip(
                GROUPS, (ya_ref, yb_ref, ym_ref), (ga_ref, gb_ref, gm_ref), (goa_ref, gob_ref, gom_ref),
                (dya_ref, dyb_ref, dym_ref), (dga_ref, dgb_ref, dgm_ref), (dgoa_ref, dgob_ref, dgom_ref)):
            dzg = dz[:, off:off + w]
            y, gt, go = y_ref[...], g_ref[...], go_ref[...]
            n, r = _rms(y, go)
            sg = _sigmoid(gt)
            dg_ref[...] = (dzg * n * (sg * (1.0 + gt * (1.0 - sg)))).astype(BF16)
            dy, dgo = _rms_bwd(dzg * (gt * sg), y, r, go)
            dy_ref[...] = dy
            dgo_ref[...] += dgo

    widths = (A_WIDTH, MLA_WIDTH, MEM_WIDTH)
    return pl.pallas_call(
        body, grid=(t // ROW_TM,),
        in_specs=[row(D_MODEL, 0), pl.BlockSpec((D_MIX, D_MODEL), lambda i: (0, 0))] + ys + gates + gains,
        out_specs=[row(w, 0) for w in widths] * 2 + [vec(w) for w in widths],
        out_shape=[SDS((t, w), F32) for w in widths] + [SDS((t, w), BF16) for w in widths] + [SDS((1, w), F32) for w in widths],
        name="gate_bwd", compiler_params=_params("arbitrary"))(du16, wout, ya, yb, ym, proj, proj, proj, goa, gob, gom)


def dh_ln_bwd(pieces, win_t, du32, x2, g_emb, xch):
    t, d = x2.shape

    def body(*refs):
        p_refs = refs[:len(pieces)]
        w_ref, du_ref, x_ref, g_ref, dx_ref, dg_ref, db_ref = refs[len(pieces):]

        @pl.when(pl.program_id(0) == 0)
        def _():
            dg_ref[...] = jnp.zeros_like(dg_ref)
            db_ref[...] = jnp.zeros_like(db_ref)

        dh = ALPHA * du_ref[...]
        for p_ref, off, w in zip(p_refs, PIECE_OFFS, PIECE_WIDTHS):
            dh = dh + _dot(p_ref[...], w_ref[off:off + w, :])
        x = x_ref[...]
        xc = x - jnp.mean(x, axis=-1, keepdims=True)
        rstd = lax.rsqrt(jnp.mean(xc * xc, axis=-1, keepdims=True) + NORM_EPS)
        xhat = xc * rstd
        dg_ref[...] += jnp.sum(dh * xhat, axis=0, keepdims=True)
        db_ref[...] += jnp.sum(dh, axis=0, keepdims=True)
        tg = dh * g_ref[...]
        dx_ref[...] = rstd * (tg - jnp.mean(tg, axis=-1, keepdims=True)
                              - xhat * jnp.mean(tg * xhat, axis=-1, keepdims=True))

    row = pl.BlockSpec((ROW_TM, d), lambda i: (i, 0))
    vec = pl.BlockSpec((1, d), lambda i: (0, 0))
    return call_hosting_exchange(
        body, xch, grid=(t // ROW_TM,),
        in_specs=[pl.BlockSpec((ROW_TM, w), lambda i: (i, 0)) for w in PIECE_WIDTHS]
        + [pl.BlockSpec(win_t.shape, lambda i: (0, 0)), row, row, vec],
        out_specs=[row, vec, vec],
        out_shape=[SDS((t, d), F32), SDS((1, d), F32), SDS((1, d), F32)],
        scratch_shapes=[], name="dh_ln_bwd", operands=(*pieces, win_t, du32, x2, g_emb))


def _adamw(w, g, m, v):
    m2 = ADAM_B1 * m + (1.0 - ADAM_B1) * g
    v2 = ADAM_B2 * v + (1.0 - ADAM_B2) * (g * g)
    m_hat = m2 / (1.0 - ADAM_B1 ** ADAM_STEP)
    v_hat = v2 / (1.0 - ADAM_B2 ** ADAM_STEP)
    return -ADAM_LR * (m_hat / (jnp.sqrt(v_hat) + ADAM_EPS) + ADAM_WD * w), m2, v2


def adamw_shard(w, parts, m, v, name):
    r, c = w.shape
    if r % 256 == 0 or r * c <= 256 * 1024:
        tr, tc = min(r, 256), c
    else:
        tr, tc = r, 256

    def body(w_ref, p_ref, m_ref, v_ref, g_ref, d_ref, nm_ref, nv_ref):
        g = p_ref[0].astype(F32)
        for k in range(1, N_DEV):
            g = g + p_ref[k].astype(F32)
        g_ref[...] = g
        d_ref[...], nm_ref[...], nv_ref[...] = _adamw(w_ref[...], g, m_ref[...], v_ref[...])

    blk = pl.BlockSpec((tr, tc), lambda i, j: (i, j))
    return pl.pallas_call(
        body, grid=(r // tr, c // tc),
        in_specs=[blk, pl.BlockSpec((N_DEV, tr, tc), lambda i, j: (0, i, j)), blk, blk],
        out_specs=[blk] * 4, out_shape=[SDS((r, c), F32)] * 4, name=name,
        compiler_params=_params("parallel", "parallel"))(w, parts, m, v)


def _place():
    return lax.axis_index("x"), lax.axis_index("y"), lax.axis_index("c")


def _flat(px, py, pc):
    return 4 * px + 2 * py + pc


def _peer(x, y, c, k):
    return (1 - x if k & 4 else x, 1 - y if k & 2 else y, 1 - c if k & 1 else c)


def cast_shards(shards):
    def body(*refs):
        n = len(refs) // 2
        for i_ref, o_ref in zip(refs[:n], refs[n:]):
            o_ref[...] = i_ref[...].astype(BF16)

    return pl.pallas_call(body, out_shape=[SDS(s.shape, BF16) for s in shards], name="cast_shards",
                          compiler_params=_params())(*shards)


def allgather_weights(shards):
    n = len(shards)

    def body(*refs):
        ins, outs = refs[:n], refs[n:2 * n]
        send_sems, recv_sems, local_sems = refs[2 * n:]
        x, y, c = _place()
        me, sib = (x, y, c), (x, y, 1 - c)
        chips = [(1 - x, y), (x, 1 - y), (1 - x, 1 - y)]

        def copy(a, k, block, to, src=None):
            dst = outs[a].at[_flat(*block)]
            return pltpu.make_async_remote_copy(
                src_ref=dst if src is None else src, dst_ref=dst,
                send_sem=send_sems.at[a * 7 + k], recv_sem=recv_sems.at[a * 7 + k],
                device_id=to, device_id_type=MESH)

        mine = [pltpu.make_async_copy(ins[a], outs[a].at[_flat(*me)], local_sems.at[a]) for a in range(n)]
        for cp in mine:
            cp.start()
        first = []
        for a in range(n):
            first.append(copy(a, 0, me, sib, src=ins[a]))
            first += [copy(a, 1 + j, me, (*chip, c), src=ins[a]) for j, chip in enumerate(chips)]
        for cp in first:
            cp.start()
        passed = []
        for j, chip in enumerate(chips):
            for a in range(n):
                copy(a, 1 + j, (*chip, c), me).wait_recv()
                fwd = copy(a, 4 + j, (*chip, c), sib)
                fwd.start()
                passed.append(fwd)
        for a in range(n):
            copy(a, 0, sib, me).wait_recv()
            for j, chip in enumerate(chips):
                copy(a, 4 + j, (*chip, 1 - c), me).wait_recv()
        for cp in first + passed:
            cp.wait_send()
        for cp in mine:
            cp.wait()

    hbm = pl.BlockSpec(memory_space=pl.ANY)
    return pl.pallas_call(
        body, out_shape=[SDS((N_DEV,) + s.shape, s.dtype) for s in shards],
        in_specs=[hbm] * n, out_specs=[hbm] * n,
        scratch_shapes=[pltpu.SemaphoreType.DMA((7 * n,)), pltpu.SemaphoreType.DMA((7 * n,)), pltpu.SemaphoreType.DMA((n,))],
        name="allgather_weights", compiler_params=_params())(*shards)


ALL_DEVICES = tuple(range(N_DEV))


def _exchange_plan(src_refs, land_refs, dests, send_sems, recv_sems, local_sems):
    x, y, c = _place()
    me = _flat(x, y, c)
    plan = []
    for a, (src, land, dl) in enumerate(zip(src_refs, land_refs, dests)):
        for li, j in enumerate(dl):
            to = ((j >> 2) & 1, (j >> 1) & 1, j & 1)
            block = src.at[li] if len(src.shape) == len(land.shape) else src

            def push(slot, a=a, block=block, land=land, j=j, to=to):
                return pltpu.make_async_remote_copy(
                    src_ref=block, dst_ref=land.at[slot], send_sem=send_sems.at[a * N_DEV + j],
                    recv_sem=recv_sems.at[a * N_DEV + slot], device_id=to, device_id_type=MESH)

            own = pltpu.make_async_copy(block, land.at[j], local_sems.at[a])
            plan.append((j, push(me), own, [push(s) for s in range(N_DEV) if s != j]))
    return me, plan


def _exchange_start(me, plan):
    for j, send, own, _ in plan:
        @pl.when(me != j)
        def _(send=send):
            send.start()

        @pl.when(me == j)
        def _(own=own):
            own.start()


def _exchange_wait(me, plan):
    for j, send, own, arrivals in plan:
        @pl.when(me != j)
        def _(send=send):
            send.wait_send()

        @pl.when(me == j)
        def _(own=own, arrivals=arrivals):
            own.wait()
            for arrival in arrivals:
                arrival.wait_recv()


def call_hosting_exchange(core, xch, *, grid, in_specs, out_specs, out_shape, scratch_shapes, name, operands):
    srcs, dests, landing = xch
    n, n_in, n_out, n_scr = len(srcs), len(in_specs), len(out_specs), len(scratch_shapes)

    def body(*refs):
        ins, src_refs = refs[:n_in], refs[n_in:n_in + n]
        outs = refs[n_in + 2 * n:n_in + 2 * n + n_out]
        land_refs = refs[n_in + 2 * n + n_out:n_in + 3 * n + n_out]
        scratch = refs[n_in + 3 * n + n_out:n_in + 3 * n + n_out + n_scr]
        sems = refs[n_in + 3 * n + n_out + n_scr:]
        first = functools.reduce(jnp.logical_and, [pl.program_id(i) == 0 for i in range(len(grid))])
        last = functools.reduce(jnp.logical_and, [pl.program_id(i) == grid[i] - 1 for i in range(len(grid))])
        me, plan = _exchange_plan(src_refs, land_refs, dests, *sems)

        @pl.when(first)
        def _():
            _exchange_start(me, plan)

        core(*ins, *outs, *scratch)

        @pl.when(last)
        def _():
            _exchange_wait(me, plan)

    hbm = pl.BlockSpec(memory_space=pl.ANY)
    res = pl.pallas_call(
        body, grid=grid,
        in_specs=list(in_specs) + [hbm] * (2 * n), out_specs=list(out_specs) + [hbm] * n,
        out_shape=list(out_shape) + [SDS(l.shape, l.dtype) for l in landing],
        scratch_shapes=list(scratch_shapes) + [pltpu.SemaphoreType.DMA((N_DEV * n,)), pltpu.SemaphoreType.DMA((N_DEV * n,)),
                                               pltpu.SemaphoreType.DMA((n,))],
        input_output_aliases={n_in + n + k: n_out + k for k in range(n)},
        name=name, compiler_params=_params(*(("arbitrary",) * len(grid))))(*operands, *srcs, *landing)
    return res[:n_out], res[n_out:]


SLOT_ROWS = 8


def small_allreduce_adamw(loss_sum, grads, ws, ms, vs):
    n = len(grads)
    rows = [g.shape[0] for g in grads]
    total = SLOT_ROWS * (n + 1)

    def body(*refs):
        loss_ref, g_refs, w_refs = refs[0], refs[1:1 + n], refs[1 + n:1 + 2 * n]
        m_refs, v_refs = refs[1 + 2 * n:1 + 3 * n], refs[1 + 3 * n:1 + 4 * n]
        outs = refs[1 + 4 * n:2 + 8 * n]
        vec, gath, tot, send_sems, recv_sems = refs[2 + 8 * n:]
        x, y, c = _place()
        me = _flat(x, y, c)
        vec[...] = jnp.zeros_like(vec)
        vec[0:1, :] = loss_ref[...]
        for i in range(n):
            vec[SLOT_ROWS * (i + 1):SLOT_ROWS * (i + 1) + rows[i], :] = g_refs[i][...]
        gath[me] = vec[...]
        copies = []
        for k in range(1, N_DEV):
            peer = _peer(x, y, c, k)
            copies.append(pltpu.make_async_remote_copy(
                src_ref=vec, dst_ref=gath.at[me], send_sem=send_sems.at[k - 1], recv_sem=recv_sems.at[k - 1],
                device_id=peer, device_id_type=MESH))
        for cp in copies:
            cp.start()
        for cp in copies:
            cp.wait_recv()
        for cp in copies:
            cp.wait_send()
        g = gath[0]
        for j in range(1, N_DEV):
            g = g + gath[j]
        tot[...] = g
        outs[0][...] = tot[0:1, :]
        for i in range(n):
            gi = tot[SLOT_ROWS * (i + 1):SLOT_ROWS * (i + 1) + rows[i], :]
            outs[1 + i][...] = gi
            outs[1 + n + i][...], outs[1 + 2 * n + i][...], outs[1 + 3 * n + i][...] = _adamw(
                w_refs[i][...], gi, m_refs[i][...], v_refs[i][...])

    shapes = [SDS(g.shape, F32) for g in grads]
    return pl.pallas_call(
        body, out_shape=[SDS((1, LANES), F32)] + shapes * 4,
        scratch_shapes=[pltpu.VMEM((total, LANES), F32), pltpu.VMEM((N_DEV, total, LANES), F32), pltpu.VMEM((total, LANES), F32),
                        pltpu.SemaphoreType.DMA((7,)), pltpu.SemaphoreType.DMA((7,))],
        name="small_allreduce_adamw", compiler_params=_params())(loss_sum, *grads, *ws, *ms, *vs)


def _rope_lane_patterns():
    inv = lambda r: ROPE_THETA ** (-(jnp.arange(0, r, 2, dtype=F32) / r))
    z = lambda n: jnp.zeros((n,), F32)
    o = lambda n: jnp.ones((n,), F32)
    half, rest = A_ROT // 2, A_HEAD_DIM - A_ROT
    ia, im = inv(A_ROT), inv(MLA_ROPE)
    mh, tail = MLA_ROPE // 2, LANES - MLA_NOPE - MLA_ROPE
    rows = [jnp.tile(jnp.concatenate([ia, ia, z(rest)]), 2),
            jnp.tile(jnp.concatenate([o(half), z(half + rest)]), 2),
            jnp.tile(jnp.concatenate([z(half), o(half), z(rest)]), 2),
            jnp.concatenate([z(MLA_NOPE), im, im, z(tail)]),
            jnp.concatenate([z(MLA_NOPE), o(mh), z(mh + tail)]),
            jnp.concatenate([z(MLA_NOPE + mh), o(mh), z(tail)]),
            z(LANES), z(LANES)]
    return jnp.stack(rows)


def rope_tables(positions):
    pos = positions.astype(F32).reshape(-1, 1)
    t = pos.shape[0]
    tm = 512

    def body(pos_ref, pat_ref, *outs):
        p = pos_ref[...]
        for k in range(2):
            inv, first, second = pat_ref[3 * k:3 * k + 1, :], pat_ref[3 * k + 1:3 * k + 2, :], pat_ref[3 * k + 2:3 * k + 3, :]
            ang = p * inv
            sn = jnp.sin(ang)
            outs[3 * k][...] = jnp.where(first + second > 0.0, jnp.cos(ang), 1.0)
            outs[3 * k + 1][...] = -first * sn
            outs[3 * k + 2][...] = second * sn

    row = pl.BlockSpec((tm, LANES), lambda i: (i, 0))
    res = pl.pallas_call(
        body, grid=(t // tm,),
        in_specs=[pl.BlockSpec((tm, 1), lambda i: (i, 0)), pl.BlockSpec((8, LANES), lambda i: (0, 0))],
        out_specs=[row] * 6, out_shape=[SDS((t, LANES), F32)] * 6, name="rope_tables",
        compiler_params=_params("parallel"))(pos, _rope_lane_patterns())
    return tuple(res[:3]), tuple(res[3:])


KR_LO, KR_HI = 4480, 4512
W_IN_SHARD = D_IN // N_DEV
AG_SPLIT = 5 * W_IN_SHARD - 3 * A_WIDTH
BG_SPLIT = 6 * W_IN_SHARD - KR_HI


def _w_in_working_t(g):
    w = g.reshape(D_IN, D_MODEL)
    z = lambda n: jnp.zeros((n, D_MODEL), w.dtype)
    return jnp.concatenate([w[:KR_LO], z(MLA_NOPE), w[KR_LO:KR_HI], z(LANES - MLA_NOPE - MLA_ROPE), w[KR_HI:]], 0)


def _shards(rows):
    return rows.reshape(-1, W_IN_SHARD, D_MODEL).astype(BF16)


def _w_in_shards_0_4(d_aq, d_ak, d_av, d_ag):
    return _shards(jnp.concatenate([d_aq, d_ak, d_av, d_ag[:AG_SPLIT]], 0))


def _w_in_shard_5(d_ag, d_cc, d_bg):
    kr = MLA_Q_RANK + MLA_KV_RANK + MLA_NOPE
    return _shards(jnp.concatenate([d_ag[AG_SPLIT:], d_cc[:MLA_Q_RANK + MLA_KV_RANK], d_cc[kr:kr + MLA_ROPE], d_bg[:BG_SPLIT]], 0))


def _w_in_shards_6_7(d_bg, d_mq, d_mg):
    return _shards(jnp.concatenate([d_bg[BG_SPLIT:], d_mq, d_mg], 0))


def _w_uq_working(g):
    w = jnp.pad(g.transpose(1, 0, 2), ((0, 0), (0, 0), (0, LANES - MLA_NOPE - MLA_ROPE)))
    return w.reshape(MLA_Q_RANK, MLA_QW)


def _w_uq_parts(dw):
    return dw.reshape(MLA_Q_RANK, MLA_HEADS, LANES)[:, :, :MLA_NOPE + MLA_ROPE].transpose(1, 0, 2)


def _w_ukv_working(g):
    wk = jnp.pad(g[:, :, :MLA_NOPE].transpose(1, 0, 2), ((0, 0), (0, 0), (0, LANES - MLA_NOPE)))
    wv = g[:, :, MLA_NOPE:].transpose(1, 0, 2)
    return jnp.concatenate([wk.reshape(MLA_KV_RANK, MLA_QW), wv.reshape(MLA_KV_RANK, MLA_WIDTH)], 1)


def _w_ukv_parts(dw):
    dk = dw[:, :MLA_QW].reshape(MLA_KV_RANK, MLA_HEADS, LANES)[:, :, :MLA_NOPE]
    dv = dw[:, MLA_QW:].reshape(MLA_KV_RANK, MLA_HEADS, MLA_V)
    return jnp.concatenate([dk, dv], -1).transpose(1, 0, 2)


SMALL_NAMES = ("g_emb", "b_emb", "g_cq", "g_ckv", "g_out_a", "g_out_b", "g_out_m", "g_post", "b_post")


def kernel(x, mem, positions, g_emb, b_emb, w_in, g_cq, g_ckv, w_uq, w_ukv, w_mem_kv, g_out_a, g_out_b, g_out_m, w_out, g_post, b_post, loss_target, m_g_emb, m_b_emb, m_w_in, m_g_cq, m_g_ckv, m_w_uq, m_w_ukv, m_w_mem_kv, m_g_out_a, m_g_out_b, m_g_out_m, m_w_out, m_g_post, m_b_post, v_g_emb, v_b_emb, v_w_in, v_g_cq, v_g_ckv, v_w_uq, v_w_ukv, v_w_mem_kv, v_g_out_a, v_g_out_b, v_g_out_m, v_w_out, v_g_post, v_b_post):
    nb = x.shape[0]
    t = nb * SEQ
    x2 = x.reshape(t, D_MODEL)
    tgt2 = loss_target.reshape(t, D_MODEL)
    mem2 = mem.reshape(nb * N_MEM, D_MODEL)
    g_emb2, b_emb2 = g_emb.reshape(1, -1), b_emb.reshape(1, -1)
    (a_c, a_sa, a_sb), (m_c, m_sa, m_sb) = rope_tables(positions)

    w_in_t, m_w_in_t, v_w_in_t = w_in[0].T, m_w_in[0].T, v_w_in[0].T
    s_in, s_uq, s_ukv, s_mem, s_out = cast_shards((w_in_t, w_uq[0], w_ukv[0], w_mem_kv[0], w_out[0]))
    (g_in,) = allgather_weights((s_in,))
    win_t = _w_in_working_t(g_in)

    h32, h16 = ln_emb_fwd(x2, g_emb2, b_emb2)
    proj = mm_nn(h16, win_t, F32, 512, 1536, "proj", rhs_transposed=True)
    later = (s_uq, s_ukv, s_mem, s_out)
    (ya, lse_a), qkv_d, (g_uq, g_ukv, g_mem, g_out) = a_attn_fwd(
        proj, a_c, a_sa, a_sb, nb,
        (later, (ALL_DEVICES,) * len(later), tuple(lax.empty((N_DEV,) + w.shape, BF16) for w in later)))
    wuq_w = _w_uq_working(g_uq)
    wkv_w = _w_ukv_working(g_ukv)
    wmem = g_mem.reshape(D_MODEL, 2 * MEM_WIDTH)
    wout = g_out.reshape(D_MIX, D_MODEL)
    qb, kb, vb = mla_prep_fwd(proj, m_c, m_sa, m_sb, g_cq, g_ckv, wuq_w, wkv_w)
    yb, lse_b = mla_attn_fwd(qb, kb, vb, nb)
    mkv = mm_nn(mem2, wmem, BF16, nb * N_MEM, 512, "mem_kv")
    ym = mem_attn_fwd(proj, mkv, nb)
    z, du32, du16, loss_sum, dg_post, db_post = gate_out_ln_loss(
        ya, yb, ym, proj, g_out_a, g_out_b, g_out_m, wout, h32, tgt2, g_post, b_post)

    dya, dyb, dym, dag, dbg, dmg, dg_out_a, dg_out_b, dg_out_m = gate_bwd(
        du16, wout, ya, yb, ym, proj, g_out_a, g_out_b, g_out_m)
    dw_out = mm_tn(z, du16, 1024, "dw_out")
    dmq, dmk, dmv = mem_attn_bwd(proj, mkv, dym, nb)
    dw_mem = mm_tn(mem2, jnp.concatenate([dmk, dmv], 1), nb * N_MEM, "dw_mem")
    d_ag, d_bg, d_mq, d_mg = [mm_tn(p, h16, 2048, "dw_in_" + n) for n, p in (("ag", dag), ("bg", dbg), ("mq", dmq), ("mg", dmg))]
    landing = lambda w, dtype=F32: lax.empty((N_DEV,) + w.shape, dtype)
    big_w = (w_in_t, w_uq[0], w_ukv[0], w_mem_kv[0], w_out[0])
    (daq, dak, dav), (p_out, p_mem, p_in) = a_attn_bwd(
        qkv_d, a_c, a_sa, a_sb, dya, ya, lse_a, nb,
        ((dw_out.reshape(N_DEV, D_MIX // N_DEV, D_MODEL), dw_mem.reshape(N_DEV, D_MODEL // N_DEV, 2 * MEM_WIDTH),
          _w_in_shards_6_7(d_bg, d_mq, d_mg)),
         (ALL_DEVICES, ALL_DEVICES, (6, 7)),
         (landing(w_out[0]), landing(w_mem_kv[0]), landing(w_in_t, BF16))))
    d_aq, d_ak, d_av = [mm_tn(p, h16, 2048, "dw_in_" + n) for n, p in (("aq", daq), ("ak", dak), ("av", dav))]
    (dqb, dkb, dvb), (p_in,) = mla_attn_bwd(
        qb, kb, vb, dyb, yb, lse_b, nb, ((_w_in_shards_0_4(d_aq, d_ak, d_av, d_ag),), ((0, 1, 2, 3, 4),), (p_in,)))
    dcc, dqf, cqn, dkvf, ckvn, dg_cq, dg_ckv = mla_prep_bwd(proj, m_c, m_sa, m_sb, g_cq, g_ckv, wuq_w, wkv_w, dqb, dkb, dvb)
    dw_uq = mm_tn(cqn, dqf, 2048, "dw_uq")
    dw_ukv = mm_tn(ckvn, dkvf, 2048, "dw_ukv")
    d_cc = mm_tn(dcc, h16, 2048, "dw_in_cc")
    pieces = (daq, dak, dav, dag, dcc, dbg, dmq, dmg)
    (grad_x, dg_emb, db_emb), (p_in, p_uq, p_ukv) = dh_ln_bwd(
        pieces, win_t, du32, x2, g_emb2,
        ((_w_in_shard_5(d_ag, d_cc, d_bg), _w_uq_parts(dw_uq), _w_ukv_parts(dw_ukv)),
         ((5,), ALL_DEVICES, ALL_DEVICES),
         (p_in, landing(w_uq[0]), landing(w_ukv[0]))))

    parts = (p_in, p_uq, p_ukv, p_mem, p_out)
    big_m = (m_w_in_t, m_w_uq[0], m_w_ukv[0], m_w_mem_kv[0], m_w_out[0])
    big_v = (v_w_in_t, v_w_uq[0], v_w_ukv[0], v_w_mem_kv[0], v_w_out[0])
    big = {}
    for name, w, p, m, v in zip(("w_in", "w_uq", "w_ukv", "w_mem_kv", "w_out"), big_w, parts, big_m, big_v):
        res = adamw_shard(w, p, m, v, "adamw_" + name)
        big[name] = [(o.T if name == "w_in" else o)[None] for o in res]

    small_w = (g_emb, b_emb, g_cq, g_ckv, g_out_a, g_out_b, g_out_m, g_post, b_post)
    small_m = (m_g_emb, m_b_emb, m_g_cq, m_g_ckv, m_g_out_a, m_g_out_b, m_g_out_m, m_g_post, m_b_post)
    small_v = (v_g_emb, v_b_emb, v_g_cq, v_g_ckv, v_g_out_a, v_g_out_b, v_g_out_m, v_g_post, v_b_post)
    small_g = (dg_emb, db_emb, dg_cq, dg_ckv, dg_out_a, dg_out_b, dg_out_m, dg_post, db_post)
    rows128 = lambda vals: [v.reshape(-1, LANES) for v in vals]
    res = small_allreduce_adamw(loss_sum, rows128(small_g), rows128(small_w), rows128(small_m), rows128(small_v))
    loss = res[0][0, 0]
    n_small = len(small_w)
    sg, sd, sm, sv = [[r.reshape(w.shape) for r, w in zip(res[1 + k * n_small:1 + (k + 1) * n_small], small_w)]
                      for k in range(4)]

    order = ("g_emb", "b_emb", "w_in", "g_cq", "g_ckv", "w_uq", "w_ukv", "w_mem_kv", "g_out_a", "g_out_b", "g_out_m",
             "w_out", "g_post", "b_post")
    small_idx = {n: i for i, n in enumerate(SMALL_NAMES)}
    outs = [loss, grad_x.reshape(x.shape)]
    for kind in range(4):
        for name in order:
            outs.append(big[name][kind] if name in big else (sg, sd, sm, sv)[kind][small_idx[name]])
    return tuple(outs)
```

```python
import functools

import jax
import jax.numpy as jnp
from jax import lax
from jax.experimental import pallas as pl
from jax.experimental.pallas import tpu as pltpu

F32 = jnp.float32
BF16 = jnp.bfloat16
SDS = jax.ShapeDtypeStruct
MESH = pl.DeviceIdType.MESH

D_MODEL = 1024
SEQ = 2048
A_HEADS, A_HEAD_DIM, A_ROT = 16, 64, 16
A_WIDTH = 1024
DILATIONS = (1, 4, 16)
N_SIDE = 64
MLA_HEADS, MLA_Q_RANK, MLA_KV_RANK = 8, 256, 128
MLA_NOPE, MLA_ROPE, MLA_V = 64, 32, 64
MLA_WIDTH = 512
N_MEM, MEM_HEADS, MEM_HEAD_DIM, MEM_WIDTH = 256, 4, 128, 512
ROPE_THETA = 500000.0
NORM_EPS = 1e-5
NEG_INF = -1e30
ALPHA = 2.0 ** 0.25
D_IN = 6048
N_DEV = 8

ADAM_LR, ADAM_B1, ADAM_B2, ADAM_EPS, ADAM_WD, ADAM_STEP = 0.001, 0.9, 0.999, 1e-08, 0.01, 10

D_INW = 6144
PIECE_WIDTHS = (1024, 1024, 1024, 1024, 512, 512, 512, 512)
PIECE_OFFS = (0, 1024, 2048, 3072, 4096, 4608, 5120, 5632)
LANES = 128
VMEM_LIMIT = 56 * 1024 * 1024


def _params(*sem):
    kw = dict(vmem_limit_bytes=VMEM_LIMIT)
    if sem:
        kw["dimension_semantics"] = sem
    return pltpu.CompilerParams(**kw)


def _dot(a, b):
    return jnp.dot(a, b, preferred_element_type=F32)


def _dot_nt(a, b):
    return lax.dot_general(a, b, (((1,), (1,)), ((), ())), preferred_element_type=F32)


def _dot_tn(a, b):
    return lax.dot_general(a, b, (((0,), (0,)), ((), ())), preferred_element_type=F32)


def _sigmoid(x):
    return 1.0 / (1.0 + jnp.exp(-x))


def _rope_fwd(x, c, sa, sb, half):
    n = x.shape[-1]
    return x * c + pltpu.roll(x, n - half, 1) * sa + pltpu.roll(x, half, 1) * sb


def _rope_bwd(dy, c, sa, sb, half):
    n = dy.shape[-1]
    return dy * c + pltpu.roll(dy * sa, half, 1) + pltpu.roll(dy * sb, n - half, 1)


def mm_nn(a, b, out_dtype, tm, tn, name, rhs_transposed=False):
    m, k = a.shape
    n = b.shape[0] if rhs_transposed else b.shape[1]
    dot = _dot_nt if rhs_transposed else _dot

    def body(a_ref, b_ref, o_ref):
        o_ref[...] = dot(a_ref[...].astype(BF16), b_ref[...].astype(BF16)).astype(o_ref.dtype)

    b_spec = pl.BlockSpec((tn, k), lambda j, i: (j, 0)) if rhs_transposed else pl.BlockSpec((k, tn), lambda j, i: (0, j))
    return pl.pallas_call(
        body, grid=(n // tn, m // tm),
        in_specs=[pl.BlockSpec((tm, k), lambda j, i: (i, 0)), b_spec],
        out_specs=pl.BlockSpec((tm, tn), lambda j, i: (i, j)),
        out_shape=SDS((m, n), out_dtype), name=name,
        compiler_params=_params("parallel", "parallel"))(a, b)


def mm_tn(a, b, tt, name):
    t, m = a.shape
    n = b.shape[1]

    def body(a_ref, b_ref, o_ref):
        @pl.when(pl.program_id(0) == 0)
        def _():
            o_ref[...] = jnp.zeros_like(o_ref)

        o_ref[...] += _dot_tn(a_ref[...].astype(BF16), b_ref[...].astype(BF16))

    return pl.pallas_call(
        body, grid=(t // tt,),
        in_specs=[pl.BlockSpec((tt, m), lambda i: (i, 0)), pl.BlockSpec((tt, n), lambda i: (i, 0))],
        out_specs=pl.BlockSpec((m, n), lambda i: (0, 0)),
        out_shape=SDS((m, n), F32), name=name,
        compiler_params=_params("arbitrary"))(a, b)


def mm_tn_group(pieces, b, tt, name):
    n, (t, w), cols = len(pieces), pieces[0].shape, b.shape[1]
    nt = t // tt

    def body(*refs):
        p_refs, b_ref, o_ref = refs[:n], refs[n], refs[n + 1]

        @pl.when(pl.program_id(1) == 0)
        def _():
            o_ref[...] = jnp.zeros_like(o_ref)

        for k in range(n):
            @pl.when(pl.program_id(0) == k)
            def _(k=k):
                o_ref[...] += _dot_tn(p_refs[k][...], b_ref[...])

    def piece_spec(k):
        return pl.BlockSpec((tt, w), lambda p, i: (jnp.where(p < k, 0, jnp.where(p > k, nt - 1, i)), 0))

    return pl.pallas_call(
        body, grid=(n, nt),
        in_specs=[piece_spec(k) for k in range(n)] + [pl.BlockSpec((tt, cols), lambda p, i: (i, 0))],
        out_specs=pl.BlockSpec((w, cols), lambda p, i: (p, 0)),
        out_shape=SDS((n * w, cols), F32), name=name,
        compiler_params=_params("arbitrary", "arbitrary"))(*pieces, b)


def ln_emb_fwd(x2, g, b):
    t, d = x2.shape
    tm = 512

    def body(x_ref, g_ref, b_ref, h32_ref, h16_ref):
        x = x_ref[...]
        mu = jnp.mean(x, axis=-1, keepdims=True)
        xc = x - mu
        var = jnp.mean(xc * xc, axis=-1, keepdims=True)
        h = xc * lax.rsqrt(var + NORM_EPS) * g_ref[...] + b_ref[...]
        h32_ref[...] = h
        h16_ref[...] = h.astype(BF16)

    row = pl.BlockSpec((tm, d), lambda i: (i, 0))
    vec = pl.BlockSpec((1, d), lambda i: (0, 0))
    return pl.pallas_call(
        body, grid=(t // tm,), in_specs=[row, vec, vec], out_specs=[row, row],
        out_shape=[SDS((t, d), F32), SDS((t, d), BF16)], name="ln_emb_fwd",
        compiler_params=_params("parallel"))(x2, g, b)


Q_BLK = 128
UNROLL_FWD = 8
UNROLL_BWD = 8


def _pattern_geometry(d):
    length = SEQ // d
    nblk = length // Q_BLK
    kwin = min(2 * Q_BLK, length)
    return length, nblk, kwin


def _block_coords(idx, d):
    length, nblk, kwin = _pattern_geometry(d)
    r = lax.shift_right_logical(idx, nblk.bit_length() - 1)
    i = idx & (nblk - 1)
    q0 = pl.multiple_of(r * length + i * Q_BLK, Q_BLK)
    ks = jnp.clip(i * Q_BLK - N_SIDE, 0, length - kwin)
    k0 = pl.multiple_of(r * length + ks, N_SIDE)
    qpos = i * Q_BLK + lax.broadcasted_iota(jnp.int32, (Q_BLK, kwin), 0)
    kpos = ks + lax.broadcasted_iota(jnp.int32, (Q_BLK, kwin), 1)
    valid = jnp.abs(kpos - qpos) <= N_SIDE
    return q0, k0, kwin, valid


def _deinterleave(src_ref, dst_ref, d, dtype, tmp_ref):
    if d == 1:
        dst_ref[...] = src_ref[...].astype(dtype)
        return
    q = SEQ // 4
    if d == 4:
        for r in range(4):
            dst_ref[r * q:(r + 1) * q, :] = src_ref[pl.ds(r, q, stride=4), :].astype(dtype)
        return
    assert d == 16
    n = SEQ // 16
    for r in range(4):
        tmp_ref[r * q:(r + 1) * q, :] = src_ref[pl.ds(r, q, stride=4), :]
    for r in range(4):
        for j in range(4):
            dst_ref[(r + 4 * j) * n:(r + 4 * j + 1) * n, :] = tmp_ref[pl.ds(r * q + j, n, stride=4), :].astype(dtype)


def _class16_to_class4(src_ref, dst_ref):
    q, n = SEQ // 4, SEQ // 16
    for r in range(4):
        for j in range(4):
            dst_ref[pl.ds(r * q + j, n, stride=4), :] = src_ref[(r + 4 * j) * n:(r + 4 * j + 1) * n, :]


def _interleave(src_ref, dst_ref, d, tmp_ref, accumulate):
    q = SEQ // 4
    if d == 16:
        _class16_to_class4(src_ref, tmp_ref)
        src_ref = tmp_ref
    else:
        assert d == 4
    for r in range(4):
        rows = pl.ds(r, q, stride=4)
        val = src_ref[r * q:(r + 1) * q, :]
        dst_ref[rows, :] = dst_ref[rows, :] + val if accumulate else val


def a_attn_fwd(proj, ca, sa, sb, nb, xch):
    t = proj.shape[0]
    n_pairs = A_WIDTH // LANES

    def body(q_ref, k_ref, v_ref, c_ref, sa_ref, sb_ref, y_ref, lse_ref, *rest):
        qkv_d, (qr_s, kr_s, oc_s, lc_s, o1_s, l1_s, o2_s, l2_s, o3_s, l3_s, tmp_s) = rest[:9], rest[9:]
        c, s_a, s_b = c_ref[...], sa_ref[...], sb_ref[...]
        qr_s[...] = _rope_fwd(q_ref[...], c, s_a, s_b, A_ROT // 2) * (A_HEAD_DIM ** -0.5)
        kr_s[...] = _rope_fwd(k_ref[...], c, s_a, s_b, A_ROT // 2)
        head0 = lax.broadcasted_iota(jnp.int32, (Q_BLK, LANES), 1) < A_HEAD_DIM
        nat = ((o1_s, l1_s), (o2_s, l2_s), (o3_s, l3_s))

        for g, d in enumerate(DILATIONS):
            qd_s, kd_s, vd_s = qkv_d[3 * g:3 * g + 3]
            _deinterleave(qr_s, qd_s, d, BF16, tmp_s)
            _deinterleave(kr_s, kd_s, d, BF16, tmp_s)
            _deinterleave(v_ref, vd_s, d, BF16, tmp_s)
            o_dst, l_dst = (nat[g] if d == 1 else (oc_s, lc_s))

            def block(idx, carry, d=d, o_dst=o_dst, l_dst=l_dst, qd_s=qd_s, kd_s=kd_s, vd_s=vd_s):
                q0, k0, kwin, valid = _block_coords(idx, d)
                qb = qd_s[pl.ds(q0, Q_BLK), :]
                kb = kd_s[pl.ds(k0, kwin), :]
                vb = vd_s[pl.ds(k0, kwin), :]
                zero = jnp.zeros_like(qb)
                q2 = jnp.concatenate([jnp.where(head0, qb, zero), jnp.where(head0, zero, qb)], 0)
                s = jnp.where(jnp.concatenate([valid, valid], 0), _dot_nt(q2, kb), NEG_INF)
                m = jnp.max(s, axis=-1, keepdims=True)
                p = jnp.exp(s - m)
                l = jnp.sum(p, axis=-1, keepdims=True)
                o2 = _dot(p.astype(BF16), vb) / l
                l2 = m + jnp.log(l)
                o_dst[pl.ds(q0, Q_BLK), :] = jnp.where(head0, o2[:Q_BLK], o2[Q_BLK:])
                l_dst[pl.ds(q0, Q_BLK), :] = jnp.where(head0, l2[:Q_BLK], l2[Q_BLK:])
                return carry

            lax.fori_loop(0, SEQ // Q_BLK, block, 0, unroll=UNROLL_FWD)
            if d > 1:
                _interleave(oc_s, nat[g][0], d, tmp_s, False)
                _interleave(lc_s, nat[g][1], d, tmp_s, False)

        def merge(ci, carry):
            rows = pl.ds(pl.multiple_of(ci * 256, 256), 256)
            l1, l2, l3 = l1_s[rows, :], l2_s[rows, :], l3_s[rows, :]
            m = jnp.maximum(jnp.maximum(l1, l2), l3)
            w1, w2, w3 = jnp.exp(l1 - m), jnp.exp(l2 - m), jnp.exp(l3 - m)
            w = w1 + w2 + w3
            y_ref[rows, :] = (w1 * o1_s[rows, :] + w2 * o2_s[rows, :] + w3 * o3_s[rows, :]) / w
            lse_ref[rows, :] = m + jnp.log(w)
            return carry

        lax.fori_loop(0, SEQ // 256, merge, 0)

    def col(off):
        return pl.BlockSpec((SEQ, LANES), lambda b, hp: (b, off + hp))

    tab = pl.BlockSpec((SEQ, LANES), lambda b, hp: (b, 0))
    out = pl.BlockSpec((SEQ, LANES), lambda b, hp: (b, hp))
    f32s = pltpu.VMEM((SEQ, LANES), F32)
    res, landed = call_hosting_exchange(
        body, xch, grid=(nb, n_pairs),
        in_specs=[col(0), col(n_pairs), col(2 * n_pairs), tab, tab, tab],
        out_specs=[out] * 11,
        out_shape=[SDS((t, A_WIDTH), F32)] * 2 + [SDS((t, A_WIDTH), BF16)] * 9,
        scratch_shapes=[f32s] * 11,
        name="a_attn_fwd", operands=(proj, proj, proj, ca, sa, sb))
    return res[:2], res[2:], landed


def a_attn_bwd(qkv_d, ca, sa, sb, dy, y, lse, nb, xch):
    t = dy.shape[0]
    n_pairs = A_WIDTH // LANES

    def body(*refs):
        qkv_refs = refs[:9]
        (c_ref, sa_ref, sb_ref, do_ref, y_ref, lse_ref, dq_ref, dk_ref, dv_ref,
         l0n_s, l1n_s, d0n_s, d1n_s, dod_s, l0d_s, l1d_s, d0d_s, d1d_s,
         dqc_s, dkc_s, dvc_s, dq4_s, dk4_s, dv4_s, dqn_s, dkn_s, dvn_s, tmp_s) = refs[9:]
        c, s_a, s_b = c_ref[...], sa_ref[...], sb_ref[...]
        head0 = lax.broadcasted_iota(jnp.int32, (Q_BLK, LANES), 1) < A_HEAD_DIM

        def per_head_rows(ci, carry):
            rows = pl.ds(pl.multiple_of(ci * 256, 256), 256)
            h0 = lax.broadcasted_iota(jnp.int32, (256, LANES), 1) < A_HEAD_DIM
            tt = do_ref[rows, :] * y_ref[rows, :]
            d0n_s[rows, :] = jnp.broadcast_to(jnp.sum(jnp.where(h0, tt, 0.0), axis=-1, keepdims=True), (256, LANES))
            d1n_s[rows, :] = jnp.broadcast_to(jnp.sum(jnp.where(h0, 0.0, tt), axis=-1, keepdims=True), (256, LANES))
            l = lse_ref[rows, :]
            lr = pltpu.roll(l, A_HEAD_DIM, 1)
            l0n_s[rows, :] = jnp.where(h0, l, lr)
            l1n_s[rows, :] = jnp.where(h0, lr, l)
            return carry

        lax.fori_loop(0, SEQ // 256, per_head_rows, 0)
        assert DILATIONS == (1, 4, 16)

        for g, d in enumerate(DILATIONS):
            qd_s, kd_s, vd_s = qkv_refs[3 * g:3 * g + 3]
            _deinterleave(do_ref, dod_s, d, BF16, tmp_s)
            if d > 1:
                for src, dst in ((l0n_s, l0d_s), (l1n_s, l1d_s), (d0n_s, d0d_s), (d1n_s, d1d_s)):
                    _deinterleave(src, dst, d, F32, tmp_s)
            l0, l1, d0, d1 = (l0n_s, l1n_s, d0n_s, d1n_s) if d == 1 else (l0d_s, l1d_s, d0d_s, d1d_s)
            dq_dst, dk_dst, dv_dst = {1: (dqn_s, dkn_s, dvn_s), 4: (dq4_s, dk4_s, dv4_s), 16: (dqc_s, dkc_s, dvc_s)}[d]
            dk_dst[...] = jnp.zeros_like(dk_dst)
            dv_dst[...] = jnp.zeros_like(dv_dst)

            def block(idx, carry, d=d, l0=l0, l1=l1, d0=d0, d1=d1, dq_dst=dq_dst, dk_dst=dk_dst, dv_dst=dv_dst,
                      qd_s=qd_s, kd_s=kd_s, vd_s=vd_s):
                q0, k0, kwin, valid = _block_coords(idx, d)
                qrows = pl.ds(q0, Q_BLK)
                krows = pl.ds(k0, kwin)
                qb, dob = qd_s[qrows, :], dod_s[qrows, :]
                kb, vb = kd_s[krows, :], vd_s[krows, :]
                zero = jnp.zeros_like(qb)
                q2 = jnp.concatenate([jnp.where(head0, qb, zero), jnp.where(head0, zero, qb)], 0)
                do2 = jnp.concatenate([jnp.where(head0, dob, zero), jnp.where(head0, zero, dob)], 0)
                wide = lambda x: jnp.concatenate([x] * (kwin // LANES), 1)
                lse2 = wide(jnp.concatenate([l0[qrows, :], l1[qrows, :]], 0))
                dd2 = wide(jnp.concatenate([d0[qrows, :], d1[qrows, :]], 0))
                s = jnp.where(jnp.concatenate([valid, valid], 0), _dot_nt(q2, kb), NEG_INF)
                p = jnp.exp(s - lse2)
                ds = (p * (_dot_nt(do2, vb) - dd2)).astype(BF16)
                dq2 = _dot(ds, kb)
                dq_dst[qrows, :] = jnp.where(head0, dq2[:Q_BLK], dq2[Q_BLK:])
                dk_dst[krows, :] += _dot_tn(ds, q2)
                dv_dst[krows, :] += _dot_tn(p.astype(BF16), do2)
                return carry

            lax.fori_loop(0, SEQ // Q_BLK, block, 0, unroll=UNROLL_BWD)

        for c16, c4, nat in ((dqc_s, dq4_s, dqn_s), (dkc_s, dk4_s, dkn_s), (dvc_s, dv4_s, dvn_s)):
            _class16_to_class4(c16, tmp_s)
            c4[...] = c4[...] + tmp_s[...]
            _interleave(c4, nat, 4, tmp_s, True)

        dq_ref[...] = _rope_bwd(dqn_s[...] * (A_HEAD_DIM ** -0.5), c, s_a, s_b, A_ROT // 2).astype(BF16)
        dk_ref[...] = _rope_bwd(dkn_s[...], c, s_a, s_b, A_ROT // 2).astype(BF16)
        dv_ref[...] = dvn_s[...].astype(BF16)

    tab = pl.BlockSpec((SEQ, LANES), lambda b, hp: (b, 0))
    blk = pl.BlockSpec((SEQ, LANES), lambda b, hp: (b, hp))
    f32s = pltpu.VMEM((SEQ, LANES), F32)
    b16s = pltpu.VMEM((SEQ, LANES), BF16)
    return call_hosting_exchange(
        body, xch, grid=(nb, n_pairs),
        in_specs=[blk] * 9 + [tab, tab, tab, blk, blk, blk],
        out_specs=[blk, blk, blk],
        out_shape=[SDS((t, A_WIDTH), BF16)] * 3,
        scratch_shapes=[f32s] * 4 + [b16s] + [f32s] * 14,
        name="a_attn_bwd", operands=(*qkv_d, ca, sa, sb, dy, y, lse))


MLA_SCALE = (MLA_NOPE + MLA_ROPE) ** -0.5
LOG2E = 1.4426950408889634
MLA_QW = MLA_HEADS * LANES
MLA_KVW = MLA_QW + MLA_WIDTH


def _rms(x, g):
    r = lax.rsqrt(jnp.mean(x * x, axis=-1, keepdims=True) + NORM_EPS)
    return x * r * g, r


def _rms_bwd(dn, x, r, g):
    tg = dn * g
    dx = r * tg - x * (r * r * r) * jnp.mean(tg * x, axis=-1, keepdims=True)
    return dx, jnp.sum(dn * x * r, axis=0, keepdims=True)


def mla_prep_fwd(proj, cm, sma, smb, g_cq, g_ckv, wuq, wkv):
    t = proj.shape[0]
    tm = 512

    def body(cq_ref, ckv_ref, kr_ref, c_ref, sa_ref, sb_ref, gq_ref, gkv_ref, wuq_ref, wkv_ref, q_ref, k_ref, v_ref):
        c, s_a, s_b = c_ref[...], sa_ref[...], sb_ref[...]
        cqn, _ = _rms(cq_ref[...], gq_ref[...])
        qf = _dot(cqn.astype(BF16), wuq_ref[...])
        ckvn, _ = _rms(ckv_ref[...], gkv_ref[...])
        kvf = _dot(ckvn.astype(BF16), wkv_ref[...])
        krope = _rope_fwd(kr_ref[...], c, s_a, s_b, MLA_ROPE // 2)
        for h in range(MLA_HEADS):
            cols = slice(h * LANES, (h + 1) * LANES)
            q_ref[:, cols] = (_rope_fwd(qf[:, cols], c, s_a, s_b, MLA_ROPE // 2) * (MLA_SCALE * LOG2E)).astype(BF16)
            k_ref[:, cols] = (kvf[:, cols] + krope).astype(BF16)
        v_ref[...] = kvf[:, MLA_QW:].astype(BF16)

    def row(w, j):
        return pl.BlockSpec((tm, w), lambda i: (i, j))

    def full(a):
        return pl.BlockSpec(a.shape, lambda i: (0, 0))

    return pl.pallas_call(
        body, grid=(t // tm,),
        in_specs=[row(256, 4096 // 256), row(128, 4352 // 128), row(128, 4480 // 128), row(128, 0), row(128, 0), row(128, 0),
                  full(g_cq), full(g_ckv), full(wuq), full(wkv)],
        out_specs=[row(MLA_QW, 0), row(MLA_QW, 0), row(MLA_WIDTH, 0)],
        out_shape=[SDS((t, MLA_QW), BF16), SDS((t, MLA_QW), BF16), SDS((t, MLA_WIDTH), BF16)],
        name="mla_prep_fwd", compiler_params=_params("parallel"))(proj, proj, proj, cm, sma, smb, g_cq, g_ckv, wuq, wkv)


def mla_prep_bwd(proj, cm, sma, smb, g_cq, g_ckv, wuq, wkv, dq, dk, dv):
    t = proj.shape[0]
    tm = 512

    def body(cq_ref, ckv_ref, c_ref, sa_ref, sb_ref, gq_ref, gkv_ref, wuq_ref, wkv_ref, dq_ref, dk_ref, dv_ref,
             dcc_ref, dqf_ref, cqn_ref, dkvf_ref, ckvn_ref, dgq_ref, dgkv_ref):
        @pl.when(pl.program_id(0) == 0)
        def _():
            dgq_ref[...] = jnp.zeros_like(dgq_ref)
            dgkv_ref[...] = jnp.zeros_like(dgkv_ref)

        c, s_a, s_b = c_ref[...], sa_ref[...], sb_ref[...]
        cq, ckv = cq_ref[...], ckv_ref[...]
        cqn, rq = _rms(cq, gq_ref[...])
        ckvn, rkv = _rms(ckv, gkv_ref[...])
        cqn_ref[...] = cqn.astype(BF16)
        ckvn_ref[...] = ckvn.astype(BF16)
        lane = lax.broadcasted_iota(jnp.int32, (tm, LANES), 1)
        rope_lanes = (lane >= MLA_NOPE) & (lane < MLA_NOPE + MLA_ROPE)
        dkrope = jnp.zeros((tm, LANES), F32)
        for h in range(MLA_HEADS):
            cols = slice(h * LANES, (h + 1) * LANES)
            dqf_ref[:, cols] = _rope_bwd(dq_ref[:, cols] * MLA_SCALE, c, s_a, s_b, MLA_ROPE // 2).astype(BF16)
            dkh = dk_ref[:, cols] * (1.0 / LOG2E)
            dkvf_ref[:, cols] = dkh.astype(BF16)
            dkrope = dkrope + dkh
        dkvf_ref[:, MLA_QW:] = dv_ref[...].astype(BF16)
        dkr = _rope_bwd(jnp.where(rope_lanes, dkrope, 0.0), c, s_a, s_b, MLA_ROPE // 2)
        dcqn = _dot_nt(dqf_ref[...], wuq_ref[...])
        dckvn = _dot_nt(dkvf_ref[...], wkv_ref[...])
        dcq, dgq = _rms_bwd(dcqn, cq, rq, gq_ref[...])
        dckv, dgkv = _rms_bwd(dckvn, ckv, rkv, gkv_ref[...])
        dgq_ref[...] += dgq
        dgkv_ref[...] += dgkv
        dcc_ref[:, 0:256] = dcq.astype(BF16)
        dcc_ref[:, 256:384] = dckv.astype(BF16)
        dcc_ref[:, 384:512] = dkr.astype(BF16)

    def row(w, j):
        return pl.BlockSpec((tm, w), lambda i: (i, j))

    def full(a):
        return pl.BlockSpec(a.shape, lambda i: (0, 0))

    return pl.pallas_call(
        body, grid=(t // tm,),
        in_specs=[row(256, 4096 // 256), row(128, 4352 // 128), row(128, 0), row(128, 0), row(128, 0),
                  full(g_cq), full(g_ckv), full(wuq), full(wkv), row(MLA_QW, 0), row(MLA_QW, 0), row(MLA_WIDTH, 0)],
        out_specs=[row(512, 0), row(MLA_QW, 0), row(256, 0), row(MLA_KVW, 0), row(128, 0), full(g_cq), full(g_ckv)],
        out_shape=[SDS((t, 512), BF16), SDS((t, MLA_QW), BF16), SDS((t, 256), BF16), SDS((t, MLA_KVW), BF16),
                   SDS((t, 128), BF16), SDS(g_cq.shape, F32), SDS(g_ckv.shape, F32)],
        name="mla_prep_bwd", compiler_params=_params("arbitrary"))(proj, proj, cm, sma, smb, g_cq, g_ckv, wuq, wkv, dq, dk, dv)


MLA_TQ = 256


def mla_attn_fwd(qb, kb, vb, nb):
    t = qb.shape[0]
    nq = SEQ // MLA_TQ
    n_pairs = MLA_HEADS // 2

    def body(q_ref, k_ref, v_ref, y_ref, lse_ref):
        head0 = lax.broadcasted_iota(jnp.int32, (MLA_TQ, LANES), 1) < MLA_V
        v = v_ref[...]
        vhead0 = lax.broadcasted_iota(jnp.int32, v.shape, 1) < MLA_V
        one = jnp.ones_like(v)
        outs, lses = [], []
        for h in range(2):
            cols = slice(h * LANES, (h + 1) * LANES)
            s = _dot_nt(q_ref[:, cols], k_ref[:, cols])
            m = jnp.max(s, axis=-1, keepdims=True)
            p = jnp.exp2(s - m).astype(BF16)
            ol = _dot(p, jnp.where(vhead0 == (h == 0), v, one))
            l = pltpu.roll(ol, MLA_V, 1)
            outs.append(ol / l)
            lses.append(m + jnp.log2(l))
        y_ref[...] = jnp.where(head0, outs[0], outs[1])
        lse_ref[...] = jnp.where(head0, lses[0], lses[1])

    return pl.pallas_call(
        body, grid=(nb, n_pairs, nq),
        in_specs=[pl.BlockSpec((MLA_TQ, 2 * LANES), lambda b, hp, i: (b * nq + i, hp)),
                  pl.BlockSpec((SEQ, 2 * LANES), lambda b, hp, i: (b, hp)),
                  pl.BlockSpec((SEQ, LANES), lambda b, hp, i: (b, hp))],
        out_specs=[pl.BlockSpec((MLA_TQ, LANES), lambda b, hp, i: (b * nq + i, hp))] * 2,
        out_shape=[SDS((t, MLA_WIDTH), F32)] * 2,
        name="mla_attn_fwd", compiler_params=_params("parallel", "parallel", "parallel"))(qb, kb, vb)


def mla_attn_bwd(qb, kb, vb, dy, y, lse, nb, xch):
    t = qb.shape[0]
    nq = SEQ // MLA_TQ
    n_pairs = MLA_HEADS // 2

    def body(q_ref, k_ref, v_ref, do_ref, y_ref, lse_ref, dq_ref, dk_ref, dv_ref):
        @pl.when(pl.program_id(2) == 0)
        def _():
            dk_ref[...] = jnp.zeros_like(dk_ref)
            dv_ref[...] = jnp.zeros_like(dv_ref)

        head0 = lax.broadcasted_iota(jnp.int32, (MLA_TQ, LANES), 1) < MLA_V
        v = v_ref[...]
        do = do_ref[...]
        lse = lse_ref[...]
        tt = do * y_ref[...]
        dv = jnp.zeros((SEQ, LANES), F32)
        for h in range(2):
            sel = head0 if h == 0 else ~head0
            lo = h * MLA_V
            cols = slice(h * LANES, (h + 1) * LANES)
            q = q_ref[:, cols]
            k = k_ref[:, cols]
            dd = jnp.sum(jnp.where(sel, tt, 0.0), axis=-1, keepdims=True)
            doh = jnp.where(sel, do, 0.0).astype(BF16)
            p = jnp.exp2(_dot_nt(q, k) - lse[:, lo:lo + 1])
            dp = _dot_nt(doh, v)
            ds = (p * (dp - dd)).astype(BF16)
            dq_ref[:, cols] = _dot(ds, k)
            dk_ref[:, cols] += _dot_tn(ds, q)
            dv = dv + _dot_tn(p.astype(BF16), doh)
        dv_ref[...] += dv

    qspec = pl.BlockSpec((MLA_TQ, 2 * LANES), lambda b, hp, i: (b * nq + i, hp))
    kspec = pl.BlockSpec((SEQ, 2 * LANES), lambda b, hp, i: (b, hp))
    vspec = pl.BlockSpec((SEQ, LANES), lambda b, hp, i: (b, hp))
    ospec = pl.BlockSpec((MLA_TQ, LANES), lambda b, hp, i: (b * nq + i, hp))
    return call_hosting_exchange(
        body, xch, grid=(nb, n_pairs, nq),
        in_specs=[qspec, kspec, vspec, ospec, ospec, ospec],
        out_specs=[qspec, kspec, vspec],
        out_shape=[SDS((t, MLA_QW), F32), SDS((t, MLA_QW), F32), SDS((t, MLA_WIDTH), F32)],
        scratch_shapes=[], name="mla_attn_bwd", operands=(qb, kb, vb, dy, y, lse))


MEM_TQ = 512
MEM_SCALE = MEM_HEAD_DIM ** -0.5
MQ_BLK4 = 5120 // MEM_WIDTH


def mem_attn_fwd(proj, mkv, nb):
    t = proj.shape[0]
    nq = SEQ // MEM_TQ

    def body(q_ref, mk_ref, mv_ref, y_ref):
        for h in range(MEM_HEADS):
            cols = slice(h * LANES, (h + 1) * LANES)
            s = _dot_nt(q_ref[:, cols].astype(BF16), mk_ref[:, cols]) * MEM_SCALE
            m = jnp.max(s, axis=-1, keepdims=True)
            p = jnp.exp(s - m)
            l = jnp.sum(p, axis=-1, keepdims=True)
            y_ref[:, cols] = _dot(p.astype(BF16), mv_ref[:, cols]) / l

    return pl.pallas_call(
        body, grid=(nb, nq),
        in_specs=[pl.BlockSpec((MEM_TQ, MEM_WIDTH), lambda b, i: (b * nq + i, MQ_BLK4)),
                  pl.BlockSpec((N_MEM, MEM_WIDTH), lambda b, i: (b, 0)),
                  pl.BlockSpec((N_MEM, MEM_WIDTH), lambda b, i: (b, 1))],
        out_specs=pl.BlockSpec((MEM_TQ, MEM_WIDTH), lambda b, i: (b * nq + i, 0)),
        out_shape=SDS((t, MEM_WIDTH), F32),
        name="mem_attn_fwd", compiler_params=_params("parallel", "parallel"))(proj, mkv, mkv)


def mem_attn_bwd(proj, mkv, dy, nb):
    t = proj.shape[0]
    nq = SEQ // MEM_TQ

    def body(q_ref, mk_ref, mv_ref, do_ref, dq_ref, dmk_ref, dmv_ref):
        @pl.when(pl.program_id(1) == 0)
        def _():
            dmk_ref[...] = jnp.zeros_like(dmk_ref)
            dmv_ref[...] = jnp.zeros_like(dmv_ref)

        for h in range(MEM_HEADS):
            cols = slice(h * LANES, (h + 1) * LANES)
            q = q_ref[:, cols].astype(BF16)
            mk, mv = mk_ref[:, cols], mv_ref[:, cols]
            do = do_ref[:, cols].astype(BF16)
            s = _dot_nt(q, mk) * MEM_SCALE
            e = jnp.exp(s - jnp.max(s, axis=-1, keepdims=True))
            p = e / jnp.sum(e, axis=-1, keepdims=True)
            dp = _dot_nt(do, mv)
            ds = (p * (dp - jnp.sum(p * dp, axis=-1, keepdims=True)) * MEM_SCALE).astype(BF16)
            dq_ref[:, cols] = _dot(ds, mk).astype(BF16)
            dmk_ref[:, cols] += _dot_tn(ds, q)
            dmv_ref[:, cols] += _dot_tn(p.astype(BF16), do)

    ospec = pl.BlockSpec((MEM_TQ, MEM_WIDTH), lambda b, i: (b * nq + i, 0))
    kspec = pl.BlockSpec((N_MEM, MEM_WIDTH), lambda b, i: (b, 0))
    return pl.pallas_call(
        body, grid=(nb, nq),
        in_specs=[pl.BlockSpec((MEM_TQ, MEM_WIDTH), lambda b, i: (b * nq + i, MQ_BLK4)),
                  kspec, pl.BlockSpec((N_MEM, MEM_WIDTH), lambda b, i: (b, 1)), ospec],
        out_specs=[ospec, kspec, kspec],
        out_shape=[SDS((t, MEM_WIDTH), BF16), SDS((nb * N_MEM, MEM_WIDTH), F32), SDS((nb * N_MEM, MEM_WIDTH), F32)],
        name="mem_attn_bwd", compiler_params=_params("parallel", "arbitrary"))(proj, mkv, mkv, dy)


ROW_TM = 512
AG_BLK = 3072 // 1024
BG_BLK = 4608 // 512
MG_BLK = 5632 // 512
GROUPS = ((0, A_WIDTH), (A_WIDTH, MLA_WIDTH), (A_WIDTH + MLA_WIDTH, MEM_WIDTH))
D_MIX = 2048


def _gate_specs():
    def row(w, j):
        return pl.BlockSpec((ROW_TM, w), lambda i: (i, j))

    def vec(w):
        return pl.BlockSpec((1, w), lambda i: (0, 0))

    ys = [row(A_WIDTH, 0), row(MLA_WIDTH, 0), row(MEM_WIDTH, 0)]
    gates = [row(A_WIDTH, AG_BLK), row(MLA_WIDTH, BG_BLK), row(MEM_WIDTH, MG_BLK)]
    gains = [vec(A_WIDTH), vec(MLA_WIDTH), vec(MEM_WIDTH)]
    return row, vec, ys, gates, gains


def gate_out_ln_loss(ya, yb, ym, proj, goa, gob, gom, wout, h32, target, gp, bp):
    t, d = h32.shape
    _, _, ys, gates, gains = _gate_specs()

    def body(ya_ref, yb_ref, ym_ref, ga_ref, gb_ref, gm_ref, goa_ref, gob_ref, gom_ref, w_ref, h_ref, t_ref, gp_ref, bp_ref,
             z_ref, du32_ref, du16_ref, loss_ref, dgp_ref, dbp_ref):
        @pl.when(pl.program_id(0) == 0)
        def _():
            loss_ref[...] = jnp.zeros_like(loss_ref)
            dgp_ref[...] = jnp.zeros_like(dgp_ref)
            dbp_ref[...] = jnp.zeros_like(dbp_ref)

        for (off, w), y_ref, g_ref, go_ref in zip(GROUPS, (ya_ref, yb_ref, ym_ref), (ga_ref, gb_ref, gm_ref),
                                                  (goa_ref, gob_ref, gom_ref)):
            n, _ = _rms(y_ref[...], go_ref[...])
            gt = g_ref[...]
            z_ref[:, off:off + w] = (n * (gt * _sigmoid(gt))).astype(BF16)
        g = gp_ref[...]
        u = ALPHA * h_ref[...] + _dot(z_ref[...], w_ref[...])
        mu = jnp.mean(u, axis=-1, keepdims=True)
        uc = u - mu
        rstd = lax.rsqrt(jnp.mean(uc * uc, axis=-1, keepdims=True) + NORM_EPS)
        xhat = uc * rstd
        err = xhat * g + bp_ref[...] - t_ref[...]
        tok = jnp.sum(err * err, axis=-1, keepdims=True) * (1.0 / d)
        loss_ref[...] += 0.5 * jnp.sum(tok, axis=0, keepdims=True)
        dout = err * (1.0 / d)
        dxhat = dout * g
        du = rstd * (dxhat - jnp.mean(dxhat, axis=-1, keepdims=True)
                     - xhat * jnp.mean(dxhat * xhat, axis=-1, keepdims=True))
        du32_ref[...] = du
        du16_ref[...] = du.astype(BF16)
        dgp_ref[...] += jnp.sum(dout * xhat, axis=0, keepdims=True)
        dbp_ref[...] += jnp.sum(dout, axis=0, keepdims=True)

    row = pl.BlockSpec((ROW_TM, d), lambda i: (i, 0))
    vec = pl.BlockSpec((1, d), lambda i: (0, 0))
    zrow = pl.BlockSpec((ROW_TM, D_MIX), lambda i: (i, 0))
    return pl.pallas_call(
        body, grid=(t // ROW_TM,),
        in_specs=ys + gates + gains + [pl.BlockSpec((D_MIX, d), lambda i: (0, 0)), row, row, vec, vec],
        out_specs=[zrow, row, row, pl.BlockSpec((1, LANES), lambda i: (0, 0)), vec, vec],
        out_shape=[SDS((t, D_MIX), BF16), SDS((t, d), F32), SDS((t, d), BF16), SDS((1, LANES), F32), SDS((1, d), F32),
                   SDS((1, d), F32)],
        name="gate_out_ln_loss", compiler_params=_params("arbitrary"))(
            ya, yb, ym, proj, proj, proj, goa, gob, gom, wout, h32, target, gp, bp)


def gate_bwd(du16, wout, ya, yb, ym, proj, goa, gob, gom):
    t = ya.shape[0]
    row, vec, ys, gates, gains = _gate_specs()

    def body(du_ref, w_ref, ya_ref, yb_ref, ym_ref, ga_ref, gb_ref, gm_ref, goa_ref, gob_ref, gom_ref,
             dya_ref, dyb_ref, dym_ref, dga_ref, dgb_ref, dgm_ref, dgoa_ref, dgob_ref, dgom_ref):
        @pl.when(pl.program_id(0) == 0)
        def _():
            dgoa_ref[...] = jnp.zeros_like(dgoa_ref)
            dgob_ref[...] = jnp.zeros_like(dgob_ref)
            dgom_ref[...] = jnp.zeros_like(dgom_ref)

        dz = _dot_nt(du_ref[...], w_ref[...])
        for (off, w), y_ref, g_ref, go_ref, dy_ref, dg_ref, dgo_ref in zip(
                GROUPS, (ya_ref, yb_ref, ym_ref), (ga_ref, gb_ref, gm_ref), (goa_ref, gob_ref, gom_ref),
                (dya_ref, dyb_ref, dym_ref), (dga_ref, dgb_ref, dgm_ref), (dgoa_ref, dgob_ref, dgom_ref)):
            dzg = dz[:, off:off + w]
            y, gt, go = y_ref[...], g_ref[...], go_ref[...]
            n, r = _rms(y, go)
            sg = _sigmoid(gt)
            dg_ref[...] = (dzg * n * (sg * (1.0 + gt * (1.0 - sg)))).astype(BF16)
            dy, dgo = _rms_bwd(dzg * (gt * sg), y, r, go)
            dy_ref[...] = dy
            dgo_ref[...] += dgo

    widths = (A_WIDTH, MLA_WIDTH, MEM_WIDTH)
    return pl.pallas_call(
        body, grid=(t // ROW_TM,),
        in_specs=[row(D_MODEL, 0), pl.BlockSpec((D_MIX, D_MODEL), lambda i: (0, 0))] + ys + gates + gains,
        out_specs=[row(w, 0) for w in widths] * 2 + [vec(w) for w in widths],
        out_shape=[SDS((t, w), F32) for w in widths] + [SDS((t, w), BF16) for w in widths] + [SDS((1, w), F32) for w in widths],
        name="gate_bwd", compiler_params=_params("arbitrary"))(du16, wout, ya, yb, ym, proj, proj, proj, goa, gob, gom)


def dh_ln_bwd(pieces, win_t, du32, x2, g_emb, xch):
    t, d = x2.shape

    def body(*refs):
        p_refs = refs[:len(pieces)]
        w_ref, du_ref, x_ref, g_ref, dx_ref, dg_ref, db_ref = refs[len(pieces):]

        @pl.when(pl.program_id(0) == 0)
        def _():
            dg_ref[...] = jnp.zeros_like(dg_ref)
            db_ref[...] = jnp.zeros_like(db_ref)

        dh = ALPHA * du_ref[...]
        for p_ref, off, w in zip(p_refs, PIECE_OFFS, PIECE_WIDTHS):
            dh = dh + _dot(p_ref[...], w_ref[off:off + w, :])
        x = x_ref[...]
        xc = x - jnp.mean(x, axis=-1, keepdims=True)
        rstd = lax.rsqrt(jnp.mean(xc * xc, axis=-1, keepdims=True) + NORM_EPS)
        xhat = xc * rstd
        dg_ref[...] += jnp.sum(dh * xhat, axis=0, keepdims=True)
        db_ref[...] += jnp.sum(dh, axis=0, keepdims=True)
        tg = dh * g_ref[...]
        dx_ref[...] = rstd * (tg - jnp.mean(tg, axis=-1, keepdims=True)
                              - xhat * jnp.mean(tg * xhat, axis=-1, keepdims=True))

    row = pl.BlockSpec((ROW_TM, d), lambda i: (i, 0))
    vec = pl.BlockSpec((1, d), lambda i: (0, 0))
    return call_hosting_exchange(
        body, xch, grid=(t // ROW_TM,),
        in_specs=[pl.BlockSpec((ROW_TM, w), lambda i: (i, 0)) for w in PIECE_WIDTHS]
        + [pl.BlockSpec(win_t.shape, lambda i: (0, 0)), row, row, vec],
        out_specs=[row, vec, vec],
        out_shape=[SDS((t, d), F32), SDS((1, d), F32), SDS((1, d), F32)],
        scratch_shapes=[], name="dh_ln_bwd", operands=(*pieces, win_t, du32, x2, g_emb))


def _adamw(w, g, m, v):
    m2 = ADAM_B1 * m + (1.0 - ADAM_B1) * g
    v2 = ADAM_B2 * v + (1.0 - ADAM_B2) * (g * g)
    m_hat = m2 / (1.0 - ADAM_B1 ** ADAM_STEP)
    v_hat = v2 / (1.0 - ADAM_B2 ** ADAM_STEP)
    return -ADAM_LR * (m_hat / (jnp.sqrt(v_hat) + ADAM_EPS) + ADAM_WD * w), m2, v2


def adamw_shard(w, parts, m, v, name):
    r, c = w.shape
    if r % 256 == 0 or r * c <= 256 * 1024:
        tr, tc = min(r, 256), c
    else:
        tr, tc = r, 256

    def body(w_ref, p_ref, m_ref, v_ref, g_ref, d_ref, nm_ref, nv_ref):
        g = p_ref[0].astype(F32)
        for k in range(1, N_DEV):
            g = g + p_ref[k].astype(F32)
        g_ref[...] = g
        d_ref[...], nm_ref[...], nv_ref[...] = _adamw(w_ref[...], g, m_ref[...], v_ref[...])

    blk = pl.BlockSpec((tr, tc), lambda i, j: (i, j))
    return pl.pallas_call(
        body, grid=(r // tr, c // tc),
        in_specs=[blk, pl.BlockSpec((N_DEV, tr, tc), lambda i, j: (0, i, j)), blk, blk],
        out_specs=[blk] * 4, out_shape=[SDS((r, c), F32)] * 4, name=name,
        compiler_params=_params("parallel", "parallel"))(w, parts, m, v)


def _place():
    return lax.axis_index("x"), lax.axis_index("y"), lax.axis_index("c")


def _flat(px, py, pc):
    return 4 * px + 2 * py + pc


def _peer(x, y, c, k):
    return (1 - x if k & 4 else x, 1 - y if k & 2 else y, 1 - c if k & 1 else c)


def cast_shards(shards):
    def body(*refs):
        n = len(refs) // 2
        for i_ref, o_ref in zip(refs[:n], refs[n:]):
            o_ref[...] = i_ref[...].astype(BF16)

    return pl.pallas_call(body, out_shape=[SDS(s.shape, BF16) for s in shards], name="cast_shards",
                          compiler_params=_params())(*shards)


def allgather_weights(shards):
    n = len(shards)

    def body(*refs):
        ins, outs = refs[:n], refs[n:2 * n]
        send_sems, recv_sems, local_sems = refs[2 * n:]
        x, y, c = _place()
        me, sib = (x, y, c), (x, y, 1 - c)
        chips = [(1 - x, y), (x, 1 - y), (1 - x, 1 - y)]

        def copy(a, k, block, to, src=None):
            dst = outs[a].at[_flat(*block)]
            return pltpu.make_async_remote_copy(
                src_ref=dst if src is None else src, dst_ref=dst,
                send_sem=send_sems.at[a * 7 + k], recv_sem=recv_sems.at[a * 7 + k],
                device_id=to, device_id_type=MESH)

        mine = [pltpu.make_async_copy(ins[a], outs[a].at[_flat(*me)], local_sems.at[a]) for a in range(n)]
        for cp in mine:
            cp.start()
        first = []
        for a in range(n):
            first.append(copy(a, 0, me, sib, src=ins[a]))
            first += [copy(a, 1 + j, me, (*chip, c), src=ins[a]) for j, chip in enumerate(chips)]
        for cp in first:
            cp.start()
        passed = []
        for j, chip in enumerate(chips):
            for a in range(n):
                copy(a, 1 + j, (*chip, c), me).wait_recv()
                fwd = copy(a, 4 + j, (*chip, c), sib)
                fwd.start()
                passed.append(fwd)
        for a in range(n):
            copy(a, 0, sib, me).wait_recv()
            for j, chip in enumerate(chips):
                copy(a, 4 + j, (*chip, 1 - c), me).wait_recv()
        for cp in first + passed:
            cp.wait_send()
        for cp in mine:
            cp.wait()

    hbm = pl.BlockSpec(memory_space=pl.ANY)
    return pl.pallas_call(
        body, out_shape=[SDS((N_DEV,) + s.shape, s.dtype) for s in shards],
        in_specs=[hbm] * n, out_specs=[hbm] * n,
        scratch_shapes=[pltpu.SemaphoreType.DMA((7 * n,)), pltpu.SemaphoreType.DMA((7 * n,)), pltpu.SemaphoreType.DMA((n,))],
        name="allgather_weights", compiler_params=_params())(*shards)


ALL_DEVICES = tuple(range(N_DEV))


def _exchange_plan(src_refs, land_refs, dests, send_sems, recv_sems, local_sems):
    x, y, c = _place()
    me = _flat(x, y, c)
    plan = []
    for a, (src, land, dl) in enumerate(zip(src_refs, land_refs, dests)):
        for li, j in enumerate(dl):
            to = ((j >> 2) & 1, (j >> 1) & 1, j & 1)
            block = src.at[li] if len(src.shape) == len(land.shape) else src

            def push(slot, a=a, block=block, land=land, j=j, to=to):
                return pltpu.make_async_remote_copy(
                    src_ref=block, dst_ref=land.at[slot], send_sem=send_sems.at[a * N_DEV + j],
                    recv_sem=recv_sems.at[a * N_DEV + slot], device_id=to, device_id_type=MESH)

            own = pltpu.make_async_copy(block, land.at[j], local_sems.at[a])
            plan.append((j, push(me), own, [push(s) for s in range(N_DEV) if s != j]))
    return me, plan


def _exchange_start(me, plan):
    for j, send, own, _ in plan:
        @pl.when(me != j)
        def _(send=send):
            send.start()

        @pl.when(me == j)
        def _(own=own):
            own.start()


def _exchange_wait(me, plan):
    for j, send, own, arrivals in plan:
        @pl.when(me != j)
        def _(send=send):
            send.wait_send()

        @pl.when(me == j)
        def _(own=own, arrivals=arrivals):
            own.wait()
            for arrival in arrivals:
                arrival.wait_recv()


def call_hosting_exchange(core, xch, *, grid, in_specs, out_specs, out_shape, scratch_shapes, name, operands):
    srcs, dests, landing = xch
    n, n_in, n_out, n_scr = len(srcs), len(in_specs), len(out_specs), len(scratch_shapes)

    def body(*refs):
        ins, src_refs = refs[:n_in], refs[n_in:n_in + n]
        outs = refs[n_in + 2 * n:n_in + 2 * n + n_out]
        land_refs = refs[n_in + 2 * n + n_out:n_in + 3 * n + n_out]
        scratch = refs[n_in + 3 * n + n_out:n_in + 3 * n + n_out + n_scr]
        sems = refs[n_in + 3 * n + n_out + n_scr:]
        first = functools.reduce(jnp.logical_and, [pl.program_id(i) == 0 for i in range(len(grid))])
        last = functools.reduce(jnp.logical_and, [pl.program_id(i) == grid[i] - 1 for i in range(len(grid))])
        me, plan = _exchange_plan(src_refs, land_refs, dests, *sems)

        @pl.when(first)
        def _():
            _exchange_start(me, plan)

        core(*ins, *outs, *scratch)

        @pl.when(last)
        def _():
            _exchange_wait(me, plan)

    hbm = pl.BlockSpec(memory_space=pl.ANY)
    res = pl.pallas_call(
        body, grid=grid,
        in_specs=list(in_specs) + [hbm] * (2 * n), out_specs=list(out_specs) + [hbm] * n,
        out_shape=list(out_shape) + [SDS(l.shape, l.dtype) for l in landing],
        scratch_shapes=list(scratch_shapes) + [pltpu.SemaphoreType.DMA((N_DEV * n,)), pltpu.SemaphoreType.DMA((N_DEV * n,)),
                                               pltpu.SemaphoreType.DMA((n,))],
        input_output_aliases={n_in + n + k: n_out + k for k in range(n)},
        name=name, compiler_params=_params(*(("arbitrary",) * len(grid))))(*operands, *srcs, *landing)
    return res[:n_out], res[n_out:]


SLOT_ROWS = 8


def small_allreduce_adamw(loss_sum, grads, ws, ms, vs):
    n = len(grads)
    rows = [g.shape[0] for g in grads]
    total = SLOT_ROWS * (n + 1)

    def body(*refs):
        loss_ref, g_refs, w_refs = refs[0], refs[1:1 + n], refs[1 + n:1 + 2 * n]
        m_refs, v_refs = refs[1 + 2 * n:1 + 3 * n], refs[1 + 3 * n:1 + 4 * n]
        outs = refs[1 + 4 * n:2 + 8 * n]
        vec, gath, tot, send_sems, recv_sems = refs[2 + 8 * n:]
        x, y, c = _place()
        me = _flat(x, y, c)
        vec[...] = jnp.zeros_like(vec)
        vec[0:1, :] = loss_ref[...]
        for i in range(n):
            vec[SLOT_ROWS * (i + 1):SLOT_ROWS * (i + 1) + rows[i], :] = g_refs[i][...]
        gath[me] = vec[...]
        copies = []
        for k in range(1, N_DEV):
            peer = _peer(x, y, c, k)
            copies.append(pltpu.make_async_remote_copy(
                src_ref=vec, dst_ref=gath.at[me], send_sem=send_sems.at[k - 1], recv_sem=recv_sems.at[k - 1],
                device_id=peer, device_id_type=MESH))
        for cp in copies:
            cp.start()
        for cp in copies:
            cp.wait_recv()
        for cp in copies:
            cp.wait_send()
        g = gath[0]
        for j in range(1, N_DEV):
            g = g + gath[j]
        tot[...] = g
        outs[0][...] = tot[0:1, :]
        for i in range(n):
            gi = tot[SLOT_ROWS * (i + 1):SLOT_ROWS * (i + 1) + rows[i], :]
            outs[1 + i][...] = gi
            outs[1 + n + i][...], outs[1 + 2 * n + i][...], outs[1 + 3 * n + i][...] = _adamw(
                w_refs[i][...], gi, m_refs[i][...], v_refs[i][...])

    shapes = [SDS(g.shape, F32) for g in grads]
    return pl.pallas_call(
        body, out_shape=[SDS((1, LANES), F32)] + shapes * 4,
        scratch_shapes=[pltpu.VMEM((total, LANES), F32), pltpu.VMEM((N_DEV, total, LANES), F32), pltpu.VMEM((total, LANES), F32),
                        pltpu.SemaphoreType.DMA((7,)), pltpu.SemaphoreType.DMA((7,))],
        name="small_allreduce_adamw", compiler_params=_params())(loss_sum, *grads, *ws, *ms, *vs)


def _rope_lane_patterns():
    inv = lambda r: ROPE_THETA ** (-(jnp.arange(0, r, 2, dtype=F32) / r))
    z = lambda n: jnp.zeros((n,), F32)
    o = lambda n: jnp.ones((n,), F32)
    half, rest = A_ROT // 2, A_HEAD_DIM - A_ROT
    ia, im = inv(A_ROT), inv(MLA_ROPE)
    mh, tail = MLA_ROPE // 2, LANES - MLA_NOPE - MLA_ROPE
    rows = [jnp.tile(jnp.concatenate([ia, ia, z(rest)]), 2),
            jnp.tile(jnp.concatenate([o(half), z(half + rest)]), 2),
            jnp.tile(jnp.concatenate([z(half), o(half), z(rest)]), 2),
            jnp.concatenate([z(MLA_NOPE), im, im, z(tail)]),
            jnp.concatenate([z(MLA_NOPE), o(mh), z(mh + tail)]),
            jnp.concatenate([z(MLA_NOPE + mh), o(mh), z(tail)]),
            z(LANES), z(LANES)]
    return jnp.stack(rows)


def rope_tables(positions):
    pos = positions.astype(F32).reshape(-1, 1)
    t = pos.shape[0]
    tm = 512

    def body(pos_ref, pat_ref, *outs):
        p = pos_ref[...]
        for k in range(2):
            inv, first, second = pat_ref[3 * k:3 * k + 1, :], pat_ref[3 * k + 1:3 * k + 2, :], pat_ref[3 * k + 2:3 * k + 3, :]
            ang = p * inv
            sn = jnp.sin(ang)
            outs[3 * k][...] = jnp.where(first + second > 0.0, jnp.cos(ang), 1.0)
            outs[3 * k + 1][...] = -first * sn
            outs[3 * k + 2][...] = second * sn

    row = pl.BlockSpec((tm, LANES), lambda i: (i, 0))
    res = pl.pallas_call(
        body, grid=(t // tm,),
        in_specs=[pl.BlockSpec((tm, 1), lambda i: (i, 0)), pl.BlockSpec((8, LANES), lambda i: (0, 0))],
        out_specs=[row] * 6, out_shape=[SDS((t, LANES), F32)] * 6, name="rope_tables",
        compiler_params=_params("parallel"))(pos, _rope_lane_patterns())
    return tuple(res[:3]), tuple(res[3:])


KR_LO, KR_HI = 4480, 4512
W_IN_SHARD = D_IN // N_DEV
AG_SPLIT = 5 * W_IN_SHARD - 3 * A_WIDTH
BG_SPLIT = 6 * W_IN_SHARD - KR_HI


def w_in_working_t(g):
    pad_lo, pad_hi = MLA_NOPE, LANES - MLA_NOPE - MLA_ROPE
    spans = []
    for lo, hi, shift in ((0, KR_LO, 0), (KR_LO, KR_HI, pad_lo), (KR_HI, D_IN, pad_lo + pad_hi)):
        r = lo
        while r < hi:
            j = r // W_IN_SHARD
            n = min(hi, (j + 1) * W_IN_SHARD) - r
            spans.append((j, r - j * W_IN_SHARD, n, r + shift))
            r += n

    def body(g_ref, o_ref):
        o_ref[KR_LO:KR_LO + pad_lo, :] = jnp.zeros((pad_lo, D_MODEL), o_ref.dtype)
        o_ref[KR_HI + pad_lo:KR_HI + pad_lo + pad_hi, :] = jnp.zeros((pad_hi, D_MODEL), o_ref.dtype)
        for j, src, n, dst in spans:
            o_ref[dst:dst + n, :] = g_ref[j, src:src + n, :]

    return pl.pallas_call(body, out_shape=SDS((D_INW, D_MODEL), g.dtype), name="w_in_working_t", compiler_params=_params())(g)


def _shards(rows):
    return rows.reshape(-1, W_IN_SHARD, D_MODEL).astype(BF16)


def _w_in_shard_5(d_ag, d_cc, d_bg):
    kr = MLA_Q_RANK + MLA_KV_RANK + MLA_NOPE
    return _shards(jnp.concatenate([d_ag[AG_SPLIT:], d_cc[:MLA_Q_RANK + MLA_KV_RANK], d_cc[kr:kr + MLA_ROPE], d_bg[:BG_SPLIT]], 0))


def _w_in_shards_6_7(d_bg, d_mq, d_mg):
    return _shards(jnp.concatenate([d_bg[BG_SPLIT:], d_mq, d_mg], 0))


def _w_uq_working(g):
    w = jnp.pad(g.transpose(1, 0, 2), ((0, 0), (0, 0), (0, LANES - MLA_NOPE - MLA_ROPE)))
    return w.reshape(MLA_Q_RANK, MLA_QW)


def _w_uq_parts(dw):
    return dw.reshape(MLA_Q_RANK, MLA_HEADS, LANES)[:, :, :MLA_NOPE + MLA_ROPE].transpose(1, 0, 2)


def _w_ukv_working(g):
    wk = jnp.pad(g[:, :, :MLA_NOPE].transpose(1, 0, 2), ((0, 0), (0, 0), (0, LANES - MLA_NOPE)))
    wv = g[:, :, MLA_NOPE:].transpose(1, 0, 2)
    return jnp.concatenate([wk.reshape(MLA_KV_RANK, MLA_QW), wv.reshape(MLA_KV_RANK, MLA_WIDTH)], 1)


def _w_ukv_parts(dw):
    dk = dw[:, :MLA_QW].reshape(MLA_KV_RANK, MLA_HEADS, LANES)[:, :, :MLA_NOPE]
    dv = dw[:, MLA_QW:].reshape(MLA_KV_RANK, MLA_HEADS, MLA_V)
    return jnp.concatenate([dk, dv], -1).transpose(1, 0, 2)


SMALL_NAMES = ("g_emb", "b_emb", "g_cq", "g_ckv", "g_out_a", "g_out_b", "g_out_m", "g_post", "b_post")


def kernel(x, mem, positions, g_emb, b_emb, w_in, g_cq, g_ckv, w_uq, w_ukv, w_mem_kv, g_out_a, g_out_b, g_out_m, w_out, g_post, b_post, loss_target, m_g_emb, m_b_emb, m_w_in, m_g_cq, m_g_ckv, m_w_uq, m_w_ukv, m_w_mem_kv, m_g_out_a, m_g_out_b, m_g_out_m, m_w_out, m_g_post, m_b_post, v_g_emb, v_b_emb, v_w_in, v_g_cq, v_g_ckv, v_w_uq, v_w_ukv, v_w_mem_kv, v_g_out_a, v_g_out_b, v_g_out_m, v_w_out, v_g_post, v_b_post):
    nb = x.shape[0]
    t = nb * SEQ
    x2 = x.reshape(t, D_MODEL)
    tgt2 = loss_target.reshape(t, D_MODEL)
    mem2 = mem.reshape(nb * N_MEM, D_MODEL)
    g_emb2, b_emb2 = g_emb.reshape(1, -1), b_emb.reshape(1, -1)
    (a_c, a_sa, a_sb), (m_c, m_sa, m_sb) = rope_tables(positions)

    w_in_t, m_w_in_t, v_w_in_t = w_in[0].T, m_w_in[0].T, v_w_in[0].T
    s_in, s_uq, s_ukv, s_mem, s_out = cast_shards((w_in_t, w_uq[0], w_ukv[0], w_mem_kv[0], w_out[0]))
    (g_in,) = allgather_weights((s_in,))
    win_t = w_in_working_t(g_in)

    h32, h16 = ln_emb_fwd(x2, g_emb2, b_emb2)
    proj = mm_nn(h16, win_t, F32, 512, 1536, "proj", rhs_transposed=True)
    later = (s_uq, s_ukv, s_mem, s_out)
    (ya, lse_a), qkv_d, (g_uq, g_ukv, g_mem, g_out) = a_attn_fwd(
        proj, a_c, a_sa, a_sb, nb,
        (later, (ALL_DEVICES,) * len(later), tuple(lax.empty((N_DEV,) + w.shape, BF16) for w in later)))
    wuq_w = _w_uq_working(g_uq)
    wkv_w = _w_ukv_working(g_ukv)
    wmem = g_mem.reshape(D_MODEL, 2 * MEM_WIDTH)
    wout = g_out.reshape(D_MIX, D_MODEL)
    qb, kb, vb = mla_prep_fwd(proj, m_c, m_sa, m_sb, g_cq, g_ckv, wuq_w, wkv_w)
    yb, lse_b = mla_attn_fwd(qb, kb, vb, nb)
    mkv = mm_nn(mem2, wmem, BF16, nb * N_MEM, 512, "mem_kv")
    ym = mem_attn_fwd(proj, mkv, nb)
    z, du32, du16, loss_sum, dg_post, db_post = gate_out_ln_loss(
        ya, yb, ym, proj, g_out_a, g_out_b, g_out_m, wout, h32, tgt2, g_post, b_post)

    dya, dyb, dym, dag, dbg, dmg, dg_out_a, dg_out_b, dg_out_m = gate_bwd(
        du16, wout, ya, yb, ym, proj, g_out_a, g_out_b, g_out_m)
    dw_out = mm_tn(z, du16, 1024, "dw_out")
    dmq, dmk, dmv = mem_attn_bwd(proj, mkv, dym, nb)
    dw_mem = mm_tn(mem2, jnp.concatenate([dmk, dmv], 1), nb * N_MEM, "dw_mem")
    d_gates = mm_tn_group((dbg, dmq, dmg), h16, 2048, "dw_in_bg_mq_mg")
    d_bg, d_mq, d_mg = d_gates[:MLA_WIDTH], d_gates[MLA_WIDTH:2 * MLA_WIDTH], d_gates[2 * MLA_WIDTH:]
    landing = lambda w, dtype=F32: lax.empty((N_DEV,) + w.shape, dtype)
    big_w = (w_in_t, w_uq[0], w_ukv[0], w_mem_kv[0], w_out[0])
    (daq, dak, dav), (p_out, p_mem, p_in) = a_attn_bwd(
        qkv_d, a_c, a_sa, a_sb, dya, ya, lse_a, nb,
        ((dw_out.reshape(N_DEV, D_MIX // N_DEV, D_MODEL), dw_mem.reshape(N_DEV, D_MODEL // N_DEV, 2 * MEM_WIDTH),
          _w_in_shards_6_7(d_bg, d_mq, d_mg)),
         (ALL_DEVICES, ALL_DEVICES, (6, 7)),
         (landing(w_out[0]), landing(w_mem_kv[0]), landing(w_in_t, BF16))))
    d_a = mm_tn_group((daq, dak, dav, dag), h16, 2048, "dw_in_aq_ak_av_ag")
    d_ag = d_a[3 * A_WIDTH:]
    (dqb, dkb, dvb), (p_in,) = mla_attn_bwd(
        qb, kb, vb, dyb, yb, lse_b, nb, ((_shards(d_a[:5 * W_IN_SHARD]),), ((0, 1, 2, 3, 4),), (p_in,)))
    dcc, dqf, cqn, dkvf, ckvn, dg_cq, dg_ckv = mla_prep_bwd(proj, m_c, m_sa, m_sb, g_cq, g_ckv, wuq_w, wkv_w, dqb, dkb, dvb)
    dw_uq = mm_tn(cqn, dqf, 2048, "dw_uq")
    dw_ukv = mm_tn(ckvn, dkvf, 2048, "dw_ukv")
    d_cc = mm_tn(dcc, h16, 2048, "dw_in_cc")
    pieces = (daq, dak, dav, dag, dcc, dbg, dmq, dmg)
    (grad_x, dg_emb, db_emb), (p_in, p_uq, p_ukv) = dh_ln_bwd(
        pieces, win_t, du32, x2, g_emb2,
        ((_w_in_shard_5(d_ag, d_cc, d_bg), _w_uq_parts(dw_uq), _w_ukv_parts(dw_ukv)),
         ((5,), ALL_DEVICES, ALL_DEVICES),
         (p_in, landing(w_uq[0]), landing(w_ukv[0]))))

    parts = (p_in, p_uq, p_ukv, p_mem, p_out)
    big_m = (m_w_in_t, m_w_uq[0], m_w_ukv[0], m_w_mem_kv[0], m_w_out[0])
    big_v = (v_w_in_t, v_w_uq[0], v_w_ukv[0], v_w_mem_kv[0], v_w_out[0])
    big = {}
    for name, w, p, m, v in zip(("w_in", "w_uq", "w_ukv", "w_mem_kv", "w_out"), big_w, parts, big_m, big_v):
        res = adamw_shard(w, p, m, v, "adamw_" + name)
        big[name] = [(o.T if name == "w_in" else o)[None] for o in res]

    small_w = (g_emb, b_emb, g_cq, g_ckv, g_out_a, g_out_b, g_out_m, g_post, b_post)
    small_m = (m_g_emb, m_b_emb, m_g_cq, m_g_ckv, m_g_out_a, m_g_out_b, m_g_out_m, m_g_post, m_b_post)
    small_v = (v_g_emb, v_b_emb, v_g_cq, v_g_ckv, v_g_out_a, v_g_out_b, v_g_out_m, v_g_post, v_b_post)
    small_g = (dg_emb, db_emb, dg_cq, dg_ckv, dg_out_a, dg_out_b, dg_out_m, dg_post, db_post)
    rows128 = lambda vals: [v.reshape(-1, LANES) for v in vals]
    res = small_allreduce_adamw(loss_sum, rows128(small_g), rows128(small_w), rows128(small_m), rows128(small_v))
    loss = res[0][0, 0]
    n_small = len(small_w)
    sg, sd, sm, sv = [[r.reshape(w.shape) for r, w in zip(res[1 + k * n_small:1 + (k + 1) * n_small], small_w)]
                      for k in range(4)]

    order = ("g_emb", "b_emb", "w_in", "g_cq", "g_ckv", "w_uq", "w_ukv", "w_mem_kv", "g_out_a", "g_out_b", "g_out_m",
             "w_out", "g_post", "b_post")
    small_idx = {n: i for i, n in enumerate(SMALL_NAMES)}
    outs = [loss, grad_x.reshape(x.shape)]
    for kind in range(4):
        for name in order:
            outs.append(big[name][kind] if name in big else (sg, sd, sm, sv)[kind][small_idx[name]])
    return tuple(outs)
```

```python
import functools

import jax
import jax.numpy as jnp
from jax import lax
from jax.experimental import pallas as pl
from jax.experimental.pallas import tpu as pltpu

F32 = jnp.float32
BF16 = jnp.bfloat16
SDS = jax.ShapeDtypeStruct
MESH = pl.DeviceIdType.MESH

D_MODEL = 1024
SEQ = 2048
A_HEADS, A_HEAD_DIM, A_ROT = 16, 64, 16
A_WIDTH = 1024
DILATIONS = (1, 4, 16)
N_SIDE = 64
MLA_HEADS, MLA_Q_RANK, MLA_KV_RANK = 8, 256, 128
MLA_NOPE, MLA_ROPE, MLA_V = 64, 32, 64
MLA_WIDTH = 512
N_MEM, MEM_HEADS, MEM_HEAD_DIM, MEM_WIDTH = 256, 4, 128, 512
ROPE_THETA = 500000.0
NORM_EPS = 1e-5
NEG_INF = -1e30
ALPHA = 2.0 ** 0.25
D_IN = 6048
N_DEV = 8

ADAM_LR, ADAM_B1, ADAM_B2, ADAM_EPS, ADAM_WD, ADAM_STEP = 0.001, 0.9, 0.999, 1e-08, 0.01, 10

D_INW = 6144
PIECE_WIDTHS = (1024, 1024, 1024, 1024, 512, 512, 512, 512)
PIECE_OFFS = (0, 1024, 2048, 3072, 4096, 4608, 5120, 5632)
LANES = 128
VMEM_LIMIT = 56 * 1024 * 1024


def _params(*sem):
    kw = dict(vmem_limit_bytes=VMEM_LIMIT)
    if sem:
        kw["dimension_semantics"] = sem
    return pltpu.CompilerParams(**kw)


def _dot(a, b):
    return jnp.dot(a, b, preferred_element_type=F32)


def _dot_nt(a, b):
    return lax.dot_general(a, b, (((1,), (1,)), ((), ())), preferred_element_type=F32)


def _dot_tn(a, b):
    return lax.dot_general(a, b, (((0,), (0,)), ((), ())), preferred_element_type=F32)


def _sigmoid(x):
    return 1.0 / (1.0 + jnp.exp(-x))


def _rope_fwd(x, c, sa, sb, half):
    n = x.shape[-1]
    return x * c + pltpu.roll(x, n - half, 1) * sa + pltpu.roll(x, half, 1) * sb


def _rope_bwd(dy, c, sa, sb, half):
    n = dy.shape[-1]
    return dy * c + pltpu.roll(dy * sa, half, 1) + pltpu.roll(dy * sb, n - half, 1)


def mm_nn(a, b, out_dtype, tm, tn, name, rhs_transposed=False):
    m, k = a.shape
    n = b.shape[0] if rhs_transposed else b.shape[1]
    dot = _dot_nt if rhs_transposed else _dot

    def body(a_ref, b_ref, o_ref):
        o_ref[...] = dot(a_ref[...].astype(BF16), b_ref[...].astype(BF16)).astype(o_ref.dtype)

    b_spec = pl.BlockSpec((tn, k), lambda j, i: (j, 0)) if rhs_transposed else pl.BlockSpec((k, tn), lambda j, i: (0, j))
    return pl.pallas_call(
        body, grid=(n // tn, m // tm),
        in_specs=[pl.BlockSpec((tm, k), lambda j, i: (i, 0)), b_spec],
        out_specs=pl.BlockSpec((tm, tn), lambda j, i: (i, j)),
        out_shape=SDS((m, n), out_dtype), name=name,
        compiler_params=_params("parallel", "parallel"))(a, b)


def mm_tn(a, b, tt, name):
    t, m = a.shape
    n = b.shape[1]

    def body(a_ref, b_ref, o_ref):
        @pl.when(pl.program_id(0) == 0)
        def _():
            o_ref[...] = jnp.zeros_like(o_ref)

        o_ref[...] += _dot_tn(a_ref[...].astype(BF16), b_ref[...].astype(BF16))

    return pl.pallas_call(
        body, grid=(t // tt,),
        in_specs=[pl.BlockSpec((tt, m), lambda i: (i, 0)), pl.BlockSpec((tt, n), lambda i: (i, 0))],
        out_specs=pl.BlockSpec((m, n), lambda i: (0, 0)),
        out_shape=SDS((m, n), F32), name=name,
        compiler_params=_params("arbitrary"))(a, b)


def mm_tn_group(pieces, b, tt, name):
    n, (t, w), cols = len(pieces), pieces[0].shape, b.shape[1]
    nt = t // tt

    def body(*refs):
        p_refs, b_ref, o_ref = refs[:n], refs[n], refs[n + 1]

        @pl.when(pl.program_id(1) == 0)
        def _():
            o_ref[...] = jnp.zeros_like(o_ref)

        for k in range(n):
            @pl.when(pl.program_id(0) == k)
            def _(k=k):
                o_ref[...] += _dot_tn(p_refs[k][...], b_ref[...])

    def piece_spec(k):
        return pl.BlockSpec((tt, w), lambda p, i: (jnp.where(p < k, 0, jnp.where(p > k, nt - 1, i)), 0))

    return pl.pallas_call(
        body, grid=(n, nt),
        in_specs=[piece_spec(k) for k in range(n)] + [pl.BlockSpec((tt, cols), lambda p, i: (i, 0))],
        out_specs=pl.BlockSpec((w, cols), lambda p, i: (p, 0)),
        out_shape=SDS((n * w, cols), F32), name=name,
        compiler_params=_params("arbitrary", "arbitrary"))(*pieces, b)


def embed_fwd(x2, g, b, positions, xch):
    t, d = x2.shape
    tm = 512
    pos = positions.astype(F32).reshape(-1, 1)

    def body(x_ref, g_ref, b_ref, pos_ref, pat_ref, h32_ref, h16_ref, *tabs):
        x = x_ref[...]
        mu = jnp.mean(x, axis=-1, keepdims=True)
        xc = x - mu
        var = jnp.mean(xc * xc, axis=-1, keepdims=True)
        h = xc * lax.rsqrt(var + NORM_EPS) * g_ref[...] + b_ref[...]
        h32_ref[...] = h
        h16_ref[...] = h.astype(BF16)
        p = pos_ref[...]
        for k in range(2):
            inv, first, second = pat_ref[3 * k:3 * k + 1, :], pat_ref[3 * k + 1:3 * k + 2, :], pat_ref[3 * k + 2:3 * k + 3, :]
            ang = p * inv
            sn = jnp.sin(ang)
            tabs[3 * k][...] = jnp.where(first + second > 0.0, jnp.cos(ang), 1.0)
            tabs[3 * k + 1][...] = -first * sn
            tabs[3 * k + 2][...] = second * sn

    row = pl.BlockSpec((tm, d), lambda i: (i, 0))
    vec = pl.BlockSpec((1, d), lambda i: (0, 0))
    tab = pl.BlockSpec((tm, LANES), lambda i: (i, 0))
    res, landed = call_hosting_exchange(
        body, xch, grid=(t // tm,),
        in_specs=[row, vec, vec, pl.BlockSpec((tm, 1), lambda i: (i, 0)), pl.BlockSpec((8, LANES), lambda i: (0, 0))],
        out_specs=[row, row] + [tab] * 6,
        out_shape=[SDS((t, d), F32), SDS((t, d), BF16)] + [SDS((t, LANES), F32)] * 6,
        scratch_shapes=[], name="embed_fwd", operands=(x2, g, b, pos, _rope_lane_patterns()))
    return (res[0], res[1], tuple(res[2:5]), tuple(res[5:8])), landed


Q_BLK = 128
UNROLL_FWD = 8
UNROLL_BWD = 8


def _pattern_geometry(d):
    length = SEQ // d
    nblk = length // Q_BLK
    kwin = min(2 * Q_BLK, length)
    return length, nblk, kwin


def _block_coords(idx, d):
    length, nblk, kwin = _pattern_geometry(d)
    r = lax.shift_right_logical(idx, nblk.bit_length() - 1)
    i = idx & (nblk - 1)
    q0 = pl.multiple_of(r * length + i * Q_BLK, Q_BLK)
    ks = jnp.clip(i * Q_BLK - N_SIDE, 0, length - kwin)
    k0 = pl.multiple_of(r * length + ks, N_SIDE)
    qpos = i * Q_BLK + lax.broadcasted_iota(jnp.int32, (Q_BLK, kwin), 0)
    kpos = ks + lax.broadcasted_iota(jnp.int32, (Q_BLK, kwin), 1)
    valid = jnp.abs(kpos - qpos) <= N_SIDE
    return q0, k0, kwin, valid


def _deinterleave(src_ref, dst_ref, d, dtype, tmp_ref):
    if d == 1:
        dst_ref[...] = src_ref[...].astype(dtype)
        return
    q = SEQ // 4
    if d == 4:
        for r in range(4):
            dst_ref[r * q:(r + 1) * q, :] = src_ref[pl.ds(r, q, stride=4), :].astype(dtype)
        return
    assert d == 16
    n = SEQ // 16
    for r in range(4):
        tmp_ref[r * q:(r + 1) * q, :] = src_ref[pl.ds(r, q, stride=4), :]
    for r in range(4):
        for j in range(4):
            dst_ref[(r + 4 * j) * n:(r + 4 * j + 1) * n, :] = tmp_ref[pl.ds(r * q + j, n, stride=4), :].astype(dtype)


def _class16_to_class4(src_ref, dst_ref):
    q, n = SEQ // 4, SEQ // 16
    for r in range(4):
        for j in range(4):
            dst_ref[pl.ds(r * q + j, n, stride=4), :] = src_ref[(r + 4 * j) * n:(r + 4 * j + 1) * n, :]


def _interleave(src_ref, dst_ref, d, tmp_ref, accumulate):
    q = SEQ // 4
    if d == 16:
        _class16_to_class4(src_ref, tmp_ref)
        src_ref = tmp_ref
    else:
        assert d == 4
    for r in range(4):
        rows = pl.ds(r, q, stride=4)
        val = src_ref[r * q:(r + 1) * q, :]
        dst_ref[rows, :] = dst_ref[rows, :] + val if accumulate else val


def a_attn_fwd(proj, ca, sa, sb, nb, xch):
    t = proj.shape[0]
    n_pairs = A_WIDTH // LANES

    def body(q_ref, k_ref, v_ref, c_ref, sa_ref, sb_ref, y_ref, lse_ref, *rest):
        qkv_d, (qr_s, kr_s, oc_s, lc_s, o1_s, l1_s, o2_s, l2_s, o3_s, l3_s, tmp_s) = rest[:9], rest[9:]
        c, s_a, s_b = c_ref[...], sa_ref[...], sb_ref[...]
        qr_s[...] = _rope_fwd(q_ref[...], c, s_a, s_b, A_ROT // 2) * (A_HEAD_DIM ** -0.5)
        kr_s[...] = _rope_fwd(k_ref[...], c, s_a, s_b, A_ROT // 2)
        head0 = lax.broadcasted_iota(jnp.int32, (Q_BLK, LANES), 1) < A_HEAD_DIM
        nat = ((o1_s, l1_s), (o2_s, l2_s), (o3_s, l3_s))

        for g, d in enumerate(DILATIONS):
            qd_s, kd_s, vd_s = qkv_d[3 * g:3 * g + 3]
            _deinterleave(qr_s, qd_s, d, BF16, tmp_s)
            _deinterleave(kr_s, kd_s, d, BF16, tmp_s)
            _deinterleave(v_ref, vd_s, d, BF16, tmp_s)
            o_dst, l_dst = (nat[g] if d == 1 else (oc_s, lc_s))

            def block(idx, carry, d=d, o_dst=o_dst, l_dst=l_dst, qd_s=qd_s, kd_s=kd_s, vd_s=vd_s):
                q0, k0, kwin, valid = _block_coords(idx, d)
                qb = qd_s[pl.ds(q0, Q_BLK), :]
                kb = kd_s[pl.ds(k0, kwin), :]
                vb = vd_s[pl.ds(k0, kwin), :]
                zero = jnp.zeros_like(qb)
                q2 = jnp.concatenate([jnp.where(head0, qb, zero), jnp.where(head0, zero, qb)], 0)
                s = jnp.where(jnp.concatenate([valid, valid], 0), _dot_nt(q2, kb), NEG_INF)
                m = jnp.max(s, axis=-1, keepdims=True)
                p = jnp.exp(s - m)
                l = jnp.sum(p, axis=-1, keepdims=True)
                o2 = _dot(p.astype(BF16), vb) / l
                l2 = m + jnp.log(l)
                o_dst[pl.ds(q0, Q_BLK), :] = jnp.where(head0, o2[:Q_BLK], o2[Q_BLK:])
                l_dst[pl.ds(q0, Q_BLK), :] = jnp.where(head0, l2[:Q_BLK], l2[Q_BLK:])
                return carry

            lax.fori_loop(0, SEQ // Q_BLK, block, 0, unroll=UNROLL_FWD)
            if d > 1:
                _interleave(oc_s, nat[g][0], d, tmp_s, False)
                _interleave(lc_s, nat[g][1], d, tmp_s, False)

        def merge(ci, carry):
            rows = pl.ds(pl.multiple_of(ci * 256, 256), 256)
            l1, l2, l3 = l1_s[rows, :], l2_s[rows, :], l3_s[rows, :]
            m = jnp.maximum(jnp.maximum(l1, l2), l3)
            w1, w2, w3 = jnp.exp(l1 - m), jnp.exp(l2 - m), jnp.exp(l3 - m)
            w = w1 + w2 + w3
            y_ref[rows, :] = (w1 * o1_s[rows, :] + w2 * o2_s[rows, :] + w3 * o3_s[rows, :]) / w
            lse_ref[rows, :] = m + jnp.log(w)
            return carry

        lax.fori_loop(0, SEQ // 256, merge, 0)

    def col(off):
        return pl.BlockSpec((SEQ, LANES), lambda b, hp: (b, off + hp))

    tab = pl.BlockSpec((SEQ, LANES), lambda b, hp: (b, 0))
    out = pl.BlockSpec((SEQ, LANES), lambda b, hp: (b, hp))
    f32s = pltpu.VMEM((SEQ, LANES), F32)
    res, landed = call_hosting_exchange(
        body, xch, grid=(nb, n_pairs),
        in_specs=[col(0), col(n_pairs), col(2 * n_pairs), tab, tab, tab],
        out_specs=[out] * 11,
        out_shape=[SDS((t, A_WIDTH), F32)] * 2 + [SDS((t, A_WIDTH), BF16)] * 9,
        scratch_shapes=[f32s] * 11,
        name="a_attn_fwd", operands=(proj, proj, proj, ca, sa, sb))
    return res[:2], res[2:], landed


def a_attn_bwd(qkv_d, ca, sa, sb, dy, y, lse, nb, xch):
    t = dy.shape[0]
    n_pairs = A_WIDTH // LANES

    def body(*refs):
        qkv_refs = refs[:9]
        (c_ref, sa_ref, sb_ref, do_ref, y_ref, lse_ref, dq_ref, dk_ref, dv_ref,
         l0n_s, l1n_s, d0n_s, d1n_s, dod_s, l0d_s, l1d_s, d0d_s, d1d_s,
         dqc_s, dkc_s, dvc_s, dq4_s, dk4_s, dv4_s, dqn_s, dkn_s, dvn_s, tmp_s) = refs[9:]
        c, s_a, s_b = c_ref[...], sa_ref[...], sb_ref[...]
        head0 = lax.broadcasted_iota(jnp.int32, (Q_BLK, LANES), 1) < A_HEAD_DIM

        def per_head_rows(ci, carry):
            rows = pl.ds(pl.multiple_of(ci * 256, 256), 256)
            h0 = lax.broadcasted_iota(jnp.int32, (256, LANES), 1) < A_HEAD_DIM
            tt = do_ref[rows, :] * y_ref[rows, :]
            d0n_s[rows, :] = jnp.broadcast_to(jnp.sum(jnp.where(h0, tt, 0.0), axis=-1, keepdims=True), (256, LANES))
            d1n_s[rows, :] = jnp.broadcast_to(jnp.sum(jnp.where(h0, 0.0, tt), axis=-1, keepdims=True), (256, LANES))
            l = lse_ref[rows, :]
            lr = pltpu.roll(l, A_HEAD_DIM, 1)
            l0n_s[rows, :] = jnp.where(h0, l, lr)
            l1n_s[rows, :] = jnp.where(h0, lr, l)
            return carry

        lax.fori_loop(0, SEQ // 256, per_head_rows, 0)
        assert DILATIONS == (1, 4, 16)

        for g, d in enumerate(DILATIONS):
            qd_s, kd_s, vd_s = qkv_refs[3 * g:3 * g + 3]
            _deinterleave(do_ref, dod_s, d, BF16, tmp_s)
            if d > 1:
                for src, dst in ((l0n_s, l0d_s), (l1n_s, l1d_s), (d0n_s, d0d_s), (d1n_s, d1d_s)):
                    _deinterleave(src, dst, d, F32, tmp_s)
            l0, l1, d0, d1 = (l0n_s, l1n_s, d0n_s, d1n_s) if d == 1 else (l0d_s, l1d_s, d0d_s, d1d_s)
            dq_dst, dk_dst, dv_dst = {1: (dqn_s, dkn_s, dvn_s), 4: (dq4_s, dk4_s, dv4_s), 16: (dqc_s, dkc_s, dvc_s)}[d]
            dk_dst[...] = jnp.zeros_like(dk_dst)
            dv_dst[...] = jnp.zeros_like(dv_dst)

            def block(idx, carry, d=d, l0=l0, l1=l1, d0=d0, d1=d1, dq_dst=dq_dst, dk_dst=dk_dst, dv_dst=dv_dst,
                      qd_s=qd_s, kd_s=kd_s, vd_s=vd_s):
                q0, k0, kwin, valid = _block_coords(idx, d)
                qrows = pl.ds(q0, Q_BLK)
                krows = pl.ds(k0, kwin)
                qb, dob = qd_s[qrows, :], dod_s[qrows, :]
                kb, vb = kd_s[krows, :], vd_s[krows, :]
                zero = jnp.zeros_like(qb)
                q2 = jnp.concatenate([jnp.where(head0, qb, zero), jnp.where(head0, zero, qb)], 0)
                do2 = jnp.concatenate([jnp.where(head0, dob, zero), jnp.where(head0, zero, dob)], 0)
                wide = lambda x: jnp.concatenate([x] * (kwin // LANES), 1)
                lse2 = wide(jnp.concatenate([l0[qrows, :], l1[qrows, :]], 0))
                dd2 = wide(jnp.concatenate([d0[qrows, :], d1[qrows, :]], 0))
                s = jnp.where(jnp.concatenate([valid, valid], 0), _dot_nt(q2, kb), NEG_INF)
                p = jnp.exp(s - lse2)
                ds = (p * (_dot_nt(do2, vb) - dd2)).astype(BF16)
                dq2 = _dot(ds, kb)
                dq_dst[qrows, :] = jnp.where(head0, dq2[:Q_BLK], dq2[Q_BLK:])
                dk_dst[krows, :] += _dot_tn(ds, q2)
                dv_dst[krows, :] += _dot_tn(p.astype(BF16), do2)
                return carry

            lax.fori_loop(0, SEQ // Q_BLK, block, 0, unroll=UNROLL_BWD)

        for c16, c4, nat in ((dqc_s, dq4_s, dqn_s), (dkc_s, dk4_s, dkn_s), (dvc_s, dv4_s, dvn_s)):
            _class16_to_class4(c16, tmp_s)
            c4[...] = c4[...] + tmp_s[...]
            _interleave(c4, nat, 4, tmp_s, True)

        dq_ref[...] = _rope_bwd(dqn_s[...] * (A_HEAD_DIM ** -0.5), c, s_a, s_b, A_ROT // 2).astype(BF16)
        dk_ref[...] = _rope_bwd(dkn_s[...], c, s_a, s_b, A_ROT // 2).astype(BF16)
        dv_ref[...] = dvn_s[...].astype(BF16)

    tab = pl.BlockSpec((SEQ, LANES), lambda b, hp: (b, 0))
    blk = pl.BlockSpec((SEQ, LANES), lambda b, hp: (b, hp))
    f32s = pltpu.VMEM((SEQ, LANES), F32)
    b16s = pltpu.VMEM((SEQ, LANES), BF16)
    return call_hosting_exchange(
        body, xch, grid=(nb, n_pairs),
        in_specs=[blk] * 9 + [tab, tab, tab, blk, blk, blk],
        out_specs=[blk, blk, blk],
        out_shape=[SDS((t, A_WIDTH), BF16)] * 3,
        scratch_shapes=[f32s] * 4 + [b16s] + [f32s] * 14,
        name="a_attn_bwd", operands=(*qkv_d, ca, sa, sb, dy, y, lse))


MLA_SCALE = (MLA_NOPE + MLA_ROPE) ** -0.5
LOG2E = 1.4426950408889634
MLA_QW = MLA_HEADS * LANES
MLA_KVW = MLA_QW + MLA_WIDTH


def _rms(x, g):
    r = lax.rsqrt(jnp.mean(x * x, axis=-1, keepdims=True) + NORM_EPS)
    return x * r * g, r


def _rms_bwd(dn, x, r, g):
    tg = dn * g
    dx = r * tg - x * (r * r * r) * jnp.mean(tg * x, axis=-1, keepdims=True)
    return dx, jnp.sum(dn * x * r, axis=0, keepdims=True)


def mla_prep_fwd(proj, cm, sma, smb, g_cq, g_ckv, wuq, wkv):
    t = proj.shape[0]
    tm = 512

    def body(cq_ref, ckv_ref, kr_ref, c_ref, sa_ref, sb_ref, gq_ref, gkv_ref, wuq_ref, wkv_ref, q_ref, k_ref, v_ref):
        c, s_a, s_b = c_ref[...], sa_ref[...], sb_ref[...]
        cqn, _ = _rms(cq_ref[...], gq_ref[...])
        qf = _dot(cqn.astype(BF16), wuq_ref[...])
        ckvn, _ = _rms(ckv_ref[...], gkv_ref[...])
        kvf = _dot(ckvn.astype(BF16), wkv_ref[...])
        krope = _rope_fwd(kr_ref[...], c, s_a, s_b, MLA_ROPE // 2)
        for h in range(MLA_HEADS):
            cols = slice(h * LANES, (h + 1) * LANES)
            q_ref[:, cols] = (_rope_fwd(qf[:, cols], c, s_a, s_b, MLA_ROPE // 2) * (MLA_SCALE * LOG2E)).astype(BF16)
            k_ref[:, cols] = (kvf[:, cols] + krope).astype(BF16)
        v_ref[...] = kvf[:, MLA_QW:].astype(BF16)

    def row(w, j):
        return pl.BlockSpec((tm, w), lambda i: (i, j))

    def full(a):
        return pl.BlockSpec(a.shape, lambda i: (0, 0))

    return pl.pallas_call(
        body, grid=(t // tm,),
        in_specs=[row(256, 4096 // 256), row(128, 4352 // 128), row(128, 4480 // 128), row(128, 0), row(128, 0), row(128, 0),
                  full(g_cq), full(g_ckv), full(wuq), full(wkv)],
        out_specs=[row(MLA_QW, 0), row(MLA_QW, 0), row(MLA_WIDTH, 0)],
        out_shape=[SDS((t, MLA_QW), BF16), SDS((t, MLA_QW), BF16), SDS((t, MLA_WIDTH), BF16)],
        name="mla_prep_fwd", compiler_params=_params("parallel"))(proj, proj, proj, cm, sma, smb, g_cq, g_ckv, wuq, wkv)


def mla_prep_bwd(proj, cm, sma, smb, g_cq, g_ckv, wuq, wkv, dq, dk, dv):
    t = proj.shape[0]
    tm = 512

    def body(cq_ref, ckv_ref, c_ref, sa_ref, sb_ref, gq_ref, gkv_ref, wuq_ref, wkv_ref, dq_ref, dk_ref, dv_ref,
             dcc_ref, dqf_ref, cqn_ref, dkvf_ref, ckvn_ref, dgq_ref, dgkv_ref):
        @pl.when(pl.program_id(0) == 0)
        def _():
            dgq_ref[...] = jnp.zeros_like(dgq_ref)
            dgkv_ref[...] = jnp.zeros_like(dgkv_ref)

        c, s_a, s_b = c_ref[...], sa_ref[...], sb_ref[...]
        cq, ckv = cq_ref[...], ckv_ref[...]
        cqn, rq = _rms(cq, gq_ref[...])
        ckvn, rkv = _rms(ckv, gkv_ref[...])
        cqn_ref[...] = cqn.astype(BF16)
        ckvn_ref[...] = ckvn.astype(BF16)
        lane = lax.broadcasted_iota(jnp.int32, (tm, LANES), 1)
        rope_lanes = (lane >= MLA_NOPE) & (lane < MLA_NOPE + MLA_ROPE)
        dkrope = jnp.zeros((tm, LANES), F32)
        for h in range(MLA_HEADS):
            cols = slice(h * LANES, (h + 1) * LANES)
            dqf_ref[:, cols] = _rope_bwd(dq_ref[:, cols] * MLA_SCALE, c, s_a, s_b, MLA_ROPE // 2).astype(BF16)
            dkh = dk_ref[:, cols] * (1.0 / LOG2E)
            dkvf_ref[:, cols] = dkh.astype(BF16)
            dkrope = dkrope + dkh
        dkvf_ref[:, MLA_QW:] = dv_ref[...].astype(BF16)
        dkr = _rope_bwd(jnp.where(rope_lanes, dkrope, 0.0), c, s_a, s_b, MLA_ROPE // 2)
        dcqn = _dot_nt(dqf_ref[...], wuq_ref[...])
        dckvn = _dot_nt(dkvf_ref[...], wkv_ref[...])
        dcq, dgq = _rms_bwd(dcqn, cq, rq, gq_ref[...])
        dckv, dgkv = _rms_bwd(dckvn, ckv, rkv, gkv_ref[...])
        dgq_ref[...] += dgq
        dgkv_ref[...] += dgkv
        dcc_ref[:, 0:256] = dcq.astype(BF16)
        dcc_ref[:, 256:384] = dckv.astype(BF16)
        dcc_ref[:, 384:512] = dkr.astype(BF16)

    def row(w, j):
        return pl.BlockSpec((tm, w), lambda i: (i, j))

    def full(a):
        return pl.BlockSpec(a.shape, lambda i: (0, 0))

    return pl.pallas_call(
        body, grid=(t // tm,),
        in_specs=[row(256, 4096 // 256), row(128, 4352 // 128), row(128, 0), row(128, 0), row(128, 0),
                  full(g_cq), full(g_ckv), full(wuq), full(wkv), row(MLA_QW, 0), row(MLA_QW, 0), row(MLA_WIDTH, 0)],
        out_specs=[row(512, 0), row(MLA_QW, 0), row(256, 0), row(MLA_KVW, 0), row(128, 0), full(g_cq), full(g_ckv)],
        out_shape=[SDS((t, 512), BF16), SDS((t, MLA_QW), BF16), SDS((t, 256), BF16), SDS((t, MLA_KVW), BF16),
                   SDS((t, 128), BF16), SDS(g_cq.shape, F32), SDS(g_ckv.shape, F32)],
        name="mla_prep_bwd", compiler_params=_params("arbitrary"))(proj, proj, cm, sma, smb, g_cq, g_ckv, wuq, wkv, dq, dk, dv)


MLA_TQ = 256


def mla_attn_fwd(qb, kb, vb, nb):
    t = qb.shape[0]
    nq = SEQ // MLA_TQ
    n_pairs = MLA_HEADS // 2

    def body(q_ref, k_ref, v_ref, y_ref, lse_ref):
        head0 = lax.broadcasted_iota(jnp.int32, (MLA_TQ, LANES), 1) < MLA_V
        v = v_ref[...]
        vhead0 = lax.broadcasted_iota(jnp.int32, v.shape, 1) < MLA_V
        one = jnp.ones_like(v)
        outs, lses = [], []
        for h in range(2):
            cols = slice(h * LANES, (h + 1) * LANES)
            s = _dot_nt(q_ref[:, cols], k_ref[:, cols])
            m = jnp.max(s, axis=-1, keepdims=True)
            p = jnp.exp2(s - m).astype(BF16)
            ol = _dot(p, jnp.where(vhead0 == (h == 0), v, one))
            l = pltpu.roll(ol, MLA_V, 1)
            outs.append(ol / l)
            lses.append(m + jnp.log2(l))
        y_ref[...] = jnp.where(head0, outs[0], outs[1])
        lse_ref[...] = jnp.where(head0, lses[0], lses[1])

    return pl.pallas_call(
        body, grid=(nb, n_pairs, nq),
        in_specs=[pl.BlockSpec((MLA_TQ, 2 * LANES), lambda b, hp, i: (b * nq + i, hp)),
                  pl.BlockSpec((SEQ, 2 * LANES), lambda b, hp, i: (b, hp)),
                  pl.BlockSpec((SEQ, LANES), lambda b, hp, i: (b, hp))],
        out_specs=[pl.BlockSpec((MLA_TQ, LANES), lambda b, hp, i: (b * nq + i, hp))] * 2,
        out_shape=[SDS((t, MLA_WIDTH), F32)] * 2,
        name="mla_attn_fwd", compiler_params=_params("parallel", "parallel", "parallel"))(qb, kb, vb)


def mla_attn_bwd(qb, kb, vb, dy, y, lse, nb, xch):
    t = qb.shape[0]
    nq = SEQ // MLA_TQ
    n_pairs = MLA_HEADS // 2

    def body(q_ref, k_ref, v_ref, do_ref, y_ref, lse_ref, dq_ref, dk_ref, dv_ref):
        @pl.when(pl.program_id(2) == 0)
        def _():
            dk_ref[...] = jnp.zeros_like(dk_ref)
            dv_ref[...] = jnp.zeros_like(dv_ref)

        head0 = lax.broadcasted_iota(jnp.int32, (MLA_TQ, LANES), 1) < MLA_V
        v = v_ref[...]
        do = do_ref[...]
        lse = lse_ref[...]
        tt = do * y_ref[...]
        dv = jnp.zeros((SEQ, LANES), F32)
        for h in range(2):
            sel = head0 if h == 0 else ~head0
            lo = h * MLA_V
            cols = slice(h * LANES, (h + 1) * LANES)
            q = q_ref[:, cols]
            k = k_ref[:, cols]
            dd = jnp.sum(jnp.where(sel, tt, 0.0), axis=-1, keepdims=True)
            doh = jnp.where(sel, do, 0.0).astype(BF16)
            p = jnp.exp2(_dot_nt(q, k) - lse[:, lo:lo + 1])
            dp = _dot_nt(doh, v)
            ds = (p * (dp - dd)).astype(BF16)
            dq_ref[:, cols] = _dot(ds, k)
            dk_ref[:, cols] += _dot_tn(ds, q)
            dv = dv + _dot_tn(p.astype(BF16), doh)
        dv_ref[...] += dv

    qspec = pl.BlockSpec((MLA_TQ, 2 * LANES), lambda b, hp, i: (b * nq + i, hp))
    kspec = pl.BlockSpec((SEQ, 2 * LANES), lambda b, hp, i: (b, hp))
    vspec = pl.BlockSpec((SEQ, LANES), lambda b, hp, i: (b, hp))
    ospec = pl.BlockSpec((MLA_TQ, LANES), lambda b, hp, i: (b * nq + i, hp))
    return call_hosting_exchange(
        body, xch, grid=(nb, n_pairs, nq),
        in_specs=[qspec, kspec, vspec, ospec, ospec, ospec],
        out_specs=[qspec, kspec, vspec],
        out_shape=[SDS((t, MLA_QW), F32), SDS((t, MLA_QW), F32), SDS((t, MLA_WIDTH), F32)],
        scratch_shapes=[], name="mla_attn_bwd", operands=(qb, kb, vb, dy, y, lse))


MEM_TQ = 512
MEM_SCALE = MEM_HEAD_DIM ** -0.5
MQ_BLK4 = 5120 // MEM_WIDTH


def mem_attn_fwd(proj, mkv, nb):
    t = proj.shape[0]
    nq = SEQ // MEM_TQ

    def body(q_ref, mk_ref, mv_ref, y_ref):
        for h in range(MEM_HEADS):
            cols = slice(h * LANES, (h + 1) * LANES)
            s = _dot_nt(q_ref[:, cols].astype(BF16), mk_ref[:, cols]) * MEM_SCALE
            m = jnp.max(s, axis=-1, keepdims=True)
            p = jnp.exp(s - m)
            l = jnp.sum(p, axis=-1, keepdims=True)
            y_ref[:, cols] = _dot(p.astype(BF16), mv_ref[:, cols]) / l

    return pl.pallas_call(
        body, grid=(nb, nq),
        in_specs=[pl.BlockSpec((MEM_TQ, MEM_WIDTH), lambda b, i: (b * nq + i, MQ_BLK4)),
                  pl.BlockSpec((N_MEM, MEM_WIDTH), lambda b, i: (b, 0)),
                  pl.BlockSpec((N_MEM, MEM_WIDTH), lambda b, i: (b, 1))],
        out_specs=pl.BlockSpec((MEM_TQ, MEM_WIDTH), lambda b, i: (b * nq + i, 0)),
        out_shape=SDS((t, MEM_WIDTH), F32),
        name="mem_attn_fwd", compiler_params=_params("parallel", "parallel"))(proj, mkv, mkv)


def mem_attn_bwd(proj, mkv, dy, nb):
    t = proj.shape[0]
    nq = SEQ // MEM_TQ

    def body(q_ref, mk_ref, mv_ref, do_ref, dq_ref, dmk_ref, dmv_ref):
        @pl.when(pl.program_id(1) == 0)
        def _():
            dmk_ref[...] = jnp.zeros_like(dmk_ref)
            dmv_ref[...] = jnp.zeros_like(dmv_ref)

        for h in range(MEM_HEADS):
            cols = slice(h * LANES, (h + 1) * LANES)
            q = q_ref[:, cols].astype(BF16)
            mk, mv = mk_ref[:, cols], mv_ref[:, cols]
            do = do_ref[:, cols].astype(BF16)
            s = _dot_nt(q, mk) * MEM_SCALE
            e = jnp.exp(s - jnp.max(s, axis=-1, keepdims=True))
            p = e / jnp.sum(e, axis=-1, keepdims=True)
            dp = _dot_nt(do, mv)
            ds = (p * (dp - jnp.sum(p * dp, axis=-1, keepdims=True)) * MEM_SCALE).astype(BF16)
            dq_ref[:, cols] = _dot(ds, mk).astype(BF16)
            dmk_ref[:, cols] += _dot_tn(ds, q)
            dmv_ref[:, cols] += _dot_tn(p.astype(BF16), do)

    ospec = pl.BlockSpec((MEM_TQ, MEM_WIDTH), lambda b, i: (b * nq + i, 0))
    kspec = pl.BlockSpec((N_MEM, MEM_WIDTH), lambda b, i: (b, 0))
    return pl.pallas_call(
        body, grid=(nb, nq),
        in_specs=[pl.BlockSpec((MEM_TQ, MEM_WIDTH), lambda b, i: (b * nq + i, MQ_BLK4)),
                  kspec, pl.BlockSpec((N_MEM, MEM_WIDTH), lambda b, i: (b, 1)), ospec],
        out_specs=[ospec, kspec, kspec],
        out_shape=[SDS((t, MEM_WIDTH), BF16), SDS((nb * N_MEM, MEM_WIDTH), F32), SDS((nb * N_MEM, MEM_WIDTH), F32)],
        name="mem_attn_bwd", compiler_params=_params("parallel", "arbitrary"))(proj, mkv, mkv, dy)


ROW_TM = 512
AG_BLK = 3072 // 1024
BG_BLK = 4608 // 512
MG_BLK = 5632 // 512
GROUPS = ((0, A_WIDTH), (A_WIDTH, MLA_WIDTH), (A_WIDTH + MLA_WIDTH, MEM_WIDTH))
D_MIX = 2048


def _gate_specs():
    def row(w, j):
        return pl.BlockSpec((ROW_TM, w), lambda i: (i, j))

    def vec(w):
        return pl.BlockSpec((1, w), lambda i: (0, 0))

    ys = [row(A_WIDTH, 0), row(MLA_WIDTH, 0), row(MEM_WIDTH, 0)]
    gates = [row(A_WIDTH, AG_BLK), row(MLA_WIDTH, BG_BLK), row(MEM_WIDTH, MG_BLK)]
    gains = [vec(A_WIDTH), vec(MLA_WIDTH), vec(MEM_WIDTH)]
    return row, vec, ys, gates, gains


def gate_out_ln_loss(ya, yb, ym, proj, goa, gob, gom, wout, h32, target, gp, bp):
    t, d = h32.shape
    _, _, ys, gates, gains = _gate_specs()

    def body(ya_ref, yb_ref, ym_ref, ga_ref, gb_ref, gm_ref, goa_ref, gob_ref, gom_ref, w_ref, h_ref, t_ref, gp_ref, bp_ref,
             z_ref, du32_ref, du16_ref, loss_ref, dgp_ref, dbp_ref):
        @pl.when(pl.program_id(0) == 0)
        def _():
            loss_ref[...] = jnp.zeros_like(loss_ref)
            dgp_ref[...] = jnp.zeros_like(dgp_ref)
            dbp_ref[...] = jnp.zeros_like(dbp_ref)

        for (off, w), y_ref, g_ref, go_ref in zip(GROUPS, (ya_ref, yb_ref, ym_ref), (ga_ref, gb_ref, gm_ref),
                                                  (goa_ref, gob_ref, gom_ref)):
            n, _ = _rms(y_ref[...], go_ref[...])
            gt = g_ref[...]
            z_ref[:, off:off + w] = (n * (gt * _sigmoid(gt))).astype(BF16)
        g = gp_ref[...]
        u = ALPHA * h_ref[...] + _dot(z_ref[...], w_ref[...])
        mu = jnp.mean(u, axis=-1, keepdims=True)
        uc = u - mu
        rstd = lax.rsqrt(jnp.mean(uc * uc, axis=-1, keepdims=True) + NORM_EPS)
        xhat = uc * rstd
        err = xhat * g + bp_ref[...] - t_ref[...]
        tok = jnp.sum(err * err, axis=-1, keepdims=True) * (1.0 / d)
        loss_ref[...] += 0.5 * jnp.sum(tok, axis=0, keepdims=True)
        dout = err * (1.0 / d)
        dxhat = dout * g
        du = rstd * (dxhat - jnp.mean(dxhat, axis=-1, keepdims=True)
                     - xhat * jnp.mean(dxhat * xhat, axis=-1, keepdims=True))
        du32_ref[...] = du
        du16_ref[...] = du.astype(BF16)
        dgp_ref[...] += jnp.sum(dout * xhat, axis=0, keepdims=True)
        dbp_ref[...] += jnp.sum(dout, axis=0, keepdims=True)

    row = pl.BlockSpec((ROW_TM, d), lambda i: (i, 0))
    vec = pl.BlockSpec((1, d), lambda i: (0, 0))
    zrow = pl.BlockSpec((ROW_TM, D_MIX), lambda i: (i, 0))
    return pl.pallas_call(
        body, grid=(t // ROW_TM,),
        in_specs=ys + gates + gains + [pl.BlockSpec((D_MIX, d), lambda i: (0, 0)), row, row, vec, vec],
        out_specs=[zrow, row, row, pl.BlockSpec((1, LANES), lambda i: (0, 0)), vec, vec],
        out_shape=[SDS((t, D_MIX), BF16), SDS((t, d), F32), SDS((t, d), BF16), SDS((1, LANES), F32), SDS((1, d), F32),
                   SDS((1, d), F32)],
        name="gate_out_ln_loss", compiler_params=_params("arbitrary"))(
            ya, yb, ym, proj, proj, proj, goa, gob, gom, wout, h32, target, gp, bp)


def gate_bwd(du16, wout, ya, yb, ym, proj, goa, gob, gom):
    t = ya.shape[0]
    row, vec, ys, gates, gains = _gate_specs()

    def body(du_ref, w_ref, ya_ref, yb_ref, ym_ref, ga_ref, gb_ref, gm_ref, goa_ref, gob_ref, gom_ref,
             dya_ref, dyb_ref, dym_ref, dga_ref, dgb_ref, dgm_ref, dgoa_ref, dgob_ref, dgom_ref):
        @pl.when(pl.program_id(0) == 0)
        def _():
            dgoa_ref[...] = jnp.zeros_like(dgoa_ref)
            dgob_ref[...] = jnp.zeros_like(dgob_ref)
            dgom_ref[...] = jnp.zeros_like(dgom_ref)

        dz = _dot_nt(du_ref[...], w_ref[...])
        for (off, w), y_ref, g_ref, go_ref, dy_ref, dg_ref, dgo_ref in zip(
                GROUPS, (ya_ref, yb_ref, ym_ref), (ga_ref, gb_ref, gm_ref), (goa_ref, gob_ref, gom_ref),
                (dya_ref, dyb_ref, dym_ref), (dga_ref, dgb_ref, dgm_ref), (dgoa_ref, dgob_ref, dgom_ref)):
            dzg = dz[:, off:off + w]
            y, gt, go = y_ref[...], g_ref[...], go_ref[...]
            n, r = _rms(y, go)
            sg = _sigmoid(gt)
            dg_ref[...] = (dzg * n * (sg * (1.0 + gt * (1.0 - sg)))).astype(BF16)
            dy, dgo = _rms_bwd(dzg * (gt * sg), y, r, go)
            dy_ref[...] = dy
            dgo_ref[...] += dgo

    widths = (A_WIDTH, MLA_WIDTH, MEM_WIDTH)
    return pl.pallas_call(
        body, grid=(t // ROW_TM,),
        in_specs=[row(D_MODEL, 0), pl.BlockSpec((D_MIX, D_MODEL), lambda i: (0, 0))] + ys + gates + gains,
        out_specs=[row(w, 0) for w in widths] * 2 + [vec(w) for w in widths],
        out_shape=[SDS((t, w), F32) for w in widths] + [SDS((t, w), BF16) for w in widths] + [SDS((1, w), F32) for w in widths],
        name="gate_bwd", compiler_params=_params("arbitrary"))(du16, wout, ya, yb, ym, proj, proj, proj, goa, gob, gom)


def dh_ln_bwd(pieces, win_t, du32, x2, g_emb, xch):
    t, d = x2.shape

    def body(*refs):
        p_refs = refs[:len(pieces)]
        w_ref, du_ref, x_ref, g_ref, dx_ref, dg_ref, db_ref = refs[len(pieces):]

        @pl.when(pl.program_id(0) == 0)
        def _():
            dg_ref[...] = jnp.zeros_like(dg_ref)
            db_ref[...] = jnp.zeros_like(db_ref)

        dh = ALPHA * du_ref[...]
        for p_ref, off, w in zip(p_refs, PIECE_OFFS, PIECE_WIDTHS):
            dh = dh + _dot(p_ref[...], w_ref[off:off + w, :])
        x = x_ref[...]
        xc = x - jnp.mean(x, axis=-1, keepdims=True)
        rstd = lax.rsqrt(jnp.mean(xc * xc, axis=-1, keepdims=True) + NORM_EPS)
        xhat = xc * rstd
        dg_ref[...] += jnp.sum(dh * xhat, axis=0, keepdims=True)
        db_ref[...] += jnp.sum(dh, axis=0, keepdims=True)
        tg = dh * g_ref[...]
        dx_ref[...] = rstd * (tg - jnp.mean(tg, axis=-1, keepdims=True)
                              - xhat * jnp.mean(tg * xhat, axis=-1, keepdims=True))

    row = pl.BlockSpec((ROW_TM, d), lambda i: (i, 0))
    vec = pl.BlockSpec((1, d), lambda i: (0, 0))
    return call_hosting_exchange(
        body, xch, grid=(t // ROW_TM,),
        in_specs=[pl.BlockSpec((ROW_TM, w), lambda i: (i, 0)) for w in PIECE_WIDTHS]
        + [pl.BlockSpec(win_t.shape, lambda i: (0, 0)), row, row, vec],
        out_specs=[row, vec, vec],
        out_shape=[SDS((t, d), F32), SDS((1, d), F32), SDS((1, d), F32)],
        scratch_shapes=[], name="dh_ln_bwd", operands=(*pieces, win_t, du32, x2, g_emb))


def _adamw(w, g, m, v):
    m2 = ADAM_B1 * m + (1.0 - ADAM_B1) * g
    v2 = ADAM_B2 * v + (1.0 - ADAM_B2) * (g * g)
    m_hat = m2 / (1.0 - ADAM_B1 ** ADAM_STEP)
    v_hat = v2 / (1.0 - ADAM_B2 ** ADAM_STEP)
    return -ADAM_LR * (m_hat / (jnp.sqrt(v_hat) + ADAM_EPS) + ADAM_WD * w), m2, v2


def adamw_shard(w, parts, m, v, name):
    r, c = w.shape
    if r % 256 == 0 or r * c <= 256 * 1024:
        tr, tc = min(r, 256), c
    else:
        tr, tc = r, 256

    def body(w_ref, p_ref, m_ref, v_ref, g_ref, d_ref, nm_ref, nv_ref):
        g = p_ref[0].astype(F32)
        for k in range(1, N_DEV):
            g = g + p_ref[k].astype(F32)
        g_ref[...] = g
        d_ref[...], nm_ref[...], nv_ref[...] = _adamw(w_ref[...], g, m_ref[...], v_ref[...])

    blk = pl.BlockSpec((tr, tc), lambda i, j: (i, j))
    return pl.pallas_call(
        body, grid=(r // tr, c // tc),
        in_specs=[blk, pl.BlockSpec((N_DEV, tr, tc), lambda i, j: (0, i, j)), blk, blk],
        out_specs=[blk] * 4, out_shape=[SDS((r, c), F32)] * 4, name=name,
        compiler_params=_params("parallel", "parallel"))(w, parts, m, v)


def _place():
    return lax.axis_index("x"), lax.axis_index("y"), lax.axis_index("c")


def _flat(px, py, pc):
    return 4 * px + 2 * py + pc


def _peer(x, y, c, k):
    return (1 - x if k & 4 else x, 1 - y if k & 2 else y, 1 - c if k & 1 else c)


def cast_shards(shards):
    def body(*refs):
        n = len(refs) // 2
        for i_ref, o_ref in zip(refs[:n], refs[n:]):
            o_ref[...] = i_ref[...].astype(BF16)

    return pl.pallas_call(body, out_shape=[SDS(s.shape, BF16) for s in shards], name="cast_shards",
                          compiler_params=_params())(*shards)


def _two_level_gather_plan(src_refs, land_refs, send_sems, recv_sems, local_sems):
    n = len(src_refs)
    x, y, c = _place()
    me, sib = (x, y, c), (x, y, 1 - c)
    chips = [(1 - x, y), (x, 1 - y), (1 - x, 1 - y)]

    def copy(a, k, block, to, src=None):
        dst = land_refs[a].at[_flat(*block)]
        return pltpu.make_async_remote_copy(
            src_ref=dst if src is None else src, dst_ref=dst,
            send_sem=send_sems.at[a * N_DEV + k], recv_sem=recv_sems.at[a * N_DEV + k],
            device_id=to, device_id_type=MESH)

    mine = [pltpu.make_async_copy(src_refs[a], land_refs[a].at[_flat(*me)], local_sems.at[a]) for a in range(n)]
    first = []
    for a in range(n):
        first.append(copy(a, 0, me, sib, src=src_refs[a]))
        first += [copy(a, 1 + j, me, (*chip, c), src=src_refs[a]) for j, chip in enumerate(chips)]

    def start():
        for cp in mine + first:
            cp.start()

    def finish():
        passed = []
        for j, chip in enumerate(chips):
            for a in range(n):
                copy(a, 1 + j, (*chip, c), me).wait_recv()
                fwd = copy(a, 4 + j, (*chip, c), sib)
                fwd.start()
                passed.append(fwd)
        for a in range(n):
            copy(a, 0, sib, me).wait_recv()
            for j, chip in enumerate(chips):
                copy(a, 4 + j, (*chip, 1 - c), me).wait_recv()
        for cp in first + passed:
            cp.wait_send()
        for cp in mine:
            cp.wait()

    return start, finish


ALL_DEVICES = tuple(range(N_DEV))


def _exchange_plan(src_refs, land_refs, dests, send_sems, recv_sems, local_sems):
    x, y, c = _place()
    me = _flat(x, y, c)
    plan = []
    for a, (src, land, dl) in enumerate(zip(src_refs, land_refs, dests)):
        for li, j in enumerate(dl):
            to = ((j >> 2) & 1, (j >> 1) & 1, j & 1)
            block = src.at[li] if len(src.shape) == len(land.shape) else src

            def push(slot, a=a, block=block, land=land, j=j, to=to):
                return pltpu.make_async_remote_copy(
                    src_ref=block, dst_ref=land.at[slot], send_sem=send_sems.at[a * N_DEV + j],
                    recv_sem=recv_sems.at[a * N_DEV + slot], device_id=to, device_id_type=MESH)

            own = pltpu.make_async_copy(block, land.at[j], local_sems.at[a])
            plan.append((j, push(me), own, [push(s) for s in range(N_DEV) if s != j]))
    return me, plan


def _exchange_start(me, plan):
    for j, send, own, _ in plan:
        @pl.when(me != j)
        def _(send=send):
            send.start()

        @pl.when(me == j)
        def _(own=own):
            own.start()


def _exchange_wait(me, plan):
    for j, send, own, arrivals in plan:
        @pl.when(me != j)
        def _(send=send):
            send.wait_send()

        @pl.when(me == j)
        def _(own=own, arrivals=arrivals):
            own.wait()
            for arrival in arrivals:
                arrival.wait_recv()


def call_hosting_exchange(core, xch, *, grid, in_specs, out_specs, out_shape, scratch_shapes, name, operands):
    srcs, dests, landing = xch
    n, n_in, n_out, n_scr = len(srcs), len(in_specs), len(out_specs), len(scratch_shapes)

    def body(*refs):
        ins, src_refs = refs[:n_in], refs[n_in:n_in + n]
        outs = refs[n_in + 2 * n:n_in + 2 * n + n_out]
        land_refs = refs[n_in + 2 * n + n_out:n_in + 3 * n + n_out]
        scratch = refs[n_in + 3 * n + n_out:n_in + 3 * n + n_out + n_scr]
        sems = refs[n_in + 3 * n + n_out + n_scr:]
        first = functools.reduce(jnp.logical_and, [pl.program_id(i) == 0 for i in range(len(grid))])
        last = functools.reduce(jnp.logical_and, [pl.program_id(i) == grid[i] - 1 for i in range(len(grid))])
        if dests is None:
            start, finish = _two_level_gather_plan(src_refs, land_refs, *sems)
        else:
            me, plan = _exchange_plan(src_refs, land_refs, dests, *sems)
            start, finish = functools.partial(_exchange_start, me, plan), functools.partial(_exchange_wait, me, plan)
        pl.when(first)(start)
        core(*ins, *outs, *scratch)
        pl.when(last)(finish)

    hbm = pl.BlockSpec(memory_space=pl.ANY)
    res = pl.pallas_call(
        body, grid=grid,
        in_specs=list(in_specs) + [hbm] * (2 * n), out_specs=list(out_specs) + [hbm] * n,
        out_shape=list(out_shape) + [SDS(l.shape, l.dtype) for l in landing],
        scratch_shapes=list(scratch_shapes) + [pltpu.SemaphoreType.DMA((N_DEV * n,)), pltpu.SemaphoreType.DMA((N_DEV * n,)),
                                               pltpu.SemaphoreType.DMA((n,))],
        input_output_aliases={n_in + n + k: n_out + k for k in range(n)},
        name=name, compiler_params=_params(*(("arbitrary",) * len(grid))))(*operands, *srcs, *landing)
    return res[:n_out], res[n_out:]


SLOT_ROWS = 8


def small_allreduce_adamw(loss_sum, grads, ws, ms, vs):
    n = len(grads)
    rows = [g.shape[0] for g in grads]
    total = SLOT_ROWS * (n + 1)

    def body(*refs):
        loss_ref, g_refs, w_refs = refs[0], refs[1:1 + n], refs[1 + n:1 + 2 * n]
        m_refs, v_refs = refs[1 + 2 * n:1 + 3 * n], refs[1 + 3 * n:1 + 4 * n]
        outs = refs[1 + 4 * n:2 + 8 * n]
        vec, gath, tot, send_sems, recv_sems = refs[2 + 8 * n:]
        x, y, c = _place()
        me = _flat(x, y, c)
        vec[...] = jnp.zeros_like(vec)
        vec[0:1, :] = loss_ref[...]
        for i in range(n):
            vec[SLOT_ROWS * (i + 1):SLOT_ROWS * (i + 1) + rows[i], :] = g_refs[i][...]
        gath[me] = vec[...]
        copies = []
        for k in range(1, N_DEV):
            peer = _peer(x, y, c, k)
            copies.append(pltpu.make_async_remote_copy(
                src_ref=vec, dst_ref=gath.at[me], send_sem=send_sems.at[k - 1], recv_sem=recv_sems.at[k - 1],
                device_id=peer, device_id_type=MESH))
        for cp in copies:
            cp.start()
        for cp in copies:
            cp.wait_recv()
        for cp in copies:
            cp.wait_send()
        g = gath[0]
        for j in range(1, N_DEV):
            g = g + gath[j]
        tot[...] = g
        outs[0][...] = tot[0:1, :]
        for i in range(n):
            gi = tot[SLOT_ROWS * (i + 1):SLOT_ROWS * (i + 1) + rows[i], :]
            outs[1 + i][...] = gi
            outs[1 + n + i][...], outs[1 + 2 * n + i][...], outs[1 + 3 * n + i][...] = _adamw(
                w_refs[i][...], gi, m_refs[i][...], v_refs[i][...])

    shapes = [SDS(g.shape, F32) for g in grads]
    return pl.pallas_call(
        body, out_shape=[SDS((1, LANES), F32)] + shapes * 4,
        scratch_shapes=[pltpu.VMEM((total, LANES), F32), pltpu.VMEM((N_DEV, total, LANES), F32), pltpu.VMEM((total, LANES), F32),
                        pltpu.SemaphoreType.DMA((7,)), pltpu.SemaphoreType.DMA((7,))],
        name="small_allreduce_adamw", compiler_params=_params())(loss_sum, *grads, *ws, *ms, *vs)


def _rope_lane_patterns():
    inv = lambda r: ROPE_THETA ** (-(jnp.arange(0, r, 2, dtype=F32) / r))
    z = lambda n: jnp.zeros((n,), F32)
    o = lambda n: jnp.ones((n,), F32)
    half, rest = A_ROT // 2, A_HEAD_DIM - A_ROT
    ia, im = inv(A_ROT), inv(MLA_ROPE)
    mh, tail = MLA_ROPE // 2, LANES - MLA_NOPE - MLA_ROPE
    rows = [jnp.tile(jnp.concatenate([ia, ia, z(rest)]), 2),
            jnp.tile(jnp.concatenate([o(half), z(half + rest)]), 2),
            jnp.tile(jnp.concatenate([z(half), o(half), z(rest)]), 2),
            jnp.concatenate([z(MLA_NOPE), im, im, z(tail)]),
            jnp.concatenate([z(MLA_NOPE), o(mh), z(mh + tail)]),
            jnp.concatenate([z(MLA_NOPE + mh), o(mh), z(tail)]),
            z(LANES), z(LANES)]
    return jnp.stack(rows)


KR_LO, KR_HI = 4480, 4512
W_IN_SHARD = D_IN // N_DEV
AG_SPLIT = 5 * W_IN_SHARD - 3 * A_WIDTH
BG_SPLIT = 6 * W_IN_SHARD - KR_HI


def w_in_working_t(g):
    pad_lo, pad_hi = MLA_NOPE, LANES - MLA_NOPE - MLA_ROPE
    spans = []
    for lo, hi, shift in ((0, KR_LO, 0), (KR_LO, KR_HI, pad_lo), (KR_HI, D_IN, pad_lo + pad_hi)):
        r = lo
        while r < hi:
            j = r // W_IN_SHARD
            n = min(hi, (j + 1) * W_IN_SHARD) - r
            spans.append((j, r - j * W_IN_SHARD, n, r + shift))
            r += n

    def body(g_ref, o_ref):
        o_ref[KR_LO:KR_LO + pad_lo, :] = jnp.zeros((pad_lo, D_MODEL), o_ref.dtype)
        o_ref[KR_HI + pad_lo:KR_HI + pad_lo + pad_hi, :] = jnp.zeros((pad_hi, D_MODEL), o_ref.dtype)
        for j, src, n, dst in spans:
            o_ref[dst:dst + n, :] = g_ref[j, src:src + n, :]

    return pl.pallas_call(body, out_shape=SDS((D_INW, D_MODEL), g.dtype), name="w_in_working_t", compiler_params=_params())(g)


def _shards(rows):
    return rows.reshape(-1, W_IN_SHARD, D_MODEL).astype(BF16)


def _w_in_shard_5(d_ag, d_cc, d_bg):
    kr = MLA_Q_RANK + MLA_KV_RANK + MLA_NOPE
    return _shards(jnp.concatenate([d_ag[AG_SPLIT:], d_cc[:MLA_Q_RANK + MLA_KV_RANK], d_cc[kr:kr + MLA_ROPE], d_bg[:BG_SPLIT]], 0))


def _w_in_shards_6_7(d_bg, d_mq, d_mg):
    return _shards(jnp.concatenate([d_bg[BG_SPLIT:], d_mq, d_mg], 0))


def _w_uq_working(g):
    w = jnp.pad(g.transpose(1, 0, 2), ((0, 0), (0, 0), (0, LANES - MLA_NOPE - MLA_ROPE)))
    return w.reshape(MLA_Q_RANK, MLA_QW)


def _w_uq_parts(dw):
    return dw.reshape(MLA_Q_RANK, MLA_HEADS, LANES)[:, :, :MLA_NOPE + MLA_ROPE].transpose(1, 0, 2)


def _w_ukv_working(g):
    wk = jnp.pad(g[:, :, :MLA_NOPE].transpose(1, 0, 2), ((0, 0), (0, 0), (0, LANES - MLA_NOPE)))
    wv = g[:, :, MLA_NOPE:].transpose(1, 0, 2)
    return jnp.concatenate([wk.reshape(MLA_KV_RANK, MLA_QW), wv.reshape(MLA_KV_RANK, MLA_WIDTH)], 1)


def _w_ukv_parts(dw):
    dk = dw[:, :MLA_QW].reshape(MLA_KV_RANK, MLA_HEADS, LANES)[:, :, :MLA_NOPE]
    dv = dw[:, MLA_QW:].reshape(MLA_KV_RANK, MLA_HEADS, MLA_V)
    return jnp.concatenate([dk, dv], -1).transpose(1, 0, 2)


SMALL_NAMES = ("g_emb", "b_emb", "g_cq", "g_ckv", "g_out_a", "g_out_b", "g_out_m", "g_post", "b_post")


def kernel(x, mem, positions, g_emb, b_emb, w_in, g_cq, g_ckv, w_uq, w_ukv, w_mem_kv, g_out_a, g_out_b, g_out_m, w_out, g_post, b_post, loss_target, m_g_emb, m_b_emb, m_w_in, m_g_cq, m_g_ckv, m_w_uq, m_w_ukv, m_w_mem_kv, m_g_out_a, m_g_out_b, m_g_out_m, m_w_out, m_g_post, m_b_post, v_g_emb, v_b_emb, v_w_in, v_g_cq, v_g_ckv, v_w_uq, v_w_ukv, v_w_mem_kv, v_g_out_a, v_g_out_b, v_g_out_m, v_w_out, v_g_post, v_b_post):
    nb = x.shape[0]
    t = nb * SEQ
    x2 = x.reshape(t, D_MODEL)
    tgt2 = loss_target.reshape(t, D_MODEL)
    mem2 = mem.reshape(nb * N_MEM, D_MODEL)
    g_emb2, b_emb2 = g_emb.reshape(1, -1), b_emb.reshape(1, -1)

    w_in_t, m_w_in_t, v_w_in_t = w_in[0].T, m_w_in[0].T, v_w_in[0].T
    s_in, s_uq, s_ukv, s_mem, s_out = cast_shards((w_in_t, w_uq[0], w_ukv[0], w_mem_kv[0], w_out[0]))
    (h32, h16, (a_c, a_sa, a_sb), (m_c, m_sa, m_sb)), (g_in,) = embed_fwd(
        x2, g_emb2, b_emb2, positions, ((s_in,), None, (lax.empty((N_DEV,) + s_in.shape, BF16),)))
    win_t = w_in_working_t(g_in)

    proj = mm_nn(h16, win_t, F32, 512, 1536, "proj", rhs_transposed=True)
    later = (s_uq, s_ukv, s_mem, s_out)
    (ya, lse_a), qkv_d, (g_uq, g_ukv, g_mem, g_out) = a_attn_fwd(
        proj, a_c, a_sa, a_sb, nb,
        (later, (ALL_DEVICES,) * len(later), tuple(lax.empty((N_DEV,) + w.shape, BF16) for w in later)))
    wuq_w = _w_uq_working(g_uq)
    wkv_w = _w_ukv_working(g_ukv)
    wmem = g_mem.reshape(D_MODEL, 2 * MEM_WIDTH)
    wout = g_out.reshape(D_MIX, D_MODEL)
    qb, kb, vb = mla_prep_fwd(proj, m_c, m_sa, m_sb, g_cq, g_ckv, wuq_w, wkv_w)
    yb, lse_b = mla_attn_fwd(qb, kb, vb, nb)
    mkv = mm_nn(mem2, wmem, BF16, nb * N_MEM, 512, "mem_kv")
    ym = mem_attn_fwd(proj, mkv, nb)
    z, du32, du16, loss_sum, dg_post, db_post = gate_out_ln_loss(
        ya, yb, ym, proj, g_out_a, g_out_b, g_out_m, wout, h32, tgt2, g_post, b_post)

    dya, dyb, dym, dag, dbg, dmg, dg_out_a, dg_out_b, dg_out_m = gate_bwd(
        du16, wout, ya, yb, ym, proj, g_out_a, g_out_b, g_out_m)
    dw_out = mm_tn(z, du16, 1024, "dw_out")
    dmq, dmk, dmv = mem_attn_bwd(proj, mkv, dym, nb)
    dw_mem = mm_tn(mem2, jnp.concatenate([dmk, dmv], 1), nb * N_MEM, "dw_mem")
    d_gates = mm_tn_group((dbg, dmq, dmg), h16, 2048, "dw_in_bg_mq_mg")
    d_bg, d_mq, d_mg = d_gates[:MLA_WIDTH], d_gates[MLA_WIDTH:2 * MLA_WIDTH], d_gates[2 * MLA_WIDTH:]
    landing = lambda w, dtype=F32: lax.empty((N_DEV,) + w.shape, dtype)
    big_w = (w_in_t, w_uq[0], w_ukv[0], w_mem_kv[0], w_out[0])
    (daq, dak, dav), (p_out, p_mem, p_in) = a_attn_bwd(
        qkv_d, a_c, a_sa, a_sb, dya, ya, lse_a, nb,
        ((dw_out.reshape(N_DEV, D_MIX // N_DEV, D_MODEL), dw_mem.reshape(N_DEV, D_MODEL // N_DEV, 2 * MEM_WIDTH),
          _w_in_shards_6_7(d_bg, d_mq, d_mg)),
         (ALL_DEVICES, ALL_DEVICES, (6, 7)),
         (landing(w_out[0]), landing(w_mem_kv[0]), landing(w_in_t, BF16))))
    d_a = mm_tn_group((daq, dak, dav, dag), h16, 2048, "dw_in_aq_ak_av_ag")
    d_ag = d_a[3 * A_WIDTH:]
    (dqb, dkb, dvb), (p_in,) = mla_attn_bwd(
        qb, kb, vb, dyb, yb, lse_b, nb, ((_shards(d_a[:5 * W_IN_SHARD]),), ((0, 1, 2, 3, 4),), (p_in,)))
    dcc, dqf, cqn, dkvf, ckvn, dg_cq, dg_ckv = mla_prep_bwd(proj, m_c, m_sa, m_sb, g_cq, g_ckv, wuq_w, wkv_w, dqb, dkb, dvb)
    dw_uq = mm_tn(cqn, dqf, 2048, "dw_uq")
    dw_ukv = mm_tn(ckvn, dkvf, 2048, "dw_ukv")
    d_cc = mm_tn(dcc, h16, 2048, "dw_in_cc")
    pieces = (daq, dak, dav, dag, dcc, dbg, dmq, dmg)
    (grad_x, dg_emb, db_emb), (p_in, p_uq, p_ukv) = dh_ln_bwd(
        pieces, win_t, du32, x2, g_emb2,
        ((_w_in_shard_5(d_ag, d_cc, d_bg), _w_uq_parts(dw_uq), _w_ukv_parts(dw_ukv)),
         ((5,), ALL_DEVICES, ALL_DEVICES),
         (p_in, landing(w_uq[0]), landing(w_ukv[0]))))

    parts = (p_in, p_uq, p_ukv, p_mem, p_out)
    big_m = (m_w_in_t, m_w_uq[0], m_w_ukv[0], m_w_mem_kv[0], m_w_out[0])
    big_v = (v_w_in_t, v_w_uq[0], v_w_ukv[0], v_w_mem_kv[0], v_w_out[0])
    big = {}
    for name, w, p, m, v in zip(("w_in", "w_uq", "w_ukv", "w_mem_kv", "w_out"), big_w, parts, big_m, big_v):
        res = adamw_shard(w, p, m, v, "adamw_" + name)
        big[name] = [(o.T if name == "w_in" else o)[None] for o in res]

    small_w = (g_emb, b_emb, g_cq, g_ckv, g_out_a, g_out_b, g_out_m, g_post, b_post)
    small_m = (m_g_emb, m_b_emb, m_g_cq, m_g_ckv, m_g_out_a, m_g_out_b, m_g_out_m, m_g_post, m_b_post)
    small_v = (v_g_emb, v_b_emb, v_g_cq, v_g_ckv, v_g_out_a, v_g_out_b, v_g_out_m, v_g_post, v_b_post)
    small_g = (dg_emb, db_emb, dg_cq, dg_ckv, dg_out_a, dg_out_b, dg_out_m, dg_post, db_post)
    rows128 = lambda vals: [v.reshape(-1, LANES) for v in vals]
    res = small_allreduce_adamw(loss_sum, rows128(small_g), rows128(small_w), rows128(small_m), rows128(small_v))
    loss = res[0][0, 0]
    n_small = len(small_w)
    sg, sd, sm, sv = [[r.reshape(w.shape) for r, w in zip(res[1 + k * n_small:1 + (k + 1) * n_small], small_w)]
                      for k in range(4)]

    order = ("g_emb", "b_emb", "w_in", "g_cq", "g_ckv", "w_uq", "w_ukv", "w_mem_kv", "g_out_a", "g_out_b", "g_out_m",
             "w_out", "g_post", "b_post")
    small_idx = {n: i for i, n in enumerate(SMALL_NAMES)}
    outs = [loss, grad_x.reshape(x.shape)]
    for kind in range(4):
        for name in order:
            outs.append(big[name][kind] if name in big else (sg, sd, sm, sv)[kind][small_idx[name]])
    return tuple(outs)
```

```python
import functools

import jax
import jax.numpy as jnp
from jax import lax
from jax.experimental import pallas as pl
from jax.experimental.pallas import tpu as pltpu

F32 = jnp.float32
BF16 = jnp.bfloat16
SDS = jax.ShapeDtypeStruct
MESH = pl.DeviceIdType.MESH

D_MODEL = 1024
SEQ = 2048
A_HEADS, A_HEAD_DIM, A_ROT = 16, 64, 16
A_WIDTH = 1024
DILATIONS = (1, 4, 16)
N_SIDE = 64
MLA_HEADS, MLA_Q_RANK, MLA_KV_RANK = 8, 256, 128
MLA_NOPE, MLA_ROPE, MLA_V = 64, 32, 64
MLA_WIDTH = 512
N_MEM, MEM_HEADS, MEM_HEAD_DIM, MEM_WIDTH = 256, 4, 128, 512
ROPE_THETA = 500000.0
NORM_EPS = 1e-5
NEG_INF = -1e30
ALPHA = 2.0 ** 0.25
D_IN = 6048
N_DEV = 8

ADAM_LR, ADAM_B1, ADAM_B2, ADAM_EPS, ADAM_WD, ADAM_STEP = 0.001, 0.9, 0.999, 1e-08, 0.01, 10

D_INW = 6144
PIECE_WIDTHS = (1024, 1024, 1024, 1024, 512, 512, 512, 512)
PIECE_OFFS = (0, 1024, 2048, 3072, 4096, 4608, 5120, 5632)
LANES = 128
VMEM_LIMIT = 56 * 1024 * 1024


def _params(*sem):
    kw = dict(vmem_limit_bytes=VMEM_LIMIT)
    if sem:
        kw["dimension_semantics"] = sem
    return pltpu.CompilerParams(**kw)


def _dot(a, b):
    return jnp.dot(a, b, preferred_element_type=F32)


def _dot_nt(a, b):
    return lax.dot_general(a, b, (((1,), (1,)), ((), ())), preferred_element_type=F32)


def _dot_tn(a, b):
    return lax.dot_general(a, b, (((0,), (0,)), ((), ())), preferred_element_type=F32)


def _sigmoid(x):
    return 1.0 / (1.0 + jnp.exp(-x))


def _rope_fwd(x, c, sa, sb, half):
    n = x.shape[-1]
    return x * c + pltpu.roll(x, n - half, 1) * sa + pltpu.roll(x, half, 1) * sb


def _rope_bwd(dy, c, sa, sb, half):
    n = dy.shape[-1]
    return dy * c + pltpu.roll(dy * sa, half, 1) + pltpu.roll(dy * sb, n - half, 1)


def mm_nn(a, b, out_dtype, tm, tn, name, rhs_transposed=False):
    m, k = a.shape
    n = b.shape[0] if rhs_transposed else b.shape[1]
    dot = _dot_nt if rhs_transposed else _dot

    def body(a_ref, b_ref, o_ref):
        o_ref[...] = dot(a_ref[...].astype(BF16), b_ref[...].astype(BF16)).astype(o_ref.dtype)

    b_spec = pl.BlockSpec((tn, k), lambda j, i: (j, 0)) if rhs_transposed else pl.BlockSpec((k, tn), lambda j, i: (0, j))
    return pl.pallas_call(
        body, grid=(n // tn, m // tm),
        in_specs=[pl.BlockSpec((tm, k), lambda j, i: (i, 0)), b_spec],
        out_specs=pl.BlockSpec((tm, tn), lambda j, i: (i, j)),
        out_shape=SDS((m, n), out_dtype), name=name,
        compiler_params=_params("parallel", "parallel"))(a, b)


def mm_tn(a, b, tt, name):
    t, m = a.shape
    n = b.shape[1]

    def body(a_ref, b_ref, o_ref):
        @pl.when(pl.program_id(0) == 0)
        def _():
            o_ref[...] = jnp.zeros_like(o_ref)

        o_ref[...] += _dot_tn(a_ref[...].astype(BF16), b_ref[...].astype(BF16))

    return pl.pallas_call(
        body, grid=(t // tt,),
        in_specs=[pl.BlockSpec((tt, m), lambda i: (i, 0)), pl.BlockSpec((tt, n), lambda i: (i, 0))],
        out_specs=pl.BlockSpec((m, n), lambda i: (0, 0)),
        out_shape=SDS((m, n), F32), name=name,
        compiler_params=_params("arbitrary"))(a, b)


def mm_tn_group(pieces, b, tt, name, slab_rows, first_slab_row, n_slabs):
    n, (t, w), cols = len(pieces), pieces[0].shape, b.shape[1]
    nt = t // tt

    def body(*refs):
        p_refs, b_ref, o_ref, slab_ref = refs[:n], refs[n], refs[n + 1], refs[n + 2]

        @pl.when(pl.program_id(1) == 0)
        def _():
            o_ref[...] = jnp.zeros_like(o_ref)

        for k in range(n):
            @pl.when(pl.program_id(0) == k)
            def _(k=k):
                o_ref[...] += _dot_tn(p_refs[k][...], b_ref[...])

            @pl.when((pl.program_id(0) == k) & (pl.program_id(1) == nt - 1))
            def _(k=k):
                for j in range(n_slabs):
                    lo = max(k * w, first_slab_row + j * slab_rows)
                    hi = min((k + 1) * w, first_slab_row + (j + 1) * slab_rows)
                    if lo < hi:
                        dst = lo - first_slab_row - j * slab_rows
                        slab_ref[j, dst:dst + hi - lo, :] = o_ref[lo - k * w:hi - k * w, :].astype(slab_ref.dtype)

    def piece_spec(k):
        return pl.BlockSpec((tt, w), lambda p, i: (jnp.where(p < k, 0, jnp.where(p > k, nt - 1, i)), 0))

    return pl.pallas_call(
        body, grid=(n, nt),
        in_specs=[piece_spec(k) for k in range(n)] + [pl.BlockSpec((tt, cols), lambda p, i: (i, 0))],
        out_specs=[pl.BlockSpec((w, cols), lambda p, i: (p, 0)),
                   pl.BlockSpec((n_slabs, slab_rows, cols), lambda p, i: (0, 0, 0))],
        out_shape=[SDS((n * w, cols), F32), SDS((n_slabs, slab_rows, cols), BF16)], name=name,
        compiler_params=_params("arbitrary", "arbitrary"))(*pieces, b)


def embed_fwd(x2, g, b, positions, xch):
    t, d = x2.shape
    tm = 512
    pos = positions.astype(F32).reshape(-1, 1)

    def body(x_ref, g_ref, b_ref, pos_ref, pat_ref, h32_ref, h16_ref, *tabs):
        x = x_ref[...]
        mu = jnp.mean(x, axis=-1, keepdims=True)
        xc = x - mu
        var = jnp.mean(xc * xc, axis=-1, keepdims=True)
        h = xc * lax.rsqrt(var + NORM_EPS) * g_ref[...] + b_ref[...]
        h32_ref[...] = h
        h16_ref[...] = h.astype(BF16)
        p = pos_ref[...]
        for k in range(2):
            inv, first, second = pat_ref[3 * k:3 * k + 1, :], pat_ref[3 * k + 1:3 * k + 2, :], pat_ref[3 * k + 2:3 * k + 3, :]
            ang = p * inv
            sn = jnp.sin(ang)
            tabs[3 * k][...] = jnp.where(first + second > 0.0, jnp.cos(ang), 1.0)
            tabs[3 * k + 1][...] = -first * sn
            tabs[3 * k + 2][...] = second * sn

    row = pl.BlockSpec((tm, d), lambda i: (i, 0))
    vec = pl.BlockSpec((1, d), lambda i: (0, 0))
    tab = pl.BlockSpec((tm, LANES), lambda i: (i, 0))
    res, landed = call_hosting_exchange(
        body, xch, grid=(t // tm,),
        in_specs=[row, vec, vec, pl.BlockSpec((tm, 1), lambda i: (i, 0)), pl.BlockSpec((8, LANES), lambda i: (0, 0))],
        out_specs=[row, row] + [tab] * 6,
        out_shape=[SDS((t, d), F32), SDS((t, d), BF16)] + [SDS((t, LANES), F32)] * 6,
        scratch_shapes=[], name="embed_fwd", operands=(x2, g, b, pos, _rope_lane_patterns()))
    return (res[0], res[1], tuple(res[2:5]), tuple(res[5:8])), landed


Q_BLK = 128
UNROLL_FWD = 8
UNROLL_BWD = 8


def _pattern_geometry(d):
    length = SEQ // d
    nblk = length // Q_BLK
    kwin = min(2 * Q_BLK, length)
    return length, nblk, kwin


def _block_coords(idx, d):
    length, nblk, kwin = _pattern_geometry(d)
    r = lax.shift_right_logical(idx, nblk.bit_length() - 1)
    i = idx & (nblk - 1)
    q0 = pl.multiple_of(r * length + i * Q_BLK, Q_BLK)
    ks = jnp.clip(i * Q_BLK - N_SIDE, 0, length - kwin)
    k0 = pl.multiple_of(r * length + ks, N_SIDE)
    qpos = i * Q_BLK + lax.broadcasted_iota(jnp.int32, (Q_BLK, kwin), 0)
    kpos = ks + lax.broadcasted_iota(jnp.int32, (Q_BLK, kwin), 1)
    valid = jnp.abs(kpos - qpos) <= N_SIDE
    return q0, k0, kwin, valid


def _deinterleave(src_ref, dst_ref, d, dtype, tmp_ref):
    if d == 1:
        dst_ref[...] = src_ref[...].astype(dtype)
        return
    q = SEQ // 4
    if d == 4:
        for r in range(4):
            dst_ref[r * q:(r + 1) * q, :] = src_ref[pl.ds(r, q, stride=4), :].astype(dtype)
        return
    assert d == 16
    n = SEQ // 16
    for r in range(4):
        tmp_ref[r * q:(r + 1) * q, :] = src_ref[pl.ds(r, q, stride=4), :]
    for r in range(4):
        for j in range(4):
            dst_ref[(r + 4 * j) * n:(r + 4 * j + 1) * n, :] = tmp_ref[pl.ds(r * q + j, n, stride=4), :].astype(dtype)


def _class16_to_class4(src_ref, dst_ref):
    q, n = SEQ // 4, SEQ // 16
    for r in range(4):
        for j in range(4):
            dst_ref[pl.ds(r * q + j, n, stride=4), :] = src_ref[(r + 4 * j) * n:(r + 4 * j + 1) * n, :]


def _interleave(src_ref, dst_ref, d, tmp_ref, accumulate):
    q = SEQ // 4
    if d == 16:
        _class16_to_class4(src_ref, tmp_ref)
        src_ref = tmp_ref
    else:
        assert d == 4
    for r in range(4):
        rows = pl.ds(r, q, stride=4)
        val = src_ref[r * q:(r + 1) * q, :]
        dst_ref[rows, :] = dst_ref[rows, :] + val if accumulate else val


def a_attn_fwd(proj, ca, sa, sb, nb, xch):
    t = proj.shape[0]
    n_pairs = A_WIDTH // LANES

    def body(q_ref, k_ref, v_ref, c_ref, sa_ref, sb_ref, y_ref, lse_ref, *rest):
        qkv_d, (qr_s, kr_s, oc_s, lc_s, o1_s, l1_s, o2_s, l2_s, o3_s, l3_s, tmp_s) = rest[:9], rest[9:]
        c, s_a, s_b = c_ref[...], sa_ref[...], sb_ref[...]
        qr_s[...] = _rope_fwd(q_ref[...], c, s_a, s_b, A_ROT // 2) * (A_HEAD_DIM ** -0.5)
        kr_s[...] = _rope_fwd(k_ref[...], c, s_a, s_b, A_ROT // 2)
        head0 = lax.broadcasted_iota(jnp.int32, (Q_BLK, LANES), 1) < A_HEAD_DIM
        nat = ((o1_s, l1_s), (o2_s, l2_s), (o3_s, l3_s))

        for g, d in enumerate(DILATIONS):
            qd_s, kd_s, vd_s = qkv_d[3 * g:3 * g + 3]
            _deinterleave(qr_s, qd_s, d, BF16, tmp_s)
            _deinterleave(kr_s, kd_s, d, BF16, tmp_s)
            _deinterleave(v_ref, vd_s, d, BF16, tmp_s)
            o_dst, l_dst = (nat[g] if d == 1 else (oc_s, lc_s))

            def block(idx, carry, d=d, o_dst=o_dst, l_dst=l_dst, qd_s=qd_s, kd_s=kd_s, vd_s=vd_s):
                q0, k0, kwin, valid = _block_coords(idx, d)
                qb = qd_s[pl.ds(q0, Q_BLK), :]
                kb = kd_s[pl.ds(k0, kwin), :]
                vb = vd_s[pl.ds(k0, kwin), :]
                zero = jnp.zeros_like(qb)
                q2 = jnp.concatenate([jnp.where(head0, qb, zero), jnp.where(head0, zero, qb)], 0)
                s = jnp.where(jnp.concatenate([valid, valid], 0), _dot_nt(q2, kb), NEG_INF)
                m = jnp.max(s, axis=-1, keepdims=True)
                p = jnp.exp(s - m)
                l = jnp.sum(p, axis=-1, keepdims=True)
                o2 = _dot(p.astype(BF16), vb) / l
                l2 = m + jnp.log(l)
                o_dst[pl.ds(q0, Q_BLK), :] = jnp.where(head0, o2[:Q_BLK], o2[Q_BLK:])
                l_dst[pl.ds(q0, Q_BLK), :] = jnp.where(head0, l2[:Q_BLK], l2[Q_BLK:])
                return carry

            lax.fori_loop(0, SEQ // Q_BLK, block, 0, unroll=UNROLL_FWD)
            if d > 1:
                _interleave(oc_s, nat[g][0], d, tmp_s, False)
                _interleave(lc_s, nat[g][1], d, tmp_s, False)

        def merge(ci, carry):
            rows = pl.ds(pl.multiple_of(ci * 256, 256), 256)
            l1, l2, l3 = l1_s[rows, :], l2_s[rows, :], l3_s[rows, :]
            m = jnp.maximum(jnp.maximum(l1, l2), l3)
            w1, w2, w3 = jnp.exp(l1 - m), jnp.exp(l2 - m), jnp.exp(l3 - m)
            w = w1 + w2 + w3
            y_ref[rows, :] = (w1 * o1_s[rows, :] + w2 * o2_s[rows, :] + w3 * o3_s[rows, :]) / w
            lse_ref[rows, :] = m + jnp.log(w)
            return carry

        lax.fori_loop(0, SEQ // 256, merge, 0)

    def col(off):
        return pl.BlockSpec((SEQ, LANES), lambda b, hp: (b, off + hp))

    tab = pl.BlockSpec((SEQ, LANES), lambda b, hp: (b, 0))
    out = pl.BlockSpec((SEQ, LANES), lambda b, hp: (b, hp))
    f32s = pltpu.VMEM((SEQ, LANES), F32)
    res, landed = call_hosting_exchange(
        body, xch, grid=(nb, n_pairs),
        in_specs=[col(0), col(n_pairs), col(2 * n_pairs), tab, tab, tab],
        out_specs=[out] * 11,
        out_shape=[SDS((t, A_WIDTH), F32)] * 2 + [SDS((t, A_WIDTH), BF16)] * 9,
        scratch_shapes=[f32s] * 11,
        name="a_attn_fwd", operands=(proj, proj, proj, ca, sa, sb))
    return res[:2], res[2:], landed


def a_attn_bwd(qkv_d, ca, sa, sb, dy, y, lse, nb, xch):
    t = dy.shape[0]
    n_pairs = A_WIDTH // LANES

    def body(*refs):
        qkv_refs = refs[:9]
        (c_ref, sa_ref, sb_ref, do_ref, y_ref, lse_ref, dq_ref, dk_ref, dv_ref,
         l0n_s, l1n_s, d0n_s, d1n_s, dod_s, l0d_s, l1d_s, d0d_s, d1d_s,
         dqc_s, dkc_s, dvc_s, dq4_s, dk4_s, dv4_s, dqn_s, dkn_s, dvn_s, tmp_s) = refs[9:]
        c, s_a, s_b = c_ref[...], sa_ref[...], sb_ref[...]
        head0 = lax.broadcasted_iota(jnp.int32, (Q_BLK, LANES), 1) < A_HEAD_DIM

        def per_head_rows(ci, carry):
            rows = pl.ds(pl.multiple_of(ci * 256, 256), 256)
            h0 = lax.broadcasted_iota(jnp.int32, (256, LANES), 1) < A_HEAD_DIM
            tt = do_ref[rows, :] * y_ref[rows, :]
            d0n_s[rows, :] = jnp.broadcast_to(jnp.sum(jnp.where(h0, tt, 0.0), axis=-1, keepdims=True), (256, LANES))
            d1n_s[rows, :] = jnp.broadcast_to(jnp.sum(jnp.where(h0, 0.0, tt), axis=-1, keepdims=True), (256, LANES))
            l = lse_ref[rows, :]
            lr = pltpu.roll(l, A_HEAD_DIM, 1)
            l0n_s[rows, :] = jnp.where(h0, l, lr)
            l1n_s[rows, :] = jnp.where(h0, lr, l)
            return carry

        lax.fori_loop(0, SEQ // 256, per_head_rows, 0)
        assert DILATIONS == (1, 4, 16)

        for g, d in enumerate(DILATIONS):
            qd_s, kd_s, vd_s = qkv_refs[3 * g:3 * g + 3]
            _deinterleave(do_ref, dod_s, d, BF16, tmp_s)
            if d > 1:
                for src, dst in ((l0n_s, l0d_s), (l1n_s, l1d_s), (d0n_s, d0d_s), (d1n_s, d1d_s)):
                    _deinterleave(src, dst, d, F32, tmp_s)
            l0, l1, d0, d1 = (l0n_s, l1n_s, d0n_s, d1n_s) if d == 1 else (l0d_s, l1d_s, d0d_s, d1d_s)
            dq_dst, dk_dst, dv_dst = {1: (dqn_s, dkn_s, dvn_s), 4: (dq4_s, dk4_s, dv4_s), 16: (dqc_s, dkc_s, dvc_s)}[d]
            dk_dst[...] = jnp.zeros_like(dk_dst)
            dv_dst[...] = jnp.zeros_like(dv_dst)

            def block(idx, carry, d=d, l0=l0, l1=l1, d0=d0, d1=d1, dq_dst=dq_dst, dk_dst=dk_dst, dv_dst=dv_dst,
                      qd_s=qd_s, kd_s=kd_s, vd_s=vd_s):
                q0, k0, kwin, valid = _block_coords(idx, d)
                qrows = pl.ds(q0, Q_BLK)
                krows = pl.ds(k0, kwin)
                qb, dob = qd_s[qrows, :], dod_s[qrows, :]
                kb, vb = kd_s[krows, :], vd_s[krows, :]
                zero = jnp.zeros_like(qb)
                q2 = jnp.concatenate([jnp.where(head0, qb, zero), jnp.where(head0, zero, qb)], 0)
                do2 = jnp.concatenate([jnp.where(head0, dob, zero), jnp.where(head0, zero, dob)], 0)
                wide = lambda x: jnp.concatenate([x] * (kwin // LANES), 1)
                lse2 = wide(jnp.concatenate([l0[qrows, :], l1[qrows, :]], 0))
                dd2 = wide(jnp.concatenate([d0[qrows, :], d1[qrows, :]], 0))
                s = jnp.where(jnp.concatenate([valid, valid], 0), _dot_nt(q2, kb), NEG_INF)
                p = jnp.exp(s - lse2)
                ds = (p * (_dot_nt(do2, vb) - dd2)).astype(BF16)
                dq2 = _dot(ds, kb)
                dq_dst[qrows, :] = jnp.where(head0, dq2[:Q_BLK], dq2[Q_BLK:])
                dk_dst[krows, :] += _dot_tn(ds, q2)
                dv_dst[krows, :] += _dot_tn(p.astype(BF16), do2)
                return carry

            lax.fori_loop(0, SEQ // Q_BLK, block, 0, unroll=UNROLL_BWD)

        for c16, c4, nat in ((dqc_s, dq4_s, dqn_s), (dkc_s, dk4_s, dkn_s), (dvc_s, dv4_s, dvn_s)):
            _class16_to_class4(c16, tmp_s)
            c4[...] = c4[...] + tmp_s[...]
            _interleave(c4, nat, 4, tmp_s, True)

        dq_ref[...] = _rope_bwd(dqn_s[...] * (A_HEAD_DIM ** -0.5), c, s_a, s_b, A_ROT // 2).astype(BF16)
        dk_ref[...] = _rope_bwd(dkn_s[...], c, s_a, s_b, A_ROT // 2).astype(BF16)
        dv_ref[...] = dvn_s[...].astype(BF16)

    tab = pl.BlockSpec((SEQ, LANES), lambda b, hp: (b, 0))
    blk = pl.BlockSpec((SEQ, LANES), lambda b, hp: (b, hp))
    f32s = pltpu.VMEM((SEQ, LANES), F32)
    b16s = pltpu.VMEM((SEQ, LANES), BF16)
    return call_hosting_exchange(
        body, xch, grid=(nb, n_pairs),
        in_specs=[blk] * 9 + [tab, tab, tab, blk, blk, blk],
        out_specs=[blk, blk, blk],
        out_shape=[SDS((t, A_WIDTH), BF16)] * 3,
        scratch_shapes=[f32s] * 4 + [b16s] + [f32s] * 14,
        name="a_attn_bwd", operands=(*qkv_d, ca, sa, sb, dy, y, lse))


MLA_SCALE = (MLA_NOPE + MLA_ROPE) ** -0.5
LOG2E = 1.4426950408889634
MLA_QW = MLA_HEADS * LANES
MLA_KVW = MLA_QW + MLA_WIDTH


def _rms(x, g):
    r = lax.rsqrt(jnp.mean(x * x, axis=-1, keepdims=True) + NORM_EPS)
    return x * r * g, r


def _rms_bwd(dn, x, r, g):
    tg = dn * g
    dx = r * tg - x * (r * r * r) * jnp.mean(tg * x, axis=-1, keepdims=True)
    return dx, jnp.sum(dn * x * r, axis=0, keepdims=True)


def mla_prep_fwd(proj, cm, sma, smb, g_cq, g_ckv, wuq, wkv):
    t = proj.shape[0]
    tm = 512

    def body(cq_ref, ckv_ref, kr_ref, c_ref, sa_ref, sb_ref, gq_ref, gkv_ref, wuq_ref, wkv_ref, q_ref, k_ref, v_ref):
        c, s_a, s_b = c_ref[...], sa_ref[...], sb_ref[...]
        cqn, _ = _rms(cq_ref[...], gq_ref[...])
        qf = _dot(cqn.astype(BF16), wuq_ref[...])
        ckvn, _ = _rms(ckv_ref[...], gkv_ref[...])
        kvf = _dot(ckvn.astype(BF16), wkv_ref[...])
        krope = _rope_fwd(kr_ref[...], c, s_a, s_b, MLA_ROPE // 2)
        for h in range(MLA_HEADS):
            cols = slice(h * LANES, (h + 1) * LANES)
            q_ref[:, cols] = (_rope_fwd(qf[:, cols], c, s_a, s_b, MLA_ROPE // 2) * (MLA_SCALE * LOG2E)).astype(BF16)
            k_ref[:, cols] = (kvf[:, cols] + krope).astype(BF16)
        v_ref[...] = kvf[:, MLA_QW:].astype(BF16)

    def row(w, j):
        return pl.BlockSpec((tm, w), lambda i: (i, j))

    def full(a):
        return pl.BlockSpec(a.shape, lambda i: (0, 0))

    return pl.pallas_call(
        body, grid=(t // tm,),
        in_specs=[row(256, 4096 // 256), row(128, 4352 // 128), row(128, 4480 // 128), row(128, 0), row(128, 0), row(128, 0),
                  full(g_cq), full(g_ckv), full(wuq), full(wkv)],
        out_specs=[row(MLA_QW, 0), row(MLA_QW, 0), row(MLA_WIDTH, 0)],
        out_shape=[SDS((t, MLA_QW), BF16), SDS((t, MLA_QW), BF16), SDS((t, MLA_WIDTH), BF16)],
        name="mla_prep_fwd", compiler_params=_params("parallel"))(proj, proj, proj, cm, sma, smb, g_cq, g_ckv, wuq, wkv)


def mla_prep_bwd(proj, cm, sma, smb, g_cq, g_ckv, wuq, wkv, dq, dk, dv):
    t = proj.shape[0]
    tm = 512

    def body(cq_ref, ckv_ref, c_ref, sa_ref, sb_ref, gq_ref, gkv_ref, wuq_ref, wkv_ref, dq_ref, dk_ref, dv_ref,
             dcc_ref, dqf_ref, cqn_ref, dkvf_ref, ckvn_ref, dgq_ref, dgkv_ref):
        @pl.when(pl.program_id(0) == 0)
        def _():
            dgq_ref[...] = jnp.zeros_like(dgq_ref)
            dgkv_ref[...] = jnp.zeros_like(dgkv_ref)

        c, s_a, s_b = c_ref[...], sa_ref[...], sb_ref[...]
        cq, ckv = cq_ref[...], ckv_ref[...]
        cqn, rq = _rms(cq, gq_ref[...])
        ckvn, rkv = _rms(ckv, gkv_ref[...])
        cqn_ref[...] = cqn.astype(BF16)
        ckvn_ref[...] = ckvn.astype(BF16)
        lane = lax.broadcasted_iota(jnp.int32, (tm, LANES), 1)
        rope_lanes = (lane >= MLA_NOPE) & (lane < MLA_NOPE + MLA_ROPE)
        dkrope = jnp.zeros((tm, LANES), F32)
        for h in range(MLA_HEADS):
            cols = slice(h * LANES, (h + 1) * LANES)
            dqf_ref[:, cols] = _rope_bwd(dq_ref[:, cols] * MLA_SCALE, c, s_a, s_b, MLA_ROPE // 2).astype(BF16)
            dkh = dk_ref[:, cols] * (1.0 / LOG2E)
            dkvf_ref[:, cols] = dkh.astype(BF16)
            dkrope = dkrope + dkh
        dkvf_ref[:, MLA_QW:] = dv_ref[...].astype(BF16)
        dkr = _rope_bwd(jnp.where(rope_lanes, dkrope, 0.0), c, s_a, s_b, MLA_ROPE // 2)
        dcqn = _dot_nt(dqf_ref[...], wuq_ref[...])
        dckvn = _dot_nt(dkvf_ref[...], wkv_ref[...])
        dcq, dgq = _rms_bwd(dcqn, cq, rq, gq_ref[...])
        dckv, dgkv = _rms_bwd(dckvn, ckv, rkv, gkv_ref[...])
        dgq_ref[...] += dgq
        dgkv_ref[...] += dgkv
        dcc_ref[:, 0:256] = dcq.astype(BF16)
        dcc_ref[:, 256:384] = dckv.astype(BF16)
        dcc_ref[:, 384:512] = dkr.astype(BF16)

    def row(w, j):
        return pl.BlockSpec((tm, w), lambda i: (i, j))

    def full(a):
        return pl.BlockSpec(a.shape, lambda i: (0, 0))

    return pl.pallas_call(
        body, grid=(t // tm,),
        in_specs=[row(256, 4096 // 256), row(128, 4352 // 128), row(128, 0), row(128, 0), row(128, 0),
                  full(g_cq), full(g_ckv), full(wuq), full(wkv), row(MLA_QW, 0), row(MLA_QW, 0), row(MLA_WIDTH, 0)],
        out_specs=[row(512, 0), row(MLA_QW, 0), row(256, 0), row(MLA_KVW, 0), row(128, 0), full(g_cq), full(g_ckv)],
        out_shape=[SDS((t, 512), BF16), SDS((t, MLA_QW), BF16), SDS((t, 256), BF16), SDS((t, MLA_KVW), BF16),
                   SDS((t, 128), BF16), SDS(g_cq.shape, F32), SDS(g_ckv.shape, F32)],
        name="mla_prep_bwd", compiler_params=_params("arbitrary"))(proj, proj, cm, sma, smb, g_cq, g_ckv, wuq, wkv, dq, dk, dv)


MLA_TQ = 256


def mla_attn_fwd(qb, kb, vb, nb):
    t = qb.shape[0]
    nq = SEQ // MLA_TQ
    n_pairs = MLA_HEADS // 2

    def body(q_ref, k_ref, v_ref, y_ref, lse_ref):
        head0 = lax.broadcasted_iota(jnp.int32, (MLA_TQ, LANES), 1) < MLA_V
        v = v_ref[...]
        vhead0 = lax.broadcasted_iota(jnp.int32, v.shape, 1) < MLA_V
        one = jnp.ones_like(v)
        outs, lses = [], []
        for h in range(2):
            cols = slice(h * LANES, (h + 1) * LANES)
            s = _dot_nt(q_ref[:, cols], k_ref[:, cols])
            m = jnp.max(s, axis=-1, keepdims=True)
            p = jnp.exp2(s - m).astype(BF16)
            ol = _dot(p, jnp.where(vhead0 == (h == 0), v, one))
            l = pltpu.roll(ol, MLA_V, 1)
            outs.append(ol / l)
            lses.append(m + jnp.log2(l))
        y_ref[...] = jnp.where(head0, outs[0], outs[1])
        lse_ref[...] = jnp.where(head0, lses[0], lses[1])

    return pl.pallas_call(
        body, grid=(nb, n_pairs, nq),
        in_specs=[pl.BlockSpec((MLA_TQ, 2 * LANES), lambda b, hp, i: (b * nq + i, hp)),
                  pl.BlockSpec((SEQ, 2 * LANES), lambda b, hp, i: (b, hp)),
                  pl.BlockSpec((SEQ, LANES), lambda b, hp, i: (b, hp))],
        out_specs=[pl.BlockSpec((MLA_TQ, LANES), lambda b, hp, i: (b * nq + i, hp))] * 2,
        out_shape=[SDS((t, MLA_WIDTH), F32)] * 2,
        name="mla_attn_fwd", compiler_params=_params("parallel", "parallel", "parallel"))(qb, kb, vb)


def mla_attn_bwd(qb, kb, vb, dy, y, lse, nb, xch):
    t = qb.shape[0]
    nq = SEQ // MLA_TQ
    n_pairs = MLA_HEADS // 2

    def body(q_ref, k_ref, v_ref, do_ref, y_ref, lse_ref, dq_ref, dk_ref, dv_ref):
        @pl.when(pl.program_id(2) == 0)
        def _():
            dk_ref[...] = jnp.zeros_like(dk_ref)
            dv_ref[...] = jnp.zeros_like(dv_ref)

        head0 = lax.broadcasted_iota(jnp.int32, (MLA_TQ, LANES), 1) < MLA_V
        v = v_ref[...]
        do = do_ref[...]
        lse = lse_ref[...]
        tt = do * y_ref[...]
        dv = jnp.zeros((SEQ, LANES), F32)
        for h in range(2):
            sel = head0 if h == 0 else ~head0
            lo = h * MLA_V
            cols = slice(h * LANES, (h + 1) * LANES)
            q = q_ref[:, cols]
            k = k_ref[:, cols]
            dd = jnp.sum(jnp.where(sel, tt, 0.0), axis=-1, keepdims=True)
            doh = jnp.where(sel, do, 0.0).astype(BF16)
            p = jnp.exp2(_dot_nt(q, k) - lse[:, lo:lo + 1])
            dp = _dot_nt(doh, v)
            ds = (p * (dp - dd)).astype(BF16)
            dq_ref[:, cols] = _dot(ds, k)
            dk_ref[:, cols] += _dot_tn(ds, q)
            dv = dv + _dot_tn(p.astype(BF16), doh)
        dv_ref[...] += dv

    qspec = pl.BlockSpec((MLA_TQ, 2 * LANES), lambda b, hp, i: (b * nq + i, hp))
    kspec = pl.BlockSpec((SEQ, 2 * LANES), lambda b, hp, i: (b, hp))
    vspec = pl.BlockSpec((SEQ, LANES), lambda b, hp, i: (b, hp))
    ospec = pl.BlockSpec((MLA_TQ, LANES), lambda b, hp, i: (b * nq + i, hp))
    return call_hosting_exchange(
        body, xch, grid=(nb, n_pairs, nq),
        in_specs=[qspec, kspec, vspec, ospec, ospec, ospec],
        out_specs=[qspec, kspec, vspec],
        out_shape=[SDS((t, MLA_QW), F32), SDS((t, MLA_QW), F32), SDS((t, MLA_WIDTH), F32)],
        scratch_shapes=[], name="mla_attn_bwd", operands=(qb, kb, vb, dy, y, lse))


MEM_TQ = 512
MEM_SCALE = MEM_HEAD_DIM ** -0.5
MQ_BLK4 = 5120 // MEM_WIDTH


def mem_attn_fwd(proj, mkv, nb):
    t = proj.shape[0]
    nq = SEQ // MEM_TQ

    def body(q_ref, mk_ref, mv_ref, y_ref):
        for h in range(MEM_HEADS):
            cols = slice(h * LANES, (h + 1) * LANES)
            s = _dot_nt(q_ref[:, cols].astype(BF16), mk_ref[:, cols]) * MEM_SCALE
            m = jnp.max(s, axis=-1, keepdims=True)
            p = jnp.exp(s - m)
            l = jnp.sum(p, axis=-1, keepdims=True)
            y_ref[:, cols] = _dot(p.astype(BF16), mv_ref[:, cols]) / l

    return pl.pallas_call(
        body, grid=(nb, nq),
        in_specs=[pl.BlockSpec((MEM_TQ, MEM_WIDTH), lambda b, i: (b * nq + i, MQ_BLK4)),
                  pl.BlockSpec((N_MEM, MEM_WIDTH), lambda b, i: (b, 0)),
                  pl.BlockSpec((N_MEM, MEM_WIDTH), lambda b, i: (b, 1))],
        out_specs=pl.BlockSpec((MEM_TQ, MEM_WIDTH), lambda b, i: (b * nq + i, 0)),
        out_shape=SDS((t, MEM_WIDTH), F32),
        name="mem_attn_fwd", compiler_params=_params("parallel", "parallel"))(proj, mkv, mkv)


def mem_attn_bwd(proj, mkv, dy, nb):
    t = proj.shape[0]
    nq = SEQ // MEM_TQ

    def body(q_ref, mk_ref, mv_ref, do_ref, dq_ref, dmk_ref, dmv_ref):
        @pl.when(pl.program_id(1) == 0)
        def _():
            dmk_ref[...] = jnp.zeros_like(dmk_ref)
            dmv_ref[...] = jnp.zeros_like(dmv_ref)

        for h in range(MEM_HEADS):
            cols = slice(h * LANES, (h + 1) * LANES)
            q = q_ref[:, cols].astype(BF16)
            mk, mv = mk_ref[:, cols], mv_ref[:, cols]
            do = do_ref[:, cols].astype(BF16)
            s = _dot_nt(q, mk) * MEM_SCALE
            e = jnp.exp(s - jnp.max(s, axis=-1, keepdims=True))
            p = e / jnp.sum(e, axis=-1, keepdims=True)
            dp = _dot_nt(do, mv)
            ds = (p * (dp - jnp.sum(p * dp, axis=-1, keepdims=True)) * MEM_SCALE).astype(BF16)
            dq_ref[:, cols] = _dot(ds, mk).astype(BF16)
            dmk_ref[:, cols] += _dot_tn(ds, q)
            dmv_ref[:, cols] += _dot_tn(p.astype(BF16), do)

    ospec = pl.BlockSpec((MEM_TQ, MEM_WIDTH), lambda b, i: (b * nq + i, 0))
    kspec = pl.BlockSpec((N_MEM, MEM_WIDTH), lambda b, i: (b, 0))
    return pl.pallas_call(
        body, grid=(nb, nq),
        in_specs=[pl.BlockSpec((MEM_TQ, MEM_WIDTH), lambda b, i: (b * nq + i, MQ_BLK4)),
                  kspec, pl.BlockSpec((N_MEM, MEM_WIDTH), lambda b, i: (b, 1)), ospec],
        out_specs=[ospec, kspec, kspec],
        out_shape=[SDS((t, MEM_WIDTH), BF16), SDS((nb * N_MEM, MEM_WIDTH), F32), SDS((nb * N_MEM, MEM_WIDTH), F32)],
        name="mem_attn_bwd", compiler_params=_params("parallel", "arbitrary"))(proj, mkv, mkv, dy)


ROW_TM = 512
AG_BLK = 3072 // 1024
BG_BLK = 4608 // 512
MG_BLK = 5632 // 512
GROUPS = ((0, A_WIDTH), (A_WIDTH, MLA_WIDTH), (A_WIDTH + MLA_WIDTH, MEM_WIDTH))
D_MIX = 2048


def _gate_specs():
    def row(w, j):
        return pl.BlockSpec((ROW_TM, w), lambda i: (i, j))

    def vec(w):
        return pl.BlockSpec((1, w), lambda i: (0, 0))

    ys = [row(A_WIDTH, 0), row(MLA_WIDTH, 0), row(MEM_WIDTH, 0)]
    gates = [row(A_WIDTH, AG_BLK), row(MLA_WIDTH, BG_BLK), row(MEM_WIDTH, MG_BLK)]
    gains = [vec(A_WIDTH), vec(MLA_WIDTH), vec(MEM_WIDTH)]
    return row, vec, ys, gates, gains


def gate_out_ln_loss(ya, yb, ym, proj, goa, gob, gom, wout, h32, target, gp, bp):
    t, d = h32.shape
    _, _, ys, gates, gains = _gate_specs()

    def body(ya_ref, yb_ref, ym_ref, ga_ref, gb_ref, gm_ref, goa_ref, gob_ref, gom_ref, w_ref, h_ref, t_ref, gp_ref, bp_ref,
             z_ref, du32_ref, du16_ref, loss_ref, dgp_ref, dbp_ref):
        @pl.when(pl.program_id(0) == 0)
        def _():
            loss_ref[...] = jnp.zeros_like(loss_ref)
            dgp_ref[...] = jnp.zeros_like(dgp_ref)
            dbp_ref[...] = jnp.zeros_like(dbp_ref)

        for (off, w), y_ref, g_ref, go_ref in zip(GROUPS, (ya_ref, yb_ref, ym_ref), (ga_ref, gb_ref, gm_ref),
                                                  (goa_ref, gob_ref, gom_ref)):
            n, _ = _rms(y_ref[...], go_ref[...])
            gt = g_ref[...]
            z_ref[:, off:off + w] = (n * (gt * _sigmoid(gt))).astype(BF16)
        g = gp_ref[...]
        u = ALPHA * h_ref[...] + _dot(z_ref[...], w_ref[...])
        mu = jnp.mean(u, axis=-1, keepdims=True)
        uc = u - mu
        rstd = lax.rsqrt(jnp.mean(uc * uc, axis=-1, keepdims=True) + NORM_EPS)
        xhat = uc * rstd
        err = xhat * g + bp_ref[...] - t_ref[...]
        tok = jnp.sum(err * err, axis=-1, keepdims=True) * (1.0 / d)
        loss_ref[...] += 0.5 * jnp.sum(tok, axis=0, keepdims=True)
        dout = err * (1.0 / d)
        dxhat = dout * g
        du = rstd * (dxhat - jnp.mean(dxhat, axis=-1, keepdims=True)
                     - xhat * jnp.mean(dxhat * xhat, axis=-1, keepdims=True))
        du32_ref[...] = du
        du16_ref[...] = du.astype(BF16)
        dgp_ref[...] += jnp.sum(dout * xhat, axis=0, keepdims=True)
        dbp_ref[...] += jnp.sum(dout, axis=0, keepdims=True)

    row = pl.BlockSpec((ROW_TM, d), lambda i: (i, 0))
    vec = pl.BlockSpec((1, d), lambda i: (0, 0))
    zrow = pl.BlockSpec((ROW_TM, D_MIX), lambda i: (i, 0))
    return pl.pallas_call(
        body, grid=(t // ROW_TM,),
        in_specs=ys + gates + gains + [pl.BlockSpec((D_MIX, d), lambda i: (0, 0)), row, row, vec, vec],
        out_specs=[zrow, row, row, pl.BlockSpec((1, LANES), lambda i: (0, 0)), vec, vec],
        out_shape=[SDS((t, D_MIX), BF16), SDS((t, d), F32), SDS((t, d), BF16), SDS((1, LANES), F32), SDS((1, d), F32),
                   SDS((1, d), F32)],
        name="gate_out_ln_loss", compiler_params=_params("arbitrary"))(
            ya, yb, ym, proj, proj, proj, goa, gob, gom, wout, h32, target, gp, bp)


def gate_bwd(du16, wout, ya, yb, ym, proj, goa, gob, gom):
    t = ya.shape[0]
    row, vec, ys, gates, gains = _gate_specs()

    def body(du_ref, w_ref, ya_ref, yb_ref, ym_ref, ga_ref, gb_ref, gm_ref, goa_ref, gob_ref, gom_ref,
             dya_ref, dyb_ref, dym_ref, dga_ref, dgb_ref, dgm_ref, dgoa_ref, dgob_ref, dgom_ref):
        @pl.when(pl.program_id(0) == 0)
        def _():
            dgoa_ref[...] = jnp.zeros_like(dgoa_ref)
            dgob_ref[...] = jnp.zeros_like(dgob_ref)
            dgom_ref[...] = jnp.zeros_like(dgom_ref)

        dz = _dot_nt(du_ref[...], w_ref[...])
        for (off, w), y_ref, g_ref, go_ref, dy_ref, dg_ref, dgo_ref in zip(
                GROUPS, (ya_ref, yb_ref, ym_ref), (ga_ref, gb_ref, gm_ref), (goa_ref, gob_ref, gom_ref),
                (dya_ref, dyb_ref, dym_ref), (dga_ref, dgb_ref, dgm_ref), (dgoa_ref, dgob_ref, dgom_ref)):
            dzg = dz[:, off:off + w]
            y, gt, go = y_ref[...], g_ref[...], go_ref[...]
            n, r = _rms(y, go)
            sg = _sigmoid(gt)
            dg_ref[...] = (dzg * n * (sg * (1.0 + gt * (1.0 - sg)))).astype(BF16)
            dy, dgo = _rms_bwd(dzg * (gt * sg), y, r, go)
            dy_ref[...] = dy
            dgo_ref[...] += dgo

    widths = (A_WIDTH, MLA_WIDTH, MEM_WIDTH)
    return pl.pallas_call(
        body, grid=(t // ROW_TM,),
        in_specs=[row(D_MODEL, 0), pl.BlockSpec((D_MIX, D_MODEL), lambda i: (0, 0))] + ys + gates + gains,
        out_specs=[row(w, 0) for w in widths] * 2 + [vec(w) for w in widths],
        out_shape=[SDS((t, w), F32) for w in widths] + [SDS((t, w), BF16) for w in widths] + [SDS((1, w), F32) for w in widths],
        name="gate_bwd", compiler_params=_params("arbitrary"))(du16, wout, ya, yb, ym, proj, proj, proj, goa, gob, gom)


def dh_ln_bwd(pieces, win_t, du32, x2, g_emb, xch):
    t, d = x2.shape

    def body(*refs):
        p_refs = refs[:len(pieces)]
        w_ref, du_ref, x_ref, g_ref, dx_ref, dg_ref, db_ref = refs[len(pieces):]

        @pl.when(pl.program_id(0) == 0)
        def _():
            dg_ref[...] = jnp.zeros_like(dg_ref)
            db_ref[...] = jnp.zeros_like(db_ref)

        dh = ALPHA * du_ref[...]
        for p_ref, off, w in zip(p_refs, PIECE_OFFS, PIECE_WIDTHS):
            dh = dh + _dot(p_ref[...], w_ref[off:off + w, :])
        x = x_ref[...]
        xc = x - jnp.mean(x, axis=-1, keepdims=True)
        rstd = lax.rsqrt(jnp.mean(xc * xc, axis=-1, keepdims=True) + NORM_EPS)
        xhat = xc * rstd
        dg_ref[...] += jnp.sum(dh * xhat, axis=0, keepdims=True)
        db_ref[...] += jnp.sum(dh, axis=0, keepdims=True)
        tg = dh * g_ref[...]
        dx_ref[...] = rstd * (tg - jnp.mean(tg, axis=-1, keepdims=True)
                              - xhat * jnp.mean(tg * xhat, axis=-1, keepdims=True))

    row = pl.BlockSpec((ROW_TM, d), lambda i: (i, 0))
    vec = pl.BlockSpec((1, d), lambda i: (0, 0))
    return call_hosting_exchange(
        body, xch, grid=(t // ROW_TM,),
        in_specs=[pl.BlockSpec((ROW_TM, w), lambda i: (i, 0)) for w in PIECE_WIDTHS]
        + [pl.BlockSpec(win_t.shape, lambda i: (0, 0)), row, row, vec],
        out_specs=[row, vec, vec],
        out_shape=[SDS((t, d), F32), SDS((1, d), F32), SDS((1, d), F32)],
        scratch_shapes=[], name="dh_ln_bwd", operands=(*pieces, win_t, du32, x2, g_emb))


def _adamw(w, g, m, v):
    m2 = ADAM_B1 * m + (1.0 - ADAM_B1) * g
    v2 = ADAM_B2 * v + (1.0 - ADAM_B2) * (g * g)
    m_hat = m2 / (1.0 - ADAM_B1 ** ADAM_STEP)
    v_hat = v2 / (1.0 - ADAM_B2 ** ADAM_STEP)
    return -ADAM_LR * (m_hat / (jnp.sqrt(v_hat) + ADAM_EPS) + ADAM_WD * w), m2, v2


def adamw_shard(w, parts, m, v, name):
    r, c = w.shape
    if r % 256 == 0 or r * c <= 256 * 1024:
        tr, tc = min(r, 256), c
    else:
        tr, tc = r, 256

    def body(w_ref, p_ref, m_ref, v_ref, g_ref, d_ref, nm_ref, nv_ref):
        g = p_ref[0].astype(F32)
        for k in range(1, N_DEV):
            g = g + p_ref[k].astype(F32)
        g_ref[...] = g
        d_ref[...], nm_ref[...], nv_ref[...] = _adamw(w_ref[...], g, m_ref[...], v_ref[...])

    blk = pl.BlockSpec((tr, tc), lambda i, j: (i, j))
    return pl.pallas_call(
        body, grid=(r // tr, c // tc),
        in_specs=[blk, pl.BlockSpec((N_DEV, tr, tc), lambda i, j: (0, i, j)), blk, blk],
        out_specs=[blk] * 4, out_shape=[SDS((r, c), F32)] * 4, name=name,
        compiler_params=_params("parallel", "parallel"))(w, parts, m, v)


def _place():
    return lax.axis_index("x"), lax.axis_index("y"), lax.axis_index("c")


def _flat(px, py, pc):
    return 4 * px + 2 * py + pc


def _peer(x, y, c, k):
    return (1 - x if k & 4 else x, 1 - y if k & 2 else y, 1 - c if k & 1 else c)


def cast_shards(shards):
    def body(*refs):
        n = len(refs) // 2
        for i_ref, o_ref in zip(refs[:n], refs[n:]):
            o_ref[...] = i_ref[...].astype(BF16)

    return pl.pallas_call(body, out_shape=[SDS(s.shape, BF16) for s in shards], name="cast_shards",
                          compiler_params=_params())(*shards)


def _two_level_gather_plan(src_refs, land_refs, send_sems, recv_sems, local_sems):
    n = len(src_refs)
    x, y, c = _place()
    me, sib = (x, y, c), (x, y, 1 - c)
    chips = [(1 - x, y), (x, 1 - y), (1 - x, 1 - y)]

    def copy(a, k, block, to, src=None):
        dst = land_refs[a].at[_flat(*block)]
        return pltpu.make_async_remote_copy(
            src_ref=dst if src is None else src, dst_ref=dst,
            send_sem=send_sems.at[a * N_DEV + k], recv_sem=recv_sems.at[a * N_DEV + k],
            device_id=to, device_id_type=MESH)

    mine = [pltpu.make_async_copy(src_refs[a], land_refs[a].at[_flat(*me)], local_sems.at[a]) for a in range(n)]
    first = []
    for a in range(n):
        first.append(copy(a, 0, me, sib, src=src_refs[a]))
        first += [copy(a, 1 + j, me, (*chip, c), src=src_refs[a]) for j, chip in enumerate(chips)]

    def start():
        for cp in mine + first:
            cp.start()

    def finish():
        passed = []
        for j, chip in enumerate(chips):
            for a in range(n):
                copy(a, 1 + j, (*chip, c), me).wait_recv()
                fwd = copy(a, 4 + j, (*chip, c), sib)
                fwd.start()
                passed.append(fwd)
        for a in range(n):
            copy(a, 0, sib, me).wait_recv()
            for j, chip in enumerate(chips):
                copy(a, 4 + j, (*chip, 1 - c), me).wait_recv()
        for cp in first + passed:
            cp.wait_send()
        for cp in mine:
            cp.wait()

    return start, finish


ALL_DEVICES = tuple(range(N_DEV))


def _exchange_plan(src_refs, land_refs, dests, send_sems, recv_sems, local_sems):
    x, y, c = _place()
    me = _flat(x, y, c)
    plan = []
    for a, (src, land, dl) in enumerate(zip(src_refs, land_refs, dests)):
        for li, j in enumerate(dl):
            to = ((j >> 2) & 1, (j >> 1) & 1, j & 1)
            block = src.at[li] if len(src.shape) == len(land.shape) else src

            def push(slot, a=a, block=block, land=land, j=j, to=to):
                return pltpu.make_async_remote_copy(
                    src_ref=block, dst_ref=land.at[slot], send_sem=send_sems.at[a * N_DEV + j],
                    recv_sem=recv_sems.at[a * N_DEV + slot], device_id=to, device_id_type=MESH)

            own = pltpu.make_async_copy(block, land.at[j], local_sems.at[a])
            plan.append((j, push(me), own, [push(s) for s in range(N_DEV) if s != j]))
    return me, plan


def _exchange_start(me, plan):
    for j, send, own, _ in plan:
        @pl.when(me != j)
        def _(send=send):
            send.start()

        @pl.when(me == j)
        def _(own=own):
            own.start()


def _exchange_wait(me, plan):
    for j, send, own, arrivals in plan:
        @pl.when(me != j)
        def _(send=send):
            send.wait_send()

        @pl.when(me == j)
        def _(own=own, arrivals=arrivals):
            own.wait()
            for arrival in arrivals:
                arrival.wait_recv()


def call_hosting_exchange(core, xch, *, grid, in_specs, out_specs, out_shape, scratch_shapes, name, operands):
    srcs, dests, landing = xch
    n, n_in, n_out, n_scr = len(srcs), len(in_specs), len(out_specs), len(scratch_shapes)

    def body(*refs):
        ins, src_refs = refs[:n_in], refs[n_in:n_in + n]
        outs = refs[n_in + 2 * n:n_in + 2 * n + n_out]
        land_refs = refs[n_in + 2 * n + n_out:n_in + 3 * n + n_out]
        scratch = refs[n_in + 3 * n + n_out:n_in + 3 * n + n_out + n_scr]
        sems = refs[n_in + 3 * n + n_out + n_scr:]
        first = functools.reduce(jnp.logical_and, [pl.program_id(i) == 0 for i in range(len(grid))])
        last = functools.reduce(jnp.logical_and, [pl.program_id(i) == grid[i] - 1 for i in range(len(grid))])
        if dests is None:
            start, finish = _two_level_gather_plan(src_refs, land_refs, *sems)
        else:
            me, plan = _exchange_plan(src_refs, land_refs, dests, *sems)
            start, finish = functools.partial(_exchange_start, me, plan), functools.partial(_exchange_wait, me, plan)
        pl.when(first)(start)
        core(*ins, *outs, *scratch)
        pl.when(last)(finish)

    hbm = pl.BlockSpec(memory_space=pl.ANY)
    res = pl.pallas_call(
        body, grid=grid,
        in_specs=list(in_specs) + [hbm] * (2 * n), out_specs=list(out_specs) + [hbm] * n,
        out_shape=list(out_shape) + [SDS(l.shape, l.dtype) for l in landing],
        scratch_shapes=list(scratch_shapes) + [pltpu.SemaphoreType.DMA((N_DEV * n,)), pltpu.SemaphoreType.DMA((N_DEV * n,)),
                                               pltpu.SemaphoreType.DMA((n,))],
        input_output_aliases={n_in + n + k: n_out + k for k in range(n)},
        name=name, compiler_params=_params(*(("arbitrary",) * len(grid))))(*operands, *srcs, *landing)
    return res[:n_out], res[n_out:]


SLOT_ROWS = 8


def small_allreduce_adamw(loss_sum, grads, ws, ms, vs):
    n = len(grads)
    rows = [g.shape[0] for g in grads]
    total = SLOT_ROWS * (n + 1)

    def body(*refs):
        loss_ref, g_refs, w_refs = refs[0], refs[1:1 + n], refs[1 + n:1 + 2 * n]
        m_refs, v_refs = refs[1 + 2 * n:1 + 3 * n], refs[1 + 3 * n:1 + 4 * n]
        outs = refs[1 + 4 * n:2 + 8 * n]
        vec, gath, tot, send_sems, recv_sems = refs[2 + 8 * n:]
        x, y, c = _place()
        me = _flat(x, y, c)
        vec[...] = jnp.zeros_like(vec)
        vec[0:1, :] = loss_ref[...]
        for i in range(n):
            vec[SLOT_ROWS * (i + 1):SLOT_ROWS * (i + 1) + rows[i], :] = g_refs[i][...]
        gath[me] = vec[...]
        copies = []
        for k in range(1, N_DEV):
            peer = _peer(x, y, c, k)
            copies.append(pltpu.make_async_remote_copy(
                src_ref=vec, dst_ref=gath.at[me], send_sem=send_sems.at[k - 1], recv_sem=recv_sems.at[k - 1],
                device_id=peer, device_id_type=MESH))
        for cp in copies:
            cp.start()
        for cp in copies:
            cp.wait_recv()
        for cp in copies:
            cp.wait_send()
        g = gath[0]
        for j in range(1, N_DEV):
            g = g + gath[j]
        tot[...] = g
        outs[0][...] = tot[0:1, :]
        for i in range(n):
            gi = tot[SLOT_ROWS * (i + 1):SLOT_ROWS * (i + 1) + rows[i], :]
            outs[1 + i][...] = gi
            outs[1 + n + i][...], outs[1 + 2 * n + i][...], outs[1 + 3 * n + i][...] = _adamw(
                w_refs[i][...], gi, m_refs[i][...], v_refs[i][...])

    shapes = [SDS(g.shape, F32) for g in grads]
    return pl.pallas_call(
        body, out_shape=[SDS((1, LANES), F32)] + shapes * 4,
        scratch_shapes=[pltpu.VMEM((total, LANES), F32), pltpu.VMEM((N_DEV, total, LANES), F32), pltpu.VMEM((total, LANES), F32),
                        pltpu.SemaphoreType.DMA((7,)), pltpu.SemaphoreType.DMA((7,))],
        name="small_allreduce_adamw", compiler_params=_params())(loss_sum, *grads, *ws, *ms, *vs)


def _rope_lane_patterns():
    inv = lambda r: ROPE_THETA ** (-(jnp.arange(0, r, 2, dtype=F32) / r))
    z = lambda n: jnp.zeros((n,), F32)
    o = lambda n: jnp.ones((n,), F32)
    half, rest = A_ROT // 2, A_HEAD_DIM - A_ROT
    ia, im = inv(A_ROT), inv(MLA_ROPE)
    mh, tail = MLA_ROPE // 2, LANES - MLA_NOPE - MLA_ROPE
    rows = [jnp.tile(jnp.concatenate([ia, ia, z(rest)]), 2),
            jnp.tile(jnp.concatenate([o(half), z(half + rest)]), 2),
            jnp.tile(jnp.concatenate([z(half), o(half), z(rest)]), 2),
            jnp.concatenate([z(MLA_NOPE), im, im, z(tail)]),
            jnp.concatenate([z(MLA_NOPE), o(mh), z(mh + tail)]),
            jnp.concatenate([z(MLA_NOPE + mh), o(mh), z(tail)]),
            z(LANES), z(LANES)]
    return jnp.stack(rows)


KR_LO, KR_HI = 4480, 4512
W_IN_SHARD = D_IN // N_DEV
BG_SPLIT = 6 * W_IN_SHARD - KR_HI


def w_in_working_t(g):
    pad_lo, pad_hi = MLA_NOPE, LANES - MLA_NOPE - MLA_ROPE
    spans = []
    for lo, hi, shift in ((0, KR_LO, 0), (KR_LO, KR_HI, pad_lo), (KR_HI, D_IN, pad_lo + pad_hi)):
        r = lo
        while r < hi:
            j = r // W_IN_SHARD
            n = min(hi, (j + 1) * W_IN_SHARD) - r
            spans.append((j, r - j * W_IN_SHARD, n, r + shift))
            r += n

    def body(g_ref, o_ref):
        o_ref[KR_LO:KR_LO + pad_lo, :] = jnp.zeros((pad_lo, D_MODEL), o_ref.dtype)
        o_ref[KR_HI + pad_lo:KR_HI + pad_lo + pad_hi, :] = jnp.zeros((pad_hi, D_MODEL), o_ref.dtype)
        for j, src, n, dst in spans:
            o_ref[dst:dst + n, :] = g_ref[j, src:src + n, :]

    return pl.pallas_call(body, out_shape=SDS((D_INW, D_MODEL), g.dtype), name="w_in_working_t", compiler_params=_params())(g)


def _w_in_shard_5(d_ag_tail, d_cc, d_bg_head):
    kr = MLA_Q_RANK + MLA_KV_RANK + MLA_NOPE
    rows = jnp.concatenate([d_ag_tail, d_cc[:MLA_Q_RANK + MLA_KV_RANK], d_cc[kr:kr + MLA_ROPE], d_bg_head], 0)
    return rows.reshape(1, W_IN_SHARD, D_MODEL).astype(BF16)


def _w_uq_working(g):
    w = jnp.pad(g.transpose(1, 0, 2), ((0, 0), (0, 0), (0, LANES - MLA_NOPE - MLA_ROPE)))
    return w.reshape(MLA_Q_RANK, MLA_QW)


def _w_uq_parts(dw):
    return dw.reshape(MLA_Q_RANK, MLA_HEADS, LANES)[:, :, :MLA_NOPE + MLA_ROPE].transpose(1, 0, 2)


def _w_ukv_working(g):
    wk = jnp.pad(g[:, :, :MLA_NOPE].transpose(1, 0, 2), ((0, 0), (0, 0), (0, LANES - MLA_NOPE)))
    wv = g[:, :, MLA_NOPE:].transpose(1, 0, 2)
    return jnp.concatenate([wk.reshape(MLA_KV_RANK, MLA_QW), wv.reshape(MLA_KV_RANK, MLA_WIDTH)], 1)


def _w_ukv_parts(dw):
    dk = dw[:, :MLA_QW].reshape(MLA_KV_RANK, MLA_HEADS, LANES)[:, :, :MLA_NOPE]
    dv = dw[:, MLA_QW:].reshape(MLA_KV_RANK, MLA_HEADS, MLA_V)
    return jnp.concatenate([dk, dv], -1).transpose(1, 0, 2)


SMALL_NAMES = ("g_emb", "b_emb", "g_cq", "g_ckv", "g_out_a", "g_out_b", "g_out_m", "g_post", "b_post")


def kernel(x, mem, positions, g_emb, b_emb, w_in, g_cq, g_ckv, w_uq, w_ukv, w_mem_kv, g_out_a, g_out_b, g_out_m, w_out, g_post, b_post, loss_target, m_g_emb, m_b_emb, m_w_in, m_g_cq, m_g_ckv, m_w_uq, m_w_ukv, m_w_mem_kv, m_g_out_a, m_g_out_b, m_g_out_m, m_w_out, m_g_post, m_b_post, v_g_emb, v_b_emb, v_w_in, v_g_cq, v_g_ckv, v_w_uq, v_w_ukv, v_w_mem_kv, v_g_out_a, v_g_out_b, v_g_out_m, v_w_out, v_g_post, v_b_post):
    nb = x.shape[0]
    t = nb * SEQ
    x2 = x.reshape(t, D_MODEL)
    tgt2 = loss_target.reshape(t, D_MODEL)
    mem2 = mem.reshape(nb * N_MEM, D_MODEL)
    g_emb2, b_emb2 = g_emb.reshape(1, -1), b_emb.reshape(1, -1)

    w_in_t, m_w_in_t, v_w_in_t = w_in[0].T, m_w_in[0].T, v_w_in[0].T
    s_in, s_uq, s_ukv, s_mem, s_out = cast_shards((w_in_t, w_uq[0], w_ukv[0], w_mem_kv[0], w_out[0]))
    (h32, h16, (a_c, a_sa, a_sb), (m_c, m_sa, m_sb)), (g_in,) = embed_fwd(
        x2, g_emb2, b_emb2, positions, ((s_in,), None, (lax.empty((N_DEV,) + s_in.shape, BF16),)))
    win_t = w_in_working_t(g_in)

    proj = mm_nn(h16, win_t, F32, 512, 1536, "proj", rhs_transposed=True)
    later = (s_uq, s_ukv, s_mem, s_out)
    (ya, lse_a), qkv_d, (g_uq, g_ukv, g_mem, g_out) = a_attn_fwd(
        proj, a_c, a_sa, a_sb, nb,
        (later, (ALL_DEVICES,) * len(later), tuple(lax.empty((N_DEV,) + w.shape, BF16) for w in later)))
    wuq_w = _w_uq_working(g_uq)
    wkv_w = _w_ukv_working(g_ukv)
    wmem = g_mem.reshape(D_MODEL, 2 * MEM_WIDTH)
    wout = g_out.reshape(D_MIX, D_MODEL)
    qb, kb, vb = mla_prep_fwd(proj, m_c, m_sa, m_sb, g_cq, g_ckv, wuq_w, wkv_w)
    yb, lse_b = mla_attn_fwd(qb, kb, vb, nb)
    mkv = mm_nn(mem2, wmem, BF16, nb * N_MEM, 512, "mem_kv")
    ym = mem_attn_fwd(proj, mkv, nb)
    z, du32, du16, loss_sum, dg_post, db_post = gate_out_ln_loss(
        ya, yb, ym, proj, g_out_a, g_out_b, g_out_m, wout, h32, tgt2, g_post, b_post)

    dya, dyb, dym, dag, dbg, dmg, dg_out_a, dg_out_b, dg_out_m = gate_bwd(
        du16, wout, ya, yb, ym, proj, g_out_a, g_out_b, g_out_m)
    dw_out = mm_tn(z, du16, 1024, "dw_out")
    dmq, dmk, dmv = mem_attn_bwd(proj, mkv, dym, nb)
    dw_mem = mm_tn(mem2, jnp.concatenate([dmk, dmv], 1), nb * N_MEM, "dw_mem")
    d_gates, shards_6_7 = mm_tn_group((dbg, dmq, dmg), h16, 2048, "dw_in_bg_mq_mg", W_IN_SHARD, BG_SPLIT, 2)
    landing = lambda w, dtype=F32: lax.empty((N_DEV,) + w.shape, dtype)
    big_w = (w_in_t, w_uq[0], w_ukv[0], w_mem_kv[0], w_out[0])
    (daq, dak, dav), (p_out, p_mem, p_in) = a_attn_bwd(
        qkv_d, a_c, a_sa, a_sb, dya, ya, lse_a, nb,
        ((dw_out.reshape(N_DEV, D_MIX // N_DEV, D_MODEL), dw_mem.reshape(N_DEV, D_MODEL // N_DEV, 2 * MEM_WIDTH),
          shards_6_7),
         (ALL_DEVICES, ALL_DEVICES, (6, 7)),
         (landing(w_out[0]), landing(w_mem_kv[0]), landing(w_in_t, BF16))))
    d_a, shards_0_4 = mm_tn_group((daq, dak, dav, dag), h16, 1024, "dw_in_aq_ak_av_ag", W_IN_SHARD, 0, 5)
    (dqb, dkb, dvb), (p_in,) = mla_attn_bwd(
        qb, kb, vb, dyb, yb, lse_b, nb, ((shards_0_4,), ((0, 1, 2, 3, 4),), (p_in,)))
    dcc, dqf, cqn, dkvf, ckvn, dg_cq, dg_ckv = mla_prep_bwd(proj, m_c, m_sa, m_sb, g_cq, g_ckv, wuq_w, wkv_w, dqb, dkb, dvb)
    dw_uq = mm_tn(cqn, dqf, 2048, "dw_uq")
    dw_ukv = mm_tn(ckvn, dkvf, 2048, "dw_ukv")
    d_cc = mm_tn(dcc, h16, 2048, "dw_in_cc")
    pieces = (daq, dak, dav, dag, dcc, dbg, dmq, dmg)
    (grad_x, dg_emb, db_emb), (p_in, p_uq, p_ukv) = dh_ln_bwd(
        pieces, win_t, du32, x2, g_emb2,
        ((_w_in_shard_5(d_a[5 * W_IN_SHARD:], d_cc, d_gates[:BG_SPLIT]), _w_uq_parts(dw_uq), _w_ukv_parts(dw_ukv)),
         ((5,), ALL_DEVICES, ALL_DEVICES),
         (p_in, landing(w_uq[0]), landing(w_ukv[0]))))

    parts = (p_in, p_uq, p_ukv, p_mem, p_out)
    big_m = (m_w_in_t, m_w_uq[0], m_w_ukv[0], m_w_mem_kv[0], m_w_out[0])
    big_v = (v_w_in_t, v_w_uq[0], v_w_ukv[0], v_w_mem_kv[0], v_w_out[0])
    big = {}
    for name, w, p, m, v in zip(("w_in", "w_uq", "w_ukv", "w_mem_kv", "w_out"), big_w, parts, big_m, big_v):
        res = adamw_shard(w, p, m, v, "adamw_" + name)
        big[name] = [(o.T if name == "w_in" else o)[None] for o in res]

    small_w = (g_emb, b_emb, g_cq, g_ckv, g_out_a, g_out_b, g_out_m, g_post, b_post)
    small_m = (m_g_emb, m_b_emb, m_g_cq, m_g_ckv, m_g_out_a, m_g_out_b, m_g_out_m, m_g_post, m_b_post)
    small_v = (v_g_emb, v_b_emb, v_g_cq, v_g_ckv, v_g_out_a, v_g_out_b, v_g_out_m, v_g_post, v_b_post)
    small_g = (dg_emb, db_emb, dg_cq, dg_ckv, dg_out_a, dg_out_b, dg_out_m, dg_post, db_post)
    rows128 = lambda vals: [v.reshape(-1, LANES) for v in vals]
    res = small_allreduce_adamw(loss_sum, rows128(small_g), rows128(small_w), rows128(small_m), rows128(small_v))
    loss = res[0][0, 0]
    n_small = len(small_w)
    sg, sd, sm, sv = [[r.reshape(w.shape) for r, w in zip(res[1 + k * n_small:1 + (k + 1) * n_small], small_w)]
                      for k in range(4)]

    order = ("g_emb", "b_emb", "w_in", "g_cq", "g_ckv", "w_uq", "w_ukv", "w_mem_kv", "g_out_a", "g_out_b", "g_out_m",
             "w_out", "g_post", "b_post")
    small_idx = {n: i for i, n in enumerate(SMALL_NAMES)}
    outs = [loss, grad_x.reshape(x.shape)]
    for kind in range(4):
        for name in order:
            outs.append(big[name][kind] if name in big else (sg, sd, sm, sv)[kind][small_idx[name]])
    return tuple(outs)
```

```python
import functools

import jax
import jax.numpy as jnp
from jax import lax
from jax.experimental import pallas as pl
from jax.experimental.pallas import tpu as pltpu

F32 = jnp.float32
BF16 = jnp.bfloat16
SDS = jax.ShapeDtypeStruct
MESH = pl.DeviceIdType.MESH

D_MODEL = 1024
SEQ = 2048
A_HEADS, A_HEAD_DIM, A_ROT = 16, 64, 16
A_WIDTH = 1024
DILATIONS = (1, 4, 16)
N_SIDE = 64
MLA_HEADS, MLA_Q_RANK, MLA_KV_RANK = 8, 256, 128
MLA_NOPE, MLA_ROPE, MLA_V = 64, 32, 64
MLA_WIDTH = 512
N_MEM, MEM_HEADS, MEM_HEAD_DIM, MEM_WIDTH = 256, 4, 128, 512
ROPE_THETA = 500000.0
NORM_EPS = 1e-5
NEG_INF = -1e30
ALPHA = 2.0 ** 0.25
D_IN = 6048
N_DEV = 8

ADAM_LR, ADAM_B1, ADAM_B2, ADAM_EPS, ADAM_WD, ADAM_STEP = 0.001, 0.9, 0.999, 1e-08, 0.01, 10

D_INW = 6144
PIECE_WIDTHS = (1024, 1024, 1024, 1024, 512, 512, 512, 512)
PIECE_OFFS = (0, 1024, 2048, 3072, 4096, 4608, 5120, 5632)
LANES = 128
VMEM_LIMIT = 56 * 1024 * 1024


def _params(*sem):
    kw = dict(vmem_limit_bytes=VMEM_LIMIT)
    if sem:
        kw["dimension_semantics"] = sem
    return pltpu.CompilerParams(**kw)


def _dot(a, b):
    return jnp.dot(a, b, preferred_element_type=F32)


def _dot_nt(a, b):
    return lax.dot_general(a, b, (((1,), (1,)), ((), ())), preferred_element_type=F32)


def _dot_tn(a, b):
    return lax.dot_general(a, b, (((0,), (0,)), ((), ())), preferred_element_type=F32)


def _sigmoid(x):
    return 1.0 / (1.0 + jnp.exp(-x))


def _rope_fwd(x, c, sa, sb, half):
    n = x.shape[-1]
    return x * c + pltpu.roll(x, n - half, 1) * sa + pltpu.roll(x, half, 1) * sb


def _rope_bwd(dy, c, sa, sb, half):
    n = dy.shape[-1]
    return dy * c + pltpu.roll(dy * sa, half, 1) + pltpu.roll(dy * sb, n - half, 1)


def mm_nn(a, b, out_dtype, tm, tn, name, rhs_transposed=False):
    m, k = a.shape
    n = b.shape[0] if rhs_transposed else b.shape[1]
    dot = _dot_nt if rhs_transposed else _dot

    def body(a_ref, b_ref, o_ref):
        o_ref[...] = dot(a_ref[...].astype(BF16), b_ref[...].astype(BF16)).astype(o_ref.dtype)

    b_spec = pl.BlockSpec((tn, k), lambda j, i: (j, 0)) if rhs_transposed else pl.BlockSpec((k, tn), lambda j, i: (0, j))
    return pl.pallas_call(
        body, grid=(n // tn, m // tm),
        in_specs=[pl.BlockSpec((tm, k), lambda j, i: (i, 0)), b_spec],
        out_specs=pl.BlockSpec((tm, tn), lambda j, i: (i, j)),
        out_shape=SDS((m, n), out_dtype), name=name,
        compiler_params=_params("parallel", "parallel"))(a, b)


def mm_tn(a, b, tt, name):
    t, m = a.shape
    n = b.shape[1]

    def body(a_ref, b_ref, o_ref):
        @pl.when(pl.program_id(0) == 0)
        def _():
            o_ref[...] = jnp.zeros_like(o_ref)

        o_ref[...] += _dot_tn(a_ref[...].astype(BF16), b_ref[...].astype(BF16))

    return pl.pallas_call(
        body, grid=(t // tt,),
        in_specs=[pl.BlockSpec((tt, m), lambda i: (i, 0)), pl.BlockSpec((tt, n), lambda i: (i, 0))],
        out_specs=pl.BlockSpec((m, n), lambda i: (0, 0)),
        out_shape=SDS((m, n), F32), name=name,
        compiler_params=_params("arbitrary"))(a, b)


def mm_tn_group(pieces, b, tt, name, slab_rows, first_slab_row, n_slabs):
    n, (t, w), cols = len(pieces), pieces[0].shape, b.shape[1]
    nt = t // tt

    def body(*refs):
        p_refs, b_ref, o_ref, slab_ref = refs[:n], refs[n], refs[n + 1], refs[n + 2]

        @pl.when(pl.program_id(1) == 0)
        def _():
            o_ref[...] = jnp.zeros_like(o_ref)

        for k in range(n):
            @pl.when(pl.program_id(0) == k)
            def _(k=k):
                o_ref[...] += _dot_tn(p_refs[k][...], b_ref[...])

            @pl.when((pl.program_id(0) == k) & (pl.program_id(1) == nt - 1))
            def _(k=k):
                for j in range(n_slabs):
                    lo = max(k * w, first_slab_row + j * slab_rows)
                    hi = min((k + 1) * w, first_slab_row + (j + 1) * slab_rows)
                    if lo < hi:
                        dst = lo - first_slab_row - j * slab_rows
                        slab_ref[j, dst:dst + hi - lo, :] = o_ref[lo - k * w:hi - k * w, :].astype(slab_ref.dtype)

    def piece_spec(k):
        return pl.BlockSpec((tt, w), lambda p, i: (jnp.where(p < k, 0, jnp.where(p > k, nt - 1, i)), 0))

    return pl.pallas_call(
        body, grid=(n, nt),
        in_specs=[piece_spec(k) for k in range(n)] + [pl.BlockSpec((tt, cols), lambda p, i: (i, 0))],
        out_specs=[pl.BlockSpec((w, cols), lambda p, i: (p, 0)),
                   pl.BlockSpec((n_slabs, slab_rows, cols), lambda p, i: (0, 0, 0))],
        out_shape=[SDS((n * w, cols), F32), SDS((n_slabs, slab_rows, cols), BF16)], name=name,
        compiler_params=_params("arbitrary", "arbitrary"))(*pieces, b)


def embed_fwd(x2, g, b, positions, xch):
    t, d = x2.shape
    tm = 512
    pos = positions.astype(F32).reshape(-1, 1)

    def body(x_ref, g_ref, b_ref, pos_ref, pat_ref, h32_ref, h16_ref, *tabs):
        x = x_ref[...]
        mu = jnp.mean(x, axis=-1, keepdims=True)
        xc = x - mu
        var = jnp.mean(xc * xc, axis=-1, keepdims=True)
        h = xc * lax.rsqrt(var + NORM_EPS) * g_ref[...] + b_ref[...]
        h32_ref[...] = h
        h16_ref[...] = h.astype(BF16)
        p = pos_ref[...]
        for k in range(2):
            inv, first, second = pat_ref[3 * k:3 * k + 1, :], pat_ref[3 * k + 1:3 * k + 2, :], pat_ref[3 * k + 2:3 * k + 3, :]
            ang = p * inv
            sn = jnp.sin(ang)
            tabs[3 * k][...] = jnp.where(first + second > 0.0, jnp.cos(ang), 1.0)
            tabs[3 * k + 1][...] = -first * sn
            tabs[3 * k + 2][...] = second * sn

    row = pl.BlockSpec((tm, d), lambda i: (i, 0))
    vec = pl.BlockSpec((1, d), lambda i: (0, 0))
    tab = pl.BlockSpec((tm, LANES), lambda i: (i, 0))
    res, landed = call_hosting_exchange(
        body, xch, grid=(t // tm,),
        in_specs=[row, vec, vec, pl.BlockSpec((tm, 1), lambda i: (i, 0)), pl.BlockSpec((8, LANES), lambda i: (0, 0))],
        out_specs=[row, row] + [tab] * 6,
        out_shape=[SDS((t, d), F32), SDS((t, d), BF16)] + [SDS((t, LANES), F32)] * 6,
        scratch_shapes=[], name="embed_fwd", operands=(x2, g, b, pos, _rope_lane_patterns()))
    return (res[0], res[1], tuple(res[2:5]), tuple(res[5:8])), landed


Q_BLK = 128
UNROLL_FWD = 16
UNROLL_BWD = 16


def _pattern_geometry(d):
    length = SEQ // d
    nblk = length // Q_BLK
    kwin = min(2 * Q_BLK, length)
    return length, nblk, kwin


def _block_coords(idx, d):
    length, nblk, kwin = _pattern_geometry(d)
    r = lax.shift_right_logical(idx, nblk.bit_length() - 1)
    i = idx & (nblk - 1)
    q0 = pl.multiple_of(r * length + i * Q_BLK, Q_BLK)
    ks = jnp.clip(i * Q_BLK - N_SIDE, 0, length - kwin)
    k0 = pl.multiple_of(r * length + ks, N_SIDE)
    qpos = i * Q_BLK + lax.broadcasted_iota(jnp.int32, (Q_BLK, kwin), 0)
    kpos = ks + lax.broadcasted_iota(jnp.int32, (Q_BLK, kwin), 1)
    valid = jnp.abs(kpos - qpos) <= N_SIDE
    return q0, k0, kwin, valid


def _deinterleave(src_ref, dst_ref, d, dtype, tmp_ref):
    if d == 1:
        dst_ref[...] = src_ref[...].astype(dtype)
        return
    q = SEQ // 4
    if d == 4:
        for r in range(4):
            dst_ref[r * q:(r + 1) * q, :] = src_ref[pl.ds(r, q, stride=4), :].astype(dtype)
        return
    assert d == 16
    n = SEQ // 16
    for r in range(4):
        tmp_ref[r * q:(r + 1) * q, :] = src_ref[pl.ds(r, q, stride=4), :]
    for r in range(4):
        for j in range(4):
            dst_ref[(r + 4 * j) * n:(r + 4 * j + 1) * n, :] = tmp_ref[pl.ds(r * q + j, n, stride=4), :].astype(dtype)


def _class16_to_class4(src_ref, dst_ref):
    q, n = SEQ // 4, SEQ // 16
    for r in range(4):
        for j in range(4):
            dst_ref[pl.ds(r * q + j, n, stride=4), :] = src_ref[(r + 4 * j) * n:(r + 4 * j + 1) * n, :]


def _interleave(src_ref, dst_ref, d, tmp_ref, accumulate):
    q = SEQ // 4
    if d == 16:
        _class16_to_class4(src_ref, tmp_ref)
        src_ref = tmp_ref
    else:
        assert d == 4
    for r in range(4):
        rows = pl.ds(r, q, stride=4)
        val = src_ref[r * q:(r + 1) * q, :]
        dst_ref[rows, :] = dst_ref[rows, :] + val if accumulate else val


def a_attn_fwd(proj, ca, sa, sb, nb, xch):
    t = proj.shape[0]
    n_pairs = A_WIDTH // LANES

    def body(q_ref, k_ref, v_ref, c_ref, sa_ref, sb_ref, y_ref, lse_ref, *rest):
        qkv_d, (qr_s, kr_s, oc_s, lc_s, o1_s, l1_s, o2_s, l2_s, o3_s, l3_s, tmp_s) = rest[:9], rest[9:]
        c, s_a, s_b = c_ref[...], sa_ref[...], sb_ref[...]
        qr_s[...] = _rope_fwd(q_ref[...], c, s_a, s_b, A_ROT // 2) * (A_HEAD_DIM ** -0.5)
        kr_s[...] = _rope_fwd(k_ref[...], c, s_a, s_b, A_ROT // 2)
        head0 = lax.broadcasted_iota(jnp.int32, (Q_BLK, LANES), 1) < A_HEAD_DIM
        nat = ((o1_s, l1_s), (o2_s, l2_s), (o3_s, l3_s))

        for g, d in enumerate(DILATIONS):
            qd_s, kd_s, vd_s = qkv_d[3 * g:3 * g + 3]
            _deinterleave(qr_s, qd_s, d, BF16, tmp_s)
            _deinterleave(kr_s, kd_s, d, BF16, tmp_s)
            _deinterleave(v_ref, vd_s, d, BF16, tmp_s)
            o_dst, l_dst = (nat[g] if d == 1 else (oc_s, lc_s))

            def block(idx, carry, d=d, o_dst=o_dst, l_dst=l_dst, qd_s=qd_s, kd_s=kd_s, vd_s=vd_s):
                q0, k0, kwin, valid = _block_coords(idx, d)
                qb = qd_s[pl.ds(q0, Q_BLK), :]
                kb = kd_s[pl.ds(k0, kwin), :]
                vb = vd_s[pl.ds(k0, kwin), :]
                zero = jnp.zeros_like(qb)
                q2 = jnp.concatenate([jnp.where(head0, qb, zero), jnp.where(head0, zero, qb)], 0)
                s = jnp.where(jnp.concatenate([valid, valid], 0), _dot_nt(q2, kb), NEG_INF)
                m = jnp.max(s, axis=-1, keepdims=True)
                p = jnp.exp(s - m)
                l = jnp.sum(p, axis=-1, keepdims=True)
                o2 = _dot(p.astype(BF16), vb) / l
                l2 = m + jnp.log(l)
                o_dst[pl.ds(q0, Q_BLK), :] = jnp.where(head0, o2[:Q_BLK], o2[Q_BLK:])
                l_dst[pl.ds(q0, Q_BLK), :] = jnp.where(head0, l2[:Q_BLK], l2[Q_BLK:])
                return carry

            lax.fori_loop(0, SEQ // Q_BLK, block, 0, unroll=UNROLL_FWD)
            if d > 1:
                _interleave(oc_s, nat[g][0], d, tmp_s, False)
                _interleave(lc_s, nat[g][1], d, tmp_s, False)

        def merge(ci, carry):
            rows = pl.ds(pl.multiple_of(ci * 256, 256), 256)
            l1, l2, l3 = l1_s[rows, :], l2_s[rows, :], l3_s[rows, :]
            m = jnp.maximum(jnp.maximum(l1, l2), l3)
            w1, w2, w3 = jnp.exp(l1 - m), jnp.exp(l2 - m), jnp.exp(l3 - m)
            w = w1 + w2 + w3
            y_ref[rows, :] = (w1 * o1_s[rows, :] + w2 * o2_s[rows, :] + w3 * o3_s[rows, :]) / w
            lse_ref[rows, :] = m + jnp.log(w)
            return carry

        lax.fori_loop(0, SEQ // 256, merge, 0)

    def col(off):
        return pl.BlockSpec((SEQ, LANES), lambda b, hp: (b, off + hp))

    tab = pl.BlockSpec((SEQ, LANES), lambda b, hp: (b, 0))
    out = pl.BlockSpec((SEQ, LANES), lambda b, hp: (b, hp))
    f32s = pltpu.VMEM((SEQ, LANES), F32)
    res, landed = call_hosting_exchange(
        body, xch, grid=(nb, n_pairs),
        in_specs=[col(0), col(n_pairs), col(2 * n_pairs), tab, tab, tab],
        out_specs=[out] * 11,
        out_shape=[SDS((t, A_WIDTH), F32)] * 2 + [SDS((t, A_WIDTH), BF16)] * 9,
        scratch_shapes=[f32s] * 11,
        name="a_attn_fwd", operands=(proj, proj, proj, ca, sa, sb))
    return res[:2], res[2:], landed


def a_attn_bwd(qkv_d, ca, sa, sb, dy, y, lse, nb, xch):
    t = dy.shape[0]
    n_pairs = A_WIDTH // LANES

    def body(*refs):
        qkv_refs = refs[:9]
        (c_ref, sa_ref, sb_ref, do_ref, y_ref, lse_ref, dq_ref, dk_ref, dv_ref,
         l0n_s, l1n_s, d0n_s, d1n_s, dod_s, l0d_s, l1d_s, d0d_s, d1d_s,
         dqc_s, dkc_s, dvc_s, dq4_s, dk4_s, dv4_s, dqn_s, dkn_s, dvn_s, tmp_s) = refs[9:]
        c, s_a, s_b = c_ref[...], sa_ref[...], sb_ref[...]
        head0 = lax.broadcasted_iota(jnp.int32, (Q_BLK, LANES), 1) < A_HEAD_DIM

        def per_head_rows(ci, carry):
            rows = pl.ds(pl.multiple_of(ci * 256, 256), 256)
            h0 = lax.broadcasted_iota(jnp.int32, (256, LANES), 1) < A_HEAD_DIM
            tt = do_ref[rows, :] * y_ref[rows, :]
            d0n_s[rows, :] = jnp.broadcast_to(jnp.sum(jnp.where(h0, tt, 0.0), axis=-1, keepdims=True), (256, LANES))
            d1n_s[rows, :] = jnp.broadcast_to(jnp.sum(jnp.where(h0, 0.0, tt), axis=-1, keepdims=True), (256, LANES))
            l = lse_ref[rows, :]
            lr = pltpu.roll(l, A_HEAD_DIM, 1)
            l0n_s[rows, :] = jnp.where(h0, l, lr)
            l1n_s[rows, :] = jnp.where(h0, lr, l)
            return carry

        lax.fori_loop(0, SEQ // 256, per_head_rows, 0)
        assert DILATIONS == (1, 4, 16)

        for g, d in enumerate(DILATIONS):
            qd_s, kd_s, vd_s = qkv_refs[3 * g:3 * g + 3]
            _deinterleave(do_ref, dod_s, d, BF16, tmp_s)
            if d > 1:
                for src, dst in ((l0n_s, l0d_s), (l1n_s, l1d_s), (d0n_s, d0d_s), (d1n_s, d1d_s)):
                    _deinterleave(src, dst, d, F32, tmp_s)
            l0, l1, d0, d1 = (l0n_s, l1n_s, d0n_s, d1n_s) if d == 1 else (l0d_s, l1d_s, d0d_s, d1d_s)
            dq_dst, dk_dst, dv_dst = {1: (dqn_s, dkn_s, dvn_s), 4: (dq4_s, dk4_s, dv4_s), 16: (dqc_s, dkc_s, dvc_s)}[d]
            dk_dst[...] = jnp.zeros_like(dk_dst)
            dv_dst[...] = jnp.zeros_like(dv_dst)

            def block(idx, carry, d=d, l0=l0, l1=l1, d0=d0, d1=d1, dq_dst=dq_dst, dk_dst=dk_dst, dv_dst=dv_dst,
                      qd_s=qd_s, kd_s=kd_s, vd_s=vd_s):
                q0, k0, kwin, valid = _block_coords(idx, d)
                qrows = pl.ds(q0, Q_BLK)
                krows = pl.ds(k0, kwin)
                qb, dob = qd_s[qrows, :], dod_s[qrows, :]
                kb, vb = kd_s[krows, :], vd_s[krows, :]
                zero = jnp.zeros_like(qb)
                q2 = jnp.concatenate([jnp.where(head0, qb, zero), jnp.where(head0, zero, qb)], 0)
                do2 = jnp.concatenate([jnp.where(head0, dob, zero), jnp.where(head0, zero, dob)], 0)
                wide = lambda x: jnp.concatenate([x] * (kwin // LANES), 1)
                lse2 = wide(jnp.concatenate([l0[qrows, :], l1[qrows, :]], 0))
                dd2 = wide(jnp.concatenate([d0[qrows, :], d1[qrows, :]], 0))
                s = jnp.where(jnp.concatenate([valid, valid], 0), _dot_nt(q2, kb), NEG_INF)
                p = jnp.exp(s - lse2)
                ds = (p * (_dot_nt(do2, vb) - dd2)).astype(BF16)
                dq2 = _dot(ds, kb)
                dq_dst[qrows, :] = jnp.where(head0, dq2[:Q_BLK], dq2[Q_BLK:])
                dk_dst[krows, :] += _dot_tn(ds, q2)
                dv_dst[krows, :] += _dot_tn(p.astype(BF16), do2)
                return carry

            lax.fori_loop(0, SEQ // Q_BLK, block, 0, unroll=UNROLL_BWD)

        for c16, c4, nat in ((dqc_s, dq4_s, dqn_s), (dkc_s, dk4_s, dkn_s), (dvc_s, dv4_s, dvn_s)):
            _class16_to_class4(c16, tmp_s)
            c4[...] = c4[...] + tmp_s[...]
            _interleave(c4, nat, 4, tmp_s, True)

        dq_ref[...] = _rope_bwd(dqn_s[...] * (A_HEAD_DIM ** -0.5), c, s_a, s_b, A_ROT // 2).astype(BF16)
        dk_ref[...] = _rope_bwd(dkn_s[...], c, s_a, s_b, A_ROT // 2).astype(BF16)
        dv_ref[...] = dvn_s[...].astype(BF16)

    tab = pl.BlockSpec((SEQ, LANES), lambda b, hp: (b, 0))
    blk = pl.BlockSpec((SEQ, LANES), lambda b, hp: (b, hp))
    f32s = pltpu.VMEM((SEQ, LANES), F32)
    b16s = pltpu.VMEM((SEQ, LANES), BF16)
    return call_hosting_exchange(
        body, xch, grid=(nb, n_pairs),
        in_specs=[blk] * 9 + [tab, tab, tab, blk, blk, blk],
        out_specs=[blk, blk, blk],
        out_shape=[SDS((t, A_WIDTH), BF16)] * 3,
        scratch_shapes=[f32s] * 4 + [b16s] + [f32s] * 14,
        name="a_attn_bwd", operands=(*qkv_d, ca, sa, sb, dy, y, lse))


MLA_SCALE = (MLA_NOPE + MLA_ROPE) ** -0.5
LOG2E = 1.4426950408889634
MLA_QW = MLA_HEADS * LANES
MLA_KVW = MLA_QW + MLA_WIDTH


def _rms(x, g):
    r = lax.rsqrt(jnp.mean(x * x, axis=-1, keepdims=True) + NORM_EPS)
    return x * r * g, r


def _rms_bwd(dn, x, r, g):
    tg = dn * g
    dx = r * tg - x * (r * r * r) * jnp.mean(tg * x, axis=-1, keepdims=True)
    return dx, jnp.sum(dn * x * r, axis=0, keepdims=True)


def mla_prep_fwd(proj, cm, sma, smb, g_cq, g_ckv, wuq, wkv):
    t = proj.shape[0]
    tm = 512

    def body(cq_ref, ckv_ref, kr_ref, c_ref, sa_ref, sb_ref, gq_ref, gkv_ref, wuq_ref, wkv_ref, q_ref, k_ref, v_ref):
        c, s_a, s_b = c_ref[...], sa_ref[...], sb_ref[...]
        cqn, _ = _rms(cq_ref[...], gq_ref[...])
        qf = _dot(cqn.astype(BF16), wuq_ref[...])
        ckvn, _ = _rms(ckv_ref[...], gkv_ref[...])
        kvf = _dot(ckvn.astype(BF16), wkv_ref[...])
        krope = _rope_fwd(kr_ref[...], c, s_a, s_b, MLA_ROPE // 2)
        for h in range(MLA_HEADS):
            cols = slice(h * LANES, (h + 1) * LANES)
            q_ref[:, cols] = (_rope_fwd(qf[:, cols], c, s_a, s_b, MLA_ROPE // 2) * (MLA_SCALE * LOG2E)).astype(BF16)
            k_ref[:, cols] = (kvf[:, cols] + krope).astype(BF16)
        v_ref[...] = kvf[:, MLA_QW:].astype(BF16)

    def row(w, j):
        return pl.BlockSpec((tm, w), lambda i: (i, j))

    def full(a):
        return pl.BlockSpec(a.shape, lambda i: (0, 0))

    return pl.pallas_call(
        body, grid=(t // tm,),
        in_specs=[row(256, 4096 // 256), row(128, 4352 // 128), row(128, 4480 // 128), row(128, 0), row(128, 0), row(128, 0),
                  full(g_cq), full(g_ckv), full(wuq), full(wkv)],
        out_specs=[row(MLA_QW, 0), row(MLA_QW, 0), row(MLA_WIDTH, 0)],
        out_shape=[SDS((t, MLA_QW), BF16), SDS((t, MLA_QW), BF16), SDS((t, MLA_WIDTH), BF16)],
        name="mla_prep_fwd", compiler_params=_params("parallel"))(proj, proj, proj, cm, sma, smb, g_cq, g_ckv, wuq, wkv)


def mla_prep_bwd(proj, cm, sma, smb, g_cq, g_ckv, wuq, wkv, dq, dk, dv):
    t = proj.shape[0]
    tm = 512

    def body(cq_ref, ckv_ref, c_ref, sa_ref, sb_ref, gq_ref, gkv_ref, wuq_ref, wkv_ref, dq_ref, dk_ref, dv_ref,
             dcc_ref, dqf_ref, cqn_ref, dkvf_ref, ckvn_ref, dgq_ref, dgkv_ref):
        @pl.when(pl.program_id(0) == 0)
        def _():
            dgq_ref[...] = jnp.zeros_like(dgq_ref)
            dgkv_ref[...] = jnp.zeros_like(dgkv_ref)

        c, s_a, s_b = c_ref[...], sa_ref[...], sb_ref[...]
        cq, ckv = cq_ref[...], ckv_ref[...]
        cqn, rq = _rms(cq, gq_ref[...])
        ckvn, rkv = _rms(ckv, gkv_ref[...])
        cqn_ref[...] = cqn.astype(BF16)
        ckvn_ref[...] = ckvn.astype(BF16)
        lane = lax.broadcasted_iota(jnp.int32, (tm, LANES), 1)
        rope_lanes = (lane >= MLA_NOPE) & (lane < MLA_NOPE + MLA_ROPE)
        dkrope = jnp.zeros((tm, LANES), F32)
        for h in range(MLA_HEADS):
            cols = slice(h * LANES, (h + 1) * LANES)
            dqf_ref[:, cols] = _rope_bwd(dq_ref[:, cols] * MLA_SCALE, c, s_a, s_b, MLA_ROPE // 2).astype(BF16)
            dkh = dk_ref[:, cols] * (1.0 / LOG2E)
            dkvf_ref[:, cols] = dkh.astype(BF16)
            dkrope = dkrope + dkh
        dkvf_ref[:, MLA_QW:] = dv_ref[...].astype(BF16)
        dkr = _rope_bwd(jnp.where(rope_lanes, dkrope, 0.0), c, s_a, s_b, MLA_ROPE // 2)
        dcqn = _dot_nt(dqf_ref[...], wuq_ref[...])
        dckvn = _dot_nt(dkvf_ref[...], wkv_ref[...])
        dcq, dgq = _rms_bwd(dcqn, cq, rq, gq_ref[...])
        dckv, dgkv = _rms_bwd(dckvn, ckv, rkv, gkv_ref[...])
        dgq_ref[...] += dgq
        dgkv_ref[...] += dgkv
        dcc_ref[:, 0:256] = dcq.astype(BF16)
        dcc_ref[:, 256:384] = dckv.astype(BF16)
        dcc_ref[:, 384:512] = dkr.astype(BF16)

    def row(w, j):
        return pl.BlockSpec((tm, w), lambda i: (i, j))

    def full(a):
        return pl.BlockSpec(a.shape, lambda i: (0, 0))

    return pl.pallas_call(
        body, grid=(t // tm,),
        in_specs=[row(256, 4096 // 256), row(128, 4352 // 128), row(128, 0), row(128, 0), row(128, 0),
                  full(g_cq), full(g_ckv), full(wuq), full(wkv), row(MLA_QW, 0), row(MLA_QW, 0), row(MLA_WIDTH, 0)],
        out_specs=[row(512, 0), row(MLA_QW, 0), row(256, 0), row(MLA_KVW, 0), row(128, 0), full(g_cq), full(g_ckv)],
        out_shape=[SDS((t, 512), BF16), SDS((t, MLA_QW), BF16), SDS((t, 256), BF16), SDS((t, MLA_KVW), BF16),
                   SDS((t, 128), BF16), SDS(g_cq.shape, F32), SDS(g_ckv.shape, F32)],
        name="mla_prep_bwd", compiler_params=_params("arbitrary"))(proj, proj, cm, sma, smb, g_cq, g_ckv, wuq, wkv, dq, dk, dv)


MLA_TQ = 256


def mla_attn_fwd(qb, kb, vb, nb):
    t = qb.shape[0]
    nq = SEQ // MLA_TQ
    n_pairs = MLA_HEADS // 2

    def body(q_ref, k_ref, v_ref, y_ref, lse_ref):
        head0 = lax.broadcasted_iota(jnp.int32, (MLA_TQ, LANES), 1) < MLA_V
        v = v_ref[...]
        vhead0 = lax.broadcasted_iota(jnp.int32, v.shape, 1) < MLA_V
        one = jnp.ones_like(v)
        outs, lses = [], []
        for h in range(2):
            cols = slice(h * LANES, (h + 1) * LANES)
            s = _dot_nt(q_ref[:, cols], k_ref[:, cols])
            m = jnp.max(s, axis=-1, keepdims=True)
            p = jnp.exp2(s - m).astype(BF16)
            ol = _dot(p, jnp.where(vhead0 == (h == 0), v, one))
            l = pltpu.roll(ol, MLA_V, 1)
            outs.append(ol / l)
            lses.append(m + jnp.log2(l))
        y_ref[...] = jnp.where(head0, outs[0], outs[1])
        lse_ref[...] = jnp.where(head0, lses[0], lses[1])

    return pl.pallas_call(
        body, grid=(nb, n_pairs, nq),
        in_specs=[pl.BlockSpec((MLA_TQ, 2 * LANES), lambda b, hp, i: (b * nq + i, hp)),
                  pl.BlockSpec((SEQ, 2 * LANES), lambda b, hp, i: (b, hp)),
                  pl.BlockSpec((SEQ, LANES), lambda b, hp, i: (b, hp))],
        out_specs=[pl.BlockSpec((MLA_TQ, LANES), lambda b, hp, i: (b * nq + i, hp))] * 2,
        out_shape=[SDS((t, MLA_WIDTH), F32)] * 2,
        name="mla_attn_fwd", compiler_params=_params("parallel", "parallel", "parallel"))(qb, kb, vb)


def mla_attn_bwd(qb, kb, vb, dy, y, lse, nb, xch):
    t = qb.shape[0]
    nq = SEQ // MLA_TQ
    n_pairs = MLA_HEADS // 2

    def body(q_ref, k_ref, v_ref, do_ref, y_ref, lse_ref, dq_ref, dk_ref, dv_ref):
        @pl.when(pl.program_id(2) == 0)
        def _():
            dk_ref[...] = jnp.zeros_like(dk_ref)
            dv_ref[...] = jnp.zeros_like(dv_ref)

        head0 = lax.broadcasted_iota(jnp.int32, (MLA_TQ, LANES), 1) < MLA_V
        v = v_ref[...]
        do = do_ref[...]
        lse = lse_ref[...]
        tt = do * y_ref[...]
        dv = jnp.zeros((SEQ, LANES), F32)
        for h in range(2):
            sel = head0 if h == 0 else ~head0
            lo = h * MLA_V
            cols = slice(h * LANES, (h + 1) * LANES)
            q = q_ref[:, cols]
            k = k_ref[:, cols]
            dd = jnp.sum(jnp.where(sel, tt, 0.0), axis=-1, keepdims=True)
            doh = jnp.where(sel, do, 0.0).astype(BF16)
            p = jnp.exp2(_dot_nt(q, k) - lse[:, lo:lo + 1])
            dp = _dot_nt(doh, v)
            ds = (p * (dp - dd)).astype(BF16)
            dq_ref[:, cols] = _dot(ds, k)
            dk_ref[:, cols] += _dot_tn(ds, q)
            dv = dv + _dot_tn(p.astype(BF16), doh)
        dv_ref[...] += dv

    qspec = pl.BlockSpec((MLA_TQ, 2 * LANES), lambda b, hp, i: (b * nq + i, hp))
    kspec = pl.BlockSpec((SEQ, 2 * LANES), lambda b, hp, i: (b, hp))
    vspec = pl.BlockSpec((SEQ, LANES), lambda b, hp, i: (b, hp))
    ospec = pl.BlockSpec((MLA_TQ, LANES), lambda b, hp, i: (b * nq + i, hp))
    return call_hosting_exchange(
        body, xch, grid=(nb, n_pairs, nq),
        in_specs=[qspec, kspec, vspec, ospec, ospec, ospec],
        out_specs=[qspec, kspec, vspec],
        out_shape=[SDS((t, MLA_QW), F32), SDS((t, MLA_QW), F32), SDS((t, MLA_WIDTH), F32)],
        scratch_shapes=[], name="mla_attn_bwd", operands=(qb, kb, vb, dy, y, lse))


MEM_TQ = 512
MEM_SCALE = MEM_HEAD_DIM ** -0.5
MQ_BLK4 = 5120 // MEM_WIDTH


def mem_attn_fwd(proj, mkv, nb):
    t = proj.shape[0]
    nq = SEQ // MEM_TQ

    def body(q_ref, mk_ref, mv_ref, y_ref):
        for h in range(MEM_HEADS):
            cols = slice(h * LANES, (h + 1) * LANES)
            s = _dot_nt(q_ref[:, cols].astype(BF16), mk_ref[:, cols]) * MEM_SCALE
            m = jnp.max(s, axis=-1, keepdims=True)
            p = jnp.exp(s - m)
            l = jnp.sum(p, axis=-1, keepdims=True)
            y_ref[:, cols] = _dot(p.astype(BF16), mv_ref[:, cols]) / l

    return pl.pallas_call(
        body, grid=(nb, nq),
        in_specs=[pl.BlockSpec((MEM_TQ, MEM_WIDTH), lambda b, i: (b * nq + i, MQ_BLK4)),
                  pl.BlockSpec((N_MEM, MEM_WIDTH), lambda b, i: (b, 0)),
                  pl.BlockSpec((N_MEM, MEM_WIDTH), lambda b, i: (b, 1))],
        out_specs=pl.BlockSpec((MEM_TQ, MEM_WIDTH), lambda b, i: (b * nq + i, 0)),
        out_shape=SDS((t, MEM_WIDTH), F32),
        name="mem_attn_fwd", compiler_params=_params("parallel", "parallel"))(proj, mkv, mkv)


def mem_attn_bwd(proj, mkv, dy, nb):
    t = proj.shape[0]
    nq = SEQ // MEM_TQ

    def body(q_ref, mk_ref, mv_ref, do_ref, dq_ref, dmk_ref, dmv_ref):
        @pl.when(pl.program_id(1) == 0)
        def _():
            dmk_ref[...] = jnp.zeros_like(dmk_ref)
            dmv_ref[...] = jnp.zeros_like(dmv_ref)

        for h in range(MEM_HEADS):
            cols = slice(h * LANES, (h + 1) * LANES)
            q = q_ref[:, cols].astype(BF16)
            mk, mv = mk_ref[:, cols], mv_ref[:, cols]
            do = do_ref[:, cols].astype(BF16)
            s = _dot_nt(q, mk) * MEM_SCALE
            e = jnp.exp(s - jnp.max(s, axis=-1, keepdims=True))
            p = e / jnp.sum(e, axis=-1, keepdims=True)
            dp = _dot_nt(do, mv)
            ds = (p * (dp - jnp.sum(p * dp, axis=-1, keepdims=True)) * MEM_SCALE).astype(BF16)
            dq_ref[:, cols] = _dot(ds, mk).astype(BF16)
            dmk_ref[:, cols] += _dot_tn(ds, q)
            dmv_ref[:, cols] += _dot_tn(p.astype(BF16), do)

    ospec = pl.BlockSpec((MEM_TQ, MEM_WIDTH), lambda b, i: (b * nq + i, 0))
    kspec = pl.BlockSpec((N_MEM, MEM_WIDTH), lambda b, i: (b, 0))
    return pl.pallas_call(
        body, grid=(nb, nq),
        in_specs=[pl.BlockSpec((MEM_TQ, MEM_WIDTH), lambda b, i: (b * nq + i, MQ_BLK4)),
                  kspec, pl.BlockSpec((N_MEM, MEM_WIDTH), lambda b, i: (b, 1)), ospec],
        out_specs=[ospec, kspec, kspec],
        out_shape=[SDS((t, MEM_WIDTH), BF16), SDS((nb * N_MEM, MEM_WIDTH), F32), SDS((nb * N_MEM, MEM_WIDTH), F32)],
        name="mem_attn_bwd", compiler_params=_params("parallel", "arbitrary"))(proj, mkv, mkv, dy)


ROW_TM = 512
AG_BLK = 3072 // 1024
BG_BLK = 4608 // 512
MG_BLK = 5632 // 512
GROUPS = ((0, A_WIDTH), (A_WIDTH, MLA_WIDTH), (A_WIDTH + MLA_WIDTH, MEM_WIDTH))
D_MIX = 2048


def _gate_specs():
    def row(w, j):
        return pl.BlockSpec((ROW_TM, w), lambda i: (i, j))

    def vec(w):
        return pl.BlockSpec((1, w), lambda i: (0, 0))

    ys = [row(A_WIDTH, 0), row(MLA_WIDTH, 0), row(MEM_WIDTH, 0)]
    gates = [row(A_WIDTH, AG_BLK), row(MLA_WIDTH, BG_BLK), row(MEM_WIDTH, MG_BLK)]
    gains = [vec(A_WIDTH), vec(MLA_WIDTH), vec(MEM_WIDTH)]
    return row, vec, ys, gates, gains


def gate_out_ln_loss(ya, yb, ym, proj, goa, gob, gom, wout, h32, target, gp, bp):
    t, d = h32.shape
    _, _, ys, gates, gains = _gate_specs()

    def body(ya_ref, yb_ref, ym_ref, ga_ref, gb_ref, gm_ref, goa_ref, gob_ref, gom_ref, w_ref, h_ref, t_ref, gp_ref, bp_ref,
             z_ref, du32_ref, du16_ref, loss_ref, dgp_ref, dbp_ref):
        @pl.when(pl.program_id(0) == 0)
        def _():
            loss_ref[...] = jnp.zeros_like(loss_ref)
            dgp_ref[...] = jnp.zeros_like(dgp_ref)
            dbp_ref[...] = jnp.zeros_like(dbp_ref)

        for (off, w), y_ref, g_ref, go_ref in zip(GROUPS, (ya_ref, yb_ref, ym_ref), (ga_ref, gb_ref, gm_ref),
                                                  (goa_ref, gob_ref, gom_ref)):
            n, _ = _rms(y_ref[...], go_ref[...])
            gt = g_ref[...]
            z_ref[:, off:off + w] = (n * (gt * _sigmoid(gt))).astype(BF16)
        g = gp_ref[...]
        u = ALPHA * h_ref[...] + _dot(z_ref[...], w_ref[...])
        mu = jnp.mean(u, axis=-1, keepdims=True)
        uc = u - mu
        rstd = lax.rsqrt(jnp.mean(uc * uc, axis=-1, keepdims=True) + NORM_EPS)
        xhat = uc * rstd
        err = xhat * g + bp_ref[...] - t_ref[...]
        tok = jnp.sum(err * err, axis=-1, keepdims=True) * (1.0 / d)
        loss_ref[...] += 0.5 * jnp.sum(tok, axis=0, keepdims=True)
        dout = err * (1.0 / d)
        dxhat = dout * g
        du = rstd * (dxhat - jnp.mean(dxhat, axis=-1, keepdims=True)
                     - xhat * jnp.mean(dxhat * xhat, axis=-1, keepdims=True))
        du32_ref[...] = du
        du16_ref[...] = du.astype(BF16)
        dgp_ref[...] += jnp.sum(dout * xhat, axis=0, keepdims=True)
        dbp_ref[...] += jnp.sum(dout, axis=0, keepdims=True)

    row = pl.BlockSpec((ROW_TM, d), lambda i: (i, 0))
    vec = pl.BlockSpec((1, d), lambda i: (0, 0))
    zrow = pl.BlockSpec((ROW_TM, D_MIX), lambda i: (i, 0))
    return pl.pallas_call(
        body, grid=(t // ROW_TM,),
        in_specs=ys + gates + gains + [pl.BlockSpec((D_MIX, d), lambda i: (0, 0)), row, row, vec, vec],
        out_specs=[zrow, row, row, pl.BlockSpec((1, LANES), lambda i: (0, 0)), vec, vec],
        out_shape=[SDS((t, D_MIX), BF16), SDS((t, d), F32), SDS((t, d), BF16), SDS((1, LANES), F32), SDS((1, d), F32),
                   SDS((1, d), F32)],
        name="gate_out_ln_loss", compiler_params=_params("arbitrary"))(
            ya, yb, ym, proj, proj, proj, goa, gob, gom, wout, h32, target, gp, bp)


def gate_bwd(du16, wout, ya, yb, ym, proj, goa, gob, gom):
    t = ya.shape[0]
    row, vec, ys, gates, gains = _gate_specs()

    def body(du_ref, w_ref, ya_ref, yb_ref, ym_ref, ga_ref, gb_ref, gm_ref, goa_ref, gob_ref, gom_ref,
             dya_ref, dyb_ref, dym_ref, dga_ref, dgb_ref, dgm_ref, dgoa_ref, dgob_ref, dgom_ref):
        @pl.when(pl.program_id(0) == 0)
        def _():
            dgoa_ref[...] = jnp.zeros_like(dgoa_ref)
            dgob_ref[...] = jnp.zeros_like(dgob_ref)
            dgom_ref[...] = jnp.zeros_like(dgom_ref)

        dz = _dot_nt(du_ref[...], w_ref[...])
        for (off, w), y_ref, g_ref, go_ref, dy_ref, dg_ref, dgo_ref in zip(
                GROUPS, (ya_ref, yb_ref, ym_ref), (ga_ref, gb_ref, gm_ref), (goa_ref, gob_ref, gom_ref),
                (dya_ref, dyb_ref, dym_ref), (dga_ref, dgb_ref, dgm_ref), (dgoa_ref, dgob_ref, dgom_ref)):
            dzg = dz[:, off:off + w]
            y, gt, go = y_ref[...], g_ref[...], go_ref[...]
            n, r = _rms(y, go)
            sg = _sigmoid(gt)
            dg_ref[...] = (dzg * n * (sg * (1.0 + gt * (1.0 - sg)))).astype(BF16)
            dy, dgo = _rms_bwd(dzg * (gt * sg), y, r, go)
            dy_ref[...] = dy
            dgo_ref[...] += dgo

    widths = (A_WIDTH, MLA_WIDTH, MEM_WIDTH)
    return pl.pallas_call(
        body, grid=(t // ROW_TM,),
        in_specs=[row(D_MODEL, 0), pl.BlockSpec((D_MIX, D_MODEL), lambda i: (0, 0))] + ys + gates + gains,
        out_specs=[row(w, 0) for w in widths] * 2 + [vec(w) for w in widths],
        out_shape=[SDS((t, w), F32) for w in widths] + [SDS((t, w), BF16) for w in widths] + [SDS((1, w), F32) for w in widths],
        name="gate_bwd", compiler_params=_params("arbitrary"))(du16, wout, ya, yb, ym, proj, proj, proj, goa, gob, gom)


def dh_ln_bwd(pieces, win_t, du32, x2, g_emb, xch):
    t, d = x2.shape

    def body(*refs):
        p_refs = refs[:len(pieces)]
        w_ref, du_ref, x_ref, g_ref, dx_ref, dg_ref, db_ref = refs[len(pieces):]

        @pl.when(pl.program_id(0) == 0)
        def _():
            dg_ref[...] = jnp.zeros_like(dg_ref)
            db_ref[...] = jnp.zeros_like(db_ref)

        dh = ALPHA * du_ref[...]
        for p_ref, off, w in zip(p_refs, PIECE_OFFS, PIECE_WIDTHS):
            dh = dh + _dot(p_ref[...], w_ref[off:off + w, :])
        x = x_ref[...]
        xc = x - jnp.mean(x, axis=-1, keepdims=True)
        rstd = lax.rsqrt(jnp.mean(xc * xc, axis=-1, keepdims=True) + NORM_EPS)
        xhat = xc * rstd
        dg_ref[...] += jnp.sum(dh * xhat, axis=0, keepdims=True)
        db_ref[...] += jnp.sum(dh, axis=0, keepdims=True)
        tg = dh * g_ref[...]
        dx_ref[...] = rstd * (tg - jnp.mean(tg, axis=-1, keepdims=True)
                              - xhat * jnp.mean(tg * xhat, axis=-1, keepdims=True))

    row = pl.BlockSpec((ROW_TM, d), lambda i: (i, 0))
    vec = pl.BlockSpec((1, d), lambda i: (0, 0))
    return call_hosting_exchange(
        body, xch, grid=(t // ROW_TM,),
        in_specs=[pl.BlockSpec((ROW_TM, w), lambda i: (i, 0)) for w in PIECE_WIDTHS]
        + [pl.BlockSpec(win_t.shape, lambda i: (0, 0)), row, row, vec],
        out_specs=[row, vec, vec],
        out_shape=[SDS((t, d), F32), SDS((1, d), F32), SDS((1, d), F32)],
        scratch_shapes=[], name="dh_ln_bwd", operands=(*pieces, win_t, du32, x2, g_emb))


def _adamw(w, g, m, v):
    m2 = ADAM_B1 * m + (1.0 - ADAM_B1) * g
    v2 = ADAM_B2 * v + (1.0 - ADAM_B2) * (g * g)
    m_hat = m2 / (1.0 - ADAM_B1 ** ADAM_STEP)
    v_hat = v2 / (1.0 - ADAM_B2 ** ADAM_STEP)
    return -ADAM_LR * (m_hat / (jnp.sqrt(v_hat) + ADAM_EPS) + ADAM_WD * w), m2, v2


def adamw_shard(w, parts, m, v, name):
    r, c = w.shape
    if r % 256 == 0 or r * c <= 256 * 1024:
        tr, tc = min(r, 256), c
    else:
        tr, tc = r, 256

    def body(w_ref, p_ref, m_ref, v_ref, g_ref, d_ref, nm_ref, nv_ref):
        g = p_ref[0].astype(F32)
        for k in range(1, N_DEV):
            g = g + p_ref[k].astype(F32)
        g_ref[...] = g
        d_ref[...], nm_ref[...], nv_ref[...] = _adamw(w_ref[...], g, m_ref[...], v_ref[...])

    blk = pl.BlockSpec((tr, tc), lambda i, j: (i, j))
    return pl.pallas_call(
        body, grid=(r // tr, c // tc),
        in_specs=[blk, pl.BlockSpec((N_DEV, tr, tc), lambda i, j: (0, i, j)), blk, blk],
        out_specs=[blk] * 4, out_shape=[SDS((r, c), F32)] * 4, name=name,
        compiler_params=_params("parallel", "parallel"))(w, parts, m, v)


def _place():
    return lax.axis_index("x"), lax.axis_index("y"), lax.axis_index("c")


def _flat(px, py, pc):
    return 4 * px + 2 * py + pc


def _peer(x, y, c, k):
    return (1 - x if k & 4 else x, 1 - y if k & 2 else y, 1 - c if k & 1 else c)


def cast_shards(shards):
    def body(*refs):
        n = len(refs) // 2
        for i_ref, o_ref in zip(refs[:n], refs[n:]):
            o_ref[...] = i_ref[...].astype(BF16)

    return pl.pallas_call(body, out_shape=[SDS(s.shape, BF16) for s in shards], name="cast_shards",
                          compiler_params=_params())(*shards)


def _two_level_gather_plan(src_refs, land_refs, send_sems, recv_sems, local_sems):
    n = len(src_refs)
    x, y, c = _place()
    me, sib = (x, y, c), (x, y, 1 - c)
    chips = [(1 - x, y), (x, 1 - y), (1 - x, 1 - y)]

    def copy(a, k, block, to, src=None):
        dst = land_refs[a].at[_flat(*block)]
        return pltpu.make_async_remote_copy(
            src_ref=dst if src is None else src, dst_ref=dst,
            send_sem=send_sems.at[a * N_DEV + k], recv_sem=recv_sems.at[a * N_DEV + k],
            device_id=to, device_id_type=MESH)

    mine = [pltpu.make_async_copy(src_refs[a], land_refs[a].at[_flat(*me)], local_sems.at[a]) for a in range(n)]
    first = []
    for a in range(n):
        first.append(copy(a, 0, me, sib, src=src_refs[a]))
        first += [copy(a, 1 + j, me, (*chip, c), src=src_refs[a]) for j, chip in enumerate(chips)]

    def start():
        for cp in mine + first:
            cp.start()

    def finish():
        passed = []
        for j, chip in enumerate(chips):
            for a in range(n):
                copy(a, 1 + j, (*chip, c), me).wait_recv()
                fwd = copy(a, 4 + j, (*chip, c), sib)
                fwd.start()
                passed.append(fwd)
        for a in range(n):
            copy(a, 0, sib, me).wait_recv()
            for j, chip in enumerate(chips):
                copy(a, 4 + j, (*chip, 1 - c), me).wait_recv()
        for cp in first + passed:
            cp.wait_send()
        for cp in mine:
            cp.wait()

    return start, finish


ALL_DEVICES = tuple(range(N_DEV))


def _exchange_plan(src_refs, land_refs, dests, send_sems, recv_sems, local_sems):
    x, y, c = _place()
    me = _flat(x, y, c)
    plan = []
    for a, (src, land, dl) in enumerate(zip(src_refs, land_refs, dests)):
        for li, j in enumerate(dl):
            to = ((j >> 2) & 1, (j >> 1) & 1, j & 1)
            block = src.at[li] if len(src.shape) == len(land.shape) else src

            def push(slot, a=a, block=block, land=land, j=j, to=to):
                return pltpu.make_async_remote_copy(
                    src_ref=block, dst_ref=land.at[slot], send_sem=send_sems.at[a * N_DEV + j],
                    recv_sem=recv_sems.at[a * N_DEV + slot], device_id=to, device_id_type=MESH)

            own = pltpu.make_async_copy(block, land.at[j], local_sems.at[a])
            plan.append((j, push(me), own, [push(s) for s in range(N_DEV) if s != j]))
    return me, plan


def _exchange_start(me, plan):
    for j, send, own, _ in plan:
        @pl.when(me != j)
        def _(send=send):
            send.start()

        @pl.when(me == j)
        def _(own=own):
            own.start()


def _exchange_wait(me, plan):
    for j, send, own, arrivals in plan:
        @pl.when(me != j)
        def _(send=send):
            send.wait_send()

        @pl.when(me == j)
        def _(own=own, arrivals=arrivals):
            own.wait()
            for arrival in arrivals:
                arrival.wait_recv()


def call_hosting_exchange(core, xch, *, grid, in_specs, out_specs, out_shape, scratch_shapes, name, operands):
    srcs, dests, landing = xch
    n, n_in, n_out, n_scr = len(srcs), len(in_specs), len(out_specs), len(scratch_shapes)

    def body(*refs):
        ins, src_refs = refs[:n_in], refs[n_in:n_in + n]
        outs = refs[n_in + 2 * n:n_in + 2 * n + n_out]
        land_refs = refs[n_in + 2 * n + n_out:n_in + 3 * n + n_out]
        scratch = refs[n_in + 3 * n + n_out:n_in + 3 * n + n_out + n_scr]
        sems = refs[n_in + 3 * n + n_out + n_scr:]
        first = functools.reduce(jnp.logical_and, [pl.program_id(i) == 0 for i in range(len(grid))])
        last = functools.reduce(jnp.logical_and, [pl.program_id(i) == grid[i] - 1 for i in range(len(grid))])
        if dests is None:
            start, finish = _two_level_gather_plan(src_refs, land_refs, *sems)
        else:
            me, plan = _exchange_plan(src_refs, land_refs, dests, *sems)
            start, finish = functools.partial(_exchange_start, me, plan), functools.partial(_exchange_wait, me, plan)
        pl.when(first)(start)
        core(*ins, *outs, *scratch)
        pl.when(last)(finish)

    hbm = pl.BlockSpec(memory_space=pl.ANY)
    res = pl.pallas_call(
        body, grid=grid,
        in_specs=list(in_specs) + [hbm] * (2 * n), out_specs=list(out_specs) + [hbm] * n,
        out_shape=list(out_shape) + [SDS(l.shape, l.dtype) for l in landing],
        scratch_shapes=list(scratch_shapes) + [pltpu.SemaphoreType.DMA((N_DEV * n,)), pltpu.SemaphoreType.DMA((N_DEV * n,)),
                                               pltpu.SemaphoreType.DMA((n,))],
        input_output_aliases={n_in + n + k: n_out + k for k in range(n)},
        name=name, compiler_params=_params(*(("arbitrary",) * len(grid))))(*operands, *srcs, *landing)
    return res[:n_out], res[n_out:]


SLOT_ROWS = 8


def small_allreduce_adamw(loss_sum, grads, ws, ms, vs):
    n = len(grads)
    rows = [g.shape[0] for g in grads]
    total = SLOT_ROWS * (n + 1)

    def body(*refs):
        loss_ref, g_refs, w_refs = refs[0], refs[1:1 + n], refs[1 + n:1 + 2 * n]
        m_refs, v_refs = refs[1 + 2 * n:1 + 3 * n], refs[1 + 3 * n:1 + 4 * n]
        outs = refs[1 + 4 * n:2 + 8 * n]
        vec, gath, tot, send_sems, recv_sems = refs[2 + 8 * n:]
        x, y, c = _place()
        me = _flat(x, y, c)
        vec[...] = jnp.zeros_like(vec)
        vec[0:1, :] = loss_ref[...]
        for i in range(n):
            vec[SLOT_ROWS * (i + 1):SLOT_ROWS * (i + 1) + rows[i], :] = g_refs[i][...]
        gath[me] = vec[...]
        copies = []
        for k in range(1, N_DEV):
            peer = _peer(x, y, c, k)
            copies.append(pltpu.make_async_remote_copy(
                src_ref=vec, dst_ref=gath.at[me], send_sem=send_sems.at[k - 1], recv_sem=recv_sems.at[k - 1],
                device_id=peer, device_id_type=MESH))
        for cp in copies:
            cp.start()
        for cp in copies:
            cp.wait_recv()
        for cp in copies:
            cp.wait_send()
        g = gath[0]
        for j in range(1, N_DEV):
            g = g + gath[j]
        tot[...] = g
        outs[0][...] = tot[0:1, :]
        for i in range(n):
            gi = tot[SLOT_ROWS * (i + 1):SLOT_ROWS * (i + 1) + rows[i], :]
            outs[1 + i][...] = gi
            outs[1 + n + i][...], outs[1 + 2 * n + i][...], outs[1 + 3 * n + i][...] = _adamw(
                w_refs[i][...], gi, m_refs[i][...], v_refs[i][...])

    shapes = [SDS(g.shape, F32) for g in grads]
    return pl.pallas_call(
        body, out_shape=[SDS((1, LANES), F32)] + shapes * 4,
        scratch_shapes=[pltpu.VMEM((total, LANES), F32), pltpu.VMEM((N_DEV, total, LANES), F32), pltpu.VMEM((total, LANES), F32),
                        pltpu.SemaphoreType.DMA((7,)), pltpu.SemaphoreType.DMA((7,))],
        name="small_allreduce_adamw", compiler_params=_params())(loss_sum, *grads, *ws, *ms, *vs)


def _rope_lane_patterns():
    inv = lambda r: ROPE_THETA ** (-(jnp.arange(0, r, 2, dtype=F32) / r))
    z = lambda n: jnp.zeros((n,), F32)
    o = lambda n: jnp.ones((n,), F32)
    half, rest = A_ROT // 2, A_HEAD_DIM - A_ROT
    ia, im = inv(A_ROT), inv(MLA_ROPE)
    mh, tail = MLA_ROPE // 2, LANES - MLA_NOPE - MLA_ROPE
    rows = [jnp.tile(jnp.concatenate([ia, ia, z(rest)]), 2),
            jnp.tile(jnp.concatenate([o(half), z(half + rest)]), 2),
            jnp.tile(jnp.concatenate([z(half), o(half), z(rest)]), 2),
            jnp.concatenate([z(MLA_NOPE), im, im, z(tail)]),
            jnp.concatenate([z(MLA_NOPE), o(mh), z(mh + tail)]),
            jnp.concatenate([z(MLA_NOPE + mh), o(mh), z(tail)]),
            z(LANES), z(LANES)]
    return jnp.stack(rows)


KR_LO, KR_HI = 4480, 4512
W_IN_SHARD = D_IN // N_DEV
BG_SPLIT = 6 * W_IN_SHARD - KR_HI


def w_in_working_t(g):
    pad_lo, pad_hi = MLA_NOPE, LANES - MLA_NOPE - MLA_ROPE
    spans = []
    for lo, hi, shift in ((0, KR_LO, 0), (KR_LO, KR_HI, pad_lo), (KR_HI, D_IN, pad_lo + pad_hi)):
        r = lo
        while r < hi:
            j = r // W_IN_SHARD
            n = min(hi, (j + 1) * W_IN_SHARD) - r
            spans.append((j, r - j * W_IN_SHARD, n, r + shift))
            r += n

    def body(g_ref, o_ref):
        o_ref[KR_LO:KR_LO + pad_lo, :] = jnp.zeros((pad_lo, D_MODEL), o_ref.dtype)
        o_ref[KR_HI + pad_lo:KR_HI + pad_lo + pad_hi, :] = jnp.zeros((pad_hi, D_MODEL), o_ref.dtype)
        for j, src, n, dst in spans:
            o_ref[dst:dst + n, :] = g_ref[j, src:src + n, :]

    return pl.pallas_call(body, out_shape=SDS((D_INW, D_MODEL), g.dtype), name="w_in_working_t", compiler_params=_params())(g)


def _w_in_shard_5(d_ag_tail, d_cc, d_bg_head):
    kr = MLA_Q_RANK + MLA_KV_RANK + MLA_NOPE
    rows = jnp.concatenate([d_ag_tail, d_cc[:MLA_Q_RANK + MLA_KV_RANK], d_cc[kr:kr + MLA_ROPE], d_bg_head], 0)
    return rows.reshape(1, W_IN_SHARD, D_MODEL).astype(BF16)


def _w_uq_working(g):
    w = jnp.pad(g.transpose(1, 0, 2), ((0, 0), (0, 0), (0, LANES - MLA_NOPE - MLA_ROPE)))
    return w.reshape(MLA_Q_RANK, MLA_QW)


def _w_uq_parts(dw):
    return dw.reshape(MLA_Q_RANK, MLA_HEADS, LANES)[:, :, :MLA_NOPE + MLA_ROPE].transpose(1, 0, 2)


def _w_ukv_working(g):
    wk = jnp.pad(g[:, :, :MLA_NOPE].transpose(1, 0, 2), ((0, 0), (0, 0), (0, LANES - MLA_NOPE)))
    wv = g[:, :, MLA_NOPE:].transpose(1, 0, 2)
    return jnp.concatenate([wk.reshape(MLA_KV_RANK, MLA_QW), wv.reshape(MLA_KV_RANK, MLA_WIDTH)], 1)


def _w_ukv_parts(dw):
    dk = dw[:, :MLA_QW].reshape(MLA_KV_RANK, MLA_HEADS, LANES)[:, :, :MLA_NOPE]
    dv = dw[:, MLA_QW:].reshape(MLA_KV_RANK, MLA_HEADS, MLA_V)
    return jnp.concatenate([dk, dv], -1).transpose(1, 0, 2)


SMALL_NAMES = ("g_emb", "b_emb", "g_cq", "g_ckv", "g_out_a", "g_out_b", "g_out_m", "g_post", "b_post")


def kernel(x, mem, positions, g_emb, b_emb, w_in, g_cq, g_ckv, w_uq, w_ukv, w_mem_kv, g_out_a, g_out_b, g_out_m, w_out, g_post, b_post, loss_target, m_g_emb, m_b_emb, m_w_in, m_g_cq, m_g_ckv, m_w_uq, m_w_ukv, m_w_mem_kv, m_g_out_a, m_g_out_b, m_g_out_m, m_w_out, m_g_post, m_b_post, v_g_emb, v_b_emb, v_w_in, v_g_cq, v_g_ckv, v_w_uq, v_w_ukv, v_w_mem_kv, v_g_out_a, v_g_out_b, v_g_out_m, v_w_out, v_g_post, v_b_post):
    nb = x.shape[0]
    t = nb * SEQ
    x2 = x.reshape(t, D_MODEL)
    tgt2 = loss_target.reshape(t, D_MODEL)
    mem2 = mem.reshape(nb * N_MEM, D_MODEL)
    g_emb2, b_emb2 = g_emb.reshape(1, -1), b_emb.reshape(1, -1)

    w_in_t, m_w_in_t, v_w_in_t = w_in[0].T, m_w_in[0].T, v_w_in[0].T
    s_in, s_uq, s_ukv, s_mem, s_out = cast_shards((w_in_t, w_uq[0], w_ukv[0], w_mem_kv[0], w_out[0]))
    (h32, h16, (a_c, a_sa, a_sb), (m_c, m_sa, m_sb)), (g_in,) = embed_fwd(
        x2, g_emb2, b_emb2, positions, ((s_in,), None, (lax.empty((N_DEV,) + s_in.shape, BF16),)))
    win_t = w_in_working_t(g_in)

    proj = mm_nn(h16, win_t, F32, 512, 1536, "proj", rhs_transposed=True)
    later = (s_uq, s_ukv, s_mem, s_out)
    (ya, lse_a), qkv_d, (g_uq, g_ukv, g_mem, g_out) = a_attn_fwd(
        proj, a_c, a_sa, a_sb, nb,
        (later, (ALL_DEVICES,) * len(later), tuple(lax.empty((N_DEV,) + w.shape, BF16) for w in later)))
    wuq_w = _w_uq_working(g_uq)
    wkv_w = _w_ukv_working(g_ukv)
    wmem = g_mem.reshape(D_MODEL, 2 * MEM_WIDTH)
    wout = g_out.reshape(D_MIX, D_MODEL)
    qb, kb, vb = mla_prep_fwd(proj, m_c, m_sa, m_sb, g_cq, g_ckv, wuq_w, wkv_w)
    yb, lse_b = mla_attn_fwd(qb, kb, vb, nb)
    mkv = mm_nn(mem2, wmem, BF16, nb * N_MEM, 512, "mem_kv")
    ym = mem_attn_fwd(proj, mkv, nb)
    z, du32, du16, loss_sum, dg_post, db_post = gate_out_ln_loss(
        ya, yb, ym, proj, g_out_a, g_out_b, g_out_m, wout, h32, tgt2, g_post, b_post)

    dya, dyb, dym, dag, dbg, dmg, dg_out_a, dg_out_b, dg_out_m = gate_bwd(
        du16, wout, ya, yb, ym, proj, g_out_a, g_out_b, g_out_m)
    dw_out = mm_tn(z, du16, 1024, "dw_out")
    dmq, dmk, dmv = mem_attn_bwd(proj, mkv, dym, nb)
    dw_mem = mm_tn(mem2, jnp.concatenate([dmk, dmv], 1), nb * N_MEM, "dw_mem")
    d_gates, shards_6_7 = mm_tn_group((dbg, dmq, dmg), h16, 2048, "dw_in_bg_mq_mg", W_IN_SHARD, BG_SPLIT, 2)
    landing = lambda w, dtype=F32: lax.empty((N_DEV,) + w.shape, dtype)
    big_w = (w_in_t, w_uq[0], w_ukv[0], w_mem_kv[0], w_out[0])
    (daq, dak, dav), (p_out, p_mem, p_in) = a_attn_bwd(
        qkv_d, a_c, a_sa, a_sb, dya, ya, lse_a, nb,
        ((dw_out.reshape(N_DEV, D_MIX // N_DEV, D_MODEL), dw_mem.reshape(N_DEV, D_MODEL // N_DEV, 2 * MEM_WIDTH),
          shards_6_7),
         (ALL_DEVICES, ALL_DEVICES, (6, 7)),
         (landing(w_out[0]), landing(w_mem_kv[0]), landing(w_in_t, BF16))))
    d_a, shards_0_4 = mm_tn_group((daq, dak, dav, dag), h16, 1024, "dw_in_aq_ak_av_ag", W_IN_SHARD, 0, 5)
    (dqb, dkb, dvb), (p_in,) = mla_attn_bwd(
        qb, kb, vb, dyb, yb, lse_b, nb, ((shards_0_4,), ((0, 1, 2, 3, 4),), (p_in,)))
    dcc, dqf, cqn, dkvf, ckvn, dg_cq, dg_ckv = mla_prep_bwd(proj, m_c, m_sa, m_sb, g_cq, g_ckv, wuq_w, wkv_w, dqb, dkb, dvb)
    dw_uq = mm_tn(cqn, dqf, 2048, "dw_uq")
    dw_ukv = mm_tn(ckvn, dkvf, 2048, "dw_ukv")
    d_cc = mm_tn(dcc, h16, 2048, "dw_in_cc")
    pieces = (daq, dak, dav, dag, dcc, dbg, dmq, dmg)
    (grad_x, dg_emb, db_emb), (p_in, p_uq, p_ukv) = dh_ln_bwd(
        pieces, win_t, du32, x2, g_emb2,
        ((_w_in_shard_5(d_a[5 * W_IN_SHARD:], d_cc, d_gates[:BG_SPLIT]), _w_uq_parts(dw_uq), _w_ukv_parts(dw_ukv)),
         ((5,), ALL_DEVICES, ALL_DEVICES),
         (p_in, landing(w_uq[0]), landing(w_ukv[0]))))

    parts = (p_in, p_uq, p_ukv, p_mem, p_out)
    big_m = (m_w_in_t, m_w_uq[0], m_w_ukv[0], m_w_mem_kv[0], m_w_out[0])
    big_v = (v_w_in_t, v_w_uq[0], v_w_ukv[0], v_w_mem_kv[0], v_w_out[0])
    big = {}
    for name, w, p, m, v in zip(("w_in", "w_uq", "w_ukv", "w_mem_kv", "w_out"), big_w, parts, big_m, big_v):
        res = adamw_shard(w, p, m, v, "adamw_" + name)
        big[name] = [(o.T if name == "w_in" else o)[None] for o in res]

    small_w = (g_emb, b_emb, g_cq, g_ckv, g_out_a, g_out_b, g_out_m, g_post, b_post)
    small_m = (m_g_emb, m_b_emb, m_g_cq, m_g_ckv, m_g_out_a, m_g_out_b, m_g_out_m, m_g_post, m_b_post)
    small_v = (v_g_emb, v_b_emb, v_g_cq, v_g_ckv, v_g_out_a, v_g_out_b, v_g_out_m, v_g_post, v_b_post)
    small_g = (dg_emb, db_emb, dg_cq, dg_ckv, dg_out_a, dg_out_b, dg_out_m, dg_post, db_post)
    rows128 = lambda vals: [v.reshape(-1, LANES) for v in vals]
    res = small_allreduce_adamw(loss_sum, rows128(small_g), rows128(small_w), rows128(small_m), rows128(small_v))
    loss = res[0][0, 0]
    n_small = len(small_w)
    sg, sd, sm, sv = [[r.reshape(w.shape) for r, w in zip(res[1 + k * n_small:1 + (k + 1) * n_small], small_w)]
                      for k in range(4)]

    order = ("g_emb", "b_emb", "w_in", "g_cq", "g_ckv", "w_uq", "w_ukv", "w_mem_kv", "g_out_a", "g_out_b", "g_out_m",
             "w_out", "g_post", "b_post")
    small_idx = {n: i for i, n in enumerate(SMALL_NAMES)}
    outs = [loss, grad_x.reshape(x.shape)]
    for kind in range(4):
        for name in order:
            outs.append(big[name][kind] if name in big else (sg, sd, sm, sv)[kind][small_idx[name]])
    return tuple(outs)
```

```python
import functools

import jax
import jax.numpy as jnp
from jax import lax
from jax.experimental import pallas as pl
from jax.experimental.pallas import tpu as pltpu

F32 = jnp.float32
BF16 = jnp.bfloat16
SDS = jax.ShapeDtypeStruct
MESH = pl.DeviceIdType.MESH

D_MODEL = 1024
SEQ = 2048
A_HEADS, A_HEAD_DIM, A_ROT = 16, 64, 16
A_WIDTH = 1024
DILATIONS = (1, 4, 16)
N_SIDE = 64
MLA_HEADS, MLA_Q_RANK, MLA_KV_RANK = 8, 256, 128
MLA_NOPE, MLA_ROPE, MLA_V = 64, 32, 64
MLA_WIDTH = 512
N_MEM, MEM_HEADS, MEM_HEAD_DIM, MEM_WIDTH = 256, 4, 128, 512
ROPE_THETA = 500000.0
NORM_EPS = 1e-5
NEG_INF = -1e30
ALPHA = 2.0 ** 0.25
D_IN = 6048
N_DEV = 8

ADAM_LR, ADAM_B1, ADAM_B2, ADAM_EPS, ADAM_WD, ADAM_STEP = 0.001, 0.9, 0.999, 1e-08, 0.01, 10

D_INW = 6144
PIECE_WIDTHS = (1024, 1024, 1024, 1024, 512, 512, 512, 512)
PIECE_OFFS = (0, 1024, 2048, 3072, 4096, 4608, 5120, 5632)
LANES = 128
VMEM_LIMIT = 56 * 1024 * 1024


def _params(*sem):
    kw = dict(vmem_limit_bytes=VMEM_LIMIT)
    if sem:
        kw["dimension_semantics"] = sem
    return pltpu.CompilerParams(**kw)


def _dot(a, b):
    return jnp.dot(a, b, preferred_element_type=F32)


def _dot_nt(a, b):
    return lax.dot_general(a, b, (((1,), (1,)), ((), ())), preferred_element_type=F32)


def _dot_tn(a, b):
    return lax.dot_general(a, b, (((0,), (0,)), ((), ())), preferred_element_type=F32)


def _sigmoid(x):
    return 1.0 / (1.0 + jnp.exp(-x))


def _rope_fwd(x, c, sa, sb, half):
    n = x.shape[-1]
    return x * c + pltpu.roll(x, n - half, 1) * sa + pltpu.roll(x, half, 1) * sb


def _rope_bwd(dy, c, sa, sb, half):
    n = dy.shape[-1]
    return dy * c + pltpu.roll(dy * sa, half, 1) + pltpu.roll(dy * sb, n - half, 1)


def mm_nn(a, b, out_dtype, tm, tn, name, rhs_transposed=False):
    m, k = a.shape
    n = b.shape[0] if rhs_transposed else b.shape[1]
    dot = _dot_nt if rhs_transposed else _dot

    def body(a_ref, b_ref, o_ref):
        o_ref[...] = dot(a_ref[...].astype(BF16), b_ref[...].astype(BF16)).astype(o_ref.dtype)

    b_spec = pl.BlockSpec((tn, k), lambda j, i: (j, 0)) if rhs_transposed else pl.BlockSpec((k, tn), lambda j, i: (0, j))
    return pl.pallas_call(
        body, grid=(n // tn, m // tm),
        in_specs=[pl.BlockSpec((tm, k), lambda j, i: (i, 0)), b_spec],
        out_specs=pl.BlockSpec((tm, tn), lambda j, i: (i, j)),
        out_shape=SDS((m, n), out_dtype), name=name,
        compiler_params=_params("parallel", "parallel"))(a, b)


def mm_tn(a, b, tt, name):
    t, m = a.shape
    n = b.shape[1]

    def body(a_ref, b_ref, o_ref):
        @pl.when(pl.program_id(0) == 0)
        def _():
            o_ref[...] = jnp.zeros_like(o_ref)

        o_ref[...] += _dot_tn(a_ref[...].astype(BF16), b_ref[...].astype(BF16))

    return pl.pallas_call(
        body, grid=(t // tt,),
        in_specs=[pl.BlockSpec((tt, m), lambda i: (i, 0)), pl.BlockSpec((tt, n), lambda i: (i, 0))],
        out_specs=pl.BlockSpec((m, n), lambda i: (0, 0)),
        out_shape=SDS((m, n), F32), name=name,
        compiler_params=_params("arbitrary"))(a, b)


def mm_tn_group(pieces, b, tt, name, slab_rows, first_slab_row, n_slabs):
    n, (t, w), cols = len(pieces), pieces[0].shape, b.shape[1]
    nt = t // tt

    def body(*refs):
        p_refs, b_ref, o_ref, slab_ref = refs[:n], refs[n], refs[n + 1], refs[n + 2]

        @pl.when(pl.program_id(1) == 0)
        def _():
            o_ref[...] = jnp.zeros_like(o_ref)

        for k in range(n):
            @pl.when(pl.program_id(0) == k)
            def _(k=k):
                o_ref[...] += _dot_tn(p_refs[k][...], b_ref[...])

            @pl.when((pl.program_id(0) == k) & (pl.program_id(1) == nt - 1))
            def _(k=k):
                for j in range(n_slabs):
                    lo = max(k * w, first_slab_row + j * slab_rows)
                    hi = min((k + 1) * w, first_slab_row + (j + 1) * slab_rows)
                    if lo < hi:
                        dst = lo - first_slab_row - j * slab_rows
                        slab_ref[j, dst:dst + hi - lo, :] = o_ref[lo - k * w:hi - k * w, :].astype(slab_ref.dtype)

    def piece_spec(k):
        return pl.BlockSpec((tt, w), lambda p, i: (jnp.where(p < k, 0, jnp.where(p > k, nt - 1, i)), 0))

    return pl.pallas_call(
        body, grid=(n, nt),
        in_specs=[piece_spec(k) for k in range(n)] + [pl.BlockSpec((tt, cols), lambda p, i: (i, 0))],
        out_specs=[pl.BlockSpec((w, cols), lambda p, i: (p, 0)),
                   pl.BlockSpec((n_slabs, slab_rows, cols), lambda p, i: (0, 0, 0))],
        out_shape=[SDS((n * w, cols), F32), SDS((n_slabs, slab_rows, cols), BF16)], name=name,
        compiler_params=_params("arbitrary", "arbitrary"))(*pieces, b)


def embed_fwd(x2, g, b, positions, xch):
    t, d = x2.shape
    tm = 512
    pos = positions.astype(F32).reshape(-1, 1)

    def body(x_ref, g_ref, b_ref, pos_ref, pat_ref, h32_ref, h16_ref, *tabs):
        x = x_ref[...]
        mu = jnp.mean(x, axis=-1, keepdims=True)
        xc = x - mu
        var = jnp.mean(xc * xc, axis=-1, keepdims=True)
        h = xc * lax.rsqrt(var + NORM_EPS) * g_ref[...] + b_ref[...]
        h32_ref[...] = h
        h16_ref[...] = h.astype(BF16)
        p = pos_ref[...]
        for k in range(2):
            inv, first, second = pat_ref[3 * k:3 * k + 1, :], pat_ref[3 * k + 1:3 * k + 2, :], pat_ref[3 * k + 2:3 * k + 3, :]
            ang = p * inv
            sn = jnp.sin(ang)
            tabs[3 * k][...] = jnp.where(first + second > 0.0, jnp.cos(ang), 1.0)
            tabs[3 * k + 1][...] = -first * sn
            tabs[3 * k + 2][...] = second * sn

    row = pl.BlockSpec((tm, d), lambda i: (i, 0))
    vec = pl.BlockSpec((1, d), lambda i: (0, 0))
    tab = pl.BlockSpec((tm, LANES), lambda i: (i, 0))
    res, landed = call_hosting_exchange(
        body, xch, grid=(t // tm,),
        in_specs=[row, vec, vec, pl.BlockSpec((tm, 1), lambda i: (i, 0)), pl.BlockSpec((8, LANES), lambda i: (0, 0))],
        out_specs=[row, row] + [tab] * 6,
        out_shape=[SDS((t, d), F32), SDS((t, d), BF16)] + [SDS((t, LANES), F32)] * 6,
        scratch_shapes=[], name="embed_fwd", operands=(x2, g, b, pos, _rope_lane_patterns()))
    return (res[0], res[1], tuple(res[2:5]), tuple(res[5:8])), landed


Q_BLK = 128
UNROLL_FWD = 16
UNROLL_BWD = 16


def _pattern_geometry(d):
    length = SEQ // d
    nblk = length // Q_BLK
    kwin = min(2 * Q_BLK, length)
    return length, nblk, kwin


def _block_coords(idx, d):
    length, nblk, kwin = _pattern_geometry(d)
    r = lax.shift_right_logical(idx, nblk.bit_length() - 1)
    i = idx & (nblk - 1)
    q0 = pl.multiple_of(r * length + i * Q_BLK, Q_BLK)
    ks = jnp.clip(i * Q_BLK - N_SIDE, 0, length - kwin)
    k0 = pl.multiple_of(r * length + ks, N_SIDE)
    qpos = i * Q_BLK + lax.broadcasted_iota(jnp.int32, (Q_BLK, kwin), 0)
    kpos = ks + lax.broadcasted_iota(jnp.int32, (Q_BLK, kwin), 1)
    valid = jnp.abs(kpos - qpos) <= N_SIDE
    return q0, k0, kwin, valid


def _deinterleave(src_ref, dst_ref, d, dtype, tmp_ref):
    if d == 1:
        dst_ref[...] = src_ref[...].astype(dtype)
        return
    q = SEQ // 4
    if d == 4:
        for r in range(4):
            dst_ref[r * q:(r + 1) * q, :] = src_ref[pl.ds(r, q, stride=4), :].astype(dtype)
        return
    assert d == 16
    n = SEQ // 16
    for r in range(4):
        tmp_ref[r * q:(r + 1) * q, :] = src_ref[pl.ds(r, q, stride=4), :]
    for r in range(4):
        for j in range(4):
            dst_ref[(r + 4 * j) * n:(r + 4 * j + 1) * n, :] = tmp_ref[pl.ds(r * q + j, n, stride=4), :].astype(dtype)


def _class16_to_class4(src_ref, dst_ref):
    q, n = SEQ // 4, SEQ // 16
    for r in range(4):
        for j in range(4):
            dst_ref[pl.ds(r * q + j, n, stride=4), :] = src_ref[(r + 4 * j) * n:(r + 4 * j + 1) * n, :]


def _interleave(src_ref, dst_ref, d, tmp_ref, accumulate):
    q = SEQ // 4
    if d == 16:
        _class16_to_class4(src_ref, tmp_ref)
        src_ref = tmp_ref
    else:
        assert d == 4
    for r in range(4):
        rows = pl.ds(r, q, stride=4)
        val = src_ref[r * q:(r + 1) * q, :]
        dst_ref[rows, :] = dst_ref[rows, :] + val if accumulate else val


def a_attn_fwd(proj, ca, sa, sb, nb, xch):
    t = proj.shape[0]
    n_pairs = A_WIDTH // LANES

    def body(q_ref, k_ref, v_ref, c_ref, sa_ref, sb_ref, y_ref, lse_ref, *rest):
        qkv_d, (qr_s, kr_s, oc_s, lc_s, o1_s, l1_s, o2_s, l2_s, o3_s, l3_s, tmp_s) = rest[:9], rest[9:]
        c, s_a, s_b = c_ref[...], sa_ref[...], sb_ref[...]
        qr_s[...] = _rope_fwd(q_ref[...], c, s_a, s_b, A_ROT // 2) * (A_HEAD_DIM ** -0.5)
        kr_s[...] = _rope_fwd(k_ref[...], c, s_a, s_b, A_ROT // 2)
        head0 = lax.broadcasted_iota(jnp.int32, (Q_BLK, LANES), 1) < A_HEAD_DIM
        nat = ((o1_s, l1_s), (o2_s, l2_s), (o3_s, l3_s))

        for g, d in enumerate(DILATIONS):
            qd_s, kd_s, vd_s = qkv_d[3 * g:3 * g + 3]
            _deinterleave(qr_s, qd_s, d, BF16, tmp_s)
            _deinterleave(kr_s, kd_s, d, BF16, tmp_s)
            _deinterleave(v_ref, vd_s, d, BF16, tmp_s)
            o_dst, l_dst = (nat[g] if d == 1 else (oc_s, lc_s))

            def block(idx, carry, d=d, o_dst=o_dst, l_dst=l_dst, qd_s=qd_s, kd_s=kd_s, vd_s=vd_s):
                q0, k0, kwin, valid = _block_coords(idx, d)
                qb = qd_s[pl.ds(q0, Q_BLK), :]
                kb = kd_s[pl.ds(k0, kwin), :]
                vb = vd_s[pl.ds(k0, kwin), :]
                zero = jnp.zeros_like(qb)
                q2 = jnp.concatenate([jnp.where(head0, qb, zero), jnp.where(head0, zero, qb)], 0)
                s = jnp.where(jnp.concatenate([valid, valid], 0), _dot_nt(q2, kb), NEG_INF)
                m = jnp.max(s, axis=-1, keepdims=True)
                p = jnp.exp(s - m)
                l = jnp.sum(p, axis=-1, keepdims=True)
                o2 = _dot(p.astype(BF16), vb) / l
                l2 = m + jnp.log(l)
                o_dst[pl.ds(q0, Q_BLK), :] = jnp.where(head0, o2[:Q_BLK], o2[Q_BLK:])
                l_dst[pl.ds(q0, Q_BLK), :] = jnp.where(head0, l2[:Q_BLK], l2[Q_BLK:])
                return carry

            lax.fori_loop(0, SEQ // Q_BLK, block, 0, unroll=UNROLL_FWD)
            if d > 1:
                _interleave(oc_s, nat[g][0], d, tmp_s, False)
                _interleave(lc_s, nat[g][1], d, tmp_s, False)

        def merge(ci, carry):
            rows = pl.ds(pl.multiple_of(ci * 256, 256), 256)
            l1, l2, l3 = l1_s[rows, :], l2_s[rows, :], l3_s[rows, :]
            m = jnp.maximum(jnp.maximum(l1, l2), l3)
            w1, w2, w3 = jnp.exp(l1 - m), jnp.exp(l2 - m), jnp.exp(l3 - m)
            w = w1 + w2 + w3
            y_ref[rows, :] = (w1 * o1_s[rows, :] + w2 * o2_s[rows, :] + w3 * o3_s[rows, :]) / w
            lse_ref[rows, :] = m + jnp.log(w)
            return carry

        lax.fori_loop(0, SEQ // 256, merge, 0)

    def col(off):
        return pl.BlockSpec((SEQ, LANES), lambda b, hp: (b, off + hp))

    tab = pl.BlockSpec((SEQ, LANES), lambda b, hp: (b, 0))
    out = pl.BlockSpec((SEQ, LANES), lambda b, hp: (b, hp))
    f32s = pltpu.VMEM((SEQ, LANES), F32)
    res, landed = call_hosting_exchange(
        body, xch, grid=(nb, n_pairs),
        in_specs=[col(0), col(n_pairs), col(2 * n_pairs), tab, tab, tab],
        out_specs=[out] * 11,
        out_shape=[SDS((t, A_WIDTH), F32)] * 2 + [SDS((t, A_WIDTH), BF16)] * 9,
        scratch_shapes=[f32s] * 11,
        name="a_attn_fwd", operands=(proj, proj, proj, ca, sa, sb))
    return res[:2], res[2:], landed


def a_attn_bwd(qkv_d, ca, sa, sb, dy, y, lse, nb, xch):
    t = dy.shape[0]
    n_pairs = A_WIDTH // LANES

    def body(*refs):
        qkv_refs = refs[:9]
        (c_ref, sa_ref, sb_ref, do_ref, y_ref, lse_ref, dq_ref, dk_ref, dv_ref,
         l0n_s, l1n_s, d0n_s, d1n_s, dod_s, l0d_s, l1d_s, d0d_s, d1d_s,
         dqc_s, dkc_s, dvc_s, dq4_s, dk4_s, dv4_s, dqn_s, dkn_s, dvn_s, tmp_s) = refs[9:]
        c, s_a, s_b = c_ref[...], sa_ref[...], sb_ref[...]
        head0 = lax.broadcasted_iota(jnp.int32, (Q_BLK, LANES), 1) < A_HEAD_DIM

        def per_head_rows(ci, carry):
            rows = pl.ds(pl.multiple_of(ci * 256, 256), 256)
            h0 = lax.broadcasted_iota(jnp.int32, (256, LANES), 1) < A_HEAD_DIM
            tt = do_ref[rows, :] * y_ref[rows, :]
            d0n_s[rows, :] = jnp.broadcast_to(jnp.sum(jnp.where(h0, tt, 0.0), axis=-1, keepdims=True), (256, LANES))
            d1n_s[rows, :] = jnp.broadcast_to(jnp.sum(jnp.where(h0, 0.0, tt), axis=-1, keepdims=True), (256, LANES))
            l = lse_ref[rows, :]
            lr = pltpu.roll(l, A_HEAD_DIM, 1)
            l0n_s[rows, :] = jnp.where(h0, l, lr)
            l1n_s[rows, :] = jnp.where(h0, lr, l)
            return carry

        lax.fori_loop(0, SEQ // 256, per_head_rows, 0)
        assert DILATIONS == (1, 4, 16)

        for g, d in enumerate(DILATIONS):
            qd_s, kd_s, vd_s = qkv_refs[3 * g:3 * g + 3]
            _deinterleave(do_ref, dod_s, d, BF16, tmp_s)
            if d > 1:
                for src, dst in ((l0n_s, l0d_s), (l1n_s, l1d_s), (d0n_s, d0d_s), (d1n_s, d1d_s)):
                    _deinterleave(src, dst, d, F32, tmp_s)
            l0, l1, d0, d1 = (l0n_s, l1n_s, d0n_s, d1n_s) if d == 1 else (l0d_s, l1d_s, d0d_s, d1d_s)
            dq_dst, dk_dst, dv_dst = {1: (dqn_s, dkn_s, dvn_s), 4: (dq4_s, dk4_s, dv4_s), 16: (dqc_s, dkc_s, dvc_s)}[d]
            dk_dst[...] = jnp.zeros_like(dk_dst)
            dv_dst[...] = jnp.zeros_like(dv_dst)

            def block(idx, carry, d=d, l0=l0, l1=l1, d0=d0, d1=d1, dq_dst=dq_dst, dk_dst=dk_dst, dv_dst=dv_dst,
                      qd_s=qd_s, kd_s=kd_s, vd_s=vd_s):
                q0, k0, kwin, valid = _block_coords(idx, d)
                qrows = pl.ds(q0, Q_BLK)
                krows = pl.ds(k0, kwin)
                qb, dob = qd_s[qrows, :], dod_s[qrows, :]
                kb, vb = kd_s[krows, :], vd_s[krows, :]
                zero = jnp.zeros_like(qb)
                q2 = jnp.concatenate([jnp.where(head0, qb, zero), jnp.where(head0, zero, qb)], 0)
                do2 = jnp.concatenate([jnp.where(head0, dob, zero), jnp.where(head0, zero, dob)], 0)
                wide = lambda x: jnp.concatenate([x] * (kwin // LANES), 1)
                lse2 = wide(jnp.concatenate([l0[qrows, :], l1[qrows, :]], 0))
                dd2 = wide(jnp.concatenate([d0[qrows, :], d1[qrows, :]], 0))
                s = jnp.where(jnp.concatenate([valid, valid], 0), _dot_nt(q2, kb), NEG_INF)
                p = jnp.exp(s - lse2)
                ds = (p * (_dot_nt(do2, vb) - dd2)).astype(BF16)
                dq2 = _dot(ds, kb)
                dq_dst[qrows, :] = jnp.where(head0, dq2[:Q_BLK], dq2[Q_BLK:])
                dk_dst[krows, :] += _dot_tn(ds, q2)
                dv_dst[krows, :] += _dot_tn(p.astype(BF16), do2)
                return carry

            lax.fori_loop(0, SEQ // Q_BLK, block, 0, unroll=UNROLL_BWD)

        for c16, c4, nat in ((dqc_s, dq4_s, dqn_s), (dkc_s, dk4_s, dkn_s), (dvc_s, dv4_s, dvn_s)):
            _class16_to_class4(c16, tmp_s)
            c4[...] = c4[...] + tmp_s[...]
            _interleave(c4, nat, 4, tmp_s, True)

        dq_ref[...] = _rope_bwd(dqn_s[...] * (A_HEAD_DIM ** -0.5), c, s_a, s_b, A_ROT // 2).astype(BF16)
        dk_ref[...] = _rope_bwd(dkn_s[...], c, s_a, s_b, A_ROT // 2).astype(BF16)
        dv_ref[...] = dvn_s[...].astype(BF16)

    tab = pl.BlockSpec((SEQ, LANES), lambda b, hp: (b, 0))
    blk = pl.BlockSpec((SEQ, LANES), lambda b, hp: (b, hp))
    f32s = pltpu.VMEM((SEQ, LANES), F32)
    b16s = pltpu.VMEM((SEQ, LANES), BF16)
    return call_hosting_exchange(
        body, xch, grid=(nb, n_pairs),
        in_specs=[blk] * 9 + [tab, tab, tab, blk, blk, blk],
        out_specs=[blk, blk, blk],
        out_shape=[SDS((t, A_WIDTH), BF16)] * 3,
        scratch_shapes=[f32s] * 4 + [b16s] + [f32s] * 14,
        name="a_attn_bwd", operands=(*qkv_d, ca, sa, sb, dy, y, lse))


MLA_SCALE = (MLA_NOPE + MLA_ROPE) ** -0.5
LOG2E = 1.4426950408889634
MLA_QW = MLA_HEADS * LANES
MLA_KVW = MLA_QW + MLA_WIDTH


def _rms(x, g):
    r = lax.rsqrt(jnp.mean(x * x, axis=-1, keepdims=True) + NORM_EPS)
    return x * r * g, r


def _rms_bwd(dn, x, r, g):
    tg = dn * g
    dx = r * tg - x * (r * r * r) * jnp.mean(tg * x, axis=-1, keepdims=True)
    return dx, jnp.sum(dn * x * r, axis=0, keepdims=True)


def mla_prep_fwd(proj, cm, sma, smb, g_cq, g_ckv, wuq, wkv):
    t = proj.shape[0]
    tm = 512

    def body(cq_ref, ckv_ref, kr_ref, c_ref, sa_ref, sb_ref, gq_ref, gkv_ref, wuq_ref, wkv_ref, q_ref, k_ref, v_ref):
        c, s_a, s_b = c_ref[...], sa_ref[...], sb_ref[...]
        cqn, _ = _rms(cq_ref[...], gq_ref[...])
        qf = _dot(cqn.astype(BF16), wuq_ref[...])
        ckvn, _ = _rms(ckv_ref[...], gkv_ref[...])
        kvf = _dot(ckvn.astype(BF16), wkv_ref[...])
        krope = _rope_fwd(kr_ref[...], c, s_a, s_b, MLA_ROPE // 2)
        for h in range(MLA_HEADS):
            cols = slice(h * LANES, (h + 1) * LANES)
            q_ref[:, cols] = (_rope_fwd(qf[:, cols], c, s_a, s_b, MLA_ROPE // 2) * (MLA_SCALE * LOG2E)).astype(BF16)
            k_ref[:, cols] = (kvf[:, cols] + krope).astype(BF16)
        v_ref[...] = kvf[:, MLA_QW:].astype(BF16)

    def row(w, j):
        return pl.BlockSpec((tm, w), lambda i: (i, j))

    def full(a):
        return pl.BlockSpec(a.shape, lambda i: (0, 0))

    return pl.pallas_call(
        body, grid=(t // tm,),
        in_specs=[row(256, 4096 // 256), row(128, 4352 // 128), row(128, 4480 // 128), row(128, 0), row(128, 0), row(128, 0),
                  full(g_cq), full(g_ckv), full(wuq), full(wkv)],
        out_specs=[row(MLA_QW, 0), row(MLA_QW, 0), row(MLA_WIDTH, 0)],
        out_shape=[SDS((t, MLA_QW), BF16), SDS((t, MLA_QW), BF16), SDS((t, MLA_WIDTH), BF16)],
        name="mla_prep_fwd", compiler_params=_params("parallel"))(proj, proj, proj, cm, sma, smb, g_cq, g_ckv, wuq, wkv)


def mla_prep_bwd(proj, cm, sma, smb, g_cq, g_ckv, wuq, wkv, dq, dk, dv):
    t = proj.shape[0]
    tm = 512

    def body(cq_ref, ckv_ref, c_ref, sa_ref, sb_ref, gq_ref, gkv_ref, wuq_ref, wkv_ref, dq_ref, dk_ref, dv_ref,
             dcc_ref, dqf_ref, cqn_ref, dkvf_ref, ckvn_ref, dgq_ref, dgkv_ref):
        @pl.when(pl.program_id(0) == 0)
        def _():
            dgq_ref[...] = jnp.zeros_like(dgq_ref)
            dgkv_ref[...] = jnp.zeros_like(dgkv_ref)

        c, s_a, s_b = c_ref[...], sa_ref[...], sb_ref[...]
        cq, ckv = cq_ref[...], ckv_ref[...]
        cqn, rq = _rms(cq, gq_ref[...])
        ckvn, rkv = _rms(ckv, gkv_ref[...])
        cqn_ref[...] = cqn.astype(BF16)
        ckvn_ref[...] = ckvn.astype(BF16)
        lane = lax.broadcasted_iota(jnp.int32, (tm, LANES), 1)
        rope_lanes = (lane >= MLA_NOPE) & (lane < MLA_NOPE + MLA_ROPE)
        dkrope = jnp.zeros((tm, LANES), F32)
        for h in range(MLA_HEADS):
            cols = slice(h * LANES, (h + 1) * LANES)
            dqf_ref[:, cols] = _rope_bwd(dq_ref[:, cols] * MLA_SCALE, c, s_a, s_b, MLA_ROPE // 2).astype(BF16)
            dkh = dk_ref[:, cols] * (1.0 / LOG2E)
            dkvf_ref[:, cols] = dkh.astype(BF16)
            dkrope = dkrope + dkh
        dkvf_ref[:, MLA_QW:] = dv_ref[...].astype(BF16)
        dkr = _rope_bwd(jnp.where(rope_lanes, dkrope, 0.0), c, s_a, s_b, MLA_ROPE // 2)
        dcqn = _dot_nt(dqf_ref[...], wuq_ref[...])
        dckvn = _dot_nt(dkvf_ref[...], wkv_ref[...])
        dcq, dgq = _rms_bwd(dcqn, cq, rq, gq_ref[...])
        dckv, dgkv = _rms_bwd(dckvn, ckv, rkv, gkv_ref[...])
        dgq_ref[...] += dgq
        dgkv_ref[...] += dgkv
        dcc_ref[:, 0:256] = dcq.astype(BF16)
        dcc_ref[:, 256:384] = dckv.astype(BF16)
        dcc_ref[:, 384:512] = dkr.astype(BF16)

    def row(w, j):
        return pl.BlockSpec((tm, w), lambda i: (i, j))

    def full(a):
        return pl.BlockSpec(a.shape, lambda i: (0, 0))

    return pl.pallas_call(
        body, grid=(t // tm,),
        in_specs=[row(256, 4096 // 256), row(128, 4352 // 128), row(128, 0), row(128, 0), row(128, 0),
                  full(g_cq), full(g_ckv), full(wuq), full(wkv), row(MLA_QW, 0), row(MLA_QW, 0), row(MLA_WIDTH, 0)],
        out_specs=[row(512, 0), row(MLA_QW, 0), row(256, 0), row(MLA_KVW, 0), row(128, 0), full(g_cq), full(g_ckv)],
        out_shape=[SDS((t, 512), BF16), SDS((t, MLA_QW), BF16), SDS((t, 256), BF16), SDS((t, MLA_KVW), BF16),
                   SDS((t, 128), BF16), SDS(g_cq.shape, F32), SDS(g_ckv.shape, F32)],
        name="mla_prep_bwd", compiler_params=_params("arbitrary"))(proj, proj, cm, sma, smb, g_cq, g_ckv, wuq, wkv, dq, dk, dv)


MLA_TQ = 1024
MLA_SUB = 256


def mla_attn_fwd(qb, kb, vb, nb):
    t = qb.shape[0]
    nq = SEQ // MLA_TQ
    n_pairs = MLA_HEADS // 2

    def body(q_ref, k_ref, v_ref, y_ref, lse_ref):
        head0 = lax.broadcasted_iota(jnp.int32, (MLA_SUB, LANES), 1) < MLA_V
        v = v_ref[...]
        vhead0 = lax.broadcasted_iota(jnp.int32, v.shape, 1) < MLA_V
        one = jnp.ones_like(v)
        vh = [jnp.where(vhead0 == (h == 0), v, one) for h in range(2)]
        for sub in range(MLA_TQ // MLA_SUB):
            rows = slice(sub * MLA_SUB, (sub + 1) * MLA_SUB)
            outs, lses = [], []
            for h in range(2):
                cols = slice(h * LANES, (h + 1) * LANES)
                s = _dot_nt(q_ref[rows, cols], k_ref[:, cols])
                m = jnp.max(s, axis=-1, keepdims=True)
                p = jnp.exp2(s - m).astype(BF16)
                ol = _dot(p, vh[h])
                l = pltpu.roll(ol, MLA_V, 1)
                outs.append(ol / l)
                lses.append(m + jnp.log2(l))
            y_ref[rows, :] = jnp.where(head0, outs[0], outs[1])
            lse_ref[rows, :] = jnp.where(head0, lses[0], lses[1])

    return pl.pallas_call(
        body, grid=(nb, n_pairs, nq),
        in_specs=[pl.BlockSpec((MLA_TQ, 2 * LANES), lambda b, hp, i: (b * nq + i, hp)),
                  pl.BlockSpec((SEQ, 2 * LANES), lambda b, hp, i: (b, hp)),
                  pl.BlockSpec((SEQ, LANES), lambda b, hp, i: (b, hp))],
        out_specs=[pl.BlockSpec((MLA_TQ, LANES), lambda b, hp, i: (b * nq + i, hp))] * 2,
        out_shape=[SDS((t, MLA_WIDTH), F32)] * 2,
        name="mla_attn_fwd", compiler_params=_params("parallel", "parallel", "parallel"))(qb, kb, vb)


def mla_attn_bwd(qb, kb, vb, dy, y, lse, nb, xch):
    t = qb.shape[0]
    nq = SEQ // MLA_TQ
    n_pairs = MLA_HEADS // 2

    def body(q_ref, k_ref, v_ref, do_ref, y_ref, lse_ref, dq_ref, dk_ref, dv_ref):
        @pl.when(pl.program_id(2) == 0)
        def _():
            dk_ref[...] = jnp.zeros_like(dk_ref)
            dv_ref[...] = jnp.zeros_like(dv_ref)

        head0 = lax.broadcasted_iota(jnp.int32, (MLA_SUB, LANES), 1) < MLA_V
        v = v_ref[...]
        for sub in range(MLA_TQ // MLA_SUB):
            rows = slice(sub * MLA_SUB, (sub + 1) * MLA_SUB)
            do = do_ref[rows, :]
            lse = lse_ref[rows, :]
            tt = do * y_ref[rows, :]
            dv = jnp.zeros((SEQ, LANES), F32)
            for h in range(2):
                sel = head0 if h == 0 else ~head0
                lo = h * MLA_V
                cols = slice(h * LANES, (h + 1) * LANES)
                q = q_ref[rows, cols]
                k = k_ref[:, cols]
                dd = jnp.sum(jnp.where(sel, tt, 0.0), axis=-1, keepdims=True)
                doh = jnp.where(sel, do, 0.0).astype(BF16)
                p = jnp.exp2(_dot_nt(q, k) - lse[:, lo:lo + 1])
                dp = _dot_nt(doh, v)
                ds = (p * (dp - dd)).astype(BF16)
                dq_ref[rows, cols] = _dot(ds, k)
                dk_ref[:, cols] += _dot_tn(ds, q)
                dv = dv + _dot_tn(p.astype(BF16), doh)
            dv_ref[...] += dv

    qspec = pl.BlockSpec((MLA_TQ, 2 * LANES), lambda b, hp, i: (b * nq + i, hp))
    kspec = pl.BlockSpec((SEQ, 2 * LANES), lambda b, hp, i: (b, hp))
    vspec = pl.BlockSpec((SEQ, LANES), lambda b, hp, i: (b, hp))
    ospec = pl.BlockSpec((MLA_TQ, LANES), lambda b, hp, i: (b * nq + i, hp))
    return call_hosting_exchange(
        body, xch, grid=(nb, n_pairs, nq),
        in_specs=[qspec, kspec, vspec, ospec, ospec, ospec],
        out_specs=[qspec, kspec, vspec],
        out_shape=[SDS((t, MLA_QW), F32), SDS((t, MLA_QW), F32), SDS((t, MLA_WIDTH), F32)],
        scratch_shapes=[], name="mla_attn_bwd", operands=(qb, kb, vb, dy, y, lse))


MEM_TQ = 512
MEM_SCALE = MEM_HEAD_DIM ** -0.5
MQ_BLK4 = 5120 // MEM_WIDTH


def mem_attn_fwd(proj, mkv, nb):
    t = proj.shape[0]
    nq = SEQ // MEM_TQ

    def body(q_ref, mk_ref, mv_ref, y_ref):
        for h in range(MEM_HEADS):
            cols = slice(h * LANES, (h + 1) * LANES)
            s = _dot_nt(q_ref[:, cols].astype(BF16), mk_ref[:, cols]) * MEM_SCALE
            m = jnp.max(s, axis=-1, keepdims=True)
            p = jnp.exp(s - m)
            l = jnp.sum(p, axis=-1, keepdims=True)
            y_ref[:, cols] = _dot(p.astype(BF16), mv_ref[:, cols]) / l

    return pl.pallas_call(
        body, grid=(nb, nq),
        in_specs=[pl.BlockSpec((MEM_TQ, MEM_WIDTH), lambda b, i: (b * nq + i, MQ_BLK4)),
                  pl.BlockSpec((N_MEM, MEM_WIDTH), lambda b, i: (b, 0)),
                  pl.BlockSpec((N_MEM, MEM_WIDTH), lambda b, i: (b, 1))],
        out_specs=pl.BlockSpec((MEM_TQ, MEM_WIDTH), lambda b, i: (b * nq + i, 0)),
        out_shape=SDS((t, MEM_WIDTH), F32),
        name="mem_attn_fwd", compiler_params=_params("parallel", "parallel"))(proj, mkv, mkv)


def mem_attn_bwd(proj, mkv, dy, nb):
    t = proj.shape[0]
    nq = SEQ // MEM_TQ

    def body(q_ref, mk_ref, mv_ref, do_ref, dq_ref, dmk_ref, dmv_ref):
        @pl.when(pl.program_id(1) == 0)
        def _():
            dmk_ref[...] = jnp.zeros_like(dmk_ref)
            dmv_ref[...] = jnp.zeros_like(dmv_ref)

        for h in range(MEM_HEADS):
            cols = slice(h * LANES, (h + 1) * LANES)
            q = q_ref[:, cols].astype(BF16)
            mk, mv = mk_ref[:, cols], mv_ref[:, cols]
            do = do_ref[:, cols].astype(BF16)
            s = _dot_nt(q, mk) * MEM_SCALE
            e = jnp.exp(s - jnp.max(s, axis=-1, keepdims=True))
            p = e / jnp.sum(e, axis=-1, keepdims=True)
            dp = _dot_nt(do, mv)
            ds = (p * (dp - jnp.sum(p * dp, axis=-1, keepdims=True)) * MEM_SCALE).astype(BF16)
            dq_ref[:, cols] = _dot(ds, mk).astype(BF16)
            dmk_ref[:, cols] += _dot_tn(ds, q)
            dmv_ref[:, cols] += _dot_tn(p.astype(BF16), do)

    ospec = pl.BlockSpec((MEM_TQ, MEM_WIDTH), lambda b, i: (b * nq + i, 0))
    kspec = pl.BlockSpec((N_MEM, MEM_WIDTH), lambda b, i: (b, 0))
    return pl.pallas_call(
        body, grid=(nb, nq),
        in_specs=[pl.BlockSpec((MEM_TQ, MEM_WIDTH), lambda b, i: (b * nq + i, MQ_BLK4)),
                  kspec, pl.BlockSpec((N_MEM, MEM_WIDTH), lambda b, i: (b, 1)), ospec],
        out_specs=[ospec, kspec, kspec],
        out_shape=[SDS((t, MEM_WIDTH), BF16), SDS((nb * N_MEM, MEM_WIDTH), F32), SDS((nb * N_MEM, MEM_WIDTH), F32)],
        name="mem_attn_bwd", compiler_params=_params("parallel", "arbitrary"))(proj, mkv, mkv, dy)


ROW_TM = 512
AG_BLK = 3072 // 1024
BG_BLK = 4608 // 512
MG_BLK = 5632 // 512
GROUPS = ((0, A_WIDTH), (A_WIDTH, MLA_WIDTH), (A_WIDTH + MLA_WIDTH, MEM_WIDTH))
D_MIX = 2048


def _gate_specs():
    def row(w, j):
        return pl.BlockSpec((ROW_TM, w), lambda i: (i, j))

    def vec(w):
        return pl.BlockSpec((1, w), lambda i: (0, 0))

    ys = [row(A_WIDTH, 0), row(MLA_WIDTH, 0), row(MEM_WIDTH, 0)]
    gates = [row(A_WIDTH, AG_BLK), row(MLA_WIDTH, BG_BLK), row(MEM_WIDTH, MG_BLK)]
    gains = [vec(A_WIDTH), vec(MLA_WIDTH), vec(MEM_WIDTH)]
    return row, vec, ys, gates, gains


def gate_out_ln_loss(ya, yb, ym, proj, goa, gob, gom, wout, h32, target, gp, bp):
    t, d = h32.shape
    _, _, ys, gates, gains = _gate_specs()

    def body(ya_ref, yb_ref, ym_ref, ga_ref, gb_ref, gm_ref, goa_ref, gob_ref, gom_ref, w_ref, h_ref, t_ref, gp_ref, bp_ref,
             z_ref, du32_ref, du16_ref, loss_ref, dgp_ref, dbp_ref):
        @pl.when(pl.program_id(0) == 0)
        def _():
            loss_ref[...] = jnp.zeros_like(loss_ref)
            dgp_ref[...] = jnp.zeros_like(dgp_ref)
            dbp_ref[...] = jnp.zeros_like(dbp_ref)

        for (off, w), y_ref, g_ref, go_ref in zip(GROUPS, (ya_ref, yb_ref, ym_ref), (ga_ref, gb_ref, gm_ref),
                                                  (goa_ref, gob_ref, gom_ref)):
            n, _ = _rms(y_ref[...], go_ref[...])
            gt = g_ref[...]
            z_ref[:, off:off + w] = (n * (gt * _sigmoid(gt))).astype(BF16)
        g = gp_ref[...]
        u = ALPHA * h_ref[...] + _dot(z_ref[...], w_ref[...])
        mu = jnp.mean(u, axis=-1, keepdims=True)
        uc = u - mu
        rstd = lax.rsqrt(jnp.mean(uc * uc, axis=-1, keepdims=True) + NORM_EPS)
        xhat = uc * rstd
        err = xhat * g + bp_ref[...] - t_ref[...]
        tok = jnp.sum(err * err, axis=-1, keepdims=True) * (1.0 / d)
        loss_ref[...] += 0.5 * jnp.sum(tok, axis=0, keepdims=True)
        dout = err * (1.0 / d)
        dxhat = dout * g
        du = rstd * (dxhat - jnp.mean(dxhat, axis=-1, keepdims=True)
                     - xhat * jnp.mean(dxhat * xhat, axis=-1, keepdims=True))
        du32_ref[...] = du
        du16_ref[...] = du.astype(BF16)
        dgp_ref[...] += jnp.sum(dout * xhat, axis=0, keepdims=True)
        dbp_ref[...] += jnp.sum(dout, axis=0, keepdims=True)

    row = pl.BlockSpec((ROW_TM, d), lambda i: (i, 0))
    vec = pl.BlockSpec((1, d), lambda i: (0, 0))
    zrow = pl.BlockSpec((ROW_TM, D_MIX), lambda i: (i, 0))
    return pl.pallas_call(
        body, grid=(t // ROW_TM,),
        in_specs=ys + gates + gains + [pl.BlockSpec((D_MIX, d), lambda i: (0, 0)), row, row, vec, vec],
        out_specs=[zrow, row, row, pl.BlockSpec((1, LANES), lambda i: (0, 0)), vec, vec],
        out_shape=[SDS((t, D_MIX), BF16), SDS((t, d), F32), SDS((t, d), BF16), SDS((1, LANES), F32), SDS((1, d), F32),
                   SDS((1, d), F32)],
        name="gate_out_ln_loss", compiler_params=_params("arbitrary"))(
            ya, yb, ym, proj, proj, proj, goa, gob, gom, wout, h32, target, gp, bp)


def gate_bwd(du16, wout, ya, yb, ym, proj, goa, gob, gom):
    t = ya.shape[0]
    row, vec, ys, gates, gains = _gate_specs()

    def body(du_ref, w_ref, ya_ref, yb_ref, ym_ref, ga_ref, gb_ref, gm_ref, goa_ref, gob_ref, gom_ref,
             dya_ref, dyb_ref, dym_ref, dga_ref, dgb_ref, dgm_ref, dgoa_ref, dgob_ref, dgom_ref):
        @pl.when(pl.program_id(0) == 0)
        def _():
            dgoa_ref[...] = jnp.zeros_like(dgoa_ref)
            dgob_ref[...] = jnp.zeros_like(dgob_ref)
            dgom_ref[...] = jnp.zeros_like(dgom_ref)

        dz = _dot_nt(du_ref[...], w_ref[...])
        for (off, w), y_ref, g_ref, go_ref, dy_ref, dg_ref, dgo_ref in zip(
                GROUPS, (ya_ref, yb_ref, ym_ref), (ga_ref, gb_ref, gm_ref), (goa_ref, gob_ref, gom_ref),
                (dya_ref, dyb_ref, dym_ref), (dga_ref, dgb_ref, dgm_ref), (dgoa_ref, dgob_ref, dgom_ref)):
            dzg = dz[:, off:off + w]
            y, gt, go = y_ref[...], g_ref[...], go_ref[...]
            n, r = _rms(y, go)
            sg = _sigmoid(gt)
            dg_ref[...] = (dzg * n * (sg * (1.0 + gt * (1.0 - sg)))).astype(BF16)
            dy, dgo = _rms_bwd(dzg * (gt * sg), y, r, go)
            dy_ref[...] = dy
            dgo_ref[...] += dgo

    widths = (A_WIDTH, MLA_WIDTH, MEM_WIDTH)
    return pl.pallas_call(
        body, grid=(t // ROW_TM,),
        in_specs=[row(D_MODEL, 0), pl.BlockSpec((D_MIX, D_MODEL), lambda i: (0, 0))] + ys + gates + gains,
        out_specs=[row(w, 0) for w in widths] * 2 + [vec(w) for w in widths],
        out_shape=[SDS((t, w), F32) for w in widths] + [SDS((t, w), BF16) for w in widths] + [SDS((1, w), F32) for w in widths],
        name="gate_bwd", compiler_params=_params("arbitrary"))(du16, wout, ya, yb, ym, proj, proj, proj, goa, gob, gom)


def dh_ln_bwd(pieces, win_t, du32, x2, g_emb, xch):
    t, d = x2.shape

    def body(*refs):
        p_refs = refs[:len(pieces)]
        w_ref, du_ref, x_ref, g_ref, dx_ref, dg_ref, db_ref = refs[len(pieces):]

        @pl.when(pl.program_id(0) == 0)
        def _():
            dg_ref[...] = jnp.zeros_like(dg_ref)
            db_ref[...] = jnp.zeros_like(db_ref)

        dh = ALPHA * du_ref[...]
        for p_ref, off, w in zip(p_refs, PIECE_OFFS, PIECE_WIDTHS):
            dh = dh + _dot(p_ref[...], w_ref[off:off + w, :])
        x = x_ref[...]
        xc = x - jnp.mean(x, axis=-1, keepdims=True)
        rstd = lax.rsqrt(jnp.mean(xc * xc, axis=-1, keepdims=True) + NORM_EPS)
        xhat = xc * rstd
        dg_ref[...] += jnp.sum(dh * xhat, axis=0, keepdims=True)
        db_ref[...] += jnp.sum(dh, axis=0, keepdims=True)
        tg = dh * g_ref[...]
        dx_ref[...] = rstd * (tg - jnp.mean(tg, axis=-1, keepdims=True)
                              - xhat * jnp.mean(tg * xhat, axis=-1, keepdims=True))

    row = pl.BlockSpec((ROW_TM, d), lambda i: (i, 0))
    vec = pl.BlockSpec((1, d), lambda i: (0, 0))
    return call_hosting_exchange(
        body, xch, grid=(t // ROW_TM,),
        in_specs=[pl.BlockSpec((ROW_TM, w), lambda i: (i, 0)) for w in PIECE_WIDTHS]
        + [pl.BlockSpec(win_t.shape, lambda i: (0, 0)), row, row, vec],
        out_specs=[row, vec, vec],
        out_shape=[SDS((t, d), F32), SDS((1, d), F32), SDS((1, d), F32)],
        scratch_shapes=[], name="dh_ln_bwd", operands=(*pieces, win_t, du32, x2, g_emb))


def _adamw(w, g, m, v):
    m2 = ADAM_B1 * m + (1.0 - ADAM_B1) * g
    v2 = ADAM_B2 * v + (1.0 - ADAM_B2) * (g * g)
    m_hat = m2 / (1.0 - ADAM_B1 ** ADAM_STEP)
    v_hat = v2 / (1.0 - ADAM_B2 ** ADAM_STEP)
    return -ADAM_LR * (m_hat / (jnp.sqrt(v_hat) + ADAM_EPS) + ADAM_WD * w), m2, v2


def adamw_shard(w, parts, m, v, name):
    r, c = w.shape
    if r % 256 == 0 or r * c <= 256 * 1024:
        tr, tc = min(r, 256), c
    else:
        tr, tc = r, 256

    def body(w_ref, p_ref, m_ref, v_ref, g_ref, d_ref, nm_ref, nv_ref):
        g = p_ref[0].astype(F32)
        for k in range(1, N_DEV):
            g = g + p_ref[k].astype(F32)
        g_ref[...] = g
        d_ref[...], nm_ref[...], nv_ref[...] = _adamw(w_ref[...], g, m_ref[...], v_ref[...])

    blk = pl.BlockSpec((tr, tc), lambda i, j: (i, j))
    return pl.pallas_call(
        body, grid=(r // tr, c // tc),
        in_specs=[blk, pl.BlockSpec((N_DEV, tr, tc), lambda i, j: (0, i, j)), blk, blk],
        out_specs=[blk] * 4, out_shape=[SDS((r, c), F32)] * 4, name=name,
        compiler_params=_params("parallel", "parallel"))(w, parts, m, v)


def _place():
    return lax.axis_index("x"), lax.axis_index("y"), lax.axis_index("c")


def _flat(px, py, pc):
    return 4 * px + 2 * py + pc


def _peer(x, y, c, k):
    return (1 - x if k & 4 else x, 1 - y if k & 2 else y, 1 - c if k & 1 else c)


def cast_shards(shards):
    def body(*refs):
        n = len(refs) // 2
        for i_ref, o_ref in zip(refs[:n], refs[n:]):
            o_ref[...] = i_ref[...].astype(BF16)

    return pl.pallas_call(body, out_shape=[SDS(s.shape, BF16) for s in shards], name="cast_shards",
                          compiler_params=_params())(*shards)


def _two_level_gather_plan(src_refs, land_refs, send_sems, recv_sems, local_sems):
    n = len(src_refs)
    x, y, c = _place()
    me, sib = (x, y, c), (x, y, 1 - c)
    chips = [(1 - x, y), (x, 1 - y), (1 - x, 1 - y)]

    def copy(a, k, block, to, src=None):
        dst = land_refs[a].at[_flat(*block)]
        return pltpu.make_async_remote_copy(
            src_ref=dst if src is None else src, dst_ref=dst,
            send_sem=send_sems.at[a * N_DEV + k], recv_sem=recv_sems.at[a * N_DEV + k],
            device_id=to, device_id_type=MESH)

    mine = [pltpu.make_async_copy(src_refs[a], land_refs[a].at[_flat(*me)], local_sems.at[a]) for a in range(n)]
    first = []
    for a in range(n):
        first.append(copy(a, 0, me, sib, src=src_refs[a]))
        first += [copy(a, 1 + j, me, (*chip, c), src=src_refs[a]) for j, chip in enumerate(chips)]

    def start():
        for cp in mine + first:
            cp.start()

    def finish():
        passed = []
        for j, chip in enumerate(chips):
            for a in range(n):
                copy(a, 1 + j, (*chip, c), me).wait_recv()
                fwd = copy(a, 4 + j, (*chip, c), sib)
                fwd.start()
                passed.append(fwd)
        for a in range(n):
            copy(a, 0, sib, me).wait_recv()
            for j, chip in enumerate(chips):
                copy(a, 4 + j, (*chip, 1 - c), me).wait_recv()
        for cp in first + passed:
            cp.wait_send()
        for cp in mine:
            cp.wait()

    return start, finish


ALL_DEVICES = tuple(range(N_DEV))


def _exchange_plan(src_refs, land_refs, dests, send_sems, recv_sems, local_sems):
    x, y, c = _place()
    me = _flat(x, y, c)
    plan = []
    for a, (src, land, dl) in enumerate(zip(src_refs, land_refs, dests)):
        for li, j in enumerate(dl):
            to = ((j >> 2) & 1, (j >> 1) & 1, j & 1)
            block = src.at[li] if len(src.shape) == len(land.shape) else src

            def push(slot, a=a, block=block, land=land, j=j, to=to):
                return pltpu.make_async_remote_copy(
                    src_ref=block, dst_ref=land.at[slot], send_sem=send_sems.at[a * N_DEV + j],
                    recv_sem=recv_sems.at[a * N_DEV + slot], device_id=to, device_id_type=MESH)

            own = pltpu.make_async_copy(block, land.at[j], local_sems.at[a])
            plan.append((j, push(me), own, [push(s) for s in range(N_DEV) if s != j]))
    return me, plan


def _exchange_start(me, plan):
    for j, send, own, _ in plan:
        @pl.when(me != j)
        def _(send=send):
            send.start()

        @pl.when(me == j)
        def _(own=own):
            own.start()


def _exchange_wait(me, plan):
    for j, send, own, arrivals in plan:
        @pl.when(me != j)
        def _(send=send):
            send.wait_send()

        @pl.when(me == j)
        def _(own=own, arrivals=arrivals):
            own.wait()
            for arrival in arrivals:
                arrival.wait_recv()


def call_hosting_exchange(core, xch, *, grid, in_specs, out_specs, out_shape, scratch_shapes, name, operands):
    srcs, dests, landing = xch
    n, n_in, n_out, n_scr = len(srcs), len(in_specs), len(out_specs), len(scratch_shapes)

    def body(*refs):
        ins, src_refs = refs[:n_in], refs[n_in:n_in + n]
        outs = refs[n_in + 2 * n:n_in + 2 * n + n_out]
        land_refs = refs[n_in + 2 * n + n_out:n_in + 3 * n + n_out]
        scratch = refs[n_in + 3 * n + n_out:n_in + 3 * n + n_out + n_scr]
        sems = refs[n_in + 3 * n + n_out + n_scr:]
        first = functools.reduce(jnp.logical_and, [pl.program_id(i) == 0 for i in range(len(grid))])
        last = functools.reduce(jnp.logical_and, [pl.program_id(i) == grid[i] - 1 for i in range(len(grid))])
        if dests is None:
            start, finish = _two_level_gather_plan(src_refs, land_refs, *sems)
        else:
            me, plan = _exchange_plan(src_refs, land_refs, dests, *sems)
            start, finish = functools.partial(_exchange_start, me, plan), functools.partial(_exchange_wait, me, plan)
        pl.when(first)(start)
        core(*ins, *outs, *scratch)
        pl.when(last)(finish)

    hbm = pl.BlockSpec(memory_space=pl.ANY)
    res = pl.pallas_call(
        body, grid=grid,
        in_specs=list(in_specs) + [hbm] * (2 * n), out_specs=list(out_specs) + [hbm] * n,
        out_shape=list(out_shape) + [SDS(l.shape, l.dtype) for l in landing],
        scratch_shapes=list(scratch_shapes) + [pltpu.SemaphoreType.DMA((N_DEV * n,)), pltpu.SemaphoreType.DMA((N_DEV * n,)),
                                               pltpu.SemaphoreType.DMA((n,))],
        input_output_aliases={n_in + n + k: n_out + k for k in range(n)},
        name=name, compiler_params=_params(*(("arbitrary",) * len(grid))))(*operands, *srcs, *landing)
    return res[:n_out], res[n_out:]


SLOT_ROWS = 8


def small_allreduce_adamw(loss_sum, grads, ws, ms, vs):
    n = len(grads)
    rows = [g.shape[0] for g in grads]
    total = SLOT_ROWS * (n + 1)

    def body(*refs):
        loss_ref, g_refs, w_refs = refs[0], refs[1:1 + n], refs[1 + n:1 + 2 * n]
        m_refs, v_refs = refs[1 + 2 * n:1 + 3 * n], refs[1 + 3 * n:1 + 4 * n]
        outs = refs[1 + 4 * n:2 + 8 * n]
        vec, gath, tot, send_sems, recv_sems = refs[2 + 8 * n:]
        x, y, c = _place()
        me = _flat(x, y, c)
        vec[...] = jnp.zeros_like(vec)
        vec[0:1, :] = loss_ref[...]
        for i in range(n):
            vec[SLOT_ROWS * (i + 1):SLOT_ROWS * (i + 1) + rows[i], :] = g_refs[i][...]
        gath[me] = vec[...]
        copies = []
        for k in range(1, N_DEV):
            peer = _peer(x, y, c, k)
            copies.append(pltpu.make_async_remote_copy(
                src_ref=vec, dst_ref=gath.at[me], send_sem=send_sems.at[k - 1], recv_sem=recv_sems.at[k - 1],
                device_id=peer, device_id_type=MESH))
        for cp in copies:
            cp.start()
        for cp in copies:
            cp.wait_recv()
        for cp in copies:
            cp.wait_send()
        g = gath[0]
        for j in range(1, N_DEV):
            g = g + gath[j]
        tot[...] = g
        outs[0][...] = tot[0:1, :]
        for i in range(n):
            gi = tot[SLOT_ROWS * (i + 1):SLOT_ROWS * (i + 1) + rows[i], :]
            outs[1 + i][...] = gi
            outs[1 + n + i][...], outs[1 + 2 * n + i][...], outs[1 + 3 * n + i][...] = _adamw(
                w_refs[i][...], gi, m_refs[i][...], v_refs[i][...])

    shapes = [SDS(g.shape, F32) for g in grads]
    return pl.pallas_call(
        body, out_shape=[SDS((1, LANES), F32)] + shapes * 4,
        scratch_shapes=[pltpu.VMEM((total, LANES), F32), pltpu.VMEM((N_DEV, total, LANES), F32), pltpu.VMEM((total, LANES), F32),
                        pltpu.SemaphoreType.DMA((7,)), pltpu.SemaphoreType.DMA((7,))],
        name="small_allreduce_adamw", compiler_params=_params())(loss_sum, *grads, *ws, *ms, *vs)


def _rope_lane_patterns():
    inv = lambda r: ROPE_THETA ** (-(jnp.arange(0, r, 2, dtype=F32) / r))
    z = lambda n: jnp.zeros((n,), F32)
    o = lambda n: jnp.ones((n,), F32)
    half, rest = A_ROT // 2, A_HEAD_DIM - A_ROT
    ia, im = inv(A_ROT), inv(MLA_ROPE)
    mh, tail = MLA_ROPE // 2, LANES - MLA_NOPE - MLA_ROPE
    rows = [jnp.tile(jnp.concatenate([ia, ia, z(rest)]), 2),
            jnp.tile(jnp.concatenate([o(half), z(half + rest)]), 2),
            jnp.tile(jnp.concatenate([z(half), o(half), z(rest)]), 2),
            jnp.concatenate([z(MLA_NOPE), im, im, z(tail)]),
            jnp.concatenate([z(MLA_NOPE), o(mh), z(mh + tail)]),
            jnp.concatenate([z(MLA_NOPE + mh), o(mh), z(tail)]),
            z(LANES), z(LANES)]
    return jnp.stack(rows)


KR_LO, KR_HI = 4480, 4512
W_IN_SHARD = D_IN // N_DEV
BG_SPLIT = 6 * W_IN_SHARD - KR_HI


def w_in_working_t(g):
    pad_lo, pad_hi = MLA_NOPE, LANES - MLA_NOPE - MLA_ROPE
    spans = []
    for lo, hi, shift in ((0, KR_LO, 0), (KR_LO, KR_HI, pad_lo), (KR_HI, D_IN, pad_lo + pad_hi)):
        r = lo
        while r < hi:
            j = r // W_IN_SHARD
            n = min(hi, (j + 1) * W_IN_SHARD) - r
            spans.append((j, r - j * W_IN_SHARD, n, r + shift))
            r += n

    def body(g_ref, o_ref):
        o_ref[KR_LO:KR_LO + pad_lo, :] = jnp.zeros((pad_lo, D_MODEL), o_ref.dtype)
        o_ref[KR_HI + pad_lo:KR_HI + pad_lo + pad_hi, :] = jnp.zeros((pad_hi, D_MODEL), o_ref.dtype)
        for j, src, n, dst in spans:
            o_ref[dst:dst + n, :] = g_ref[j, src:src + n, :]

    return pl.pallas_call(body, out_shape=SDS((D_INW, D_MODEL), g.dtype), name="w_in_working_t", compiler_params=_params())(g)


def _w_in_shard_5(d_ag_tail, d_cc, d_bg_head):
    kr = MLA_Q_RANK + MLA_KV_RANK + MLA_NOPE
    rows = jnp.concatenate([d_ag_tail, d_cc[:MLA_Q_RANK + MLA_KV_RANK], d_cc[kr:kr + MLA_ROPE], d_bg_head], 0)
    return rows.reshape(1, W_IN_SHARD, D_MODEL).astype(BF16)


def _w_uq_working(g):
    w = jnp.pad(g.transpose(1, 0, 2), ((0, 0), (0, 0), (0, LANES - MLA_NOPE - MLA_ROPE)))
    return w.reshape(MLA_Q_RANK, MLA_QW)


def _w_uq_parts(dw):
    return dw.reshape(MLA_Q_RANK, MLA_HEADS, LANES)[:, :, :MLA_NOPE + MLA_ROPE].transpose(1, 0, 2)


def _w_ukv_working(g):
    wk = jnp.pad(g[:, :, :MLA_NOPE].transpose(1, 0, 2), ((0, 0), (0, 0), (0, LANES - MLA_NOPE)))
    wv = g[:, :, MLA_NOPE:].transpose(1, 0, 2)
    return jnp.concatenate([wk.reshape(MLA_KV_RANK, MLA_QW), wv.reshape(MLA_KV_RANK, MLA_WIDTH)], 1)


def _w_ukv_parts(dw):
    dk = dw[:, :MLA_QW].reshape(MLA_KV_RANK, MLA_HEADS, LANES)[:, :, :MLA_NOPE]
    dv = dw[:, MLA_QW:].reshape(MLA_KV_RANK, MLA_HEADS, MLA_V)
    return jnp.concatenate([dk, dv], -1).transpose(1, 0, 2)


SMALL_NAMES = ("g_emb", "b_emb", "g_cq", "g_ckv", "g_out_a", "g_out_b", "g_out_m", "g_post", "b_post")


def kernel(x, mem, positions, g_emb, b_emb, w_in, g_cq, g_ckv, w_uq, w_ukv, w_mem_kv, g_out_a, g_out_b, g_out_m, w_out, g_post, b_post, loss_target, m_g_emb, m_b_emb, m_w_in, m_g_cq, m_g_ckv, m_w_uq, m_w_ukv, m_w_mem_kv, m_g_out_a, m_g_out_b, m_g_out_m, m_w_out, m_g_post, m_b_post, v_g_emb, v_b_emb, v_w_in, v_g_cq, v_g_ckv, v_w_uq, v_w_ukv, v_w_mem_kv, v_g_out_a, v_g_out_b, v_g_out_m, v_w_out, v_g_post, v_b_post):
    nb = x.shape[0]
    t = nb * SEQ
    x2 = x.reshape(t, D_MODEL)
    tgt2 = loss_target.reshape(t, D_MODEL)
    mem2 = mem.reshape(nb * N_MEM, D_MODEL)
    g_emb2, b_emb2 = g_emb.reshape(1, -1), b_emb.reshape(1, -1)

    w_in_t, m_w_in_t, v_w_in_t = w_in[0].T, m_w_in[0].T, v_w_in[0].T
    s_in, s_uq, s_ukv, s_mem, s_out = cast_shards((w_in_t, w_uq[0], w_ukv[0], w_mem_kv[0], w_out[0]))
    (h32, h16, (a_c, a_sa, a_sb), (m_c, m_sa, m_sb)), (g_in,) = embed_fwd(
        x2, g_emb2, b_emb2, positions, ((s_in,), None, (lax.empty((N_DEV,) + s_in.shape, BF16),)))
    win_t = w_in_working_t(g_in)

    proj = mm_nn(h16, win_t, F32, 512, 1536, "proj", rhs_transposed=True)
    later = (s_uq, s_ukv, s_mem, s_out)
    (ya, lse_a), qkv_d, (g_uq, g_ukv, g_mem, g_out) = a_attn_fwd(
        proj, a_c, a_sa, a_sb, nb,
        (later, (ALL_DEVICES,) * len(later), tuple(lax.empty((N_DEV,) + w.shape, BF16) for w in later)))
    wuq_w = _w_uq_working(g_uq)
    wkv_w = _w_ukv_working(g_ukv)
    wmem = g_mem.reshape(D_MODEL, 2 * MEM_WIDTH)
    wout = g_out.reshape(D_MIX, D_MODEL)
    qb, kb, vb = mla_prep_fwd(proj, m_c, m_sa, m_sb, g_cq, g_ckv, wuq_w, wkv_w)
    yb, lse_b = mla_attn_fwd(qb, kb, vb, nb)
    mkv = mm_nn(mem2, wmem, BF16, nb * N_MEM, 512, "mem_kv")
    ym = mem_attn_fwd(proj, mkv, nb)
    z, du32, du16, loss_sum, dg_post, db_post = gate_out_ln_loss(
        ya, yb, ym, proj, g_out_a, g_out_b, g_out_m, wout, h32, tgt2, g_post, b_post)

    dya, dyb, dym, dag, dbg, dmg, dg_out_a, dg_out_b, dg_out_m = gate_bwd(
        du16, wout, ya, yb, ym, proj, g_out_a, g_out_b, g_out_m)
    dw_out = mm_tn(z, du16, 1024, "dw_out")
    dmq, dmk, dmv = mem_attn_bwd(proj, mkv, dym, nb)
    dw_mem = mm_tn(mem2, jnp.concatenate([dmk, dmv], 1), nb * N_MEM, "dw_mem")
    d_gates, shards_6_7 = mm_tn_group((dbg, dmq, dmg), h16, 2048, "dw_in_bg_mq_mg", W_IN_SHARD, BG_SPLIT, 2)
    landing = lambda w, dtype=F32: lax.empty((N_DEV,) + w.shape, dtype)
    big_w = (w_in_t, w_uq[0], w_ukv[0], w_mem_kv[0], w_out[0])
    (daq, dak, dav), (p_out, p_mem, p_in) = a_attn_bwd(
        qkv_d, a_c, a_sa, a_sb, dya, ya, lse_a, nb,
        ((dw_out.reshape(N_DEV, D_MIX // N_DEV, D_MODEL), dw_mem.reshape(N_DEV, D_MODEL // N_DEV, 2 * MEM_WIDTH),
          shards_6_7),
         (ALL_DEVICES, ALL_DEVICES, (6, 7)),
         (landing(w_out[0]), landing(w_mem_kv[0]), landing(w_in_t, BF16))))
    d_a, shards_0_4 = mm_tn_group((daq, dak, dav, dag), h16, 1024, "dw_in_aq_ak_av_ag", W_IN_SHARD, 0, 5)
    (dqb, dkb, dvb), (p_in,) = mla_attn_bwd(
        qb, kb, vb, dyb, yb, lse_b, nb, ((shards_0_4,), ((0, 1, 2, 3, 4),), (p_in,)))
    dcc, dqf, cqn, dkvf, ckvn, dg_cq, dg_ckv = mla_prep_bwd(proj, m_c, m_sa, m_sb, g_cq, g_ckv, wuq_w, wkv_w, dqb, dkb, dvb)
    dw_uq = mm_tn(cqn, dqf, 2048, "dw_uq")
    dw_ukv = mm_tn(ckvn, dkvf, 2048, "dw_ukv")
    d_cc = mm_tn(dcc, h16, 2048, "dw_in_cc")
    pieces = (daq, dak, dav, dag, dcc, dbg, dmq, dmg)
    (grad_x, dg_emb, db_emb), (p_in, p_uq, p_ukv) = dh_ln_bwd(
        pieces, win_t, du32, x2, g_emb2,
        ((_w_in_shard_5(d_a[5 * W_IN_SHARD:], d_cc, d_gates[:BG_SPLIT]), _w_uq_parts(dw_uq), _w_ukv_parts(dw_ukv)),
         ((5,), ALL_DEVICES, ALL_DEVICES),
         (p_in, landing(w_uq[0]), landing(w_ukv[0]))))

    parts = (p_in, p_uq, p_ukv, p_mem, p_out)
    big_m = (m_w_in_t, m_w_uq[0], m_w_ukv[0], m_w_mem_kv[0], m_w_out[0])
    big_v = (v_w_in_t, v_w_uq[0], v_w_ukv[0], v_w_mem_kv[0], v_w_out[0])
    big = {}
    for name, w, p, m, v in zip(("w_in", "w_uq", "w_ukv", "w_mem_kv", "w_out"), big_w, parts, big_m, big_v):
        res = adamw_shard(w, p, m, v, "adamw_" + name)
        big[name] = [(o.T if name == "w_in" else o)[None] for o in res]

    small_w = (g_emb, b_emb, g_cq, g_ckv, g_out_a, g_out_b, g_out_m, g_post, b_post)
    small_m = (m_g_emb, m_b_emb, m_g_cq, m_g_ckv, m_g_out_a, m_g_out_b, m_g_out_m, m_g_post, m_b_post)
    small_v = (v_g_emb, v_b_emb, v_g_cq, v_g_ckv, v_g_out_a, v_g_out_b, v_g_out_m, v_g_post, v_b_post)
    small_g = (dg_emb, db_emb, dg_cq, dg_ckv, dg_out_a, dg_out_b, dg_out_m, dg_post, db_post)
    rows128 = lambda vals: [v.reshape(-1, LANES) for v in vals]
    res = small_allreduce_adamw(loss_sum, rows128(small_g), rows128(small_w), rows128(small_m), rows128(small_v))
    loss = res[0][0, 0]
    n_small = len(small_w)
    sg, sd, sm, sv = [[r.reshape(w.shape) for r, w in zip(res[1 + k * n_small:1 + (k + 1) * n_small], small_w)]
                      for k in range(4)]

    order = ("g_emb", "b_emb", "w_in", "g_cq", "g_ckv", "w_uq", "w_ukv", "w_mem_kv", "g_out_a", "g_out_b", "g_out_m",
             "w_out", "g_post", "b_post")
    small_idx = {n: i for i, n in enumerate(SMALL_NAMES)}
    outs = [loss, grad_x.reshape(x.shape)]
    for kind in range(4):
        for name in order:
            outs.append(big[name][kind] if name in big else (sg, sd, sm, sv)[kind][small_idx[name]])
    return tuple(outs)
```

```python
import functools

import jax
import jax.numpy as jnp
from jax import lax
from jax.experimental import pallas as pl
from jax.experimental.pallas import tpu as pltpu

F32 = jnp.float32
BF16 = jnp.bfloat16
SDS = jax.ShapeDtypeStruct
MESH = pl.DeviceIdType.MESH

D_MODEL = 1024
SEQ = 2048
A_HEADS, A_HEAD_DIM, A_ROT = 16, 64, 16
A_WIDTH = 1024
DILATIONS = (1, 4, 16)
N_SIDE = 64
MLA_HEADS, MLA_Q_RANK, MLA_KV_RANK = 8, 256, 128
MLA_NOPE, MLA_ROPE, MLA_V = 64, 32, 64
MLA_WIDTH = 512
N_MEM, MEM_HEADS, MEM_HEAD_DIM, MEM_WIDTH = 256, 4, 128, 512
ROPE_THETA = 500000.0
NORM_EPS = 1e-5
NEG_INF = -1e30
ALPHA = 2.0 ** 0.25
D_IN = 6048
N_DEV = 8

ADAM_LR, ADAM_B1, ADAM_B2, ADAM_EPS, ADAM_WD, ADAM_STEP = 0.001, 0.9, 0.999, 1e-08, 0.01, 10

D_INW = 6144
PIECE_WIDTHS = (1024, 1024, 1024, 1024, 512, 512, 512, 512)
PIECE_OFFS = (0, 1024, 2048, 3072, 4096, 4608, 5120, 5632)
LANES = 128
VMEM_LIMIT = 56 * 1024 * 1024


def _params(*sem):
    kw = dict(vmem_limit_bytes=VMEM_LIMIT)
    if sem:
        kw["dimension_semantics"] = sem
    return pltpu.CompilerParams(**kw)


def _dot(a, b):
    return jnp.dot(a, b, preferred_element_type=F32)


def _dot_nt(a, b):
    return lax.dot_general(a, b, (((1,), (1,)), ((), ())), preferred_element_type=F32)


def _dot_tn(a, b):
    return lax.dot_general(a, b, (((0,), (0,)), ((), ())), preferred_element_type=F32)


def _sigmoid(x):
    return 1.0 / (1.0 + jnp.exp(-x))


def _rope_fwd(x, c, sa, sb, half):
    n = x.shape[-1]
    return x * c + pltpu.roll(x, n - half, 1) * sa + pltpu.roll(x, half, 1) * sb


def _rope_bwd(dy, c, sa, sb, half):
    n = dy.shape[-1]
    return dy * c + pltpu.roll(dy * sa, half, 1) + pltpu.roll(dy * sb, n - half, 1)


def mm_nn(a, b, out_dtype, tm, tn, name, rhs_transposed=False):
    m, k = a.shape
    n = b.shape[0] if rhs_transposed else b.shape[1]
    dot = _dot_nt if rhs_transposed else _dot

    def body(a_ref, b_ref, o_ref):
        o_ref[...] = dot(a_ref[...].astype(BF16), b_ref[...].astype(BF16)).astype(o_ref.dtype)

    b_spec = pl.BlockSpec((tn, k), lambda j, i: (j, 0)) if rhs_transposed else pl.BlockSpec((k, tn), lambda j, i: (0, j))
    return pl.pallas_call(
        body, grid=(n // tn, m // tm),
        in_specs=[pl.BlockSpec((tm, k), lambda j, i: (i, 0)), b_spec],
        out_specs=pl.BlockSpec((tm, tn), lambda j, i: (i, j)),
        out_shape=SDS((m, n), out_dtype), name=name,
        compiler_params=_params("parallel", "parallel"))(a, b)


def mm_tn(a, b, tt, name):
    t, m = a.shape
    n = b.shape[1]

    def body(a_ref, b_ref, o_ref):
        @pl.when(pl.program_id(0) == 0)
        def _():
            o_ref[...] = jnp.zeros_like(o_ref)

        o_ref[...] += _dot_tn(a_ref[...].astype(BF16), b_ref[...].astype(BF16))

    return pl.pallas_call(
        body, grid=(t // tt,),
        in_specs=[pl.BlockSpec((tt, m), lambda i: (i, 0)), pl.BlockSpec((tt, n), lambda i: (i, 0))],
        out_specs=pl.BlockSpec((m, n), lambda i: (0, 0)),
        out_shape=SDS((m, n), F32), name=name,
        compiler_params=_params("arbitrary"))(a, b)


def mm_tn_group(pieces, b, tt, name, slab_rows, first_slab_row, n_slabs):
    n, (t, w), cols = len(pieces), pieces[0].shape, b.shape[1]
    nt = t // tt

    def body(*refs):
        p_refs, b_ref, o_ref, slab_ref = refs[:n], refs[n], refs[n + 1], refs[n + 2]

        @pl.when(pl.program_id(1) == 0)
        def _():
            o_ref[...] = jnp.zeros_like(o_ref)

        for k in range(n):
            @pl.when(pl.program_id(0) == k)
            def _(k=k):
                o_ref[...] += _dot_tn(p_refs[k][...], b_ref[...])

            @pl.when((pl.program_id(0) == k) & (pl.program_id(1) == nt - 1))
            def _(k=k):
                for j in range(n_slabs):
                    lo = max(k * w, first_slab_row + j * slab_rows)
                    hi = min((k + 1) * w, first_slab_row + (j + 1) * slab_rows)
                    if lo < hi:
                        dst = lo - first_slab_row - j * slab_rows
                        slab_ref[j, dst:dst + hi - lo, :] = o_ref[lo - k * w:hi - k * w, :].astype(slab_ref.dtype)

    def piece_spec(k):
        return pl.BlockSpec((tt, w), lambda p, i: (jnp.where(p < k, 0, jnp.where(p > k, nt - 1, i)), 0))

    return pl.pallas_call(
        body, grid=(n, nt),
        in_specs=[piece_spec(k) for k in range(n)] + [pl.BlockSpec((tt, cols), lambda p, i: (i, 0))],
        out_specs=[pl.BlockSpec((w, cols), lambda p, i: (p, 0)),
                   pl.BlockSpec((n_slabs, slab_rows, cols), lambda p, i: (0, 0, 0))],
        out_shape=[SDS((n * w, cols), F32), SDS((n_slabs, slab_rows, cols), BF16)], name=name,
        compiler_params=_params("arbitrary", "arbitrary"))(*pieces, b)


def embed_fwd(x2, g, b, positions, xch):
    t, d = x2.shape
    tm = 512
    pos = positions.astype(F32).reshape(-1, 1)

    def body(x_ref, g_ref, b_ref, pos_ref, pat_ref, h32_ref, h16_ref, *tabs):
        x = x_ref[...]
        mu = jnp.mean(x, axis=-1, keepdims=True)
        xc = x - mu
        var = jnp.mean(xc * xc, axis=-1, keepdims=True)
        h = xc * lax.rsqrt(var + NORM_EPS) * g_ref[...] + b_ref[...]
        h32_ref[...] = h
        h16_ref[...] = h.astype(BF16)
        p = pos_ref[...]
        for k in range(2):
            inv, first, second = pat_ref[3 * k:3 * k + 1, :], pat_ref[3 * k + 1:3 * k + 2, :], pat_ref[3 * k + 2:3 * k + 3, :]
            ang = p * inv
            sn = jnp.sin(ang)
            tabs[3 * k][...] = jnp.where(first + second > 0.0, jnp.cos(ang), 1.0)
            tabs[3 * k + 1][...] = -first * sn
            tabs[3 * k + 2][...] = second * sn

    row = pl.BlockSpec((tm, d), lambda i: (i, 0))
    vec = pl.BlockSpec((1, d), lambda i: (0, 0))
    tab = pl.BlockSpec((tm, LANES), lambda i: (i, 0))
    res, landed = call_hosting_exchange(
        body, xch, grid=(t // tm,),
        in_specs=[row, vec, vec, pl.BlockSpec((tm, 1), lambda i: (i, 0)), pl.BlockSpec((8, LANES), lambda i: (0, 0))],
        out_specs=[row, row] + [tab] * 6,
        out_shape=[SDS((t, d), F32), SDS((t, d), BF16)] + [SDS((t, LANES), F32)] * 6,
        scratch_shapes=[], name="embed_fwd", operands=(x2, g, b, pos, _rope_lane_patterns()))
    return (res[0], res[1], tuple(res[2:5]), tuple(res[5:8])), landed


Q_BLK = 128
UNROLL_FWD = 16
UNROLL_BWD = 16


def _pattern_geometry(d):
    length = SEQ // d
    nblk = length // Q_BLK
    kwin = min(2 * Q_BLK, length)
    return length, nblk, kwin


def _block_coords(idx, d):
    length, nblk, kwin = _pattern_geometry(d)
    r = lax.shift_right_logical(idx, nblk.bit_length() - 1)
    i = idx & (nblk - 1)
    q0 = pl.multiple_of(r * length + i * Q_BLK, Q_BLK)
    ks = jnp.clip(i * Q_BLK - N_SIDE, 0, length - kwin)
    k0 = pl.multiple_of(r * length + ks, N_SIDE)
    qpos = i * Q_BLK + lax.broadcasted_iota(jnp.int32, (Q_BLK, kwin), 0)
    kpos = ks + lax.broadcasted_iota(jnp.int32, (Q_BLK, kwin), 1)
    valid = jnp.abs(kpos - qpos) <= N_SIDE
    return q0, k0, kwin, valid


def _deinterleave(src_ref, dst_ref, d, dtype, tmp_ref):
    if d == 1:
        dst_ref[...] = src_ref[...].astype(dtype)
        return
    q = SEQ // 4
    if d == 4:
        for r in range(4):
            dst_ref[r * q:(r + 1) * q, :] = src_ref[pl.ds(r, q, stride=4), :].astype(dtype)
        return
    assert d == 16
    n = SEQ // 16
    for r in range(4):
        tmp_ref[r * q:(r + 1) * q, :] = src_ref[pl.ds(r, q, stride=4), :]
    for r in range(4):
        for j in range(4):
            dst_ref[(r + 4 * j) * n:(r + 4 * j + 1) * n, :] = tmp_ref[pl.ds(r * q + j, n, stride=4), :].astype(dtype)


def _class16_to_class4(src_ref, dst_ref):
    q, n = SEQ // 4, SEQ // 16
    for r in range(4):
        for j in range(4):
            dst_ref[pl.ds(r * q + j, n, stride=4), :] = src_ref[(r + 4 * j) * n:(r + 4 * j + 1) * n, :]


def _interleave(src_ref, dst_ref, d, tmp_ref, accumulate):
    q = SEQ // 4
    if d == 16:
        _class16_to_class4(src_ref, tmp_ref)
        src_ref = tmp_ref
    else:
        assert d == 4
    for r in range(4):
        rows = pl.ds(r, q, stride=4)
        val = src_ref[r * q:(r + 1) * q, :]
        dst_ref[rows, :] = dst_ref[rows, :] + val if accumulate else val


def a_attn_fwd(proj, ca, sa, sb, nb, xch):
    t = proj.shape[0]
    n_pairs = A_WIDTH // LANES

    def body(q_ref, k_ref, v_ref, c_ref, sa_ref, sb_ref, y_ref, lse_ref, *rest):
        qkv_d, (qr_s, kr_s, oc_s, lc_s, o1_s, l1_s, o2_s, l2_s, o3_s, l3_s, tmp_s) = rest[:9], rest[9:]
        c, s_a, s_b = c_ref[...], sa_ref[...], sb_ref[...]
        qr_s[...] = _rope_fwd(q_ref[...], c, s_a, s_b, A_ROT // 2) * (A_HEAD_DIM ** -0.5)
        kr_s[...] = _rope_fwd(k_ref[...], c, s_a, s_b, A_ROT // 2)
        head0 = lax.broadcasted_iota(jnp.int32, (Q_BLK, LANES), 1) < A_HEAD_DIM
        nat = ((o1_s, l1_s), (o2_s, l2_s), (o3_s, l3_s))

        for g, d in enumerate(DILATIONS):
            qd_s, kd_s, vd_s = qkv_d[3 * g:3 * g + 3]
            _deinterleave(qr_s, qd_s, d, BF16, tmp_s)
            _deinterleave(kr_s, kd_s, d, BF16, tmp_s)
            _deinterleave(v_ref, vd_s, d, BF16, tmp_s)
            o_dst, l_dst = (nat[g] if d == 1 else (oc_s, lc_s))

            def block(idx, carry, d=d, o_dst=o_dst, l_dst=l_dst, qd_s=qd_s, kd_s=kd_s, vd_s=vd_s):
                q0, k0, kwin, valid = _block_coords(idx, d)
                qb = qd_s[pl.ds(q0, Q_BLK), :]
                kb = kd_s[pl.ds(k0, kwin), :]
                vb = vd_s[pl.ds(k0, kwin), :]
                zero = jnp.zeros_like(qb)
                q2 = jnp.concatenate([jnp.where(head0, qb, zero), jnp.where(head0, zero, qb)], 0)
                s = jnp.where(jnp.concatenate([valid, valid], 0), _dot_nt(q2, kb), NEG_INF)
                m = jnp.max(s, axis=-1, keepdims=True)
                p = jnp.exp(s - m)
                l = jnp.sum(p, axis=-1, keepdims=True)
                o2 = _dot(p.astype(BF16), vb) / l
                l2 = m + jnp.log(l)
                o_dst[pl.ds(q0, Q_BLK), :] = jnp.where(head0, o2[:Q_BLK], o2[Q_BLK:])
                l_dst[pl.ds(q0, Q_BLK), :] = jnp.where(head0, l2[:Q_BLK], l2[Q_BLK:])
                return carry

            lax.fori_loop(0, SEQ // Q_BLK, block, 0, unroll=UNROLL_FWD)
            if d > 1:
                _interleave(oc_s, nat[g][0], d, tmp_s, False)
                _interleave(lc_s, nat[g][1], d, tmp_s, False)

        def merge(ci, carry):
            rows = pl.ds(pl.multiple_of(ci * 256, 256), 256)
            l1, l2, l3 = l1_s[rows, :], l2_s[rows, :], l3_s[rows, :]
            m = jnp.maximum(jnp.maximum(l1, l2), l3)
            w1, w2, w3 = jnp.exp(l1 - m), jnp.exp(l2 - m), jnp.exp(l3 - m)
            w = w1 + w2 + w3
            y_ref[rows, :] = (w1 * o1_s[rows, :] + w2 * o2_s[rows, :] + w3 * o3_s[rows, :]) / w
            lse_ref[rows, :] = m + jnp.log(w)
            return carry

        lax.fori_loop(0, SEQ // 256, merge, 0)

    def col(off):
        return pl.BlockSpec((SEQ, LANES), lambda b, hp: (b, off + hp))

    tab = pl.BlockSpec((SEQ, LANES), lambda b, hp: (b, 0))
    out = pl.BlockSpec((SEQ, LANES), lambda b, hp: (b, hp))
    f32s = pltpu.VMEM((SEQ, LANES), F32)
    res, landed = call_hosting_exchange(
        body, xch, grid=(nb, n_pairs),
        in_specs=[col(0), col(n_pairs), col(2 * n_pairs), tab, tab, tab],
        out_specs=[out] * 11,
        out_shape=[SDS((t, A_WIDTH), F32)] * 2 + [SDS((t, A_WIDTH), BF16)] * 9,
        scratch_shapes=[f32s] * 11,
        name="a_attn_fwd", operands=(proj, proj, proj, ca, sa, sb))
    return res[:2], res[2:], landed


def a_attn_bwd(qkv_d, ca, sa, sb, dy, y, lse, nb, xch):
    t = dy.shape[0]
    n_pairs = A_WIDTH // LANES

    def body(*refs):
        qkv_refs = refs[:9]
        (c_ref, sa_ref, sb_ref, do_ref, y_ref, lse_ref, dq_ref, dk_ref, dv_ref,
         l0n_s, l1n_s, d0n_s, d1n_s, dod_s, l0d_s, l1d_s, d0d_s, d1d_s,
         dqc_s, dkc_s, dvc_s, dq4_s, dk4_s, dv4_s, dqn_s, dkn_s, dvn_s, tmp_s) = refs[9:]
        c, s_a, s_b = c_ref[...], sa_ref[...], sb_ref[...]
        head0 = lax.broadcasted_iota(jnp.int32, (Q_BLK, LANES), 1) < A_HEAD_DIM

        def per_head_rows(ci, carry):
            rows = pl.ds(pl.multiple_of(ci * 256, 256), 256)
            h0 = lax.broadcasted_iota(jnp.int32, (256, LANES), 1) < A_HEAD_DIM
            tt = do_ref[rows, :] * y_ref[rows, :]
            d0n_s[rows, :] = jnp.broadcast_to(jnp.sum(jnp.where(h0, tt, 0.0), axis=-1, keepdims=True), (256, LANES))
            d1n_s[rows, :] = jnp.broadcast_to(jnp.sum(jnp.where(h0, 0.0, tt), axis=-1, keepdims=True), (256, LANES))
            l = lse_ref[rows, :]
            lr = pltpu.roll(l, A_HEAD_DIM, 1)
            l0n_s[rows, :] = jnp.where(h0, l, lr)
            l1n_s[rows, :] = jnp.where(h0, lr, l)
            return carry

        lax.fori_loop(0, SEQ // 256, per_head_rows, 0)
        assert DILATIONS == (1, 4, 16)

        for g, d in enumerate(DILATIONS):
            qd_s, kd_s, vd_s = qkv_refs[3 * g:3 * g + 3]
            _deinterleave(do_ref, dod_s, d, BF16, tmp_s)
            if d > 1:
                for src, dst in ((l0n_s, l0d_s), (l1n_s, l1d_s), (d0n_s, d0d_s), (d1n_s, d1d_s)):
                    _deinterleave(src, dst, d, F32, tmp_s)
            l0, l1, d0, d1 = (l0n_s, l1n_s, d0n_s, d1n_s) if d == 1 else (l0d_s, l1d_s, d0d_s, d1d_s)
            dq_dst, dk_dst, dv_dst = {1: (dqn_s, dkn_s, dvn_s), 4: (dq4_s, dk4_s, dv4_s), 16: (dqc_s, dkc_s, dvc_s)}[d]
            dk_dst[...] = jnp.zeros_like(dk_dst)
            dv_dst[...] = jnp.zeros_like(dv_dst)

            def block(idx, carry, d=d, l0=l0, l1=l1, d0=d0, d1=d1, dq_dst=dq_dst, dk_dst=dk_dst, dv_dst=dv_dst,
                      qd_s=qd_s, kd_s=kd_s, vd_s=vd_s):
                q0, k0, kwin, valid = _block_coords(idx, d)
                qrows = pl.ds(q0, Q_BLK)
                krows = pl.ds(k0, kwin)
                qb, dob = qd_s[qrows, :], dod_s[qrows, :]
                kb, vb = kd_s[krows, :], vd_s[krows, :]
                zero = jnp.zeros_like(qb)
                q2 = jnp.concatenate([jnp.where(head0, qb, zero), jnp.where(head0, zero, qb)], 0)
                do2 = jnp.concatenate([jnp.where(head0, dob, zero), jnp.where(head0, zero, dob)], 0)
                wide = lambda x: jnp.concatenate([x] * (kwin // LANES), 1)
                lse2 = wide(jnp.concatenate([l0[qrows, :], l1[qrows, :]], 0))
                dd2 = wide(jnp.concatenate([d0[qrows, :], d1[qrows, :]], 0))
                s = jnp.where(jnp.concatenate([valid, valid], 0), _dot_nt(q2, kb), NEG_INF)
                p = jnp.exp(s - lse2)
                ds = (p * (_dot_nt(do2, vb) - dd2)).astype(BF16)
                dq2 = _dot(ds, kb)
                dq_dst[qrows, :] = jnp.where(head0, dq2[:Q_BLK], dq2[Q_BLK:])
                dk_dst[krows, :] += _dot_tn(ds, q2)
                dv_dst[krows, :] += _dot_tn(p.astype(BF16), do2)
                return carry

            lax.fori_loop(0, SEQ // Q_BLK, block, 0, unroll=UNROLL_BWD)

        for c16, c4, nat in ((dqc_s, dq4_s, dqn_s), (dkc_s, dk4_s, dkn_s), (dvc_s, dv4_s, dvn_s)):
            _class16_to_class4(c16, tmp_s)
            c4[...] = c4[...] + tmp_s[...]
            _interleave(c4, nat, 4, tmp_s, True)

        dq_ref[...] = _rope_bwd(dqn_s[...] * (A_HEAD_DIM ** -0.5), c, s_a, s_b, A_ROT // 2).astype(BF16)
        dk_ref[...] = _rope_bwd(dkn_s[...], c, s_a, s_b, A_ROT // 2).astype(BF16)
        dv_ref[...] = dvn_s[...].astype(BF16)

    tab = pl.BlockSpec((SEQ, LANES), lambda b, hp: (b, 0))
    blk = pl.BlockSpec((SEQ, LANES), lambda b, hp: (b, hp))
    f32s = pltpu.VMEM((SEQ, LANES), F32)
    b16s = pltpu.VMEM((SEQ, LANES), BF16)
    return call_hosting_exchange(
        body, xch, grid=(nb, n_pairs),
        in_specs=[blk] * 9 + [tab, tab, tab, blk, blk, blk],
        out_specs=[blk, blk, blk],
        out_shape=[SDS((t, A_WIDTH), BF16)] * 3,
        scratch_shapes=[f32s] * 4 + [b16s] + [f32s] * 14,
        name="a_attn_bwd", operands=(*qkv_d, ca, sa, sb, dy, y, lse))


MLA_SCALE = (MLA_NOPE + MLA_ROPE) ** -0.5
LOG2E = 1.4426950408889634
MLA_QW = MLA_HEADS * LANES
MLA_KVW = MLA_QW + MLA_WIDTH


def _rms(x, g):
    r = lax.rsqrt(jnp.mean(x * x, axis=-1, keepdims=True) + NORM_EPS)
    return x * r * g, r


def _rms_bwd(dn, x, r, g):
    tg = dn * g
    dx = r * tg - x * (r * r * r) * jnp.mean(tg * x, axis=-1, keepdims=True)
    return dx, jnp.sum(dn * x * r, axis=0, keepdims=True)


def mla_prep_fwd(proj, cm, sma, smb, g_cq, g_ckv, wuq, wkv):
    t = proj.shape[0]
    tm = 512

    def body(cq_ref, ckv_ref, kr_ref, c_ref, sa_ref, sb_ref, gq_ref, gkv_ref, wuq_ref, wkv_ref, q_ref, k_ref, v_ref):
        c, s_a, s_b = c_ref[...], sa_ref[...], sb_ref[...]
        cqn, _ = _rms(cq_ref[...], gq_ref[...])
        qf = _dot(cqn.astype(BF16), wuq_ref[...])
        ckvn, _ = _rms(ckv_ref[...], gkv_ref[...])
        kvf = _dot(ckvn.astype(BF16), wkv_ref[...])
        krope = _rope_fwd(kr_ref[...], c, s_a, s_b, MLA_ROPE // 2)
        for h in range(MLA_HEADS):
            cols = slice(h * LANES, (h + 1) * LANES)
            q_ref[:, cols] = (_rope_fwd(qf[:, cols], c, s_a, s_b, MLA_ROPE // 2) * (MLA_SCALE * LOG2E)).astype(BF16)
            k_ref[:, cols] = (kvf[:, cols] + krope).astype(BF16)
        v_ref[...] = kvf[:, MLA_QW:].astype(BF16)

    def row(w, j):
        return pl.BlockSpec((tm, w), lambda i: (i, j))

    def full(a):
        return pl.BlockSpec(a.shape, lambda i: (0, 0))

    return pl.pallas_call(
        body, grid=(t // tm,),
        in_specs=[row(256, 4096 // 256), row(128, 4352 // 128), row(128, 4480 // 128), row(128, 0), row(128, 0), row(128, 0),
                  full(g_cq), full(g_ckv), full(wuq), full(wkv)],
        out_specs=[row(MLA_QW, 0), row(MLA_QW, 0), row(MLA_WIDTH, 0)],
        out_shape=[SDS((t, MLA_QW), BF16), SDS((t, MLA_QW), BF16), SDS((t, MLA_WIDTH), BF16)],
        name="mla_prep_fwd", compiler_params=_params("parallel"))(proj, proj, proj, cm, sma, smb, g_cq, g_ckv, wuq, wkv)


def mla_prep_bwd(proj, cm, sma, smb, g_cq, g_ckv, wuq, wkv, dq, dk, dv):
    t = proj.shape[0]
    tm = 512

    def body(cq_ref, ckv_ref, c_ref, sa_ref, sb_ref, gq_ref, gkv_ref, wuq_ref, wkv_ref, dq_ref, dk_ref, dv_ref,
             dcc_ref, dqf_ref, cqn_ref, dkvf_ref, ckvn_ref, dgq_ref, dgkv_ref):
        @pl.when(pl.program_id(0) == 0)
        def _():
            dgq_ref[...] = jnp.zeros_like(dgq_ref)
            dgkv_ref[...] = jnp.zeros_like(dgkv_ref)

        c, s_a, s_b = c_ref[...], sa_ref[...], sb_ref[...]
        cq, ckv = cq_ref[...], ckv_ref[...]
        cqn, rq = _rms(cq, gq_ref[...])
        ckvn, rkv = _rms(ckv, gkv_ref[...])
        cqn_ref[...] = cqn.astype(BF16)
        ckvn_ref[...] = ckvn.astype(BF16)
        lane = lax.broadcasted_iota(jnp.int32, (tm, LANES), 1)
        rope_lanes = (lane >= MLA_NOPE) & (lane < MLA_NOPE + MLA_ROPE)
        dkrope = jnp.zeros((tm, LANES), F32)
        for h in range(MLA_HEADS):
            cols = slice(h * LANES, (h + 1) * LANES)
            dqf_ref[:, cols] = _rope_bwd(dq_ref[:, cols] * MLA_SCALE, c, s_a, s_b, MLA_ROPE // 2).astype(BF16)
            dkh = dk_ref[:, cols] * (1.0 / LOG2E)
            dkvf_ref[:, cols] = dkh.astype(BF16)
            dkrope = dkrope + dkh
        dkvf_ref[:, MLA_QW:] = dv_ref[...].astype(BF16)
        dkr = _rope_bwd(jnp.where(rope_lanes, dkrope, 0.0), c, s_a, s_b, MLA_ROPE // 2)
        dcqn = _dot_nt(dqf_ref[...], wuq_ref[...])
        dckvn = _dot_nt(dkvf_ref[...], wkv_ref[...])
        dcq, dgq = _rms_bwd(dcqn, cq, rq, gq_ref[...])
        dckv, dgkv = _rms_bwd(dckvn, ckv, rkv, gkv_ref[...])
        dgq_ref[...] += dgq
        dgkv_ref[...] += dgkv
        dcc_ref[:, 0:256] = dcq.astype(BF16)
        dcc_ref[:, 256:384] = dckv.astype(BF16)
        dcc_ref[:, 384:512] = dkr.astype(BF16)

    def row(w, j):
        return pl.BlockSpec((tm, w), lambda i: (i, j))

    def full(a):
        return pl.BlockSpec(a.shape, lambda i: (0, 0))

    return pl.pallas_call(
        body, grid=(t // tm,),
        in_specs=[row(256, 4096 // 256), row(128, 4352 // 128), row(128, 0), row(128, 0), row(128, 0),
                  full(g_cq), full(g_ckv), full(wuq), full(wkv), row(MLA_QW, 0), row(MLA_QW, 0), row(MLA_WIDTH, 0)],
        out_specs=[row(512, 0), row(MLA_QW, 0), row(256, 0), row(MLA_KVW, 0), row(128, 0), full(g_cq), full(g_ckv)],
        out_shape=[SDS((t, 512), BF16), SDS((t, MLA_QW), BF16), SDS((t, 256), BF16), SDS((t, MLA_KVW), BF16),
                   SDS((t, 128), BF16), SDS(g_cq.shape, F32), SDS(g_ckv.shape, F32)],
        name="mla_prep_bwd", compiler_params=_params("arbitrary"))(proj, proj, cm, sma, smb, g_cq, g_ckv, wuq, wkv, dq, dk, dv)


MLA_TQ = SEQ
MLA_SUB = 256


def mla_attn_fwd(qb, kb, vb, nb):
    t = qb.shape[0]
    nq = SEQ // MLA_TQ
    n_pairs = MLA_HEADS // 2

    def body(q_ref, k_ref, v_ref, y_ref, lse_ref):
        head0 = lax.broadcasted_iota(jnp.int32, (MLA_SUB, LANES), 1) < MLA_V
        v = v_ref[...]
        vhead0 = lax.broadcasted_iota(jnp.int32, v.shape, 1) < MLA_V
        one = jnp.ones_like(v)
        vh = [jnp.where(vhead0 == (h == 0), v, one) for h in range(2)]
        for sub in range(MLA_TQ // MLA_SUB):
            rows = slice(sub * MLA_SUB, (sub + 1) * MLA_SUB)
            outs, lses = [], []
            for h in range(2):
                cols = slice(h * LANES, (h + 1) * LANES)
                s = _dot_nt(q_ref[rows, cols], k_ref[:, cols])
                m = jnp.max(s, axis=-1, keepdims=True)
                p = jnp.exp2(s - m).astype(BF16)
                ol = _dot(p, vh[h])
                l = pltpu.roll(ol, MLA_V, 1)
                outs.append(ol / l)
                lses.append(m + jnp.log2(l))
            y_ref[rows, :] = jnp.where(head0, outs[0], outs[1])
            lse_ref[rows, :] = jnp.where(head0, lses[0], lses[1])

    return pl.pallas_call(
        body, grid=(nb, n_pairs, nq),
        in_specs=[pl.BlockSpec((MLA_TQ, 2 * LANES), lambda b, hp, i: (b * nq + i, hp)),
                  pl.BlockSpec((SEQ, 2 * LANES), lambda b, hp, i: (b, hp)),
                  pl.BlockSpec((SEQ, LANES), lambda b, hp, i: (b, hp))],
        out_specs=[pl.BlockSpec((MLA_TQ, LANES), lambda b, hp, i: (b * nq + i, hp))] * 2,
        out_shape=[SDS((t, MLA_WIDTH), F32)] * 2,
        name="mla_attn_fwd", compiler_params=_params("parallel", "parallel", "parallel"))(qb, kb, vb)


def mla_attn_bwd(qb, kb, vb, dy, y, lse, nb, xch):
    t = qb.shape[0]
    nq = SEQ // MLA_TQ
    n_pairs = MLA_HEADS // 2

    def body(q_ref, k_ref, v_ref, do_ref, y_ref, lse_ref, dq_ref, dk_ref, dv_ref):
        @pl.when(pl.program_id(2) == 0)
        def _():
            dk_ref[...] = jnp.zeros_like(dk_ref)
            dv_ref[...] = jnp.zeros_like(dv_ref)

        head0 = lax.broadcasted_iota(jnp.int32, (MLA_SUB, LANES), 1) < MLA_V
        v = v_ref[...]
        for sub in range(MLA_TQ // MLA_SUB):
            rows = slice(sub * MLA_SUB, (sub + 1) * MLA_SUB)
            do = do_ref[rows, :]
            lse = lse_ref[rows, :]
            tt = do * y_ref[rows, :]
            dv = jnp.zeros((SEQ, LANES), F32)
            for h in range(2):
                sel = head0 if h == 0 else ~head0
                lo = h * MLA_V
                cols = slice(h * LANES, (h + 1) * LANES)
                q = q_ref[rows, cols]
                k = k_ref[:, cols]
                dd = jnp.sum(jnp.where(sel, tt, 0.0), axis=-1, keepdims=True)
                doh = jnp.where(sel, do, 0.0).astype(BF16)
                p = jnp.exp2(_dot_nt(q, k) - lse[:, lo:lo + 1])
                dp = _dot_nt(doh, v)
                ds = (p * (dp - dd)).astype(BF16)
                dq_ref[rows, cols] = _dot(ds, k)
                dk_ref[:, cols] += _dot_tn(ds, q)
                dv = dv + _dot_tn(p.astype(BF16), doh)
            dv_ref[...] += dv

    qspec = pl.BlockSpec((MLA_TQ, 2 * LANES), lambda b, hp, i: (b * nq + i, hp))
    kspec = pl.BlockSpec((SEQ, 2 * LANES), lambda b, hp, i: (b, hp))
    vspec = pl.BlockSpec((SEQ, LANES), lambda b, hp, i: (b, hp))
    ospec = pl.BlockSpec((MLA_TQ, LANES), lambda b, hp, i: (b * nq + i, hp))
    return call_hosting_exchange(
        body, xch, grid=(nb, n_pairs, nq),
        in_specs=[qspec, kspec, vspec, ospec, ospec, ospec],
        out_specs=[qspec, kspec, vspec],
        out_shape=[SDS((t, MLA_QW), F32), SDS((t, MLA_QW), F32), SDS((t, MLA_WIDTH), F32)],
        scratch_shapes=[], name="mla_attn_bwd", operands=(qb, kb, vb, dy, y, lse))


MEM_TQ = SEQ
MEM_SUB = 512
MEM_SCALE = MEM_HEAD_DIM ** -0.5
MQ_BLK4 = 5120 // MEM_WIDTH


def mem_attn_fwd(proj, mkv, nb):
    t = proj.shape[0]
    nq = SEQ // MEM_TQ

    def body(q_ref, mk_ref, mv_ref, y_ref):
        for sub in range(MEM_TQ // MEM_SUB):
            rows = slice(sub * MEM_SUB, (sub + 1) * MEM_SUB)
            for h in range(MEM_HEADS):
                cols = slice(h * LANES, (h + 1) * LANES)
                s = _dot_nt(q_ref[rows, cols].astype(BF16), mk_ref[:, cols]) * MEM_SCALE
                m = jnp.max(s, axis=-1, keepdims=True)
                p = jnp.exp(s - m)
                l = jnp.sum(p, axis=-1, keepdims=True)
                y_ref[rows, cols] = _dot(p.astype(BF16), mv_ref[:, cols]) / l

    return pl.pallas_call(
        body, grid=(nb, nq),
        in_specs=[pl.BlockSpec((MEM_TQ, MEM_WIDTH), lambda b, i: (b * nq + i, MQ_BLK4)),
                  pl.BlockSpec((N_MEM, MEM_WIDTH), lambda b, i: (b, 0)),
                  pl.BlockSpec((N_MEM, MEM_WIDTH), lambda b, i: (b, 1))],
        out_specs=pl.BlockSpec((MEM_TQ, MEM_WIDTH), lambda b, i: (b * nq + i, 0)),
        out_shape=SDS((t, MEM_WIDTH), F32),
        name="mem_attn_fwd", compiler_params=_params("parallel", "parallel"))(proj, mkv, mkv)


def mem_attn_bwd(proj, mkv, dy, nb):
    t = proj.shape[0]
    nq = SEQ // MEM_TQ

    def body(q_ref, mk_ref, mv_ref, do_ref, dq_ref, dmk_ref, dmv_ref):
        @pl.when(pl.program_id(1) == 0)
        def _():
            dmk_ref[...] = jnp.zeros_like(dmk_ref)
            dmv_ref[...] = jnp.zeros_like(dmv_ref)

        for sub in range(MEM_TQ // MEM_SUB):
            rows = slice(sub * MEM_SUB, (sub + 1) * MEM_SUB)
            for h in range(MEM_HEADS):
                cols = slice(h * LANES, (h + 1) * LANES)
                q = q_ref[rows, cols].astype(BF16)
                mk, mv = mk_ref[:, cols], mv_ref[:, cols]
                do = do_ref[rows, cols].astype(BF16)
                s = _dot_nt(q, mk) * MEM_SCALE
                e = jnp.exp(s - jnp.max(s, axis=-1, keepdims=True))
                p = e / jnp.sum(e, axis=-1, keepdims=True)
                dp = _dot_nt(do, mv)
                ds = (p * (dp - jnp.sum(p * dp, axis=-1, keepdims=True)) * MEM_SCALE).astype(BF16)
                dq_ref[rows, cols] = _dot(ds, mk).astype(BF16)
                dmk_ref[:, cols] += _dot_tn(ds, q)
                dmv_ref[:, cols] += _dot_tn(p.astype(BF16), do)

    ospec = pl.BlockSpec((MEM_TQ, MEM_WIDTH), lambda b, i: (b * nq + i, 0))
    kspec = pl.BlockSpec((N_MEM, MEM_WIDTH), lambda b, i: (b, 0))
    return pl.pallas_call(
        body, grid=(nb, nq),
        in_specs=[pl.BlockSpec((MEM_TQ, MEM_WIDTH), lambda b, i: (b * nq + i, MQ_BLK4)),
                  kspec, pl.BlockSpec((N_MEM, MEM_WIDTH), lambda b, i: (b, 1)), ospec],
        out_specs=[ospec, kspec, kspec],
        out_shape=[SDS((t, MEM_WIDTH), BF16), SDS((nb * N_MEM, MEM_WIDTH), F32), SDS((nb * N_MEM, MEM_WIDTH), F32)],
        name="mem_attn_bwd", compiler_params=_params("parallel", "arbitrary"))(proj, mkv, mkv, dy)


ROW_TM = 512
AG_BLK = 3072 // 1024
BG_BLK = 4608 // 512
MG_BLK = 5632 // 512
GROUPS = ((0, A_WIDTH), (A_WIDTH, MLA_WIDTH), (A_WIDTH + MLA_WIDTH, MEM_WIDTH))
D_MIX = 2048


def _gate_specs():
    def row(w, j):
        return pl.BlockSpec((ROW_TM, w), lambda i: (i, j))

    def vec(w):
        return pl.BlockSpec((1, w), lambda i: (0, 0))

    ys = [row(A_WIDTH, 0), row(MLA_WIDTH, 0), row(MEM_WIDTH, 0)]
    gates = [row(A_WIDTH, AG_BLK), row(MLA_WIDTH, BG_BLK), row(MEM_WIDTH, MG_BLK)]
    gains = [vec(A_WIDTH), vec(MLA_WIDTH), vec(MEM_WIDTH)]
    return row, vec, ys, gates, gains


def gate_out_ln_loss(ya, yb, ym, proj, goa, gob, gom, wout, h32, target, gp, bp):
    t, d = h32.shape
    _, _, ys, gates, gains = _gate_specs()

    def body(ya_ref, yb_ref, ym_ref, ga_ref, gb_ref, gm_ref, goa_ref, gob_ref, gom_ref, w_ref, h_ref, t_ref, gp_ref, bp_ref,
             z_ref, du32_ref, du16_ref, loss_ref, dgp_ref, dbp_ref):
        @pl.when(pl.program_id(0) == 0)
        def _():
            loss_ref[...] = jnp.zeros_like(loss_ref)
            dgp_ref[...] = jnp.zeros_like(dgp_ref)
            dbp_ref[...] = jnp.zeros_like(dbp_ref)

        for (off, w), y_ref, g_ref, go_ref in zip(GROUPS, (ya_ref, yb_ref, ym_ref), (ga_ref, gb_ref, gm_ref),
                                                  (goa_ref, gob_ref, gom_ref)):
            n, _ = _rms(y_ref[...], go_ref[...])
            gt = g_ref[...]
            z_ref[:, off:off + w] = (n * (gt * _sigmoid(gt))).astype(BF16)
        g = gp_ref[...]
        u = ALPHA * h_ref[...] + _dot(z_ref[...], w_ref[...])
        mu = jnp.mean(u, axis=-1, keepdims=True)
        uc = u - mu
        rstd = lax.rsqrt(jnp.mean(uc * uc, axis=-1, keepdims=True) + NORM_EPS)
        xhat = uc * rstd
        err = xhat * g + bp_ref[...] - t_ref[...]
        tok = jnp.sum(err * err, axis=-1, keepdims=True) * (1.0 / d)
        loss_ref[...] += 0.5 * jnp.sum(tok, axis=0, keepdims=True)
        dout = err * (1.0 / d)
        dxhat = dout * g
        du = rstd * (dxhat - jnp.mean(dxhat, axis=-1, keepdims=True)
                     - xhat * jnp.mean(dxhat * xhat, axis=-1, keepdims=True))
        du32_ref[...] = du
        du16_ref[...] = du.astype(BF16)
        dgp_ref[...] += jnp.sum(dout * xhat, axis=0, keepdims=True)
        dbp_ref[...] += jnp.sum(dout, axis=0, keepdims=True)

    row = pl.BlockSpec((ROW_TM, d), lambda i: (i, 0))
    vec = pl.BlockSpec((1, d), lambda i: (0, 0))
    zrow = pl.BlockSpec((ROW_TM, D_MIX), lambda i: (i, 0))
    return pl.pallas_call(
        body, grid=(t // ROW_TM,),
        in_specs=ys + gates + gains + [pl.BlockSpec((D_MIX, d), lambda i: (0, 0)), row, row, vec, vec],
        out_specs=[zrow, row, row, pl.BlockSpec((1, LANES), lambda i: (0, 0)), vec, vec],
        out_shape=[SDS((t, D_MIX), BF16), SDS((t, d), F32), SDS((t, d), BF16), SDS((1, LANES), F32), SDS((1, d), F32),
                   SDS((1, d), F32)],
        name="gate_out_ln_loss", compiler_params=_params("arbitrary"))(
            ya, yb, ym, proj, proj, proj, goa, gob, gom, wout, h32, target, gp, bp)


def gate_bwd(du16, wout, ya, yb, ym, proj, goa, gob, gom):
    t = ya.shape[0]
    row, vec, ys, gates, gains = _gate_specs()

    def body(du_ref, w_ref, ya_ref, yb_ref, ym_ref, ga_ref, gb_ref, gm_ref, goa_ref, gob_ref, gom_ref,
             dya_ref, dyb_ref, dym_ref, dga_ref, dgb_ref, dgm_ref, dgoa_ref, dgob_ref, dgom_ref):
        @pl.when(pl.program_id(0) == 0)
        def _():
            dgoa_ref[...] = jnp.zeros_like(dgoa_ref)
            dgob_ref[...] = jnp.zeros_like(dgob_ref)
            dgom_ref[...] = jnp.zeros_like(dgom_ref)

        dz = _dot_nt(du_ref[...], w_ref[...])
        for (off, w), y_ref, g_ref, go_ref, dy_ref, dg_ref, dgo_ref in zip(
                GROUPS, (ya_ref, yb_ref, ym_ref), (ga_ref, gb_ref, gm_ref), (goa_ref, gob_ref, gom_ref),
                (dya_ref, dyb_ref, dym_ref), (dga_ref, dgb_ref, dgm_ref), (dgoa_ref, dgob_ref, dgom_ref)):
            dzg = dz[:, off:off + w]
            y, gt, go = y_ref[...], g_ref[...], go_ref[...]
            n, r = _rms(y, go)
            sg = _sigmoid(gt)
            dg_ref[...] = (dzg * n * (sg * (1.0 + gt * (1.0 - sg)))).astype(BF16)
            dy, dgo = _rms_bwd(dzg * (gt * sg), y, r, go)
            dy_ref[...] = dy
            dgo_ref[...] += dgo

    widths = (A_WIDTH, MLA_WIDTH, MEM_WIDTH)
    return pl.pallas_call(
        body, grid=(t // ROW_TM,),
        in_specs=[row(D_MODEL, 0), pl.BlockSpec((D_MIX, D_MODEL), lambda i: (0, 0))] + ys + gates + gains,
        out_specs=[row(w, 0) for w in widths] * 2 + [vec(w) for w in widths],
        out_shape=[SDS((t, w), F32) for w in widths] + [SDS((t, w), BF16) for w in widths] + [SDS((1, w), F32) for w in widths],
        name="gate_bwd", compiler_params=_params("arbitrary"))(du16, wout, ya, yb, ym, proj, proj, proj, goa, gob, gom)


def dh_ln_bwd(pieces, win_t, du32, x2, g_emb, xch):
    t, d = x2.shape

    def body(*refs):
        p_refs = refs[:len(pieces)]
        w_ref, du_ref, x_ref, g_ref, dx_ref, dg_ref, db_ref = refs[len(pieces):]

        @pl.when(pl.program_id(0) == 0)
        def _():
            dg_ref[...] = jnp.zeros_like(dg_ref)
            db_ref[...] = jnp.zeros_like(db_ref)

        dh = ALPHA * du_ref[...]
        for p_ref, off, w in zip(p_refs, PIECE_OFFS, PIECE_WIDTHS):
            dh = dh + _dot(p_ref[...], w_ref[off:off + w, :])
        x = x_ref[...]
        xc = x - jnp.mean(x, axis=-1, keepdims=True)
        rstd = lax.rsqrt(jnp.mean(xc * xc, axis=-1, keepdims=True) + NORM_EPS)
        xhat = xc * rstd
        dg_ref[...] += jnp.sum(dh * xhat, axis=0, keepdims=True)
        db_ref[...] += jnp.sum(dh, axis=0, keepdims=True)
        tg = dh * g_ref[...]
        dx_ref[...] = rstd * (tg - jnp.mean(tg, axis=-1, keepdims=True)
                              - xhat * jnp.mean(tg * xhat, axis=-1, keepdims=True))

    row = pl.BlockSpec((ROW_TM, d), lambda i: (i, 0))
    vec = pl.BlockSpec((1, d), lambda i: (0, 0))
    return call_hosting_exchange(
        body, xch, grid=(t // ROW_TM,),
        in_specs=[pl.BlockSpec((ROW_TM, w), lambda i: (i, 0)) for w in PIECE_WIDTHS]
        + [pl.BlockSpec(win_t.shape, lambda i: (0, 0)), row, row, vec],
        out_specs=[row, vec, vec],
        out_shape=[SDS((t, d), F32), SDS((1, d), F32), SDS((1, d), F32)],
        scratch_shapes=[], name="dh_ln_bwd", operands=(*pieces, win_t, du32, x2, g_emb))


def _adamw(w, g, m, v):
    m2 = ADAM_B1 * m + (1.0 - ADAM_B1) * g
    v2 = ADAM_B2 * v + (1.0 - ADAM_B2) * (g * g)
    m_hat = m2 / (1.0 - ADAM_B1 ** ADAM_STEP)
    v_hat = v2 / (1.0 - ADAM_B2 ** ADAM_STEP)
    return -ADAM_LR * (m_hat / (jnp.sqrt(v_hat) + ADAM_EPS) + ADAM_WD * w), m2, v2


def adamw_shard(w, parts, m, v, name):
    r, c = w.shape
    if r % 256 == 0 or r * c <= 256 * 1024:
        tr, tc = min(r, 256), c
    else:
        tr, tc = r, 256

    def body(w_ref, p_ref, m_ref, v_ref, g_ref, d_ref, nm_ref, nv_ref):
        g = p_ref[0].astype(F32)
        for k in range(1, N_DEV):
            g = g + p_ref[k].astype(F32)
        g_ref[...] = g
        d_ref[...], nm_ref[...], nv_ref[...] = _adamw(w_ref[...], g, m_ref[...], v_ref[...])

    blk = pl.BlockSpec((tr, tc), lambda i, j: (i, j))
    return pl.pallas_call(
        body, grid=(r // tr, c // tc),
        in_specs=[blk, pl.BlockSpec((N_DEV, tr, tc), lambda i, j: (0, i, j)), blk, blk],
        out_specs=[blk] * 4, out_shape=[SDS((r, c), F32)] * 4, name=name,
        compiler_params=_params("parallel", "parallel"))(w, parts, m, v)


def _place():
    return lax.axis_index("x"), lax.axis_index("y"), lax.axis_index("c")


def _flat(px, py, pc):
    return 4 * px + 2 * py + pc


def _peer(x, y, c, k):
    return (1 - x if k & 4 else x, 1 - y if k & 2 else y, 1 - c if k & 1 else c)


def cast_shards(shards):
    def body(*refs):
        n = len(refs) // 2
        for i_ref, o_ref in zip(refs[:n], refs[n:]):
            o_ref[...] = i_ref[...].astype(BF16)

    return pl.pallas_call(body, out_shape=[SDS(s.shape, BF16) for s in shards], name="cast_shards",
                          compiler_params=_params())(*shards)


def _two_level_gather_plan(src_refs, land_refs, send_sems, recv_sems, local_sems):
    n = len(src_refs)
    x, y, c = _place()
    me, sib = (x, y, c), (x, y, 1 - c)
    chips = [(1 - x, y), (x, 1 - y), (1 - x, 1 - y)]

    def copy(a, k, block, to, src=None):
        dst = land_refs[a].at[_flat(*block)]
        return pltpu.make_async_remote_copy(
            src_ref=dst if src is None else src, dst_ref=dst,
            send_sem=send_sems.at[a * N_DEV + k], recv_sem=recv_sems.at[a * N_DEV + k],
            device_id=to, device_id_type=MESH)

    mine = [pltpu.make_async_copy(src_refs[a], land_refs[a].at[_flat(*me)], local_sems.at[a]) for a in range(n)]
    first = []
    for a in range(n):
        first.append(copy(a, 0, me, sib, src=src_refs[a]))
        first += [copy(a, 1 + j, me, (*chip, c), src=src_refs[a]) for j, chip in enumerate(chips)]

    def start():
        for cp in mine + first:
            cp.start()

    def finish():
        passed = []
        for j, chip in enumerate(chips):
            for a in range(n):
                copy(a, 1 + j, (*chip, c), me).wait_recv()
                fwd = copy(a, 4 + j, (*chip, c), sib)
                fwd.start()
                passed.append(fwd)
        for a in range(n):
            copy(a, 0, sib, me).wait_recv()
            for j, chip in enumerate(chips):
                copy(a, 4 + j, (*chip, 1 - c), me).wait_recv()
        for cp in first + passed:
            cp.wait_send()
        for cp in mine:
            cp.wait()

    return start, finish


ALL_DEVICES = tuple(range(N_DEV))


def _exchange_plan(src_refs, land_refs, dests, send_sems, recv_sems, local_sems):
    x, y, c = _place()
    me = _flat(x, y, c)
    plan = []
    for a, (src, land, dl) in enumerate(zip(src_refs, land_refs, dests)):
        for li, j in enumerate(dl):
            to = ((j >> 2) & 1, (j >> 1) & 1, j & 1)
            block = src.at[li] if len(src.shape) == len(land.shape) else src

            def push(slot, a=a, block=block, land=land, j=j, to=to):
                return pltpu.make_async_remote_copy(
                    src_ref=block, dst_ref=land.at[slot], send_sem=send_sems.at[a * N_DEV + j],
                    recv_sem=recv_sems.at[a * N_DEV + slot], device_id=to, device_id_type=MESH)

            own = pltpu.make_async_copy(block, land.at[j], local_sems.at[a])
            plan.append((j, push(me), own, [push(s) for s in range(N_DEV) if s != j]))
    return me, plan


def _exchange_start(me, plan):
    for j, send, own, _ in plan:
        @pl.when(me != j)
        def _(send=send):
            send.start()

        @pl.when(me == j)
        def _(own=own):
            own.start()


def _exchange_wait(me, plan):
    for j, send, own, arrivals in plan:
        @pl.when(me != j)
        def _(send=send):
            send.wait_send()

        @pl.when(me == j)
        def _(own=own, arrivals=arrivals):
            own.wait()
            for arrival in arrivals:
                arrival.wait_recv()


def call_hosting_exchange(core, xch, *, grid, in_specs, out_specs, out_shape, scratch_shapes, name, operands):
    srcs, dests, landing = xch
    n, n_in, n_out, n_scr = len(srcs), len(in_specs), len(out_specs), len(scratch_shapes)

    def body(*refs):
        ins, src_refs = refs[:n_in], refs[n_in:n_in + n]
        outs = refs[n_in + 2 * n:n_in + 2 * n + n_out]
        land_refs = refs[n_in + 2 * n + n_out:n_in + 3 * n + n_out]
        scratch = refs[n_in + 3 * n + n_out:n_in + 3 * n + n_out + n_scr]
        sems = refs[n_in + 3 * n + n_out + n_scr:]
        first = functools.reduce(jnp.logical_and, [pl.program_id(i) == 0 for i in range(len(grid))])
        last = functools.reduce(jnp.logical_and, [pl.program_id(i) == grid[i] - 1 for i in range(len(grid))])
        if dests is None:
            start, finish = _two_level_gather_plan(src_refs, land_refs, *sems)
        else:
            me, plan = _exchange_plan(src_refs, land_refs, dests, *sems)
            start, finish = functools.partial(_exchange_start, me, plan), functools.partial(_exchange_wait, me, plan)
        pl.when(first)(start)
        core(*ins, *outs, *scratch)
        pl.when(last)(finish)

    hbm = pl.BlockSpec(memory_space=pl.ANY)
    res = pl.pallas_call(
        body, grid=grid,
        in_specs=list(in_specs) + [hbm] * (2 * n), out_specs=list(out_specs) + [hbm] * n,
        out_shape=list(out_shape) + [SDS(l.shape, l.dtype) for l in landing],
        scratch_shapes=list(scratch_shapes) + [pltpu.SemaphoreType.DMA((N_DEV * n,)), pltpu.SemaphoreType.DMA((N_DEV * n,)),
                                               pltpu.SemaphoreType.DMA((n,))],
        input_output_aliases={n_in + n + k: n_out + k for k in range(n)},
        name=name, compiler_params=_params(*(("arbitrary",) * len(grid))))(*operands, *srcs, *landing)
    return res[:n_out], res[n_out:]


SLOT_ROWS = 8


def small_allreduce_adamw(loss_sum, grads, ws, ms, vs):
    n = len(grads)
    rows = [g.shape[0] for g in grads]
    total = SLOT_ROWS * (n + 1)

    def body(*refs):
        loss_ref, g_refs, w_refs = refs[0], refs[1:1 + n], refs[1 + n:1 + 2 * n]
        m_refs, v_refs = refs[1 + 2 * n:1 + 3 * n], refs[1 + 3 * n:1 + 4 * n]
        outs = refs[1 + 4 * n:2 + 8 * n]
        vec, gath, tot, send_sems, recv_sems = refs[2 + 8 * n:]
        x, y, c = _place()
        me = _flat(x, y, c)
        vec[...] = jnp.zeros_like(vec)
        vec[0:1, :] = loss_ref[...]
        for i in range(n):
            vec[SLOT_ROWS * (i + 1):SLOT_ROWS * (i + 1) + rows[i], :] = g_refs[i][...]
        gath[me] = vec[...]
        copies = []
        for k in range(1, N_DEV):
            peer = _peer(x, y, c, k)
            copies.append(pltpu.make_async_remote_copy(
                src_ref=vec, dst_ref=gath.at[me], send_sem=send_sems.at[k - 1], recv_sem=recv_sems.at[k - 1],
                device_id=peer, device_id_type=MESH))
        for cp in copies:
            cp.start()
        for cp in copies:
            cp.wait_recv()
        for cp in copies:
            cp.wait_send()
        g = gath[0]
        for j in range(1, N_DEV):
            g = g + gath[j]
        tot[...] = g
        outs[0][...] = tot[0:1, :]
        for i in range(n):
            gi = tot[SLOT_ROWS * (i + 1):SLOT_ROWS * (i + 1) + rows[i], :]
            outs[1 + i][...] = gi
            outs[1 + n + i][...], outs[1 + 2 * n + i][...], outs[1 + 3 * n + i][...] = _adamw(
                w_refs[i][...], gi, m_refs[i][...], v_refs[i][...])

    shapes = [SDS(g.shape, F32) for g in grads]
    return pl.pallas_call(
        body, out_shape=[SDS((1, LANES), F32)] + shapes * 4,
        scratch_shapes=[pltpu.VMEM((total, LANES), F32), pltpu.VMEM((N_DEV, total, LANES), F32), pltpu.VMEM((total, LANES), F32),
                        pltpu.SemaphoreType.DMA((7,)), pltpu.SemaphoreType.DMA((7,))],
        name="small_allreduce_adamw", compiler_params=_params())(loss_sum, *grads, *ws, *ms, *vs)


def _rope_lane_patterns():
    inv = lambda r: ROPE_THETA ** (-(jnp.arange(0, r, 2, dtype=F32) / r))
    z = lambda n: jnp.zeros((n,), F32)
    o = lambda n: jnp.ones((n,), F32)
    half, rest = A_ROT // 2, A_HEAD_DIM - A_ROT
    ia, im = inv(A_ROT), inv(MLA_ROPE)
    mh, tail = MLA_ROPE // 2, LANES - MLA_NOPE - MLA_ROPE
    rows = [jnp.tile(jnp.concatenate([ia, ia, z(rest)]), 2),
            jnp.tile(jnp.concatenate([o(half), z(half + rest)]), 2),
            jnp.tile(jnp.concatenate([z(half), o(half), z(rest)]), 2),
            jnp.concatenate([z(MLA_NOPE), im, im, z(tail)]),
            jnp.concatenate([z(MLA_NOPE), o(mh), z(mh + tail)]),
            jnp.concatenate([z(MLA_NOPE + mh), o(mh), z(tail)]),
            z(LANES), z(LANES)]
    return jnp.stack(rows)


KR_LO, KR_HI = 4480, 4512
W_IN_SHARD = D_IN // N_DEV
BG_SPLIT = 6 * W_IN_SHARD - KR_HI


def w_in_working_t(g):
    pad_lo, pad_hi = MLA_NOPE, LANES - MLA_NOPE - MLA_ROPE
    spans = []
    for lo, hi, shift in ((0, KR_LO, 0), (KR_LO, KR_HI, pad_lo), (KR_HI, D_IN, pad_lo + pad_hi)):
        r = lo
        while r < hi:
            j = r // W_IN_SHARD
            n = min(hi, (j + 1) * W_IN_SHARD) - r
            spans.append((j, r - j * W_IN_SHARD, n, r + shift))
            r += n

    def body(g_ref, o_ref):
        o_ref[KR_LO:KR_LO + pad_lo, :] = jnp.zeros((pad_lo, D_MODEL), o_ref.dtype)
        o_ref[KR_HI + pad_lo:KR_HI + pad_lo + pad_hi, :] = jnp.zeros((pad_hi, D_MODEL), o_ref.dtype)
        for j, src, n, dst in spans:
            o_ref[dst:dst + n, :] = g_ref[j, src:src + n, :]

    return pl.pallas_call(body, out_shape=SDS((D_INW, D_MODEL), g.dtype), name="w_in_working_t", compiler_params=_params())(g)


def _w_in_shard_5(d_ag_tail, d_cc, d_bg_head):
    kr = MLA_Q_RANK + MLA_KV_RANK + MLA_NOPE
    rows = jnp.concatenate([d_ag_tail, d_cc[:MLA_Q_RANK + MLA_KV_RANK], d_cc[kr:kr + MLA_ROPE], d_bg_head], 0)
    return rows.reshape(1, W_IN_SHARD, D_MODEL).astype(BF16)


def _w_uq_working(g):
    w = jnp.pad(g.transpose(1, 0, 2), ((0, 0), (0, 0), (0, LANES - MLA_NOPE - MLA_ROPE)))
    return w.reshape(MLA_Q_RANK, MLA_QW)


def _w_uq_parts(dw):
    return dw.reshape(MLA_Q_RANK, MLA_HEADS, LANES)[:, :, :MLA_NOPE + MLA_ROPE].transpose(1, 0, 2)


def _w_ukv_working(g):
    wk = jnp.pad(g[:, :, :MLA_NOPE].transpose(1, 0, 2), ((0, 0), (0, 0), (0, LANES - MLA_NOPE)))
    wv = g[:, :, MLA_NOPE:].transpose(1, 0, 2)
    return jnp.concatenate([wk.reshape(MLA_KV_RANK, MLA_QW), wv.reshape(MLA_KV_RANK, MLA_WIDTH)], 1)


def _w_ukv_parts(dw):
    dk = dw[:, :MLA_QW].reshape(MLA_KV_RANK, MLA_HEADS, LANES)[:, :, :MLA_NOPE]
    dv = dw[:, MLA_QW:].reshape(MLA_KV_RANK, MLA_HEADS, MLA_V)
    return jnp.concatenate([dk, dv], -1).transpose(1, 0, 2)


SMALL_NAMES = ("g_emb", "b_emb", "g_cq", "g_ckv", "g_out_a", "g_out_b", "g_out_m", "g_post", "b_post")


def kernel(x, mem, positions, g_emb, b_emb, w_in, g_cq, g_ckv, w_uq, w_ukv, w_mem_kv, g_out_a, g_out_b, g_out_m, w_out, g_post, b_post, loss_target, m_g_emb, m_b_emb, m_w_in, m_g_cq, m_g_ckv, m_w_uq, m_w_ukv, m_w_mem_kv, m_g_out_a, m_g_out_b, m_g_out_m, m_w_out, m_g_post, m_b_post, v_g_emb, v_b_emb, v_w_in, v_g_cq, v_g_ckv, v_w_uq, v_w_ukv, v_w_mem_kv, v_g_out_a, v_g_out_b, v_g_out_m, v_w_out, v_g_post, v_b_post):
    nb = x.shape[0]
    t = nb * SEQ
    x2 = x.reshape(t, D_MODEL)
    tgt2 = loss_target.reshape(t, D_MODEL)
    mem2 = mem.reshape(nb * N_MEM, D_MODEL)
    g_emb2, b_emb2 = g_emb.reshape(1, -1), b_emb.reshape(1, -1)

    w_in_t, m_w_in_t, v_w_in_t = w_in[0].T, m_w_in[0].T, v_w_in[0].T
    s_in, s_uq, s_ukv, s_mem, s_out = cast_shards((w_in_t, w_uq[0], w_ukv[0], w_mem_kv[0], w_out[0]))
    (h32, h16, (a_c, a_sa, a_sb), (m_c, m_sa, m_sb)), (g_in,) = embed_fwd(
        x2, g_emb2, b_emb2, positions, ((s_in,), None, (lax.empty((N_DEV,) + s_in.shape, BF16),)))
    win_t = w_in_working_t(g_in)

    proj = mm_nn(h16, win_t, F32, 1024, 1536, "proj", rhs_transposed=True)
    later = (s_uq, s_ukv, s_mem, s_out)
    (ya, lse_a), qkv_d, (g_uq, g_ukv, g_mem, g_out) = a_attn_fwd(
        proj, a_c, a_sa, a_sb, nb,
        (later, (ALL_DEVICES,) * len(later), tuple(lax.empty((N_DEV,) + w.shape, BF16) for w in later)))
    wuq_w = _w_uq_working(g_uq)
    wkv_w = _w_ukv_working(g_ukv)
    wmem = g_mem.reshape(D_MODEL, 2 * MEM_WIDTH)
    wout = g_out.reshape(D_MIX, D_MODEL)
    qb, kb, vb = mla_prep_fwd(proj, m_c, m_sa, m_sb, g_cq, g_ckv, wuq_w, wkv_w)
    yb, lse_b = mla_attn_fwd(qb, kb, vb, nb)
    mkv = mm_nn(mem2, wmem, BF16, nb * N_MEM, 512, "mem_kv")
    ym = mem_attn_fwd(proj, mkv, nb)
    z, du32, du16, loss_sum, dg_post, db_post = gate_out_ln_loss(
        ya, yb, ym, proj, g_out_a, g_out_b, g_out_m, wout, h32, tgt2, g_post, b_post)

    dya, dyb, dym, dag, dbg, dmg, dg_out_a, dg_out_b, dg_out_m = gate_bwd(
        du16, wout, ya, yb, ym, proj, g_out_a, g_out_b, g_out_m)
    dw_out = mm_tn(z, du16, 1024, "dw_out")
    dmq, dmk, dmv = mem_attn_bwd(proj, mkv, dym, nb)
    dw_mem = mm_tn(mem2, jnp.concatenate([dmk, dmv], 1), nb * N_MEM, "dw_mem")
    d_gates, shards_6_7 = mm_tn_group((dbg, dmq, dmg), h16, 2048, "dw_in_bg_mq_mg", W_IN_SHARD, BG_SPLIT, 2)
    landing = lambda w, dtype=F32: lax.empty((N_DEV,) + w.shape, dtype)
    big_w = (w_in_t, w_uq[0], w_ukv[0], w_mem_kv[0], w_out[0])
    (daq, dak, dav), (p_out, p_mem, p_in) = a_attn_bwd(
        qkv_d, a_c, a_sa, a_sb, dya, ya, lse_a, nb,
        ((dw_out.reshape(N_DEV, D_MIX // N_DEV, D_MODEL), dw_mem.reshape(N_DEV, D_MODEL // N_DEV, 2 * MEM_WIDTH),
          shards_6_7),
         (ALL_DEVICES, ALL_DEVICES, (6, 7)),
         (landing(w_out[0]), landing(w_mem_kv[0]), landing(w_in_t, BF16))))
    d_a, shards_0_4 = mm_tn_group((daq, dak, dav, dag), h16, 1024, "dw_in_aq_ak_av_ag", W_IN_SHARD, 0, 5)
    (dqb, dkb, dvb), (p_in,) = mla_attn_bwd(
        qb, kb, vb, dyb, yb, lse_b, nb, ((shards_0_4,), ((0, 1, 2, 3, 4),), (p_in,)))
    dcc, dqf, cqn, dkvf, ckvn, dg_cq, dg_ckv = mla_prep_bwd(proj, m_c, m_sa, m_sb, g_cq, g_ckv, wuq_w, wkv_w, dqb, dkb, dvb)
    dw_uq = mm_tn(cqn, dqf, 2048, "dw_uq")
    dw_ukv = mm_tn(ckvn, dkvf, 2048, "dw_ukv")
    d_cc = mm_tn(dcc, h16, 2048, "dw_in_cc")
    pieces = (daq, dak, dav, dag, dcc, dbg, dmq, dmg)
    (grad_x, dg_emb, db_emb), (p_in, p_uq, p_ukv) = dh_ln_bwd(
        pieces, win_t, du32, x2, g_emb2,
        ((_w_in_shard_5(d_a[5 * W_IN_SHARD:], d_cc, d_gates[:BG_SPLIT]), _w_uq_parts(dw_uq), _w_ukv_parts(dw_ukv)),
         ((5,), ALL_DEVICES, ALL_DEVICES),
         (p_in, landing(w_uq[0]), landing(w_ukv[0]))))

    parts = (p_in, p_uq, p_ukv, p_mem, p_out)
    big_m = (m_w_in_t, m_w_uq[0], m_w_ukv[0], m_w_mem_kv[0], m_w_out[0])
    big_v = (v_w_in_t, v_w_uq[0], v_w_ukv[0], v_w_mem_kv[0], v_w_out[0])
    big = {}
    for name, w, p, m, v in zip(("w_in", "w_uq", "w_ukv", "w_mem_kv", "w_out"), big_w, parts, big_m, big_v):
        res = adamw_shard(w, p, m, v, "adamw_" + name)
        big[name] = [(o.T if name == "w_in" else o)[None] for o in res]

    small_w = (g_emb, b_emb, g_cq, g_ckv, g_out_a, g_out_b, g_out_m, g_post, b_post)
    small_m = (m_g_emb, m_b_emb, m_g_cq, m_g_ckv, m_g_out_a, m_g_out_b, m_g_out_m, m_g_post, m_b_post)
    small_v = (v_g_emb, v_b_emb, v_g_cq, v_g_ckv, v_g_out_a, v_g_out_b, v_g_out_m, v_g_post, v_b_post)
    small_g = (dg_emb, db_emb, dg_cq, dg_ckv, dg_out_a, dg_out_b, dg_out_m, dg_post, db_post)
    rows128 = lambda vals: [v.reshape(-1, LANES) for v in vals]
    res = small_allreduce_adamw(loss_sum, rows128(small_g), rows128(small_w), rows128(small_m), rows128(small_v))
    loss = res[0][0, 0]
    n_small = len(small_w)
    sg, sd, sm, sv = [[r.reshape(w.shape) for r, w in zip(res[1 + k * n_small:1 + (k + 1) * n_small], small_w)]
                      for k in range(4)]

    order = ("g_emb", "b_emb", "w_in", "g_cq", "g_ckv", "w_uq", "w_ukv", "w_mem_kv", "g_out_a", "g_out_b", "g_out_m",
             "w_out", "g_post", "b_post")
    small_idx = {n: i for i, n in enumerate(SMALL_NAMES)}
    outs = [loss, grad_x.reshape(x.shape)]
    for kind in range(4):
        for name in order:
            outs.append(big[name][kind] if name in big else (sg, sd, sm, sv)[kind][small_idx[name]])
    return tuple(outs)
```

```python
import functools

import jax
import jax.numpy as jnp
from jax import lax
from jax.experimental import pallas as pl
from jax.experimental.pallas import tpu as pltpu

F32 = jnp.float32
BF16 = jnp.bfloat16
SDS = jax.ShapeDtypeStruct
MESH = pl.DeviceIdType.MESH

D_MODEL = 1024
SEQ = 2048
A_HEADS, A_HEAD_DIM, A_ROT = 16, 64, 16
A_WIDTH = 1024
DILATIONS = (1, 4, 16)
N_SIDE = 64
MLA_HEADS, MLA_Q_RANK, MLA_KV_RANK = 8, 256, 128
MLA_NOPE, MLA_ROPE, MLA_V = 64, 32, 64
MLA_WIDTH = 512
N_MEM, MEM_HEADS, MEM_HEAD_DIM, MEM_WIDTH = 256, 4, 128, 512
ROPE_THETA = 500000.0
NORM_EPS = 1e-5
NEG_INF = -1e30
ALPHA = 2.0 ** 0.25
D_IN = 6048
N_DEV = 8

ADAM_LR, ADAM_B1, ADAM_B2, ADAM_EPS, ADAM_WD, ADAM_STEP = 0.001, 0.9, 0.999, 1e-08, 0.01, 10

D_INW = 6144
PIECE_WIDTHS = (1024, 1024, 1024, 1024, 512, 512, 512, 512)
PIECE_OFFS = (0, 1024, 2048, 3072, 4096, 4608, 5120, 5632)
LANES = 128
VMEM_LIMIT = 56 * 1024 * 1024


def _params(*sem):
    kw = dict(vmem_limit_bytes=VMEM_LIMIT)
    if sem:
        kw["dimension_semantics"] = sem
    return pltpu.CompilerParams(**kw)


def _dot(a, b):
    return jnp.dot(a, b, preferred_element_type=F32)


def _dot_nt(a, b):
    return lax.dot_general(a, b, (((1,), (1,)), ((), ())), preferred_element_type=F32)


def _dot_tn(a, b):
    return lax.dot_general(a, b, (((0,), (0,)), ((), ())), preferred_element_type=F32)


def _sigmoid(x):
    return 1.0 / (1.0 + jnp.exp(-x))


def _rope_fwd(x, c, sa, sb, half):
    n = x.shape[-1]
    return x * c + pltpu.roll(x, n - half, 1) * sa + pltpu.roll(x, half, 1) * sb


def _rope_bwd(dy, c, sa, sb, half):
    n = dy.shape[-1]
    return dy * c + pltpu.roll(dy * sa, half, 1) + pltpu.roll(dy * sb, n - half, 1)


def mm_nn(a, b, out_dtype, tm, tn, name, rhs_transposed=False):
    m, k = a.shape
    n = b.shape[0] if rhs_transposed else b.shape[1]
    dot = _dot_nt if rhs_transposed else _dot

    def body(a_ref, b_ref, o_ref):
        o_ref[...] = dot(a_ref[...].astype(BF16), b_ref[...].astype(BF16)).astype(o_ref.dtype)

    b_spec = pl.BlockSpec((tn, k), lambda j, i: (j, 0)) if rhs_transposed else pl.BlockSpec((k, tn), lambda j, i: (0, j))
    return pl.pallas_call(
        body, grid=(n // tn, m // tm),
        in_specs=[pl.BlockSpec((tm, k), lambda j, i: (i, 0)), b_spec],
        out_specs=pl.BlockSpec((tm, tn), lambda j, i: (i, j)),
        out_shape=SDS((m, n), out_dtype), name=name,
        compiler_params=_params("parallel", "parallel"))(a, b)


def mm_tn(a, b, tt, name):
    t, m = a.shape
    n = b.shape[1]

    def body(a_ref, b_ref, o_ref):
        @pl.when(pl.program_id(0) == 0)
        def _():
            o_ref[...] = jnp.zeros_like(o_ref)

        o_ref[...] += _dot_tn(a_ref[...].astype(BF16), b_ref[...].astype(BF16))

    return pl.pallas_call(
        body, grid=(t // tt,),
        in_specs=[pl.BlockSpec((tt, m), lambda i: (i, 0)), pl.BlockSpec((tt, n), lambda i: (i, 0))],
        out_specs=pl.BlockSpec((m, n), lambda i: (0, 0)),
        out_shape=SDS((m, n), F32), name=name,
        compiler_params=_params("arbitrary"))(a, b)


def mm_tn_group(pieces, b, tt, name, slab_rows, first_slab_row, n_slabs):
    n, (t, w), cols = len(pieces), pieces[0].shape, b.shape[1]
    nt = t // tt

    def body(*refs):
        p_refs, b_ref, o_ref, slab_ref = refs[:n], refs[n], refs[n + 1], refs[n + 2]

        @pl.when(pl.program_id(1) == 0)
        def _():
            o_ref[...] = jnp.zeros_like(o_ref)

        for k in range(n):
            @pl.when(pl.program_id(0) == k)
            def _(k=k):
                o_ref[...] += _dot_tn(p_refs[k][...], b_ref[...])

            @pl.when((pl.program_id(0) == k) & (pl.program_id(1) == nt - 1))
            def _(k=k):
                for j in range(n_slabs):
                    lo = max(k * w, first_slab_row + j * slab_rows)
                    hi = min((k + 1) * w, first_slab_row + (j + 1) * slab_rows)
                    if lo < hi:
                        dst = lo - first_slab_row - j * slab_rows
                        slab_ref[j, dst:dst + hi - lo, :] = o_ref[lo - k * w:hi - k * w, :].astype(slab_ref.dtype)

    def piece_spec(k):
        return pl.BlockSpec((tt, w), lambda p, i: (jnp.where(p < k, 0, jnp.where(p > k, nt - 1, i)), 0))

    return pl.pallas_call(
        body, grid=(n, nt),
        in_specs=[piece_spec(k) for k in range(n)] + [pl.BlockSpec((tt, cols), lambda p, i: (i, 0))],
        out_specs=[pl.BlockSpec((w, cols), lambda p, i: (p, 0)),
                   pl.BlockSpec((n_slabs, slab_rows, cols), lambda p, i: (0, 0, 0))],
        out_shape=[SDS((n * w, cols), F32), SDS((n_slabs, slab_rows, cols), BF16)], name=name,
        compiler_params=_params("arbitrary", "arbitrary"))(*pieces, b)


def embed_fwd(x2, g, b, positions, xch):
    t, d = x2.shape
    tm = 512
    pos = positions.astype(F32).reshape(-1, 1)

    def body(x_ref, g_ref, b_ref, pos_ref, pat_ref, h32_ref, h16_ref, *tabs):
        x = x_ref[...]
        mu = jnp.mean(x, axis=-1, keepdims=True)
        xc = x - mu
        var = jnp.mean(xc * xc, axis=-1, keepdims=True)
        h = xc * lax.rsqrt(var + NORM_EPS) * g_ref[...] + b_ref[...]
        h32_ref[...] = h
        h16_ref[...] = h.astype(BF16)
        p = pos_ref[...]
        for k in range(2):
            inv, first, second = pat_ref[3 * k:3 * k + 1, :], pat_ref[3 * k + 1:3 * k + 2, :], pat_ref[3 * k + 2:3 * k + 3, :]
            ang = p * inv
            sn = jnp.sin(ang)
            tabs[3 * k][...] = jnp.where(first + second > 0.0, jnp.cos(ang), 1.0)
            tabs[3 * k + 1][...] = -first * sn
            tabs[3 * k + 2][...] = second * sn

    row = pl.BlockSpec((tm, d), lambda i: (i, 0))
    vec = pl.BlockSpec((1, d), lambda i: (0, 0))
    tab = pl.BlockSpec((tm, LANES), lambda i: (i, 0))
    res, landed = call_hosting_exchange(
        body, xch, grid=(t // tm,),
        in_specs=[row, vec, vec, pl.BlockSpec((tm, 1), lambda i: (i, 0)), pl.BlockSpec((8, LANES), lambda i: (0, 0))],
        out_specs=[row, row] + [tab] * 6,
        out_shape=[SDS((t, d), F32), SDS((t, d), BF16)] + [SDS((t, LANES), F32)] * 6,
        scratch_shapes=[], name="embed_fwd", operands=(x2, g, b, pos, _rope_lane_patterns()))
    return (res[0], res[1], tuple(res[2:5]), tuple(res[5:8])), landed


Q_BLK = 128
UNROLL_FWD = 16
UNROLL_BWD = 16


def _pattern_geometry(d):
    length = SEQ // d
    nblk = length // Q_BLK
    kwin = min(2 * Q_BLK, length)
    return length, nblk, kwin


def _block_coords(idx, d):
    length, nblk, kwin = _pattern_geometry(d)
    r = lax.shift_right_logical(idx, nblk.bit_length() - 1)
    i = idx & (nblk - 1)
    q0 = pl.multiple_of(r * length + i * Q_BLK, Q_BLK)
    ks = jnp.clip(i * Q_BLK - N_SIDE, 0, length - kwin)
    k0 = pl.multiple_of(r * length + ks, N_SIDE)
    qpos = i * Q_BLK + lax.broadcasted_iota(jnp.int32, (Q_BLK, kwin), 0)
    kpos = ks + lax.broadcasted_iota(jnp.int32, (Q_BLK, kwin), 1)
    valid = jnp.abs(kpos - qpos) <= N_SIDE
    return q0, k0, kwin, valid


def _deinterleave(src_ref, dst_ref, d, dtype, tmp_ref):
    if d == 1:
        dst_ref[...] = src_ref[...].astype(dtype)
        return
    q = SEQ // 4
    if d == 4:
        for r in range(4):
            dst_ref[r * q:(r + 1) * q, :] = src_ref[pl.ds(r, q, stride=4), :].astype(dtype)
        return
    assert d == 16
    n = SEQ // 16
    for r in range(4):
        tmp_ref[r * q:(r + 1) * q, :] = src_ref[pl.ds(r, q, stride=4), :]
    for r in range(4):
        for j in range(4):
            dst_ref[(r + 4 * j) * n:(r + 4 * j + 1) * n, :] = tmp_ref[pl.ds(r * q + j, n, stride=4), :].astype(dtype)


def _class16_to_class4(src_ref, dst_ref):
    q, n = SEQ // 4, SEQ // 16
    for r in range(4):
        for j in range(4):
            dst_ref[pl.ds(r * q + j, n, stride=4), :] = src_ref[(r + 4 * j) * n:(r + 4 * j + 1) * n, :]


def _interleave(src_ref, dst_ref, d, tmp_ref, accumulate):
    q = SEQ // 4
    if d == 16:
        _class16_to_class4(src_ref, tmp_ref)
        src_ref = tmp_ref
    else:
        assert d == 4
    for r in range(4):
        rows = pl.ds(r, q, stride=4)
        val = src_ref[r * q:(r + 1) * q, :]
        dst_ref[rows, :] = dst_ref[rows, :] + val if accumulate else val


def a_attn_fwd(proj, ca, sa, sb, nb, xch):
    t = proj.shape[0]
    n_pairs = A_WIDTH // LANES

    def body(q_ref, k_ref, v_ref, c_ref, sa_ref, sb_ref, y_ref, lse_ref, *rest):
        qkv_d, (qr_s, kr_s, oc_s, lc_s, o1_s, l1_s, o2_s, l2_s, o3_s, l3_s, tmp_s) = rest[:9], rest[9:]
        c, s_a, s_b = c_ref[...], sa_ref[...], sb_ref[...]
        qr_s[...] = _rope_fwd(q_ref[...], c, s_a, s_b, A_ROT // 2) * (A_HEAD_DIM ** -0.5)
        kr_s[...] = _rope_fwd(k_ref[...], c, s_a, s_b, A_ROT // 2)
        head0 = lax.broadcasted_iota(jnp.int32, (Q_BLK, LANES), 1) < A_HEAD_DIM
        nat = ((o1_s, l1_s), (o2_s, l2_s), (o3_s, l3_s))

        for g, d in enumerate(DILATIONS):
            qd_s, kd_s, vd_s = qkv_d[3 * g:3 * g + 3]
            _deinterleave(qr_s, qd_s, d, BF16, tmp_s)
            _deinterleave(kr_s, kd_s, d, BF16, tmp_s)
            _deinterleave(v_ref, vd_s, d, BF16, tmp_s)
            o_dst, l_dst = (nat[g] if d == 1 else (oc_s, lc_s))

            def block(idx, carry, d=d, o_dst=o_dst, l_dst=l_dst, qd_s=qd_s, kd_s=kd_s, vd_s=vd_s):
                q0, k0, kwin, valid = _block_coords(idx, d)
                qb = qd_s[pl.ds(q0, Q_BLK), :]
                kb = kd_s[pl.ds(k0, kwin), :]
                vb = vd_s[pl.ds(k0, kwin), :]
                zero = jnp.zeros_like(qb)
                q2 = jnp.concatenate([jnp.where(head0, qb, zero), jnp.where(head0, zero, qb)], 0)
                s = jnp.where(jnp.concatenate([valid, valid], 0), _dot_nt(q2, kb), NEG_INF)
                m = jnp.max(s, axis=-1, keepdims=True)
                p = jnp.exp(s - m)
                l = jnp.sum(p, axis=-1, keepdims=True)
                o2 = _dot(p.astype(BF16), vb) / l
                l2 = m + jnp.log(l)
                o_dst[pl.ds(q0, Q_BLK), :] = jnp.where(head0, o2[:Q_BLK], o2[Q_BLK:])
                l_dst[pl.ds(q0, Q_BLK), :] = jnp.where(head0, l2[:Q_BLK], l2[Q_BLK:])
                return carry

            lax.fori_loop(0, SEQ // Q_BLK, block, 0, unroll=UNROLL_FWD)
            if d > 1:
                _interleave(oc_s, nat[g][0], d, tmp_s, False)
                _interleave(lc_s, nat[g][1], d, tmp_s, False)

        def merge(ci, carry):
            rows = pl.ds(pl.multiple_of(ci * 256, 256), 256)
            l1, l2, l3 = l1_s[rows, :], l2_s[rows, :], l3_s[rows, :]
            m = jnp.maximum(jnp.maximum(l1, l2), l3)
            w1, w2, w3 = jnp.exp(l1 - m), jnp.exp(l2 - m), jnp.exp(l3 - m)
            w = w1 + w2 + w3
            y_ref[rows, :] = (w1 * o1_s[rows, :] + w2 * o2_s[rows, :] + w3 * o3_s[rows, :]) / w
            lse_ref[rows, :] = m + jnp.log(w)
            return carry

        lax.fori_loop(0, SEQ // 256, merge, 0)

    def col(off):
        return pl.BlockSpec((SEQ, LANES), lambda b, hp: (b, off + hp))

    tab = pl.BlockSpec((SEQ, LANES), lambda b, hp: (b, 0))
    out = pl.BlockSpec((SEQ, LANES), lambda b, hp: (b, hp))
    f32s = pltpu.VMEM((SEQ, LANES), F32)
    res, landed = call_hosting_exchange(
        body, xch, grid=(nb, n_pairs),
        in_specs=[col(0), col(n_pairs), col(2 * n_pairs), tab, tab, tab],
        out_specs=[out] * 11,
        out_shape=[SDS((t, A_WIDTH), F32)] * 2 + [SDS((t, A_WIDTH), BF16)] * 9,
        scratch_shapes=[f32s] * 11,
        name="a_attn_fwd", operands=(proj, proj, proj, ca, sa, sb))
    return res[:2], res[2:], landed


def a_attn_bwd(qkv_d, ca, sa, sb, dy, y, lse, nb, xch):
    t = dy.shape[0]
    n_pairs = A_WIDTH // LANES

    def body(*refs):
        qkv_refs = refs[:9]
        (c_ref, sa_ref, sb_ref, do_ref, y_ref, lse_ref, dq_ref, dk_ref, dv_ref,
         l0n_s, l1n_s, d0n_s, d1n_s, dod_s, l0d_s, l1d_s, d0d_s, d1d_s,
         dqc_s, dkc_s, dvc_s, dq4_s, dk4_s, dv4_s, dqn_s, dkn_s, dvn_s, tmp_s) = refs[9:]
        c, s_a, s_b = c_ref[...], sa_ref[...], sb_ref[...]
        head0 = lax.broadcasted_iota(jnp.int32, (Q_BLK, LANES), 1) < A_HEAD_DIM

        def per_head_rows(ci, carry):
            rows = pl.ds(pl.multiple_of(ci * 256, 256), 256)
            h0 = lax.broadcasted_iota(jnp.int32, (256, LANES), 1) < A_HEAD_DIM
            tt = do_ref[rows, :] * y_ref[rows, :]
            d0n_s[rows, :] = jnp.broadcast_to(jnp.sum(jnp.where(h0, tt, 0.0), axis=-1, keepdims=True), (256, LANES))
            d1n_s[rows, :] = jnp.broadcast_to(jnp.sum(jnp.where(h0, 0.0, tt), axis=-1, keepdims=True), (256, LANES))
            l = lse_ref[rows, :]
            lr = pltpu.roll(l, A_HEAD_DIM, 1)
            l0n_s[rows, :] = jnp.where(h0, l, lr)
            l1n_s[rows, :] = jnp.where(h0, lr, l)
            return carry

        lax.fori_loop(0, SEQ // 256, per_head_rows, 0)
        assert DILATIONS == (1, 4, 16)

        for g, d in enumerate(DILATIONS):
            qd_s, kd_s, vd_s = qkv_refs[3 * g:3 * g + 3]
            _deinterleave(do_ref, dod_s, d, BF16, tmp_s)
            if d > 1:
                for src, dst in ((l0n_s, l0d_s), (l1n_s, l1d_s), (d0n_s, d0d_s), (d1n_s, d1d_s)):
                    _deinterleave(src, dst, d, F32, tmp_s)
            l0, l1, d0, d1 = (l0n_s, l1n_s, d0n_s, d1n_s) if d == 1 else (l0d_s, l1d_s, d0d_s, d1d_s)
            dq_dst, dk_dst, dv_dst = {1: (dqn_s, dkn_s, dvn_s), 4: (dq4_s, dk4_s, dv4_s), 16: (dqc_s, dkc_s, dvc_s)}[d]
            dk_dst[...] = jnp.zeros_like(dk_dst)
            dv_dst[...] = jnp.zeros_like(dv_dst)

            def block(idx, carry, d=d, l0=l0, l1=l1, d0=d0, d1=d1, dq_dst=dq_dst, dk_dst=dk_dst, dv_dst=dv_dst,
                      qd_s=qd_s, kd_s=kd_s, vd_s=vd_s):
                q0, k0, kwin, valid = _block_coords(idx, d)
                qrows = pl.ds(q0, Q_BLK)
                krows = pl.ds(k0, kwin)
                qb, dob = qd_s[qrows, :], dod_s[qrows, :]
                kb, vb = kd_s[krows, :], vd_s[krows, :]
                zero = jnp.zeros_like(qb)
                q2 = jnp.concatenate([jnp.where(head0, qb, zero), jnp.where(head0, zero, qb)], 0)
                do2 = jnp.concatenate([jnp.where(head0, dob, zero), jnp.where(head0, zero, dob)], 0)
                wide = lambda x: jnp.concatenate([x] * (kwin // LANES), 1)
                lse2 = wide(jnp.concatenate([l0[qrows, :], l1[qrows, :]], 0))
                dd2 = wide(jnp.concatenate([d0[qrows, :], d1[qrows, :]], 0))
                s = jnp.where(jnp.concatenate([valid, valid], 0), _dot_nt(q2, kb), NEG_INF)
                p = jnp.exp(s - lse2)
                ds = (p * (_dot_nt(do2, vb) - dd2)).astype(BF16)
                dq2 = _dot(ds, kb)
                dq_dst[qrows, :] = jnp.where(head0, dq2[:Q_BLK], dq2[Q_BLK:])
                dk_dst[krows, :] += _dot_tn(ds, q2)
                dv_dst[krows, :] += _dot_tn(p.astype(BF16), do2)
                return carry

            lax.fori_loop(0, SEQ // Q_BLK, block, 0, unroll=UNROLL_BWD)

        for c16, c4, nat in ((dqc_s, dq4_s, dqn_s), (dkc_s, dk4_s, dkn_s), (dvc_s, dv4_s, dvn_s)):
            _class16_to_class4(c16, tmp_s)
            c4[...] = c4[...] + tmp_s[...]
            _interleave(c4, nat, 4, tmp_s, True)

        dq_ref[...] = _rope_bwd(dqn_s[...] * (A_HEAD_DIM ** -0.5), c, s_a, s_b, A_ROT // 2).astype(BF16)
        dk_ref[...] = _rope_bwd(dkn_s[...], c, s_a, s_b, A_ROT // 2).astype(BF16)
        dv_ref[...] = dvn_s[...].astype(BF16)

    tab = pl.BlockSpec((SEQ, LANES), lambda b, hp: (b, 0))
    blk = pl.BlockSpec((SEQ, LANES), lambda b, hp: (b, hp))
    f32s = pltpu.VMEM((SEQ, LANES), F32)
    b16s = pltpu.VMEM((SEQ, LANES), BF16)
    return call_hosting_exchange(
        body, xch, grid=(nb, n_pairs),
        in_specs=[blk] * 9 + [tab, tab, tab, blk, blk, blk],
        out_specs=[blk, blk, blk],
        out_shape=[SDS((t, A_WIDTH), BF16)] * 3,
        scratch_shapes=[f32s] * 4 + [b16s] + [f32s] * 14,
        name="a_attn_bwd", operands=(*qkv_d, ca, sa, sb, dy, y, lse))


MLA_SCALE = (MLA_NOPE + MLA_ROPE) ** -0.5
LOG2E = 1.4426950408889634
MLA_QW = MLA_HEADS * LANES
MLA_KVW = MLA_QW + MLA_WIDTH


def _rms(x, g):
    r = lax.rsqrt(jnp.mean(x * x, axis=-1, keepdims=True) + NORM_EPS)
    return x * r * g, r


def _rms_bwd(dn, x, r, g):
    tg = dn * g
    dx = r * tg - x * (r * r * r) * jnp.mean(tg * x, axis=-1, keepdims=True)
    return dx, jnp.sum(dn * x * r, axis=0, keepdims=True)


def mla_prep_fwd(proj, cm, sma, smb, g_cq, g_ckv, wuq, wkv):
    t = proj.shape[0]
    tm = 1024

    def body(cq_ref, ckv_ref, kr_ref, c_ref, sa_ref, sb_ref, gq_ref, gkv_ref, wuq_ref, wkv_ref, q_ref, k_ref, v_ref):
        c, s_a, s_b = c_ref[...], sa_ref[...], sb_ref[...]
        cqn, _ = _rms(cq_ref[...], gq_ref[...])
        qf = _dot(cqn.astype(BF16), wuq_ref[...])
        ckvn, _ = _rms(ckv_ref[...], gkv_ref[...])
        kvf = _dot(ckvn.astype(BF16), wkv_ref[...])
        krope = _rope_fwd(kr_ref[...], c, s_a, s_b, MLA_ROPE // 2)
        for h in range(MLA_HEADS):
            cols = slice(h * LANES, (h + 1) * LANES)
            q_ref[:, cols] = (_rope_fwd(qf[:, cols], c, s_a, s_b, MLA_ROPE // 2) * (MLA_SCALE * LOG2E)).astype(BF16)
            k_ref[:, cols] = (kvf[:, cols] + krope).astype(BF16)
        v_ref[...] = kvf[:, MLA_QW:].astype(BF16)

    def row(w, j):
        return pl.BlockSpec((tm, w), lambda i: (i, j))

    def full(a):
        return pl.BlockSpec(a.shape, lambda i: (0, 0))

    return pl.pallas_call(
        body, grid=(t // tm,),
        in_specs=[row(256, 4096 // 256), row(128, 4352 // 128), row(128, 4480 // 128), row(128, 0), row(128, 0), row(128, 0),
                  full(g_cq), full(g_ckv), full(wuq), full(wkv)],
        out_specs=[row(MLA_QW, 0), row(MLA_QW, 0), row(MLA_WIDTH, 0)],
        out_shape=[SDS((t, MLA_QW), BF16), SDS((t, MLA_QW), BF16), SDS((t, MLA_WIDTH), BF16)],
        name="mla_prep_fwd", compiler_params=_params("parallel"))(proj, proj, proj, cm, sma, smb, g_cq, g_ckv, wuq, wkv)


def mla_prep_bwd(proj, cm, sma, smb, g_cq, g_ckv, wuq, wkv, dq, dk, dv):
    t = proj.shape[0]
    tm = 1024

    def body(cq_ref, ckv_ref, c_ref, sa_ref, sb_ref, gq_ref, gkv_ref, wuq_ref, wkv_ref, dq_ref, dk_ref, dv_ref,
             dcc_ref, dqf_ref, cqn_ref, dkvf_ref, ckvn_ref, dgq_ref, dgkv_ref):
        @pl.when(pl.program_id(0) == 0)
        def _():
            dgq_ref[...] = jnp.zeros_like(dgq_ref)
            dgkv_ref[...] = jnp.zeros_like(dgkv_ref)

        c, s_a, s_b = c_ref[...], sa_ref[...], sb_ref[...]
        cq, ckv = cq_ref[...], ckv_ref[...]
        cqn, rq = _rms(cq, gq_ref[...])
        ckvn, rkv = _rms(ckv, gkv_ref[...])
        cqn_ref[...] = cqn.astype(BF16)
        ckvn_ref[...] = ckvn.astype(BF16)
        lane = lax.broadcasted_iota(jnp.int32, (tm, LANES), 1)
        rope_lanes = (lane >= MLA_NOPE) & (lane < MLA_NOPE + MLA_ROPE)
        dkrope = jnp.zeros((tm, LANES), F32)
        for h in range(MLA_HEADS):
            cols = slice(h * LANES, (h + 1) * LANES)
            dqf_ref[:, cols] = _rope_bwd(dq_ref[:, cols].astype(F32) * MLA_SCALE, c, s_a, s_b, MLA_ROPE // 2).astype(BF16)
            dkh = dk_ref[:, cols].astype(F32) * (1.0 / LOG2E)
            dkvf_ref[:, cols] = dkh.astype(BF16)
            dkrope = dkrope + dkh
        dkvf_ref[:, MLA_QW:] = dv_ref[...].astype(BF16)
        dkr = _rope_bwd(jnp.where(rope_lanes, dkrope, 0.0), c, s_a, s_b, MLA_ROPE // 2)
        dcqn = _dot_nt(dqf_ref[...], wuq_ref[...])
        dckvn = _dot_nt(dkvf_ref[...], wkv_ref[...])
        dcq, dgq = _rms_bwd(dcqn, cq, rq, gq_ref[...])
        dckv, dgkv = _rms_bwd(dckvn, ckv, rkv, gkv_ref[...])
        dgq_ref[...] += dgq
        dgkv_ref[...] += dgkv
        dcc_ref[:, 0:256] = dcq.astype(BF16)
        dcc_ref[:, 256:384] = dckv.astype(BF16)
        dcc_ref[:, 384:512] = dkr.astype(BF16)

    def row(w, j):
        return pl.BlockSpec((tm, w), lambda i: (i, j))

    def full(a):
        return pl.BlockSpec(a.shape, lambda i: (0, 0))

    return pl.pallas_call(
        body, grid=(t // tm,),
        in_specs=[row(256, 4096 // 256), row(128, 4352 // 128), row(128, 0), row(128, 0), row(128, 0),
                  full(g_cq), full(g_ckv), full(wuq), full(wkv), row(MLA_QW, 0), row(MLA_QW, 0), row(MLA_WIDTH, 0)],
        out_specs=[row(512, 0), row(MLA_QW, 0), row(256, 0), row(MLA_KVW, 0), row(128, 0), full(g_cq), full(g_ckv)],
        out_shape=[SDS((t, 512), BF16), SDS((t, MLA_QW), BF16), SDS((t, 256), BF16), SDS((t, MLA_KVW), BF16),
                   SDS((t, 128), BF16), SDS(g_cq.shape, F32), SDS(g_ckv.shape, F32)],
        name="mla_prep_bwd", compiler_params=_params("arbitrary"))(proj, proj, cm, sma, smb, g_cq, g_ckv, wuq, wkv, dq, dk, dv)


MLA_TQ = SEQ
MLA_SUB = 256


def mla_attn_fwd(qb, kb, vb, nb):
    t = qb.shape[0]
    nq = SEQ // MLA_TQ
    n_pairs = MLA_HEADS // 2

    def body(q_ref, k_ref, v_ref, y_ref, lse_ref):
        head0 = lax.broadcasted_iota(jnp.int32, (MLA_SUB, LANES), 1) < MLA_V
        v = v_ref[...]
        vhead0 = lax.broadcasted_iota(jnp.int32, v.shape, 1) < MLA_V
        one = jnp.ones_like(v)
        vh = [jnp.where(vhead0 == (h == 0), v, one) for h in range(2)]
        for sub in range(MLA_TQ // MLA_SUB):
            rows = slice(sub * MLA_SUB, (sub + 1) * MLA_SUB)
            outs, lses = [], []
            for h in range(2):
                cols = slice(h * LANES, (h + 1) * LANES)
                s = _dot_nt(q_ref[rows, cols], k_ref[:, cols])
                m = jnp.max(s, axis=-1, keepdims=True)
                p = jnp.exp2(s - m).astype(BF16)
                ol = _dot(p, vh[h])
                l = pltpu.roll(ol, MLA_V, 1)
                outs.append(ol / l)
                lses.append(m + jnp.log2(l))
            y_ref[rows, :] = jnp.where(head0, outs[0], outs[1])
            lse_ref[rows, :] = jnp.where(head0, lses[0], lses[1])

    return pl.pallas_call(
        body, grid=(nb, n_pairs, nq),
        in_specs=[pl.BlockSpec((MLA_TQ, 2 * LANES), lambda b, hp, i: (b * nq + i, hp)),
                  pl.BlockSpec((SEQ, 2 * LANES), lambda b, hp, i: (b, hp)),
                  pl.BlockSpec((SEQ, LANES), lambda b, hp, i: (b, hp))],
        out_specs=[pl.BlockSpec((MLA_TQ, LANES), lambda b, hp, i: (b * nq + i, hp))] * 2,
        out_shape=[SDS((t, MLA_WIDTH), F32)] * 2,
        name="mla_attn_fwd", compiler_params=_params("parallel", "parallel", "parallel"))(qb, kb, vb)


def mla_attn_bwd(qb, kb, vb, dy, y, lse, nb, xch):
    t = qb.shape[0]
    nq = SEQ // MLA_TQ
    n_pairs = MLA_HEADS // 2

    assert nq == 1

    def body(q_ref, k_ref, v_ref, do_ref, y_ref, lse_ref, dq_ref, dk_ref, dv_ref, dk_s, dv_s):
        dk_s[...] = jnp.zeros_like(dk_s)
        dv_s[...] = jnp.zeros_like(dv_s)
        head0 = lax.broadcasted_iota(jnp.int32, (MLA_SUB, LANES), 1) < MLA_V
        v = v_ref[...]
        for sub in range(MLA_TQ // MLA_SUB):
            rows = slice(sub * MLA_SUB, (sub + 1) * MLA_SUB)
            do = do_ref[rows, :]
            lse = lse_ref[rows, :]
            tt = do * y_ref[rows, :]
            dv = jnp.zeros((SEQ, LANES), F32)
            for h in range(2):
                sel = head0 if h == 0 else ~head0
                lo = h * MLA_V
                cols = slice(h * LANES, (h + 1) * LANES)
                q = q_ref[rows, cols]
                k = k_ref[:, cols]
                dd = jnp.sum(jnp.where(sel, tt, 0.0), axis=-1, keepdims=True)
                doh = jnp.where(sel, do, 0.0).astype(BF16)
                p = jnp.exp2(_dot_nt(q, k) - lse[:, lo:lo + 1])
                dp = _dot_nt(doh, v)
                ds = (p * (dp - dd)).astype(BF16)
                dq_ref[rows, cols] = _dot(ds, k).astype(dq_ref.dtype)
                dk_s[:, cols] += _dot_tn(ds, q)
                dv = dv + _dot_tn(p.astype(BF16), doh)
            dv_s[...] += dv
        dk_ref[...] = dk_s[...].astype(dk_ref.dtype)
        dv_ref[...] = dv_s[...].astype(dv_ref.dtype)

    qspec = pl.BlockSpec((MLA_TQ, 2 * LANES), lambda b, hp, i: (b * nq + i, hp))
    kspec = pl.BlockSpec((SEQ, 2 * LANES), lambda b, hp, i: (b, hp))
    vspec = pl.BlockSpec((SEQ, LANES), lambda b, hp, i: (b, hp))
    ospec = pl.BlockSpec((MLA_TQ, LANES), lambda b, hp, i: (b * nq + i, hp))
    return call_hosting_exchange(
        body, xch, grid=(nb, n_pairs, nq),
        in_specs=[qspec, kspec, vspec, ospec, ospec, ospec],
        out_specs=[qspec, kspec, vspec],
        out_shape=[SDS((t, MLA_QW), BF16), SDS((t, MLA_QW), BF16), SDS((t, MLA_WIDTH), BF16)],
        scratch_shapes=[pltpu.VMEM((SEQ, 2 * LANES), F32), pltpu.VMEM((SEQ, LANES), F32)],
        name="mla_attn_bwd", operands=(qb, kb, vb, dy, y, lse))


MEM_TQ = SEQ
MEM_SUB = 512
MEM_SCALE = MEM_HEAD_DIM ** -0.5
MQ_BLK4 = 5120 // MEM_WIDTH


def mem_attn_fwd(proj, mkv, nb):
    t = proj.shape[0]
    nq = SEQ // MEM_TQ

    def body(q_ref, mk_ref, mv_ref, y_ref):
        for sub in range(MEM_TQ // MEM_SUB):
            rows = slice(sub * MEM_SUB, (sub + 1) * MEM_SUB)
            for h in range(MEM_HEADS):
                cols = slice(h * LANES, (h + 1) * LANES)
                s = _dot_nt(q_ref[rows, cols].astype(BF16), mk_ref[:, cols]) * MEM_SCALE
                m = jnp.max(s, axis=-1, keepdims=True)
                p = jnp.exp(s - m)
                l = jnp.sum(p, axis=-1, keepdims=True)
                y_ref[rows, cols] = _dot(p.astype(BF16), mv_ref[:, cols]) / l

    return pl.pallas_call(
        body, grid=(nb, nq),
        in_specs=[pl.BlockSpec((MEM_TQ, MEM_WIDTH), lambda b, i: (b * nq + i, MQ_BLK4)),
                  pl.BlockSpec((N_MEM, MEM_WIDTH), lambda b, i: (b, 0)),
                  pl.BlockSpec((N_MEM, MEM_WIDTH), lambda b, i: (b, 1))],
        out_specs=pl.BlockSpec((MEM_TQ, MEM_WIDTH), lambda b, i: (b * nq + i, 0)),
        out_shape=SDS((t, MEM_WIDTH), F32),
        name="mem_attn_fwd", compiler_params=_params("parallel", "parallel"))(proj, mkv, mkv)


def mem_attn_bwd(proj, mkv, dy, nb):
    t = proj.shape[0]
    nq = SEQ // MEM_TQ

    def body(q_ref, mk_ref, mv_ref, do_ref, dq_ref, dmk_ref, dmv_ref):
        @pl.when(pl.program_id(1) == 0)
        def _():
            dmk_ref[...] = jnp.zeros_like(dmk_ref)
            dmv_ref[...] = jnp.zeros_like(dmv_ref)

        for sub in range(MEM_TQ // MEM_SUB):
            rows = slice(sub * MEM_SUB, (sub + 1) * MEM_SUB)
            for h in range(MEM_HEADS):
                cols = slice(h * LANES, (h + 1) * LANES)
                q = q_ref[rows, cols].astype(BF16)
                mk, mv = mk_ref[:, cols], mv_ref[:, cols]
                do = do_ref[rows, cols].astype(BF16)
                s = _dot_nt(q, mk) * MEM_SCALE
                e = jnp.exp(s - jnp.max(s, axis=-1, keepdims=True))
                p = e / jnp.sum(e, axis=-1, keepdims=True)
                dp = _dot_nt(do, mv)
                ds = (p * (dp - jnp.sum(p * dp, axis=-1, keepdims=True)) * MEM_SCALE).astype(BF16)
                dq_ref[rows, cols] = _dot(ds, mk).astype(BF16)
                dmk_ref[:, cols] += _dot_tn(ds, q)
                dmv_ref[:, cols] += _dot_tn(p.astype(BF16), do)

    ospec = pl.BlockSpec((MEM_TQ, MEM_WIDTH), lambda b, i: (b * nq + i, 0))
    kspec = pl.BlockSpec((N_MEM, MEM_WIDTH), lambda b, i: (b, 0))
    return pl.pallas_call(
        body, grid=(nb, nq),
        in_specs=[pl.BlockSpec((MEM_TQ, MEM_WIDTH), lambda b, i: (b * nq + i, MQ_BLK4)),
                  kspec, pl.BlockSpec((N_MEM, MEM_WIDTH), lambda b, i: (b, 1)), ospec],
        out_specs=[ospec, kspec, kspec],
        out_shape=[SDS((t, MEM_WIDTH), BF16), SDS((nb * N_MEM, MEM_WIDTH), F32), SDS((nb * N_MEM, MEM_WIDTH), F32)],
        name="mem_attn_bwd", compiler_params=_params("parallel", "arbitrary"))(proj, mkv, mkv, dy)


ROW_TM = 512
AG_BLK = 3072 // 1024
BG_BLK = 4608 // 512
MG_BLK = 5632 // 512
GROUPS = ((0, A_WIDTH), (A_WIDTH, MLA_WIDTH), (A_WIDTH + MLA_WIDTH, MEM_WIDTH))
D_MIX = 2048


def _gate_specs():
    def row(w, j):
        return pl.BlockSpec((ROW_TM, w), lambda i: (i, j))

    def vec(w):
        return pl.BlockSpec((1, w), lambda i: (0, 0))

    ys = [row(A_WIDTH, 0), row(MLA_WIDTH, 0), row(MEM_WIDTH, 0)]
    gates = [row(A_WIDTH, AG_BLK), row(MLA_WIDTH, BG_BLK), row(MEM_WIDTH, MG_BLK)]
    gains = [vec(A_WIDTH), vec(MLA_WIDTH), vec(MEM_WIDTH)]
    return row, vec, ys, gates, gains


def gate_out_ln_loss(ya, yb, ym, proj, goa, gob, gom, wout, h32, target, gp, bp):
    t, d = h32.shape
    _, _, ys, gates, gains = _gate_specs()

    def body(ya_ref, yb_ref, ym_ref, ga_ref, gb_ref, gm_ref, goa_ref, gob_ref, gom_ref, w_ref, h_ref, t_ref, gp_ref, bp_ref,
             z_ref, du32_ref, du16_ref, loss_ref, dgp_ref, dbp_ref):
        @pl.when(pl.program_id(0) == 0)
        def _():
            loss_ref[...] = jnp.zeros_like(loss_ref)
            dgp_ref[...] = jnp.zeros_like(dgp_ref)
            dbp_ref[...] = jnp.zeros_like(dbp_ref)

        for (off, w), y_ref, g_ref, go_ref in zip(GROUPS, (ya_ref, yb_ref, ym_ref), (ga_ref, gb_ref, gm_ref),
                                                  (goa_ref, gob_ref, gom_ref)):
            n, _ = _rms(y_ref[...], go_ref[...])
            gt = g_ref[...]
            z_ref[:, off:off + w] = (n * (gt * _sigmoid(gt))).astype(BF16)
        g = gp_ref[...]
        u = ALPHA * h_ref[...] + _dot(z_ref[...], w_ref[...])
        mu = jnp.mean(u, axis=-1, keepdims=True)
        uc = u - mu
        rstd = lax.rsqrt(jnp.mean(uc * uc, axis=-1, keepdims=True) + NORM_EPS)
        xhat = uc * rstd
        err = xhat * g + bp_ref[...] - t_ref[...]
        tok = jnp.sum(err * err, axis=-1, keepdims=True) * (1.0 / d)
        loss_ref[...] += 0.5 * jnp.sum(tok, axis=0, keepdims=True)
        dout = err * (1.0 / d)
        dxhat = dout * g
        du = rstd * (dxhat - jnp.mean(dxhat, axis=-1, keepdims=True)
                     - xhat * jnp.mean(dxhat * xhat, axis=-1, keepdims=True))
        du32_ref[...] = du
        du16_ref[...] = du.astype(BF16)
        dgp_ref[...] += jnp.sum(dout * xhat, axis=0, keepdims=True)
        dbp_ref[...] += jnp.sum(dout, axis=0, keepdims=True)

    row = pl.BlockSpec((ROW_TM, d), lambda i: (i, 0))
    vec = pl.BlockSpec((1, d), lambda i: (0, 0))
    zrow = pl.BlockSpec((ROW_TM, D_MIX), lambda i: (i, 0))
    return pl.pallas_call(
        body, grid=(t // ROW_TM,),
        in_specs=ys + gates + gains + [pl.BlockSpec((D_MIX, d), lambda i: (0, 0)), row, row, vec, vec],
        out_specs=[zrow, row, row, pl.BlockSpec((1, LANES), lambda i: (0, 0)), vec, vec],
        out_shape=[SDS((t, D_MIX), BF16), SDS((t, d), F32), SDS((t, d), BF16), SDS((1, LANES), F32), SDS((1, d), F32),
                   SDS((1, d), F32)],
        name="gate_out_ln_loss", compiler_params=_params("arbitrary"))(
            ya, yb, ym, proj, proj, proj, goa, gob, gom, wout, h32, target, gp, bp)


def gate_bwd(du16, wout, ya, yb, ym, proj, goa, gob, gom):
    t = ya.shape[0]
    row, vec, ys, gates, gains = _gate_specs()

    def body(du_ref, w_ref, ya_ref, yb_ref, ym_ref, ga_ref, gb_ref, gm_ref, goa_ref, gob_ref, gom_ref,
             dya_ref, dyb_ref, dym_ref, dga_ref, dgb_ref, dgm_ref, dgoa_ref, dgob_ref, dgom_ref):
        @pl.when(pl.program_id(0) == 0)
        def _():
            dgoa_ref[...] = jnp.zeros_like(dgoa_ref)
            dgob_ref[...] = jnp.zeros_like(dgob_ref)
            dgom_ref[...] = jnp.zeros_like(dgom_ref)

        dz = _dot_nt(du_ref[...], w_ref[...])
        for (off, w), y_ref, g_ref, go_ref, dy_ref, dg_ref, dgo_ref in zip(
                GROUPS, (ya_ref, yb_ref, ym_ref), (ga_ref, gb_ref, gm_ref), (goa_ref, gob_ref, gom_ref),
                (dya_ref, dyb_ref, dym_ref), (dga_ref, dgb_ref, dgm_ref), (dgoa_ref, dgob_ref, dgom_ref)):
            dzg = dz[:, off:off + w]
            y, gt, go = y_ref[...], g_ref[...], go_ref[...]
            n, r = _rms(y, go)
            sg = _sigmoid(gt)
            dg_ref[...] = (dzg * n * (sg * (1.0 + gt * (1.0 - sg)))).astype(BF16)
            dy, dgo = _rms_bwd(dzg * (gt * sg), y, r, go)
            dy_ref[...] = dy
            dgo_ref[...] += dgo

    widths = (A_WIDTH, MLA_WIDTH, MEM_WIDTH)
    return pl.pallas_call(
        body, grid=(t // ROW_TM,),
        in_specs=[row(D_MODEL, 0), pl.BlockSpec((D_MIX, D_MODEL), lambda i: (0, 0))] + ys + gates + gains,
        out_specs=[row(w, 0) for w in widths] * 2 + [vec(w) for w in widths],
        out_shape=[SDS((t, w), F32) for w in widths] + [SDS((t, w), BF16) for w in widths] + [SDS((1, w), F32) for w in widths],
        name="gate_bwd", compiler_params=_params("arbitrary"))(du16, wout, ya, yb, ym, proj, proj, proj, goa, gob, gom)


def dh_ln_bwd(pieces, win_t, du32, x2, g_emb, xch):
    t, d = x2.shape

    def body(*refs):
        p_refs = refs[:len(pieces)]
        w_ref, du_ref, x_ref, g_ref, dx_ref, dg_ref, db_ref = refs[len(pieces):]

        @pl.when(pl.program_id(0) == 0)
        def _():
            dg_ref[...] = jnp.zeros_like(dg_ref)
            db_ref[...] = jnp.zeros_like(db_ref)

        dh = ALPHA * du_ref[...]
        for p_ref, off, w in zip(p_refs, PIECE_OFFS, PIECE_WIDTHS):
            dh = dh + _dot(p_ref[...], w_ref[off:off + w, :])
        x = x_ref[...]
        xc = x - jnp.mean(x, axis=-1, keepdims=True)
        rstd = lax.rsqrt(jnp.mean(xc * xc, axis=-1, keepdims=True) + NORM_EPS)
        xhat = xc * rstd
        dg_ref[...] += jnp.sum(dh * xhat, axis=0, keepdims=True)
        db_ref[...] += jnp.sum(dh, axis=0, keepdims=True)
        tg = dh * g_ref[...]
        dx_ref[...] = rstd * (tg - jnp.mean(tg, axis=-1, keepdims=True)
                              - xhat * jnp.mean(tg * xhat, axis=-1, keepdims=True))

    row = pl.BlockSpec((ROW_TM, d), lambda i: (i, 0))
    vec = pl.BlockSpec((1, d), lambda i: (0, 0))
    return call_hosting_exchange(
        body, xch, grid=(t // ROW_TM,),
        in_specs=[pl.BlockSpec((ROW_TM, w), lambda i: (i, 0)) for w in PIECE_WIDTHS]
        + [pl.BlockSpec(win_t.shape, lambda i: (0, 0)), row, row, vec],
        out_specs=[row, vec, vec],
        out_shape=[SDS((t, d), F32), SDS((1, d), F32), SDS((1, d), F32)],
        scratch_shapes=[], name="dh_ln_bwd", operands=(*pieces, win_t, du32, x2, g_emb))


def _adamw(w, g, m, v):
    m2 = ADAM_B1 * m + (1.0 - ADAM_B1) * g
    v2 = ADAM_B2 * v + (1.0 - ADAM_B2) * (g * g)
    m_hat = m2 / (1.0 - ADAM_B1 ** ADAM_STEP)
    v_hat = v2 / (1.0 - ADAM_B2 ** ADAM_STEP)
    return -ADAM_LR * (m_hat / (jnp.sqrt(v_hat) + ADAM_EPS) + ADAM_WD * w), m2, v2


def adamw_shard(w, parts, m, v, name):
    r, c = w.shape
    if r % 256 == 0 or r * c <= 256 * 1024:
        tr, tc = min(r, 256), c
    else:
        tr, tc = r, 256

    def body(w_ref, p_ref, m_ref, v_ref, g_ref, d_ref, nm_ref, nv_ref):
        g = p_ref[0].astype(F32)
        for k in range(1, N_DEV):
            g = g + p_ref[k].astype(F32)
        g_ref[...] = g
        d_ref[...], nm_ref[...], nv_ref[...] = _adamw(w_ref[...], g, m_ref[...], v_ref[...])

    blk = pl.BlockSpec((tr, tc), lambda i, j: (i, j))
    return pl.pallas_call(
        body, grid=(r // tr, c // tc),
        in_specs=[blk, pl.BlockSpec((N_DEV, tr, tc), lambda i, j: (0, i, j)), blk, blk],
        out_specs=[blk] * 4, out_shape=[SDS((r, c), F32)] * 4, name=name,
        compiler_params=_params("parallel", "parallel"))(w, parts, m, v)


def _place():
    return lax.axis_index("x"), lax.axis_index("y"), lax.axis_index("c")


def _flat(px, py, pc):
    return 4 * px + 2 * py + pc


def _peer(x, y, c, k):
    return (1 - x if k & 4 else x, 1 - y if k & 2 else y, 1 - c if k & 1 else c)


def cast_shards(shards):
    def body(*refs):
        n = len(refs) // 2
        for i_ref, o_ref in zip(refs[:n], refs[n:]):
            o_ref[...] = i_ref[...].astype(BF16)

    return pl.pallas_call(body, out_shape=[SDS(s.shape, BF16) for s in shards], name="cast_shards",
                          compiler_params=_params())(*shards)


def _two_level_gather_plan(src_refs, land_refs, send_sems, recv_sems, local_sems):
    n = len(src_refs)
    x, y, c = _place()
    me, sib = (x, y, c), (x, y, 1 - c)
    chips = [(1 - x, y), (x, 1 - y), (1 - x, 1 - y)]

    def copy(a, k, block, to, src=None):
        dst = land_refs[a].at[_flat(*block)]
        return pltpu.make_async_remote_copy(
            src_ref=dst if src is None else src, dst_ref=dst,
            send_sem=send_sems.at[a * N_DEV + k], recv_sem=recv_sems.at[a * N_DEV + k],
            device_id=to, device_id_type=MESH)

    mine = [pltpu.make_async_copy(src_refs[a], land_refs[a].at[_flat(*me)], local_sems.at[a]) for a in range(n)]
    first = []
    for a in range(n):
        first.append(copy(a, 0, me, sib, src=src_refs[a]))
        first += [copy(a, 1 + j, me, (*chip, c), src=src_refs[a]) for j, chip in enumerate(chips)]

    def start():
        for cp in mine + first:
            cp.start()

    def finish():
        passed = []
        for j, chip in enumerate(chips):
            for a in range(n):
                copy(a, 1 + j, (*chip, c), me).wait_recv()
                fwd = copy(a, 4 + j, (*chip, c), sib)
                fwd.start()
                passed.append(fwd)
        for a in range(n):
            copy(a, 0, sib, me).wait_recv()
            for j, chip in enumerate(chips):
                copy(a, 4 + j, (*chip, 1 - c), me).wait_recv()
        for cp in first + passed:
            cp.wait_send()
        for cp in mine:
            cp.wait()

    return start, finish


ALL_DEVICES = tuple(range(N_DEV))


def _exchange_plan(src_refs, land_refs, dests, send_sems, recv_sems, local_sems):
    x, y, c = _place()
    me = _flat(x, y, c)
    plan = []
    for a, (src, land, dl) in enumerate(zip(src_refs, land_refs, dests)):
        for li, j in enumerate(dl):
            to = ((j >> 2) & 1, (j >> 1) & 1, j & 1)
            block = src.at[li] if len(src.shape) == len(land.shape) else src

            def push(slot, a=a, block=block, land=land, j=j, to=to):
                return pltpu.make_async_remote_copy(
                    src_ref=block, dst_ref=land.at[slot], send_sem=send_sems.at[a * N_DEV + j],
                    recv_sem=recv_sems.at[a * N_DEV + slot], device_id=to, device_id_type=MESH)

            own = pltpu.make_async_copy(block, land.at[j], local_sems.at[a])
            plan.append((j, push(me), own, [push(s) for s in range(N_DEV) if s != j]))
    return me, plan


def _exchange_start(me, plan):
    for j, send, own, _ in plan:
        @pl.when(me != j)
        def _(send=send):
            send.start()

        @pl.when(me == j)
        def _(own=own):
            own.start()


def _exchange_wait(me, plan):
    for j, send, own, arrivals in plan:
        @pl.when(me != j)
        def _(send=send):
            send.wait_send()

        @pl.when(me == j)
        def _(own=own, arrivals=arrivals):
            own.wait()
            for arrival in arrivals:
                arrival.wait_recv()


def call_hosting_exchange(core, xch, *, grid, in_specs, out_specs, out_shape, scratch_shapes, name, operands):
    srcs, dests, landing = xch
    n, n_in, n_out, n_scr = len(srcs), len(in_specs), len(out_specs), len(scratch_shapes)

    def body(*refs):
        ins, src_refs = refs[:n_in], refs[n_in:n_in + n]
        outs = refs[n_in + 2 * n:n_in + 2 * n + n_out]
        land_refs = refs[n_in + 2 * n + n_out:n_in + 3 * n + n_out]
        scratch = refs[n_in + 3 * n + n_out:n_in + 3 * n + n_out + n_scr]
        sems = refs[n_in + 3 * n + n_out + n_scr:]
        first = functools.reduce(jnp.logical_and, [pl.program_id(i) == 0 for i in range(len(grid))])
        last = functools.reduce(jnp.logical_and, [pl.program_id(i) == grid[i] - 1 for i in range(len(grid))])
        if dests is None:
            start, finish = _two_level_gather_plan(src_refs, land_refs, *sems)
        else:
            me, plan = _exchange_plan(src_refs, land_refs, dests, *sems)
            start, finish = functools.partial(_exchange_start, me, plan), functools.partial(_exchange_wait, me, plan)
        pl.when(first)(start)
        core(*ins, *outs, *scratch)
        pl.when(last)(finish)

    hbm = pl.BlockSpec(memory_space=pl.ANY)
    res = pl.pallas_call(
        body, grid=grid,
        in_specs=list(in_specs) + [hbm] * (2 * n), out_specs=list(out_specs) + [hbm] * n,
        out_shape=list(out_shape) + [SDS(l.shape, l.dtype) for l in landing],
        scratch_shapes=list(scratch_shapes) + [pltpu.SemaphoreType.DMA((N_DEV * n,)), pltpu.SemaphoreType.DMA((N_DEV * n,)),
                                               pltpu.SemaphoreType.DMA((n,))],
        input_output_aliases={n_in + n + k: n_out + k for k in range(n)},
        name=name, compiler_params=_params(*(("arbitrary",) * len(grid))))(*operands, *srcs, *landing)
    return res[:n_out], res[n_out:]


SLOT_ROWS = 8


def small_allreduce_adamw(loss_sum, grads, ws, ms, vs):
    n = len(grads)
    rows = [g.shape[0] for g in grads]
    total = SLOT_ROWS * (n + 1)

    def body(*refs):
        loss_ref, g_refs, w_refs = refs[0], refs[1:1 + n], refs[1 + n:1 + 2 * n]
        m_refs, v_refs = refs[1 + 2 * n:1 + 3 * n], refs[1 + 3 * n:1 + 4 * n]
        outs = refs[1 + 4 * n:2 + 8 * n]
        vec, gath, tot, send_sems, recv_sems = refs[2 + 8 * n:]
        x, y, c = _place()
        me = _flat(x, y, c)
        vec[...] = jnp.zeros_like(vec)
        vec[0:1, :] = loss_ref[...]
        for i in range(n):
            vec[SLOT_ROWS * (i + 1):SLOT_ROWS * (i + 1) + rows[i], :] = g_refs[i][...]
        gath[me] = vec[...]
        copies = []
        for k in range(1, N_DEV):
            peer = _peer(x, y, c, k)
            copies.append(pltpu.make_async_remote_copy(
                src_ref=vec, dst_ref=gath.at[me], send_sem=send_sems.at[k - 1], recv_sem=recv_sems.at[k - 1],
                device_id=peer, device_id_type=MESH))
        for cp in copies:
            cp.start()
        for cp in copies:
            cp.wait_recv()
        for cp in copies:
            cp.wait_send()
        g = gath[0]
        for j in range(1, N_DEV):
            g = g + gath[j]
        tot[...] = g
        outs[0][...] = tot[0:1, :]
        for i in range(n):
            gi = tot[SLOT_ROWS * (i + 1):SLOT_ROWS * (i + 1) + rows[i], :]
            outs[1 + i][...] = gi
            outs[1 + n + i][...], outs[1 + 2 * n + i][...], outs[1 + 3 * n + i][...] = _adamw(
                w_refs[i][...], gi, m_refs[i][...], v_refs[i][...])

    shapes = [SDS(g.shape, F32) for g in grads]
    return pl.pallas_call(
        body, out_shape=[SDS((1, LANES), F32)] + shapes * 4,
        scratch_shapes=[pltpu.VMEM((total, LANES), F32), pltpu.VMEM((N_DEV, total, LANES), F32), pltpu.VMEM((total, LANES), F32),
                        pltpu.SemaphoreType.DMA((7,)), pltpu.SemaphoreType.DMA((7,))],
        name="small_allreduce_adamw", compiler_params=_params())(loss_sum, *grads, *ws, *ms, *vs)


def _rope_lane_patterns():
    inv = lambda r: ROPE_THETA ** (-(jnp.arange(0, r, 2, dtype=F32) / r))
    z = lambda n: jnp.zeros((n,), F32)
    o = lambda n: jnp.ones((n,), F32)
    half, rest = A_ROT // 2, A_HEAD_DIM - A_ROT
    ia, im = inv(A_ROT), inv(MLA_ROPE)
    mh, tail = MLA_ROPE // 2, LANES - MLA_NOPE - MLA_ROPE
    rows = [jnp.tile(jnp.concatenate([ia, ia, z(rest)]), 2),
            jnp.tile(jnp.concatenate([o(half), z(half + rest)]), 2),
            jnp.tile(jnp.concatenate([z(half), o(half), z(rest)]), 2),
            jnp.concatenate([z(MLA_NOPE), im, im, z(tail)]),
            jnp.concatenate([z(MLA_NOPE), o(mh), z(mh + tail)]),
            jnp.concatenate([z(MLA_NOPE + mh), o(mh), z(tail)]),
            z(LANES), z(LANES)]
    return jnp.stack(rows)


KR_LO, KR_HI = 4480, 4512
W_IN_SHARD = D_IN // N_DEV
BG_SPLIT = 6 * W_IN_SHARD - KR_HI


def w_in_working_t(g):
    pad_lo, pad_hi = MLA_NOPE, LANES - MLA_NOPE - MLA_ROPE
    spans = []
    for lo, hi, shift in ((0, KR_LO, 0), (KR_LO, KR_HI, pad_lo), (KR_HI, D_IN, pad_lo + pad_hi)):
        r = lo
        while r < hi:
            j = r // W_IN_SHARD
            n = min(hi, (j + 1) * W_IN_SHARD) - r
            spans.append((j, r - j * W_IN_SHARD, n, r + shift))
            r += n

    def body(g_ref, o_ref):
        o_ref[KR_LO:KR_LO + pad_lo, :] = jnp.zeros((pad_lo, D_MODEL), o_ref.dtype)
        o_ref[KR_HI + pad_lo:KR_HI + pad_lo + pad_hi, :] = jnp.zeros((pad_hi, D_MODEL), o_ref.dtype)
        for j, src, n, dst in spans:
            o_ref[dst:dst + n, :] = g_ref[j, src:src + n, :]

    return pl.pallas_call(body, out_shape=SDS((D_INW, D_MODEL), g.dtype), name="w_in_working_t", compiler_params=_params())(g)


def _w_in_shard_5(d_ag_tail, d_cc, d_bg_head):
    kr = MLA_Q_RANK + MLA_KV_RANK + MLA_NOPE
    rows = jnp.concatenate([d_ag_tail, d_cc[:MLA_Q_RANK + MLA_KV_RANK], d_cc[kr:kr + MLA_ROPE], d_bg_head], 0)
    return rows.reshape(1, W_IN_SHARD, D_MODEL).astype(BF16)


def _w_uq_working(g):
    w = jnp.pad(g.transpose(1, 0, 2), ((0, 0), (0, 0), (0, LANES - MLA_NOPE - MLA_ROPE)))
    return w.reshape(MLA_Q_RANK, MLA_QW)


def _w_uq_parts(dw):
    return dw.reshape(MLA_Q_RANK, MLA_HEADS, LANES)[:, :, :MLA_NOPE + MLA_ROPE].transpose(1, 0, 2)


def _w_ukv_working(g):
    wk = jnp.pad(g[:, :, :MLA_NOPE].transpose(1, 0, 2), ((0, 0), (0, 0), (0, LANES - MLA_NOPE)))
    wv = g[:, :, MLA_NOPE:].transpose(1, 0, 2)
    return jnp.concatenate([wk.reshape(MLA_KV_RANK, MLA_QW), wv.reshape(MLA_KV_RANK, MLA_WIDTH)], 1)


def _w_ukv_parts(dw):
    dk = dw[:, :MLA_QW].reshape(MLA_KV_RANK, MLA_HEADS, LANES)[:, :, :MLA_NOPE]
    dv = dw[:, MLA_QW:].reshape(MLA_KV_RANK, MLA_HEADS, MLA_V)
    return jnp.concatenate([dk, dv], -1).transpose(1, 0, 2)


SMALL_NAMES = ("g_emb", "b_emb", "g_cq", "g_ckv", "g_out_a", "g_out_b", "g_out_m", "g_post", "b_post")


def kernel(x, mem, positions, g_emb, b_emb, w_in, g_cq, g_ckv, w_uq, w_ukv, w_mem_kv, g_out_a, g_out_b, g_out_m, w_out, g_post, b_post, loss_target, m_g_emb, m_b_emb, m_w_in, m_g_cq, m_g_ckv, m_w_uq, m_w_ukv, m_w_mem_kv, m_g_out_a, m_g_out_b, m_g_out_m, m_w_out, m_g_post, m_b_post, v_g_emb, v_b_emb, v_w_in, v_g_cq, v_g_ckv, v_w_uq, v_w_ukv, v_w_mem_kv, v_g_out_a, v_g_out_b, v_g_out_m, v_w_out, v_g_post, v_b_post):
    nb = x.shape[0]
    t = nb * SEQ
    x2 = x.reshape(t, D_MODEL)
    tgt2 = loss_target.reshape(t, D_MODEL)
    mem2 = mem.reshape(nb * N_MEM, D_MODEL)
    g_emb2, b_emb2 = g_emb.reshape(1, -1), b_emb.reshape(1, -1)

    w_in_t, m_w_in_t, v_w_in_t = w_in[0].T, m_w_in[0].T, v_w_in[0].T
    s_in, s_uq, s_ukv, s_mem, s_out = cast_shards((w_in_t, w_uq[0], w_ukv[0], w_mem_kv[0], w_out[0]))
    (h32, h16, (a_c, a_sa, a_sb), (m_c, m_sa, m_sb)), (g_in,) = embed_fwd(
        x2, g_emb2, b_emb2, positions, ((s_in,), None, (lax.empty((N_DEV,) + s_in.shape, BF16),)))
    win_t = w_in_working_t(g_in)

    proj = mm_nn(h16, win_t, F32, 1024, 1536, "proj", rhs_transposed=True)
    later = (s_uq, s_ukv, s_mem, s_out)
    (ya, lse_a), qkv_d, (g_uq, g_ukv, g_mem, g_out) = a_attn_fwd(
        proj, a_c, a_sa, a_sb, nb,
        (later, (ALL_DEVICES,) * len(later), tuple(lax.empty((N_DEV,) + w.shape, BF16) for w in later)))
    wuq_w = _w_uq_working(g_uq)
    wkv_w = _w_ukv_working(g_ukv)
    wmem = g_mem.reshape(D_MODEL, 2 * MEM_WIDTH)
    wout = g_out.reshape(D_MIX, D_MODEL)
    qb, kb, vb = mla_prep_fwd(proj, m_c, m_sa, m_sb, g_cq, g_ckv, wuq_w, wkv_w)
    yb, lse_b = mla_attn_fwd(qb, kb, vb, nb)
    mkv = mm_nn(mem2, wmem, BF16, nb * N_MEM, 512, "mem_kv")
    ym = mem_attn_fwd(proj, mkv, nb)
    z, du32, du16, loss_sum, dg_post, db_post = gate_out_ln_loss(
        ya, yb, ym, proj, g_out_a, g_out_b, g_out_m, wout, h32, tgt2, g_post, b_post)

    dya, dyb, dym, dag, dbg, dmg, dg_out_a, dg_out_b, dg_out_m = gate_bwd(
        du16, wout, ya, yb, ym, proj, g_out_a, g_out_b, g_out_m)
    dw_out = mm_tn(z, du16, 1024, "dw_out")
    dmq, dmk, dmv = mem_attn_bwd(proj, mkv, dym, nb)
    dw_mem = mm_tn(mem2, jnp.concatenate([dmk, dmv], 1), nb * N_MEM, "dw_mem")
    d_gates, shards_6_7 = mm_tn_group((dbg, dmq, dmg), h16, 2048, "dw_in_bg_mq_mg", W_IN_SHARD, BG_SPLIT, 2)
    landing = lambda w, dtype=F32: lax.empty((N_DEV,) + w.shape, dtype)
    big_w = (w_in_t, w_uq[0], w_ukv[0], w_mem_kv[0], w_out[0])
    (daq, dak, dav), (p_out, p_mem, p_in) = a_attn_bwd(
        qkv_d, a_c, a_sa, a_sb, dya, ya, lse_a, nb,
        ((dw_out.reshape(N_DEV, D_MIX // N_DEV, D_MODEL), dw_mem.reshape(N_DEV, D_MODEL // N_DEV, 2 * MEM_WIDTH),
          shards_6_7),
         (ALL_DEVICES, ALL_DEVICES, (6, 7)),
         (landing(w_out[0]), landing(w_mem_kv[0]), landing(w_in_t, BF16))))
    d_a, shards_0_4 = mm_tn_group((daq, dak, dav, dag), h16, 1024, "dw_in_aq_ak_av_ag", W_IN_SHARD, 0, 5)
    (dqb, dkb, dvb), (p_in,) = mla_attn_bwd(
        qb, kb, vb, dyb, yb, lse_b, nb, ((shards_0_4,), ((0, 1, 2, 3, 4),), (p_in,)))
    dcc, dqf, cqn, dkvf, ckvn, dg_cq, dg_ckv = mla_prep_bwd(proj, m_c, m_sa, m_sb, g_cq, g_ckv, wuq_w, wkv_w, dqb, dkb, dvb)
    dw_uq = mm_tn(cqn, dqf, 2048, "dw_uq")
    dw_ukv = mm_tn(ckvn, dkvf, 2048, "dw_ukv")
    d_cc = mm_tn(dcc, h16, 2048, "dw_in_cc")
    pieces = (daq, dak, dav, dag, dcc, dbg, dmq, dmg)
    (grad_x, dg_emb, db_emb), (p_in, p_uq, p_ukv) = dh_ln_bwd(
        pieces, win_t, du32, x2, g_emb2,
        ((_w_in_shard_5(d_a[5 * W_IN_SHARD:], d_cc, d_gates[:BG_SPLIT]), _w_uq_parts(dw_uq), _w_ukv_parts(dw_ukv)),
         ((5,), ALL_DEVICES, ALL_DEVICES),
         (p_in, landing(w_uq[0]), landing(w_ukv[0]))))

    parts = (p_in, p_uq, p_ukv, p_mem, p_out)
    big_m = (m_w_in_t, m_w_uq[0], m_w_ukv[0], m_w_mem_kv[0], m_w_out[0])
    big_v = (v_w_in_t, v_w_uq[0], v_w_ukv[0], v_w_mem_kv[0], v_w_out[0])
    big = {}
    for name, w, p, m, v in zip(("w_in", "w_uq", "w_ukv", "w_mem_kv", "w_out"), big_w, parts, big_m, big_v):
        res = adamw_shard(w, p, m, v, "adamw_" + name)
        big[name] = [(o.T if name == "w_in" else o)[None] for o in res]

    small_w = (g_emb, b_emb, g_cq, g_ckv, g_out_a, g_out_b, g_out_m, g_post, b_post)
    small_m = (m_g_emb, m_b_emb, m_g_cq, m_g_ckv, m_g_out_a, m_g_out_b, m_g_out_m, m_g_post, m_b_post)
    small_v = (v_g_emb, v_b_emb, v_g_cq, v_g_ckv, v_g_out_a, v_g_out_b, v_g_out_m, v_g_post, v_b_post)
    small_g = (dg_emb, db_emb, dg_cq, dg_ckv, dg_out_a, dg_out_b, dg_out_m, dg_post, db_post)
    rows128 = lambda vals: [v.reshape(-1, LANES) for v in vals]
    res = small_allreduce_adamw(loss_sum, rows128(small_g), rows128(small_w), rows128(small_m), rows128(small_v))
    loss = res[0][0, 0]
    n_small = len(small_w)
    sg, sd, sm, sv = [[r.reshape(w.shape) for r, w in zip(res[1 + k * n_small:1 + (k + 1) * n_small], small_w)]
                      for k in range(4)]

    order = ("g_emb", "b_emb", "w_in", "g_cq", "g_ckv", "w_uq", "w_ukv", "w_mem_kv", "g_out_a", "g_out_b", "g_out_m",
             "w_out", "g_post", "b_post")
    small_idx = {n: i for i, n in enumerate(SMALL_NAMES)}
    outs = [loss, grad_x.reshape(x.shape)]
    for kind in range(4):
        for name in order:
            outs.append(big[name][kind] if name in big else (sg, sd, sm, sv)[kind][small_idx[name]])
    return tuple(outs)
```

```python
import functools

import jax
import jax.numpy as jnp
from jax import lax
from jax.experimental import pallas as pl
from jax.experimental.pallas import tpu as pltpu

F32 = jnp.float32
BF16 = jnp.bfloat16
SDS = jax.ShapeDtypeStruct
MESH = pl.DeviceIdType.MESH

D_MODEL = 1024
SEQ = 2048
A_HEADS, A_HEAD_DIM, A_ROT = 16, 64, 16
A_WIDTH = 1024
DILATIONS = (1, 4, 16)
N_SIDE = 64
MLA_HEADS, MLA_Q_RANK, MLA_KV_RANK = 8, 256, 128
MLA_NOPE, MLA_ROPE, MLA_V = 64, 32, 64
MLA_WIDTH = 512
N_MEM, MEM_HEADS, MEM_HEAD_DIM, MEM_WIDTH = 256, 4, 128, 512
ROPE_THETA = 500000.0
NORM_EPS = 1e-5
NEG_INF = -1e30
ALPHA = 2.0 ** 0.25
D_IN = 6048
N_DEV = 8

ADAM_LR, ADAM_B1, ADAM_B2, ADAM_EPS, ADAM_WD, ADAM_STEP = 0.001, 0.9, 0.999, 1e-08, 0.01, 10

D_INW = 6144
PIECE_WIDTHS = (1024, 1024, 1024, 1024, 512, 512, 512, 512)
PIECE_OFFS = (0, 1024, 2048, 3072, 4096, 4608, 5120, 5632)
LANES = 128
VMEM_LIMIT = 56 * 1024 * 1024


def _params(*sem):
    kw = dict(vmem_limit_bytes=VMEM_LIMIT)
    if sem:
        kw["dimension_semantics"] = sem
    return pltpu.CompilerParams(**kw)


def _dot(a, b):
    return jnp.dot(a, b, preferred_element_type=F32)


def _dot_nt(a, b):
    return lax.dot_general(a, b, (((1,), (1,)), ((), ())), preferred_element_type=F32)


def _dot_tn(a, b):
    return lax.dot_general(a, b, (((0,), (0,)), ((), ())), preferred_element_type=F32)


def _sigmoid(x):
    return 1.0 / (1.0 + jnp.exp(-x))


def _rope_fwd(x, c, sa, sb, half):
    n = x.shape[-1]
    return x * c + pltpu.roll(x, n - half, 1) * sa + pltpu.roll(x, half, 1) * sb


def _rope_bwd(dy, c, sa, sb, half):
    n = dy.shape[-1]
    return dy * c + pltpu.roll(dy * sa, half, 1) + pltpu.roll(dy * sb, n - half, 1)


def mm_nn(a, b, out_dtype, tm, tn, name, rhs_transposed=False):
    m, k = a.shape
    n = b.shape[0] if rhs_transposed else b.shape[1]
    dot = _dot_nt if rhs_transposed else _dot

    def body(a_ref, b_ref, o_ref):
        o_ref[...] = dot(a_ref[...].astype(BF16), b_ref[...].astype(BF16)).astype(o_ref.dtype)

    b_spec = pl.BlockSpec((tn, k), lambda j, i: (j, 0)) if rhs_transposed else pl.BlockSpec((k, tn), lambda j, i: (0, j))
    return pl.pallas_call(
        body, grid=(n // tn, m // tm),
        in_specs=[pl.BlockSpec((tm, k), lambda j, i: (i, 0)), b_spec],
        out_specs=pl.BlockSpec((tm, tn), lambda j, i: (i, j)),
        out_shape=SDS((m, n), out_dtype), name=name,
        compiler_params=_params("parallel", "parallel"))(a, b)


def mm_tn(a, b, tt, name):
    t, m = a.shape
    n = b.shape[1]

    def body(a_ref, b_ref, o_ref):
        @pl.when(pl.program_id(0) == 0)
        def _():
            o_ref[...] = jnp.zeros_like(o_ref)

        o_ref[...] += _dot_tn(a_ref[...].astype(BF16), b_ref[...].astype(BF16))

    return pl.pallas_call(
        body, grid=(t // tt,),
        in_specs=[pl.BlockSpec((tt, m), lambda i: (i, 0)), pl.BlockSpec((tt, n), lambda i: (i, 0))],
        out_specs=pl.BlockSpec((m, n), lambda i: (0, 0)),
        out_shape=SDS((m, n), F32), name=name,
        compiler_params=_params("arbitrary"))(a, b)


def mm_tn_group(pieces, b, tt, name, slab_rows, first_slab_row, n_slabs):
    n, (t, w), cols = len(pieces), pieces[0].shape, b.shape[1]
    nt = t // tt

    def body(*refs):
        p_refs, b_ref, o_ref, slab_ref = refs[:n], refs[n], refs[n + 1], refs[n + 2]

        @pl.when(pl.program_id(1) == 0)
        def _():
            o_ref[...] = jnp.zeros_like(o_ref)

        for k in range(n):
            @pl.when(pl.program_id(0) == k)
            def _(k=k):
                o_ref[...] += _dot_tn(p_refs[k][...], b_ref[...])

            @pl.when((pl.program_id(0) == k) & (pl.program_id(1) == nt - 1))
            def _(k=k):
                for j in range(n_slabs):
                    lo = max(k * w, first_slab_row + j * slab_rows)
                    hi = min((k + 1) * w, first_slab_row + (j + 1) * slab_rows)
                    if lo < hi:
                        dst = lo - first_slab_row - j * slab_rows
                        slab_ref[j, dst:dst + hi - lo, :] = o_ref[lo - k * w:hi - k * w, :].astype(slab_ref.dtype)

    def piece_spec(k):
        return pl.BlockSpec((tt, w), lambda p, i: (jnp.where(p < k, 0, jnp.where(p > k, nt - 1, i)), 0))

    return pl.pallas_call(
        body, grid=(n, nt),
        in_specs=[piece_spec(k) for k in range(n)] + [pl.BlockSpec((tt, cols), lambda p, i: (i, 0))],
        out_specs=[pl.BlockSpec((w, cols), lambda p, i: (p, 0)),
                   pl.BlockSpec((n_slabs, slab_rows, cols), lambda p, i: (0, 0, 0))],
        out_shape=[SDS((n * w, cols), F32), SDS((n_slabs, slab_rows, cols), BF16)], name=name,
        compiler_params=_params("arbitrary", "arbitrary"))(*pieces, b)


def embed_fwd(x2, g, b, positions, xch):
    t, d = x2.shape
    tm = 512
    pos = positions.astype(F32).reshape(-1, 1)

    def body(x_ref, g_ref, b_ref, pos_ref, pat_ref, h32_ref, h16_ref, *tabs):
        x = x_ref[...]
        mu = jnp.mean(x, axis=-1, keepdims=True)
        xc = x - mu
        var = jnp.mean(xc * xc, axis=-1, keepdims=True)
        h = xc * lax.rsqrt(var + NORM_EPS) * g_ref[...] + b_ref[...]
        h32_ref[...] = h
        h16_ref[...] = h.astype(BF16)
        p = pos_ref[...]
        for k in range(2):
            inv, first, second = pat_ref[3 * k:3 * k + 1, :], pat_ref[3 * k + 1:3 * k + 2, :], pat_ref[3 * k + 2:3 * k + 3, :]
            ang = p * inv
            sn = jnp.sin(ang)
            tabs[3 * k][...] = jnp.where(first + second > 0.0, jnp.cos(ang), 1.0)
            tabs[3 * k + 1][...] = -first * sn
            tabs[3 * k + 2][...] = second * sn

    row = pl.BlockSpec((tm, d), lambda i: (i, 0))
    vec = pl.BlockSpec((1, d), lambda i: (0, 0))
    tab = pl.BlockSpec((tm, LANES), lambda i: (i, 0))
    res, landed = call_hosting_exchange(
        body, xch, grid=(t // tm,),
        in_specs=[row, vec, vec, pl.BlockSpec((tm, 1), lambda i: (i, 0)), pl.BlockSpec((8, LANES), lambda i: (0, 0))],
        out_specs=[row, row] + [tab] * 6,
        out_shape=[SDS((t, d), F32), SDS((t, d), BF16)] + [SDS((t, LANES), F32)] * 6,
        scratch_shapes=[], name="embed_fwd", operands=(x2, g, b, pos, _rope_lane_patterns()))
    return (res[0], res[1], tuple(res[2:5]), tuple(res[5:8])), landed


Q_BLK = 128
UNROLL_FWD = 16
UNROLL_BWD = 16


def _pattern_geometry(d):
    length = SEQ // d
    nblk = length // Q_BLK
    kwin = min(2 * Q_BLK, length)
    return length, nblk, kwin


def _block_coords(idx, d):
    length, nblk, kwin = _pattern_geometry(d)
    r = lax.shift_right_logical(idx, nblk.bit_length() - 1)
    i = idx & (nblk - 1)
    q0 = pl.multiple_of(r * length + i * Q_BLK, Q_BLK)
    ks = jnp.clip(i * Q_BLK - N_SIDE, 0, length - kwin)
    k0 = pl.multiple_of(r * length + ks, N_SIDE)
    qpos = i * Q_BLK + lax.broadcasted_iota(jnp.int32, (Q_BLK, kwin), 0)
    kpos = ks + lax.broadcasted_iota(jnp.int32, (Q_BLK, kwin), 1)
    valid = jnp.abs(kpos - qpos) <= N_SIDE
    return q0, k0, kwin, valid


def _deinterleave(src_ref, dst_ref, d, dtype, tmp_ref):
    if d == 1:
        dst_ref[...] = src_ref[...].astype(dtype)
        return
    q = SEQ // 4
    if d == 4:
        for r in range(4):
            dst_ref[r * q:(r + 1) * q, :] = src_ref[pl.ds(r, q, stride=4), :].astype(dtype)
        return
    assert d == 16
    n = SEQ // 16
    for r in range(4):
        tmp_ref[r * q:(r + 1) * q, :] = src_ref[pl.ds(r, q, stride=4), :]
    for r in range(4):
        for j in range(4):
            dst_ref[(r + 4 * j) * n:(r + 4 * j + 1) * n, :] = tmp_ref[pl.ds(r * q + j, n, stride=4), :].astype(dtype)


def _class16_to_class4(src_ref, dst_ref):
    q, n = SEQ // 4, SEQ // 16
    for r in range(4):
        for j in range(4):
            dst_ref[pl.ds(r * q + j, n, stride=4), :] = src_ref[(r + 4 * j) * n:(r + 4 * j + 1) * n, :]


def _interleave(src_ref, dst_ref, d, tmp_ref, accumulate):
    q = SEQ // 4
    if d == 16:
        _class16_to_class4(src_ref, tmp_ref)
        src_ref = tmp_ref
    else:
        assert d == 4
    for r in range(4):
        rows = pl.ds(r, q, stride=4)
        val = src_ref[r * q:(r + 1) * q, :]
        dst_ref[rows, :] = dst_ref[rows, :] + val if accumulate else val


def a_attn_fwd(proj, ca, sa, sb, nb, xch):
    t = proj.shape[0]
    n_pairs = A_WIDTH // LANES

    def body(q_ref, k_ref, v_ref, c_ref, sa_ref, sb_ref, y_ref, lse_ref, *rest):
        qkv_d, (qr_s, kr_s, oc_s, lc_s, o1_s, l1_s, o2_s, l2_s, o3_s, l3_s, tmp_s) = rest[:9], rest[9:]
        c, s_a, s_b = c_ref[...], sa_ref[...], sb_ref[...]
        qr_s[...] = _rope_fwd(q_ref[...], c, s_a, s_b, A_ROT // 2) * (A_HEAD_DIM ** -0.5)
        kr_s[...] = _rope_fwd(k_ref[...], c, s_a, s_b, A_ROT // 2)
        head0 = lax.broadcasted_iota(jnp.int32, (Q_BLK, LANES), 1) < A_HEAD_DIM
        nat = ((o1_s, l1_s), (o2_s, l2_s), (o3_s, l3_s))

        for g, d in enumerate(DILATIONS):
            qd_s, kd_s, vd_s = qkv_d[3 * g:3 * g + 3]
            _deinterleave(qr_s, qd_s, d, BF16, tmp_s)
            _deinterleave(kr_s, kd_s, d, BF16, tmp_s)
            _deinterleave(v_ref, vd_s, d, BF16, tmp_s)
            o_dst, l_dst = (nat[g] if d == 1 else (oc_s, lc_s))

            def block(idx, carry, d=d, o_dst=o_dst, l_dst=l_dst, qd_s=qd_s, kd_s=kd_s, vd_s=vd_s):
                q0, k0, kwin, valid = _block_coords(idx, d)
                qb = qd_s[pl.ds(q0, Q_BLK), :]
                kb = kd_s[pl.ds(k0, kwin), :]
                vb = vd_s[pl.ds(k0, kwin), :]
                zero = jnp.zeros_like(qb)
                q2 = jnp.concatenate([jnp.where(head0, qb, zero), jnp.where(head0, zero, qb)], 0)
                s = jnp.where(jnp.concatenate([valid, valid], 0), _dot_nt(q2, kb), NEG_INF)
                m = jnp.max(s, axis=-1, keepdims=True)
                p = jnp.exp(s - m)
                l = jnp.sum(p, axis=-1, keepdims=True)
                o2 = _dot(p.astype(BF16), vb) / l
                l2 = m + jnp.log(l)
                o_dst[pl.ds(q0, Q_BLK), :] = jnp.where(head0, o2[:Q_BLK], o2[Q_BLK:])
                l_dst[pl.ds(q0, Q_BLK), :] = jnp.where(head0, l2[:Q_BLK], l2[Q_BLK:])
                return carry

            lax.fori_loop(0, SEQ // Q_BLK, block, 0, unroll=UNROLL_FWD)
            if d > 1:
                _interleave(oc_s, nat[g][0], d, tmp_s, False)
                _interleave(lc_s, nat[g][1], d, tmp_s, False)

        def merge(ci, carry):
            rows = pl.ds(pl.multiple_of(ci * 256, 256), 256)
            l1, l2, l3 = l1_s[rows, :], l2_s[rows, :], l3_s[rows, :]
            m = jnp.maximum(jnp.maximum(l1, l2), l3)
            w1, w2, w3 = jnp.exp(l1 - m), jnp.exp(l2 - m), jnp.exp(l3 - m)
            w = w1 + w2 + w3
            y_ref[rows, :] = (w1 * o1_s[rows, :] + w2 * o2_s[rows, :] + w3 * o3_s[rows, :]) / w
            lse_ref[rows, :] = m + jnp.log(w)
            return carry

        lax.fori_loop(0, SEQ // 256, merge, 0)

    def col(off):
        return pl.BlockSpec((SEQ, LANES), lambda b, hp: (b, off + hp))

    tab = pl.BlockSpec((SEQ, LANES), lambda b, hp: (b, 0))
    out = pl.BlockSpec((SEQ, LANES), lambda b, hp: (b, hp))
    f32s = pltpu.VMEM((SEQ, LANES), F32)
    res, landed = call_hosting_exchange(
        body, xch, grid=(nb, n_pairs),
        in_specs=[col(0), col(n_pairs), col(2 * n_pairs), tab, tab, tab],
        out_specs=[out] * 11,
        out_shape=[SDS((t, A_WIDTH), F32)] * 2 + [SDS((t, A_WIDTH), BF16)] * 9,
        scratch_shapes=[f32s] * 11,
        name="a_attn_fwd", operands=(proj, proj, proj, ca, sa, sb))
    return res[:2], res[2:], landed


def a_attn_bwd(qkv_d, ca, sa, sb, dy, y, lse, nb, xch):
    t = dy.shape[0]
    n_pairs = A_WIDTH // LANES

    def body(*refs):
        qkv_refs = refs[:9]
        (c_ref, sa_ref, sb_ref, do_ref, y_ref, lse_ref, dq_ref, dk_ref, dv_ref,
         l0n_s, l1n_s, d0n_s, d1n_s, dod_s, l0d_s, l1d_s, d0d_s, d1d_s,
         dqc_s, dkc_s, dvc_s, dq4_s, dk4_s, dv4_s, dqn_s, dkn_s, dvn_s, tmp_s) = refs[9:]
        c, s_a, s_b = c_ref[...], sa_ref[...], sb_ref[...]
        head0 = lax.broadcasted_iota(jnp.int32, (Q_BLK, LANES), 1) < A_HEAD_DIM

        def per_head_rows(ci, carry):
            rows = pl.ds(pl.multiple_of(ci * 256, 256), 256)
            h0 = lax.broadcasted_iota(jnp.int32, (256, LANES), 1) < A_HEAD_DIM
            tt = do_ref[rows, :] * y_ref[rows, :]
            d0n_s[rows, :] = jnp.broadcast_to(jnp.sum(jnp.where(h0, tt, 0.0), axis=-1, keepdims=True), (256, LANES))
            d1n_s[rows, :] = jnp.broadcast_to(jnp.sum(jnp.where(h0, 0.0, tt), axis=-1, keepdims=True), (256, LANES))
            l = lse_ref[rows, :]
            lr = pltpu.roll(l, A_HEAD_DIM, 1)
            l0n_s[rows, :] = jnp.where(h0, l, lr)
            l1n_s[rows, :] = jnp.where(h0, lr, l)
            return carry

        lax.fori_loop(0, SEQ // 256, per_head_rows, 0)
        assert DILATIONS == (1, 4, 16)

        for g, d in enumerate(DILATIONS):
            qd_s, kd_s, vd_s = qkv_refs[3 * g:3 * g + 3]
            _deinterleave(do_ref, dod_s, d, BF16, tmp_s)
            if d > 1:
                for src, dst in ((l0n_s, l0d_s), (l1n_s, l1d_s), (d0n_s, d0d_s), (d1n_s, d1d_s)):
                    _deinterleave(src, dst, d, F32, tmp_s)
            l0, l1, d0, d1 = (l0n_s, l1n_s, d0n_s, d1n_s) if d == 1 else (l0d_s, l1d_s, d0d_s, d1d_s)
            dq_dst, dk_dst, dv_dst = {1: (dqn_s, dkn_s, dvn_s), 4: (dq4_s, dk4_s, dv4_s), 16: (dqc_s, dkc_s, dvc_s)}[d]
            dk_dst[...] = jnp.zeros_like(dk_dst)
            dv_dst[...] = jnp.zeros_like(dv_dst)

            def block(idx, carry, d=d, l0=l0, l1=l1, d0=d0, d1=d1, dq_dst=dq_dst, dk_dst=dk_dst, dv_dst=dv_dst,
                      qd_s=qd_s, kd_s=kd_s, vd_s=vd_s):
                q0, k0, kwin, valid = _block_coords(idx, d)
                qrows = pl.ds(q0, Q_BLK)
                krows = pl.ds(k0, kwin)
                qb, dob = qd_s[qrows, :], dod_s[qrows, :]
                kb, vb = kd_s[krows, :], vd_s[krows, :]
                zero = jnp.zeros_like(qb)
                q2 = jnp.concatenate([jnp.where(head0, qb, zero), jnp.where(head0, zero, qb)], 0)
                do2 = jnp.concatenate([jnp.where(head0, dob, zero), jnp.where(head0, zero, dob)], 0)
                wide = lambda x: jnp.concatenate([x] * (kwin // LANES), 1)
                lse2 = wide(jnp.concatenate([l0[qrows, :], l1[qrows, :]], 0))
                dd2 = wide(jnp.concatenate([d0[qrows, :], d1[qrows, :]], 0))
                s = jnp.where(jnp.concatenate([valid, valid], 0), _dot_nt(q2, kb), NEG_INF)
                p = jnp.exp(s - lse2)
                ds = (p * (_dot_nt(do2, vb) - dd2)).astype(BF16)
                dq2 = _dot(ds, kb)
                dq_dst[qrows, :] = jnp.where(head0, dq2[:Q_BLK], dq2[Q_BLK:])
                dk_dst[krows, :] += _dot_tn(ds, q2)
                dv_dst[krows, :] += _dot_tn(p.astype(BF16), do2)
                return carry

            lax.fori_loop(0, SEQ // Q_BLK, block, 0, unroll=UNROLL_BWD)

        for c16, c4, nat in ((dqc_s, dq4_s, dqn_s), (dkc_s, dk4_s, dkn_s), (dvc_s, dv4_s, dvn_s)):
            _class16_to_class4(c16, tmp_s)
            c4[...] = c4[...] + tmp_s[...]
            _interleave(c4, nat, 4, tmp_s, True)

        dq_ref[...] = _rope_bwd(dqn_s[...] * (A_HEAD_DIM ** -0.5), c, s_a, s_b, A_ROT // 2).astype(BF16)
        dk_ref[...] = _rope_bwd(dkn_s[...], c, s_a, s_b, A_ROT // 2).astype(BF16)
        dv_ref[...] = dvn_s[...].astype(BF16)

    tab = pl.BlockSpec((SEQ, LANES), lambda b, hp: (b, 0))
    blk = pl.BlockSpec((SEQ, LANES), lambda b, hp: (b, hp))
    f32s = pltpu.VMEM((SEQ, LANES), F32)
    b16s = pltpu.VMEM((SEQ, LANES), BF16)
    return call_hosting_exchange(
        body, xch, grid=(nb, n_pairs),
        in_specs=[blk] * 9 + [tab, tab, tab, blk, blk, blk],
        out_specs=[blk, blk, blk],
        out_shape=[SDS((t, A_WIDTH), BF16)] * 3,
        scratch_shapes=[f32s] * 4 + [b16s] + [f32s] * 14,
        name="a_attn_bwd", operands=(*qkv_d, ca, sa, sb, dy, y, lse))


MLA_SCALE = (MLA_NOPE + MLA_ROPE) ** -0.5
LOG2E = 1.4426950408889634
MLA_QW = MLA_HEADS * LANES
MLA_KVW = MLA_QW + MLA_WIDTH


def _rms(x, g):
    r = lax.rsqrt(jnp.mean(x * x, axis=-1, keepdims=True) + NORM_EPS)
    return x * r * g, r


def _rms_bwd(dn, x, r, g):
    tg = dn * g
    dx = r * tg - x * (r * r * r) * jnp.mean(tg * x, axis=-1, keepdims=True)
    return dx, jnp.sum(dn * x * r, axis=0, keepdims=True)


def mla_prep_fwd(proj, cm, sma, smb, g_cq, g_ckv, wuq, wkv):
    t = proj.shape[0]
    tm = 1024

    def body(cq_ref, ckv_ref, kr_ref, c_ref, sa_ref, sb_ref, gq_ref, gkv_ref, wuq_ref, wkv_ref, q_ref, k_ref, v_ref):
        c, s_a, s_b = c_ref[...], sa_ref[...], sb_ref[...]
        cqn, _ = _rms(cq_ref[...], gq_ref[...])
        qf = _dot(cqn.astype(BF16), wuq_ref[...])
        ckvn, _ = _rms(ckv_ref[...], gkv_ref[...])
        kvf = _dot(ckvn.astype(BF16), wkv_ref[...])
        krope = _rope_fwd(kr_ref[...], c, s_a, s_b, MLA_ROPE // 2)
        for h in range(MLA_HEADS):
            cols = slice(h * LANES, (h + 1) * LANES)
            q_ref[:, cols] = (_rope_fwd(qf[:, cols], c, s_a, s_b, MLA_ROPE // 2) * (MLA_SCALE * LOG2E)).astype(BF16)
            k_ref[:, cols] = (kvf[:, cols] + krope).astype(BF16)
        v_ref[...] = kvf[:, MLA_QW:].astype(BF16)

    def row(w, j):
        return pl.BlockSpec((tm, w), lambda i: (i, j))

    def full(a):
        return pl.BlockSpec(a.shape, lambda i: (0, 0))

    return pl.pallas_call(
        body, grid=(t // tm,),
        in_specs=[row(256, 4096 // 256), row(128, 4352 // 128), row(128, 4480 // 128), row(128, 0), row(128, 0), row(128, 0),
                  full(g_cq), full(g_ckv), full(wuq), full(wkv)],
        out_specs=[row(MLA_QW, 0), row(MLA_QW, 0), row(MLA_WIDTH, 0)],
        out_shape=[SDS((t, MLA_QW), BF16), SDS((t, MLA_QW), BF16), SDS((t, MLA_WIDTH), BF16)],
        name="mla_prep_fwd", compiler_params=_params("parallel"))(proj, proj, proj, cm, sma, smb, g_cq, g_ckv, wuq, wkv)


def mla_prep_bwd(proj, cm, sma, smb, g_cq, g_ckv, wuq, wkv, dq, dk, dv):
    t = proj.shape[0]
    tm = 1024

    def body(cq_ref, ckv_ref, c_ref, sa_ref, sb_ref, gq_ref, gkv_ref, wuq_ref, wkv_ref, dq_ref, dk_ref, dv_ref,
             dcc_ref, dqf_ref, cqn_ref, dkvf_ref, ckvn_ref, dgq_ref, dgkv_ref):
        @pl.when(pl.program_id(0) == 0)
        def _():
            dgq_ref[...] = jnp.zeros_like(dgq_ref)
            dgkv_ref[...] = jnp.zeros_like(dgkv_ref)

        c, s_a, s_b = c_ref[...], sa_ref[...], sb_ref[...]
        cq, ckv = cq_ref[...], ckv_ref[...]
        cqn, rq = _rms(cq, gq_ref[...])
        ckvn, rkv = _rms(ckv, gkv_ref[...])
        cqn_ref[...] = cqn.astype(BF16)
        ckvn_ref[...] = ckvn.astype(BF16)
        lane = lax.broadcasted_iota(jnp.int32, (tm, LANES), 1)
        rope_lanes = (lane >= MLA_NOPE) & (lane < MLA_NOPE + MLA_ROPE)
        dkrope = jnp.zeros((tm, LANES), F32)
        for h in range(MLA_HEADS):
            cols = slice(h * LANES, (h + 1) * LANES)
            dqf_ref[:, cols] = _rope_bwd(dq_ref[:, cols].astype(F32) * MLA_SCALE, c, s_a, s_b, MLA_ROPE // 2).astype(BF16)
            dkh = dk_ref[:, cols].astype(F32) * (1.0 / LOG2E)
            dkvf_ref[:, cols] = dkh.astype(BF16)
            dkrope = dkrope + dkh
        dkvf_ref[:, MLA_QW:] = dv_ref[...].astype(BF16)
        dkr = _rope_bwd(jnp.where(rope_lanes, dkrope, 0.0), c, s_a, s_b, MLA_ROPE // 2)
        dcqn = _dot_nt(dqf_ref[...], wuq_ref[...])
        dckvn = _dot_nt(dkvf_ref[...], wkv_ref[...])
        dcq, dgq = _rms_bwd(dcqn, cq, rq, gq_ref[...])
        dckv, dgkv = _rms_bwd(dckvn, ckv, rkv, gkv_ref[...])
        dgq_ref[...] += dgq
        dgkv_ref[...] += dgkv
        dcc_ref[:, 0:256] = dcq.astype(BF16)
        dcc_ref[:, 256:384] = dckv.astype(BF16)
        dcc_ref[:, 384:512] = dkr.astype(BF16)

    def row(w, j):
        return pl.BlockSpec((tm, w), lambda i: (i, j))

    def full(a):
        return pl.BlockSpec(a.shape, lambda i: (0, 0))

    return pl.pallas_call(
        body, grid=(t // tm,),
        in_specs=[row(256, 4096 // 256), row(128, 4352 // 128), row(128, 0), row(128, 0), row(128, 0),
                  full(g_cq), full(g_ckv), full(wuq), full(wkv), row(MLA_QW, 0), row(MLA_QW, 0), row(MLA_WIDTH, 0)],
        out_specs=[row(512, 0), row(MLA_QW, 0), row(256, 0), row(MLA_KVW, 0), row(128, 0), full(g_cq), full(g_ckv)],
        out_shape=[SDS((t, 512), BF16), SDS((t, MLA_QW), BF16), SDS((t, 256), BF16), SDS((t, MLA_KVW), BF16),
                   SDS((t, 128), BF16), SDS(g_cq.shape, F32), SDS(g_ckv.shape, F32)],
        name="mla_prep_bwd", compiler_params=_params("arbitrary"))(proj, proj, cm, sma, smb, g_cq, g_ckv, wuq, wkv, dq, dk, dv)


MLA_TQ = SEQ
MLA_SUB_FWD = 512
MLA_SUB_BWD = 256


def mla_attn_fwd(qb, kb, vb, nb):
    t = qb.shape[0]
    nq = SEQ // MLA_TQ
    n_pairs = MLA_HEADS // 2

    def body(q_ref, k_ref, v_ref, y_ref, lse_ref):
        head0 = lax.broadcasted_iota(jnp.int32, (MLA_SUB_FWD, LANES), 1) < MLA_V
        v = v_ref[...]
        vhead0 = lax.broadcasted_iota(jnp.int32, v.shape, 1) < MLA_V
        one = jnp.ones_like(v)
        vh = [jnp.where(vhead0 == (h == 0), v, one) for h in range(2)]
        for sub in range(MLA_TQ // MLA_SUB_FWD):
            rows = slice(sub * MLA_SUB_FWD, (sub + 1) * MLA_SUB_FWD)
            outs, lses = [], []
            for h in range(2):
                cols = slice(h * LANES, (h + 1) * LANES)
                s = _dot_nt(q_ref[rows, cols], k_ref[:, cols])
                m = jnp.max(s, axis=-1, keepdims=True)
                p = jnp.exp2(s - m).astype(BF16)
                ol = _dot(p, vh[h])
                l = pltpu.roll(ol, MLA_V, 1)
                outs.append(ol / l)
                lses.append(m + jnp.log2(l))
            y_ref[rows, :] = jnp.where(head0, outs[0], outs[1])
            lse_ref[rows, :] = jnp.where(head0, lses[0], lses[1])

    return pl.pallas_call(
        body, grid=(nb, n_pairs, nq),
        in_specs=[pl.BlockSpec((MLA_TQ, 2 * LANES), lambda b, hp, i: (b * nq + i, hp)),
                  pl.BlockSpec((SEQ, 2 * LANES), lambda b, hp, i: (b, hp)),
                  pl.BlockSpec((SEQ, LANES), lambda b, hp, i: (b, hp))],
        out_specs=[pl.BlockSpec((MLA_TQ, LANES), lambda b, hp, i: (b * nq + i, hp))] * 2,
        out_shape=[SDS((t, MLA_WIDTH), F32)] * 2,
        name="mla_attn_fwd", compiler_params=_params("parallel", "parallel", "parallel"))(qb, kb, vb)


def mla_attn_bwd(qb, kb, vb, dy, y, lse, nb, xch):
    t = qb.shape[0]
    nq = SEQ // MLA_TQ
    n_pairs = MLA_HEADS // 2

    assert nq == 1

    def body(q_ref, k_ref, v_ref, do_ref, y_ref, lse_ref, dq_ref, dk_ref, dv_ref, dk_s, dv_s):
        dk_s[...] = jnp.zeros_like(dk_s)
        dv_s[...] = jnp.zeros_like(dv_s)
        head0 = lax.broadcasted_iota(jnp.int32, (MLA_SUB_BWD, LANES), 1) < MLA_V
        v = v_ref[...]
        for sub in range(MLA_TQ // MLA_SUB_BWD):
            rows = slice(sub * MLA_SUB_BWD, (sub + 1) * MLA_SUB_BWD)
            do = do_ref[rows, :]
            lse = lse_ref[rows, :]
            tt = do * y_ref[rows, :]
            dv = jnp.zeros((SEQ, LANES), F32)
            for h in range(2):
                sel = head0 if h == 0 else ~head0
                lo = h * MLA_V
                cols = slice(h * LANES, (h + 1) * LANES)
                q = q_ref[rows, cols]
                k = k_ref[:, cols]
                dd = jnp.sum(jnp.where(sel, tt, 0.0), axis=-1, keepdims=True)
                doh = jnp.where(sel, do, 0.0).astype(BF16)
                p = jnp.exp2(_dot_nt(q, k) - lse[:, lo:lo + 1])
                dp = _dot_nt(doh, v)
                ds = (p * (dp - dd)).astype(BF16)
                dq_ref[rows, cols] = _dot(ds, k).astype(dq_ref.dtype)
                dk_s[:, cols] += _dot_tn(ds, q)
                dv = dv + _dot_tn(p.astype(BF16), doh)
            dv_s[...] += dv
        dk_ref[...] = dk_s[...].astype(dk_ref.dtype)
        dv_ref[...] = dv_s[...].astype(dv_ref.dtype)

    qspec = pl.BlockSpec((MLA_TQ, 2 * LANES), lambda b, hp, i: (b * nq + i, hp))
    kspec = pl.BlockSpec((SEQ, 2 * LANES), lambda b, hp, i: (b, hp))
    vspec = pl.BlockSpec((SEQ, LANES), lambda b, hp, i: (b, hp))
    ospec = pl.BlockSpec((MLA_TQ, LANES), lambda b, hp, i: (b * nq + i, hp))
    return call_hosting_exchange(
        body, xch, grid=(nb, n_pairs, nq),
        in_specs=[qspec, kspec, vspec, ospec, ospec, ospec],
        out_specs=[qspec, kspec, vspec],
        out_shape=[SDS((t, MLA_QW), BF16), SDS((t, MLA_QW), BF16), SDS((t, MLA_WIDTH), BF16)],
        scratch_shapes=[pltpu.VMEM((SEQ, 2 * LANES), F32), pltpu.VMEM((SEQ, LANES), F32)],
        name="mla_attn_bwd", operands=(qb, kb, vb, dy, y, lse))


MEM_TQ = SEQ
MEM_SUB = 512
MEM_SCALE = MEM_HEAD_DIM ** -0.5
MQ_BLK4 = 5120 // MEM_WIDTH


def mem_attn_fwd(proj, mkv, nb):
    t = proj.shape[0]
    nq = SEQ // MEM_TQ

    def body(q_ref, mk_ref, mv_ref, y_ref):
        for sub in range(MEM_TQ // MEM_SUB):
            rows = slice(sub * MEM_SUB, (sub + 1) * MEM_SUB)
            for h in range(MEM_HEADS):
                cols = slice(h * LANES, (h + 1) * LANES)
                s = _dot_nt(q_ref[rows, cols].astype(BF16), mk_ref[:, cols]) * MEM_SCALE
                m = jnp.max(s, axis=-1, keepdims=True)
                p = jnp.exp(s - m)
                l = jnp.sum(p, axis=-1, keepdims=True)
                y_ref[rows, cols] = _dot(p.astype(BF16), mv_ref[:, cols]) / l

    return pl.pallas_call(
        body, grid=(nb, nq),
        in_specs=[pl.BlockSpec((MEM_TQ, MEM_WIDTH), lambda b, i: (b * nq + i, MQ_BLK4)),
                  pl.BlockSpec((N_MEM, MEM_WIDTH), lambda b, i: (b, 0)),
                  pl.BlockSpec((N_MEM, MEM_WIDTH), lambda b, i: (b, 1))],
        out_specs=pl.BlockSpec((MEM_TQ, MEM_WIDTH), lambda b, i: (b * nq + i, 0)),
        out_shape=SDS((t, MEM_WIDTH), F32),
        name="mem_attn_fwd", compiler_params=_params("parallel", "parallel"))(proj, mkv, mkv)


def mem_attn_bwd(proj, mkv, dy, nb):
    t = proj.shape[0]
    nq = SEQ // MEM_TQ

    def body(q_ref, mk_ref, mv_ref, do_ref, dq_ref, dmk_ref, dmv_ref):
        @pl.when(pl.program_id(1) == 0)
        def _():
            dmk_ref[...] = jnp.zeros_like(dmk_ref)
            dmv_ref[...] = jnp.zeros_like(dmv_ref)

        for sub in range(MEM_TQ // MEM_SUB):
            rows = slice(sub * MEM_SUB, (sub + 1) * MEM_SUB)
            for h in range(MEM_HEADS):
                cols = slice(h * LANES, (h + 1) * LANES)
                q = q_ref[rows, cols].astype(BF16)
                mk, mv = mk_ref[:, cols], mv_ref[:, cols]
                do = do_ref[rows, cols].astype(BF16)
                s = _dot_nt(q, mk) * MEM_SCALE
                e = jnp.exp(s - jnp.max(s, axis=-1, keepdims=True))
                p = e / jnp.sum(e, axis=-1, keepdims=True)
                dp = _dot_nt(do, mv)
                ds = (p * (dp - jnp.sum(p * dp, axis=-1, keepdims=True)) * MEM_SCALE).astype(BF16)
                dq_ref[rows, cols] = _dot(ds, mk).astype(BF16)
                dmk_ref[:, cols] += _dot_tn(ds, q)
                dmv_ref[:, cols] += _dot_tn(p.astype(BF16), do)

    ospec = pl.BlockSpec((MEM_TQ, MEM_WIDTH), lambda b, i: (b * nq + i, 0))
    kspec = pl.BlockSpec((N_MEM, MEM_WIDTH), lambda b, i: (b, 0))
    return pl.pallas_call(
        body, grid=(nb, nq),
        in_specs=[pl.BlockSpec((MEM_TQ, MEM_WIDTH), lambda b, i: (b * nq + i, MQ_BLK4)),
                  kspec, pl.BlockSpec((N_MEM, MEM_WIDTH), lambda b, i: (b, 1)), ospec],
        out_specs=[ospec, kspec, kspec],
        out_shape=[SDS((t, MEM_WIDTH), BF16), SDS((nb * N_MEM, MEM_WIDTH), F32), SDS((nb * N_MEM, MEM_WIDTH), F32)],
        name="mem_attn_bwd", compiler_params=_params("parallel", "arbitrary"))(proj, mkv, mkv, dy)


ROW_TM = 512
AG_BLK = 3072 // 1024
BG_BLK = 4608 // 512
MG_BLK = 5632 // 512
GROUPS = ((0, A_WIDTH), (A_WIDTH, MLA_WIDTH), (A_WIDTH + MLA_WIDTH, MEM_WIDTH))
D_MIX = 2048


def _gate_specs():
    def row(w, j):
        return pl.BlockSpec((ROW_TM, w), lambda i: (i, j))

    def vec(w):
        return pl.BlockSpec((1, w), lambda i: (0, 0))

    ys = [row(A_WIDTH, 0), row(MLA_WIDTH, 0), row(MEM_WIDTH, 0)]
    gates = [row(A_WIDTH, AG_BLK), row(MLA_WIDTH, BG_BLK), row(MEM_WIDTH, MG_BLK)]
    gains = [vec(A_WIDTH), vec(MLA_WIDTH), vec(MEM_WIDTH)]
    return row, vec, ys, gates, gains


def gate_out_ln_loss(ya, yb, ym, proj, goa, gob, gom, wout, h32, target, gp, bp):
    t, d = h32.shape
    _, _, ys, gates, gains = _gate_specs()

    def body(ya_ref, yb_ref, ym_ref, ga_ref, gb_ref, gm_ref, goa_ref, gob_ref, gom_ref, w_ref, h_ref, t_ref, gp_ref, bp_ref,
             z_ref, du32_ref, du16_ref, loss_ref, dgp_ref, dbp_ref):
        @pl.when(pl.program_id(0) == 0)
        def _():
            loss_ref[...] = jnp.zeros_like(loss_ref)
            dgp_ref[...] = jnp.zeros_like(dgp_ref)
            dbp_ref[...] = jnp.zeros_like(dbp_ref)

        for (off, w), y_ref, g_ref, go_ref in zip(GROUPS, (ya_ref, yb_ref, ym_ref), (ga_ref, gb_ref, gm_ref),
                                                  (goa_ref, gob_ref, gom_ref)):
            n, _ = _rms(y_ref[...], go_ref[...])
            gt = g_ref[...]
            z_ref[:, off:off + w] = (n * (gt * _sigmoid(gt))).astype(BF16)
        g = gp_ref[...]
        u = ALPHA * h_ref[...] + _dot(z_ref[...], w_ref[...])
        mu = jnp.mean(u, axis=-1, keepdims=True)
        uc = u - mu
        rstd = lax.rsqrt(jnp.mean(uc * uc, axis=-1, keepdims=True) + NORM_EPS)
        xhat = uc * rstd
        err = xhat * g + bp_ref[...] - t_ref[...]
        tok = jnp.sum(err * err, axis=-1, keepdims=True) * (1.0 / d)
        loss_ref[...] += 0.5 * jnp.sum(tok, axis=0, keepdims=True)
        dout = err * (1.0 / d)
        dxhat = dout * g
        du = rstd * (dxhat - jnp.mean(dxhat, axis=-1, keepdims=True)
                     - xhat * jnp.mean(dxhat * xhat, axis=-1, keepdims=True))
        du32_ref[...] = du
        du16_ref[...] = du.astype(BF16)
        dgp_ref[...] += jnp.sum(dout * xhat, axis=0, keepdims=True)
        dbp_ref[...] += jnp.sum(dout, axis=0, keepdims=True)

    row = pl.BlockSpec((ROW_TM, d), lambda i: (i, 0))
    vec = pl.BlockSpec((1, d), lambda i: (0, 0))
    zrow = pl.BlockSpec((ROW_TM, D_MIX), lambda i: (i, 0))
    return pl.pallas_call(
        body, grid=(t // ROW_TM,),
        in_specs=ys + gates + gains + [pl.BlockSpec((D_MIX, d), lambda i: (0, 0)), row, row, vec, vec],
        out_specs=[zrow, row, row, pl.BlockSpec((1, LANES), lambda i: (0, 0)), vec, vec],
        out_shape=[SDS((t, D_MIX), BF16), SDS((t, d), F32), SDS((t, d), BF16), SDS((1, LANES), F32), SDS((1, d), F32),
                   SDS((1, d), F32)],
        name="gate_out_ln_loss", compiler_params=_params("arbitrary"))(
            ya, yb, ym, proj, proj, proj, goa, gob, gom, wout, h32, target, gp, bp)


def gate_bwd(du16, wout, ya, yb, ym, proj, goa, gob, gom):
    t = ya.shape[0]
    row, vec, ys, gates, gains = _gate_specs()

    def body(du_ref, w_ref, ya_ref, yb_ref, ym_ref, ga_ref, gb_ref, gm_ref, goa_ref, gob_ref, gom_ref,
             dya_ref, dyb_ref, dym_ref, dga_ref, dgb_ref, dgm_ref, dgoa_ref, dgob_ref, dgom_ref):
        @pl.when(pl.program_id(0) == 0)
        def _():
            dgoa_ref[...] = jnp.zeros_like(dgoa_ref)
            dgob_ref[...] = jnp.zeros_like(dgob_ref)
            dgom_ref[...] = jnp.zeros_like(dgom_ref)

        dz = _dot_nt(du_ref[...], w_ref[...])
        for (off, w), y_ref, g_ref, go_ref, dy_ref, dg_ref, dgo_ref in zip(
                GROUPS, (ya_ref, yb_ref, ym_ref), (ga_ref, gb_ref, gm_ref), (goa_ref, gob_ref, gom_ref),
                (dya_ref, dyb_ref, dym_ref), (dga_ref, dgb_ref, dgm_ref), (dgoa_ref, dgob_ref, dgom_ref)):
            dzg = dz[:, off:off + w]
            y, gt, go = y_ref[...], g_ref[...], go_ref[...]
            n, r = _rms(y, go)
            sg = _sigmoid(gt)
            dg_ref[...] = (dzg * n * (sg * (1.0 + gt * (1.0 - sg)))).astype(BF16)
            dy, dgo = _rms_bwd(dzg * (gt * sg), y, r, go)
            dy_ref[...] = dy
            dgo_ref[...] += dgo

    widths = (A_WIDTH, MLA_WIDTH, MEM_WIDTH)
    return pl.pallas_call(
        body, grid=(t // ROW_TM,),
        in_specs=[row(D_MODEL, 0), pl.BlockSpec((D_MIX, D_MODEL), lambda i: (0, 0))] + ys + gates + gains,
        out_specs=[row(w, 0) for w in widths] * 2 + [vec(w) for w in widths],
        out_shape=[SDS((t, w), F32) for w in widths] + [SDS((t, w), BF16) for w in widths] + [SDS((1, w), F32) for w in widths],
        name="gate_bwd", compiler_params=_params("arbitrary"))(du16, wout, ya, yb, ym, proj, proj, proj, goa, gob, gom)


def dh_ln_bwd(pieces, win_t, du32, x2, g_emb, xch):
    t, d = x2.shape

    def body(*refs):
        p_refs = refs[:len(pieces)]
        w_ref, du_ref, x_ref, g_ref, dx_ref, dg_ref, db_ref = refs[len(pieces):]

        @pl.when(pl.program_id(0) == 0)
        def _():
            dg_ref[...] = jnp.zeros_like(dg_ref)
            db_ref[...] = jnp.zeros_like(db_ref)

        dh = ALPHA * du_ref[...]
        for p_ref, off, w in zip(p_refs, PIECE_OFFS, PIECE_WIDTHS):
            dh = dh + _dot(p_ref[...], w_ref[off:off + w, :])
        x = x_ref[...]
        xc = x - jnp.mean(x, axis=-1, keepdims=True)
        rstd = lax.rsqrt(jnp.mean(xc * xc, axis=-1, keepdims=True) + NORM_EPS)
        xhat = xc * rstd
        dg_ref[...] += jnp.sum(dh * xhat, axis=0, keepdims=True)
        db_ref[...] += jnp.sum(dh, axis=0, keepdims=True)
        tg = dh * g_ref[...]
        dx_ref[...] = rstd * (tg - jnp.mean(tg, axis=-1, keepdims=True)
                              - xhat * jnp.mean(tg * xhat, axis=-1, keepdims=True))

    row = pl.BlockSpec((ROW_TM, d), lambda i: (i, 0))
    vec = pl.BlockSpec((1, d), lambda i: (0, 0))
    return call_hosting_exchange(
        body, xch, grid=(t // ROW_TM,),
        in_specs=[pl.BlockSpec((ROW_TM, w), lambda i: (i, 0)) for w in PIECE_WIDTHS]
        + [pl.BlockSpec(win_t.shape, lambda i: (0, 0)), row, row, vec],
        out_specs=[row, vec, vec],
        out_shape=[SDS((t, d), F32), SDS((1, d), F32), SDS((1, d), F32)],
        scratch_shapes=[], name="dh_ln_bwd", operands=(*pieces, win_t, du32, x2, g_emb))


def _adamw(w, g, m, v):
    m2 = ADAM_B1 * m + (1.0 - ADAM_B1) * g
    v2 = ADAM_B2 * v + (1.0 - ADAM_B2) * (g * g)
    m_hat = m2 / (1.0 - ADAM_B1 ** ADAM_STEP)
    v_hat = v2 / (1.0 - ADAM_B2 ** ADAM_STEP)
    return -ADAM_LR * (m_hat / (jnp.sqrt(v_hat) + ADAM_EPS) + ADAM_WD * w), m2, v2


def adamw_shard(w, parts, m, v, name):
    r, c = w.shape
    if r % 256 == 0 or r * c <= 256 * 1024:
        tr, tc = min(r, 256), c
    else:
        tr, tc = r, 256

    def body(w_ref, p_ref, m_ref, v_ref, g_ref, d_ref, nm_ref, nv_ref):
        g = p_ref[0].astype(F32)
        for k in range(1, N_DEV):
            g = g + p_ref[k].astype(F32)
        g_ref[...] = g
        d_ref[...], nm_ref[...], nv_ref[...] = _adamw(w_ref[...], g, m_ref[...], v_ref[...])

    blk = pl.BlockSpec((tr, tc), lambda i, j: (i, j))
    return pl.pallas_call(
        body, grid=(r // tr, c // tc),
        in_specs=[blk, pl.BlockSpec((N_DEV, tr, tc), lambda i, j: (0, i, j)), blk, blk],
        out_specs=[blk] * 4, out_shape=[SDS((r, c), F32)] * 4, name=name,
        compiler_params=_params("parallel", "parallel"))(w, parts, m, v)


def _place():
    return lax.axis_index("x"), lax.axis_index("y"), lax.axis_index("c")


def _flat(px, py, pc):
    return 4 * px + 2 * py + pc


def _peer(x, y, c, k):
    return (1 - x if k & 4 else x, 1 - y if k & 2 else y, 1 - c if k & 1 else c)


def cast_shards(shards):
    def body(*refs):
        n = len(refs) // 2
        for i_ref, o_ref in zip(refs[:n], refs[n:]):
            o_ref[...] = i_ref[...].astype(BF16)

    return pl.pallas_call(body, out_shape=[SDS(s.shape, BF16) for s in shards], name="cast_shards",
                          compiler_params=_params())(*shards)


def _two_level_gather_plan(src_refs, land_refs, send_sems, recv_sems, local_sems):
    n = len(src_refs)
    x, y, c = _place()
    me, sib = (x, y, c), (x, y, 1 - c)
    chips = [(1 - x, y), (x, 1 - y), (1 - x, 1 - y)]

    def copy(a, k, block, to, src=None):
        dst = land_refs[a].at[_flat(*block)]
        return pltpu.make_async_remote_copy(
            src_ref=dst if src is None else src, dst_ref=dst,
            send_sem=send_sems.at[a * N_DEV + k], recv_sem=recv_sems.at[a * N_DEV + k],
            device_id=to, device_id_type=MESH)

    mine = [pltpu.make_async_copy(src_refs[a], land_refs[a].at[_flat(*me)], local_sems.at[a]) for a in range(n)]
    first = []
    for a in range(n):
        first.append(copy(a, 0, me, sib, src=src_refs[a]))
        first += [copy(a, 1 + j, me, (*chip, c), src=src_refs[a]) for j, chip in enumerate(chips)]

    def start():
        for cp in mine + first:
            cp.start()

    def finish():
        passed = []
        for j, chip in enumerate(chips):
            for a in range(n):
                copy(a, 1 + j, (*chip, c), me).wait_recv()
                fwd = copy(a, 4 + j, (*chip, c), sib)
                fwd.start()
                passed.append(fwd)
        for a in range(n):
            copy(a, 0, sib, me).wait_recv()
            for j, chip in enumerate(chips):
                copy(a, 4 + j, (*chip, 1 - c), me).wait_recv()
        for cp in first + passed:
            cp.wait_send()
        for cp in mine:
            cp.wait()

    return start, finish


ALL_DEVICES = tuple(range(N_DEV))


def _exchange_plan(src_refs, land_refs, dests, send_sems, recv_sems, local_sems):
    x, y, c = _place()
    me = _flat(x, y, c)
    plan = []
    for a, (src, land, dl) in enumerate(zip(src_refs, land_refs, dests)):
        for li, j in enumerate(dl):
            to = ((j >> 2) & 1, (j >> 1) & 1, j & 1)
            block = src.at[li] if len(src.shape) == len(land.shape) else src

            def push(slot, a=a, block=block, land=land, j=j, to=to):
                return pltpu.make_async_remote_copy(
                    src_ref=block, dst_ref=land.at[slot], send_sem=send_sems.at[a * N_DEV + j],
                    recv_sem=recv_sems.at[a * N_DEV + slot], device_id=to, device_id_type=MESH)

            own = pltpu.make_async_copy(block, land.at[j], local_sems.at[a])
            plan.append((j, push(me), own, [push(s) for s in range(N_DEV) if s != j]))
    return me, plan


def _exchange_start(me, plan):
    for j, send, own, _ in plan:
        @pl.when(me != j)
        def _(send=send):
            send.start()

        @pl.when(me == j)
        def _(own=own):
            own.start()


def _exchange_wait(me, plan):
    for j, send, own, arrivals in plan:
        @pl.when(me != j)
        def _(send=send):
            send.wait_send()

        @pl.when(me == j)
        def _(own=own, arrivals=arrivals):
            own.wait()
            for arrival in arrivals:
                arrival.wait_recv()


def call_hosting_exchange(core, xch, *, grid, in_specs, out_specs, out_shape, scratch_shapes, name, operands):
    srcs, dests, landing = xch
    n, n_in, n_out, n_scr = len(srcs), len(in_specs), len(out_specs), len(scratch_shapes)

    def body(*refs):
        ins, src_refs = refs[:n_in], refs[n_in:n_in + n]
        outs = refs[n_in + 2 * n:n_in + 2 * n + n_out]
        land_refs = refs[n_in + 2 * n + n_out:n_in + 3 * n + n_out]
        scratch = refs[n_in + 3 * n + n_out:n_in + 3 * n + n_out + n_scr]
        sems = refs[n_in + 3 * n + n_out + n_scr:]
        first = functools.reduce(jnp.logical_and, [pl.program_id(i) == 0 for i in range(len(grid))])
        last = functools.reduce(jnp.logical_and, [pl.program_id(i) == grid[i] - 1 for i in range(len(grid))])
        if dests is None:
            start, finish = _two_level_gather_plan(src_refs, land_refs, *sems)
        else:
            me, plan = _exchange_plan(src_refs, land_refs, dests, *sems)
            start, finish = functools.partial(_exchange_start, me, plan), functools.partial(_exchange_wait, me, plan)
        pl.when(first)(start)
        core(*ins, *outs, *scratch)
        pl.when(last)(finish)

    hbm = pl.BlockSpec(memory_space=pl.ANY)
    res = pl.pallas_call(
        body, grid=grid,
        in_specs=list(in_specs) + [hbm] * (2 * n), out_specs=list(out_specs) + [hbm] * n,
        out_shape=list(out_shape) + [SDS(l.shape, l.dtype) for l in landing],
        scratch_shapes=list(scratch_shapes) + [pltpu.SemaphoreType.DMA((N_DEV * n,)), pltpu.SemaphoreType.DMA((N_DEV * n,)),
                                               pltpu.SemaphoreType.DMA((n,))],
        input_output_aliases={n_in + n + k: n_out + k for k in range(n)},
        name=name, compiler_params=_params(*(("arbitrary",) * len(grid))))(*operands, *srcs, *landing)
    return res[:n_out], res[n_out:]


SLOT_ROWS = 8


def small_allreduce_adamw(loss_sum, grads, ws, ms, vs):
    n = len(grads)
    rows = [g.shape[0] for g in grads]
    total = SLOT_ROWS * (n + 1)

    def body(*refs):
        loss_ref, g_refs, w_refs = refs[0], refs[1:1 + n], refs[1 + n:1 + 2 * n]
        m_refs, v_refs = refs[1 + 2 * n:1 + 3 * n], refs[1 + 3 * n:1 + 4 * n]
        outs = refs[1 + 4 * n:2 + 8 * n]
        vec, gath, tot, send_sems, recv_sems = refs[2 + 8 * n:]
        x, y, c = _place()
        me = _flat(x, y, c)
        vec[...] = jnp.zeros_like(vec)
        vec[0:1, :] = loss_ref[...]
        for i in range(n):
            vec[SLOT_ROWS * (i + 1):SLOT_ROWS * (i + 1) + rows[i], :] = g_refs[i][...]
        gath[me] = vec[...]
        copies = []
        for k in range(1, N_DEV):
            peer = _peer(x, y, c, k)
            copies.append(pltpu.make_async_remote_copy(
                src_ref=vec, dst_ref=gath.at[me], send_sem=send_sems.at[k - 1], recv_sem=recv_sems.at[k - 1],
                device_id=peer, device_id_type=MESH))
        for cp in copies:
            cp.start()
        for cp in copies:
            cp.wait_recv()
        for cp in copies:
            cp.wait_send()
        g = gath[0]
        for j in range(1, N_DEV):
            g = g + gath[j]
        tot[...] = g
        outs[0][...] = tot[0:1, :]
        for i in range(n):
            gi = tot[SLOT_ROWS * (i + 1):SLOT_ROWS * (i + 1) + rows[i], :]
            outs[1 + i][...] = gi
            outs[1 + n + i][...], outs[1 + 2 * n + i][...], outs[1 + 3 * n + i][...] = _adamw(
                w_refs[i][...], gi, m_refs[i][...], v_refs[i][...])

    shapes = [SDS(g.shape, F32) for g in grads]
    return pl.pallas_call(
        body, out_shape=[SDS((1, LANES), F32)] + shapes * 4,
        scratch_shapes=[pltpu.VMEM((total, LANES), F32), pltpu.VMEM((N_DEV, total, LANES), F32), pltpu.VMEM((total, LANES), F32),
                        pltpu.SemaphoreType.DMA((7,)), pltpu.SemaphoreType.DMA((7,))],
        name="small_allreduce_adamw", compiler_params=_params())(loss_sum, *grads, *ws, *ms, *vs)


def _rope_lane_patterns():
    inv = lambda r: ROPE_THETA ** (-(jnp.arange(0, r, 2, dtype=F32) / r))
    z = lambda n: jnp.zeros((n,), F32)
    o = lambda n: jnp.ones((n,), F32)
    half, rest = A_ROT // 2, A_HEAD_DIM - A_ROT
    ia, im = inv(A_ROT), inv(MLA_ROPE)
    mh, tail = MLA_ROPE // 2, LANES - MLA_NOPE - MLA_ROPE
    rows = [jnp.tile(jnp.concatenate([ia, ia, z(rest)]), 2),
            jnp.tile(jnp.concatenate([o(half), z(half + rest)]), 2),
            jnp.tile(jnp.concatenate([z(half), o(half), z(rest)]), 2),
            jnp.concatenate([z(MLA_NOPE), im, im, z(tail)]),
            jnp.concatenate([z(MLA_NOPE), o(mh), z(mh + tail)]),
            jnp.concatenate([z(MLA_NOPE + mh), o(mh), z(tail)]),
            z(LANES), z(LANES)]
    return jnp.stack(rows)


KR_LO, KR_HI = 4480, 4512
W_IN_SHARD = D_IN // N_DEV
BG_SPLIT = 6 * W_IN_SHARD - KR_HI


def w_in_working_t(g):
    pad_lo, pad_hi = MLA_NOPE, LANES - MLA_NOPE - MLA_ROPE
    spans = []
    for lo, hi, shift in ((0, KR_LO, 0), (KR_LO, KR_HI, pad_lo), (KR_HI, D_IN, pad_lo + pad_hi)):
        r = lo
        while r < hi:
            j = r // W_IN_SHARD
            n = min(hi, (j + 1) * W_IN_SHARD) - r
            spans.append((j, r - j * W_IN_SHARD, n, r + shift))
            r += n

    def body(g_ref, o_ref):
        o_ref[KR_LO:KR_LO + pad_lo, :] = jnp.zeros((pad_lo, D_MODEL), o_ref.dtype)
        o_ref[KR_HI + pad_lo:KR_HI + pad_lo + pad_hi, :] = jnp.zeros((pad_hi, D_MODEL), o_ref.dtype)
        for j, src, n, dst in spans:
            o_ref[dst:dst + n, :] = g_ref[j, src:src + n, :]

    return pl.pallas_call(body, out_shape=SDS((D_INW, D_MODEL), g.dtype), name="w_in_working_t", compiler_params=_params())(g)


def _w_in_shard_5(d_ag_tail, d_cc, d_bg_head):
    kr = MLA_Q_RANK + MLA_KV_RANK + MLA_NOPE
    rows = jnp.concatenate([d_ag_tail, d_cc[:MLA_Q_RANK + MLA_KV_RANK], d_cc[kr:kr + MLA_ROPE], d_bg_head], 0)
    return rows.reshape(1, W_IN_SHARD, D_MODEL).astype(BF16)


def _w_uq_working(g):
    w = jnp.pad(g.transpose(1, 0, 2), ((0, 0), (0, 0), (0, LANES - MLA_NOPE - MLA_ROPE)))
    return w.reshape(MLA_Q_RANK, MLA_QW)


def _w_uq_parts(dw):
    return dw.reshape(MLA_Q_RANK, MLA_HEADS, LANES)[:, :, :MLA_NOPE + MLA_ROPE].transpose(1, 0, 2)


def _w_ukv_working(g):
    wk = jnp.pad(g[:, :, :MLA_NOPE].transpose(1, 0, 2), ((0, 0), (0, 0), (0, LANES - MLA_NOPE)))
    wv = g[:, :, MLA_NOPE:].transpose(1, 0, 2)
    return jnp.concatenate([wk.reshape(MLA_KV_RANK, MLA_QW), wv.reshape(MLA_KV_RANK, MLA_WIDTH)], 1)


def _w_ukv_parts(dw):
    dk = dw[:, :MLA_QW].reshape(MLA_KV_RANK, MLA_HEADS, LANES)[:, :, :MLA_NOPE]
    dv = dw[:, MLA_QW:].reshape(MLA_KV_RANK, MLA_HEADS, MLA_V)
    return jnp.concatenate([dk, dv], -1).transpose(1, 0, 2)


SMALL_NAMES = ("g_emb", "b_emb", "g_cq", "g_ckv", "g_out_a", "g_out_b", "g_out_m", "g_post", "b_post")


def kernel(x, mem, positions, g_emb, b_emb, w_in, g_cq, g_ckv, w_uq, w_ukv, w_mem_kv, g_out_a, g_out_b, g_out_m, w_out, g_post, b_post, loss_target, m_g_emb, m_b_emb, m_w_in, m_g_cq, m_g_ckv, m_w_uq, m_w_ukv, m_w_mem_kv, m_g_out_a, m_g_out_b, m_g_out_m, m_w_out, m_g_post, m_b_post, v_g_emb, v_b_emb, v_w_in, v_g_cq, v_g_ckv, v_w_uq, v_w_ukv, v_w_mem_kv, v_g_out_a, v_g_out_b, v_g_out_m, v_w_out, v_g_post, v_b_post):
    nb = x.shape[0]
    t = nb * SEQ
    x2 = x.reshape(t, D_MODEL)
    tgt2 = loss_target.reshape(t, D_MODEL)
    mem2 = mem.reshape(nb * N_MEM, D_MODEL)
    g_emb2, b_emb2 = g_emb.reshape(1, -1), b_emb.reshape(1, -1)

    w_in_t, m_w_in_t, v_w_in_t = w_in[0].T, m_w_in[0].T, v_w_in[0].T
    s_in, s_uq, s_ukv, s_mem, s_out = cast_shards((w_in_t, w_uq[0], w_ukv[0], w_mem_kv[0], w_out[0]))
    (h32, h16, (a_c, a_sa, a_sb), (m_c, m_sa, m_sb)), (g_in,) = embed_fwd(
        x2, g_emb2, b_emb2, positions, ((s_in,), None, (lax.empty((N_DEV,) + s_in.shape, BF16),)))
    win_t = w_in_working_t(g_in)

    proj = mm_nn(h16, win_t, F32, 1024, 1536, "proj", rhs_transposed=True)
    later = (s_uq, s_ukv, s_mem, s_out)
    (ya, lse_a), qkv_d, (g_uq, g_ukv, g_mem, g_out) = a_attn_fwd(
        proj, a_c, a_sa, a_sb, nb,
        (later, (ALL_DEVICES,) * len(later), tuple(lax.empty((N_DEV,) + w.shape, BF16) for w in later)))
    wuq_w = _w_uq_working(g_uq)
    wkv_w = _w_ukv_working(g_ukv)
    wmem = g_mem.reshape(D_MODEL, 2 * MEM_WIDTH)
    wout = g_out.reshape(D_MIX, D_MODEL)
    qb, kb, vb = mla_prep_fwd(proj, m_c, m_sa, m_sb, g_cq, g_ckv, wuq_w, wkv_w)
    yb, lse_b = mla_attn_fwd(qb, kb, vb, nb)
    mkv = mm_nn(mem2, wmem, BF16, nb * N_MEM, 512, "mem_kv")
    ym = mem_attn_fwd(proj, mkv, nb)
    z, du32, du16, loss_sum, dg_post, db_post = gate_out_ln_loss(
        ya, yb, ym, proj, g_out_a, g_out_b, g_out_m, wout, h32, tgt2, g_post, b_post)

    dya, dyb, dym, dag, dbg, dmg, dg_out_a, dg_out_b, dg_out_m = gate_bwd(
        du16, wout, ya, yb, ym, proj, g_out_a, g_out_b, g_out_m)
    dw_out = mm_tn(z, du16, 1024, "dw_out")
    dmq, dmk, dmv = mem_attn_bwd(proj, mkv, dym, nb)
    dw_mem = mm_tn(mem2, jnp.concatenate([dmk, dmv], 1), nb * N_MEM, "dw_mem")
    d_gates, shards_6_7 = mm_tn_group((dbg, dmq, dmg), h16, 2048, "dw_in_bg_mq_mg", W_IN_SHARD, BG_SPLIT, 2)
    landing = lambda w, dtype=F32: lax.empty((N_DEV,) + w.shape, dtype)
    big_w = (w_in_t, w_uq[0], w_ukv[0], w_mem_kv[0], w_out[0])
    (daq, dak, dav), (p_out, p_mem, p_in) = a_attn_bwd(
        qkv_d, a_c, a_sa, a_sb, dya, ya, lse_a, nb,
        ((dw_out.reshape(N_DEV, D_MIX // N_DEV, D_MODEL), dw_mem.reshape(N_DEV, D_MODEL // N_DEV, 2 * MEM_WIDTH),
          shards_6_7),
         (ALL_DEVICES, ALL_DEVICES, (6, 7)),
         (landing(w_out[0]), landing(w_mem_kv[0]), landing(w_in_t, BF16))))
    d_a, shards_0_4 = mm_tn_group((daq, dak, dav, dag), h16, 1024, "dw_in_aq_ak_av_ag", W_IN_SHARD, 0, 5)
    (dqb, dkb, dvb), (p_in,) = mla_attn_bwd(
        qb, kb, vb, dyb, yb, lse_b, nb, ((shards_0_4,), ((0, 1, 2, 3, 4),), (p_in,)))
    dcc, dqf, cqn, dkvf, ckvn, dg_cq, dg_ckv = mla_prep_bwd(proj, m_c, m_sa, m_sb, g_cq, g_ckv, wuq_w, wkv_w, dqb, dkb, dvb)
    dw_uq = mm_tn(cqn, dqf, 2048, "dw_uq")
    dw_ukv = mm_tn(ckvn, dkvf, 2048, "dw_ukv")
    d_cc = mm_tn(dcc, h16, 2048, "dw_in_cc")
    pieces = (daq, dak, dav, dag, dcc, dbg, dmq, dmg)
    (grad_x, dg_emb, db_emb), (p_in, p_uq, p_ukv) = dh_ln_bwd(
        pieces, win_t, du32, x2, g_emb2,
        ((_w_in_shard_5(d_a[5 * W_IN_SHARD:], d_cc, d_gates[:BG_SPLIT]), _w_uq_parts(dw_uq), _w_ukv_parts(dw_ukv)),
         ((5,), ALL_DEVICES, ALL_DEVICES),
         (p_in, landing(w_uq[0]), landing(w_ukv[0]))))

    parts = (p_in, p_uq, p_ukv, p_mem, p_out)
    big_m = (m_w_in_t, m_w_uq[0], m_w_ukv[0], m_w_mem_kv[0], m_w_out[0])
    big_v = (v_w_in_t, v_w_uq[0], v_w_ukv[0], v_w_mem_kv[0], v_w_out[0])
    big = {}
    for name, w, p, m, v in zip(("w_in", "w_uq", "w_ukv", "w_mem_kv", "w_out"), big_w, parts, big_m, big_v):
        res = adamw_shard(w, p, m, v, "adamw_" + name)
        big[name] = [(o.T if name == "w_in" else o)[None] for o in res]

    small_w = (g_emb, b_emb, g_cq, g_ckv, g_out_a, g_out_b, g_out_m, g_post, b_post)
    small_m = (m_g_emb, m_b_emb, m_g_cq, m_g_ckv, m_g_out_a, m_g_out_b, m_g_out_m, m_g_post, m_b_post)
    small_v = (v_g_emb, v_b_emb, v_g_cq, v_g_ckv, v_g_out_a, v_g_out_b, v_g_out_m, v_g_post, v_b_post)
    small_g = (dg_emb, db_emb, dg_cq, dg_ckv, dg_out_a, dg_out_b, dg_out_m, dg_post, db_post)
    rows128 = lambda vals: [v.reshape(-1, LANES) for v in vals]
    res = small_allreduce_adamw(loss_sum, rows128(small_g), rows128(small_w), rows128(small_m), rows128(small_v))
    loss = res[0][0, 0]
    n_small = len(small_w)
    sg, sd, sm, sv = [[r.reshape(w.shape) for r, w in zip(res[1 + k * n_small:1 + (k + 1) * n_small], small_w)]
                      for k in range(4)]

    order = ("g_emb", "b_emb", "w_in", "g_cq", "g_ckv", "w_uq", "w_ukv", "w_mem_kv", "g_out_a", "g_out_b", "g_out_m",
             "w_out", "g_post", "b_post")
    small_idx = {n: i for i, n in enumerate(SMALL_NAMES)}
    outs = [loss, grad_x.reshape(x.shape)]
    for kind in range(4):
        for name in order:
            outs.append(big[name][kind] if name in big else (sg, sd, sm, sv)[kind][small_idx[name]])
    return tuple(outs)
```

```python
import functools

import jax
import jax.numpy as jnp
from jax import lax
from jax.experimental import pallas as pl
from jax.experimental.pallas import tpu as pltpu

F32 = jnp.float32
BF16 = jnp.bfloat16
SDS = jax.ShapeDtypeStruct
MESH = pl.DeviceIdType.MESH

D_MODEL = 1024
SEQ = 2048
A_HEADS, A_HEAD_DIM, A_ROT = 16, 64, 16
A_WIDTH = 1024
DILATIONS = (1, 4, 16)
N_SIDE = 64
MLA_HEADS, MLA_Q_RANK, MLA_KV_RANK = 8, 256, 128
MLA_NOPE, MLA_ROPE, MLA_V = 64, 32, 64
MLA_WIDTH = 512
N_MEM, MEM_HEADS, MEM_HEAD_DIM, MEM_WIDTH = 256, 4, 128, 512
ROPE_THETA = 500000.0
NORM_EPS = 1e-5
NEG_INF = -1e30
ALPHA = 2.0 ** 0.25
D_IN = 6048
N_DEV = 8

ADAM_LR, ADAM_B1, ADAM_B2, ADAM_EPS, ADAM_WD, ADAM_STEP = 0.001, 0.9, 0.999, 1e-08, 0.01, 10

D_INW = 6144
PIECE_WIDTHS = (1024, 1024, 1024, 1024, 512, 512, 512, 512)
PIECE_OFFS = (0, 1024, 2048, 3072, 4096, 4608, 5120, 5632)
LANES = 128
VMEM_LIMIT = 56 * 1024 * 1024


def _params(*sem):
    kw = dict(vmem_limit_bytes=VMEM_LIMIT)
    if sem:
        kw["dimension_semantics"] = sem
    return pltpu.CompilerParams(**kw)


def _dot(a, b):
    return jnp.dot(a, b, preferred_element_type=F32)


def _dot_nt(a, b):
    return lax.dot_general(a, b, (((1,), (1,)), ((), ())), preferred_element_type=F32)


def _dot_tn(a, b):
    return lax.dot_general(a, b, (((0,), (0,)), ((), ())), preferred_element_type=F32)


def _sigmoid(x):
    return 1.0 / (1.0 + jnp.exp(-x))


def _rope_fwd(x, c, sa, sb, half):
    n = x.shape[-1]
    return x * c + pltpu.roll(x, n - half, 1) * sa + pltpu.roll(x, half, 1) * sb


def _rope_bwd(dy, c, sa, sb, half):
    n = dy.shape[-1]
    return dy * c + pltpu.roll(dy * sa, half, 1) + pltpu.roll(dy * sb, n - half, 1)


def mm_nn(a, b, out_dtype, tm, tn, name, rhs_transposed=False):
    m, k = a.shape
    n = b.shape[0] if rhs_transposed else b.shape[1]
    dot = _dot_nt if rhs_transposed else _dot

    def body(a_ref, b_ref, o_ref):
        o_ref[...] = dot(a_ref[...].astype(BF16), b_ref[...].astype(BF16)).astype(o_ref.dtype)

    b_spec = pl.BlockSpec((tn, k), lambda j, i: (j, 0)) if rhs_transposed else pl.BlockSpec((k, tn), lambda j, i: (0, j))
    return pl.pallas_call(
        body, grid=(n // tn, m // tm),
        in_specs=[pl.BlockSpec((tm, k), lambda j, i: (i, 0)), b_spec],
        out_specs=pl.BlockSpec((tm, tn), lambda j, i: (i, j)),
        out_shape=SDS((m, n), out_dtype), name=name,
        compiler_params=_params("parallel", "parallel"))(a, b)


def mm_tn(a, b, tt, name):
    t, m = a.shape
    n = b.shape[1]

    def body(a_ref, b_ref, o_ref):
        @pl.when(pl.program_id(0) == 0)
        def _():
            o_ref[...] = jnp.zeros_like(o_ref)

        o_ref[...] += _dot_tn(a_ref[...].astype(BF16), b_ref[...].astype(BF16))

    return pl.pallas_call(
        body, grid=(t // tt,),
        in_specs=[pl.BlockSpec((tt, m), lambda i: (i, 0)), pl.BlockSpec((tt, n), lambda i: (i, 0))],
        out_specs=pl.BlockSpec((m, n), lambda i: (0, 0)),
        out_shape=SDS((m, n), F32), name=name,
        compiler_params=_params("arbitrary"))(a, b)


def mm_tn_group(pieces, b, tt, name, slab_rows, first_slab_row, n_slabs):
    n, (t, w), cols = len(pieces), pieces[0].shape, b.shape[1]
    nt = t // tt

    def body(*refs):
        p_refs, b_ref, o_ref, slab_ref = refs[:n], refs[n], refs[n + 1], refs[n + 2]

        @pl.when(pl.program_id(1) == 0)
        def _():
            o_ref[...] = jnp.zeros_like(o_ref)

        for k in range(n):
            @pl.when(pl.program_id(0) == k)
            def _(k=k):
                o_ref[...] += _dot_tn(p_refs[k][...], b_ref[...])

            @pl.when((pl.program_id(0) == k) & (pl.program_id(1) == nt - 1))
            def _(k=k):
                for j in range(n_slabs):
                    lo = max(k * w, first_slab_row + j * slab_rows)
                    hi = min((k + 1) * w, first_slab_row + (j + 1) * slab_rows)
                    if lo < hi:
                        dst = lo - first_slab_row - j * slab_rows
                        slab_ref[j, dst:dst + hi - lo, :] = o_ref[lo - k * w:hi - k * w, :].astype(slab_ref.dtype)

    def piece_spec(k):
        return pl.BlockSpec((tt, w), lambda p, i: (jnp.where(p < k, 0, jnp.where(p > k, nt - 1, i)), 0))

    return pl.pallas_call(
        body, grid=(n, nt),
        in_specs=[piece_spec(k) for k in range(n)] + [pl.BlockSpec((tt, cols), lambda p, i: (i, 0))],
        out_specs=[pl.BlockSpec((w, cols), lambda p, i: (p, 0)),
                   pl.BlockSpec((n_slabs, slab_rows, cols), lambda p, i: (0, 0, 0))],
        out_shape=[SDS((n * w, cols), F32), SDS((n_slabs, slab_rows, cols), BF16)], name=name,
        compiler_params=_params("arbitrary", "arbitrary"))(*pieces, b)


def embed_fwd(x2, g, b, positions, xch):
    t, d = x2.shape
    tm = 512
    pos = positions.astype(F32).reshape(-1, 1)

    def body(x_ref, g_ref, b_ref, pos_ref, pat_ref, h32_ref, h16_ref, *tabs):
        x = x_ref[...]
        mu = jnp.mean(x, axis=-1, keepdims=True)
        xc = x - mu
        var = jnp.mean(xc * xc, axis=-1, keepdims=True)
        h = xc * lax.rsqrt(var + NORM_EPS) * g_ref[...] + b_ref[...]
        h32_ref[...] = h
        h16_ref[...] = h.astype(BF16)
        p = pos_ref[...]
        for k in range(2):
            inv, first, second = pat_ref[3 * k:3 * k + 1, :], pat_ref[3 * k + 1:3 * k + 2, :], pat_ref[3 * k + 2:3 * k + 3, :]
            ang = p * inv
            sn = jnp.sin(ang)
            tabs[3 * k][...] = jnp.where(first + second > 0.0, jnp.cos(ang), 1.0)
            tabs[3 * k + 1][...] = -first * sn
            tabs[3 * k + 2][...] = second * sn

    row = pl.BlockSpec((tm, d), lambda i: (i, 0))
    vec = pl.BlockSpec((1, d), lambda i: (0, 0))
    tab = pl.BlockSpec((tm, LANES), lambda i: (i, 0))
    res, landed = call_hosting_exchange(
        body, xch, grid=(t // tm,),
        in_specs=[row, vec, vec, pl.BlockSpec((tm, 1), lambda i: (i, 0)), pl.BlockSpec((8, LANES), lambda i: (0, 0))],
        out_specs=[row, row] + [tab] * 6,
        out_shape=[SDS((t, d), F32), SDS((t, d), BF16)] + [SDS((t, LANES), F32)] * 6,
        scratch_shapes=[], name="embed_fwd", operands=(x2, g, b, pos, _rope_lane_patterns()))
    return (res[0], res[1], tuple(res[2:5]), tuple(res[5:8])), landed


Q_BLK = 128
UNROLL_FWD = 16
UNROLL_BWD = 16


def _pattern_geometry(d):
    length = SEQ // d
    nblk = length // Q_BLK
    kwin = min(2 * Q_BLK, length)
    return length, nblk, kwin


def _block_coords(idx, d):
    length, nblk, kwin = _pattern_geometry(d)
    r = lax.shift_right_logical(idx, nblk.bit_length() - 1)
    i = idx & (nblk - 1)
    q0 = pl.multiple_of(r * length + i * Q_BLK, Q_BLK)
    ks = jnp.clip(i * Q_BLK - N_SIDE, 0, length - kwin)
    k0 = pl.multiple_of(r * length + ks, N_SIDE)
    qpos = i * Q_BLK + lax.broadcasted_iota(jnp.int32, (Q_BLK, kwin), 0)
    kpos = ks + lax.broadcasted_iota(jnp.int32, (Q_BLK, kwin), 1)
    valid = jnp.abs(kpos - qpos) <= N_SIDE
    return q0, k0, kwin, valid


def _deinterleave(src_ref, dst_ref, d, dtype, tmp_ref):
    if d == 1:
        dst_ref[...] = src_ref[...].astype(dtype)
        return
    q = SEQ // 4
    if d == 4:
        for r in range(4):
            dst_ref[r * q:(r + 1) * q, :] = src_ref[pl.ds(r, q, stride=4), :].astype(dtype)
        return
    assert d == 16
    n = SEQ // 16
    for r in range(4):
        tmp_ref[r * q:(r + 1) * q, :] = src_ref[pl.ds(r, q, stride=4), :]
    for r in range(4):
        for j in range(4):
            dst_ref[(r + 4 * j) * n:(r + 4 * j + 1) * n, :] = tmp_ref[pl.ds(r * q + j, n, stride=4), :].astype(dtype)


def _class16_to_class4(src_ref, dst_ref):
    q, n = SEQ // 4, SEQ // 16
    for r in range(4):
        for j in range(4):
            dst_ref[pl.ds(r * q + j, n, stride=4), :] = src_ref[(r + 4 * j) * n:(r + 4 * j + 1) * n, :]


def _interleave(src_ref, dst_ref, d, tmp_ref, accumulate):
    q = SEQ // 4
    if d == 16:
        _class16_to_class4(src_ref, tmp_ref)
        src_ref = tmp_ref
    else:
        assert d == 4
    for r in range(4):
        rows = pl.ds(r, q, stride=4)
        val = src_ref[r * q:(r + 1) * q, :]
        dst_ref[rows, :] = dst_ref[rows, :] + val if accumulate else val


def a_attn_fwd(proj, ca, sa, sb, nb, xch):
    t = proj.shape[0]
    n_pairs = A_WIDTH // LANES

    def body(q_ref, k_ref, v_ref, c_ref, sa_ref, sb_ref, y_ref, lse_ref, *rest):
        qkv_d, (qr_s, kr_s, oc_s, lc_s, o1_s, l1_s, o2_s, l2_s, o3_s, l3_s, tmp_s) = rest[:9], rest[9:]
        c, s_a, s_b = c_ref[...], sa_ref[...], sb_ref[...]
        qr_s[...] = _rope_fwd(q_ref[...], c, s_a, s_b, A_ROT // 2) * (A_HEAD_DIM ** -0.5)
        kr_s[...] = _rope_fwd(k_ref[...], c, s_a, s_b, A_ROT // 2)
        head0 = lax.broadcasted_iota(jnp.int32, (Q_BLK, LANES), 1) < A_HEAD_DIM
        nat = ((o1_s, l1_s), (o2_s, l2_s), (o3_s, l3_s))

        for g, d in enumerate(DILATIONS):
            qd_s, kd_s, vd_s = qkv_d[3 * g:3 * g + 3]
            _deinterleave(qr_s, qd_s, d, BF16, tmp_s)
            _deinterleave(kr_s, kd_s, d, BF16, tmp_s)
            _deinterleave(v_ref, vd_s, d, BF16, tmp_s)
            o_dst, l_dst = (nat[g] if d == 1 else (oc_s, lc_s))

            def block(idx, carry, d=d, o_dst=o_dst, l_dst=l_dst, qd_s=qd_s, kd_s=kd_s, vd_s=vd_s):
                q0, k0, kwin, valid = _block_coords(idx, d)
                qb = qd_s[pl.ds(q0, Q_BLK), :]
                kb = kd_s[pl.ds(k0, kwin), :]
                vb = vd_s[pl.ds(k0, kwin), :]
                zero = jnp.zeros_like(qb)
                q2 = jnp.concatenate([jnp.where(head0, qb, zero), jnp.where(head0, zero, qb)], 0)
                s = jnp.where(jnp.concatenate([valid, valid], 0), _dot_nt(q2, kb), NEG_INF)
                m = jnp.max(s, axis=-1, keepdims=True)
                p = jnp.exp(s - m)
                l = jnp.sum(p, axis=-1, keepdims=True)
                o2 = _dot(p.astype(BF16), vb) / l
                l2 = m + jnp.log(l)
                o_dst[pl.ds(q0, Q_BLK), :] = jnp.where(head0, o2[:Q_BLK], o2[Q_BLK:])
                l_dst[pl.ds(q0, Q_BLK), :] = jnp.where(head0, l2[:Q_BLK], l2[Q_BLK:])
                return carry

            lax.fori_loop(0, SEQ // Q_BLK, block, 0, unroll=UNROLL_FWD)
            if d > 1:
                _interleave(oc_s, nat[g][0], d, tmp_s, False)
                _interleave(lc_s, nat[g][1], d, tmp_s, False)

        def merge(ci, carry):
            rows = pl.ds(pl.multiple_of(ci * 256, 256), 256)
            l1, l2, l3 = l1_s[rows, :], l2_s[rows, :], l3_s[rows, :]
            m = jnp.maximum(jnp.maximum(l1, l2), l3)
            w1, w2, w3 = jnp.exp(l1 - m), jnp.exp(l2 - m), jnp.exp(l3 - m)
            w = w1 + w2 + w3
            y_ref[rows, :] = (w1 * o1_s[rows, :] + w2 * o2_s[rows, :] + w3 * o3_s[rows, :]) / w
            lse_ref[rows, :] = m + jnp.log(w)
            return carry

        lax.fori_loop(0, SEQ // 256, merge, 0)

    def col(off):
        return pl.BlockSpec((SEQ, LANES), lambda b, hp: (b, off + hp))

    tab = pl.BlockSpec((SEQ, LANES), lambda b, hp: (b, 0))
    out = pl.BlockSpec((SEQ, LANES), lambda b, hp: (b, hp))
    f32s = pltpu.VMEM((SEQ, LANES), F32)
    res, landed = call_hosting_exchange(
        body, xch, grid=(nb, n_pairs),
        in_specs=[col(0), col(n_pairs), col(2 * n_pairs), tab, tab, tab],
        out_specs=[out] * 11,
        out_shape=[SDS((t, A_WIDTH), F32)] * 2 + [SDS((t, A_WIDTH), BF16)] * 9,
        scratch_shapes=[f32s] * 11,
        name="a_attn_fwd", operands=(proj, proj, proj, ca, sa, sb))
    return res[:2], res[2:], landed


def a_attn_bwd(qkv_d, ca, sa, sb, dy, y, lse, nb, xch):
    t = dy.shape[0]
    n_pairs = A_WIDTH // LANES

    def body(*refs):
        qkv_refs = refs[:9]
        (c_ref, sa_ref, sb_ref, do_ref, y_ref, lse_ref, dq_ref, dk_ref, dv_ref,
         l0n_s, l1n_s, d0n_s, d1n_s, dod_s, l0d_s, l1d_s, d0d_s, d1d_s,
         dqc_s, dkc_s, dvc_s, dq4_s, dk4_s, dv4_s, dqn_s, dkn_s, dvn_s, tmp_s) = refs[9:]
        c, s_a, s_b = c_ref[...], sa_ref[...], sb_ref[...]
        head0 = lax.broadcasted_iota(jnp.int32, (Q_BLK, LANES), 1) < A_HEAD_DIM

        def per_head_rows(ci, carry):
            rows = pl.ds(pl.multiple_of(ci * 256, 256), 256)
            h0 = lax.broadcasted_iota(jnp.int32, (256, LANES), 1) < A_HEAD_DIM
            tt = do_ref[rows, :] * y_ref[rows, :]
            d0n_s[rows, :] = jnp.broadcast_to(jnp.sum(jnp.where(h0, tt, 0.0), axis=-1, keepdims=True), (256, LANES))
            d1n_s[rows, :] = jnp.broadcast_to(jnp.sum(jnp.where(h0, 0.0, tt), axis=-1, keepdims=True), (256, LANES))
            l = lse_ref[rows, :]
            lr = pltpu.roll(l, A_HEAD_DIM, 1)
            l0n_s[rows, :] = jnp.where(h0, l, lr)
            l1n_s[rows, :] = jnp.where(h0, lr, l)
            return carry

        lax.fori_loop(0, SEQ // 256, per_head_rows, 0)
        assert DILATIONS == (1, 4, 16)

        for g, d in enumerate(DILATIONS):
            qd_s, kd_s, vd_s = qkv_refs[3 * g:3 * g + 3]
            _deinterleave(do_ref, dod_s, d, BF16, tmp_s)
            if d > 1:
                for src, dst in ((l0n_s, l0d_s), (l1n_s, l1d_s), (d0n_s, d0d_s), (d1n_s, d1d_s)):
                    _deinterleave(src, dst, d, F32, tmp_s)
            l0, l1, d0, d1 = (l0n_s, l1n_s, d0n_s, d1n_s) if d == 1 else (l0d_s, l1d_s, d0d_s, d1d_s)
            dq_dst, dk_dst, dv_dst = {1: (dqn_s, dkn_s, dvn_s), 4: (dq4_s, dk4_s, dv4_s), 16: (dqc_s, dkc_s, dvc_s)}[d]
            dk_dst[...] = jnp.zeros_like(dk_dst)
            dv_dst[...] = jnp.zeros_like(dv_dst)

            def block(idx, carry, d=d, l0=l0, l1=l1, d0=d0, d1=d1, dq_dst=dq_dst, dk_dst=dk_dst, dv_dst=dv_dst,
                      qd_s=qd_s, kd_s=kd_s, vd_s=vd_s):
                q0, k0, kwin, valid = _block_coords(idx, d)
                qrows = pl.ds(q0, Q_BLK)
                krows = pl.ds(k0, kwin)
                qb, dob = qd_s[qrows, :], dod_s[qrows, :]
                kb, vb = kd_s[krows, :], vd_s[krows, :]
                zero = jnp.zeros_like(qb)
                q2 = jnp.concatenate([jnp.where(head0, qb, zero), jnp.where(head0, zero, qb)], 0)
                do2 = jnp.concatenate([jnp.where(head0, dob, zero), jnp.where(head0, zero, dob)], 0)
                wide = lambda x: jnp.concatenate([x] * (kwin // LANES), 1)
                lse2 = wide(jnp.concatenate([l0[qrows, :], l1[qrows, :]], 0))
                dd2 = wide(jnp.concatenate([d0[qrows, :], d1[qrows, :]], 0))
                s = jnp.where(jnp.concatenate([valid, valid], 0), _dot_nt(q2, kb), NEG_INF)
                p = jnp.exp(s - lse2)
                ds = (p * (_dot_nt(do2, vb) - dd2)).astype(BF16)
                dq2 = _dot(ds, kb)
                dq_dst[qrows, :] = jnp.where(head0, dq2[:Q_BLK], dq2[Q_BLK:])
                dk_dst[krows, :] += _dot_tn(ds, q2)
                dv_dst[krows, :] += _dot_tn(p.astype(BF16), do2)
                return carry

            lax.fori_loop(0, SEQ // Q_BLK, block, 0, unroll=UNROLL_BWD)

        for c16, c4, nat in ((dqc_s, dq4_s, dqn_s), (dkc_s, dk4_s, dkn_s), (dvc_s, dv4_s, dvn_s)):
            _class16_to_class4(c16, tmp_s)
            c4[...] = c4[...] + tmp_s[...]
            _interleave(c4, nat, 4, tmp_s, True)

        dq_ref[...] = _rope_bwd(dqn_s[...] * (A_HEAD_DIM ** -0.5), c, s_a, s_b, A_ROT // 2).astype(BF16)
        dk_ref[...] = _rope_bwd(dkn_s[...], c, s_a, s_b, A_ROT // 2).astype(BF16)
        dv_ref[...] = dvn_s[...].astype(BF16)

    tab = pl.BlockSpec((SEQ, LANES), lambda b, hp: (b, 0))
    blk = pl.BlockSpec((SEQ, LANES), lambda b, hp: (b, hp))
    f32s = pltpu.VMEM((SEQ, LANES), F32)
    b16s = pltpu.VMEM((SEQ, LANES), BF16)
    return call_hosting_exchange(
        body, xch, grid=(nb, n_pairs),
        in_specs=[blk] * 9 + [tab, tab, tab, blk, blk, blk],
        out_specs=[blk, blk, blk],
        out_shape=[SDS((t, A_WIDTH), BF16)] * 3,
        scratch_shapes=[f32s] * 4 + [b16s] + [f32s] * 14,
        name="a_attn_bwd", operands=(*qkv_d, ca, sa, sb, dy, y, lse))


MLA_SCALE = (MLA_NOPE + MLA_ROPE) ** -0.5
LOG2E = 1.4426950408889634
MLA_QW = MLA_HEADS * LANES
MLA_KVW = MLA_QW + MLA_WIDTH


def _rms(x, g):
    r = lax.rsqrt(jnp.mean(x * x, axis=-1, keepdims=True) + NORM_EPS)
    return x * r * g, r


def _rms_bwd(dn, x, r, g):
    tg = dn * g
    dx = r * tg - x * (r * r * r) * jnp.mean(tg * x, axis=-1, keepdims=True)
    return dx, jnp.sum(dn * x * r, axis=0, keepdims=True)


def mla_prep_fwd(proj, cm, sma, smb, g_cq, g_ckv, wuq, wkv):
    t = proj.shape[0]
    tm = 1024

    def body(cq_ref, ckv_ref, kr_ref, c_ref, sa_ref, sb_ref, gq_ref, gkv_ref, wuq_ref, wkv_ref, q_ref, k_ref, v_ref):
        c, s_a, s_b = c_ref[...], sa_ref[...], sb_ref[...]
        cqn, _ = _rms(cq_ref[...], gq_ref[...])
        qf = _dot(cqn.astype(BF16), wuq_ref[...])
        ckvn, _ = _rms(ckv_ref[...], gkv_ref[...])
        kvf = _dot(ckvn.astype(BF16), wkv_ref[...])
        krope = _rope_fwd(kr_ref[...], c, s_a, s_b, MLA_ROPE // 2)
        for h in range(MLA_HEADS):
            cols = slice(h * LANES, (h + 1) * LANES)
            q_ref[:, cols] = (_rope_fwd(qf[:, cols], c, s_a, s_b, MLA_ROPE // 2) * (MLA_SCALE * LOG2E)).astype(BF16)
            k_ref[:, cols] = (kvf[:, cols] + krope).astype(BF16)
        v_ref[...] = kvf[:, MLA_QW:].astype(BF16)

    def row(w, j):
        return pl.BlockSpec((tm, w), lambda i: (i, j))

    def full(a):
        return pl.BlockSpec(a.shape, lambda i: (0, 0))

    return pl.pallas_call(
        body, grid=(t // tm,),
        in_specs=[row(256, 4096 // 256), row(128, 4352 // 128), row(128, 4480 // 128), row(128, 0), row(128, 0), row(128, 0),
                  full(g_cq), full(g_ckv), full(wuq), full(wkv)],
        out_specs=[row(MLA_QW, 0), row(MLA_QW, 0), row(MLA_WIDTH, 0)],
        out_shape=[SDS((t, MLA_QW), BF16), SDS((t, MLA_QW), BF16), SDS((t, MLA_WIDTH), BF16)],
        name="mla_prep_fwd", compiler_params=_params("parallel"))(proj, proj, proj, cm, sma, smb, g_cq, g_ckv, wuq, wkv)


def mla_prep_bwd(proj, cm, sma, smb, g_cq, g_ckv, wuq, wkv, dq, dk, dv):
    t = proj.shape[0]
    tm = 1024

    def body(cq_ref, ckv_ref, c_ref, sa_ref, sb_ref, gq_ref, gkv_ref, wuq_ref, wkv_ref, dq_ref, dk_ref, dv_ref,
             dcc_ref, dqf_ref, cqn_ref, dkvf_ref, ckvn_ref, dgq_ref, dgkv_ref):
        @pl.when(pl.program_id(0) == 0)
        def _():
            dgq_ref[...] = jnp.zeros_like(dgq_ref)
            dgkv_ref[...] = jnp.zeros_like(dgkv_ref)

        c, s_a, s_b = c_ref[...], sa_ref[...], sb_ref[...]
        cq, ckv = cq_ref[...], ckv_ref[...]
        cqn, rq = _rms(cq, gq_ref[...])
        ckvn, rkv = _rms(ckv, gkv_ref[...])
        cqn_ref[...] = cqn.astype(BF16)
        ckvn_ref[...] = ckvn.astype(BF16)
        lane = lax.broadcasted_iota(jnp.int32, (tm, LANES), 1)
        rope_lanes = (lane >= MLA_NOPE) & (lane < MLA_NOPE + MLA_ROPE)
        dkrope = jnp.zeros((tm, LANES), F32)
        for h in range(MLA_HEADS):
            cols = slice(h * LANES, (h + 1) * LANES)
            dqf_ref[:, cols] = _rope_bwd(dq_ref[:, cols].astype(F32) * MLA_SCALE, c, s_a, s_b, MLA_ROPE // 2).astype(BF16)
            dkh = dk_ref[:, cols].astype(F32) * (1.0 / LOG2E)
            dkvf_ref[:, cols] = dkh.astype(BF16)
            dkrope = dkrope + dkh
        dkvf_ref[:, MLA_QW:] = dv_ref[...].astype(BF16)
        dkr = _rope_bwd(jnp.where(rope_lanes, dkrope, 0.0), c, s_a, s_b, MLA_ROPE // 2)
        dcqn = _dot_nt(dqf_ref[...], wuq_ref[...])
        dckvn = _dot_nt(dkvf_ref[...], wkv_ref[...])
        dcq, dgq = _rms_bwd(dcqn, cq, rq, gq_ref[...])
        dckv, dgkv = _rms_bwd(dckvn, ckv, rkv, gkv_ref[...])
        dgq_ref[...] += dgq
        dgkv_ref[...] += dgkv
        dcc_ref[:, 0:256] = dcq.astype(BF16)
        dcc_ref[:, 256:384] = dckv.astype(BF16)
        dcc_ref[:, 384:512] = dkr.astype(BF16)

    def row(w, j):
        return pl.BlockSpec((tm, w), lambda i: (i, j))

    def full(a):
        return pl.BlockSpec(a.shape, lambda i: (0, 0))

    return pl.pallas_call(
        body, grid=(t // tm,),
        in_specs=[row(256, 4096 // 256), row(128, 4352 // 128), row(128, 0), row(128, 0), row(128, 0),
                  full(g_cq), full(g_ckv), full(wuq), full(wkv), row(MLA_QW, 0), row(MLA_QW, 0), row(MLA_WIDTH, 0)],
        out_specs=[row(512, 0), row(MLA_QW, 0), row(256, 0), row(MLA_KVW, 0), row(128, 0), full(g_cq), full(g_ckv)],
        out_shape=[SDS((t, 512), BF16), SDS((t, MLA_QW), BF16), SDS((t, 256), BF16), SDS((t, MLA_KVW), BF16),
                   SDS((t, 128), BF16), SDS(g_cq.shape, F32), SDS(g_ckv.shape, F32)],
        name="mla_prep_bwd", compiler_params=_params("arbitrary"))(proj, proj, cm, sma, smb, g_cq, g_ckv, wuq, wkv, dq, dk, dv)


MLA_TQ = SEQ
MLA_SUB_FWD = 512
MLA_SUB_BWD = 256


def mla_attn_fwd(qb, kb, vb, nb):
    t = qb.shape[0]
    nq = SEQ // MLA_TQ
    n_pairs = MLA_HEADS // 2

    def body(q_ref, k_ref, v_ref, y_ref, lse_ref):
        head0 = lax.broadcasted_iota(jnp.int32, (MLA_SUB_FWD, LANES), 1) < MLA_V
        v = v_ref[...]
        vhead0 = lax.broadcasted_iota(jnp.int32, v.shape, 1) < MLA_V
        one = jnp.ones_like(v)
        vh = [jnp.where(vhead0 == (h == 0), v, one) for h in range(2)]
        for sub in range(MLA_TQ // MLA_SUB_FWD):
            rows = slice(sub * MLA_SUB_FWD, (sub + 1) * MLA_SUB_FWD)
            outs, lses = [], []
            for h in range(2):
                cols = slice(h * LANES, (h + 1) * LANES)
                s = _dot_nt(q_ref[rows, cols], k_ref[:, cols])
                m = jnp.max(s, axis=-1, keepdims=True)
                p = jnp.exp2(s - m).astype(BF16)
                ol = _dot(p, vh[h])
                l = pltpu.roll(ol, MLA_V, 1)
                outs.append(ol / l)
                lses.append(m + jnp.log2(l))
            y_ref[rows, :] = jnp.where(head0, outs[0], outs[1])
            lse_ref[rows, :] = jnp.where(head0, lses[0], lses[1])

    return pl.pallas_call(
        body, grid=(nb, n_pairs, nq),
        in_specs=[pl.BlockSpec((MLA_TQ, 2 * LANES), lambda b, hp, i: (b * nq + i, hp)),
                  pl.BlockSpec((SEQ, 2 * LANES), lambda b, hp, i: (b, hp)),
                  pl.BlockSpec((SEQ, LANES), lambda b, hp, i: (b, hp))],
        out_specs=[pl.BlockSpec((MLA_TQ, LANES), lambda b, hp, i: (b * nq + i, hp))] * 2,
        out_shape=[SDS((t, MLA_WIDTH), F32)] * 2,
        name="mla_attn_fwd", compiler_params=_params("parallel", "parallel", "parallel"))(qb, kb, vb)


def mla_attn_bwd(qb, kb, vb, dy, y, lse, nb, xch):
    t = qb.shape[0]
    nq = SEQ // MLA_TQ
    n_pairs = MLA_HEADS // 2

    assert nq == 1

    def body(q_ref, k_ref, v_ref, do_ref, y_ref, lse_ref, dq_ref, dk_ref, dv_ref, dk_s, dv_s):
        dk_s[...] = jnp.zeros_like(dk_s)
        dv_s[...] = jnp.zeros_like(dv_s)
        head0 = lax.broadcasted_iota(jnp.int32, (MLA_SUB_BWD, LANES), 1) < MLA_V
        v = v_ref[...]
        for sub in range(MLA_TQ // MLA_SUB_BWD):
            rows = slice(sub * MLA_SUB_BWD, (sub + 1) * MLA_SUB_BWD)
            do = do_ref[rows, :]
            lse = lse_ref[rows, :]
            tt = do * y_ref[rows, :]
            dv = jnp.zeros((SEQ, LANES), F32)
            for h in range(2):
                sel = head0 if h == 0 else ~head0
                lo = h * MLA_V
                cols = slice(h * LANES, (h + 1) * LANES)
                q = q_ref[rows, cols]
                k = k_ref[:, cols]
                dd = jnp.sum(jnp.where(sel, tt, 0.0), axis=-1, keepdims=True)
                doh = jnp.where(sel, do, 0.0).astype(BF16)
                p = jnp.exp2(_dot_nt(q, k) - lse[:, lo:lo + 1])
                dp = _dot_nt(doh, v)
                ds = (p * (dp - dd)).astype(BF16)
                dq_ref[rows, cols] = _dot(ds, k).astype(dq_ref.dtype)
                dk_s[:, cols] += _dot_tn(ds, q)
                dv = dv + _dot_tn(p.astype(BF16), doh)
            dv_s[...] += dv
        dk_ref[...] = dk_s[...].astype(dk_ref.dtype)
        dv_ref[...] = dv_s[...].astype(dv_ref.dtype)

    qspec = pl.BlockSpec((MLA_TQ, 2 * LANES), lambda b, hp, i: (b * nq + i, hp))
    kspec = pl.BlockSpec((SEQ, 2 * LANES), lambda b, hp, i: (b, hp))
    vspec = pl.BlockSpec((SEQ, LANES), lambda b, hp, i: (b, hp))
    ospec = pl.BlockSpec((MLA_TQ, LANES), lambda b, hp, i: (b * nq + i, hp))
    return call_hosting_exchange(
        body, xch, grid=(nb, n_pairs, nq),
        in_specs=[qspec, kspec, vspec, ospec, ospec, ospec],
        out_specs=[qspec, kspec, vspec],
        out_shape=[SDS((t, MLA_QW), BF16), SDS((t, MLA_QW), BF16), SDS((t, MLA_WIDTH), BF16)],
        scratch_shapes=[pltpu.VMEM((SEQ, 2 * LANES), F32), pltpu.VMEM((SEQ, LANES), F32)],
        name="mla_attn_bwd", operands=(qb, kb, vb, dy, y, lse))


MEM_TQ = SEQ
MEM_SUB = SEQ
MEM_SCALE = MEM_HEAD_DIM ** -0.5
MQ_BLK4 = 5120 // MEM_WIDTH


def mem_attn_fwd(proj, mkv, nb):
    t = proj.shape[0]
    nq = SEQ // MEM_TQ

    def body(q_ref, mk_ref, mv_ref, y_ref):
        for sub in range(MEM_TQ // MEM_SUB):
            rows = slice(sub * MEM_SUB, (sub + 1) * MEM_SUB)
            for h in range(MEM_HEADS):
                cols = slice(h * LANES, (h + 1) * LANES)
                s = _dot_nt(q_ref[rows, cols].astype(BF16), mk_ref[:, cols]) * MEM_SCALE
                m = jnp.max(s, axis=-1, keepdims=True)
                p = jnp.exp(s - m)
                l = jnp.sum(p, axis=-1, keepdims=True)
                y_ref[rows, cols] = _dot(p.astype(BF16), mv_ref[:, cols]) / l

    return pl.pallas_call(
        body, grid=(nb, nq),
        in_specs=[pl.BlockSpec((MEM_TQ, MEM_WIDTH), lambda b, i: (b * nq + i, MQ_BLK4)),
                  pl.BlockSpec((N_MEM, MEM_WIDTH), lambda b, i: (b, 0)),
                  pl.BlockSpec((N_MEM, MEM_WIDTH), lambda b, i: (b, 1))],
        out_specs=pl.BlockSpec((MEM_TQ, MEM_WIDTH), lambda b, i: (b * nq + i, 0)),
        out_shape=SDS((t, MEM_WIDTH), F32),
        name="mem_attn_fwd", compiler_params=_params("parallel", "parallel"))(proj, mkv, mkv)


def mem_attn_bwd(proj, mkv, dy, nb):
    t = proj.shape[0]
    nq = SEQ // MEM_TQ

    def body(q_ref, mk_ref, mv_ref, do_ref, dq_ref, dmk_ref, dmv_ref):
        @pl.when(pl.program_id(1) == 0)
        def _():
            dmk_ref[...] = jnp.zeros_like(dmk_ref)
            dmv_ref[...] = jnp.zeros_like(dmv_ref)

        for sub in range(MEM_TQ // MEM_SUB):
            rows = slice(sub * MEM_SUB, (sub + 1) * MEM_SUB)
            for h in range(MEM_HEADS):
                cols = slice(h * LANES, (h + 1) * LANES)
                q = q_ref[rows, cols].astype(BF16)
                mk, mv = mk_ref[:, cols], mv_ref[:, cols]
                do = do_ref[rows, cols].astype(BF16)
                s = _dot_nt(q, mk) * MEM_SCALE
                e = jnp.exp(s - jnp.max(s, axis=-1, keepdims=True))
                p = e / jnp.sum(e, axis=-1, keepdims=True)
                dp = _dot_nt(do, mv)
                ds = (p * (dp - jnp.sum(p * dp, axis=-1, keepdims=True)) * MEM_SCALE).astype(BF16)
                dq_ref[rows, cols] = _dot(ds, mk).astype(BF16)
                dmk_ref[:, cols] += _dot_tn(ds, q)
                dmv_ref[:, cols] += _dot_tn(p.astype(BF16), do)

    ospec = pl.BlockSpec((MEM_TQ, MEM_WIDTH), lambda b, i: (b * nq + i, 0))
    kspec = pl.BlockSpec((N_MEM, MEM_WIDTH), lambda b, i: (b, 0))
    return pl.pallas_call(
        body, grid=(nb, nq),
        in_specs=[pl.BlockSpec((MEM_TQ, MEM_WIDTH), lambda b, i: (b * nq + i, MQ_BLK4)),
                  kspec, pl.BlockSpec((N_MEM, MEM_WIDTH), lambda b, i: (b, 1)), ospec],
        out_specs=[ospec, kspec, kspec],
        out_shape=[SDS((t, MEM_WIDTH), BF16), SDS((nb * N_MEM, MEM_WIDTH), F32), SDS((nb * N_MEM, MEM_WIDTH), F32)],
        name="mem_attn_bwd", compiler_params=_params("parallel", "arbitrary"))(proj, mkv, mkv, dy)


ROW_TM = 512
AG_BLK = 3072 // 1024
BG_BLK = 4608 // 512
MG_BLK = 5632 // 512
GROUPS = ((0, A_WIDTH), (A_WIDTH, MLA_WIDTH), (A_WIDTH + MLA_WIDTH, MEM_WIDTH))
D_MIX = 2048


def _gate_specs():
    def row(w, j):
        return pl.BlockSpec((ROW_TM, w), lambda i: (i, j))

    def vec(w):
        return pl.BlockSpec((1, w), lambda i: (0, 0))

    ys = [row(A_WIDTH, 0), row(MLA_WIDTH, 0), row(MEM_WIDTH, 0)]
    gates = [row(A_WIDTH, AG_BLK), row(MLA_WIDTH, BG_BLK), row(MEM_WIDTH, MG_BLK)]
    gains = [vec(A_WIDTH), vec(MLA_WIDTH), vec(MEM_WIDTH)]
    return row, vec, ys, gates, gains


def gate_out_ln_loss(ya, yb, ym, proj, goa, gob, gom, wout, h32, target, gp, bp):
    t, d = h32.shape
    _, _, ys, gates, gains = _gate_specs()

    def body(ya_ref, yb_ref, ym_ref, ga_ref, gb_ref, gm_ref, goa_ref, gob_ref, gom_ref, w_ref, h_ref, t_ref, gp_ref, bp_ref,
             z_ref, du32_ref, du16_ref, loss_ref, dgp_ref, dbp_ref):
        @pl.when(pl.program_id(0) == 0)
        def _():
            loss_ref[...] = jnp.zeros_like(loss_ref)
            dgp_ref[...] = jnp.zeros_like(dgp_ref)
            dbp_ref[...] = jnp.zeros_like(dbp_ref)

        for (off, w), y_ref, g_ref, go_ref in zip(GROUPS, (ya_ref, yb_ref, ym_ref), (ga_ref, gb_ref, gm_ref),
                                                  (goa_ref, gob_ref, gom_ref)):
            n, _ = _rms(y_ref[...], go_ref[...])
            gt = g_ref[...]
            z_ref[:, off:off + w] = (n * (gt * _sigmoid(gt))).astype(BF16)
        g = gp_ref[...]
        u = ALPHA * h_ref[...] + _dot(z_ref[...], w_ref[...])
        mu = jnp.mean(u, axis=-1, keepdims=True)
        uc = u - mu
        rstd = lax.rsqrt(jnp.mean(uc * uc, axis=-1, keepdims=True) + NORM_EPS)
        xhat = uc * rstd
        err = xhat * g + bp_ref[...] - t_ref[...]
        tok = jnp.sum(err * err, axis=-1, keepdims=True) * (1.0 / d)
        loss_ref[...] += 0.5 * jnp.sum(tok, axis=0, keepdims=True)
        dout = err * (1.0 / d)
        dxhat = dout * g
        du = rstd * (dxhat - jnp.mean(dxhat, axis=-1, keepdims=True)
                     - xhat * jnp.mean(dxhat * xhat, axis=-1, keepdims=True))
        du32_ref[...] = du
        du16_ref[...] = du.astype(BF16)
        dgp_ref[...] += jnp.sum(dout * xhat, axis=0, keepdims=True)
        dbp_ref[...] += jnp.sum(dout, axis=0, keepdims=True)

    row = pl.BlockSpec((ROW_TM, d), lambda i: (i, 0))
    vec = pl.BlockSpec((1, d), lambda i: (0, 0))
    zrow = pl.BlockSpec((ROW_TM, D_MIX), lambda i: (i, 0))
    return pl.pallas_call(
        body, grid=(t // ROW_TM,),
        in_specs=ys + gates + gains + [pl.BlockSpec((D_MIX, d), lambda i: (0, 0)), row, row, vec, vec],
        out_specs=[zrow, row, row, pl.BlockSpec((1, LANES), lambda i: (0, 0)), vec, vec],
        out_shape=[SDS((t, D_MIX), BF16), SDS((t, d), F32), SDS((t, d), BF16), SDS((1, LANES), F32), SDS((1, d), F32),
                   SDS((1, d), F32)],
        name="gate_out_ln_loss", compiler_params=_params("arbitrary"))(
            ya, yb, ym, proj, proj, proj, goa, gob, gom, wout, h32, target, gp, bp)


def gate_bwd(du16, wout, ya, yb, ym, proj, goa, gob, gom):
    t = ya.shape[0]
    row, vec, ys, gates, gains = _gate_specs()

    def body(du_ref, w_ref, ya_ref, yb_ref, ym_ref, ga_ref, gb_ref, gm_ref, goa_ref, gob_ref, gom_ref,
             dya_ref, dyb_ref, dym_ref, dga_ref, dgb_ref, dgm_ref, dgoa_ref, dgob_ref, dgom_ref):
        @pl.when(pl.program_id(0) == 0)
        def _():
            dgoa_ref[...] = jnp.zeros_like(dgoa_ref)
            dgob_ref[...] = jnp.zeros_like(dgob_ref)
            dgom_ref[...] = jnp.zeros_like(dgom_ref)

        dz = _dot_nt(du_ref[...], w_ref[...])
        for (off, w), y_ref, g_ref, go_ref, dy_ref, dg_ref, dgo_ref in zip(
                GROUPS, (ya_ref, yb_ref, ym_ref), (ga_ref, gb_ref, gm_ref), (goa_ref, gob_ref, gom_ref),
                (dya_ref, dyb_ref, dym_ref), (dga_ref, dgb_ref, dgm_ref), (dgoa_ref, dgob_ref, dgom_ref)):
            dzg = dz[:, off:off + w]
            y, gt, go = y_ref[...], g_ref[...], go_ref[...]
            n, r = _rms(y, go)
            sg = _sigmoid(gt)
            dg_ref[...] = (dzg * n * (sg * (1.0 + gt * (1.0 - sg)))).astype(BF16)
            dy, dgo = _rms_bwd(dzg * (gt * sg), y, r, go)
            dy_ref[...] = dy
            dgo_ref[...] += dgo

    widths = (A_WIDTH, MLA_WIDTH, MEM_WIDTH)
    return pl.pallas_call(
        body, grid=(t // ROW_TM,),
        in_specs=[row(D_MODEL, 0), pl.BlockSpec((D_MIX, D_MODEL), lambda i: (0, 0))] + ys + gates + gains,
        out_specs=[row(w, 0) for w in widths] * 2 + [vec(w) for w in widths],
        out_shape=[SDS((t, w), F32) for w in widths] + [SDS((t, w), BF16) for w in widths] + [SDS((1, w), F32) for w in widths],
        name="gate_bwd", compiler_params=_params("arbitrary"))(du16, wout, ya, yb, ym, proj, proj, proj, goa, gob, gom)


def dh_ln_bwd(pieces, win_t, du32, x2, g_emb, xch):
    t, d = x2.shape

    def body(*refs):
        p_refs = refs[:len(pieces)]
        w_ref, du_ref, x_ref, g_ref, dx_ref, dg_ref, db_ref = refs[len(pieces):]

        @pl.when(pl.program_id(0) == 0)
        def _():
            dg_ref[...] = jnp.zeros_like(dg_ref)
            db_ref[...] = jnp.zeros_like(db_ref)

        dh = ALPHA * du_ref[...]
        for p_ref, off, w in zip(p_refs, PIECE_OFFS, PIECE_WIDTHS):
            dh = dh + _dot(p_ref[...], w_ref[off:off + w, :])
        x = x_ref[...]
        xc = x - jnp.mean(x, axis=-1, keepdims=True)
        rstd = lax.rsqrt(jnp.mean(xc * xc, axis=-1, keepdims=True) + NORM_EPS)
        xhat = xc * rstd
        dg_ref[...] += jnp.sum(dh * xhat, axis=0, keepdims=True)
        db_ref[...] += jnp.sum(dh, axis=0, keepdims=True)
        tg = dh * g_ref[...]
        dx_ref[...] = rstd * (tg - jnp.mean(tg, axis=-1, keepdims=True)
                              - xhat * jnp.mean(tg * xhat, axis=-1, keepdims=True))

    row = pl.BlockSpec((ROW_TM, d), lambda i: (i, 0))
    vec = pl.BlockSpec((1, d), lambda i: (0, 0))
    return call_hosting_exchange(
        body, xch, grid=(t // ROW_TM,),
        in_specs=[pl.BlockSpec((ROW_TM, w), lambda i: (i, 0)) for w in PIECE_WIDTHS]
        + [pl.BlockSpec(win_t.shape, lambda i: (0, 0)), row, row, vec],
        out_specs=[row, vec, vec],
        out_shape=[SDS((t, d), F32), SDS((1, d), F32), SDS((1, d), F32)],
        scratch_shapes=[], name="dh_ln_bwd", operands=(*pieces, win_t, du32, x2, g_emb))


def _adamw(w, g, m, v):
    m2 = ADAM_B1 * m + (1.0 - ADAM_B1) * g
    v2 = ADAM_B2 * v + (1.0 - ADAM_B2) * (g * g)
    m_hat = m2 / (1.0 - ADAM_B1 ** ADAM_STEP)
    v_hat = v2 / (1.0 - ADAM_B2 ** ADAM_STEP)
    return -ADAM_LR * (m_hat / (jnp.sqrt(v_hat) + ADAM_EPS) + ADAM_WD * w), m2, v2


def adamw_shard(w, parts, m, v, name):
    r, c = w.shape
    if r % 256 == 0 or r * c <= 256 * 1024:
        tr, tc = min(r, 256), c
    else:
        tr, tc = r, 256

    def body(w_ref, p_ref, m_ref, v_ref, g_ref, d_ref, nm_ref, nv_ref):
        g = p_ref[0].astype(F32)
        for k in range(1, N_DEV):
            g = g + p_ref[k].astype(F32)
        g_ref[...] = g
        d_ref[...], nm_ref[...], nv_ref[...] = _adamw(w_ref[...], g, m_ref[...], v_ref[...])

    blk = pl.BlockSpec((tr, tc), lambda i, j: (i, j))
    return pl.pallas_call(
        body, grid=(r // tr, c // tc),
        in_specs=[blk, pl.BlockSpec((N_DEV, tr, tc), lambda i, j: (0, i, j)), blk, blk],
        out_specs=[blk] * 4, out_shape=[SDS((r, c), F32)] * 4, name=name,
        compiler_params=_params("parallel", "parallel"))(w, parts, m, v)


def _place():
    return lax.axis_index("x"), lax.axis_index("y"), lax.axis_index("c")


def _flat(px, py, pc):
    return 4 * px + 2 * py + pc


def _peer(x, y, c, k):
    return (1 - x if k & 4 else x, 1 - y if k & 2 else y, 1 - c if k & 1 else c)


def cast_shards(shards):
    def body(*refs):
        n = len(refs) // 2
        for i_ref, o_ref in zip(refs[:n], refs[n:]):
            o_ref[...] = i_ref[...].astype(BF16)

    return pl.pallas_call(body, out_shape=[SDS(s.shape, BF16) for s in shards], name="cast_shards",
                          compiler_params=_params())(*shards)


def _two_level_gather_plan(src_refs, land_refs, send_sems, recv_sems, local_sems):
    n = len(src_refs)
    x, y, c = _place()
    me, sib = (x, y, c), (x, y, 1 - c)
    chips = [(1 - x, y), (x, 1 - y), (1 - x, 1 - y)]

    def copy(a, k, block, to, src=None):
        dst = land_refs[a].at[_flat(*block)]
        return pltpu.make_async_remote_copy(
            src_ref=dst if src is None else src, dst_ref=dst,
            send_sem=send_sems.at[a * N_DEV + k], recv_sem=recv_sems.at[a * N_DEV + k],
            device_id=to, device_id_type=MESH)

    mine = [pltpu.make_async_copy(src_refs[a], land_refs[a].at[_flat(*me)], local_sems.at[a]) for a in range(n)]
    first = []
    for a in range(n):
        first.append(copy(a, 0, me, sib, src=src_refs[a]))
        first += [copy(a, 1 + j, me, (*chip, c), src=src_refs[a]) for j, chip in enumerate(chips)]

    def start():
        for cp in mine + first:
            cp.start()

    def finish():
        passed = []
        for j, chip in enumerate(chips):
            for a in range(n):
                copy(a, 1 + j, (*chip, c), me).wait_recv()
                fwd = copy(a, 4 + j, (*chip, c), sib)
                fwd.start()
                passed.append(fwd)
        for a in range(n):
            copy(a, 0, sib, me).wait_recv()
            for j, chip in enumerate(chips):
                copy(a, 4 + j, (*chip, 1 - c), me).wait_recv()
        for cp in first + passed:
            cp.wait_send()
        for cp in mine:
            cp.wait()

    return start, finish


ALL_DEVICES = tuple(range(N_DEV))


def _exchange_plan(src_refs, land_refs, dests, send_sems, recv_sems, local_sems):
    x, y, c = _place()
    me = _flat(x, y, c)
    plan = []
    for a, (src, land, dl) in enumerate(zip(src_refs, land_refs, dests)):
        for li, j in enumerate(dl):
            to = ((j >> 2) & 1, (j >> 1) & 1, j & 1)
            block = src.at[li] if len(src.shape) == len(land.shape) else src

            def push(slot, a=a, block=block, land=land, j=j, to=to):
                return pltpu.make_async_remote_copy(
                    src_ref=block, dst_ref=land.at[slot], send_sem=send_sems.at[a * N_DEV + j],
                    recv_sem=recv_sems.at[a * N_DEV + slot], device_id=to, device_id_type=MESH)

            own = pltpu.make_async_copy(block, land.at[j], local_sems.at[a])
            plan.append((j, push(me), own, [push(s) for s in range(N_DEV) if s != j]))
    return me, plan


def _exchange_start(me, plan):
    for j, send, own, _ in plan:
        @pl.when(me != j)
        def _(send=send):
            send.start()

        @pl.when(me == j)
        def _(own=own):
            own.start()


def _exchange_wait(me, plan):
    for j, send, own, arrivals in plan:
        @pl.when(me != j)
        def _(send=send):
            send.wait_send()

        @pl.when(me == j)
        def _(own=own, arrivals=arrivals):
            own.wait()
            for arrival in arrivals:
                arrival.wait_recv()


def call_hosting_exchange(core, xch, *, grid, in_specs, out_specs, out_shape, scratch_shapes, name, operands):
    srcs, dests, landing = xch
    n, n_in, n_out, n_scr = len(srcs), len(in_specs), len(out_specs), len(scratch_shapes)

    def body(*refs):
        ins, src_refs = refs[:n_in], refs[n_in:n_in + n]
        outs = refs[n_in + 2 * n:n_in + 2 * n + n_out]
        land_refs = refs[n_in + 2 * n + n_out:n_in + 3 * n + n_out]
        scratch = refs[n_in + 3 * n + n_out:n_in + 3 * n + n_out + n_scr]
        sems = refs[n_in + 3 * n + n_out + n_scr:]
        first = functools.reduce(jnp.logical_and, [pl.program_id(i) == 0 for i in range(len(grid))])
        last = functools.reduce(jnp.logical_and, [pl.program_id(i) == grid[i] - 1 for i in range(len(grid))])
        if dests is None:
            start, finish = _two_level_gather_plan(src_refs, land_refs, *sems)
        else:
            me, plan = _exchange_plan(src_refs, land_refs, dests, *sems)
            start, finish = functools.partial(_exchange_start, me, plan), functools.partial(_exchange_wait, me, plan)
        pl.when(first)(start)
        core(*ins, *outs, *scratch)
        pl.when(last)(finish)

    hbm = pl.BlockSpec(memory_space=pl.ANY)
    res = pl.pallas_call(
        body, grid=grid,
        in_specs=list(in_specs) + [hbm] * (2 * n), out_specs=list(out_specs) + [hbm] * n,
        out_shape=list(out_shape) + [SDS(l.shape, l.dtype) for l in landing],
        scratch_shapes=list(scratch_shapes) + [pltpu.SemaphoreType.DMA((N_DEV * n,)), pltpu.SemaphoreType.DMA((N_DEV * n,)),
                                               pltpu.SemaphoreType.DMA((n,))],
        input_output_aliases={n_in + n + k: n_out + k for k in range(n)},
        name=name, compiler_params=_params(*(("arbitrary",) * len(grid))))(*operands, *srcs, *landing)
    return res[:n_out], res[n_out:]


SLOT_ROWS = 8


def small_allreduce_adamw(loss_sum, grads, ws, ms, vs):
    n = len(grads)
    rows = [g.shape[0] for g in grads]
    total = SLOT_ROWS * (n + 1)

    def body(*refs):
        loss_ref, g_refs, w_refs = refs[0], refs[1:1 + n], refs[1 + n:1 + 2 * n]
        m_refs, v_refs = refs[1 + 2 * n:1 + 3 * n], refs[1 + 3 * n:1 + 4 * n]
        outs = refs[1 + 4 * n:2 + 8 * n]
        vec, gath, tot, send_sems, recv_sems = refs[2 + 8 * n:]
        x, y, c = _place()
        me = _flat(x, y, c)
        vec[...] = jnp.zeros_like(vec)
        vec[0:1, :] = loss_ref[...]
        for i in range(n):
            vec[SLOT_ROWS * (i + 1):SLOT_ROWS * (i + 1) + rows[i], :] = g_refs[i][...]
        gath[me] = vec[...]
        copies = []
        for k in range(1, N_DEV):
            peer = _peer(x, y, c, k)
            copies.append(pltpu.make_async_remote_copy(
                src_ref=vec, dst_ref=gath.at[me], send_sem=send_sems.at[k - 1], recv_sem=recv_sems.at[k - 1],
                device_id=peer, device_id_type=MESH))
        for cp in copies:
            cp.start()
        for cp in copies:
            cp.wait_recv()
        for cp in copies:
            cp.wait_send()
        g = gath[0]
        for j in range(1, N_DEV):
            g = g + gath[j]
        tot[...] = g
        outs[0][...] = tot[0:1, :]
        for i in range(n):
            gi = tot[SLOT_ROWS * (i + 1):SLOT_ROWS * (i + 1) + rows[i], :]
            outs[1 + i][...] = gi
            outs[1 + n + i][...], outs[1 + 2 * n + i][...], outs[1 + 3 * n + i][...] = _adamw(
                w_refs[i][...], gi, m_refs[i][...], v_refs[i][...])

    shapes = [SDS(g.shape, F32) for g in grads]
    return pl.pallas_call(
        body, out_shape=[SDS((1, LANES), F32)] + shapes * 4,
        scratch_shapes=[pltpu.VMEM((total, LANES), F32), pltpu.VMEM((N_DEV, total, LANES), F32), pltpu.VMEM((total, LANES), F32),
                        pltpu.SemaphoreType.DMA((7,)), pltpu.SemaphoreType.DMA((7,))],
        name="small_allreduce_adamw", compiler_params=_params())(loss_sum, *grads, *ws, *ms, *vs)


def _rope_lane_patterns():
    inv = lambda r: ROPE_THETA ** (-(jnp.arange(0, r, 2, dtype=F32) / r))
    z = lambda n: jnp.zeros((n,), F32)
    o = lambda n: jnp.ones((n,), F32)
    half, rest = A_ROT // 2, A_HEAD_DIM - A_ROT
    ia, im = inv(A_ROT), inv(MLA_ROPE)
    mh, tail = MLA_ROPE // 2, LANES - MLA_NOPE - MLA_ROPE
    rows = [jnp.tile(jnp.concatenate([ia, ia, z(rest)]), 2),
            jnp.tile(jnp.concatenate([o(half), z(half + rest)]), 2),
            jnp.tile(jnp.concatenate([z(half), o(half), z(rest)]), 2),
            jnp.concatenate([z(MLA_NOPE), im, im, z(tail)]),
            jnp.concatenate([z(MLA_NOPE), o(mh), z(mh + tail)]),
            jnp.concatenate([z(MLA_NOPE + mh), o(mh), z(tail)]),
            z(LANES), z(LANES)]
    return jnp.stack(rows)


KR_LO, KR_HI = 4480, 4512
W_IN_SHARD = D_IN // N_DEV
BG_SPLIT = 6 * W_IN_SHARD - KR_HI


def w_in_working_t(g):
    pad_lo, pad_hi = MLA_NOPE, LANES - MLA_NOPE - MLA_ROPE
    spans = []
    for lo, hi, shift in ((0, KR_LO, 0), (KR_LO, KR_HI, pad_lo), (KR_HI, D_IN, pad_lo + pad_hi)):
        r = lo
        while r < hi:
            j = r // W_IN_SHARD
            n = min(hi, (j + 1) * W_IN_SHARD) - r
            spans.append((j, r - j * W_IN_SHARD, n, r + shift))
            r += n

    def body(g_ref, o_ref):
        o_ref[KR_LO:KR_LO + pad_lo, :] = jnp.zeros((pad_lo, D_MODEL), o_ref.dtype)
        o_ref[KR_HI + pad_lo:KR_HI + pad_lo + pad_hi, :] = jnp.zeros((pad_hi, D_MODEL), o_ref.dtype)
        for j, src, n, dst in spans:
            o_ref[dst:dst + n, :] = g_ref[j, src:src + n, :]

    return pl.pallas_call(body, out_shape=SDS((D_INW, D_MODEL), g.dtype), name="w_in_working_t", compiler_params=_params())(g)


def _w_in_shard_5(d_ag_tail, d_cc, d_bg_head):
    kr = MLA_Q_RANK + MLA_KV_RANK + MLA_NOPE
    rows = jnp.concatenate([d_ag_tail, d_cc[:MLA_Q_RANK + MLA_KV_RANK], d_cc[kr:kr + MLA_ROPE], d_bg_head], 0)
    return rows.reshape(1, W_IN_SHARD, D_MODEL).astype(BF16)


def _w_uq_working(g):
    w = jnp.pad(g.transpose(1, 0, 2), ((0, 0), (0, 0), (0, LANES - MLA_NOPE - MLA_ROPE)))
    return w.reshape(MLA_Q_RANK, MLA_QW)


def _w_uq_parts(dw):
    return dw.reshape(MLA_Q_RANK, MLA_HEADS, LANES)[:, :, :MLA_NOPE + MLA_ROPE].transpose(1, 0, 2)


def _w_ukv_working(g):
    wk = jnp.pad(g[:, :, :MLA_NOPE].transpose(1, 0, 2), ((0, 0), (0, 0), (0, LANES - MLA_NOPE)))
    wv = g[:, :, MLA_NOPE:].transpose(1, 0, 2)
    return jnp.concatenate([wk.reshape(MLA_KV_RANK, MLA_QW), wv.reshape(MLA_KV_RANK, MLA_WIDTH)], 1)


def _w_ukv_parts(dw):
    dk = dw[:, :MLA_QW].reshape(MLA_KV_RANK, MLA_HEADS, LANES)[:, :, :MLA_NOPE]
    dv = dw[:, MLA_QW:].reshape(MLA_KV_RANK, MLA_HEADS, MLA_V)
    return jnp.concatenate([dk, dv], -1).transpose(1, 0, 2)


SMALL_NAMES = ("g_emb", "b_emb", "g_cq", "g_ckv", "g_out_a", "g_out_b", "g_out_m", "g_post", "b_post")


def kernel(x, mem, positions, g_emb, b_emb, w_in, g_cq, g_ckv, w_uq, w_ukv, w_mem_kv, g_out_a, g_out_b, g_out_m, w_out, g_post, b_post, loss_target, m_g_emb, m_b_emb, m_w_in, m_g_cq, m_g_ckv, m_w_uq, m_w_ukv, m_w_mem_kv, m_g_out_a, m_g_out_b, m_g_out_m, m_w_out, m_g_post, m_b_post, v_g_emb, v_b_emb, v_w_in, v_g_cq, v_g_ckv, v_w_uq, v_w_ukv, v_w_mem_kv, v_g_out_a, v_g_out_b, v_g_out_m, v_w_out, v_g_post, v_b_post):
    nb = x.shape[0]
    t = nb * SEQ
    x2 = x.reshape(t, D_MODEL)
    tgt2 = loss_target.reshape(t, D_MODEL)
    mem2 = mem.reshape(nb * N_MEM, D_MODEL)
    g_emb2, b_emb2 = g_emb.reshape(1, -1), b_emb.reshape(1, -1)

    w_in_t, m_w_in_t, v_w_in_t = w_in[0].T, m_w_in[0].T, v_w_in[0].T
    s_in, s_uq, s_ukv, s_mem, s_out = cast_shards((w_in_t, w_uq[0], w_ukv[0], w_mem_kv[0], w_out[0]))
    (h32, h16, (a_c, a_sa, a_sb), (m_c, m_sa, m_sb)), (g_in,) = embed_fwd(
        x2, g_emb2, b_emb2, positions, ((s_in,), None, (lax.empty((N_DEV,) + s_in.shape, BF16),)))
    win_t = w_in_working_t(g_in)

    proj = mm_nn(h16, win_t, F32, 2048, 1536, "proj", rhs_transposed=True)
    later = (s_uq, s_ukv, s_mem, s_out)
    (ya, lse_a), qkv_d, (g_uq, g_ukv, g_mem, g_out) = a_attn_fwd(
        proj, a_c, a_sa, a_sb, nb,
        (later, (ALL_DEVICES,) * len(later), tuple(lax.empty((N_DEV,) + w.shape, BF16) for w in later)))
    wuq_w = _w_uq_working(g_uq)
    wkv_w = _w_ukv_working(g_ukv)
    wmem = g_mem.reshape(D_MODEL, 2 * MEM_WIDTH)
    wout = g_out.reshape(D_MIX, D_MODEL)
    qb, kb, vb = mla_prep_fwd(proj, m_c, m_sa, m_sb, g_cq, g_ckv, wuq_w, wkv_w)
    yb, lse_b = mla_attn_fwd(qb, kb, vb, nb)
    mkv = mm_nn(mem2, wmem, BF16, nb * N_MEM, 512, "mem_kv")
    ym = mem_attn_fwd(proj, mkv, nb)
    z, du32, du16, loss_sum, dg_post, db_post = gate_out_ln_loss(
        ya, yb, ym, proj, g_out_a, g_out_b, g_out_m, wout, h32, tgt2, g_post, b_post)

    dya, dyb, dym, dag, dbg, dmg, dg_out_a, dg_out_b, dg_out_m = gate_bwd(
        du16, wout, ya, yb, ym, proj, g_out_a, g_out_b, g_out_m)
    dw_out = mm_tn(z, du16, 1024, "dw_out")
    dmq, dmk, dmv = mem_attn_bwd(proj, mkv, dym, nb)
    dw_mem = mm_tn(mem2, jnp.concatenate([dmk, dmv], 1), nb * N_MEM, "dw_mem")
    d_gates, shards_6_7 = mm_tn_group((dbg, dmq, dmg), h16, 2048, "dw_in_bg_mq_mg", W_IN_SHARD, BG_SPLIT, 2)
    landing = lambda w, dtype=F32: lax.empty((N_DEV,) + w.shape, dtype)
    big_w = (w_in_t, w_uq[0], w_ukv[0], w_mem_kv[0], w_out[0])
    (daq, dak, dav), (p_out, p_mem, p_in) = a_attn_bwd(
        qkv_d, a_c, a_sa, a_sb, dya, ya, lse_a, nb,
        ((dw_out.reshape(N_DEV, D_MIX // N_DEV, D_MODEL), dw_mem.reshape(N_DEV, D_MODEL // N_DEV, 2 * MEM_WIDTH),
          shards_6_7),
         (ALL_DEVICES, ALL_DEVICES, (6, 7)),
         (landing(w_out[0]), landing(w_mem_kv[0]), landing(w_in_t, BF16))))
    d_a, shards_0_4 = mm_tn_group((daq, dak, dav, dag), h16, 1024, "dw_in_aq_ak_av_ag", W_IN_SHARD, 0, 5)
    (dqb, dkb, dvb), (p_in,) = mla_attn_bwd(
        qb, kb, vb, dyb, yb, lse_b, nb, ((shards_0_4,), ((0, 1, 2, 3, 4),), (p_in,)))
    dcc, dqf, cqn, dkvf, ckvn, dg_cq, dg_ckv = mla_prep_bwd(proj, m_c, m_sa, m_sb, g_cq, g_ckv, wuq_w, wkv_w, dqb, dkb, dvb)
    dw_uq = mm_tn(cqn, dqf, 2048, "dw_uq")
    dw_ukv = mm_tn(ckvn, dkvf, 2048, "dw_ukv")
    d_cc = mm_tn(dcc, h16, 2048, "dw_in_cc")
    pieces = (daq, dak, dav, dag, dcc, dbg, dmq, dmg)
    (grad_x, dg_emb, db_emb), (p_in, p_uq, p_ukv) = dh_ln_bwd(
        pieces, win_t, du32, x2, g_emb2,
        ((_w_in_shard_5(d_a[5 * W_IN_SHARD:], d_cc, d_gates[:BG_SPLIT]), _w_uq_parts(dw_uq), _w_ukv_parts(dw_ukv)),
         ((5,), ALL_DEVICES, ALL_DEVICES),
         (p_in, landing(w_uq[0]), landing(w_ukv[0]))))

    parts = (p_in, p_uq, p_ukv, p_mem, p_out)
    big_m = (m_w_in_t, m_w_uq[0], m_w_ukv[0], m_w_mem_kv[0], m_w_out[0])
    big_v = (v_w_in_t, v_w_uq[0], v_w_ukv[0], v_w_mem_kv[0], v_w_out[0])
    big = {}
    for name, w, p, m, v in zip(("w_in", "w_uq", "w_ukv", "w_mem_kv", "w_out"), big_w, parts, big_m, big_v):
        res = adamw_shard(w, p, m, v, "adamw_" + name)
        big[name] = [(o.T if name == "w_in" else o)[None] for o in res]

    small_w = (g_emb, b_emb, g_cq, g_ckv, g_out_a, g_out_b, g_out_m, g_post, b_post)
    small_m = (m_g_emb, m_b_emb, m_g_cq, m_g_ckv, m_g_out_a, m_g_out_b, m_g_out_m, m_g_post, m_b_post)
    small_v = (v_g_emb, v_b_emb, v_g_cq, v_g_ckv, v_g_out_a, v_g_out_b, v_g_out_m, v_g_post, v_b_post)
    small_g = (dg_emb, db_emb, dg_cq, dg_ckv, dg_out_a, dg_out_b, dg_out_m, dg_post, db_post)
    rows128 = lambda vals: [v.reshape(-1, LANES) for v in vals]
    res = small_allreduce_adamw(loss_sum, rows128(small_g), rows128(small_w), rows128(small_m), rows128(small_v))
    loss = res[0][0, 0]
    n_small = len(small_w)
    sg, sd, sm, sv = [[r.reshape(w.shape) for r, w in zip(res[1 + k * n_small:1 + (k + 1) * n_small], small_w)]
                      for k in range(4)]

    order = ("g_emb", "b_emb", "w_in", "g_cq", "g_ckv", "w_uq", "w_ukv", "w_mem_kv", "g_out_a", "g_out_b", "g_out_m",
             "w_out", "g_post", "b_post")
    small_idx = {n: i for i, n in enumerate(SMALL_NAMES)}
    outs = [loss, grad_x.reshape(x.shape)]
    for kind in range(4):
        for name in order:
            outs.append(big[name][kind] if name in big else (sg, sd, sm, sv)[kind][small_idx[name]])
    return tuple(outs)
```

```python
import functools

import jax
import jax.numpy as jnp
from jax import lax
from jax.experimental import pallas as pl
from jax.experimental.pallas import tpu as pltpu

F32 = jnp.float32
BF16 = jnp.bfloat16
SDS = jax.ShapeDtypeStruct
MESH = pl.DeviceIdType.MESH

D_MODEL = 1024
SEQ = 2048
A_HEADS, A_HEAD_DIM, A_ROT = 16, 64, 16
A_WIDTH = 1024
DILATIONS = (1, 4, 16)
N_SIDE = 64
MLA_HEADS, MLA_Q_RANK, MLA_KV_RANK = 8, 256, 128
MLA_NOPE, MLA_ROPE, MLA_V = 64, 32, 64
MLA_WIDTH = 512
N_MEM, MEM_HEADS, MEM_HEAD_DIM, MEM_WIDTH = 256, 4, 128, 512
ROPE_THETA = 500000.0
NORM_EPS = 1e-5
NEG_INF = -1e30
ALPHA = 2.0 ** 0.25
D_IN = 6048
N_DEV = 8

ADAM_LR, ADAM_B1, ADAM_B2, ADAM_EPS, ADAM_WD, ADAM_STEP = 0.001, 0.9, 0.999, 1e-08, 0.01, 10

D_INW = 6144
PIECE_WIDTHS = (1024, 1024, 1024, 1024, 512, 512, 512, 512)
PIECE_OFFS = (0, 1024, 2048, 3072, 4096, 4608, 5120, 5632)
LANES = 128
VMEM_LIMIT = 56 * 1024 * 1024


def _params(*sem):
    kw = dict(vmem_limit_bytes=VMEM_LIMIT)
    if sem:
        kw["dimension_semantics"] = sem
    return pltpu.CompilerParams(**kw)


def _dot(a, b):
    return jnp.dot(a, b, preferred_element_type=F32)


def _dot_nt(a, b):
    return lax.dot_general(a, b, (((1,), (1,)), ((), ())), preferred_element_type=F32)


def _dot_tn(a, b):
    return lax.dot_general(a, b, (((0,), (0,)), ((), ())), preferred_element_type=F32)


def _sigmoid(x):
    return 1.0 / (1.0 + jnp.exp(-x))


def _rope_fwd(x, c, sa, sb, half):
    n = x.shape[-1]
    return x * c + pltpu.roll(x, n - half, 1) * sa + pltpu.roll(x, half, 1) * sb


def _rope_bwd(dy, c, sa, sb, half):
    n = dy.shape[-1]
    return dy * c + pltpu.roll(dy * sa, half, 1) + pltpu.roll(dy * sb, n - half, 1)


def mm_nn(a, b, out_dtype, tm, tn, name, rhs_transposed=False):
    m, k = a.shape
    n = b.shape[0] if rhs_transposed else b.shape[1]
    dot = _dot_nt if rhs_transposed else _dot

    def body(a_ref, b_ref, o_ref):
        o_ref[...] = dot(a_ref[...].astype(BF16), b_ref[...].astype(BF16)).astype(o_ref.dtype)

    b_spec = pl.BlockSpec((tn, k), lambda j, i: (j, 0)) if rhs_transposed else pl.BlockSpec((k, tn), lambda j, i: (0, j))
    return pl.pallas_call(
        body, grid=(n // tn, m // tm),
        in_specs=[pl.BlockSpec((tm, k), lambda j, i: (i, 0)), b_spec],
        out_specs=pl.BlockSpec((tm, tn), lambda j, i: (i, j)),
        out_shape=SDS((m, n), out_dtype), name=name,
        compiler_params=_params("parallel", "parallel"))(a, b)


def mm_tn(a, b, tt, name):
    t, m = a.shape
    n = b.shape[1]

    def body(a_ref, b_ref, o_ref):
        @pl.when(pl.program_id(0) == 0)
        def _():
            o_ref[...] = jnp.zeros_like(o_ref)

        o_ref[...] += _dot_tn(a_ref[...].astype(BF16), b_ref[...].astype(BF16))

    return pl.pallas_call(
        body, grid=(t // tt,),
        in_specs=[pl.BlockSpec((tt, m), lambda i: (i, 0)), pl.BlockSpec((tt, n), lambda i: (i, 0))],
        out_specs=pl.BlockSpec((m, n), lambda i: (0, 0)),
        out_shape=SDS((m, n), F32), name=name,
        compiler_params=_params("arbitrary"))(a, b)


def mm_tn_group(pieces, b, tt, name, slab_rows, first_slab_row, n_slabs):
    n, (t, w), cols = len(pieces), pieces[0].shape, b.shape[1]
    nt = t // tt

    def body(*refs):
        p_refs, b_ref, o_ref, slab_ref = refs[:n], refs[n], refs[n + 1], refs[n + 2]

        @pl.when(pl.program_id(1) == 0)
        def _():
            o_ref[...] = jnp.zeros_like(o_ref)

        for k in range(n):
            @pl.when(pl.program_id(0) == k)
            def _(k=k):
                o_ref[...] += _dot_tn(p_refs[k][...], b_ref[...])

            @pl.when((pl.program_id(0) == k) & (pl.program_id(1) == nt - 1))
            def _(k=k):
                for j in range(n_slabs):
                    lo = max(k * w, first_slab_row + j * slab_rows)
                    hi = min((k + 1) * w, first_slab_row + (j + 1) * slab_rows)
                    if lo < hi:
                        dst = lo - first_slab_row - j * slab_rows
                        slab_ref[j, dst:dst + hi - lo, :] = o_ref[lo - k * w:hi - k * w, :].astype(slab_ref.dtype)

    def piece_spec(k):
        return pl.BlockSpec((tt, w), lambda p, i: (jnp.where(p < k, 0, jnp.where(p > k, nt - 1, i)), 0))

    return pl.pallas_call(
        body, grid=(n, nt),
        in_specs=[piece_spec(k) for k in range(n)] + [pl.BlockSpec((tt, cols), lambda p, i: (i, 0))],
        out_specs=[pl.BlockSpec((w, cols), lambda p, i: (p, 0)),
                   pl.BlockSpec((n_slabs, slab_rows, cols), lambda p, i: (0, 0, 0))],
        out_shape=[SDS((n * w, cols), F32), SDS((n_slabs, slab_rows, cols), BF16)], name=name,
        compiler_params=_params("arbitrary", "arbitrary"))(*pieces, b)


def embed_fwd(x2, g, b, positions, xch):
    t, d = x2.shape
    tm = 512
    pos = positions.astype(F32).reshape(-1, 1)

    def body(x_ref, g_ref, b_ref, pos_ref, pat_ref, h32_ref, h16_ref, *tabs):
        x = x_ref[...]
        mu = jnp.mean(x, axis=-1, keepdims=True)
        xc = x - mu
        var = jnp.mean(xc * xc, axis=-1, keepdims=True)
        h = xc * lax.rsqrt(var + NORM_EPS) * g_ref[...] + b_ref[...]
        h32_ref[...] = h
        h16_ref[...] = h.astype(BF16)
        p = pos_ref[...]
        for k in range(2):
            inv, first, second = pat_ref[3 * k:3 * k + 1, :], pat_ref[3 * k + 1:3 * k + 2, :], pat_ref[3 * k + 2:3 * k + 3, :]
            ang = p * inv
            sn = jnp.sin(ang)
            tabs[3 * k][...] = jnp.where(first + second > 0.0, jnp.cos(ang), 1.0)
            tabs[3 * k + 1][...] = -first * sn
            tabs[3 * k + 2][...] = second * sn

    row = pl.BlockSpec((tm, d), lambda i: (i, 0))
    vec = pl.BlockSpec((1, d), lambda i: (0, 0))
    tab = pl.BlockSpec((tm, LANES), lambda i: (i, 0))
    res, landed = call_hosting_exchange(
        body, xch, grid=(t // tm,),
        in_specs=[row, vec, vec, pl.BlockSpec((tm, 1), lambda i: (i, 0)), pl.BlockSpec((8, LANES), lambda i: (0, 0))],
        out_specs=[row, row] + [tab] * 6,
        out_shape=[SDS((t, d), F32), SDS((t, d), BF16)] + [SDS((t, LANES), F32)] * 6,
        scratch_shapes=[], name="embed_fwd", operands=(x2, g, b, pos, _rope_lane_patterns()))
    return (res[0], res[1], tuple(res[2:5]), tuple(res[5:8])), landed


Q_BLK = 128
UNROLL_FWD = 16
UNROLL_BWD = 16


def _pattern_geometry(d):
    length = SEQ // d
    nblk = length // Q_BLK
    kwin = min(2 * Q_BLK, length)
    return length, nblk, kwin


def _block_coords(idx, d):
    length, nblk, kwin = _pattern_geometry(d)
    r = lax.shift_right_logical(idx, nblk.bit_length() - 1)
    i = idx & (nblk - 1)
    q0 = pl.multiple_of(r * length + i * Q_BLK, Q_BLK)
    ks = jnp.clip(i * Q_BLK - N_SIDE, 0, length - kwin)
    k0 = pl.multiple_of(r * length + ks, N_SIDE)
    qpos = i * Q_BLK + lax.broadcasted_iota(jnp.int32, (Q_BLK, kwin), 0)
    kpos = ks + lax.broadcasted_iota(jnp.int32, (Q_BLK, kwin), 1)
    valid = jnp.abs(kpos - qpos) <= N_SIDE
    return q0, k0, kwin, valid


def _deinterleave(src_ref, dst_ref, d, dtype, tmp_ref):
    if d == 1:
        dst_ref[...] = src_ref[...].astype(dtype)
        return
    q = SEQ // 4
    if d == 4:
        for r in range(4):
            dst_ref[r * q:(r + 1) * q, :] = src_ref[pl.ds(r, q, stride=4), :].astype(dtype)
        return
    assert d == 16
    n = SEQ // 16
    for r in range(4):
        tmp_ref[r * q:(r + 1) * q, :] = src_ref[pl.ds(r, q, stride=4), :]
    for r in range(4):
        for j in range(4):
            dst_ref[(r + 4 * j) * n:(r + 4 * j + 1) * n, :] = tmp_ref[pl.ds(r * q + j, n, stride=4), :].astype(dtype)


def _class16_to_class4(src_ref, dst_ref):
    q, n = SEQ // 4, SEQ // 16
    for r in range(4):
        for j in range(4):
            dst_ref[pl.ds(r * q + j, n, stride=4), :] = src_ref[(r + 4 * j) * n:(r + 4 * j + 1) * n, :]


def _interleave(src_ref, dst_ref, d, tmp_ref, accumulate):
    q = SEQ // 4
    if d == 16:
        _class16_to_class4(src_ref, tmp_ref)
        src_ref = tmp_ref
    else:
        assert d == 4
    for r in range(4):
        rows = pl.ds(r, q, stride=4)
        val = src_ref[r * q:(r + 1) * q, :]
        dst_ref[rows, :] = dst_ref[rows, :] + val if accumulate else val


def a_attn_fwd(proj, ca, sa, sb, nb, xch):
    t = proj.shape[0]
    n_pairs = A_WIDTH // LANES

    def body(q_ref, k_ref, v_ref, c_ref, sa_ref, sb_ref, y_ref, lse_ref, *rest):
        qkv_d, (qr_s, kr_s, oc_s, lc_s, o1_s, l1_s, o2_s, l2_s, o3_s, l3_s, tmp_s) = rest[:9], rest[9:]
        c, s_a, s_b = c_ref[...], sa_ref[...], sb_ref[...]
        qr_s[...] = _rope_fwd(q_ref[...], c, s_a, s_b, A_ROT // 2) * (A_HEAD_DIM ** -0.5)
        kr_s[...] = _rope_fwd(k_ref[...], c, s_a, s_b, A_ROT // 2)
        head0 = lax.broadcasted_iota(jnp.int32, (Q_BLK, LANES), 1) < A_HEAD_DIM
        nat = ((o1_s, l1_s), (o2_s, l2_s), (o3_s, l3_s))

        for g, d in enumerate(DILATIONS):
            qd_s, kd_s, vd_s = qkv_d[3 * g:3 * g + 3]
            _deinterleave(qr_s, qd_s, d, BF16, tmp_s)
            _deinterleave(kr_s, kd_s, d, BF16, tmp_s)
            _deinterleave(v_ref, vd_s, d, BF16, tmp_s)
            o_dst, l_dst = (nat[g] if d == 1 else (oc_s, lc_s))

            def block(idx, carry, d=d, o_dst=o_dst, l_dst=l_dst, qd_s=qd_s, kd_s=kd_s, vd_s=vd_s):
                q0, k0, kwin, valid = _block_coords(idx, d)
                qb = qd_s[pl.ds(q0, Q_BLK), :]
                kb = kd_s[pl.ds(k0, kwin), :]
                vb = vd_s[pl.ds(k0, kwin), :]
                zero = jnp.zeros_like(qb)
                q2 = jnp.concatenate([jnp.where(head0, qb, zero), jnp.where(head0, zero, qb)], 0)
                s = jnp.where(jnp.concatenate([valid, valid], 0), _dot_nt(q2, kb), NEG_INF)
                m = jnp.max(s, axis=-1, keepdims=True)
                p = jnp.exp(s - m)
                l = jnp.sum(p, axis=-1, keepdims=True)
                o2 = _dot(p.astype(BF16), vb) / l
                l2 = m + jnp.log(l)
                o_dst[pl.ds(q0, Q_BLK), :] = jnp.where(head0, o2[:Q_BLK], o2[Q_BLK:])
                l_dst[pl.ds(q0, Q_BLK), :] = jnp.where(head0, l2[:Q_BLK], l2[Q_BLK:])
                return carry

            lax.fori_loop(0, SEQ // Q_BLK, block, 0, unroll=UNROLL_FWD)
            if d > 1:
                _interleave(oc_s, nat[g][0], d, tmp_s, False)
                _interleave(lc_s, nat[g][1], d, tmp_s, False)

        def merge(ci, carry):
            rows = pl.ds(pl.multiple_of(ci * 256, 256), 256)
            l1, l2, l3 = l1_s[rows, :], l2_s[rows, :], l3_s[rows, :]
            m = jnp.maximum(jnp.maximum(l1, l2), l3)
            w1, w2, w3 = jnp.exp(l1 - m), jnp.exp(l2 - m), jnp.exp(l3 - m)
            w = w1 + w2 + w3
            y_ref[rows, :] = (w1 * o1_s[rows, :] + w2 * o2_s[rows, :] + w3 * o3_s[rows, :]) / w
            lse_ref[rows, :] = m + jnp.log(w)
            return carry

        lax.fori_loop(0, SEQ // 256, merge, 0)

    def col(off):
        return pl.BlockSpec((SEQ, LANES), lambda b, hp: (b, off + hp))

    tab = pl.BlockSpec((SEQ, LANES), lambda b, hp: (b, 0))
    out = pl.BlockSpec((SEQ, LANES), lambda b, hp: (b, hp))
    f32s = pltpu.VMEM((SEQ, LANES), F32)
    res, landed = call_hosting_exchange(
        body, xch, grid=(nb, n_pairs),
        in_specs=[col(0), col(n_pairs), col(2 * n_pairs), tab, tab, tab],
        out_specs=[out] * 11,
        out_shape=[SDS((t, A_WIDTH), F32)] * 2 + [SDS((t, A_WIDTH), BF16)] * 9,
        scratch_shapes=[f32s] * 11,
        name="a_attn_fwd", operands=(proj, proj, proj, ca, sa, sb))
    return res[:2], res[2:], landed


def a_attn_bwd(qkv_d, ca, sa, sb, dy, y, lse, nb, xch):
    t = dy.shape[0]
    n_pairs = A_WIDTH // LANES

    def body(*refs):
        qkv_refs = refs[:9]
        (c_ref, sa_ref, sb_ref, do_ref, y_ref, lse_ref, dq_ref, dk_ref, dv_ref,
         l0n_s, l1n_s, d0n_s, d1n_s, dod_s, l0d_s, l1d_s, d0d_s, d1d_s,
         dqc_s, dkc_s, dvc_s, dq4_s, dk4_s, dv4_s, dqn_s, dkn_s, dvn_s, tmp_s) = refs[9:]
        c, s_a, s_b = c_ref[...], sa_ref[...], sb_ref[...]
        head0 = lax.broadcasted_iota(jnp.int32, (Q_BLK, LANES), 1) < A_HEAD_DIM

        def per_head_rows(ci, carry):
            rows = pl.ds(pl.multiple_of(ci * 256, 256), 256)
            h0 = lax.broadcasted_iota(jnp.int32, (256, LANES), 1) < A_HEAD_DIM
            tt = do_ref[rows, :] * y_ref[rows, :]
            d0n_s[rows, :] = jnp.broadcast_to(jnp.sum(jnp.where(h0, tt, 0.0), axis=-1, keepdims=True), (256, LANES))
            d1n_s[rows, :] = jnp.broadcast_to(jnp.sum(jnp.where(h0, 0.0, tt), axis=-1, keepdims=True), (256, LANES))
            l = lse_ref[rows, :]
            lr = pltpu.roll(l, A_HEAD_DIM, 1)
            l0n_s[rows, :] = jnp.where(h0, l, lr)
            l1n_s[rows, :] = jnp.where(h0, lr, l)
            return carry

        lax.fori_loop(0, SEQ // 256, per_head_rows, 0)
        assert DILATIONS == (1, 4, 16)

        for g, d in enumerate(DILATIONS):
            qd_s, kd_s, vd_s = qkv_refs[3 * g:3 * g + 3]
            _deinterleave(do_ref, dod_s, d, BF16, tmp_s)
            if d > 1:
                for src, dst in ((l0n_s, l0d_s), (l1n_s, l1d_s), (d0n_s, d0d_s), (d1n_s, d1d_s)):
                    _deinterleave(src, dst, d, F32, tmp_s)
            l0, l1, d0, d1 = (l0n_s, l1n_s, d0n_s, d1n_s) if d == 1 else (l0d_s, l1d_s, d0d_s, d1d_s)
            dq_dst, dk_dst, dv_dst = {1: (dqn_s, dkn_s, dvn_s), 4: (dq4_s, dk4_s, dv4_s), 16: (dqc_s, dkc_s, dvc_s)}[d]
            dk_dst[...] = jnp.zeros_like(dk_dst)
            dv_dst[...] = jnp.zeros_like(dv_dst)

            def block(idx, carry, d=d, l0=l0, l1=l1, d0=d0, d1=d1, dq_dst=dq_dst, dk_dst=dk_dst, dv_dst=dv_dst,
                      qd_s=qd_s, kd_s=kd_s, vd_s=vd_s):
                q0, k0, kwin, valid = _block_coords(idx, d)
                qrows = pl.ds(q0, Q_BLK)
                krows = pl.ds(k0, kwin)
                qb, dob = qd_s[qrows, :], dod_s[qrows, :]
                kb, vb = kd_s[krows, :], vd_s[krows, :]
                zero = jnp.zeros_like(qb)
                q2 = jnp.concatenate([jnp.where(head0, qb, zero), jnp.where(head0, zero, qb)], 0)
                do2 = jnp.concatenate([jnp.where(head0, dob, zero), jnp.where(head0, zero, dob)], 0)
                wide = lambda x: jnp.concatenate([x] * (kwin // LANES), 1)
                lse2 = wide(jnp.concatenate([l0[qrows, :], l1[qrows, :]], 0))
                dd2 = wide(jnp.concatenate([d0[qrows, :], d1[qrows, :]], 0))
                s = jnp.where(jnp.concatenate([valid, valid], 0), _dot_nt(q2, kb), NEG_INF)
                p = jnp.exp(s - lse2)
                ds = (p * (_dot_nt(do2, vb) - dd2)).astype(BF16)
                dq2 = _dot(ds, kb)
                dq_dst[qrows, :] = jnp.where(head0, dq2[:Q_BLK], dq2[Q_BLK:])
                dk_dst[krows, :] += _dot_tn(ds, q2)
                dv_dst[krows, :] += _dot_tn(p.astype(BF16), do2)
                return carry

            lax.fori_loop(0, SEQ // Q_BLK, block, 0, unroll=UNROLL_BWD)

        for c16, c4, nat in ((dqc_s, dq4_s, dqn_s), (dkc_s, dk4_s, dkn_s), (dvc_s, dv4_s, dvn_s)):
            _class16_to_class4(c16, tmp_s)
            c4[...] = c4[...] + tmp_s[...]
            _interleave(c4, nat, 4, tmp_s, True)

        dq_ref[...] = _rope_bwd(dqn_s[...] * (A_HEAD_DIM ** -0.5), c, s_a, s_b, A_ROT // 2).astype(BF16)
        dk_ref[...] = _rope_bwd(dkn_s[...], c, s_a, s_b, A_ROT // 2).astype(BF16)
        dv_ref[...] = dvn_s[...].astype(BF16)

    tab = pl.BlockSpec((SEQ, LANES), lambda b, hp: (b, 0))
    blk = pl.BlockSpec((SEQ, LANES), lambda b, hp: (b, hp))
    f32s = pltpu.VMEM((SEQ, LANES), F32)
    b16s = pltpu.VMEM((SEQ, LANES), BF16)
    return call_hosting_exchange(
        body, xch, grid=(nb, n_pairs),
        in_specs=[blk] * 9 + [tab, tab, tab, blk, blk, blk],
        out_specs=[blk, blk, blk],
        out_shape=[SDS((t, A_WIDTH), BF16)] * 3,
        scratch_shapes=[f32s] * 4 + [b16s] + [f32s] * 14,
        name="a_attn_bwd", operands=(*qkv_d, ca, sa, sb, dy, y, lse))


MLA_SCALE = (MLA_NOPE + MLA_ROPE) ** -0.5
LOG2E = 1.4426950408889634
MLA_QW = MLA_HEADS * LANES
MLA_KVW = MLA_QW + MLA_WIDTH


def _rms(x, g):
    r = lax.rsqrt(jnp.mean(x * x, axis=-1, keepdims=True) + NORM_EPS)
    return x * r * g, r


def _rms_bwd(dn, x, r, g):
    tg = dn * g
    dx = r * tg - x * (r * r * r) * jnp.mean(tg * x, axis=-1, keepdims=True)
    return dx, jnp.sum(dn * x * r, axis=0, keepdims=True)


def mla_prep_fwd(proj, cm, sma, smb, g_cq, g_ckv, wuq, wkv):
    t = proj.shape[0]
    tm = 1024

    def body(cq_ref, ckv_ref, kr_ref, c_ref, sa_ref, sb_ref, gq_ref, gkv_ref, wuq_ref, wkv_ref, q_ref, k_ref, v_ref):
        c, s_a, s_b = c_ref[...], sa_ref[...], sb_ref[...]
        cqn, _ = _rms(cq_ref[...], gq_ref[...])
        qf = _dot(cqn.astype(BF16), wuq_ref[...])
        ckvn, _ = _rms(ckv_ref[...], gkv_ref[...])
        kvf = _dot(ckvn.astype(BF16), wkv_ref[...])
        krope = _rope_fwd(kr_ref[...], c, s_a, s_b, MLA_ROPE // 2)
        for h in range(MLA_HEADS):
            cols = slice(h * LANES, (h + 1) * LANES)
            q_ref[:, cols] = (_rope_fwd(qf[:, cols], c, s_a, s_b, MLA_ROPE // 2) * (MLA_SCALE * LOG2E)).astype(BF16)
            k_ref[:, cols] = (kvf[:, cols] + krope).astype(BF16)
        v_ref[...] = kvf[:, MLA_QW:].astype(BF16)

    def row(w, j):
        return pl.BlockSpec((tm, w), lambda i: (i, j))

    def full(a):
        return pl.BlockSpec(a.shape, lambda i: (0, 0))

    return pl.pallas_call(
        body, grid=(t // tm,),
        in_specs=[row(256, 4096 // 256), row(128, 4352 // 128), row(128, 4480 // 128), row(128, 0), row(128, 0), row(128, 0),
                  full(g_cq), full(g_ckv), full(wuq), full(wkv)],
        out_specs=[row(MLA_QW, 0), row(MLA_QW, 0), row(MLA_WIDTH, 0)],
        out_shape=[SDS((t, MLA_QW), BF16), SDS((t, MLA_QW), BF16), SDS((t, MLA_WIDTH), BF16)],
        name="mla_prep_fwd", compiler_params=_params("parallel"))(proj, proj, proj, cm, sma, smb, g_cq, g_ckv, wuq, wkv)


def mla_prep_bwd(proj, cm, sma, smb, g_cq, g_ckv, wuq, wkv, dq, dk, dv):
    t = proj.shape[0]
    tm = 1024

    def body(cq_ref, ckv_ref, c_ref, sa_ref, sb_ref, gq_ref, gkv_ref, wuq_ref, wkv_ref, dq_ref, dk_ref, dv_ref,
             dcc_ref, dqf_ref, cqn_ref, dkvf_ref, ckvn_ref, dgq_ref, dgkv_ref):
        @pl.when(pl.program_id(0) == 0)
        def _():
            dgq_ref[...] = jnp.zeros_like(dgq_ref)
            dgkv_ref[...] = jnp.zeros_like(dgkv_ref)

        c, s_a, s_b = c_ref[...], sa_ref[...], sb_ref[...]
        cq, ckv = cq_ref[...], ckv_ref[...]
        cqn, rq = _rms(cq, gq_ref[...])
        ckvn, rkv = _rms(ckv, gkv_ref[...])
        cqn_ref[...] = cqn.astype(BF16)
        ckvn_ref[...] = ckvn.astype(BF16)
        lane = lax.broadcasted_iota(jnp.int32, (tm, LANES), 1)
        rope_lanes = (lane >= MLA_NOPE) & (lane < MLA_NOPE + MLA_ROPE)
        dkrope = jnp.zeros((tm, LANES), F32)
        for h in range(MLA_HEADS):
            cols = slice(h * LANES, (h + 1) * LANES)
            dqf_ref[:, cols] = _rope_bwd(dq_ref[:, cols].astype(F32) * MLA_SCALE, c, s_a, s_b, MLA_ROPE // 2).astype(BF16)
            dkh = dk_ref[:, cols].astype(F32) * (1.0 / LOG2E)
            dkvf_ref[:, cols] = dkh.astype(BF16)
            dkrope = dkrope + dkh
        dkvf_ref[:, MLA_QW:] = dv_ref[...].astype(BF16)
        dkr = _rope_bwd(jnp.where(rope_lanes, dkrope, 0.0), c, s_a, s_b, MLA_ROPE // 2)
        dcqn = _dot_nt(dqf_ref[...], wuq_ref[...])
        dckvn = _dot_nt(dkvf_ref[...], wkv_ref[...])
        dcq, dgq = _rms_bwd(dcqn, cq, rq, gq_ref[...])
        dckv, dgkv = _rms_bwd(dckvn, ckv, rkv, gkv_ref[...])
        dgq_ref[...] += dgq
        dgkv_ref[...] += dgkv
        dcc_ref[:, 0:256] = dcq.astype(BF16)
        dcc_ref[:, 256:384] = dckv.astype(BF16)
        dcc_ref[:, 384:512] = dkr.astype(BF16)

    def row(w, j):
        return pl.BlockSpec((tm, w), lambda i: (i, j))

    def full(a):
        return pl.BlockSpec(a.shape, lambda i: (0, 0))

    return pl.pallas_call(
        body, grid=(t // tm,),
        in_specs=[row(256, 4096 // 256), row(128, 4352 // 128), row(128, 0), row(128, 0), row(128, 0),
                  full(g_cq), full(g_ckv), full(wuq), full(wkv), row(MLA_QW, 0), row(MLA_QW, 0), row(MLA_WIDTH, 0)],
        out_specs=[row(512, 0), row(MLA_QW, 0), row(256, 0), row(MLA_KVW, 0), row(128, 0), full(g_cq), full(g_ckv)],
        out_shape=[SDS((t, 512), BF16), SDS((t, MLA_QW), BF16), SDS((t, 256), BF16), SDS((t, MLA_KVW), BF16),
                   SDS((t, 128), BF16), SDS(g_cq.shape, F32), SDS(g_ckv.shape, F32)],
        name="mla_prep_bwd", compiler_params=_params("arbitrary"))(proj, proj, cm, sma, smb, g_cq, g_ckv, wuq, wkv, dq, dk, dv)


MLA_TQ = SEQ
MLA_SUB_FWD = 512
MLA_SUB_BWD = 512


def mla_attn_fwd(qb, kb, vb, nb):
    t = qb.shape[0]
    nq = SEQ // MLA_TQ
    n_pairs = MLA_HEADS // 2

    def body(q_ref, k_ref, v_ref, y_ref, lse_ref):
        head0 = lax.broadcasted_iota(jnp.int32, (MLA_SUB_FWD, LANES), 1) < MLA_V
        v = v_ref[...]
        vhead0 = lax.broadcasted_iota(jnp.int32, v.shape, 1) < MLA_V
        one = jnp.ones_like(v)
        vh = [jnp.where(vhead0 == (h == 0), v, one) for h in range(2)]
        for sub in range(MLA_TQ // MLA_SUB_FWD):
            rows = slice(sub * MLA_SUB_FWD, (sub + 1) * MLA_SUB_FWD)
            outs, lses = [], []
            for h in range(2):
                cols = slice(h * LANES, (h + 1) * LANES)
                s = _dot_nt(q_ref[rows, cols], k_ref[:, cols])
                m = jnp.max(s, axis=-1, keepdims=True)
                p = jnp.exp2(s - m).astype(BF16)
                ol = _dot(p, vh[h])
                l = pltpu.roll(ol, MLA_V, 1)
                outs.append(ol / l)
                lses.append(m + jnp.log2(l))
            y_ref[rows, :] = jnp.where(head0, outs[0], outs[1])
            lse_ref[rows, :] = jnp.where(head0, lses[0], lses[1])

    return pl.pallas_call(
        body, grid=(nb, n_pairs, nq),
        in_specs=[pl.BlockSpec((MLA_TQ, 2 * LANES), lambda b, hp, i: (b * nq + i, hp)),
                  pl.BlockSpec((SEQ, 2 * LANES), lambda b, hp, i: (b, hp)),
                  pl.BlockSpec((SEQ, LANES), lambda b, hp, i: (b, hp))],
        out_specs=[pl.BlockSpec((MLA_TQ, LANES), lambda b, hp, i: (b * nq + i, hp))] * 2,
        out_shape=[SDS((t, MLA_WIDTH), F32)] * 2,
        name="mla_attn_fwd", compiler_params=_params("parallel", "parallel", "parallel"))(qb, kb, vb)


def mla_attn_bwd(qb, kb, vb, dy, y, lse, nb, xch):
    t = qb.shape[0]
    nq = SEQ // MLA_TQ
    n_pairs = MLA_HEADS // 2

    assert nq == 1

    def body(q_ref, k_ref, v_ref, do_ref, y_ref, lse_ref, dq_ref, dk_ref, dv_ref, dk_s, dv_s):
        dk_s[...] = jnp.zeros_like(dk_s)
        dv_s[...] = jnp.zeros_like(dv_s)
        head0 = lax.broadcasted_iota(jnp.int32, (MLA_SUB_BWD, LANES), 1) < MLA_V
        v = v_ref[...]
        for sub in range(MLA_TQ // MLA_SUB_BWD):
            rows = slice(sub * MLA_SUB_BWD, (sub + 1) * MLA_SUB_BWD)
            do = do_ref[rows, :]
            lse = lse_ref[rows, :]
            tt = do * y_ref[rows, :]
            dv = jnp.zeros((SEQ, LANES), F32)
            for h in range(2):
                sel = head0 if h == 0 else ~head0
                lo = h * MLA_V
                cols = slice(h * LANES, (h + 1) * LANES)
                q = q_ref[rows, cols]
                k = k_ref[:, cols]
                dd = jnp.sum(jnp.where(sel, tt, 0.0), axis=-1, keepdims=True)
                doh = jnp.where(sel, do, 0.0).astype(BF16)
                p = jnp.exp2(_dot_nt(q, k) - lse[:, lo:lo + 1])
                dp = _dot_nt(doh, v)
                ds = (p * (dp - dd)).astype(BF16)
                dq_ref[rows, cols] = _dot(ds, k).astype(dq_ref.dtype)
                dk_s[:, cols] += _dot_tn(ds, q)
                dv = dv + _dot_tn(p.astype(BF16), doh)
            dv_s[...] += dv
        dk_ref[...] = dk_s[...].astype(dk_ref.dtype)
        dv_ref[...] = dv_s[...].astype(dv_ref.dtype)

    qspec = pl.BlockSpec((MLA_TQ, 2 * LANES), lambda b, hp, i: (b * nq + i, hp))
    kspec = pl.BlockSpec((SEQ, 2 * LANES), lambda b, hp, i: (b, hp))
    vspec = pl.BlockSpec((SEQ, LANES), lambda b, hp, i: (b, hp))
    ospec = pl.BlockSpec((MLA_TQ, LANES), lambda b, hp, i: (b * nq + i, hp))
    return call_hosting_exchange(
        body, xch, grid=(nb, n_pairs, nq),
        in_specs=[qspec, kspec, vspec, ospec, ospec, ospec],
        out_specs=[qspec, kspec, vspec],
        out_shape=[SDS((t, MLA_QW), BF16), SDS((t, MLA_QW), BF16), SDS((t, MLA_WIDTH), BF16)],
        scratch_shapes=[pltpu.VMEM((SEQ, 2 * LANES), F32), pltpu.VMEM((SEQ, LANES), F32)],
        name="mla_attn_bwd", operands=(qb, kb, vb, dy, y, lse))


MEM_TQ = SEQ
MEM_SUB = SEQ
MEM_SCALE = MEM_HEAD_DIM ** -0.5
MQ_BLK4 = 5120 // MEM_WIDTH


def mem_attn_fwd(proj, mkv, nb):
    t = proj.shape[0]
    nq = SEQ // MEM_TQ

    def body(q_ref, mk_ref, mv_ref, y_ref):
        for sub in range(MEM_TQ // MEM_SUB):
            rows = slice(sub * MEM_SUB, (sub + 1) * MEM_SUB)
            for h in range(MEM_HEADS):
                cols = slice(h * LANES, (h + 1) * LANES)
                s = _dot_nt(q_ref[rows, cols].astype(BF16), mk_ref[:, cols]) * MEM_SCALE
                m = jnp.max(s, axis=-1, keepdims=True)
                p = jnp.exp(s - m)
                l = jnp.sum(p, axis=-1, keepdims=True)
                y_ref[rows, cols] = _dot(p.astype(BF16), mv_ref[:, cols]) / l

    return pl.pallas_call(
        body, grid=(nb, nq),
        in_specs=[pl.BlockSpec((MEM_TQ, MEM_WIDTH), lambda b, i: (b * nq + i, MQ_BLK4)),
                  pl.BlockSpec((N_MEM, MEM_WIDTH), lambda b, i: (b, 0)),
                  pl.BlockSpec((N_MEM, MEM_WIDTH), lambda b, i: (b, 1))],
        out_specs=pl.BlockSpec((MEM_TQ, MEM_WIDTH), lambda b, i: (b * nq + i, 0)),
        out_shape=SDS((t, MEM_WIDTH), F32),
        name="mem_attn_fwd", compiler_params=_params("parallel", "parallel"))(proj, mkv, mkv)


def mem_attn_bwd(proj, mkv, dy, nb):
    t = proj.shape[0]
    nq = SEQ // MEM_TQ

    def body(q_ref, mk_ref, mv_ref, do_ref, dq_ref, dmk_ref, dmv_ref):
        @pl.when(pl.program_id(1) == 0)
        def _():
            dmk_ref[...] = jnp.zeros_like(dmk_ref)
            dmv_ref[...] = jnp.zeros_like(dmv_ref)

        for sub in range(MEM_TQ // MEM_SUB):
            rows = slice(sub * MEM_SUB, (sub + 1) * MEM_SUB)
            for h in range(MEM_HEADS):
                cols = slice(h * LANES, (h + 1) * LANES)
                q = q_ref[rows, cols].astype(BF16)
                mk, mv = mk_ref[:, cols], mv_ref[:, cols]
                do = do_ref[rows, cols].astype(BF16)
                s = _dot_nt(q, mk) * MEM_SCALE
                e = jnp.exp(s - jnp.max(s, axis=-1, keepdims=True))
                p = e / jnp.sum(e, axis=-1, keepdims=True)
                dp = _dot_nt(do, mv)
                ds = (p * (dp - jnp.sum(p * dp, axis=-1, keepdims=True)) * MEM_SCALE).astype(BF16)
                dq_ref[rows, cols] = _dot(ds, mk).astype(BF16)
                dmk_ref[:, cols] += _dot_tn(ds, q)
                dmv_ref[:, cols] += _dot_tn(p.astype(BF16), do)

    ospec = pl.BlockSpec((MEM_TQ, MEM_WIDTH), lambda b, i: (b * nq + i, 0))
    kspec = pl.BlockSpec((N_MEM, MEM_WIDTH), lambda b, i: (b, 0))
    return pl.pallas_call(
        body, grid=(nb, nq),
        in_specs=[pl.BlockSpec((MEM_TQ, MEM_WIDTH), lambda b, i: (b * nq + i, MQ_BLK4)),
                  kspec, pl.BlockSpec((N_MEM, MEM_WIDTH), lambda b, i: (b, 1)), ospec],
        out_specs=[ospec, kspec, kspec],
        out_shape=[SDS((t, MEM_WIDTH), BF16), SDS((nb * N_MEM, MEM_WIDTH), F32), SDS((nb * N_MEM, MEM_WIDTH), F32)],
        name="mem_attn_bwd", compiler_params=_params("parallel", "arbitrary"))(proj, mkv, mkv, dy)


ROW_TM = 512
ROW_SUB = 256
AG_BLK = 3072 // 1024
BG_BLK = 4608 // 512
MG_BLK = 5632 // 512
GROUPS = ((0, A_WIDTH), (A_WIDTH, MLA_WIDTH), (A_WIDTH + MLA_WIDTH, MEM_WIDTH))
D_MIX = 2048


def _gate_specs():
    def row(w, j):
        return pl.BlockSpec((ROW_TM, w), lambda i: (i, j))

    def vec(w):
        return pl.BlockSpec((1, w), lambda i: (0, 0))

    ys = [row(A_WIDTH, 0), row(MLA_WIDTH, 0), row(MEM_WIDTH, 0)]
    gates = [row(A_WIDTH, AG_BLK), row(MLA_WIDTH, BG_BLK), row(MEM_WIDTH, MG_BLK)]
    gains = [vec(A_WIDTH), vec(MLA_WIDTH), vec(MEM_WIDTH)]
    return row, vec, ys, gates, gains


def gate_out_ln_loss(ya, yb, ym, proj, goa, gob, gom, wout, h32, target, gp, bp):
    t, d = h32.shape
    _, _, ys, gates, gains = _gate_specs()

    def body(ya_ref, yb_ref, ym_ref, ga_ref, gb_ref, gm_ref, goa_ref, gob_ref, gom_ref, w_ref, h_ref, t_ref, gp_ref, bp_ref,
             z_ref, du32_ref, du16_ref, loss_ref, dgp_ref, dbp_ref):
        @pl.when(pl.program_id(0) == 0)
        def _():
            loss_ref[...] = jnp.zeros_like(loss_ref)
            dgp_ref[...] = jnp.zeros_like(dgp_ref)
            dbp_ref[...] = jnp.zeros_like(dbp_ref)

        for (off, w), y_ref, g_ref, go_ref in zip(GROUPS, (ya_ref, yb_ref, ym_ref), (ga_ref, gb_ref, gm_ref),
                                                  (goa_ref, gob_ref, gom_ref)):
            n, _ = _rms(y_ref[...], go_ref[...])
            gt = g_ref[...]
            z_ref[:, off:off + w] = (n * (gt * _sigmoid(gt))).astype(BF16)
        g = gp_ref[...]
        u = ALPHA * h_ref[...] + _dot(z_ref[...], w_ref[...])
        mu = jnp.mean(u, axis=-1, keepdims=True)
        uc = u - mu
        rstd = lax.rsqrt(jnp.mean(uc * uc, axis=-1, keepdims=True) + NORM_EPS)
        xhat = uc * rstd
        err = xhat * g + bp_ref[...] - t_ref[...]
        tok = jnp.sum(err * err, axis=-1, keepdims=True) * (1.0 / d)
        loss_ref[...] += 0.5 * jnp.sum(tok, axis=0, keepdims=True)
        dout = err * (1.0 / d)
        dxhat = dout * g
        du = rstd * (dxhat - jnp.mean(dxhat, axis=-1, keepdims=True)
                     - xhat * jnp.mean(dxhat * xhat, axis=-1, keepdims=True))
        du32_ref[...] = du
        du16_ref[...] = du.astype(BF16)
        dgp_ref[...] += jnp.sum(dout * xhat, axis=0, keepdims=True)
        dbp_ref[...] += jnp.sum(dout, axis=0, keepdims=True)

    row = pl.BlockSpec((ROW_TM, d), lambda i: (i, 0))
    vec = pl.BlockSpec((1, d), lambda i: (0, 0))
    zrow = pl.BlockSpec((ROW_TM, D_MIX), lambda i: (i, 0))
    return pl.pallas_call(
        body, grid=(t // ROW_TM,),
        in_specs=ys + gates + gains + [pl.BlockSpec((D_MIX, d), lambda i: (0, 0)), row, row, vec, vec],
        out_specs=[zrow, row, row, pl.BlockSpec((1, LANES), lambda i: (0, 0)), vec, vec],
        out_shape=[SDS((t, D_MIX), BF16), SDS((t, d), F32), SDS((t, d), BF16), SDS((1, LANES), F32), SDS((1, d), F32),
                   SDS((1, d), F32)],
        name="gate_out_ln_loss", compiler_params=_params("arbitrary"))(
            ya, yb, ym, proj, proj, proj, goa, gob, gom, wout, h32, target, gp, bp)


def gate_bwd(du16, wout, ya, yb, ym, proj, goa, gob, gom):
    t = ya.shape[0]
    row, vec, ys, gates, gains = _gate_specs()

    def body(du_ref, w_ref, ya_ref, yb_ref, ym_ref, ga_ref, gb_ref, gm_ref, goa_ref, gob_ref, gom_ref,
             dya_ref, dyb_ref, dym_ref, dga_ref, dgb_ref, dgm_ref, dgoa_ref, dgob_ref, dgom_ref):
        @pl.when(pl.program_id(0) == 0)
        def _():
            dgoa_ref[...] = jnp.zeros_like(dgoa_ref)
            dgob_ref[...] = jnp.zeros_like(dgob_ref)
            dgom_ref[...] = jnp.zeros_like(dgom_ref)

        for sub in range(ROW_TM // ROW_SUB):
            rows = slice(sub * ROW_SUB, (sub + 1) * ROW_SUB)
            dz = _dot_nt(du_ref[rows, :], w_ref[...])
            for (off, w), y_ref, g_ref, go_ref, dy_ref, dg_ref, dgo_ref in zip(
                    GROUPS, (ya_ref, yb_ref, ym_ref), (ga_ref, gb_ref, gm_ref), (goa_ref, gob_ref, gom_ref),
                    (dya_ref, dyb_ref, dym_ref), (dga_ref, dgb_ref, dgm_ref), (dgoa_ref, dgob_ref, dgom_ref)):
                dzg = dz[:, off:off + w]
                y, gt, go = y_ref[rows, :], g_ref[rows, :], go_ref[...]
                n, r = _rms(y, go)
                sg = _sigmoid(gt)
                dg_ref[rows, :] = (dzg * n * (sg * (1.0 + gt * (1.0 - sg)))).astype(BF16)
                dy, dgo = _rms_bwd(dzg * (gt * sg), y, r, go)
                dy_ref[rows, :] = dy
                dgo_ref[...] += dgo

    widths = (A_WIDTH, MLA_WIDTH, MEM_WIDTH)
    return pl.pallas_call(
        body, grid=(t // ROW_TM,),
        in_specs=[row(D_MODEL, 0), pl.BlockSpec((D_MIX, D_MODEL), lambda i: (0, 0))] + ys + gates + gains,
        out_specs=[row(w, 0) for w in widths] * 2 + [vec(w) for w in widths],
        out_shape=[SDS((t, w), F32) for w in widths] + [SDS((t, w), BF16) for w in widths] + [SDS((1, w), F32) for w in widths],
        name="gate_bwd", compiler_params=_params("arbitrary"))(du16, wout, ya, yb, ym, proj, proj, proj, goa, gob, gom)


def dh_ln_bwd(pieces, win_t, du32, x2, g_emb, xch):
    t, d = x2.shape

    def body(*refs):
        p_refs = refs[:len(pieces)]
        w_ref, du_ref, x_ref, g_ref, dx_ref, dg_ref, db_ref = refs[len(pieces):]

        @pl.when(pl.program_id(0) == 0)
        def _():
            dg_ref[...] = jnp.zeros_like(dg_ref)
            db_ref[...] = jnp.zeros_like(db_ref)

        dh = ALPHA * du_ref[...]
        for p_ref, off, w in zip(p_refs, PIECE_OFFS, PIECE_WIDTHS):
            dh = dh + _dot(p_ref[...], w_ref[off:off + w, :])
        x = x_ref[...]
        xc = x - jnp.mean(x, axis=-1, keepdims=True)
        rstd = lax.rsqrt(jnp.mean(xc * xc, axis=-1, keepdims=True) + NORM_EPS)
        xhat = xc * rstd
        dg_ref[...] += jnp.sum(dh * xhat, axis=0, keepdims=True)
        db_ref[...] += jnp.sum(dh, axis=0, keepdims=True)
        tg = dh * g_ref[...]
        dx_ref[...] = rstd * (tg - jnp.mean(tg, axis=-1, keepdims=True)
                              - xhat * jnp.mean(tg * xhat, axis=-1, keepdims=True))

    row = pl.BlockSpec((ROW_TM, d), lambda i: (i, 0))
    vec = pl.BlockSpec((1, d), lambda i: (0, 0))
    return call_hosting_exchange(
        body, xch, grid=(t // ROW_TM,),
        in_specs=[pl.BlockSpec((ROW_TM, w), lambda i: (i, 0)) for w in PIECE_WIDTHS]
        + [pl.BlockSpec(win_t.shape, lambda i: (0, 0)), row, row, vec],
        out_specs=[row, vec, vec],
        out_shape=[SDS((t, d), F32), SDS((1, d), F32), SDS((1, d), F32)],
        scratch_shapes=[], name="dh_ln_bwd", operands=(*pieces, win_t, du32, x2, g_emb))


def _adamw(w, g, m, v):
    m2 = ADAM_B1 * m + (1.0 - ADAM_B1) * g
    v2 = ADAM_B2 * v + (1.0 - ADAM_B2) * (g * g)
    m_hat = m2 / (1.0 - ADAM_B1 ** ADAM_STEP)
    v_hat = v2 / (1.0 - ADAM_B2 ** ADAM_STEP)
    return -ADAM_LR * (m_hat / (jnp.sqrt(v_hat) + ADAM_EPS) + ADAM_WD * w), m2, v2


def adamw_shard(w, parts, m, v, name):
    r, c = w.shape
    if r % 256 == 0 or r * c <= 256 * 1024:
        tr, tc = min(r, 256), c
    else:
        tr, tc = r, 256

    def body(w_ref, p_ref, m_ref, v_ref, g_ref, d_ref, nm_ref, nv_ref):
        g = p_ref[0].astype(F32)
        for k in range(1, N_DEV):
            g = g + p_ref[k].astype(F32)
        g_ref[...] = g
        d_ref[...], nm_ref[...], nv_ref[...] = _adamw(w_ref[...], g, m_ref[...], v_ref[...])

    blk = pl.BlockSpec((tr, tc), lambda i, j: (i, j))
    return pl.pallas_call(
        body, grid=(r // tr, c // tc),
        in_specs=[blk, pl.BlockSpec((N_DEV, tr, tc), lambda i, j: (0, i, j)), blk, blk],
        out_specs=[blk] * 4, out_shape=[SDS((r, c), F32)] * 4, name=name,
        compiler_params=_params("parallel", "parallel"))(w, parts, m, v)


def _place():
    return lax.axis_index("x"), lax.axis_index("y"), lax.axis_index("c")


def _flat(px, py, pc):
    return 4 * px + 2 * py + pc


def _peer(x, y, c, k):
    return (1 - x if k & 4 else x, 1 - y if k & 2 else y, 1 - c if k & 1 else c)


def cast_shards(shards):
    def body(*refs):
        n = len(refs) // 2
        for i_ref, o_ref in zip(refs[:n], refs[n:]):
            o_ref[...] = i_ref[...].astype(BF16)

    return pl.pallas_call(body, out_shape=[SDS(s.shape, BF16) for s in shards], name="cast_shards",
                          compiler_params=_params())(*shards)


def _two_level_gather_plan(src_refs, land_refs, send_sems, recv_sems, local_sems):
    n = len(src_refs)
    x, y, c = _place()
    me, sib = (x, y, c), (x, y, 1 - c)
    chips = [(1 - x, y), (x, 1 - y), (1 - x, 1 - y)]

    def copy(a, k, block, to, src=None):
        dst = land_refs[a].at[_flat(*block)]
        return pltpu.make_async_remote_copy(
            src_ref=dst if src is None else src, dst_ref=dst,
            send_sem=send_sems.at[a * N_DEV + k], recv_sem=recv_sems.at[a * N_DEV + k],
            device_id=to, device_id_type=MESH)

    mine = [pltpu.make_async_copy(src_refs[a], land_refs[a].at[_flat(*me)], local_sems.at[a]) for a in range(n)]
    first = []
    for a in range(n):
        first.append(copy(a, 0, me, sib, src=src_refs[a]))
        first += [copy(a, 1 + j, me, (*chip, c), src=src_refs[a]) for j, chip in enumerate(chips)]

    def start():
        for cp in mine + first:
            cp.start()

    def finish():
        passed = []
        for j, chip in enumerate(chips):
            for a in range(n):
                copy(a, 1 + j, (*chip, c), me).wait_recv()
                fwd = copy(a, 4 + j, (*chip, c), sib)
                fwd.start()
                passed.append(fwd)
        for a in range(n):
            copy(a, 0, sib, me).wait_recv()
            for j, chip in enumerate(chips):
                copy(a, 4 + j, (*chip, 1 - c), me).wait_recv()
        for cp in first + passed:
            cp.wait_send()
        for cp in mine:
            cp.wait()

    return start, finish


ALL_DEVICES = tuple(range(N_DEV))


def _exchange_plan(src_refs, land_refs, dests, send_sems, recv_sems, local_sems):
    x, y, c = _place()
    me = _flat(x, y, c)
    plan = []
    for a, (src, land, dl) in enumerate(zip(src_refs, land_refs, dests)):
        for li, j in enumerate(dl):
            to = ((j >> 2) & 1, (j >> 1) & 1, j & 1)
            block = src.at[li] if len(src.shape) == len(land.shape) else src

            def push(slot, a=a, block=block, land=land, j=j, to=to):
                return pltpu.make_async_remote_copy(
                    src_ref=block, dst_ref=land.at[slot], send_sem=send_sems.at[a * N_DEV + j],
                    recv_sem=recv_sems.at[a * N_DEV + slot], device_id=to, device_id_type=MESH)

            own = pltpu.make_async_copy(block, land.at[j], local_sems.at[a])
            plan.append((j, push(me), own, [push(s) for s in range(N_DEV) if s != j]))
    return me, plan


def _exchange_start(me, plan):
    for j, send, own, _ in plan:
        @pl.when(me != j)
        def _(send=send):
            send.start()

        @pl.when(me == j)
        def _(own=own):
            own.start()


def _exchange_wait(me, plan):
    for j, send, own, arrivals in plan:
        @pl.when(me != j)
        def _(send=send):
            send.wait_send()

        @pl.when(me == j)
        def _(own=own, arrivals=arrivals):
            own.wait()
            for arrival in arrivals:
                arrival.wait_recv()


def call_hosting_exchange(core, xch, *, grid, in_specs, out_specs, out_shape, scratch_shapes, name, operands):
    srcs, dests, landing = xch
    n, n_in, n_out, n_scr = len(srcs), len(in_specs), len(out_specs), len(scratch_shapes)

    def body(*refs):
        ins, src_refs = refs[:n_in], refs[n_in:n_in + n]
        outs = refs[n_in + 2 * n:n_in + 2 * n + n_out]
        land_refs = refs[n_in + 2 * n + n_out:n_in + 3 * n + n_out]
        scratch = refs[n_in + 3 * n + n_out:n_in + 3 * n + n_out + n_scr]
        sems = refs[n_in + 3 * n + n_out + n_scr:]
        first = functools.reduce(jnp.logical_and, [pl.program_id(i) == 0 for i in range(len(grid))])
        last = functools.reduce(jnp.logical_and, [pl.program_id(i) == grid[i] - 1 for i in range(len(grid))])
        if dests is None:
            start, finish = _two_level_gather_plan(src_refs, land_refs, *sems)
        else:
            me, plan = _exchange_plan(src_refs, land_refs, dests, *sems)
            start, finish = functools.partial(_exchange_start, me, plan), functools.partial(_exchange_wait, me, plan)
        pl.when(first)(start)
        core(*ins, *outs, *scratch)
        pl.when(last)(finish)

    hbm = pl.BlockSpec(memory_space=pl.ANY)
    res = pl.pallas_call(
        body, grid=grid,
        in_specs=list(in_specs) + [hbm] * (2 * n), out_specs=list(out_specs) + [hbm] * n,
        out_shape=list(out_shape) + [SDS(l.shape, l.dtype) for l in landing],
        scratch_shapes=list(scratch_shapes) + [pltpu.SemaphoreType.DMA((N_DEV * n,)), pltpu.SemaphoreType.DMA((N_DEV * n,)),
                                               pltpu.SemaphoreType.DMA((n,))],
        input_output_aliases={n_in + n + k: n_out + k for k in range(n)},
        name=name, compiler_params=_params(*(("arbitrary",) * len(grid))))(*operands, *srcs, *landing)
    return res[:n_out], res[n_out:]


SLOT_ROWS = 8


def small_allreduce_adamw(loss_sum, grads, ws, ms, vs):
    n = len(grads)
    rows = [g.shape[0] for g in grads]
    total = SLOT_ROWS * (n + 1)

    def body(*refs):
        loss_ref, g_refs, w_refs = refs[0], refs[1:1 + n], refs[1 + n:1 + 2 * n]
        m_refs, v_refs = refs[1 + 2 * n:1 + 3 * n], refs[1 + 3 * n:1 + 4 * n]
        outs = refs[1 + 4 * n:2 + 8 * n]
        vec, gath, tot, send_sems, recv_sems = refs[2 + 8 * n:]
        x, y, c = _place()
        me = _flat(x, y, c)
        vec[...] = jnp.zeros_like(vec)
        vec[0:1, :] = loss_ref[...]
        for i in range(n):
            vec[SLOT_ROWS * (i + 1):SLOT_ROWS * (i + 1) + rows[i], :] = g_refs[i][...]
        gath[me] = vec[...]
        copies = []
        for k in range(1, N_DEV):
            peer = _peer(x, y, c, k)
            copies.append(pltpu.make_async_remote_copy(
                src_ref=vec, dst_ref=gath.at[me], send_sem=send_sems.at[k - 1], recv_sem=recv_sems.at[k - 1],
                device_id=peer, device_id_type=MESH))
        for cp in copies:
            cp.start()
        for cp in copies:
            cp.wait_recv()
        for cp in copies:
            cp.wait_send()
        g = gath[0]
        for j in range(1, N_DEV):
            g = g + gath[j]
        tot[...] = g
        outs[0][...] = tot[0:1, :]
        for i in range(n):
            gi = tot[SLOT_ROWS * (i + 1):SLOT_ROWS * (i + 1) + rows[i], :]
            outs[1 + i][...] = gi
            outs[1 + n + i][...], outs[1 + 2 * n + i][...], outs[1 + 3 * n + i][...] = _adamw(
                w_refs[i][...], gi, m_refs[i][...], v_refs[i][...])

    shapes = [SDS(g.shape, F32) for g in grads]
    return pl.pallas_call(
        body, out_shape=[SDS((1, LANES), F32)] + shapes * 4,
        scratch_shapes=[pltpu.VMEM((total, LANES), F32), pltpu.VMEM((N_DEV, total, LANES), F32), pltpu.VMEM((total, LANES), F32),
                        pltpu.SemaphoreType.DMA((7,)), pltpu.SemaphoreType.DMA((7,))],
        name="small_allreduce_adamw", compiler_params=_params())(loss_sum, *grads, *ws, *ms, *vs)


def _rope_lane_patterns():
    inv = lambda r: ROPE_THETA ** (-(jnp.arange(0, r, 2, dtype=F32) / r))
    z = lambda n: jnp.zeros((n,), F32)
    o = lambda n: jnp.ones((n,), F32)
    half, rest = A_ROT // 2, A_HEAD_DIM - A_ROT
    ia, im = inv(A_ROT), inv(MLA_ROPE)
    mh, tail = MLA_ROPE // 2, LANES - MLA_NOPE - MLA_ROPE
    rows = [jnp.tile(jnp.concatenate([ia, ia, z(rest)]), 2),
            jnp.tile(jnp.concatenate([o(half), z(half + rest)]), 2),
            jnp.tile(jnp.concatenate([z(half), o(half), z(rest)]), 2),
            jnp.concatenate([z(MLA_NOPE), im, im, z(tail)]),
            jnp.concatenate([z(MLA_NOPE), o(mh), z(mh + tail)]),
            jnp.concatenate([z(MLA_NOPE + mh), o(mh), z(tail)]),
            z(LANES), z(LANES)]
    return jnp.stack(rows)


KR_LO, KR_HI = 4480, 4512
W_IN_SHARD = D_IN // N_DEV
BG_SPLIT = 6 * W_IN_SHARD - KR_HI


def w_in_working_t(g):
    pad_lo, pad_hi = MLA_NOPE, LANES - MLA_NOPE - MLA_ROPE
    spans = []
    for lo, hi, shift in ((0, KR_LO, 0), (KR_LO, KR_HI, pad_lo), (KR_HI, D_IN, pad_lo + pad_hi)):
        r = lo
        while r < hi:
            j = r // W_IN_SHARD
            n = min(hi, (j + 1) * W_IN_SHARD) - r
            spans.append((j, r - j * W_IN_SHARD, n, r + shift))
            r += n

    def body(g_ref, o_ref):
        o_ref[KR_LO:KR_LO + pad_lo, :] = jnp.zeros((pad_lo, D_MODEL), o_ref.dtype)
        o_ref[KR_HI + pad_lo:KR_HI + pad_lo + pad_hi, :] = jnp.zeros((pad_hi, D_MODEL), o_ref.dtype)
        for j, src, n, dst in spans:
            o_ref[dst:dst + n, :] = g_ref[j, src:src + n, :]

    return pl.pallas_call(body, out_shape=SDS((D_INW, D_MODEL), g.dtype), name="w_in_working_t", compiler_params=_params())(g)


def _w_in_shard_5(d_ag_tail, d_cc, d_bg_head):
    kr = MLA_Q_RANK + MLA_KV_RANK + MLA_NOPE
    rows = jnp.concatenate([d_ag_tail, d_cc[:MLA_Q_RANK + MLA_KV_RANK], d_cc[kr:kr + MLA_ROPE], d_bg_head], 0)
    return rows.reshape(1, W_IN_SHARD, D_MODEL).astype(BF16)


def _w_uq_working(g):
    w = jnp.pad(g.transpose(1, 0, 2), ((0, 0), (0, 0), (0, LANES - MLA_NOPE - MLA_ROPE)))
    return w.reshape(MLA_Q_RANK, MLA_QW)


def _w_uq_parts(dw):
    return dw.reshape(MLA_Q_RANK, MLA_HEADS, LANES)[:, :, :MLA_NOPE + MLA_ROPE].transpose(1, 0, 2)


def _w_ukv_working(g):
    wk = jnp.pad(g[:, :, :MLA_NOPE].transpose(1, 0, 2), ((0, 0), (0, 0), (0, LANES - MLA_NOPE)))
    wv = g[:, :, MLA_NOPE:].transpose(1, 0, 2)
    return jnp.concatenate([wk.reshape(MLA_KV_RANK, MLA_QW), wv.reshape(MLA_KV_RANK, MLA_WIDTH)], 1)


def _w_ukv_parts(dw):
    dk = dw[:, :MLA_QW].reshape(MLA_KV_RANK, MLA_HEADS, LANES)[:, :, :MLA_NOPE]
    dv = dw[:, MLA_QW:].reshape(MLA_KV_RANK, MLA_HEADS, MLA_V)
    return jnp.concatenate([dk, dv], -1).transpose(1, 0, 2)


SMALL_NAMES = ("g_emb", "b_emb", "g_cq", "g_ckv", "g_out_a", "g_out_b", "g_out_m", "g_post", "b_post")


def kernel(x, mem, positions, g_emb, b_emb, w_in, g_cq, g_ckv, w_uq, w_ukv, w_mem_kv, g_out_a, g_out_b, g_out_m, w_out, g_post, b_post, loss_target, m_g_emb, m_b_emb, m_w_in, m_g_cq, m_g_ckv, m_w_uq, m_w_ukv, m_w_mem_kv, m_g_out_a, m_g_out_b, m_g_out_m, m_w_out, m_g_post, m_b_post, v_g_emb, v_b_emb, v_w_in, v_g_cq, v_g_ckv, v_w_uq, v_w_ukv, v_w_mem_kv, v_g_out_a, v_g_out_b, v_g_out_m, v_w_out, v_g_post, v_b_post):
    nb = x.shape[0]
    t = nb * SEQ
    x2 = x.reshape(t, D_MODEL)
    tgt2 = loss_target.reshape(t, D_MODEL)
    mem2 = mem.reshape(nb * N_MEM, D_MODEL)
    g_emb2, b_emb2 = g_emb.reshape(1, -1), b_emb.reshape(1, -1)

    w_in_t, m_w_in_t, v_w_in_t = w_in[0].T, m_w_in[0].T, v_w_in[0].T
    s_in, s_uq, s_ukv, s_mem, s_out = cast_shards((w_in_t, w_uq[0], w_ukv[0], w_mem_kv[0], w_out[0]))
    (h32, h16, (a_c, a_sa, a_sb), (m_c, m_sa, m_sb)), (g_in,) = embed_fwd(
        x2, g_emb2, b_emb2, positions, ((s_in,), None, (lax.empty((N_DEV,) + s_in.shape, BF16),)))
    win_t = w_in_working_t(g_in)

    proj = mm_nn(h16, win_t, F32, 2048, 1536, "proj", rhs_transposed=True)
    later = (s_uq, s_ukv, s_mem, s_out)
    (ya, lse_a), qkv_d, (g_uq, g_ukv, g_mem, g_out) = a_attn_fwd(
        proj, a_c, a_sa, a_sb, nb,
        (later, (ALL_DEVICES,) * len(later), tuple(lax.empty((N_DEV,) + w.shape, BF16) for w in later)))
    wuq_w = _w_uq_working(g_uq)
    wkv_w = _w_ukv_working(g_ukv)
    wmem = g_mem.reshape(D_MODEL, 2 * MEM_WIDTH)
    wout = g_out.reshape(D_MIX, D_MODEL)
    qb, kb, vb = mla_prep_fwd(proj, m_c, m_sa, m_sb, g_cq, g_ckv, wuq_w, wkv_w)
    yb, lse_b = mla_attn_fwd(qb, kb, vb, nb)
    mkv = mm_nn(mem2, wmem, BF16, nb * N_MEM, 512, "mem_kv")
    ym = mem_attn_fwd(proj, mkv, nb)
    z, du32, du16, loss_sum, dg_post, db_post = gate_out_ln_loss(
        ya, yb, ym, proj, g_out_a, g_out_b, g_out_m, wout, h32, tgt2, g_post, b_post)

    dya, dyb, dym, dag, dbg, dmg, dg_out_a, dg_out_b, dg_out_m = gate_bwd(
        du16, wout, ya, yb, ym, proj, g_out_a, g_out_b, g_out_m)
    dw_out = mm_tn(z, du16, 2048, "dw_out")
    dmq, dmk, dmv = mem_attn_bwd(proj, mkv, dym, nb)
    dw_mem = mm_tn(mem2, jnp.concatenate([dmk, dmv], 1), nb * N_MEM, "dw_mem")
    d_gates, shards_6_7 = mm_tn_group((dbg, dmq, dmg), h16, 2048, "dw_in_bg_mq_mg", W_IN_SHARD, BG_SPLIT, 2)
    landing = lambda w, dtype=F32: lax.empty((N_DEV,) + w.shape, dtype)
    big_w = (w_in_t, w_uq[0], w_ukv[0], w_mem_kv[0], w_out[0])
    (daq, dak, dav), (p_out, p_mem, p_in) = a_attn_bwd(
        qkv_d, a_c, a_sa, a_sb, dya, ya, lse_a, nb,
        ((dw_out.reshape(N_DEV, D_MIX // N_DEV, D_MODEL), dw_mem.reshape(N_DEV, D_MODEL // N_DEV, 2 * MEM_WIDTH),
          shards_6_7),
         (ALL_DEVICES, ALL_DEVICES, (6, 7)),
         (landing(w_out[0]), landing(w_mem_kv[0]), landing(w_in_t, BF16))))
    d_a, shards_0_4 = mm_tn_group((daq, dak, dav, dag), h16, 1024, "dw_in_aq_ak_av_ag", W_IN_SHARD, 0, 5)
    (dqb, dkb, dvb), (p_in,) = mla_attn_bwd(
        qb, kb, vb, dyb, yb, lse_b, nb, ((shards_0_4,), ((0, 1, 2, 3, 4),), (p_in,)))
    dcc, dqf, cqn, dkvf, ckvn, dg_cq, dg_ckv = mla_prep_bwd(proj, m_c, m_sa, m_sb, g_cq, g_ckv, wuq_w, wkv_w, dqb, dkb, dvb)
    dw_uq = mm_tn(cqn, dqf, 2048, "dw_uq")
    dw_ukv = mm_tn(ckvn, dkvf, 2048, "dw_ukv")
    d_cc = mm_tn(dcc, h16, 2048, "dw_in_cc")
    pieces = (daq, dak, dav, dag, dcc, dbg, dmq, dmg)
    (grad_x, dg_emb, db_emb), (p_in, p_uq, p_ukv) = dh_ln_bwd(
        pieces, win_t, du32, x2, g_emb2,
        ((_w_in_shard_5(d_a[5 * W_IN_SHARD:], d_cc, d_gates[:BG_SPLIT]), _w_uq_parts(dw_uq), _w_ukv_parts(dw_ukv)),
         ((5,), ALL_DEVICES, ALL_DEVICES),
         (p_in, landing(w_uq[0]), landing(w_ukv[0]))))

    parts = (p_in, p_uq, p_ukv, p_mem, p_out)
    big_m = (m_w_in_t, m_w_uq[0], m_w_ukv[0], m_w_mem_kv[0], m_w_out[0])
    big_v = (v_w_in_t, v_w_uq[0], v_w_ukv[0], v_w_mem_kv[0], v_w_out[0])
    big = {}
    for name, w, p, m, v in zip(("w_in", "w_uq", "w_ukv", "w_mem_kv", "w_out"), big_w, parts, big_m, big_v):
        res = adamw_shard(w, p, m, v, "adamw_" + name)
        big[name] = [(o.T if name == "w_in" else o)[None] for o in res]

    small_w = (g_emb, b_emb, g_cq, g_ckv, g_out_a, g_out_b, g_out_m, g_post, b_post)
    small_m = (m_g_emb, m_b_emb, m_g_cq, m_g_ckv, m_g_out_a, m_g_out_b, m_g_out_m, m_g_post, m_b_post)
    small_v = (v_g_emb, v_b_emb, v_g_cq, v_g_ckv, v_g_out_a, v_g_out_b, v_g_out_m, v_g_post, v_b_post)
    small_g = (dg_emb, db_emb, dg_cq, dg_ckv, dg_out_a, dg_out_b, dg_out_m, dg_post, db_post)
    rows128 = lambda vals: [v.reshape(-1, LANES) for v in vals]
    res = small_allreduce_adamw(loss_sum, rows128(small_g), rows128(small_w), rows128(small_m), rows128(small_v))
    loss = res[0][0, 0]
    n_small = len(small_w)
    sg, sd, sm, sv = [[r.reshape(w.shape) for r, w in zip(res[1 + k * n_small:1 + (k + 1) * n_small], small_w)]
                      for k in range(4)]

    order = ("g_emb", "b_emb", "w_in", "g_cq", "g_ckv", "w_uq", "w_ukv", "w_mem_kv", "g_out_a", "g_out_b", "g_out_m",
             "w_out", "g_post", "b_post")
    small_idx = {n: i for i, n in enumerate(SMALL_NAMES)}
    outs = [loss, grad_x.reshape(x.shape)]
    for kind in range(4):
        for name in order:
            outs.append(big[name][kind] if name in big else (sg, sd, sm, sv)[kind][small_idx[name]])
    return tuple(outs)
```

```python
import functools

import jax
import jax.numpy as jnp
from jax import lax
from jax.experimental import pallas as pl
from jax.experimental.pallas import tpu as pltpu

F32 = jnp.float32
BF16 = jnp.bfloat16
SDS = jax.ShapeDtypeStruct
MESH = pl.DeviceIdType.MESH

D_MODEL = 1024
SEQ = 2048
A_HEADS, A_HEAD_DIM, A_ROT = 16, 64, 16
A_WIDTH = 1024
DILATIONS = (1, 4, 16)
N_SIDE = 64
MLA_HEADS, MLA_Q_RANK, MLA_KV_RANK = 8, 256, 128
MLA_NOPE, MLA_ROPE, MLA_V = 64, 32, 64
MLA_WIDTH = 512
N_MEM, MEM_HEADS, MEM_HEAD_DIM, MEM_WIDTH = 256, 4, 128, 512
ROPE_THETA = 500000.0
NORM_EPS = 1e-5
NEG_INF = -1e30
ALPHA = 2.0 ** 0.25
D_IN = 6048
N_DEV = 8

ADAM_LR, ADAM_B1, ADAM_B2, ADAM_EPS, ADAM_WD, ADAM_STEP = 0.001, 0.9, 0.999, 1e-08, 0.01, 10

D_INW = 6144
PIECE_WIDTHS = (1024, 1024, 1024, 1024, 512, 512, 512, 512)
PIECE_OFFS = (0, 1024, 2048, 3072, 4096, 4608, 5120, 5632)
LANES = 128
VMEM_LIMIT = 56 * 1024 * 1024


def _params(*sem):
    kw = dict(vmem_limit_bytes=VMEM_LIMIT)
    if sem:
        kw["dimension_semantics"] = sem
    return pltpu.CompilerParams(**kw)


def _dot(a, b):
    return jnp.dot(a, b, preferred_element_type=F32)


def _dot_nt(a, b):
    return lax.dot_general(a, b, (((1,), (1,)), ((), ())), preferred_element_type=F32)


def _dot_tn(a, b):
    return lax.dot_general(a, b, (((0,), (0,)), ((), ())), preferred_element_type=F32)


def _sigmoid(x):
    return 1.0 / (1.0 + jnp.exp(-x))


def _rope_fwd(x, c, sa, sb, half):
    n = x.shape[-1]
    return x * c + pltpu.roll(x, n - half, 1) * sa + pltpu.roll(x, half, 1) * sb


def _rope_bwd(dy, c, sa, sb, half):
    n = dy.shape[-1]
    return dy * c + pltpu.roll(dy * sa, half, 1) + pltpu.roll(dy * sb, n - half, 1)


def mm_nn(a, b, out_dtype, tm, tn, name, rhs_transposed=False):
    m, k = a.shape
    n = b.shape[0] if rhs_transposed else b.shape[1]
    dot = _dot_nt if rhs_transposed else _dot

    def body(a_ref, b_ref, o_ref):
        o_ref[...] = dot(a_ref[...].astype(BF16), b_ref[...].astype(BF16)).astype(o_ref.dtype)

    b_spec = pl.BlockSpec((tn, k), lambda j, i: (j, 0)) if rhs_transposed else pl.BlockSpec((k, tn), lambda j, i: (0, j))
    return pl.pallas_call(
        body, grid=(n // tn, m // tm),
        in_specs=[pl.BlockSpec((tm, k), lambda j, i: (i, 0)), b_spec],
        out_specs=pl.BlockSpec((tm, tn), lambda j, i: (i, j)),
        out_shape=SDS((m, n), out_dtype), name=name,
        compiler_params=_params("parallel", "parallel"))(a, b)


def mm_tn(a, b, tt, name):
    t, m = a.shape
    n = b.shape[1]

    def body(a_ref, b_ref, o_ref):
        @pl.when(pl.program_id(0) == 0)
        def _():
            o_ref[...] = jnp.zeros_like(o_ref)

        o_ref[...] += _dot_tn(a_ref[...].astype(BF16), b_ref[...].astype(BF16))

    return pl.pallas_call(
        body, grid=(t // tt,),
        in_specs=[pl.BlockSpec((tt, m), lambda i: (i, 0)), pl.BlockSpec((tt, n), lambda i: (i, 0))],
        out_specs=pl.BlockSpec((m, n), lambda i: (0, 0)),
        out_shape=SDS((m, n), F32), name=name,
        compiler_params=_params("arbitrary"))(a, b)


def mm_tn_group(pieces, b, tt, name, slab_rows, first_slab_row, n_slabs):
    n, (t, w), cols = len(pieces), pieces[0].shape, b.shape[1]
    nt = t // tt

    def body(*refs):
        p_refs, b_ref, o_ref, slab_ref = refs[:n], refs[n], refs[n + 1], refs[n + 2]

        @pl.when(pl.program_id(1) == 0)
        def _():
            o_ref[...] = jnp.zeros_like(o_ref)

        for k in range(n):
            @pl.when(pl.program_id(0) == k)
            def _(k=k):
                o_ref[...] += _dot_tn(p_refs[k][...], b_ref[...])

            @pl.when((pl.program_id(0) == k) & (pl.program_id(1) == nt - 1))
            def _(k=k):
                for j in range(n_slabs):
                    lo = max(k * w, first_slab_row + j * slab_rows)
                    hi = min((k + 1) * w, first_slab_row + (j + 1) * slab_rows)
                    if lo < hi:
                        dst = lo - first_slab_row - j * slab_rows
                        slab_ref[j, dst:dst + hi - lo, :] = o_ref[lo - k * w:hi - k * w, :].astype(slab_ref.dtype)

    def piece_spec(k):
        return pl.BlockSpec((tt, w), lambda p, i: (jnp.where(p < k, 0, jnp.where(p > k, nt - 1, i)), 0))

    return pl.pallas_call(
        body, grid=(n, nt),
        in_specs=[piece_spec(k) for k in range(n)] + [pl.BlockSpec((tt, cols), lambda p, i: (i, 0))],
        out_specs=[pl.BlockSpec((w, cols), lambda p, i: (p, 0)),
                   pl.BlockSpec((n_slabs, slab_rows, cols), lambda p, i: (0, 0, 0))],
        out_shape=[SDS((n * w, cols), F32), SDS((n_slabs, slab_rows, cols), BF16)], name=name,
        compiler_params=_params("arbitrary", "arbitrary"))(*pieces, b)


def embed_fwd(x2, g, b, positions, xch):
    t, d = x2.shape
    tm = 512
    pos = positions.astype(F32).reshape(-1, 1)

    def body(x_ref, g_ref, b_ref, pos_ref, pat_ref, h32_ref, h16_ref, *tabs):
        x = x_ref[...]
        mu = jnp.mean(x, axis=-1, keepdims=True)
        xc = x - mu
        var = jnp.mean(xc * xc, axis=-1, keepdims=True)
        h = xc * lax.rsqrt(var + NORM_EPS) * g_ref[...] + b_ref[...]
        h32_ref[...] = h
        h16_ref[...] = h.astype(BF16)
        p = pos_ref[...]
        for k in range(2):
            inv, first, second = pat_ref[3 * k:3 * k + 1, :], pat_ref[3 * k + 1:3 * k + 2, :], pat_ref[3 * k + 2:3 * k + 3, :]
            ang = p * inv
            sn = jnp.sin(ang)
            tabs[3 * k][...] = jnp.where(first + second > 0.0, jnp.cos(ang), 1.0)
            tabs[3 * k + 1][...] = -first * sn
            tabs[3 * k + 2][...] = second * sn

    row = pl.BlockSpec((tm, d), lambda i: (i, 0))
    vec = pl.BlockSpec((1, d), lambda i: (0, 0))
    tab = pl.BlockSpec((tm, LANES), lambda i: (i, 0))
    res, landed = call_hosting_exchange(
        body, xch, grid=(t // tm,),
        in_specs=[row, vec, vec, pl.BlockSpec((tm, 1), lambda i: (i, 0)), pl.BlockSpec((8, LANES), lambda i: (0, 0))],
        out_specs=[row, row] + [tab] * 6,
        out_shape=[SDS((t, d), F32), SDS((t, d), BF16)] + [SDS((t, LANES), F32)] * 6,
        scratch_shapes=[], name="embed_fwd", operands=(x2, g, b, pos, _rope_lane_patterns()))
    return (res[0], res[1], tuple(res[2:5]), tuple(res[5:8])), landed


Q_BLK = 128
UNROLL_FWD = 16
UNROLL_BWD = 16


def _pattern_geometry(d):
    length = SEQ // d
    nblk = length // Q_BLK
    kwin = min(2 * Q_BLK, length)
    return length, nblk, kwin


def _block_coords(idx, d):
    length, nblk, kwin = _pattern_geometry(d)
    r = lax.shift_right_logical(idx, nblk.bit_length() - 1)
    i = idx & (nblk - 1)
    q0 = pl.multiple_of(r * length + i * Q_BLK, Q_BLK)
    ks = jnp.clip(i * Q_BLK - N_SIDE, 0, length - kwin)
    k0 = pl.multiple_of(r * length + ks, N_SIDE)
    qpos = i * Q_BLK + lax.broadcasted_iota(jnp.int32, (Q_BLK, kwin), 0)
    kpos = ks + lax.broadcasted_iota(jnp.int32, (Q_BLK, kwin), 1)
    valid = jnp.abs(kpos - qpos) <= N_SIDE
    return q0, k0, kwin, valid


def _deinterleave(src_ref, dst_ref, d, dtype, tmp_ref):
    if d == 1:
        dst_ref[...] = src_ref[...].astype(dtype)
        return
    q = SEQ // 4
    if d == 4:
        for r in range(4):
            dst_ref[r * q:(r + 1) * q, :] = src_ref[pl.ds(r, q, stride=4), :].astype(dtype)
        return
    assert d == 16
    n = SEQ // 16
    for r in range(4):
        tmp_ref[r * q:(r + 1) * q, :] = src_ref[pl.ds(r, q, stride=4), :]
    for r in range(4):
        for j in range(4):
            dst_ref[(r + 4 * j) * n:(r + 4 * j + 1) * n, :] = tmp_ref[pl.ds(r * q + j, n, stride=4), :].astype(dtype)


def _class16_to_class4(src_ref, dst_ref):
    q, n = SEQ // 4, SEQ // 16
    for r in range(4):
        for j in range(4):
            dst_ref[pl.ds(r * q + j, n, stride=4), :] = src_ref[(r + 4 * j) * n:(r + 4 * j + 1) * n, :]


def _interleave(src_ref, dst_ref, d, tmp_ref, accumulate):
    q = SEQ // 4
    if d == 16:
        _class16_to_class4(src_ref, tmp_ref)
        src_ref = tmp_ref
    else:
        assert d == 4
    for r in range(4):
        rows = pl.ds(r, q, stride=4)
        val = src_ref[r * q:(r + 1) * q, :]
        dst_ref[rows, :] = dst_ref[rows, :] + val if accumulate else val


def a_attn_fwd(proj, ca, sa, sb, nb, xch):
    t = proj.shape[0]
    n_pairs = A_WIDTH // LANES

    def body(q_ref, k_ref, v_ref, c_ref, sa_ref, sb_ref, y_ref, lse_ref, *rest):
        qkv_d, (qr_s, kr_s, oc_s, lc_s, o1_s, l1_s, o2_s, l2_s, o3_s, l3_s, tmp_s) = rest[:9], rest[9:]
        c, s_a, s_b = c_ref[...], sa_ref[...], sb_ref[...]
        qr_s[...] = _rope_fwd(q_ref[...], c, s_a, s_b, A_ROT // 2) * (A_HEAD_DIM ** -0.5)
        kr_s[...] = _rope_fwd(k_ref[...], c, s_a, s_b, A_ROT // 2)
        head0 = lax.broadcasted_iota(jnp.int32, (Q_BLK, LANES), 1) < A_HEAD_DIM
        nat = ((o1_s, l1_s), (o2_s, l2_s), (o3_s, l3_s))

        for g, d in enumerate(DILATIONS):
            qd_s, kd_s, vd_s = qkv_d[3 * g:3 * g + 3]
            _deinterleave(qr_s, qd_s, d, BF16, tmp_s)
            _deinterleave(kr_s, kd_s, d, BF16, tmp_s)
            _deinterleave(v_ref, vd_s, d, BF16, tmp_s)
            o_dst, l_dst = (nat[g] if d == 1 else (oc_s, lc_s))

            def block(idx, carry, d=d, o_dst=o_dst, l_dst=l_dst, qd_s=qd_s, kd_s=kd_s, vd_s=vd_s):
                q0, k0, kwin, valid = _block_coords(idx, d)
                qb = qd_s[pl.ds(q0, Q_BLK), :]
                kb = kd_s[pl.ds(k0, kwin), :]
                vb = vd_s[pl.ds(k0, kwin), :]
                zero = jnp.zeros_like(qb)
                q2 = jnp.concatenate([jnp.where(head0, qb, zero), jnp.where(head0, zero, qb)], 0)
                s = jnp.where(jnp.concatenate([valid, valid], 0), _dot_nt(q2, kb), NEG_INF)
                m = jnp.max(s, axis=-1, keepdims=True)
                p = jnp.exp(s - m)
                l = jnp.sum(p, axis=-1, keepdims=True)
                o2 = _dot(p.astype(BF16), vb) / l
                l2 = m + jnp.log(l)
                o_dst[pl.ds(q0, Q_BLK), :] = jnp.where(head0, o2[:Q_BLK], o2[Q_BLK:])
                l_dst[pl.ds(q0, Q_BLK), :] = jnp.where(head0, l2[:Q_BLK], l2[Q_BLK:])
                return carry

            lax.fori_loop(0, SEQ // Q_BLK, block, 0, unroll=UNROLL_FWD)
            if d > 1:
                _interleave(oc_s, nat[g][0], d, tmp_s, False)
                _interleave(lc_s, nat[g][1], d, tmp_s, False)

        def merge(ci, carry):
            rows = pl.ds(pl.multiple_of(ci * 256, 256), 256)
            l1, l2, l3 = l1_s[rows, :], l2_s[rows, :], l3_s[rows, :]
            m = jnp.maximum(jnp.maximum(l1, l2), l3)
            w1, w2, w3 = jnp.exp(l1 - m), jnp.exp(l2 - m), jnp.exp(l3 - m)
            w = w1 + w2 + w3
            y_ref[rows, :] = (w1 * o1_s[rows, :] + w2 * o2_s[rows, :] + w3 * o3_s[rows, :]) / w
            lse_ref[rows, :] = m + jnp.log(w)
            return carry

        lax.fori_loop(0, SEQ // 256, merge, 0)

    def col(off):
        return pl.BlockSpec((SEQ, LANES), lambda b, hp: (b, off + hp))

    tab = pl.BlockSpec((SEQ, LANES), lambda b, hp: (b, 0))
    out = pl.BlockSpec((SEQ, LANES), lambda b, hp: (b, hp))
    f32s = pltpu.VMEM((SEQ, LANES), F32)
    res, landed = call_hosting_exchange(
        body, xch, grid=(nb, n_pairs),
        in_specs=[col(0), col(n_pairs), col(2 * n_pairs), tab, tab, tab],
        out_specs=[out] * 11,
        out_shape=[SDS((t, A_WIDTH), F32)] * 2 + [SDS((t, A_WIDTH), BF16)] * 9,
        scratch_shapes=[f32s] * 11,
        name="a_attn_fwd", operands=(proj, proj, proj, ca, sa, sb))
    return res[:2], res[2:], landed


def a_attn_bwd(qkv_d, ca, sa, sb, dy, y, lse, nb, xch):
    t = dy.shape[0]
    n_pairs = A_WIDTH // LANES

    def body(*refs):
        qkv_refs = refs[:9]
        (c_ref, sa_ref, sb_ref, do_ref, y_ref, lse_ref, dq_ref, dk_ref, dv_ref,
         l0n_s, l1n_s, d0n_s, d1n_s, dod_s, l0d_s, l1d_s, d0d_s, d1d_s,
         dqc_s, dkc_s, dvc_s, dq4_s, dk4_s, dv4_s, dqn_s, dkn_s, dvn_s, tmp_s) = refs[9:]
        c, s_a, s_b = c_ref[...], sa_ref[...], sb_ref[...]
        head0 = lax.broadcasted_iota(jnp.int32, (Q_BLK, LANES), 1) < A_HEAD_DIM

        def per_head_rows(ci, carry):
            rows = pl.ds(pl.multiple_of(ci * 256, 256), 256)
            h0 = lax.broadcasted_iota(jnp.int32, (256, LANES), 1) < A_HEAD_DIM
            tt = do_ref[rows, :] * y_ref[rows, :]
            d0n_s[rows, :] = jnp.broadcast_to(jnp.sum(jnp.where(h0, tt, 0.0), axis=-1, keepdims=True), (256, LANES))
            d1n_s[rows, :] = jnp.broadcast_to(jnp.sum(jnp.where(h0, 0.0, tt), axis=-1, keepdims=True), (256, LANES))
            l = lse_ref[rows, :]
            lr = pltpu.roll(l, A_HEAD_DIM, 1)
            l0n_s[rows, :] = jnp.where(h0, l, lr)
            l1n_s[rows, :] = jnp.where(h0, lr, l)
            return carry

        lax.fori_loop(0, SEQ // 256, per_head_rows, 0)
        assert DILATIONS == (1, 4, 16)

        for g, d in enumerate(DILATIONS):
            qd_s, kd_s, vd_s = qkv_refs[3 * g:3 * g + 3]
            _deinterleave(do_ref, dod_s, d, BF16, tmp_s)
            if d > 1:
                for src, dst in ((l0n_s, l0d_s), (l1n_s, l1d_s), (d0n_s, d0d_s), (d1n_s, d1d_s)):
                    _deinterleave(src, dst, d, F32, tmp_s)
            l0, l1, d0, d1 = (l0n_s, l1n_s, d0n_s, d1n_s) if d == 1 else (l0d_s, l1d_s, d0d_s, d1d_s)
            dq_dst, dk_dst, dv_dst = {1: (dqn_s, dkn_s, dvn_s), 4: (dq4_s, dk4_s, dv4_s), 16: (dqc_s, dkc_s, dvc_s)}[d]
            dk_dst[...] = jnp.zeros_like(dk_dst)
            dv_dst[...] = jnp.zeros_like(dv_dst)

            def block(idx, carry, d=d, l0=l0, l1=l1, d0=d0, d1=d1, dq_dst=dq_dst, dk_dst=dk_dst, dv_dst=dv_dst,
                      qd_s=qd_s, kd_s=kd_s, vd_s=vd_s):
                q0, k0, kwin, valid = _block_coords(idx, d)
                qrows = pl.ds(q0, Q_BLK)
                krows = pl.ds(k0, kwin)
                qb, dob = qd_s[qrows, :], dod_s[qrows, :]
                kb, vb = kd_s[krows, :], vd_s[krows, :]
                zero = jnp.zeros_like(qb)
                q2 = jnp.concatenate([jnp.where(head0, qb, zero), jnp.where(head0, zero, qb)], 0)
                do2 = jnp.concatenate([jnp.where(head0, dob, zero), jnp.where(head0, zero, dob)], 0)
                wide = lambda x: jnp.concatenate([x] * (kwin // LANES), 1)
                lse2 = wide(jnp.concatenate([l0[qrows, :], l1[qrows, :]], 0))
                dd2 = wide(jnp.concatenate([d0[qrows, :], d1[qrows, :]], 0))
                s = jnp.where(jnp.concatenate([valid, valid], 0), _dot_nt(q2, kb), NEG_INF)
                p = jnp.exp(s - lse2)
                ds = (p * (_dot_nt(do2, vb) - dd2)).astype(BF16)
                dq2 = _dot(ds, kb)
                dq_dst[qrows, :] = jnp.where(head0, dq2[:Q_BLK], dq2[Q_BLK:])
                dk_dst[krows, :] += _dot_tn(ds, q2)
                dv_dst[krows, :] += _dot_tn(p.astype(BF16), do2)
                return carry

            lax.fori_loop(0, SEQ // Q_BLK, block, 0, unroll=UNROLL_BWD)

        for c16, c4, nat in ((dqc_s, dq4_s, dqn_s), (dkc_s, dk4_s, dkn_s), (dvc_s, dv4_s, dvn_s)):
            _class16_to_class4(c16, tmp_s)
            c4[...] = c4[...] + tmp_s[...]
            _interleave(c4, nat, 4, tmp_s, True)

        dq_ref[...] = _rope_bwd(dqn_s[...] * (A_HEAD_DIM ** -0.5), c, s_a, s_b, A_ROT // 2).astype(BF16)
        dk_ref[...] = _rope_bwd(dkn_s[...], c, s_a, s_b, A_ROT // 2).astype(BF16)
        dv_ref[...] = dvn_s[...].astype(BF16)

    tab = pl.BlockSpec((SEQ, LANES), lambda b, hp: (b, 0))
    blk = pl.BlockSpec((SEQ, LANES), lambda b, hp: (b, hp))
    f32s = pltpu.VMEM((SEQ, LANES), F32)
    b16s = pltpu.VMEM((SEQ, LANES), BF16)
    return call_hosting_exchange(
        body, xch, grid=(nb, n_pairs),
        in_specs=[blk] * 9 + [tab, tab, tab, blk, blk, blk],
        out_specs=[blk, blk, blk],
        out_shape=[SDS((t, A_WIDTH), BF16)] * 3,
        scratch_shapes=[f32s] * 4 + [b16s] + [f32s] * 14,
        name="a_attn_bwd", operands=(*qkv_d, ca, sa, sb, dy, y, lse))


MLA_SCALE = (MLA_NOPE + MLA_ROPE) ** -0.5
LOG2E = 1.4426950408889634
MLA_QW = MLA_HEADS * LANES
MLA_KVW = MLA_QW + MLA_WIDTH


def _rms(x, g):
    r = lax.rsqrt(jnp.mean(x * x, axis=-1, keepdims=True) + NORM_EPS)
    return x * r * g, r


def _rms_bwd(dn, x, r, g):
    tg = dn * g
    dx = r * tg - x * (r * r * r) * jnp.mean(tg * x, axis=-1, keepdims=True)
    return dx, jnp.sum(dn * x * r, axis=0, keepdims=True)


def mla_prep_fwd(proj, cm, sma, smb, g_cq, g_ckv, wuq, wkv):
    t = proj.shape[0]
    tm = 1024

    def body(cq_ref, ckv_ref, kr_ref, c_ref, sa_ref, sb_ref, gq_ref, gkv_ref, wuq_ref, wkv_ref, q_ref, k_ref, v_ref):
        c, s_a, s_b = c_ref[...], sa_ref[...], sb_ref[...]
        cqn, _ = _rms(cq_ref[...], gq_ref[...])
        qf = _dot(cqn.astype(BF16), wuq_ref[...])
        ckvn, _ = _rms(ckv_ref[...], gkv_ref[...])
        kvf = _dot(ckvn.astype(BF16), wkv_ref[...])
        krope = _rope_fwd(kr_ref[...], c, s_a, s_b, MLA_ROPE // 2)
        for h in range(MLA_HEADS):
            cols = slice(h * LANES, (h + 1) * LANES)
            q_ref[:, cols] = (_rope_fwd(qf[:, cols], c, s_a, s_b, MLA_ROPE // 2) * (MLA_SCALE * LOG2E)).astype(BF16)
            k_ref[:, cols] = (kvf[:, cols] + krope).astype(BF16)
        v_ref[...] = kvf[:, MLA_QW:].astype(BF16)

    def row(w, j):
        return pl.BlockSpec((tm, w), lambda i: (i, j))

    def full(a):
        return pl.BlockSpec(a.shape, lambda i: (0, 0))

    return pl.pallas_call(
        body, grid=(t // tm,),
        in_specs=[row(256, 4096 // 256), row(128, 4352 // 128), row(128, 4480 // 128), row(128, 0), row(128, 0), row(128, 0),
                  full(g_cq), full(g_ckv), full(wuq), full(wkv)],
        out_specs=[row(MLA_QW, 0), row(MLA_QW, 0), row(MLA_WIDTH, 0)],
        out_shape=[SDS((t, MLA_QW), BF16), SDS((t, MLA_QW), BF16), SDS((t, MLA_WIDTH), BF16)],
        name="mla_prep_fwd", compiler_params=_params("parallel"))(proj, proj, proj, cm, sma, smb, g_cq, g_ckv, wuq, wkv)


def mla_prep_bwd(proj, cm, sma, smb, g_cq, g_ckv, wuq, wkv, dq, dk, dv):
    t = proj.shape[0]
    tm = 1024

    def body(cq_ref, ckv_ref, c_ref, sa_ref, sb_ref, gq_ref, gkv_ref, wuq_ref, wkv_ref, dq_ref, dk_ref, dv_ref,
             dcc_ref, dqf_ref, cqn_ref, dkvf_ref, ckvn_ref, dgq_ref, dgkv_ref):
        @pl.when(pl.program_id(0) == 0)
        def _():
            dgq_ref[...] = jnp.zeros_like(dgq_ref)
            dgkv_ref[...] = jnp.zeros_like(dgkv_ref)

        c, s_a, s_b = c_ref[...], sa_ref[...], sb_ref[...]
        cq, ckv = cq_ref[...], ckv_ref[...]
        cqn, rq = _rms(cq, gq_ref[...])
        ckvn, rkv = _rms(ckv, gkv_ref[...])
        cqn_ref[...] = cqn.astype(BF16)
        ckvn_ref[...] = ckvn.astype(BF16)
        lane = lax.broadcasted_iota(jnp.int32, (tm, LANES), 1)
        rope_lanes = (lane >= MLA_NOPE) & (lane < MLA_NOPE + MLA_ROPE)
        dkrope = jnp.zeros((tm, LANES), F32)
        for h in range(MLA_HEADS):
            cols = slice(h * LANES, (h + 1) * LANES)
            dqf_ref[:, cols] = _rope_bwd(dq_ref[:, cols].astype(F32) * MLA_SCALE, c, s_a, s_b, MLA_ROPE // 2).astype(BF16)
            dkh = dk_ref[:, cols].astype(F32) * (1.0 / LOG2E)
            dkvf_ref[:, cols] = dkh.astype(BF16)
            dkrope = dkrope + dkh
        dkvf_ref[:, MLA_QW:] = dv_ref[...].astype(BF16)
        dkr = _rope_bwd(jnp.where(rope_lanes, dkrope, 0.0), c, s_a, s_b, MLA_ROPE // 2)
        dcqn = _dot_nt(dqf_ref[...], wuq_ref[...])
        dckvn = _dot_nt(dkvf_ref[...], wkv_ref[...])
        dcq, dgq = _rms_bwd(dcqn, cq, rq, gq_ref[...])
        dckv, dgkv = _rms_bwd(dckvn, ckv, rkv, gkv_ref[...])
        dgq_ref[...] += dgq
        dgkv_ref[...] += dgkv
        dcc_ref[:, 0:256] = dcq.astype(BF16)
        dcc_ref[:, 256:384] = dckv.astype(BF16)
        dcc_ref[:, 384:512] = dkr.astype(BF16)

    def row(w, j):
        return pl.BlockSpec((tm, w), lambda i: (i, j))

    def full(a):
        return pl.BlockSpec(a.shape, lambda i: (0, 0))

    return pl.pallas_call(
        body, grid=(t // tm,),
        in_specs=[row(256, 4096 // 256), row(128, 4352 // 128), row(128, 0), row(128, 0), row(128, 0),
                  full(g_cq), full(g_ckv), full(wuq), full(wkv), row(MLA_QW, 0), row(MLA_QW, 0), row(MLA_WIDTH, 0)],
        out_specs=[row(512, 0), row(MLA_QW, 0), row(256, 0), row(MLA_KVW, 0), row(128, 0), full(g_cq), full(g_ckv)],
        out_shape=[SDS((t, 512), BF16), SDS((t, MLA_QW), BF16), SDS((t, 256), BF16), SDS((t, MLA_KVW), BF16),
                   SDS((t, 128), BF16), SDS(g_cq.shape, F32), SDS(g_ckv.shape, F32)],
        name="mla_prep_bwd", compiler_params=_params("arbitrary"))(proj, proj, cm, sma, smb, g_cq, g_ckv, wuq, wkv, dq, dk, dv)


MLA_TQ = SEQ
MLA_SUB_FWD = 512
MLA_SUB_BWD = 256


def mla_attn_fwd(qb, kb, vb, nb):
    t = qb.shape[0]
    nq = SEQ // MLA_TQ
    n_pairs = MLA_HEADS // 2

    def body(q_ref, k_ref, v_ref, y_ref, lse_ref):
        head0 = lax.broadcasted_iota(jnp.int32, (MLA_SUB_FWD, LANES), 1) < MLA_V
        v = v_ref[...]
        vhead0 = lax.broadcasted_iota(jnp.int32, v.shape, 1) < MLA_V
        one = jnp.ones_like(v)
        vh = [jnp.where(vhead0 == (h == 0), v, one) for h in range(2)]
        for sub in range(MLA_TQ // MLA_SUB_FWD):
            rows = slice(sub * MLA_SUB_FWD, (sub + 1) * MLA_SUB_FWD)
            outs, lses = [], []
            for h in range(2):
                cols = slice(h * LANES, (h + 1) * LANES)
                s = _dot_nt(q_ref[rows, cols], k_ref[:, cols])
                m = jnp.max(s, axis=-1, keepdims=True)
                p = jnp.exp2(s - m).astype(BF16)
                ol = _dot(p, vh[h])
                l = pltpu.roll(ol, MLA_V, 1)
                outs.append(ol / l)
                lses.append(m + jnp.log2(l))
            y_ref[rows, :] = jnp.where(head0, outs[0], outs[1])
            lse_ref[rows, :] = jnp.where(head0, lses[0], lses[1])

    return pl.pallas_call(
        body, grid=(nb, n_pairs, nq),
        in_specs=[pl.BlockSpec((MLA_TQ, 2 * LANES), lambda b, hp, i: (b * nq + i, hp)),
                  pl.BlockSpec((SEQ, 2 * LANES), lambda b, hp, i: (b, hp)),
                  pl.BlockSpec((SEQ, LANES), lambda b, hp, i: (b, hp))],
        out_specs=[pl.BlockSpec((MLA_TQ, LANES), lambda b, hp, i: (b * nq + i, hp))] * 2,
        out_shape=[SDS((t, MLA_WIDTH), F32)] * 2,
        name="mla_attn_fwd", compiler_params=_params("parallel", "parallel", "parallel"))(qb, kb, vb)


def mla_attn_bwd(qb, kb, vb, dy, y, lse, nb, xch):
    t = qb.shape[0]
    nq = SEQ // MLA_TQ
    n_pairs = MLA_HEADS // 2

    assert nq == 1

    def body(q_ref, k_ref, v_ref, do_ref, y_ref, lse_ref, dq_ref, dk_ref, dv_ref, dk_s, dv_s):
        dk_s[...] = jnp.zeros_like(dk_s)
        dv_s[...] = jnp.zeros_like(dv_s)
        head0 = lax.broadcasted_iota(jnp.int32, (MLA_SUB_BWD, LANES), 1) < MLA_V
        v = v_ref[...]
        for sub in range(MLA_TQ // MLA_SUB_BWD):
            rows = slice(sub * MLA_SUB_BWD, (sub + 1) * MLA_SUB_BWD)
            do = do_ref[rows, :]
            lse = lse_ref[rows, :]
            tt = do * y_ref[rows, :]
            dv = jnp.zeros((SEQ, LANES), F32)
            for h in range(2):
                sel = head0 if h == 0 else ~head0
                lo = h * MLA_V
                cols = slice(h * LANES, (h + 1) * LANES)
                q = q_ref[rows, cols]
                k = k_ref[:, cols]
                dd = jnp.sum(jnp.where(sel, tt, 0.0), axis=-1, keepdims=True)
                doh = jnp.where(sel, do, 0.0).astype(BF16)
                p = jnp.exp2(_dot_nt(q, k) - lse[:, lo:lo + 1])
                dp = _dot_nt(doh, v)
                ds = (p * (dp - dd)).astype(BF16)
                dq_ref[rows, cols] = _dot(ds, k).astype(dq_ref.dtype)
                dk_s[:, cols] += _dot_tn(ds, q)
                dv = dv + _dot_tn(p.astype(BF16), doh)
            dv_s[...] += dv
        dk_ref[...] = dk_s[...].astype(dk_ref.dtype)
        dv_ref[...] = dv_s[...].astype(dv_ref.dtype)

    qspec = pl.BlockSpec((MLA_TQ, 2 * LANES), lambda b, hp, i: (b * nq + i, hp))
    kspec = pl.BlockSpec((SEQ, 2 * LANES), lambda b, hp, i: (b, hp))
    vspec = pl.BlockSpec((SEQ, LANES), lambda b, hp, i: (b, hp))
    ospec = pl.BlockSpec((MLA_TQ, LANES), lambda b, hp, i: (b * nq + i, hp))
    return call_hosting_exchange(
        body, xch, grid=(nb, n_pairs, nq),
        in_specs=[qspec, kspec, vspec, ospec, ospec, ospec],
        out_specs=[qspec, kspec, vspec],
        out_shape=[SDS((t, MLA_QW), BF16), SDS((t, MLA_QW), BF16), SDS((t, MLA_WIDTH), BF16)],
        scratch_shapes=[pltpu.VMEM((SEQ, 2 * LANES), F32), pltpu.VMEM((SEQ, LANES), F32)],
        name="mla_attn_bwd", operands=(qb, kb, vb, dy, y, lse))


MEM_TQ = SEQ
MEM_SUB = SEQ
MEM_SCALE = MEM_HEAD_DIM ** -0.5
MQ_BLK4 = 5120 // MEM_WIDTH


def mem_attn_fwd(proj, mkv, nb):
    t = proj.shape[0]
    nq = SEQ // MEM_TQ

    def body(q_ref, mk_ref, mv_ref, y_ref):
        for sub in range(MEM_TQ // MEM_SUB):
            rows = slice(sub * MEM_SUB, (sub + 1) * MEM_SUB)
            for h in range(MEM_HEADS):
                cols = slice(h * LANES, (h + 1) * LANES)
                s = _dot_nt(q_ref[rows, cols].astype(BF16), mk_ref[:, cols]) * MEM_SCALE
                m = jnp.max(s, axis=-1, keepdims=True)
                p = jnp.exp(s - m)
                l = jnp.sum(p, axis=-1, keepdims=True)
                y_ref[rows, cols] = _dot(p.astype(BF16), mv_ref[:, cols]) / l

    return pl.pallas_call(
        body, grid=(nb, nq),
        in_specs=[pl.BlockSpec((MEM_TQ, MEM_WIDTH), lambda b, i: (b * nq + i, MQ_BLK4)),
                  pl.BlockSpec((N_MEM, MEM_WIDTH), lambda b, i: (b, 0)),
                  pl.BlockSpec((N_MEM, MEM_WIDTH), lambda b, i: (b, 1))],
        out_specs=pl.BlockSpec((MEM_TQ, MEM_WIDTH), lambda b, i: (b * nq + i, 0)),
        out_shape=SDS((t, MEM_WIDTH), F32),
        name="mem_attn_fwd", compiler_params=_params("parallel", "parallel"))(proj, mkv, mkv)


def mem_attn_bwd(proj, mkv, dy, nb):
    t = proj.shape[0]
    nq = SEQ // MEM_TQ

    def body(q_ref, mk_ref, mv_ref, do_ref, dq_ref, dmk_ref, dmv_ref):
        @pl.when(pl.program_id(1) == 0)
        def _():
            dmk_ref[...] = jnp.zeros_like(dmk_ref)
            dmv_ref[...] = jnp.zeros_like(dmv_ref)

        for sub in range(MEM_TQ // MEM_SUB):
            rows = slice(sub * MEM_SUB, (sub + 1) * MEM_SUB)
            for h in range(MEM_HEADS):
                cols = slice(h * LANES, (h + 1) * LANES)
                q = q_ref[rows, cols].astype(BF16)
                mk, mv = mk_ref[:, cols], mv_ref[:, cols]
                do = do_ref[rows, cols].astype(BF16)
                s = _dot_nt(q, mk) * MEM_SCALE
                e = jnp.exp(s - jnp.max(s, axis=-1, keepdims=True))
                p = e / jnp.sum(e, axis=-1, keepdims=True)
                dp = _dot_nt(do, mv)
                ds = (p * (dp - jnp.sum(p * dp, axis=-1, keepdims=True)) * MEM_SCALE).astype(BF16)
                dq_ref[rows, cols] = _dot(ds, mk).astype(BF16)
                dmk_ref[:, cols] += _dot_tn(ds, q)
                dmv_ref[:, cols] += _dot_tn(p.astype(BF16), do)

    ospec = pl.BlockSpec((MEM_TQ, MEM_WIDTH), lambda b, i: (b * nq + i, 0))
    kspec = pl.BlockSpec((N_MEM, MEM_WIDTH), lambda b, i: (b, 0))
    return pl.pallas_call(
        body, grid=(nb, nq),
        in_specs=[pl.BlockSpec((MEM_TQ, MEM_WIDTH), lambda b, i: (b * nq + i, MQ_BLK4)),
                  kspec, pl.BlockSpec((N_MEM, MEM_WIDTH), lambda b, i: (b, 1)), ospec],
        out_specs=[ospec, kspec, kspec],
        out_shape=[SDS((t, MEM_WIDTH), BF16), SDS((nb * N_MEM, MEM_WIDTH), F32), SDS((nb * N_MEM, MEM_WIDTH), F32)],
        name="mem_attn_bwd", compiler_params=_params("parallel", "arbitrary"))(proj, mkv, mkv, dy)


ROW_TM = 512
AG_BLK = 3072 // 1024
BG_BLK = 4608 // 512
MG_BLK = 5632 // 512
GROUPS = ((0, A_WIDTH), (A_WIDTH, MLA_WIDTH), (A_WIDTH + MLA_WIDTH, MEM_WIDTH))
D_MIX = 2048


def _gate_specs():
    def row(w, j):
        return pl.BlockSpec((ROW_TM, w), lambda i: (i, j))

    def vec(w):
        return pl.BlockSpec((1, w), lambda i: (0, 0))

    ys = [row(A_WIDTH, 0), row(MLA_WIDTH, 0), row(MEM_WIDTH, 0)]
    gates = [row(A_WIDTH, AG_BLK), row(MLA_WIDTH, BG_BLK), row(MEM_WIDTH, MG_BLK)]
    gains = [vec(A_WIDTH), vec(MLA_WIDTH), vec(MEM_WIDTH)]
    return row, vec, ys, gates, gains


def gate_out_ln_loss(ya, yb, ym, proj, goa, gob, gom, wout, h32, target, gp, bp):
    t, d = h32.shape
    _, _, ys, gates, gains = _gate_specs()

    def body(ya_ref, yb_ref, ym_ref, ga_ref, gb_ref, gm_ref, goa_ref, gob_ref, gom_ref, w_ref, h_ref, t_ref, gp_ref, bp_ref,
             z_ref, du32_ref, du16_ref, loss_ref, dgp_ref, dbp_ref):
        @pl.when(pl.program_id(0) == 0)
        def _():
            loss_ref[...] = jnp.zeros_like(loss_ref)
            dgp_ref[...] = jnp.zeros_like(dgp_ref)
            dbp_ref[...] = jnp.zeros_like(dbp_ref)

        for (off, w), y_ref, g_ref, go_ref in zip(GROUPS, (ya_ref, yb_ref, ym_ref), (ga_ref, gb_ref, gm_ref),
                                                  (goa_ref, gob_ref, gom_ref)):
            n, _ = _rms(y_ref[...], go_ref[...])
            gt = g_ref[...]
            z_ref[:, off:off + w] = (n * (gt * _sigmoid(gt))).astype(BF16)
        g = gp_ref[...]
        u = ALPHA * h_ref[...] + _dot(z_ref[...], w_ref[...])
        mu = jnp.mean(u, axis=-1, keepdims=True)
        uc = u - mu
        rstd = lax.rsqrt(jnp.mean(uc * uc, axis=-1, keepdims=True) + NORM_EPS)
        xhat = uc * rstd
        err = xhat * g + bp_ref[...] - t_ref[...]
        tok = jnp.sum(err * err, axis=-1, keepdims=True) * (1.0 / d)
        loss_ref[...] += 0.5 * jnp.sum(tok, axis=0, keepdims=True)
        dout = err * (1.0 / d)
        dxhat = dout * g
        du = rstd * (dxhat - jnp.mean(dxhat, axis=-1, keepdims=True)
                     - xhat * jnp.mean(dxhat * xhat, axis=-1, keepdims=True))
        du32_ref[...] = du
        du16_ref[...] = du.astype(BF16)
        dgp_ref[...] += jnp.sum(dout * xhat, axis=0, keepdims=True)
        dbp_ref[...] += jnp.sum(dout, axis=0, keepdims=True)

    row = pl.BlockSpec((ROW_TM, d), lambda i: (i, 0))
    vec = pl.BlockSpec((1, d), lambda i: (0, 0))
    zrow = pl.BlockSpec((ROW_TM, D_MIX), lambda i: (i, 0))
    return pl.pallas_call(
        body, grid=(t // ROW_TM,),
        in_specs=ys + gates + gains + [pl.BlockSpec((D_MIX, d), lambda i: (0, 0)), row, row, vec, vec],
        out_specs=[zrow, row, row, pl.BlockSpec((1, LANES), lambda i: (0, 0)), vec, vec],
        out_shape=[SDS((t, D_MIX), BF16), SDS((t, d), F32), SDS((t, d), BF16), SDS((1, LANES), F32), SDS((1, d), F32),
                   SDS((1, d), F32)],
        name="gate_out_ln_loss", compiler_params=_params("arbitrary"))(
            ya, yb, ym, proj, proj, proj, goa, gob, gom, wout, h32, target, gp, bp)


def gate_bwd(du16, wout, ya, yb, ym, proj, goa, gob, gom):
    t = ya.shape[0]
    row, vec, ys, gates, gains = _gate_specs()

    def body(du_ref, w_ref, ya_ref, yb_ref, ym_ref, ga_ref, gb_ref, gm_ref, goa_ref, gob_ref, gom_ref,
             dya_ref, dyb_ref, dym_ref, dga_ref, dgb_ref, dgm_ref, dgoa_ref, dgob_ref, dgom_ref):
        @pl.when(pl.program_id(0) == 0)
        def _():
            dgoa_ref[...] = jnp.zeros_like(dgoa_ref)
            dgob_ref[...] = jnp.zeros_like(dgob_ref)
            dgom_ref[...] = jnp.zeros_like(dgom_ref)

        dz = _dot_nt(du_ref[...], w_ref[...])
        for (off, w), y_ref, g_ref, go_ref, dy_ref, dg_ref, dgo_ref in zip(
                GROUPS, (ya_ref, yb_ref, ym_ref), (ga_ref, gb_ref, gm_ref), (goa_ref, gob_ref, gom_ref),
                (dya_ref, dyb_ref, dym_ref), (dga_ref, dgb_ref, dgm_ref), (dgoa_ref, dgob_ref, dgom_ref)):
            dzg = dz[:, off:off + w]
            y, gt, go = y_ref[...], g_ref[...], go_ref[...]
            n, r = _rms(y, go)
            sg = _sigmoid(gt)
            dg_ref[...] = (dzg * n * (sg * (1.0 + gt * (1.0 - sg)))).astype(BF16)
            dy, dgo = _rms_bwd(dzg * (gt * sg), y, r, go)
            dy_ref[...] = dy
            dgo_ref[...] += dgo

    widths = (A_WIDTH, MLA_WIDTH, MEM_WIDTH)
    return pl.pallas_call(
        body, grid=(t // ROW_TM,),
        in_specs=[row(D_MODEL, 0), pl.BlockSpec((D_MIX, D_MODEL), lambda i: (0, 0))] + ys + gates + gains,
        out_specs=[row(w, 0) for w in widths] * 2 + [vec(w) for w in widths],
        out_shape=[SDS((t, w), F32) for w in widths] + [SDS((t, w), BF16) for w in widths] + [SDS((1, w), F32) for w in widths],
        name="gate_bwd", compiler_params=_params("arbitrary"))(du16, wout, ya, yb, ym, proj, proj, proj, goa, gob, gom)


def dh_ln_bwd(pieces, win_t, du32, x2, g_emb, xch):
    t, d = x2.shape

    def body(*refs):
        p_refs = refs[:len(pieces)]
        w_ref, du_ref, x_ref, g_ref, dx_ref, dg_ref, db_ref = refs[len(pieces):]

        @pl.when(pl.program_id(0) == 0)
        def _():
            dg_ref[...] = jnp.zeros_like(dg_ref)
            db_ref[...] = jnp.zeros_like(db_ref)

        dh = ALPHA * du_ref[...]
        for p_ref, off, w in zip(p_refs, PIECE_OFFS, PIECE_WIDTHS):
            dh = dh + _dot(p_ref[...], w_ref[off:off + w, :])
        x = x_ref[...]
        xc = x - jnp.mean(x, axis=-1, keepdims=True)
        rstd = lax.rsqrt(jnp.mean(xc * xc, axis=-1, keepdims=True) + NORM_EPS)
        xhat = xc * rstd
        dg_ref[...] += jnp.sum(dh * xhat, axis=0, keepdims=True)
        db_ref[...] += jnp.sum(dh, axis=0, keepdims=True)
        tg = dh * g_ref[...]
        dx_ref[...] = rstd * (tg - jnp.mean(tg, axis=-1, keepdims=True)
                              - xhat * jnp.mean(tg * xhat, axis=-1, keepdims=True))

    row = pl.BlockSpec((ROW_TM, d), lambda i: (i, 0))
    vec = pl.BlockSpec((1, d), lambda i: (0, 0))
    return call_hosting_exchange(
        body, xch, grid=(t // ROW_TM,),
        in_specs=[pl.BlockSpec((ROW_TM, w), lambda i: (i, 0)) for w in PIECE_WIDTHS]
        + [pl.BlockSpec(win_t.shape, lambda i: (0, 0)), row, row, vec],
        out_specs=[row, vec, vec],
        out_shape=[SDS((t, d), F32), SDS((1, d), F32), SDS((1, d), F32)],
        scratch_shapes=[], name="dh_ln_bwd", operands=(*pieces, win_t, du32, x2, g_emb))


def _adamw(w, g, m, v):
    m2 = ADAM_B1 * m + (1.0 - ADAM_B1) * g
    v2 = ADAM_B2 * v + (1.0 - ADAM_B2) * (g * g)
    m_hat = m2 / (1.0 - ADAM_B1 ** ADAM_STEP)
    v_hat = v2 / (1.0 - ADAM_B2 ** ADAM_STEP)
    return -ADAM_LR * (m_hat / (jnp.sqrt(v_hat) + ADAM_EPS) + ADAM_WD * w), m2, v2


def adamw_shard(w, parts, m, v, name):
    r, c = w.shape
    if r % 256 == 0 or r * c <= 256 * 1024:
        tr, tc = min(r, 256), c
    else:
        tr, tc = r, 256

    def body(w_ref, p_ref, m_ref, v_ref, g_ref, d_ref, nm_ref, nv_ref):
        g = p_ref[0].astype(F32)
        for k in range(1, N_DEV):
            g = g + p_ref[k].astype(F32)
        g_ref[...] = g
        d_ref[...], nm_ref[...], nv_ref[...] = _adamw(w_ref[...], g, m_ref[...], v_ref[...])

    blk = pl.BlockSpec((tr, tc), lambda i, j: (i, j))
    return pl.pallas_call(
        body, grid=(r // tr, c // tc),
        in_specs=[blk, pl.BlockSpec((N_DEV, tr, tc), lambda i, j: (0, i, j)), blk, blk],
        out_specs=[blk] * 4, out_shape=[SDS((r, c), F32)] * 4, name=name,
        compiler_params=_params("parallel", "parallel"))(w, parts, m, v)


def adamw_shards_whole(ws, parts, ms, vs, name):
    n = len(ws)

    def body(*refs):
        w_refs, p_refs, m_refs, v_refs = refs[:n], refs[n:2 * n], refs[2 * n:3 * n], refs[3 * n:4 * n]
        outs = refs[4 * n:]
        for i in range(n):
            g = p_refs[i][0].astype(F32)
            for k in range(1, N_DEV):
                g = g + p_refs[i][k].astype(F32)
            outs[4 * i][...] = g
            outs[4 * i + 1][...], outs[4 * i + 2][...], outs[4 * i + 3][...] = _adamw(
                w_refs[i][...], g, m_refs[i][...], v_refs[i][...])

    res = pl.pallas_call(
        body, out_shape=[SDS(w.shape, F32) for w in ws for _ in range(4)], name=name,
        compiler_params=_params())(*ws, *parts, *ms, *vs)
    return [res[4 * i:4 * i + 4] for i in range(n)]


def _place():
    return lax.axis_index("x"), lax.axis_index("y"), lax.axis_index("c")


def _flat(px, py, pc):
    return 4 * px + 2 * py + pc


def _peer(x, y, c, k):
    return (1 - x if k & 4 else x, 1 - y if k & 2 else y, 1 - c if k & 1 else c)


def cast_shards(shards):
    def body(*refs):
        n = len(refs) // 2
        for i_ref, o_ref in zip(refs[:n], refs[n:]):
            o_ref[...] = i_ref[...].astype(BF16)

    return pl.pallas_call(body, out_shape=[SDS(s.shape, BF16) for s in shards], name="cast_shards",
                          compiler_params=_params())(*shards)


def _two_level_gather_plan(src_refs, land_refs, send_sems, recv_sems, local_sems):
    n = len(src_refs)
    x, y, c = _place()
    me, sib = (x, y, c), (x, y, 1 - c)
    chips = [(1 - x, y), (x, 1 - y), (1 - x, 1 - y)]

    def copy(a, k, block, to, src=None):
        dst = land_refs[a].at[_flat(*block)]
        return pltpu.make_async_remote_copy(
            src_ref=dst if src is None else src, dst_ref=dst,
            send_sem=send_sems.at[a * N_DEV + k], recv_sem=recv_sems.at[a * N_DEV + k],
            device_id=to, device_id_type=MESH)

    mine = [pltpu.make_async_copy(src_refs[a], land_refs[a].at[_flat(*me)], local_sems.at[a]) for a in range(n)]
    first = []
    for a in range(n):
        first.append(copy(a, 0, me, sib, src=src_refs[a]))
        first += [copy(a, 1 + j, me, (*chip, c), src=src_refs[a]) for j, chip in enumerate(chips)]

    def start():
        for cp in mine + first:
            cp.start()

    def finish():
        passed = []
        for j, chip in enumerate(chips):
            for a in range(n):
                copy(a, 1 + j, (*chip, c), me).wait_recv()
                fwd = copy(a, 4 + j, (*chip, c), sib)
                fwd.start()
                passed.append(fwd)
        for a in range(n):
            copy(a, 0, sib, me).wait_recv()
            for j, chip in enumerate(chips):
                copy(a, 4 + j, (*chip, 1 - c), me).wait_recv()
        for cp in first + passed:
            cp.wait_send()
        for cp in mine:
            cp.wait()

    return start, finish


ALL_DEVICES = tuple(range(N_DEV))


def _exchange_plan(src_refs, land_refs, dests, send_sems, recv_sems, local_sems):
    x, y, c = _place()
    me = _flat(x, y, c)
    plan = []
    for a, (src, land, dl) in enumerate(zip(src_refs, land_refs, dests)):
        for li, j in enumerate(dl):
            to = ((j >> 2) & 1, (j >> 1) & 1, j & 1)
            block = src.at[li] if len(src.shape) == len(land.shape) else src

            def push(slot, a=a, block=block, land=land, j=j, to=to):
                return pltpu.make_async_remote_copy(
                    src_ref=block, dst_ref=land.at[slot], send_sem=send_sems.at[a * N_DEV + j],
                    recv_sem=recv_sems.at[a * N_DEV + slot], device_id=to, device_id_type=MESH)

            own = pltpu.make_async_copy(block, land.at[j], local_sems.at[a])
            plan.append((j, push(me), own, [push(s) for s in range(N_DEV) if s != j]))
    return me, plan


def _exchange_start(me, plan):
    for j, send, own, _ in plan:
        @pl.when(me != j)
        def _(send=send):
            send.start()

        @pl.when(me == j)
        def _(own=own):
            own.start()


def _exchange_wait(me, plan):
    for j, send, own, arrivals in plan:
        @pl.when(me != j)
        def _(send=send):
            send.wait_send()

        @pl.when(me == j)
        def _(own=own, arrivals=arrivals):
            own.wait()
            for arrival in arrivals:
                arrival.wait_recv()


def call_hosting_exchange(core, xch, *, grid, in_specs, out_specs, out_shape, scratch_shapes, name, operands):
    srcs, dests, landing = xch
    n, n_in, n_out, n_scr = len(srcs), len(in_specs), len(out_specs), len(scratch_shapes)

    def body(*refs):
        ins, src_refs = refs[:n_in], refs[n_in:n_in + n]
        outs = refs[n_in + 2 * n:n_in + 2 * n + n_out]
        land_refs = refs[n_in + 2 * n + n_out:n_in + 3 * n + n_out]
        scratch = refs[n_in + 3 * n + n_out:n_in + 3 * n + n_out + n_scr]
        sems = refs[n_in + 3 * n + n_out + n_scr:]
        first = functools.reduce(jnp.logical_and, [pl.program_id(i) == 0 for i in range(len(grid))])
        last = functools.reduce(jnp.logical_and, [pl.program_id(i) == grid[i] - 1 for i in range(len(grid))])
        if dests is None:
            start, finish = _two_level_gather_plan(src_refs, land_refs, *sems)
        else:
            me, plan = _exchange_plan(src_refs, land_refs, dests, *sems)
            start, finish = functools.partial(_exchange_start, me, plan), functools.partial(_exchange_wait, me, plan)
        pl.when(first)(start)
        core(*ins, *outs, *scratch)
        pl.when(last)(finish)

    hbm = pl.BlockSpec(memory_space=pl.ANY)
    res = pl.pallas_call(
        body, grid=grid,
        in_specs=list(in_specs) + [hbm] * (2 * n), out_specs=list(out_specs) + [hbm] * n,
        out_shape=list(out_shape) + [SDS(l.shape, l.dtype) for l in landing],
        scratch_shapes=list(scratch_shapes) + [pltpu.SemaphoreType.DMA((N_DEV * n,)), pltpu.SemaphoreType.DMA((N_DEV * n,)),
                                               pltpu.SemaphoreType.DMA((n,))],
        input_output_aliases={n_in + n + k: n_out + k for k in range(n)},
        name=name, compiler_params=_params(*(("arbitrary",) * len(grid))))(*operands, *srcs, *landing)
    return res[:n_out], res[n_out:]


SLOT_ROWS = 8


def small_allreduce_adamw(loss_sum, grads, ws, ms, vs):
    n = len(grads)
    rows = [g.shape[0] for g in grads]
    total = SLOT_ROWS * (n + 1)

    def body(*refs):
        loss_ref, g_refs, w_refs = refs[0], refs[1:1 + n], refs[1 + n:1 + 2 * n]
        m_refs, v_refs = refs[1 + 2 * n:1 + 3 * n], refs[1 + 3 * n:1 + 4 * n]
        outs = refs[1 + 4 * n:2 + 8 * n]
        vec, gath, tot, send_sems, recv_sems = refs[2 + 8 * n:]
        x, y, c = _place()
        me = _flat(x, y, c)
        vec[...] = jnp.zeros_like(vec)
        vec[0:1, :] = loss_ref[...]
        for i in range(n):
            vec[SLOT_ROWS * (i + 1):SLOT_ROWS * (i + 1) + rows[i], :] = g_refs[i][...]
        gath[me] = vec[...]
        copies = []
        for k in range(1, N_DEV):
            peer = _peer(x, y, c, k)
            copies.append(pltpu.make_async_remote_copy(
                src_ref=vec, dst_ref=gath.at[me], send_sem=send_sems.at[k - 1], recv_sem=recv_sems.at[k - 1],
                device_id=peer, device_id_type=MESH))
        for cp in copies:
            cp.start()
        for cp in copies:
            cp.wait_recv()
        for cp in copies:
            cp.wait_send()
        g = gath[0]
        for j in range(1, N_DEV):
            g = g + gath[j]
        tot[...] = g
        outs[0][...] = tot[0:1, :]
        for i in range(n):
            gi = tot[SLOT_ROWS * (i + 1):SLOT_ROWS * (i + 1) + rows[i], :]
            outs[1 + i][...] = gi
            outs[1 + n + i][...], outs[1 + 2 * n + i][...], outs[1 + 3 * n + i][...] = _adamw(
                w_refs[i][...], gi, m_refs[i][...], v_refs[i][...])

    shapes = [SDS(g.shape, F32) for g in grads]
    return pl.pallas_call(
        body, out_shape=[SDS((1, LANES), F32)] + shapes * 4,
        scratch_shapes=[pltpu.VMEM((total, LANES), F32), pltpu.VMEM((N_DEV, total, LANES), F32), pltpu.VMEM((total, LANES), F32),
                        pltpu.SemaphoreType.DMA((7,)), pltpu.SemaphoreType.DMA((7,))],
        name="small_allreduce_adamw", compiler_params=_params())(loss_sum, *grads, *ws, *ms, *vs)


def _rope_lane_patterns():
    inv = lambda r: ROPE_THETA ** (-(jnp.arange(0, r, 2, dtype=F32) / r))
    z = lambda n: jnp.zeros((n,), F32)
    o = lambda n: jnp.ones((n,), F32)
    half, rest = A_ROT // 2, A_HEAD_DIM - A_ROT
    ia, im = inv(A_ROT), inv(MLA_ROPE)
    mh, tail = MLA_ROPE // 2, LANES - MLA_NOPE - MLA_ROPE
    rows = [jnp.tile(jnp.concatenate([ia, ia, z(rest)]), 2),
            jnp.tile(jnp.concatenate([o(half), z(half + rest)]), 2),
            jnp.tile(jnp.concatenate([z(half), o(half), z(rest)]), 2),
            jnp.concatenate([z(MLA_NOPE), im, im, z(tail)]),
            jnp.concatenate([z(MLA_NOPE), o(mh), z(mh + tail)]),
            jnp.concatenate([z(MLA_NOPE + mh), o(mh), z(tail)]),
            z(LANES), z(LANES)]
    return jnp.stack(rows)


KR_LO, KR_HI = 4480, 4512
W_IN_SHARD = D_IN // N_DEV
BG_SPLIT = 6 * W_IN_SHARD - KR_HI


def w_in_working_t(g):
    pad_lo, pad_hi = MLA_NOPE, LANES - MLA_NOPE - MLA_ROPE
    spans = []
    for lo, hi, shift in ((0, KR_LO, 0), (KR_LO, KR_HI, pad_lo), (KR_HI, D_IN, pad_lo + pad_hi)):
        r = lo
        while r < hi:
            j = r // W_IN_SHARD
            n = min(hi, (j + 1) * W_IN_SHARD) - r
            spans.append((j, r - j * W_IN_SHARD, n, r + shift))
            r += n

    def body(g_ref, o_ref):
        o_ref[KR_LO:KR_LO + pad_lo, :] = jnp.zeros((pad_lo, D_MODEL), o_ref.dtype)
        o_ref[KR_HI + pad_lo:KR_HI + pad_lo + pad_hi, :] = jnp.zeros((pad_hi, D_MODEL), o_ref.dtype)
        for j, src, n, dst in spans:
            o_ref[dst:dst + n, :] = g_ref[j, src:src + n, :]

    return pl.pallas_call(body, out_shape=SDS((D_INW, D_MODEL), g.dtype), name="w_in_working_t", compiler_params=_params())(g)


def _w_in_shard_5(d_ag_tail, d_cc, d_bg_head):
    kr = MLA_Q_RANK + MLA_KV_RANK + MLA_NOPE
    rows = jnp.concatenate([d_ag_tail, d_cc[:MLA_Q_RANK + MLA_KV_RANK], d_cc[kr:kr + MLA_ROPE], d_bg_head], 0)
    return rows.reshape(1, W_IN_SHARD, D_MODEL).astype(BF16)


def _w_uq_working(g):
    w = jnp.pad(g.transpose(1, 0, 2), ((0, 0), (0, 0), (0, LANES - MLA_NOPE - MLA_ROPE)))
    return w.reshape(MLA_Q_RANK, MLA_QW)


def _w_uq_parts(dw):
    return dw.reshape(MLA_Q_RANK, MLA_HEADS, LANES)[:, :, :MLA_NOPE + MLA_ROPE].transpose(1, 0, 2)


def _w_ukv_working(g):
    wk = jnp.pad(g[:, :, :MLA_NOPE].transpose(1, 0, 2), ((0, 0), (0, 0), (0, LANES - MLA_NOPE)))
    wv = g[:, :, MLA_NOPE:].transpose(1, 0, 2)
    return jnp.concatenate([wk.reshape(MLA_KV_RANK, MLA_QW), wv.reshape(MLA_KV_RANK, MLA_WIDTH)], 1)


def _w_ukv_parts(dw):
    dk = dw[:, :MLA_QW].reshape(MLA_KV_RANK, MLA_HEADS, LANES)[:, :, :MLA_NOPE]
    dv = dw[:, MLA_QW:].reshape(MLA_KV_RANK, MLA_HEADS, MLA_V)
    return jnp.concatenate([dk, dv], -1).transpose(1, 0, 2)


SMALL_NAMES = ("g_emb", "b_emb", "g_cq", "g_ckv", "g_out_a", "g_out_b", "g_out_m", "g_post", "b_post")


def kernel(x, mem, positions, g_emb, b_emb, w_in, g_cq, g_ckv, w_uq, w_ukv, w_mem_kv, g_out_a, g_out_b, g_out_m, w_out, g_post, b_post, loss_target, m_g_emb, m_b_emb, m_w_in, m_g_cq, m_g_ckv, m_w_uq, m_w_ukv, m_w_mem_kv, m_g_out_a, m_g_out_b, m_g_out_m, m_w_out, m_g_post, m_b_post, v_g_emb, v_b_emb, v_w_in, v_g_cq, v_g_ckv, v_w_uq, v_w_ukv, v_w_mem_kv, v_g_out_a, v_g_out_b, v_g_out_m, v_w_out, v_g_post, v_b_post):
    nb = x.shape[0]
    t = nb * SEQ
    x2 = x.reshape(t, D_MODEL)
    tgt2 = loss_target.reshape(t, D_MODEL)
    mem2 = mem.reshape(nb * N_MEM, D_MODEL)
    g_emb2, b_emb2 = g_emb.reshape(1, -1), b_emb.reshape(1, -1)

    w_in_t, m_w_in_t, v_w_in_t = w_in[0].T, m_w_in[0].T, v_w_in[0].T
    s_in, s_uq, s_ukv, s_mem, s_out = cast_shards((w_in_t, w_uq[0], w_ukv[0], w_mem_kv[0], w_out[0]))
    (h32, h16, (a_c, a_sa, a_sb), (m_c, m_sa, m_sb)), (g_in,) = embed_fwd(
        x2, g_emb2, b_emb2, positions, ((s_in,), None, (lax.empty((N_DEV,) + s_in.shape, BF16),)))
    win_t = w_in_working_t(g_in)

    proj = mm_nn(h16, win_t, F32, 2048, 1536, "proj", rhs_transposed=True)
    later = (s_uq, s_ukv, s_mem, s_out)
    (ya, lse_a), qkv_d, (g_uq, g_ukv, g_mem, g_out) = a_attn_fwd(
        proj, a_c, a_sa, a_sb, nb,
        (later, (ALL_DEVICES,) * len(later), tuple(lax.empty((N_DEV,) + w.shape, BF16) for w in later)))
    wuq_w = _w_uq_working(g_uq)
    wkv_w = _w_ukv_working(g_ukv)
    wmem = g_mem.reshape(D_MODEL, 2 * MEM_WIDTH)
    wout = g_out.reshape(D_MIX, D_MODEL)
    qb, kb, vb = mla_prep_fwd(proj, m_c, m_sa, m_sb, g_cq, g_ckv, wuq_w, wkv_w)
    yb, lse_b = mla_attn_fwd(qb, kb, vb, nb)
    mkv = mm_nn(mem2, wmem, BF16, nb * N_MEM, 512, "mem_kv")
    ym = mem_attn_fwd(proj, mkv, nb)
    z, du32, du16, loss_sum, dg_post, db_post = gate_out_ln_loss(
        ya, yb, ym, proj, g_out_a, g_out_b, g_out_m, wout, h32, tgt2, g_post, b_post)

    dya, dyb, dym, dag, dbg, dmg, dg_out_a, dg_out_b, dg_out_m = gate_bwd(
        du16, wout, ya, yb, ym, proj, g_out_a, g_out_b, g_out_m)
    dw_out = mm_tn(z, du16, 1024, "dw_out")
    dmq, dmk, dmv = mem_attn_bwd(proj, mkv, dym, nb)
    dw_mem = mm_tn(mem2, jnp.concatenate([dmk, dmv], 1), nb * N_MEM, "dw_mem")
    d_gates, shards_6_7 = mm_tn_group((dbg, dmq, dmg), h16, 2048, "dw_in_bg_mq_mg", W_IN_SHARD, BG_SPLIT, 2)
    landing = lambda w, dtype=F32: lax.empty((N_DEV,) + w.shape, dtype)
    big_w = (w_in_t, w_uq[0], w_ukv[0], w_mem_kv[0], w_out[0])
    (daq, dak, dav), (p_out, p_mem, p_in) = a_attn_bwd(
        qkv_d, a_c, a_sa, a_sb, dya, ya, lse_a, nb,
        ((dw_out.reshape(N_DEV, D_MIX // N_DEV, D_MODEL), dw_mem.reshape(N_DEV, D_MODEL // N_DEV, 2 * MEM_WIDTH),
          shards_6_7),
         (ALL_DEVICES, ALL_DEVICES, (6, 7)),
         (landing(w_out[0]), landing(w_mem_kv[0]), landing(w_in_t, BF16))))
    d_a, shards_0_4 = mm_tn_group((daq, dak, dav, dag), h16, 1024, "dw_in_aq_ak_av_ag", W_IN_SHARD, 0, 5)
    (dqb, dkb, dvb), (p_in,) = mla_attn_bwd(
        qb, kb, vb, dyb, yb, lse_b, nb, ((shards_0_4,), ((0, 1, 2, 3, 4),), (p_in,)))
    dcc, dqf, cqn, dkvf, ckvn, dg_cq, dg_ckv = mla_prep_bwd(proj, m_c, m_sa, m_sb, g_cq, g_ckv, wuq_w, wkv_w, dqb, dkb, dvb)
    dw_uq = mm_tn(cqn, dqf, 2048, "dw_uq")
    dw_ukv = mm_tn(ckvn, dkvf, 2048, "dw_ukv")
    d_cc = mm_tn(dcc, h16, 2048, "dw_in_cc")
    pieces = (daq, dak, dav, dag, dcc, dbg, dmq, dmg)
    (grad_x, dg_emb, db_emb), (p_in, p_uq, p_ukv) = dh_ln_bwd(
        pieces, win_t, du32, x2, g_emb2,
        ((_w_in_shard_5(d_a[5 * W_IN_SHARD:], d_cc, d_gates[:BG_SPLIT]), _w_uq_parts(dw_uq), _w_ukv_parts(dw_ukv)),
         ((5,), ALL_DEVICES, ALL_DEVICES),
         (p_in, landing(w_uq[0]), landing(w_ukv[0]))))

    parts = (p_in, p_uq, p_ukv, p_mem, p_out)
    big_m = (m_w_in_t, m_w_uq[0], m_w_ukv[0], m_w_mem_kv[0], m_w_out[0])
    big_v = (v_w_in_t, v_w_uq[0], v_w_ukv[0], v_w_mem_kv[0], v_w_out[0])
    big = {"w_in": [o.T[None] for o in adamw_shard(big_w[0], parts[0], big_m[0], big_v[0], "adamw_w_in")]}
    rest = adamw_shards_whole(big_w[1:], parts[1:], big_m[1:], big_v[1:], "adamw_rest")
    for name, res in zip(("w_uq", "w_ukv", "w_mem_kv", "w_out"), rest):
        big[name] = [o[None] for o in res]

    small_w = (g_emb, b_emb, g_cq, g_ckv, g_out_a, g_out_b, g_out_m, g_post, b_post)
    small_m = (m_g_emb, m_b_emb, m_g_cq, m_g_ckv, m_g_out_a, m_g_out_b, m_g_out_m, m_g_post, m_b_post)
    small_v = (v_g_emb, v_b_emb, v_g_cq, v_g_ckv, v_g_out_a, v_g_out_b, v_g_out_m, v_g_post, v_b_post)
    small_g = (dg_emb, db_emb, dg_cq, dg_ckv, dg_out_a, dg_out_b, dg_out_m, dg_post, db_post)
    rows128 = lambda vals: [v.reshape(-1, LANES) for v in vals]
    res = small_allreduce_adamw(loss_sum, rows128(small_g), rows128(small_w), rows128(small_m), rows128(small_v))
    loss = res[0][0, 0]
    n_small = len(small_w)
    sg, sd, sm, sv = [[r.reshape(w.shape) for r, w in zip(res[1 + k * n_small:1 + (k + 1) * n_small], small_w)]
                      for k in range(4)]

    order = ("g_emb", "b_emb", "w_in", "g_cq", "g_ckv", "w_uq", "w_ukv", "w_mem_kv", "g_out_a", "g_out_b", "g_out_m",
             "w_out", "g_post", "b_post")
    small_idx = {n: i for i, n in enumerate(SMALL_NAMES)}
    outs = [loss, grad_x.reshape(x.shape)]
    for kind in range(4):
        for name in order:
            outs.append(big[name][kind] if name in big else (sg, sd, sm, sv)[kind][small_idx[name]])
    return tuple(outs)
```

```python
import functools

import jax
import jax.numpy as jnp
from jax import lax
from jax.experimental import pallas as pl
from jax.experimental.pallas import tpu as pltpu

F32 = jnp.float32
BF16 = jnp.bfloat16
SDS = jax.ShapeDtypeStruct
MESH = pl.DeviceIdType.MESH

D_MODEL = 1024
SEQ = 2048
A_HEADS, A_HEAD_DIM, A_ROT = 16, 64, 16
A_WIDTH = 1024
DILATIONS = (1, 4, 16)
N_SIDE = 64
MLA_HEADS, MLA_Q_RANK, MLA_KV_RANK = 8, 256, 128
MLA_NOPE, MLA_ROPE, MLA_V = 64, 32, 64
MLA_WIDTH = 512
N_MEM, MEM_HEADS, MEM_HEAD_DIM, MEM_WIDTH = 256, 4, 128, 512
ROPE_THETA = 500000.0
NORM_EPS = 1e-5
NEG_INF = -1e30
ALPHA = 2.0 ** 0.25
D_IN = 6048
N_DEV = 8

ADAM_LR, ADAM_B1, ADAM_B2, ADAM_EPS, ADAM_WD, ADAM_STEP = 0.001, 0.9, 0.999, 1e-08, 0.01, 10

D_INW = 6144
PIECE_WIDTHS = (1024, 1024, 1024, 1024, 512, 512, 512, 512)
PIECE_OFFS = (0, 1024, 2048, 3072, 4096, 4608, 5120, 5632)
LANES = 128
VMEM_LIMIT = 56 * 1024 * 1024


def _params(*sem):
    kw = dict(vmem_limit_bytes=VMEM_LIMIT)
    if sem:
        kw["dimension_semantics"] = sem
    return pltpu.CompilerParams(**kw)


def _dot(a, b):
    return jnp.dot(a, b, preferred_element_type=F32)


def _dot_nt(a, b):
    return lax.dot_general(a, b, (((1,), (1,)), ((), ())), preferred_element_type=F32)


def _dot_tn(a, b):
    return lax.dot_general(a, b, (((0,), (0,)), ((), ())), preferred_element_type=F32)


def _sigmoid(x):
    return 1.0 / (1.0 + jnp.exp(-x))


def _rope_fwd(x, c, sa, sb, half):
    n = x.shape[-1]
    return x * c + pltpu.roll(x, n - half, 1) * sa + pltpu.roll(x, half, 1) * sb


def _rope_bwd(dy, c, sa, sb, half):
    n = dy.shape[-1]
    return dy * c + pltpu.roll(dy * sa, half, 1) + pltpu.roll(dy * sb, n - half, 1)


def mm_nn(a, b, out_dtype, tm, tn, name, rhs_transposed=False):
    m, k = a.shape
    n = b.shape[0] if rhs_transposed else b.shape[1]
    dot = _dot_nt if rhs_transposed else _dot

    def body(a_ref, b_ref, o_ref):
        o_ref[...] = dot(a_ref[...].astype(BF16), b_ref[...].astype(BF16)).astype(o_ref.dtype)

    b_spec = pl.BlockSpec((tn, k), lambda j, i: (j, 0)) if rhs_transposed else pl.BlockSpec((k, tn), lambda j, i: (0, j))
    return pl.pallas_call(
        body, grid=(n // tn, m // tm),
        in_specs=[pl.BlockSpec((tm, k), lambda j, i: (i, 0)), b_spec],
        out_specs=pl.BlockSpec((tm, tn), lambda j, i: (i, j)),
        out_shape=SDS((m, n), out_dtype), name=name,
        compiler_params=_params("parallel", "parallel"))(a, b)


def mm_tn(a, b, tt, name):
    t, m = a.shape
    n = b.shape[1]

    def body(a_ref, b_ref, o_ref):
        @pl.when(pl.program_id(0) == 0)
        def _():
            o_ref[...] = jnp.zeros_like(o_ref)

        o_ref[...] += _dot_tn(a_ref[...].astype(BF16), b_ref[...].astype(BF16))

    return pl.pallas_call(
        body, grid=(t // tt,),
        in_specs=[pl.BlockSpec((tt, m), lambda i: (i, 0)), pl.BlockSpec((tt, n), lambda i: (i, 0))],
        out_specs=pl.BlockSpec((m, n), lambda i: (0, 0)),
        out_shape=SDS((m, n), F32), name=name,
        compiler_params=_params("arbitrary"))(a, b)


def mm_tn_pairs(pairs, tt, name):
    n = len(pairs)
    t = pairs[0][0].shape[0]

    def body(*refs):
        @pl.when(pl.program_id(0) == 0)
        def _():
            for o_ref in refs[2 * n:]:
                o_ref[...] = jnp.zeros_like(o_ref)

        for i in range(n):
            refs[2 * n + i][...] += _dot_tn(refs[2 * i][...].astype(BF16), refs[2 * i + 1][...].astype(BF16))

    rows = lambda x: pl.BlockSpec((tt, x.shape[1]), lambda i: (i, 0))
    return pl.pallas_call(
        body, grid=(t // tt,),
        in_specs=[rows(x) for pair in pairs for x in pair],
        out_specs=[pl.BlockSpec((a.shape[1], b.shape[1]), lambda i: (0, 0)) for a, b in pairs],
        out_shape=[SDS((a.shape[1], b.shape[1]), F32) for a, b in pairs], name=name,
        compiler_params=_params("arbitrary"))(*[x for pair in pairs for x in pair])


def mm_tn_group(pieces, b, tt, name, slab_rows, first_slab_row, n_slabs):
    n, (t, w), cols = len(pieces), pieces[0].shape, b.shape[1]
    nt = t // tt

    def body(*refs):
        p_refs, b_ref, o_ref, slab_ref = refs[:n], refs[n], refs[n + 1], refs[n + 2]

        @pl.when(pl.program_id(1) == 0)
        def _():
            o_ref[...] = jnp.zeros_like(o_ref)

        for k in range(n):
            @pl.when(pl.program_id(0) == k)
            def _(k=k):
                o_ref[...] += _dot_tn(p_refs[k][...], b_ref[...])

            @pl.when((pl.program_id(0) == k) & (pl.program_id(1) == nt - 1))
            def _(k=k):
                for j in range(n_slabs):
                    lo = max(k * w, first_slab_row + j * slab_rows)
                    hi = min((k + 1) * w, first_slab_row + (j + 1) * slab_rows)
                    if lo < hi:
                        dst = lo - first_slab_row - j * slab_rows
                        slab_ref[j, dst:dst + hi - lo, :] = o_ref[lo - k * w:hi - k * w, :].astype(slab_ref.dtype)

    def piece_spec(k):
        return pl.BlockSpec((tt, w), lambda p, i: (jnp.where(p < k, 0, jnp.where(p > k, nt - 1, i)), 0))

    return pl.pallas_call(
        body, grid=(n, nt),
        in_specs=[piece_spec(k) for k in range(n)] + [pl.BlockSpec((tt, cols), lambda p, i: (i, 0))],
        out_specs=[pl.BlockSpec((w, cols), lambda p, i: (p, 0)),
                   pl.BlockSpec((n_slabs, slab_rows, cols), lambda p, i: (0, 0, 0))],
        out_shape=[SDS((n * w, cols), F32), SDS((n_slabs, slab_rows, cols), BF16)], name=name,
        compiler_params=_params("arbitrary", "arbitrary"))(*pieces, b)


def embed_fwd(x2, g, b, positions, xch):
    t, d = x2.shape
    tm = 512
    pos = positions.astype(F32).reshape(-1, 1)

    def body(x_ref, g_ref, b_ref, pos_ref, pat_ref, h32_ref, h16_ref, *tabs):
        x = x_ref[...]
        mu = jnp.mean(x, axis=-1, keepdims=True)
        xc = x - mu
        var = jnp.mean(xc * xc, axis=-1, keepdims=True)
        h = xc * lax.rsqrt(var + NORM_EPS) * g_ref[...] + b_ref[...]
        h32_ref[...] = h
        h16_ref[...] = h.astype(BF16)
        p = pos_ref[...]
        for k in range(2):
            inv, first, second = pat_ref[3 * k:3 * k + 1, :], pat_ref[3 * k + 1:3 * k + 2, :], pat_ref[3 * k + 2:3 * k + 3, :]
            ang = p * inv
            sn = jnp.sin(ang)
            tabs[3 * k][...] = jnp.where(first + second > 0.0, jnp.cos(ang), 1.0)
            tabs[3 * k + 1][...] = -first * sn
            tabs[3 * k + 2][...] = second * sn

    row = pl.BlockSpec((tm, d), lambda i: (i, 0))
    vec = pl.BlockSpec((1, d), lambda i: (0, 0))
    tab = pl.BlockSpec((tm, LANES), lambda i: (i, 0))
    res, landed = call_hosting_exchange(
        body, xch, grid=(t // tm,),
        in_specs=[row, vec, vec, pl.BlockSpec((tm, 1), lambda i: (i, 0)), pl.BlockSpec((8, LANES), lambda i: (0, 0))],
        out_specs=[row, row] + [tab] * 6,
        out_shape=[SDS((t, d), F32), SDS((t, d), BF16)] + [SDS((t, LANES), F32)] * 6,
        scratch_shapes=[], name="embed_fwd", operands=(x2, g, b, pos, _rope_lane_patterns()))
    return (res[0], res[1], tuple(res[2:5]), tuple(res[5:8])), landed


Q_BLK = 128
UNROLL_FWD = 16
UNROLL_BWD = 16


def _pattern_geometry(d):
    length = SEQ // d
    nblk = length // Q_BLK
    kwin = min(2 * Q_BLK, length)
    return length, nblk, kwin


def _block_coords(idx, d):
    length, nblk, kwin = _pattern_geometry(d)
    r = lax.shift_right_logical(idx, nblk.bit_length() - 1)
    i = idx & (nblk - 1)
    q0 = pl.multiple_of(r * length + i * Q_BLK, Q_BLK)
    ks = jnp.clip(i * Q_BLK - N_SIDE, 0, length - kwin)
    k0 = pl.multiple_of(r * length + ks, N_SIDE)
    qpos = i * Q_BLK + lax.broadcasted_iota(jnp.int32, (Q_BLK, kwin), 0)
    kpos = ks + lax.broadcasted_iota(jnp.int32, (Q_BLK, kwin), 1)
    valid = jnp.abs(kpos - qpos) <= N_SIDE
    return q0, k0, kwin, valid


def _deinterleave(src_ref, dst_ref, d, dtype, tmp_ref):
    if d == 1:
        dst_ref[...] = src_ref[...].astype(dtype)
        return
    q = SEQ // 4
    if d == 4:
        for r in range(4):
            dst_ref[r * q:(r + 1) * q, :] = src_ref[pl.ds(r, q, stride=4), :].astype(dtype)
        return
    assert d == 16
    n = SEQ // 16
    for r in range(4):
        tmp_ref[r * q:(r + 1) * q, :] = src_ref[pl.ds(r, q, stride=4), :]
    for r in range(4):
        for j in range(4):
            dst_ref[(r + 4 * j) * n:(r + 4 * j + 1) * n, :] = tmp_ref[pl.ds(r * q + j, n, stride=4), :].astype(dtype)


def _class16_to_class4(src_ref, dst_ref):
    q, n = SEQ // 4, SEQ // 16
    for r in range(4):
        for j in range(4):
            dst_ref[pl.ds(r * q + j, n, stride=4), :] = src_ref[(r + 4 * j) * n:(r + 4 * j + 1) * n, :]


def _interleave(src_ref, dst_ref, d, tmp_ref, accumulate):
    q = SEQ // 4
    if d == 16:
        _class16_to_class4(src_ref, tmp_ref)
        src_ref = tmp_ref
    else:
        assert d == 4
    for r in range(4):
        rows = pl.ds(r, q, stride=4)
        val = src_ref[r * q:(r + 1) * q, :]
        dst_ref[rows, :] = dst_ref[rows, :] + val if accumulate else val


def a_attn_fwd(proj, ca, sa, sb, nb, xch):
    t = proj.shape[0]
    n_pairs = A_WIDTH // LANES

    def body(q_ref, k_ref, v_ref, c_ref, sa_ref, sb_ref, y_ref, lse_ref, *rest):
        qkv_d, (qr_s, kr_s, oc_s, lc_s, o1_s, l1_s, o2_s, l2_s, o3_s, l3_s, tmp_s) = rest[:9], rest[9:]
        c, s_a, s_b = c_ref[...], sa_ref[...], sb_ref[...]
        qr_s[...] = _rope_fwd(q_ref[...], c, s_a, s_b, A_ROT // 2) * (A_HEAD_DIM ** -0.5)
        kr_s[...] = _rope_fwd(k_ref[...], c, s_a, s_b, A_ROT // 2)
        head0 = lax.broadcasted_iota(jnp.int32, (Q_BLK, LANES), 1) < A_HEAD_DIM
        nat = ((o1_s, l1_s), (o2_s, l2_s), (o3_s, l3_s))

        for g, d in enumerate(DILATIONS):
            qd_s, kd_s, vd_s = qkv_d[3 * g:3 * g + 3]
            _deinterleave(qr_s, qd_s, d, BF16, tmp_s)
            _deinterleave(kr_s, kd_s, d, BF16, tmp_s)
            _deinterleave(v_ref, vd_s, d, BF16, tmp_s)
            o_dst, l_dst = (nat[g] if d == 1 else (oc_s, lc_s))

            def block(idx, carry, d=d, o_dst=o_dst, l_dst=l_dst, qd_s=qd_s, kd_s=kd_s, vd_s=vd_s):
                q0, k0, kwin, valid = _block_coords(idx, d)
                qb = qd_s[pl.ds(q0, Q_BLK), :]
                kb = kd_s[pl.ds(k0, kwin), :]
                vb = vd_s[pl.ds(k0, kwin), :]
                zero = jnp.zeros_like(qb)
                q2 = jnp.concatenate([jnp.where(head0, qb, zero), jnp.where(head0, zero, qb)], 0)
                s = jnp.where(jnp.concatenate([valid, valid], 0), _dot_nt(q2, kb), NEG_INF)
                m = jnp.max(s, axis=-1, keepdims=True)
                p = jnp.exp(s - m)
                l = jnp.sum(p, axis=-1, keepdims=True)
                o2 = _dot(p.astype(BF16), vb) / l
                l2 = m + jnp.log(l)
                o_dst[pl.ds(q0, Q_BLK), :] = jnp.where(head0, o2[:Q_BLK], o2[Q_BLK:])
                l_dst[pl.ds(q0, Q_BLK), :] = jnp.where(head0, l2[:Q_BLK], l2[Q_BLK:])
                return carry

            lax.fori_loop(0, SEQ // Q_BLK, block, 0, unroll=UNROLL_FWD)
            if d > 1:
                _interleave(oc_s, nat[g][0], d, tmp_s, False)
                _interleave(lc_s, nat[g][1], d, tmp_s, False)

        def merge(ci, carry):
            rows = pl.ds(pl.multiple_of(ci * 256, 256), 256)
            l1, l2, l3 = l1_s[rows, :], l2_s[rows, :], l3_s[rows, :]
            m = jnp.maximum(jnp.maximum(l1, l2), l3)
            w1, w2, w3 = jnp.exp(l1 - m), jnp.exp(l2 - m), jnp.exp(l3 - m)
            w = w1 + w2 + w3
            y_ref[rows, :] = (w1 * o1_s[rows, :] + w2 * o2_s[rows, :] + w3 * o3_s[rows, :]) / w
            lse_ref[rows, :] = m + jnp.log(w)
            return carry

        lax.fori_loop(0, SEQ // 256, merge, 0)

    def col(off):
        return pl.BlockSpec((SEQ, LANES), lambda b, hp: (b, off + hp))

    tab = pl.BlockSpec((SEQ, LANES), lambda b, hp: (b, 0))
    out = pl.BlockSpec((SEQ, LANES), lambda b, hp: (b, hp))
    f32s = pltpu.VMEM((SEQ, LANES), F32)
    res, landed = call_hosting_exchange(
        body, xch, grid=(nb, n_pairs),
        in_specs=[col(0), col(n_pairs), col(2 * n_pairs), tab, tab, tab],
        out_specs=[out] * 11,
        out_shape=[SDS((t, A_WIDTH), F32)] * 2 + [SDS((t, A_WIDTH), BF16)] * 9,
        scratch_shapes=[f32s] * 11,
        name="a_attn_fwd", operands=(proj, proj, proj, ca, sa, sb))
    return res[:2], res[2:], landed


def a_attn_bwd(qkv_d, ca, sa, sb, dy, y, lse, nb, xch):
    t = dy.shape[0]
    n_pairs = A_WIDTH // LANES

    def body(*refs):
        qkv_refs = refs[:9]
        (c_ref, sa_ref, sb_ref, do_ref, y_ref, lse_ref, dq_ref, dk_ref, dv_ref,
         l0n_s, l1n_s, d0n_s, d1n_s, dod_s, l0d_s, l1d_s, d0d_s, d1d_s,
         dqc_s, dkc_s, dvc_s, dq4_s, dk4_s, dv4_s, dqn_s, dkn_s, dvn_s, tmp_s) = refs[9:]
        c, s_a, s_b = c_ref[...], sa_ref[...], sb_ref[...]
        head0 = lax.broadcasted_iota(jnp.int32, (Q_BLK, LANES), 1) < A_HEAD_DIM

        def per_head_rows(ci, carry):
            rows = pl.ds(pl.multiple_of(ci * 256, 256), 256)
            h0 = lax.broadcasted_iota(jnp.int32, (256, LANES), 1) < A_HEAD_DIM
            tt = do_ref[rows, :] * y_ref[rows, :]
            d0n_s[rows, :] = jnp.broadcast_to(jnp.sum(jnp.where(h0, tt, 0.0), axis=-1, keepdims=True), (256, LANES))
            d1n_s[rows, :] = jnp.broadcast_to(jnp.sum(jnp.where(h0, 0.0, tt), axis=-1, keepdims=True), (256, LANES))
            l = lse_ref[rows, :]
            lr = pltpu.roll(l, A_HEAD_DIM, 1)
            l0n_s[rows, :] = jnp.where(h0, l, lr)
            l1n_s[rows, :] = jnp.where(h0, lr, l)
            return carry

        lax.fori_loop(0, SEQ // 256, per_head_rows, 0)
        assert DILATIONS == (1, 4, 16)

        for g, d in enumerate(DILATIONS):
            qd_s, kd_s, vd_s = qkv_refs[3 * g:3 * g + 3]
            _deinterleave(do_ref, dod_s, d, BF16, tmp_s)
            if d > 1:
                for src, dst in ((l0n_s, l0d_s), (l1n_s, l1d_s), (d0n_s, d0d_s), (d1n_s, d1d_s)):
                    _deinterleave(src, dst, d, F32, tmp_s)
            l0, l1, d0, d1 = (l0n_s, l1n_s, d0n_s, d1n_s) if d == 1 else (l0d_s, l1d_s, d0d_s, d1d_s)
            dq_dst, dk_dst, dv_dst = {1: (dqn_s, dkn_s, dvn_s), 4: (dq4_s, dk4_s, dv4_s), 16: (dqc_s, dkc_s, dvc_s)}[d]
            dk_dst[...] = jnp.zeros_like(dk_dst)
            dv_dst[...] = jnp.zeros_like(dv_dst)

            def block(idx, carry, d=d, l0=l0, l1=l1, d0=d0, d1=d1, dq_dst=dq_dst, dk_dst=dk_dst, dv_dst=dv_dst,
                      qd_s=qd_s, kd_s=kd_s, vd_s=vd_s):
                q0, k0, kwin, valid = _block_coords(idx, d)
                qrows = pl.ds(q0, Q_BLK)
                krows = pl.ds(k0, kwin)
                qb, dob = qd_s[qrows, :], dod_s[qrows, :]
                kb, vb = kd_s[krows, :], vd_s[krows, :]
                zero = jnp.zeros_like(qb)
                q2 = jnp.concatenate([jnp.where(head0, qb, zero), jnp.where(head0, zero, qb)], 0)
                do2 = jnp.concatenate([jnp.where(head0, dob, zero), jnp.where(head0, zero, dob)], 0)
                wide = lambda x: jnp.concatenate([x] * (kwin // LANES), 1)
                lse2 = wide(jnp.concatenate([l0[qrows, :], l1[qrows, :]], 0))
                dd2 = wide(jnp.concatenate([d0[qrows, :], d1[qrows, :]], 0))
                s = jnp.where(jnp.concatenate([valid, valid], 0), _dot_nt(q2, kb), NEG_INF)
                p = jnp.exp(s - lse2)
                ds = (p * (_dot_nt(do2, vb) - dd2)).astype(BF16)
                dq2 = _dot(ds, kb)
                dq_dst[qrows, :] = jnp.where(head0, dq2[:Q_BLK], dq2[Q_BLK:])
                dk_dst[krows, :] += _dot_tn(ds, q2)
                dv_dst[krows, :] += _dot_tn(p.astype(BF16), do2)
                return carry

            lax.fori_loop(0, SEQ // Q_BLK, block, 0, unroll=UNROLL_BWD)

        for c16, c4, nat in ((dqc_s, dq4_s, dqn_s), (dkc_s, dk4_s, dkn_s), (dvc_s, dv4_s, dvn_s)):
            _class16_to_class4(c16, tmp_s)
            c4[...] = c4[...] + tmp_s[...]
            _interleave(c4, nat, 4, tmp_s, True)

        dq_ref[...] = _rope_bwd(dqn_s[...] * (A_HEAD_DIM ** -0.5), c, s_a, s_b, A_ROT // 2).astype(BF16)
        dk_ref[...] = _rope_bwd(dkn_s[...], c, s_a, s_b, A_ROT // 2).astype(BF16)
        dv_ref[...] = dvn_s[...].astype(BF16)

    tab = pl.BlockSpec((SEQ, LANES), lambda b, hp: (b, 0))
    blk = pl.BlockSpec((SEQ, LANES), lambda b, hp: (b, hp))
    f32s = pltpu.VMEM((SEQ, LANES), F32)
    b16s = pltpu.VMEM((SEQ, LANES), BF16)
    return call_hosting_exchange(
        body, xch, grid=(nb, n_pairs),
        in_specs=[blk] * 9 + [tab, tab, tab, blk, blk, blk],
        out_specs=[blk, blk, blk],
        out_shape=[SDS((t, A_WIDTH), BF16)] * 3,
        scratch_shapes=[f32s] * 4 + [b16s] + [f32s] * 14,
        name="a_attn_bwd", operands=(*qkv_d, ca, sa, sb, dy, y, lse))


MLA_SCALE = (MLA_NOPE + MLA_ROPE) ** -0.5
LOG2E = 1.4426950408889634
MLA_QW = MLA_HEADS * LANES
MLA_KVW = MLA_QW + MLA_WIDTH


def _rms(x, g):
    r = lax.rsqrt(jnp.mean(x * x, axis=-1, keepdims=True) + NORM_EPS)
    return x * r * g, r


def _rms_bwd(dn, x, r, g):
    tg = dn * g
    dx = r * tg - x * (r * r * r) * jnp.mean(tg * x, axis=-1, keepdims=True)
    return dx, jnp.sum(dn * x * r, axis=0, keepdims=True)


def mla_prep_fwd(proj, cm, sma, smb, g_cq, g_ckv, wuq, wkv):
    t = proj.shape[0]
    tm = 1024

    def body(cq_ref, ckv_ref, kr_ref, c_ref, sa_ref, sb_ref, gq_ref, gkv_ref, wuq_ref, wkv_ref, q_ref, k_ref, v_ref):
        c, s_a, s_b = c_ref[...], sa_ref[...], sb_ref[...]
        cqn, _ = _rms(cq_ref[...], gq_ref[...])
        qf = _dot(cqn.astype(BF16), wuq_ref[...])
        ckvn, _ = _rms(ckv_ref[...], gkv_ref[...])
        kvf = _dot(ckvn.astype(BF16), wkv_ref[...])
        krope = _rope_fwd(kr_ref[...], c, s_a, s_b, MLA_ROPE // 2)
        for h in range(MLA_HEADS):
            cols = slice(h * LANES, (h + 1) * LANES)
            q_ref[:, cols] = (_rope_fwd(qf[:, cols], c, s_a, s_b, MLA_ROPE // 2) * (MLA_SCALE * LOG2E)).astype(BF16)
            k_ref[:, cols] = (kvf[:, cols] + krope).astype(BF16)
        v_ref[...] = kvf[:, MLA_QW:].astype(BF16)

    def row(w, j):
        return pl.BlockSpec((tm, w), lambda i: (i, j))

    def full(a):
        return pl.BlockSpec(a.shape, lambda i: (0, 0))

    return pl.pallas_call(
        body, grid=(t // tm,),
        in_specs=[row(256, 4096 // 256), row(128, 4352 // 128), row(128, 4480 // 128), row(128, 0), row(128, 0), row(128, 0),
                  full(g_cq), full(g_ckv), full(wuq), full(wkv)],
        out_specs=[row(MLA_QW, 0), row(MLA_QW, 0), row(MLA_WIDTH, 0)],
        out_shape=[SDS((t, MLA_QW), BF16), SDS((t, MLA_QW), BF16), SDS((t, MLA_WIDTH), BF16)],
        name="mla_prep_fwd", compiler_params=_params("parallel"))(proj, proj, proj, cm, sma, smb, g_cq, g_ckv, wuq, wkv)


def mla_prep_bwd(proj, cm, sma, smb, g_cq, g_ckv, wuq, wkv, dq, dk, dv):
    t = proj.shape[0]
    tm = 1024

    def body(cq_ref, ckv_ref, c_ref, sa_ref, sb_ref, gq_ref, gkv_ref, wuq_ref, wkv_ref, dq_ref, dk_ref, dv_ref,
             dcc_ref, dqf_ref, cqn_ref, dkvf_ref, ckvn_ref, dgq_ref, dgkv_ref):
        @pl.when(pl.program_id(0) == 0)
        def _():
            dgq_ref[...] = jnp.zeros_like(dgq_ref)
            dgkv_ref[...] = jnp.zeros_like(dgkv_ref)

        c, s_a, s_b = c_ref[...], sa_ref[...], sb_ref[...]
        cq, ckv = cq_ref[...], ckv_ref[...]
        cqn, rq = _rms(cq, gq_ref[...])
        ckvn, rkv = _rms(ckv, gkv_ref[...])
        cqn_ref[...] = cqn.astype(BF16)
        ckvn_ref[...] = ckvn.astype(BF16)
        lane = lax.broadcasted_iota(jnp.int32, (tm, LANES), 1)
        rope_lanes = (lane >= MLA_NOPE) & (lane < MLA_NOPE + MLA_ROPE)
        dkrope = jnp.zeros((tm, LANES), F32)
        for h in range(MLA_HEADS):
            cols = slice(h * LANES, (h + 1) * LANES)
            dqf_ref[:, cols] = _rope_bwd(dq_ref[:, cols].astype(F32) * MLA_SCALE, c, s_a, s_b, MLA_ROPE // 2).astype(BF16)
            dkh = dk_ref[:, cols].astype(F32) * (1.0 / LOG2E)
            dkvf_ref[:, cols] = dkh.astype(BF16)
            dkrope = dkrope + dkh
        dkvf_ref[:, MLA_QW:] = dv_ref[...].astype(BF16)
        dkr = _rope_bwd(jnp.where(rope_lanes, dkrope, 0.0), c, s_a, s_b, MLA_ROPE // 2)
        dcqn = _dot_nt(dqf_ref[...], wuq_ref[...])
        dckvn = _dot_nt(dkvf_ref[...], wkv_ref[...])
        dcq, dgq = _rms_bwd(dcqn, cq, rq, gq_ref[...])
        dckv, dgkv = _rms_bwd(dckvn, ckv, rkv, gkv_ref[...])
        dgq_ref[...] += dgq
        dgkv_ref[...] += dgkv
        dcc_ref[:, 0:256] = dcq.astype(BF16)
        dcc_ref[:, 256:384] = dckv.astype(BF16)
        dcc_ref[:, 384:512] = dkr.astype(BF16)

    def row(w, j):
        return pl.BlockSpec((tm, w), lambda i: (i, j))

    def full(a):
        return pl.BlockSpec(a.shape, lambda i: (0, 0))

    return pl.pallas_call(
        body, grid=(t // tm,),
        in_specs=[row(256, 4096 // 256), row(128, 4352 // 128), row(128, 0), row(128, 0), row(128, 0),
                  full(g_cq), full(g_ckv), full(wuq), full(wkv), row(MLA_QW, 0), row(MLA_QW, 0), row(MLA_WIDTH, 0)],
        out_specs=[row(512, 0), row(MLA_QW, 0), row(256, 0), row(MLA_KVW, 0), row(128, 0), full(g_cq), full(g_ckv)],
        out_shape=[SDS((t, 512), BF16), SDS((t, MLA_QW), BF16), SDS((t, 256), BF16), SDS((t, MLA_KVW), BF16),
                   SDS((t, 128), BF16), SDS(g_cq.shape, F32), SDS(g_ckv.shape, F32)],
        name="mla_prep_bwd", compiler_params=_params("arbitrary"))(proj, proj, cm, sma, smb, g_cq, g_ckv, wuq, wkv, dq, dk, dv)


MLA_TQ = SEQ
MLA_SUB_FWD = 512
MLA_SUB_BWD = 256


def mla_attn_fwd(qb, kb, vb, nb):
    t = qb.shape[0]
    nq = SEQ // MLA_TQ
    n_pairs = MLA_HEADS // 2

    def body(q_ref, k_ref, v_ref, y_ref, lse_ref):
        head0 = lax.broadcasted_iota(jnp.int32, (MLA_SUB_FWD, LANES), 1) < MLA_V
        v = v_ref[...]
        vhead0 = lax.broadcasted_iota(jnp.int32, v.shape, 1) < MLA_V
        one = jnp.ones_like(v)
        vh = [jnp.where(vhead0 == (h == 0), v, one) for h in range(2)]
        for sub in range(MLA_TQ // MLA_SUB_FWD):
            rows = slice(sub * MLA_SUB_FWD, (sub + 1) * MLA_SUB_FWD)
            outs, lses = [], []
            for h in range(2):
                cols = slice(h * LANES, (h + 1) * LANES)
                s = _dot_nt(q_ref[rows, cols], k_ref[:, cols])
                m = jnp.max(s, axis=-1, keepdims=True)
                p = jnp.exp2(s - m).astype(BF16)
                ol = _dot(p, vh[h])
                l = pltpu.roll(ol, MLA_V, 1)
                outs.append(ol / l)
                lses.append(m + jnp.log2(l))
            y_ref[rows, :] = jnp.where(head0, outs[0], outs[1])
            lse_ref[rows, :] = jnp.where(head0, lses[0], lses[1])

    return pl.pallas_call(
        body, grid=(nb, n_pairs, nq),
        in_specs=[pl.BlockSpec((MLA_TQ, 2 * LANES), lambda b, hp, i: (b * nq + i, hp)),
                  pl.BlockSpec((SEQ, 2 * LANES), lambda b, hp, i: (b, hp)),
                  pl.BlockSpec((SEQ, LANES), lambda b, hp, i: (b, hp))],
        out_specs=[pl.BlockSpec((MLA_TQ, LANES), lambda b, hp, i: (b * nq + i, hp))] * 2,
        out_shape=[SDS((t, MLA_WIDTH), F32)] * 2,
        name="mla_attn_fwd", compiler_params=_params("parallel", "parallel", "parallel"))(qb, kb, vb)


def mla_attn_bwd(qb, kb, vb, dy, y, lse, nb, xch):
    t = qb.shape[0]
    nq = SEQ // MLA_TQ
    n_pairs = MLA_HEADS // 2

    assert nq == 1

    def body(q_ref, k_ref, v_ref, do_ref, y_ref, lse_ref, dq_ref, dk_ref, dv_ref, dk_s, dv_s):
        dk_s[...] = jnp.zeros_like(dk_s)
        dv_s[...] = jnp.zeros_like(dv_s)
        head0 = lax.broadcasted_iota(jnp.int32, (MLA_SUB_BWD, LANES), 1) < MLA_V
        v = v_ref[...]
        for sub in range(MLA_TQ // MLA_SUB_BWD):
            rows = slice(sub * MLA_SUB_BWD, (sub + 1) * MLA_SUB_BWD)
            do = do_ref[rows, :]
            lse = lse_ref[rows, :]
            tt = do * y_ref[rows, :]
            dv = jnp.zeros((SEQ, LANES), F32)
            for h in range(2):
                sel = head0 if h == 0 else ~head0
                lo = h * MLA_V
                cols = slice(h * LANES, (h + 1) * LANES)
                q = q_ref[rows, cols]
                k = k_ref[:, cols]
                dd = jnp.sum(jnp.where(sel, tt, 0.0), axis=-1, keepdims=True)
                doh = jnp.where(sel, do, 0.0).astype(BF16)
                p = jnp.exp2(_dot_nt(q, k) - lse[:, lo:lo + 1])
                dp = _dot_nt(doh, v)
                ds = (p * (dp - dd)).astype(BF16)
                dq_ref[rows, cols] = _dot(ds, k).astype(dq_ref.dtype)
                dk_s[:, cols] += _dot_tn(ds, q)
                dv = dv + _dot_tn(p.astype(BF16), doh)
            dv_s[...] += dv
        dk_ref[...] = dk_s[...].astype(dk_ref.dtype)
        dv_ref[...] = dv_s[...].astype(dv_ref.dtype)

    qspec = pl.BlockSpec((MLA_TQ, 2 * LANES), lambda b, hp, i: (b * nq + i, hp))
    kspec = pl.BlockSpec((SEQ, 2 * LANES), lambda b, hp, i: (b, hp))
    vspec = pl.BlockSpec((SEQ, LANES), lambda b, hp, i: (b, hp))
    ospec = pl.BlockSpec((MLA_TQ, LANES), lambda b, hp, i: (b * nq + i, hp))
    return call_hosting_exchange(
        body, xch, grid=(nb, n_pairs, nq),
        in_specs=[qspec, kspec, vspec, ospec, ospec, ospec],
        out_specs=[qspec, kspec, vspec],
        out_shape=[SDS((t, MLA_QW), BF16), SDS((t, MLA_QW), BF16), SDS((t, MLA_WIDTH), BF16)],
        scratch_shapes=[pltpu.VMEM((SEQ, 2 * LANES), F32), pltpu.VMEM((SEQ, LANES), F32)],
        name="mla_attn_bwd", operands=(qb, kb, vb, dy, y, lse))


MEM_TQ = SEQ
MEM_SUB = SEQ
MEM_SCALE = MEM_HEAD_DIM ** -0.5
MQ_BLK4 = 5120 // MEM_WIDTH


def mem_attn_fwd(proj, mkv, nb):
    t = proj.shape[0]
    nq = SEQ // MEM_TQ

    def body(q_ref, mk_ref, mv_ref, y_ref):
        for sub in range(MEM_TQ // MEM_SUB):
            rows = slice(sub * MEM_SUB, (sub + 1) * MEM_SUB)
            for h in range(MEM_HEADS):
                cols = slice(h * LANES, (h + 1) * LANES)
                s = _dot_nt(q_ref[rows, cols].astype(BF16), mk_ref[:, cols]) * MEM_SCALE
                m = jnp.max(s, axis=-1, keepdims=True)
                p = jnp.exp(s - m)
                l = jnp.sum(p, axis=-1, keepdims=True)
                y_ref[rows, cols] = _dot(p.astype(BF16), mv_ref[:, cols]) / l

    return pl.pallas_call(
        body, grid=(nb, nq),
        in_specs=[pl.BlockSpec((MEM_TQ, MEM_WIDTH), lambda b, i: (b * nq + i, MQ_BLK4)),
                  pl.BlockSpec((N_MEM, MEM_WIDTH), lambda b, i: (b, 0)),
                  pl.BlockSpec((N_MEM, MEM_WIDTH), lambda b, i: (b, 1))],
        out_specs=pl.BlockSpec((MEM_TQ, MEM_WIDTH), lambda b, i: (b * nq + i, 0)),
        out_shape=SDS((t, MEM_WIDTH), F32),
        name="mem_attn_fwd", compiler_params=_params("parallel", "parallel"))(proj, mkv, mkv)


def mem_attn_bwd(proj, mkv, dy, nb):
    t = proj.shape[0]
    nq = SEQ // MEM_TQ

    def body(q_ref, mk_ref, mv_ref, do_ref, dq_ref, dmk_ref, dmv_ref):
        @pl.when(pl.program_id(1) == 0)
        def _():
            dmk_ref[...] = jnp.zeros_like(dmk_ref)
            dmv_ref[...] = jnp.zeros_like(dmv_ref)

        for sub in range(MEM_TQ // MEM_SUB):
            rows = slice(sub * MEM_SUB, (sub + 1) * MEM_SUB)
            for h in range(MEM_HEADS):
                cols = slice(h * LANES, (h + 1) * LANES)
                q = q_ref[rows, cols].astype(BF16)
                mk, mv = mk_ref[:, cols], mv_ref[:, cols]
                do = do_ref[rows, cols].astype(BF16)
                s = _dot_nt(q, mk) * MEM_SCALE
                e = jnp.exp(s - jnp.max(s, axis=-1, keepdims=True))
                p = e / jnp.sum(e, axis=-1, keepdims=True)
                dp = _dot_nt(do, mv)
                ds = (p * (dp - jnp.sum(p * dp, axis=-1, keepdims=True)) * MEM_SCALE).astype(BF16)
                dq_ref[rows, cols] = _dot(ds, mk).astype(BF16)
                dmk_ref[:, cols] += _dot_tn(ds, q)
                dmv_ref[:, cols] += _dot_tn(p.astype(BF16), do)

    ospec = pl.BlockSpec((MEM_TQ, MEM_WIDTH), lambda b, i: (b * nq + i, 0))
    kspec = pl.BlockSpec((N_MEM, MEM_WIDTH), lambda b, i: (b, 0))
    return pl.pallas_call(
        body, grid=(nb, nq),
        in_specs=[pl.BlockSpec((MEM_TQ, MEM_WIDTH), lambda b, i: (b * nq + i, MQ_BLK4)),
                  kspec, pl.BlockSpec((N_MEM, MEM_WIDTH), lambda b, i: (b, 1)), ospec],
        out_specs=[ospec, kspec, kspec],
        out_shape=[SDS((t, MEM_WIDTH), BF16), SDS((nb * N_MEM, MEM_WIDTH), F32), SDS((nb * N_MEM, MEM_WIDTH), F32)],
        name="mem_attn_bwd", compiler_params=_params("parallel", "arbitrary"))(proj, mkv, mkv, dy)


ROW_TM = 512
AG_BLK = 3072 // 1024
BG_BLK = 4608 // 512
MG_BLK = 5632 // 512
GROUPS = ((0, A_WIDTH), (A_WIDTH, MLA_WIDTH), (A_WIDTH + MLA_WIDTH, MEM_WIDTH))
D_MIX = 2048


def _gate_specs():
    def row(w, j):
        return pl.BlockSpec((ROW_TM, w), lambda i: (i, j))

    def vec(w):
        return pl.BlockSpec((1, w), lambda i: (0, 0))

    ys = [row(A_WIDTH, 0), row(MLA_WIDTH, 0), row(MEM_WIDTH, 0)]
    gates = [row(A_WIDTH, AG_BLK), row(MLA_WIDTH, BG_BLK), row(MEM_WIDTH, MG_BLK)]
    gains = [vec(A_WIDTH), vec(MLA_WIDTH), vec(MEM_WIDTH)]
    return row, vec, ys, gates, gains


def gate_out_ln_loss(ya, yb, ym, proj, goa, gob, gom, wout, h32, target, gp, bp):
    t, d = h32.shape
    _, _, ys, gates, gains = _gate_specs()

    def body(ya_ref, yb_ref, ym_ref, ga_ref, gb_ref, gm_ref, goa_ref, gob_ref, gom_ref, w_ref, h_ref, t_ref, gp_ref, bp_ref,
             z_ref, du32_ref, du16_ref, loss_ref, dgp_ref, dbp_ref):
        @pl.when(pl.program_id(0) == 0)
        def _():
            loss_ref[...] = jnp.zeros_like(loss_ref)
            dgp_ref[...] = jnp.zeros_like(dgp_ref)
            dbp_ref[...] = jnp.zeros_like(dbp_ref)

        for (off, w), y_ref, g_ref, go_ref in zip(GROUPS, (ya_ref, yb_ref, ym_ref), (ga_ref, gb_ref, gm_ref),
                                                  (goa_ref, gob_ref, gom_ref)):
            n, _ = _rms(y_ref[...], go_ref[...])
            gt = g_ref[...]
            z_ref[:, off:off + w] = (n * (gt * _sigmoid(gt))).astype(BF16)
        g = gp_ref[...]
        u = ALPHA * h_ref[...] + _dot(z_ref[...], w_ref[...])
        mu = jnp.mean(u, axis=-1, keepdims=True)
        uc = u - mu
        rstd = lax.rsqrt(jnp.mean(uc * uc, axis=-1, keepdims=True) + NORM_EPS)
        xhat = uc * rstd
        err = xhat * g + bp_ref[...] - t_ref[...]
        tok = jnp.sum(err * err, axis=-1, keepdims=True) * (1.0 / d)
        loss_ref[...] += 0.5 * jnp.sum(tok, axis=0, keepdims=True)
        dout = err * (1.0 / d)
        dxhat = dout * g
        du = rstd * (dxhat - jnp.mean(dxhat, axis=-1, keepdims=True)
                     - xhat * jnp.mean(dxhat * xhat, axis=-1, keepdims=True))
        du32_ref[...] = du
        du16_ref[...] = du.astype(BF16)
        dgp_ref[...] += jnp.sum(dout * xhat, axis=0, keepdims=True)
        dbp_ref[...] += jnp.sum(dout, axis=0, keepdims=True)

    row = pl.BlockSpec((ROW_TM, d), lambda i: (i, 0))
    vec = pl.BlockSpec((1, d), lambda i: (0, 0))
    zrow = pl.BlockSpec((ROW_TM, D_MIX), lambda i: (i, 0))
    return pl.pallas_call(
        body, grid=(t // ROW_TM,),
        in_specs=ys + gates + gains + [pl.BlockSpec((D_MIX, d), lambda i: (0, 0)), row, row, vec, vec],
        out_specs=[zrow, row, row, pl.BlockSpec((1, LANES), lambda i: (0, 0)), vec, vec],
        out_shape=[SDS((t, D_MIX), BF16), SDS((t, d), F32), SDS((t, d), BF16), SDS((1, LANES), F32), SDS((1, d), F32),
                   SDS((1, d), F32)],
        name="gate_out_ln_loss", compiler_params=_params("arbitrary"))(
            ya, yb, ym, proj, proj, proj, goa, gob, gom, wout, h32, target, gp, bp)


def gate_bwd(du16, wout, ya, yb, ym, proj, goa, gob, gom):
    t = ya.shape[0]
    row, vec, ys, gates, gains = _gate_specs()

    def body(du_ref, w_ref, ya_ref, yb_ref, ym_ref, ga_ref, gb_ref, gm_ref, goa_ref, gob_ref, gom_ref,
             dya_ref, dyb_ref, dym_ref, dga_ref, dgb_ref, dgm_ref, dgoa_ref, dgob_ref, dgom_ref):
        @pl.when(pl.program_id(0) == 0)
        def _():
            dgoa_ref[...] = jnp.zeros_like(dgoa_ref)
            dgob_ref[...] = jnp.zeros_like(dgob_ref)
            dgom_ref[...] = jnp.zeros_like(dgom_ref)

        dz = _dot_nt(du_ref[...], w_ref[...])
        for (off, w), y_ref, g_ref, go_ref, dy_ref, dg_ref, dgo_ref in zip(
                GROUPS, (ya_ref, yb_ref, ym_ref), (ga_ref, gb_ref, gm_ref), (goa_ref, gob_ref, gom_ref),
                (dya_ref, dyb_ref, dym_ref), (dga_ref, dgb_ref, dgm_ref), (dgoa_ref, dgob_ref, dgom_ref)):
            dzg = dz[:, off:off + w]
            y, gt, go = y_ref[...], g_ref[...], go_ref[...]
            n, r = _rms(y, go)
            sg = _sigmoid(gt)
            dg_ref[...] = (dzg * n * (sg * (1.0 + gt * (1.0 - sg)))).astype(BF16)
            dy, dgo = _rms_bwd(dzg * (gt * sg), y, r, go)
            dy_ref[...] = dy
            dgo_ref[...] += dgo

    widths = (A_WIDTH, MLA_WIDTH, MEM_WIDTH)
    return pl.pallas_call(
        body, grid=(t // ROW_TM,),
        in_specs=[row(D_MODEL, 0), pl.BlockSpec((D_MIX, D_MODEL), lambda i: (0, 0))] + ys + gates + gains,
        out_specs=[row(w, 0) for w in widths] * 2 + [vec(w) for w in widths],
        out_shape=[SDS((t, w), F32) for w in widths] + [SDS((t, w), BF16) for w in widths] + [SDS((1, w), F32) for w in widths],
        name="gate_bwd", compiler_params=_params("arbitrary"))(du16, wout, ya, yb, ym, proj, proj, proj, goa, gob, gom)


def dh_ln_bwd(pieces, win_t, du32, x2, g_emb, xch):
    t, d = x2.shape

    def body(*refs):
        p_refs = refs[:len(pieces)]
        w_ref, du_ref, x_ref, g_ref, dx_ref, dg_ref, db_ref = refs[len(pieces):]

        @pl.when(pl.program_id(0) == 0)
        def _():
            dg_ref[...] = jnp.zeros_like(dg_ref)
            db_ref[...] = jnp.zeros_like(db_ref)

        dh = ALPHA * du_ref[...]
        for p_ref, off, w in zip(p_refs, PIECE_OFFS, PIECE_WIDTHS):
            dh = dh + _dot(p_ref[...], w_ref[off:off + w, :])
        x = x_ref[...]
        xc = x - jnp.mean(x, axis=-1, keepdims=True)
        rstd = lax.rsqrt(jnp.mean(xc * xc, axis=-1, keepdims=True) + NORM_EPS)
        xhat = xc * rstd
        dg_ref[...] += jnp.sum(dh * xhat, axis=0, keepdims=True)
        db_ref[...] += jnp.sum(dh, axis=0, keepdims=True)
        tg = dh * g_ref[...]
        dx_ref[...] = rstd * (tg - jnp.mean(tg, axis=-1, keepdims=True)
                              - xhat * jnp.mean(tg * xhat, axis=-1, keepdims=True))

    row = pl.BlockSpec((ROW_TM, d), lambda i: (i, 0))
    vec = pl.BlockSpec((1, d), lambda i: (0, 0))
    return call_hosting_exchange(
        body, xch, grid=(t // ROW_TM,),
        in_specs=[pl.BlockSpec((ROW_TM, w), lambda i: (i, 0)) for w in PIECE_WIDTHS]
        + [pl.BlockSpec(win_t.shape, lambda i: (0, 0)), row, row, vec],
        out_specs=[row, vec, vec],
        out_shape=[SDS((t, d), F32), SDS((1, d), F32), SDS((1, d), F32)],
        scratch_shapes=[], name="dh_ln_bwd", operands=(*pieces, win_t, du32, x2, g_emb))


def _adamw(w, g, m, v):
    m2 = ADAM_B1 * m + (1.0 - ADAM_B1) * g
    v2 = ADAM_B2 * v + (1.0 - ADAM_B2) * (g * g)
    m_hat = m2 / (1.0 - ADAM_B1 ** ADAM_STEP)
    v_hat = v2 / (1.0 - ADAM_B2 ** ADAM_STEP)
    return -ADAM_LR * (m_hat / (jnp.sqrt(v_hat) + ADAM_EPS) + ADAM_WD * w), m2, v2


def adamw_shard(w, parts, m, v, name):
    r, c = w.shape
    if r % 256 == 0 or r * c <= 256 * 1024:
        tr, tc = min(r, 256), c
    else:
        tr, tc = r, 256

    def body(w_ref, p_ref, m_ref, v_ref, g_ref, d_ref, nm_ref, nv_ref):
        g = p_ref[0].astype(F32)
        for k in range(1, N_DEV):
            g = g + p_ref[k].astype(F32)
        g_ref[...] = g
        d_ref[...], nm_ref[...], nv_ref[...] = _adamw(w_ref[...], g, m_ref[...], v_ref[...])

    blk = pl.BlockSpec((tr, tc), lambda i, j: (i, j))
    return pl.pallas_call(
        body, grid=(r // tr, c // tc),
        in_specs=[blk, pl.BlockSpec((N_DEV, tr, tc), lambda i, j: (0, i, j)), blk, blk],
        out_specs=[blk] * 4, out_shape=[SDS((r, c), F32)] * 4, name=name,
        compiler_params=_params("parallel", "parallel"))(w, parts, m, v)


def adamw_shards_whole(ws, parts, ms, vs, name):
    n = len(ws)

    def body(*refs):
        w_refs, p_refs, m_refs, v_refs = refs[:n], refs[n:2 * n], refs[2 * n:3 * n], refs[3 * n:4 * n]
        outs = refs[4 * n:]
        for i in range(n):
            g = p_refs[i][0].astype(F32)
            for k in range(1, N_DEV):
                g = g + p_refs[i][k].astype(F32)
            outs[4 * i][...] = g
            outs[4 * i + 1][...], outs[4 * i + 2][...], outs[4 * i + 3][...] = _adamw(
                w_refs[i][...], g, m_refs[i][...], v_refs[i][...])

    res = pl.pallas_call(
        body, out_shape=[SDS(w.shape, F32) for w in ws for _ in range(4)], name=name,
        compiler_params=_params())(*ws, *parts, *ms, *vs)
    return [res[4 * i:4 * i + 4] for i in range(n)]


def _place():
    return lax.axis_index("x"), lax.axis_index("y"), lax.axis_index("c")


def _flat(px, py, pc):
    return 4 * px + 2 * py + pc


def _peer(x, y, c, k):
    return (1 - x if k & 4 else x, 1 - y if k & 2 else y, 1 - c if k & 1 else c)


def cast_shards(shards):
    def body(*refs):
        n = len(refs) // 2
        for i_ref, o_ref in zip(refs[:n], refs[n:]):
            o_ref[...] = i_ref[...].astype(BF16)

    return pl.pallas_call(body, out_shape=[SDS(s.shape, BF16) for s in shards], name="cast_shards",
                          compiler_params=_params())(*shards)


def _two_level_gather_plan(src_refs, land_refs, send_sems, recv_sems, local_sems):
    n = len(src_refs)
    x, y, c = _place()
    me, sib = (x, y, c), (x, y, 1 - c)
    chips = [(1 - x, y), (x, 1 - y), (1 - x, 1 - y)]

    def copy(a, k, block, to, src=None):
        dst = land_refs[a].at[_flat(*block)]
        return pltpu.make_async_remote_copy(
            src_ref=dst if src is None else src, dst_ref=dst,
            send_sem=send_sems.at[a * N_DEV + k], recv_sem=recv_sems.at[a * N_DEV + k],
            device_id=to, device_id_type=MESH)

    mine = [pltpu.make_async_copy(src_refs[a], land_refs[a].at[_flat(*me)], local_sems.at[a]) for a in range(n)]
    first = []
    for a in range(n):
        first.append(copy(a, 0, me, sib, src=src_refs[a]))
        first += [copy(a, 1 + j, me, (*chip, c), src=src_refs[a]) for j, chip in enumerate(chips)]

    def start():
        for cp in mine + first:
            cp.start()

    def finish():
        passed = []
        for j, chip in enumerate(chips):
            for a in range(n):
                copy(a, 1 + j, (*chip, c), me).wait_recv()
                fwd = copy(a, 4 + j, (*chip, c), sib)
                fwd.start()
                passed.append(fwd)
        for a in range(n):
            copy(a, 0, sib, me).wait_recv()
            for j, chip in enumerate(chips):
                copy(a, 4 + j, (*chip, 1 - c), me).wait_recv()
        for cp in first + passed:
            cp.wait_send()
        for cp in mine:
            cp.wait()

    return start, finish


ALL_DEVICES = tuple(range(N_DEV))


def _exchange_plan(src_refs, land_refs, dests, send_sems, recv_sems, local_sems):
    x, y, c = _place()
    me = _flat(x, y, c)
    plan = []
    for a, (src, land, dl) in enumerate(zip(src_refs, land_refs, dests)):
        for li, j in enumerate(dl):
            to = ((j >> 2) & 1, (j >> 1) & 1, j & 1)
            block = src.at[li] if len(src.shape) == len(land.shape) else src

            def push(slot, a=a, block=block, land=land, j=j, to=to):
                return pltpu.make_async_remote_copy(
                    src_ref=block, dst_ref=land.at[slot], send_sem=send_sems.at[a * N_DEV + j],
                    recv_sem=recv_sems.at[a * N_DEV + slot], device_id=to, device_id_type=MESH)

            own = pltpu.make_async_copy(block, land.at[j], local_sems.at[a])
            plan.append((j, push(me), own, [push(s) for s in range(N_DEV) if s != j]))
    return me, plan


def _exchange_start(me, plan):
    for j, send, own, _ in plan:
        @pl.when(me != j)
        def _(send=send):
            send.start()

        @pl.when(me == j)
        def _(own=own):
            own.start()


def _exchange_wait(me, plan):
    for j, send, own, arrivals in plan:
        @pl.when(me != j)
        def _(send=send):
            send.wait_send()

        @pl.when(me == j)
        def _(own=own, arrivals=arrivals):
            own.wait()
            for arrival in arrivals:
                arrival.wait_recv()


def call_hosting_exchange(core, xch, *, grid, in_specs, out_specs, out_shape, scratch_shapes, name, operands):
    srcs, dests, landing = xch
    n, n_in, n_out, n_scr = len(srcs), len(in_specs), len(out_specs), len(scratch_shapes)

    def body(*refs):
        ins, src_refs = refs[:n_in], refs[n_in:n_in + n]
        outs = refs[n_in + 2 * n:n_in + 2 * n + n_out]
        land_refs = refs[n_in + 2 * n + n_out:n_in + 3 * n + n_out]
        scratch = refs[n_in + 3 * n + n_out:n_in + 3 * n + n_out + n_scr]
        sems = refs[n_in + 3 * n + n_out + n_scr:]
        first = functools.reduce(jnp.logical_and, [pl.program_id(i) == 0 for i in range(len(grid))])
        last = functools.reduce(jnp.logical_and, [pl.program_id(i) == grid[i] - 1 for i in range(len(grid))])
        if dests is None:
            start, finish = _two_level_gather_plan(src_refs, land_refs, *sems)
        else:
            me, plan = _exchange_plan(src_refs, land_refs, dests, *sems)
            start, finish = functools.partial(_exchange_start, me, plan), functools.partial(_exchange_wait, me, plan)
        pl.when(first)(start)
        core(*ins, *outs, *scratch)
        pl.when(last)(finish)

    hbm = pl.BlockSpec(memory_space=pl.ANY)
    res = pl.pallas_call(
        body, grid=grid,
        in_specs=list(in_specs) + [hbm] * (2 * n), out_specs=list(out_specs) + [hbm] * n,
        out_shape=list(out_shape) + [SDS(l.shape, l.dtype) for l in landing],
        scratch_shapes=list(scratch_shapes) + [pltpu.SemaphoreType.DMA((N_DEV * n,)), pltpu.SemaphoreType.DMA((N_DEV * n,)),
                                               pltpu.SemaphoreType.DMA((n,))],
        input_output_aliases={n_in + n + k: n_out + k for k in range(n)},
        name=name, compiler_params=_params(*(("arbitrary",) * len(grid))))(*operands, *srcs, *landing)
    return res[:n_out], res[n_out:]


SLOT_ROWS = 8


def small_allreduce_adamw(loss_sum, grads, ws, ms, vs):
    n = len(grads)
    rows = [g.shape[0] for g in grads]
    total = SLOT_ROWS * (n + 1)

    def body(*refs):
        loss_ref, g_refs, w_refs = refs[0], refs[1:1 + n], refs[1 + n:1 + 2 * n]
        m_refs, v_refs = refs[1 + 2 * n:1 + 3 * n], refs[1 + 3 * n:1 + 4 * n]
        outs = refs[1 + 4 * n:2 + 8 * n]
        vec, gath, tot, send_sems, recv_sems = refs[2 + 8 * n:]
        x, y, c = _place()
        me = _flat(x, y, c)
        vec[...] = jnp.zeros_like(vec)
        vec[0:1, :] = loss_ref[...]
        for i in range(n):
            vec[SLOT_ROWS * (i + 1):SLOT_ROWS * (i + 1) + rows[i], :] = g_refs[i][...]
        gath[me] = vec[...]
        copies = []
        for k in range(1, N_DEV):
            peer = _peer(x, y, c, k)
            copies.append(pltpu.make_async_remote_copy(
                src_ref=vec, dst_ref=gath.at[me], send_sem=send_sems.at[k - 1], recv_sem=recv_sems.at[k - 1],
                device_id=peer, device_id_type=MESH))
        for cp in copies:
            cp.start()
        for cp in copies:
            cp.wait_recv()
        for cp in copies:
            cp.wait_send()
        g = gath[0]
        for j in range(1, N_DEV):
            g = g + gath[j]
        tot[...] = g
        outs[0][...] = tot[0:1, :]
        for i in range(n):
            gi = tot[SLOT_ROWS * (i + 1):SLOT_ROWS * (i + 1) + rows[i], :]
            outs[1 + i][...] = gi
            outs[1 + n + i][...], outs[1 + 2 * n + i][...], outs[1 + 3 * n + i][...] = _adamw(
                w_refs[i][...], gi, m_refs[i][...], v_refs[i][...])

    shapes = [SDS(g.shape, F32) for g in grads]
    return pl.pallas_call(
        body, out_shape=[SDS((1, LANES), F32)] + shapes * 4,
        scratch_shapes=[pltpu.VMEM((total, LANES), F32), pltpu.VMEM((N_DEV, total, LANES), F32), pltpu.VMEM((total, LANES), F32),
                        pltpu.SemaphoreType.DMA((7,)), pltpu.SemaphoreType.DMA((7,))],
        name="small_allreduce_adamw", compiler_params=_params())(loss_sum, *grads, *ws, *ms, *vs)


def _rope_lane_patterns():
    inv = lambda r: ROPE_THETA ** (-(jnp.arange(0, r, 2, dtype=F32) / r))
    z = lambda n: jnp.zeros((n,), F32)
    o = lambda n: jnp.ones((n,), F32)
    half, rest = A_ROT // 2, A_HEAD_DIM - A_ROT
    ia, im = inv(A_ROT), inv(MLA_ROPE)
    mh, tail = MLA_ROPE // 2, LANES - MLA_NOPE - MLA_ROPE
    rows = [jnp.tile(jnp.concatenate([ia, ia, z(rest)]), 2),
            jnp.tile(jnp.concatenate([o(half), z(half + rest)]), 2),
            jnp.tile(jnp.concatenate([z(half), o(half), z(rest)]), 2),
            jnp.concatenate([z(MLA_NOPE), im, im, z(tail)]),
            jnp.concatenate([z(MLA_NOPE), o(mh), z(mh + tail)]),
            jnp.concatenate([z(MLA_NOPE + mh), o(mh), z(tail)]),
            z(LANES), z(LANES)]
    return jnp.stack(rows)


KR_LO, KR_HI = 4480, 4512
W_IN_SHARD = D_IN // N_DEV
BG_SPLIT = 6 * W_IN_SHARD - KR_HI


def w_in_working_t(g):
    pad_lo, pad_hi = MLA_NOPE, LANES - MLA_NOPE - MLA_ROPE
    spans = []
    for lo, hi, shift in ((0, KR_LO, 0), (KR_LO, KR_HI, pad_lo), (KR_HI, D_IN, pad_lo + pad_hi)):
        r = lo
        while r < hi:
            j = r // W_IN_SHARD
            n = min(hi, (j + 1) * W_IN_SHARD) - r
            spans.append((j, r - j * W_IN_SHARD, n, r + shift))
            r += n

    def body(g_ref, o_ref):
        o_ref[KR_LO:KR_LO + pad_lo, :] = jnp.zeros((pad_lo, D_MODEL), o_ref.dtype)
        o_ref[KR_HI + pad_lo:KR_HI + pad_lo + pad_hi, :] = jnp.zeros((pad_hi, D_MODEL), o_ref.dtype)
        for j, src, n, dst in spans:
            o_ref[dst:dst + n, :] = g_ref[j, src:src + n, :]

    return pl.pallas_call(body, out_shape=SDS((D_INW, D_MODEL), g.dtype), name="w_in_working_t", compiler_params=_params())(g)


def _w_in_shard_5(d_ag_tail, d_cc, d_bg_head):
    kr = MLA_Q_RANK + MLA_KV_RANK + MLA_NOPE
    rows = jnp.concatenate([d_ag_tail, d_cc[:MLA_Q_RANK + MLA_KV_RANK], d_cc[kr:kr + MLA_ROPE], d_bg_head], 0)
    return rows.reshape(1, W_IN_SHARD, D_MODEL).astype(BF16)


def _w_uq_working(g):
    w = jnp.pad(g.transpose(1, 0, 2), ((0, 0), (0, 0), (0, LANES - MLA_NOPE - MLA_ROPE)))
    return w.reshape(MLA_Q_RANK, MLA_QW)


def _w_uq_parts(dw):
    return dw.reshape(MLA_Q_RANK, MLA_HEADS, LANES)[:, :, :MLA_NOPE + MLA_ROPE].transpose(1, 0, 2)


def _w_ukv_working(g):
    wk = jnp.pad(g[:, :, :MLA_NOPE].transpose(1, 0, 2), ((0, 0), (0, 0), (0, LANES - MLA_NOPE)))
    wv = g[:, :, MLA_NOPE:].transpose(1, 0, 2)
    return jnp.concatenate([wk.reshape(MLA_KV_RANK, MLA_QW), wv.reshape(MLA_KV_RANK, MLA_WIDTH)], 1)


def _w_ukv_parts(dw):
    dk = dw[:, :MLA_QW].reshape(MLA_KV_RANK, MLA_HEADS, LANES)[:, :, :MLA_NOPE]
    dv = dw[:, MLA_QW:].reshape(MLA_KV_RANK, MLA_HEADS, MLA_V)
    return jnp.concatenate([dk, dv], -1).transpose(1, 0, 2)


SMALL_NAMES = ("g_emb", "b_emb", "g_cq", "g_ckv", "g_out_a", "g_out_b", "g_out_m", "g_post", "b_post")


def kernel(x, mem, positions, g_emb, b_emb, w_in, g_cq, g_ckv, w_uq, w_ukv, w_mem_kv, g_out_a, g_out_b, g_out_m, w_out, g_post, b_post, loss_target, m_g_emb, m_b_emb, m_w_in, m_g_cq, m_g_ckv, m_w_uq, m_w_ukv, m_w_mem_kv, m_g_out_a, m_g_out_b, m_g_out_m, m_w_out, m_g_post, m_b_post, v_g_emb, v_b_emb, v_w_in, v_g_cq, v_g_ckv, v_w_uq, v_w_ukv, v_w_mem_kv, v_g_out_a, v_g_out_b, v_g_out_m, v_w_out, v_g_post, v_b_post):
    nb = x.shape[0]
    t = nb * SEQ
    x2 = x.reshape(t, D_MODEL)
    tgt2 = loss_target.reshape(t, D_MODEL)
    mem2 = mem.reshape(nb * N_MEM, D_MODEL)
    g_emb2, b_emb2 = g_emb.reshape(1, -1), b_emb.reshape(1, -1)

    w_in_t, m_w_in_t, v_w_in_t = w_in[0].T, m_w_in[0].T, v_w_in[0].T
    s_in, s_uq, s_ukv, s_mem, s_out = cast_shards((w_in_t, w_uq[0], w_ukv[0], w_mem_kv[0], w_out[0]))
    (h32, h16, (a_c, a_sa, a_sb), (m_c, m_sa, m_sb)), (g_in,) = embed_fwd(
        x2, g_emb2, b_emb2, positions, ((s_in,), None, (lax.empty((N_DEV,) + s_in.shape, BF16),)))
    win_t = w_in_working_t(g_in)

    proj = mm_nn(h16, win_t, F32, 2048, 1536, "proj", rhs_transposed=True)
    later = (s_uq, s_ukv, s_mem, s_out)
    (ya, lse_a), qkv_d, (g_uq, g_ukv, g_mem, g_out) = a_attn_fwd(
        proj, a_c, a_sa, a_sb, nb,
        (later, (ALL_DEVICES,) * len(later), tuple(lax.empty((N_DEV,) + w.shape, BF16) for w in later)))
    wuq_w = _w_uq_working(g_uq)
    wkv_w = _w_ukv_working(g_ukv)
    wmem = g_mem.reshape(D_MODEL, 2 * MEM_WIDTH)
    wout = g_out.reshape(D_MIX, D_MODEL)
    qb, kb, vb = mla_prep_fwd(proj, m_c, m_sa, m_sb, g_cq, g_ckv, wuq_w, wkv_w)
    yb, lse_b = mla_attn_fwd(qb, kb, vb, nb)
    mkv = mm_nn(mem2, wmem, BF16, nb * N_MEM, 512, "mem_kv")
    ym = mem_attn_fwd(proj, mkv, nb)
    z, du32, du16, loss_sum, dg_post, db_post = gate_out_ln_loss(
        ya, yb, ym, proj, g_out_a, g_out_b, g_out_m, wout, h32, tgt2, g_post, b_post)

    dya, dyb, dym, dag, dbg, dmg, dg_out_a, dg_out_b, dg_out_m = gate_bwd(
        du16, wout, ya, yb, ym, proj, g_out_a, g_out_b, g_out_m)
    dw_out = mm_tn(z, du16, 1024, "dw_out")
    dmq, dmk, dmv = mem_attn_bwd(proj, mkv, dym, nb)
    dw_mem = mm_tn(mem2, jnp.concatenate([dmk, dmv], 1), nb * N_MEM, "dw_mem")
    d_gates, shards_6_7 = mm_tn_group((dbg, dmq, dmg), h16, 2048, "dw_in_bg_mq_mg", W_IN_SHARD, BG_SPLIT, 2)
    landing = lambda w, dtype=F32: lax.empty((N_DEV,) + w.shape, dtype)
    big_w = (w_in_t, w_uq[0], w_ukv[0], w_mem_kv[0], w_out[0])
    (daq, dak, dav), (p_out, p_mem, p_in) = a_attn_bwd(
        qkv_d, a_c, a_sa, a_sb, dya, ya, lse_a, nb,
        ((dw_out.reshape(N_DEV, D_MIX // N_DEV, D_MODEL), dw_mem.reshape(N_DEV, D_MODEL // N_DEV, 2 * MEM_WIDTH),
          shards_6_7),
         (ALL_DEVICES, ALL_DEVICES, (6, 7)),
         (landing(w_out[0]), landing(w_mem_kv[0]), landing(w_in_t, BF16))))
    d_a, shards_0_4 = mm_tn_group((daq, dak, dav, dag), h16, 1024, "dw_in_aq_ak_av_ag", W_IN_SHARD, 0, 5)
    (dqb, dkb, dvb), (p_in,) = mla_attn_bwd(
        qb, kb, vb, dyb, yb, lse_b, nb, ((shards_0_4,), ((0, 1, 2, 3, 4),), (p_in,)))
    dcc, dqf, cqn, dkvf, ckvn, dg_cq, dg_ckv = mla_prep_bwd(proj, m_c, m_sa, m_sb, g_cq, g_ckv, wuq_w, wkv_w, dqb, dkb, dvb)
    dw_uq, dw_ukv, d_cc = mm_tn_pairs(((cqn, dqf), (ckvn, dkvf), (dcc, h16)), 1024, "dw_uq_ukv_cc")
    pieces = (daq, dak, dav, dag, dcc, dbg, dmq, dmg)
    (grad_x, dg_emb, db_emb), (p_in, p_uq, p_ukv) = dh_ln_bwd(
        pieces, win_t, du32, x2, g_emb2,
        ((_w_in_shard_5(d_a[5 * W_IN_SHARD:], d_cc, d_gates[:BG_SPLIT]), _w_uq_parts(dw_uq), _w_ukv_parts(dw_ukv)),
         ((5,), ALL_DEVICES, ALL_DEVICES),
         (p_in, landing(w_uq[0]), landing(w_ukv[0]))))

    parts = (p_in, p_uq, p_ukv, p_mem, p_out)
    big_m = (m_w_in_t, m_w_uq[0], m_w_ukv[0], m_w_mem_kv[0], m_w_out[0])
    big_v = (v_w_in_t, v_w_uq[0], v_w_ukv[0], v_w_mem_kv[0], v_w_out[0])
    big = {"w_in": [o.T[None] for o in adamw_shard(big_w[0], parts[0], big_m[0], big_v[0], "adamw_w_in")]}
    rest = adamw_shards_whole(big_w[1:], parts[1:], big_m[1:], big_v[1:], "adamw_rest")
    for name, res in zip(("w_uq", "w_ukv", "w_mem_kv", "w_out"), rest):
        big[name] = [o[None] for o in res]

    small_w = (g_emb, b_emb, g_cq, g_ckv, g_out_a, g_out_b, g_out_m, g_post, b_post)
    small_m = (m_g_emb, m_b_emb, m_g_cq, m_g_ckv, m_g_out_a, m_g_out_b, m_g_out_m, m_g_post, m_b_post)
    small_v = (v_g_emb, v_b_emb, v_g_cq, v_g_ckv, v_g_out_a, v_g_out_b, v_g_out_m, v_g_post, v_b_post)
    small_g = (dg_emb, db_emb, dg_cq, dg_ckv, dg_out_a, dg_out_b, dg_out_m, dg_post, db_post)
    rows128 = lambda vals: [v.reshape(-1, LANES) for v in vals]
    res = small_allreduce_adamw(loss_sum, rows128(small_g), rows128(small_w), rows128(small_m), rows128(small_v))
    loss = res[0][0, 0]
    n_small = len(small_w)
    sg, sd, sm, sv = [[r.reshape(w.shape) for r, w in zip(res[1 + k * n_small:1 + (k + 1) * n_small], small_w)]
                      for k in range(4)]

    order = ("g_emb", "b_emb", "w_in", "g_cq", "g_ckv", "w_uq", "w_ukv", "w_mem_kv", "g_out_a", "g_out_b", "g_out_m",
             "w_out", "g_post", "b_post")
    small_idx = {n: i for i, n in enumerate(SMALL_NAMES)}
    outs = [loss, grad_x.reshape(x.shape)]
    for kind in range(4):
        for name in order:
            outs.append(big[name][kind] if name in big else (sg, sd, sm, sv)[kind][small_idx[name]])
    return tuple(outs)
```

```python
import functools

import jax
import jax.numpy as jnp
from jax import lax
from jax.experimental import pallas as pl
from jax.experimental.pallas import tpu as pltpu

F32 = jnp.float32
BF16 = jnp.bfloat16
SDS = jax.ShapeDtypeStruct
MESH = pl.DeviceIdType.MESH

D_MODEL = 1024
SEQ = 2048
A_HEADS, A_HEAD_DIM, A_ROT = 16, 64, 16
A_WIDTH = 1024
DILATIONS = (1, 4, 16)
N_SIDE = 64
MLA_HEADS, MLA_Q_RANK, MLA_KV_RANK = 8, 256, 128
MLA_NOPE, MLA_ROPE, MLA_V = 64, 32, 64
MLA_WIDTH = 512
N_MEM, MEM_HEADS, MEM_HEAD_DIM, MEM_WIDTH = 256, 4, 128, 512
ROPE_THETA = 500000.0
NORM_EPS = 1e-5
NEG_INF = -1e30
ALPHA = 2.0 ** 0.25
D_IN = 6048
N_DEV = 8

ADAM_LR, ADAM_B1, ADAM_B2, ADAM_EPS, ADAM_WD, ADAM_STEP = 0.001, 0.9, 0.999, 1e-08, 0.01, 10

D_INW = 6144
PIECE_WIDTHS = (1024, 1024, 1024, 1024, 512, 512, 512, 512)
PIECE_OFFS = (0, 1024, 2048, 3072, 4096, 4608, 5120, 5632)
LANES = 128
VMEM_LIMIT = 56 * 1024 * 1024


def _params(*sem):
    kw = dict(vmem_limit_bytes=VMEM_LIMIT)
    if sem:
        kw["dimension_semantics"] = sem
    return pltpu.CompilerParams(**kw)


def _dot(a, b):
    return jnp.dot(a, b, preferred_element_type=F32)


def _dot_nt(a, b):
    return lax.dot_general(a, b, (((1,), (1,)), ((), ())), preferred_element_type=F32)


def _dot_tn(a, b):
    return lax.dot_general(a, b, (((0,), (0,)), ((), ())), preferred_element_type=F32)


def _sigmoid(x):
    return 1.0 / (1.0 + jnp.exp(-x))


def _rope_fwd(x, c, sa, sb, half):
    n = x.shape[-1]
    return x * c + pltpu.roll(x, n - half, 1) * sa + pltpu.roll(x, half, 1) * sb


def _rope_bwd(dy, c, sa, sb, half):
    n = dy.shape[-1]
    return dy * c + pltpu.roll(dy * sa, half, 1) + pltpu.roll(dy * sb, n - half, 1)


def mm_nn(a, b, out_dtype, tm, tn, name, rhs_transposed=False):
    m, k = a.shape
    n = b.shape[0] if rhs_transposed else b.shape[1]
    dot = _dot_nt if rhs_transposed else _dot

    def body(a_ref, b_ref, o_ref):
        o_ref[...] = dot(a_ref[...].astype(BF16), b_ref[...].astype(BF16)).astype(o_ref.dtype)

    b_spec = pl.BlockSpec((tn, k), lambda j, i: (j, 0)) if rhs_transposed else pl.BlockSpec((k, tn), lambda j, i: (0, j))
    return pl.pallas_call(
        body, grid=(n // tn, m // tm),
        in_specs=[pl.BlockSpec((tm, k), lambda j, i: (i, 0)), b_spec],
        out_specs=pl.BlockSpec((tm, tn), lambda j, i: (i, j)),
        out_shape=SDS((m, n), out_dtype), name=name,
        compiler_params=_params("parallel", "parallel"))(a, b)


def mm_tn(a, b, tt, name):
    t, m = a.shape
    n = b.shape[1]

    def body(a_ref, b_ref, o_ref):
        @pl.when(pl.program_id(0) == 0)
        def _():
            o_ref[...] = jnp.zeros_like(o_ref)

        o_ref[...] += _dot_tn(a_ref[...].astype(BF16), b_ref[...].astype(BF16))

    return pl.pallas_call(
        body, grid=(t // tt,),
        in_specs=[pl.BlockSpec((tt, m), lambda i: (i, 0)), pl.BlockSpec((tt, n), lambda i: (i, 0))],
        out_specs=pl.BlockSpec((m, n), lambda i: (0, 0)),
        out_shape=SDS((m, n), F32), name=name,
        compiler_params=_params("arbitrary"))(a, b)


def mm_tn_pairs(pairs, tt, name):
    n = len(pairs)
    t = pairs[0][0].shape[0]

    def body(*refs):
        @pl.when(pl.program_id(0) == 0)
        def _():
            for o_ref in refs[2 * n:]:
                o_ref[...] = jnp.zeros_like(o_ref)

        for i in range(n):
            refs[2 * n + i][...] += _dot_tn(refs[2 * i][...].astype(BF16), refs[2 * i + 1][...].astype(BF16))

    rows = lambda x: pl.BlockSpec((tt, x.shape[1]), lambda i: (i, 0))
    return pl.pallas_call(
        body, grid=(t // tt,),
        in_specs=[rows(x) for pair in pairs for x in pair],
        out_specs=[pl.BlockSpec((a.shape[1], b.shape[1]), lambda i: (0, 0)) for a, b in pairs],
        out_shape=[SDS((a.shape[1], b.shape[1]), F32) for a, b in pairs], name=name,
        compiler_params=_params("arbitrary"))(*[x for pair in pairs for x in pair])


def mm_tn_group(pieces, b, tt, name, slab_rows, first_slab_row, n_slabs, keep_rows):
    n, (t, w), cols = len(pieces), pieces[0].shape, b.shape[1]
    nt = t // tt
    keep_lo, keep_hi = keep_rows

    def body(*refs):
        p_refs, b_ref, slab_ref, keep_ref, acc = refs[:n], refs[n], refs[n + 1], refs[n + 2], refs[n + 3]

        @pl.when(pl.program_id(1) == 0)
        def _():
            acc[...] = jnp.zeros_like(acc)

        for k in range(n):
            @pl.when(pl.program_id(0) == k)
            def _(k=k):
                acc[...] += _dot_tn(p_refs[k][...], b_ref[...])

            @pl.when((pl.program_id(0) == k) & (pl.program_id(1) == nt - 1))
            def _(k=k):
                for j in range(n_slabs):
                    lo = max(k * w, first_slab_row + j * slab_rows)
                    hi = min((k + 1) * w, first_slab_row + (j + 1) * slab_rows)
                    if lo < hi:
                        dst = lo - first_slab_row - j * slab_rows
                        slab_ref[j, dst:dst + hi - lo, :] = acc[lo - k * w:hi - k * w, :].astype(slab_ref.dtype)
                lo, hi = max(k * w, keep_lo), min((k + 1) * w, keep_hi)
                if lo < hi:
                    keep_ref[lo - keep_lo:hi - keep_lo, :] = acc[lo - k * w:hi - k * w, :]

    def piece_spec(k):
        return pl.BlockSpec((tt, w), lambda p, i: (jnp.where(p < k, 0, jnp.where(p > k, nt - 1, i)), 0))

    return pl.pallas_call(
        body, grid=(n, nt),
        in_specs=[piece_spec(k) for k in range(n)] + [pl.BlockSpec((tt, cols), lambda p, i: (i, 0))],
        out_specs=[pl.BlockSpec((n_slabs, slab_rows, cols), lambda p, i: (0, 0, 0)),
                   pl.BlockSpec((keep_hi - keep_lo, cols), lambda p, i: (0, 0))],
        out_shape=[SDS((n_slabs, slab_rows, cols), BF16), SDS((keep_hi - keep_lo, cols), F32)],
        scratch_shapes=[pltpu.VMEM((w, cols), F32)], name=name,
        compiler_params=_params("arbitrary", "arbitrary"))(*pieces, b)


def embed_fwd(x2, g, b, positions, xch):
    t, d = x2.shape
    tm = 512
    pos = positions.astype(F32).reshape(-1, 1)

    def body(x_ref, g_ref, b_ref, pos_ref, pat_ref, h32_ref, h16_ref, *tabs):
        x = x_ref[...]
        mu = jnp.mean(x, axis=-1, keepdims=True)
        xc = x - mu
        var = jnp.mean(xc * xc, axis=-1, keepdims=True)
        h = xc * lax.rsqrt(var + NORM_EPS) * g_ref[...] + b_ref[...]
        h32_ref[...] = h
        h16_ref[...] = h.astype(BF16)
        p = pos_ref[...]
        for k in range(2):
            inv, first, second = pat_ref[3 * k:3 * k + 1, :], pat_ref[3 * k + 1:3 * k + 2, :], pat_ref[3 * k + 2:3 * k + 3, :]
            ang = p * inv
            sn = jnp.sin(ang)
            tabs[3 * k][...] = jnp.where(first + second > 0.0, jnp.cos(ang), 1.0)
            tabs[3 * k + 1][...] = -first * sn
            tabs[3 * k + 2][...] = second * sn

    row = pl.BlockSpec((tm, d), lambda i: (i, 0))
    vec = pl.BlockSpec((1, d), lambda i: (0, 0))
    tab = pl.BlockSpec((tm, LANES), lambda i: (i, 0))
    res, landed = call_hosting_exchange(
        body, xch, grid=(t // tm,),
        in_specs=[row, vec, vec, pl.BlockSpec((tm, 1), lambda i: (i, 0)), pl.BlockSpec((8, LANES), lambda i: (0, 0))],
        out_specs=[row, row] + [tab] * 6,
        out_shape=[SDS((t, d), F32), SDS((t, d), BF16)] + [SDS((t, LANES), F32)] * 6,
        scratch_shapes=[], name="embed_fwd", operands=(x2, g, b, pos, _rope_lane_patterns()))
    return (res[0], res[1], tuple(res[2:5]), tuple(res[5:8])), landed


Q_BLK = 128
UNROLL_FWD = 16
UNROLL_BWD = 16


def _pattern_geometry(d):
    length = SEQ // d
    nblk = length // Q_BLK
    kwin = min(2 * Q_BLK, length)
    return length, nblk, kwin


def _block_coords(idx, d):
    length, nblk, kwin = _pattern_geometry(d)
    r = lax.shift_right_logical(idx, nblk.bit_length() - 1)
    i = idx & (nblk - 1)
    q0 = pl.multiple_of(r * length + i * Q_BLK, Q_BLK)
    ks = jnp.clip(i * Q_BLK - N_SIDE, 0, length - kwin)
    k0 = pl.multiple_of(r * length + ks, N_SIDE)
    qpos = i * Q_BLK + lax.broadcasted_iota(jnp.int32, (Q_BLK, kwin), 0)
    kpos = ks + lax.broadcasted_iota(jnp.int32, (Q_BLK, kwin), 1)
    valid = jnp.abs(kpos - qpos) <= N_SIDE
    return q0, k0, kwin, valid


def _deinterleave(src_ref, dst_ref, d, dtype, tmp_ref):
    if d == 1:
        dst_ref[...] = src_ref[...].astype(dtype)
        return
    q = SEQ // 4
    if d == 4:
        for r in range(4):
            dst_ref[r * q:(r + 1) * q, :] = src_ref[pl.ds(r, q, stride=4), :].astype(dtype)
        return
    assert d == 16
    n = SEQ // 16
    for r in range(4):
        tmp_ref[r * q:(r + 1) * q, :] = src_ref[pl.ds(r, q, stride=4), :]
    for r in range(4):
        for j in range(4):
            dst_ref[(r + 4 * j) * n:(r + 4 * j + 1) * n, :] = tmp_ref[pl.ds(r * q + j, n, stride=4), :].astype(dtype)


def _class16_to_class4(src_ref, dst_ref):
    q, n = SEQ // 4, SEQ // 16
    for r in range(4):
        for j in range(4):
            dst_ref[pl.ds(r * q + j, n, stride=4), :] = src_ref[(r + 4 * j) * n:(r + 4 * j + 1) * n, :]


def _interleave(src_ref, dst_ref, d, tmp_ref, accumulate):
    q = SEQ // 4
    if d == 16:
        _class16_to_class4(src_ref, tmp_ref)
        src_ref = tmp_ref
    else:
        assert d == 4
    for r in range(4):
        rows = pl.ds(r, q, stride=4)
        val = src_ref[r * q:(r + 1) * q, :]
        dst_ref[rows, :] = dst_ref[rows, :] + val if accumulate else val


def a_attn_fwd(proj, ca, sa, sb, nb, xch):
    t = proj.shape[0]
    n_pairs = A_WIDTH // LANES

    def body(q_ref, k_ref, v_ref, c_ref, sa_ref, sb_ref, y_ref, lse_ref, *rest):
        qkv_d, (qr_s, kr_s, oc_s, lc_s, o1_s, l1_s, o2_s, l2_s, o3_s, l3_s, tmp_s) = rest[:9], rest[9:]
        c, s_a, s_b = c_ref[...], sa_ref[...], sb_ref[...]
        qr_s[...] = _rope_fwd(q_ref[...], c, s_a, s_b, A_ROT // 2) * (A_HEAD_DIM ** -0.5)
        kr_s[...] = _rope_fwd(k_ref[...], c, s_a, s_b, A_ROT // 2)
        head0 = lax.broadcasted_iota(jnp.int32, (Q_BLK, LANES), 1) < A_HEAD_DIM
        nat = ((o1_s, l1_s), (o2_s, l2_s), (o3_s, l3_s))

        for g, d in enumerate(DILATIONS):
            qd_s, kd_s, vd_s = qkv_d[3 * g:3 * g + 3]
            _deinterleave(qr_s, qd_s, d, BF16, tmp_s)
            _deinterleave(kr_s, kd_s, d, BF16, tmp_s)
            _deinterleave(v_ref, vd_s, d, BF16, tmp_s)
            o_dst, l_dst = (nat[g] if d == 1 else (oc_s, lc_s))

            def block(idx, carry, d=d, o_dst=o_dst, l_dst=l_dst, qd_s=qd_s, kd_s=kd_s, vd_s=vd_s):
                q0, k0, kwin, valid = _block_coords(idx, d)
                qb = qd_s[pl.ds(q0, Q_BLK), :]
                kb = kd_s[pl.ds(k0, kwin), :]
                vb = vd_s[pl.ds(k0, kwin), :]
                zero = jnp.zeros_like(qb)
                q2 = jnp.concatenate([jnp.where(head0, qb, zero), jnp.where(head0, zero, qb)], 0)
                s = jnp.where(jnp.concatenate([valid, valid], 0), _dot_nt(q2, kb), NEG_INF)
                m = jnp.max(s, axis=-1, keepdims=True)
                p = jnp.exp(s - m)
                l = jnp.sum(p, axis=-1, keepdims=True)
                o2 = _dot(p.astype(BF16), vb) / l
                l2 = m + jnp.log(l)
                o_dst[pl.ds(q0, Q_BLK), :] = jnp.where(head0, o2[:Q_BLK], o2[Q_BLK:])
                l_dst[pl.ds(q0, Q_BLK), :] = jnp.where(head0, l2[:Q_BLK], l2[Q_BLK:])
                return carry

            lax.fori_loop(0, SEQ // Q_BLK, block, 0, unroll=UNROLL_FWD)
            if d > 1:
                _interleave(oc_s, nat[g][0], d, tmp_s, False)
                _interleave(lc_s, nat[g][1], d, tmp_s, False)

        def merge(ci, carry):
            rows = pl.ds(pl.multiple_of(ci * 256, 256), 256)
            l1, l2, l3 = l1_s[rows, :], l2_s[rows, :], l3_s[rows, :]
            m = jnp.maximum(jnp.maximum(l1, l2), l3)
            w1, w2, w3 = jnp.exp(l1 - m), jnp.exp(l2 - m), jnp.exp(l3 - m)
            w = w1 + w2 + w3
            y_ref[rows, :] = (w1 * o1_s[rows, :] + w2 * o2_s[rows, :] + w3 * o3_s[rows, :]) / w
            lse_ref[rows, :] = m + jnp.log(w)
            return carry

        lax.fori_loop(0, SEQ // 256, merge, 0)

    def col(off):
        return pl.BlockSpec((SEQ, LANES), lambda b, hp: (b, off + hp))

    tab = pl.BlockSpec((SEQ, LANES), lambda b, hp: (b, 0))
    out = pl.BlockSpec((SEQ, LANES), lambda b, hp: (b, hp))
    f32s = pltpu.VMEM((SEQ, LANES), F32)
    res, landed = call_hosting_exchange(
        body, xch, grid=(nb, n_pairs),
        in_specs=[col(0), col(n_pairs), col(2 * n_pairs), tab, tab, tab],
        out_specs=[out] * 11,
        out_shape=[SDS((t, A_WIDTH), F32)] * 2 + [SDS((t, A_WIDTH), BF16)] * 9,
        scratch_shapes=[f32s] * 11,
        name="a_attn_fwd", operands=(proj, proj, proj, ca, sa, sb))
    return res[:2], res[2:], landed


def a_attn_bwd(qkv_d, ca, sa, sb, dy, y, lse, nb, xch):
    t = dy.shape[0]
    n_pairs = A_WIDTH // LANES

    def body(*refs):
        qkv_refs = refs[:9]
        (c_ref, sa_ref, sb_ref, do_ref, y_ref, lse_ref, dq_ref, dk_ref, dv_ref,
         l0n_s, l1n_s, d0n_s, d1n_s, dod_s, l0d_s, l1d_s, d0d_s, d1d_s,
         dqc_s, dkc_s, dvc_s, dq4_s, dk4_s, dv4_s, dqn_s, dkn_s, dvn_s, tmp_s) = refs[9:]
        c, s_a, s_b = c_ref[...], sa_ref[...], sb_ref[...]
        head0 = lax.broadcasted_iota(jnp.int32, (Q_BLK, LANES), 1) < A_HEAD_DIM

        def per_head_rows(ci, carry):
            rows = pl.ds(pl.multiple_of(ci * 256, 256), 256)
            h0 = lax.broadcasted_iota(jnp.int32, (256, LANES), 1) < A_HEAD_DIM
            tt = do_ref[rows, :] * y_ref[rows, :]
            d0n_s[rows, :] = jnp.broadcast_to(jnp.sum(jnp.where(h0, tt, 0.0), axis=-1, keepdims=True), (256, LANES))
            d1n_s[rows, :] = jnp.broadcast_to(jnp.sum(jnp.where(h0, 0.0, tt), axis=-1, keepdims=True), (256, LANES))
            l = lse_ref[rows, :]
            lr = pltpu.roll(l, A_HEAD_DIM, 1)
            l0n_s[rows, :] = jnp.where(h0, l, lr)
            l1n_s[rows, :] = jnp.where(h0, lr, l)
            return carry

        lax.fori_loop(0, SEQ // 256, per_head_rows, 0)
        assert DILATIONS == (1, 4, 16)

        for g, d in enumerate(DILATIONS):
            qd_s, kd_s, vd_s = qkv_refs[3 * g:3 * g + 3]
            _deinterleave(do_ref, dod_s, d, BF16, tmp_s)
            if d > 1:
                for src, dst in ((l0n_s, l0d_s), (l1n_s, l1d_s), (d0n_s, d0d_s), (d1n_s, d1d_s)):
                    _deinterleave(src, dst, d, F32, tmp_s)
            l0, l1, d0, d1 = (l0n_s, l1n_s, d0n_s, d1n_s) if d == 1 else (l0d_s, l1d_s, d0d_s, d1d_s)
            dq_dst, dk_dst, dv_dst = {1: (dqn_s, dkn_s, dvn_s), 4: (dq4_s, dk4_s, dv4_s), 16: (dqc_s, dkc_s, dvc_s)}[d]
            dk_dst[...] = jnp.zeros_like(dk_dst)
            dv_dst[...] = jnp.zeros_like(dv_dst)

            def block(idx, carry, d=d, l0=l0, l1=l1, d0=d0, d1=d1, dq_dst=dq_dst, dk_dst=dk_dst, dv_dst=dv_dst,
                      qd_s=qd_s, kd_s=kd_s, vd_s=vd_s):
                q0, k0, kwin, valid = _block_coords(idx, d)
                qrows = pl.ds(q0, Q_BLK)
                krows = pl.ds(k0, kwin)
                qb, dob = qd_s[qrows, :], dod_s[qrows, :]
                kb, vb = kd_s[krows, :], vd_s[krows, :]
                zero = jnp.zeros_like(qb)
                q2 = jnp.concatenate([jnp.where(head0, qb, zero), jnp.where(head0, zero, qb)], 0)
                do2 = jnp.concatenate([jnp.where(head0, dob, zero), jnp.where(head0, zero, dob)], 0)
                wide = lambda x: jnp.concatenate([x] * (kwin // LANES), 1)
                lse2 = wide(jnp.concatenate([l0[qrows, :], l1[qrows, :]], 0))
                dd2 = wide(jnp.concatenate([d0[qrows, :], d1[qrows, :]], 0))
                s = jnp.where(jnp.concatenate([valid, valid], 0), _dot_nt(q2, kb), NEG_INF)
                p = jnp.exp(s - lse2)
                ds = (p * (_dot_nt(do2, vb) - dd2)).astype(BF16)
                dq2 = _dot(ds, kb)
                dq_dst[qrows, :] = jnp.where(head0, dq2[:Q_BLK], dq2[Q_BLK:])
                dk_dst[krows, :] += _dot_tn(ds, q2)
                dv_dst[krows, :] += _dot_tn(p.astype(BF16), do2)
                return carry

            lax.fori_loop(0, SEQ // Q_BLK, block, 0, unroll=UNROLL_BWD)

        for c16, c4, nat in ((dqc_s, dq4_s, dqn_s), (dkc_s, dk4_s, dkn_s), (dvc_s, dv4_s, dvn_s)):
            _class16_to_class4(c16, tmp_s)
            c4[...] = c4[...] + tmp_s[...]
            _interleave(c4, nat, 4, tmp_s, True)

        dq_ref[...] = _rope_bwd(dqn_s[...] * (A_HEAD_DIM ** -0.5), c, s_a, s_b, A_ROT // 2).astype(BF16)
        dk_ref[...] = _rope_bwd(dkn_s[...], c, s_a, s_b, A_ROT // 2).astype(BF16)
        dv_ref[...] = dvn_s[...].astype(BF16)

    tab = pl.BlockSpec((SEQ, LANES), lambda b, hp: (b, 0))
    blk = pl.BlockSpec((SEQ, LANES), lambda b, hp: (b, hp))
    f32s = pltpu.VMEM((SEQ, LANES), F32)
    b16s = pltpu.VMEM((SEQ, LANES), BF16)
    return call_hosting_exchange(
        body, xch, grid=(nb, n_pairs),
        in_specs=[blk] * 9 + [tab, tab, tab, blk, blk, blk],
        out_specs=[blk, blk, blk],
        out_shape=[SDS((t, A_WIDTH), BF16)] * 3,
        scratch_shapes=[f32s] * 4 + [b16s] + [f32s] * 14,
        name="a_attn_bwd", operands=(*qkv_d, ca, sa, sb, dy, y, lse))


MLA_SCALE = (MLA_NOPE + MLA_ROPE) ** -0.5
LOG2E = 1.4426950408889634
MLA_QW = MLA_HEADS * LANES
MLA_KVW = MLA_QW + MLA_WIDTH


def _rms(x, g):
    r = lax.rsqrt(jnp.mean(x * x, axis=-1, keepdims=True) + NORM_EPS)
    return x * r * g, r


def _rms_bwd(dn, x, r, g):
    tg = dn * g
    dx = r * tg - x * (r * r * r) * jnp.mean(tg * x, axis=-1, keepdims=True)
    return dx, jnp.sum(dn * x * r, axis=0, keepdims=True)


def mla_prep_fwd(proj, cm, sma, smb, g_cq, g_ckv, wuq, wkv):
    t = proj.shape[0]
    tm = 1024

    def body(cq_ref, ckv_ref, kr_ref, c_ref, sa_ref, sb_ref, gq_ref, gkv_ref, wuq_ref, wkv_ref, q_ref, k_ref, v_ref):
        c, s_a, s_b = c_ref[...], sa_ref[...], sb_ref[...]
        cqn, _ = _rms(cq_ref[...], gq_ref[...])
        qf = _dot(cqn.astype(BF16), wuq_ref[...])
        ckvn, _ = _rms(ckv_ref[...], gkv_ref[...])
        kvf = _dot(ckvn.astype(BF16), wkv_ref[...])
        krope = _rope_fwd(kr_ref[...], c, s_a, s_b, MLA_ROPE // 2)
        for h in range(MLA_HEADS):
            cols = slice(h * LANES, (h + 1) * LANES)
            q_ref[:, cols] = (_rope_fwd(qf[:, cols], c, s_a, s_b, MLA_ROPE // 2) * (MLA_SCALE * LOG2E)).astype(BF16)
            k_ref[:, cols] = (kvf[:, cols] + krope).astype(BF16)
        v_ref[...] = kvf[:, MLA_QW:].astype(BF16)

    def row(w, j):
        return pl.BlockSpec((tm, w), lambda i: (i, j))

    def full(a):
        return pl.BlockSpec(a.shape, lambda i: (0, 0))

    return pl.pallas_call(
        body, grid=(t // tm,),
        in_specs=[row(256, 4096 // 256), row(128, 4352 // 128), row(128, 4480 // 128), row(128, 0), row(128, 0), row(128, 0),
                  full(g_cq), full(g_ckv), full(wuq), full(wkv)],
        out_specs=[row(MLA_QW, 0), row(MLA_QW, 0), row(MLA_WIDTH, 0)],
        out_shape=[SDS((t, MLA_QW), BF16), SDS((t, MLA_QW), BF16), SDS((t, MLA_WIDTH), BF16)],
        name="mla_prep_fwd", compiler_params=_params("parallel"))(proj, proj, proj, cm, sma, smb, g_cq, g_ckv, wuq, wkv)


def mla_prep_bwd(proj, cm, sma, smb, g_cq, g_ckv, wuq, wkv, dq, dk, dv):
    t = proj.shape[0]
    tm = 1024

    def body(cq_ref, ckv_ref, c_ref, sa_ref, sb_ref, gq_ref, gkv_ref, wuq_ref, wkv_ref, dq_ref, dk_ref, dv_ref,
             dcc_ref, dqf_ref, cqn_ref, dkvf_ref, ckvn_ref, dgq_ref, dgkv_ref):
        @pl.when(pl.program_id(0) == 0)
        def _():
            dgq_ref[...] = jnp.zeros_like(dgq_ref)
            dgkv_ref[...] = jnp.zeros_like(dgkv_ref)

        c, s_a, s_b = c_ref[...], sa_ref[...], sb_ref[...]
        cq, ckv = cq_ref[...], ckv_ref[...]
        cqn, rq = _rms(cq, gq_ref[...])
        ckvn, rkv = _rms(ckv, gkv_ref[...])
        cqn_ref[...] = cqn.astype(BF16)
        ckvn_ref[...] = ckvn.astype(BF16)
        lane = lax.broadcasted_iota(jnp.int32, (tm, LANES), 1)
        rope_lanes = (lane >= MLA_NOPE) & (lane < MLA_NOPE + MLA_ROPE)
        dkrope = jnp.zeros((tm, LANES), F32)
        for h in range(MLA_HEADS):
            cols = slice(h * LANES, (h + 1) * LANES)
            dqf_ref[:, cols] = _rope_bwd(dq_ref[:, cols].astype(F32) * MLA_SCALE, c, s_a, s_b, MLA_ROPE // 2).astype(BF16)
            dkh = dk_ref[:, cols].astype(F32) * (1.0 / LOG2E)
            dkvf_ref[:, cols] = dkh.astype(BF16)
            dkrope = dkrope + dkh
        dkvf_ref[:, MLA_QW:] = dv_ref[...].astype(BF16)
        dkr = _rope_bwd(jnp.where(rope_lanes, dkrope, 0.0), c, s_a, s_b, MLA_ROPE // 2)
        dcqn = _dot_nt(dqf_ref[...], wuq_ref[...])
        dckvn = _dot_nt(dkvf_ref[...], wkv_ref[...])
        dcq, dgq = _rms_bwd(dcqn, cq, rq, gq_ref[...])
        dckv, dgkv = _rms_bwd(dckvn, ckv, rkv, gkv_ref[...])
        dgq_ref[...] += dgq
        dgkv_ref[...] += dgkv
        dcc_ref[:, 0:256] = dcq.astype(BF16)
        dcc_ref[:, 256:384] = dckv.astype(BF16)
        dcc_ref[:, 384:512] = dkr.astype(BF16)

    def row(w, j):
        return pl.BlockSpec((tm, w), lambda i: (i, j))

    def full(a):
        return pl.BlockSpec(a.shape, lambda i: (0, 0))

    return pl.pallas_call(
        body, grid=(t // tm,),
        in_specs=[row(256, 4096 // 256), row(128, 4352 // 128), row(128, 0), row(128, 0), row(128, 0),
                  full(g_cq), full(g_ckv), full(wuq), full(wkv), row(MLA_QW, 0), row(MLA_QW, 0), row(MLA_WIDTH, 0)],
        out_specs=[row(512, 0), row(MLA_QW, 0), row(256, 0), row(MLA_KVW, 0), row(128, 0), full(g_cq), full(g_ckv)],
        out_shape=[SDS((t, 512), BF16), SDS((t, MLA_QW), BF16), SDS((t, 256), BF16), SDS((t, MLA_KVW), BF16),
                   SDS((t, 128), BF16), SDS(g_cq.shape, F32), SDS(g_ckv.shape, F32)],
        name="mla_prep_bwd", compiler_params=_params("arbitrary"))(proj, proj, cm, sma, smb, g_cq, g_ckv, wuq, wkv, dq, dk, dv)


MLA_TQ = SEQ
MLA_SUB_FWD = 512
MLA_SUB_BWD = 256


def mla_attn_fwd(qb, kb, vb, nb):
    t = qb.shape[0]
    nq = SEQ // MLA_TQ
    n_pairs = MLA_HEADS // 2

    def body(q_ref, k_ref, v_ref, y_ref, lse_ref):
        head0 = lax.broadcasted_iota(jnp.int32, (MLA_SUB_FWD, LANES), 1) < MLA_V
        v = v_ref[...]
        vhead0 = lax.broadcasted_iota(jnp.int32, v.shape, 1) < MLA_V
        one = jnp.ones_like(v)
        vh = [jnp.where(vhead0 == (h == 0), v, one) for h in range(2)]
        for sub in range(MLA_TQ // MLA_SUB_FWD):
            rows = slice(sub * MLA_SUB_FWD, (sub + 1) * MLA_SUB_FWD)
            outs, lses = [], []
            for h in range(2):
                cols = slice(h * LANES, (h + 1) * LANES)
                s = _dot_nt(q_ref[rows, cols], k_ref[:, cols])
                m = jnp.max(s, axis=-1, keepdims=True)
                p = jnp.exp2(s - m).astype(BF16)
                ol = _dot(p, vh[h])
                l = pltpu.roll(ol, MLA_V, 1)
                outs.append(ol / l)
                lses.append(m + jnp.log2(l))
            y_ref[rows, :] = jnp.where(head0, outs[0], outs[1])
            lse_ref[rows, :] = jnp.where(head0, lses[0], lses[1])

    return pl.pallas_call(
        body, grid=(nb, n_pairs, nq),
        in_specs=[pl.BlockSpec((MLA_TQ, 2 * LANES), lambda b, hp, i: (b * nq + i, hp)),
                  pl.BlockSpec((SEQ, 2 * LANES), lambda b, hp, i: (b, hp)),
                  pl.BlockSpec((SEQ, LANES), lambda b, hp, i: (b, hp))],
        out_specs=[pl.BlockSpec((MLA_TQ, LANES), lambda b, hp, i: (b * nq + i, hp))] * 2,
        out_shape=[SDS((t, MLA_WIDTH), F32)] * 2,
        name="mla_attn_fwd", compiler_params=_params("parallel", "parallel", "parallel"))(qb, kb, vb)


def mla_attn_bwd(qb, kb, vb, dy, y, lse, nb, xch):
    t = qb.shape[0]
    nq = SEQ // MLA_TQ
    n_pairs = MLA_HEADS // 2

    assert nq == 1

    def body(q_ref, k_ref, v_ref, do_ref, y_ref, lse_ref, dq_ref, dk_ref, dv_ref, dk_s, dv_s):
        dk_s[...] = jnp.zeros_like(dk_s)
        dv_s[...] = jnp.zeros_like(dv_s)
        head0 = lax.broadcasted_iota(jnp.int32, (MLA_SUB_BWD, LANES), 1) < MLA_V
        v = v_ref[...]
        for sub in range(MLA_TQ // MLA_SUB_BWD):
            rows = slice(sub * MLA_SUB_BWD, (sub + 1) * MLA_SUB_BWD)
            do = do_ref[rows, :]
            lse = lse_ref[rows, :]
            tt = do * y_ref[rows, :]
            dv = jnp.zeros((SEQ, LANES), F32)
            for h in range(2):
                sel = head0 if h == 0 else ~head0
                lo = h * MLA_V
                cols = slice(h * LANES, (h + 1) * LANES)
                q = q_ref[rows, cols]
                k = k_ref[:, cols]
                dd = jnp.sum(jnp.where(sel, tt, 0.0), axis=-1, keepdims=True)
                doh = jnp.where(sel, do, 0.0).astype(BF16)
                p = jnp.exp2(_dot_nt(q, k) - lse[:, lo:lo + 1])
                dp = _dot_nt(doh, v)
                ds = (p * (dp - dd)).astype(BF16)
                dq_ref[rows, cols] = _dot(ds, k).astype(dq_ref.dtype)
                dk_s[:, cols] += _dot_tn(ds, q)
                dv = dv + _dot_tn(p.astype(BF16), doh)
            dv_s[...] += dv
        dk_ref[...] = dk_s[...].astype(dk_ref.dtype)
        dv_ref[...] = dv_s[...].astype(dv_ref.dtype)

    qspec = pl.BlockSpec((MLA_TQ, 2 * LANES), lambda b, hp, i: (b * nq + i, hp))
    kspec = pl.BlockSpec((SEQ, 2 * LANES), lambda b, hp, i: (b, hp))
    vspec = pl.BlockSpec((SEQ, LANES), lambda b, hp, i: (b, hp))
    ospec = pl.BlockSpec((MLA_TQ, LANES), lambda b, hp, i: (b * nq + i, hp))
    return call_hosting_exchange(
        body, xch, grid=(nb, n_pairs, nq),
        in_specs=[qspec, kspec, vspec, ospec, ospec, ospec],
        out_specs=[qspec, kspec, vspec],
        out_shape=[SDS((t, MLA_QW), BF16), SDS((t, MLA_QW), BF16), SDS((t, MLA_WIDTH), BF16)],
        scratch_shapes=[pltpu.VMEM((SEQ, 2 * LANES), F32), pltpu.VMEM((SEQ, LANES), F32)],
        name="mla_attn_bwd", operands=(qb, kb, vb, dy, y, lse))


MEM_TQ = SEQ
MEM_SUB = SEQ
MEM_SCALE = MEM_HEAD_DIM ** -0.5
MQ_BLK4 = 5120 // MEM_WIDTH


def mem_attn_fwd(proj, mkv, nb):
    t = proj.shape[0]
    nq = SEQ // MEM_TQ

    def body(q_ref, mk_ref, mv_ref, y_ref):
        for sub in range(MEM_TQ // MEM_SUB):
            rows = slice(sub * MEM_SUB, (sub + 1) * MEM_SUB)
            for h in range(MEM_HEADS):
                cols = slice(h * LANES, (h + 1) * LANES)
                s = _dot_nt(q_ref[rows, cols].astype(BF16), mk_ref[:, cols]) * MEM_SCALE
                m = jnp.max(s, axis=-1, keepdims=True)
                p = jnp.exp(s - m)
                l = jnp.sum(p, axis=-1, keepdims=True)
                y_ref[rows, cols] = _dot(p.astype(BF16), mv_ref[:, cols]) / l

    return pl.pallas_call(
        body, grid=(nb, nq),
        in_specs=[pl.BlockSpec((MEM_TQ, MEM_WIDTH), lambda b, i: (b * nq + i, MQ_BLK4)),
                  pl.BlockSpec((N_MEM, MEM_WIDTH), lambda b, i: (b, 0)),
                  pl.BlockSpec((N_MEM, MEM_WIDTH), lambda b, i: (b, 1))],
        out_specs=pl.BlockSpec((MEM_TQ, MEM_WIDTH), lambda b, i: (b * nq + i, 0)),
        out_shape=SDS((t, MEM_WIDTH), F32),
        name="mem_attn_fwd", compiler_params=_params("parallel", "parallel"))(proj, mkv, mkv)


def mem_attn_bwd(proj, mkv, dy, nb):
    t = proj.shape[0]
    nq = SEQ // MEM_TQ

    def body(q_ref, mk_ref, mv_ref, do_ref, dq_ref, dmk_ref, dmv_ref):
        @pl.when(pl.program_id(1) == 0)
        def _():
            dmk_ref[...] = jnp.zeros_like(dmk_ref)
            dmv_ref[...] = jnp.zeros_like(dmv_ref)

        for sub in range(MEM_TQ // MEM_SUB):
            rows = slice(sub * MEM_SUB, (sub + 1) * MEM_SUB)
            for h in range(MEM_HEADS):
                cols = slice(h * LANES, (h + 1) * LANES)
                q = q_ref[rows, cols].astype(BF16)
                mk, mv = mk_ref[:, cols], mv_ref[:, cols]
                do = do_ref[rows, cols].astype(BF16)
                s = _dot_nt(q, mk) * MEM_SCALE
                e = jnp.exp(s - jnp.max(s, axis=-1, keepdims=True))
                p = e / jnp.sum(e, axis=-1, keepdims=True)
                dp = _dot_nt(do, mv)
                ds = (p * (dp - jnp.sum(p * dp, axis=-1, keepdims=True)) * MEM_SCALE).astype(BF16)
                dq_ref[rows, cols] = _dot(ds, mk).astype(BF16)
                dmk_ref[:, cols] += _dot_tn(ds, q)
                dmv_ref[:, cols] += _dot_tn(p.astype(BF16), do)

    ospec = pl.BlockSpec((MEM_TQ, MEM_WIDTH), lambda b, i: (b * nq + i, 0))
    kspec = pl.BlockSpec((N_MEM, MEM_WIDTH), lambda b, i: (b, 0))
    return pl.pallas_call(
        body, grid=(nb, nq),
        in_specs=[pl.BlockSpec((MEM_TQ, MEM_WIDTH), lambda b, i: (b * nq + i, MQ_BLK4)),
                  kspec, pl.BlockSpec((N_MEM, MEM_WIDTH), lambda b, i: (b, 1)), ospec],
        out_specs=[ospec, kspec, kspec],
        out_shape=[SDS((t, MEM_WIDTH), BF16), SDS((nb * N_MEM, MEM_WIDTH), F32), SDS((nb * N_MEM, MEM_WIDTH), F32)],
        name="mem_attn_bwd", compiler_params=_params("parallel", "arbitrary"))(proj, mkv, mkv, dy)


ROW_TM = 512
AG_BLK = 3072 // 1024
BG_BLK = 4608 // 512
MG_BLK = 5632 // 512
GROUPS = ((0, A_WIDTH), (A_WIDTH, MLA_WIDTH), (A_WIDTH + MLA_WIDTH, MEM_WIDTH))
D_MIX = 2048


def _gate_specs():
    def row(w, j):
        return pl.BlockSpec((ROW_TM, w), lambda i: (i, j))

    def vec(w):
        return pl.BlockSpec((1, w), lambda i: (0, 0))

    ys = [row(A_WIDTH, 0), row(MLA_WIDTH, 0), row(MEM_WIDTH, 0)]
    gates = [row(A_WIDTH, AG_BLK), row(MLA_WIDTH, BG_BLK), row(MEM_WIDTH, MG_BLK)]
    gains = [vec(A_WIDTH), vec(MLA_WIDTH), vec(MEM_WIDTH)]
    return row, vec, ys, gates, gains


def gate_out_ln_loss(ya, yb, ym, proj, goa, gob, gom, wout, h32, target, gp, bp):
    t, d = h32.shape
    _, _, ys, gates, gains = _gate_specs()

    def body(ya_ref, yb_ref, ym_ref, ga_ref, gb_ref, gm_ref, goa_ref, gob_ref, gom_ref, w_ref, h_ref, t_ref, gp_ref, bp_ref,
             z_ref, du32_ref, du16_ref, loss_ref, dgp_ref, dbp_ref):
        @pl.when(pl.program_id(0) == 0)
        def _():
            loss_ref[...] = jnp.zeros_like(loss_ref)
            dgp_ref[...] = jnp.zeros_like(dgp_ref)
            dbp_ref[...] = jnp.zeros_like(dbp_ref)

        for (off, w), y_ref, g_ref, go_ref in zip(GROUPS, (ya_ref, yb_ref, ym_ref), (ga_ref, gb_ref, gm_ref),
                                                  (goa_ref, gob_ref, gom_ref)):
            n, _ = _rms(y_ref[...], go_ref[...])
            gt = g_ref[...]
            z_ref[:, off:off + w] = (n * (gt * _sigmoid(gt))).astype(BF16)
        g = gp_ref[...]
        u = ALPHA * h_ref[...] + _dot(z_ref[...], w_ref[...])
        mu = jnp.mean(u, axis=-1, keepdims=True)
        uc = u - mu
        rstd = lax.rsqrt(jnp.mean(uc * uc, axis=-1, keepdims=True) + NORM_EPS)
        xhat = uc * rstd
        err = xhat * g + bp_ref[...] - t_ref[...]
        tok = jnp.sum(err * err, axis=-1, keepdims=True) * (1.0 / d)
        loss_ref[...] += 0.5 * jnp.sum(tok, axis=0, keepdims=True)
        dout = err * (1.0 / d)
        dxhat = dout * g
        du = rstd * (dxhat - jnp.mean(dxhat, axis=-1, keepdims=True)
                     - xhat * jnp.mean(dxhat * xhat, axis=-1, keepdims=True))
        du32_ref[...] = du
        du16_ref[...] = du.astype(BF16)
        dgp_ref[...] += jnp.sum(dout * xhat, axis=0, keepdims=True)
        dbp_ref[...] += jnp.sum(dout, axis=0, keepdims=True)

    row = pl.BlockSpec((ROW_TM, d), lambda i: (i, 0))
    vec = pl.BlockSpec((1, d), lambda i: (0, 0))
    zrow = pl.BlockSpec((ROW_TM, D_MIX), lambda i: (i, 0))
    return pl.pallas_call(
        body, grid=(t // ROW_TM,),
        in_specs=ys + gates + gains + [pl.BlockSpec((D_MIX, d), lambda i: (0, 0)), row, row, vec, vec],
        out_specs=[zrow, row, row, pl.BlockSpec((1, LANES), lambda i: (0, 0)), vec, vec],
        out_shape=[SDS((t, D_MIX), BF16), SDS((t, d), F32), SDS((t, d), BF16), SDS((1, LANES), F32), SDS((1, d), F32),
                   SDS((1, d), F32)],
        name="gate_out_ln_loss", compiler_params=_params("arbitrary"))(
            ya, yb, ym, proj, proj, proj, goa, gob, gom, wout, h32, target, gp, bp)


def gate_bwd(du16, wout, ya, yb, ym, proj, goa, gob, gom):
    t = ya.shape[0]
    row, vec, ys, gates, gains = _gate_specs()

    def body(du_ref, w_ref, ya_ref, yb_ref, ym_ref, ga_ref, gb_ref, gm_ref, goa_ref, gob_ref, gom_ref,
             dya_ref, dyb_ref, dym_ref, dga_ref, dgb_ref, dgm_ref, dgoa_ref, dgob_ref, dgom_ref):
        @pl.when(pl.program_id(0) == 0)
        def _():
            dgoa_ref[...] = jnp.zeros_like(dgoa_ref)
            dgob_ref[...] = jnp.zeros_like(dgob_ref)
            dgom_ref[...] = jnp.zeros_like(dgom_ref)

        dz = _dot_nt(du_ref[...], w_ref[...])
        for (off, w), y_ref, g_ref, go_ref, dy_ref, dg_ref, dgo_ref in zip(
                GROUPS, (ya_ref, yb_ref, ym_ref), (ga_ref, gb_ref, gm_ref), (goa_ref, gob_ref, gom_ref),
                (dya_ref, dyb_ref, dym_ref), (dga_ref, dgb_ref, dgm_ref), (dgoa_ref, dgob_ref, dgom_ref)):
            dzg = dz[:, off:off + w]
            y, gt, go = y_ref[...], g_ref[...], go_ref[...]
            n, r = _rms(y, go)
            sg = _sigmoid(gt)
            dg_ref[...] = (dzg * n * (sg * (1.0 + gt * (1.0 - sg)))).astype(BF16)
            dy, dgo = _rms_bwd(dzg * (gt * sg), y, r, go)
            dy_ref[...] = dy
            dgo_ref[...] += dgo

    widths = (A_WIDTH, MLA_WIDTH, MEM_WIDTH)
    return pl.pallas_call(
        body, grid=(t // ROW_TM,),
        in_specs=[row(D_MODEL, 0), pl.BlockSpec((D_MIX, D_MODEL), lambda i: (0, 0))] + ys + gates + gains,
        out_specs=[row(w, 0) for w in widths] * 2 + [vec(w) for w in widths],
        out_shape=[SDS((t, w), F32) for w in widths] + [SDS((t, w), BF16) for w in widths] + [SDS((1, w), F32) for w in widths],
        name="gate_bwd", compiler_params=_params("arbitrary"))(du16, wout, ya, yb, ym, proj, proj, proj, goa, gob, gom)


def dh_ln_bwd(pieces, win_t, du32, x2, g_emb, xch):
    t, d = x2.shape

    def body(*refs):
        p_refs = refs[:len(pieces)]
        w_ref, du_ref, x_ref, g_ref, dx_ref, dg_ref, db_ref = refs[len(pieces):]

        @pl.when(pl.program_id(0) == 0)
        def _():
            dg_ref[...] = jnp.zeros_like(dg_ref)
            db_ref[...] = jnp.zeros_like(db_ref)

        dh = ALPHA * du_ref[...]
        for p_ref, off, w in zip(p_refs, PIECE_OFFS, PIECE_WIDTHS):
            dh = dh + _dot(p_ref[...], w_ref[off:off + w, :])
        x = x_ref[...]
        xc = x - jnp.mean(x, axis=-1, keepdims=True)
        rstd = lax.rsqrt(jnp.mean(xc * xc, axis=-1, keepdims=True) + NORM_EPS)
        xhat = xc * rstd
        dg_ref[...] += jnp.sum(dh * xhat, axis=0, keepdims=True)
        db_ref[...] += jnp.sum(dh, axis=0, keepdims=True)
        tg = dh * g_ref[...]
        dx_ref[...] = rstd * (tg - jnp.mean(tg, axis=-1, keepdims=True)
                              - xhat * jnp.mean(tg * xhat, axis=-1, keepdims=True))

    row = pl.BlockSpec((ROW_TM, d), lambda i: (i, 0))
    vec = pl.BlockSpec((1, d), lambda i: (0, 0))
    return call_hosting_exchange(
        body, xch, grid=(t // ROW_TM,),
        in_specs=[pl.BlockSpec((ROW_TM, w), lambda i: (i, 0)) for w in PIECE_WIDTHS]
        + [pl.BlockSpec(win_t.shape, lambda i: (0, 0)), row, row, vec],
        out_specs=[row, vec, vec],
        out_shape=[SDS((t, d), F32), SDS((1, d), F32), SDS((1, d), F32)],
        scratch_shapes=[], name="dh_ln_bwd", operands=(*pieces, win_t, du32, x2, g_emb))


def _adamw(w, g, m, v):
    m2 = ADAM_B1 * m + (1.0 - ADAM_B1) * g
    v2 = ADAM_B2 * v + (1.0 - ADAM_B2) * (g * g)
    m_hat = m2 / (1.0 - ADAM_B1 ** ADAM_STEP)
    v_hat = v2 / (1.0 - ADAM_B2 ** ADAM_STEP)
    return -ADAM_LR * (m_hat / (jnp.sqrt(v_hat) + ADAM_EPS) + ADAM_WD * w), m2, v2


def adamw_shard(w, parts, m, v, name):
    r, c = w.shape
    if r % 256 == 0 or r * c <= 256 * 1024:
        tr, tc = min(r, 256), c
    else:
        tr, tc = r, 256

    def body(w_ref, p_ref, m_ref, v_ref, g_ref, d_ref, nm_ref, nv_ref):
        g = p_ref[0].astype(F32)
        for k in range(1, N_DEV):
            g = g + p_ref[k].astype(F32)
        g_ref[...] = g
        d_ref[...], nm_ref[...], nv_ref[...] = _adamw(w_ref[...], g, m_ref[...], v_ref[...])

    blk = pl.BlockSpec((tr, tc), lambda i, j: (i, j))
    return pl.pallas_call(
        body, grid=(r // tr, c // tc),
        in_specs=[blk, pl.BlockSpec((N_DEV, tr, tc), lambda i, j: (0, i, j)), blk, blk],
        out_specs=[blk] * 4, out_shape=[SDS((r, c), F32)] * 4, name=name,
        compiler_params=_params("parallel", "parallel"))(w, parts, m, v)


def adamw_shards_whole(ws, parts, ms, vs, name):
    n = len(ws)

    def body(*refs):
        w_refs, p_refs, m_refs, v_refs = refs[:n], refs[n:2 * n], refs[2 * n:3 * n], refs[3 * n:4 * n]
        outs = refs[4 * n:]
        for i in range(n):
            g = p_refs[i][0].astype(F32)
            for k in range(1, N_DEV):
                g = g + p_refs[i][k].astype(F32)
            outs[4 * i][...] = g
            outs[4 * i + 1][...], outs[4 * i + 2][...], outs[4 * i + 3][...] = _adamw(
                w_refs[i][...], g, m_refs[i][...], v_refs[i][...])

    res = pl.pallas_call(
        body, out_shape=[SDS(w.shape, F32) for w in ws for _ in range(4)], name=name,
        compiler_params=_params())(*ws, *parts, *ms, *vs)
    return [res[4 * i:4 * i + 4] for i in range(n)]


def _place():
    return lax.axis_index("x"), lax.axis_index("y"), lax.axis_index("c")


def _flat(px, py, pc):
    return 4 * px + 2 * py + pc


def _peer(x, y, c, k):
    return (1 - x if k & 4 else x, 1 - y if k & 2 else y, 1 - c if k & 1 else c)


def cast_shards(shards):
    def body(*refs):
        n = len(refs) // 2
        for i_ref, o_ref in zip(refs[:n], refs[n:]):
            o_ref[...] = i_ref[...].astype(BF16)

    return pl.pallas_call(body, out_shape=[SDS(s.shape, BF16) for s in shards], name="cast_shards",
                          compiler_params=_params())(*shards)


def _two_level_gather_plan(src_refs, land_refs, send_sems, recv_sems, local_sems):
    n = len(src_refs)
    x, y, c = _place()
    me, sib = (x, y, c), (x, y, 1 - c)
    chips = [(1 - x, y), (x, 1 - y), (1 - x, 1 - y)]

    def copy(a, k, block, to, src=None):
        dst = land_refs[a].at[_flat(*block)]
        return pltpu.make_async_remote_copy(
            src_ref=dst if src is None else src, dst_ref=dst,
            send_sem=send_sems.at[a * N_DEV + k], recv_sem=recv_sems.at[a * N_DEV + k],
            device_id=to, device_id_type=MESH)

    mine = [pltpu.make_async_copy(src_refs[a], land_refs[a].at[_flat(*me)], local_sems.at[a]) for a in range(n)]
    first = []
    for a in range(n):
        first.append(copy(a, 0, me, sib, src=src_refs[a]))
        first += [copy(a, 1 + j, me, (*chip, c), src=src_refs[a]) for j, chip in enumerate(chips)]

    def start():
        for cp in mine + first:
            cp.start()

    def finish():
        passed = []
        for j, chip in enumerate(chips):
            for a in range(n):
                copy(a, 1 + j, (*chip, c), me).wait_recv()
                fwd = copy(a, 4 + j, (*chip, c), sib)
                fwd.start()
                passed.append(fwd)
        for a in range(n):
            copy(a, 0, sib, me).wait_recv()
            for j, chip in enumerate(chips):
                copy(a, 4 + j, (*chip, 1 - c), me).wait_recv()
        for cp in first + passed:
            cp.wait_send()
        for cp in mine:
            cp.wait()

    return start, finish


ALL_DEVICES = tuple(range(N_DEV))


def _exchange_plan(src_refs, land_refs, dests, send_sems, recv_sems, local_sems):
    x, y, c = _place()
    me = _flat(x, y, c)
    plan = []
    for a, (src, land, dl) in enumerate(zip(src_refs, land_refs, dests)):
        for li, j in enumerate(dl):
            to = ((j >> 2) & 1, (j >> 1) & 1, j & 1)
            block = src.at[li] if len(src.shape) == len(land.shape) else src

            def push(slot, a=a, block=block, land=land, j=j, to=to):
                return pltpu.make_async_remote_copy(
                    src_ref=block, dst_ref=land.at[slot], send_sem=send_sems.at[a * N_DEV + j],
                    recv_sem=recv_sems.at[a * N_DEV + slot], device_id=to, device_id_type=MESH)

            own = pltpu.make_async_copy(block, land.at[j], local_sems.at[a])
            plan.append((j, push(me), own, [push(s) for s in range(N_DEV) if s != j]))
    return me, plan


def _exchange_start(me, plan):
    for j, send, own, _ in plan:
        @pl.when(me != j)
        def _(send=send):
            send.start()

        @pl.when(me == j)
        def _(own=own):
            own.start()


def _exchange_wait(me, plan):
    for j, send, own, arrivals in plan:
        @pl.when(me != j)
        def _(send=send):
            send.wait_send()

        @pl.when(me == j)
        def _(own=own, arrivals=arrivals):
            own.wait()
            for arrival in arrivals:
                arrival.wait_recv()


def call_hosting_exchange(core, xch, *, grid, in_specs, out_specs, out_shape, scratch_shapes, name, operands):
    srcs, dests, landing = xch
    n, n_in, n_out, n_scr = len(srcs), len(in_specs), len(out_specs), len(scratch_shapes)

    def body(*refs):
        ins, src_refs = refs[:n_in], refs[n_in:n_in + n]
        outs = refs[n_in + 2 * n:n_in + 2 * n + n_out]
        land_refs = refs[n_in + 2 * n + n_out:n_in + 3 * n + n_out]
        scratch = refs[n_in + 3 * n + n_out:n_in + 3 * n + n_out + n_scr]
        sems = refs[n_in + 3 * n + n_out + n_scr:]
        first = functools.reduce(jnp.logical_and, [pl.program_id(i) == 0 for i in range(len(grid))])
        last = functools.reduce(jnp.logical_and, [pl.program_id(i) == grid[i] - 1 for i in range(len(grid))])
        if dests is None:
            start, finish = _two_level_gather_plan(src_refs, land_refs, *sems)
        else:
            me, plan = _exchange_plan(src_refs, land_refs, dests, *sems)
            start, finish = functools.partial(_exchange_start, me, plan), functools.partial(_exchange_wait, me, plan)
        pl.when(first)(start)
        core(*ins, *outs, *scratch)
        pl.when(last)(finish)

    hbm = pl.BlockSpec(memory_space=pl.ANY)
    res = pl.pallas_call(
        body, grid=grid,
        in_specs=list(in_specs) + [hbm] * (2 * n), out_specs=list(out_specs) + [hbm] * n,
        out_shape=list(out_shape) + [SDS(l.shape, l.dtype) for l in landing],
        scratch_shapes=list(scratch_shapes) + [pltpu.SemaphoreType.DMA((N_DEV * n,)), pltpu.SemaphoreType.DMA((N_DEV * n,)),
                                               pltpu.SemaphoreType.DMA((n,))],
        input_output_aliases={n_in + n + k: n_out + k for k in range(n)},
        name=name, compiler_params=_params(*(("arbitrary",) * len(grid))))(*operands, *srcs, *landing)
    return res[:n_out], res[n_out:]


SLOT_ROWS = 8


def small_allreduce_adamw(loss_sum, grads, ws, ms, vs):
    n = len(grads)
    rows = [g.shape[0] for g in grads]
    total = SLOT_ROWS * (n + 1)

    def body(*refs):
        loss_ref, g_refs, w_refs = refs[0], refs[1:1 + n], refs[1 + n:1 + 2 * n]
        m_refs, v_refs = refs[1 + 2 * n:1 + 3 * n], refs[1 + 3 * n:1 + 4 * n]
        outs = refs[1 + 4 * n:2 + 8 * n]
        vec, gath, tot, send_sems, recv_sems = refs[2 + 8 * n:]
        x, y, c = _place()
        me = _flat(x, y, c)
        vec[...] = jnp.zeros_like(vec)
        vec[0:1, :] = loss_ref[...]
        for i in range(n):
            vec[SLOT_ROWS * (i + 1):SLOT_ROWS * (i + 1) + rows[i], :] = g_refs[i][...]
        gath[me] = vec[...]
        copies = []
        for k in range(1, N_DEV):
            peer = _peer(x, y, c, k)
            copies.append(pltpu.make_async_remote_copy(
                src_ref=vec, dst_ref=gath.at[me], send_sem=send_sems.at[k - 1], recv_sem=recv_sems.at[k - 1],
                device_id=peer, device_id_type=MESH))
        for cp in copies:
            cp.start()
        for cp in copies:
            cp.wait_recv()
        for cp in copies:
            cp.wait_send()
        g = gath[0]
        for j in range(1, N_DEV):
            g = g + gath[j]
        tot[...] = g
        outs[0][...] = tot[0:1, :]
        for i in range(n):
            gi = tot[SLOT_ROWS * (i + 1):SLOT_ROWS * (i + 1) + rows[i], :]
            outs[1 + i][...] = gi
            outs[1 + n + i][...], outs[1 + 2 * n + i][...], outs[1 + 3 * n + i][...] = _adamw(
                w_refs[i][...], gi, m_refs[i][...], v_refs[i][...])

    shapes = [SDS(g.shape, F32) for g in grads]
    return pl.pallas_call(
        body, out_shape=[SDS((1, LANES), F32)] + shapes * 4,
        scratch_shapes=[pltpu.VMEM((total, LANES), F32), pltpu.VMEM((N_DEV, total, LANES), F32), pltpu.VMEM((total, LANES), F32),
                        pltpu.SemaphoreType.DMA((7,)), pltpu.SemaphoreType.DMA((7,))],
        name="small_allreduce_adamw", compiler_params=_params())(loss_sum, *grads, *ws, *ms, *vs)


def _rope_lane_patterns():
    inv = lambda r: ROPE_THETA ** (-(jnp.arange(0, r, 2, dtype=F32) / r))
    z = lambda n: jnp.zeros((n,), F32)
    o = lambda n: jnp.ones((n,), F32)
    half, rest = A_ROT // 2, A_HEAD_DIM - A_ROT
    ia, im = inv(A_ROT), inv(MLA_ROPE)
    mh, tail = MLA_ROPE // 2, LANES - MLA_NOPE - MLA_ROPE
    rows = [jnp.tile(jnp.concatenate([ia, ia, z(rest)]), 2),
            jnp.tile(jnp.concatenate([o(half), z(half + rest)]), 2),
            jnp.tile(jnp.concatenate([z(half), o(half), z(rest)]), 2),
            jnp.concatenate([z(MLA_NOPE), im, im, z(tail)]),
            jnp.concatenate([z(MLA_NOPE), o(mh), z(mh + tail)]),
            jnp.concatenate([z(MLA_NOPE + mh), o(mh), z(tail)]),
            z(LANES), z(LANES)]
    return jnp.stack(rows)


KR_LO, KR_HI = 4480, 4512
W_IN_SHARD = D_IN // N_DEV
BG_SPLIT = 6 * W_IN_SHARD - KR_HI


def w_in_working_t(g):
    pad_lo, pad_hi = MLA_NOPE, LANES - MLA_NOPE - MLA_ROPE
    spans = []
    for lo, hi, shift in ((0, KR_LO, 0), (KR_LO, KR_HI, pad_lo), (KR_HI, D_IN, pad_lo + pad_hi)):
        r = lo
        while r < hi:
            j = r // W_IN_SHARD
            n = min(hi, (j + 1) * W_IN_SHARD) - r
            spans.append((j, r - j * W_IN_SHARD, n, r + shift))
            r += n

    def body(g_ref, o_ref):
        o_ref[KR_LO:KR_LO + pad_lo, :] = jnp.zeros((pad_lo, D_MODEL), o_ref.dtype)
        o_ref[KR_HI + pad_lo:KR_HI + pad_lo + pad_hi, :] = jnp.zeros((pad_hi, D_MODEL), o_ref.dtype)
        for j, src, n, dst in spans:
            o_ref[dst:dst + n, :] = g_ref[j, src:src + n, :]

    return pl.pallas_call(body, out_shape=SDS((D_INW, D_MODEL), g.dtype), name="w_in_working_t", compiler_params=_params())(g)


def _w_in_shard_5(d_ag_tail, d_cc, d_bg_head):
    kr = MLA_Q_RANK + MLA_KV_RANK + MLA_NOPE
    rows = jnp.concatenate([d_ag_tail, d_cc[:MLA_Q_RANK + MLA_KV_RANK], d_cc[kr:kr + MLA_ROPE], d_bg_head], 0)
    return rows.reshape(1, W_IN_SHARD, D_MODEL).astype(BF16)


def _w_uq_working(g):
    w = jnp.pad(g.transpose(1, 0, 2), ((0, 0), (0, 0), (0, LANES - MLA_NOPE - MLA_ROPE)))
    return w.reshape(MLA_Q_RANK, MLA_QW)


def _w_uq_parts(dw):
    return dw.reshape(MLA_Q_RANK, MLA_HEADS, LANES)[:, :, :MLA_NOPE + MLA_ROPE].transpose(1, 0, 2)


def _w_ukv_working(g):
    wk = jnp.pad(g[:, :, :MLA_NOPE].transpose(1, 0, 2), ((0, 0), (0, 0), (0, LANES - MLA_NOPE)))
    wv = g[:, :, MLA_NOPE:].transpose(1, 0, 2)
    return jnp.concatenate([wk.reshape(MLA_KV_RANK, MLA_QW), wv.reshape(MLA_KV_RANK, MLA_WIDTH)], 1)


def _w_ukv_parts(dw):
    dk = dw[:, :MLA_QW].reshape(MLA_KV_RANK, MLA_HEADS, LANES)[:, :, :MLA_NOPE]
    dv = dw[:, MLA_QW:].reshape(MLA_KV_RANK, MLA_HEADS, MLA_V)
    return jnp.concatenate([dk, dv], -1).transpose(1, 0, 2)


SMALL_NAMES = ("g_emb", "b_emb", "g_cq", "g_ckv", "g_out_a", "g_out_b", "g_out_m", "g_post", "b_post")


def kernel(x, mem, positions, g_emb, b_emb, w_in, g_cq, g_ckv, w_uq, w_ukv, w_mem_kv, g_out_a, g_out_b, g_out_m, w_out, g_post, b_post, loss_target, m_g_emb, m_b_emb, m_w_in, m_g_cq, m_g_ckv, m_w_uq, m_w_ukv, m_w_mem_kv, m_g_out_a, m_g_out_b, m_g_out_m, m_w_out, m_g_post, m_b_post, v_g_emb, v_b_emb, v_w_in, v_g_cq, v_g_ckv, v_w_uq, v_w_ukv, v_w_mem_kv, v_g_out_a, v_g_out_b, v_g_out_m, v_w_out, v_g_post, v_b_post):
    nb = x.shape[0]
    t = nb * SEQ
    x2 = x.reshape(t, D_MODEL)
    tgt2 = loss_target.reshape(t, D_MODEL)
    mem2 = mem.reshape(nb * N_MEM, D_MODEL)
    g_emb2, b_emb2 = g_emb.reshape(1, -1), b_emb.reshape(1, -1)

    w_in_t, m_w_in_t, v_w_in_t = w_in[0].T, m_w_in[0].T, v_w_in[0].T
    s_in, s_uq, s_ukv, s_mem, s_out = cast_shards((w_in_t, w_uq[0], w_ukv[0], w_mem_kv[0], w_out[0]))
    (h32, h16, (a_c, a_sa, a_sb), (m_c, m_sa, m_sb)), (g_in,) = embed_fwd(
        x2, g_emb2, b_emb2, positions, ((s_in,), None, (lax.empty((N_DEV,) + s_in.shape, BF16),)))
    win_t = w_in_working_t(g_in)

    proj = mm_nn(h16, win_t, F32, 2048, 1536, "proj", rhs_transposed=True)
    later = (s_uq, s_ukv, s_mem, s_out)
    (ya, lse_a), qkv_d, (g_uq, g_ukv, g_mem, g_out) = a_attn_fwd(
        proj, a_c, a_sa, a_sb, nb,
        (later, (ALL_DEVICES,) * len(later), tuple(lax.empty((N_DEV,) + w.shape, BF16) for w in later)))
    wuq_w = _w_uq_working(g_uq)
    wkv_w = _w_ukv_working(g_ukv)
    wmem = g_mem.reshape(D_MODEL, 2 * MEM_WIDTH)
    wout = g_out.reshape(D_MIX, D_MODEL)
    qb, kb, vb = mla_prep_fwd(proj, m_c, m_sa, m_sb, g_cq, g_ckv, wuq_w, wkv_w)
    yb, lse_b = mla_attn_fwd(qb, kb, vb, nb)
    mkv = mm_nn(mem2, wmem, BF16, nb * N_MEM, 512, "mem_kv")
    ym = mem_attn_fwd(proj, mkv, nb)
    z, du32, du16, loss_sum, dg_post, db_post = gate_out_ln_loss(
        ya, yb, ym, proj, g_out_a, g_out_b, g_out_m, wout, h32, tgt2, g_post, b_post)

    dya, dyb, dym, dag, dbg, dmg, dg_out_a, dg_out_b, dg_out_m = gate_bwd(
        du16, wout, ya, yb, ym, proj, g_out_a, g_out_b, g_out_m)
    dw_out = mm_tn(z, du16, 1024, "dw_out")
    dmq, dmk, dmv = mem_attn_bwd(proj, mkv, dym, nb)
    dw_mem = mm_tn(mem2, jnp.concatenate([dmk, dmv], 1), nb * N_MEM, "dw_mem")
    shards_6_7, d_bg_head = mm_tn_group((dbg, dmq, dmg), h16, 2048, "dw_in_bg_mq_mg", W_IN_SHARD, BG_SPLIT, 2, (0, BG_SPLIT))
    landing = lambda w, dtype=F32: lax.empty((N_DEV,) + w.shape, dtype)
    big_w = (w_in_t, w_uq[0], w_ukv[0], w_mem_kv[0], w_out[0])
    (daq, dak, dav), (p_out, p_mem, p_in) = a_attn_bwd(
        qkv_d, a_c, a_sa, a_sb, dya, ya, lse_a, nb,
        ((dw_out.reshape(N_DEV, D_MIX // N_DEV, D_MODEL), dw_mem.reshape(N_DEV, D_MODEL // N_DEV, 2 * MEM_WIDTH),
          shards_6_7),
         (ALL_DEVICES, ALL_DEVICES, (6, 7)),
         (landing(w_out[0]), landing(w_mem_kv[0]), landing(w_in_t, BF16))))
    shards_0_4, d_ag_tail = mm_tn_group((daq, dak, dav, dag), h16, 1024, "dw_in_aq_ak_av_ag", W_IN_SHARD, 0, 5,
                                        (5 * W_IN_SHARD, 4 * A_WIDTH))
    (dqb, dkb, dvb), (p_in,) = mla_attn_bwd(
        qb, kb, vb, dyb, yb, lse_b, nb, ((shards_0_4,), ((0, 1, 2, 3, 4),), (p_in,)))
    dcc, dqf, cqn, dkvf, ckvn, dg_cq, dg_ckv = mla_prep_bwd(proj, m_c, m_sa, m_sb, g_cq, g_ckv, wuq_w, wkv_w, dqb, dkb, dvb)
    dw_uq, dw_ukv, d_cc = mm_tn_pairs(((cqn, dqf), (ckvn, dkvf), (dcc, h16)), 1024, "dw_uq_ukv_cc")
    pieces = (daq, dak, dav, dag, dcc, dbg, dmq, dmg)
    (grad_x, dg_emb, db_emb), (p_in, p_uq, p_ukv) = dh_ln_bwd(
        pieces, win_t, du32, x2, g_emb2,
        ((_w_in_shard_5(d_ag_tail, d_cc, d_bg_head), _w_uq_parts(dw_uq), _w_ukv_parts(dw_ukv)),
         ((5,), ALL_DEVICES, ALL_DEVICES),
         (p_in, landing(w_uq[0]), landing(w_ukv[0]))))

    parts = (p_in, p_uq, p_ukv, p_mem, p_out)
    big_m = (m_w_in_t, m_w_uq[0], m_w_ukv[0], m_w_mem_kv[0], m_w_out[0])
    big_v = (v_w_in_t, v_w_uq[0], v_w_ukv[0], v_w_mem_kv[0], v_w_out[0])
    big = {"w_in": [o.T[None] for o in adamw_shard(big_w[0], parts[0], big_m[0], big_v[0], "adamw_w_in")]}
    rest = adamw_shards_whole(big_w[1:], parts[1:], big_m[1:], big_v[1:], "adamw_rest")
    for name, res in zip(("w_uq", "w_ukv", "w_mem_kv", "w_out"), rest):
        big[name] = [o[None] for o in res]

    small_w = (g_emb, b_emb, g_cq, g_ckv, g_out_a, g_out_b, g_out_m, g_post, b_post)
    small_m = (m_g_emb, m_b_emb, m_g_cq, m_g_ckv, m_g_out_a, m_g_out_b, m_g_out_m, m_g_post, m_b_post)
    small_v = (v_g_emb, v_b_emb, v_g_cq, v_g_ckv, v_g_out_a, v_g_out_b, v_g_out_m, v_g_post, v_b_post)
    small_g = (dg_emb, db_emb, dg_cq, dg_ckv, dg_out_a, dg_out_b, dg_out_m, dg_post, db_post)
    rows128 = lambda vals: [v.reshape(-1, LANES) for v in vals]
    res = small_allreduce_adamw(loss_sum, rows128(small_g), rows128(small_w), rows128(small_m), rows128(small_v))
    loss = res[0][0, 0]
    n_small = len(small_w)
    sg, sd, sm, sv = [[r.reshape(w.shape) for r, w in zip(res[1 + k * n_small:1 + (k + 1) * n_small], small_w)]
                      for k in range(4)]

    order = ("g_emb", "b_emb", "w_in", "g_cq", "g_ckv", "w_uq", "w_ukv", "w_mem_kv", "g_out_a", "g_out_b", "g_out_m",
             "w_out", "g_post", "b_post")
    small_idx = {n: i for i, n in enumerate(SMALL_NAMES)}
    outs = [loss, grad_x.reshape(x.shape)]
    for kind in range(4):
        for name in order:
            outs.append(big[name][kind] if name in big else (sg, sd, sm, sv)[kind][small_idx[name]])
    return tuple(outs)
```

```python
import functools

import jax
import jax.numpy as jnp
from jax import lax
from jax.experimental import pallas as pl
from jax.experimental.pallas import tpu as pltpu

F32 = jnp.float32
BF16 = jnp.bfloat16
SDS = jax.ShapeDtypeStruct
MESH = pl.DeviceIdType.MESH

D_MODEL = 1024
SEQ = 2048
A_HEADS, A_HEAD_DIM, A_ROT = 16, 64, 16
A_WIDTH = 1024
DILATIONS = (1, 4, 16)
N_SIDE = 64
MLA_HEADS, MLA_Q_RANK, MLA_KV_RANK = 8, 256, 128
MLA_NOPE, MLA_ROPE, MLA_V = 64, 32, 64
MLA_WIDTH = 512
N_MEM, MEM_HEADS, MEM_HEAD_DIM, MEM_WIDTH = 256, 4, 128, 512
ROPE_THETA = 500000.0
NORM_EPS = 1e-5
NEG_INF = -1e30
ALPHA = 2.0 ** 0.25
D_IN = 6048
N_DEV = 8

ADAM_LR, ADAM_B1, ADAM_B2, ADAM_EPS, ADAM_WD, ADAM_STEP = 0.001, 0.9, 0.999, 1e-08, 0.01, 10

D_INW = 6144
PIECE_WIDTHS = (1024, 1024, 1024, 1024, 512, 512, 512, 512)
PIECE_OFFS = (0, 1024, 2048, 3072, 4096, 4608, 5120, 5632)
LANES = 128
VMEM_LIMIT = 56 * 1024 * 1024


def _params(*sem):
    kw = dict(vmem_limit_bytes=VMEM_LIMIT)
    if sem:
        kw["dimension_semantics"] = sem
    return pltpu.CompilerParams(**kw)


def _dot(a, b):
    return jnp.dot(a, b, preferred_element_type=F32)


def _dot_nt(a, b):
    return lax.dot_general(a, b, (((1,), (1,)), ((), ())), preferred_element_type=F32)


def _dot_tn(a, b):
    return lax.dot_general(a, b, (((0,), (0,)), ((), ())), preferred_element_type=F32)


def _sigmoid(x):
    return 1.0 / (1.0 + jnp.exp(-x))


def _rope_fwd(x, c, sa, sb, half):
    n = x.shape[-1]
    return x * c + pltpu.roll(x, n - half, 1) * sa + pltpu.roll(x, half, 1) * sb


def _rope_bwd(dy, c, sa, sb, half):
    n = dy.shape[-1]
    return dy * c + pltpu.roll(dy * sa, half, 1) + pltpu.roll(dy * sb, n - half, 1)


def mm_nn(a, b, out_dtype, tm, tn, name, rhs_transposed=False):
    m, k = a.shape
    n = b.shape[0] if rhs_transposed else b.shape[1]
    dot = _dot_nt if rhs_transposed else _dot

    def body(a_ref, b_ref, o_ref):
        o_ref[...] = dot(a_ref[...].astype(BF16), b_ref[...].astype(BF16)).astype(o_ref.dtype)

    b_spec = pl.BlockSpec((tn, k), lambda j, i: (j, 0)) if rhs_transposed else pl.BlockSpec((k, tn), lambda j, i: (0, j))
    return pl.pallas_call(
        body, grid=(n // tn, m // tm),
        in_specs=[pl.BlockSpec((tm, k), lambda j, i: (i, 0)), b_spec],
        out_specs=pl.BlockSpec((tm, tn), lambda j, i: (i, j)),
        out_shape=SDS((m, n), out_dtype), name=name,
        compiler_params=_params("parallel", "parallel"))(a, b)


def mm_tn(a, b, tt, name):
    t, m = a.shape
    n = b.shape[1]

    def body(a_ref, b_ref, o_ref):
        @pl.when(pl.program_id(0) == 0)
        def _():
            o_ref[...] = jnp.zeros_like(o_ref)

        o_ref[...] += _dot_tn(a_ref[...].astype(BF16), b_ref[...].astype(BF16))

    return pl.pallas_call(
        body, grid=(t // tt,),
        in_specs=[pl.BlockSpec((tt, m), lambda i: (i, 0)), pl.BlockSpec((tt, n), lambda i: (i, 0))],
        out_specs=pl.BlockSpec((m, n), lambda i: (0, 0)),
        out_shape=SDS((m, n), F32), name=name,
        compiler_params=_params("arbitrary"))(a, b)


def mm_tn_pairs(pairs, tt, name):
    n = len(pairs)
    t = pairs[0][0].shape[0]

    def body(*refs):
        @pl.when(pl.program_id(0) == 0)
        def _():
            for o_ref in refs[2 * n:]:
                o_ref[...] = jnp.zeros_like(o_ref)

        for i in range(n):
            refs[2 * n + i][...] += _dot_tn(refs[2 * i][...].astype(BF16), refs[2 * i + 1][...].astype(BF16))

    rows = lambda x: pl.BlockSpec((tt, x.shape[1]), lambda i: (i, 0))
    return pl.pallas_call(
        body, grid=(t // tt,),
        in_specs=[rows(x) for pair in pairs for x in pair],
        out_specs=[pl.BlockSpec((a.shape[1], b.shape[1]), lambda i: (0, 0)) for a, b in pairs],
        out_shape=[SDS((a.shape[1], b.shape[1]), F32) for a, b in pairs], name=name,
        compiler_params=_params("arbitrary"))(*[x for pair in pairs for x in pair])


def mm_tn_group(pieces, b, tt, name, slab_rows, first_slab_row, n_slabs, keep_rows):
    n, (t, w), cols = len(pieces), pieces[0].shape, b.shape[1]
    nt = t // tt
    keep_lo, keep_hi = keep_rows

    def body(*refs):
        p_refs, b_ref, slab_ref, keep_ref, acc = refs[:n], refs[n], refs[n + 1], refs[n + 2], refs[n + 3]

        @pl.when(pl.program_id(1) == 0)
        def _():
            acc[...] = jnp.zeros_like(acc)

        for k in range(n):
            @pl.when(pl.program_id(0) == k)
            def _(k=k):
                acc[...] += _dot_tn(p_refs[k][...], b_ref[...])

            @pl.when((pl.program_id(0) == k) & (pl.program_id(1) == nt - 1))
            def _(k=k):
                for j in range(n_slabs):
                    lo = max(k * w, first_slab_row + j * slab_rows)
                    hi = min((k + 1) * w, first_slab_row + (j + 1) * slab_rows)
                    if lo < hi:
                        dst = lo - first_slab_row - j * slab_rows
                        slab_ref[j, dst:dst + hi - lo, :] = acc[lo - k * w:hi - k * w, :].astype(slab_ref.dtype)
                lo, hi = max(k * w, keep_lo), min((k + 1) * w, keep_hi)
                if lo < hi:
                    keep_ref[lo - keep_lo:hi - keep_lo, :] = acc[lo - k * w:hi - k * w, :]

    def piece_spec(k):
        return pl.BlockSpec((tt, w), lambda p, i: (jnp.where(p < k, 0, jnp.where(p > k, nt - 1, i)), 0))

    return pl.pallas_call(
        body, grid=(n, nt),
        in_specs=[piece_spec(k) for k in range(n)] + [pl.BlockSpec((tt, cols), lambda p, i: (i, 0))],
        out_specs=[pl.BlockSpec((n_slabs, slab_rows, cols), lambda p, i: (0, 0, 0)),
                   pl.BlockSpec((keep_hi - keep_lo, cols), lambda p, i: (0, 0))],
        out_shape=[SDS((n_slabs, slab_rows, cols), BF16), SDS((keep_hi - keep_lo, cols), F32)],
        scratch_shapes=[pltpu.VMEM((w, cols), F32)], name=name,
        compiler_params=_params("arbitrary", "arbitrary"))(*pieces, b)


def embed_fwd(x2, g, b, positions, xch):
    t, d = x2.shape
    tm = 512
    pos = positions.astype(F32).reshape(-1, 1)

    def body(x_ref, g_ref, b_ref, pos_ref, pat_ref, h32_ref, h16_ref, *tabs):
        x = x_ref[...]
        mu = jnp.mean(x, axis=-1, keepdims=True)
        xc = x - mu
        var = jnp.mean(xc * xc, axis=-1, keepdims=True)
        h = xc * lax.rsqrt(var + NORM_EPS) * g_ref[...] + b_ref[...]
        h32_ref[...] = h
        h16_ref[...] = h.astype(BF16)
        p = pos_ref[...]
        for k in range(2):
            inv, first, second = pat_ref[3 * k:3 * k + 1, :], pat_ref[3 * k + 1:3 * k + 2, :], pat_ref[3 * k + 2:3 * k + 3, :]
            ang = p * inv
            sn = jnp.sin(ang)
            tabs[3 * k][...] = jnp.where(first + second > 0.0, jnp.cos(ang), 1.0)
            tabs[3 * k + 1][...] = -first * sn
            tabs[3 * k + 2][...] = second * sn

    row = pl.BlockSpec((tm, d), lambda i: (i, 0))
    vec = pl.BlockSpec((1, d), lambda i: (0, 0))
    tab = pl.BlockSpec((tm, LANES), lambda i: (i, 0))
    res, landed = call_hosting_exchange(
        body, xch, grid=(t // tm,),
        in_specs=[row, vec, vec, pl.BlockSpec((tm, 1), lambda i: (i, 0)), pl.BlockSpec((8, LANES), lambda i: (0, 0))],
        out_specs=[row, row] + [tab] * 6,
        out_shape=[SDS((t, d), F32), SDS((t, d), BF16)] + [SDS((t, LANES), F32)] * 6,
        scratch_shapes=[], name="embed_fwd", operands=(x2, g, b, pos, _rope_lane_patterns()))
    return (res[0], res[1], tuple(res[2:5]), tuple(res[5:8])), landed


Q_BLK = 128
UNROLL_FWD = 16
UNROLL_BWD = 16


def _pattern_geometry(d):
    length = SEQ // d
    nblk = length // Q_BLK
    kwin = min(2 * Q_BLK, length)
    return length, nblk, kwin


def _block_coords(idx, d):
    length, nblk, kwin = _pattern_geometry(d)
    r = lax.shift_right_logical(idx, nblk.bit_length() - 1)
    i = idx & (nblk - 1)
    q0 = pl.multiple_of(r * length + i * Q_BLK, Q_BLK)
    ks = jnp.clip(i * Q_BLK - N_SIDE, 0, length - kwin)
    k0 = pl.multiple_of(r * length + ks, N_SIDE)
    qpos = i * Q_BLK + lax.broadcasted_iota(jnp.int32, (Q_BLK, kwin), 0)
    kpos = ks + lax.broadcasted_iota(jnp.int32, (Q_BLK, kwin), 1)
    valid = jnp.abs(kpos - qpos) <= N_SIDE
    return q0, k0, kwin, valid


def _deinterleave(src_ref, dst_ref, d, dtype, tmp_ref):
    if d == 1:
        dst_ref[...] = src_ref[...].astype(dtype)
        return
    q = SEQ // 4
    if d == 4:
        for r in range(4):
            dst_ref[r * q:(r + 1) * q, :] = src_ref[pl.ds(r, q, stride=4), :].astype(dtype)
        return
    assert d == 16
    n = SEQ // 16
    for r in range(4):
        tmp_ref[r * q:(r + 1) * q, :] = src_ref[pl.ds(r, q, stride=4), :]
    for r in range(4):
        for j in range(4):
            dst_ref[(r + 4 * j) * n:(r + 4 * j + 1) * n, :] = tmp_ref[pl.ds(r * q + j, n, stride=4), :].astype(dtype)


def _class16_to_class4(src_ref, dst_ref):
    q, n = SEQ // 4, SEQ // 16
    for r in range(4):
        for j in range(4):
            dst_ref[pl.ds(r * q + j, n, stride=4), :] = src_ref[(r + 4 * j) * n:(r + 4 * j + 1) * n, :]


def _interleave(src_ref, dst_ref, d, tmp_ref, accumulate):
    q = SEQ // 4
    if d == 16:
        _class16_to_class4(src_ref, tmp_ref)
        src_ref = tmp_ref
    else:
        assert d == 4
    for r in range(4):
        rows = pl.ds(r, q, stride=4)
        val = src_ref[r * q:(r + 1) * q, :]
        dst_ref[rows, :] = dst_ref[rows, :] + val if accumulate else val


def a_attn_fwd(proj, ca, sa, sb, nb, xch):
    t = proj.shape[0]
    n_pairs = A_WIDTH // LANES

    def body(q_ref, k_ref, v_ref, c_ref, sa_ref, sb_ref, y_ref, lse_ref, *rest):
        qkv_d, (qr_s, kr_s, oc_s, lc_s, o1_s, l1_s, o2_s, l2_s, o3_s, l3_s, tmp_s) = rest[:9], rest[9:]
        c, s_a, s_b = c_ref[...], sa_ref[...], sb_ref[...]
        qr_s[...] = _rope_fwd(q_ref[...], c, s_a, s_b, A_ROT // 2) * (A_HEAD_DIM ** -0.5)
        kr_s[...] = _rope_fwd(k_ref[...], c, s_a, s_b, A_ROT // 2)
        head0 = lax.broadcasted_iota(jnp.int32, (Q_BLK, LANES), 1) < A_HEAD_DIM
        nat = ((o1_s, l1_s), (o2_s, l2_s), (o3_s, l3_s))

        for g, d in enumerate(DILATIONS):
            qd_s, kd_s, vd_s = qkv_d[3 * g:3 * g + 3]
            _deinterleave(qr_s, qd_s, d, BF16, tmp_s)
            _deinterleave(kr_s, kd_s, d, BF16, tmp_s)
            _deinterleave(v_ref, vd_s, d, BF16, tmp_s)
            o_dst, l_dst = (nat[g] if d == 1 else (oc_s, lc_s))

            def block(idx, carry, d=d, o_dst=o_dst, l_dst=l_dst, qd_s=qd_s, kd_s=kd_s, vd_s=vd_s):
                q0, k0, kwin, valid = _block_coords(idx, d)
                qb = qd_s[pl.ds(q0, Q_BLK), :]
                kb = kd_s[pl.ds(k0, kwin), :]
                vb = vd_s[pl.ds(k0, kwin), :]
                zero = jnp.zeros_like(qb)
                q2 = jnp.concatenate([jnp.where(head0, qb, zero), jnp.where(head0, zero, qb)], 0)
                s = jnp.where(jnp.concatenate([valid, valid], 0), _dot_nt(q2, kb), NEG_INF)
                m = jnp.max(s, axis=-1, keepdims=True)
                p = jnp.exp(s - m)
                l = jnp.sum(p, axis=-1, keepdims=True)
                o2 = _dot(p.astype(BF16), vb) / l
                l2 = m + jnp.log(l)
                o_dst[pl.ds(q0, Q_BLK), :] = jnp.where(head0, o2[:Q_BLK], o2[Q_BLK:])
                l_dst[pl.ds(q0, Q_BLK), :] = jnp.where(head0, l2[:Q_BLK], l2[Q_BLK:])
                return carry

            lax.fori_loop(0, SEQ // Q_BLK, block, 0, unroll=UNROLL_FWD)
            if d > 1:
                _interleave(oc_s, nat[g][0], d, tmp_s, False)
                _interleave(lc_s, nat[g][1], d, tmp_s, False)

        def merge(ci, carry):
            rows = pl.ds(pl.multiple_of(ci * 256, 256), 256)
            l1, l2, l3 = l1_s[rows, :], l2_s[rows, :], l3_s[rows, :]
            m = jnp.maximum(jnp.maximum(l1, l2), l3)
            w1, w2, w3 = jnp.exp(l1 - m), jnp.exp(l2 - m), jnp.exp(l3 - m)
            w = w1 + w2 + w3
            y_ref[rows, :] = (w1 * o1_s[rows, :] + w2 * o2_s[rows, :] + w3 * o3_s[rows, :]) / w
            lse_ref[rows, :] = m + jnp.log(w)
            return carry

        lax.fori_loop(0, SEQ // 256, merge, 0)

    def col(off):
        return pl.BlockSpec((SEQ, LANES), lambda b, hp: (b, off + hp))

    tab = pl.BlockSpec((SEQ, LANES), lambda b, hp: (b, 0))
    out = pl.BlockSpec((SEQ, LANES), lambda b, hp: (b, hp))
    f32s = pltpu.VMEM((SEQ, LANES), F32)
    res, landed = call_hosting_exchange(
        body, xch, grid=(nb, n_pairs),
        in_specs=[col(0), col(n_pairs), col(2 * n_pairs), tab, tab, tab],
        out_specs=[out] * 11,
        out_shape=[SDS((t, A_WIDTH), F32)] * 2 + [SDS((t, A_WIDTH), BF16)] * 9,
        scratch_shapes=[f32s] * 11,
        name="a_attn_fwd", operands=(proj, proj, proj, ca, sa, sb))
    return res[:2], res[2:], landed


def a_attn_bwd(qkv_d, ca, sa, sb, dy, y, lse, nb, xch):
    t = dy.shape[0]
    n_pairs = A_WIDTH // LANES

    def body(*refs):
        qkv_refs = refs[:9]
        (c_ref, sa_ref, sb_ref, do_ref, y_ref, lse_ref, dq_ref, dk_ref, dv_ref,
         l0n_s, l1n_s, d0n_s, d1n_s, dod_s, l0d_s, l1d_s, d0d_s, d1d_s,
         dqc_s, dkc_s, dvc_s, dq4_s, dk4_s, dv4_s, dqn_s, dkn_s, dvn_s, tmp_s) = refs[9:]
        c, s_a, s_b = c_ref[...], sa_ref[...], sb_ref[...]
        head0 = lax.broadcasted_iota(jnp.int32, (Q_BLK, LANES), 1) < A_HEAD_DIM

        def per_head_rows(ci, carry):
            rows = pl.ds(pl.multiple_of(ci * 256, 256), 256)
            h0 = lax.broadcasted_iota(jnp.int32, (256, LANES), 1) < A_HEAD_DIM
            tt = do_ref[rows, :] * y_ref[rows, :]
            d0n_s[rows, :] = jnp.broadcast_to(jnp.sum(jnp.where(h0, tt, 0.0), axis=-1, keepdims=True), (256, LANES))
            d1n_s[rows, :] = jnp.broadcast_to(jnp.sum(jnp.where(h0, 0.0, tt), axis=-1, keepdims=True), (256, LANES))
            l = lse_ref[rows, :]
            lr = pltpu.roll(l, A_HEAD_DIM, 1)
            l0n_s[rows, :] = jnp.where(h0, l, lr)
            l1n_s[rows, :] = jnp.where(h0, lr, l)
            return carry

        lax.fori_loop(0, SEQ // 256, per_head_rows, 0)
        assert DILATIONS == (1, 4, 16)

        for g, d in enumerate(DILATIONS):
            qd_s, kd_s, vd_s = qkv_refs[3 * g:3 * g + 3]
            _deinterleave(do_ref, dod_s, d, BF16, tmp_s)
            if d > 1:
                for src, dst in ((l0n_s, l0d_s), (l1n_s, l1d_s), (d0n_s, d0d_s), (d1n_s, d1d_s)):
                    _deinterleave(src, dst, d, F32, tmp_s)
            l0, l1, d0, d1 = (l0n_s, l1n_s, d0n_s, d1n_s) if d == 1 else (l0d_s, l1d_s, d0d_s, d1d_s)
            dq_dst, dk_dst, dv_dst = {1: (dqn_s, dkn_s, dvn_s), 4: (dq4_s, dk4_s, dv4_s), 16: (dqc_s, dkc_s, dvc_s)}[d]
            dk_dst[...] = jnp.zeros_like(dk_dst)
            dv_dst[...] = jnp.zeros_like(dv_dst)

            def block(idx, carry, d=d, l0=l0, l1=l1, d0=d0, d1=d1, dq_dst=dq_dst, dk_dst=dk_dst, dv_dst=dv_dst,
                      qd_s=qd_s, kd_s=kd_s, vd_s=vd_s):
                q0, k0, kwin, valid = _block_coords(idx, d)
                qrows = pl.ds(q0, Q_BLK)
                krows = pl.ds(k0, kwin)
                qb, dob = qd_s[qrows, :], dod_s[qrows, :]
                kb, vb = kd_s[krows, :], vd_s[krows, :]
                zero = jnp.zeros_like(qb)
                q2 = jnp.concatenate([jnp.where(head0, qb, zero), jnp.where(head0, zero, qb)], 0)
                do2 = jnp.concatenate([jnp.where(head0, dob, zero), jnp.where(head0, zero, dob)], 0)
                wide = lambda x: jnp.concatenate([x] * (kwin // LANES), 1)
                lse2 = wide(jnp.concatenate([l0[qrows, :], l1[qrows, :]], 0))
                dd2 = wide(jnp.concatenate([d0[qrows, :], d1[qrows, :]], 0))
                s = jnp.where(jnp.concatenate([valid, valid], 0), _dot_nt(q2, kb), NEG_INF)
                p = jnp.exp(s - lse2)
                ds = (p * (_dot_nt(do2, vb) - dd2)).astype(BF16)
                dq2 = _dot(ds, kb)
                dq_dst[qrows, :] = jnp.where(head0, dq2[:Q_BLK], dq2[Q_BLK:])
                dk_dst[krows, :] += _dot_tn(ds, q2)
                dv_dst[krows, :] += _dot_tn(p.astype(BF16), do2)
                return carry

            lax.fori_loop(0, SEQ // Q_BLK, block, 0, unroll=UNROLL_BWD)

        for c16, c4, nat in ((dqc_s, dq4_s, dqn_s), (dkc_s, dk4_s, dkn_s), (dvc_s, dv4_s, dvn_s)):
            _class16_to_class4(c16, tmp_s)
            c4[...] = c4[...] + tmp_s[...]
            _interleave(c4, nat, 4, tmp_s, True)

        dq_ref[...] = _rope_bwd(dqn_s[...] * (A_HEAD_DIM ** -0.5), c, s_a, s_b, A_ROT // 2).astype(BF16)
        dk_ref[...] = _rope_bwd(dkn_s[...], c, s_a, s_b, A_ROT // 2).astype(BF16)
        dv_ref[...] = dvn_s[...].astype(BF16)

    tab = pl.BlockSpec((SEQ, LANES), lambda b, hp: (b, 0))
    blk = pl.BlockSpec((SEQ, LANES), lambda b, hp: (b, hp))
    f32s = pltpu.VMEM((SEQ, LANES), F32)
    b16s = pltpu.VMEM((SEQ, LANES), BF16)
    return call_hosting_exchange(
        body, xch, grid=(nb, n_pairs),
        in_specs=[blk] * 9 + [tab, tab, tab, blk, blk, blk],
        out_specs=[blk, blk, blk],
        out_shape=[SDS((t, A_WIDTH), BF16)] * 3,
        scratch_shapes=[f32s] * 4 + [b16s] + [f32s] * 14,
        name="a_attn_bwd", operands=(*qkv_d, ca, sa, sb, dy, y, lse))


MLA_SCALE = (MLA_NOPE + MLA_ROPE) ** -0.5
LOG2E = 1.4426950408889634
MLA_QW = MLA_HEADS * LANES
MLA_KVW = MLA_QW + MLA_WIDTH


def _rms(x, g):
    r = lax.rsqrt(jnp.mean(x * x, axis=-1, keepdims=True) + NORM_EPS)
    return x * r * g, r


def _rms_bwd(dn, x, r, g):
    tg = dn * g
    dx = r * tg - x * (r * r * r) * jnp.mean(tg * x, axis=-1, keepdims=True)
    return dx, jnp.sum(dn * x * r, axis=0, keepdims=True)


def mla_prep_fwd(proj, cm, sma, smb, g_cq, g_ckv, wuq, wkv):
    t = proj.shape[0]
    tm = 1024

    def body(cq_ref, ckv_ref, kr_ref, c_ref, sa_ref, sb_ref, gq_ref, gkv_ref, wuq_ref, wkv_ref, q_ref, k_ref, v_ref):
        c, s_a, s_b = c_ref[...], sa_ref[...], sb_ref[...]
        cqn, _ = _rms(cq_ref[...], gq_ref[...])
        qf = _dot(cqn.astype(BF16), wuq_ref[...])
        ckvn, _ = _rms(ckv_ref[...], gkv_ref[...])
        kvf = _dot(ckvn.astype(BF16), wkv_ref[...])
        krope = _rope_fwd(kr_ref[...], c, s_a, s_b, MLA_ROPE // 2)
        for h in range(MLA_HEADS):
            cols = slice(h * LANES, (h + 1) * LANES)
            q_ref[:, cols] = (_rope_fwd(qf[:, cols], c, s_a, s_b, MLA_ROPE // 2) * (MLA_SCALE * LOG2E)).astype(BF16)
            k_ref[:, cols] = (kvf[:, cols] + krope).astype(BF16)
        v_ref[...] = kvf[:, MLA_QW:].astype(BF16)

    def row(w, j):
        return pl.BlockSpec((tm, w), lambda i: (i, j))

    def full(a):
        return pl.BlockSpec(a.shape, lambda i: (0, 0))

    return pl.pallas_call(
        body, grid=(t // tm,),
        in_specs=[row(256, 4096 // 256), row(128, 4352 // 128), row(128, 4480 // 128), row(128, 0), row(128, 0), row(128, 0),
                  full(g_cq), full(g_ckv), full(wuq), full(wkv)],
        out_specs=[row(MLA_QW, 0), row(MLA_QW, 0), row(MLA_WIDTH, 0)],
        out_shape=[SDS((t, MLA_QW), BF16), SDS((t, MLA_QW), BF16), SDS((t, MLA_WIDTH), BF16)],
        name="mla_prep_fwd", compiler_params=_params("parallel"))(proj, proj, proj, cm, sma, smb, g_cq, g_ckv, wuq, wkv)


def mla_prep_bwd(proj, cm, sma, smb, g_cq, g_ckv, wuq, wkv, dq, dk, dv):
    t = proj.shape[0]
    tm = 1024

    def body(cq_ref, ckv_ref, c_ref, sa_ref, sb_ref, gq_ref, gkv_ref, wuq_ref, wkv_ref, dq_ref, dk_ref, dv_ref,
             dcc_ref, dqf_ref, cqn_ref, dkvf_ref, ckvn_ref, dgq_ref, dgkv_ref):
        @pl.when(pl.program_id(0) == 0)
        def _():
            dgq_ref[...] = jnp.zeros_like(dgq_ref)
            dgkv_ref[...] = jnp.zeros_like(dgkv_ref)

        c, s_a, s_b = c_ref[...], sa_ref[...], sb_ref[...]
        cq, ckv = cq_ref[...], ckv_ref[...]
        cqn, rq = _rms(cq, gq_ref[...])
        ckvn, rkv = _rms(ckv, gkv_ref[...])
        cqn_ref[...] = cqn.astype(BF16)
        ckvn_ref[...] = ckvn.astype(BF16)
        lane = lax.broadcasted_iota(jnp.int32, (tm, LANES), 1)
        rope_lanes = (lane >= MLA_NOPE) & (lane < MLA_NOPE + MLA_ROPE)
        dkrope = jnp.zeros((tm, LANES), F32)
        for h in range(MLA_HEADS):
            cols = slice(h * LANES, (h + 1) * LANES)
            dqf_ref[:, cols] = _rope_bwd(dq_ref[:, cols].astype(F32) * MLA_SCALE, c, s_a, s_b, MLA_ROPE // 2).astype(BF16)
            dkh = dk_ref[:, cols].astype(F32) * (1.0 / LOG2E)
            dkvf_ref[:, cols] = dkh.astype(BF16)
            dkrope = dkrope + dkh
        dkvf_ref[:, MLA_QW:] = dv_ref[...].astype(BF16)
        dkr = _rope_bwd(jnp.where(rope_lanes, dkrope, 0.0), c, s_a, s_b, MLA_ROPE // 2)
        dcqn = _dot_nt(dqf_ref[...], wuq_ref[...])
        dckvn = _dot_nt(dkvf_ref[...], wkv_ref[...])
        dcq, dgq = _rms_bwd(dcqn, cq, rq, gq_ref[...])
        dckv, dgkv = _rms_bwd(dckvn, ckv, rkv, gkv_ref[...])
        dgq_ref[...] += dgq
        dgkv_ref[...] += dgkv
        dcc_ref[:, 0:256] = dcq.astype(BF16)
        dcc_ref[:, 256:384] = dckv.astype(BF16)
        dcc_ref[:, 384:512] = dkr.astype(BF16)

    def row(w, j):
        return pl.BlockSpec((tm, w), lambda i: (i, j))

    def full(a):
        return pl.BlockSpec(a.shape, lambda i: (0, 0))

    return pl.pallas_call(
        body, grid=(t // tm,),
        in_specs=[row(256, 4096 // 256), row(128, 4352 // 128), row(128, 0), row(128, 0), row(128, 0),
                  full(g_cq), full(g_ckv), full(wuq), full(wkv), row(MLA_QW, 0), row(MLA_QW, 0), row(MLA_WIDTH, 0)],
        out_specs=[row(512, 0), row(MLA_QW, 0), row(256, 0), row(MLA_KVW, 0), row(128, 0), full(g_cq), full(g_ckv)],
        out_shape=[SDS((t, 512), BF16), SDS((t, MLA_QW), BF16), SDS((t, 256), BF16), SDS((t, MLA_KVW), BF16),
                   SDS((t, 128), BF16), SDS(g_cq.shape, F32), SDS(g_ckv.shape, F32)],
        name="mla_prep_bwd", compiler_params=_params("arbitrary"))(proj, proj, cm, sma, smb, g_cq, g_ckv, wuq, wkv, dq, dk, dv)


MLA_TQ = SEQ
MLA_SUB_FWD = 512
MLA_SUB_BWD = 256


def mla_attn_fwd(qb, kb, vb, nb):
    t = qb.shape[0]
    nq = SEQ // MLA_TQ
    n_pairs = MLA_HEADS // 2

    def body(q_ref, k_ref, v_ref, y_ref, lse_ref):
        head0 = lax.broadcasted_iota(jnp.int32, (MLA_SUB_FWD, LANES), 1) < MLA_V
        v = v_ref[...]
        vhead0 = lax.broadcasted_iota(jnp.int32, v.shape, 1) < MLA_V
        one = jnp.ones_like(v)
        vh = [jnp.where(vhead0 == (h == 0), v, one) for h in range(2)]
        for sub in range(MLA_TQ // MLA_SUB_FWD):
            rows = slice(sub * MLA_SUB_FWD, (sub + 1) * MLA_SUB_FWD)
            outs, lses = [], []
            for h in range(2):
                cols = slice(h * LANES, (h + 1) * LANES)
                s = _dot_nt(q_ref[rows, cols], k_ref[:, cols])
                m = jnp.max(s, axis=-1, keepdims=True)
                p = jnp.exp2(s - m).astype(BF16)
                ol = _dot(p, vh[h])
                l = pltpu.roll(ol, MLA_V, 1)
                outs.append(ol / l)
                lses.append(m + jnp.log2(l))
            y_ref[rows, :] = jnp.where(head0, outs[0], outs[1])
            lse_ref[rows, :] = jnp.where(head0, lses[0], lses[1])

    return pl.pallas_call(
        body, grid=(nb, n_pairs, nq),
        in_specs=[pl.BlockSpec((MLA_TQ, 2 * LANES), lambda b, hp, i: (b * nq + i, hp)),
                  pl.BlockSpec((SEQ, 2 * LANES), lambda b, hp, i: (b, hp)),
                  pl.BlockSpec((SEQ, LANES), lambda b, hp, i: (b, hp))],
        out_specs=[pl.BlockSpec((MLA_TQ, LANES), lambda b, hp, i: (b * nq + i, hp))] * 2,
        out_shape=[SDS((t, MLA_WIDTH), F32)] * 2,
        name="mla_attn_fwd", compiler_params=_params("parallel", "parallel", "parallel"))(qb, kb, vb)


def mla_attn_bwd(qb, kb, vb, dy, y, lse, nb, xch):
    t = qb.shape[0]
    nq = SEQ // MLA_TQ
    n_pairs = MLA_HEADS // 2

    assert nq == 1

    def body(q_ref, k_ref, v_ref, do_ref, y_ref, lse_ref, dq_ref, dk_ref, dv_ref, dk_s, dv_s):
        dk_s[...] = jnp.zeros_like(dk_s)
        dv_s[...] = jnp.zeros_like(dv_s)
        head0 = lax.broadcasted_iota(jnp.int32, (MLA_SUB_BWD, LANES), 1) < MLA_V
        v = v_ref[...]
        for sub in range(MLA_TQ // MLA_SUB_BWD):
            rows = slice(sub * MLA_SUB_BWD, (sub + 1) * MLA_SUB_BWD)
            do = do_ref[rows, :]
            lse = lse_ref[rows, :]
            tt = do * y_ref[rows, :]
            dv = jnp.zeros((SEQ, LANES), F32)
            for h in range(2):
                sel = head0 if h == 0 else ~head0
                lo = h * MLA_V
                cols = slice(h * LANES, (h + 1) * LANES)
                q = q_ref[rows, cols]
                k = k_ref[:, cols]
                dd = jnp.sum(jnp.where(sel, tt, 0.0), axis=-1, keepdims=True)
                doh = jnp.where(sel, do, 0.0).astype(BF16)
                p = jnp.exp2(_dot_nt(q, k) - lse[:, lo:lo + 1])
                dp = _dot_nt(doh, v)
                ds = (p * (dp - dd)).astype(BF16)
                dq_ref[rows, cols] = _dot(ds, k).astype(dq_ref.dtype)
                dk_s[:, cols] += _dot_tn(ds, q)
                dv = dv + _dot_tn(p.astype(BF16), doh)
            dv_s[...] += dv
        dk_ref[...] = dk_s[...].astype(dk_ref.dtype)
        dv_ref[...] = dv_s[...].astype(dv_ref.dtype)

    qspec = pl.BlockSpec((MLA_TQ, 2 * LANES), lambda b, hp, i: (b * nq + i, hp))
    kspec = pl.BlockSpec((SEQ, 2 * LANES), lambda b, hp, i: (b, hp))
    vspec = pl.BlockSpec((SEQ, LANES), lambda b, hp, i: (b, hp))
    ospec = pl.BlockSpec((MLA_TQ, LANES), lambda b, hp, i: (b * nq + i, hp))
    return call_hosting_exchange(
        body, xch, grid=(nb, n_pairs, nq),
        in_specs=[qspec, kspec, vspec, ospec, ospec, ospec],
        out_specs=[qspec, kspec, vspec],
        out_shape=[SDS((t, MLA_QW), BF16), SDS((t, MLA_QW), BF16), SDS((t, MLA_WIDTH), BF16)],
        scratch_shapes=[pltpu.VMEM((SEQ, 2 * LANES), F32), pltpu.VMEM((SEQ, LANES), F32)],
        name="mla_attn_bwd", operands=(qb, kb, vb, dy, y, lse))


MEM_TQ = SEQ
MEM_SUB = SEQ
MEM_SCALE = MEM_HEAD_DIM ** -0.5
MQ_BLK4 = 5120 // MEM_WIDTH


def mem_attn_fwd(proj, mkv, nb):
    t = proj.shape[0]
    nq = SEQ // MEM_TQ

    def body(q_ref, mk_ref, mv_ref, y_ref):
        for sub in range(MEM_TQ // MEM_SUB):
            rows = slice(sub * MEM_SUB, (sub + 1) * MEM_SUB)
            for h in range(MEM_HEADS):
                cols = slice(h * LANES, (h + 1) * LANES)
                s = _dot_nt(q_ref[rows, cols].astype(BF16), mk_ref[:, cols]) * MEM_SCALE
                m = jnp.max(s, axis=-1, keepdims=True)
                p = jnp.exp(s - m)
                l = jnp.sum(p, axis=-1, keepdims=True)
                y_ref[rows, cols] = _dot(p.astype(BF16), mv_ref[:, cols]) / l

    return pl.pallas_call(
        body, grid=(nb, nq),
        in_specs=[pl.BlockSpec((MEM_TQ, MEM_WIDTH), lambda b, i: (b * nq + i, MQ_BLK4)),
                  pl.BlockSpec((N_MEM, MEM_WIDTH), lambda b, i: (b, 0)),
                  pl.BlockSpec((N_MEM, MEM_WIDTH), lambda b, i: (b, 1))],
        out_specs=pl.BlockSpec((MEM_TQ, MEM_WIDTH), lambda b, i: (b * nq + i, 0)),
        out_shape=SDS((t, MEM_WIDTH), F32),
        name="mem_attn_fwd", compiler_params=_params("parallel", "parallel"))(proj, mkv, mkv)


def mem_attn_bwd(proj, mkv, dy, nb):
    t = proj.shape[0]
    nq = SEQ // MEM_TQ

    def body(q_ref, mk_ref, mv_ref, do_ref, dq_ref, dmk_ref, dmv_ref):
        @pl.when(pl.program_id(1) == 0)
        def _():
            dmk_ref[...] = jnp.zeros_like(dmk_ref)
            dmv_ref[...] = jnp.zeros_like(dmv_ref)

        for sub in range(MEM_TQ // MEM_SUB):
            rows = slice(sub * MEM_SUB, (sub + 1) * MEM_SUB)
            for h in range(MEM_HEADS):
                cols = slice(h * LANES, (h + 1) * LANES)
                q = q_ref[rows, cols].astype(BF16)
                mk, mv = mk_ref[:, cols], mv_ref[:, cols]
                do = do_ref[rows, cols].astype(BF16)
                s = _dot_nt(q, mk) * MEM_SCALE
                e = jnp.exp(s - jnp.max(s, axis=-1, keepdims=True))
                p = e / jnp.sum(e, axis=-1, keepdims=True)
                dp = _dot_nt(do, mv)
                ds = (p * (dp - jnp.sum(p * dp, axis=-1, keepdims=True)) * MEM_SCALE).astype(BF16)
                dq_ref[rows, cols] = _dot(ds, mk).astype(BF16)
                dmk_ref[:, cols] += _dot_tn(ds, q)
                dmv_ref[:, cols] += _dot_tn(p.astype(BF16), do)

    ospec = pl.BlockSpec((MEM_TQ, MEM_WIDTH), lambda b, i: (b * nq + i, 0))
    kspec = pl.BlockSpec((N_MEM, MEM_WIDTH), lambda b, i: (b, 0))
    return pl.pallas_call(
        body, grid=(nb, nq),
        in_specs=[pl.BlockSpec((MEM_TQ, MEM_WIDTH), lambda b, i: (b * nq + i, MQ_BLK4)),
                  kspec, pl.BlockSpec((N_MEM, MEM_WIDTH), lambda b, i: (b, 1)), ospec],
        out_specs=[ospec, kspec, kspec],
        out_shape=[SDS((t, MEM_WIDTH), BF16), SDS((nb * N_MEM, MEM_WIDTH), F32), SDS((nb * N_MEM, MEM_WIDTH), F32)],
        name="mem_attn_bwd", compiler_params=_params("parallel", "arbitrary"))(proj, mkv, mkv, dy)


ROW_TM = 512
AG_BLK = 3072 // 1024
BG_BLK = 4608 // 512
MG_BLK = 5632 // 512
GROUPS = ((0, A_WIDTH), (A_WIDTH, MLA_WIDTH), (A_WIDTH + MLA_WIDTH, MEM_WIDTH))
D_MIX = 2048


def _gate_specs():
    def row(w, j):
        return pl.BlockSpec((ROW_TM, w), lambda i: (i, j))

    def vec(w):
        return pl.BlockSpec((1, w), lambda i: (0, 0))

    ys = [row(A_WIDTH, 0), row(MLA_WIDTH, 0), row(MEM_WIDTH, 0)]
    gates = [row(A_WIDTH, AG_BLK), row(MLA_WIDTH, BG_BLK), row(MEM_WIDTH, MG_BLK)]
    gains = [vec(A_WIDTH), vec(MLA_WIDTH), vec(MEM_WIDTH)]
    return row, vec, ys, gates, gains


def gate_out_ln_loss(ya, yb, ym, proj, goa, gob, gom, wout, h32, target, gp, bp):
    t, d = h32.shape
    _, _, ys, gates, gains = _gate_specs()

    def body(ya_ref, yb_ref, ym_ref, ga_ref, gb_ref, gm_ref, goa_ref, gob_ref, gom_ref, w_ref, h_ref, t_ref, gp_ref, bp_ref,
             z_ref, du32_ref, du16_ref, loss_ref, dgp_ref, dbp_ref):
        @pl.when(pl.program_id(0) == 0)
        def _():
            loss_ref[...] = jnp.zeros_like(loss_ref)
            dgp_ref[...] = jnp.zeros_like(dgp_ref)
            dbp_ref[...] = jnp.zeros_like(dbp_ref)

        for (off, w), y_ref, g_ref, go_ref in zip(GROUPS, (ya_ref, yb_ref, ym_ref), (ga_ref, gb_ref, gm_ref),
                                                  (goa_ref, gob_ref, gom_ref)):
            n, _ = _rms(y_ref[...], go_ref[...])
            gt = g_ref[...]
            z_ref[:, off:off + w] = (n * (gt * _sigmoid(gt))).astype(BF16)
        g = gp_ref[...]
        u = ALPHA * h_ref[...] + _dot(z_ref[...], w_ref[...])
        mu = jnp.mean(u, axis=-1, keepdims=True)
        uc = u - mu
        rstd = lax.rsqrt(jnp.mean(uc * uc, axis=-1, keepdims=True) + NORM_EPS)
        xhat = uc * rstd
        err = xhat * g + bp_ref[...] - t_ref[...]
        tok = jnp.sum(err * err, axis=-1, keepdims=True) * (1.0 / d)
        loss_ref[...] += 0.5 * jnp.sum(tok, axis=0, keepdims=True)
        dout = err * (1.0 / d)
        dxhat = dout * g
        du = rstd * (dxhat - jnp.mean(dxhat, axis=-1, keepdims=True)
                     - xhat * jnp.mean(dxhat * xhat, axis=-1, keepdims=True))
        du32_ref[...] = du
        du16_ref[...] = du.astype(BF16)
        dgp_ref[...] += jnp.sum(dout * xhat, axis=0, keepdims=True)
        dbp_ref[...] += jnp.sum(dout, axis=0, keepdims=True)

    row = pl.BlockSpec((ROW_TM, d), lambda i: (i, 0))
    vec = pl.BlockSpec((1, d), lambda i: (0, 0))
    zrow = pl.BlockSpec((ROW_TM, D_MIX), lambda i: (i, 0))
    return pl.pallas_call(
        body, grid=(t // ROW_TM,),
        in_specs=ys + gates + gains + [pl.BlockSpec((D_MIX, d), lambda i: (0, 0)), row, row, vec, vec],
        out_specs=[zrow, row, row, pl.BlockSpec((1, LANES), lambda i: (0, 0)), vec, vec],
        out_shape=[SDS((t, D_MIX), BF16), SDS((t, d), F32), SDS((t, d), BF16), SDS((1, LANES), F32), SDS((1, d), F32),
                   SDS((1, d), F32)],
        name="gate_out_ln_loss", compiler_params=_params("arbitrary"))(
            ya, yb, ym, proj, proj, proj, goa, gob, gom, wout, h32, target, gp, bp)


def gate_bwd(du16, wout, ya, yb, ym, proj, goa, gob, gom):
    t = ya.shape[0]
    row, vec, ys, gates, gains = _gate_specs()

    def body(du_ref, w_ref, ya_ref, yb_ref, ym_ref, ga_ref, gb_ref, gm_ref, goa_ref, gob_ref, gom_ref,
             dya_ref, dyb_ref, dym_ref, dga_ref, dgb_ref, dgm_ref, dgoa_ref, dgob_ref, dgom_ref):
        @pl.when(pl.program_id(0) == 0)
        def _():
            dgoa_ref[...] = jnp.zeros_like(dgoa_ref)
            dgob_ref[...] = jnp.zeros_like(dgob_ref)
            dgom_ref[...] = jnp.zeros_like(dgom_ref)

        dz = _dot_nt(du_ref[...], w_ref[...])
        for (off, w), y_ref, g_ref, go_ref, dy_ref, dg_ref, dgo_ref in zip(
                GROUPS, (ya_ref, yb_ref, ym_ref), (ga_ref, gb_ref, gm_ref), (goa_ref, gob_ref, gom_ref),
                (dya_ref, dyb_ref, dym_ref), (dga_ref, dgb_ref, dgm_ref), (dgoa_ref, dgob_ref, dgom_ref)):
            dzg = dz[:, off:off + w]
            y, gt, go = y_ref[...], g_ref[...], go_ref[...]
            n, r = _rms(y, go)
            sg = _sigmoid(gt)
            dg_ref[...] = (dzg * n * (sg * (1.0 + gt * (1.0 - sg)))).astype(BF16)
            dy, dgo = _rms_bwd(dzg * (gt * sg), y, r, go)
            dy_ref[...] = dy
            dgo_ref[...] += dgo

    widths = (A_WIDTH, MLA_WIDTH, MEM_WIDTH)
    return pl.pallas_call(
        body, grid=(t // ROW_TM,),
        in_specs=[row(D_MODEL, 0), pl.BlockSpec((D_MIX, D_MODEL), lambda i: (0, 0))] + ys + gates + gains,
        out_specs=[row(w, 0) for w in widths] * 2 + [vec(w) for w in widths],
        out_shape=[SDS((t, w), F32) for w in widths] + [SDS((t, w), BF16) for w in widths] + [SDS((1, w), F32) for w in widths],
        name="gate_bwd", compiler_params=_params("arbitrary"))(du16, wout, ya, yb, ym, proj, proj, proj, goa, gob, gom)


def dh_ln_bwd(pieces, win_t, du32, x2, g_emb, xch):
    t, d = x2.shape

    def body(*refs):
        p_refs = refs[:len(pieces)]
        w_ref, du_ref, x_ref, g_ref, dx_ref, dg_ref, db_ref = refs[len(pieces):]

        @pl.when(pl.program_id(0) == 0)
        def _():
            dg_ref[...] = jnp.zeros_like(dg_ref)
            db_ref[...] = jnp.zeros_like(db_ref)

        dh = ALPHA * du_ref[...]
        for p_ref, off, w in zip(p_refs, PIECE_OFFS, PIECE_WIDTHS):
            dh = dh + _dot(p_ref[...], w_ref[off:off + w, :])
        x = x_ref[...]
        xc = x - jnp.mean(x, axis=-1, keepdims=True)
        rstd = lax.rsqrt(jnp.mean(xc * xc, axis=-1, keepdims=True) + NORM_EPS)
        xhat = xc * rstd
        dg_ref[...] += jnp.sum(dh * xhat, axis=0, keepdims=True)
        db_ref[...] += jnp.sum(dh, axis=0, keepdims=True)
        tg = dh * g_ref[...]
        dx_ref[...] = rstd * (tg - jnp.mean(tg, axis=-1, keepdims=True)
                              - xhat * jnp.mean(tg * xhat, axis=-1, keepdims=True))

    row = pl.BlockSpec((ROW_TM, d), lambda i: (i, 0))
    vec = pl.BlockSpec((1, d), lambda i: (0, 0))
    return call_hosting_exchange(
        body, xch, grid=(t // ROW_TM,),
        in_specs=[pl.BlockSpec((ROW_TM, w), lambda i: (i, 0)) for w in PIECE_WIDTHS]
        + [pl.BlockSpec(win_t.shape, lambda i: (0, 0)), row, row, vec],
        out_specs=[row, vec, vec],
        out_shape=[SDS((t, d), F32), SDS((1, d), F32), SDS((1, d), F32)],
        scratch_shapes=[], name="dh_ln_bwd", operands=(*pieces, win_t, du32, x2, g_emb))


def _adamw(w, g, m, v):
    m2 = ADAM_B1 * m + (1.0 - ADAM_B1) * g
    v2 = ADAM_B2 * v + (1.0 - ADAM_B2) * (g * g)
    m_hat = m2 / (1.0 - ADAM_B1 ** ADAM_STEP)
    v_hat = v2 / (1.0 - ADAM_B2 ** ADAM_STEP)
    return -ADAM_LR * (m_hat / (jnp.sqrt(v_hat) + ADAM_EPS) + ADAM_WD * w), m2, v2


def adamw_shard(w, parts, m, v, name):
    r, c = w.shape
    if r % 256 == 0 or r * c <= 256 * 1024:
        tr, tc = min(r, 256), c
    else:
        tr, tc = r, 256

    def body(w_ref, p_ref, m_ref, v_ref, g_ref, d_ref, nm_ref, nv_ref):
        g = p_ref[0].astype(F32)
        for k in range(1, N_DEV):
            g = g + p_ref[k].astype(F32)
        g_ref[...] = g
        d_ref[...], nm_ref[...], nv_ref[...] = _adamw(w_ref[...], g, m_ref[...], v_ref[...])

    blk = pl.BlockSpec((tr, tc), lambda i, j: (i, j))
    return pl.pallas_call(
        body, grid=(r // tr, c // tc),
        in_specs=[blk, pl.BlockSpec((N_DEV, tr, tc), lambda i, j: (0, i, j)), blk, blk],
        out_specs=[blk] * 4, out_shape=[SDS((r, c), F32)] * 4, name=name,
        compiler_params=_params("parallel", "parallel"))(w, parts, m, v)


def _place():
    return lax.axis_index("x"), lax.axis_index("y"), lax.axis_index("c")


def _flat(px, py, pc):
    return 4 * px + 2 * py + pc


def _peer(x, y, c, k):
    return (1 - x if k & 4 else x, 1 - y if k & 2 else y, 1 - c if k & 1 else c)


def cast_shards(shards):
    def body(*refs):
        n = len(refs) // 2
        for i_ref, o_ref in zip(refs[:n], refs[n:]):
            o_ref[...] = i_ref[...].astype(BF16)

    return pl.pallas_call(body, out_shape=[SDS(s.shape, BF16) for s in shards], name="cast_shards",
                          compiler_params=_params())(*shards)


def _two_level_gather_plan(src_refs, land_refs, send_sems, recv_sems, local_sems):
    n = len(src_refs)
    x, y, c = _place()
    me, sib = (x, y, c), (x, y, 1 - c)
    chips = [(1 - x, y), (x, 1 - y), (1 - x, 1 - y)]

    def copy(a, k, block, to, src=None):
        dst = land_refs[a].at[_flat(*block)]
        return pltpu.make_async_remote_copy(
            src_ref=dst if src is None else src, dst_ref=dst,
            send_sem=send_sems.at[a * N_DEV + k], recv_sem=recv_sems.at[a * N_DEV + k],
            device_id=to, device_id_type=MESH)

    mine = [pltpu.make_async_copy(src_refs[a], land_refs[a].at[_flat(*me)], local_sems.at[a]) for a in range(n)]
    first = []
    for a in range(n):
        first.append(copy(a, 0, me, sib, src=src_refs[a]))
        first += [copy(a, 1 + j, me, (*chip, c), src=src_refs[a]) for j, chip in enumerate(chips)]

    def start():
        for cp in mine + first:
            cp.start()

    def finish():
        passed = []
        for j, chip in enumerate(chips):
            for a in range(n):
                copy(a, 1 + j, (*chip, c), me).wait_recv()
                fwd = copy(a, 4 + j, (*chip, c), sib)
                fwd.start()
                passed.append(fwd)
        for a in range(n):
            copy(a, 0, sib, me).wait_recv()
            for j, chip in enumerate(chips):
                copy(a, 4 + j, (*chip, 1 - c), me).wait_recv()
        for cp in first + passed:
            cp.wait_send()
        for cp in mine:
            cp.wait()

    return start, finish


ALL_DEVICES = tuple(range(N_DEV))


def _exchange_plan(src_refs, land_refs, dests, send_sems, recv_sems, local_sems):
    x, y, c = _place()
    me = _flat(x, y, c)
    plan = []
    for a, (src, land, dl) in enumerate(zip(src_refs, land_refs, dests)):
        for li, j in enumerate(dl):
            to = ((j >> 2) & 1, (j >> 1) & 1, j & 1)
            block = src.at[li] if len(src.shape) == len(land.shape) else src

            def push(slot, a=a, block=block, land=land, j=j, to=to):
                return pltpu.make_async_remote_copy(
                    src_ref=block, dst_ref=land.at[slot], send_sem=send_sems.at[a * N_DEV + j],
                    recv_sem=recv_sems.at[a * N_DEV + slot], device_id=to, device_id_type=MESH)

            own = pltpu.make_async_copy(block, land.at[j], local_sems.at[a])
            plan.append((j, push(me), own, [push(s) for s in range(N_DEV) if s != j]))
    return me, plan


def _exchange_start(me, plan):
    for j, send, own, _ in plan:
        @pl.when(me != j)
        def _(send=send):
            send.start()

        @pl.when(me == j)
        def _(own=own):
            own.start()


def _exchange_wait(me, plan):
    for j, send, own, arrivals in plan:
        @pl.when(me != j)
        def _(send=send):
            send.wait_send()

        @pl.when(me == j)
        def _(own=own, arrivals=arrivals):
            own.wait()
            for arrival in arrivals:
                arrival.wait_recv()


def call_hosting_exchange(core, xch, *, grid, in_specs, out_specs, out_shape, scratch_shapes, name, operands):
    srcs, dests, landing = xch
    n, n_in, n_out, n_scr = len(srcs), len(in_specs), len(out_specs), len(scratch_shapes)

    def body(*refs):
        ins, src_refs = refs[:n_in], refs[n_in:n_in + n]
        outs = refs[n_in + 2 * n:n_in + 2 * n + n_out]
        land_refs = refs[n_in + 2 * n + n_out:n_in + 3 * n + n_out]
        scratch = refs[n_in + 3 * n + n_out:n_in + 3 * n + n_out + n_scr]
        sems = refs[n_in + 3 * n + n_out + n_scr:]
        first = functools.reduce(jnp.logical_and, [pl.program_id(i) == 0 for i in range(len(grid))])
        last = functools.reduce(jnp.logical_and, [pl.program_id(i) == grid[i] - 1 for i in range(len(grid))])
        if dests is None:
            start, finish = _two_level_gather_plan(src_refs, land_refs, *sems)
        else:
            me, plan = _exchange_plan(src_refs, land_refs, dests, *sems)
            start, finish = functools.partial(_exchange_start, me, plan), functools.partial(_exchange_wait, me, plan)
        pl.when(first)(start)
        core(*ins, *outs, *scratch)
        pl.when(last)(finish)

    hbm = pl.BlockSpec(memory_space=pl.ANY)
    res = pl.pallas_call(
        body, grid=grid,
        in_specs=list(in_specs) + [hbm] * (2 * n), out_specs=list(out_specs) + [hbm] * n,
        out_shape=list(out_shape) + [SDS(l.shape, l.dtype) for l in landing],
        scratch_shapes=list(scratch_shapes) + [pltpu.SemaphoreType.DMA((N_DEV * n,)), pltpu.SemaphoreType.DMA((N_DEV * n,)),
                                               pltpu.SemaphoreType.DMA((n,))],
        input_output_aliases={n_in + n + k: n_out + k for k in range(n)},
        name=name, compiler_params=_params(*(("arbitrary",) * len(grid))))(*operands, *srcs, *landing)
    return res[:n_out], res[n_out:]


SLOT_ROWS = 8


def small_allreduce_adamw(loss_sum, grads, ws, ms, vs, shards):
    n = len(grads)
    rows = [g.shape[0] for g in grads]
    total = SLOT_ROWS * (n + 1)
    sh_w, sh_p, sh_m, sh_v = shards
    k_sh = len(sh_w)
    n_in, n_out = 1 + 4 * n + 4 * k_sh, 1 + 4 * n + 4 * k_sh

    def body(*refs):
        loss_ref, g_refs, w_refs = refs[0], refs[1:1 + n], refs[1 + n:1 + 2 * n]
        m_refs, v_refs = refs[1 + 2 * n:1 + 3 * n], refs[1 + 3 * n:1 + 4 * n]
        sh = refs[1 + 4 * n:n_in]
        outs = refs[n_in:n_in + 1 + 4 * n]
        sh_outs = refs[n_in + 1 + 4 * n:n_in + n_out]
        vec, gath, tot, send_sems, recv_sems = refs[n_in + n_out:]
        x, y, c = _place()
        me = _flat(x, y, c)
        vec[...] = jnp.zeros_like(vec)
        vec[0:1, :] = loss_ref[...]
        for i in range(n):
            vec[SLOT_ROWS * (i + 1):SLOT_ROWS * (i + 1) + rows[i], :] = g_refs[i][...]
        gath[me] = vec[...]
        copies = []
        for k in range(1, N_DEV):
            peer = _peer(x, y, c, k)
            copies.append(pltpu.make_async_remote_copy(
                src_ref=vec, dst_ref=gath.at[me], send_sem=send_sems.at[k - 1], recv_sem=recv_sems.at[k - 1],
                device_id=peer, device_id_type=MESH))
        for cp in copies:
            cp.start()
        for i in range(k_sh):
            g = sh[k_sh + i][0].astype(F32)
            for k in range(1, N_DEV):
                g = g + sh[k_sh + i][k].astype(F32)
            sh_outs[4 * i][...] = g
            sh_outs[4 * i + 1][...], sh_outs[4 * i + 2][...], sh_outs[4 * i + 3][...] = _adamw(
                sh[i][...], g, sh[2 * k_sh + i][...], sh[3 * k_sh + i][...])
        for cp in copies:
            cp.wait_recv()
        for cp in copies:
            cp.wait_send()
        g = gath[0]
        for j in range(1, N_DEV):
            g = g + gath[j]
        tot[...] = g
        outs[0][...] = tot[0:1, :]
        for i in range(n):
            gi = tot[SLOT_ROWS * (i + 1):SLOT_ROWS * (i + 1) + rows[i], :]
            outs[1 + i][...] = gi
            outs[1 + n + i][...], outs[1 + 2 * n + i][...], outs[1 + 3 * n + i][...] = _adamw(
                w_refs[i][...], gi, m_refs[i][...], v_refs[i][...])

    shapes = [SDS(g.shape, F32) for g in grads]
    res = pl.pallas_call(
        body, out_shape=[SDS((1, LANES), F32)] + shapes * 4 + [SDS(w.shape, F32) for w in sh_w for _ in range(4)],
        scratch_shapes=[pltpu.VMEM((total, LANES), F32), pltpu.VMEM((N_DEV, total, LANES), F32), pltpu.VMEM((total, LANES), F32),
                        pltpu.SemaphoreType.DMA((7,)), pltpu.SemaphoreType.DMA((7,))],
        name="small_allreduce_adamw", compiler_params=_params())(loss_sum, *grads, *ws, *ms, *vs, *sh_w, *sh_p, *sh_m, *sh_v)
    small, rest = res[:1 + 4 * n], res[1 + 4 * n:]
    return small, [rest[4 * i:4 * i + 4] for i in range(k_sh)]


def _rope_lane_patterns():
    inv = lambda r: ROPE_THETA ** (-(jnp.arange(0, r, 2, dtype=F32) / r))
    z = lambda n: jnp.zeros((n,), F32)
    o = lambda n: jnp.ones((n,), F32)
    half, rest = A_ROT // 2, A_HEAD_DIM - A_ROT
    ia, im = inv(A_ROT), inv(MLA_ROPE)
    mh, tail = MLA_ROPE // 2, LANES - MLA_NOPE - MLA_ROPE
    rows = [jnp.tile(jnp.concatenate([ia, ia, z(rest)]), 2),
            jnp.tile(jnp.concatenate([o(half), z(half + rest)]), 2),
            jnp.tile(jnp.concatenate([z(half), o(half), z(rest)]), 2),
            jnp.concatenate([z(MLA_NOPE), im, im, z(tail)]),
            jnp.concatenate([z(MLA_NOPE), o(mh), z(mh + tail)]),
            jnp.concatenate([z(MLA_NOPE + mh), o(mh), z(tail)]),
            z(LANES), z(LANES)]
    return jnp.stack(rows)


KR_LO, KR_HI = 4480, 4512
W_IN_SHARD = D_IN // N_DEV
BG_SPLIT = 6 * W_IN_SHARD - KR_HI


def w_in_working_t(g):
    pad_lo, pad_hi = MLA_NOPE, LANES - MLA_NOPE - MLA_ROPE
    spans = []
    for lo, hi, shift in ((0, KR_LO, 0), (KR_LO, KR_HI, pad_lo), (KR_HI, D_IN, pad_lo + pad_hi)):
        r = lo
        while r < hi:
            j = r // W_IN_SHARD
            n = min(hi, (j + 1) * W_IN_SHARD) - r
            spans.append((j, r - j * W_IN_SHARD, n, r + shift))
            r += n

    def body(g_ref, o_ref):
        o_ref[KR_LO:KR_LO + pad_lo, :] = jnp.zeros((pad_lo, D_MODEL), o_ref.dtype)
        o_ref[KR_HI + pad_lo:KR_HI + pad_lo + pad_hi, :] = jnp.zeros((pad_hi, D_MODEL), o_ref.dtype)
        for j, src, n, dst in spans:
            o_ref[dst:dst + n, :] = g_ref[j, src:src + n, :]

    return pl.pallas_call(body, out_shape=SDS((D_INW, D_MODEL), g.dtype), name="w_in_working_t", compiler_params=_params())(g)


def _w_in_shard_5(d_ag_tail, d_cc, d_bg_head):
    kr = MLA_Q_RANK + MLA_KV_RANK + MLA_NOPE
    rows = jnp.concatenate([d_ag_tail, d_cc[:MLA_Q_RANK + MLA_KV_RANK], d_cc[kr:kr + MLA_ROPE], d_bg_head], 0)
    return rows.reshape(1, W_IN_SHARD, D_MODEL).astype(BF16)


def _w_uq_working(g):
    w = jnp.pad(g.transpose(1, 0, 2), ((0, 0), (0, 0), (0, LANES - MLA_NOPE - MLA_ROPE)))
    return w.reshape(MLA_Q_RANK, MLA_QW)


def _w_uq_parts(dw):
    return dw.reshape(MLA_Q_RANK, MLA_HEADS, LANES)[:, :, :MLA_NOPE + MLA_ROPE].transpose(1, 0, 2)


def _w_ukv_working(g):
    wk = jnp.pad(g[:, :, :MLA_NOPE].transpose(1, 0, 2), ((0, 0), (0, 0), (0, LANES - MLA_NOPE)))
    wv = g[:, :, MLA_NOPE:].transpose(1, 0, 2)
    return jnp.concatenate([wk.reshape(MLA_KV_RANK, MLA_QW), wv.reshape(MLA_KV_RANK, MLA_WIDTH)], 1)


def _w_ukv_parts(dw):
    dk = dw[:, :MLA_QW].reshape(MLA_KV_RANK, MLA_HEADS, LANES)[:, :, :MLA_NOPE]
    dv = dw[:, MLA_QW:].reshape(MLA_KV_RANK, MLA_HEADS, MLA_V)
    return jnp.concatenate([dk, dv], -1).transpose(1, 0, 2)


SMALL_NAMES = ("g_emb", "b_emb", "g_cq", "g_ckv", "g_out_a", "g_out_b", "g_out_m", "g_post", "b_post")


def kernel(x, mem, positions, g_emb, b_emb, w_in, g_cq, g_ckv, w_uq, w_ukv, w_mem_kv, g_out_a, g_out_b, g_out_m, w_out, g_post, b_post, loss_target, m_g_emb, m_b_emb, m_w_in, m_g_cq, m_g_ckv, m_w_uq, m_w_ukv, m_w_mem_kv, m_g_out_a, m_g_out_b, m_g_out_m, m_w_out, m_g_post, m_b_post, v_g_emb, v_b_emb, v_w_in, v_g_cq, v_g_ckv, v_w_uq, v_w_ukv, v_w_mem_kv, v_g_out_a, v_g_out_b, v_g_out_m, v_w_out, v_g_post, v_b_post):
    nb = x.shape[0]
    t = nb * SEQ
    x2 = x.reshape(t, D_MODEL)
    tgt2 = loss_target.reshape(t, D_MODEL)
    mem2 = mem.reshape(nb * N_MEM, D_MODEL)
    g_emb2, b_emb2 = g_emb.reshape(1, -1), b_emb.reshape(1, -1)

    w_in_t, m_w_in_t, v_w_in_t = w_in[0].T, m_w_in[0].T, v_w_in[0].T
    s_in, s_uq, s_ukv, s_mem, s_out = cast_shards((w_in_t, w_uq[0], w_ukv[0], w_mem_kv[0], w_out[0]))
    (h32, h16, (a_c, a_sa, a_sb), (m_c, m_sa, m_sb)), (g_in,) = embed_fwd(
        x2, g_emb2, b_emb2, positions, ((s_in,), None, (lax.empty((N_DEV,) + s_in.shape, BF16),)))
    win_t = w_in_working_t(g_in)

    proj = mm_nn(h16, win_t, F32, 2048, 1536, "proj", rhs_transposed=True)
    later = (s_uq, s_ukv, s_mem, s_out)
    (ya, lse_a), qkv_d, (g_uq, g_ukv, g_mem, g_out) = a_attn_fwd(
        proj, a_c, a_sa, a_sb, nb,
        (later, (ALL_DEVICES,) * len(later), tuple(lax.empty((N_DEV,) + w.shape, BF16) for w in later)))
    wuq_w = _w_uq_working(g_uq)
    wkv_w = _w_ukv_working(g_ukv)
    wmem = g_mem.reshape(D_MODEL, 2 * MEM_WIDTH)
    wout = g_out.reshape(D_MIX, D_MODEL)
    qb, kb, vb = mla_prep_fwd(proj, m_c, m_sa, m_sb, g_cq, g_ckv, wuq_w, wkv_w)
    yb, lse_b = mla_attn_fwd(qb, kb, vb, nb)
    mkv = mm_nn(mem2, wmem, BF16, nb * N_MEM, 512, "mem_kv")
    ym = mem_attn_fwd(proj, mkv, nb)
    z, du32, du16, loss_sum, dg_post, db_post = gate_out_ln_loss(
        ya, yb, ym, proj, g_out_a, g_out_b, g_out_m, wout, h32, tgt2, g_post, b_post)

    dya, dyb, dym, dag, dbg, dmg, dg_out_a, dg_out_b, dg_out_m = gate_bwd(
        du16, wout, ya, yb, ym, proj, g_out_a, g_out_b, g_out_m)
    dw_out = mm_tn(z, du16, 1024, "dw_out")
    dmq, dmk, dmv = mem_attn_bwd(proj, mkv, dym, nb)
    dw_mem = mm_tn(mem2, jnp.concatenate([dmk, dmv], 1), nb * N_MEM, "dw_mem")
    shards_6_7, d_bg_head = mm_tn_group((dbg, dmq, dmg), h16, 2048, "dw_in_bg_mq_mg", W_IN_SHARD, BG_SPLIT, 2, (0, BG_SPLIT))
    landing = lambda w, dtype=F32: lax.empty((N_DEV,) + w.shape, dtype)
    big_w = (w_in_t, w_uq[0], w_ukv[0], w_mem_kv[0], w_out[0])
    (daq, dak, dav), (p_out, p_mem, p_in) = a_attn_bwd(
        qkv_d, a_c, a_sa, a_sb, dya, ya, lse_a, nb,
        ((dw_out.reshape(N_DEV, D_MIX // N_DEV, D_MODEL), dw_mem.reshape(N_DEV, D_MODEL // N_DEV, 2 * MEM_WIDTH),
          shards_6_7),
         (ALL_DEVICES, ALL_DEVICES, (6, 7)),
         (landing(w_out[0]), landing(w_mem_kv[0]), landing(w_in_t, BF16))))
    shards_0_4, d_ag_tail = mm_tn_group((daq, dak, dav, dag), h16, 1024, "dw_in_aq_ak_av_ag", W_IN_SHARD, 0, 5,
                                        (5 * W_IN_SHARD, 4 * A_WIDTH))
    (dqb, dkb, dvb), (p_in,) = mla_attn_bwd(
        qb, kb, vb, dyb, yb, lse_b, nb, ((shards_0_4,), ((0, 1, 2, 3, 4),), (p_in,)))
    dcc, dqf, cqn, dkvf, ckvn, dg_cq, dg_ckv = mla_prep_bwd(proj, m_c, m_sa, m_sb, g_cq, g_ckv, wuq_w, wkv_w, dqb, dkb, dvb)
    dw_uq, dw_ukv, d_cc = mm_tn_pairs(((cqn, dqf), (ckvn, dkvf), (dcc, h16)), 1024, "dw_uq_ukv_cc")
    pieces = (daq, dak, dav, dag, dcc, dbg, dmq, dmg)
    (grad_x, dg_emb, db_emb), (p_in, p_uq, p_ukv) = dh_ln_bwd(
        pieces, win_t, du32, x2, g_emb2,
        ((_w_in_shard_5(d_ag_tail, d_cc, d_bg_head), _w_uq_parts(dw_uq), _w_ukv_parts(dw_ukv)),
         ((5,), ALL_DEVICES, ALL_DEVICES),
         (p_in, landing(w_uq[0]), landing(w_ukv[0]))))

    parts = (p_in, p_uq, p_ukv, p_mem, p_out)
    big_m = (m_w_in_t, m_w_uq[0], m_w_ukv[0], m_w_mem_kv[0], m_w_out[0])
    big_v = (v_w_in_t, v_w_uq[0], v_w_ukv[0], v_w_mem_kv[0], v_w_out[0])
    big = {"w_in": [o.T[None] for o in adamw_shard(big_w[0], parts[0], big_m[0], big_v[0], "adamw_w_in")]}

    small_w = (g_emb, b_emb, g_cq, g_ckv, g_out_a, g_out_b, g_out_m, g_post, b_post)
    small_m = (m_g_emb, m_b_emb, m_g_cq, m_g_ckv, m_g_out_a, m_g_out_b, m_g_out_m, m_g_post, m_b_post)
    small_v = (v_g_emb, v_b_emb, v_g_cq, v_g_ckv, v_g_out_a, v_g_out_b, v_g_out_m, v_g_post, v_b_post)
    small_g = (dg_emb, db_emb, dg_cq, dg_ckv, dg_out_a, dg_out_b, dg_out_m, dg_post, db_post)
    rows128 = lambda vals: [v.reshape(-1, LANES) for v in vals]
    res, rest = small_allreduce_adamw(loss_sum, rows128(small_g), rows128(small_w), rows128(small_m), rows128(small_v),
                                      (big_w[1:], parts[1:], big_m[1:], big_v[1:]))
    for name, out4 in zip(("w_uq", "w_ukv", "w_mem_kv", "w_out"), rest):
        big[name] = [o[None] for o in out4]
    loss = res[0][0, 0]
    n_small = len(small_w)
    sg, sd, sm, sv = [[r.reshape(w.shape) for r, w in zip(res[1 + k * n_small:1 + (k + 1) * n_small], small_w)]
                      for k in range(4)]

    order = ("g_emb", "b_emb", "w_in", "g_cq", "g_ckv", "w_uq", "w_ukv", "w_mem_kv", "g_out_a", "g_out_b", "g_out_m",
             "w_out", "g_post", "b_post")
    small_idx = {n: i for i, n in enumerate(SMALL_NAMES)}
    outs = [loss, grad_x.reshape(x.shape)]
    for kind in range(4):
        for name in order:
            outs.append(big[name][kind] if name in big else (sg, sd, sm, sv)[kind][small_idx[name]])
    return tuple(outs)
```

```python
import functools

import jax
import jax.numpy as jnp
from jax import lax
from jax.experimental import pallas as pl
from jax.experimental.pallas import tpu as pltpu

F32 = jnp.float32
BF16 = jnp.bfloat16
SDS = jax.ShapeDtypeStruct
MESH = pl.DeviceIdType.MESH

D_MODEL = 1024
SEQ = 2048
A_HEADS, A_HEAD_DIM, A_ROT = 16, 64, 16
A_WIDTH = 1024
DILATIONS = (1, 4, 16)
N_SIDE = 64
MLA_HEADS, MLA_Q_RANK, MLA_KV_RANK = 8, 256, 128
MLA_NOPE, MLA_ROPE, MLA_V = 64, 32, 64
MLA_WIDTH = 512
N_MEM, MEM_HEADS, MEM_HEAD_DIM, MEM_WIDTH = 256, 4, 128, 512
ROPE_THETA = 500000.0
NORM_EPS = 1e-5
NEG_INF = -1e30
ALPHA = 2.0 ** 0.25
D_IN = 6048
N_DEV = 8

ADAM_LR, ADAM_B1, ADAM_B2, ADAM_EPS, ADAM_WD, ADAM_STEP = 0.001, 0.9, 0.999, 1e-08, 0.01, 10

D_INW = 6144
PIECE_WIDTHS = (1024, 1024, 1024, 1024, 512, 512, 512, 512)
PIECE_OFFS = (0, 1024, 2048, 3072, 4096, 4608, 5120, 5632)
LANES = 128
VMEM_LIMIT = 56 * 1024 * 1024


def _params(*sem):
    kw = dict(vmem_limit_bytes=VMEM_LIMIT)
    if sem:
        kw["dimension_semantics"] = sem
    return pltpu.CompilerParams(**kw)


def _dot(a, b):
    return jnp.dot(a, b, preferred_element_type=F32)


def _dot_nt(a, b):
    return lax.dot_general(a, b, (((1,), (1,)), ((), ())), preferred_element_type=F32)


def _dot_tn(a, b):
    return lax.dot_general(a, b, (((0,), (0,)), ((), ())), preferred_element_type=F32)


def _sigmoid(x):
    return 1.0 / (1.0 + jnp.exp(-x))


def _rope_fwd(x, c, sa, sb, half):
    n = x.shape[-1]
    return x * c + pltpu.roll(x, n - half, 1) * sa + pltpu.roll(x, half, 1) * sb


def _rope_bwd(dy, c, sa, sb, half):
    n = dy.shape[-1]
    return dy * c + pltpu.roll(dy * sa, half, 1) + pltpu.roll(dy * sb, n - half, 1)


def mm_nn(a, b, out_dtype, tm, tn, name, rhs_transposed=False):
    m, k = a.shape
    n = b.shape[0] if rhs_transposed else b.shape[1]
    dot = _dot_nt if rhs_transposed else _dot

    def body(a_ref, b_ref, o_ref):
        o_ref[...] = dot(a_ref[...].astype(BF16), b_ref[...].astype(BF16)).astype(o_ref.dtype)

    b_spec = pl.BlockSpec((tn, k), lambda j, i: (j, 0)) if rhs_transposed else pl.BlockSpec((k, tn), lambda j, i: (0, j))
    return pl.pallas_call(
        body, grid=(n // tn, m // tm),
        in_specs=[pl.BlockSpec((tm, k), lambda j, i: (i, 0)), b_spec],
        out_specs=pl.BlockSpec((tm, tn), lambda j, i: (i, j)),
        out_shape=SDS((m, n), out_dtype), name=name,
        compiler_params=_params("parallel", "parallel"))(a, b)


def mm_tn(a, b, tt, name):
    t, m = a.shape
    n = b.shape[1]

    def body(a_ref, b_ref, o_ref):
        @pl.when(pl.program_id(0) == 0)
        def _():
            o_ref[...] = jnp.zeros_like(o_ref)

        o_ref[...] += _dot_tn(a_ref[...].astype(BF16), b_ref[...].astype(BF16))

    return pl.pallas_call(
        body, grid=(t // tt,),
        in_specs=[pl.BlockSpec((tt, m), lambda i: (i, 0)), pl.BlockSpec((tt, n), lambda i: (i, 0))],
        out_specs=pl.BlockSpec((m, n), lambda i: (0, 0)),
        out_shape=SDS((m, n), F32), name=name,
        compiler_params=_params("arbitrary"))(a, b)


def mm_tn_pairs(pairs, tt, name):
    n = len(pairs)
    t = pairs[0][0].shape[0]

    def body(*refs):
        @pl.when(pl.program_id(0) == 0)
        def _():
            for o_ref in refs[2 * n:]:
                o_ref[...] = jnp.zeros_like(o_ref)

        for i in range(n):
            refs[2 * n + i][...] += _dot_tn(refs[2 * i][...].astype(BF16), refs[2 * i + 1][...].astype(BF16))

    rows = lambda x: pl.BlockSpec((tt, x.shape[1]), lambda i: (i, 0))
    return pl.pallas_call(
        body, grid=(t // tt,),
        in_specs=[rows(x) for pair in pairs for x in pair],
        out_specs=[pl.BlockSpec((a.shape[1], b.shape[1]), lambda i: (0, 0)) for a, b in pairs],
        out_shape=[SDS((a.shape[1], b.shape[1]), F32) for a, b in pairs], name=name,
        compiler_params=_params("arbitrary"))(*[x for pair in pairs for x in pair])


def mm_tn_group(pieces, b, tt, name, slab_rows, first_slab_row, n_slabs, keep_rows):
    n, (t, w), cols = len(pieces), pieces[0].shape, b.shape[1]
    nt = t // tt
    keep_lo, keep_hi = keep_rows

    def body(*refs):
        p_refs, b_ref, slab_ref, keep_ref, acc = refs[:n], refs[n], refs[n + 1], refs[n + 2], refs[n + 3]

        @pl.when(pl.program_id(1) == 0)
        def _():
            acc[...] = jnp.zeros_like(acc)

        for k in range(n):
            @pl.when(pl.program_id(0) == k)
            def _(k=k):
                acc[...] += _dot_tn(p_refs[k][...], b_ref[...])

            @pl.when((pl.program_id(0) == k) & (pl.program_id(1) == nt - 1))
            def _(k=k):
                for j in range(n_slabs):
                    lo = max(k * w, first_slab_row + j * slab_rows)
                    hi = min((k + 1) * w, first_slab_row + (j + 1) * slab_rows)
                    if lo < hi:
                        dst = lo - first_slab_row - j * slab_rows
                        slab_ref[j, dst:dst + hi - lo, :] = acc[lo - k * w:hi - k * w, :].astype(slab_ref.dtype)
                lo, hi = max(k * w, keep_lo), min((k + 1) * w, keep_hi)
                if lo < hi:
                    keep_ref[lo - keep_lo:hi - keep_lo, :] = acc[lo - k * w:hi - k * w, :]

    def piece_spec(k):
        return pl.BlockSpec((tt, w), lambda p, i: (jnp.where(p < k, 0, jnp.where(p > k, nt - 1, i)), 0))

    return pl.pallas_call(
        body, grid=(n, nt),
        in_specs=[piece_spec(k) for k in range(n)] + [pl.BlockSpec((tt, cols), lambda p, i: (i, 0))],
        out_specs=[pl.BlockSpec((n_slabs, slab_rows, cols), lambda p, i: (0, 0, 0)),
                   pl.BlockSpec((keep_hi - keep_lo, cols), lambda p, i: (0, 0))],
        out_shape=[SDS((n_slabs, slab_rows, cols), BF16), SDS((keep_hi - keep_lo, cols), F32)],
        scratch_shapes=[pltpu.VMEM((w, cols), F32)], name=name,
        compiler_params=_params("arbitrary", "arbitrary"))(*pieces, b)


def embed_fwd(x2, g, b, positions, xch):
    t, d = x2.shape
    tm = 512
    pos = positions.astype(F32).reshape(-1, 1)

    def body(x_ref, g_ref, b_ref, pos_ref, pat_ref, h32_ref, h16_ref, *tabs):
        x = x_ref[...]
        mu = jnp.mean(x, axis=-1, keepdims=True)
        xc = x - mu
        var = jnp.mean(xc * xc, axis=-1, keepdims=True)
        h = xc * lax.rsqrt(var + NORM_EPS) * g_ref[...] + b_ref[...]
        h32_ref[...] = h
        h16_ref[...] = h.astype(BF16)
        p = pos_ref[...]
        for k in range(2):
            inv, first, second = pat_ref[3 * k:3 * k + 1, :], pat_ref[3 * k + 1:3 * k + 2, :], pat_ref[3 * k + 2:3 * k + 3, :]
            ang = p * inv
            sn = jnp.sin(ang)
            tabs[3 * k][...] = jnp.where(first + second > 0.0, jnp.cos(ang), 1.0)
            tabs[3 * k + 1][...] = -first * sn
            tabs[3 * k + 2][...] = second * sn

    row = pl.BlockSpec((tm, d), lambda i: (i, 0))
    vec = pl.BlockSpec((1, d), lambda i: (0, 0))
    tab = pl.BlockSpec((tm, LANES), lambda i: (i, 0))
    res, landed = call_hosting_exchange(
        body, xch, grid=(t // tm,),
        in_specs=[row, vec, vec, pl.BlockSpec((tm, 1), lambda i: (i, 0)), pl.BlockSpec((8, LANES), lambda i: (0, 0))],
        out_specs=[row, row] + [tab] * 6,
        out_shape=[SDS((t, d), F32), SDS((t, d), BF16)] + [SDS((t, LANES), F32)] * 6,
        scratch_shapes=[], name="embed_fwd", operands=(x2, g, b, pos, _rope_lane_patterns()))
    return (res[0], res[1], tuple(res[2:5]), tuple(res[5:8])), landed


Q_BLK = 128
UNROLL_FWD = 16
UNROLL_BWD = 16


def _pattern_geometry(d):
    length = SEQ // d
    nblk = length // Q_BLK
    kwin = min(2 * Q_BLK, length)
    return length, nblk, kwin


def _block_coords(idx, d):
    length, nblk, kwin = _pattern_geometry(d)
    r = lax.shift_right_logical(idx, nblk.bit_length() - 1)
    i = idx & (nblk - 1)
    q0 = pl.multiple_of(r * length + i * Q_BLK, Q_BLK)
    ks = jnp.clip(i * Q_BLK - N_SIDE, 0, length - kwin)
    k0 = pl.multiple_of(r * length + ks, N_SIDE)
    qpos = i * Q_BLK + lax.broadcasted_iota(jnp.int32, (Q_BLK, kwin), 0)
    kpos = ks + lax.broadcasted_iota(jnp.int32, (Q_BLK, kwin), 1)
    valid = jnp.abs(kpos - qpos) <= N_SIDE
    return q0, k0, kwin, valid


def _deinterleave(src_ref, dst_ref, d, dtype, tmp_ref):
    if d == 1:
        dst_ref[...] = src_ref[...].astype(dtype)
        return
    q = SEQ // 4
    if d == 4:
        for r in range(4):
            dst_ref[r * q:(r + 1) * q, :] = src_ref[pl.ds(r, q, stride=4), :].astype(dtype)
        return
    assert d == 16
    n = SEQ // 16
    for r in range(4):
        tmp_ref[r * q:(r + 1) * q, :] = src_ref[pl.ds(r, q, stride=4), :]
    for r in range(4):
        for j in range(4):
            dst_ref[(r + 4 * j) * n:(r + 4 * j + 1) * n, :] = tmp_ref[pl.ds(r * q + j, n, stride=4), :].astype(dtype)


def _class16_to_class4(src_ref, dst_ref):
    q, n = SEQ // 4, SEQ // 16
    for r in range(4):
        for j in range(4):
            dst_ref[pl.ds(r * q + j, n, stride=4), :] = src_ref[(r + 4 * j) * n:(r + 4 * j + 1) * n, :]


def _interleave(src_ref, dst_ref, d, tmp_ref, accumulate):
    q = SEQ // 4
    if d == 16:
        _class16_to_class4(src_ref, tmp_ref)
        src_ref = tmp_ref
    else:
        assert d == 4
    for r in range(4):
        rows = pl.ds(r, q, stride=4)
        val = src_ref[r * q:(r + 1) * q, :]
        dst_ref[rows, :] = dst_ref[rows, :] + val if accumulate else val


def a_attn_fwd(proj, ca, sa, sb, nb, xch):
    t = proj.shape[0]
    n_pairs = A_WIDTH // LANES

    def body(q_ref, k_ref, v_ref, c_ref, sa_ref, sb_ref, y_ref, lse_ref, *rest):
        qkv_d, (qr_s, kr_s, oc_s, lc_s, o1_s, l1_s, o2_s, l2_s, o3_s, l3_s, tmp_s) = rest[:9], rest[9:]
        c, s_a, s_b = c_ref[...], sa_ref[...], sb_ref[...]
        qr_s[...] = _rope_fwd(q_ref[...], c, s_a, s_b, A_ROT // 2) * (A_HEAD_DIM ** -0.5)
        kr_s[...] = _rope_fwd(k_ref[...], c, s_a, s_b, A_ROT // 2)
        head0 = lax.broadcasted_iota(jnp.int32, (Q_BLK, LANES), 1) < A_HEAD_DIM
        nat = ((o1_s, l1_s), (o2_s, l2_s), (o3_s, l3_s))

        for g, d in enumerate(DILATIONS):
            qd_s, kd_s, vd_s = qkv_d[3 * g:3 * g + 3]
            _deinterleave(qr_s, qd_s, d, BF16, tmp_s)
            _deinterleave(kr_s, kd_s, d, BF16, tmp_s)
            _deinterleave(v_ref, vd_s, d, BF16, tmp_s)
            o_dst, l_dst = (nat[g] if d == 1 else (oc_s, lc_s))

            def block(idx, carry, d=d, o_dst=o_dst, l_dst=l_dst, qd_s=qd_s, kd_s=kd_s, vd_s=vd_s):
                q0, k0, kwin, valid = _block_coords(idx, d)
                qb = qd_s[pl.ds(q0, Q_BLK), :]
                kb = kd_s[pl.ds(k0, kwin), :]
                vb = vd_s[pl.ds(k0, kwin), :]
                zero = jnp.zeros_like(qb)
                q2 = jnp.concatenate([jnp.where(head0, qb, zero), jnp.where(head0, zero, qb)], 0)
                s = jnp.where(jnp.concatenate([valid, valid], 0), _dot_nt(q2, kb), NEG_INF)
                m = jnp.max(s, axis=-1, keepdims=True)
                p = jnp.exp(s - m)
                l = jnp.sum(p, axis=-1, keepdims=True)
                o2 = _dot(p.astype(BF16), vb) / l
                l2 = m + jnp.log(l)
                o_dst[pl.ds(q0, Q_BLK), :] = jnp.where(head0, o2[:Q_BLK], o2[Q_BLK:])
                l_dst[pl.ds(q0, Q_BLK), :] = jnp.where(head0, l2[:Q_BLK], l2[Q_BLK:])
                return carry

            lax.fori_loop(0, SEQ // Q_BLK, block, 0, unroll=UNROLL_FWD)
            if d > 1:
                _interleave(oc_s, nat[g][0], d, tmp_s, False)
                _interleave(lc_s, nat[g][1], d, tmp_s, False)

        def merge(ci, carry):
            rows = pl.ds(pl.multiple_of(ci * 256, 256), 256)
            l1, l2, l3 = l1_s[rows, :], l2_s[rows, :], l3_s[rows, :]
            m = jnp.maximum(jnp.maximum(l1, l2), l3)
            w1, w2, w3 = jnp.exp(l1 - m), jnp.exp(l2 - m), jnp.exp(l3 - m)
            w = w1 + w2 + w3
            y_ref[rows, :] = (w1 * o1_s[rows, :] + w2 * o2_s[rows, :] + w3 * o3_s[rows, :]) / w
            lse_ref[rows, :] = m + jnp.log(w)
            return carry

        lax.fori_loop(0, SEQ // 256, merge, 0)

    def col(off):
        return pl.BlockSpec((SEQ, LANES), lambda b, hp: (b, off + hp))

    tab = pl.BlockSpec((SEQ, LANES), lambda b, hp: (b, 0))
    out = pl.BlockSpec((SEQ, LANES), lambda b, hp: (b, hp))
    f32s = pltpu.VMEM((SEQ, LANES), F32)
    res, landed = call_hosting_exchange(
        body, xch, grid=(nb, n_pairs),
        in_specs=[col(0), col(n_pairs), col(2 * n_pairs), tab, tab, tab],
        out_specs=[out] * 11,
        out_shape=[SDS((t, A_WIDTH), F32)] * 2 + [SDS((t, A_WIDTH), BF16)] * 9,
        scratch_shapes=[f32s] * 11,
        name="a_attn_fwd", operands=(proj, proj, proj, ca, sa, sb))
    return res[:2], res[2:], landed


def a_attn_bwd(qkv_d, ca, sa, sb, dy, y, lse, nb, xch):
    t = dy.shape[0]
    n_pairs = A_WIDTH // LANES

    def body(*refs):
        qkv_refs = refs[:9]
        (c_ref, sa_ref, sb_ref, do_ref, y_ref, lse_ref, dq_ref, dk_ref, dv_ref,
         l0n_s, l1n_s, d0n_s, d1n_s, dod_s, l0d_s, l1d_s, d0d_s, d1d_s,
         dqc_s, dkc_s, dvc_s, dq4_s, dk4_s, dv4_s, dqn_s, dkn_s, dvn_s, tmp_s) = refs[9:]
        c, s_a, s_b = c_ref[...], sa_ref[...], sb_ref[...]
        head0 = lax.broadcasted_iota(jnp.int32, (Q_BLK, LANES), 1) < A_HEAD_DIM

        def per_head_rows(ci, carry):
            rows = pl.ds(pl.multiple_of(ci * 256, 256), 256)
            h0 = lax.broadcasted_iota(jnp.int32, (256, LANES), 1) < A_HEAD_DIM
            tt = do_ref[rows, :] * y_ref[rows, :]
            d0n_s[rows, :] = jnp.broadcast_to(jnp.sum(jnp.where(h0, tt, 0.0), axis=-1, keepdims=True), (256, LANES))
            d1n_s[rows, :] = jnp.broadcast_to(jnp.sum(jnp.where(h0, 0.0, tt), axis=-1, keepdims=True), (256, LANES))
            l = lse_ref[rows, :]
            lr = pltpu.roll(l, A_HEAD_DIM, 1)
            l0n_s[rows, :] = jnp.where(h0, l, lr)
            l1n_s[rows, :] = jnp.where(h0, lr, l)
            return carry

        lax.fori_loop(0, SEQ // 256, per_head_rows, 0)
        assert DILATIONS == (1, 4, 16)

        for g, d in enumerate(DILATIONS):
            qd_s, kd_s, vd_s = qkv_refs[3 * g:3 * g + 3]
            _deinterleave(do_ref, dod_s, d, BF16, tmp_s)
            if d > 1:
                for src, dst in ((l0n_s, l0d_s), (l1n_s, l1d_s), (d0n_s, d0d_s), (d1n_s, d1d_s)):
                    _deinterleave(src, dst, d, F32, tmp_s)
            l0, l1, d0, d1 = (l0n_s, l1n_s, d0n_s, d1n_s) if d == 1 else (l0d_s, l1d_s, d0d_s, d1d_s)
            dq_dst, dk_dst, dv_dst = {1: (dqn_s, dkn_s, dvn_s), 4: (dq4_s, dk4_s, dv4_s), 16: (dqc_s, dkc_s, dvc_s)}[d]
            dk_dst[...] = jnp.zeros_like(dk_dst)
            dv_dst[...] = jnp.zeros_like(dv_dst)

            def block(idx, carry, d=d, l0=l0, l1=l1, d0=d0, d1=d1, dq_dst=dq_dst, dk_dst=dk_dst, dv_dst=dv_dst,
                      qd_s=qd_s, kd_s=kd_s, vd_s=vd_s):
                q0, k0, kwin, valid = _block_coords(idx, d)
                qrows = pl.ds(q0, Q_BLK)
                krows = pl.ds(k0, kwin)
                qb, dob = qd_s[qrows, :], dod_s[qrows, :]
                kb, vb = kd_s[krows, :], vd_s[krows, :]
                zero = jnp.zeros_like(qb)
                q2 = jnp.concatenate([jnp.where(head0, qb, zero), jnp.where(head0, zero, qb)], 0)
                do2 = jnp.concatenate([jnp.where(head0, dob, zero), jnp.where(head0, zero, dob)], 0)
                wide = lambda x: jnp.concatenate([x] * (kwin // LANES), 1)
                lse2 = wide(jnp.concatenate([l0[qrows, :], l1[qrows, :]], 0))
                dd2 = wide(jnp.concatenate([d0[qrows, :], d1[qrows, :]], 0))
                s = jnp.where(jnp.concatenate([valid, valid], 0), _dot_nt(q2, kb), NEG_INF)
                p = jnp.exp(s - lse2)
                ds = (p * (_dot_nt(do2, vb) - dd2)).astype(BF16)
                dq2 = _dot(ds, kb)
                dq_dst[qrows, :] = jnp.where(head0, dq2[:Q_BLK], dq2[Q_BLK:])
                dk_dst[krows, :] += _dot_tn(ds, q2)
                dv_dst[krows, :] += _dot_tn(p.astype(BF16), do2)
                return carry

            lax.fori_loop(0, SEQ // Q_BLK, block, 0, unroll=UNROLL_BWD)

        for c16, c4, nat in ((dqc_s, dq4_s, dqn_s), (dkc_s, dk4_s, dkn_s), (dvc_s, dv4_s, dvn_s)):
            _class16_to_class4(c16, tmp_s)
            c4[...] = c4[...] + tmp_s[...]
            _interleave(c4, nat, 4, tmp_s, True)

        dq_ref[...] = _rope_bwd(dqn_s[...] * (A_HEAD_DIM ** -0.5), c, s_a, s_b, A_ROT // 2).astype(BF16)
        dk_ref[...] = _rope_bwd(dkn_s[...], c, s_a, s_b, A_ROT // 2).astype(BF16)
        dv_ref[...] = dvn_s[...].astype(BF16)

    tab = pl.BlockSpec((SEQ, LANES), lambda b, hp: (b, 0))
    blk = pl.BlockSpec((SEQ, LANES), lambda b, hp: (b, hp))
    f32s = pltpu.VMEM((SEQ, LANES), F32)
    b16s = pltpu.VMEM((SEQ, LANES), BF16)
    return call_hosting_exchange(
        body, xch, grid=(nb, n_pairs),
        in_specs=[blk] * 9 + [tab, tab, tab, blk, blk, blk],
        out_specs=[blk, blk, blk],
        out_shape=[SDS((t, A_WIDTH), BF16)] * 3,
        scratch_shapes=[f32s] * 4 + [b16s] + [f32s] * 14,
        name="a_attn_bwd", operands=(*qkv_d, ca, sa, sb, dy, y, lse))


MLA_SCALE = (MLA_NOPE + MLA_ROPE) ** -0.5
LOG2E = 1.4426950408889634
MLA_QW = MLA_HEADS * LANES
MLA_KVW = MLA_QW + MLA_WIDTH


def _rms(x, g):
    r = lax.rsqrt(jnp.mean(x * x, axis=-1, keepdims=True) + NORM_EPS)
    return x * r * g, r


def _rms_bwd(dn, x, r, g):
    tg = dn * g
    dx = r * tg - x * (r * r * r) * jnp.mean(tg * x, axis=-1, keepdims=True)
    return dx, jnp.sum(dn * x * r, axis=0, keepdims=True)


def mla_prep_fwd(proj, cm, sma, smb, g_cq, g_ckv, wuq, wkv):
    t = proj.shape[0]
    tm = 1024

    def body(cq_ref, ckv_ref, kr_ref, c_ref, sa_ref, sb_ref, gq_ref, gkv_ref, wuq_ref, wkv_ref, q_ref, k_ref, v_ref):
        c, s_a, s_b = c_ref[...], sa_ref[...], sb_ref[...]
        cqn, _ = _rms(cq_ref[...], gq_ref[...])
        qf = _dot(cqn.astype(BF16), wuq_ref[...])
        ckvn, _ = _rms(ckv_ref[...], gkv_ref[...])
        kvf = _dot(ckvn.astype(BF16), wkv_ref[...])
        krope = _rope_fwd(kr_ref[...], c, s_a, s_b, MLA_ROPE // 2)
        for h in range(MLA_HEADS):
            cols = slice(h * LANES, (h + 1) * LANES)
            q_ref[:, cols] = (_rope_fwd(qf[:, cols], c, s_a, s_b, MLA_ROPE // 2) * (MLA_SCALE * LOG2E)).astype(BF16)
            k_ref[:, cols] = (kvf[:, cols] + krope).astype(BF16)
        v_ref[...] = kvf[:, MLA_QW:].astype(BF16)

    def row(w, j):
        return pl.BlockSpec((tm, w), lambda i: (i, j))

    def full(a):
        return pl.BlockSpec(a.shape, lambda i: (0, 0))

    return pl.pallas_call(
        body, grid=(t // tm,),
        in_specs=[row(256, 4096 // 256), row(128, 4352 // 128), row(128, 4480 // 128), row(128, 0), row(128, 0), row(128, 0),
                  full(g_cq), full(g_ckv), full(wuq), full(wkv)],
        out_specs=[row(MLA_QW, 0), row(MLA_QW, 0), row(MLA_WIDTH, 0)],
        out_shape=[SDS((t, MLA_QW), BF16), SDS((t, MLA_QW), BF16), SDS((t, MLA_WIDTH), BF16)],
        name="mla_prep_fwd", compiler_params=_params("parallel"))(proj, proj, proj, cm, sma, smb, g_cq, g_ckv, wuq, wkv)


def mla_prep_bwd(proj, cm, sma, smb, g_cq, g_ckv, wuq, wkv, dq, dk, dv):
    t = proj.shape[0]
    tm = 1024

    def body(cq_ref, ckv_ref, c_ref, sa_ref, sb_ref, gq_ref, gkv_ref, wuq_ref, wkv_ref, dq_ref, dk_ref, dv_ref,
             dcc_ref, dqf_ref, cqn_ref, dkvf_ref, ckvn_ref, dgq_ref, dgkv_ref):
        @pl.when(pl.program_id(0) == 0)
        def _():
            dgq_ref[...] = jnp.zeros_like(dgq_ref)
            dgkv_ref[...] = jnp.zeros_like(dgkv_ref)

        c, s_a, s_b = c_ref[...], sa_ref[...], sb_ref[...]
        cq, ckv = cq_ref[...], ckv_ref[...]
        cqn, rq = _rms(cq, gq_ref[...])
        ckvn, rkv = _rms(ckv, gkv_ref[...])
        cqn_ref[...] = cqn.astype(BF16)
        ckvn_ref[...] = ckvn.astype(BF16)
        lane = lax.broadcasted_iota(jnp.int32, (tm, LANES), 1)
        rope_lanes = (lane >= MLA_NOPE) & (lane < MLA_NOPE + MLA_ROPE)
        dkrope = jnp.zeros((tm, LANES), F32)
        for h in range(MLA_HEADS):
            cols = slice(h * LANES, (h + 1) * LANES)
            dqf_ref[:, cols] = _rope_bwd(dq_ref[:, cols].astype(F32) * MLA_SCALE, c, s_a, s_b, MLA_ROPE // 2).astype(BF16)
            dkh = dk_ref[:, cols].astype(F32) * (1.0 / LOG2E)
            dkvf_ref[:, cols] = dkh.astype(BF16)
            dkrope = dkrope + dkh
        dkvf_ref[:, MLA_QW:] = dv_ref[...].astype(BF16)
        dkr = _rope_bwd(jnp.where(rope_lanes, dkrope, 0.0), c, s_a, s_b, MLA_ROPE // 2)
        dcqn = _dot_nt(dqf_ref[...], wuq_ref[...])
        dckvn = _dot_nt(dkvf_ref[...], wkv_ref[...])
        dcq, dgq = _rms_bwd(dcqn, cq, rq, gq_ref[...])
        dckv, dgkv = _rms_bwd(dckvn, ckv, rkv, gkv_ref[...])
        dgq_ref[...] += dgq
        dgkv_ref[...] += dgkv
        dcc_ref[:, 0:256] = dcq.astype(BF16)
        dcc_ref[:, 256:384] = dckv.astype(BF16)
        dcc_ref[:, 384:512] = dkr.astype(BF16)

    def row(w, j):
        return pl.BlockSpec((tm, w), lambda i: (i, j))

    def full(a):
        return pl.BlockSpec(a.shape, lambda i: (0, 0))

    return pl.pallas_call(
        body, grid=(t // tm,),
        in_specs=[row(256, 4096 // 256), row(128, 4352 // 128), row(128, 0), row(128, 0), row(128, 0),
                  full(g_cq), full(g_ckv), full(wuq), full(wkv), row(MLA_QW, 0), row(MLA_QW, 0), row(MLA_WIDTH, 0)],
        out_specs=[row(512, 0), row(MLA_QW, 0), row(256, 0), row(MLA_KVW, 0), row(128, 0), full(g_cq), full(g_ckv)],
        out_shape=[SDS((t, 512), BF16), SDS((t, MLA_QW), BF16), SDS((t, 256), BF16), SDS((t, MLA_KVW), BF16),
                   SDS((t, 128), BF16), SDS(g_cq.shape, F32), SDS(g_ckv.shape, F32)],
        name="mla_prep_bwd", compiler_params=_params("arbitrary"))(proj, proj, cm, sma, smb, g_cq, g_ckv, wuq, wkv, dq, dk, dv)


MLA_TQ = SEQ
MLA_SUB_FWD = 512
MLA_SUB_BWD = 256


def mla_attn_fwd(qb, kb, vb, nb):
    t = qb.shape[0]
    nq = SEQ // MLA_TQ
    n_pairs = MLA_HEADS // 2

    def body(q_ref, k_ref, v_ref, y_ref, lse_ref):
        head0 = lax.broadcasted_iota(jnp.int32, (MLA_SUB_FWD, LANES), 1) < MLA_V
        v = v_ref[...]
        vhead0 = lax.broadcasted_iota(jnp.int32, v.shape, 1) < MLA_V
        one = jnp.ones_like(v)
        vh = [jnp.where(vhead0 == (h == 0), v, one) for h in range(2)]
        for sub in range(MLA_TQ // MLA_SUB_FWD):
            rows = slice(sub * MLA_SUB_FWD, (sub + 1) * MLA_SUB_FWD)
            outs, lses = [], []
            for h in range(2):
                cols = slice(h * LANES, (h + 1) * LANES)
                s = _dot_nt(q_ref[rows, cols], k_ref[:, cols])
                m = jnp.max(s, axis=-1, keepdims=True)
                p = jnp.exp2(s - m).astype(BF16)
                ol = _dot(p, vh[h])
                l = pltpu.roll(ol, MLA_V, 1)
                outs.append(ol / l)
                lses.append(m + jnp.log2(l))
            y_ref[rows, :] = jnp.where(head0, outs[0], outs[1])
            lse_ref[rows, :] = jnp.where(head0, lses[0], lses[1])

    return pl.pallas_call(
        body, grid=(nb, n_pairs, nq),
        in_specs=[pl.BlockSpec((MLA_TQ, 2 * LANES), lambda b, hp, i: (b * nq + i, hp)),
                  pl.BlockSpec((SEQ, 2 * LANES), lambda b, hp, i: (b, hp)),
                  pl.BlockSpec((SEQ, LANES), lambda b, hp, i: (b, hp))],
        out_specs=[pl.BlockSpec((MLA_TQ, LANES), lambda b, hp, i: (b * nq + i, hp))] * 2,
        out_shape=[SDS((t, MLA_WIDTH), F32)] * 2,
        name="mla_attn_fwd", compiler_params=_params("parallel", "parallel", "parallel"))(qb, kb, vb)


def mla_attn_bwd(qb, kb, vb, dy, y, lse, nb, xch):
    t = qb.shape[0]
    nq = SEQ // MLA_TQ
    n_pairs = MLA_HEADS // 2

    assert nq == 1

    def body(q_ref, k_ref, v_ref, do_ref, y_ref, lse_ref, dq_ref, dk_ref, dv_ref, dk_s, dv_s):
        dk_s[...] = jnp.zeros_like(dk_s)
        dv_s[...] = jnp.zeros_like(dv_s)
        head0 = lax.broadcasted_iota(jnp.int32, (MLA_SUB_BWD, LANES), 1) < MLA_V
        v = v_ref[...]
        for sub in range(MLA_TQ // MLA_SUB_BWD):
            rows = slice(sub * MLA_SUB_BWD, (sub + 1) * MLA_SUB_BWD)
            do = do_ref[rows, :]
            lse = lse_ref[rows, :]
            tt = do * y_ref[rows, :]
            dv = jnp.zeros((SEQ, LANES), F32)
            for h in range(2):
                sel = head0 if h == 0 else ~head0
                lo = h * MLA_V
                cols = slice(h * LANES, (h + 1) * LANES)
                q = q_ref[rows, cols]
                k = k_ref[:, cols]
                dd = jnp.sum(jnp.where(sel, tt, 0.0), axis=-1, keepdims=True)
                doh = jnp.where(sel, do, 0.0).astype(BF16)
                p = jnp.exp2(_dot_nt(q, k) - lse[:, lo:lo + 1])
                dp = _dot_nt(doh, v)
                ds = (p * (dp - dd)).astype(BF16)
                dq_ref[rows, cols] = _dot(ds, k).astype(dq_ref.dtype)
                dk_s[:, cols] += _dot_tn(ds, q)
                dv = dv + _dot_tn(p.astype(BF16), doh)
            dv_s[...] += dv
        dk_ref[...] = dk_s[...].astype(dk_ref.dtype)
        dv_ref[...] = dv_s[...].astype(dv_ref.dtype)

    qspec = pl.BlockSpec((MLA_TQ, 2 * LANES), lambda b, hp, i: (b * nq + i, hp))
    kspec = pl.BlockSpec((SEQ, 2 * LANES), lambda b, hp, i: (b, hp))
    vspec = pl.BlockSpec((SEQ, LANES), lambda b, hp, i: (b, hp))
    ospec = pl.BlockSpec((MLA_TQ, LANES), lambda b, hp, i: (b * nq + i, hp))
    return call_hosting_exchange(
        body, xch, grid=(nb, n_pairs, nq),
        in_specs=[qspec, kspec, vspec, ospec, ospec, ospec],
        out_specs=[qspec, kspec, vspec],
        out_shape=[SDS((t, MLA_QW), BF16), SDS((t, MLA_QW), BF16), SDS((t, MLA_WIDTH), BF16)],
        scratch_shapes=[pltpu.VMEM((SEQ, 2 * LANES), F32), pltpu.VMEM((SEQ, LANES), F32)],
        name="mla_attn_bwd", operands=(qb, kb, vb, dy, y, lse))


MEM_TQ = SEQ
MEM_SUB = SEQ
MEM_SCALE = MEM_HEAD_DIM ** -0.5
MQ_BLK4 = 5120 // MEM_WIDTH


def mem_attn_fwd(proj, mkv, nb):
    t = proj.shape[0]
    nq = SEQ // MEM_TQ

    def body(q_ref, mk_ref, mv_ref, y_ref):
        for sub in range(MEM_TQ // MEM_SUB):
            rows = slice(sub * MEM_SUB, (sub + 1) * MEM_SUB)
            for h in range(MEM_HEADS):
                cols = slice(h * LANES, (h + 1) * LANES)
                s = _dot_nt(q_ref[rows, cols].astype(BF16), mk_ref[:, cols]) * MEM_SCALE
                m = jnp.max(s, axis=-1, keepdims=True)
                p = jnp.exp(s - m)
                l = jnp.sum(p, axis=-1, keepdims=True)
                y_ref[rows, cols] = _dot(p.astype(BF16), mv_ref[:, cols]) / l

    return pl.pallas_call(
        body, grid=(nb, nq),
        in_specs=[pl.BlockSpec((MEM_TQ, MEM_WIDTH), lambda b, i: (b * nq + i, MQ_BLK4)),
                  pl.BlockSpec((N_MEM, MEM_WIDTH), lambda b, i: (b, 0)),
                  pl.BlockSpec((N_MEM, MEM_WIDTH), lambda b, i: (b, 1))],
        out_specs=pl.BlockSpec((MEM_TQ, MEM_WIDTH), lambda b, i: (b * nq + i, 0)),
        out_shape=SDS((t, MEM_WIDTH), F32),
        name="mem_attn_fwd", compiler_params=_params("parallel", "parallel"))(proj, mkv, mkv)


def mem_attn_bwd(proj, mkv, dy, nb):
    t = proj.shape[0]
    nq = SEQ // MEM_TQ

    def body(q_ref, mk_ref, mv_ref, do_ref, dq_ref, dmk_ref, dmv_ref):
        @pl.when(pl.program_id(1) == 0)
        def _():
            dmk_ref[...] = jnp.zeros_like(dmk_ref)
            dmv_ref[...] = jnp.zeros_like(dmv_ref)

        for sub in range(MEM_TQ // MEM_SUB):
            rows = slice(sub * MEM_SUB, (sub + 1) * MEM_SUB)
            for h in range(MEM_HEADS):
                cols = slice(h * LANES, (h + 1) * LANES)
                q = q_ref[rows, cols].astype(BF16)
                mk, mv = mk_ref[:, cols], mv_ref[:, cols]
                do = do_ref[rows, cols].astype(BF16)
                s = _dot_nt(q, mk) * MEM_SCALE
                e = jnp.exp(s - jnp.max(s, axis=-1, keepdims=True))
                p = e / jnp.sum(e, axis=-1, keepdims=True)
                dp = _dot_nt(do, mv)
                ds = (p * (dp - jnp.sum(p * dp, axis=-1, keepdims=True)) * MEM_SCALE).astype(BF16)
                dq_ref[rows, cols] = _dot(ds, mk).astype(BF16)
                dmk_ref[:, cols] += _dot_tn(ds, q)
                dmv_ref[:, cols] += _dot_tn(p.astype(BF16), do)

    ospec = pl.BlockSpec((MEM_TQ, MEM_WIDTH), lambda b, i: (b * nq + i, 0))
    kspec = pl.BlockSpec((N_MEM, MEM_WIDTH), lambda b, i: (b, 0))
    return pl.pallas_call(
        body, grid=(nb, nq),
        in_specs=[pl.BlockSpec((MEM_TQ, MEM_WIDTH), lambda b, i: (b * nq + i, MQ_BLK4)),
                  kspec, pl.BlockSpec((N_MEM, MEM_WIDTH), lambda b, i: (b, 1)), ospec],
        out_specs=[ospec, kspec, kspec],
        out_shape=[SDS((t, MEM_WIDTH), BF16), SDS((nb * N_MEM, MEM_WIDTH), F32), SDS((nb * N_MEM, MEM_WIDTH), F32)],
        name="mem_attn_bwd", compiler_params=_params("parallel", "arbitrary"))(proj, mkv, mkv, dy)


ROW_TM = 512
AG_BLK = 3072 // 1024
BG_BLK = 4608 // 512
MG_BLK = 5632 // 512
GROUPS = ((0, A_WIDTH), (A_WIDTH, MLA_WIDTH), (A_WIDTH + MLA_WIDTH, MEM_WIDTH))
D_MIX = 2048


def _gate_specs():
    def row(w, j):
        return pl.BlockSpec((ROW_TM, w), lambda i: (i, j))

    def vec(w):
        return pl.BlockSpec((1, w), lambda i: (0, 0))

    ys = [row(A_WIDTH, 0), row(MLA_WIDTH, 0), row(MEM_WIDTH, 0)]
    gates = [row(A_WIDTH, AG_BLK), row(MLA_WIDTH, BG_BLK), row(MEM_WIDTH, MG_BLK)]
    gains = [vec(A_WIDTH), vec(MLA_WIDTH), vec(MEM_WIDTH)]
    return row, vec, ys, gates, gains


def gate_out_ln_loss(ya, yb, ym, proj, goa, gob, gom, wout, h32, target, gp, bp):
    t, d = h32.shape
    _, _, ys, gates, gains = _gate_specs()

    def body(ya_ref, yb_ref, ym_ref, ga_ref, gb_ref, gm_ref, goa_ref, gob_ref, gom_ref, w_ref, h_ref, t_ref, gp_ref, bp_ref,
             z_ref, du32_ref, du16_ref, loss_ref, dgp_ref, dbp_ref):
        @pl.when(pl.program_id(0) == 0)
        def _():
            loss_ref[...] = jnp.zeros_like(loss_ref)
            dgp_ref[...] = jnp.zeros_like(dgp_ref)
            dbp_ref[...] = jnp.zeros_like(dbp_ref)

        for (off, w), y_ref, g_ref, go_ref in zip(GROUPS, (ya_ref, yb_ref, ym_ref), (ga_ref, gb_ref, gm_ref),
                                                  (goa_ref, gob_ref, gom_ref)):
            n, _ = _rms(y_ref[...], go_ref[...])
            gt = g_ref[...]
            z_ref[:, off:off + w] = (n * (gt * _sigmoid(gt))).astype(BF16)
        g = gp_ref[...]
        u = ALPHA * h_ref[...] + _dot(z_ref[...], w_ref[...])
        mu = jnp.mean(u, axis=-1, keepdims=True)
        uc = u - mu
        rstd = lax.rsqrt(jnp.mean(uc * uc, axis=-1, keepdims=True) + NORM_EPS)
        xhat = uc * rstd
        err = xhat * g + bp_ref[...] - t_ref[...]
        tok = jnp.sum(err * err, axis=-1, keepdims=True) * (1.0 / d)
        loss_ref[...] += 0.5 * jnp.sum(tok, axis=0, keepdims=True)
        dout = err * (1.0 / d)
        dxhat = dout * g
        du = rstd * (dxhat - jnp.mean(dxhat, axis=-1, keepdims=True)
                     - xhat * jnp.mean(dxhat * xhat, axis=-1, keepdims=True))
        du32_ref[...] = du
        du16_ref[...] = du.astype(BF16)
        dgp_ref[...] += jnp.sum(dout * xhat, axis=0, keepdims=True)
        dbp_ref[...] += jnp.sum(dout, axis=0, keepdims=True)

    row = pl.BlockSpec((ROW_TM, d), lambda i: (i, 0))
    vec = pl.BlockSpec((1, d), lambda i: (0, 0))
    zrow = pl.BlockSpec((ROW_TM, D_MIX), lambda i: (i, 0))
    return pl.pallas_call(
        body, grid=(t // ROW_TM,),
        in_specs=ys + gates + gains + [pl.BlockSpec((D_MIX, d), lambda i: (0, 0)), row, row, vec, vec],
        out_specs=[zrow, row, row, pl.BlockSpec((1, LANES), lambda i: (0, 0)), vec, vec],
        out_shape=[SDS((t, D_MIX), BF16), SDS((t, d), F32), SDS((t, d), BF16), SDS((1, LANES), F32), SDS((1, d), F32),
                   SDS((1, d), F32)],
        name="gate_out_ln_loss", compiler_params=_params("arbitrary"))(
            ya, yb, ym, proj, proj, proj, goa, gob, gom, wout, h32, target, gp, bp)


def gate_bwd(du16, wout, ya, yb, ym, proj, goa, gob, gom):
    t = ya.shape[0]
    row, vec, ys, gates, gains = _gate_specs()

    def body(du_ref, w_ref, ya_ref, yb_ref, ym_ref, ga_ref, gb_ref, gm_ref, goa_ref, gob_ref, gom_ref,
             dya_ref, dyb_ref, dym_ref, dga_ref, dgb_ref, dgm_ref, dgoa_ref, dgob_ref, dgom_ref):
        @pl.when(pl.program_id(0) == 0)
        def _():
            dgoa_ref[...] = jnp.zeros_like(dgoa_ref)
            dgob_ref[...] = jnp.zeros_like(dgob_ref)
            dgom_ref[...] = jnp.zeros_like(dgom_ref)

        dz = _dot_nt(du_ref[...], w_ref[...])
        for (off, w), y_ref, g_ref, go_ref, dy_ref, dg_ref, dgo_ref in zip(
                GROUPS, (ya_ref, yb_ref, ym_ref), (ga_ref, gb_ref, gm_ref), (goa_ref, gob_ref, gom_ref),
                (dya_ref, dyb_ref, dym_ref), (dga_ref, dgb_ref, dgm_ref), (dgoa_ref, dgob_ref, dgom_ref)):
            dzg = dz[:, off:off + w]
            y, gt, go = y_ref[...], g_ref[...], go_ref[...]
            n, r = _rms(y, go)
            sg = _sigmoid(gt)
            dg_ref[...] = (dzg * n * (sg * (1.0 + gt * (1.0 - sg)))).astype(BF16)
            dy, dgo = _rms_bwd(dzg * (gt * sg), y, r, go)
            dy_ref[...] = dy
            dgo_ref[...] += dgo

    widths = (A_WIDTH, MLA_WIDTH, MEM_WIDTH)
    return pl.pallas_call(
        body, grid=(t // ROW_TM,),
        in_specs=[row(D_MODEL, 0), pl.BlockSpec((D_MIX, D_MODEL), lambda i: (0, 0))] + ys + gates + gains,
        out_specs=[row(w, 0) for w in widths] * 2 + [vec(w) for w in widths],
        out_shape=[SDS((t, w), F32) for w in widths] + [SDS((t, w), BF16) for w in widths] + [SDS((1, w), F32) for w in widths],
        name="gate_bwd", compiler_params=_params("arbitrary"))(du16, wout, ya, yb, ym, proj, proj, proj, goa, gob, gom)


def dh_ln_bwd(pieces, win_t, du32, x2, g_emb, xch):
    t, d = x2.shape

    def body(*refs):
        p_refs = refs[:len(pieces)]
        w_ref, du_ref, x_ref, g_ref, dx_ref, dg_ref, db_ref = refs[len(pieces):]

        @pl.when(pl.program_id(0) == 0)
        def _():
            dg_ref[...] = jnp.zeros_like(dg_ref)
            db_ref[...] = jnp.zeros_like(db_ref)

        dh = ALPHA * du_ref[...]
        for p_ref, off, w in zip(p_refs, PIECE_OFFS, PIECE_WIDTHS):
            dh = dh + _dot(p_ref[...], w_ref[off:off + w, :])
        x = x_ref[...]
        xc = x - jnp.mean(x, axis=-1, keepdims=True)
        rstd = lax.rsqrt(jnp.mean(xc * xc, axis=-1, keepdims=True) + NORM_EPS)
        xhat = xc * rstd
        dg_ref[...] += jnp.sum(dh * xhat, axis=0, keepdims=True)
        db_ref[...] += jnp.sum(dh, axis=0, keepdims=True)
        tg = dh * g_ref[...]
        dx_ref[...] = rstd * (tg - jnp.mean(tg, axis=-1, keepdims=True)
                              - xhat * jnp.mean(tg * xhat, axis=-1, keepdims=True))

    row = pl.BlockSpec((ROW_TM, d), lambda i: (i, 0))
    vec = pl.BlockSpec((1, d), lambda i: (0, 0))
    return call_hosting_exchange(
        body, xch, grid=(t // ROW_TM,),
        in_specs=[pl.BlockSpec((ROW_TM, w), lambda i: (i, 0)) for w in PIECE_WIDTHS]
        + [pl.BlockSpec(win_t.shape, lambda i: (0, 0)), row, row, vec],
        out_specs=[row, vec, vec],
        out_shape=[SDS((t, d), F32), SDS((1, d), F32), SDS((1, d), F32)],
        scratch_shapes=[], name="dh_ln_bwd", operands=(*pieces, win_t, du32, x2, g_emb))


def _adamw(w, g, m, v):
    m2 = ADAM_B1 * m + (1.0 - ADAM_B1) * g
    v2 = ADAM_B2 * v + (1.0 - ADAM_B2) * (g * g)
    m_hat = m2 / (1.0 - ADAM_B1 ** ADAM_STEP)
    v_hat = v2 / (1.0 - ADAM_B2 ** ADAM_STEP)
    return -ADAM_LR * (m_hat / (jnp.sqrt(v_hat) + ADAM_EPS) + ADAM_WD * w), m2, v2


def adamw_shard(w, parts, m, v, name):
    r, c = w.shape
    if r % 256 == 0 or r * c <= 256 * 1024:
        tr, tc = min(r, 256), c
    else:
        tr, tc = r, 256

    def body(w_ref, p_ref, m_ref, v_ref, g_ref, d_ref, nm_ref, nv_ref):
        g = p_ref[0].astype(F32)
        for k in range(1, N_DEV):
            g = g + p_ref[k].astype(F32)
        g_ref[...] = g
        d_ref[...], nm_ref[...], nv_ref[...] = _adamw(w_ref[...], g, m_ref[...], v_ref[...])

    blk = pl.BlockSpec((tr, tc), lambda i, j: (i, j))
    return pl.pallas_call(
        body, grid=(r // tr, c // tc),
        in_specs=[blk, pl.BlockSpec((N_DEV, tr, tc), lambda i, j: (0, i, j)), blk, blk],
        out_specs=[blk] * 4, out_shape=[SDS((r, c), F32)] * 4, name=name,
        compiler_params=_params("parallel", "parallel"))(w, parts, m, v)


def adamw_shard_t(w, parts, m, v, name, tc=256):
    r, c = w.shape
    rp = -(-r // LANES) * LANES

    def body(w_ref, p_ref, m_ref, v_ref, g_ref, d_ref, nm_ref, nv_ref, buf):
        g = p_ref[0].astype(F32)
        for k in range(1, N_DEV):
            g = g + p_ref[k].astype(F32)
        buf[:, rp - LANES:, :] = jnp.zeros((4, LANES, tc), F32)
        for i, val in enumerate((g,) + _adamw(w_ref[...], g, m_ref[...], v_ref[...])):
            buf[i, :r, :] = val
        for i, o_ref in enumerate((g_ref, d_ref, nm_ref, nv_ref)):
            o_ref[...] = buf[i].T[:, :r]

    blk = pl.BlockSpec((r, tc), lambda j: (0, j))
    oblk = pl.BlockSpec((tc, r), lambda j: (j, 0))
    return pl.pallas_call(
        body, grid=(c // tc,),
        in_specs=[blk, pl.BlockSpec((N_DEV, r, tc), lambda j: (0, 0, j)), blk, blk],
        out_specs=[oblk] * 4, out_shape=[SDS((c, r), F32)] * 4,
        scratch_shapes=[pltpu.VMEM((4, rp, tc), F32)], name=name,
        compiler_params=_params("parallel"))(w, parts, m, v)


def _place():
    return lax.axis_index("x"), lax.axis_index("y"), lax.axis_index("c")


def _flat(px, py, pc):
    return 4 * px + 2 * py + pc


def _peer(x, y, c, k):
    return (1 - x if k & 4 else x, 1 - y if k & 2 else y, 1 - c if k & 1 else c)


def cast_shards(shards):
    def body(*refs):
        n = len(refs) // 2
        for i_ref, o_ref in zip(refs[:n], refs[n:]):
            o_ref[...] = i_ref[...].astype(BF16)

    return pl.pallas_call(body, out_shape=[SDS(s.shape, BF16) for s in shards], name="cast_shards",
                          compiler_params=_params())(*shards)


def _two_level_gather_plan(src_refs, land_refs, send_sems, recv_sems, local_sems):
    n = len(src_refs)
    x, y, c = _place()
    me, sib = (x, y, c), (x, y, 1 - c)
    chips = [(1 - x, y), (x, 1 - y), (1 - x, 1 - y)]

    def copy(a, k, block, to, src=None):
        dst = land_refs[a].at[_flat(*block)]
        return pltpu.make_async_remote_copy(
            src_ref=dst if src is None else src, dst_ref=dst,
            send_sem=send_sems.at[a * N_DEV + k], recv_sem=recv_sems.at[a * N_DEV + k],
            device_id=to, device_id_type=MESH)

    mine = [pltpu.make_async_copy(src_refs[a], land_refs[a].at[_flat(*me)], local_sems.at[a]) for a in range(n)]
    first = []
    for a in range(n):
        first.append(copy(a, 0, me, sib, src=src_refs[a]))
        first += [copy(a, 1 + j, me, (*chip, c), src=src_refs[a]) for j, chip in enumerate(chips)]

    def start():
        for cp in mine + first:
            cp.start()

    def finish():
        passed = []
        for j, chip in enumerate(chips):
            for a in range(n):
                copy(a, 1 + j, (*chip, c), me).wait_recv()
                fwd = copy(a, 4 + j, (*chip, c), sib)
                fwd.start()
                passed.append(fwd)
        for a in range(n):
            copy(a, 0, sib, me).wait_recv()
            for j, chip in enumerate(chips):
                copy(a, 4 + j, (*chip, 1 - c), me).wait_recv()
        for cp in first + passed:
            cp.wait_send()
        for cp in mine:
            cp.wait()

    return start, finish


ALL_DEVICES = tuple(range(N_DEV))


def _exchange_plan(src_refs, land_refs, dests, send_sems, recv_sems, local_sems):
    x, y, c = _place()
    me = _flat(x, y, c)
    plan = []
    for a, (src, land, dl) in enumerate(zip(src_refs, land_refs, dests)):
        for li, j in enumerate(dl):
            to = ((j >> 2) & 1, (j >> 1) & 1, j & 1)
            block = src.at[li] if len(src.shape) == len(land.shape) else src

            def push(slot, a=a, block=block, land=land, j=j, to=to):
                return pltpu.make_async_remote_copy(
                    src_ref=block, dst_ref=land.at[slot], send_sem=send_sems.at[a * N_DEV + j],
                    recv_sem=recv_sems.at[a * N_DEV + slot], device_id=to, device_id_type=MESH)

            own = pltpu.make_async_copy(block, land.at[j], local_sems.at[a])
            plan.append((j, push(me), own, [push(s) for s in range(N_DEV) if s != j]))
    return me, plan


def _exchange_start(me, plan):
    for j, send, own, _ in plan:
        @pl.when(me != j)
        def _(send=send):
            send.start()

        @pl.when(me == j)
        def _(own=own):
            own.start()


def _exchange_wait(me, plan):
    for j, send, own, arrivals in plan:
        @pl.when(me != j)
        def _(send=send):
            send.wait_send()

        @pl.when(me == j)
        def _(own=own, arrivals=arrivals):
            own.wait()
            for arrival in arrivals:
                arrival.wait_recv()


def call_hosting_exchange(core, xch, *, grid, in_specs, out_specs, out_shape, scratch_shapes, name, operands):
    srcs, dests, landing = xch
    n, n_in, n_out, n_scr = len(srcs), len(in_specs), len(out_specs), len(scratch_shapes)

    def body(*refs):
        ins, src_refs = refs[:n_in], refs[n_in:n_in + n]
        outs = refs[n_in + 2 * n:n_in + 2 * n + n_out]
        land_refs = refs[n_in + 2 * n + n_out:n_in + 3 * n + n_out]
        scratch = refs[n_in + 3 * n + n_out:n_in + 3 * n + n_out + n_scr]
        sems = refs[n_in + 3 * n + n_out + n_scr:]
        first = functools.reduce(jnp.logical_and, [pl.program_id(i) == 0 for i in range(len(grid))])
        last = functools.reduce(jnp.logical_and, [pl.program_id(i) == grid[i] - 1 for i in range(len(grid))])
        if dests is None:
            start, finish = _two_level_gather_plan(src_refs, land_refs, *sems)
        else:
            me, plan = _exchange_plan(src_refs, land_refs, dests, *sems)
            start, finish = functools.partial(_exchange_start, me, plan), functools.partial(_exchange_wait, me, plan)
        pl.when(first)(start)
        core(*ins, *outs, *scratch)
        pl.when(last)(finish)

    hbm = pl.BlockSpec(memory_space=pl.ANY)
    res = pl.pallas_call(
        body, grid=grid,
        in_specs=list(in_specs) + [hbm] * (2 * n), out_specs=list(out_specs) + [hbm] * n,
        out_shape=list(out_shape) + [SDS(l.shape, l.dtype) for l in landing],
        scratch_shapes=list(scratch_shapes) + [pltpu.SemaphoreType.DMA((N_DEV * n,)), pltpu.SemaphoreType.DMA((N_DEV * n,)),
                                               pltpu.SemaphoreType.DMA((n,))],
        input_output_aliases={n_in + n + k: n_out + k for k in range(n)},
        name=name, compiler_params=_params(*(("arbitrary",) * len(grid))))(*operands, *srcs, *landing)
    return res[:n_out], res[n_out:]


SLOT_ROWS = 8


def small_allreduce_adamw(loss_sum, grads, ws, ms, vs, shards):
    n = len(grads)
    rows = [g.shape[0] for g in grads]
    total = SLOT_ROWS * (n + 1)
    sh_w, sh_p, sh_m, sh_v = shards
    k_sh = len(sh_w)
    n_in, n_out = 1 + 4 * n + 4 * k_sh, 1 + 4 * n + 4 * k_sh

    def body(*refs):
        loss_ref, g_refs, w_refs = refs[0], refs[1:1 + n], refs[1 + n:1 + 2 * n]
        m_refs, v_refs = refs[1 + 2 * n:1 + 3 * n], refs[1 + 3 * n:1 + 4 * n]
        sh = refs[1 + 4 * n:n_in]
        outs = refs[n_in:n_in + 1 + 4 * n]
        sh_outs = refs[n_in + 1 + 4 * n:n_in + n_out]
        vec, gath, tot, send_sems, recv_sems = refs[n_in + n_out:]
        x, y, c = _place()
        me = _flat(x, y, c)
        vec[...] = jnp.zeros_like(vec)
        vec[0:1, :] = loss_ref[...]
        for i in range(n):
            vec[SLOT_ROWS * (i + 1):SLOT_ROWS * (i + 1) + rows[i], :] = g_refs[i][...]
        gath[me] = vec[...]
        copies = []
        for k in range(1, N_DEV):
            peer = _peer(x, y, c, k)
            copies.append(pltpu.make_async_remote_copy(
                src_ref=vec, dst_ref=gath.at[me], send_sem=send_sems.at[k - 1], recv_sem=recv_sems.at[k - 1],
                device_id=peer, device_id_type=MESH))
        for cp in copies:
            cp.start()
        for i in range(k_sh):
            g = sh[k_sh + i][0].astype(F32)
            for k in range(1, N_DEV):
                g = g + sh[k_sh + i][k].astype(F32)
            sh_outs[4 * i][...] = g
            sh_outs[4 * i + 1][...], sh_outs[4 * i + 2][...], sh_outs[4 * i + 3][...] = _adamw(
                sh[i][...], g, sh[2 * k_sh + i][...], sh[3 * k_sh + i][...])
        for cp in copies:
            cp.wait_recv()
        for cp in copies:
            cp.wait_send()
        g = gath[0]
        for j in range(1, N_DEV):
            g = g + gath[j]
        tot[...] = g
        outs[0][...] = tot[0:1, :]
        for i in range(n):
            gi = tot[SLOT_ROWS * (i + 1):SLOT_ROWS * (i + 1) + rows[i], :]
            outs[1 + i][...] = gi
            outs[1 + n + i][...], outs[1 + 2 * n + i][...], outs[1 + 3 * n + i][...] = _adamw(
                w_refs[i][...], gi, m_refs[i][...], v_refs[i][...])

    shapes = [SDS(g.shape, F32) for g in grads]
    res = pl.pallas_call(
        body, out_shape=[SDS((1, LANES), F32)] + shapes * 4 + [SDS(w.shape, F32) for w in sh_w for _ in range(4)],
        scratch_shapes=[pltpu.VMEM((total, LANES), F32), pltpu.VMEM((N_DEV, total, LANES), F32), pltpu.VMEM((total, LANES), F32),
                        pltpu.SemaphoreType.DMA((7,)), pltpu.SemaphoreType.DMA((7,))],
        name="small_allreduce_adamw", compiler_params=_params())(loss_sum, *grads, *ws, *ms, *vs, *sh_w, *sh_p, *sh_m, *sh_v)
    small, rest = res[:1 + 4 * n], res[1 + 4 * n:]
    return small, [rest[4 * i:4 * i + 4] for i in range(k_sh)]


def _rope_lane_patterns():
    inv = lambda r: ROPE_THETA ** (-(jnp.arange(0, r, 2, dtype=F32) / r))
    z = lambda n: jnp.zeros((n,), F32)
    o = lambda n: jnp.ones((n,), F32)
    half, rest = A_ROT // 2, A_HEAD_DIM - A_ROT
    ia, im = inv(A_ROT), inv(MLA_ROPE)
    mh, tail = MLA_ROPE // 2, LANES - MLA_NOPE - MLA_ROPE
    rows = [jnp.tile(jnp.concatenate([ia, ia, z(rest)]), 2),
            jnp.tile(jnp.concatenate([o(half), z(half + rest)]), 2),
            jnp.tile(jnp.concatenate([z(half), o(half), z(rest)]), 2),
            jnp.concatenate([z(MLA_NOPE), im, im, z(tail)]),
            jnp.concatenate([z(MLA_NOPE), o(mh), z(mh + tail)]),
            jnp.concatenate([z(MLA_NOPE + mh), o(mh), z(tail)]),
            z(LANES), z(LANES)]
    return jnp.stack(rows)


KR_LO, KR_HI = 4480, 4512
W_IN_SHARD = D_IN // N_DEV
BG_SPLIT = 6 * W_IN_SHARD - KR_HI


def w_in_working_t(g):
    pad_lo, pad_hi = MLA_NOPE, LANES - MLA_NOPE - MLA_ROPE
    spans = []
    for lo, hi, shift in ((0, KR_LO, 0), (KR_LO, KR_HI, pad_lo), (KR_HI, D_IN, pad_lo + pad_hi)):
        r = lo
        while r < hi:
            j = r // W_IN_SHARD
            n = min(hi, (j + 1) * W_IN_SHARD) - r
            spans.append((j, r - j * W_IN_SHARD, n, r + shift))
            r += n

    def body(g_ref, o_ref):
        o_ref[KR_LO:KR_LO + pad_lo, :] = jnp.zeros((pad_lo, D_MODEL), o_ref.dtype)
        o_ref[KR_HI + pad_lo:KR_HI + pad_lo + pad_hi, :] = jnp.zeros((pad_hi, D_MODEL), o_ref.dtype)
        for j, src, n, dst in spans:
            o_ref[dst:dst + n, :] = g_ref[j, src:src + n, :]

    return pl.pallas_call(body, out_shape=SDS((D_INW, D_MODEL), g.dtype), name="w_in_working_t", compiler_params=_params())(g)


def _w_in_shard_5(d_ag_tail, d_cc, d_bg_head):
    kr = MLA_Q_RANK + MLA_KV_RANK + MLA_NOPE
    rows = jnp.concatenate([d_ag_tail, d_cc[:MLA_Q_RANK + MLA_KV_RANK], d_cc[kr:kr + MLA_ROPE], d_bg_head], 0)
    return rows.reshape(1, W_IN_SHARD, D_MODEL).astype(BF16)


def _w_uq_working(g):
    w = jnp.pad(g.transpose(1, 0, 2), ((0, 0), (0, 0), (0, LANES - MLA_NOPE - MLA_ROPE)))
    return w.reshape(MLA_Q_RANK, MLA_QW)


def _w_uq_parts(dw):
    return dw.reshape(MLA_Q_RANK, MLA_HEADS, LANES)[:, :, :MLA_NOPE + MLA_ROPE].transpose(1, 0, 2)


def _w_ukv_working(g):
    wk = jnp.pad(g[:, :, :MLA_NOPE].transpose(1, 0, 2), ((0, 0), (0, 0), (0, LANES - MLA_NOPE)))
    wv = g[:, :, MLA_NOPE:].transpose(1, 0, 2)
    return jnp.concatenate([wk.reshape(MLA_KV_RANK, MLA_QW), wv.reshape(MLA_KV_RANK, MLA_WIDTH)], 1)


def _w_ukv_parts(dw):
    dk = dw[:, :MLA_QW].reshape(MLA_KV_RANK, MLA_HEADS, LANES)[:, :, :MLA_NOPE]
    dv = dw[:, MLA_QW:].reshape(MLA_KV_RANK, MLA_HEADS, MLA_V)
    return jnp.concatenate([dk, dv], -1).transpose(1, 0, 2)


SMALL_NAMES = ("g_emb", "b_emb", "g_cq", "g_ckv", "g_out_a", "g_out_b", "g_out_m", "g_post", "b_post")


def kernel(x, mem, positions, g_emb, b_emb, w_in, g_cq, g_ckv, w_uq, w_ukv, w_mem_kv, g_out_a, g_out_b, g_out_m, w_out, g_post, b_post, loss_target, m_g_emb, m_b_emb, m_w_in, m_g_cq, m_g_ckv, m_w_uq, m_w_ukv, m_w_mem_kv, m_g_out_a, m_g_out_b, m_g_out_m, m_w_out, m_g_post, m_b_post, v_g_emb, v_b_emb, v_w_in, v_g_cq, v_g_ckv, v_w_uq, v_w_ukv, v_w_mem_kv, v_g_out_a, v_g_out_b, v_g_out_m, v_w_out, v_g_post, v_b_post):
    nb = x.shape[0]
    t = nb * SEQ
    x2 = x.reshape(t, D_MODEL)
    tgt2 = loss_target.reshape(t, D_MODEL)
    mem2 = mem.reshape(nb * N_MEM, D_MODEL)
    g_emb2, b_emb2 = g_emb.reshape(1, -1), b_emb.reshape(1, -1)

    w_in_t, m_w_in_t, v_w_in_t = w_in[0].T, m_w_in[0].T, v_w_in[0].T
    s_in, s_uq, s_ukv, s_mem, s_out = cast_shards((w_in_t, w_uq[0], w_ukv[0], w_mem_kv[0], w_out[0]))
    (h32, h16, (a_c, a_sa, a_sb), (m_c, m_sa, m_sb)), (g_in,) = embed_fwd(
        x2, g_emb2, b_emb2, positions, ((s_in,), None, (lax.empty((N_DEV,) + s_in.shape, BF16),)))
    win_t = w_in_working_t(g_in)

    proj = mm_nn(h16, win_t, F32, 2048, 1536, "proj", rhs_transposed=True)
    later = (s_uq, s_ukv, s_mem, s_out)
    (ya, lse_a), qkv_d, (g_uq, g_ukv, g_mem, g_out) = a_attn_fwd(
        proj, a_c, a_sa, a_sb, nb,
        (later, (ALL_DEVICES,) * len(later), tuple(lax.empty((N_DEV,) + w.shape, BF16) for w in later)))
    wuq_w = _w_uq_working(g_uq)
    wkv_w = _w_ukv_working(g_ukv)
    wmem = g_mem.reshape(D_MODEL, 2 * MEM_WIDTH)
    wout = g_out.reshape(D_MIX, D_MODEL)
    qb, kb, vb = mla_prep_fwd(proj, m_c, m_sa, m_sb, g_cq, g_ckv, wuq_w, wkv_w)
    yb, lse_b = mla_attn_fwd(qb, kb, vb, nb)
    mkv = mm_nn(mem2, wmem, BF16, nb * N_MEM, 512, "mem_kv")
    ym = mem_attn_fwd(proj, mkv, nb)
    z, du32, du16, loss_sum, dg_post, db_post = gate_out_ln_loss(
        ya, yb, ym, proj, g_out_a, g_out_b, g_out_m, wout, h32, tgt2, g_post, b_post)

    dya, dyb, dym, dag, dbg, dmg, dg_out_a, dg_out_b, dg_out_m = gate_bwd(
        du16, wout, ya, yb, ym, proj, g_out_a, g_out_b, g_out_m)
    dw_out = mm_tn(z, du16, 1024, "dw_out")
    dmq, dmk, dmv = mem_attn_bwd(proj, mkv, dym, nb)
    dw_mem = mm_tn(mem2, jnp.concatenate([dmk, dmv], 1), nb * N_MEM, "dw_mem")
    shards_6_7, d_bg_head = mm_tn_group((dbg, dmq, dmg), h16, 2048, "dw_in_bg_mq_mg", W_IN_SHARD, BG_SPLIT, 2, (0, BG_SPLIT))
    landing = lambda w, dtype=F32: lax.empty((N_DEV,) + w.shape, dtype)
    big_w = (w_in_t, w_uq[0], w_ukv[0], w_mem_kv[0], w_out[0])
    (daq, dak, dav), (p_out, p_mem, p_in) = a_attn_bwd(
        qkv_d, a_c, a_sa, a_sb, dya, ya, lse_a, nb,
        ((dw_out.reshape(N_DEV, D_MIX // N_DEV, D_MODEL), dw_mem.reshape(N_DEV, D_MODEL // N_DEV, 2 * MEM_WIDTH),
          shards_6_7),
         (ALL_DEVICES, ALL_DEVICES, (6, 7)),
         (landing(w_out[0]), landing(w_mem_kv[0]), landing(w_in_t, BF16))))
    shards_0_4, d_ag_tail = mm_tn_group((daq, dak, dav, dag), h16, 1024, "dw_in_aq_ak_av_ag", W_IN_SHARD, 0, 5,
                                        (5 * W_IN_SHARD, 4 * A_WIDTH))
    (dqb, dkb, dvb), (p_in,) = mla_attn_bwd(
        qb, kb, vb, dyb, yb, lse_b, nb, ((shards_0_4,), ((0, 1, 2, 3, 4),), (p_in,)))
    dcc, dqf, cqn, dkvf, ckvn, dg_cq, dg_ckv = mla_prep_bwd(proj, m_c, m_sa, m_sb, g_cq, g_ckv, wuq_w, wkv_w, dqb, dkb, dvb)
    dw_uq, dw_ukv, d_cc = mm_tn_pairs(((cqn, dqf), (ckvn, dkvf), (dcc, h16)), 1024, "dw_uq_ukv_cc")
    pieces = (daq, dak, dav, dag, dcc, dbg, dmq, dmg)
    (grad_x, dg_emb, db_emb), (p_in, p_uq, p_ukv) = dh_ln_bwd(
        pieces, win_t, du32, x2, g_emb2,
        ((_w_in_shard_5(d_ag_tail, d_cc, d_bg_head), _w_uq_parts(dw_uq), _w_ukv_parts(dw_ukv)),
         ((5,), ALL_DEVICES, ALL_DEVICES),
         (p_in, landing(w_uq[0]), landing(w_ukv[0]))))

    parts = (p_in, p_uq, p_ukv, p_mem, p_out)
    big_m = (m_w_in_t, m_w_uq[0], m_w_ukv[0], m_w_mem_kv[0], m_w_out[0])
    big_v = (v_w_in_t, v_w_uq[0], v_w_ukv[0], v_w_mem_kv[0], v_w_out[0])
    big = {"w_in": [o[None] for o in adamw_shard_t(big_w[0], parts[0], big_m[0], big_v[0], "adamw_w_in")]}

    small_w = (g_emb, b_emb, g_cq, g_ckv, g_out_a, g_out_b, g_out_m, g_post, b_post)
    small_m = (m_g_emb, m_b_emb, m_g_cq, m_g_ckv, m_g_out_a, m_g_out_b, m_g_out_m, m_g_post, m_b_post)
    small_v = (v_g_emb, v_b_emb, v_g_cq, v_g_ckv, v_g_out_a, v_g_out_b, v_g_out_m, v_g_post, v_b_post)
    small_g = (dg_emb, db_emb, dg_cq, dg_ckv, dg_out_a, dg_out_b, dg_out_m, dg_post, db_post)
    rows128 = lambda vals: [v.reshape(-1, LANES) for v in vals]
    res, rest = small_allreduce_adamw(loss_sum, rows128(small_g), rows128(small_w), rows128(small_m), rows128(small_v),
                                      (big_w[1:], parts[1:], big_m[1:], big_v[1:]))
    for name, out4 in zip(("w_uq", "w_ukv", "w_mem_kv", "w_out"), rest):
        big[name] = [o[None] for o in out4]
    loss = res[0][0, 0]
    n_small = len(small_w)
    sg, sd, sm, sv = [[r.reshape(w.shape) for r, w in zip(res[1 + k * n_small:1 + (k + 1) * n_small], small_w)]
                      for k in range(4)]

    order = ("g_emb", "b_emb", "w_in", "g_cq", "g_ckv", "w_uq", "w_ukv", "w_mem_kv", "g_out_a", "g_out_b", "g_out_m",
             "w_out", "g_post", "b_post")
    small_idx = {n: i for i, n in enumerate(SMALL_NAMES)}
    outs = [loss, grad_x.reshape(x.shape)]
    for kind in range(4):
        for name in order:
            outs.append(big[name][kind] if name in big else (sg, sd, sm, sv)[kind][small_idx[name]])
    return tuple(outs)
```

```python
import functools

import jax
import jax.numpy as jnp
from jax import lax
from jax.experimental import pallas as pl
from jax.experimental.pallas import tpu as pltpu

F32 = jnp.float32
BF16 = jnp.bfloat16
SDS = jax.ShapeDtypeStruct
MESH = pl.DeviceIdType.MESH

D_MODEL = 1024
SEQ = 2048
A_HEADS, A_HEAD_DIM, A_ROT = 16, 64, 16
A_WIDTH = 1024
DILATIONS = (1, 4, 16)
N_SIDE = 64
MLA_HEADS, MLA_Q_RANK, MLA_KV_RANK = 8, 256, 128
MLA_NOPE, MLA_ROPE, MLA_V = 64, 32, 64
MLA_WIDTH = 512
N_MEM, MEM_HEADS, MEM_HEAD_DIM, MEM_WIDTH = 256, 4, 128, 512
ROPE_THETA = 500000.0
NORM_EPS = 1e-5
NEG_INF = -1e30
ALPHA = 2.0 ** 0.25
D_IN = 6048
N_DEV = 8

ADAM_LR, ADAM_B1, ADAM_B2, ADAM_EPS, ADAM_WD, ADAM_STEP = 0.001, 0.9, 0.999, 1e-08, 0.01, 10

D_INW = 6144
PIECE_WIDTHS = (1024, 1024, 1024, 1024, 512, 512, 512, 512)
PIECE_OFFS = (0, 1024, 2048, 3072, 4096, 4608, 5120, 5632)
LANES = 128
VMEM_LIMIT = 56 * 1024 * 1024


def _params(*sem):
    kw = dict(vmem_limit_bytes=VMEM_LIMIT)
    if sem:
        kw["dimension_semantics"] = sem
    return pltpu.CompilerParams(**kw)


def _dot(a, b):
    return jnp.dot(a, b, preferred_element_type=F32)


def _dot_nt(a, b):
    return lax.dot_general(a, b, (((1,), (1,)), ((), ())), preferred_element_type=F32)


def _dot_tn(a, b):
    return lax.dot_general(a, b, (((0,), (0,)), ((), ())), preferred_element_type=F32)


def _sigmoid(x):
    return 1.0 / (1.0 + jnp.exp(-x))


def _rope_fwd(x, c, sa, sb, half):
    n = x.shape[-1]
    return x * c + pltpu.roll(x, n - half, 1) * sa + pltpu.roll(x, half, 1) * sb


def _rope_bwd(dy, c, sa, sb, half):
    n = dy.shape[-1]
    return dy * c + pltpu.roll(dy * sa, half, 1) + pltpu.roll(dy * sb, n - half, 1)


def mm_nn(a, b, out_dtype, tm, tn, name, rhs_transposed=False):
    m, k = a.shape
    n = b.shape[0] if rhs_transposed else b.shape[1]
    dot = _dot_nt if rhs_transposed else _dot

    def body(a_ref, b_ref, o_ref):
        o_ref[...] = dot(a_ref[...].astype(BF16), b_ref[...].astype(BF16)).astype(o_ref.dtype)

    b_spec = pl.BlockSpec((tn, k), lambda j, i: (j, 0)) if rhs_transposed else pl.BlockSpec((k, tn), lambda j, i: (0, j))
    return pl.pallas_call(
        body, grid=(n // tn, m // tm),
        in_specs=[pl.BlockSpec((tm, k), lambda j, i: (i, 0)), b_spec],
        out_specs=pl.BlockSpec((tm, tn), lambda j, i: (i, j)),
        out_shape=SDS((m, n), out_dtype), name=name,
        compiler_params=_params("parallel", "parallel"))(a, b)


def mm_tn(a, b, tt, name):
    t, m = a.shape
    n = b.shape[1]

    def body(a_ref, b_ref, o_ref):
        @pl.when(pl.program_id(0) == 0)
        def _():
            o_ref[...] = jnp.zeros_like(o_ref)

        o_ref[...] += _dot_tn(a_ref[...].astype(BF16), b_ref[...].astype(BF16))

    return pl.pallas_call(
        body, grid=(t // tt,),
        in_specs=[pl.BlockSpec((tt, m), lambda i: (i, 0)), pl.BlockSpec((tt, n), lambda i: (i, 0))],
        out_specs=pl.BlockSpec((m, n), lambda i: (0, 0)),
        out_shape=SDS((m, n), F32), name=name,
        compiler_params=_params("arbitrary"))(a, b)


def mm_tn_pairs(pairs, tt, name):
    n = len(pairs)
    t = pairs[0][0].shape[0]

    def body(*refs):
        @pl.when(pl.program_id(0) == 0)
        def _():
            for o_ref in refs[2 * n:]:
                o_ref[...] = jnp.zeros_like(o_ref)

        for i in range(n):
            refs[2 * n + i][...] += _dot_tn(refs[2 * i][...].astype(BF16), refs[2 * i + 1][...].astype(BF16))

    rows = lambda x: pl.BlockSpec((tt, x.shape[1]), lambda i: (i, 0))
    return pl.pallas_call(
        body, grid=(t // tt,),
        in_specs=[rows(x) for pair in pairs for x in pair],
        out_specs=[pl.BlockSpec((a.shape[1], b.shape[1]), lambda i: (0, 0)) for a, b in pairs],
        out_shape=[SDS((a.shape[1], b.shape[1]), F32) for a, b in pairs], name=name,
        compiler_params=_params("arbitrary"))(*[x for pair in pairs for x in pair])


def mm_tn_group(pieces, b, tt, name, slab_rows, first_slab_row, n_slabs, keep_rows):
    n, (t, w), cols = len(pieces), pieces[0].shape, b.shape[1]
    nt = t // tt
    keep_lo, keep_hi = keep_rows

    def body(*refs):
        p_refs, b_ref, slab_ref, keep_ref, acc = refs[:n], refs[n], refs[n + 1], refs[n + 2], refs[n + 3]

        @pl.when(pl.program_id(1) == 0)
        def _():
            acc[...] = jnp.zeros_like(acc)

        for k in range(n):
            @pl.when(pl.program_id(0) == k)
            def _(k=k):
                acc[...] += _dot_tn(p_refs[k][...], b_ref[...])

            @pl.when((pl.program_id(0) == k) & (pl.program_id(1) == nt - 1))
            def _(k=k):
                for j in range(n_slabs):
                    lo = max(k * w, first_slab_row + j * slab_rows)
                    hi = min((k + 1) * w, first_slab_row + (j + 1) * slab_rows)
                    if lo < hi:
                        dst = lo - first_slab_row - j * slab_rows
                        slab_ref[j, dst:dst + hi - lo, :] = acc[lo - k * w:hi - k * w, :].astype(slab_ref.dtype)
                lo, hi = max(k * w, keep_lo), min((k + 1) * w, keep_hi)
                if lo < hi:
                    keep_ref[lo - keep_lo:hi - keep_lo, :] = acc[lo - k * w:hi - k * w, :]

    def piece_spec(k):
        return pl.BlockSpec((tt, w), lambda p, i: (jnp.where(p < k, 0, jnp.where(p > k, nt - 1, i)), 0))

    return pl.pallas_call(
        body, grid=(n, nt),
        in_specs=[piece_spec(k) for k in range(n)] + [pl.BlockSpec((tt, cols), lambda p, i: (i, 0))],
        out_specs=[pl.BlockSpec((n_slabs, slab_rows, cols), lambda p, i: (0, 0, 0)),
                   pl.BlockSpec((keep_hi - keep_lo, cols), lambda p, i: (0, 0))],
        out_shape=[SDS((n_slabs, slab_rows, cols), BF16), SDS((keep_hi - keep_lo, cols), F32)],
        scratch_shapes=[pltpu.VMEM((w, cols), F32)], name=name,
        compiler_params=_params("arbitrary", "arbitrary"))(*pieces, b)


def embed_fwd(x2, g, b, positions, xch):
    t, d = x2.shape
    tm = 512
    pos = positions.astype(F32).reshape(-1, 1)

    def body(x_ref, g_ref, b_ref, pos_ref, pat_ref, h32_ref, h16_ref, *tabs):
        x = x_ref[...]
        mu = jnp.mean(x, axis=-1, keepdims=True)
        xc = x - mu
        var = jnp.mean(xc * xc, axis=-1, keepdims=True)
        h = xc * lax.rsqrt(var + NORM_EPS) * g_ref[...] + b_ref[...]
        h32_ref[...] = h
        h16_ref[...] = h.astype(BF16)
        p = pos_ref[...]
        for k in range(2):
            inv, first, second = pat_ref[3 * k:3 * k + 1, :], pat_ref[3 * k + 1:3 * k + 2, :], pat_ref[3 * k + 2:3 * k + 3, :]
            ang = p * inv
            sn = jnp.sin(ang)
            tabs[3 * k][...] = jnp.where(first + second > 0.0, jnp.cos(ang), 1.0)
            tabs[3 * k + 1][...] = -first * sn
            tabs[3 * k + 2][...] = second * sn

    row = pl.BlockSpec((tm, d), lambda i: (i, 0))
    vec = pl.BlockSpec((1, d), lambda i: (0, 0))
    tab = pl.BlockSpec((tm, LANES), lambda i: (i, 0))
    res, landed = call_hosting_exchange(
        body, xch, grid=(t // tm,),
        in_specs=[row, vec, vec, pl.BlockSpec((tm, 1), lambda i: (i, 0)), pl.BlockSpec((8, LANES), lambda i: (0, 0))],
        out_specs=[row, row] + [tab] * 6,
        out_shape=[SDS((t, d), F32), SDS((t, d), BF16)] + [SDS((t, LANES), F32)] * 6,
        scratch_shapes=[], name="embed_fwd", operands=(x2, g, b, pos, _rope_lane_patterns()))
    return (res[0], res[1], tuple(res[2:5]), tuple(res[5:8])), landed


Q_BLK = 128
UNROLL_FWD = 16
UNROLL_BWD = 16


def _pattern_geometry(d):
    length = SEQ // d
    nblk = length // Q_BLK
    kwin = min(2 * Q_BLK, length)
    return length, nblk, kwin


def _block_coords(idx, d):
    length, nblk, kwin = _pattern_geometry(d)
    r = lax.shift_right_logical(idx, nblk.bit_length() - 1)
    i = idx & (nblk - 1)
    q0 = pl.multiple_of(r * length + i * Q_BLK, Q_BLK)
    ks = jnp.clip(i * Q_BLK - N_SIDE, 0, length - kwin)
    k0 = pl.multiple_of(r * length + ks, N_SIDE)
    qpos = i * Q_BLK + lax.broadcasted_iota(jnp.int32, (Q_BLK, kwin), 0)
    kpos = ks + lax.broadcasted_iota(jnp.int32, (Q_BLK, kwin), 1)
    valid = jnp.abs(kpos - qpos) <= N_SIDE
    return q0, k0, kwin, valid


def _deinterleave(src_ref, dst_ref, d, dtype, tmp_ref):
    if d == 1:
        dst_ref[...] = src_ref[...].astype(dtype)
        return
    q = SEQ // 4
    if d == 4:
        for r in range(4):
            dst_ref[r * q:(r + 1) * q, :] = src_ref[pl.ds(r, q, stride=4), :].astype(dtype)
        return
    assert d == 16
    n = SEQ // 16
    for r in range(4):
        tmp_ref[r * q:(r + 1) * q, :] = src_ref[pl.ds(r, q, stride=4), :]
    for r in range(4):
        for j in range(4):
            dst_ref[(r + 4 * j) * n:(r + 4 * j + 1) * n, :] = tmp_ref[pl.ds(r * q + j, n, stride=4), :].astype(dtype)


def _class16_to_class4(src_ref, dst_ref):
    q, n = SEQ // 4, SEQ // 16
    for r in range(4):
        for j in range(4):
            dst_ref[pl.ds(r * q + j, n, stride=4), :] = src_ref[(r + 4 * j) * n:(r + 4 * j + 1) * n, :]


def _interleave(src_ref, dst_ref, d, tmp_ref, accumulate):
    q = SEQ // 4
    if d == 16:
        _class16_to_class4(src_ref, tmp_ref)
        src_ref = tmp_ref
    else:
        assert d == 4
    for r in range(4):
        rows = pl.ds(r, q, stride=4)
        val = src_ref[r * q:(r + 1) * q, :]
        dst_ref[rows, :] = dst_ref[rows, :] + val if accumulate else val


def a_attn_fwd(proj, ca, sa, sb, nb, xch):
    t = proj.shape[0]
    n_pairs = A_WIDTH // LANES

    def body(q_ref, k_ref, v_ref, c_ref, sa_ref, sb_ref, y_ref, lse_ref, *rest):
        qkv_d, (qr_s, kr_s, oc_s, lc_s, o1_s, l1_s, o2_s, l2_s, o3_s, l3_s, tmp_s) = rest[:9], rest[9:]
        c, s_a, s_b = c_ref[...], sa_ref[...], sb_ref[...]
        qr_s[...] = _rope_fwd(q_ref[...], c, s_a, s_b, A_ROT // 2) * (A_HEAD_DIM ** -0.5)
        kr_s[...] = _rope_fwd(k_ref[...], c, s_a, s_b, A_ROT // 2)
        head0 = lax.broadcasted_iota(jnp.int32, (Q_BLK, LANES), 1) < A_HEAD_DIM
        nat = ((o1_s, l1_s), (o2_s, l2_s), (o3_s, l3_s))

        for g, d in enumerate(DILATIONS):
            qd_s, kd_s, vd_s = qkv_d[3 * g:3 * g + 3]
            _deinterleave(qr_s, qd_s, d, BF16, tmp_s)
            _deinterleave(kr_s, kd_s, d, BF16, tmp_s)
            _deinterleave(v_ref, vd_s, d, BF16, tmp_s)
            o_dst, l_dst = (nat[g] if d == 1 else (oc_s, lc_s))

            def block(idx, carry, d=d, o_dst=o_dst, l_dst=l_dst, qd_s=qd_s, kd_s=kd_s, vd_s=vd_s):
                q0, k0, kwin, valid = _block_coords(idx, d)
                qb = qd_s[pl.ds(q0, Q_BLK), :]
                kb = kd_s[pl.ds(k0, kwin), :]
                vb = vd_s[pl.ds(k0, kwin), :]
                zero = jnp.zeros_like(qb)
                q2 = jnp.concatenate([jnp.where(head0, qb, zero), jnp.where(head0, zero, qb)], 0)
                s = jnp.where(jnp.concatenate([valid, valid], 0), _dot_nt(q2, kb), NEG_INF)
                m = jnp.max(s, axis=-1, keepdims=True)
                p = jnp.exp(s - m)
                l = jnp.sum(p, axis=-1, keepdims=True)
                o2 = _dot(p.astype(BF16), vb) / l
                l2 = m + jnp.log(l)
                o_dst[pl.ds(q0, Q_BLK), :] = jnp.where(head0, o2[:Q_BLK], o2[Q_BLK:])
                l_dst[pl.ds(q0, Q_BLK), :] = jnp.where(head0, l2[:Q_BLK], l2[Q_BLK:])
                return carry

            lax.fori_loop(0, SEQ // Q_BLK, block, 0, unroll=UNROLL_FWD)
            if d > 1:
                _interleave(oc_s, nat[g][0], d, tmp_s, False)
                _interleave(lc_s, nat[g][1], d, tmp_s, False)

        def merge(ci, carry):
            rows = pl.ds(pl.multiple_of(ci * 256, 256), 256)
            l1, l2, l3 = l1_s[rows, :], l2_s[rows, :], l3_s[rows, :]
            m = jnp.maximum(jnp.maximum(l1, l2), l3)
            w1, w2, w3 = jnp.exp(l1 - m), jnp.exp(l2 - m), jnp.exp(l3 - m)
            w = w1 + w2 + w3
            y_ref[rows, :] = (w1 * o1_s[rows, :] + w2 * o2_s[rows, :] + w3 * o3_s[rows, :]) / w
            lse_ref[rows, :] = m + jnp.log(w)
            return carry

        lax.fori_loop(0, SEQ // 256, merge, 0)

    def col(off):
        return pl.BlockSpec((SEQ, LANES), lambda b, hp: (b, off + hp))

    tab = pl.BlockSpec((SEQ, LANES), lambda b, hp: (b, 0))
    out = pl.BlockSpec((SEQ, LANES), lambda b, hp: (b, hp))
    f32s = pltpu.VMEM((SEQ, LANES), F32)
    res, landed = call_hosting_exchange(
        body, xch, grid=(nb, n_pairs),
        in_specs=[col(0), col(n_pairs), col(2 * n_pairs), tab, tab, tab],
        out_specs=[out] * 11,
        out_shape=[SDS((t, A_WIDTH), F32)] * 2 + [SDS((t, A_WIDTH), BF16)] * 9,
        scratch_shapes=[f32s] * 11,
        name="a_attn_fwd", operands=(proj, proj, proj, ca, sa, sb))
    return res[:2], res[2:], landed


def a_attn_bwd(qkv_d, ca, sa, sb, dy, y, lse, nb, xch):
    t = dy.shape[0]
    n_pairs = A_WIDTH // LANES

    def body(*refs):
        qkv_refs = refs[:9]
        (c_ref, sa_ref, sb_ref, do_ref, y_ref, lse_ref, dq_ref, dk_ref, dv_ref,
         l0n_s, l1n_s, d0n_s, d1n_s, dod_s, l0d_s, l1d_s, d0d_s, d1d_s,
         dqc_s, dkc_s, dvc_s, dq4_s, dk4_s, dv4_s, dqn_s, dkn_s, dvn_s, tmp_s) = refs[9:]
        c, s_a, s_b = c_ref[...], sa_ref[...], sb_ref[...]
        head0 = lax.broadcasted_iota(jnp.int32, (Q_BLK, LANES), 1) < A_HEAD_DIM

        def per_head_rows(ci, carry):
            rows = pl.ds(pl.multiple_of(ci * 256, 256), 256)
            h0 = lax.broadcasted_iota(jnp.int32, (256, LANES), 1) < A_HEAD_DIM
            tt = do_ref[rows, :] * y_ref[rows, :]
            d0n_s[rows, :] = jnp.broadcast_to(jnp.sum(jnp.where(h0, tt, 0.0), axis=-1, keepdims=True), (256, LANES))
            d1n_s[rows, :] = jnp.broadcast_to(jnp.sum(jnp.where(h0, 0.0, tt), axis=-1, keepdims=True), (256, LANES))
            l = lse_ref[rows, :]
            lr = pltpu.roll(l, A_HEAD_DIM, 1)
            l0n_s[rows, :] = jnp.where(h0, l, lr)
            l1n_s[rows, :] = jnp.where(h0, lr, l)
            return carry

        lax.fori_loop(0, SEQ // 256, per_head_rows, 0)
        assert DILATIONS == (1, 4, 16)

        for g, d in enumerate(DILATIONS):
            qd_s, kd_s, vd_s = qkv_refs[3 * g:3 * g + 3]
            _deinterleave(do_ref, dod_s, d, BF16, tmp_s)
            if d > 1:
                for src, dst in ((l0n_s, l0d_s), (l1n_s, l1d_s), (d0n_s, d0d_s), (d1n_s, d1d_s)):
                    _deinterleave(src, dst, d, F32, tmp_s)
            l0, l1, d0, d1 = (l0n_s, l1n_s, d0n_s, d1n_s) if d == 1 else (l0d_s, l1d_s, d0d_s, d1d_s)
            dq_dst, dk_dst, dv_dst = {1: (dqn_s, dkn_s, dvn_s), 4: (dq4_s, dk4_s, dv4_s), 16: (dqc_s, dkc_s, dvc_s)}[d]
            dk_dst[...] = jnp.zeros_like(dk_dst)
            dv_dst[...] = jnp.zeros_like(dv_dst)

            def block(idx, carry, d=d, l0=l0, l1=l1, d0=d0, d1=d1, dq_dst=dq_dst, dk_dst=dk_dst, dv_dst=dv_dst,
                      qd_s=qd_s, kd_s=kd_s, vd_s=vd_s):
                q0, k0, kwin, valid = _block_coords(idx, d)
                qrows = pl.ds(q0, Q_BLK)
                krows = pl.ds(k0, kwin)
                qb, dob = qd_s[qrows, :], dod_s[qrows, :]
                kb, vb = kd_s[krows, :], vd_s[krows, :]
                zero = jnp.zeros_like(qb)
                q2 = jnp.concatenate([jnp.where(head0, qb, zero), jnp.where(head0, zero, qb)], 0)
                do2 = jnp.concatenate([jnp.where(head0, dob, zero), jnp.where(head0, zero, dob)], 0)
                wide = lambda x: jnp.concatenate([x] * (kwin // LANES), 1)
                lse2 = wide(jnp.concatenate([l0[qrows, :], l1[qrows, :]], 0))
                dd2 = wide(jnp.concatenate([d0[qrows, :], d1[qrows, :]], 0))
                s = jnp.where(jnp.concatenate([valid, valid], 0), _dot_nt(q2, kb), NEG_INF)
                p = jnp.exp(s - lse2)
                ds = (p * (_dot_nt(do2, vb) - dd2)).astype(BF16)
                dq2 = _dot(ds, kb)
                dq_dst[qrows, :] = jnp.where(head0, dq2[:Q_BLK], dq2[Q_BLK:])
                dk_dst[krows, :] += _dot_tn(ds, q2)
                dv_dst[krows, :] += _dot_tn(p.astype(BF16), do2)
                return carry

            lax.fori_loop(0, SEQ // Q_BLK, block, 0, unroll=UNROLL_BWD)

        for c16, c4, nat in ((dqc_s, dq4_s, dqn_s), (dkc_s, dk4_s, dkn_s), (dvc_s, dv4_s, dvn_s)):
            _class16_to_class4(c16, tmp_s)
            c4[...] = c4[...] + tmp_s[...]
            _interleave(c4, nat, 4, tmp_s, True)

        dq_ref[...] = _rope_bwd(dqn_s[...] * (A_HEAD_DIM ** -0.5), c, s_a, s_b, A_ROT // 2).astype(BF16)
        dk_ref[...] = _rope_bwd(dkn_s[...], c, s_a, s_b, A_ROT // 2).astype(BF16)
        dv_ref[...] = dvn_s[...].astype(BF16)

    tab = pl.BlockSpec((SEQ, LANES), lambda b, hp: (b, 0))
    blk = pl.BlockSpec((SEQ, LANES), lambda b, hp: (b, hp))
    f32s = pltpu.VMEM((SEQ, LANES), F32)
    b16s = pltpu.VMEM((SEQ, LANES), BF16)
    return call_hosting_exchange(
        body, xch, grid=(nb, n_pairs),
        in_specs=[blk] * 9 + [tab, tab, tab, blk, blk, blk],
        out_specs=[blk, blk, blk],
        out_shape=[SDS((t, A_WIDTH), BF16)] * 3,
        scratch_shapes=[f32s] * 4 + [b16s] + [f32s] * 14,
        name="a_attn_bwd", operands=(*qkv_d, ca, sa, sb, dy, y, lse))


MLA_SCALE = (MLA_NOPE + MLA_ROPE) ** -0.5
LOG2E = 1.4426950408889634
MLA_QW = MLA_HEADS * LANES
MLA_KVW = MLA_QW + MLA_WIDTH


def _rms(x, g):
    r = lax.rsqrt(jnp.mean(x * x, axis=-1, keepdims=True) + NORM_EPS)
    return x * r * g, r


def _rms_bwd(dn, x, r, g):
    tg = dn * g
    dx = r * tg - x * (r * r * r) * jnp.mean(tg * x, axis=-1, keepdims=True)
    return dx, jnp.sum(dn * x * r, axis=0, keepdims=True)


def mla_prep_fwd(proj, cm, sma, smb, g_cq, g_ckv, wuq, wkv):
    t = proj.shape[0]
    tm = 1024

    def body(cq_ref, ckv_ref, kr_ref, c_ref, sa_ref, sb_ref, gq_ref, gkv_ref, wuq_ref, wkv_ref, q_ref, k_ref, v_ref):
        c, s_a, s_b = c_ref[...], sa_ref[...], sb_ref[...]
        cqn, _ = _rms(cq_ref[...], gq_ref[...])
        qf = _dot(cqn.astype(BF16), wuq_ref[...])
        ckvn, _ = _rms(ckv_ref[...], gkv_ref[...])
        kvf = _dot(ckvn.astype(BF16), wkv_ref[...])
        krope = _rope_fwd(kr_ref[...], c, s_a, s_b, MLA_ROPE // 2)
        for h in range(MLA_HEADS):
            cols = slice(h * LANES, (h + 1) * LANES)
            q_ref[:, cols] = (_rope_fwd(qf[:, cols], c, s_a, s_b, MLA_ROPE // 2) * (MLA_SCALE * LOG2E)).astype(BF16)
            k_ref[:, cols] = (kvf[:, cols] + krope).astype(BF16)
        v_ref[...] = kvf[:, MLA_QW:].astype(BF16)

    def row(w, j):
        return pl.BlockSpec((tm, w), lambda i: (i, j))

    def full(a):
        return pl.BlockSpec(a.shape, lambda i: (0, 0))

    return pl.pallas_call(
        body, grid=(t // tm,),
        in_specs=[row(256, 4096 // 256), row(128, 4352 // 128), row(128, 4480 // 128), row(128, 0), row(128, 0), row(128, 0),
                  full(g_cq), full(g_ckv), full(wuq), full(wkv)],
        out_specs=[row(MLA_QW, 0), row(MLA_QW, 0), row(MLA_WIDTH, 0)],
        out_shape=[SDS((t, MLA_QW), BF16), SDS((t, MLA_QW), BF16), SDS((t, MLA_WIDTH), BF16)],
        name="mla_prep_fwd", compiler_params=_params("parallel"))(proj, proj, proj, cm, sma, smb, g_cq, g_ckv, wuq, wkv)


def mla_prep_bwd(proj, cm, sma, smb, g_cq, g_ckv, wuq, wkv, dq, dk, dv):
    t = proj.shape[0]
    tm = 1024

    def body(cq_ref, ckv_ref, c_ref, sa_ref, sb_ref, gq_ref, gkv_ref, wuq_ref, wkv_ref, dq_ref, dk_ref, dv_ref,
             dcc_ref, dqf_ref, cqn_ref, dkvf_ref, ckvn_ref, dgq_ref, dgkv_ref):
        @pl.when(pl.program_id(0) == 0)
        def _():
            dgq_ref[...] = jnp.zeros_like(dgq_ref)
            dgkv_ref[...] = jnp.zeros_like(dgkv_ref)

        c, s_a, s_b = c_ref[...], sa_ref[...], sb_ref[...]
        cq, ckv = cq_ref[...], ckv_ref[...]
        cqn, rq = _rms(cq, gq_ref[...])
        ckvn, rkv = _rms(ckv, gkv_ref[...])
        cqn_ref[...] = cqn.astype(BF16)
        ckvn_ref[...] = ckvn.astype(BF16)
        lane = lax.broadcasted_iota(jnp.int32, (tm, LANES), 1)
        rope_lanes = (lane >= MLA_NOPE) & (lane < MLA_NOPE + MLA_ROPE)
        dkrope = jnp.zeros((tm, LANES), F32)
        for h in range(MLA_HEADS):
            cols = slice(h * LANES, (h + 1) * LANES)
            dqf_ref[:, cols] = _rope_bwd(dq_ref[:, cols].astype(F32) * MLA_SCALE, c, s_a, s_b, MLA_ROPE // 2).astype(BF16)
            dkh = dk_ref[:, cols].astype(F32) * (1.0 / LOG2E)
            dkvf_ref[:, cols] = dkh.astype(BF16)
            dkrope = dkrope + dkh
        dkvf_ref[:, MLA_QW:] = dv_ref[...].astype(BF16)
        dkr = _rope_bwd(jnp.where(rope_lanes, dkrope, 0.0), c, s_a, s_b, MLA_ROPE // 2)
        dcqn = _dot_nt(dqf_ref[...], wuq_ref[...])
        dckvn = _dot_nt(dkvf_ref[...], wkv_ref[...])
        dcq, dgq = _rms_bwd(dcqn, cq, rq, gq_ref[...])
        dckv, dgkv = _rms_bwd(dckvn, ckv, rkv, gkv_ref[...])
        dgq_ref[...] += dgq
        dgkv_ref[...] += dgkv
        dcc_ref[:, 0:256] = dcq.astype(BF16)
        dcc_ref[:, 256:384] = dckv.astype(BF16)
        dcc_ref[:, 384:512] = dkr.astype(BF16)

    def row(w, j):
        return pl.BlockSpec((tm, w), lambda i: (i, j))

    def full(a):
        return pl.BlockSpec(a.shape, lambda i: (0, 0))

    return pl.pallas_call(
        body, grid=(t // tm,),
        in_specs=[row(256, 4096 // 256), row(128, 4352 // 128), row(128, 0), row(128, 0), row(128, 0),
                  full(g_cq), full(g_ckv), full(wuq), full(wkv), row(MLA_QW, 0), row(MLA_QW, 0), row(MLA_WIDTH, 0)],
        out_specs=[row(512, 0), row(MLA_QW, 0), row(256, 0), row(MLA_KVW, 0), row(128, 0), full(g_cq), full(g_ckv)],
        out_shape=[SDS((t, 512), BF16), SDS((t, MLA_QW), BF16), SDS((t, 256), BF16), SDS((t, MLA_KVW), BF16),
                   SDS((t, 128), BF16), SDS(g_cq.shape, F32), SDS(g_ckv.shape, F32)],
        name="mla_prep_bwd", compiler_params=_params("arbitrary"))(proj, proj, cm, sma, smb, g_cq, g_ckv, wuq, wkv, dq, dk, dv)


MLA_TQ = SEQ
MLA_SUB_FWD = 512
MLA_SUB_BWD = 256


def mla_attn_fwd(qb, kb, vb, nb):
    t = qb.shape[0]
    nq = SEQ // MLA_TQ
    n_pairs = MLA_HEADS // 2

    def body(q_ref, k_ref, v_ref, y_ref, lse_ref):
        head0 = lax.broadcasted_iota(jnp.int32, (MLA_SUB_FWD, LANES), 1) < MLA_V
        v = v_ref[...]
        vhead0 = lax.broadcasted_iota(jnp.int32, v.shape, 1) < MLA_V
        one = jnp.ones_like(v)
        vh = [jnp.where(vhead0 == (h == 0), v, one) for h in range(2)]
        for sub in range(MLA_TQ // MLA_SUB_FWD):
            rows = slice(sub * MLA_SUB_FWD, (sub + 1) * MLA_SUB_FWD)
            outs, lses = [], []
            for h in range(2):
                cols = slice(h * LANES, (h + 1) * LANES)
                s = _dot_nt(q_ref[rows, cols], k_ref[:, cols])
                m = jnp.max(s, axis=-1, keepdims=True)
                p = jnp.exp2(s - m).astype(BF16)
                ol = _dot(p, vh[h])
                l = pltpu.roll(ol, MLA_V, 1)
                outs.append(ol / l)
                lses.append(m + jnp.log2(l))
            y_ref[rows, :] = jnp.where(head0, outs[0], outs[1])
            lse_ref[rows, :] = jnp.where(head0, lses[0], lses[1])

    return pl.pallas_call(
        body, grid=(nb, n_pairs, nq),
        in_specs=[pl.BlockSpec((MLA_TQ, 2 * LANES), lambda b, hp, i: (b * nq + i, hp)),
                  pl.BlockSpec((SEQ, 2 * LANES), lambda b, hp, i: (b, hp)),
                  pl.BlockSpec((SEQ, LANES), lambda b, hp, i: (b, hp))],
        out_specs=[pl.BlockSpec((MLA_TQ, LANES), lambda b, hp, i: (b * nq + i, hp))] * 2,
        out_shape=[SDS((t, MLA_WIDTH), F32)] * 2,
        name="mla_attn_fwd", compiler_params=_params("parallel", "parallel", "parallel"))(qb, kb, vb)


def mla_attn_bwd(qb, kb, vb, dy, y, lse, nb, xch):
    t = qb.shape[0]
    nq = SEQ // MLA_TQ
    n_pairs = MLA_HEADS // 2

    assert nq == 1

    def body(q_ref, k_ref, v_ref, do_ref, y_ref, lse_ref, dq_ref, dk_ref, dv_ref, dk_s, dv_s):
        dk_s[...] = jnp.zeros_like(dk_s)
        dv_s[...] = jnp.zeros_like(dv_s)
        head0 = lax.broadcasted_iota(jnp.int32, (MLA_SUB_BWD, LANES), 1) < MLA_V
        v = v_ref[...]
        for sub in range(MLA_TQ // MLA_SUB_BWD):
            rows = slice(sub * MLA_SUB_BWD, (sub + 1) * MLA_SUB_BWD)
            do = do_ref[rows, :]
            lse = lse_ref[rows, :]
            tt = do * y_ref[rows, :]
            dv = jnp.zeros((SEQ, LANES), F32)
            for h in range(2):
                sel = head0 if h == 0 else ~head0
                lo = h * MLA_V
                cols = slice(h * LANES, (h + 1) * LANES)
                q = q_ref[rows, cols]
                k = k_ref[:, cols]
                dd = jnp.sum(jnp.where(sel, tt, 0.0), axis=-1, keepdims=True)
                doh = jnp.where(sel, do, 0.0).astype(BF16)
                p = jnp.exp2(_dot_nt(q, k) - lse[:, lo:lo + 1])
                dp = _dot_nt(doh, v)
                ds = (p * (dp - dd)).astype(BF16)
                dq_ref[rows, cols] = _dot(ds, k).astype(dq_ref.dtype)
                dk_s[:, cols] += _dot_tn(ds, q)
                dv = dv + _dot_tn(p.astype(BF16), doh)
            dv_s[...] += dv
        dk_ref[...] = dk_s[...].astype(dk_ref.dtype)
        dv_ref[...] = dv_s[...].astype(dv_ref.dtype)

    qspec = pl.BlockSpec((MLA_TQ, 2 * LANES), lambda b, hp, i: (b * nq + i, hp))
    kspec = pl.BlockSpec((SEQ, 2 * LANES), lambda b, hp, i: (b, hp))
    vspec = pl.BlockSpec((SEQ, LANES), lambda b, hp, i: (b, hp))
    ospec = pl.BlockSpec((MLA_TQ, LANES), lambda b, hp, i: (b * nq + i, hp))
    return call_hosting_exchange(
        body, xch, grid=(nb, n_pairs, nq),
        in_specs=[qspec, kspec, vspec, ospec, ospec, ospec],
        out_specs=[qspec, kspec, vspec],
        out_shape=[SDS((t, MLA_QW), BF16), SDS((t, MLA_QW), BF16), SDS((t, MLA_WIDTH), BF16)],
        scratch_shapes=[pltpu.VMEM((SEQ, 2 * LANES), F32), pltpu.VMEM((SEQ, LANES), F32)],
        name="mla_attn_bwd", operands=(qb, kb, vb, dy, y, lse))


MEM_TQ = SEQ
MEM_SUB = SEQ
MEM_SCALE = MEM_HEAD_DIM ** -0.5
MQ_BLK4 = 5120 // MEM_WIDTH


def mem_attn_fwd(proj, mkv, nb):
    t = proj.shape[0]
    nq = SEQ // MEM_TQ

    def body(q_ref, mk_ref, mv_ref, y_ref):
        for sub in range(MEM_TQ // MEM_SUB):
            rows = slice(sub * MEM_SUB, (sub + 1) * MEM_SUB)
            for h in range(MEM_HEADS):
                cols = slice(h * LANES, (h + 1) * LANES)
                s = _dot_nt(q_ref[rows, cols].astype(BF16), mk_ref[:, cols]) * MEM_SCALE
                m = jnp.max(s, axis=-1, keepdims=True)
                p = jnp.exp(s - m)
                l = jnp.sum(p, axis=-1, keepdims=True)
                y_ref[rows, cols] = _dot(p.astype(BF16), mv_ref[:, cols]) / l

    return pl.pallas_call(
        body, grid=(nb, nq),
        in_specs=[pl.BlockSpec((MEM_TQ, MEM_WIDTH), lambda b, i: (b * nq + i, MQ_BLK4)),
                  pl.BlockSpec((N_MEM, MEM_WIDTH), lambda b, i: (b, 0)),
                  pl.BlockSpec((N_MEM, MEM_WIDTH), lambda b, i: (b, 1))],
        out_specs=pl.BlockSpec((MEM_TQ, MEM_WIDTH), lambda b, i: (b * nq + i, 0)),
        out_shape=SDS((t, MEM_WIDTH), F32),
        name="mem_attn_fwd", compiler_params=_params("parallel", "parallel"))(proj, mkv, mkv)


def mem_attn_bwd(proj, mkv, dy, nb):
    t = proj.shape[0]
    nq = SEQ // MEM_TQ

    def body(q_ref, mk_ref, mv_ref, do_ref, dq_ref, dmk_ref, dmv_ref):
        @pl.when(pl.program_id(1) == 0)
        def _():
            dmk_ref[...] = jnp.zeros_like(dmk_ref)
            dmv_ref[...] = jnp.zeros_like(dmv_ref)

        for sub in range(MEM_TQ // MEM_SUB):
            rows = slice(sub * MEM_SUB, (sub + 1) * MEM_SUB)
            for h in range(MEM_HEADS):
                cols = slice(h * LANES, (h + 1) * LANES)
                q = q_ref[rows, cols].astype(BF16)
                mk, mv = mk_ref[:, cols], mv_ref[:, cols]
                do = do_ref[rows, cols].astype(BF16)
                s = _dot_nt(q, mk) * MEM_SCALE
                e = jnp.exp(s - jnp.max(s, axis=-1, keepdims=True))
                p = e / jnp.sum(e, axis=-1, keepdims=True)
                dp = _dot_nt(do, mv)
                ds = (p * (dp - jnp.sum(p * dp, axis=-1, keepdims=True)) * MEM_SCALE).astype(BF16)
                dq_ref[rows, cols] = _dot(ds, mk).astype(BF16)
                dmk_ref[:, cols] += _dot_tn(ds, q)
                dmv_ref[:, cols] += _dot_tn(p.astype(BF16), do)

    ospec = pl.BlockSpec((MEM_TQ, MEM_WIDTH), lambda b, i: (b * nq + i, 0))
    kspec = pl.BlockSpec((N_MEM, MEM_WIDTH), lambda b, i: (b, 0))
    return pl.pallas_call(
        body, grid=(nb, nq),
        in_specs=[pl.BlockSpec((MEM_TQ, MEM_WIDTH), lambda b, i: (b * nq + i, MQ_BLK4)),
                  kspec, pl.BlockSpec((N_MEM, MEM_WIDTH), lambda b, i: (b, 1)), ospec],
        out_specs=[ospec, kspec, kspec],
        out_shape=[SDS((t, MEM_WIDTH), BF16), SDS((nb * N_MEM, MEM_WIDTH), F32), SDS((nb * N_MEM, MEM_WIDTH), F32)],
        name="mem_attn_bwd", compiler_params=_params("parallel", "arbitrary"))(proj, mkv, mkv, dy)


ROW_TM = 512
AG_BLK = 3072 // 1024
BG_BLK = 4608 // 512
MG_BLK = 5632 // 512
GROUPS = ((0, A_WIDTH), (A_WIDTH, MLA_WIDTH), (A_WIDTH + MLA_WIDTH, MEM_WIDTH))
D_MIX = 2048


def _gate_specs():
    def row(w, j):
        return pl.BlockSpec((ROW_TM, w), lambda i: (i, j))

    def vec(w):
        return pl.BlockSpec((1, w), lambda i: (0, 0))

    ys = [row(A_WIDTH, 0), row(MLA_WIDTH, 0), row(MEM_WIDTH, 0)]
    gates = [row(A_WIDTH, AG_BLK), row(MLA_WIDTH, BG_BLK), row(MEM_WIDTH, MG_BLK)]
    gains = [vec(A_WIDTH), vec(MLA_WIDTH), vec(MEM_WIDTH)]
    return row, vec, ys, gates, gains


def gate_out_ln_loss(ya, yb, ym, proj, goa, gob, gom, wout, h32, target, gp, bp):
    t, d = h32.shape
    _, _, ys, gates, gains = _gate_specs()

    def body(ya_ref, yb_ref, ym_ref, ga_ref, gb_ref, gm_ref, goa_ref, gob_ref, gom_ref, w_ref, h_ref, t_ref, gp_ref, bp_ref,
             z_ref, du32_ref, du16_ref, loss_ref, dgp_ref, dbp_ref):
        @pl.when(pl.program_id(0) == 0)
        def _():
            loss_ref[...] = jnp.zeros_like(loss_ref)
            dgp_ref[...] = jnp.zeros_like(dgp_ref)
            dbp_ref[...] = jnp.zeros_like(dbp_ref)

        for (off, w), y_ref, g_ref, go_ref in zip(GROUPS, (ya_ref, yb_ref, ym_ref), (ga_ref, gb_ref, gm_ref),
                                                  (goa_ref, gob_ref, gom_ref)):
            n, _ = _rms(y_ref[...], go_ref[...])
            gt = g_ref[...]
            z_ref[:, off:off + w] = (n * (gt * _sigmoid(gt))).astype(BF16)
        g = gp_ref[...]
        u = ALPHA * h_ref[...] + _dot(z_ref[...], w_ref[...])
        mu = jnp.mean(u, axis=-1, keepdims=True)
        uc = u - mu
        rstd = lax.rsqrt(jnp.mean(uc * uc, axis=-1, keepdims=True) + NORM_EPS)
        xhat = uc * rstd
        err = xhat * g + bp_ref[...] - t_ref[...]
        tok = jnp.sum(err * err, axis=-1, keepdims=True) * (1.0 / d)
        loss_ref[...] += 0.5 * jnp.sum(tok, axis=0, keepdims=True)
        dout = err * (1.0 / d)
        dxhat = dout * g
        du = rstd * (dxhat - jnp.mean(dxhat, axis=-1, keepdims=True)
                     - xhat * jnp.mean(dxhat * xhat, axis=-1, keepdims=True))
        du32_ref[...] = du
        du16_ref[...] = du.astype(BF16)
        dgp_ref[...] += jnp.sum(dout * xhat, axis=0, keepdims=True)
        dbp_ref[...] += jnp.sum(dout, axis=0, keepdims=True)

    row = pl.BlockSpec((ROW_TM, d), lambda i: (i, 0))
    vec = pl.BlockSpec((1, d), lambda i: (0, 0))
    zrow = pl.BlockSpec((ROW_TM, D_MIX), lambda i: (i, 0))
    return pl.pallas_call(
        body, grid=(t // ROW_TM,),
        in_specs=ys + gates + gains + [pl.BlockSpec((D_MIX, d), lambda i: (0, 0)), row, row, vec, vec],
        out_specs=[zrow, row, row, pl.BlockSpec((1, LANES), lambda i: (0, 0)), vec, vec],
        out_shape=[SDS((t, D_MIX), BF16), SDS((t, d), F32), SDS((t, d), BF16), SDS((1, LANES), F32), SDS((1, d), F32),
                   SDS((1, d), F32)],
        name="gate_out_ln_loss", compiler_params=_params("arbitrary"))(
            ya, yb, ym, proj, proj, proj, goa, gob, gom, wout, h32, target, gp, bp)


def gate_bwd(du16, wout, ya, yb, ym, proj, goa, gob, gom):
    t = ya.shape[0]
    row, vec, ys, gates, gains = _gate_specs()

    def body(du_ref, w_ref, ya_ref, yb_ref, ym_ref, ga_ref, gb_ref, gm_ref, goa_ref, gob_ref, gom_ref,
             dya_ref, dyb_ref, dym_ref, dga_ref, dgb_ref, dgm_ref, dgoa_ref, dgob_ref, dgom_ref):
        @pl.when(pl.program_id(0) == 0)
        def _():
            dgoa_ref[...] = jnp.zeros_like(dgoa_ref)
            dgob_ref[...] = jnp.zeros_like(dgob_ref)
            dgom_ref[...] = jnp.zeros_like(dgom_ref)

        dz = _dot_nt(du_ref[...], w_ref[...])
        for (off, w), y_ref, g_ref, go_ref, dy_ref, dg_ref, dgo_ref in zip(
                GROUPS, (ya_ref, yb_ref, ym_ref), (ga_ref, gb_ref, gm_ref), (goa_ref, gob_ref, gom_ref),
                (dya_ref, dyb_ref, dym_ref), (dga_ref, dgb_ref, dgm_ref), (dgoa_ref, dgob_ref, dgom_ref)):
            dzg = dz[:, off:off + w]
            y, gt, go = y_ref[...], g_ref[...], go_ref[...]
            n, r = _rms(y, go)
            sg = _sigmoid(gt)
            dg_ref[...] = (dzg * n * (sg * (1.0 + gt * (1.0 - sg)))).astype(BF16)
            dy, dgo = _rms_bwd(dzg * (gt * sg), y, r, go)
            dy_ref[...] = dy
            dgo_ref[...] += dgo

    widths = (A_WIDTH, MLA_WIDTH, MEM_WIDTH)
    return pl.pallas_call(
        body, grid=(t // ROW_TM,),
        in_specs=[row(D_MODEL, 0), pl.BlockSpec((D_MIX, D_MODEL), lambda i: (0, 0))] + ys + gates + gains,
        out_specs=[row(w, 0) for w in widths] * 2 + [vec(w) for w in widths],
        out_shape=[SDS((t, w), F32) for w in widths] + [SDS((t, w), BF16) for w in widths] + [SDS((1, w), F32) for w in widths],
        name="gate_bwd", compiler_params=_params("arbitrary"))(du16, wout, ya, yb, ym, proj, proj, proj, goa, gob, gom)


def dh_ln_bwd(pieces, win_t, du32, x2, g_emb, xch):
    t, d = x2.shape

    def body(*refs):
        p_refs = refs[:len(pieces)]
        w_ref, du_ref, x_ref, g_ref, dx_ref, dg_ref, db_ref = refs[len(pieces):]

        @pl.when(pl.program_id(0) == 0)
        def _():
            dg_ref[...] = jnp.zeros_like(dg_ref)
            db_ref[...] = jnp.zeros_like(db_ref)

        dh = ALPHA * du_ref[...]
        for p_ref, off, w in zip(p_refs, PIECE_OFFS, PIECE_WIDTHS):
            dh = dh + _dot(p_ref[...], w_ref[off:off + w, :])
        x = x_ref[...]
        xc = x - jnp.mean(x, axis=-1, keepdims=True)
        rstd = lax.rsqrt(jnp.mean(xc * xc, axis=-1, keepdims=True) + NORM_EPS)
        xhat = xc * rstd
        dg_ref[...] += jnp.sum(dh * xhat, axis=0, keepdims=True)
        db_ref[...] += jnp.sum(dh, axis=0, keepdims=True)
        tg = dh * g_ref[...]
        dx_ref[...] = rstd * (tg - jnp.mean(tg, axis=-1, keepdims=True)
                              - xhat * jnp.mean(tg * xhat, axis=-1, keepdims=True))

    row = pl.BlockSpec((ROW_TM, d), lambda i: (i, 0))
    vec = pl.BlockSpec((1, d), lambda i: (0, 0))
    return call_hosting_exchange(
        body, xch, grid=(t // ROW_TM,),
        in_specs=[pl.BlockSpec((ROW_TM, w), lambda i: (i, 0)) for w in PIECE_WIDTHS]
        + [pl.BlockSpec(win_t.shape, lambda i: (0, 0)), row, row, vec],
        out_specs=[row, vec, vec],
        out_shape=[SDS((t, d), F32), SDS((1, d), F32), SDS((1, d), F32)],
        scratch_shapes=[], name="dh_ln_bwd", operands=(*pieces, win_t, du32, x2, g_emb))


def _adamw(w, g, m, v):
    m2 = ADAM_B1 * m + (1.0 - ADAM_B1) * g
    v2 = ADAM_B2 * v + (1.0 - ADAM_B2) * (g * g)
    m_hat = m2 / (1.0 - ADAM_B1 ** ADAM_STEP)
    v_hat = v2 / (1.0 - ADAM_B2 ** ADAM_STEP)
    return -ADAM_LR * (m_hat / (jnp.sqrt(v_hat) + ADAM_EPS) + ADAM_WD * w), m2, v2


def adamw_shard(w, parts, m, v, name):
    r, c = w.shape
    if r % 256 == 0 or r * c <= 256 * 1024:
        tr, tc = min(r, 256), c
    else:
        tr, tc = r, 256

    def body(w_ref, p_ref, m_ref, v_ref, g_ref, d_ref, nm_ref, nv_ref):
        g = p_ref[0].astype(F32)
        for k in range(1, N_DEV):
            g = g + p_ref[k].astype(F32)
        g_ref[...] = g
        d_ref[...], nm_ref[...], nv_ref[...] = _adamw(w_ref[...], g, m_ref[...], v_ref[...])

    blk = pl.BlockSpec((tr, tc), lambda i, j: (i, j))
    return pl.pallas_call(
        body, grid=(r // tr, c // tc),
        in_specs=[blk, pl.BlockSpec((N_DEV, tr, tc), lambda i, j: (0, i, j)), blk, blk],
        out_specs=[blk] * 4, out_shape=[SDS((r, c), F32)] * 4, name=name,
        compiler_params=_params("parallel", "parallel"))(w, parts, m, v)


def adamw_shard_rows(w, parts, m, v, name, tr=128):
    r, c = w.shape
    k = c // LANES

    def body(w_ref, p_ref, m_ref, v_ref, *o_refs):
        g = p_ref[0].astype(F32)
        for j in range(1, N_DEV):
            g = g + p_ref[j].astype(F32)
        for o_ref, val in zip(o_refs, (g,) + _adamw(w_ref[...], g, m_ref[...], v_ref[...])):
            for s in range(k):
                o_ref[pl.ds(s, tr, stride=k), :] = val[:, s * LANES:(s + 1) * LANES]

    blk = pl.BlockSpec((tr, c), lambda i: (i, 0))
    oblk = pl.BlockSpec((tr * k, LANES), lambda i: (i, 0))
    res = pl.pallas_call(
        body, grid=(pl.cdiv(r, tr),),
        in_specs=[blk, pl.BlockSpec((N_DEV, tr, c), lambda i: (0, i, 0)), blk, blk],
        out_specs=[oblk] * 4, out_shape=[SDS((r * k, LANES), F32)] * 4, name=name,
        compiler_params=_params("parallel"))(w, parts, m, v)
    return [o.reshape(r, c) for o in res]


def adamw_shards_whole(ws, parts, ms, vs, name):
    n = len(ws)

    def body(*refs):
        w_refs, p_refs, m_refs, v_refs = refs[:n], refs[n:2 * n], refs[2 * n:3 * n], refs[3 * n:4 * n]
        outs = refs[4 * n:]
        for i in range(n):
            g = p_refs[i][0].astype(F32)
            for k in range(1, N_DEV):
                g = g + p_refs[i][k].astype(F32)
            outs[4 * i][...] = g
            outs[4 * i + 1][...], outs[4 * i + 2][...], outs[4 * i + 3][...] = _adamw(
                w_refs[i][...], g, m_refs[i][...], v_refs[i][...])

    res = pl.pallas_call(
        body, out_shape=[SDS(w.shape, F32) for w in ws for _ in range(4)], name=name,
        compiler_params=_params())(*ws, *parts, *ms, *vs)
    return [res[4 * i:4 * i + 4] for i in range(n)]


def _place():
    return lax.axis_index("x"), lax.axis_index("y"), lax.axis_index("c")


def _flat(px, py, pc):
    return 4 * px + 2 * py + pc


def _peer(x, y, c, k):
    return (1 - x if k & 4 else x, 1 - y if k & 2 else y, 1 - c if k & 1 else c)


def cast_shards(shards):
    def body(*refs):
        n = len(refs) // 2
        for i_ref, o_ref in zip(refs[:n], refs[n:]):
            o_ref[...] = i_ref[...].astype(BF16)

    return pl.pallas_call(body, out_shape=[SDS(s.shape, BF16) for s in shards], name="cast_shards",
                          compiler_params=_params())(*shards)


def _two_level_gather_plan(src_refs, land_refs, send_sems, recv_sems, local_sems):
    n = len(src_refs)
    x, y, c = _place()
    me, sib = (x, y, c), (x, y, 1 - c)
    chips = [(1 - x, y), (x, 1 - y), (1 - x, 1 - y)]

    def copy(a, k, block, to, src=None):
        dst = land_refs[a].at[_flat(*block)]
        return pltpu.make_async_remote_copy(
            src_ref=dst if src is None else src, dst_ref=dst,
            send_sem=send_sems.at[a * N_DEV + k], recv_sem=recv_sems.at[a * N_DEV + k],
            device_id=to, device_id_type=MESH)

    mine = [pltpu.make_async_copy(src_refs[a], land_refs[a].at[_flat(*me)], local_sems.at[a]) for a in range(n)]
    first = []
    for a in range(n):
        first.append(copy(a, 0, me, sib, src=src_refs[a]))
        first += [copy(a, 1 + j, me, (*chip, c), src=src_refs[a]) for j, chip in enumerate(chips)]

    def start():
        for cp in mine + first:
            cp.start()

    def finish():
        passed = []
        for j, chip in enumerate(chips):
            for a in range(n):
                copy(a, 1 + j, (*chip, c), me).wait_recv()
                fwd = copy(a, 4 + j, (*chip, c), sib)
                fwd.start()
                passed.append(fwd)
        for a in range(n):
            copy(a, 0, sib, me).wait_recv()
            for j, chip in enumerate(chips):
                copy(a, 4 + j, (*chip, 1 - c), me).wait_recv()
        for cp in first + passed:
            cp.wait_send()
        for cp in mine:
            cp.wait()

    return start, finish


ALL_DEVICES = tuple(range(N_DEV))


def _exchange_plan(src_refs, land_refs, dests, send_sems, recv_sems, local_sems):
    x, y, c = _place()
    me = _flat(x, y, c)
    plan = []
    for a, (src, land, dl) in enumerate(zip(src_refs, land_refs, dests)):
        for li, j in enumerate(dl):
            to = ((j >> 2) & 1, (j >> 1) & 1, j & 1)
            block = src.at[li] if len(src.shape) == len(land.shape) else src

            def push(slot, a=a, block=block, land=land, j=j, to=to):
                return pltpu.make_async_remote_copy(
                    src_ref=block, dst_ref=land.at[slot], send_sem=send_sems.at[a * N_DEV + j],
                    recv_sem=recv_sems.at[a * N_DEV + slot], device_id=to, device_id_type=MESH)

            own = pltpu.make_async_copy(block, land.at[j], local_sems.at[a])
            plan.append((j, push(me), own, [push(s) for s in range(N_DEV) if s != j]))
    return me, plan


def _exchange_start(me, plan):
    for j, send, own, _ in plan:
        @pl.when(me != j)
        def _(send=send):
            send.start()

        @pl.when(me == j)
        def _(own=own):
            own.start()


def _exchange_wait(me, plan):
    for j, send, own, arrivals in plan:
        @pl.when(me != j)
        def _(send=send):
            send.wait_send()

        @pl.when(me == j)
        def _(own=own, arrivals=arrivals):
            own.wait()
            for arrival in arrivals:
                arrival.wait_recv()


def call_hosting_exchange(core, xch, *, grid, in_specs, out_specs, out_shape, scratch_shapes, name, operands):
    srcs, dests, landing = xch
    n, n_in, n_out, n_scr = len(srcs), len(in_specs), len(out_specs), len(scratch_shapes)

    def body(*refs):
        ins, src_refs = refs[:n_in], refs[n_in:n_in + n]
        outs = refs[n_in + 2 * n:n_in + 2 * n + n_out]
        land_refs = refs[n_in + 2 * n + n_out:n_in + 3 * n + n_out]
        scratch = refs[n_in + 3 * n + n_out:n_in + 3 * n + n_out + n_scr]
        sems = refs[n_in + 3 * n + n_out + n_scr:]
        first = functools.reduce(jnp.logical_and, [pl.program_id(i) == 0 for i in range(len(grid))])
        last = functools.reduce(jnp.logical_and, [pl.program_id(i) == grid[i] - 1 for i in range(len(grid))])
        if dests is None:
            start, finish = _two_level_gather_plan(src_refs, land_refs, *sems)
        else:
            me, plan = _exchange_plan(src_refs, land_refs, dests, *sems)
            start, finish = functools.partial(_exchange_start, me, plan), functools.partial(_exchange_wait, me, plan)
        pl.when(first)(start)
        core(*ins, *outs, *scratch)
        pl.when(last)(finish)

    hbm = pl.BlockSpec(memory_space=pl.ANY)
    res = pl.pallas_call(
        body, grid=grid,
        in_specs=list(in_specs) + [hbm] * (2 * n), out_specs=list(out_specs) + [hbm] * n,
        out_shape=list(out_shape) + [SDS(l.shape, l.dtype) for l in landing],
        scratch_shapes=list(scratch_shapes) + [pltpu.SemaphoreType.DMA((N_DEV * n,)), pltpu.SemaphoreType.DMA((N_DEV * n,)),
                                               pltpu.SemaphoreType.DMA((n,))],
        input_output_aliases={n_in + n + k: n_out + k for k in range(n)},
        name=name, compiler_params=_params(*(("arbitrary",) * len(grid))))(*operands, *srcs, *landing)
    return res[:n_out], res[n_out:]


SLOT_ROWS = 8


def small_allreduce_adamw(loss_sum, grads, ws, ms, vs):
    n = len(grads)
    rows = [g.shape[0] for g in grads]
    total = SLOT_ROWS * (n + 1)

    def body(*refs):
        loss_ref, g_refs, w_refs = refs[0], refs[1:1 + n], refs[1 + n:1 + 2 * n]
        m_refs, v_refs = refs[1 + 2 * n:1 + 3 * n], refs[1 + 3 * n:1 + 4 * n]
        outs = refs[1 + 4 * n:2 + 8 * n]
        vec, gath, tot, send_sems, recv_sems = refs[2 + 8 * n:]
        x, y, c = _place()
        me = _flat(x, y, c)
        vec[...] = jnp.zeros_like(vec)
        vec[0:1, :] = loss_ref[...]
        for i in range(n):
            vec[SLOT_ROWS * (i + 1):SLOT_ROWS * (i + 1) + rows[i], :] = g_refs[i][...]
        gath[me] = vec[...]
        copies = []
        for k in range(1, N_DEV):
            peer = _peer(x, y, c, k)
            copies.append(pltpu.make_async_remote_copy(
                src_ref=vec, dst_ref=gath.at[me], send_sem=send_sems.at[k - 1], recv_sem=recv_sems.at[k - 1],
                device_id=peer, device_id_type=MESH))
        for cp in copies:
            cp.start()
        for cp in copies:
            cp.wait_recv()
        for cp in copies:
            cp.wait_send()
        g = gath[0]
        for j in range(1, N_DEV):
            g = g + gath[j]
        tot[...] = g
        outs[0][...] = tot[0:1, :]
        for i in range(n):
            gi = tot[SLOT_ROWS * (i + 1):SLOT_ROWS * (i + 1) + rows[i], :]
            outs[1 + i][...] = gi
            outs[1 + n + i][...], outs[1 + 2 * n + i][...], outs[1 + 3 * n + i][...] = _adamw(
                w_refs[i][...], gi, m_refs[i][...], v_refs[i][...])

    shapes = [SDS(g.shape, F32) for g in grads]
    return pl.pallas_call(
        body, out_shape=[SDS((1, LANES), F32)] + shapes * 4,
        scratch_shapes=[pltpu.VMEM((total, LANES), F32), pltpu.VMEM((N_DEV, total, LANES), F32), pltpu.VMEM((total, LANES), F32),
                        pltpu.SemaphoreType.DMA((7,)), pltpu.SemaphoreType.DMA((7,))],
        name="small_allreduce_adamw", compiler_params=_params())(loss_sum, *grads, *ws, *ms, *vs)


def _rope_lane_patterns():
    inv = lambda r: ROPE_THETA ** (-(jnp.arange(0, r, 2, dtype=F32) / r))
    z = lambda n: jnp.zeros((n,), F32)
    o = lambda n: jnp.ones((n,), F32)
    half, rest = A_ROT // 2, A_HEAD_DIM - A_ROT
    ia, im = inv(A_ROT), inv(MLA_ROPE)
    mh, tail = MLA_ROPE // 2, LANES - MLA_NOPE - MLA_ROPE
    rows = [jnp.tile(jnp.concatenate([ia, ia, z(rest)]), 2),
            jnp.tile(jnp.concatenate([o(half), z(half + rest)]), 2),
            jnp.tile(jnp.concatenate([z(half), o(half), z(rest)]), 2),
            jnp.concatenate([z(MLA_NOPE), im, im, z(tail)]),
            jnp.concatenate([z(MLA_NOPE), o(mh), z(mh + tail)]),
            jnp.concatenate([z(MLA_NOPE + mh), o(mh), z(tail)]),
            z(LANES), z(LANES)]
    return jnp.stack(rows)


KR_LO, KR_HI = 4480, 4512
W_IN_SHARD = D_IN // N_DEV
BG_SPLIT = 6 * W_IN_SHARD - KR_HI


def w_in_working_t(g):
    pad_lo, pad_hi = MLA_NOPE, LANES - MLA_NOPE - MLA_ROPE
    spans = []
    for lo, hi, shift in ((0, KR_LO, 0), (KR_LO, KR_HI, pad_lo), (KR_HI, D_IN, pad_lo + pad_hi)):
        r = lo
        while r < hi:
            j = r // W_IN_SHARD
            n = min(hi, (j + 1) * W_IN_SHARD) - r
            spans.append((j, r - j * W_IN_SHARD, n, r + shift))
            r += n

    def body(g_ref, o_ref):
        o_ref[KR_LO:KR_LO + pad_lo, :] = jnp.zeros((pad_lo, D_MODEL), o_ref.dtype)
        o_ref[KR_HI + pad_lo:KR_HI + pad_lo + pad_hi, :] = jnp.zeros((pad_hi, D_MODEL), o_ref.dtype)
        for j, src, n, dst in spans:
            o_ref[dst:dst + n, :] = g_ref[j, src:src + n, :]

    return pl.pallas_call(body, out_shape=SDS((D_INW, D_MODEL), g.dtype), name="w_in_working_t", compiler_params=_params())(g)


def _w_in_shard_5(d_ag_tail, d_cc, d_bg_head):
    kr = MLA_Q_RANK + MLA_KV_RANK + MLA_NOPE
    rows = jnp.concatenate([d_ag_tail, d_cc[:MLA_Q_RANK + MLA_KV_RANK], d_cc[kr:kr + MLA_ROPE], d_bg_head], 0)
    return rows.reshape(1, W_IN_SHARD, D_MODEL).astype(BF16)


def _w_uq_working(g):
    w = jnp.pad(g.transpose(1, 0, 2), ((0, 0), (0, 0), (0, LANES - MLA_NOPE - MLA_ROPE)))
    return w.reshape(MLA_Q_RANK, MLA_QW)


def _w_uq_parts(dw):
    return dw.reshape(MLA_Q_RANK, MLA_HEADS, LANES)[:, :, :MLA_NOPE + MLA_ROPE].transpose(1, 0, 2)


def _w_ukv_working(g):
    wk = jnp.pad(g[:, :, :MLA_NOPE].transpose(1, 0, 2), ((0, 0), (0, 0), (0, LANES - MLA_NOPE)))
    wv = g[:, :, MLA_NOPE:].transpose(1, 0, 2)
    return jnp.concatenate([wk.reshape(MLA_KV_RANK, MLA_QW), wv.reshape(MLA_KV_RANK, MLA_WIDTH)], 1)


def _w_ukv_parts(dw):
    dk = dw[:, :MLA_QW].reshape(MLA_KV_RANK, MLA_HEADS, LANES)[:, :, :MLA_NOPE]
    dv = dw[:, MLA_QW:].reshape(MLA_KV_RANK, MLA_HEADS, MLA_V)
    return jnp.concatenate([dk, dv], -1).transpose(1, 0, 2)


SMALL_NAMES = ("g_emb", "b_emb", "g_cq", "g_ckv", "g_out_a", "g_out_b", "g_out_m", "g_post", "b_post")


def kernel(x, mem, positions, g_emb, b_emb, w_in, g_cq, g_ckv, w_uq, w_ukv, w_mem_kv, g_out_a, g_out_b, g_out_m, w_out, g_post, b_post, loss_target, m_g_emb, m_b_emb, m_w_in, m_g_cq, m_g_ckv, m_w_uq, m_w_ukv, m_w_mem_kv, m_g_out_a, m_g_out_b, m_g_out_m, m_w_out, m_g_post, m_b_post, v_g_emb, v_b_emb, v_w_in, v_g_cq, v_g_ckv, v_w_uq, v_w_ukv, v_w_mem_kv, v_g_out_a, v_g_out_b, v_g_out_m, v_w_out, v_g_post, v_b_post):
    nb = x.shape[0]
    t = nb * SEQ
    x2 = x.reshape(t, D_MODEL)
    tgt2 = loss_target.reshape(t, D_MODEL)
    mem2 = mem.reshape(nb * N_MEM, D_MODEL)
    g_emb2, b_emb2 = g_emb.reshape(1, -1), b_emb.reshape(1, -1)

    w_in_t, m_w_in_t, v_w_in_t = w_in[0].T, m_w_in[0].T, v_w_in[0].T
    s_in, s_uq, s_ukv, s_mem, s_out = cast_shards((w_in_t, w_uq[0], w_ukv[0], w_mem_kv[0], w_out[0]))
    (h32, h16, (a_c, a_sa, a_sb), (m_c, m_sa, m_sb)), (g_in,) = embed_fwd(
        x2, g_emb2, b_emb2, positions, ((s_in,), None, (lax.empty((N_DEV,) + s_in.shape, BF16),)))
    win_t = w_in_working_t(g_in)

    proj = mm_nn(h16, win_t, F32, 2048, 1536, "proj", rhs_transposed=True)
    later = (s_uq, s_ukv, s_mem, s_out)
    (ya, lse_a), qkv_d, (g_uq, g_ukv, g_mem, g_out) = a_attn_fwd(
        proj, a_c, a_sa, a_sb, nb,
        (later, (ALL_DEVICES,) * len(later), tuple(lax.empty((N_DEV,) + w.shape, BF16) for w in later)))
    wuq_w = _w_uq_working(g_uq)
    wkv_w = _w_ukv_working(g_ukv)
    wmem = g_mem.reshape(D_MODEL, 2 * MEM_WIDTH)
    wout = g_out.reshape(D_MIX, D_MODEL)
    qb, kb, vb = mla_prep_fwd(proj, m_c, m_sa, m_sb, g_cq, g_ckv, wuq_w, wkv_w)
    yb, lse_b = mla_attn_fwd(qb, kb, vb, nb)
    mkv = mm_nn(mem2, wmem, BF16, nb * N_MEM, 512, "mem_kv")
    ym = mem_attn_fwd(proj, mkv, nb)
    z, du32, du16, loss_sum, dg_post, db_post = gate_out_ln_loss(
        ya, yb, ym, proj, g_out_a, g_out_b, g_out_m, wout, h32, tgt2, g_post, b_post)

    dya, dyb, dym, dag, dbg, dmg, dg_out_a, dg_out_b, dg_out_m = gate_bwd(
        du16, wout, ya, yb, ym, proj, g_out_a, g_out_b, g_out_m)
    dw_out = mm_tn(z, du16, 1024, "dw_out")
    dmq, dmk, dmv = mem_attn_bwd(proj, mkv, dym, nb)
    dw_mem = mm_tn(mem2, jnp.concatenate([dmk, dmv], 1), nb * N_MEM, "dw_mem")
    shards_6_7, d_bg_head = mm_tn_group((dbg, dmq, dmg), h16, 2048, "dw_in_bg_mq_mg", W_IN_SHARD, BG_SPLIT, 2, (0, BG_SPLIT))
    landing = lambda w, dtype=F32: lax.empty((N_DEV,) + w.shape, dtype)
    big_w = (w_in_t, w_uq[0], w_ukv[0], w_mem_kv[0], w_out[0])
    (daq, dak, dav), (p_out, p_mem, p_in) = a_attn_bwd(
        qkv_d, a_c, a_sa, a_sb, dya, ya, lse_a, nb,
        ((dw_out.reshape(N_DEV, D_MIX // N_DEV, D_MODEL), dw_mem.reshape(N_DEV, D_MODEL // N_DEV, 2 * MEM_WIDTH),
          shards_6_7),
         (ALL_DEVICES, ALL_DEVICES, (6, 7)),
         (landing(w_out[0]), landing(w_mem_kv[0]), landing(w_in_t, BF16))))
    shards_0_4, d_ag_tail = mm_tn_group((daq, dak, dav, dag), h16, 1024, "dw_in_aq_ak_av_ag", W_IN_SHARD, 0, 5,
                                        (5 * W_IN_SHARD, 4 * A_WIDTH))
    (dqb, dkb, dvb), (p_in,) = mla_attn_bwd(
        qb, kb, vb, dyb, yb, lse_b, nb, ((shards_0_4,), ((0, 1, 2, 3, 4),), (p_in,)))
    dcc, dqf, cqn, dkvf, ckvn, dg_cq, dg_ckv = mla_prep_bwd(proj, m_c, m_sa, m_sb, g_cq, g_ckv, wuq_w, wkv_w, dqb, dkb, dvb)
    dw_uq, dw_ukv, d_cc = mm_tn_pairs(((cqn, dqf), (ckvn, dkvf), (dcc, h16)), 1024, "dw_uq_ukv_cc")
    pieces = (daq, dak, dav, dag, dcc, dbg, dmq, dmg)
    (grad_x, dg_emb, db_emb), (p_in, p_uq, p_ukv) = dh_ln_bwd(
        pieces, win_t, du32, x2, g_emb2,
        ((_w_in_shard_5(d_ag_tail, d_cc, d_bg_head), _w_uq_parts(dw_uq), _w_ukv_parts(dw_ukv)),
         ((5,), ALL_DEVICES, ALL_DEVICES),
         (p_in, landing(w_uq[0]), landing(w_ukv[0]))))

    parts = (p_in, p_uq, p_ukv, p_mem, p_out)
    big_m = (m_w_in_t, m_w_uq[0], m_w_ukv[0], m_w_mem_kv[0], m_w_out[0])
    big_v = (v_w_in_t, v_w_uq[0], v_w_ukv[0], v_w_mem_kv[0], v_w_out[0])
    big = {"w_in": [o.T[None] for o in adamw_shard_rows(big_w[0], parts[0], big_m[0], big_v[0], "adamw_w_in")]}
    rest = adamw_shards_whole(big_w[1:], parts[1:], big_m[1:], big_v[1:], "adamw_rest")
    for name, res in zip(("w_uq", "w_ukv", "w_mem_kv", "w_out"), rest):
        big[name] = [o[None] for o in res]

    small_w = (g_emb, b_emb, g_cq, g_ckv, g_out_a, g_out_b, g_out_m, g_post, b_post)
    small_m = (m_g_emb, m_b_emb, m_g_cq, m_g_ckv, m_g_out_a, m_g_out_b, m_g_out_m, m_g_post, m_b_post)
    small_v = (v_g_emb, v_b_emb, v_g_cq, v_g_ckv, v_g_out_a, v_g_out_b, v_g_out_m, v_g_post, v_b_post)
    small_g = (dg_emb, db_emb, dg_cq, dg_ckv, dg_out_a, dg_out_b, dg_out_m, dg_post, db_post)
    rows128 = lambda vals: [v.reshape(-1, LANES) for v in vals]
    res = small_allreduce_adamw(loss_sum, rows128(small_g), rows128(small_w), rows128(small_m), rows128(small_v))
    loss = res[0][0, 0]
    n_small = len(small_w)
    sg, sd, sm, sv = [[r.reshape(w.shape) for r, w in zip(res[1 + k * n_small:1 + (k + 1) * n_small], small_w)]
                      for k in range(4)]

    order = ("g_emb", "b_emb", "w_in", "g_cq", "g_ckv", "w_uq", "w_ukv", "w_mem_kv", "g_out_a", "g_out_b", "g_out_m",
             "w_out", "g_post", "b_post")
    small_idx = {n: i for i, n in enumerate(SMALL_NAMES)}
    outs = [loss, grad_x.reshape(x.shape)]
    for kind in range(4):
        for name in order:
            outs.append(big[name][kind] if name in big else (sg, sd, sm, sv)[kind][small_idx[name]])
    return tuple(outs)
```
